```python
import jax, jax.numpy as jnp
from jax import lax
import numpy as np

D_MODEL = 1024
BATCH = 8
SEQ = 4096
DEPTH = 2

N_BRANCH = 3
POOL_WINDOWS = (2, 4, 8, 16)
POOL_WIDTH = D_MODEL // 2
POOL_GROUP = POOL_WIDTH // len(POOL_WINDOWS)
CONV_WIDTH = D_MODEL // 2
CONV_K = 3
HEAD_DIM = 64
ATTN_GROUPS = ((128, 1), (512, 4), (2048, 16))
HEADS_PER_GROUP = 4
N_HEADS = HEADS_PER_GROUP * len(ATTN_GROUPS)
ATTN_WIDTH = N_HEADS * HEAD_DIM
ATTN_OUT = HEADS_PER_GROUP * HEAD_DIM
ATTN_BLOCK = 128
D_FF = 4 * D_MODEL
EPS = 1e-6
MASK_VALUE = -1e30

OFF_POOL = 0
OFF_CONV_B = OFF_POOL + POOL_WIDTH
OFF_CONV_C = OFF_CONV_B + CONV_WIDTH
OFF_CONV_X = OFF_CONV_C + CONV_WIDTH
OFF_Q = OFF_CONV_X + CONV_WIDTH
OFF_K = OFF_Q + ATTN_WIDTH
OFF_V = OFF_K + ATTN_WIDTH
OFF_GATE = OFF_V + ATTN_WIDTH
IN_COLS = OFF_GATE + N_BRANCH * D_MODEL

kernel_name = "hybrid_pool_conv_dilated_attn_block"


def rms_norm(x, gain):
    xf = x.astype(jnp.float32)
    y = xf * lax.rsqrt(jnp.mean(xf * xf, axis=-1, keepdims=True) + EPS)
    return (y * gain.astype(jnp.float32)).astype(x.dtype)


def multiscale_pool(u, w_mix, scale):
    B, S, _ = u.shape
    cs = jnp.cumsum(u.astype(jnp.float32), axis=1)
    pos = jnp.arange(S)
    diffs = []
    for g, w in enumerate(POOL_WINDOWS):
        c = cs[..., g * POOL_GROUP:(g + 1) * POOL_GROUP]
        lag = jnp.pad(c, ((0, 0), (w, 0), (0, 0)))[:, :S]
        count = jnp.minimum(pos + 1, w).astype(jnp.float32)[None, :, None]
        diffs.append((c - lag) / count - u[..., g * POOL_GROUP:(g + 1) * POOL_GROUP].astype(jnp.float32))
    d = jnp.stack(diffs, axis=2).astype(u.dtype)
    y = jnp.einsum('bsgc,gcd->bsgd', d, w_mix).reshape(B, S, POOL_WIDTH)
    return y * scale


def short_gated_conv(b_gate, c_gate, xin, conv_w):
    S = xin.shape[1]
    u = c_gate * xin
    up = jnp.pad(u, ((0, 0), (CONV_K - 1, 0), (0, 0)))
    y = conv_w[CONV_K - 1] * up[:, CONV_K - 1:CONV_K - 1 + S]
    for j in range(CONV_K - 1):
        y = y + conv_w[j] * up[:, j:j + S]
    return b_gate * y


def dilated_window_attention(q, k, v, window, dilation):
    B, S, H, Dh = q.shape
    span = window // dilation
    assert span <= ATTN_BLOCK
    L = S // dilation
    nb = -(-L // ATTN_BLOCK)
    Lp = nb * ATTN_BLOCK

    def fold(t):
        t = t.reshape(B, L, dilation, H, Dh).transpose(0, 2, 3, 1, 4)
        t = jnp.pad(t, ((0, 0), (0, 0), (0, 0), (0, Lp - L), (0, 0)))
        return t.reshape(B, dilation, H, nb, ATTN_BLOCK, Dh)

    def with_prev(t):
        prev = jnp.pad(t, ((0, 0), (0, 0), (0, 0), (1, 0), (0, 0), (0, 0)))[:, :, :, :nb]
        return jnp.concatenate([prev, t], axis=4)

    qb = fold(q).astype(jnp.float32)
    kc = with_prev(fold(k)).astype(jnp.float32)
    vc = with_prev(fold(v)).astype(jnp.float32)
    s = jnp.einsum('bdhnqc,bdhnkc->bdhnqk', qb, kc) * (Dh ** -0.5)
    qi = jnp.arange(ATTN_BLOCK)[:, None]
    ki = jnp.arange(2 * ATTN_BLOCK)[None, :] - ATTN_BLOCK
    rel = qi - ki
    band = (rel >= 0) & (rel <= span)
    has_prev = (jnp.arange(nb) > 0)[:, None, None] | (ki >= 0)[None]
    mask = band[None] & has_prev
    s = jnp.where(mask, s, MASK_VALUE)
    m = jnp.max(s, axis=-1, keepdims=True)
    p = jnp.exp(s - m)
    den = jnp.sum(p, axis=-1, keepdims=True)
    o = jnp.einsum('bdhnqk,bdhnkc->bdhnqc', p, vc) / den
    lse = (m + jnp.log(den))[..., 0]
    o = o.reshape(B, dilation, H, Lp, Dh)[:, :, :, :L].transpose(0, 3, 1, 2, 4).reshape(B, S, H, Dh)
    lse = lse.reshape(B, dilation, H, Lp)[..., :L].transpose(0, 3, 1, 2).reshape(B, S, H)
    return o.astype(q.dtype), lse


def dilated_mixture_attention(q, k, v):
    B, S = q.shape[:2]
    outs, lses = [], []
    for g, (window, dilation) in enumerate(ATTN_GROUPS):
        hs = slice(g * HEADS_PER_GROUP, (g + 1) * HEADS_PER_GROUP)
        o, lse = dilated_window_attention(q[:, :, hs], k[:, :, hs], v[:, :, hs], window, dilation)
        outs.append(o)
        lses.append(lse)
    wts = jax.nn.softmax(jnp.stack(lses, axis=0), axis=0)
    o = jnp.sum(wts[..., None] * jnp.stack(outs, axis=0).astype(jnp.float32), axis=0)
    return o.reshape(B, S, ATTN_OUT).astype(q.dtype)


def hybrid_layer(x, norm_mix, w_in, b_gate, pool_mix, pool_scale, conv_w, q_gain, k_gain,
                 w_pool_up, w_conv_out, w_attn_up, w_o, norm_mlp, w_ff1, w_ff2):
    B, S, D = x.shape
    h = rms_norm(x, norm_mix)
    z = jnp.einsum('bsd,dc->bsc', h, w_in)
    y_pool = multiscale_pool(z[..., OFF_POOL:OFF_CONV_B], pool_mix, pool_scale)
    y_conv = short_gated_conv(z[..., OFF_CONV_B:OFF_CONV_C], z[..., OFF_CONV_C:OFF_CONV_X],
                              z[..., OFF_CONV_X:OFF_Q], conv_w)
    q = rms_norm(z[..., OFF_Q:OFF_K].reshape(B, S, N_HEADS, HEAD_DIM), q_gain)
    k = rms_norm(z[..., OFF_K:OFF_V].reshape(B, S, N_HEADS, HEAD_DIM), k_gain)
    v = z[..., OFF_V:OFF_GATE].reshape(B, S, N_HEADS, HEAD_DIM)
    y_attn = dilated_mixture_attention(q, k, v)
    gates = jax.nn.sigmoid((z[..., OFF_GATE:] + b_gate).astype(jnp.float32)).astype(x.dtype)
    gates = gates.reshape(B, S, N_BRANCH, D)
    merged = (gates[:, :, 0] * (y_pool @ w_pool_up)
              + gates[:, :, 1] * (y_conv @ w_conv_out)
              + gates[:, :, 2] * (y_attn @ w_attn_up))
    x = x + merged @ w_o
    h2 = rms_norm(x, norm_mlp)
    x = x + jnp.square(jax.nn.relu(h2 @ w_ff1)) @ w_ff2
    return x


def _fwd_setup_inputs(seed: int = 0) -> dict:
    key = jax.random.key(seed)
    ks = jax.random.split(key, 17)
    L, D = DEPTH, D_MODEL

    def nrm(k, shape, fan_in):
        return jax.random.normal(k, shape, jnp.float32) * (fan_in ** -0.5)

    def gain(k, shape):
        return 1.0 + 0.02 * jax.random.normal(k, shape, jnp.float32)

    return {
        "x": jax.random.normal(ks[0], (BATCH, SEQ, D), jnp.float32),
        "norm_mix": gain(ks[1], (L, D)),
        "w_in": nrm(ks[2], (L, D, IN_COLS), D),
        "b_gate": 0.01 * jax.random.normal(ks[3], (L, N_BRANCH * D), jnp.float32),
        "pool_mix": nrm(ks[4], (L, len(POOL_WINDOWS), POOL_GROUP, POOL_GROUP), POOL_GROUP),
        "pool_scale": gain(ks[5], (L, POOL_WIDTH)),
        "conv_w": nrm(ks[6], (L, CONV_K, CONV_WIDTH), CONV_K),
        "q_gain": gain(ks[7], (L, HEAD_DIM)),
        "k_gain": gain(ks[8], (L, HEAD_DIM)),
        "w_pool_up": nrm(ks[9], (L, POOL_WIDTH, D), POOL_WIDTH),
        "w_conv_out": nrm(ks[10], (L, CONV_WIDTH, D), CONV_WIDTH),
        "w_attn_up": nrm(ks[11], (L, ATTN_OUT, D), ATTN_OUT),
        "w_o": nrm(ks[12], (L, D, D), D),
        "norm_mlp": gain(ks[13], (L, D)),
        "w_ff1": nrm(ks[14], (L, D, D_FF), D),
        "w_ff2": nrm(ks[15], (L, D_FF, D), D_FF),
    }


def _fwd_reference(x, norm_mix, w_in, b_gate, pool_mix, pool_scale, conv_w, q_gain, k_gain,
              w_pool_up, w_conv_out, w_attn_up, w_o, norm_mlp, w_ff1, w_ff2):
    for l in range(DEPTH):
        x = hybrid_layer(x, norm_mix[l], w_in[l], b_gate[l], pool_mix[l], pool_scale[l], conv_w[l],
                         q_gain[l], k_gain[l], w_pool_up[l], w_conv_out[l], w_attn_up[l], w_o[l],
                         norm_mlp[l], w_ff1[l], w_ff2[l])
    return x


import jax as _jax
import jax.numpy as _jnp

TWIN_FORMAT = 'train_step'
FWD_PARAMS = ['x', 'norm_mix', 'w_in', 'b_gate', 'pool_mix', 'pool_scale', 'conv_w', 'q_gain', 'k_gain', 'w_pool_up', 'w_conv_out', 'w_attn_up', 'w_o', 'norm_mlp', 'w_ff1', 'w_ff2']
TWIN_WEIGHTS = ['norm_mix', 'w_in', 'b_gate', 'pool_mix', 'pool_scale', 'conv_w', 'q_gain', 'k_gain', 'w_pool_up', 'w_conv_out', 'w_attn_up', 'w_o', 'norm_mlp', 'w_ff1', 'w_ff2']
TWIN_DIFF_INPUT = 'x'
TWIN_INPUTS = ['x', 'norm_mix', 'w_in', 'b_gate', 'pool_mix', 'pool_scale', 'conv_w', 'q_gain', 'k_gain', 'w_pool_up', 'w_conv_out', 'w_attn_up', 'w_o', 'norm_mlp', 'w_ff1', 'w_ff2', 'loss_target', 'm_norm_mix', 'm_w_in', 'm_b_gate', 'm_pool_mix', 'm_pool_scale', 'm_conv_w', 'm_q_gain', 'm_k_gain', 'm_w_pool_up', 'm_w_conv_out', 'm_w_attn_up', 'm_w_o', 'm_norm_mlp', 'm_w_ff1', 'm_w_ff2', 'v_norm_mix', 'v_w_in', 'v_b_gate', 'v_pool_mix', 'v_pool_scale', 'v_conv_w', 'v_q_gain', 'v_k_gain', 'v_w_pool_up', 'v_w_conv_out', 'v_w_attn_up', 'v_w_o', 'v_norm_mlp', 'v_w_ff1', 'v_w_ff2']
TWIN_OUTPUTS = ['loss', 'grad_x', 'grad_norm_mix', 'grad_w_in', 'grad_b_gate', 'grad_pool_mix', 'grad_pool_scale', 'grad_conv_w', 'grad_q_gain', 'grad_k_gain', 'grad_w_pool_up', 'grad_w_conv_out', 'grad_w_attn_up', 'grad_w_o', 'grad_norm_mlp', 'grad_w_ff1', 'grad_w_ff2', 'delta_norm_mix', 'delta_w_in', 'delta_b_gate', 'delta_pool_mix', 'delta_pool_scale', 'delta_conv_w', 'delta_q_gain', 'delta_k_gain', 'delta_w_pool_up', 'delta_w_conv_out', 'delta_w_attn_up', 'delta_w_o', 'delta_norm_mlp', 'delta_w_ff1', 'delta_w_ff2', 'new_m_norm_mix', 'new_m_w_in', 'new_m_b_gate', 'new_m_pool_mix', 'new_m_pool_scale', 'new_m_conv_w', 'new_m_q_gain', 'new_m_k_gain', 'new_m_w_pool_up', 'new_m_w_conv_out', 'new_m_w_attn_up', 'new_m_w_o', 'new_m_norm_mlp', 'new_m_w_ff1', 'new_m_w_ff2', 'new_v_norm_mix', 'new_v_w_in', 'new_v_b_gate', 'new_v_pool_mix', 'new_v_pool_scale', 'new_v_conv_w', 'new_v_q_gain', 'new_v_k_gain', 'new_v_w_pool_up', 'new_v_w_conv_out', 'new_v_w_attn_up', 'new_v_w_o', 'new_v_norm_mlp', 'new_v_w_ff1', 'new_v_w_ff2']
TWIN_LEAF_KINDS = {'loss': 'loss', 'grad_x': 'grad_x', 'grad_norm_mix': 'grad_w', 'grad_w_in': 'grad_w', 'grad_b_gate': 'grad_w', 'grad_pool_mix': 'grad_w', 'grad_pool_scale': 'grad_w', 'grad_conv_w': 'grad_w', 'grad_q_gain': 'grad_w', 'grad_k_gain': 'grad_w', 'grad_w_pool_up': 'grad_w', 'grad_w_conv_out': 'grad_w', 'grad_w_attn_up': 'grad_w', 'grad_w_o': 'grad_w', 'grad_norm_mlp': 'grad_w', 'grad_w_ff1': 'grad_w', 'grad_w_ff2': 'grad_w', 'delta_norm_mix': 'delta_w', 'delta_w_in': 'delta_w', 'delta_b_gate': 'delta_w', 'delta_pool_mix': 'delta_w', 'delta_pool_scale': 'delta_w', 'delta_conv_w': 'delta_w', 'delta_q_gain': 'delta_w', 'delta_k_gain': 'delta_w', 'delta_w_pool_up': 'delta_w', 'delta_w_conv_out': 'delta_w', 'delta_w_attn_up': 'delta_w', 'delta_w_o': 'delta_w', 'delta_norm_mlp': 'delta_w', 'delta_w_ff1': 'delta_w', 'delta_w_ff2': 'delta_w', 'new_m_norm_mix': 'new_m', 'new_m_w_in': 'new_m', 'new_m_b_gate': 'new_m', 'new_m_pool_mix': 'new_m', 'new_m_pool_scale': 'new_m', 'new_m_conv_w': 'new_m', 'new_m_q_gain': 'new_m', 'new_m_k_gain': 'new_m', 'new_m_w_pool_up': 'new_m', 'new_m_w_conv_out': 'new_m', 'new_m_w_attn_up': 'new_m', 'new_m_w_o': 'new_m', 'new_m_norm_mlp': 'new_m', 'new_m_w_ff1': 'new_m', 'new_m_w_ff2': 'new_m', 'new_v_norm_mix': 'new_v', 'new_v_w_in': 'new_v', 'new_v_b_gate': 'new_v', 'new_v_pool_mix': 'new_v', 'new_v_pool_scale': 'new_v', 'new_v_conv_w': 'new_v', 'new_v_q_gain': 'new_v', 'new_v_k_gain': 'new_v', 'new_v_w_pool_up': 'new_v', 'new_v_w_conv_out': 'new_v', 'new_v_w_attn_up': 'new_v', 'new_v_w_o': 'new_v', 'new_v_norm_mlp': 'new_v', 'new_v_w_ff1': 'new_v', 'new_v_w_ff2': 'new_v'}


def _forward(args):
    return _fwd_reference(*[args[k] for k in FWD_PARAMS])


def _output_shape():
    def fwd():
        inp = _fwd_setup_inputs(0)
        return _fwd_reference(*[inp[k] for k in FWD_PARAMS])
    out = _jax.eval_shape(fwd)
    return out.shape, out.dtype

N_MICROBATCH = 1
ADAM_LR = 0.001
ADAM_B1 = 0.9
ADAM_B2 = 0.999
ADAM_EPS = 1e-08
ADAM_WD = 0.01
ADAM_STEP = 10
PER_EXAMPLE_BATCH_AXIS = {'x': 0, 'loss_target': 0}
SHARED_INPUTS = []
_WEIGHT_DTYPES = {'norm_mix': _jnp.float32, 'w_in': _jnp.float32, 'b_gate': _jnp.float32, 'pool_mix': _jnp.float32, 'pool_scale': _jnp.float32, 'conv_w': _jnp.float32, 'q_gain': _jnp.float32, 'k_gain': _jnp.float32, 'w_pool_up': _jnp.float32, 'w_conv_out': _jnp.float32, 'w_attn_up': _jnp.float32, 'w_o': _jnp.float32, 'norm_mlp': _jnp.float32, 'w_ff1': _jnp.float32, 'w_ff2': _jnp.float32}
MOMENT_SCALE = {'norm_mix': 3.635983e+01, 'w_in': 1.034967e+00, 'b_gate': 2.380428e+00, 'pool_mix': 1.936952e+00, 'pool_scale': 1.427453e+01, 'conv_w': 1.063779e+01, 'q_gain': 9.574551e-01, 'k_gain': 9.553940e-01, 'w_pool_up': 1.197847e+00, 'w_conv_out': 9.508065e-01, 'w_attn_up': 1.785147e+00, 'w_o': 2.321286e+00, 'norm_mlp': 9.623547e+01, 'w_ff1': 3.535561e+00, 'w_ff2': 1.413468e+01}


def _to_microbatches(a, axis):
    t = _jnp.moveaxis(a, axis, 0)
    t = t.reshape((N_MICROBATCH, t.shape[0] // N_MICROBATCH) + t.shape[1:])
    return _jnp.moveaxis(t, 1, axis + 1)


def setup_inputs(seed: int = 0) -> dict:
    inp = _fwd_setup_inputs(seed)
    key = _jax.random.fold_in(_jax.random.key(seed), 7919)
    shape, _ = _output_shape()
    out = dict(inp)
    out["loss_target"] = _jax.random.normal(_jax.random.fold_in(key, 0), shape, _jnp.float32)
    for i, name in enumerate(TWIN_WEIGHTS):
        w = inp[name].astype(_jnp.float32)
        if MOMENT_SCALE is None:
            s = _jnp.sqrt(_jnp.mean(_jnp.square(w)) + 1e-30)
        else:
            s = MOMENT_SCALE[name]
        km, kv = _jax.random.split(_jax.random.fold_in(key, i + 1))
        out[name] = w
        out["m_" + name] = s * _jax.random.normal(km, w.shape, _jnp.float32)
        out["v_" + name] = (s * s) * _jax.random.uniform(kv, w.shape, _jnp.float32, 0.5, 1.5)
    if N_MICROBATCH > 1:
        for name, axis in PER_EXAMPLE_BATCH_AXIS.items():
            out[name] = _to_microbatches(out[name], axis)
    return {'x': out['x'], 'norm_mix': out['norm_mix'], 'w_in': out['w_in'], 'b_gate': out['b_gate'], 'pool_mix': out['pool_mix'], 'pool_scale': out['pool_scale'], 'conv_w': out['conv_w'], 'q_gain': out['q_gain'], 'k_gain': out['k_gain'], 'w_pool_up': out['w_pool_up'], 'w_conv_out': out['w_conv_out'], 'w_attn_up': out['w_attn_up'], 'w_o': out['w_o'], 'norm_mlp': out['norm_mlp'], 'w_ff1': out['w_ff1'], 'w_ff2': out['w_ff2'], 'loss_target': out['loss_target'], 'm_norm_mix': out['m_norm_mix'], 'm_w_in': out['m_w_in'], 'm_b_gate': out['m_b_gate'], 'm_pool_mix': out['m_pool_mix'], 'm_pool_scale': out['m_pool_scale'], 'm_conv_w': out['m_conv_w'], 'm_q_gain': out['m_q_gain'], 'm_k_gain': out['m_k_gain'], 'm_w_pool_up': out['m_w_pool_up'], 'm_w_conv_out': out['m_w_conv_out'], 'm_w_attn_up': out['m_w_attn_up'], 'm_w_o': out['m_w_o'], 'm_norm_mlp': out['m_norm_mlp'], 'm_w_ff1': out['m_w_ff1'], 'm_w_ff2': out['m_w_ff2'], 'v_norm_mix': out['v_norm_mix'], 'v_w_in': out['v_w_in'], 'v_b_gate': out['v_b_gate'], 'v_pool_mix': out['v_pool_mix'], 'v_pool_scale': out['v_pool_scale'], 'v_conv_w': out['v_conv_w'], 'v_q_gain': out['v_q_gain'], 'v_k_gain': out['v_k_gain'], 'v_w_pool_up': out['v_w_pool_up'], 'v_w_conv_out': out['v_w_conv_out'], 'v_w_attn_up': out['v_w_attn_up'], 'v_w_o': out['v_w_o'], 'v_norm_mlp': out['v_norm_mlp'], 'v_w_ff1': out['v_w_ff1'], 'v_w_ff2': out['v_w_ff2']}


def _loss(weights, diff, rest, loss_target):
    with _jax.named_scope("forward"):
        args = {**rest, TWIN_DIFF_INPUT: diff, **{k: w.astype(_WEIGHT_DTYPES[k]) for k, w in weights.items()}}
        y = _forward(args)
    with _jax.named_scope("loss_head"):
        err = _jnp.square(y.astype(_jnp.float32) - loss_target)
        return 0.5 * _jnp.sum(_jnp.mean(err, axis=-1)) if err.ndim else 0.5 * err


def _adamw(w, g, m, v):
    m = ADAM_B1 * m + (1.0 - ADAM_B1) * g
    v = ADAM_B2 * v + (1.0 - ADAM_B2) * _jnp.square(g)
    m_hat = m / (1.0 - ADAM_B1 ** ADAM_STEP)
    v_hat = v / (1.0 - ADAM_B2 ** ADAM_STEP)
    delta = -ADAM_LR * (m_hat / (_jnp.sqrt(v_hat) + ADAM_EPS) + ADAM_WD * w)
    return delta, m, v


def reference(x, norm_mix, w_in, b_gate, pool_mix, pool_scale, conv_w, q_gain, k_gain, w_pool_up, w_conv_out, w_attn_up, w_o, norm_mlp, w_ff1, w_ff2, loss_target, m_norm_mix, m_w_in, m_b_gate, m_pool_mix, m_pool_scale, m_conv_w, m_q_gain, m_k_gain, m_w_pool_up, m_w_conv_out, m_w_attn_up, m_w_o, m_norm_mlp, m_w_ff1, m_w_ff2, v_norm_mix, v_w_in, v_b_gate, v_pool_mix, v_pool_scale, v_conv_w, v_q_gain, v_k_gain, v_w_pool_up, v_w_conv_out, v_w_attn_up, v_w_o, v_norm_mlp, v_w_ff1, v_w_ff2):
    given = dict(x=x, norm_mix=norm_mix, w_in=w_in, b_gate=b_gate, pool_mix=pool_mix, pool_scale=pool_scale, conv_w=conv_w, q_gain=q_gain, k_gain=k_gain, w_pool_up=w_pool_up, w_conv_out=w_conv_out, w_attn_up=w_attn_up, w_o=w_o, norm_mlp=norm_mlp, w_ff1=w_ff1, w_ff2=w_ff2, loss_target=loss_target, m_norm_mix=m_norm_mix, m_w_in=m_w_in, m_b_gate=m_b_gate, m_pool_mix=m_pool_mix, m_pool_scale=m_pool_scale, m_conv_w=m_conv_w, m_q_gain=m_q_gain, m_k_gain=m_k_gain, m_w_pool_up=m_w_pool_up, m_w_conv_out=m_w_conv_out, m_w_attn_up=m_w_attn_up, m_w_o=m_w_o, m_norm_mlp=m_norm_mlp, m_w_ff1=m_w_ff1, m_w_ff2=m_w_ff2, v_norm_mix=v_norm_mix, v_w_in=v_w_in, v_b_gate=v_b_gate, v_pool_mix=v_pool_mix, v_pool_scale=v_pool_scale, v_conv_w=v_conv_w, v_q_gain=v_q_gain, v_k_gain=v_k_gain, v_w_pool_up=v_w_pool_up, v_w_conv_out=v_w_conv_out, v_w_attn_up=v_w_attn_up, v_w_o=v_w_o, v_norm_mlp=v_norm_mlp, v_w_ff1=v_w_ff1, v_w_ff2=v_w_ff2)
    weights = {n: given[n] for n in TWIN_WEIGHTS}
    shared = {n: given[n] for n in SHARED_INPUTS}
    per_example = {n: given[n] for n in ['x']}
    grad_fn = _jax.value_and_grad(_loss, argnums=(0, 1))

    def one_microbatch(ex, loss_target):
        ex = dict(ex)
        diff = ex.pop(TWIN_DIFF_INPUT)
        return grad_fn(weights, diff, {**shared, **ex}, loss_target)

    if N_MICROBATCH == 1:
        loss, (grad_w, grad_x) = one_microbatch(per_example, given["loss_target"])
    else:
        def body(carry, xs):
            loss_sum, grad_sum = carry
            l_k, (gw_k, gx_k) = one_microbatch(xs[0], xs[1])
            with _jax.named_scope("update"):
                return (loss_sum + l_k, _jax.tree.map(_jnp.add, grad_sum, gw_k)), gx_k

        init = (_jnp.zeros((), _jnp.float32), _jax.tree.map(_jnp.zeros_like, weights))
        (loss, grad_w), grad_x = _jax.lax.scan(body, init, (per_example, given["loss_target"]))
    with _jax.named_scope("update"):
        delta_w, new_m, new_v = {}, {}, {}
        for n in TWIN_WEIGHTS:
            delta_w[n], new_m[n], new_v[n] = _adamw(weights[n], grad_w[n], given["m_" + n], given["v_" + n])
    return (loss, grad_x, *[grad_w[n] for n in TWIN_WEIGHTS], *[delta_w[n] for n in TWIN_WEIGHTS],
            *[new_m[n] for n in TWIN_WEIGHTS], *[new_v[n] for n in TWIN_WEIGHTS])
```

```python
import functools

import jax
import jax.numpy as jnp
from jax import lax
from jax.experimental import pallas as pl
from jax.experimental.pallas import tpu as pltpu

F32 = jnp.float32
BF16 = jnp.bfloat16
MESH = pl.DeviceIdType.MESH

D = 1024
SEQ = 4096
DEPTH = 2
N_DEV = 8
POOL_WINDOWS = (2, 4, 8, 16)
POOL_W = 512
POOL_G = 128
CONV_W = 512
HEAD = 64
ATTN_GROUPS = ((128, 1), (512, 4), (2048, 16))
HPG = 4
ATTN_W = 768
ATTN_OUT = 256
ATTN_BLK = 128
D_FF = 4096
EPS = 1e-6
MASK = -1e30
OFF_POOL = 0
OFF_CB = 512
OFF_CC = 1024
OFF_CX = 1536
OFF_Q = 2048
OFF_K = 2816
OFF_V = 3584
OFF_GATE = 4352
IN_COLS = 7424
ADAM_LR = 0.001
ADAM_B1 = 0.9
ADAM_B2 = 0.999
ADAM_EPS = 1e-08
ADAM_WD = 0.01
ADAM_STEP = 10

ROW_TILE = 512
SEQ_CHUNK = 256
HALO = 16
VMEM_LIMIT = 56 * 1024 * 1024


def _cparams(sem=None):
    return pltpu.CompilerParams(dimension_semantics=sem, vmem_limit_bytes=VMEM_LIMIT)


_DIMS = {"nn": (((1,), (0,)), ((), ())), "nt": (((1,), (1,)), ((), ())), "tn": (((0,), (0,)), ((), ()))}


def _matmul(name, a, b, mode, tm, tn, tk, out_dtypes=(F32,), extras=(), epilogue=None):
    if mode == "tn":
        K, M = a.shape
    else:
        M, K = a.shape
    N = b.shape[0] if mode == "nt" else b.shape[1]
    assert M % tm == 0 and N % tn == 0 and K % tk == 0, (name, M, N, K, tm, tn, tk)
    nk = K // tk
    n_extra = len(extras)
    n_out = len(out_dtypes)
    dims = _DIMS[mode]

    def body(a_ref, b_ref, *rest):
        extra_refs = rest[:n_extra]
        out_refs = rest[n_extra:n_extra + n_out]

        def finish(acc):
            if epilogue is None:
                res = (acc,)
            else:
                res = epilogue(acc, *[r[...] for r in extra_refs])
            for o_ref, v in zip(out_refs, res):
                o_ref[...] = v.astype(o_ref.dtype)

        part = lax.dot_general(a_ref[...].astype(BF16), b_ref[...].astype(BF16), dims,
                               preferred_element_type=F32)
        if nk == 1:
            finish(part)
        else:
            acc_ref = rest[-1]
            k = pl.program_id(2)

            @pl.when(k == 0)
            def _():
                acc_ref[...] = part

            @pl.when(k > 0)
            def _():
                acc_ref[...] += part

            @pl.when(k == nk - 1)
            def _():
                finish(acc_ref[...])

    if mode == "tn":
        a_spec = pl.BlockSpec((tk, tm), lambda i, j, k: (k, i))
    else:
        a_spec = pl.BlockSpec((tm, tk), lambda i, j, k: (i, k))
    if mode == "nt":
        b_spec = pl.BlockSpec((tn, tk), lambda i, j, k: (j, k))
    else:
        b_spec = pl.BlockSpec((tk, tn), lambda i, j, k: (k, j))
    in_specs = [a_spec, b_spec]
    args = [a, b]
    for arr, kind in extras:
        if kind == "mn":
            in_specs.append(pl.BlockSpec((tm, tn), lambda i, j, k: (i, j)))
        else:
            in_specs.append(pl.BlockSpec((1, tn), lambda i, j, k: (0, j)))
        args.append(arr)
    out_shape = tuple(jax.ShapeDtypeStruct((M, N), dt) for dt in out_dtypes)
    out_specs = tuple(pl.BlockSpec((tm, tn), lambda i, j, k: (i, j)) for _ in out_dtypes)
    scratch = [] if nk == 1 else [pltpu.VMEM((tm, tn), F32)]
    res = pl.pallas_call(
        body, name=name, grid=(M // tm, N // tn, nk), in_specs=in_specs, out_specs=out_specs,
        out_shape=out_shape, scratch_shapes=scratch,
        compiler_params=_cparams(("parallel", "parallel", "arbitrary")),
    )(*args)
    return res if n_out > 1 else res[0]


def _rmsnorm_fwd(name, x, gain):
    S_, D_ = x.shape

    def body(x_ref, g_ref, h_ref):
        xf = x_ref[...]
        ms = jnp.mean(xf * xf, axis=-1, keepdims=True)
        h_ref[...] = (xf * lax.rsqrt(ms + EPS) * g_ref[...]).astype(BF16)

    return pl.pallas_call(
        body, name=name, grid=(S_ // ROW_TILE,),
        in_specs=[pl.BlockSpec((ROW_TILE, D_), lambda i: (i, 0)), pl.BlockSpec((1, D_), lambda i: (0, 0))],
        out_specs=pl.BlockSpec((ROW_TILE, D_), lambda i: (i, 0)),
        out_shape=jax.ShapeDtypeStruct((S_, D_), BF16),
        compiler_params=_cparams(("parallel",)),
    )(x, gain)


def _rmsnorm_bwd(name, x, gain, dh, dres):
    S_, D_ = x.shape
    n = S_ // ROW_TILE

    def body(x_ref, g_ref, dh_ref, dres_ref, dx_ref, dg_ref, acc_ref):
        i = pl.program_id(0)
        xf = x_ref[...]
        rstd = lax.rsqrt(jnp.mean(xf * xf, axis=-1, keepdims=True) + EPS)
        xhat = xf * rstd
        dhv = dh_ref[...]
        dxhat = dhv * g_ref[...]
        c = jnp.mean(dxhat * xhat, axis=-1, keepdims=True)
        dx_ref[...] = dres_ref[...] + rstd * (dxhat - xhat * c)
        part = jnp.sum((dhv * xhat).reshape(ROW_TILE // 8, 8, D_), axis=0)

        @pl.when(i == 0)
        def _():
            acc_ref[...] = part

        @pl.when(i > 0)
        def _():
            acc_ref[...] += part

        @pl.when(i == n - 1)
        def _():
            dg_ref[...] = jnp.sum(acc_ref[...], axis=0, keepdims=True)

    row = pl.BlockSpec((ROW_TILE, D_), lambda i: (i, 0))
    vec = pl.BlockSpec((1, D_), lambda i: (0, 0))
    return pl.pallas_call(
        body, name=name, grid=(n,), in_specs=[row, vec, row, row], out_specs=(row, vec),
        out_shape=(jax.ShapeDtypeStruct((S_, D_), F32), jax.ShapeDtypeStruct((1, D_), F32)),
        scratch_shapes=[pltpu.VMEM((8, D_), F32)],
        compiler_params=_cparams(("arbitrary",)),
    )(x, gain, dh, dres)


def _loss_head(name, y, target):
    S_, D_ = y.shape
    n = S_ // ROW_TILE

    def body(y_ref, t_ref, dy_ref, l_ref, acc_ref):
        i = pl.program_id(0)
        e = y_ref[...] - t_ref[...]
        dy_ref[...] = e * (1.0 / D_)
        part = jnp.sum((e * e).reshape(ROW_TILE // 8, 8, D_), axis=0)

        @pl.when(i == 0)
        def _():
            acc_ref[...] = part

        @pl.when(i > 0)
        def _():
            acc_ref[...] += part

        @pl.when(i == n - 1)
        def _():
            s = jnp.sum(acc_ref[...], axis=1, keepdims=True)
            l_ref[...] = jnp.sum(s, axis=0, keepdims=True) * (0.5 / D_)

    row = pl.BlockSpec((ROW_TILE, D_), lambda i: (i, 0))
    return pl.pallas_call(
        body, name=name, grid=(n,), in_specs=[row, row],
        out_specs=(row, pl.BlockSpec((1, 1), lambda i: (0, 0))),
        out_shape=(jax.ShapeDtypeStruct((S_, D_), F32), jax.ShapeDtypeStruct((1, 1), F32)),
        scratch_shapes=[pltpu.VMEM((8, D_), F32)],
        compiler_params=_cparams(("arbitrary",)),
    )(y, target)


def _rows_with_halo(ref, cols, i, n_chunks, before, after):
    r0 = pl.multiple_of(i * SEQ_CHUNK, SEQ_CHUNK)
    parts = []
    if before:
        h0 = pl.multiple_of(jnp.maximum(r0 - HALO, 0), 8)
        halo = ref[pl.ds(h0, HALO), cols]
        parts.append(jnp.where(i > 0, halo, jnp.zeros_like(halo)))
    parts.append(ref[pl.ds(r0, SEQ_CHUNK), cols])
    if after:
        a0 = pl.multiple_of(jnp.minimum(r0 + SEQ_CHUNK, (n_chunks - 1) * SEQ_CHUNK + SEQ_CHUNK - HALO), 8)
        halo = ref[pl.ds(a0, HALO), cols]
        parts.append(jnp.where(i < n_chunks - 1, halo, jnp.zeros_like(halo)))
    return parts[0] if len(parts) == 1 else jnp.concatenate(parts, axis=0)


def _shift_down(v, k):
    return pltpu.roll(v, k, 0)


def _shift_up(v, k):
    return pltpu.roll(v, v.shape[0] - k, 0)


def _pool_diff(xx, w, t_main):
    s = xx
    k = 1
    while k < w:
        s = s + _shift_down(s, k)
        k *= 2
    cnt = jnp.minimum(t_main + 1, w).astype(F32)
    return s[HALO:] / cnt - xx[HALO:]


def _pool_fwd(name, z, pool_mix, pool_scale):
    S_ = z.shape[0]
    n_chunks = S_ // SEQ_CHUNK

    def body(u_ref, mix_ref, sc_ref, y_ref):
        for g, w in enumerate(POOL_WINDOWS):
            cols = slice(g * POOL_G, (g + 1) * POOL_G)
            mixg = mix_ref[g].astype(BF16)
            scg = sc_ref[:, cols]

            def chunk(i, carry, cols=cols, mixg=mixg, scg=scg, w=w):
                r0 = pl.multiple_of(i * SEQ_CHUNK, SEQ_CHUNK)
                xx = _rows_with_halo(u_ref, cols, i, n_chunks, True, False)
                t = r0 + lax.broadcasted_iota(jnp.int32, (SEQ_CHUNK, POOL_G), 0)
                d = _pool_diff(xx, w, t)
                y = jnp.dot(d.astype(BF16), mixg, preferred_element_type=F32) * scg
                y_ref[pl.ds(r0, SEQ_CHUNK), cols] = y.astype(BF16)
                return carry

            lax.fori_loop(0, n_chunks, chunk, 0)

    return pl.pallas_call(
        body, name=name, grid=(1,),
        in_specs=[pl.BlockSpec((S_, POOL_W), lambda i: (0, 0)),
                  pl.BlockSpec((4, POOL_G, POOL_G), lambda i: (0, 0, 0)),
                  pl.BlockSpec((1, POOL_W), lambda i: (0, 0))],
        out_specs=pl.BlockSpec((S_, POOL_W), lambda i: (0, 0)),
        out_shape=jax.ShapeDtypeStruct((S_, POOL_W), BF16),
        compiler_params=_cparams(("arbitrary",)),
    )(z, pool_mix, pool_scale)


def _pool_bwd(name, z, dy, pool_mix, pool_scale):
    S_ = z.shape[0]
    n_chunks = S_ // SEQ_CHUNK
    rows_a = SEQ_CHUNK + HALO

    def body(u_ref, dy_ref, mix_ref, sc_ref, du_ref, dmix_ref, dsc_ref):
        for g, w in enumerate(POOL_WINDOWS):
            cols = slice(g * POOL_G, (g + 1) * POOL_G)
            mixg = mix_ref[g].astype(BF16)
            scg = sc_ref[:, cols]

            def chunk(i, carry, cols=cols, mixg=mixg, scg=scg, w=w):
                dmix_acc, dsc_acc = carry
                r0 = pl.multiple_of(i * SEQ_CHUNK, SEQ_CHUNK)
                xx = _rows_with_halo(u_ref, cols, i, n_chunks, True, False)
                t = r0 + lax.broadcasted_iota(jnp.int32, (SEQ_CHUNK, POOL_G), 0)
                d = _pool_diff(xx, w, t).astype(BF16)
                ypre = jnp.dot(d, mixg, preferred_element_type=F32)
                dyy = _rows_with_halo(dy_ref, cols, i, n_chunks, False, True)
                dys = (dyy * scg).astype(BF16)
                dsc_acc = dsc_acc + jnp.sum((dyy[:SEQ_CHUNK] * ypre).reshape(SEQ_CHUNK // 8, 8, POOL_G), axis=0)
                dmix_acc = dmix_acc + lax.dot_general(d, dys[:SEQ_CHUNK], _DIMS["tn"], preferred_element_type=F32)
                dd = lax.dot_general(dys, mixg, _DIMS["nt"], preferred_element_type=F32)
                ta = r0 + lax.broadcasted_iota(jnp.int32, (rows_a, POOL_G), 0)
                f = dd / jnp.minimum(ta + 1, w).astype(F32)
                k = 1
                while k < w:
                    f = f + _shift_up(f, k)
                    k *= 2
                du_ref[pl.ds(r0, SEQ_CHUNK), cols] = (f[:SEQ_CHUNK] - dd[:SEQ_CHUNK]).astype(BF16)
                return dmix_acc, dsc_acc

            dmix_acc, dsc_acc = lax.fori_loop(
                0, n_chunks, chunk, (jnp.zeros((POOL_G, POOL_G), F32), jnp.zeros((8, POOL_G), F32)))
            dmix_ref[g] = dmix_acc
            dsc_ref[:, cols] = jnp.sum(dsc_acc, axis=0, keepdims=True)

    full = pl.BlockSpec((S_, POOL_W), lambda i: (0, 0))
    mix_spec = pl.BlockSpec((4, POOL_G, POOL_G), lambda i: (0, 0, 0))
    vec = pl.BlockSpec((1, POOL_W), lambda i: (0, 0))
    return pl.pallas_call(
        body, name=name, grid=(1,), in_specs=[full, full, mix_spec, vec], out_specs=(full, mix_spec, vec),
        out_shape=(jax.ShapeDtypeStruct((S_, POOL_W), BF16), jax.ShapeDtypeStruct((4, POOL_G, POOL_G), F32),
                   jax.ShapeDtypeStruct((1, POOL_W), F32)),
        compiler_params=_cparams(("arbitrary",)),
    )(z, dy, pool_mix, pool_scale)


def _conv_specs(S_):
    slab = lambda off: pl.BlockSpec((S_, 128), lambda c, off=off: (0, off // 128 + c))
    return slab(OFF_CB), slab(OFF_CC), slab(OFF_CX)


def _conv_fwd(name, z, conv_w):
    S_ = z.shape[0]
    n_chunks = S_ // SEQ_CHUNK
    col = slice(0, 128)

    def body(b_ref, c_ref, x_ref, w_ref, y_ref):
        w0, w1, w2 = w_ref[0:1, :], w_ref[1:2, :], w_ref[2:3, :]

        def chunk(i, carry):
            r0 = pl.multiple_of(i * SEQ_CHUNK, SEQ_CHUNK)
            u = _rows_with_halo(c_ref, col, i, n_chunks, True, False) * _rows_with_halo(x_ref, col, i, n_chunks, True, False)
            y = w2 * u + w1 * _shift_down(u, 1) + w0 * _shift_down(u, 2)
            y_ref[pl.ds(r0, SEQ_CHUNK), :] = (b_ref[pl.ds(r0, SEQ_CHUNK), :] * y[HALO:]).astype(BF16)
            return carry

        lax.fori_loop(0, n_chunks, chunk, 0)

    sb, sc, sx = _conv_specs(S_)
    return pl.pallas_call(
        body, name=name, grid=(CONV_W // 128,),
        in_specs=[sb, sc, sx, pl.BlockSpec((3, 128), lambda c: (0, c))],
        out_specs=pl.BlockSpec((S_, 128), lambda c: (0, c)),
        out_shape=jax.ShapeDtypeStruct((S_, CONV_W), BF16),
        compiler_params=_cparams(("parallel",)),
    )(z, z, z, conv_w)


def _conv_bwd(name, z, dy, conv_w):
    S_ = z.shape[0]
    n_chunks = S_ // SEQ_CHUNK
    col = slice(0, 128)
    lo, hi = HALO, HALO + SEQ_CHUNK

    def body(b_ref, c_ref, x_ref, dy_ref, w_ref, db_ref, dc_ref, dx_ref, dw_ref):
        w0, w1, w2 = w_ref[0:1, :], w_ref[1:2, :], w_ref[2:3, :]

        def chunk(i, carry):
            a0, a1, a2 = carry
            r0 = pl.multiple_of(i * SEQ_CHUNK, SEQ_CHUNK)
            cc = _rows_with_halo(c_ref, col, i, n_chunks, True, True)
            xx = _rows_with_halo(x_ref, col, i, n_chunks, True, True)
            bb = _rows_with_halo(b_ref, col, i, n_chunks, True, True)
            dyy = _rows_with_halo(dy_ref, col, i, n_chunks, True, True)
            u = cc * xx
            u1 = _shift_down(u, 1)
            u2 = _shift_down(u, 2)
            y = w2 * u + w1 * u1 + w0 * u2
            dyv = dyy * bb
            du = w2 * dyv + w1 * _shift_up(dyv, 1) + w0 * _shift_up(dyv, 2)
            db_ref[pl.ds(r0, SEQ_CHUNK), :] = (dyy[lo:hi] * y[lo:hi]).astype(BF16)
            dc_ref[pl.ds(r0, SEQ_CHUNK), :] = (du[lo:hi] * xx[lo:hi]).astype(BF16)
            dx_ref[pl.ds(r0, SEQ_CHUNK), :] = (du[lo:hi] * cc[lo:hi]).astype(BF16)
            red = lambda v: jnp.sum(v.reshape(SEQ_CHUNK // 8, 8, 128), axis=0)
            dm = dyv[lo:hi]
            return a0 + red(dm * u2[lo:hi]), a1 + red(dm * u1[lo:hi]), a2 + red(dm * u[lo:hi])

        zero = jnp.zeros((8, 128), F32)
        a0, a1, a2 = lax.fori_loop(0, n_chunks, chunk, (zero, zero, zero))
        dw_ref[0:1, :] = jnp.sum(a0, axis=0, keepdims=True)
        dw_ref[1:2, :] = jnp.sum(a1, axis=0, keepdims=True)
        dw_ref[2:3, :] = jnp.sum(a2, axis=0, keepdims=True)

    sb, sc, sx = _conv_specs(S_)
    slab = pl.BlockSpec((S_, 128), lambda c: (0, c))
    wspec = pl.BlockSpec((3, 128), lambda c: (0, c))
    act = jax.ShapeDtypeStruct((S_, CONV_W), BF16)
    return pl.pallas_call(
        body, name=name, grid=(CONV_W // 128,), in_specs=[sb, sc, sx, slab, wspec],
        out_specs=(slab, slab, slab, wspec),
        out_shape=(act, act, act, jax.ShapeDtypeStruct((3, CONV_W), F32)),
        compiler_params=_cparams(("parallel",)),
    )(z, z, z, dy, conv_w)


def _head_norm(v, gain):
    rstd = lax.rsqrt(jnp.mean(v * v, axis=-1, keepdims=True) + EPS)
    xhat = v * rstd
    return xhat * gain, xhat, rstd


def _head_norm_bwd(dy, xhat, rstd, gain):
    dxhat = dy * gain
    c = jnp.mean(dxhat * xhat, axis=-1, keepdims=True)
    dv = rstd * (dxhat - xhat * c)
    dg = jnp.sum((dy * xhat).reshape(ATTN_BLK // 8, 8, HEAD), axis=0)
    return dv, dg


def _band_masks():
    qi = lax.broadcasted_iota(jnp.int32, (ATTN_BLK, ATTN_BLK), 0)
    ki = lax.broadcasted_iota(jnp.int32, (ATTN_BLK, ATTN_BLK), 1)
    return ki <= qi, ki >= qi


def _attn_specs(S_, g, dil):
    rows = ATTN_BLK * dil
    nb = S_ // rows
    cq, ck, cv = (OFF_Q // PAIR_W + g * PAIRS, OFF_K // PAIR_W + g * PAIRS, OFF_V // PAIR_W + g * PAIRS)
    return rows, nb, cq, ck, cv


PAIR_W = 128
PAIR = PAIR_W // HEAD
PAIRS = HPG // PAIR


def _rows_of(ref, r, dil):
    if dil == 1:
        return ref[...]
    return ref[pl.ds(r, ATTN_BLK, stride=dil), :]


def _put_rows(ref, r, dil, val):
    if dil == 1:
        ref[...] = val.astype(ref.dtype)
    else:
        ref[pl.ds(r, ATTN_BLK, stride=dil), :] = val.astype(ref.dtype)


def _attn_fwd(name, z, q_gain, k_gain, g, dil):
    S_ = z.shape[0]
    rows, nb, cq, ck, cv = _attn_specs(S_, g, dil)
    scale = HEAD ** -0.5

    def body(q_ref, kc_ref, kp_ref, vc_ref, vp_ref, gq_ref, gk_ref, o_ref, l_ref):
        n = pl.program_id(1)
        m_cur, m_prev = _band_masks()
        m_prev = jnp.logical_and(m_prev, n > 0)
        gq, gk = gq_ref[...], gk_ref[...]
        for r in range(dil):
            q, kc, kp = _rows_of(q_ref, r, dil), _rows_of(kc_ref, r, dil), _rows_of(kp_ref, r, dil)
            vc, vp = _rows_of(vc_ref, r, dil), _rows_of(vp_ref, r, dil)
            outs, lses = [], []
            for h in range(PAIR):
                hs = slice(h * HEAD, (h + 1) * HEAD)
                qn = _head_norm(q[:, hs], gq)[0].astype(BF16)
                kcn = _head_norm(kc[:, hs], gk)[0].astype(BF16)
                kpn = _head_norm(kp[:, hs], gk)[0].astype(BF16)
                s_c = lax.dot_general(qn, kcn, _DIMS["nt"], preferred_element_type=F32) * scale
                s_p = lax.dot_general(qn, kpn, _DIMS["nt"], preferred_element_type=F32) * scale
                s_c = jnp.where(m_cur, s_c, MASK)
                s_p = jnp.where(m_prev, s_p, MASK)
                m = jnp.maximum(jnp.max(s_c, axis=-1, keepdims=True), jnp.max(s_p, axis=-1, keepdims=True))
                p_c = jnp.exp(s_c - m)
                p_p = jnp.exp(s_p - m)
                den = jnp.sum(p_c, axis=-1, keepdims=True) + jnp.sum(p_p, axis=-1, keepdims=True)
                acc = (jnp.dot(p_c.astype(BF16), vc[:, hs].astype(BF16), preferred_element_type=F32)
                       + jnp.dot(p_p.astype(BF16), vp[:, hs].astype(BF16), preferred_element_type=F32))
                outs.append(acc / den)
                lses.append(jnp.broadcast_to(m + jnp.log(den), (ATTN_BLK, HEAD)))
            _put_rows(o_ref, r, dil, jnp.concatenate(outs, axis=1))
            _put_rows(l_ref, r, dil, jnp.concatenate(lses, axis=1))

    cur = lambda c: pl.BlockSpec((rows, PAIR_W), lambda hp, n, c=c: (n, c + hp))
    prev = lambda c: pl.BlockSpec((rows, PAIR_W), lambda hp, n, c=c: (jnp.maximum(n - 1, 0), c + hp))
    gspec = pl.BlockSpec((1, HEAD), lambda hp, n: (0, 0))
    shp = jax.ShapeDtypeStruct((S_, ATTN_OUT), F32)
    return pl.pallas_call(
        body, name=name, grid=(PAIRS, nb),
        in_specs=[cur(cq), cur(ck), prev(ck), cur(cv), prev(cv), gspec, gspec],
        out_specs=(cur(0), cur(0)), out_shape=(shp, shp),
        compiler_params=_cparams(("parallel", "parallel")),
    )(z, z, z, z, z, q_gain, k_gain)


def _attn_combine(name, os_, ls_):
    S_ = os_[0].shape[0]

    def body(o0, o1, o2, l0, l1, l2, o_ref, l_ref):
        a, b, c = l0[...], l1[...], l2[...]
        m = jnp.maximum(jnp.maximum(a, b), c)
        ea, eb, ec = jnp.exp(a - m), jnp.exp(b - m), jnp.exp(c - m)
        zsum = ea + eb + ec
        o_ref[...] = (ea * o0[...] + eb * o1[...] + ec * o2[...]) / zsum
        l_ref[...] = m + jnp.log(zsum)

    row = pl.BlockSpec((ROW_TILE, ATTN_OUT), lambda i: (i, 0))
    shp = jax.ShapeDtypeStruct((S_, ATTN_OUT), F32)
    return pl.pallas_call(
        body, name=name, grid=(S_ // ROW_TILE,), in_specs=[row] * 6, out_specs=(row, row), out_shape=(shp, shp),
        compiler_params=_cparams(("parallel",)),
    )(*os_, *ls_)


def _attn_bwd(name, z, q_gain, k_gain, do, o, lse, g, dil):
    S_ = z.shape[0]
    rows, nb, cq, ck, cv = _attn_specs(S_, g, dil)
    scale = HEAD ** -0.5

    def body(q_ref, kc_ref, kp_ref, vc_ref, vp_ref, gq_ref, gk_ref, do_ref, o_ref, l_ref,
             dq_ref, dk_ref, dv_ref, dgq_ref, dgk_ref, ck_ref, cvv_ref, gq_acc, gk_acc):
        hp = pl.program_id(0)
        n = pl.program_id(1)
        live = n < nb
        m_cur, m_prev = _band_masks()
        m_cur = jnp.logical_and(m_cur, live)
        m_prev = jnp.logical_and(m_prev, jnp.logical_and(n > 0, live))
        gq, gk = gq_ref[...], gk_ref[...]

        @pl.when(n == 0)
        def _():
            ck_ref[...] = jnp.zeros_like(ck_ref)
            cvv_ref[...] = jnp.zeros_like(cvv_ref)

        @pl.when(jnp.logical_and(n == 0, hp == 0))
        def _():
            gq_acc[...] = jnp.zeros_like(gq_acc)
            gk_acc[...] = jnp.zeros_like(gk_acc)

        for r in range(dil):
            q, kc, kp = _rows_of(q_ref, r, dil), _rows_of(kc_ref, r, dil), _rows_of(kp_ref, r, dil)
            vc, vp = _rows_of(vc_ref, r, dil), _rows_of(vp_ref, r, dil)
            dov, ov, lv = _rows_of(do_ref, r, dil), _rows_of(o_ref, r, dil), _rows_of(l_ref, r, dil)
            carry_k, carry_v = _rows_of(ck_ref, r, dil), _rows_of(cvv_ref, r, dil)
            dqs, dks, dvs, nck, ncv = [], [], [], [], []
            dgq = jnp.zeros((8, HEAD), F32)
            dgk = jnp.zeros((8, HEAD), F32)
            for h in range(PAIR):
                hs = slice(h * HEAD, (h + 1) * HEAD)
                qn, q_hat, q_rstd = _head_norm(q[:, hs], gq)
                kcn = _head_norm(kc[:, hs], gk)[0].astype(BF16)
                kpn, kp_hat, kp_rstd = _head_norm(kp[:, hs], gk)
                qn, kpn = qn.astype(BF16), kpn.astype(BF16)
                do_h = dov[:, hs]
                do_b = do_h.astype(BF16)
                delta = jnp.sum(do_h * ov[:, hs], axis=-1, keepdims=True)
                l_h = lv[:, h * HEAD:h * HEAD + 1]
                s_c = lax.dot_general(qn, kcn, _DIMS["nt"], preferred_element_type=F32) * scale
                s_p = lax.dot_general(qn, kpn, _DIMS["nt"], preferred_element_type=F32) * scale
                p_c = jnp.where(m_cur, jnp.exp(jnp.where(m_cur, s_c, MASK) - l_h), 0.0)
                p_p = jnp.where(m_prev, jnp.exp(jnp.where(m_prev, s_p, MASK) - l_h), 0.0)
                dp_c = lax.dot_general(do_b, vc[:, hs].astype(BF16), _DIMS["nt"], preferred_element_type=F32)
                dp_p = lax.dot_general(do_b, vp[:, hs].astype(BF16), _DIMS["nt"], preferred_element_type=F32)
                ds_c = (p_c * (dp_c - delta) * scale).astype(BF16)
                ds_p = (p_p * (dp_p - delta) * scale).astype(BF16)
                dqn = (jnp.dot(ds_c, kcn, preferred_element_type=F32)
                       + jnp.dot(ds_p, kpn, preferred_element_type=F32))
                dq_h, dg = _head_norm_bwd(dqn, q_hat, q_rstd, gq)
                dgq = dgq + dg
                dqs.append(dq_h)
                dkn_prev = carry_k[:, hs] + lax.dot_general(ds_p, qn, _DIMS["tn"], preferred_element_type=F32)
                dk_h, dg = _head_norm_bwd(dkn_prev, kp_hat, kp_rstd, gk)
                dgk = dgk + dg
                dks.append(dk_h)
                dvs.append(carry_v[:, hs] + lax.dot_general(p_p.astype(BF16), do_b, _DIMS["tn"],
                                                            preferred_element_type=F32))
                nck.append(lax.dot_general(ds_c, qn, _DIMS["tn"], preferred_element_type=F32))
                ncv.append(lax.dot_general(p_c.astype(BF16), do_b, _DIMS["tn"], preferred_element_type=F32))
            dq_all = jnp.concatenate(dqs, axis=1)

            @pl.when(live)
            def _(dq_all=dq_all, r=r):
                _put_rows(dq_ref, r, dil, dq_all)

            _put_rows(dk_ref, r, dil, jnp.concatenate(dks, axis=1))
            _put_rows(dv_ref, r, dil, jnp.concatenate(dvs, axis=1))
            _put_rows(ck_ref, r, dil, jnp.concatenate(nck, axis=1))
            _put_rows(cvv_ref, r, dil, jnp.concatenate(ncv, axis=1))
            gq_acc[...] += dgq
            gk_acc[...] += dgk

        @pl.when(jnp.logical_and(n == nb, hp == PAIRS - 1))
        def _():
            dgq_ref[...] = jnp.sum(gq_acc[...], axis=0, keepdims=True)
            dgk_ref[...] = jnp.sum(gk_acc[...], axis=0, keepdims=True)

    last = nb - 1
    cur = lambda c: pl.BlockSpec((rows, PAIR_W), lambda hp, n, c=c: (jnp.minimum(n, last), c + hp))
    prev = lambda c: pl.BlockSpec((rows, PAIR_W), lambda hp, n, c=c: (jnp.maximum(n - 1, 0), c + hp))
    gspec = pl.BlockSpec((1, HEAD), lambda hp, n: (0, 0))
    act = jax.ShapeDtypeStruct((S_, ATTN_OUT), F32)
    vec = jax.ShapeDtypeStruct((1, HEAD), F32)
    return pl.pallas_call(
        body, name=name, grid=(PAIRS, nb + 1),
        in_specs=[cur(cq), cur(ck), prev(ck), cur(cv), prev(cv), gspec, gspec, cur(0), cur(0), cur(0)],
        out_specs=(cur(0), prev(0), prev(0), gspec, gspec),
        out_shape=(act, act, act, vec, vec),
        scratch_shapes=[pltpu.VMEM((rows, PAIR_W), F32), pltpu.VMEM((rows, PAIR_W), F32),
                        pltpu.VMEM((8, HEAD), F32), pltpu.VMEM((8, HEAD), F32)],
        compiler_params=_cparams(("arbitrary", "arbitrary")),
    )(z, z, z, z, z, q_gain, k_gain, do, o, lse)


MIX_TN = 256


def _sigmoid(v):
    return 1.0 / (1.0 + jnp.exp(-v))


def _mix_fwd(name, z, b_gate, ys, ws):
    S_ = z.shape[0]
    tm, tn = ROW_TILE, MIX_TN
    gblk = OFF_GATE // tn

    def body(yp, yc, ya, wp, wc, wa, g0, g1, g2, b0, b1, b2, m_ref):
        acc = None
        for y_ref, w_ref, g_ref, b_ref in ((yp, wp, g0, b0), (yc, wc, g1, b1), (ya, wa, g2, b2)):
            u = jnp.dot(y_ref[...].astype(BF16), w_ref[...], preferred_element_type=F32)
            t = _sigmoid(g_ref[...] + b_ref[...]) * u
            acc = t if acc is None else acc + t
        m_ref[...] = acc.astype(BF16)

    yspec = lambda w: pl.BlockSpec((tm, w), lambda i, j: (i, 0))
    wspec = lambda w: pl.BlockSpec((w, tn), lambda i, j: (0, j))
    gspec = lambda b: pl.BlockSpec((tm, tn), lambda i, j, b=b: (i, gblk + b * (D // tn) + j))
    bspec = lambda b: pl.BlockSpec((1, tn), lambda i, j, b=b: (0, b * (D // tn) + j))
    return pl.pallas_call(
        body, name=name, grid=(S_ // tm, D // tn),
        in_specs=[yspec(POOL_W), yspec(CONV_W), yspec(ATTN_OUT), wspec(POOL_W), wspec(CONV_W), wspec(ATTN_OUT),
                  gspec(0), gspec(1), gspec(2), bspec(0), bspec(1), bspec(2)],
        out_specs=pl.BlockSpec((tm, tn), lambda i, j: (i, j)),
        out_shape=jax.ShapeDtypeStruct((S_, D), BF16),
        compiler_params=_cparams(("parallel", "parallel")),
    )(*ys, *ws, z, z, z, b_gate, b_gate, b_gate)


def _mix_bwd(name, z, b_gate, y, w, dmerged, branch):
    S_ = z.shape[0]
    tm, tn = ROW_TILE, MIX_TN
    width = y.shape[1]
    gblk = OFF_GATE // tn + branch * (D // tn)
    ni = S_ // tm

    def body(y_ref, w_ref, g_ref, b_ref, dm_ref, du_ref, dg_ref, db_ref, acc_ref):
        i = pl.program_id(1)
        u = jnp.dot(y_ref[...].astype(BF16), w_ref[...], preferred_element_type=F32)
        sg = _sigmoid(g_ref[...] + b_ref[...])
        dm = dm_ref[...]
        du_ref[...] = (sg * dm).astype(BF16)
        dpre = dm * u * sg * (1.0 - sg)
        dg_ref[...] = dpre.astype(BF16)
        part = jnp.sum(dpre.reshape(tm // 8, 8, tn), axis=0)

        @pl.when(i == 0)
        def _():
            acc_ref[...] = part

        @pl.when(i > 0)
        def _():
            acc_ref[...] += part

        @pl.when(i == ni - 1)
        def _():
            db_ref[...] = jnp.sum(acc_ref[...], axis=0, keepdims=True)

    blk = pl.BlockSpec((tm, tn), lambda j, i: (i, j))
    vec = pl.BlockSpec((1, tn), lambda j, i: (0, j))
    act = jax.ShapeDtypeStruct((S_, D), BF16)
    return pl.pallas_call(
        body, name=name, grid=(D // tn, ni),
        in_specs=[pl.BlockSpec((tm, width), lambda j, i: (i, 0)), pl.BlockSpec((width, tn), lambda j, i: (0, j)),
                  pl.BlockSpec((tm, tn), lambda j, i: (i, gblk + j)),
                  pl.BlockSpec((1, tn), lambda j, i: (0, branch * (D // tn) + j)), blk],
        out_specs=(blk, blk, vec), out_shape=(act, act, jax.ShapeDtypeStruct((1, D), F32)),
        scratch_shapes=[pltpu.VMEM((8, tn), F32)],
        compiler_params=_cparams(("parallel", "arbitrary")),
    )(y, w, z, b_gate, dmerged)


def _relu2_epilogue(acc):
    r = jnp.maximum(acc, 0.0)
    return acc, r * r


def _layer_fwd(l, x, p):
    t = f"l{l}_"
    h = _rmsnorm_fwd(t + "norm_mix", x, p["norm_mix"])
    z = _matmul(t + "in_proj", h, p["w_in"], "nn", 512, 3712, 1024)
    y_pool = _pool_fwd(t + "pool", z, p["pool_mix"], p["pool_scale"])
    y_conv = _conv_fwd(t + "conv", z, p["conv_w"])
    os_, ls_ = [], []
    for g, (_, dil) in enumerate(ATTN_GROUPS):
        o_g, l_g = _attn_fwd(t + f"attn{g}", z, p["q_gain"], p["k_gain"], g, dil)
        os_.append(o_g)
        ls_.append(l_g)
    y_attn, lse = _attn_combine(t + "attn_mix", os_, ls_)
    merged = _mix_fwd(t + "merge", z, p["b_gate"], (y_pool, y_conv, y_attn),
                      (p["w_pool_up"], p["w_conv_out"], p["w_attn_up"]))
    x1 = _matmul(t + "out_proj", merged, p["w_o"], "nn", 512, 1024, 1024,
                 extras=((x, "mn"),), epilogue=lambda acc, xr: (xr + acc,))
    h2 = _rmsnorm_fwd(t + "norm_mlp", x1, p["norm_mlp"])
    a, r = _matmul(t + "ff1", h2, p["w_ff1"], "nn", 512, 1024, 1024, out_dtypes=(F32, BF16),
                   epilogue=_relu2_epilogue)
    x2 = _matmul(t + "ff2", r, p["w_ff2"], "nn", 512, 1024, 1024,
                 extras=((x1, "mn"),), epilogue=lambda acc, xr: (xr + acc,))
    saved = dict(x=x, h=h, z=z, y_pool=y_pool, y_conv=y_conv, y_attn=y_attn, lse=lse, merged=merged,
                 x1=x1, h2=h2, a=a, r=r)
    return x2, saved


def _layer_bwd(l, dx2, p, s):
    t = f"l{l}_b_"
    g = {}
    da = _matmul(t + "d_ff2_in", dx2, p["w_ff2"], "nt", 512, 1024, 1024, out_dtypes=(BF16,),
                 extras=((s["a"], "mn"),), epilogue=lambda acc, a: (acc * (2.0 * jnp.maximum(a, 0.0)),))
    g["w_ff2"] = _matmul(t + "dw_ff2", s["r"], dx2, "tn", 512, 1024, 512)
    dh2 = _matmul(t + "d_ff1_in", da, p["w_ff1"], "nt", 512, 1024, 1024)
    g["w_ff1"] = _matmul(t + "dw_ff1", s["h2"], da, "tn", 512, 1024, 512)
    dx1, g["norm_mlp"] = _rmsnorm_bwd(t + "norm_mlp", s["x1"], p["norm_mlp"], dh2, dx2)
    dmerged = _matmul(t + "d_out_proj_in", dx1, p["w_o"], "nt", 512, 1024, 1024)
    g["w_o"] = _matmul(t + "dw_o", s["merged"], dx1, "tn", 512, 1024, 512)
    ys = (s["y_pool"], s["y_conv"], s["y_attn"])
    names = ("w_pool_up", "w_conv_out", "w_attn_up")
    dys, dgates, dbs = [], [], []
    for b in range(3):
        du, dgz, db = _mix_bwd(t + f"merge{b}", s["z"], p["b_gate"], ys[b], p[names[b]], dmerged, b)
        width = ys[b].shape[1]
        dys.append(_matmul(t + f"d_up{b}_in", du, p[names[b]], "nt", 512, width, 1024))
        g[names[b]] = _matmul(t + f"dw_up{b}", ys[b], du, "tn", width, 1024, 512)
        dgates.append(dgz)
        dbs.append(db)
    g["b_gate"] = jnp.concatenate(dbs, axis=1)
    du_pool, g["pool_mix"], g["pool_scale"] = _pool_bwd(t + "pool", s["z"], dys[0], p["pool_mix"], p["pool_scale"])
    dcb, dcc, dcx, g["conv_w"] = _conv_bwd(t + "conv", s["z"], dys[1], p["conv_w"])
    dqs, dks, dvs = [], [], []
    gq = gk = None
    for gi, (_, dil) in enumerate(ATTN_GROUPS):
        dq, dk, dv, dgq, dgk = _attn_bwd(t + f"attn{gi}", s["z"], p["q_gain"], p["k_gain"], dys[2], s["y_attn"],
                                         s["lse"], gi, dil)
        dqs.append(dq)
        dks.append(dk)
        dvs.append(dv)
        gq = dgq if gq is None else gq + dgq
        gk = dgk if gk is None else gk + dgk
    g["q_gain"], g["k_gain"] = gq, gk
    dz = jnp.concatenate([du_pool, dcb, dcc, dcx] + [v.astype(BF16) for v in dqs + dks + dvs] + dgates, axis=1)
    dh = _matmul(t + "d_in_proj_in", dz, p["w_in"], "nt", 512, 1024, 3712)
    g["w_in"] = _matmul(t + "dw_in", s["h"], dz, "tn", 512, 3712, 512)
    dx, g["norm_mix"] = _rmsnorm_bwd(t + "norm_mix", s["x"], p["norm_mix"], dh, dx1)
    return dx, g


def _local_step(x, target, params):
    saved = []
    act = x
    for l in range(DEPTH):
        act, s = _layer_fwd(l, act, params[l])
        saved.append(s)
    dy, loss = _loss_head("loss_head", act, target)
    grads = [None] * DEPTH
    for l in reversed(range(DEPTH)):
        dy, grads[l] = _layer_bwd(l, dy, params[l], saved[l])
    return loss, dy, grads


ANY = pl.BlockSpec(memory_space=pl.ANY)


def _mesh_pos():
    return lax.axis_index("x"), lax.axis_index("y"), lax.axis_index("c")


def _other_chips(x, y):
    return [(1 - x, y), (x, 1 - y), (1 - x, 1 - y)]


def _all_gather(name, shard):
    R, C = shard.shape

    def body(x_ref, out_ref, send_sems, recv_sems, local_sem):
        x, y, c = _mesh_pos()
        me, sibling = (x, y, c), (x, y, 1 - c)
        chips = _other_chips(x, y)

        def slot(px, py, pc):
            return out_ref.at[4 * px + 2 * py + pc]

        def copy(k, block, to, src=None):
            return pltpu.make_async_remote_copy(
                src_ref=slot(*block) if src is None else src, dst_ref=slot(*block),
                send_sem=send_sems.at[k], recv_sem=recv_sems.at[k], device_id=to, device_id_type=MESH)

        mine = pltpu.make_async_copy(x_ref, slot(*me), local_sem)
        mine.start()
        first = [copy(0, me, sibling, src=x_ref)]
        first += [copy(1 + j, me, (*chip, c), src=x_ref) for j, chip in enumerate(chips)]
        for cp in first:
            cp.start()
        passed = [copy(4 + j, (*chip, c), sibling) for j, chip in enumerate(chips)]
        for j, chip in enumerate(chips):
            copy(1 + j, (*chip, c), me).wait_recv()
            passed[j].start()
        copy(0, sibling, me).wait_recv()
        for j, chip in enumerate(chips):
            copy(4 + j, (*chip, 1 - c), me).wait_recv()
        for cp in first + passed:
            cp.wait_send()
        mine.wait()

    return pl.pallas_call(
        body, name=name, out_shape=jax.ShapeDtypeStruct((N_DEV, R, C), shard.dtype),
        in_specs=[ANY], out_specs=ANY,
        scratch_shapes=[pltpu.SemaphoreType.DMA((7,)), pltpu.SemaphoreType.DMA((7,)), pltpu.SemaphoreType.DMA],
    )(shard)


def _rs_sibling_exchange(name, big, small):
    _, _, R, C = big.shape

    def body(big_ref, small_ref, rbig_ref, rsmall_ref, send_sems, recv_sems):
        x, y, c = _mesh_pos()
        sibling = (x, y, 1 - c)
        cps = [
            pltpu.make_async_remote_copy(src_ref=big_ref.at[1 - c], dst_ref=rbig_ref, send_sem=send_sems.at[0],
                                         recv_sem=recv_sems.at[0], device_id=sibling, device_id_type=MESH),
            pltpu.make_async_remote_copy(src_ref=small_ref, dst_ref=rsmall_ref, send_sem=send_sems.at[1],
                                         recv_sem=recv_sems.at[1], device_id=sibling, device_id_type=MESH),
        ]
        for cp in cps:
            cp.start()
        for cp in cps:
            cp.wait()

    return pl.pallas_call(
        body, name=name,
        out_shape=(jax.ShapeDtypeStruct((4, R, C), big.dtype), jax.ShapeDtypeStruct(small.shape, small.dtype)),
        in_specs=[ANY, ANY], out_specs=(ANY, ANY),
        scratch_shapes=[pltpu.SemaphoreType.DMA((2,)), pltpu.SemaphoreType.DMA((2,))],
    )(big, small)


RS_ROWS = 2560


def _rs_chip_sum(name, ids, big, rbig):
    _, _, R, C = big.shape
    assert R % RS_ROWS == 0

    def body(ids_ref, mine_ref, theirs_ref, t16_ref, own_ref):
        p = pl.program_id(1)
        s = mine_ref[...] + theirs_ref[...]
        t16_ref[...] = s.astype(BF16)

        @pl.when(p == ids_ref[1])
        def _():
            own_ref[...] = s

    grid_spec = pltpu.PrefetchScalarGridSpec(
        num_scalar_prefetch=1, grid=(R // RS_ROWS, 4),
        in_specs=[pl.BlockSpec((None, None, RS_ROWS, C), lambda i, p, ids: (ids[0], p, i, 0)),
                  pl.BlockSpec((None, RS_ROWS, C), lambda i, p, ids: (p, i, 0))],
        out_specs=(pl.BlockSpec((None, RS_ROWS, C), lambda i, p, ids: (p, i, 0)),
                   pl.BlockSpec((RS_ROWS, C), lambda i, p, ids: (i, 0))))
    return pl.pallas_call(
        body, name=name, grid_spec=grid_spec,
        out_shape=(jax.ShapeDtypeStruct((4, R, C), BF16), jax.ShapeDtypeStruct((R, C), F32)),
        compiler_params=_cparams(("parallel", "arbitrary")),
    )(ids, big, rbig)


def _add2(name, a, b):
    def body(a_ref, b_ref, o_ref):
        o_ref[...] = a_ref[...] + b_ref[...]

    full = pl.BlockSpec(a.shape, lambda i: (0, 0))
    return pl.pallas_call(body, name=name, grid=(1,), in_specs=[full, full], out_specs=full,
                          out_shape=jax.ShapeDtypeStruct(a.shape, a.dtype))(a, b)


def _rs_chip_exchange(name, t16, ts):
    _, R, C = t16.shape
    Rs = ts.shape[0]

    def body(t_ref, s_ref, r_ref, rs_ref, send_sems, recv_sems, local_sems):
        x, y, c = _mesh_pos()
        p_me = 2 * x + y
        chips = _other_chips(x, y)
        local = [pltpu.make_async_copy(t_ref.at[p_me], r_ref.at[p_me], local_sems.at[0]),
                 pltpu.make_async_copy(s_ref, rs_ref.at[p_me], local_sems.at[1])]
        for cp in local:
            cp.start()
        sends = []
        for j, (px, py) in enumerate(chips):
            sends.append(pltpu.make_async_remote_copy(
                src_ref=t_ref.at[2 * px + py], dst_ref=r_ref.at[p_me], send_sem=send_sems.at[2 * j],
                recv_sem=recv_sems.at[2 * j], device_id=(px, py, c), device_id_type=MESH))
            sends.append(pltpu.make_async_remote_copy(
                src_ref=s_ref, dst_ref=rs_ref.at[p_me], send_sem=send_sems.at[2 * j + 1],
                recv_sem=recv_sems.at[2 * j + 1], device_id=(px, py, c), device_id_type=MESH))
        for cp in sends:
            cp.start()
        for j, (px, py) in enumerate(chips):
            pltpu.make_async_remote_copy(
                src_ref=t_ref.at[p_me], dst_ref=r_ref.at[2 * px + py], send_sem=send_sems.at[2 * j],
                recv_sem=recv_sems.at[2 * j], device_id=(px, py, c), device_id_type=MESH).wait_recv()
            pltpu.make_async_remote_copy(
                src_ref=s_ref, dst_ref=rs_ref.at[2 * px + py], send_sem=send_sems.at[2 * j + 1],
                recv_sem=recv_sems.at[2 * j + 1], device_id=(px, py, c), device_id_type=MESH).wait_recv()
        for cp in sends:
            cp.wait_send()
        for cp in local:
            cp.wait()

    return pl.pallas_call(
        body, name=name,
        out_shape=(jax.ShapeDtypeStruct((4, R, C), t16.dtype), jax.ShapeDtypeStruct((4, Rs, C), ts.dtype)),
        in_specs=[ANY, ANY], out_specs=(ANY, ANY),
        scratch_shapes=[pltpu.SemaphoreType.DMA((6,)), pltpu.SemaphoreType.DMA((6,)), pltpu.SemaphoreType.DMA((2,))],
    )(t16, ts)


def _rs_final_sum(name, ids, recv, own, rows):
    _, R, C = recv.shape
    assert R % rows == 0

    def body(ids_ref, r_ref, own_ref, o_ref):
        acc = None
        for p in range(4):
            term = jnp.where(ids_ref[1] == p, own_ref[...], r_ref[p].astype(F32))
            acc = term if acc is None else acc + term
        o_ref[...] = acc

    grid_spec = pltpu.PrefetchScalarGridSpec(
        num_scalar_prefetch=1, grid=(R // rows,),
        in_specs=[pl.BlockSpec((4, rows, C), lambda i, ids: (0, i, 0)), pl.BlockSpec((rows, C), lambda i, ids: (i, 0))],
        out_specs=pl.BlockSpec((rows, C), lambda i, ids: (i, 0)))
    return pl.pallas_call(
        body, name=name, grid_spec=grid_spec, out_shape=jax.ShapeDtypeStruct((R, C), F32),
        compiler_params=_cparams(("parallel",)),
    )(ids, recv, own)


def _reduce_scatter(big, small):
    x, y, c = _mesh_pos()
    ids = jnp.stack([c, 2 * x + y]).astype(jnp.int32)
    rbig, rsmall = _rs_sibling_exchange("rs_sibling", big, small)
    t16, own = _rs_chip_sum("rs_chip_sum", ids, big, rbig)
    ts = _add2("rs_chip_sum_small", small, rsmall)
    r16, rs = _rs_chip_exchange("rs_chips", t16, ts)
    out_big = _rs_final_sum("rs_final", ids, r16, own, RS_ROWS)
    out_small = _rs_final_sum("rs_final_small", ids, rs, ts, small.shape[0])
    return out_big, out_small


def _adamw(name, w, g, m, v):
    R, C = w.shape
    tr = max(t for t in range(8, 513, 8) if R % t == 0)
    c1 = 1.0 - ADAM_B1 ** ADAM_STEP
    c2 = 1.0 - ADAM_B2 ** ADAM_STEP

    def body(w_ref, g_ref, m_ref, v_ref, d_ref, nm_ref, nv_ref):
        gv = g_ref[...]
        nm = ADAM_B1 * m_ref[...] + (1.0 - ADAM_B1) * gv
        nv = ADAM_B2 * v_ref[...] + (1.0 - ADAM_B2) * (gv * gv)
        d_ref[...] = -ADAM_LR * ((nm / c1) / (jnp.sqrt(nv / c2) + ADAM_EPS) + ADAM_WD * w_ref[...])
        nm_ref[...] = nm
        nv_ref[...] = nv

    blk = pl.BlockSpec((tr, C), lambda i: (i, 0))
    shp = jax.ShapeDtypeStruct((R, C), F32)
    return pl.pallas_call(
        body, name=name, grid=(R // tr,), in_specs=[blk] * 4, out_specs=(blk,) * 3, out_shape=(shp,) * 3,
        compiler_params=_cparams(("parallel",)),
    )(w, g, m, v)


BIG = (("w_in", (DEPTH, D, IN_COLS // N_DEV), 2), ("w_pool_up", (DEPTH, POOL_W, D // N_DEV), 2),
       ("w_conv_out", (DEPTH, CONV_W, D // N_DEV), 2), ("w_attn_up", (DEPTH, ATTN_OUT, D // N_DEV), 2),
       ("w_o", (DEPTH, D // N_DEV, D), 1), ("w_ff1", (DEPTH, D, D_FF // N_DEV), 2),
       ("w_ff2", (DEPTH, D_FF // N_DEV, D), 1))
SMALL = (("norm_mix", (DEPTH, D)), ("b_gate", (DEPTH, 3 * D)), ("pool_mix", (DEPTH, 4, POOL_G, POOL_G)),
         ("pool_scale", (DEPTH, POOL_W)), ("conv_w", (DEPTH, 3, CONV_W)), ("q_gain", (DEPTH, HEAD)),
         ("k_gain", (DEPTH, HEAD)), ("norm_mlp", (DEPTH, D)))
CONV_ROWS = 16


def _rows(shape):
    n = 1
    for s in shape:
        n *= s
    assert n % 128 == 0
    return n // 128


def _pack_weight_shards(w):
    parts = [w[name].astype(BF16).reshape(-1, 128) for name, _, _ in BIG]
    conv = lax.bitcast_convert_type(w["conv_w"], BF16).reshape(-1, 128)
    parts.append(jnp.pad(conv, ((0, CONV_ROWS - conv.shape[0]), (0, 0))))
    return jnp.concatenate(parts, axis=0)


def _unpack_gathered(gathered, small_w):
    full = {}
    off = 0
    for name, shp, axis in BIG:
        n = _rows(shp)
        blk = gathered[:, off:off + n].reshape((N_DEV,) + shp)
        off += n
        if axis == 2:
            full[name] = jnp.transpose(blk, (1, 2, 0, 3)).reshape(shp[0], shp[1], N_DEV * shp[2])
        else:
            full[name] = jnp.transpose(blk, (1, 0, 2, 3)).reshape(shp[0], N_DEV * shp[1], shp[2])
    conv = gathered[:, off:off + 6].reshape(N_DEV, DEPTH, 3, CONV_W // N_DEV, 2)
    conv = lax.bitcast_convert_type(conv, F32)
    full["conv_w"] = jnp.transpose(conv, (1, 2, 0, 3)).reshape(DEPTH, 3, CONV_W)
    params = []
    for l in range(DEPTH):
        p = {name: full[name][l] for name in full}
        for name in ("norm_mix", "b_gate", "pool_scale", "q_gain", "k_gain", "norm_mlp"):
            p[name] = small_w[name][l][None, :]
        p["pool_mix"] = small_w["pool_mix"][l]
        params.append(p)
    return params


def _pack_grads(grads):
    parts = []
    for name, shp, axis in BIG:
        full = jnp.stack([grads[l][name] for l in range(DEPTH)])
        if axis == 2:
            blk = jnp.transpose(full.reshape(shp[0], shp[1], N_DEV, shp[2]), (2, 0, 1, 3))
        else:
            blk = jnp.transpose(full.reshape(shp[0], N_DEV, shp[1], shp[2]), (1, 0, 2, 3))
        blk = blk.reshape(4, 2, _rows(shp), 128)
        parts.append(jnp.transpose(blk, (1, 0, 2, 3)))
    big = jnp.concatenate(parts, axis=2)
    flat = [jnp.stack([grads[l][name] for l in range(DEPTH)]).reshape(-1) for name, _ in SMALL]
    flat = jnp.concatenate(flat)
    pad = (-flat.shape[0]) % (8 * 128)
    small = jnp.pad(flat, (0, pad)).reshape(-1, 128)
    return big, small


def _unpack_grad_shards(big, small, dev):
    out = {}
    off = 0
    for name, shp, _ in BIG:
        n = _rows(shp)
        out[name] = big[off:off + n].reshape(shp)
        off += n
    flat = small.reshape(-1)
    off = 0
    for name, shp in SMALL:
        n = 1
        for s in shp:
            n *= s
        out[name] = flat[off:off + n].reshape(shp)
        off += n
    width = CONV_W // N_DEV
    out["conv_w"] = lax.dynamic_slice_in_dim(out["conv_w"], dev * width, width, axis=2)
    return out


def _pack_small(arrs):
    flat = jnp.concatenate([a.reshape(-1) for a in arrs])
    pad = (-flat.shape[0]) % (8 * 128)
    return jnp.pad(flat, (0, pad)).reshape(-1, 128)


def _unpack_small(packed, like):
    flat = packed.reshape(-1)
    out, off = [], 0
    for a in like:
        out.append(flat[off:off + a.size].reshape(a.shape))
        off += a.size
    return out


WEIGHTS = ("norm_mix", "w_in", "b_gate", "pool_mix", "pool_scale", "conv_w", "q_gain", "k_gain", "w_pool_up",
           "w_conv_out", "w_attn_up", "w_o", "norm_mlp", "w_ff1", "w_ff2")


def kernel(x, norm_mix, w_in, b_gate, pool_mix, pool_scale, conv_w, q_gain, k_gain, w_pool_up, w_conv_out, w_attn_up, w_o, norm_mlp, w_ff1, w_ff2, loss_target, m_norm_mix, m_w_in, m_b_gate, m_pool_mix, m_pool_scale, m_conv_w, m_q_gain, m_k_gain, m_w_pool_up, m_w_conv_out, m_w_attn_up, m_w_o, m_norm_mlp, m_w_ff1, m_w_ff2, v_norm_mix, v_w_in, v_b_gate, v_pool_mix, v_pool_scale, v_conv_w, v_q_gain, v_k_gain, v_w_pool_up, v_w_conv_out, v_w_attn_up, v_w_o, v_norm_mlp, v_w_ff1, v_w_ff2):
    w = dict(zip(WEIGHTS, (norm_mix, w_in, b_gate, pool_mix, pool_scale, conv_w, q_gain, k_gain, w_pool_up,
                           w_conv_out, w_attn_up, w_o, norm_mlp, w_ff1, w_ff2)))
    m = dict(zip(WEIGHTS, (m_norm_mix, m_w_in, m_b_gate, m_pool_mix, m_pool_scale, m_conv_w, m_q_gain, m_k_gain,
                           m_w_pool_up, m_w_conv_out, m_w_attn_up, m_w_o, m_norm_mlp, m_w_ff1, m_w_ff2)))
    v = dict(zip(WEIGHTS, (v_norm_mix, v_w_in, v_b_gate, v_pool_mix, v_pool_scale, v_conv_w, v_q_gain, v_k_gain,
                           v_w_pool_up, v_w_conv_out, v_w_attn_up, v_w_o, v_norm_mlp, v_w_ff1, v_w_ff2)))
    xi, yi, ci = _mesh_pos()
    dev = 4 * xi + 2 * yi + ci

    gathered = _all_gather("gather_weights", _pack_weight_shards(w))
    params = _unpack_gathered(gathered, w)
    loss, dx, grads = _local_step(x[0], loss_target[0], params)
    big, small = _pack_grads(grads)
    g_big, g_small = _reduce_scatter(big, small)
    g = _unpack_grad_shards(g_big, g_small, dev)

    delta, new_m, new_v = {}, {}, {}
    for name, shp, _ in BIG:
        two_d = (shp[0] * shp[1], shp[2])
        d_, m_, v_ = _adamw("adamw_" + name, w[name].reshape(two_d), g[name].reshape(two_d),
                            m[name].reshape(two_d), v[name].reshape(two_d))
        delta[name], new_m[name], new_v[name] = d_.reshape(shp), m_.reshape(shp), v_.reshape(shp)
    small_names = [name for name, _ in SMALL]
    packed = [_pack_small([src[name] for name in small_names]) for src in (w, g, m, v)]
    outs = _adamw("adamw_small", *packed)
    for dst, arr in zip((delta, new_m, new_v), outs):
        for name, val in zip(small_names, _unpack_small(arr, [w[name] for name in small_names])):
            dst[name] = val

    total = lax.psum(loss[0, 0], ("x", "y", "c"))
    return (total, dx[None], *[g[n] for n in WEIGHTS], *[delta[n] for n in WEIGHTS],
            *[new_m[n] for n in WEIGHTS], *[new_v[n] for n in WEIGHTS])
```

```python
import functools

import jax
import jax.numpy as jnp
from jax import lax
from jax.experimental import pallas as pl
from jax.experimental.pallas import tpu as pltpu
from jax.experimental.pallas import tpu_sc as plsc

F32 = jnp.float32
BF16 = jnp.bfloat16
MESH = pl.DeviceIdType.MESH

D = 1024
SEQ = 4096
DEPTH = 2
N_DEV = 8
POOL_WINDOWS = (2, 4, 8, 16)
POOL_W = 512
POOL_G = 128
CONV_W = 512
HEAD = 64
ATTN_GROUPS = ((128, 1), (512, 4), (2048, 16))
HPG = 4
ATTN_W = 768
ATTN_OUT = 256
ATTN_BLK = 128
D_FF = 4096
EPS = 1e-6
MASK = -1e30
OFF_POOL = 0
OFF_CB = 512
OFF_CC = 1024
OFF_CX = 1536
OFF_Q = 2048
OFF_K = 2816
OFF_V = 3584
OFF_GATE = 4352
IN_COLS = 7424
ADAM_LR = 0.001
ADAM_B1 = 0.9
ADAM_B2 = 0.999
ADAM_EPS = 1e-08
ADAM_WD = 0.01
ADAM_STEP = 10

ROW_TILE = 512
SEQ_CHUNK = 256
HALO = 16
VMEM_LIMIT = 56 * 1024 * 1024


def _cparams(sem=None):
    return pltpu.CompilerParams(dimension_semantics=sem, vmem_limit_bytes=VMEM_LIMIT)


_DIMS = {"nn": (((1,), (0,)), ((), ())), "nt": (((1,), (1,)), ((), ())), "tn": (((0,), (0,)), ((), ()))}


def _matmul(name, a, b, mode, tm, tn, tk, out_dtypes=(F32,), extras=(), epilogue=None, into=None):
    if mode == "tn":
        K, M = a.shape
    else:
        M, K = a.shape
    N = b.shape[0] if mode == "nt" else b.shape[1]
    assert M % tm == 0 and N % tn == 0 and K % tk == 0, (name, M, N, K, tm, tn, tk)
    nk = K // tk
    n_extra = len(extras)
    n_out = len(out_dtypes)
    dims = _DIMS[mode]
    n_alias = 0 if into is None or isinstance(into[0], jax.ShapeDtypeStruct) else 1

    def body(a_ref, b_ref, *rest):
        extra_refs = rest[:n_extra]
        out_refs = rest[n_extra + n_alias:n_extra + n_alias + n_out]

        def finish(acc):
            if epilogue is None:
                res = (acc,)
            else:
                res = epilogue(acc, *[r[...] for r in extra_refs])
            for o_ref, v in zip(out_refs, res):
                o_ref[...] = v.astype(o_ref.dtype)

        part = lax.dot_general(a_ref[...].astype(BF16), b_ref[...].astype(BF16), dims,
                               preferred_element_type=F32)
        if nk == 1:
            finish(part)
        else:
            acc_ref = rest[-1]
            k = pl.program_id(2)

            @pl.when(k == 0)
            def _():
                acc_ref[...] = part

            @pl.when(k > 0)
            def _():
                acc_ref[...] += part

            @pl.when(k == nk - 1)
            def _():
                finish(acc_ref[...])

    if mode == "tn":
        a_spec = pl.BlockSpec((tk, tm), lambda i, j, k: (k, i))
    else:
        a_spec = pl.BlockSpec((tm, tk), lambda i, j, k: (i, k))
    if mode == "nt":
        b_spec = pl.BlockSpec((tn, tk), lambda i, j, k: (j, k))
    else:
        b_spec = pl.BlockSpec((tk, tn), lambda i, j, k: (k, j))
    in_specs = [a_spec, b_spec]
    args = [a, b]
    for arr, kind in extras:
        if kind == "mn":
            in_specs.append(pl.BlockSpec((tm, tn), lambda i, j, k: (i, j)))
        else:
            in_specs.append(pl.BlockSpec((1, tn), lambda i, j, k: (0, j)))
        args.append(arr)
    out_shape = tuple(jax.ShapeDtypeStruct((M, N), dt) for dt in out_dtypes)
    out_specs = tuple(pl.BlockSpec((tm, tn), lambda i, j, k: (i, j)) for _ in out_dtypes)
    aliases = {}
    if into is not None:
        buf, row0, col0 = into
        assert n_out == 1 and (M // N_DEV) % tm == 0 and row0 % tm == 0 and col0 % tn == 0
        per_dev = M // N_DEV // tm
        out_shape = (jax.ShapeDtypeStruct(buf.shape, buf.dtype),)
        out_specs = (pl.BlockSpec((None, tm, tn), lambda i, j, k: (i // per_dev, row0 // tm + i % per_dev,
                                                                   col0 // tn + j)),)
        if not isinstance(buf, jax.ShapeDtypeStruct):
            aliases = {len(args): 0}
            in_specs.append(pl.BlockSpec(memory_space=pl.ANY))
            args.append(buf)
    scratch = [] if nk == 1 else [pltpu.VMEM((tm, tn), F32)]
    res = pl.pallas_call(
        body, name=name, grid=(M // tm, N // tn, nk), in_specs=in_specs, out_specs=out_specs,
        out_shape=out_shape, scratch_shapes=scratch, input_output_aliases=aliases,
        compiler_params=_cparams(("parallel", "parallel", "arbitrary")),
    )(*args)
    return res if n_out > 1 else res[0]


def _rmsnorm_fwd(name, x, gain):
    S_, D_ = x.shape

    def body(x_ref, g_ref, h_ref):
        xf = x_ref[...]
        ms = jnp.mean(xf * xf, axis=-1, keepdims=True)
        h_ref[...] = (xf * lax.rsqrt(ms + EPS) * g_ref[...]).astype(BF16)

    return pl.pallas_call(
        body, name=name, grid=(S_ // ROW_TILE,),
        in_specs=[pl.BlockSpec((ROW_TILE, D_), lambda i: (i, 0)), pl.BlockSpec((1, D_), lambda i: (0, 0))],
        out_specs=pl.BlockSpec((ROW_TILE, D_), lambda i: (i, 0)),
        out_shape=jax.ShapeDtypeStruct((S_, D_), BF16),
        compiler_params=_cparams(("parallel",)),
    )(x, gain)


def _rmsnorm_bwd(name, x, gain, dh, dres):
    S_, D_ = x.shape
    n = S_ // ROW_TILE

    def body(x_ref, g_ref, dh_ref, dres_ref, dx_ref, dg_ref, acc_ref):
        i = pl.program_id(0)
        xf = x_ref[...]
        rstd = lax.rsqrt(jnp.mean(xf * xf, axis=-1, keepdims=True) + EPS)
        xhat = xf * rstd
        dhv = dh_ref[...]
        dxhat = dhv * g_ref[...]
        c = jnp.mean(dxhat * xhat, axis=-1, keepdims=True)
        dx_ref[...] = dres_ref[...] + rstd * (dxhat - xhat * c)
        part = jnp.sum((dhv * xhat).reshape(ROW_TILE // 8, 8, D_), axis=0)

        @pl.when(i == 0)
        def _():
            acc_ref[...] = part

        @pl.when(i > 0)
        def _():
            acc_ref[...] += part

        @pl.when(i == n - 1)
        def _():
            dg_ref[...] = jnp.sum(acc_ref[...], axis=0, keepdims=True)

    row = pl.BlockSpec((ROW_TILE, D_), lambda i: (i, 0))
    vec = pl.BlockSpec((1, D_), lambda i: (0, 0))
    return pl.pallas_call(
        body, name=name, grid=(n,), in_specs=[row, vec, row, row], out_specs=(row, vec),
        out_shape=(jax.ShapeDtypeStruct((S_, D_), F32), jax.ShapeDtypeStruct((1, D_), F32)),
        scratch_shapes=[pltpu.VMEM((8, D_), F32)],
        compiler_params=_cparams(("arbitrary",)),
    )(x, gain, dh, dres)


def _loss_head(name, y, target):
    S_, D_ = y.shape
    n = S_ // ROW_TILE

    def body(y_ref, t_ref, dy_ref, l_ref, acc_ref):
        i = pl.program_id(0)
        e = y_ref[...] - t_ref[...]
        dy_ref[...] = e * (1.0 / D_)
        part = jnp.sum((e * e).reshape(ROW_TILE // 8, 8, D_), axis=0)

        @pl.when(i == 0)
        def _():
            acc_ref[...] = part

        @pl.when(i > 0)
        def _():
            acc_ref[...] += part

        @pl.when(i == n - 1)
        def _():
            s = jnp.sum(acc_ref[...], axis=1, keepdims=True)
            l_ref[...] = jnp.sum(s, axis=0, keepdims=True) * (0.5 / D_)

    row = pl.BlockSpec((ROW_TILE, D_), lambda i: (i, 0))
    return pl.pallas_call(
        body, name=name, grid=(n,), in_specs=[row, row],
        out_specs=(row, pl.BlockSpec((1, 1), lambda i: (0, 0))),
        out_shape=(jax.ShapeDtypeStruct((S_, D_), F32), jax.ShapeDtypeStruct((1, 1), F32)),
        scratch_shapes=[pltpu.VMEM((8, D_), F32)],
        compiler_params=_cparams(("arbitrary",)),
    )(y, target)


def _rows_with_halo(ref, cols, i, n_chunks, before, after):
    r0 = pl.multiple_of(i * SEQ_CHUNK, SEQ_CHUNK)
    parts = []
    if before:
        h0 = pl.multiple_of(jnp.maximum(r0 - HALO, 0), 8)
        halo = ref[pl.ds(h0, HALO), cols]
        parts.append(jnp.where(i > 0, halo, jnp.zeros_like(halo)))
    parts.append(ref[pl.ds(r0, SEQ_CHUNK), cols])
    if after:
        a0 = pl.multiple_of(jnp.minimum(r0 + SEQ_CHUNK, (n_chunks - 1) * SEQ_CHUNK + SEQ_CHUNK - HALO), 8)
        halo = ref[pl.ds(a0, HALO), cols]
        parts.append(jnp.where(i < n_chunks - 1, halo, jnp.zeros_like(halo)))
    return parts[0] if len(parts) == 1 else jnp.concatenate(parts, axis=0)


def _shift_down(v, k):
    return pltpu.roll(v, k, 0)


def _shift_up(v, k):
    return pltpu.roll(v, v.shape[0] - k, 0)


def _pool_diff(xx, w, t_main):
    s = xx
    k = 1
    while k < w:
        s = s + _shift_down(s, k)
        k *= 2
    cnt = jnp.minimum(t_main + 1, w).astype(F32)
    return s[HALO:] / cnt - xx[HALO:]


def _pool_fwd(name, z, pool_mix, pool_scale):
    S_ = z.shape[0]
    n_chunks = S_ // SEQ_CHUNK

    def body(u_ref, mix_ref, sc_ref, y_ref):
        for g, w in enumerate(POOL_WINDOWS):
            cols = slice(g * POOL_G, (g + 1) * POOL_G)
            mixg = mix_ref[g].astype(BF16)
            scg = sc_ref[:, cols]

            def chunk(i, carry, cols=cols, mixg=mixg, scg=scg, w=w):
                r0 = pl.multiple_of(i * SEQ_CHUNK, SEQ_CHUNK)
                xx = _rows_with_halo(u_ref, cols, i, n_chunks, True, False)
                t = r0 + lax.broadcasted_iota(jnp.int32, (SEQ_CHUNK, POOL_G), 0)
                d = _pool_diff(xx, w, t)
                y = jnp.dot(d.astype(BF16), mixg, preferred_element_type=F32) * scg
                y_ref[pl.ds(r0, SEQ_CHUNK), cols] = y.astype(BF16)
                return carry

            lax.fori_loop(0, n_chunks, chunk, 0)

    return pl.pallas_call(
        body, name=name, grid=(1,),
        in_specs=[pl.BlockSpec((S_, POOL_W), lambda i: (0, 0)),
                  pl.BlockSpec((4, POOL_G, POOL_G), lambda i: (0, 0, 0)),
                  pl.BlockSpec((1, POOL_W), lambda i: (0, 0))],
        out_specs=pl.BlockSpec((S_, POOL_W), lambda i: (0, 0)),
        out_shape=jax.ShapeDtypeStruct((S_, POOL_W), BF16),
        compiler_params=_cparams(("arbitrary",)),
    )(z, pool_mix, pool_scale)


def _pool_bwd(name, z, dy, pool_mix, pool_scale):
    S_ = z.shape[0]
    n_chunks = S_ // SEQ_CHUNK
    rows_a = SEQ_CHUNK + HALO

    def body(u_ref, dy_ref, mix_ref, sc_ref, du_ref, dmix_ref, dsc_ref):
        for g, w in enumerate(POOL_WINDOWS):
            cols = slice(g * POOL_G, (g + 1) * POOL_G)
            mixg = mix_ref[g].astype(BF16)
            scg = sc_ref[:, cols]

            def chunk(i, carry, cols=cols, mixg=mixg, scg=scg, w=w):
                dmix_acc, dsc_acc = carry
                r0 = pl.multiple_of(i * SEQ_CHUNK, SEQ_CHUNK)
                xx = _rows_with_halo(u_ref, cols, i, n_chunks, True, False)
                t = r0 + lax.broadcasted_iota(jnp.int32, (SEQ_CHUNK, POOL_G), 0)
                d = _pool_diff(xx, w, t).astype(BF16)
                ypre = jnp.dot(d, mixg, preferred_element_type=F32)
                dyy = _rows_with_halo(dy_ref, cols, i, n_chunks, False, True)
                dys = (dyy * scg).astype(BF16)
                dsc_acc = dsc_acc + jnp.sum((dyy[:SEQ_CHUNK] * ypre).reshape(SEQ_CHUNK // 8, 8, POOL_G), axis=0)
                dmix_acc = dmix_acc + lax.dot_general(d, dys[:SEQ_CHUNK], _DIMS["tn"], preferred_element_type=F32)
                dd = lax.dot_general(dys, mixg, _DIMS["nt"], preferred_element_type=F32)
                ta = r0 + lax.broadcasted_iota(jnp.int32, (rows_a, POOL_G), 0)
                f = dd / jnp.minimum(ta + 1, w).astype(F32)
                k = 1
                while k < w:
                    f = f + _shift_up(f, k)
                    k *= 2
                du_ref[pl.ds(r0, SEQ_CHUNK), cols] = (f[:SEQ_CHUNK] - dd[:SEQ_CHUNK]).astype(BF16)
                return dmix_acc, dsc_acc

            dmix_acc, dsc_acc = lax.fori_loop(
                0, n_chunks, chunk, (jnp.zeros((POOL_G, POOL_G), F32), jnp.zeros((8, POOL_G), F32)))
            dmix_ref[g] = dmix_acc
            dsc_ref[:, cols] = jnp.sum(dsc_acc, axis=0, keepdims=True)

    full = pl.BlockSpec((S_, POOL_W), lambda i: (0, 0))
    mix_spec = pl.BlockSpec((4, POOL_G, POOL_G), lambda i: (0, 0, 0))
    vec = pl.BlockSpec((1, POOL_W), lambda i: (0, 0))
    return pl.pallas_call(
        body, name=name, grid=(1,), in_specs=[full, full, mix_spec, vec], out_specs=(full, mix_spec, vec),
        out_shape=(jax.ShapeDtypeStruct((S_, POOL_W), BF16), jax.ShapeDtypeStruct((4, POOL_G, POOL_G), F32),
                   jax.ShapeDtypeStruct((1, POOL_W), F32)),
        compiler_params=_cparams(("arbitrary",)),
    )(z, dy, pool_mix, pool_scale)


def _conv_specs(S_):
    slab = lambda off: pl.BlockSpec((S_, 128), lambda c, off=off: (0, off // 128 + c))
    return slab(OFF_CB), slab(OFF_CC), slab(OFF_CX)


def _conv_fwd(name, z, conv_w):
    S_ = z.shape[0]
    n_chunks = S_ // SEQ_CHUNK
    col = slice(0, 128)

    def body(b_ref, c_ref, x_ref, w_ref, y_ref):
        w0, w1, w2 = w_ref[0:1, :], w_ref[1:2, :], w_ref[2:3, :]

        def chunk(i, carry):
            r0 = pl.multiple_of(i * SEQ_CHUNK, SEQ_CHUNK)
            u = _rows_with_halo(c_ref, col, i, n_chunks, True, False) * _rows_with_halo(x_ref, col, i, n_chunks, True, False)
            y = w2 * u + w1 * _shift_down(u, 1) + w0 * _shift_down(u, 2)
            y_ref[pl.ds(r0, SEQ_CHUNK), :] = (b_ref[pl.ds(r0, SEQ_CHUNK), :] * y[HALO:]).astype(BF16)
            return carry

        lax.fori_loop(0, n_chunks, chunk, 0)

    sb, sc, sx = _conv_specs(S_)
    return pl.pallas_call(
        body, name=name, grid=(CONV_W // 128,),
        in_specs=[sb, sc, sx, pl.BlockSpec((3, 128), lambda c: (0, c))],
        out_specs=pl.BlockSpec((S_, 128), lambda c: (0, c)),
        out_shape=jax.ShapeDtypeStruct((S_, CONV_W), BF16),
        compiler_params=_cparams(("parallel",)),
    )(z, z, z, conv_w)


def _conv_bwd(name, z, dy, conv_w):
    S_ = z.shape[0]
    n_chunks = S_ // SEQ_CHUNK
    col = slice(0, 128)
    lo, hi = HALO, HALO + SEQ_CHUNK

    def body(b_ref, c_ref, x_ref, dy_ref, w_ref, db_ref, dc_ref, dx_ref, dw_ref):
        w0, w1, w2 = w_ref[0:1, :], w_ref[1:2, :], w_ref[2:3, :]

        def chunk(i, carry):
            a0, a1, a2 = carry
            r0 = pl.multiple_of(i * SEQ_CHUNK, SEQ_CHUNK)
            cc = _rows_with_halo(c_ref, col, i, n_chunks, True, True)
            xx = _rows_with_halo(x_ref, col, i, n_chunks, True, True)
            bb = _rows_with_halo(b_ref, col, i, n_chunks, True, True)
            dyy = _rows_with_halo(dy_ref, col, i, n_chunks, True, True)
            u = cc * xx
            u1 = _shift_down(u, 1)
            u2 = _shift_down(u, 2)
            y = w2 * u + w1 * u1 + w0 * u2
            dyv = dyy * bb
            du = w2 * dyv + w1 * _shift_up(dyv, 1) + w0 * _shift_up(dyv, 2)
            db_ref[pl.ds(r0, SEQ_CHUNK), :] = (dyy[lo:hi] * y[lo:hi]).astype(BF16)
            dc_ref[pl.ds(r0, SEQ_CHUNK), :] = (du[lo:hi] * xx[lo:hi]).astype(BF16)
            dx_ref[pl.ds(r0, SEQ_CHUNK), :] = (du[lo:hi] * cc[lo:hi]).astype(BF16)
            red = lambda v: jnp.sum(v.reshape(SEQ_CHUNK // 8, 8, 128), axis=0)
            dm = dyv[lo:hi]
            return a0 + red(dm * u2[lo:hi]), a1 + red(dm * u1[lo:hi]), a2 + red(dm * u[lo:hi])

        zero = jnp.zeros((8, 128), F32)
        a0, a1, a2 = lax.fori_loop(0, n_chunks, chunk, (zero, zero, zero))
        dw_ref[0:1, :] = jnp.sum(a0, axis=0, keepdims=True)
        dw_ref[1:2, :] = jnp.sum(a1, axis=0, keepdims=True)
        dw_ref[2:3, :] = jnp.sum(a2, axis=0, keepdims=True)

    sb, sc, sx = _conv_specs(S_)
    slab = pl.BlockSpec((S_, 128), lambda c: (0, c))
    wspec = pl.BlockSpec((3, 128), lambda c: (0, c))
    act = jax.ShapeDtypeStruct((S_, CONV_W), BF16)
    return pl.pallas_call(
        body, name=name, grid=(CONV_W // 128,), in_specs=[sb, sc, sx, slab, wspec],
        out_specs=(slab, slab, slab, wspec),
        out_shape=(act, act, act, jax.ShapeDtypeStruct((3, CONV_W), F32)),
        compiler_params=_cparams(("parallel",)),
    )(z, z, z, dy, conv_w)


def _head_norm(v, gain):
    rstd = lax.rsqrt(jnp.mean(v * v, axis=-1, keepdims=True) + EPS)
    xhat = v * rstd
    return xhat * gain, xhat, rstd


def _head_norm_bwd(dy, xhat, rstd, gain):
    dxhat = dy * gain
    c = jnp.mean(dxhat * xhat, axis=-1, keepdims=True)
    dv = rstd * (dxhat - xhat * c)
    dg = jnp.sum((dy * xhat).reshape(ATTN_BLK // 8, 8, HEAD), axis=0)
    return dv, dg


def _band_masks():
    qi = lax.broadcasted_iota(jnp.int32, (ATTN_BLK, ATTN_BLK), 0)
    ki = lax.broadcasted_iota(jnp.int32, (ATTN_BLK, ATTN_BLK), 1)
    return ki <= qi, ki >= qi


def _attn_specs(S_, g, dil):
    rows = ATTN_BLK * dil
    nb = S_ // rows
    cq, ck, cv = (OFF_Q // PAIR_W + g * PAIRS, OFF_K // PAIR_W + g * PAIRS, OFF_V // PAIR_W + g * PAIRS)
    return rows, nb, cq, ck, cv


PAIR_W = 128
PAIR = PAIR_W // HEAD
PAIRS = HPG // PAIR


def _rows_of(ref, r, dil):
    if dil == 1:
        return ref[...]
    return ref[pl.ds(r, ATTN_BLK, stride=dil), :]


def _put_rows(ref, r, dil, val):
    if dil == 1:
        ref[...] = val.astype(ref.dtype)
    else:
        ref[pl.ds(r, ATTN_BLK, stride=dil), :] = val.astype(ref.dtype)


def _attn_fwd(name, z, q_gain, k_gain, g, dil):
    S_ = z.shape[0]
    rows, nb, cq, ck, cv = _attn_specs(S_, g, dil)
    scale = HEAD ** -0.5

    def body(q_ref, kc_ref, kp_ref, vc_ref, vp_ref, gq_ref, gk_ref, o_ref, l_ref):
        n = pl.program_id(1)
        m_cur, m_prev = _band_masks()
        m_prev = jnp.logical_and(m_prev, n > 0)
        gq, gk = gq_ref[...], gk_ref[...]
        for r in range(dil):
            q, kc, kp = _rows_of(q_ref, r, dil), _rows_of(kc_ref, r, dil), _rows_of(kp_ref, r, dil)
            vc, vp = _rows_of(vc_ref, r, dil), _rows_of(vp_ref, r, dil)
            outs, lses = [], []
            for h in range(PAIR):
                hs = slice(h * HEAD, (h + 1) * HEAD)
                qn = _head_norm(q[:, hs], gq)[0].astype(BF16)
                kcn = _head_norm(kc[:, hs], gk)[0].astype(BF16)
                kpn = _head_norm(kp[:, hs], gk)[0].astype(BF16)
                s_c = lax.dot_general(qn, kcn, _DIMS["nt"], preferred_element_type=F32) * scale
                s_p = lax.dot_general(qn, kpn, _DIMS["nt"], preferred_element_type=F32) * scale
                s_c = jnp.where(m_cur, s_c, MASK)
                s_p = jnp.where(m_prev, s_p, MASK)
                m = jnp.maximum(jnp.max(s_c, axis=-1, keepdims=True), jnp.max(s_p, axis=-1, keepdims=True))
                p_c = jnp.exp(s_c - m)
                p_p = jnp.exp(s_p - m)
                den = jnp.sum(p_c, axis=-1, keepdims=True) + jnp.sum(p_p, axis=-1, keepdims=True)
                acc = (jnp.dot(p_c.astype(BF16), vc[:, hs].astype(BF16), preferred_element_type=F32)
                       + jnp.dot(p_p.astype(BF16), vp[:, hs].astype(BF16), preferred_element_type=F32))
                outs.append(acc / den)
                lses.append(jnp.broadcast_to(m + jnp.log(den), (ATTN_BLK, HEAD)))
            _put_rows(o_ref, r, dil, jnp.concatenate(outs, axis=1))
            _put_rows(l_ref, r, dil, jnp.concatenate(lses, axis=1))

    cur = lambda c: pl.BlockSpec((rows, PAIR_W), lambda hp, n, c=c: (n, c + hp))
    prev = lambda c: pl.BlockSpec((rows, PAIR_W), lambda hp, n, c=c: (jnp.maximum(n - 1, 0), c + hp))
    gspec = pl.BlockSpec((1, HEAD), lambda hp, n: (0, 0))
    shp = jax.ShapeDtypeStruct((S_, ATTN_OUT), F32)
    return pl.pallas_call(
        body, name=name, grid=(PAIRS, nb),
        in_specs=[cur(cq), cur(ck), prev(ck), cur(cv), prev(cv), gspec, gspec],
        out_specs=(cur(0), cur(0)), out_shape=(shp, shp),
        compiler_params=_cparams(("parallel", "parallel")),
    )(z, z, z, z, z, q_gain, k_gain)


def _attn_combine(name, os_, ls_):
    S_ = os_[0].shape[0]

    def body(o0, o1, o2, l0, l1, l2, o_ref, l_ref):
        a, b, c = l0[...], l1[...], l2[...]
        m = jnp.maximum(jnp.maximum(a, b), c)
        ea, eb, ec = jnp.exp(a - m), jnp.exp(b - m), jnp.exp(c - m)
        zsum = ea + eb + ec
        o_ref[...] = (ea * o0[...] + eb * o1[...] + ec * o2[...]) / zsum
        l_ref[...] = m + jnp.log(zsum)

    row = pl.BlockSpec((ROW_TILE, ATTN_OUT), lambda i: (i, 0))
    shp = jax.ShapeDtypeStruct((S_, ATTN_OUT), F32)
    return pl.pallas_call(
        body, name=name, grid=(S_ // ROW_TILE,), in_specs=[row] * 6, out_specs=(row, row), out_shape=(shp, shp),
        compiler_params=_cparams(("parallel",)),
    )(*os_, *ls_)


def _attn_bwd(name, z, q_gain, k_gain, do, o, lse, g, dil):
    S_ = z.shape[0]
    rows, nb, cq, ck, cv = _attn_specs(S_, g, dil)
    scale = HEAD ** -0.5

    def body(q_ref, kc_ref, kp_ref, vc_ref, vp_ref, gq_ref, gk_ref, do_ref, o_ref, l_ref,
             dq_ref, dk_ref, dv_ref, dgq_ref, dgk_ref, ck_ref, cvv_ref, gq_acc, gk_acc):
        hp = pl.program_id(0)
        n = pl.program_id(1)
        live = n < nb
        m_cur, m_prev = _band_masks()
        m_cur = jnp.logical_and(m_cur, live)
        m_prev = jnp.logical_and(m_prev, jnp.logical_and(n > 0, live))
        gq, gk = gq_ref[...], gk_ref[...]

        @pl.when(n == 0)
        def _():
            ck_ref[...] = jnp.zeros_like(ck_ref)
            cvv_ref[...] = jnp.zeros_like(cvv_ref)

        @pl.when(jnp.logical_and(n == 0, hp == 0))
        def _():
            gq_acc[...] = jnp.zeros_like(gq_acc)
            gk_acc[...] = jnp.zeros_like(gk_acc)

        for r in range(dil):
            q, kc, kp = _rows_of(q_ref, r, dil), _rows_of(kc_ref, r, dil), _rows_of(kp_ref, r, dil)
            vc, vp = _rows_of(vc_ref, r, dil), _rows_of(vp_ref, r, dil)
            dov, ov, lv = _rows_of(do_ref, r, dil), _rows_of(o_ref, r, dil), _rows_of(l_ref, r, dil)
            carry_k, carry_v = _rows_of(ck_ref, r, dil), _rows_of(cvv_ref, r, dil)
            dqs, dks, dvs, nck, ncv = [], [], [], [], []
            dgq = jnp.zeros((8, HEAD), F32)
            dgk = jnp.zeros((8, HEAD), F32)
            for h in range(PAIR):
                hs = slice(h * HEAD, (h + 1) * HEAD)
                qn, q_hat, q_rstd = _head_norm(q[:, hs], gq)
                kcn = _head_norm(kc[:, hs], gk)[0].astype(BF16)
                kpn, kp_hat, kp_rstd = _head_norm(kp[:, hs], gk)
                qn, kpn = qn.astype(BF16), kpn.astype(BF16)
                do_h = dov[:, hs]
                do_b = do_h.astype(BF16)
                delta = jnp.sum(do_h * ov[:, hs], axis=-1, keepdims=True)
                l_h = lv[:, h * HEAD:h * HEAD + 1]
                s_c = lax.dot_general(qn, kcn, _DIMS["nt"], preferred_element_type=F32) * scale
                s_p = lax.dot_general(qn, kpn, _DIMS["nt"], preferred_element_type=F32) * scale
                p_c = jnp.where(m_cur, jnp.exp(jnp.where(m_cur, s_c, MASK) - l_h), 0.0)
                p_p = jnp.where(m_prev, jnp.exp(jnp.where(m_prev, s_p, MASK) - l_h), 0.0)
                dp_c = lax.dot_general(do_b, vc[:, hs].astype(BF16), _DIMS["nt"], preferred_element_type=F32)
                dp_p = lax.dot_general(do_b, vp[:, hs].astype(BF16), _DIMS["nt"], preferred_element_type=F32)
                ds_c = (p_c * (dp_c - delta) * scale).astype(BF16)
                ds_p = (p_p * (dp_p - delta) * scale).astype(BF16)
                dqn = (jnp.dot(ds_c, kcn, preferred_element_type=F32)
                       + jnp.dot(ds_p, kpn, preferred_element_type=F32))
                dq_h, dg = _head_norm_bwd(dqn, q_hat, q_rstd, gq)
                dgq = dgq + dg
                dqs.append(dq_h)
                dkn_prev = carry_k[:, hs] + lax.dot_general(ds_p, qn, _DIMS["tn"], preferred_element_type=F32)
                dk_h, dg = _head_norm_bwd(dkn_prev, kp_hat, kp_rstd, gk)
                dgk = dgk + dg
                dks.append(dk_h)
                dvs.append(carry_v[:, hs] + lax.dot_general(p_p.astype(BF16), do_b, _DIMS["tn"],
                                                            preferred_element_type=F32))
                nck.append(lax.dot_general(ds_c, qn, _DIMS["tn"], preferred_element_type=F32))
                ncv.append(lax.dot_general(p_c.astype(BF16), do_b, _DIMS["tn"], preferred_element_type=F32))
            dq_all = jnp.concatenate(dqs, axis=1)

            @pl.when(live)
            def _(dq_all=dq_all, r=r):
                _put_rows(dq_ref, r, dil, dq_all)

            _put_rows(dk_ref, r, dil, jnp.concatenate(dks, axis=1))
            _put_rows(dv_ref, r, dil, jnp.concatenate(dvs, axis=1))
            _put_rows(ck_ref, r, dil, jnp.concatenate(nck, axis=1))
            _put_rows(cvv_ref, r, dil, jnp.concatenate(ncv, axis=1))
            gq_acc[...] += dgq
            gk_acc[...] += dgk

        @pl.when(jnp.logical_and(n == nb, hp == PAIRS - 1))
        def _():
            dgq_ref[...] = jnp.sum(gq_acc[...], axis=0, keepdims=True)
            dgk_ref[...] = jnp.sum(gk_acc[...], axis=0, keepdims=True)

    last = nb - 1
    cur = lambda c: pl.BlockSpec((rows, PAIR_W), lambda hp, n, c=c: (jnp.minimum(n, last), c + hp))
    prev = lambda c: pl.BlockSpec((rows, PAIR_W), lambda hp, n, c=c: (jnp.maximum(n - 1, 0), c + hp))
    gspec = pl.BlockSpec((1, HEAD), lambda hp, n: (0, 0))
    act = jax.ShapeDtypeStruct((S_, ATTN_OUT), F32)
    vec = jax.ShapeDtypeStruct((1, HEAD), F32)
    return pl.pallas_call(
        body, name=name, grid=(PAIRS, nb + 1),
        in_specs=[cur(cq), cur(ck), prev(ck), cur(cv), prev(cv), gspec, gspec, cur(0), cur(0), cur(0)],
        out_specs=(cur(0), prev(0), prev(0), gspec, gspec),
        out_shape=(act, act, act, vec, vec),
        scratch_shapes=[pltpu.VMEM((rows, PAIR_W), F32), pltpu.VMEM((rows, PAIR_W), F32),
                        pltpu.VMEM((8, HEAD), F32), pltpu.VMEM((8, HEAD), F32)],
        compiler_params=_cparams(("arbitrary", "arbitrary")),
    )(z, z, z, z, z, q_gain, k_gain, do, o, lse)


MIX_TN = 256


def _sigmoid(v):
    return 1.0 / (1.0 + jnp.exp(-v))


def _mix_fwd(name, z, b_gate, ys, ws):
    S_ = z.shape[0]
    tm, tn = ROW_TILE, MIX_TN
    gblk = OFF_GATE // tn

    def body(yp, yc, ya, wp, wc, wa, g0, g1, g2, b0, b1, b2, m_ref):
        acc = None
        for y_ref, w_ref, g_ref, b_ref in ((yp, wp, g0, b0), (yc, wc, g1, b1), (ya, wa, g2, b2)):
            u = lax.dot_general(y_ref[...].astype(BF16), w_ref[...], _DIMS["nt"], preferred_element_type=F32)
            t = _sigmoid(g_ref[...] + b_ref[...]) * u
            acc = t if acc is None else acc + t
        m_ref[...] = acc.astype(BF16)

    yspec = lambda w: pl.BlockSpec((tm, w), lambda i, j: (i, 0))
    wspec = lambda w: pl.BlockSpec((tn, w), lambda i, j: (j, 0))
    gspec = lambda b: pl.BlockSpec((tm, tn), lambda i, j, b=b: (i, gblk + b * (D // tn) + j))
    bspec = lambda b: pl.BlockSpec((1, tn), lambda i, j, b=b: (0, b * (D // tn) + j))
    return pl.pallas_call(
        body, name=name, grid=(S_ // tm, D // tn),
        in_specs=[yspec(POOL_W), yspec(CONV_W), yspec(ATTN_OUT), wspec(POOL_W), wspec(CONV_W), wspec(ATTN_OUT),
                  gspec(0), gspec(1), gspec(2), bspec(0), bspec(1), bspec(2)],
        out_specs=pl.BlockSpec((tm, tn), lambda i, j: (i, j)),
        out_shape=jax.ShapeDtypeStruct((S_, D), BF16),
        compiler_params=_cparams(("parallel", "parallel")),
    )(*ys, *ws, z, z, z, b_gate, b_gate, b_gate)


def _mix_bwd(name, z, b_gate, y, w, dmerged, branch):
    S_ = z.shape[0]
    tm, tn = ROW_TILE, MIX_TN
    width = y.shape[1]
    gblk = OFF_GATE // tn + branch * (D // tn)
    ni = S_ // tm

    def body(y_ref, w_ref, g_ref, b_ref, dm_ref, du_ref, dg_ref, db_ref, acc_ref):
        i = pl.program_id(1)
        u = lax.dot_general(y_ref[...].astype(BF16), w_ref[...], _DIMS["nt"], preferred_element_type=F32)
        sg = _sigmoid(g_ref[...] + b_ref[...])
        dm = dm_ref[...]
        du_ref[...] = (sg * dm).astype(BF16)
        dpre = dm * u * sg * (1.0 - sg)
        dg_ref[...] = dpre.astype(BF16)
        part = jnp.sum(dpre.reshape(tm // 8, 8, tn), axis=0)

        @pl.when(i == 0)
        def _():
            acc_ref[...] = part

        @pl.when(i > 0)
        def _():
            acc_ref[...] += part

        @pl.when(i == ni - 1)
        def _():
            db_ref[...] = jnp.sum(acc_ref[...], axis=0, keepdims=True)

    blk = pl.BlockSpec((tm, tn), lambda j, i: (i, j))
    vec = pl.BlockSpec((1, tn), lambda j, i: (0, j))
    act = jax.ShapeDtypeStruct((S_, D), BF16)
    return pl.pallas_call(
        body, name=name, grid=(D // tn, ni),
        in_specs=[pl.BlockSpec((tm, width), lambda j, i: (i, 0)), pl.BlockSpec((tn, width), lambda j, i: (j, 0)),
                  pl.BlockSpec((tm, tn), lambda j, i: (i, gblk + j)),
                  pl.BlockSpec((1, tn), lambda j, i: (0, branch * (D // tn) + j)), blk],
        out_specs=(blk, blk, vec), out_shape=(act, act, jax.ShapeDtypeStruct((1, D), F32)),
        scratch_shapes=[pltpu.VMEM((8, tn), F32)],
        compiler_params=_cparams(("parallel", "arbitrary")),
    )(y, w, z, b_gate, dmerged)


def _relu2_epilogue(acc):
    r = jnp.maximum(acc, 0.0)
    return acc, r * r


def _layer_fwd(l, x, p):
    t = f"l{l}_"
    h = _rmsnorm_fwd(t + "norm_mix", x, p["norm_mix"])
    z = _matmul(t + "in_proj", h, p["w_in_t"], "nt", 512, 3712, 1024)
    y_pool = _pool_fwd(t + "pool", z, p["pool_mix"], p["pool_scale"])
    y_conv = _conv_fwd(t + "conv", z, p["conv_w"])
    os_, ls_ = [], []
    for g, (_, dil) in enumerate(ATTN_GROUPS):
        o_g, l_g = _attn_fwd(t + f"attn{g}", z, p["q_gain"], p["k_gain"], g, dil)
        os_.append(o_g)
        ls_.append(l_g)
    y_attn, lse = _attn_combine(t + "attn_mix", os_, ls_)
    merged = _mix_fwd(t + "merge", z, p["b_gate"], (y_pool, y_conv, y_attn),
                      (p["w_pool_up_t"], p["w_conv_out_t"], p["w_attn_up_t"]))
    x1 = _matmul(t + "out_proj", merged, p["w_o"], "nn", 512, 1024, 1024,
                 extras=((x, "mn"),), epilogue=lambda acc, xr: (xr + acc,))
    h2 = _rmsnorm_fwd(t + "norm_mlp", x1, p["norm_mlp"])
    a, r = _matmul(t + "ff1", h2, p["w_ff1_t"], "nt", 512, 1024, 1024, out_dtypes=(F32, BF16),
                   epilogue=_relu2_epilogue)
    x2 = _matmul(t + "ff2", r, p["w_ff2"], "nn", 512, 1024, 1024,
                 extras=((x1, "mn"),), epilogue=lambda acc, xr: (xr + acc,))
    saved = dict(x=x, h=h, z=z, y_pool=y_pool, y_conv=y_conv, y_attn=y_attn, lse=lse, merged=merged,
                 x1=x1, h2=h2, a=a, r=r)
    return x2, saved


def _layer_bwd(l, dx2, p, s):
    t = f"l{l}_b_"
    g = {}
    rest = jax.ShapeDtypeStruct((N_DEV, REST_ROWS, D), F32)
    da = _matmul(t + "d_ff2_in", dx2, p["w_ff2"], "nt", 512, 1024, 1024, out_dtypes=(BF16,),
                 extras=((s["a"], "mn"),), epilogue=lambda acc, a: (acc * (2.0 * jnp.maximum(a, 0.0)),))
    rest = _matmul(t + "dw_ff2", s["r"], dx2, "tn", 512, 1024, 512, into=(rest, REST_FF2, 0))
    dh2 = _matmul(t + "d_ff1_in", da, p["w_ff1_t"], "nn", 512, 1024, 1024)
    rest = _matmul(t + "dw_ff1", da, s["h2"], "tn", 512, 1024, 512, into=(rest, REST_FF1, 0))
    dx1, g["norm_mlp"] = _rmsnorm_bwd(t + "norm_mlp", s["x1"], p["norm_mlp"], dh2, dx2)
    dmerged = _matmul(t + "d_out_proj_in", dx1, p["w_o"], "nt", 512, 1024, 1024)
    rest = _matmul(t + "dw_o", s["merged"], dx1, "tn", 128, 1024, 2048, into=(rest, REST_WO, 0))
    ys = (s["y_pool"], s["y_conv"], s["y_attn"])
    names = ("w_pool_up_t", "w_conv_out_t", "w_attn_up_t")
    dys, dgates, dbs = [], [], []
    for b in range(3):
        du, dgz, db = _mix_bwd(t + f"merge{b}", s["z"], p["b_gate"], ys[b], p[names[b]], dmerged, b)
        width = ys[b].shape[1]
        dys.append(_matmul(t + f"d_up{b}_in", du, p[names[b]], "nn", 512, width, 1024))
        if b < 2:
            rest = _matmul(t + f"dw_up{b}", du, ys[b], "tn", 128, width, 2048, into=(rest, REST_UP, b * width))
        else:
            d_attn_up_t = _matmul(t + f"dw_up{b}", du, ys[b], "tn", 128, width, 2048)
            rest = rest.at[:, REST_ATTN:REST_ROWS, :].set(d_attn_up_t.reshape(N_DEV, REST_ROWS - REST_ATTN, D))
        dgates.append(dgz)
        dbs.append(db)
    g["rest"] = rest
    g["b_gate"] = jnp.concatenate(dbs, axis=1)
    du_pool, g["pool_mix"], g["pool_scale"] = _pool_bwd(t + "pool", s["z"], dys[0], p["pool_mix"], p["pool_scale"])
    dcb, dcc, dcx, g["conv_w"] = _conv_bwd(t + "conv", s["z"], dys[1], p["conv_w"])
    dqs, dks, dvs = [], [], []
    gq = gk = None
    for gi, (_, dil) in enumerate(ATTN_GROUPS):
        dq, dk, dv, dgq, dgk = _attn_bwd(t + f"attn{gi}", s["z"], p["q_gain"], p["k_gain"], dys[2], s["y_attn"],
                                         s["lse"], gi, dil)
        dqs.append(dq)
        dks.append(dk)
        dvs.append(dv)
        gq = dgq if gq is None else gq + dgq
        gk = dgk if gk is None else gk + dgk
    g["q_gain"], g["k_gain"] = gq, gk
    dz = jnp.concatenate([du_pool, dcb, dcc, dcx] + [v.astype(BF16) for v in dqs + dks + dvs] + dgates, axis=1)
    dh = _matmul(t + "d_in_proj_in", dz, p["w_in_t"], "nn", 512, 1024, 3712)
    g["in_t"] = _matmul(t + "dw_in", dz, s["h"], "tn", 256, 1024, 4096)
    dx, g["norm_mix"] = _rmsnorm_bwd(t + "norm_mix", s["x"], p["norm_mix"], dh, dx1)
    return dx, g


def _local_step(x, target, params):
    saved = []
    act = x
    for l in range(DEPTH):
        act, s = _layer_fwd(l, act, params[l])
        saved.append(s)
    dy, loss = _loss_head("loss_head", act, target)
    grads = [None] * DEPTH
    for l in reversed(range(DEPTH)):
        dy, grads[l] = _layer_bwd(l, dy, params[l], saved[l])
    return loss, dy, grads


ANY = pl.BlockSpec(memory_space=pl.ANY)


def _mesh_pos():
    return lax.axis_index("x"), lax.axis_index("y"), lax.axis_index("c")


def _other_chips(x, y):
    return [(1 - x, y), (x, 1 - y), (1 - x, 1 - y)]


def _all_gather(name, shard):
    R, C = shard.shape

    def body(x_ref, out_ref, send_sems, recv_sems, local_sem):
        x, y, c = _mesh_pos()
        me, sibling = (x, y, c), (x, y, 1 - c)
        chips = _other_chips(x, y)

        def slot(px, py, pc):
            return out_ref.at[4 * px + 2 * py + pc]

        def copy(k, block, to, src=None):
            return pltpu.make_async_remote_copy(
                src_ref=slot(*block) if src is None else src, dst_ref=slot(*block),
                send_sem=send_sems.at[k], recv_sem=recv_sems.at[k], device_id=to, device_id_type=MESH)

        mine = pltpu.make_async_copy(x_ref, slot(*me), local_sem)
        mine.start()
        first = [copy(0, me, sibling, src=x_ref)]
        first += [copy(1 + j, me, (*chip, c), src=x_ref) for j, chip in enumerate(chips)]
        for cp in first:
            cp.start()
        passed = [copy(4 + j, (*chip, c), sibling) for j, chip in enumerate(chips)]
        for j, chip in enumerate(chips):
            copy(1 + j, (*chip, c), me).wait_recv()
            passed[j].start()
        copy(0, sibling, me).wait_recv()
        for j, chip in enumerate(chips):
            copy(4 + j, (*chip, 1 - c), me).wait_recv()
        for cp in first + passed:
            cp.wait_send()
        mine.wait()

    return pl.pallas_call(
        body, name=name, out_shape=jax.ShapeDtypeStruct((N_DEV, R, C), shard.dtype),
        in_specs=[ANY], out_specs=ANY,
        scratch_shapes=[pltpu.SemaphoreType.DMA((7,)), pltpu.SemaphoreType.DMA((7,)), pltpu.SemaphoreType.DMA],
    )(shard)


def _all_gather_sc(name, shard, collective_id):
    R, C = shard.shape
    x_ref = jax.new_ref(shard, memory_space=pltpu.MemorySpace.HBM)
    out_ref = jax.empty_ref(jax.ShapeDtypeStruct((N_DEV, R, C), shard.dtype), memory_space=pltpu.MemorySpace.HBM)

    @pl.kernel(mesh=plsc.ScalarSubcoreMesh(axis_name="seq", num_cores=1), name=name,
               scratch_types=(pltpu.SemaphoreType.DMA((7,)), pltpu.SemaphoreType.DMA((7,)), pltpu.SemaphoreType.DMA),
               compiler_params=pltpu.CompilerParams(collective_id=collective_id))
    def launch(send_sems, recv_sems, local_sem):
        x, y, c = _mesh_pos()
        me, sibling = (x, y, c), (x, y, 1 - c)
        chips = _other_chips(x, y)
        barrier = pltpu.get_barrier_semaphore()
        peers = [sibling] + [(*chip, c) for chip in chips]
        for peer in peers:
            pl.semaphore_signal(barrier, inc=1, device_id=peer, device_id_type=MESH)
        pl.semaphore_wait(barrier, len(peers))

        def slot(px, py, pc):
            return out_ref.at[4 * px + 2 * py + pc]

        def copy(k, block, to, src=None):
            return pltpu.make_async_remote_copy(
                src_ref=slot(*block) if src is None else src, dst_ref=slot(*block),
                send_sem=send_sems.at[k], recv_sem=recv_sems.at[k], device_id=to, device_id_type=MESH)

        mine = pltpu.make_async_copy(x_ref, slot(*me), local_sem)
        mine.start()
        first = [copy(0, me, sibling, src=x_ref)]
        first += [copy(1 + j, me, (*chip, c), src=x_ref) for j, chip in enumerate(chips)]
        for cp in first:
            cp.start()
        passed = [copy(4 + j, (*chip, c), sibling) for j, chip in enumerate(chips)]
        for j, chip in enumerate(chips):
            copy(1 + j, (*chip, c), me).wait_recv()
            passed[j].start()
        copy(0, sibling, me).wait_recv()
        for j, chip in enumerate(chips):
            copy(4 + j, (*chip, 1 - c), me).wait_recv()
        for cp in first + passed:
            cp.wait_send()
        mine.wait()

    launch()
    return out_ref[...]


def _rs_sibling_exchange(name, arrs):
    n = len(arrs)

    def body(*refs):
        in_refs, out_refs, send_sems, recv_sems = refs[:n], refs[n:2 * n], refs[2 * n], refs[2 * n + 1]
        x, y, c = _mesh_pos()
        cps = []
        for k, (src, dst) in enumerate(zip(in_refs, out_refs)):
            src = src.at[:, 1 - c] if len(src.shape) == 4 else src
            cps.append(pltpu.make_async_remote_copy(src_ref=src, dst_ref=dst, send_sem=send_sems.at[k],
                                                    recv_sem=recv_sems.at[k], device_id=(x, y, 1 - c),
                                                    device_id_type=MESH))
        for cp in cps:
            cp.start()
        for cp in cps:
            cp.wait()

    out_shape = tuple(jax.ShapeDtypeStruct(a.shape[:1] + a.shape[2:] if a.ndim == 4 else a.shape, a.dtype)
                      for a in arrs)
    return pl.pallas_call(
        body, name=name, out_shape=out_shape, in_specs=[ANY] * n, out_specs=(ANY,) * n,
        scratch_shapes=[pltpu.SemaphoreType.DMA((n,)), pltpu.SemaphoreType.DMA((n,))],
    )(*arrs)


def _rs_rows(r):
    return r // 2 if (r // 2) % 16 == 0 else r


def _rs_chip_sum(name, ids, big, rbig):
    _, _, R, C = big.shape
    rows = _rs_rows(R)

    def body(ids_ref, mine_ref, theirs_ref, t16_ref, own_ref):
        p = pl.program_id(1)
        s = mine_ref[...] + theirs_ref[...]
        t16_ref[...] = s.astype(BF16)

        @pl.when(p == ids_ref[1])
        def _():
            own_ref[...] = s

    grid_spec = pltpu.PrefetchScalarGridSpec(
        num_scalar_prefetch=1, grid=(R // rows, 4),
        in_specs=[pl.BlockSpec((None, None, rows, C), lambda i, p, ids: (p, ids[0], i, 0)),
                  pl.BlockSpec((None, rows, C), lambda i, p, ids: (p, i, 0))],
        out_specs=(pl.BlockSpec((None, rows, C), lambda i, p, ids: (p, i, 0)),
                   pl.BlockSpec((rows, C), lambda i, p, ids: (i, 0))))
    return pl.pallas_call(
        body, name=name, grid_spec=grid_spec,
        out_shape=(jax.ShapeDtypeStruct((4, R, C), BF16), jax.ShapeDtypeStruct((R, C), F32)),
        compiler_params=_cparams(("parallel", "arbitrary")),
    )(ids, big, rbig)


def _add2(name, a, b):
    def body(a_ref, b_ref, o_ref):
        o_ref[...] = a_ref[...] + b_ref[...]

    full = pl.BlockSpec(a.shape, lambda i: (0, 0))
    return pl.pallas_call(body, name=name, grid=(1,), in_specs=[full, full], out_specs=full,
                          out_shape=jax.ShapeDtypeStruct(a.shape, a.dtype))(a, b)


def _rs_chip_exchange(name, arrs):
    n = len(arrs)

    def body(*refs):
        in_refs, out_refs = refs[:n], refs[n:2 * n]
        send_sems, recv_sems, local_sems = refs[2 * n:]
        x, y, c = _mesh_pos()
        p_me = 2 * x + y
        chips = _other_chips(x, y)

        def part(ref, p):
            return ref.at[p] if len(ref.shape) == 3 else ref

        local = [pltpu.make_async_copy(part(src, p_me), dst.at[p_me], local_sems.at[k])
                 for k, (src, dst) in enumerate(zip(in_refs, out_refs))]
        for cp in local:
            cp.start()
        sends = []
        for j, (px, py) in enumerate(chips):
            for k, (src, dst) in enumerate(zip(in_refs, out_refs)):
                sends.append(pltpu.make_async_remote_copy(
                    src_ref=part(src, 2 * px + py), dst_ref=dst.at[p_me], send_sem=send_sems.at[n * j + k],
                    recv_sem=recv_sems.at[n * j + k], device_id=(px, py, c), device_id_type=MESH))
        for cp in sends:
            cp.start()
        for j, (px, py) in enumerate(chips):
            for k, (src, dst) in enumerate(zip(in_refs, out_refs)):
                pltpu.make_async_remote_copy(
                    src_ref=part(src, p_me), dst_ref=dst.at[2 * px + py], send_sem=send_sems.at[n * j + k],
                    recv_sem=recv_sems.at[n * j + k], device_id=(px, py, c), device_id_type=MESH).wait_recv()
        for cp in sends:
            cp.wait_send()
        for cp in local:
            cp.wait()

    out_shape = tuple(jax.ShapeDtypeStruct((4,) + a.shape[-2:], a.dtype) for a in arrs)
    return pl.pallas_call(
        body, name=name, out_shape=out_shape, in_specs=[ANY] * n, out_specs=(ANY,) * n,
        scratch_shapes=[pltpu.SemaphoreType.DMA((3 * n,)), pltpu.SemaphoreType.DMA((3 * n,)),
                        pltpu.SemaphoreType.DMA((n,))],
    )(*arrs)


def _rs_final_sum(name, ids, recv, own, rows):
    _, R, C = recv.shape
    assert R % rows == 0

    def body(ids_ref, r_ref, own_ref, o_ref):
        acc = None
        for p in range(4):
            term = jnp.where(ids_ref[1] == p, own_ref[...], r_ref[p].astype(F32))
            acc = term if acc is None else acc + term
        o_ref[...] = acc

    grid_spec = pltpu.PrefetchScalarGridSpec(
        num_scalar_prefetch=1, grid=(R // rows,),
        in_specs=[pl.BlockSpec((4, rows, C), lambda i, ids: (0, i, 0)), pl.BlockSpec((rows, C), lambda i, ids: (i, 0))],
        out_specs=pl.BlockSpec((rows, C), lambda i, ids: (i, 0)))
    return pl.pallas_call(
        body, name=name, grid_spec=grid_spec, out_shape=jax.ShapeDtypeStruct((R, C), F32),
        compiler_params=_cparams(("parallel",)),
    )(ids, recv, own)


def _reduce_scatter(tag, bigs, small=None):
    x, y, c = _mesh_pos()
    ids = jnp.stack([c, 2 * x + y]).astype(jnp.int32)
    bigs = [b.reshape((4, 2) + b.shape[1:]) for b in bigs]
    smalls = [] if small is None else [small]
    got = _rs_sibling_exchange(tag + "_sibling", bigs + smalls)
    t16s, owns = [], []
    for k, (b, rb) in enumerate(zip(bigs, got)):
        t16, own = _rs_chip_sum(tag + f"_chip_sum{k}", ids, b, rb)
        t16s.append(t16)
        owns.append(own)
    tss = [_add2(tag + "_chip_sum_small", small, got[-1])] if smalls else []
    recv = _rs_chip_exchange(tag + "_chips", t16s + tss)
    outs = [_rs_final_sum(tag + f"_final{k}", ids, r, own, _rs_rows(own.shape[0]))
            for k, (r, own) in enumerate(zip(recv, owns))]
    out_small = _rs_final_sum(tag + "_final_small", ids, recv[-1], tss[0], small.shape[0]) if smalls else None
    return outs, out_small


def _adamw(name, w, g, m, v):
    R, C = w.shape
    tr = max(t for t in range(8, 513, 8) if R % t == 0)
    c1 = 1.0 - ADAM_B1 ** ADAM_STEP
    c2 = 1.0 - ADAM_B2 ** ADAM_STEP

    def body(w_ref, g_ref, m_ref, v_ref, d_ref, nm_ref, nv_ref):
        gv = g_ref[...]
        nm = ADAM_B1 * m_ref[...] + (1.0 - ADAM_B1) * gv
        nv = ADAM_B2 * v_ref[...] + (1.0 - ADAM_B2) * (gv * gv)
        d_ref[...] = -ADAM_LR * ((nm / c1) / (jnp.sqrt(nv / c2) + ADAM_EPS) + ADAM_WD * w_ref[...])
        nm_ref[...] = nm
        nv_ref[...] = nv

    blk = pl.BlockSpec((tr, C), lambda i: (i, 0))
    shp = jax.ShapeDtypeStruct((R, C), F32)
    return pl.pallas_call(
        body, name=name, grid=(R // tr,), in_specs=[blk] * 4, out_specs=(blk,) * 3, out_shape=(shp,) * 3,
        compiler_params=_cparams(("parallel",)),
    )(w, g, m, v)


REST_FF1 = 0
REST_FF2 = 512
REST_WO = 1024
REST_UP = 1152
REST_ATTN = 1280
REST_ROWS = 1312
IN_ROWS = IN_COLS // N_DEV
CONV_BITS_ROWS = 16
SHARD_ROWS = IN_ROWS + REST_ROWS + CONV_BITS_ROWS
SMALL = (("norm_mix", (DEPTH, D)), ("b_gate", (DEPTH, 3 * D)), ("pool_mix", (DEPTH, 4, POOL_G, POOL_G)),
         ("pool_scale", (DEPTH, POOL_W)), ("conv_w", (DEPTH, 3, CONV_W)), ("q_gain", (DEPTH, HEAD)),
         ("k_gain", (DEPTH, HEAD)), ("norm_mlp", (DEPTH, D)))


def _pack_weight_shards(w, l):
    b = lambda a: a.astype(BF16)
    conv = lax.bitcast_convert_type(w["conv_w"][l], BF16).reshape(3, 128)
    conv = jnp.pad(conv, ((0, CONV_BITS_ROWS - 3), (0, D - 128)))
    return jnp.concatenate([
        b(w["w_in"][l].T), b(w["w_ff1"][l].T), b(w["w_ff2"][l]), b(w["w_o"][l]),
        jnp.concatenate([b(w["w_pool_up"][l].T), b(w["w_conv_out"][l].T)], axis=1),
        b(w["w_attn_up"][l].T).reshape(REST_ROWS - REST_ATTN, D), conv], axis=0)


def _unpack_gathered(gathered, small_w, l):
    rest = gathered[:, IN_ROWS:IN_ROWS + REST_ROWS]
    take = lambda r0, rows, c0=0, cols=D: rest[:, r0:r0 + rows, c0:c0 + cols].reshape(N_DEV * rows, cols)
    conv = gathered[:, IN_ROWS + REST_ROWS:IN_ROWS + REST_ROWS + 3, :128].reshape(N_DEV, 3, CONV_W // N_DEV, 2)
    conv = lax.bitcast_convert_type(conv, F32)
    p = {
        "w_in_t": gathered[:, :IN_ROWS].reshape(IN_COLS, D),
        "w_ff1_t": take(REST_FF1, 512), "w_ff2": take(REST_FF2, 512), "w_o": take(REST_WO, 128),
        "w_pool_up_t": take(REST_UP, 128, 0, POOL_W), "w_conv_out_t": take(REST_UP, 128, POOL_W, CONV_W),
        "w_attn_up_t": rest[:, REST_ATTN:].reshape(D, ATTN_OUT),
        "conv_w": jnp.transpose(conv, (1, 0, 2)).reshape(3, CONV_W),
        "pool_mix": small_w["pool_mix"][l],
    }
    for name in ("norm_mix", "b_gate", "pool_scale", "q_gain", "k_gain", "norm_mlp"):
        p[name] = small_w[name][l][None, :]
    return p


def _pack_small_grads(grads):
    flat = jnp.concatenate([jnp.stack([grads[l][name] for l in range(DEPTH)]).reshape(-1) for name, _ in SMALL])
    return jnp.pad(flat, (0, (-flat.shape[0]) % (8 * 128))).reshape(-1, 128)


def _unpack_grads(in_t, rest, small, dev):
    out = {
        "w_in": jnp.stack([a.T for a in in_t]),
        "w_ff1": jnp.stack([a[REST_FF1:REST_FF1 + 512].T for a in rest]),
        "w_ff2": jnp.stack([a[REST_FF2:REST_FF2 + 512] for a in rest]),
        "w_o": jnp.stack([a[REST_WO:REST_WO + 128] for a in rest]),
        "w_pool_up": jnp.stack([a[REST_UP:REST_UP + 128, :POOL_W].T for a in rest]),
        "w_conv_out": jnp.stack([a[REST_UP:REST_UP + 128, POOL_W:].T for a in rest]),
        "w_attn_up": jnp.stack([a[REST_ATTN:].reshape(D // N_DEV, ATTN_OUT).T for a in rest]),
    }
    flat = small.reshape(-1)
    off = 0
    for name, shp in SMALL:
        n = 1
        for s in shp:
            n *= s
        out[name] = flat[off:off + n].reshape(shp)
        off += n
    width = CONV_W // N_DEV
    out["conv_w"] = lax.dynamic_slice_in_dim(out["conv_w"], dev * width, width, axis=2)
    return out


def _pack_small(arrs):
    flat = jnp.concatenate([a.reshape(-1) for a in arrs])
    pad = (-flat.shape[0]) % (8 * 128)
    return jnp.pad(flat, (0, pad)).reshape(-1, 128)


def _unpack_small(packed, like):
    flat = packed.reshape(-1)
    out, off = [], 0
    for a in like:
        out.append(flat[off:off + a.size].reshape(a.shape))
        off += a.size
    return out


WEIGHTS = ("norm_mix", "w_in", "b_gate", "pool_mix", "pool_scale", "conv_w", "q_gain", "k_gain", "w_pool_up",
           "w_conv_out", "w_attn_up", "w_o", "norm_mlp", "w_ff1", "w_ff2")


def kernel(x, norm_mix, w_in, b_gate, pool_mix, pool_scale, conv_w, q_gain, k_gain, w_pool_up, w_conv_out, w_attn_up, w_o, norm_mlp, w_ff1, w_ff2, loss_target, m_norm_mix, m_w_in, m_b_gate, m_pool_mix, m_pool_scale, m_conv_w, m_q_gain, m_k_gain, m_w_pool_up, m_w_conv_out, m_w_attn_up, m_w_o, m_norm_mlp, m_w_ff1, m_w_ff2, v_norm_mix, v_w_in, v_b_gate, v_pool_mix, v_pool_scale, v_conv_w, v_q_gain, v_k_gain, v_w_pool_up, v_w_conv_out, v_w_attn_up, v_w_o, v_norm_mlp, v_w_ff1, v_w_ff2):
    w = dict(zip(WEIGHTS, (norm_mix, w_in, b_gate, pool_mix, pool_scale, conv_w, q_gain, k_gain, w_pool_up,
                           w_conv_out, w_attn_up, w_o, norm_mlp, w_ff1, w_ff2)))
    m = dict(zip(WEIGHTS, (m_norm_mix, m_w_in, m_b_gate, m_pool_mix, m_pool_scale, m_conv_w, m_q_gain, m_k_gain,
                           m_w_pool_up, m_w_conv_out, m_w_attn_up, m_w_o, m_norm_mlp, m_w_ff1, m_w_ff2)))
    v = dict(zip(WEIGHTS, (v_norm_mix, v_w_in, v_b_gate, v_pool_mix, v_pool_scale, v_conv_w, v_q_gain, v_k_gain,
                           v_w_pool_up, v_w_conv_out, v_w_attn_up, v_w_o, v_norm_mlp, v_w_ff1, v_w_ff2)))
    xi, yi, ci = _mesh_pos()
    dev = 4 * xi + 2 * yi + ci

    params = []
    for l in range(DEPTH):
        gathered = _all_gather_sc(f"gather_weights{l}", _pack_weight_shards(w, l), 1 + l)
        params.append(_unpack_gathered(gathered, w, l))
    loss, dx, grads = _local_step(x[0], loss_target[0], params)
    in_t, rest, g_small = [None] * DEPTH, [None] * DEPTH, None
    for l in reversed(range(DEPTH)):
        bigs = [grads[l]["in_t"].reshape(N_DEV, IN_ROWS, D), grads[l]["rest"]]
        (in_t[l], rest[l]), small = _reduce_scatter(f"rs{l}", bigs, _pack_small_grads(grads) if l == 0 else None)
        g_small = small if l == 0 else g_small
    g = _unpack_grads(in_t, rest, g_small, dev)

    delta, new_m, new_v = {}, {}, {}
    for name in ("w_in", "w_pool_up", "w_conv_out", "w_attn_up", "w_o", "w_ff1", "w_ff2"):
        shp = w[name].shape
        two_d = (shp[0] * shp[1], shp[2])
        d_, m_, v_ = _adamw("adamw_" + name, w[name].reshape(two_d), g[name].reshape(two_d),
                            m[name].reshape(two_d), v[name].reshape(two_d))
        delta[name], new_m[name], new_v[name] = d_.reshape(shp), m_.reshape(shp), v_.reshape(shp)
    small_names = [name for name, _ in SMALL]
    packed = [_pack_small([src[name] for name in small_names]) for src in (w, g, m, v)]
    outs = _adamw("adamw_small", *packed)
    for dst, arr in zip((delta, new_m, new_v), outs):
        for name, val in zip(small_names, _unpack_small(arr, [w[name] for name in small_names])):
            dst[name] = val

    total = lax.psum(loss[0, 0], ("x", "y", "c"))
    return (total, dx[None], *[g[n] for n in WEIGHTS], *[delta[n] for n in WEIGHTS],
            *[new_m[n] for n in WEIGHTS], *[new_v[n] for n in WEIGHTS])
```

```python
import functools

import jax
import jax.numpy as jnp
from jax import lax
from jax.experimental import pallas as pl
from jax.experimental.pallas import tpu as pltpu
from jax.experimental.pallas import tpu_sc as plsc

F32 = jnp.float32
BF16 = jnp.bfloat16
MESH = pl.DeviceIdType.MESH

D = 1024
SEQ = 4096
DEPTH = 2
N_DEV = 8
POOL_WINDOWS = (2, 4, 8, 16)
POOL_W = 512
POOL_G = 128
CONV_W = 512
HEAD = 64
ATTN_GROUPS = ((128, 1), (512, 4), (2048, 16))
HPG = 4
ATTN_W = 768
ATTN_OUT = 256
ATTN_BLK = 128
D_FF = 4096
EPS = 1e-6
MASK = -1e30
OFF_POOL = 0
OFF_CB = 512
OFF_CC = 1024
OFF_CX = 1536
OFF_Q = 2048
OFF_K = 2816
OFF_V = 3584
OFF_GATE = 4352
IN_COLS = 7424
ADAM_LR = 0.001
ADAM_B1 = 0.9
ADAM_B2 = 0.999
ADAM_EPS = 1e-08
ADAM_WD = 0.01
ADAM_STEP = 10

ROW_TILE = 512
SEQ_CHUNK = 256
HALO = 16
VMEM_LIMIT = 56 * 1024 * 1024


def _cparams(sem=None):
    return pltpu.CompilerParams(dimension_semantics=sem, vmem_limit_bytes=VMEM_LIMIT)


_DIMS = {"nn": (((1,), (0,)), ((), ())), "nt": (((1,), (1,)), ((), ())), "tn": (((0,), (0,)), ((), ()))}


def _matmul(name, a, b, mode, tm, tn, tk, out_dtypes=(F32,), extras=(), epilogue=None, into=None):
    if mode == "tn":
        K, M = a.shape
    else:
        M, K = a.shape
    N = b.shape[0] if mode == "nt" else b.shape[1]
    assert M % tm == 0 and N % tn == 0 and K % tk == 0, (name, M, N, K, tm, tn, tk)
    nk = K // tk
    n_extra = len(extras)
    n_out = len(out_dtypes)
    dims = _DIMS[mode]
    n_alias = 0 if into is None or isinstance(into[0], jax.ShapeDtypeStruct) else 1

    def body(a_ref, b_ref, *rest):
        extra_refs = rest[:n_extra]
        out_refs = rest[n_extra + n_alias:n_extra + n_alias + n_out]

        def finish(acc):
            if epilogue is None:
                res = (acc,)
            else:
                res = epilogue(acc, *[r[...] for r in extra_refs])
            for o_ref, v in zip(out_refs, res):
                o_ref[...] = v.astype(o_ref.dtype)

        part = lax.dot_general(a_ref[...].astype(BF16), b_ref[...].astype(BF16), dims,
                               preferred_element_type=F32)
        if nk == 1:
            finish(part)
        else:
            acc_ref = rest[-1]
            k = pl.program_id(2)

            @pl.when(k == 0)
            def _():
                acc_ref[...] = part

            @pl.when(k > 0)
            def _():
                acc_ref[...] += part

            @pl.when(k == nk - 1)
            def _():
                finish(acc_ref[...])

    if mode == "tn":
        a_spec = pl.BlockSpec((tk, tm), lambda i, j, k: (k, i))
    else:
        a_spec = pl.BlockSpec((tm, tk), lambda i, j, k: (i, k))
    if mode == "nt":
        b_spec = pl.BlockSpec((tn, tk), lambda i, j, k: (j, k))
    else:
        b_spec = pl.BlockSpec((tk, tn), lambda i, j, k: (k, j))
    in_specs = [a_spec, b_spec]
    args = [a, b]
    for arr, kind in extras:
        if kind == "mn":
            in_specs.append(pl.BlockSpec((tm, tn), lambda i, j, k: (i, j)))
        else:
            in_specs.append(pl.BlockSpec((1, tn), lambda i, j, k: (0, j)))
        args.append(arr)
    out_shape = tuple(jax.ShapeDtypeStruct((M, N), dt) for dt in out_dtypes)
    out_specs = tuple(pl.BlockSpec((tm, tn), lambda i, j, k: (i, j)) for _ in out_dtypes)
    aliases = {}
    if into is not None:
        buf, row0, col0 = into
        assert n_out == 1 and (M // N_DEV) % tm == 0 and row0 % tm == 0 and col0 % tn == 0
        per_dev = M // N_DEV // tm
        out_shape = (jax.ShapeDtypeStruct(buf.shape, buf.dtype),)
        out_specs = (pl.BlockSpec((None, tm, tn), lambda i, j, k: (i // per_dev, row0 // tm + i % per_dev,
                                                                   col0 // tn + j)),)
        if not isinstance(buf, jax.ShapeDtypeStruct):
            aliases = {len(args): 0}
            in_specs.append(pl.BlockSpec(memory_space=pl.ANY))
            args.append(buf)
    scratch = [] if nk == 1 else [pltpu.VMEM((tm, tn), F32)]
    res = pl.pallas_call(
        body, name=name, grid=(M // tm, N // tn, nk), in_specs=in_specs, out_specs=out_specs,
        out_shape=out_shape, scratch_shapes=scratch, input_output_aliases=aliases,
        compiler_params=_cparams(("parallel", "parallel", "arbitrary")),
    )(*args)
    return res if n_out > 1 else res[0]


def _rmsnorm_fwd(name, x, gain):
    S_, D_ = x.shape

    def body(x_ref, g_ref, h_ref):
        xf = x_ref[...]
        ms = jnp.mean(xf * xf, axis=-1, keepdims=True)
        h_ref[...] = (xf * lax.rsqrt(ms + EPS) * g_ref[...]).astype(BF16)

    return pl.pallas_call(
        body, name=name, grid=(S_ // ROW_TILE,),
        in_specs=[pl.BlockSpec((ROW_TILE, D_), lambda i: (i, 0)), pl.BlockSpec((1, D_), lambda i: (0, 0))],
        out_specs=pl.BlockSpec((ROW_TILE, D_), lambda i: (i, 0)),
        out_shape=jax.ShapeDtypeStruct((S_, D_), BF16),
        compiler_params=_cparams(("parallel",)),
    )(x, gain)


def _rmsnorm_bwd(name, x, gain, dh, dres):
    S_, D_ = x.shape
    n = S_ // ROW_TILE

    def body(x_ref, g_ref, dh_ref, dres_ref, dx_ref, dg_ref, acc_ref):
        i = pl.program_id(0)
        xf = x_ref[...]
        rstd = lax.rsqrt(jnp.mean(xf * xf, axis=-1, keepdims=True) + EPS)
        xhat = xf * rstd
        dhv = dh_ref[...]
        dxhat = dhv * g_ref[...]
        c = jnp.mean(dxhat * xhat, axis=-1, keepdims=True)
        dx_ref[...] = dres_ref[...] + rstd * (dxhat - xhat * c)
        part = jnp.sum((dhv * xhat).reshape(ROW_TILE // 8, 8, D_), axis=0)

        @pl.when(i == 0)
        def _():
            acc_ref[...] = part

        @pl.when(i > 0)
        def _():
            acc_ref[...] += part

        @pl.when(i == n - 1)
        def _():
            dg_ref[...] = jnp.sum(acc_ref[...], axis=0, keepdims=True)

    row = pl.BlockSpec((ROW_TILE, D_), lambda i: (i, 0))
    vec = pl.BlockSpec((1, D_), lambda i: (0, 0))
    return pl.pallas_call(
        body, name=name, grid=(n,), in_specs=[row, vec, row, row], out_specs=(row, vec),
        out_shape=(jax.ShapeDtypeStruct((S_, D_), F32), jax.ShapeDtypeStruct((1, D_), F32)),
        scratch_shapes=[pltpu.VMEM((8, D_), F32)],
        compiler_params=_cparams(("arbitrary",)),
    )(x, gain, dh, dres)


def _loss_head(name, y, target):
    S_, D_ = y.shape
    n = S_ // ROW_TILE

    def body(y_ref, t_ref, dy_ref, l_ref, acc_ref):
        i = pl.program_id(0)
        e = y_ref[...] - t_ref[...]
        dy_ref[...] = e * (1.0 / D_)
        part = jnp.sum((e * e).reshape(ROW_TILE // 8, 8, D_), axis=0)

        @pl.when(i == 0)
        def _():
            acc_ref[...] = part

        @pl.when(i > 0)
        def _():
            acc_ref[...] += part

        @pl.when(i == n - 1)
        def _():
            s = jnp.sum(acc_ref[...], axis=1, keepdims=True)
            l_ref[...] = jnp.sum(s, axis=0, keepdims=True) * (0.5 / D_)

    row = pl.BlockSpec((ROW_TILE, D_), lambda i: (i, 0))
    return pl.pallas_call(
        body, name=name, grid=(n,), in_specs=[row, row],
        out_specs=(row, pl.BlockSpec((1, 1), lambda i: (0, 0))),
        out_shape=(jax.ShapeDtypeStruct((S_, D_), F32), jax.ShapeDtypeStruct((1, 1), F32)),
        scratch_shapes=[pltpu.VMEM((8, D_), F32)],
        compiler_params=_cparams(("arbitrary",)),
    )(y, target)


def _rows_with_halo(ref, cols, i, n_chunks, before, after):
    r0 = pl.multiple_of(i * SEQ_CHUNK, SEQ_CHUNK)
    parts = []
    if before:
        h0 = pl.multiple_of(jnp.maximum(r0 - HALO, 0), 8)
        halo = ref[pl.ds(h0, HALO), cols]
        parts.append(jnp.where(i > 0, halo, jnp.zeros_like(halo)))
    parts.append(ref[pl.ds(r0, SEQ_CHUNK), cols])
    if after:
        a0 = pl.multiple_of(jnp.minimum(r0 + SEQ_CHUNK, (n_chunks - 1) * SEQ_CHUNK + SEQ_CHUNK - HALO), 8)
        halo = ref[pl.ds(a0, HALO), cols]
        parts.append(jnp.where(i < n_chunks - 1, halo, jnp.zeros_like(halo)))
    return parts[0] if len(parts) == 1 else jnp.concatenate(parts, axis=0)


def _shift_down(v, k):
    return pltpu.roll(v, k, 0)


def _shift_up(v, k):
    return pltpu.roll(v, v.shape[0] - k, 0)


def _pool_diff(xx, w, t_main):
    s = xx
    k = 1
    while k < w:
        s = s + _shift_down(s, k)
        k *= 2
    cnt = jnp.minimum(t_main + 1, w).astype(F32)
    return s[HALO:] / cnt - xx[HALO:]


def _pool_fwd(name, z, pool_mix, pool_scale):
    S_ = z.shape[0]
    n_chunks = S_ // SEQ_CHUNK

    def body(u_ref, mix_ref, sc_ref, y_ref):
        for g, w in enumerate(POOL_WINDOWS):
            cols = slice(g * POOL_G, (g + 1) * POOL_G)
            mixg = mix_ref[g].astype(BF16)
            scg = sc_ref[:, cols]

            def chunk(i, carry, cols=cols, mixg=mixg, scg=scg, w=w):
                r0 = pl.multiple_of(i * SEQ_CHUNK, SEQ_CHUNK)
                xx = _rows_with_halo(u_ref, cols, i, n_chunks, True, False)
                t = r0 + lax.broadcasted_iota(jnp.int32, (SEQ_CHUNK, POOL_G), 0)
                d = _pool_diff(xx, w, t)
                y = jnp.dot(d.astype(BF16), mixg, preferred_element_type=F32) * scg
                y_ref[pl.ds(r0, SEQ_CHUNK), cols] = y.astype(BF16)
                return carry

            lax.fori_loop(0, n_chunks, chunk, 0)

    return pl.pallas_call(
        body, name=name, grid=(1,),
        in_specs=[pl.BlockSpec((S_, POOL_W), lambda i: (0, 0)),
                  pl.BlockSpec((4, POOL_G, POOL_G), lambda i: (0, 0, 0)),
                  pl.BlockSpec((1, POOL_W), lambda i: (0, 0))],
        out_specs=pl.BlockSpec((S_, POOL_W), lambda i: (0, 0)),
        out_shape=jax.ShapeDtypeStruct((S_, POOL_W), BF16),
        compiler_params=_cparams(("arbitrary",)),
    )(z, pool_mix, pool_scale)


def _pool_bwd(name, z, dy, pool_mix, pool_scale):
    S_ = z.shape[0]
    n_chunks = S_ // SEQ_CHUNK
    rows_a = SEQ_CHUNK + HALO

    def body(u_ref, dy_ref, mix_ref, sc_ref, du_ref, dmix_ref, dsc_ref):
        for g, w in enumerate(POOL_WINDOWS):
            cols = slice(g * POOL_G, (g + 1) * POOL_G)
            mixg = mix_ref[g].astype(BF16)
            scg = sc_ref[:, cols]

            def chunk(i, carry, cols=cols, mixg=mixg, scg=scg, w=w):
                dmix_acc, dsc_acc = carry
                r0 = pl.multiple_of(i * SEQ_CHUNK, SEQ_CHUNK)
                xx = _rows_with_halo(u_ref, cols, i, n_chunks, True, False)
                t = r0 + lax.broadcasted_iota(jnp.int32, (SEQ_CHUNK, POOL_G), 0)
                d = _pool_diff(xx, w, t).astype(BF16)
                ypre = jnp.dot(d, mixg, preferred_element_type=F32)
                dyy = _rows_with_halo(dy_ref, cols, i, n_chunks, False, True)
                dys = (dyy * scg).astype(BF16)
                dsc_acc = dsc_acc + jnp.sum((dyy[:SEQ_CHUNK] * ypre).reshape(SEQ_CHUNK // 8, 8, POOL_G), axis=0)
                dmix_acc = dmix_acc + lax.dot_general(d, dys[:SEQ_CHUNK], _DIMS["tn"], preferred_element_type=F32)
                dd = lax.dot_general(dys, mixg, _DIMS["nt"], preferred_element_type=F32)
                ta = r0 + lax.broadcasted_iota(jnp.int32, (rows_a, POOL_G), 0)
                f = dd / jnp.minimum(ta + 1, w).astype(F32)
                k = 1
                while k < w:
                    f = f + _shift_up(f, k)
                    k *= 2
                du_ref[pl.ds(r0, SEQ_CHUNK), cols] = (f[:SEQ_CHUNK] - dd[:SEQ_CHUNK]).astype(BF16)
                return dmix_acc, dsc_acc

            dmix_acc, dsc_acc = lax.fori_loop(
                0, n_chunks, chunk, (jnp.zeros((POOL_G, POOL_G), F32), jnp.zeros((8, POOL_G), F32)))
            dmix_ref[g] = dmix_acc
            dsc_ref[:, cols] = jnp.sum(dsc_acc, axis=0, keepdims=True)

    full = pl.BlockSpec((S_, POOL_W), lambda i: (0, 0))
    mix_spec = pl.BlockSpec((4, POOL_G, POOL_G), lambda i: (0, 0, 0))
    vec = pl.BlockSpec((1, POOL_W), lambda i: (0, 0))
    return pl.pallas_call(
        body, name=name, grid=(1,), in_specs=[full, full, mix_spec, vec], out_specs=(full, mix_spec, vec),
        out_shape=(jax.ShapeDtypeStruct((S_, POOL_W), BF16), jax.ShapeDtypeStruct((4, POOL_G, POOL_G), F32),
                   jax.ShapeDtypeStruct((1, POOL_W), F32)),
        compiler_params=_cparams(("arbitrary",)),
    )(z, dy, pool_mix, pool_scale)


def _conv_specs(S_):
    slab = lambda off: pl.BlockSpec((S_, 128), lambda c, off=off: (0, off // 128 + c))
    return slab(OFF_CB), slab(OFF_CC), slab(OFF_CX)


def _conv_fwd(name, z, conv_w):
    S_ = z.shape[0]
    n_chunks = S_ // SEQ_CHUNK
    col = slice(0, 128)

    def body(b_ref, c_ref, x_ref, w_ref, y_ref):
        w0, w1, w2 = w_ref[0:1, :], w_ref[1:2, :], w_ref[2:3, :]

        def chunk(i, carry):
            r0 = pl.multiple_of(i * SEQ_CHUNK, SEQ_CHUNK)
            u = _rows_with_halo(c_ref, col, i, n_chunks, True, False) * _rows_with_halo(x_ref, col, i, n_chunks, True, False)
            y = w2 * u + w1 * _shift_down(u, 1) + w0 * _shift_down(u, 2)
            y_ref[pl.ds(r0, SEQ_CHUNK), :] = (b_ref[pl.ds(r0, SEQ_CHUNK), :] * y[HALO:]).astype(BF16)
            return carry

        lax.fori_loop(0, n_chunks, chunk, 0)

    sb, sc, sx = _conv_specs(S_)
    return pl.pallas_call(
        body, name=name, grid=(CONV_W // 128,),
        in_specs=[sb, sc, sx, pl.BlockSpec((3, 128), lambda c: (0, c))],
        out_specs=pl.BlockSpec((S_, 128), lambda c: (0, c)),
        out_shape=jax.ShapeDtypeStruct((S_, CONV_W), BF16),
        compiler_params=_cparams(("parallel",)),
    )(z, z, z, conv_w)


def _conv_bwd(name, z, dy, conv_w):
    S_ = z.shape[0]
    n_chunks = S_ // SEQ_CHUNK
    col = slice(0, 128)
    lo, hi = HALO, HALO + SEQ_CHUNK

    def body(b_ref, c_ref, x_ref, dy_ref, w_ref, db_ref, dc_ref, dx_ref, dw_ref):
        w0, w1, w2 = w_ref[0:1, :], w_ref[1:2, :], w_ref[2:3, :]

        def chunk(i, carry):
            a0, a1, a2 = carry
            r0 = pl.multiple_of(i * SEQ_CHUNK, SEQ_CHUNK)
            cc = _rows_with_halo(c_ref, col, i, n_chunks, True, True)
            xx = _rows_with_halo(x_ref, col, i, n_chunks, True, True)
            bb = _rows_with_halo(b_ref, col, i, n_chunks, True, True)
            dyy = _rows_with_halo(dy_ref, col, i, n_chunks, True, True)
            u = cc * xx
            u1 = _shift_down(u, 1)
            u2 = _shift_down(u, 2)
            y = w2 * u + w1 * u1 + w0 * u2
            dyv = dyy * bb
            du = w2 * dyv + w1 * _shift_up(dyv, 1) + w0 * _shift_up(dyv, 2)
            db_ref[pl.ds(r0, SEQ_CHUNK), :] = (dyy[lo:hi] * y[lo:hi]).astype(BF16)
            dc_ref[pl.ds(r0, SEQ_CHUNK), :] = (du[lo:hi] * xx[lo:hi]).astype(BF16)
            dx_ref[pl.ds(r0, SEQ_CHUNK), :] = (du[lo:hi] * cc[lo:hi]).astype(BF16)
            red = lambda v: jnp.sum(v.reshape(SEQ_CHUNK // 8, 8, 128), axis=0)
            dm = dyv[lo:hi]
            return a0 + red(dm * u2[lo:hi]), a1 + red(dm * u1[lo:hi]), a2 + red(dm * u[lo:hi])

        zero = jnp.zeros((8, 128), F32)
        a0, a1, a2 = lax.fori_loop(0, n_chunks, chunk, (zero, zero, zero))
        dw_ref[0:1, :] = jnp.sum(a0, axis=0, keepdims=True)
        dw_ref[1:2, :] = jnp.sum(a1, axis=0, keepdims=True)
        dw_ref[2:3, :] = jnp.sum(a2, axis=0, keepdims=True)

    sb, sc, sx = _conv_specs(S_)
    slab = pl.BlockSpec((S_, 128), lambda c: (0, c))
    wspec = pl.BlockSpec((3, 128), lambda c: (0, c))
    act = jax.ShapeDtypeStruct((S_, CONV_W), BF16)
    return pl.pallas_call(
        body, name=name, grid=(CONV_W // 128,), in_specs=[sb, sc, sx, slab, wspec],
        out_specs=(slab, slab, slab, wspec),
        out_shape=(act, act, act, jax.ShapeDtypeStruct((3, CONV_W), F32)),
        compiler_params=_cparams(("parallel",)),
    )(z, z, z, dy, conv_w)


def _head_norm(v, gain):
    rstd = lax.rsqrt(jnp.mean(v * v, axis=-1, keepdims=True) + EPS)
    xhat = v * rstd
    return xhat * gain, xhat, rstd


def _head_norm_bwd(dy, xhat, rstd, gain):
    dxhat = dy * gain
    c = jnp.mean(dxhat * xhat, axis=-1, keepdims=True)
    dv = rstd * (dxhat - xhat * c)
    dg = jnp.sum((dy * xhat).reshape(ATTN_BLK // 8, 8, HEAD), axis=0)
    return dv, dg


def _band_masks():
    qi = lax.broadcasted_iota(jnp.int32, (ATTN_BLK, ATTN_BLK), 0)
    ki = lax.broadcasted_iota(jnp.int32, (ATTN_BLK, ATTN_BLK), 1)
    return ki <= qi, ki >= qi


def _attn_specs(S_, g, dil):
    rows = ATTN_BLK * dil
    nb = S_ // rows
    cq, ck, cv = (OFF_Q // PAIR_W + g * PAIRS, OFF_K // PAIR_W + g * PAIRS, OFF_V // PAIR_W + g * PAIRS)
    return rows, nb, cq, ck, cv


PAIR_W = 128
PAIR = PAIR_W // HEAD
PAIRS = HPG // PAIR


def _rows_of(ref, r, dil):
    if dil == 1:
        return ref[...]
    return ref[pl.ds(r, ATTN_BLK, stride=dil), :]


def _put_rows(ref, r, dil, val):
    if dil == 1:
        ref[...] = val.astype(ref.dtype)
    else:
        ref[pl.ds(r, ATTN_BLK, stride=dil), :] = val.astype(ref.dtype)


def _attn_fwd(name, z, q_gain, k_gain, g, dil):
    S_ = z.shape[0]
    rows, nb, cq, ck, cv = _attn_specs(S_, g, dil)
    scale = HEAD ** -0.5

    def body(q_ref, kc_ref, kp_ref, vc_ref, vp_ref, gq_ref, gk_ref, o_ref, l_ref):
        n = pl.program_id(1)
        m_cur, m_prev = _band_masks()
        m_prev = jnp.logical_and(m_prev, n > 0)
        gq, gk = gq_ref[...], gk_ref[...]
        for r in range(dil):
            q, kc, kp = _rows_of(q_ref, r, dil), _rows_of(kc_ref, r, dil), _rows_of(kp_ref, r, dil)
            vc, vp = _rows_of(vc_ref, r, dil), _rows_of(vp_ref, r, dil)
            outs, lses = [], []
            for h in range(PAIR):
                hs = slice(h * HEAD, (h + 1) * HEAD)
                qn = _head_norm(q[:, hs], gq)[0].astype(BF16)
                kcn = _head_norm(kc[:, hs], gk)[0].astype(BF16)
                kpn = _head_norm(kp[:, hs], gk)[0].astype(BF16)
                s_c = lax.dot_general(qn, kcn, _DIMS["nt"], preferred_element_type=F32) * scale
                s_p = lax.dot_general(qn, kpn, _DIMS["nt"], preferred_element_type=F32) * scale
                s_c = jnp.where(m_cur, s_c, MASK)
                s_p = jnp.where(m_prev, s_p, MASK)
                m = jnp.maximum(jnp.max(s_c, axis=-1, keepdims=True), jnp.max(s_p, axis=-1, keepdims=True))
                p_c = jnp.exp(s_c - m)
                p_p = jnp.exp(s_p - m)
                den = jnp.sum(p_c, axis=-1, keepdims=True) + jnp.sum(p_p, axis=-1, keepdims=True)
                acc = (jnp.dot(p_c.astype(BF16), vc[:, hs].astype(BF16), preferred_element_type=F32)
                       + jnp.dot(p_p.astype(BF16), vp[:, hs].astype(BF16), preferred_element_type=F32))
                outs.append(acc / den)
                lses.append(jnp.broadcast_to(m + jnp.log(den), (ATTN_BLK, HEAD)))
            _put_rows(o_ref, r, dil, jnp.concatenate(outs, axis=1))
            _put_rows(l_ref, r, dil, jnp.concatenate(lses, axis=1))

    cur = lambda c: pl.BlockSpec((rows, PAIR_W), lambda hp, n, c=c: (n, c + hp))
    prev = lambda c: pl.BlockSpec((rows, PAIR_W), lambda hp, n, c=c: (jnp.maximum(n - 1, 0), c + hp))
    gspec = pl.BlockSpec((1, HEAD), lambda hp, n: (0, 0))
    shp = jax.ShapeDtypeStruct((S_, ATTN_OUT), F32)
    return pl.pallas_call(
        body, name=name, grid=(PAIRS, nb),
        in_specs=[cur(cq), cur(ck), prev(ck), cur(cv), prev(cv), gspec, gspec],
        out_specs=(cur(0), cur(0)), out_shape=(shp, shp),
        compiler_params=_cparams(("parallel", "parallel")),
    )(z, z, z, z, z, q_gain, k_gain)


def _attn_combine(name, os_, ls_):
    S_ = os_[0].shape[0]

    def body(o0, o1, o2, l0, l1, l2, o_ref, l_ref):
        a, b, c = l0[...], l1[...], l2[...]
        m = jnp.maximum(jnp.maximum(a, b), c)
        ea, eb, ec = jnp.exp(a - m), jnp.exp(b - m), jnp.exp(c - m)
        zsum = ea + eb + ec
        o_ref[...] = (ea * o0[...] + eb * o1[...] + ec * o2[...]) / zsum
        l_ref[...] = m + jnp.log(zsum)

    row = pl.BlockSpec((ROW_TILE, ATTN_OUT), lambda i: (i, 0))
    shp = jax.ShapeDtypeStruct((S_, ATTN_OUT), F32)
    return pl.pallas_call(
        body, name=name, grid=(S_ // ROW_TILE,), in_specs=[row] * 6, out_specs=(row, row), out_shape=(shp, shp),
        compiler_params=_cparams(("parallel",)),
    )(*os_, *ls_)


def _attn_bwd(name, z, q_gain, k_gain, do, o, lse, g, dil):
    S_ = z.shape[0]
    rows, nb, cq, ck, cv = _attn_specs(S_, g, dil)
    scale = HEAD ** -0.5

    def body(q_ref, kc_ref, kp_ref, vc_ref, vp_ref, gq_ref, gk_ref, do_ref, o_ref, l_ref,
             dq_ref, dk_ref, dv_ref, dgq_ref, dgk_ref, ck_ref, cvv_ref, gq_acc, gk_acc):
        hp = pl.program_id(0)
        n = pl.program_id(1)
        live = n < nb
        m_cur, m_prev = _band_masks()
        m_cur = jnp.logical_and(m_cur, live)
        m_prev = jnp.logical_and(m_prev, jnp.logical_and(n > 0, live))
        gq, gk = gq_ref[...], gk_ref[...]

        @pl.when(n == 0)
        def _():
            ck_ref[...] = jnp.zeros_like(ck_ref)
            cvv_ref[...] = jnp.zeros_like(cvv_ref)

        @pl.when(jnp.logical_and(n == 0, hp == 0))
        def _():
            gq_acc[...] = jnp.zeros_like(gq_acc)
            gk_acc[...] = jnp.zeros_like(gk_acc)

        for r in range(dil):
            q, kc, kp = _rows_of(q_ref, r, dil), _rows_of(kc_ref, r, dil), _rows_of(kp_ref, r, dil)
            vc, vp = _rows_of(vc_ref, r, dil), _rows_of(vp_ref, r, dil)
            dov, ov, lv = _rows_of(do_ref, r, dil), _rows_of(o_ref, r, dil), _rows_of(l_ref, r, dil)
            carry_k, carry_v = _rows_of(ck_ref, r, dil), _rows_of(cvv_ref, r, dil)
            dqs, dks, dvs, nck, ncv = [], [], [], [], []
            dgq = jnp.zeros((8, HEAD), F32)
            dgk = jnp.zeros((8, HEAD), F32)
            for h in range(PAIR):
                hs = slice(h * HEAD, (h + 1) * HEAD)
                qn, q_hat, q_rstd = _head_norm(q[:, hs], gq)
                kcn = _head_norm(kc[:, hs], gk)[0].astype(BF16)
                kpn, kp_hat, kp_rstd = _head_norm(kp[:, hs], gk)
                qn, kpn = qn.astype(BF16), kpn.astype(BF16)
                do_h = dov[:, hs]
                do_b = do_h.astype(BF16)
                delta = jnp.sum(do_h * ov[:, hs], axis=-1, keepdims=True)
                l_h = lv[:, h * HEAD:h * HEAD + 1]
                s_c = lax.dot_general(qn, kcn, _DIMS["nt"], preferred_element_type=F32) * scale
                s_p = lax.dot_general(qn, kpn, _DIMS["nt"], preferred_element_type=F32) * scale
                p_c = jnp.where(m_cur, jnp.exp(jnp.where(m_cur, s_c, MASK) - l_h), 0.0)
                p_p = jnp.where(m_prev, jnp.exp(jnp.where(m_prev, s_p, MASK) - l_h), 0.0)
                dp_c = lax.dot_general(do_b, vc[:, hs].astype(BF16), _DIMS["nt"], preferred_element_type=F32)
                dp_p = lax.dot_general(do_b, vp[:, hs].astype(BF16), _DIMS["nt"], preferred_element_type=F32)
                ds_c = (p_c * (dp_c - delta) * scale).astype(BF16)
                ds_p = (p_p * (dp_p - delta) * scale).astype(BF16)
                dqn = (jnp.dot(ds_c, kcn, preferred_element_type=F32)
                       + jnp.dot(ds_p, kpn, preferred_element_type=F32))
                dq_h, dg = _head_norm_bwd(dqn, q_hat, q_rstd, gq)
                dgq = dgq + dg
                dqs.append(dq_h)
                dkn_prev = carry_k[:, hs] + lax.dot_general(ds_p, qn, _DIMS["tn"], preferred_element_type=F32)
                dk_h, dg = _head_norm_bwd(dkn_prev, kp_hat, kp_rstd, gk)
                dgk = dgk + dg
                dks.append(dk_h)
                dvs.append(carry_v[:, hs] + lax.dot_general(p_p.astype(BF16), do_b, _DIMS["tn"],
                                                            preferred_element_type=F32))
                nck.append(lax.dot_general(ds_c, qn, _DIMS["tn"], preferred_element_type=F32))
                ncv.append(lax.dot_general(p_c.astype(BF16), do_b, _DIMS["tn"], preferred_element_type=F32))
            dq_all = jnp.concatenate(dqs, axis=1)

            @pl.when(live)
            def _(dq_all=dq_all, r=r):
                _put_rows(dq_ref, r, dil, dq_all)

            _put_rows(dk_ref, r, dil, jnp.concatenate(dks, axis=1))
            _put_rows(dv_ref, r, dil, jnp.concatenate(dvs, axis=1))
            _put_rows(ck_ref, r, dil, jnp.concatenate(nck, axis=1))
            _put_rows(cvv_ref, r, dil, jnp.concatenate(ncv, axis=1))
            gq_acc[...] += dgq
            gk_acc[...] += dgk

        @pl.when(jnp.logical_and(n == nb, hp == PAIRS - 1))
        def _():
            dgq_ref[...] = jnp.sum(gq_acc[...], axis=0, keepdims=True)
            dgk_ref[...] = jnp.sum(gk_acc[...], axis=0, keepdims=True)

    last = nb - 1
    cur = lambda c: pl.BlockSpec((rows, PAIR_W), lambda hp, n, c=c: (jnp.minimum(n, last), c + hp))
    prev = lambda c: pl.BlockSpec((rows, PAIR_W), lambda hp, n, c=c: (jnp.maximum(n - 1, 0), c + hp))
    gspec = pl.BlockSpec((1, HEAD), lambda hp, n: (0, 0))
    act = jax.ShapeDtypeStruct((S_, ATTN_OUT), F32)
    vec = jax.ShapeDtypeStruct((1, HEAD), F32)
    return pl.pallas_call(
        body, name=name, grid=(PAIRS, nb + 1),
        in_specs=[cur(cq), cur(ck), prev(ck), cur(cv), prev(cv), gspec, gspec, cur(0), cur(0), cur(0)],
        out_specs=(cur(0), prev(0), prev(0), gspec, gspec),
        out_shape=(act, act, act, vec, vec),
        scratch_shapes=[pltpu.VMEM((rows, PAIR_W), F32), pltpu.VMEM((rows, PAIR_W), F32),
                        pltpu.VMEM((8, HEAD), F32), pltpu.VMEM((8, HEAD), F32)],
        compiler_params=_cparams(("arbitrary", "arbitrary")),
    )(z, z, z, z, z, q_gain, k_gain, do, o, lse)


MIX_TN = 256


def _sigmoid(v):
    return 1.0 / (1.0 + jnp.exp(-v))


def _mix_fwd(name, z, b_gate, ys, ws):
    S_ = z.shape[0]
    tm, tn = ROW_TILE, MIX_TN
    gblk = OFF_GATE // tn

    def body(yp, yc, ya, wp, wc, wa, g0, g1, g2, b0, b1, b2, m_ref):
        acc = None
        for y_ref, w_ref, g_ref, b_ref in ((yp, wp, g0, b0), (yc, wc, g1, b1), (ya, wa, g2, b2)):
            u = lax.dot_general(y_ref[...].astype(BF16), w_ref[...], _DIMS["nt"], preferred_element_type=F32)
            t = _sigmoid(g_ref[...] + b_ref[...]) * u
            acc = t if acc is None else acc + t
        m_ref[...] = acc.astype(BF16)

    yspec = lambda w: pl.BlockSpec((tm, w), lambda i, j: (i, 0))
    wspec = lambda w: pl.BlockSpec((tn, w), lambda i, j: (j, 0))
    gspec = lambda b: pl.BlockSpec((tm, tn), lambda i, j, b=b: (i, gblk + b * (D // tn) + j))
    bspec = lambda b: pl.BlockSpec((1, tn), lambda i, j, b=b: (0, b * (D // tn) + j))
    return pl.pallas_call(
        body, name=name, grid=(S_ // tm, D // tn),
        in_specs=[yspec(POOL_W), yspec(CONV_W), yspec(ATTN_OUT), wspec(POOL_W), wspec(CONV_W), wspec(ATTN_OUT),
                  gspec(0), gspec(1), gspec(2), bspec(0), bspec(1), bspec(2)],
        out_specs=pl.BlockSpec((tm, tn), lambda i, j: (i, j)),
        out_shape=jax.ShapeDtypeStruct((S_, D), BF16),
        compiler_params=_cparams(("parallel", "parallel")),
    )(*ys, *ws, z, z, z, b_gate, b_gate, b_gate)


def _mix_bwd(name, z, b_gate, y, w, dmerged, branch):
    S_ = z.shape[0]
    tm, tn = ROW_TILE, MIX_TN
    width = y.shape[1]
    gblk = OFF_GATE // tn + branch * (D // tn)
    ni = S_ // tm

    def body(y_ref, w_ref, g_ref, b_ref, dm_ref, du_ref, dg_ref, db_ref, acc_ref):
        i = pl.program_id(1)
        u = lax.dot_general(y_ref[...].astype(BF16), w_ref[...], _DIMS["nt"], preferred_element_type=F32)
        sg = _sigmoid(g_ref[...] + b_ref[...])
        dm = dm_ref[...]
        du_ref[...] = (sg * dm).astype(BF16)
        dpre = dm * u * sg * (1.0 - sg)
        dg_ref[...] = dpre.astype(BF16)
        part = jnp.sum(dpre.reshape(tm // 8, 8, tn), axis=0)

        @pl.when(i == 0)
        def _():
            acc_ref[...] = part

        @pl.when(i > 0)
        def _():
            acc_ref[...] += part

        @pl.when(i == ni - 1)
        def _():
            db_ref[...] = jnp.sum(acc_ref[...], axis=0, keepdims=True)

    blk = pl.BlockSpec((tm, tn), lambda j, i: (i, j))
    vec = pl.BlockSpec((1, tn), lambda j, i: (0, j))
    act = jax.ShapeDtypeStruct((S_, D), BF16)
    return pl.pallas_call(
        body, name=name, grid=(D // tn, ni),
        in_specs=[pl.BlockSpec((tm, width), lambda j, i: (i, 0)), pl.BlockSpec((tn, width), lambda j, i: (j, 0)),
                  pl.BlockSpec((tm, tn), lambda j, i: (i, gblk + j)),
                  pl.BlockSpec((1, tn), lambda j, i: (0, branch * (D // tn) + j)), blk],
        out_specs=(blk, blk, vec), out_shape=(act, act, jax.ShapeDtypeStruct((1, D), F32)),
        scratch_shapes=[pltpu.VMEM((8, tn), F32)],
        compiler_params=_cparams(("parallel", "arbitrary")),
    )(y, w, z, b_gate, dmerged)


def _relu2_epilogue(acc):
    r = jnp.maximum(acc, 0.0)
    return acc, r * r


def _layer_fwd(l, x, p):
    t = f"l{l}_"
    h = _rmsnorm_fwd(t + "norm_mix", x, p["norm_mix"])
    z = _matmul(t + "in_proj", h, p["w_in_t"], "nt", 512, 3712, 1024)
    y_pool = _pool_fwd(t + "pool", z, p["pool_mix"], p["pool_scale"])
    y_conv = _conv_fwd(t + "conv", z, p["conv_w"])
    os_, ls_ = [], []
    for g, (_, dil) in enumerate(ATTN_GROUPS):
        o_g, l_g = _attn_fwd(t + f"attn{g}", z, p["q_gain"], p["k_gain"], g, dil)
        os_.append(o_g)
        ls_.append(l_g)
    y_attn, lse = _attn_combine(t + "attn_mix", os_, ls_)
    merged = _mix_fwd(t + "merge", z, p["b_gate"], (y_pool, y_conv, y_attn),
                      (p["w_pool_up_t"], p["w_conv_out_t"], p["w_attn_up_t"]))
    x1 = _matmul(t + "out_proj", merged, p["w_o"], "nn", 512, 1024, 1024,
                 extras=((x, "mn"),), epilogue=lambda acc, xr: (xr + acc,))
    h2 = _rmsnorm_fwd(t + "norm_mlp", x1, p["norm_mlp"])
    a, r = _matmul(t + "ff1", h2, p["w_ff1_t"], "nt", 512, 1024, 1024, out_dtypes=(F32, BF16),
                   epilogue=_relu2_epilogue)
    x2 = _matmul(t + "ff2", r, p["w_ff2"], "nn", 512, 1024, 1024,
                 extras=((x1, "mn"),), epilogue=lambda acc, xr: (xr + acc,))
    saved = dict(x=x, h=h, z=z, y_pool=y_pool, y_conv=y_conv, y_attn=y_attn, lse=lse, merged=merged,
                 x1=x1, h2=h2, a=a, r=r)
    return x2, saved


def _layer_bwd(l, dx2, p, s):
    t = f"l{l}_b_"
    g = {}
    rest = jax.ShapeDtypeStruct((N_DEV, REST_ROWS, D), F32)
    da = _matmul(t + "d_ff2_in", dx2, p["w_ff2"], "nt", 512, 1024, 1024, out_dtypes=(BF16,),
                 extras=((s["a"], "mn"),), epilogue=lambda acc, a: (acc * (2.0 * jnp.maximum(a, 0.0)),))
    rest = _matmul(t + "dw_ff2", s["r"], dx2, "tn", 512, 1024, 512, into=(rest, REST_FF2, 0))
    dh2 = _matmul(t + "d_ff1_in", da, p["w_ff1_t"], "nn", 512, 1024, 1024)
    rest = _matmul(t + "dw_ff1", da, s["h2"], "tn", 512, 1024, 512, into=(rest, REST_FF1, 0))
    dx1, g["norm_mlp"] = _rmsnorm_bwd(t + "norm_mlp", s["x1"], p["norm_mlp"], dh2, dx2)
    dmerged = _matmul(t + "d_out_proj_in", dx1, p["w_o"], "nt", 512, 1024, 1024)
    rest = _matmul(t + "dw_o", s["merged"], dx1, "tn", 128, 1024, 2048, into=(rest, REST_WO, 0))
    ys = (s["y_pool"], s["y_conv"], s["y_attn"])
    names = ("w_pool_up_t", "w_conv_out_t", "w_attn_up_t")
    dys, dgates, dbs = [], [], []
    for b in range(3):
        du, dgz, db = _mix_bwd(t + f"merge{b}", s["z"], p["b_gate"], ys[b], p[names[b]], dmerged, b)
        width = ys[b].shape[1]
        dys.append(_matmul(t + f"d_up{b}_in", du, p[names[b]], "nn", 512, width, 1024))
        if b < 2:
            rest = _matmul(t + f"dw_up{b}", du, ys[b], "tn", 128, width, 2048, into=(rest, REST_UP, b * width))
        else:
            d_attn_up_t = _matmul(t + f"dw_up{b}", du, ys[b], "tn", 128, width, 2048)
            rest = rest.at[:, REST_ATTN:REST_ROWS, :].set(d_attn_up_t.reshape(N_DEV, REST_ROWS - REST_ATTN, D))
        dgates.append(dgz)
        dbs.append(db)
    g["rest"] = rest
    g["b_gate"] = jnp.concatenate(dbs, axis=1)
    du_pool, g["pool_mix"], g["pool_scale"] = _pool_bwd(t + "pool", s["z"], dys[0], p["pool_mix"], p["pool_scale"])
    dcb, dcc, dcx, g["conv_w"] = _conv_bwd(t + "conv", s["z"], dys[1], p["conv_w"])
    dqs, dks, dvs = [], [], []
    gq = gk = None
    for gi, (_, dil) in enumerate(ATTN_GROUPS):
        dq, dk, dv, dgq, dgk = _attn_bwd(t + f"attn{gi}", s["z"], p["q_gain"], p["k_gain"], dys[2], s["y_attn"],
                                         s["lse"], gi, dil)
        dqs.append(dq)
        dks.append(dk)
        dvs.append(dv)
        gq = dgq if gq is None else gq + dgq
        gk = dgk if gk is None else gk + dgk
    g["q_gain"], g["k_gain"] = gq, gk
    dz = jnp.concatenate([du_pool, dcb, dcc, dcx] + [v.astype(BF16) for v in dqs + dks + dvs] + dgates, axis=1)
    dh = _matmul(t + "d_in_proj_in", dz, p["w_in_t"], "nn", 512, 1024, 3712)
    g["in_t"] = _matmul(t + "dw_in", dz, s["h"], "tn", 256, 1024, 4096)
    dx, g["norm_mix"] = _rmsnorm_bwd(t + "norm_mix", s["x"], p["norm_mix"], dh, dx1)
    return dx, g


def _local_step(x, target, params):
    saved = []
    act = x
    for l in range(DEPTH):
        act, s = _layer_fwd(l, act, params[l])
        saved.append(s)
    dy, loss = _loss_head("loss_head", act, target)
    grads = [None] * DEPTH
    for l in reversed(range(DEPTH)):
        dy, grads[l] = _layer_bwd(l, dy, params[l], saved[l])
    return loss, dy, grads


ANY = pl.BlockSpec(memory_space=pl.ANY)


def _mesh_pos():
    return lax.axis_index("x"), lax.axis_index("y"), lax.axis_index("c")


def _other_chips(x, y):
    return [(1 - x, y), (x, 1 - y), (1 - x, 1 - y)]


def _comm_call(name, collective_id, peers, body, arrs, out_shape, sem_counts):
    n_in, n_out = len(arrs), len(out_shape)

    def seq_body(*refs):
        barrier = pltpu.get_barrier_semaphore()
        ps = peers(*_mesh_pos())
        for p in ps:
            pl.semaphore_signal(barrier, inc=1, device_id=p, device_id_type=MESH)
        pl.semaphore_wait(barrier, len(ps))
        body(refs[:n_in], refs[n_in:n_in + n_out], *refs[n_in + n_out:])

    return pl.kernel(
        seq_body, out_type=tuple(out_shape), mesh=plsc.ScalarSubcoreMesh(axis_name="seq", num_cores=1), name=name,
        scratch_types=[pltpu.SemaphoreType.DMA((n,)) for n in sem_counts],
        compiler_params=pltpu.CompilerParams(collective_id=collective_id),
    )(*arrs)


def _all_gather(name, collective_id, shard):
    R, C = shard.shape

    def peers(x, y, c):
        return [(x, y, 1 - c)] + [(*chip, c) for chip in _other_chips(x, y)]

    def body(in_refs, out_refs, send_sems, recv_sems, local_sems):
        (x_ref,), (out_ref,) = in_refs, out_refs
        x, y, c = _mesh_pos()
        me, sibling = (x, y, c), (x, y, 1 - c)
        chips = _other_chips(x, y)

        def slot(px, py, pc):
            return out_ref.at[4 * px + 2 * py + pc]

        def copy(k, block, to, src=None):
            return pltpu.make_async_remote_copy(
                src_ref=slot(*block) if src is None else src, dst_ref=slot(*block),
                send_sem=send_sems.at[k], recv_sem=recv_sems.at[k], device_id=to, device_id_type=MESH)

        mine = pltpu.make_async_copy(x_ref, slot(*me), local_sems.at[0])
        mine.start()
        first = [copy(0, me, sibling, src=x_ref)]
        first += [copy(1 + j, me, (*chip, c), src=x_ref) for j, chip in enumerate(chips)]
        for cp in first:
            cp.start()
        passed = [copy(4 + j, (*chip, c), sibling) for j, chip in enumerate(chips)]
        for j, chip in enumerate(chips):
            copy(1 + j, (*chip, c), me).wait_recv()
            passed[j].start()
        copy(0, sibling, me).wait_recv()
        for j, chip in enumerate(chips):
            copy(4 + j, (*chip, 1 - c), me).wait_recv()
        for cp in first + passed:
            cp.wait_send()
        mine.wait()

    return _comm_call(name, collective_id, peers, body, [shard],
                      [jax.ShapeDtypeStruct((N_DEV, R, C), shard.dtype)], (7, 7, 1))[0]


def _rs_sibling_exchange(name, collective_id, arrs):
    n = len(arrs)

    def body(in_refs, out_refs, send_sems, recv_sems):
        x, y, c = _mesh_pos()
        cps = []
        for k, (src, dst) in enumerate(zip(in_refs, out_refs)):
            src = src.at[:, 1 - c] if len(src.shape) == 4 else src
            cps.append(pltpu.make_async_remote_copy(src_ref=src, dst_ref=dst, send_sem=send_sems.at[k],
                                                    recv_sem=recv_sems.at[k], device_id=(x, y, 1 - c),
                                                    device_id_type=MESH))
        for cp in cps:
            cp.start()
        for cp in cps:
            cp.wait()

    out_shape = [jax.ShapeDtypeStruct(a.shape[:1] + a.shape[2:] if a.ndim == 4 else a.shape, a.dtype) for a in arrs]
    return _comm_call(name, collective_id, lambda x, y, c: [(x, y, 1 - c)], body, arrs, out_shape, (n, n))


def _rs_rows(r):
    return r // 2 if (r // 2) % 16 == 0 else r


def _rs_chip_sum(name, ids, big, rbig):
    _, _, R, C = big.shape
    rows = _rs_rows(R)

    def body(ids_ref, mine_ref, theirs_ref, t16_ref, own_ref):
        p = pl.program_id(1)
        s = mine_ref[...] + theirs_ref[...]
        t16_ref[...] = s.astype(BF16)

        @pl.when(p == ids_ref[1])
        def _():
            own_ref[...] = s

    grid_spec = pltpu.PrefetchScalarGridSpec(
        num_scalar_prefetch=1, grid=(R // rows, 4),
        in_specs=[pl.BlockSpec((None, None, rows, C), lambda i, p, ids: (p, ids[0], i, 0)),
                  pl.BlockSpec((None, rows, C), lambda i, p, ids: (p, i, 0))],
        out_specs=(pl.BlockSpec((None, rows, C), lambda i, p, ids: (p, i, 0)),
                   pl.BlockSpec((rows, C), lambda i, p, ids: (i, 0))))
    return pl.pallas_call(
        body, name=name, grid_spec=grid_spec,
        out_shape=(jax.ShapeDtypeStruct((4, R, C), BF16), jax.ShapeDtypeStruct((R, C), F32)),
        compiler_params=_cparams(("parallel", "arbitrary")),
    )(ids, big, rbig)


def _add2(name, a, b):
    def body(a_ref, b_ref, o_ref):
        o_ref[...] = a_ref[...] + b_ref[...]

    full = pl.BlockSpec(a.shape, lambda i: (0, 0))
    return pl.pallas_call(body, name=name, grid=(1,), in_specs=[full, full], out_specs=full,
                          out_shape=jax.ShapeDtypeStruct(a.shape, a.dtype))(a, b)


def _rs_chip_exchange(name, collective_id, arrs):
    n = len(arrs)

    def body(in_refs, out_refs, send_sems, recv_sems, local_sems):
        x, y, c = _mesh_pos()
        p_me = 2 * x + y
        chips = _other_chips(x, y)

        def part(ref, p):
            return ref.at[p] if len(ref.shape) == 3 else ref

        local = [pltpu.make_async_copy(part(src, p_me), dst.at[p_me], local_sems.at[k])
                 for k, (src, dst) in enumerate(zip(in_refs, out_refs))]
        for cp in local:
            cp.start()
        sends = []
        for j, (px, py) in enumerate(chips):
            for k, (src, dst) in enumerate(zip(in_refs, out_refs)):
                sends.append(pltpu.make_async_remote_copy(
                    src_ref=part(src, 2 * px + py), dst_ref=dst.at[p_me], send_sem=send_sems.at[n * j + k],
                    recv_sem=recv_sems.at[n * j + k], device_id=(px, py, c), device_id_type=MESH))
        for cp in sends:
            cp.start()
        for j, (px, py) in enumerate(chips):
            for k, (src, dst) in enumerate(zip(in_refs, out_refs)):
                pltpu.make_async_remote_copy(
                    src_ref=part(src, p_me), dst_ref=dst.at[2 * px + py], send_sem=send_sems.at[n * j + k],
                    recv_sem=recv_sems.at[n * j + k], device_id=(px, py, c), device_id_type=MESH).wait_recv()
        for cp in sends:
            cp.wait_send()
        for cp in local:
            cp.wait()

    out_shape = [jax.ShapeDtypeStruct((4,) + a.shape[-2:], a.dtype) for a in arrs]
    return _comm_call(name, collective_id, lambda x, y, c: [(*chip, c) for chip in _other_chips(x, y)], body, arrs,
                      out_shape, (3 * n, 3 * n, n))


def _rs_final_sum(name, ids, recv, own, rows):
    _, R, C = recv.shape
    assert R % rows == 0

    def body(ids_ref, r_ref, own_ref, o_ref):
        acc = None
        for p in range(4):
            term = jnp.where(ids_ref[1] == p, own_ref[...], r_ref[p].astype(F32))
            acc = term if acc is None else acc + term
        o_ref[...] = acc

    grid_spec = pltpu.PrefetchScalarGridSpec(
        num_scalar_prefetch=1, grid=(R // rows,),
        in_specs=[pl.BlockSpec((4, rows, C), lambda i, ids: (0, i, 0)), pl.BlockSpec((rows, C), lambda i, ids: (i, 0))],
        out_specs=pl.BlockSpec((rows, C), lambda i, ids: (i, 0)))
    return pl.pallas_call(
        body, name=name, grid_spec=grid_spec, out_shape=jax.ShapeDtypeStruct((R, C), F32),
        compiler_params=_cparams(("parallel",)),
    )(ids, recv, own)


def _reduce_scatter(tag, collective_id, bigs, small=None):
    x, y, c = _mesh_pos()
    ids = jnp.stack([c, 2 * x + y]).astype(jnp.int32)
    bigs = [b.reshape((4, 2) + b.shape[1:]) for b in bigs]
    smalls = [] if small is None else [small]
    got = _rs_sibling_exchange(tag + "_sibling", collective_id, bigs + smalls)
    t16s, owns = [], []
    for k, (b, rb) in enumerate(zip(bigs, got)):
        t16, own = _rs_chip_sum(tag + f"_chip_sum{k}", ids, b, rb)
        t16s.append(t16)
        owns.append(own)
    tss = [_add2(tag + "_chip_sum_small", small, got[-1])] if smalls else []
    recv = _rs_chip_exchange(tag + "_chips", collective_id + 1, t16s + tss)
    outs = [_rs_final_sum(tag + f"_final{k}", ids, r, own, _rs_rows(own.shape[0]))
            for k, (r, own) in enumerate(zip(recv, owns))]
    out_small = _rs_final_sum(tag + "_final_small", ids, recv[-1], tss[0], small.shape[0]) if smalls else None
    return outs, out_small


def _adamw(name, w, g, m, v):
    R, C = w.shape
    tr = max(t for t in range(8, 513, 8) if R % t == 0)
    c1 = 1.0 - ADAM_B1 ** ADAM_STEP
    c2 = 1.0 - ADAM_B2 ** ADAM_STEP

    def body(w_ref, g_ref, m_ref, v_ref, d_ref, nm_ref, nv_ref):
        gv = g_ref[...]
        nm = ADAM_B1 * m_ref[...] + (1.0 - ADAM_B1) * gv
        nv = ADAM_B2 * v_ref[...] + (1.0 - ADAM_B2) * (gv * gv)
        d_ref[...] = -ADAM_LR * ((nm / c1) / (jnp.sqrt(nv / c2) + ADAM_EPS) + ADAM_WD * w_ref[...])
        nm_ref[...] = nm
        nv_ref[...] = nv

    blk = pl.BlockSpec((tr, C), lambda i: (i, 0))
    shp = jax.ShapeDtypeStruct((R, C), F32)
    return pl.pallas_call(
        body, name=name, grid=(R // tr,), in_specs=[blk] * 4, out_specs=(blk,) * 3, out_shape=(shp,) * 3,
        compiler_params=_cparams(("parallel",)),
    )(w, g, m, v)


REST_FF1 = 0
REST_FF2 = 512
REST_WO = 1024
REST_UP = 1152
REST_ATTN = 1280
REST_ROWS = 1312
IN_ROWS = IN_COLS // N_DEV
CONV_BITS_ROWS = 16
SHARD_ROWS = IN_ROWS + REST_ROWS + CONV_BITS_ROWS
SMALL = (("norm_mix", (DEPTH, D)), ("b_gate", (DEPTH, 3 * D)), ("pool_mix", (DEPTH, 4, POOL_G, POOL_G)),
         ("pool_scale", (DEPTH, POOL_W)), ("conv_w", (DEPTH, 3, CONV_W)), ("q_gain", (DEPTH, HEAD)),
         ("k_gain", (DEPTH, HEAD)), ("norm_mlp", (DEPTH, D)))


def _pack_weight_shards(w, l):
    b = lambda a: a.astype(BF16)
    conv = lax.bitcast_convert_type(w["conv_w"][l], BF16).reshape(3, 128)
    conv = jnp.pad(conv, ((0, CONV_BITS_ROWS - 3), (0, D - 128)))
    rest = jnp.concatenate([
        b(w["w_ff1"][l].T), b(w["w_ff2"][l]), b(w["w_o"][l]),
        jnp.concatenate([b(w["w_pool_up"][l].T), b(w["w_conv_out"][l].T)], axis=1),
        b(w["w_attn_up"][l].T).reshape(REST_ROWS - REST_ATTN, D), conv], axis=0)
    return b(w["w_in"][l].T), rest


def _unpack_gathered(g_in, g_rest, small_w, l):
    rest = g_rest[:, :REST_ROWS]
    take = lambda r0, rows, c0=0, cols=D: rest[:, r0:r0 + rows, c0:c0 + cols].reshape(N_DEV * rows, cols)
    conv = g_rest[:, REST_ROWS:REST_ROWS + 3, :128].reshape(N_DEV, 3, CONV_W // N_DEV, 2)
    conv = lax.bitcast_convert_type(conv, F32)
    p = {
        "w_in_t": g_in.reshape(IN_COLS, D),
        "w_ff1_t": take(REST_FF1, 512), "w_ff2": take(REST_FF2, 512), "w_o": take(REST_WO, 128),
        "w_pool_up_t": take(REST_UP, 128, 0, POOL_W), "w_conv_out_t": take(REST_UP, 128, POOL_W, CONV_W),
        "w_attn_up_t": rest[:, REST_ATTN:].reshape(D, ATTN_OUT),
        "conv_w": jnp.transpose(conv, (1, 0, 2)).reshape(3, CONV_W),
        "pool_mix": small_w["pool_mix"][l],
    }
    for name in ("norm_mix", "b_gate", "pool_scale", "q_gain", "k_gain", "norm_mlp"):
        p[name] = small_w[name][l][None, :]
    return p


def _pack_small_grads(grads):
    flat = jnp.concatenate([jnp.stack([grads[l][name] for l in range(DEPTH)]).reshape(-1) for name, _ in SMALL])
    return jnp.pad(flat, (0, (-flat.shape[0]) % (8 * 128))).reshape(-1, 128)


def _unpack_grads(in_t, rest, small, dev):
    out = {
        "w_in": jnp.stack([a.T for a in in_t]),
        "w_ff1": jnp.stack([a[REST_FF1:REST_FF1 + 512].T for a in rest]),
        "w_ff2": jnp.stack([a[REST_FF2:REST_FF2 + 512] for a in rest]),
        "w_o": jnp.stack([a[REST_WO:REST_WO + 128] for a in rest]),
        "w_pool_up": jnp.stack([a[REST_UP:REST_UP + 128, :POOL_W].T for a in rest]),
        "w_conv_out": jnp.stack([a[REST_UP:REST_UP + 128, POOL_W:].T for a in rest]),
        "w_attn_up": jnp.stack([a[REST_ATTN:].reshape(D // N_DEV, ATTN_OUT).T for a in rest]),
    }
    flat = small.reshape(-1)
    off = 0
    for name, shp in SMALL:
        n = 1
        for s in shp:
            n *= s
        out[name] = flat[off:off + n].reshape(shp)
        off += n
    width = CONV_W // N_DEV
    out["conv_w"] = lax.dynamic_slice_in_dim(out["conv_w"], dev * width, width, axis=2)
    return out


def _pack_small(arrs):
    flat = jnp.concatenate([a.reshape(-1) for a in arrs])
    pad = (-flat.shape[0]) % (8 * 128)
    return jnp.pad(flat, (0, pad)).reshape(-1, 128)


def _unpack_small(packed, like):
    flat = packed.reshape(-1)
    out, off = [], 0
    for a in like:
        out.append(flat[off:off + a.size].reshape(a.shape))
        off += a.size
    return out


WEIGHTS = ("norm_mix", "w_in", "b_gate", "pool_mix", "pool_scale", "conv_w", "q_gain", "k_gain", "w_pool_up",
           "w_conv_out", "w_attn_up", "w_o", "norm_mlp", "w_ff1", "w_ff2")


def kernel(x, norm_mix, w_in, b_gate, pool_mix, pool_scale, conv_w, q_gain, k_gain, w_pool_up, w_conv_out, w_attn_up, w_o, norm_mlp, w_ff1, w_ff2, loss_target, m_norm_mix, m_w_in, m_b_gate, m_pool_mix, m_pool_scale, m_conv_w, m_q_gain, m_k_gain, m_w_pool_up, m_w_conv_out, m_w_attn_up, m_w_o, m_norm_mlp, m_w_ff1, m_w_ff2, v_norm_mix, v_w_in, v_b_gate, v_pool_mix, v_pool_scale, v_conv_w, v_q_gain, v_k_gain, v_w_pool_up, v_w_conv_out, v_w_attn_up, v_w_o, v_norm_mlp, v_w_ff1, v_w_ff2):
    w = dict(zip(WEIGHTS, (norm_mix, w_in, b_gate, pool_mix, pool_scale, conv_w, q_gain, k_gain, w_pool_up,
                           w_conv_out, w_attn_up, w_o, norm_mlp, w_ff1, w_ff2)))
    m = dict(zip(WEIGHTS, (m_norm_mix, m_w_in, m_b_gate, m_pool_mix, m_pool_scale, m_conv_w, m_q_gain, m_k_gain,
                           m_w_pool_up, m_w_conv_out, m_w_attn_up, m_w_o, m_norm_mlp, m_w_ff1, m_w_ff2)))
    v = dict(zip(WEIGHTS, (v_norm_mix, v_w_in, v_b_gate, v_pool_mix, v_pool_scale, v_conv_w, v_q_gain, v_k_gain,
                           v_w_pool_up, v_w_conv_out, v_w_attn_up, v_w_o, v_norm_mlp, v_w_ff1, v_w_ff2)))
    xi, yi, ci = _mesh_pos()
    dev = 4 * xi + 2 * yi + ci

    params = []
    for l in range(DEPTH):
        s_in, s_rest = _pack_weight_shards(w, l)
        g_in = _all_gather(f"gather_in{l}", 1 + 2 * l, s_in)
        g_rest = _all_gather(f"gather_rest{l}", 2 + 2 * l, s_rest)
        params.append(_unpack_gathered(g_in, g_rest, w, l))
    loss, dx, grads = _local_step(x[0], loss_target[0], params)
    in_t, rest, g_small = [None] * DEPTH, [None] * DEPTH, None
    for l in reversed(range(DEPTH)):
        (rest[l],), _ = _reduce_scatter(f"rs_rest{l}", 5 + 4 * l, [grads[l]["rest"]])
        (in_t[l],), small = _reduce_scatter(f"rs_in{l}", 7 + 4 * l, [grads[l]["in_t"].reshape(N_DEV, IN_ROWS, D)],
                                            _pack_small_grads(grads) if l == 0 else None)
        g_small = small if l == 0 else g_small
    g = _unpack_grads(in_t, rest, g_small, dev)

    delta, new_m, new_v = {}, {}, {}
    for name in ("w_in", "w_pool_up", "w_conv_out", "w_attn_up", "w_o", "w_ff1", "w_ff2"):
        shp = w[name].shape
        two_d = (shp[0] * shp[1], shp[2])
        d_, m_, v_ = _adamw("adamw_" + name, w[name].reshape(two_d), g[name].reshape(two_d),
                            m[name].reshape(two_d), v[name].reshape(two_d))
        delta[name], new_m[name], new_v[name] = d_.reshape(shp), m_.reshape(shp), v_.reshape(shp)
    small_names = [name for name, _ in SMALL]
    packed = [_pack_small([src[name] for name in small_names]) for src in (w, g, m, v)]
    outs = _adamw("adamw_small", *packed)
    for dst, arr in zip((delta, new_m, new_v), outs):
        for name, val in zip(small_names, _unpack_small(arr, [w[name] for name in small_names])):
            dst[name] = val

    total = lax.psum(loss[0, 0], ("x", "y", "c"))
    return (total, dx[None], *[g[n] for n in WEIGHTS], *[delta[n] for n in WEIGHTS],
            *[new_m[n] for n in WEIGHTS], *[new_v[n] for n in WEIGHTS])
```

```python
import functools

import jax
import jax.numpy as jnp
from jax import lax
from jax.experimental import pallas as pl
from jax.experimental.pallas import tpu as pltpu
from jax.experimental.pallas import tpu_sc as plsc

F32 = jnp.float32
BF16 = jnp.bfloat16
MESH = pl.DeviceIdType.MESH

D = 1024
SEQ = 4096
DEPTH = 2
N_DEV = 8
POOL_WINDOWS = (2, 4, 8, 16)
POOL_W = 512
POOL_G = 128
CONV_W = 512
HEAD = 64
ATTN_GROUPS = ((128, 1), (512, 4), (2048, 16))
HPG = 4
ATTN_W = 768
ATTN_OUT = 256
ATTN_BLK = 128
D_FF = 4096
EPS = 1e-6
MASK = -1e30
OFF_POOL = 0
OFF_CB = 512
OFF_CC = 1024
OFF_CX = 1536
OFF_Q = 2048
OFF_K = 2816
OFF_V = 3584
OFF_GATE = 4352
IN_COLS = 7424
ADAM_LR = 0.001
ADAM_B1 = 0.9
ADAM_B2 = 0.999
ADAM_EPS = 1e-08
ADAM_WD = 0.01
ADAM_STEP = 10

ROW_TILE = 512
SEQ_CHUNK = 256
HALO = 16
VMEM_LIMIT = 56 * 1024 * 1024


def _cparams(sem=None):
    return pltpu.CompilerParams(dimension_semantics=sem, vmem_limit_bytes=VMEM_LIMIT)


def _call(body, args, in_specs, after=None, **kw):
    if after is not None:
        n, inner = len(args), body

        def body(*refs):
            return inner(*refs[:n], *refs[n + 1:])

        args = list(args) + [after]
        in_specs = list(in_specs) + [pl.BlockSpec(memory_space=pl.ANY)]
    return pl.pallas_call(body, in_specs=in_specs, **kw)(*args)


_DIMS = {"nn": (((1,), (0,)), ((), ())), "nt": (((1,), (1,)), ((), ())), "tn": (((0,), (0,)), ((), ()))}


def _matmul(name, a, b, mode, tm, tn, tk, out_dtypes=(F32,), extras=(), epilogue=None, into=None, after=None):
    if mode == "tn":
        K, M = a.shape
    else:
        M, K = a.shape
    N = b.shape[0] if mode == "nt" else b.shape[1]
    assert M % tm == 0 and N % tn == 0 and K % tk == 0, (name, M, N, K, tm, tn, tk)
    nk = K // tk
    n_extra = len(extras)
    n_out = len(out_dtypes)
    dims = _DIMS[mode]
    n_alias = 0 if into is None or isinstance(into[0], jax.ShapeDtypeStruct) else 1

    def body(a_ref, b_ref, *rest):
        extra_refs = rest[:n_extra]
        out_refs = rest[n_extra + n_alias:n_extra + n_alias + n_out]

        def finish(acc):
            if epilogue is None:
                res = (acc,)
            else:
                res = epilogue(acc, *[r[...] for r in extra_refs])
            for o_ref, v in zip(out_refs, res):
                o_ref[...] = v.astype(o_ref.dtype)

        part = lax.dot_general(a_ref[...].astype(BF16), b_ref[...].astype(BF16), dims,
                               preferred_element_type=F32)
        if nk == 1:
            finish(part)
        else:
            acc_ref = rest[-1]
            k = pl.program_id(2)

            @pl.when(k == 0)
            def _():
                acc_ref[...] = part

            @pl.when(k > 0)
            def _():
                acc_ref[...] += part

            @pl.when(k == nk - 1)
            def _():
                finish(acc_ref[...])

    if mode == "tn":
        a_spec = pl.BlockSpec((tk, tm), lambda i, j, k: (k, i))
    else:
        a_spec = pl.BlockSpec((tm, tk), lambda i, j, k: (i, k))
    if mode == "nt":
        b_spec = pl.BlockSpec((tn, tk), lambda i, j, k: (j, k))
    else:
        b_spec = pl.BlockSpec((tk, tn), lambda i, j, k: (k, j))
    in_specs = [a_spec, b_spec]
    args = [a, b]
    for arr, kind in extras:
        if kind == "mn":
            in_specs.append(pl.BlockSpec((tm, tn), lambda i, j, k: (i, j)))
        else:
            in_specs.append(pl.BlockSpec((1, tn), lambda i, j, k: (0, j)))
        args.append(arr)
    out_shape = tuple(jax.ShapeDtypeStruct((M, N), dt) for dt in out_dtypes)
    out_specs = tuple(pl.BlockSpec((tm, tn), lambda i, j, k: (i, j)) for _ in out_dtypes)
    aliases = {}
    if into is not None:
        buf, row0, col0 = into
        assert n_out == 1 and (M // N_DEV) % tm == 0 and row0 % tm == 0 and col0 % tn == 0
        per_dev = M // N_DEV // tm
        out_shape = (jax.ShapeDtypeStruct(buf.shape, buf.dtype),)
        out_specs = (pl.BlockSpec((None, tm, tn), lambda i, j, k: (i // per_dev, row0 // tm + i % per_dev,
                                                                   col0 // tn + j)),)
        if not isinstance(buf, jax.ShapeDtypeStruct):
            aliases = {len(args): 0}
            in_specs.append(pl.BlockSpec(memory_space=pl.ANY))
            args.append(buf)
    scratch = [] if nk == 1 else [pltpu.VMEM((tm, tn), F32)]
    res = _call(
        body, args, in_specs, after, name=name, grid=(M // tm, N // tn, nk), out_specs=out_specs,
        out_shape=out_shape, scratch_shapes=scratch, input_output_aliases=aliases,
        compiler_params=_cparams(("parallel", "parallel", "arbitrary")))
    return res if n_out > 1 else res[0]


def _rmsnorm_fwd(name, x, gain):
    S_, D_ = x.shape

    def body(x_ref, g_ref, h_ref):
        xf = x_ref[...]
        ms = jnp.mean(xf * xf, axis=-1, keepdims=True)
        h_ref[...] = (xf * lax.rsqrt(ms + EPS) * g_ref[...]).astype(BF16)

    return pl.pallas_call(
        body, name=name, grid=(S_ // ROW_TILE,),
        in_specs=[pl.BlockSpec((ROW_TILE, D_), lambda i: (i, 0)), pl.BlockSpec((1, D_), lambda i: (0, 0))],
        out_specs=pl.BlockSpec((ROW_TILE, D_), lambda i: (i, 0)),
        out_shape=jax.ShapeDtypeStruct((S_, D_), BF16),
        compiler_params=_cparams(("parallel",)),
    )(x, gain)


def _rmsnorm_bwd(name, x, gain, dh, dres, after=None):
    S_, D_ = x.shape
    n = S_ // ROW_TILE

    def body(x_ref, g_ref, dh_ref, dres_ref, dx_ref, dg_ref, acc_ref):
        i = pl.program_id(0)
        xf = x_ref[...]
        rstd = lax.rsqrt(jnp.mean(xf * xf, axis=-1, keepdims=True) + EPS)
        xhat = xf * rstd
        dhv = dh_ref[...]
        dxhat = dhv * g_ref[...]
        c = jnp.mean(dxhat * xhat, axis=-1, keepdims=True)
        dx_ref[...] = dres_ref[...] + rstd * (dxhat - xhat * c)
        part = jnp.sum((dhv * xhat).reshape(ROW_TILE // 8, 8, D_), axis=0)

        @pl.when(i == 0)
        def _():
            acc_ref[...] = part

        @pl.when(i > 0)
        def _():
            acc_ref[...] += part

        @pl.when(i == n - 1)
        def _():
            dg_ref[...] = jnp.sum(acc_ref[...], axis=0, keepdims=True)

    row = pl.BlockSpec((ROW_TILE, D_), lambda i: (i, 0))
    vec = pl.BlockSpec((1, D_), lambda i: (0, 0))
    return _call(
        body, [x, gain, dh, dres], [row, vec, row, row], after, name=name, grid=(n,), out_specs=(row, vec),
        out_shape=(jax.ShapeDtypeStruct((S_, D_), F32), jax.ShapeDtypeStruct((1, D_), F32)),
        scratch_shapes=[pltpu.VMEM((8, D_), F32)],
        compiler_params=_cparams(("arbitrary",)))


def _loss_head(name, y, target):
    S_, D_ = y.shape
    n = S_ // ROW_TILE

    def body(y_ref, t_ref, dy_ref, l_ref, acc_ref):
        i = pl.program_id(0)
        e = y_ref[...] - t_ref[...]
        dy_ref[...] = e * (1.0 / D_)
        part = jnp.sum((e * e).reshape(ROW_TILE // 8, 8, D_), axis=0)

        @pl.when(i == 0)
        def _():
            acc_ref[...] = part

        @pl.when(i > 0)
        def _():
            acc_ref[...] += part

        @pl.when(i == n - 1)
        def _():
            s = jnp.sum(acc_ref[...], axis=1, keepdims=True)
            l_ref[...] = jnp.sum(s, axis=0, keepdims=True) * (0.5 / D_)

    row = pl.BlockSpec((ROW_TILE, D_), lambda i: (i, 0))
    return pl.pallas_call(
        body, name=name, grid=(n,), in_specs=[row, row],
        out_specs=(row, pl.BlockSpec((1, 1), lambda i: (0, 0))),
        out_shape=(jax.ShapeDtypeStruct((S_, D_), F32), jax.ShapeDtypeStruct((1, 1), F32)),
        scratch_shapes=[pltpu.VMEM((8, D_), F32)],
        compiler_params=_cparams(("arbitrary",)),
    )(y, target)


def _rows_with_halo(ref, cols, i, n_chunks, before, after):
    r0 = pl.multiple_of(i * SEQ_CHUNK, SEQ_CHUNK)
    parts = []
    if before:
        h0 = pl.multiple_of(jnp.maximum(r0 - HALO, 0), 8)
        halo = ref[pl.ds(h0, HALO), cols]
        parts.append(jnp.where(i > 0, halo, jnp.zeros_like(halo)))
    parts.append(ref[pl.ds(r0, SEQ_CHUNK), cols])
    if after:
        a0 = pl.multiple_of(jnp.minimum(r0 + SEQ_CHUNK, (n_chunks - 1) * SEQ_CHUNK + SEQ_CHUNK - HALO), 8)
        halo = ref[pl.ds(a0, HALO), cols]
        parts.append(jnp.where(i < n_chunks - 1, halo, jnp.zeros_like(halo)))
    return parts[0] if len(parts) == 1 else jnp.concatenate(parts, axis=0)


def _shift_down(v, k):
    return pltpu.roll(v, k, 0)


def _shift_up(v, k):
    return pltpu.roll(v, v.shape[0] - k, 0)


def _pool_diff(xx, w, t_main):
    s = xx
    k = 1
    while k < w:
        s = s + _shift_down(s, k)
        k *= 2
    cnt = jnp.minimum(t_main + 1, w).astype(F32)
    return s[HALO:] / cnt - xx[HALO:]


def _pool_fwd(name, z, pool_mix, pool_scale):
    S_ = z.shape[0]
    n_chunks = S_ // SEQ_CHUNK

    def body(u_ref, mix_ref, sc_ref, y_ref):
        for g, w in enumerate(POOL_WINDOWS):
            cols = slice(g * POOL_G, (g + 1) * POOL_G)
            mixg = mix_ref[g].astype(BF16)
            scg = sc_ref[:, cols]

            def chunk(i, carry, cols=cols, mixg=mixg, scg=scg, w=w):
                r0 = pl.multiple_of(i * SEQ_CHUNK, SEQ_CHUNK)
                xx = _rows_with_halo(u_ref, cols, i, n_chunks, True, False)
                t = r0 + lax.broadcasted_iota(jnp.int32, (SEQ_CHUNK, POOL_G), 0)
                d = _pool_diff(xx, w, t)
                y = jnp.dot(d.astype(BF16), mixg, preferred_element_type=F32) * scg
                y_ref[pl.ds(r0, SEQ_CHUNK), cols] = y.astype(BF16)
                return carry

            lax.fori_loop(0, n_chunks, chunk, 0)

    return pl.pallas_call(
        body, name=name, grid=(1,),
        in_specs=[pl.BlockSpec((S_, POOL_W), lambda i: (0, 0)),
                  pl.BlockSpec((4, POOL_G, POOL_G), lambda i: (0, 0, 0)),
                  pl.BlockSpec((1, POOL_W), lambda i: (0, 0))],
        out_specs=pl.BlockSpec((S_, POOL_W), lambda i: (0, 0)),
        out_shape=jax.ShapeDtypeStruct((S_, POOL_W), BF16),
        compiler_params=_cparams(("arbitrary",)),
    )(z, pool_mix, pool_scale)


def _pool_bwd(name, z, dy, pool_mix, pool_scale, after=None):
    S_ = z.shape[0]
    n_chunks = S_ // SEQ_CHUNK
    rows_a = SEQ_CHUNK + HALO

    def body(u_ref, dy_ref, mix_ref, sc_ref, du_ref, dmix_ref, dsc_ref):
        for g, w in enumerate(POOL_WINDOWS):
            cols = slice(g * POOL_G, (g + 1) * POOL_G)
            mixg = mix_ref[g].astype(BF16)
            scg = sc_ref[:, cols]

            def chunk(i, carry, cols=cols, mixg=mixg, scg=scg, w=w):
                dmix_acc, dsc_acc = carry
                r0 = pl.multiple_of(i * SEQ_CHUNK, SEQ_CHUNK)
                xx = _rows_with_halo(u_ref, cols, i, n_chunks, True, False)
                t = r0 + lax.broadcasted_iota(jnp.int32, (SEQ_CHUNK, POOL_G), 0)
                d = _pool_diff(xx, w, t).astype(BF16)
                ypre = jnp.dot(d, mixg, preferred_element_type=F32)
                dyy = _rows_with_halo(dy_ref, cols, i, n_chunks, False, True)
                dys = (dyy * scg).astype(BF16)
                dsc_acc = dsc_acc + jnp.sum((dyy[:SEQ_CHUNK] * ypre).reshape(SEQ_CHUNK // 8, 8, POOL_G), axis=0)
                dmix_acc = dmix_acc + lax.dot_general(d, dys[:SEQ_CHUNK], _DIMS["tn"], preferred_element_type=F32)
                dd = lax.dot_general(dys, mixg, _DIMS["nt"], preferred_element_type=F32)
                ta = r0 + lax.broadcasted_iota(jnp.int32, (rows_a, POOL_G), 0)
                f = dd / jnp.minimum(ta + 1, w).astype(F32)
                k = 1
                while k < w:
                    f = f + _shift_up(f, k)
                    k *= 2
                du_ref[pl.ds(r0, SEQ_CHUNK), cols] = (f[:SEQ_CHUNK] - dd[:SEQ_CHUNK]).astype(BF16)
                return dmix_acc, dsc_acc

            dmix_acc, dsc_acc = lax.fori_loop(
                0, n_chunks, chunk, (jnp.zeros((POOL_G, POOL_G), F32), jnp.zeros((8, POOL_G), F32)))
            dmix_ref[g] = dmix_acc
            dsc_ref[:, cols] = jnp.sum(dsc_acc, axis=0, keepdims=True)

    full = pl.BlockSpec((S_, POOL_W), lambda i: (0, 0))
    mix_spec = pl.BlockSpec((4, POOL_G, POOL_G), lambda i: (0, 0, 0))
    vec = pl.BlockSpec((1, POOL_W), lambda i: (0, 0))
    return _call(
        body, [z, dy, pool_mix, pool_scale], [full, full, mix_spec, vec], after, name=name, grid=(1,),
        out_specs=(full, mix_spec, vec),
        out_shape=(jax.ShapeDtypeStruct((S_, POOL_W), BF16), jax.ShapeDtypeStruct((4, POOL_G, POOL_G), F32),
                   jax.ShapeDtypeStruct((1, POOL_W), F32)),
        compiler_params=_cparams(("arbitrary",)))


def _conv_specs(S_):
    slab = lambda off: pl.BlockSpec((S_, 128), lambda c, off=off: (0, off // 128 + c))
    return slab(OFF_CB), slab(OFF_CC), slab(OFF_CX)


def _conv_fwd(name, z, conv_w):
    S_ = z.shape[0]
    n_chunks = S_ // SEQ_CHUNK
    col = slice(0, 128)

    def body(b_ref, c_ref, x_ref, w_ref, y_ref):
        w0, w1, w2 = w_ref[0:1, :], w_ref[1:2, :], w_ref[2:3, :]

        def chunk(i, carry):
            r0 = pl.multiple_of(i * SEQ_CHUNK, SEQ_CHUNK)
            u = _rows_with_halo(c_ref, col, i, n_chunks, True, False) * _rows_with_halo(x_ref, col, i, n_chunks, True, False)
            y = w2 * u + w1 * _shift_down(u, 1) + w0 * _shift_down(u, 2)
            y_ref[pl.ds(r0, SEQ_CHUNK), :] = (b_ref[pl.ds(r0, SEQ_CHUNK), :] * y[HALO:]).astype(BF16)
            return carry

        lax.fori_loop(0, n_chunks, chunk, 0)

    sb, sc, sx = _conv_specs(S_)
    return pl.pallas_call(
        body, name=name, grid=(CONV_W // 128,),
        in_specs=[sb, sc, sx, pl.BlockSpec((3, 128), lambda c: (0, c))],
        out_specs=pl.BlockSpec((S_, 128), lambda c: (0, c)),
        out_shape=jax.ShapeDtypeStruct((S_, CONV_W), BF16),
        compiler_params=_cparams(("parallel",)),
    )(z, z, z, conv_w)


def _conv_bwd(name, z, dy, conv_w, after=None):
    S_ = z.shape[0]
    n_chunks = S_ // SEQ_CHUNK
    col = slice(0, 128)
    lo, hi = HALO, HALO + SEQ_CHUNK

    def body(b_ref, c_ref, x_ref, dy_ref, w_ref, db_ref, dc_ref, dx_ref, dw_ref):
        w0, w1, w2 = w_ref[0:1, :], w_ref[1:2, :], w_ref[2:3, :]

        def chunk(i, carry):
            a0, a1, a2 = carry
            r0 = pl.multiple_of(i * SEQ_CHUNK, SEQ_CHUNK)
            cc = _rows_with_halo(c_ref, col, i, n_chunks, True, True)
            xx = _rows_with_halo(x_ref, col, i, n_chunks, True, True)
            bb = _rows_with_halo(b_ref, col, i, n_chunks, True, True)
            dyy = _rows_with_halo(dy_ref, col, i, n_chunks, True, True)
            u = cc * xx
            u1 = _shift_down(u, 1)
            u2 = _shift_down(u, 2)
            y = w2 * u + w1 * u1 + w0 * u2
            dyv = dyy * bb
            du = w2 * dyv + w1 * _shift_up(dyv, 1) + w0 * _shift_up(dyv, 2)
            db_ref[pl.ds(r0, SEQ_CHUNK), :] = (dyy[lo:hi] * y[lo:hi]).astype(BF16)
            dc_ref[pl.ds(r0, SEQ_CHUNK), :] = (du[lo:hi] * xx[lo:hi]).astype(BF16)
            dx_ref[pl.ds(r0, SEQ_CHUNK), :] = (du[lo:hi] * cc[lo:hi]).astype(BF16)
            red = lambda v: jnp.sum(v.reshape(SEQ_CHUNK // 8, 8, 128), axis=0)
            dm = dyv[lo:hi]
            return a0 + red(dm * u2[lo:hi]), a1 + red(dm * u1[lo:hi]), a2 + red(dm * u[lo:hi])

        zero = jnp.zeros((8, 128), F32)
        a0, a1, a2 = lax.fori_loop(0, n_chunks, chunk, (zero, zero, zero))
        dw_ref[0:1, :] = jnp.sum(a0, axis=0, keepdims=True)
        dw_ref[1:2, :] = jnp.sum(a1, axis=0, keepdims=True)
        dw_ref[2:3, :] = jnp.sum(a2, axis=0, keepdims=True)

    sb, sc, sx = _conv_specs(S_)
    slab = pl.BlockSpec((S_, 128), lambda c: (0, c))
    wspec = pl.BlockSpec((3, 128), lambda c: (0, c))
    act = jax.ShapeDtypeStruct((S_, CONV_W), BF16)
    return _call(
        body, [z, z, z, dy, conv_w], [sb, sc, sx, slab, wspec], after, name=name, grid=(CONV_W // 128,),
        out_specs=(slab, slab, slab, wspec),
        out_shape=(act, act, act, jax.ShapeDtypeStruct((3, CONV_W), F32)),
        compiler_params=_cparams(("parallel",)))


def _head_norm(v, gain):
    rstd = lax.rsqrt(jnp.mean(v * v, axis=-1, keepdims=True) + EPS)
    xhat = v * rstd
    return xhat * gain, xhat, rstd


def _head_norm_bwd(dy, xhat, rstd, gain):
    dxhat = dy * gain
    c = jnp.mean(dxhat * xhat, axis=-1, keepdims=True)
    dv = rstd * (dxhat - xhat * c)
    dg = jnp.sum((dy * xhat).reshape(ATTN_BLK // 8, 8, HEAD), axis=0)
    return dv, dg


def _band_masks():
    qi = lax.broadcasted_iota(jnp.int32, (ATTN_BLK, ATTN_BLK), 0)
    ki = lax.broadcasted_iota(jnp.int32, (ATTN_BLK, ATTN_BLK), 1)
    return ki <= qi, ki >= qi


def _attn_specs(S_, g, dil):
    rows = ATTN_BLK * dil
    nb = S_ // rows
    cq, ck, cv = (OFF_Q // PAIR_W + g * PAIRS, OFF_K // PAIR_W + g * PAIRS, OFF_V // PAIR_W + g * PAIRS)
    return rows, nb, cq, ck, cv


PAIR_W = 128
PAIR = PAIR_W // HEAD
PAIRS = HPG // PAIR


def _rows_of(ref, r, dil):
    if dil == 1:
        return ref[...]
    return ref[pl.ds(r, ATTN_BLK, stride=dil), :]


def _put_rows(ref, r, dil, val):
    if dil == 1:
        ref[...] = val.astype(ref.dtype)
    else:
        ref[pl.ds(r, ATTN_BLK, stride=dil), :] = val.astype(ref.dtype)


def _attn_fwd(name, z, q_gain, k_gain, g, dil):
    S_ = z.shape[0]
    rows, nb, cq, ck, cv = _attn_specs(S_, g, dil)
    scale = HEAD ** -0.5

    def body(q_ref, kc_ref, kp_ref, vc_ref, vp_ref, gq_ref, gk_ref, o_ref, l_ref):
        n = pl.program_id(1)
        m_cur, m_prev = _band_masks()
        m_prev = jnp.logical_and(m_prev, n > 0)
        gq, gk = gq_ref[...], gk_ref[...]
        for r in range(dil):
            q, kc, kp = _rows_of(q_ref, r, dil), _rows_of(kc_ref, r, dil), _rows_of(kp_ref, r, dil)
            vc, vp = _rows_of(vc_ref, r, dil), _rows_of(vp_ref, r, dil)
            outs, lses = [], []
            for h in range(PAIR):
                hs = slice(h * HEAD, (h + 1) * HEAD)
                qn = _head_norm(q[:, hs], gq)[0].astype(BF16)
                kcn = _head_norm(kc[:, hs], gk)[0].astype(BF16)
                kpn = _head_norm(kp[:, hs], gk)[0].astype(BF16)
                s_c = lax.dot_general(qn, kcn, _DIMS["nt"], preferred_element_type=F32) * scale
                s_p = lax.dot_general(qn, kpn, _DIMS["nt"], preferred_element_type=F32) * scale
                s_c = jnp.where(m_cur, s_c, MASK)
                s_p = jnp.where(m_prev, s_p, MASK)
                m = jnp.maximum(jnp.max(s_c, axis=-1, keepdims=True), jnp.max(s_p, axis=-1, keepdims=True))
                p_c = jnp.exp(s_c - m)
                p_p = jnp.exp(s_p - m)
                den = jnp.sum(p_c, axis=-1, keepdims=True) + jnp.sum(p_p, axis=-1, keepdims=True)
                acc = (jnp.dot(p_c.astype(BF16), vc[:, hs].astype(BF16), preferred_element_type=F32)
                       + jnp.dot(p_p.astype(BF16), vp[:, hs].astype(BF16), preferred_element_type=F32))
                outs.append(acc / den)
                lses.append(jnp.broadcast_to(m + jnp.log(den), (ATTN_BLK, HEAD)))
            _put_rows(o_ref, r, dil, jnp.concatenate(outs, axis=1))
            _put_rows(l_ref, r, dil, jnp.concatenate(lses, axis=1))

    cur = lambda c: pl.BlockSpec((rows, PAIR_W), lambda hp, n, c=c: (n, c + hp))
    prev = lambda c: pl.BlockSpec((rows, PAIR_W), lambda hp, n, c=c: (jnp.maximum(n - 1, 0), c + hp))
    gspec = pl.BlockSpec((1, HEAD), lambda hp, n: (0, 0))
    shp = jax.ShapeDtypeStruct((S_, ATTN_OUT), F32)
    return pl.pallas_call(
        body, name=name, grid=(PAIRS, nb),
        in_specs=[cur(cq), cur(ck), prev(ck), cur(cv), prev(cv), gspec, gspec],
        out_specs=(cur(0), cur(0)), out_shape=(shp, shp),
        compiler_params=_cparams(("parallel", "parallel")),
    )(z, z, z, z, z, q_gain, k_gain)


def _attn_combine(name, os_, ls_):
    S_ = os_[0].shape[0]

    def body(o0, o1, o2, l0, l1, l2, o_ref, l_ref):
        a, b, c = l0[...], l1[...], l2[...]
        m = jnp.maximum(jnp.maximum(a, b), c)
        ea, eb, ec = jnp.exp(a - m), jnp.exp(b - m), jnp.exp(c - m)
        zsum = ea + eb + ec
        o_ref[...] = (ea * o0[...] + eb * o1[...] + ec * o2[...]) / zsum
        l_ref[...] = m + jnp.log(zsum)

    row = pl.BlockSpec((ROW_TILE, ATTN_OUT), lambda i: (i, 0))
    shp = jax.ShapeDtypeStruct((S_, ATTN_OUT), F32)
    return pl.pallas_call(
        body, name=name, grid=(S_ // ROW_TILE,), in_specs=[row] * 6, out_specs=(row, row), out_shape=(shp, shp),
        compiler_params=_cparams(("parallel",)),
    )(*os_, *ls_)


def _attn_bwd(name, z, q_gain, k_gain, do, o, lse, g, dil, after=None):
    S_ = z.shape[0]
    rows, nb, cq, ck, cv = _attn_specs(S_, g, dil)
    scale = HEAD ** -0.5

    def body(q_ref, kc_ref, kp_ref, vc_ref, vp_ref, gq_ref, gk_ref, do_ref, o_ref, l_ref,
             dq_ref, dk_ref, dv_ref, dgq_ref, dgk_ref, ck_ref, cvv_ref, gq_acc, gk_acc):
        hp = pl.program_id(0)
        n = pl.program_id(1)
        live = n < nb
        m_cur, m_prev = _band_masks()
        m_cur = jnp.logical_and(m_cur, live)
        m_prev = jnp.logical_and(m_prev, jnp.logical_and(n > 0, live))
        gq, gk = gq_ref[...], gk_ref[...]

        @pl.when(n == 0)
        def _():
            ck_ref[...] = jnp.zeros_like(ck_ref)
            cvv_ref[...] = jnp.zeros_like(cvv_ref)

        @pl.when(jnp.logical_and(n == 0, hp == 0))
        def _():
            gq_acc[...] = jnp.zeros_like(gq_acc)
            gk_acc[...] = jnp.zeros_like(gk_acc)

        for r in range(dil):
            q, kc, kp = _rows_of(q_ref, r, dil), _rows_of(kc_ref, r, dil), _rows_of(kp_ref, r, dil)
            vc, vp = _rows_of(vc_ref, r, dil), _rows_of(vp_ref, r, dil)
            dov, ov, lv = _rows_of(do_ref, r, dil), _rows_of(o_ref, r, dil), _rows_of(l_ref, r, dil)
            carry_k, carry_v = _rows_of(ck_ref, r, dil), _rows_of(cvv_ref, r, dil)
            dqs, dks, dvs, nck, ncv = [], [], [], [], []
            dgq = jnp.zeros((8, HEAD), F32)
            dgk = jnp.zeros((8, HEAD), F32)
            for h in range(PAIR):
                hs = slice(h * HEAD, (h + 1) * HEAD)
                qn, q_hat, q_rstd = _head_norm(q[:, hs], gq)
                kcn = _head_norm(kc[:, hs], gk)[0].astype(BF16)
                kpn, kp_hat, kp_rstd = _head_norm(kp[:, hs], gk)
                qn, kpn = qn.astype(BF16), kpn.astype(BF16)
                do_h = dov[:, hs]
                do_b = do_h.astype(BF16)
                delta = jnp.sum(do_h * ov[:, hs], axis=-1, keepdims=True)
                l_h = lv[:, h * HEAD:h * HEAD + 1]
                s_c = lax.dot_general(qn, kcn, _DIMS["nt"], preferred_element_type=F32) * scale
                s_p = lax.dot_general(qn, kpn, _DIMS["nt"], preferred_element_type=F32) * scale
                p_c = jnp.where(m_cur, jnp.exp(jnp.where(m_cur, s_c, MASK) - l_h), 0.0)
                p_p = jnp.where(m_prev, jnp.exp(jnp.where(m_prev, s_p, MASK) - l_h), 0.0)
                dp_c = lax.dot_general(do_b, vc[:, hs].astype(BF16), _DIMS["nt"], preferred_element_type=F32)
                dp_p = lax.dot_general(do_b, vp[:, hs].astype(BF16), _DIMS["nt"], preferred_element_type=F32)
                ds_c = (p_c * (dp_c - delta) * scale).astype(BF16)
                ds_p = (p_p * (dp_p - delta) * scale).astype(BF16)
                dqn = (jnp.dot(ds_c, kcn, preferred_element_type=F32)
                       + jnp.dot(ds_p, kpn, preferred_element_type=F32))
                dq_h, dg = _head_norm_bwd(dqn, q_hat, q_rstd, gq)
                dgq = dgq + dg
                dqs.append(dq_h)
                dkn_prev = carry_k[:, hs] + lax.dot_general(ds_p, qn, _DIMS["tn"], preferred_element_type=F32)
                dk_h, dg = _head_norm_bwd(dkn_prev, kp_hat, kp_rstd, gk)
                dgk = dgk + dg
                dks.append(dk_h)
                dvs.append(carry_v[:, hs] + lax.dot_general(p_p.astype(BF16), do_b, _DIMS["tn"],
                                                            preferred_element_type=F32))
                nck.append(lax.dot_general(ds_c, qn, _DIMS["tn"], preferred_element_type=F32))
                ncv.append(lax.dot_general(p_c.astype(BF16), do_b, _DIMS["tn"], preferred_element_type=F32))
            dq_all = jnp.concatenate(dqs, axis=1)

            @pl.when(live)
            def _(dq_all=dq_all, r=r):
                _put_rows(dq_ref, r, dil, dq_all)

            _put_rows(dk_ref, r, dil, jnp.concatenate(dks, axis=1))
            _put_rows(dv_ref, r, dil, jnp.concatenate(dvs, axis=1))
            _put_rows(ck_ref, r, dil, jnp.concatenate(nck, axis=1))
            _put_rows(cvv_ref, r, dil, jnp.concatenate(ncv, axis=1))
            gq_acc[...] += dgq
            gk_acc[...] += dgk

        @pl.when(jnp.logical_and(n == nb, hp == PAIRS - 1))
        def _():
            dgq_ref[...] = jnp.sum(gq_acc[...], axis=0, keepdims=True)
            dgk_ref[...] = jnp.sum(gk_acc[...], axis=0, keepdims=True)

    last = nb - 1
    cur = lambda c: pl.BlockSpec((rows, PAIR_W), lambda hp, n, c=c: (jnp.minimum(n, last), c + hp))
    prev = lambda c: pl.BlockSpec((rows, PAIR_W), lambda hp, n, c=c: (jnp.maximum(n - 1, 0), c + hp))
    gspec = pl.BlockSpec((1, HEAD), lambda hp, n: (0, 0))
    act = jax.ShapeDtypeStruct((S_, ATTN_OUT), F32)
    vec = jax.ShapeDtypeStruct((1, HEAD), F32)
    return _call(
        body, [z, z, z, z, z, q_gain, k_gain, do, o, lse],
        [cur(cq), cur(ck), prev(ck), cur(cv), prev(cv), gspec, gspec, cur(0), cur(0), cur(0)], after,
        name=name, grid=(PAIRS, nb + 1),
        out_specs=(cur(0), prev(0), prev(0), gspec, gspec),
        out_shape=(act, act, act, vec, vec),
        scratch_shapes=[pltpu.VMEM((rows, PAIR_W), F32), pltpu.VMEM((rows, PAIR_W), F32),
                        pltpu.VMEM((8, HEAD), F32), pltpu.VMEM((8, HEAD), F32)],
        compiler_params=_cparams(("arbitrary", "arbitrary")))


MIX_TN = 256


def _sigmoid(v):
    return 1.0 / (1.0 + jnp.exp(-v))


def _mix_fwd(name, z, b_gate, ys, ws):
    S_ = z.shape[0]
    tm, tn = ROW_TILE, MIX_TN
    gblk = OFF_GATE // tn

    def body(yp, yc, ya, wp, wc, wa, g0, g1, g2, b0, b1, b2, m_ref):
        acc = None
        for y_ref, w_ref, g_ref, b_ref in ((yp, wp, g0, b0), (yc, wc, g1, b1), (ya, wa, g2, b2)):
            u = lax.dot_general(y_ref[...].astype(BF16), w_ref[...], _DIMS["nt"], preferred_element_type=F32)
            t = _sigmoid(g_ref[...] + b_ref[...]) * u
            acc = t if acc is None else acc + t
        m_ref[...] = acc.astype(BF16)

    yspec = lambda w: pl.BlockSpec((tm, w), lambda i, j: (i, 0))
    wspec = lambda w: pl.BlockSpec((tn, w), lambda i, j: (j, 0))
    gspec = lambda b: pl.BlockSpec((tm, tn), lambda i, j, b=b: (i, gblk + b * (D // tn) + j))
    bspec = lambda b: pl.BlockSpec((1, tn), lambda i, j, b=b: (0, b * (D // tn) + j))
    return pl.pallas_call(
        body, name=name, grid=(S_ // tm, D // tn),
        in_specs=[yspec(POOL_W), yspec(CONV_W), yspec(ATTN_OUT), wspec(POOL_W), wspec(CONV_W), wspec(ATTN_OUT),
                  gspec(0), gspec(1), gspec(2), bspec(0), bspec(1), bspec(2)],
        out_specs=pl.BlockSpec((tm, tn), lambda i, j: (i, j)),
        out_shape=jax.ShapeDtypeStruct((S_, D), BF16),
        compiler_params=_cparams(("parallel", "parallel")),
    )(*ys, *ws, z, z, z, b_gate, b_gate, b_gate)


def _mix_bwd(name, z, b_gate, y, w, dmerged, branch, after=None):
    S_ = z.shape[0]
    tm, tn = ROW_TILE, MIX_TN
    width = y.shape[1]
    gblk = OFF_GATE // tn + branch * (D // tn)
    ni = S_ // tm

    def body(y_ref, w_ref, g_ref, b_ref, dm_ref, du_ref, dg_ref, db_ref, acc_ref):
        i = pl.program_id(1)
        u = lax.dot_general(y_ref[...].astype(BF16), w_ref[...], _DIMS["nt"], preferred_element_type=F32)
        sg = _sigmoid(g_ref[...] + b_ref[...])
        dm = dm_ref[...]
        du_ref[...] = (sg * dm).astype(BF16)
        dpre = dm * u * sg * (1.0 - sg)
        dg_ref[...] = dpre.astype(BF16)
        part = jnp.sum(dpre.reshape(tm // 8, 8, tn), axis=0)

        @pl.when(i == 0)
        def _():
            acc_ref[...] = part

        @pl.when(i > 0)
        def _():
            acc_ref[...] += part

        @pl.when(i == ni - 1)
        def _():
            db_ref[...] = jnp.sum(acc_ref[...], axis=0, keepdims=True)

    blk = pl.BlockSpec((tm, tn), lambda j, i: (i, j))
    vec = pl.BlockSpec((1, tn), lambda j, i: (0, j))
    act = jax.ShapeDtypeStruct((S_, D), BF16)
    return _call(
        body, [y, w, z, b_gate, dmerged],
        [pl.BlockSpec((tm, width), lambda j, i: (i, 0)), pl.BlockSpec((tn, width), lambda j, i: (j, 0)),
         pl.BlockSpec((tm, tn), lambda j, i: (i, gblk + j)),
         pl.BlockSpec((1, tn), lambda j, i: (0, branch * (D // tn) + j)), blk], after,
        name=name, grid=(D // tn, ni),
        out_specs=(blk, blk, vec), out_shape=(act, act, jax.ShapeDtypeStruct((1, D), F32)),
        scratch_shapes=[pltpu.VMEM((8, tn), F32)],
        compiler_params=_cparams(("parallel", "arbitrary")))


def _relu2_epilogue(acc):
    r = jnp.maximum(acc, 0.0)
    return acc, r * r


def _layer_fwd(l, x, p):
    t = f"l{l}_"
    h = _rmsnorm_fwd(t + "norm_mix", x, p["norm_mix"])
    z = _matmul(t + "in_proj", h, p["w_in_t"], "nt", 512, 3712, 1024)
    y_pool = _pool_fwd(t + "pool", z, p["pool_mix"], p["pool_scale"])
    y_conv = _conv_fwd(t + "conv", z, p["conv_w"])
    os_, ls_ = [], []
    for g, (_, dil) in enumerate(ATTN_GROUPS):
        o_g, l_g = _attn_fwd(t + f"attn{g}", z, p["q_gain"], p["k_gain"], g, dil)
        os_.append(o_g)
        ls_.append(l_g)
    y_attn, lse = _attn_combine(t + "attn_mix", os_, ls_)
    merged = _mix_fwd(t + "merge", z, p["b_gate"], (y_pool, y_conv, y_attn),
                      (p["w_pool_up_t"], p["w_conv_out_t"], p["w_attn_up_t"]))
    x1 = _matmul(t + "out_proj", merged, p["w_o"], "nn", 512, 1024, 1024,
                 extras=((x, "mn"),), epilogue=lambda acc, xr: (xr + acc,))
    h2 = _rmsnorm_fwd(t + "norm_mlp", x1, p["norm_mlp"])
    a, r = _matmul(t + "ff1", h2, p["w_ff1_t"], "nt", 512, 1024, 1024, out_dtypes=(F32, BF16),
                   epilogue=_relu2_epilogue)
    x2 = _matmul(t + "ff2", r, p["w_ff2"], "nn", 512, 1024, 1024,
                 extras=((x1, "mn"),), epilogue=lambda acc, xr: (xr + acc,))
    saved = dict(x=x, h=h, z=z, y_pool=y_pool, y_conv=y_conv, y_attn=y_attn, lse=lse, merged=merged,
                 x1=x1, h2=h2, a=a, r=r)
    return x2, saved


def _layer_bwd(l, dx2, p, s, pending, collective_id):
    t = f"l{l}_b_"
    g = {}
    rest = jax.ShapeDtypeStruct((N_DEV, REST_ROWS, D), F32)
    da = _matmul(t + "d_ff2_in", dx2, p["w_ff2"], "nt", 512, 1024, 1024, out_dtypes=(BF16,),
                 extras=((s["a"], "mn"),), epilogue=lambda acc, a: (acc * (2.0 * jnp.maximum(a, 0.0)),))
    rest = _matmul(t + "dw_ff2", s["r"], dx2, "tn", 512, 1024, 512, into=(rest, REST_FF2, 0), after=da)
    tok = rest if pending is None else pending.chip_sums(after=rest)
    dh2 = _matmul(t + "d_ff1_in", da, p["w_ff1_t"], "nn", 512, 1024, 1024, after=tok)
    rest = _matmul(t + "dw_ff1", da, s["h2"], "tn", 512, 1024, 512, into=(rest, REST_FF1, 0), after=dh2)
    dx1, g["norm_mlp"] = _rmsnorm_bwd(t + "norm_mlp", s["x1"], p["norm_mlp"], dh2, dx2, after=rest)
    dmerged = _matmul(t + "d_out_proj_in", dx1, p["w_o"], "nt", 512, 1024, 1024)
    rest = _matmul(t + "dw_o", s["merged"], dx1, "tn", 128, 1024, 2048, into=(rest, REST_WO, 0), after=dmerged)
    ys = (s["y_pool"], s["y_conv"], s["y_attn"])
    names = ("w_pool_up_t", "w_conv_out_t", "w_attn_up_t")
    dys, dgates, dbs = [], [], []
    tok = rest
    for b in range(3):
        du, dgz, db = _mix_bwd(t + f"merge{b}", s["z"], p["b_gate"], ys[b], p[names[b]], dmerged, b, after=tok)
        width = ys[b].shape[1]
        dys.append(_matmul(t + f"d_up{b}_in", du, p[names[b]], "nn", 512, width, 1024))
        if b < 2:
            rest = _matmul(t + f"dw_up{b}", du, ys[b], "tn", 128, width, 2048, into=(rest, REST_UP, b * width),
                           after=dys[b])
            tok = rest
        else:
            d_attn_up_t = _matmul(t + f"dw_up{b}", du, ys[b], "tn", 128, width, 2048, after=dys[b])
            rest = rest.at[:, REST_ATTN:REST_ROWS, :].set(d_attn_up_t.reshape(N_DEV, REST_ROWS - REST_ATTN, D))
        dgates.append(dgz)
        dbs.append(db)
    g["b_gate"] = jnp.concatenate(dbs, axis=1)
    rs_rest = _ReduceScatter(f"rs_rest{l}", collective_id, rest)
    pending_sum = None if pending is None else pending.finish(after=rest)
    du_pool, g["pool_mix"], g["pool_scale"] = _pool_bwd(t + "pool", s["z"], dys[0], p["pool_mix"], p["pool_scale"],
                                                        after=rest if pending is None else pending_sum[0])
    dcb, dcc, dcx, g["conv_w"] = _conv_bwd(t + "conv", s["z"], dys[1], p["conv_w"], after=du_pool)
    tok = rs_rest.chip_sums(after=dcb)
    dqs, dks, dvs = [], [], []
    gq = gk = None
    for gi, (_, dil) in enumerate(ATTN_GROUPS):
        dq, dk, dv, dgq, dgk = _attn_bwd(t + f"attn{gi}", s["z"], p["q_gain"], p["k_gain"], dys[2], s["y_attn"],
                                         s["lse"], gi, dil, after=tok)
        tok = dq
        dqs.append(dq)
        dks.append(dk)
        dvs.append(dv)
        gq = dgq if gq is None else gq + dgq
        gk = dgk if gk is None else gk + dgk
    g["q_gain"], g["k_gain"] = gq, gk
    dz = jnp.concatenate([du_pool, dcb, dcc, dcx] + [v.astype(BF16) for v in dqs + dks + dvs] + dgates, axis=1)
    dh = _matmul(t + "d_in_proj_in", dz, p["w_in_t"], "nn", 512, 1024, 3712)
    rest_sum, _ = rs_rest.finish(after=dh)
    g["in_t"] = _matmul(t + "dw_in", dz, s["h"], "tn", 256, 1024, 4096, after=rest_sum)
    dx, g["norm_mix"] = _rmsnorm_bwd(t + "norm_mix", s["x"], p["norm_mix"], dh, dx1, after=g["in_t"])
    return dx, g, rest_sum, pending_sum


ANY = pl.BlockSpec(memory_space=pl.ANY)


def _mesh_pos():
    return lax.axis_index("x"), lax.axis_index("y"), lax.axis_index("c")


def _other_chips(x, y):
    return [(1 - x, y), (x, 1 - y), (1 - x, 1 - y)]


def _comm_call(name, collective_id, peers, body, arrs, out_shape, sem_counts):
    n_in, n_out = len(arrs), len(out_shape)

    def seq_body(*refs):
        barrier = pltpu.get_barrier_semaphore()
        ps = peers(*_mesh_pos())
        for p in ps:
            pl.semaphore_signal(barrier, inc=1, device_id=p, device_id_type=MESH)
        pl.semaphore_wait(barrier, len(ps))
        body(refs[:n_in], refs[n_in:n_in + n_out], *refs[n_in + n_out:])

    return pl.kernel(
        seq_body, out_type=tuple(out_shape), mesh=plsc.ScalarSubcoreMesh(axis_name="seq", num_cores=1), name=name,
        scratch_types=[pltpu.SemaphoreType.DMA((n,)) for n in sem_counts],
        compiler_params=pltpu.CompilerParams(collective_id=collective_id),
    )(*arrs)


def _all_gather(name, collective_id, shard):
    R, C = shard.shape

    def peers(x, y, c):
        return [(x, y, 1 - c)] + [(*chip, c) for chip in _other_chips(x, y)]

    def body(in_refs, out_refs, send_sems, recv_sems, local_sems):
        (x_ref,), (out_ref,) = in_refs, out_refs
        x, y, c = _mesh_pos()
        me, sibling = (x, y, c), (x, y, 1 - c)
        chips = _other_chips(x, y)

        def slot(px, py, pc):
            return out_ref.at[4 * px + 2 * py + pc]

        def copy(k, block, to, src=None):
            return pltpu.make_async_remote_copy(
                src_ref=slot(*block) if src is None else src, dst_ref=slot(*block),
                send_sem=send_sems.at[k], recv_sem=recv_sems.at[k], device_id=to, device_id_type=MESH)

        mine = pltpu.make_async_copy(x_ref, slot(*me), local_sems.at[0])
        mine.start()
        first = [copy(0, me, sibling, src=x_ref)]
        first += [copy(1 + j, me, (*chip, c), src=x_ref) for j, chip in enumerate(chips)]
        for cp in first:
            cp.start()
        passed = [copy(4 + j, (*chip, c), sibling) for j, chip in enumerate(chips)]
        for j, chip in enumerate(chips):
            copy(1 + j, (*chip, c), me).wait_recv()
            passed[j].start()
        copy(0, sibling, me).wait_recv()
        for j, chip in enumerate(chips):
            copy(4 + j, (*chip, 1 - c), me).wait_recv()
        for cp in first + passed:
            cp.wait_send()
        mine.wait()

    return _comm_call(name, collective_id, peers, body, [shard],
                      [jax.ShapeDtypeStruct((N_DEV, R, C), shard.dtype)], (7, 7, 1))[0]


def _rs_sibling_exchange(name, collective_id, arrs):
    n = len(arrs)

    def body(in_refs, out_refs, send_sems, recv_sems):
        x, y, c = _mesh_pos()
        cps = []
        for k, (src, dst) in enumerate(zip(in_refs, out_refs)):
            src = src.at[:, 1 - c] if len(src.shape) == 4 else src
            cps.append(pltpu.make_async_remote_copy(src_ref=src, dst_ref=dst, send_sem=send_sems.at[k],
                                                    recv_sem=recv_sems.at[k], device_id=(x, y, 1 - c),
                                                    device_id_type=MESH))
        for cp in cps:
            cp.start()
        for cp in cps:
            cp.wait()

    out_shape = [jax.ShapeDtypeStruct(a.shape[:1] + a.shape[2:] if a.ndim == 4 else a.shape, a.dtype) for a in arrs]
    return _comm_call(name, collective_id, lambda x, y, c: [(x, y, 1 - c)], body, arrs, out_shape, (n, n))


def _rs_rows(r):
    return r // 2 if (r // 2) % 16 == 0 else r


def _drop_ref(body, idx):
    def wrapped(*refs):
        return body(*refs[:idx], *refs[idx + 1:])

    return wrapped


def _rs_chip_sum(name, ids, big, rbig, after=None):
    _, _, R, C = big.shape
    rows = _rs_rows(R)

    def body(ids_ref, mine_ref, theirs_ref, t16_ref, own_ref):
        p = pl.program_id(1)
        s = mine_ref[...] + theirs_ref[...]
        t16_ref[...] = s.astype(BF16)

        @pl.when(p == ids_ref[1])
        def _():
            own_ref[...] = s

    in_specs = [pl.BlockSpec((None, None, rows, C), lambda i, p, ids: (p, ids[0], i, 0)),
                pl.BlockSpec((None, rows, C), lambda i, p, ids: (p, i, 0))]
    args = [ids, big, rbig]
    if after is not None:
        body = _drop_ref(body, len(args))
        in_specs.append(ANY)
        args.append(after)
    grid_spec = pltpu.PrefetchScalarGridSpec(
        num_scalar_prefetch=1, grid=(R // rows, 4), in_specs=in_specs,
        out_specs=(pl.BlockSpec((None, rows, C), lambda i, p, ids: (p, i, 0)),
                   pl.BlockSpec((rows, C), lambda i, p, ids: (i, 0))))
    return pl.pallas_call(
        body, name=name, grid_spec=grid_spec,
        out_shape=(jax.ShapeDtypeStruct((4, R, C), BF16), jax.ShapeDtypeStruct((R, C), F32)),
        compiler_params=_cparams(("parallel", "arbitrary")),
    )(*args)


def _add2(name, a, b):
    def body(a_ref, b_ref, o_ref):
        o_ref[...] = a_ref[...] + b_ref[...]

    full = pl.BlockSpec(a.shape, lambda i: (0, 0))
    return pl.pallas_call(body, name=name, grid=(1,), in_specs=[full, full], out_specs=full,
                          out_shape=jax.ShapeDtypeStruct(a.shape, a.dtype))(a, b)


def _rs_chip_exchange(name, collective_id, arrs):
    n = len(arrs)

    def body(in_refs, out_refs, send_sems, recv_sems, local_sems):
        x, y, c = _mesh_pos()
        p_me = 2 * x + y
        chips = _other_chips(x, y)

        def part(ref, p):
            return ref.at[p] if len(ref.shape) == 3 else ref

        local = [pltpu.make_async_copy(part(src, p_me), dst.at[p_me], local_sems.at[k])
                 for k, (src, dst) in enumerate(zip(in_refs, out_refs))]
        for cp in local:
            cp.start()
        sends = []
        for j, (px, py) in enumerate(chips):
            for k, (src, dst) in enumerate(zip(in_refs, out_refs)):
                sends.append(pltpu.make_async_remote_copy(
                    src_ref=part(src, 2 * px + py), dst_ref=dst.at[p_me], send_sem=send_sems.at[n * j + k],
                    recv_sem=recv_sems.at[n * j + k], device_id=(px, py, c), device_id_type=MESH))
        for cp in sends:
            cp.start()
        for j, (px, py) in enumerate(chips):
            for k, (src, dst) in enumerate(zip(in_refs, out_refs)):
                pltpu.make_async_remote_copy(
                    src_ref=part(src, p_me), dst_ref=dst.at[2 * px + py], send_sem=send_sems.at[n * j + k],
                    recv_sem=recv_sems.at[n * j + k], device_id=(px, py, c), device_id_type=MESH).wait_recv()
        for cp in sends:
            cp.wait_send()
        for cp in local:
            cp.wait()

    out_shape = [jax.ShapeDtypeStruct((4,) + a.shape[-2:], a.dtype) for a in arrs]
    return _comm_call(name, collective_id, lambda x, y, c: [(*chip, c) for chip in _other_chips(x, y)], body, arrs,
                      out_shape, (3 * n, 3 * n, n))


def _rs_final_sum(name, ids, recv, own, rows, after=None):
    _, R, C = recv.shape
    assert R % rows == 0

    def body(ids_ref, r_ref, own_ref, o_ref):
        acc = None
        for p in range(4):
            term = jnp.where(ids_ref[1] == p, own_ref[...], r_ref[p].astype(F32))
            acc = term if acc is None else acc + term
        o_ref[...] = acc

    in_specs = [pl.BlockSpec((4, rows, C), lambda i, ids: (0, i, 0)), pl.BlockSpec((rows, C), lambda i, ids: (i, 0))]
    args = [ids, recv, own]
    if after is not None:
        body = _drop_ref(body, len(args))
        in_specs.append(ANY)
        args.append(after)
    grid_spec = pltpu.PrefetchScalarGridSpec(
        num_scalar_prefetch=1, grid=(R // rows,), in_specs=in_specs,
        out_specs=pl.BlockSpec((rows, C), lambda i, ids: (i, 0)))
    return pl.pallas_call(
        body, name=name, grid_spec=grid_spec, out_shape=jax.ShapeDtypeStruct((R, C), F32),
        compiler_params=_cparams(("parallel",)),
    )(*args)


class _ReduceScatter:
    def __init__(self, tag, collective_id, big, small=None):
        x, y, c = _mesh_pos()
        self.tag, self.cid, self.small = tag, collective_id, small
        self.ids = jnp.stack([c, 2 * x + y]).astype(jnp.int32)
        self.big = big.reshape((4, 2) + big.shape[1:])
        self.got = _rs_sibling_exchange(tag + "_sibling", collective_id,
                                        [self.big] + ([] if small is None else [small]))

    def chip_sums(self, after=None):
        t16, self.own = _rs_chip_sum(self.tag + "_chip_sum", self.ids, self.big, self.got[0], after)
        arrs = [t16]
        if self.small is not None:
            self.ts = _add2(self.tag + "_chip_sum_small", self.small, self.got[1])
            arrs.append(self.ts)
        self.recv = _rs_chip_exchange(self.tag + "_chips", self.cid + 1, arrs)
        return t16

    def finish(self, after=None):
        out = _rs_final_sum(self.tag + "_final", self.ids, self.recv[0], self.own, _rs_rows(self.own.shape[0]), after)
        out_small = None
        if self.small is not None:
            out_small = _rs_final_sum(self.tag + "_final_small", self.ids, self.recv[1], self.ts, self.small.shape[0])
        return out, out_small


def _adamw(name, w, g, m, v):
    R, C = w.shape
    tr = max(t for t in range(8, 513, 8) if R % t == 0)
    c1 = 1.0 - ADAM_B1 ** ADAM_STEP
    c2 = 1.0 - ADAM_B2 ** ADAM_STEP

    def body(w_ref, g_ref, m_ref, v_ref, d_ref, nm_ref, nv_ref):
        gv = g_ref[...]
        nm = ADAM_B1 * m_ref[...] + (1.0 - ADAM_B1) * gv
        nv = ADAM_B2 * v_ref[...] + (1.0 - ADAM_B2) * (gv * gv)
        d_ref[...] = -ADAM_LR * ((nm / c1) / (jnp.sqrt(nv / c2) + ADAM_EPS) + ADAM_WD * w_ref[...])
        nm_ref[...] = nm
        nv_ref[...] = nv

    blk = pl.BlockSpec((tr, C), lambda i: (i, 0))
    shp = jax.ShapeDtypeStruct((R, C), F32)
    return pl.pallas_call(
        body, name=name, grid=(R // tr,), in_specs=[blk] * 4, out_specs=(blk,) * 3, out_shape=(shp,) * 3,
        compiler_params=_cparams(("parallel",)),
    )(w, g, m, v)


def _adamw_t(name, w, g_t, m, v):
    L, R, C = w.shape
    tr = 256
    c1 = 1.0 - ADAM_B1 ** ADAM_STEP
    c2 = 1.0 - ADAM_B2 ** ADAM_STEP

    def body(w_ref, g_ref, m_ref, v_ref, go_ref, d_ref, nm_ref, nv_ref):
        gv = g_ref[...].T
        go_ref[...] = gv
        nm = ADAM_B1 * m_ref[...] + (1.0 - ADAM_B1) * gv
        nv = ADAM_B2 * v_ref[...] + (1.0 - ADAM_B2) * (gv * gv)
        d_ref[...] = -ADAM_LR * ((nm / c1) / (jnp.sqrt(nv / c2) + ADAM_EPS) + ADAM_WD * w_ref[...])
        nm_ref[...] = nm
        nv_ref[...] = nv

    blk = pl.BlockSpec((None, tr, C), lambda l, i: (l, i, 0))
    blk_t = pl.BlockSpec((None, C, tr), lambda l, i: (l, 0, i))
    shp = jax.ShapeDtypeStruct((L, R, C), F32)
    return pl.pallas_call(
        body, name=name, grid=(L, R // tr), in_specs=[blk, blk_t, blk, blk], out_specs=(blk,) * 4,
        out_shape=(shp,) * 4, compiler_params=_cparams(("parallel", "parallel")),
    )(w, g_t, m, v)


REST_FF1 = 0
REST_FF2 = 512
REST_WO = 1024
REST_UP = 1152
REST_ATTN = 1280
REST_ROWS = 1312
IN_ROWS = IN_COLS // N_DEV
CONV_BITS_ROWS = 16
SHARD_ROWS = IN_ROWS + REST_ROWS + CONV_BITS_ROWS
SMALL = (("norm_mix", (DEPTH, D)), ("b_gate", (DEPTH, 3 * D)), ("pool_mix", (DEPTH, 4, POOL_G, POOL_G)),
         ("pool_scale", (DEPTH, POOL_W)), ("conv_w", (DEPTH, 3, CONV_W)), ("q_gain", (DEPTH, HEAD)),
         ("k_gain", (DEPTH, HEAD)), ("norm_mlp", (DEPTH, D)))


def _pack_weight_shards(w, l):
    b = lambda a: a.astype(BF16)
    conv = lax.bitcast_convert_type(w["conv_w"][l], BF16).reshape(3, 128)
    conv = jnp.pad(conv, ((0, CONV_BITS_ROWS - 3), (0, D - 128)))
    rest = jnp.concatenate([
        b(w["w_ff1"][l].T), b(w["w_ff2"][l]), b(w["w_o"][l]),
        jnp.concatenate([b(w["w_pool_up"][l].T), b(w["w_conv_out"][l].T)], axis=1),
        b(w["w_attn_up"][l].T).reshape(REST_ROWS - REST_ATTN, D), conv], axis=0)
    return b(w["w_in"][l].T), rest


def _unpack_gathered(g_in, g_rest, small_w, l):
    rest = g_rest[:, :REST_ROWS]
    take = lambda r0, rows, c0=0, cols=D: rest[:, r0:r0 + rows, c0:c0 + cols].reshape(N_DEV * rows, cols)
    conv = g_rest[:, REST_ROWS:REST_ROWS + 3, :128].reshape(N_DEV, 3, CONV_W // N_DEV, 2)
    conv = lax.bitcast_convert_type(conv, F32)
    p = {
        "w_in_t": g_in.reshape(IN_COLS, D),
        "w_ff1_t": take(REST_FF1, 512), "w_ff2": take(REST_FF2, 512), "w_o": take(REST_WO, 128),
        "w_pool_up_t": take(REST_UP, 128, 0, POOL_W), "w_conv_out_t": take(REST_UP, 128, POOL_W, CONV_W),
        "w_attn_up_t": rest[:, REST_ATTN:].reshape(D, ATTN_OUT),
        "conv_w": jnp.transpose(conv, (1, 0, 2)).reshape(3, CONV_W),
        "pool_mix": small_w["pool_mix"][l],
    }
    for name in ("norm_mix", "b_gate", "pool_scale", "q_gain", "k_gain", "norm_mlp"):
        p[name] = small_w[name][l][None, :]
    return p


def _pack_small_grads(grads):
    flat = jnp.concatenate([jnp.stack([grads[l][name] for l in range(DEPTH)]).reshape(-1) for name, _ in SMALL])
    return jnp.pad(flat, (0, (-flat.shape[0]) % (8 * 128))).reshape(-1, 128)


def _unpack_grads(in_t, rest, small, dev):
    out = {
        "w_in_t": jnp.stack(in_t),
        "w_ff1_t": jnp.stack([a[REST_FF1:REST_FF1 + 512] for a in rest]),
        "w_ff2": jnp.stack([a[REST_FF2:REST_FF2 + 512] for a in rest]),
        "w_o": jnp.stack([a[REST_WO:REST_WO + 128] for a in rest]),
        "w_pool_up": jnp.stack([a[REST_UP:REST_UP + 128, :POOL_W].T for a in rest]),
        "w_conv_out": jnp.stack([a[REST_UP:REST_UP + 128, POOL_W:].T for a in rest]),
        "w_attn_up": jnp.stack([a[REST_ATTN:].reshape(D // N_DEV, ATTN_OUT).T for a in rest]),
    }
    flat = small.reshape(-1)
    off = 0
    for name, shp in SMALL:
        n = 1
        for s in shp:
            n *= s
        out[name] = flat[off:off + n].reshape(shp)
        off += n
    width = CONV_W // N_DEV
    out["conv_w"] = lax.dynamic_slice_in_dim(out["conv_w"], dev * width, width, axis=2)
    return out


def _pack_small(arrs):
    flat = jnp.concatenate([a.reshape(-1) for a in arrs])
    pad = (-flat.shape[0]) % (8 * 128)
    return jnp.pad(flat, (0, pad)).reshape(-1, 128)


def _unpack_small(packed, like):
    flat = packed.reshape(-1)
    out, off = [], 0
    for a in like:
        out.append(flat[off:off + a.size].reshape(a.shape))
        off += a.size
    return out


WEIGHTS = ("norm_mix", "w_in", "b_gate", "pool_mix", "pool_scale", "conv_w", "q_gain", "k_gain", "w_pool_up",
           "w_conv_out", "w_attn_up", "w_o", "norm_mlp", "w_ff1", "w_ff2")


def kernel(x, norm_mix, w_in, b_gate, pool_mix, pool_scale, conv_w, q_gain, k_gain, w_pool_up, w_conv_out, w_attn_up, w_o, norm_mlp, w_ff1, w_ff2, loss_target, m_norm_mix, m_w_in, m_b_gate, m_pool_mix, m_pool_scale, m_conv_w, m_q_gain, m_k_gain, m_w_pool_up, m_w_conv_out, m_w_attn_up, m_w_o, m_norm_mlp, m_w_ff1, m_w_ff2, v_norm_mix, v_w_in, v_b_gate, v_pool_mix, v_pool_scale, v_conv_w, v_q_gain, v_k_gain, v_w_pool_up, v_w_conv_out, v_w_attn_up, v_w_o, v_norm_mlp, v_w_ff1, v_w_ff2):
    w = dict(zip(WEIGHTS, (norm_mix, w_in, b_gate, pool_mix, pool_scale, conv_w, q_gain, k_gain, w_pool_up,
                           w_conv_out, w_attn_up, w_o, norm_mlp, w_ff1, w_ff2)))
    m = dict(zip(WEIGHTS, (m_norm_mix, m_w_in, m_b_gate, m_pool_mix, m_pool_scale, m_conv_w, m_q_gain, m_k_gain,
                           m_w_pool_up, m_w_conv_out, m_w_attn_up, m_w_o, m_norm_mlp, m_w_ff1, m_w_ff2)))
    v = dict(zip(WEIGHTS, (v_norm_mix, v_w_in, v_b_gate, v_pool_mix, v_pool_scale, v_conv_w, v_q_gain, v_k_gain,
                           v_w_pool_up, v_w_conv_out, v_w_attn_up, v_w_o, v_norm_mlp, v_w_ff1, v_w_ff2)))
    xi, yi, ci = _mesh_pos()
    dev = 4 * xi + 2 * yi + ci

    params = []
    for l in range(DEPTH):
        s_in, s_rest = _pack_weight_shards(w, l)
        g_in = _all_gather(f"gather_in{l}", 1 + 2 * l, s_in)
        g_rest = _all_gather(f"gather_rest{l}", 2 + 2 * l, s_rest)
        params.append(_unpack_gathered(g_in, g_rest, w, l))
    saved = []
    act = x[0]
    for l in range(DEPTH):
        act, s = _layer_fwd(l, act, params[l])
        saved.append(s)
    dy, loss = _loss_head("loss_head", act, loss_target[0])
    in_t, rest, grads, pending = [None] * DEPTH, [None] * DEPTH, [None] * DEPTH, None
    for l in reversed(range(DEPTH)):
        dy, grads[l], rest[l], done = _layer_bwd(l, dy, params[l], saved[l], pending, 5 + 4 * l)
        if done is not None:
            in_t[l + 1] = done[0]
        pending = _ReduceScatter(f"rs_in{l}", 7 + 4 * l, grads[l]["in_t"].reshape(N_DEV, IN_ROWS, D),
                                 _pack_small_grads(grads) if l == 0 else None)
    pending.chip_sums()
    in_t[0], g_small = pending.finish()
    dx = dy
    g = _unpack_grads(in_t, rest, g_small, dev)

    delta, new_m, new_v = {}, {}, {}
    for name in ("w_in", "w_ff1"):
        g[name], delta[name], new_m[name], new_v[name] = _adamw_t("adamw_" + name, w[name], g[name + "_t"],
                                                                  m[name], v[name])
    for name in ("w_pool_up", "w_conv_out", "w_attn_up", "w_o", "w_ff2"):
        shp = w[name].shape
        two_d = (shp[0] * shp[1], shp[2])
        d_, m_, v_ = _adamw("adamw_" + name, w[name].reshape(two_d), g[name].reshape(two_d),
                            m[name].reshape(two_d), v[name].reshape(two_d))
        delta[name], new_m[name], new_v[name] = d_.reshape(shp), m_.reshape(shp), v_.reshape(shp)
    small_names = [name for name, _ in SMALL]
    packed = [_pack_small([src[name] for name in small_names]) for src in (w, g, m, v)]
    outs = _adamw("adamw_small", *packed)
    for dst, arr in zip((delta, new_m, new_v), outs):
        for name, val in zip(small_names, _unpack_small(arr, [w[name] for name in small_names])):
            dst[name] = val

    total = lax.psum(loss[0, 0], ("x", "y", "c"))
    return (total, dx[None], *[g[n] for n in WEIGHTS], *[delta[n] for n in WEIGHTS],
            *[new_m[n] for n in WEIGHTS], *[new_v[n] for n in WEIGHTS])
```

```python
import functools

import jax
import jax.numpy as jnp
from jax import lax
from jax.experimental import pallas as pl
from jax.experimental.pallas import tpu as pltpu
from jax.experimental.pallas import tpu_sc as plsc

F32 = jnp.float32
BF16 = jnp.bfloat16
MESH = pl.DeviceIdType.MESH

D = 1024
SEQ = 4096
DEPTH = 2
N_DEV = 8
POOL_WINDOWS = (2, 4, 8, 16)
POOL_W = 512
POOL_G = 128
CONV_W = 512
HEAD = 64
ATTN_GROUPS = ((128, 1), (512, 4), (2048, 16))
HPG = 4
ATTN_W = 768
ATTN_OUT = 256
ATTN_BLK = 128
D_FF = 4096
EPS = 1e-6
MASK = -1e30
OFF_POOL = 0
OFF_CB = 512
OFF_CC = 1024
OFF_CX = 1536
OFF_Q = 2048
OFF_K = 2816
OFF_V = 3584
OFF_GATE = 4352
IN_COLS = 7424
ADAM_LR = 0.001
ADAM_B1 = 0.9
ADAM_B2 = 0.999
ADAM_EPS = 1e-08
ADAM_WD = 0.01
ADAM_STEP = 10

ROW_TILE = 512
SEQ_CHUNK = 256
HALO = 16
VMEM_LIMIT = 56 * 1024 * 1024


def _cparams(sem=None):
    return pltpu.CompilerParams(dimension_semantics=sem, vmem_limit_bytes=VMEM_LIMIT)


def _call(body, args, in_specs, after=None, **kw):
    if after is not None:
        n, inner = len(args), body

        def body(*refs):
            return inner(*refs[:n], *refs[n + 1:])

        args = list(args) + [after]
        in_specs = list(in_specs) + [pl.BlockSpec(memory_space=pl.ANY)]
    return pl.pallas_call(body, in_specs=in_specs, **kw)(*args)


_DIMS = {"nn": (((1,), (0,)), ((), ())), "nt": (((1,), (1,)), ((), ())), "tn": (((0,), (0,)), ((), ()))}


def _matmul(name, a, b, mode, tm, tn, tk, out_dtypes=(F32,), extras=(), epilogue=None, into=None, after=None):
    if mode == "tn":
        K, M = a.shape
    else:
        M, K = a.shape
    N = b.shape[0] if mode == "nt" else b.shape[1]
    assert M % tm == 0 and N % tn == 0 and K % tk == 0, (name, M, N, K, tm, tn, tk)
    nk = K // tk
    n_extra = len(extras)
    n_out = len(out_dtypes)
    dims = _DIMS[mode]
    n_alias = 0 if into is None or isinstance(into[0], jax.ShapeDtypeStruct) else 1

    def body(a_ref, b_ref, *rest):
        extra_refs = rest[:n_extra]
        out_refs = rest[n_extra + n_alias:n_extra + n_alias + n_out]

        def finish(acc):
            if epilogue is None:
                res = (acc,)
            else:
                res = epilogue(acc, *[r[...] for r in extra_refs])
            for o_ref, v in zip(out_refs, res):
                o_ref[...] = v.astype(o_ref.dtype)

        part = lax.dot_general(a_ref[...].astype(BF16), b_ref[...].astype(BF16), dims,
                               preferred_element_type=F32)
        if nk == 1:
            finish(part)
        else:
            acc_ref = rest[-1]
            k = pl.program_id(2)

            @pl.when(k == 0)
            def _():
                acc_ref[...] = part

            @pl.when(k > 0)
            def _():
                acc_ref[...] += part

            @pl.when(k == nk - 1)
            def _():
                finish(acc_ref[...])

    if mode == "tn":
        a_spec = pl.BlockSpec((tk, tm), lambda i, j, k: (k, i))
    else:
        a_spec = pl.BlockSpec((tm, tk), lambda i, j, k: (i, k))
    if mode == "nt":
        b_spec = pl.BlockSpec((tn, tk), lambda i, j, k: (j, k))
    else:
        b_spec = pl.BlockSpec((tk, tn), lambda i, j, k: (k, j))
    in_specs = [a_spec, b_spec]
    args = [a, b]
    for arr, kind in extras:
        if kind == "mn":
            in_specs.append(pl.BlockSpec((tm, tn), lambda i, j, k: (i, j)))
        else:
            in_specs.append(pl.BlockSpec((1, tn), lambda i, j, k: (0, j)))
        args.append(arr)
    out_shape = tuple(jax.ShapeDtypeStruct((M, N), dt) for dt in out_dtypes)
    out_specs = tuple(pl.BlockSpec((tm, tn), lambda i, j, k: (i, j)) for _ in out_dtypes)
    aliases = {}
    if into is not None:
        buf, row0, col0 = into
        assert n_out == 1 and (M // N_DEV) % tm == 0 and row0 % tm == 0 and col0 % tn == 0
        per_dev = M // N_DEV // tm
        out_shape = (jax.ShapeDtypeStruct(buf.shape, buf.dtype),)
        out_specs = (pl.BlockSpec((None, tm, tn), lambda i, j, k: (i // per_dev, row0 // tm + i % per_dev,
                                                                   col0 // tn + j)),)
        if not isinstance(buf, jax.ShapeDtypeStruct):
            aliases = {len(args): 0}
            in_specs.append(pl.BlockSpec(memory_space=pl.ANY))
            args.append(buf)
    scratch = [] if nk == 1 else [pltpu.VMEM((tm, tn), F32)]
    res = _call(
        body, args, in_specs, after, name=name, grid=(M // tm, N // tn, nk), out_specs=out_specs,
        out_shape=out_shape, scratch_shapes=scratch, input_output_aliases=aliases,
        compiler_params=_cparams(("parallel", "parallel", "arbitrary")))
    return res if n_out > 1 else res[0]


def _rmsnorm_fwd(name, x, gain):
    S_, D_ = x.shape

    def body(x_ref, g_ref, h_ref):
        xf = x_ref[...]
        ms = jnp.mean(xf * xf, axis=-1, keepdims=True)
        h_ref[...] = (xf * lax.rsqrt(ms + EPS) * g_ref[...]).astype(BF16)

    return pl.pallas_call(
        body, name=name, grid=(S_ // ROW_TILE,),
        in_specs=[pl.BlockSpec((ROW_TILE, D_), lambda i: (i, 0)), pl.BlockSpec((1, D_), lambda i: (0, 0))],
        out_specs=pl.BlockSpec((ROW_TILE, D_), lambda i: (i, 0)),
        out_shape=jax.ShapeDtypeStruct((S_, D_), BF16),
        compiler_params=_cparams(("parallel",)),
    )(x, gain)


def _rmsnorm_bwd(name, x, gain, dh, dres, after=None):
    S_, D_ = x.shape
    n = S_ // ROW_TILE

    def body(x_ref, g_ref, dh_ref, dres_ref, dx_ref, dg_ref, acc_ref):
        i = pl.program_id(0)
        xf = x_ref[...]
        rstd = lax.rsqrt(jnp.mean(xf * xf, axis=-1, keepdims=True) + EPS)
        xhat = xf * rstd
        dhv = dh_ref[...]
        dxhat = dhv * g_ref[...]
        c = jnp.mean(dxhat * xhat, axis=-1, keepdims=True)
        dx_ref[...] = dres_ref[...] + rstd * (dxhat - xhat * c)
        part = jnp.sum((dhv * xhat).reshape(ROW_TILE // 8, 8, D_), axis=0)

        @pl.when(i == 0)
        def _():
            acc_ref[...] = part

        @pl.when(i > 0)
        def _():
            acc_ref[...] += part

        @pl.when(i == n - 1)
        def _():
            dg_ref[...] = jnp.sum(acc_ref[...], axis=0, keepdims=True)

    row = pl.BlockSpec((ROW_TILE, D_), lambda i: (i, 0))
    vec = pl.BlockSpec((1, D_), lambda i: (0, 0))
    return _call(
        body, [x, gain, dh, dres], [row, vec, row, row], after, name=name, grid=(n,), out_specs=(row, vec),
        out_shape=(jax.ShapeDtypeStruct((S_, D_), F32), jax.ShapeDtypeStruct((1, D_), F32)),
        scratch_shapes=[pltpu.VMEM((8, D_), F32)],
        compiler_params=_cparams(("arbitrary",)))


def _loss_head(name, y, target):
    S_, D_ = y.shape
    n = S_ // ROW_TILE

    def body(y_ref, t_ref, dy_ref, l_ref, acc_ref):
        i = pl.program_id(0)
        e = y_ref[...] - t_ref[...]
        dy_ref[...] = e * (1.0 / D_)
        part = jnp.sum((e * e).reshape(ROW_TILE // 8, 8, D_), axis=0)

        @pl.when(i == 0)
        def _():
            acc_ref[...] = part

        @pl.when(i > 0)
        def _():
            acc_ref[...] += part

        @pl.when(i == n - 1)
        def _():
            s = jnp.sum(acc_ref[...], axis=1, keepdims=True)
            l_ref[...] = jnp.sum(s, axis=0, keepdims=True) * (0.5 / D_)

    row = pl.BlockSpec((ROW_TILE, D_), lambda i: (i, 0))
    return pl.pallas_call(
        body, name=name, grid=(n,), in_specs=[row, row],
        out_specs=(row, pl.BlockSpec((1, 1), lambda i: (0, 0))),
        out_shape=(jax.ShapeDtypeStruct((S_, D_), F32), jax.ShapeDtypeStruct((1, 1), F32)),
        scratch_shapes=[pltpu.VMEM((8, D_), F32)],
        compiler_params=_cparams(("arbitrary",)),
    )(y, target)


def _rows_with_halo(ref, cols, i, n_chunks, before, after):
    r0 = pl.multiple_of(i * SEQ_CHUNK, SEQ_CHUNK)
    parts = []
    if before:
        h0 = pl.multiple_of(jnp.maximum(r0 - HALO, 0), 8)
        halo = ref[pl.ds(h0, HALO), cols]
        parts.append(jnp.where(i > 0, halo, jnp.zeros_like(halo)))
    parts.append(ref[pl.ds(r0, SEQ_CHUNK), cols])
    if after:
        a0 = pl.multiple_of(jnp.minimum(r0 + SEQ_CHUNK, (n_chunks - 1) * SEQ_CHUNK + SEQ_CHUNK - HALO), 8)
        halo = ref[pl.ds(a0, HALO), cols]
        parts.append(jnp.where(i < n_chunks - 1, halo, jnp.zeros_like(halo)))
    return parts[0] if len(parts) == 1 else jnp.concatenate(parts, axis=0)


def _shift_down(v, k):
    return pltpu.roll(v, k, 0)


def _shift_up(v, k):
    return pltpu.roll(v, v.shape[0] - k, 0)


def _pool_diff(xx, w, t_main):
    s = xx
    k = 1
    while k < w:
        s = s + _shift_down(s, k)
        k *= 2
    cnt = jnp.minimum(t_main + 1, w).astype(F32)
    return s[HALO:] / cnt - xx[HALO:]


def _pool_fwd(name, z, pool_mix, pool_scale):
    S_ = z.shape[0]
    n_chunks = S_ // SEQ_CHUNK

    def body(u_ref, mix_ref, sc_ref, y_ref):
        for g, w in enumerate(POOL_WINDOWS):
            cols = slice(g * POOL_G, (g + 1) * POOL_G)
            mixg = mix_ref[g].astype(BF16)
            scg = sc_ref[:, cols]

            def chunk(i, carry, cols=cols, mixg=mixg, scg=scg, w=w):
                r0 = pl.multiple_of(i * SEQ_CHUNK, SEQ_CHUNK)
                xx = _rows_with_halo(u_ref, cols, i, n_chunks, True, False)
                t = r0 + lax.broadcasted_iota(jnp.int32, (SEQ_CHUNK, POOL_G), 0)
                d = _pool_diff(xx, w, t)
                y = jnp.dot(d.astype(BF16), mixg, preferred_element_type=F32) * scg
                y_ref[pl.ds(r0, SEQ_CHUNK), cols] = y.astype(BF16)
                return carry

            lax.fori_loop(0, n_chunks, chunk, 0)

    return pl.pallas_call(
        body, name=name, grid=(1,),
        in_specs=[pl.BlockSpec((S_, POOL_W), lambda i: (0, 0)),
                  pl.BlockSpec((4, POOL_G, POOL_G), lambda i: (0, 0, 0)),
                  pl.BlockSpec((1, POOL_W), lambda i: (0, 0))],
        out_specs=pl.BlockSpec((S_, POOL_W), lambda i: (0, 0)),
        out_shape=jax.ShapeDtypeStruct((S_, POOL_W), BF16),
        compiler_params=_cparams(("arbitrary",)),
    )(z, pool_mix, pool_scale)


def _pool_bwd(name, z, dy, pool_mix, pool_scale, after=None):
    S_ = z.shape[0]
    n_chunks = S_ // SEQ_CHUNK
    rows_a = SEQ_CHUNK + HALO

    def body(u_ref, dy_ref, mix_ref, sc_ref, du_ref, dmix_ref, dsc_ref):
        for g, w in enumerate(POOL_WINDOWS):
            cols = slice(g * POOL_G, (g + 1) * POOL_G)
            mixg = mix_ref[g].astype(BF16)
            scg = sc_ref[:, cols]

            def chunk(i, carry, cols=cols, mixg=mixg, scg=scg, w=w):
                dmix_acc, dsc_acc = carry
                r0 = pl.multiple_of(i * SEQ_CHUNK, SEQ_CHUNK)
                xx = _rows_with_halo(u_ref, cols, i, n_chunks, True, False)
                t = r0 + lax.broadcasted_iota(jnp.int32, (SEQ_CHUNK, POOL_G), 0)
                d = _pool_diff(xx, w, t).astype(BF16)
                ypre = jnp.dot(d, mixg, preferred_element_type=F32)
                dyy = _rows_with_halo(dy_ref, cols, i, n_chunks, False, True)
                dys = (dyy * scg).astype(BF16)
                dsc_acc = dsc_acc + jnp.sum((dyy[:SEQ_CHUNK] * ypre).reshape(SEQ_CHUNK // 8, 8, POOL_G), axis=0)
                dmix_acc = dmix_acc + lax.dot_general(d, dys[:SEQ_CHUNK], _DIMS["tn"], preferred_element_type=F32)
                dd = lax.dot_general(dys, mixg, _DIMS["nt"], preferred_element_type=F32)
                ta = r0 + lax.broadcasted_iota(jnp.int32, (rows_a, POOL_G), 0)
                f = dd / jnp.minimum(ta + 1, w).astype(F32)
                k = 1
                while k < w:
                    f = f + _shift_up(f, k)
                    k *= 2
                du_ref[pl.ds(r0, SEQ_CHUNK), cols] = (f[:SEQ_CHUNK] - dd[:SEQ_CHUNK]).astype(BF16)
                return dmix_acc, dsc_acc

            dmix_acc, dsc_acc = lax.fori_loop(
                0, n_chunks, chunk, (jnp.zeros((POOL_G, POOL_G), F32), jnp.zeros((8, POOL_G), F32)))
            dmix_ref[g] = dmix_acc
            dsc_ref[:, cols] = jnp.sum(dsc_acc, axis=0, keepdims=True)

    full = pl.BlockSpec((S_, POOL_W), lambda i: (0, 0))
    mix_spec = pl.BlockSpec((4, POOL_G, POOL_G), lambda i: (0, 0, 0))
    vec = pl.BlockSpec((1, POOL_W), lambda i: (0, 0))
    return _call(
        body, [z, dy, pool_mix, pool_scale], [full, full, mix_spec, vec], after, name=name, grid=(1,),
        out_specs=(full, mix_spec, vec),
        out_shape=(jax.ShapeDtypeStruct((S_, POOL_W), BF16), jax.ShapeDtypeStruct((4, POOL_G, POOL_G), F32),
                   jax.ShapeDtypeStruct((1, POOL_W), F32)),
        compiler_params=_cparams(("arbitrary",)))


def _conv_specs(S_):
    slab = lambda off: pl.BlockSpec((S_, 128), lambda c, off=off: (0, off // 128 + c))
    return slab(OFF_CB), slab(OFF_CC), slab(OFF_CX)


def _conv_fwd(name, z, conv_w):
    S_ = z.shape[0]
    n_chunks = S_ // SEQ_CHUNK
    col = slice(0, 128)

    def body(b_ref, c_ref, x_ref, w_ref, y_ref):
        w0, w1, w2 = w_ref[0:1, :], w_ref[1:2, :], w_ref[2:3, :]

        def chunk(i, carry):
            r0 = pl.multiple_of(i * SEQ_CHUNK, SEQ_CHUNK)
            u = _rows_with_halo(c_ref, col, i, n_chunks, True, False) * _rows_with_halo(x_ref, col, i, n_chunks, True, False)
            y = w2 * u + w1 * _shift_down(u, 1) + w0 * _shift_down(u, 2)
            y_ref[pl.ds(r0, SEQ_CHUNK), :] = (b_ref[pl.ds(r0, SEQ_CHUNK), :] * y[HALO:]).astype(BF16)
            return carry

        lax.fori_loop(0, n_chunks, chunk, 0)

    sb, sc, sx = _conv_specs(S_)
    return pl.pallas_call(
        body, name=name, grid=(CONV_W // 128,),
        in_specs=[sb, sc, sx, pl.BlockSpec((3, 128), lambda c: (0, c))],
        out_specs=pl.BlockSpec((S_, 128), lambda c: (0, c)),
        out_shape=jax.ShapeDtypeStruct((S_, CONV_W), BF16),
        compiler_params=_cparams(("parallel",)),
    )(z, z, z, conv_w)


def _conv_bwd(name, z, dy, conv_w, after=None):
    S_ = z.shape[0]
    n_chunks = S_ // SEQ_CHUNK
    col = slice(0, 128)
    lo, hi = HALO, HALO + SEQ_CHUNK

    def body(b_ref, c_ref, x_ref, dy_ref, w_ref, db_ref, dc_ref, dx_ref, dw_ref):
        w0, w1, w2 = w_ref[0:1, :], w_ref[1:2, :], w_ref[2:3, :]

        def chunk(i, carry):
            a0, a1, a2 = carry
            r0 = pl.multiple_of(i * SEQ_CHUNK, SEQ_CHUNK)
            cc = _rows_with_halo(c_ref, col, i, n_chunks, True, True)
            xx = _rows_with_halo(x_ref, col, i, n_chunks, True, True)
            bb = _rows_with_halo(b_ref, col, i, n_chunks, True, True)
            dyy = _rows_with_halo(dy_ref, col, i, n_chunks, True, True)
            u = cc * xx
            u1 = _shift_down(u, 1)
            u2 = _shift_down(u, 2)
            y = w2 * u + w1 * u1 + w0 * u2
            dyv = dyy * bb
            du = w2 * dyv + w1 * _shift_up(dyv, 1) + w0 * _shift_up(dyv, 2)
            db_ref[pl.ds(r0, SEQ_CHUNK), :] = (dyy[lo:hi] * y[lo:hi]).astype(BF16)
            dc_ref[pl.ds(r0, SEQ_CHUNK), :] = (du[lo:hi] * xx[lo:hi]).astype(BF16)
            dx_ref[pl.ds(r0, SEQ_CHUNK), :] = (du[lo:hi] * cc[lo:hi]).astype(BF16)
            red = lambda v: jnp.sum(v.reshape(SEQ_CHUNK // 8, 8, 128), axis=0)
            dm = dyv[lo:hi]
            return a0 + red(dm * u2[lo:hi]), a1 + red(dm * u1[lo:hi]), a2 + red(dm * u[lo:hi])

        zero = jnp.zeros((8, 128), F32)
        a0, a1, a2 = lax.fori_loop(0, n_chunks, chunk, (zero, zero, zero))
        dw_ref[0:1, :] = jnp.sum(a0, axis=0, keepdims=True)
        dw_ref[1:2, :] = jnp.sum(a1, axis=0, keepdims=True)
        dw_ref[2:3, :] = jnp.sum(a2, axis=0, keepdims=True)

    sb, sc, sx = _conv_specs(S_)
    slab = pl.BlockSpec((S_, 128), lambda c: (0, c))
    wspec = pl.BlockSpec((3, 128), lambda c: (0, c))
    act = jax.ShapeDtypeStruct((S_, CONV_W), BF16)
    return _call(
        body, [z, z, z, dy, conv_w], [sb, sc, sx, slab, wspec], after, name=name, grid=(CONV_W // 128,),
        out_specs=(slab, slab, slab, wspec),
        out_shape=(act, act, act, jax.ShapeDtypeStruct((3, CONV_W), F32)),
        compiler_params=_cparams(("parallel",)))


def _head_norm(v, gain):
    rstd = lax.rsqrt(jnp.mean(v * v, axis=-1, keepdims=True) + EPS)
    xhat = v * rstd
    return xhat * gain, xhat, rstd


def _head_norm_bwd(dy, xhat, rstd, gain):
    dxhat = dy * gain
    c = jnp.mean(dxhat * xhat, axis=-1, keepdims=True)
    dv = rstd * (dxhat - xhat * c)
    dg = jnp.sum((dy * xhat).reshape(ATTN_BLK // 8, 8, HEAD), axis=0)
    return dv, dg


def _band_masks():
    qi = lax.broadcasted_iota(jnp.int32, (ATTN_BLK, ATTN_BLK), 0)
    ki = lax.broadcasted_iota(jnp.int32, (ATTN_BLK, ATTN_BLK), 1)
    return ki <= qi, ki >= qi


def _attn_specs(S_, g, dil):
    rows = ATTN_BLK * dil
    nb = S_ // rows
    cq, ck, cv = (OFF_Q // PAIR_W + g * PAIRS, OFF_K // PAIR_W + g * PAIRS, OFF_V // PAIR_W + g * PAIRS)
    return rows, nb, cq, ck, cv


PAIR_W = 128
PAIR = PAIR_W // HEAD
PAIRS = HPG // PAIR


def _rows_of(ref, r, dil):
    if dil == 1:
        return ref[...]
    return ref[pl.ds(r, ATTN_BLK, stride=dil), :]


def _put_rows(ref, r, dil, val):
    if dil == 1:
        ref[...] = val.astype(ref.dtype)
    else:
        ref[pl.ds(r, ATTN_BLK, stride=dil), :] = val.astype(ref.dtype)


def _attn_fwd(name, z, q_gain, k_gain, g, dil):
    S_ = z.shape[0]
    rows, nb, cq, ck, cv = _attn_specs(S_, g, dil)
    scale = HEAD ** -0.5

    def body(q_ref, kc_ref, kp_ref, vc_ref, vp_ref, gq_ref, gk_ref, o_ref, l_ref):
        n = pl.program_id(1)
        m_cur, m_prev = _band_masks()
        m_prev = jnp.logical_and(m_prev, n > 0)
        gq, gk = gq_ref[...], gk_ref[...]
        for r in range(dil):
            q, kc, kp = _rows_of(q_ref, r, dil), _rows_of(kc_ref, r, dil), _rows_of(kp_ref, r, dil)
            vc, vp = _rows_of(vc_ref, r, dil), _rows_of(vp_ref, r, dil)
            outs, lses = [], []
            for h in range(PAIR):
                hs = slice(h * HEAD, (h + 1) * HEAD)
                qn = _head_norm(q[:, hs], gq)[0].astype(BF16)
                kcn = _head_norm(kc[:, hs], gk)[0].astype(BF16)
                kpn = _head_norm(kp[:, hs], gk)[0].astype(BF16)
                s_c = lax.dot_general(qn, kcn, _DIMS["nt"], preferred_element_type=F32) * scale
                s_p = lax.dot_general(qn, kpn, _DIMS["nt"], preferred_element_type=F32) * scale
                s_c = jnp.where(m_cur, s_c, MASK)
                s_p = jnp.where(m_prev, s_p, MASK)
                m = jnp.maximum(jnp.max(s_c, axis=-1, keepdims=True), jnp.max(s_p, axis=-1, keepdims=True))
                p_c = jnp.exp(s_c - m)
                p_p = jnp.exp(s_p - m)
                den = jnp.sum(p_c, axis=-1, keepdims=True) + jnp.sum(p_p, axis=-1, keepdims=True)
                acc = (jnp.dot(p_c.astype(BF16), vc[:, hs].astype(BF16), preferred_element_type=F32)
                       + jnp.dot(p_p.astype(BF16), vp[:, hs].astype(BF16), preferred_element_type=F32))
                outs.append(acc / den)
                lses.append(jnp.broadcast_to(m + jnp.log(den), (ATTN_BLK, HEAD)))
            _put_rows(o_ref, r, dil, jnp.concatenate(outs, axis=1))
            _put_rows(l_ref, r, dil, jnp.concatenate(lses, axis=1))

    cur = lambda c: pl.BlockSpec((rows, PAIR_W), lambda hp, n, c=c: (n, c + hp))
    prev = lambda c: pl.BlockSpec((rows, PAIR_W), lambda hp, n, c=c: (jnp.maximum(n - 1, 0), c + hp))
    gspec = pl.BlockSpec((1, HEAD), lambda hp, n: (0, 0))
    shp = jax.ShapeDtypeStruct((S_, ATTN_OUT), F32)
    return pl.pallas_call(
        body, name=name, grid=(PAIRS, nb),
        in_specs=[cur(cq), cur(ck), prev(ck), cur(cv), prev(cv), gspec, gspec],
        out_specs=(cur(0), cur(0)), out_shape=(shp, shp),
        compiler_params=_cparams(("parallel", "parallel")),
    )(z, z, z, z, z, q_gain, k_gain)


def _attn_combine(name, os_, ls_):
    S_ = os_[0].shape[0]

    def body(o0, o1, o2, l0, l1, l2, o_ref, l_ref):
        a, b, c = l0[...], l1[...], l2[...]
        m = jnp.maximum(jnp.maximum(a, b), c)
        ea, eb, ec = jnp.exp(a - m), jnp.exp(b - m), jnp.exp(c - m)
        zsum = ea + eb + ec
        o_ref[...] = (ea * o0[...] + eb * o1[...] + ec * o2[...]) / zsum
        l_ref[...] = m + jnp.log(zsum)

    row = pl.BlockSpec((ROW_TILE, ATTN_OUT), lambda i: (i, 0))
    shp = jax.ShapeDtypeStruct((S_, ATTN_OUT), F32)
    return pl.pallas_call(
        body, name=name, grid=(S_ // ROW_TILE,), in_specs=[row] * 6, out_specs=(row, row), out_shape=(shp, shp),
        compiler_params=_cparams(("parallel",)),
    )(*os_, *ls_)


def _attn_bwd(name, z, q_gain, k_gain, do, o, lse, g, dil, after=None):
    S_ = z.shape[0]
    rows, nb, cq, ck, cv = _attn_specs(S_, g, dil)
    scale = HEAD ** -0.5

    def body(q_ref, kc_ref, kp_ref, vc_ref, vp_ref, gq_ref, gk_ref, do_ref, o_ref, l_ref,
             dq_ref, dk_ref, dv_ref, dgq_ref, dgk_ref, ck_ref, cvv_ref, gq_acc, gk_acc):
        hp = pl.program_id(0)
        n = pl.program_id(1)
        live = n < nb
        m_cur, m_prev = _band_masks()
        m_cur = jnp.logical_and(m_cur, live)
        m_prev = jnp.logical_and(m_prev, jnp.logical_and(n > 0, live))
        gq, gk = gq_ref[...], gk_ref[...]

        @pl.when(n == 0)
        def _():
            ck_ref[...] = jnp.zeros_like(ck_ref)
            cvv_ref[...] = jnp.zeros_like(cvv_ref)

        @pl.when(jnp.logical_and(n == 0, hp == 0))
        def _():
            gq_acc[...] = jnp.zeros_like(gq_acc)
            gk_acc[...] = jnp.zeros_like(gk_acc)

        for r in range(dil):
            q, kc, kp = _rows_of(q_ref, r, dil), _rows_of(kc_ref, r, dil), _rows_of(kp_ref, r, dil)
            vc, vp = _rows_of(vc_ref, r, dil), _rows_of(vp_ref, r, dil)
            dov, ov, lv = _rows_of(do_ref, r, dil), _rows_of(o_ref, r, dil), _rows_of(l_ref, r, dil)
            carry_k, carry_v = _rows_of(ck_ref, r, dil), _rows_of(cvv_ref, r, dil)
            dqs, dks, dvs, nck, ncv = [], [], [], [], []
            dgq = jnp.zeros((8, HEAD), F32)
            dgk = jnp.zeros((8, HEAD), F32)
            for h in range(PAIR):
                hs = slice(h * HEAD, (h + 1) * HEAD)
                qn, q_hat, q_rstd = _head_norm(q[:, hs], gq)
                kcn = _head_norm(kc[:, hs], gk)[0].astype(BF16)
                kpn, kp_hat, kp_rstd = _head_norm(kp[:, hs], gk)
                qn, kpn = qn.astype(BF16), kpn.astype(BF16)
                do_h = dov[:, hs]
                do_b = do_h.astype(BF16)
                delta = jnp.sum(do_h * ov[:, hs], axis=-1, keepdims=True)
                l_h = lv[:, h * HEAD:h * HEAD + 1]
                s_c = lax.dot_general(qn, kcn, _DIMS["nt"], preferred_element_type=F32) * scale
                s_p = lax.dot_general(qn, kpn, _DIMS["nt"], preferred_element_type=F32) * scale
                p_c = jnp.where(m_cur, jnp.exp(jnp.where(m_cur, s_c, MASK) - l_h), 0.0)
                p_p = jnp.where(m_prev, jnp.exp(jnp.where(m_prev, s_p, MASK) - l_h), 0.0)
                dp_c = lax.dot_general(do_b, vc[:, hs].astype(BF16), _DIMS["nt"], preferred_element_type=F32)
                dp_p = lax.dot_general(do_b, vp[:, hs].astype(BF16), _DIMS["nt"], preferred_element_type=F32)
                ds_c = (p_c * (dp_c - delta) * scale).astype(BF16)
                ds_p = (p_p * (dp_p - delta) * scale).astype(BF16)
                dqn = (jnp.dot(ds_c, kcn, preferred_element_type=F32)
                       + jnp.dot(ds_p, kpn, preferred_element_type=F32))
                dq_h, dg = _head_norm_bwd(dqn, q_hat, q_rstd, gq)
                dgq = dgq + dg
                dqs.append(dq_h)
                dkn_prev = carry_k[:, hs] + lax.dot_general(ds_p, qn, _DIMS["tn"], preferred_element_type=F32)
                dk_h, dg = _head_norm_bwd(dkn_prev, kp_hat, kp_rstd, gk)
                dgk = dgk + dg
                dks.append(dk_h)
                dvs.append(carry_v[:, hs] + lax.dot_general(p_p.astype(BF16), do_b, _DIMS["tn"],
                                                            preferred_element_type=F32))
                nck.append(lax.dot_general(ds_c, qn, _DIMS["tn"], preferred_element_type=F32))
                ncv.append(lax.dot_general(p_c.astype(BF16), do_b, _DIMS["tn"], preferred_element_type=F32))
            dq_all = jnp.concatenate(dqs, axis=1)

            @pl.when(live)
            def _(dq_all=dq_all, r=r):
                _put_rows(dq_ref, r, dil, dq_all)

            _put_rows(dk_ref, r, dil, jnp.concatenate(dks, axis=1))
            _put_rows(dv_ref, r, dil, jnp.concatenate(dvs, axis=1))
            _put_rows(ck_ref, r, dil, jnp.concatenate(nck, axis=1))
            _put_rows(cvv_ref, r, dil, jnp.concatenate(ncv, axis=1))
            gq_acc[...] += dgq
            gk_acc[...] += dgk

        @pl.when(jnp.logical_and(n == nb, hp == PAIRS - 1))
        def _():
            dgq_ref[...] = jnp.sum(gq_acc[...], axis=0, keepdims=True)
            dgk_ref[...] = jnp.sum(gk_acc[...], axis=0, keepdims=True)

    last = nb - 1
    cur = lambda c: pl.BlockSpec((rows, PAIR_W), lambda hp, n, c=c: (jnp.minimum(n, last), c + hp))
    prev = lambda c: pl.BlockSpec((rows, PAIR_W), lambda hp, n, c=c: (jnp.maximum(n - 1, 0), c + hp))
    gspec = pl.BlockSpec((1, HEAD), lambda hp, n: (0, 0))
    act = jax.ShapeDtypeStruct((S_, ATTN_OUT), F32)
    vec = jax.ShapeDtypeStruct((1, HEAD), F32)
    return _call(
        body, [z, z, z, z, z, q_gain, k_gain, do, o, lse],
        [cur(cq), cur(ck), prev(ck), cur(cv), prev(cv), gspec, gspec, cur(0), cur(0), cur(0)], after,
        name=name, grid=(PAIRS, nb + 1),
        out_specs=(cur(0), prev(0), prev(0), gspec, gspec),
        out_shape=(act, act, act, vec, vec),
        scratch_shapes=[pltpu.VMEM((rows, PAIR_W), F32), pltpu.VMEM((rows, PAIR_W), F32),
                        pltpu.VMEM((8, HEAD), F32), pltpu.VMEM((8, HEAD), F32)],
        compiler_params=_cparams(("arbitrary", "arbitrary")))


MIX_TN = 256


def _sigmoid(v):
    return 1.0 / (1.0 + jnp.exp(-v))


def _mix_fwd(name, z, b_gate, ys, ws):
    S_ = z.shape[0]
    tm, tn = ROW_TILE, MIX_TN
    gblk = OFF_GATE // tn

    def body(yp, yc, ya, wp, wc, wa, g0, g1, g2, b0, b1, b2, m_ref):
        acc = None
        for y_ref, w_ref, g_ref, b_ref in ((yp, wp, g0, b0), (yc, wc, g1, b1), (ya, wa, g2, b2)):
            u = lax.dot_general(y_ref[...].astype(BF16), w_ref[...], _DIMS["nt"], preferred_element_type=F32)
            t = _sigmoid(g_ref[...] + b_ref[...]) * u
            acc = t if acc is None else acc + t
        m_ref[...] = acc.astype(BF16)

    yspec = lambda w: pl.BlockSpec((tm, w), lambda i, j: (i, 0))
    wspec = lambda w: pl.BlockSpec((tn, w), lambda i, j: (j, 0))
    gspec = lambda b: pl.BlockSpec((tm, tn), lambda i, j, b=b: (i, gblk + b * (D // tn) + j))
    bspec = lambda b: pl.BlockSpec((1, tn), lambda i, j, b=b: (0, b * (D // tn) + j))
    return pl.pallas_call(
        body, name=name, grid=(S_ // tm, D // tn),
        in_specs=[yspec(POOL_W), yspec(CONV_W), yspec(ATTN_OUT), wspec(POOL_W), wspec(CONV_W), wspec(ATTN_OUT),
                  gspec(0), gspec(1), gspec(2), bspec(0), bspec(1), bspec(2)],
        out_specs=pl.BlockSpec((tm, tn), lambda i, j: (i, j)),
        out_shape=jax.ShapeDtypeStruct((S_, D), BF16),
        compiler_params=_cparams(("parallel", "parallel")),
    )(*ys, *ws, z, z, z, b_gate, b_gate, b_gate)


def _mix_bwd(name, z, b_gate, y, w, dmerged, branch, after=None):
    S_ = z.shape[0]
    tm, tn = ROW_TILE, MIX_TN
    width = y.shape[1]
    gblk = OFF_GATE // tn + branch * (D // tn)
    ni = S_ // tm

    def body(y_ref, w_ref, g_ref, b_ref, dm_ref, du_ref, dg_ref, db_ref, acc_ref):
        i = pl.program_id(1)
        u = lax.dot_general(y_ref[...].astype(BF16), w_ref[...], _DIMS["nt"], preferred_element_type=F32)
        sg = _sigmoid(g_ref[...] + b_ref[...])
        dm = dm_ref[...]
        du_ref[...] = (sg * dm).astype(BF16)
        dpre = dm * u * sg * (1.0 - sg)
        dg_ref[...] = dpre.astype(BF16)
        part = jnp.sum(dpre.reshape(tm // 8, 8, tn), axis=0)

        @pl.when(i == 0)
        def _():
            acc_ref[...] = part

        @pl.when(i > 0)
        def _():
            acc_ref[...] += part

        @pl.when(i == ni - 1)
        def _():
            db_ref[...] = jnp.sum(acc_ref[...], axis=0, keepdims=True)

    blk = pl.BlockSpec((tm, tn), lambda j, i: (i, j))
    vec = pl.BlockSpec((1, tn), lambda j, i: (0, j))
    act = jax.ShapeDtypeStruct((S_, D), BF16)
    return _call(
        body, [y, w, z, b_gate, dmerged],
        [pl.BlockSpec((tm, width), lambda j, i: (i, 0)), pl.BlockSpec((tn, width), lambda j, i: (j, 0)),
         pl.BlockSpec((tm, tn), lambda j, i: (i, gblk + j)),
         pl.BlockSpec((1, tn), lambda j, i: (0, branch * (D // tn) + j)), blk], after,
        name=name, grid=(D // tn, ni),
        out_specs=(blk, blk, vec), out_shape=(act, act, jax.ShapeDtypeStruct((1, D), F32)),
        scratch_shapes=[pltpu.VMEM((8, tn), F32)],
        compiler_params=_cparams(("parallel", "arbitrary")))


def _relu2_epilogue(acc):
    r = jnp.maximum(acc, 0.0)
    return acc, r * r


def _layer_fwd(l, x, p):
    t = f"l{l}_"
    h = _rmsnorm_fwd(t + "norm_mix", x, p["norm_mix"])
    z = _matmul(t + "in_proj", h, p["w_in_t"], "nt", 512, 3712, 1024)
    y_pool = _pool_fwd(t + "pool", z, p["pool_mix"], p["pool_scale"])
    y_conv = _conv_fwd(t + "conv", z, p["conv_w"])
    os_, ls_ = [], []
    for g, (_, dil) in enumerate(ATTN_GROUPS):
        o_g, l_g = _attn_fwd(t + f"attn{g}", z, p["q_gain"], p["k_gain"], g, dil)
        os_.append(o_g)
        ls_.append(l_g)
    y_attn, lse = _attn_combine(t + "attn_mix", os_, ls_)
    merged = _mix_fwd(t + "merge", z, p["b_gate"], (y_pool, y_conv, y_attn),
                      (p["w_pool_up_t"], p["w_conv_out_t"], p["w_attn_up_t"]))
    x1 = _matmul(t + "out_proj", merged, p["w_o"], "nn", 1024, 1024, 1024,
                 extras=((x, "mn"),), epilogue=lambda acc, xr: (xr + acc,))
    h2 = _rmsnorm_fwd(t + "norm_mlp", x1, p["norm_mlp"])
    a, r = _matmul(t + "ff1", h2, p["w_ff1_t"], "nt", 1024, 1024, 1024, out_dtypes=(F32, BF16),
                   epilogue=_relu2_epilogue)
    x2 = _matmul(t + "ff2", r, p["w_ff2"], "nn", 1024, 1024, 1024,
                 extras=((x1, "mn"),), epilogue=lambda acc, xr: (xr + acc,))
    saved = dict(x=x, h=h, z=z, y_pool=y_pool, y_conv=y_conv, y_attn=y_attn, lse=lse, merged=merged,
                 x1=x1, h2=h2, a=a, r=r)
    return x2, saved


def _layer_bwd(l, dx2, p, s, pending, collective_id, last, start_after=None):
    t = f"l{l}_b_"
    g = {}
    rest = jax.ShapeDtypeStruct((N_DEV, REST_ROWS, D), F32)
    da = _matmul(t + "d_ff2_in", dx2, p["w_ff2"], "nt", 1024, 1024, 1024, out_dtypes=(BF16,),
                 extras=((s["a"], "mn"),), epilogue=lambda acc, a: (acc * (2.0 * jnp.maximum(a, 0.0)),),
                 after=start_after)
    rest = _matmul(t + "dw_ff2", s["r"], dx2, "tn", 512, 1024, 4096, into=(rest, REST_FF2, 0), after=da)
    tok = rest if pending is None else pending.chip_sums(after=rest)
    dh2 = _matmul(t + "d_ff1_in", da, p["w_ff1_t"], "nn", 1024, 1024, 1024, after=tok)
    rest = _matmul(t + "dw_ff1", da, s["h2"], "tn", 512, 1024, 4096, into=(rest, REST_FF1, 0), after=dh2)
    dx1, g["norm_mlp"] = _rmsnorm_bwd(t + "norm_mlp", s["x1"], p["norm_mlp"], dh2, dx2, after=rest)
    dmerged = _matmul(t + "d_out_proj_in", dx1, p["w_o"], "nt", 1024, 1024, 1024)
    rest = _matmul(t + "dw_o", s["merged"], dx1, "tn", 128, 1024, 4096, into=(rest, REST_WO, 0), after=dmerged)
    ys = (s["y_pool"], s["y_conv"], s["y_attn"])
    names = ("w_pool_up_t", "w_conv_out_t", "w_attn_up_t")
    dys, dgates, dbs = [], [], []
    tok = rest
    for b in range(3):
        du, dgz, db = _mix_bwd(t + f"merge{b}", s["z"], p["b_gate"], ys[b], p[names[b]], dmerged, b, after=tok)
        width = ys[b].shape[1]
        dys.append(_matmul(t + f"d_up{b}_in", du, p[names[b]], "nn", 512, width, 1024))
        if b < 2:
            rest = _matmul(t + f"dw_up{b}", du, ys[b], "tn", 128, width, 4096, into=(rest, REST_UP, b * width),
                           after=dys[b])
            tok = rest
        else:
            d_attn_up_t = _matmul(t + f"dw_up{b}", du, ys[b], "tn", 128, width, 4096, after=dys[b])
            rest = rest.at[:, REST_ATTN:REST_ROWS, :].set(d_attn_up_t.reshape(N_DEV, REST_ROWS - REST_ATTN, D))
        dgates.append(dgz)
        dbs.append(db)
    g["b_gate"] = jnp.concatenate(dbs, axis=1)
    rs_rest = _ReduceScatter(f"rs_rest{l}", collective_id, rest)
    pending_sum = None if pending is None else pending.finish(after=rest)
    du_pool, g["pool_mix"], g["pool_scale"] = _pool_bwd(t + "pool", s["z"], dys[0], p["pool_mix"], p["pool_scale"],
                                                        after=rest if pending is None else pending_sum[0])
    dcb, dcc, dcx, g["conv_w"] = _conv_bwd(t + "conv", s["z"], dys[1], p["conv_w"], after=du_pool)
    tok = rs_rest.chip_sums(after=dcb)
    dqs, dks, dvs = [], [], []
    gq = gk = None
    for gi, (_, dil) in enumerate(ATTN_GROUPS):
        dq, dk, dv, dgq, dgk = _attn_bwd(t + f"attn{gi}", s["z"], p["q_gain"], p["k_gain"], dys[2], s["y_attn"],
                                         s["lse"], gi, dil, after=tok)
        tok = dq
        dqs.append(dq)
        dks.append(dk)
        dvs.append(dv)
        gq = dgq if gq is None else gq + dgq
        gk = dgk if gk is None else gk + dgk
    g["q_gain"], g["k_gain"] = gq, gk
    dz = jnp.concatenate([du_pool, dcb, dcc, dcx] + [v.astype(BF16) for v in dqs + dks + dvs] + dgates, axis=1)
    in_t = _matmul(t + "dw_in", dz, s["h"], "tn", 256, 1024, 4096)
    rs_in = _ReduceScatter(f"rs_in{l}", collective_id + 2, in_t.reshape(N_DEV, IN_ROWS, D))
    dh = _matmul(t + "d_in_proj_in", dz, p["w_in_t"], "nn", 512, 1024, 3712, after=in_t)
    tok = rs_in.chip_sums(after=dh) if last else dh
    rest_sum, _ = rs_rest.finish(after=tok)
    dx, g["norm_mix"] = _rmsnorm_bwd(t + "norm_mix", s["x"], p["norm_mix"], dh, dx1, after=rest_sum)
    return dx, g, rest_sum, pending_sum, rs_in


ANY = pl.BlockSpec(memory_space=pl.ANY)


def _mesh_pos():
    return lax.axis_index("x"), lax.axis_index("y"), lax.axis_index("c")


def _other_chips(x, y):
    return [(1 - x, y), (x, 1 - y), (1 - x, 1 - y)]


def _comm_call(name, collective_id, peers, body, arrs, out_shape, sem_counts):
    n_in, n_out = len(arrs), len(out_shape)
    if collective_id is None:
        def tc_body(*refs):
            body(refs[:n_in], refs[n_in:n_in + n_out], *refs[n_in + n_out:])

        return pl.pallas_call(
            tc_body, name=name, out_shape=tuple(out_shape), in_specs=[ANY] * n_in, out_specs=(ANY,) * n_out,
            scratch_shapes=[pltpu.SemaphoreType.DMA((n,)) for n in sem_counts])(*arrs)

    def seq_body(*refs):
        barrier = pltpu.get_barrier_semaphore()
        ps = peers(*_mesh_pos())
        for p in ps:
            pl.semaphore_signal(barrier, inc=1, device_id=p, device_id_type=MESH)
        pl.semaphore_wait(barrier, len(ps))
        body(refs[:n_in], refs[n_in:n_in + n_out], *refs[n_in + n_out:])

    return pl.kernel(
        seq_body, out_type=tuple(out_shape), mesh=plsc.ScalarSubcoreMesh(axis_name="seq", num_cores=1), name=name,
        scratch_types=[pltpu.SemaphoreType.DMA((n,)) for n in sem_counts],
        compiler_params=pltpu.CompilerParams(collective_id=collective_id),
    )(*arrs)


def _all_gather(name, collective_id, shard):
    R, C = shard.shape

    def peers(x, y, c):
        return [(x, y, 1 - c)] + [(*chip, c) for chip in _other_chips(x, y)]

    def body(in_refs, out_refs, send_sems, recv_sems, local_sems):
        (x_ref,), (out_ref,) = in_refs, out_refs
        x, y, c = _mesh_pos()
        me, sibling = (x, y, c), (x, y, 1 - c)
        chips = _other_chips(x, y)

        def slot(px, py, pc):
            return out_ref.at[4 * px + 2 * py + pc]

        def copy(k, block, to, src=None):
            return pltpu.make_async_remote_copy(
                src_ref=slot(*block) if src is None else src, dst_ref=slot(*block),
                send_sem=send_sems.at[k], recv_sem=recv_sems.at[k], device_id=to, device_id_type=MESH)

        mine = pltpu.make_async_copy(x_ref, slot(*me), local_sems.at[0])
        mine.start()
        first = [copy(0, me, sibling, src=x_ref)]
        first += [copy(1 + j, me, (*chip, c), src=x_ref) for j, chip in enumerate(chips)]
        for cp in first:
            cp.start()
        passed = [copy(4 + j, (*chip, c), sibling) for j, chip in enumerate(chips)]
        for j, chip in enumerate(chips):
            copy(1 + j, (*chip, c), me).wait_recv()
            passed[j].start()
        copy(0, sibling, me).wait_recv()
        for j, chip in enumerate(chips):
            copy(4 + j, (*chip, 1 - c), me).wait_recv()
        for cp in first + passed:
            cp.wait_send()
        mine.wait()

    return _comm_call(name, collective_id, peers, body, [shard],
                      [jax.ShapeDtypeStruct((N_DEV, R, C), shard.dtype)], (7, 7, 1))[0]


def _rs_sibling_exchange(name, collective_id, arrs):
    n = len(arrs)

    def body(in_refs, out_refs, send_sems, recv_sems):
        x, y, c = _mesh_pos()
        cps = []
        for k, (src, dst) in enumerate(zip(in_refs, out_refs)):
            src = src.at[:, 1 - c] if len(src.shape) == 4 else src
            cps.append(pltpu.make_async_remote_copy(src_ref=src, dst_ref=dst, send_sem=send_sems.at[k],
                                                    recv_sem=recv_sems.at[k], device_id=(x, y, 1 - c),
                                                    device_id_type=MESH))
        for cp in cps:
            cp.start()
        for cp in cps:
            cp.wait()

    out_shape = [jax.ShapeDtypeStruct(a.shape[:1] + a.shape[2:] if a.ndim == 4 else a.shape, a.dtype) for a in arrs]
    return _comm_call(name, collective_id, lambda x, y, c: [(x, y, 1 - c)], body, arrs, out_shape, (n, n))


def _rs_rows(r):
    return r // 2 if (r // 2) % 16 == 0 else r


def _drop_ref(body, idx):
    def wrapped(*refs):
        return body(*refs[:idx], *refs[idx + 1:])

    return wrapped


def _rs_chip_sum(name, ids, big, rbig, after=None):
    _, _, R, C = big.shape
    rows = _rs_rows(R)

    def body(ids_ref, mine_ref, theirs_ref, t16_ref, own_ref):
        p = pl.program_id(1)
        s = mine_ref[...] + theirs_ref[...]
        t16_ref[...] = s.astype(BF16)

        @pl.when(p == ids_ref[1])
        def _():
            own_ref[...] = s

    in_specs = [pl.BlockSpec((None, None, rows, C), lambda i, p, ids: (p, ids[0], i, 0)),
                pl.BlockSpec((None, rows, C), lambda i, p, ids: (p, i, 0))]
    args = [ids, big, rbig]
    if after is not None:
        body = _drop_ref(body, len(args))
        in_specs.append(ANY)
        args.append(after)
    grid_spec = pltpu.PrefetchScalarGridSpec(
        num_scalar_prefetch=1, grid=(R // rows, 4), in_specs=in_specs,
        out_specs=(pl.BlockSpec((None, rows, C), lambda i, p, ids: (p, i, 0)),
                   pl.BlockSpec((rows, C), lambda i, p, ids: (i, 0))))
    return pl.pallas_call(
        body, name=name, grid_spec=grid_spec,
        out_shape=(jax.ShapeDtypeStruct((4, R, C), BF16), jax.ShapeDtypeStruct((R, C), F32)),
        compiler_params=_cparams(("parallel", "arbitrary")),
    )(*args)


def _add2(name, a, b):
    def body(a_ref, b_ref, o_ref):
        o_ref[...] = a_ref[...] + b_ref[...]

    full = pl.BlockSpec(a.shape, lambda i: (0, 0))
    return pl.pallas_call(body, name=name, grid=(1,), in_specs=[full, full], out_specs=full,
                          out_shape=jax.ShapeDtypeStruct(a.shape, a.dtype))(a, b)


def _rs_chip_exchange(name, collective_id, arrs):
    n = len(arrs)

    def body(in_refs, out_refs, send_sems, recv_sems, local_sems):
        x, y, c = _mesh_pos()
        p_me = 2 * x + y
        chips = _other_chips(x, y)

        def part(ref, p):
            return ref.at[p] if len(ref.shape) == 3 else ref

        local = [pltpu.make_async_copy(part(src, p_me), dst.at[p_me], local_sems.at[k])
                 for k, (src, dst) in enumerate(zip(in_refs, out_refs))]
        for cp in local:
            cp.start()
        sends = []
        for j, (px, py) in enumerate(chips):
            for k, (src, dst) in enumerate(zip(in_refs, out_refs)):
                sends.append(pltpu.make_async_remote_copy(
                    src_ref=part(src, 2 * px + py), dst_ref=dst.at[p_me], send_sem=send_sems.at[n * j + k],
                    recv_sem=recv_sems.at[n * j + k], device_id=(px, py, c), device_id_type=MESH))
        for cp in sends:
            cp.start()
        for j, (px, py) in enumerate(chips):
            for k, (src, dst) in enumerate(zip(in_refs, out_refs)):
                pltpu.make_async_remote_copy(
                    src_ref=part(src, p_me), dst_ref=dst.at[2 * px + py], send_sem=send_sems.at[n * j + k],
                    recv_sem=recv_sems.at[n * j + k], device_id=(px, py, c), device_id_type=MESH).wait_recv()
        for cp in sends:
            cp.wait_send()
        for cp in local:
            cp.wait()

    out_shape = [jax.ShapeDtypeStruct((4,) + a.shape[-2:], a.dtype) for a in arrs]
    return _comm_call(name, collective_id, lambda x, y, c: [(*chip, c) for chip in _other_chips(x, y)], body, arrs,
                      out_shape, (3 * n, 3 * n, n))


def _rs_final_sum(name, ids, recv, own, rows, after=None):
    _, R, C = recv.shape
    assert R % rows == 0

    def body(ids_ref, r_ref, own_ref, o_ref):
        acc = None
        for p in range(4):
            term = jnp.where(ids_ref[1] == p, own_ref[...], r_ref[p].astype(F32))
            acc = term if acc is None else acc + term
        o_ref[...] = acc

    in_specs = [pl.BlockSpec((4, rows, C), lambda i, ids: (0, i, 0)), pl.BlockSpec((rows, C), lambda i, ids: (i, 0))]
    args = [ids, recv, own]
    if after is not None:
        body = _drop_ref(body, len(args))
        in_specs.append(ANY)
        args.append(after)
    grid_spec = pltpu.PrefetchScalarGridSpec(
        num_scalar_prefetch=1, grid=(R // rows,), in_specs=in_specs,
        out_specs=pl.BlockSpec((rows, C), lambda i, ids: (i, 0)))
    return pl.pallas_call(
        body, name=name, grid_spec=grid_spec, out_shape=jax.ShapeDtypeStruct((R, C), F32),
        compiler_params=_cparams(("parallel",)),
    )(*args)


class _ReduceScatter:
    def __init__(self, tag, collective_id, big, small=None):
        x, y, c = _mesh_pos()
        self.tag, self.cid, self.small = tag, collective_id, small
        self.ids = jnp.stack([c, 2 * x + y]).astype(jnp.int32)
        self.big = big.reshape((4, 2) + big.shape[1:])
        self.got = _rs_sibling_exchange(tag + "_sibling", collective_id,
                                        [self.big] + ([] if small is None else [small]))

    def chip_sums(self, after=None):
        t16, self.own = _rs_chip_sum(self.tag + "_chip_sum", self.ids, self.big, self.got[0], after)
        arrs = [t16]
        if self.small is not None:
            self.ts = _add2(self.tag + "_chip_sum_small", self.small, self.got[1])
            arrs.append(self.ts)
        self.recv = _rs_chip_exchange(self.tag + "_chips", self.cid + 1, arrs)
        return t16

    def finish(self, after=None):
        out = _rs_final_sum(self.tag + "_final", self.ids, self.recv[0], self.own, _rs_rows(self.own.shape[0]), after)
        out_small = None
        if self.small is not None:
            out_small = _rs_final_sum(self.tag + "_final_small", self.ids, self.recv[1], self.ts, self.small.shape[0])
        return out, out_small


def _all_reduce_small(tag, small):
    x, y, c = _mesh_pos()
    ids = jnp.stack([c, 2 * x + y]).astype(jnp.int32)
    (theirs,) = _rs_sibling_exchange(tag + "_sibling", None, [small])
    ts = _add2(tag + "_chip_sum", small, theirs)
    (recv,) = _rs_chip_exchange(tag + "_chips", None, [ts])
    return _rs_final_sum(tag + "_final", ids, recv, ts, small.shape[0])


def _adamw(name, w, g, m, v):
    R, C = w.shape
    tr = max(t for t in range(8, 513, 8) if R % t == 0)
    c1 = 1.0 - ADAM_B1 ** ADAM_STEP
    c2 = 1.0 - ADAM_B2 ** ADAM_STEP

    def body(w_ref, g_ref, m_ref, v_ref, d_ref, nm_ref, nv_ref):
        gv = g_ref[...]
        nm = ADAM_B1 * m_ref[...] + (1.0 - ADAM_B1) * gv
        nv = ADAM_B2 * v_ref[...] + (1.0 - ADAM_B2) * (gv * gv)
        d_ref[...] = -ADAM_LR * ((nm / c1) / (jnp.sqrt(nv / c2) + ADAM_EPS) + ADAM_WD * w_ref[...])
        nm_ref[...] = nm
        nv_ref[...] = nv

    blk = pl.BlockSpec((tr, C), lambda i: (i, 0))
    shp = jax.ShapeDtypeStruct((R, C), F32)
    return pl.pallas_call(
        body, name=name, grid=(R // tr,), in_specs=[blk] * 4, out_specs=(blk,) * 3, out_shape=(shp,) * 3,
        compiler_params=_cparams(("parallel",)),
    )(w, g, m, v)


def _adamw_t(name, w, g_t, m, v):
    L, R, C = w.shape
    tr = 256
    c1 = 1.0 - ADAM_B1 ** ADAM_STEP
    c2 = 1.0 - ADAM_B2 ** ADAM_STEP

    def body(w_ref, g_ref, m_ref, v_ref, go_ref, d_ref, nm_ref, nv_ref):
        gv = g_ref[...].T
        go_ref[...] = gv
        nm = ADAM_B1 * m_ref[...] + (1.0 - ADAM_B1) * gv
        nv = ADAM_B2 * v_ref[...] + (1.0 - ADAM_B2) * (gv * gv)
        d_ref[...] = -ADAM_LR * ((nm / c1) / (jnp.sqrt(nv / c2) + ADAM_EPS) + ADAM_WD * w_ref[...])
        nm_ref[...] = nm
        nv_ref[...] = nv

    blk = pl.BlockSpec((None, tr, C), lambda l, i: (l, i, 0))
    blk_t = pl.BlockSpec((None, C, tr), lambda l, i: (l, 0, i))
    shp = jax.ShapeDtypeStruct((L, R, C), F32)
    return pl.pallas_call(
        body, name=name, grid=(L, R // tr), in_specs=[blk, blk_t, blk, blk], out_specs=(blk,) * 4,
        out_shape=(shp,) * 4, compiler_params=_cparams(("parallel", "parallel")),
    )(w, g_t, m, v)


REST_FF1 = 0
REST_FF2 = 512
REST_WO = 1024
REST_UP = 1152
REST_ATTN = 1280
REST_ROWS = 1312
IN_ROWS = IN_COLS // N_DEV
CONV_BITS_ROWS = 16
SHARD_ROWS = IN_ROWS + REST_ROWS + CONV_BITS_ROWS
SMALL = (("norm_mix", (DEPTH, D)), ("b_gate", (DEPTH, 3 * D)), ("pool_mix", (DEPTH, 4, POOL_G, POOL_G)),
         ("pool_scale", (DEPTH, POOL_W)), ("conv_w", (DEPTH, 3, CONV_W)), ("q_gain", (DEPTH, HEAD)),
         ("k_gain", (DEPTH, HEAD)), ("norm_mlp", (DEPTH, D)))


def _pack_weight_shards(w, l):
    b = lambda a: a.astype(BF16)
    conv = lax.bitcast_convert_type(w["conv_w"][l], BF16).reshape(3, 128)
    conv = jnp.pad(conv, ((0, CONV_BITS_ROWS - 3), (0, D - 128)))
    rest = jnp.concatenate([
        b(w["w_ff1"][l].T), b(w["w_ff2"][l]), b(w["w_o"][l]),
        jnp.concatenate([b(w["w_pool_up"][l].T), b(w["w_conv_out"][l].T)], axis=1),
        b(w["w_attn_up"][l].T).reshape(REST_ROWS - REST_ATTN, D), conv], axis=0)
    return b(w["w_in"][l].T), rest


def _unpack_gathered(g_in, g_rest, small_w, l):
    rest = g_rest[:, :REST_ROWS]
    take = lambda r0, rows, c0=0, cols=D: rest[:, r0:r0 + rows, c0:c0 + cols].reshape(N_DEV * rows, cols)
    conv = g_rest[:, REST_ROWS:REST_ROWS + 3, :128].reshape(N_DEV, 3, CONV_W // N_DEV, 2)
    conv = lax.bitcast_convert_type(conv, F32)
    p = {
        "w_in_t": g_in.reshape(IN_COLS, D),
        "w_ff1_t": take(REST_FF1, 512), "w_ff2": take(REST_FF2, 512), "w_o": take(REST_WO, 128),
        "w_pool_up_t": take(REST_UP, 128, 0, POOL_W), "w_conv_out_t": take(REST_UP, 128, POOL_W, CONV_W),
        "w_attn_up_t": rest[:, REST_ATTN:].reshape(D, ATTN_OUT),
        "conv_w": jnp.transpose(conv, (1, 0, 2)).reshape(3, CONV_W),
        "pool_mix": small_w["pool_mix"][l],
    }
    for name in ("norm_mix", "b_gate", "pool_scale", "q_gain", "k_gain", "norm_mlp"):
        p[name] = small_w[name][l][None, :]
    return p


def _pack_small_grads(grads):
    flat = jnp.concatenate([jnp.stack([grads[l][name] for l in range(DEPTH)]).reshape(-1) for name, _ in SMALL])
    return jnp.pad(flat, (0, (-flat.shape[0]) % (8 * 128))).reshape(-1, 128)


def _unpack_grads(in_t, rest, small, dev):
    out = {
        "w_in_t": jnp.stack(in_t),
        "w_ff1_t": jnp.stack([a[REST_FF1:REST_FF1 + 512] for a in rest]),
        "w_ff2": jnp.stack([a[REST_FF2:REST_FF2 + 512] for a in rest]),
        "w_o": jnp.stack([a[REST_WO:REST_WO + 128] for a in rest]),
        "w_pool_up": jnp.stack([a[REST_UP:REST_UP + 128, :POOL_W].T for a in rest]),
        "w_conv_out": jnp.stack([a[REST_UP:REST_UP + 128, POOL_W:].T for a in rest]),
        "w_attn_up": jnp.stack([a[REST_ATTN:].reshape(D // N_DEV, ATTN_OUT).T for a in rest]),
    }
    flat = small.reshape(-1)
    off = 0
    for name, shp in SMALL:
        n = 1
        for s in shp:
            n *= s
        out[name] = flat[off:off + n].reshape(shp)
        off += n
    width = CONV_W // N_DEV
    out["conv_w"] = lax.dynamic_slice_in_dim(out["conv_w"], dev * width, width, axis=2)
    return out


def _pack_small(arrs):
    flat = jnp.concatenate([a.reshape(-1) for a in arrs])
    pad = (-flat.shape[0]) % (8 * 128)
    return jnp.pad(flat, (0, pad)).reshape(-1, 128)


def _unpack_small(packed, like):
    flat = packed.reshape(-1)
    out, off = [], 0
    for a in like:
        out.append(flat[off:off + a.size].reshape(a.shape))
        off += a.size
    return out


WEIGHTS = ("norm_mix", "w_in", "b_gate", "pool_mix", "pool_scale", "conv_w", "q_gain", "k_gain", "w_pool_up",
           "w_conv_out", "w_attn_up", "w_o", "norm_mlp", "w_ff1", "w_ff2")


def kernel(x, norm_mix, w_in, b_gate, pool_mix, pool_scale, conv_w, q_gain, k_gain, w_pool_up, w_conv_out, w_attn_up, w_o, norm_mlp, w_ff1, w_ff2, loss_target, m_norm_mix, m_w_in, m_b_gate, m_pool_mix, m_pool_scale, m_conv_w, m_q_gain, m_k_gain, m_w_pool_up, m_w_conv_out, m_w_attn_up, m_w_o, m_norm_mlp, m_w_ff1, m_w_ff2, v_norm_mix, v_w_in, v_b_gate, v_pool_mix, v_pool_scale, v_conv_w, v_q_gain, v_k_gain, v_w_pool_up, v_w_conv_out, v_w_attn_up, v_w_o, v_norm_mlp, v_w_ff1, v_w_ff2):
    w = dict(zip(WEIGHTS, (norm_mix, w_in, b_gate, pool_mix, pool_scale, conv_w, q_gain, k_gain, w_pool_up,
                           w_conv_out, w_attn_up, w_o, norm_mlp, w_ff1, w_ff2)))
    m = dict(zip(WEIGHTS, (m_norm_mix, m_w_in, m_b_gate, m_pool_mix, m_pool_scale, m_conv_w, m_q_gain, m_k_gain,
                           m_w_pool_up, m_w_conv_out, m_w_attn_up, m_w_o, m_norm_mlp, m_w_ff1, m_w_ff2)))
    v = dict(zip(WEIGHTS, (v_norm_mix, v_w_in, v_b_gate, v_pool_mix, v_pool_scale, v_conv_w, v_q_gain, v_k_gain,
                           v_w_pool_up, v_w_conv_out, v_w_attn_up, v_w_o, v_norm_mlp, v_w_ff1, v_w_ff2)))
    xi, yi, ci = _mesh_pos()
    dev = 4 * xi + 2 * yi + ci

    params = []
    for l in range(DEPTH):
        s_in, s_rest = _pack_weight_shards(w, l)
        g_in = _all_gather(f"gather_in{l}", 1 + 2 * l, s_in)
        g_rest = _all_gather(f"gather_rest{l}", 2 + 2 * l, s_rest)
        params.append(_unpack_gathered(g_in, g_rest, w, l))
    saved = []
    act = x[0]
    for l in range(DEPTH):
        act, s = _layer_fwd(l, act, params[l])
        saved.append(s)
    dy, loss = _loss_head("loss_head", act, loss_target[0])
    total = lax.psum(loss[0, 0], ("x", "y", "c"))
    in_t, rest, grads, pending = [None] * DEPTH, [None] * DEPTH, [None] * DEPTH, None
    start = total.reshape(1, 1)
    for l in reversed(range(DEPTH)):
        dy, grads[l], rest[l], done, pending = _layer_bwd(l, dy, params[l], saved[l], pending, 5 + 4 * l, l == 0,
                                                          start)
        start = None
        if done is not None:
            in_t[l + 1] = done[0]
    dx = dy
    g_small = _all_reduce_small("ar_small", _pack_small_grads(grads))
    in_t[0], _ = pending.finish(after=g_small)
    g = _unpack_grads(in_t, rest, g_small, dev)

    delta, new_m, new_v = {}, {}, {}
    for name in ("w_in", "w_ff1"):
        g[name], delta[name], new_m[name], new_v[name] = _adamw_t("adamw_" + name, w[name], g[name + "_t"],
                                                                  m[name], v[name])
    for name in ("w_pool_up", "w_conv_out", "w_attn_up", "w_o", "w_ff2"):
        shp = w[name].shape
        two_d = (shp[0] * shp[1], shp[2])
        d_, m_, v_ = _adamw("adamw_" + name, w[name].reshape(two_d), g[name].reshape(two_d),
                            m[name].reshape(two_d), v[name].reshape(two_d))
        delta[name], new_m[name], new_v[name] = d_.reshape(shp), m_.reshape(shp), v_.reshape(shp)
    small_names = [name for name, _ in SMALL]
    packed = [_pack_small([src[name] for name in small_names]) for src in (w, g, m, v)]
    outs = _adamw("adamw_small", *packed)
    for dst, arr in zip((delta, new_m, new_v), outs):
        for name, val in zip(small_names, _unpack_small(arr, [w[name] for name in small_names])):
            dst[name] = val

    return (total, dx[None], *[g[n] for n in WEIGHTS], *[delta[n] for n in WEIGHTS],
            *[new_m[n] for n in WEIGHTS], *[new_v[n] for n in WEIGHTS])
```

```python
import functools

import jax
import jax.numpy as jnp
from jax import lax
from jax.experimental import pallas as pl
from jax.experimental.pallas import tpu as pltpu
from jax.experimental.pallas import tpu_sc as plsc

F32 = jnp.float32
BF16 = jnp.bfloat16
MESH = pl.DeviceIdType.MESH

D = 1024
SEQ = 4096
DEPTH = 2
N_DEV = 8
POOL_WINDOWS = (2, 4, 8, 16)
POOL_W = 512
POOL_G = 128
CONV_W = 512
HEAD = 64
ATTN_GROUPS = ((128, 1), (512, 4), (2048, 16))
HPG = 4
ATTN_W = 768
ATTN_OUT = 256
ATTN_BLK = 128
D_FF = 4096
EPS = 1e-6
MASK = -1e30
OFF_POOL = 0
OFF_CB = 512
OFF_CC = 1024
OFF_CX = 1536
OFF_Q = 2048
OFF_K = 2816
OFF_V = 3584
OFF_GATE = 4352
IN_COLS = 7424
ADAM_LR = 0.001
ADAM_B1 = 0.9
ADAM_B2 = 0.999
ADAM_EPS = 1e-08
ADAM_WD = 0.01
ADAM_STEP = 10

ROW_TILE = 512
SEQ_CHUNK = 256
HALO = 16
VMEM_LIMIT = 56 * 1024 * 1024


def _cparams(sem=None):
    return pltpu.CompilerParams(dimension_semantics=sem, vmem_limit_bytes=VMEM_LIMIT)


def _call(body, args, in_specs, after=None, **kw):
    if after is not None:
        n, inner = len(args), body

        def body(*refs):
            return inner(*refs[:n], *refs[n + 1:])

        args = list(args) + [after]
        in_specs = list(in_specs) + [pl.BlockSpec(memory_space=pl.ANY)]
    return pl.pallas_call(body, in_specs=in_specs, **kw)(*args)


_DIMS = {"nn": (((1,), (0,)), ((), ())), "nt": (((1,), (1,)), ((), ())), "tn": (((0,), (0,)), ((), ()))}


def _matmul(name, a, b, mode, tm, tn, tk, out_dtypes=(F32,), extras=(), epilogue=None, into=None, after=None):
    if mode == "tn":
        K, M = a.shape
    else:
        M, K = a.shape
    N = b.shape[0] if mode == "nt" else b.shape[1]
    assert M % tm == 0 and N % tn == 0 and K % tk == 0, (name, M, N, K, tm, tn, tk)
    nk = K // tk
    n_extra = len(extras)
    n_out = len(out_dtypes)
    dims = _DIMS[mode]
    n_alias = 0 if into is None or isinstance(into[0], jax.ShapeDtypeStruct) else 1

    def body(a_ref, b_ref, *rest):
        extra_refs = rest[:n_extra]
        out_refs = rest[n_extra + n_alias:n_extra + n_alias + n_out]

        def finish(acc):
            if epilogue is None:
                res = (acc,)
            else:
                res = epilogue(acc, *[r[...] for r in extra_refs])
            for o_ref, v in zip(out_refs, res):
                o_ref[...] = v.astype(o_ref.dtype)

        part = lax.dot_general(a_ref[...].astype(BF16), b_ref[...].astype(BF16), dims,
                               preferred_element_type=F32)
        if nk == 1:
            finish(part)
        else:
            acc_ref = rest[-1]
            k = pl.program_id(2)

            @pl.when(k == 0)
            def _():
                acc_ref[...] = part

            @pl.when(k > 0)
            def _():
                acc_ref[...] += part

            @pl.when(k == nk - 1)
            def _():
                finish(acc_ref[...])

    if mode == "tn":
        a_spec = pl.BlockSpec((tk, tm), lambda i, j, k: (k, i))
    else:
        a_spec = pl.BlockSpec((tm, tk), lambda i, j, k: (i, k))
    if mode == "nt":
        b_spec = pl.BlockSpec((tn, tk), lambda i, j, k: (j, k))
    else:
        b_spec = pl.BlockSpec((tk, tn), lambda i, j, k: (k, j))
    in_specs = [a_spec, b_spec]
    args = [a, b]
    for arr, kind in extras:
        if kind == "mn":
            in_specs.append(pl.BlockSpec((tm, tn), lambda i, j, k: (i, j)))
        else:
            in_specs.append(pl.BlockSpec((1, tn), lambda i, j, k: (0, j)))
        args.append(arr)
    out_shape = tuple(jax.ShapeDtypeStruct((M, N), dt) for dt in out_dtypes)
    out_specs = tuple(pl.BlockSpec((tm, tn), lambda i, j, k: (i, j)) for _ in out_dtypes)
    aliases = {}
    if into is not None:
        buf, row0, col0 = into
        assert n_out == 1 and (M // N_DEV) % tm == 0 and row0 % tm == 0 and col0 % tn == 0
        per_dev = M // N_DEV // tm
        out_shape = (jax.ShapeDtypeStruct(buf.shape, buf.dtype),)
        out_specs = (pl.BlockSpec((None, tm, tn), lambda i, j, k: (i // per_dev, row0 // tm + i % per_dev,
                                                                   col0 // tn + j)),)
        if not isinstance(buf, jax.ShapeDtypeStruct):
            aliases = {len(args): 0}
            in_specs.append(pl.BlockSpec(memory_space=pl.ANY))
            args.append(buf)
    scratch = [] if nk == 1 else [pltpu.VMEM((tm, tn), F32)]
    res = _call(
        body, args, in_specs, after, name=name, grid=(M // tm, N // tn, nk), out_specs=out_specs,
        out_shape=out_shape, scratch_shapes=scratch, input_output_aliases=aliases,
        compiler_params=_cparams(("parallel", "parallel", "arbitrary")))
    return res if n_out > 1 else res[0]


def _rmsnorm_fwd(name, x, gain):
    S_, D_ = x.shape

    def body(x_ref, g_ref, h_ref):
        xf = x_ref[...]
        ms = jnp.mean(xf * xf, axis=-1, keepdims=True)
        h_ref[...] = (xf * lax.rsqrt(ms + EPS) * g_ref[...]).astype(BF16)

    return pl.pallas_call(
        body, name=name, grid=(S_ // ROW_TILE,),
        in_specs=[pl.BlockSpec((ROW_TILE, D_), lambda i: (i, 0)), pl.BlockSpec((1, D_), lambda i: (0, 0))],
        out_specs=pl.BlockSpec((ROW_TILE, D_), lambda i: (i, 0)),
        out_shape=jax.ShapeDtypeStruct((S_, D_), BF16),
        compiler_params=_cparams(("parallel",)),
    )(x, gain)


def _rmsnorm_bwd(name, x, gain, dh, dres, after=None):
    S_, D_ = x.shape
    n = S_ // ROW_TILE

    def body(x_ref, g_ref, dh_ref, dres_ref, dx_ref, dg_ref, acc_ref):
        i = pl.program_id(0)
        xf = x_ref[...]
        rstd = lax.rsqrt(jnp.mean(xf * xf, axis=-1, keepdims=True) + EPS)
        xhat = xf * rstd
        dhv = dh_ref[...]
        dxhat = dhv * g_ref[...]
        c = jnp.mean(dxhat * xhat, axis=-1, keepdims=True)
        dx_ref[...] = dres_ref[...] + rstd * (dxhat - xhat * c)
        part = jnp.sum((dhv * xhat).reshape(ROW_TILE // 8, 8, D_), axis=0)

        @pl.when(i == 0)
        def _():
            acc_ref[...] = part

        @pl.when(i > 0)
        def _():
            acc_ref[...] += part

        @pl.when(i == n - 1)
        def _():
            dg_ref[...] = jnp.sum(acc_ref[...], axis=0, keepdims=True)

    row = pl.BlockSpec((ROW_TILE, D_), lambda i: (i, 0))
    vec = pl.BlockSpec((1, D_), lambda i: (0, 0))
    return _call(
        body, [x, gain, dh, dres], [row, vec, row, row], after, name=name, grid=(n,), out_specs=(row, vec),
        out_shape=(jax.ShapeDtypeStruct((S_, D_), F32), jax.ShapeDtypeStruct((1, D_), F32)),
        scratch_shapes=[pltpu.VMEM((8, D_), F32)],
        compiler_params=_cparams(("arbitrary",)))


def _loss_head(name, y, target):
    S_, D_ = y.shape
    n = S_ // ROW_TILE

    def body(y_ref, t_ref, dy_ref, l_ref, acc_ref):
        i = pl.program_id(0)
        e = y_ref[...] - t_ref[...]
        dy_ref[...] = e * (1.0 / D_)
        part = jnp.sum((e * e).reshape(ROW_TILE // 8, 8, D_), axis=0)

        @pl.when(i == 0)
        def _():
            acc_ref[...] = part

        @pl.when(i > 0)
        def _():
            acc_ref[...] += part

        @pl.when(i == n - 1)
        def _():
            s = jnp.sum(acc_ref[...], axis=1, keepdims=True)
            l_ref[...] = jnp.sum(s, axis=0, keepdims=True) * (0.5 / D_)

    row = pl.BlockSpec((ROW_TILE, D_), lambda i: (i, 0))
    return pl.pallas_call(
        body, name=name, grid=(n,), in_specs=[row, row],
        out_specs=(row, pl.BlockSpec((1, 1), lambda i: (0, 0))),
        out_shape=(jax.ShapeDtypeStruct((S_, D_), F32), jax.ShapeDtypeStruct((1, 1), F32)),
        scratch_shapes=[pltpu.VMEM((8, D_), F32)],
        compiler_params=_cparams(("arbitrary",)),
    )(y, target)


def _rows_with_halo(ref, cols, i, n_chunks, before, after):
    r0 = pl.multiple_of(i * SEQ_CHUNK, SEQ_CHUNK)
    parts = []
    if before:
        h0 = pl.multiple_of(jnp.maximum(r0 - HALO, 0), 8)
        halo = ref[pl.ds(h0, HALO), cols]
        parts.append(jnp.where(i > 0, halo, jnp.zeros_like(halo)))
    parts.append(ref[pl.ds(r0, SEQ_CHUNK), cols])
    if after:
        a0 = pl.multiple_of(jnp.minimum(r0 + SEQ_CHUNK, (n_chunks - 1) * SEQ_CHUNK + SEQ_CHUNK - HALO), 8)
        halo = ref[pl.ds(a0, HALO), cols]
        parts.append(jnp.where(i < n_chunks - 1, halo, jnp.zeros_like(halo)))
    return parts[0] if len(parts) == 1 else jnp.concatenate(parts, axis=0)


def _shift_down(v, k):
    return pltpu.roll(v, k, 0)


def _shift_up(v, k):
    return pltpu.roll(v, v.shape[0] - k, 0)


def _pool_diff(xx, w, t_main):
    s = xx
    k = 1
    while k < w:
        s = s + _shift_down(s, k)
        k *= 2
    cnt = jnp.minimum(t_main + 1, w).astype(F32)
    return s[HALO:] / cnt - xx[HALO:]


def _pool_fwd(name, z, pool_mix, pool_scale):
    S_ = z.shape[0]
    n_chunks = S_ // SEQ_CHUNK

    def body(u_ref, mix_ref, sc_ref, y_ref):
        for g, w in enumerate(POOL_WINDOWS):
            cols = slice(g * POOL_G, (g + 1) * POOL_G)
            mixg = mix_ref[g].astype(BF16)
            scg = sc_ref[:, cols]

            def chunk(i, carry, cols=cols, mixg=mixg, scg=scg, w=w):
                r0 = pl.multiple_of(i * SEQ_CHUNK, SEQ_CHUNK)
                xx = _rows_with_halo(u_ref, cols, i, n_chunks, True, False)
                t = r0 + lax.broadcasted_iota(jnp.int32, (SEQ_CHUNK, POOL_G), 0)
                d = _pool_diff(xx, w, t)
                y = jnp.dot(d.astype(BF16), mixg, preferred_element_type=F32) * scg
                y_ref[pl.ds(r0, SEQ_CHUNK), cols] = y.astype(BF16)
                return carry

            lax.fori_loop(0, n_chunks, chunk, 0)

    return pl.pallas_call(
        body, name=name, grid=(1,),
        in_specs=[pl.BlockSpec((S_, POOL_W), lambda i: (0, 0)),
                  pl.BlockSpec((4, POOL_G, POOL_G), lambda i: (0, 0, 0)),
                  pl.BlockSpec((1, POOL_W), lambda i: (0, 0))],
        out_specs=pl.BlockSpec((S_, POOL_W), lambda i: (0, 0)),
        out_shape=jax.ShapeDtypeStruct((S_, POOL_W), BF16),
        compiler_params=_cparams(("arbitrary",)),
    )(z, pool_mix, pool_scale)


def _pool_bwd(name, z, dy, pool_mix, pool_scale, after=None):
    S_ = z.shape[0]
    n_chunks = S_ // SEQ_CHUNK
    rows_a = SEQ_CHUNK + HALO

    def body(u_ref, dy_ref, mix_ref, sc_ref, du_ref, dmix_ref, dsc_ref):
        for g, w in enumerate(POOL_WINDOWS):
            cols = slice(g * POOL_G, (g + 1) * POOL_G)
            mixg = mix_ref[g].astype(BF16)
            scg = sc_ref[:, cols]

            def chunk(i, carry, cols=cols, mixg=mixg, scg=scg, w=w):
                dmix_acc, dsc_acc = carry
                r0 = pl.multiple_of(i * SEQ_CHUNK, SEQ_CHUNK)
                xx = _rows_with_halo(u_ref, cols, i, n_chunks, True, False)
                t = r0 + lax.broadcasted_iota(jnp.int32, (SEQ_CHUNK, POOL_G), 0)
                d = _pool_diff(xx, w, t).astype(BF16)
                ypre = jnp.dot(d, mixg, preferred_element_type=F32)
                dyy = _rows_with_halo(dy_ref, cols, i, n_chunks, False, True)
                dys = (dyy * scg).astype(BF16)
                dsc_acc = dsc_acc + jnp.sum((dyy[:SEQ_CHUNK] * ypre).reshape(SEQ_CHUNK // 8, 8, POOL_G), axis=0)
                dmix_acc = dmix_acc + lax.dot_general(d, dys[:SEQ_CHUNK], _DIMS["tn"], preferred_element_type=F32)
                dd = lax.dot_general(dys, mixg, _DIMS["nt"], preferred_element_type=F32)
                ta = r0 + lax.broadcasted_iota(jnp.int32, (rows_a, POOL_G), 0)
                f = dd / jnp.minimum(ta + 1, w).astype(F32)
                k = 1
                while k < w:
                    f = f + _shift_up(f, k)
                    k *= 2
                du_ref[pl.ds(r0, SEQ_CHUNK), cols] = (f[:SEQ_CHUNK] - dd[:SEQ_CHUNK]).astype(BF16)
                return dmix_acc, dsc_acc

            dmix_acc, dsc_acc = lax.fori_loop(
                0, n_chunks, chunk, (jnp.zeros((POOL_G, POOL_G), F32), jnp.zeros((8, POOL_G), F32)))
            dmix_ref[g] = dmix_acc
            dsc_ref[:, cols] = jnp.sum(dsc_acc, axis=0, keepdims=True)

    full = pl.BlockSpec((S_, POOL_W), lambda i: (0, 0))
    mix_spec = pl.BlockSpec((4, POOL_G, POOL_G), lambda i: (0, 0, 0))
    vec = pl.BlockSpec((1, POOL_W), lambda i: (0, 0))
    return _call(
        body, [z, dy, pool_mix, pool_scale], [full, full, mix_spec, vec], after, name=name, grid=(1,),
        out_specs=(full, mix_spec, vec),
        out_shape=(jax.ShapeDtypeStruct((S_, POOL_W), BF16), jax.ShapeDtypeStruct((4, POOL_G, POOL_G), F32),
                   jax.ShapeDtypeStruct((1, POOL_W), F32)),
        compiler_params=_cparams(("arbitrary",)))


def _conv_specs(S_):
    slab = lambda off: pl.BlockSpec((S_, 128), lambda c, off=off: (0, off // 128 + c))
    return slab(OFF_CB), slab(OFF_CC), slab(OFF_CX)


def _conv_fwd(name, z, conv_w):
    S_ = z.shape[0]
    n_chunks = S_ // SEQ_CHUNK
    col = slice(0, 128)

    def body(b_ref, c_ref, x_ref, w_ref, y_ref):
        w0, w1, w2 = w_ref[0:1, :], w_ref[1:2, :], w_ref[2:3, :]

        def chunk(i, carry):
            r0 = pl.multiple_of(i * SEQ_CHUNK, SEQ_CHUNK)
            u = _rows_with_halo(c_ref, col, i, n_chunks, True, False) * _rows_with_halo(x_ref, col, i, n_chunks, True, False)
            y = w2 * u + w1 * _shift_down(u, 1) + w0 * _shift_down(u, 2)
            y_ref[pl.ds(r0, SEQ_CHUNK), :] = (b_ref[pl.ds(r0, SEQ_CHUNK), :] * y[HALO:]).astype(BF16)
            return carry

        lax.fori_loop(0, n_chunks, chunk, 0)

    sb, sc, sx = _conv_specs(S_)
    return pl.pallas_call(
        body, name=name, grid=(CONV_W // 128,),
        in_specs=[sb, sc, sx, pl.BlockSpec((3, 128), lambda c: (0, c))],
        out_specs=pl.BlockSpec((S_, 128), lambda c: (0, c)),
        out_shape=jax.ShapeDtypeStruct((S_, CONV_W), BF16),
        compiler_params=_cparams(("parallel",)),
    )(z, z, z, conv_w)


def _conv_bwd(name, z, dy, conv_w, after=None):
    S_ = z.shape[0]
    n_chunks = S_ // SEQ_CHUNK
    col = slice(0, 128)
    lo, hi = HALO, HALO + SEQ_CHUNK

    def body(b_ref, c_ref, x_ref, dy_ref, w_ref, db_ref, dc_ref, dx_ref, dw_ref):
        w0, w1, w2 = w_ref[0:1, :], w_ref[1:2, :], w_ref[2:3, :]

        def chunk(i, carry):
            a0, a1, a2 = carry
            r0 = pl.multiple_of(i * SEQ_CHUNK, SEQ_CHUNK)
            cc = _rows_with_halo(c_ref, col, i, n_chunks, True, True)
            xx = _rows_with_halo(x_ref, col, i, n_chunks, True, True)
            bb = _rows_with_halo(b_ref, col, i, n_chunks, True, True)
            dyy = _rows_with_halo(dy_ref, col, i, n_chunks, True, True)
            u = cc * xx
            u1 = _shift_down(u, 1)
            u2 = _shift_down(u, 2)
            y = w2 * u + w1 * u1 + w0 * u2
            dyv = dyy * bb
            du = w2 * dyv + w1 * _shift_up(dyv, 1) + w0 * _shift_up(dyv, 2)
            db_ref[pl.ds(r0, SEQ_CHUNK), :] = (dyy[lo:hi] * y[lo:hi]).astype(BF16)
            dc_ref[pl.ds(r0, SEQ_CHUNK), :] = (du[lo:hi] * xx[lo:hi]).astype(BF16)
            dx_ref[pl.ds(r0, SEQ_CHUNK), :] = (du[lo:hi] * cc[lo:hi]).astype(BF16)
            red = lambda v: jnp.sum(v.reshape(SEQ_CHUNK // 8, 8, 128), axis=0)
            dm = dyv[lo:hi]
            return a0 + red(dm * u2[lo:hi]), a1 + red(dm * u1[lo:hi]), a2 + red(dm * u[lo:hi])

        zero = jnp.zeros((8, 128), F32)
        a0, a1, a2 = lax.fori_loop(0, n_chunks, chunk, (zero, zero, zero))
        dw_ref[0:1, :] = jnp.sum(a0, axis=0, keepdims=True)
        dw_ref[1:2, :] = jnp.sum(a1, axis=0, keepdims=True)
        dw_ref[2:3, :] = jnp.sum(a2, axis=0, keepdims=True)

    sb, sc, sx = _conv_specs(S_)
    slab = pl.BlockSpec((S_, 128), lambda c: (0, c))
    wspec = pl.BlockSpec((3, 128), lambda c: (0, c))
    act = jax.ShapeDtypeStruct((S_, CONV_W), BF16)
    return _call(
        body, [z, z, z, dy, conv_w], [sb, sc, sx, slab, wspec], after, name=name, grid=(CONV_W // 128,),
        out_specs=(slab, slab, slab, wspec),
        out_shape=(act, act, act, jax.ShapeDtypeStruct((3, CONV_W), F32)),
        compiler_params=_cparams(("parallel",)))


def _head_norm(v, gain):
    rstd = lax.rsqrt(jnp.mean(v * v, axis=-1, keepdims=True) + EPS)
    xhat = v * rstd
    return xhat * gain, xhat, rstd


def _head_norm_bwd(dy, xhat, rstd, gain):
    dxhat = dy * gain
    c = jnp.mean(dxhat * xhat, axis=-1, keepdims=True)
    dv = rstd * (dxhat - xhat * c)
    dg = jnp.sum((dy * xhat).reshape(ATTN_BLK // 8, 8, HEAD), axis=0)
    return dv, dg


def _band_masks():
    qi = lax.broadcasted_iota(jnp.int32, (ATTN_BLK, ATTN_BLK), 0)
    ki = lax.broadcasted_iota(jnp.int32, (ATTN_BLK, ATTN_BLK), 1)
    return ki <= qi, ki >= qi


def _attn_specs(S_, g, dil):
    rows = ATTN_BLK * dil
    nb = S_ // rows
    pw = 128 if dil > 1 else ATTN_OUT
    cq, ck, cv = ((OFF_Q + g * ATTN_OUT) // pw, (OFF_K + g * ATTN_OUT) // pw, (OFF_V + g * ATTN_OUT) // pw)
    return rows, nb, pw, pw // HEAD, ATTN_OUT // pw, cq, ck, cv


ATTN_BATCH = 4


def _attn_group(dil):
    return 4 if dil == 1 else 1


def _attn_block_specs(rows, grp, pw, last=None):
    step = (lambda n: n) if last is None else (lambda n: jnp.minimum(n, last))
    cur = lambda c: pl.BlockSpec((rows * grp, pw), lambda hp, n, c=c: (step(n), c + hp))
    prev = lambda c: pl.BlockSpec((rows, pw), lambda hp, n, c=c: (jnp.maximum(step(n) * grp - 1, 0), c + hp))
    return cur, prev


def _band_mask(has_prev):
    qi = lax.broadcasted_iota(jnp.int32, (ATTN_BLK, 2 * ATTN_BLK), 0)
    ki = lax.broadcasted_iota(jnp.int32, (ATTN_BLK, 2 * ATTN_BLK), 1)
    in_prev = jnp.logical_and(ki < ATTN_BLK, ki >= qi)
    if has_prev is not True:
        in_prev = jnp.logical_and(in_prev, has_prev)
    return jnp.logical_or(in_prev, jnp.logical_and(ki >= ATTN_BLK, ki - ATTN_BLK <= qi))


def _rows_of(ref, r, dil):
    if dil == 1:
        return ref[r * ATTN_BLK:(r + 1) * ATTN_BLK, :]
    return ref[pl.ds(r, ATTN_BLK, stride=dil), :]


def _put_rows(ref, r, dil, val):
    if dil == 1:
        ref[r * ATTN_BLK:(r + 1) * ATTN_BLK, :] = val.astype(ref.dtype)
    else:
        ref[pl.ds(r, ATTN_BLK, stride=dil), :] = val.astype(ref.dtype)


def _attn_fwd(name, z, q_gain, k_gain, g, dil):
    S_ = z.shape[0]
    rows, nb, pw, heads, npairs, cq, ck, cv = _attn_specs(S_, g, dil)
    scale = HEAD ** -0.5

    grp = _attn_group(dil)
    nsteps = nb // grp

    def body(q_ref, k_ref, kp_ref, v_ref, vp_ref, gq_ref, gk_ref, o_ref, l_ref):
        n = pl.program_id(1)
        gq, gk = gq_ref[...], gk_ref[...]
        mask_first, mask_rest = _band_mask(n > 0), _band_mask(True)
        for r0 in range(0, dil * grp, ATTN_BATCH):
            rs = range(r0, min(r0 + ATTN_BATCH, dil * grp))
            qn, kn, vv, s, p = {}, {}, {}, {}, {}
            for r in rs:
                q, kc, vc = _rows_of(q_ref, r, dil), _rows_of(k_ref, r, dil), _rows_of(v_ref, r, dil)
                if dil == 1 and r > 0:
                    kp, vp = _rows_of(k_ref, r - 1, dil), _rows_of(v_ref, r - 1, dil)
                else:
                    kp, vp = _rows_of(kp_ref, r, dil), _rows_of(vp_ref, r, dil)
                for h in range(heads):
                    hs = slice(h * HEAD, (h + 1) * HEAD)
                    qn[r, h] = _head_norm(q[:, hs], gq)[0].astype(BF16)
                    kn[r, h] = jnp.concatenate([_head_norm(kp[:, hs], gk)[0], _head_norm(kc[:, hs], gk)[0]],
                                               axis=0).astype(BF16)
                    vv[r, h] = jnp.concatenate([vp[:, hs], vc[:, hs]], axis=0).astype(BF16)
            for key in qn:
                s[key] = lax.dot_general(qn[key], kn[key], _DIMS["nt"], preferred_element_type=F32) * scale
            lse, den = {}, {}
            for key in qn:
                mask = mask_rest if (dil == 1 and key[0] > 0) else mask_first
                sm = jnp.where(mask, s[key], MASK)
                m = jnp.max(sm, axis=-1, keepdims=True)
                e = jnp.exp(sm - m)
                den[key] = jnp.sum(e, axis=-1, keepdims=True)
                p[key] = e.astype(BF16)
                lse[key] = m + jnp.log(den[key])
            for r in rs:
                outs = [jnp.dot(p[r, h], vv[r, h], preferred_element_type=F32) / den[r, h] for h in range(heads)]
                lses = [jnp.broadcast_to(lse[r, h], (ATTN_BLK, HEAD)) for h in range(heads)]
                _put_rows(o_ref, r, dil, jnp.concatenate(outs, axis=1))
                _put_rows(l_ref, r, dil, jnp.concatenate(lses, axis=1))

    cur, prev = _attn_block_specs(rows, grp, pw)
    gspec = pl.BlockSpec((1, HEAD), lambda hp, n: (0, 0))
    shp = jax.ShapeDtypeStruct((S_, ATTN_OUT), F32)
    return pl.pallas_call(
        body, name=name, grid=(npairs, nsteps),
        in_specs=[cur(cq), cur(ck), prev(ck), cur(cv), prev(cv), gspec, gspec],
        out_specs=(cur(0), cur(0)), out_shape=(shp, shp),
        compiler_params=_cparams(("parallel", "parallel")),
    )(z, z, z, z, z, q_gain, k_gain)


def _attn_combine(name, os_, ls_):
    S_ = os_[0].shape[0]

    def body(o0, o1, o2, l0, l1, l2, o_ref, l_ref):
        a, b, c = l0[...], l1[...], l2[...]
        m = jnp.maximum(jnp.maximum(a, b), c)
        ea, eb, ec = jnp.exp(a - m), jnp.exp(b - m), jnp.exp(c - m)
        zsum = ea + eb + ec
        o_ref[...] = (ea * o0[...] + eb * o1[...] + ec * o2[...]) / zsum
        l_ref[...] = m + jnp.log(zsum)

    row = pl.BlockSpec((ROW_TILE, ATTN_OUT), lambda i: (i, 0))
    shp = jax.ShapeDtypeStruct((S_, ATTN_OUT), F32)
    return pl.pallas_call(
        body, name=name, grid=(S_ // ROW_TILE,), in_specs=[row] * 6, out_specs=(row, row), out_shape=(shp, shp),
        compiler_params=_cparams(("parallel",)),
    )(*os_, *ls_)


def _attn_bwd(name, z, q_gain, k_gain, do, o, lse, g, dil, after=None):
    S_ = z.shape[0]
    rows, nb, pw, heads, npairs, cq, ck, cv = _attn_specs(S_, g, dil)
    scale = HEAD ** -0.5

    def body(q_ref, kc_ref, kp_ref, vc_ref, vp_ref, gq_ref, gk_ref, do_ref, o_ref, l_ref,
             dq_ref, dk_ref, dv_ref, dgq_ref, dgk_ref, ck_ref, cvv_ref, gq_acc, gk_acc):
        hp = pl.program_id(0)
        n = pl.program_id(1)
        live = n < nb
        mask = jnp.logical_and(_band_mask(n > 0), live)
        gq, gk = gq_ref[...], gk_ref[...]

        @pl.when(n == 0)
        def _():
            ck_ref[...] = jnp.zeros_like(ck_ref)
            cvv_ref[...] = jnp.zeros_like(cvv_ref)

        @pl.when(jnp.logical_and(n == 0, hp == 0))
        def _():
            gq_acc[...] = jnp.zeros_like(gq_acc)
            gk_acc[...] = jnp.zeros_like(gk_acc)

        dgq = jnp.zeros((8, HEAD), F32)
        dgk = jnp.zeros((8, HEAD), F32)
        for r0 in range(0, dil, ATTN_BATCH):
            rs = range(r0, min(r0 + ATTN_BATCH, dil))
            keys = [(r, h) for r in rs for h in range(heads)]
            qn, qhat, qrstd, kn, kphat, kprstd, vv, dob, delta, lrow = ({} for _ in range(10))
            for r in rs:
                q, kc, kp = _rows_of(q_ref, r, dil), _rows_of(kc_ref, r, dil), _rows_of(kp_ref, r, dil)
                vc, vp = _rows_of(vc_ref, r, dil), _rows_of(vp_ref, r, dil)
                dov, ov, lv = _rows_of(do_ref, r, dil), _rows_of(o_ref, r, dil), _rows_of(l_ref, r, dil)
                for h in range(heads):
                    hs = slice(h * HEAD, (h + 1) * HEAD)
                    qn_f, qhat[r, h], qrstd[r, h] = _head_norm(q[:, hs], gq)
                    kpn, kphat[r, h], kprstd[r, h] = _head_norm(kp[:, hs], gk)
                    qn[r, h] = qn_f.astype(BF16)
                    kn[r, h] = jnp.concatenate([kpn, _head_norm(kc[:, hs], gk)[0]], axis=0).astype(BF16)
                    vv[r, h] = jnp.concatenate([vp[:, hs], vc[:, hs]], axis=0).astype(BF16)
                    dob[r, h] = dov[:, hs].astype(BF16)
                    delta[r, h] = jnp.sum(dov[:, hs] * ov[:, hs], axis=-1, keepdims=True)
                    lrow[r, h] = lv[:, h * HEAD:h * HEAD + 1]
            s = {k: lax.dot_general(qn[k], kn[k], _DIMS["nt"], preferred_element_type=F32) * scale for k in keys}
            dp = {k: lax.dot_general(dob[k], vv[k], _DIMS["nt"], preferred_element_type=F32) for k in keys}
            p, ds = {}, {}
            for k in keys:
                pk = jnp.where(mask, jnp.exp(jnp.where(mask, s[k], MASK) - lrow[k]), 0.0)
                ds[k] = (pk * (dp[k] - delta[k]) * scale).astype(BF16)
                p[k] = pk.astype(BF16)
            dqn = {k: jnp.dot(ds[k], kn[k], preferred_element_type=F32) for k in keys}
            dkn = {k: lax.dot_general(ds[k], qn[k], _DIMS["tn"], preferred_element_type=F32) for k in keys}
            dvv = {k: lax.dot_general(p[k], dob[k], _DIMS["tn"], preferred_element_type=F32) for k in keys}
            for r in rs:
                carry_k, carry_v = _rows_of(ck_ref, r, dil), _rows_of(cvv_ref, r, dil)
                dqs, dks, dvs = [], [], []
                for h in range(heads):
                    hs = slice(h * HEAD, (h + 1) * HEAD)
                    dq_h, dg = _head_norm_bwd(dqn[r, h], qhat[r, h], qrstd[r, h], gq)
                    dgq = dgq + dg
                    dqs.append(dq_h)
                    dk_h, dg = _head_norm_bwd(carry_k[:, hs] + dkn[r, h][:ATTN_BLK], kphat[r, h], kprstd[r, h], gk)
                    dgk = dgk + dg
                    dks.append(dk_h)
                    dvs.append(carry_v[:, hs] + dvv[r, h][:ATTN_BLK])
                dq_all = jnp.concatenate(dqs, axis=1)

                @pl.when(live)
                def _(dq_all=dq_all, r=r):
                    _put_rows(dq_ref, r, dil, dq_all)

                _put_rows(dk_ref, r, dil, jnp.concatenate(dks, axis=1))
                _put_rows(dv_ref, r, dil, jnp.concatenate(dvs, axis=1))
                _put_rows(ck_ref, r, dil, jnp.concatenate([dkn[r, h][ATTN_BLK:] for h in range(heads)], axis=1))
                _put_rows(cvv_ref, r, dil, jnp.concatenate([dvv[r, h][ATTN_BLK:] for h in range(heads)], axis=1))
        gq_acc[...] += dgq
        gk_acc[...] += dgk

        @pl.when(jnp.logical_and(n == nb, hp == npairs - 1))
        def _():
            dgq_ref[...] = jnp.sum(gq_acc[...], axis=0, keepdims=True)
            dgk_ref[...] = jnp.sum(gk_acc[...], axis=0, keepdims=True)

    last = nb - 1
    cur = lambda c: pl.BlockSpec((rows, pw), lambda hp, n, c=c: (jnp.minimum(n, last), c + hp))
    prev = lambda c: pl.BlockSpec((rows, pw), lambda hp, n, c=c: (jnp.maximum(n - 1, 0), c + hp))
    gspec = pl.BlockSpec((1, HEAD), lambda hp, n: (0, 0))
    act = jax.ShapeDtypeStruct((S_, ATTN_OUT), F32)
    vec = jax.ShapeDtypeStruct((1, HEAD), F32)
    return _call(
        body, [z, z, z, z, z, q_gain, k_gain, do, o, lse],
        [cur(cq), cur(ck), prev(ck), cur(cv), prev(cv), gspec, gspec, cur(0), cur(0), cur(0)], after,
        name=name, grid=(npairs, nb + 1),
        out_specs=(cur(0), prev(0), prev(0), gspec, gspec),
        out_shape=(act, act, act, vec, vec),
        scratch_shapes=[pltpu.VMEM((rows, pw), F32), pltpu.VMEM((rows, pw), F32),
                        pltpu.VMEM((8, HEAD), F32), pltpu.VMEM((8, HEAD), F32)],
        compiler_params=_cparams(("arbitrary", "arbitrary")))


MIX_TN = 256


def _sigmoid(v):
    return 1.0 / (1.0 + jnp.exp(-v))


def _mix_fwd(name, z, b_gate, ys, ws):
    S_ = z.shape[0]
    tm, tn = ROW_TILE, MIX_TN
    gblk = OFF_GATE // tn

    def body(yp, yc, ya, wp, wc, wa, g0, g1, g2, b0, b1, b2, m_ref):
        acc = None
        for y_ref, w_ref, g_ref, b_ref in ((yp, wp, g0, b0), (yc, wc, g1, b1), (ya, wa, g2, b2)):
            u = lax.dot_general(y_ref[...].astype(BF16), w_ref[...], _DIMS["nt"], preferred_element_type=F32)
            t = _sigmoid(g_ref[...] + b_ref[...]) * u
            acc = t if acc is None else acc + t
        m_ref[...] = acc.astype(BF16)

    yspec = lambda w: pl.BlockSpec((tm, w), lambda i, j: (i, 0))
    wspec = lambda w: pl.BlockSpec((tn, w), lambda i, j: (j, 0))
    gspec = lambda b: pl.BlockSpec((tm, tn), lambda i, j, b=b: (i, gblk + b * (D // tn) + j))
    bspec = lambda b: pl.BlockSpec((1, tn), lambda i, j, b=b: (0, b * (D // tn) + j))
    return pl.pallas_call(
        body, name=name, grid=(S_ // tm, D // tn),
        in_specs=[yspec(POOL_W), yspec(CONV_W), yspec(ATTN_OUT), wspec(POOL_W), wspec(CONV_W), wspec(ATTN_OUT),
                  gspec(0), gspec(1), gspec(2), bspec(0), bspec(1), bspec(2)],
        out_specs=pl.BlockSpec((tm, tn), lambda i, j: (i, j)),
        out_shape=jax.ShapeDtypeStruct((S_, D), BF16),
        compiler_params=_cparams(("parallel", "parallel")),
    )(*ys, *ws, z, z, z, b_gate, b_gate, b_gate)


def _mix_bwd(name, z, b_gate, y, w, dmerged, branch, after=None):
    S_ = z.shape[0]
    tm, tn = ROW_TILE, MIX_TN
    width = y.shape[1]
    gblk = OFF_GATE // tn + branch * (D // tn)
    ni = S_ // tm

    def body(y_ref, w_ref, g_ref, b_ref, dm_ref, du_ref, dg_ref, db_ref, acc_ref):
        i = pl.program_id(1)
        u = lax.dot_general(y_ref[...].astype(BF16), w_ref[...], _DIMS["nt"], preferred_element_type=F32)
        sg = _sigmoid(g_ref[...] + b_ref[...])
        dm = dm_ref[...]
        du_ref[...] = (sg * dm).astype(BF16)
        dpre = dm * u * sg * (1.0 - sg)
        dg_ref[...] = dpre.astype(BF16)
        part = jnp.sum(dpre.reshape(tm // 8, 8, tn), axis=0)

        @pl.when(i == 0)
        def _():
            acc_ref[...] = part

        @pl.when(i > 0)
        def _():
            acc_ref[...] += part

        @pl.when(i == ni - 1)
        def _():
            db_ref[...] = jnp.sum(acc_ref[...], axis=0, keepdims=True)

    blk = pl.BlockSpec((tm, tn), lambda j, i: (i, j))
    vec = pl.BlockSpec((1, tn), lambda j, i: (0, j))
    act = jax.ShapeDtypeStruct((S_, D), BF16)
    return _call(
        body, [y, w, z, b_gate, dmerged],
        [pl.BlockSpec((tm, width), lambda j, i: (i, 0)), pl.BlockSpec((tn, width), lambda j, i: (j, 0)),
         pl.BlockSpec((tm, tn), lambda j, i: (i, gblk + j)),
         pl.BlockSpec((1, tn), lambda j, i: (0, branch * (D // tn) + j)), blk], after,
        name=name, grid=(D // tn, ni),
        out_specs=(blk, blk, vec), out_shape=(act, act, jax.ShapeDtypeStruct((1, D), F32)),
        scratch_shapes=[pltpu.VMEM((8, tn), F32)],
        compiler_params=_cparams(("parallel", "arbitrary")))


def _relu2_epilogue(acc):
    r = jnp.maximum(acc, 0.0)
    return acc, r * r


def _layer_fwd(l, x, p):
    t = f"l{l}_"
    h = _rmsnorm_fwd(t + "norm_mix", x, p["norm_mix"])
    z = _matmul(t + "in_proj", h, p["w_in_t"], "nt", 512, 3712, 1024)
    y_pool = _pool_fwd(t + "pool", z, p["pool_mix"], p["pool_scale"])
    y_conv = _conv_fwd(t + "conv", z, p["conv_w"])
    os_, ls_ = [], []
    for g, (_, dil) in enumerate(ATTN_GROUPS):
        o_g, l_g = _attn_fwd(t + f"attn{g}", z, p["q_gain"], p["k_gain"], g, dil)
        os_.append(o_g)
        ls_.append(l_g)
    y_attn, lse = _attn_combine(t + "attn_mix", os_, ls_)
    merged = _mix_fwd(t + "merge", z, p["b_gate"], (y_pool, y_conv, y_attn),
                      (p["w_pool_up_t"], p["w_conv_out_t"], p["w_attn_up_t"]))
    x1 = _matmul(t + "out_proj", merged, p["w_o"], "nn", 1024, 1024, 1024,
                 extras=((x, "mn"),), epilogue=lambda acc, xr: (xr + acc,))
    h2 = _rmsnorm_fwd(t + "norm_mlp", x1, p["norm_mlp"])
    a, r = _matmul(t + "ff1", h2, p["w_ff1_t"], "nt", 1024, 1024, 1024, out_dtypes=(F32, BF16),
                   epilogue=_relu2_epilogue)
    x2 = _matmul(t + "ff2", r, p["w_ff2"], "nn", 1024, 1024, 1024,
                 extras=((x1, "mn"),), epilogue=lambda acc, xr: (xr + acc,))
    saved = dict(x=x, h=h, z=z, y_pool=y_pool, y_conv=y_conv, y_attn=y_attn, lse=lse, merged=merged,
                 x1=x1, h2=h2, a=a, r=r)
    return x2, saved


def _layer_bwd(l, dx2, p, s, pending, collective_id, last, start_after=None):
    t = f"l{l}_b_"
    g = {}
    rest = jax.ShapeDtypeStruct((N_DEV, REST_ROWS, D), F32)
    da = _matmul(t + "d_ff2_in", dx2, p["w_ff2"], "nt", 1024, 1024, 1024, out_dtypes=(BF16,),
                 extras=((s["a"], "mn"),), epilogue=lambda acc, a: (acc * (2.0 * jnp.maximum(a, 0.0)),),
                 after=start_after)
    rest = _matmul(t + "dw_ff2", s["r"], dx2, "tn", 512, 1024, 4096, into=(rest, REST_FF2, 0), after=da)
    tok = rest if pending is None else pending.chip_sums(after=rest)
    dh2 = _matmul(t + "d_ff1_in", da, p["w_ff1_t"], "nn", 1024, 1024, 1024, after=tok)
    rest = _matmul(t + "dw_ff1", da, s["h2"], "tn", 512, 1024, 4096, into=(rest, REST_FF1, 0), after=dh2)
    dx1, g["norm_mlp"] = _rmsnorm_bwd(t + "norm_mlp", s["x1"], p["norm_mlp"], dh2, dx2, after=rest)
    dmerged = _matmul(t + "d_out_proj_in", dx1, p["w_o"], "nt", 1024, 1024, 1024)
    rest = _matmul(t + "dw_o", s["merged"], dx1, "tn", 128, 1024, 4096, into=(rest, REST_WO, 0), after=dmerged)
    ys = (s["y_pool"], s["y_conv"], s["y_attn"])
    names = ("w_pool_up_t", "w_conv_out_t", "w_attn_up_t")
    dys, dgates, dbs = [], [], []
    tok = rest
    for b in range(3):
        du, dgz, db = _mix_bwd(t + f"merge{b}", s["z"], p["b_gate"], ys[b], p[names[b]], dmerged, b, after=tok)
        width = ys[b].shape[1]
        dys.append(_matmul(t + f"d_up{b}_in", du, p[names[b]], "nn", 512, width, 1024))
        if b < 2:
            rest = _matmul(t + f"dw_up{b}", du, ys[b], "tn", 128, width, 4096, into=(rest, REST_UP, b * width),
                           after=dys[b])
            tok = rest
        else:
            d_attn_up_t = _matmul(t + f"dw_up{b}", du, ys[b], "tn", 128, width, 4096, after=dys[b])
            rest = rest.at[:, REST_ATTN:REST_ROWS, :].set(d_attn_up_t.reshape(N_DEV, REST_ROWS - REST_ATTN, D))
        dgates.append(dgz)
        dbs.append(db)
    g["b_gate"] = jnp.concatenate(dbs, axis=1)
    rs_rest = _ReduceScatter(f"rs_rest{l}", collective_id, rest)
    pending_sum = None if pending is None else pending.finish(after=rest)
    du_pool, g["pool_mix"], g["pool_scale"] = _pool_bwd(t + "pool", s["z"], dys[0], p["pool_mix"], p["pool_scale"],
                                                        after=rest if pending is None else pending_sum[0])
    dcb, dcc, dcx, g["conv_w"] = _conv_bwd(t + "conv", s["z"], dys[1], p["conv_w"], after=du_pool)
    tok = rs_rest.chip_sums(after=dcb)
    dqs, dks, dvs = [], [], []
    gq = gk = None
    for gi, (_, dil) in enumerate(ATTN_GROUPS):
        dq, dk, dv, dgq, dgk = _attn_bwd(t + f"attn{gi}", s["z"], p["q_gain"], p["k_gain"], dys[2], s["y_attn"],
                                         s["lse"], gi, dil, after=tok)
        tok = dq
        dqs.append(dq)
        dks.append(dk)
        dvs.append(dv)
        gq = dgq if gq is None else gq + dgq
        gk = dgk if gk is None else gk + dgk
    g["q_gain"], g["k_gain"] = gq, gk
    dz = jnp.concatenate([du_pool, dcb, dcc, dcx] + [v.astype(BF16) for v in dqs + dks + dvs] + dgates, axis=1)
    in_t = _matmul(t + "dw_in", dz, s["h"], "tn", 256, 1024, 4096)
    rs_in = _ReduceScatter(f"rs_in{l}", collective_id + 2, in_t.reshape(N_DEV, IN_ROWS, D))
    dh = _matmul(t + "d_in_proj_in", dz, p["w_in_t"], "nn", 512, 1024, 3712, after=in_t)
    tok = rs_in.chip_sums(after=dh) if last else dh
    rest_sum, _ = rs_rest.finish(after=tok)
    dx, g["norm_mix"] = _rmsnorm_bwd(t + "norm_mix", s["x"], p["norm_mix"], dh, dx1, after=rest_sum)
    return dx, g, rest_sum, pending_sum, rs_in


ANY = pl.BlockSpec(memory_space=pl.ANY)


def _mesh_pos():
    return lax.axis_index("x"), lax.axis_index("y"), lax.axis_index("c")


def _other_chips(x, y):
    return [(1 - x, y), (x, 1 - y), (1 - x, 1 - y)]


def _comm_call(name, collective_id, peers, body, arrs, out_shape, sem_counts):
    n_in, n_out = len(arrs), len(out_shape)
    if collective_id is None:
        def tc_body(*refs):
            body(refs[:n_in], refs[n_in:n_in + n_out], *refs[n_in + n_out:])

        return pl.pallas_call(
            tc_body, name=name, out_shape=tuple(out_shape), in_specs=[ANY] * n_in, out_specs=(ANY,) * n_out,
            scratch_shapes=[pltpu.SemaphoreType.DMA((n,)) for n in sem_counts])(*arrs)

    def seq_body(*refs):
        barrier = pltpu.get_barrier_semaphore()
        ps = peers(*_mesh_pos())
        for p in ps:
            pl.semaphore_signal(barrier, inc=1, device_id=p, device_id_type=MESH)
        pl.semaphore_wait(barrier, len(ps))
        body(refs[:n_in], refs[n_in:n_in + n_out], *refs[n_in + n_out:])

    return pl.kernel(
        seq_body, out_type=tuple(out_shape), mesh=plsc.ScalarSubcoreMesh(axis_name="seq", num_cores=1), name=name,
        scratch_types=[pltpu.SemaphoreType.DMA((n,)) for n in sem_counts],
        compiler_params=pltpu.CompilerParams(collective_id=collective_id),
    )(*arrs)


def _all_gather(name, collective_id, shard):
    R, C = shard.shape

    def peers(x, y, c):
        return [(x, y, 1 - c)] + [(*chip, c) for chip in _other_chips(x, y)]

    def body(in_refs, out_refs, send_sems, recv_sems, local_sems):
        (x_ref,), (out_ref,) = in_refs, out_refs
        x, y, c = _mesh_pos()
        me, sibling = (x, y, c), (x, y, 1 - c)
        chips = _other_chips(x, y)

        def slot(px, py, pc):
            return out_ref.at[4 * px + 2 * py + pc]

        def copy(k, block, to, src=None):
            return pltpu.make_async_remote_copy(
                src_ref=slot(*block) if src is None else src, dst_ref=slot(*block),
                send_sem=send_sems.at[k], recv_sem=recv_sems.at[k], device_id=to, device_id_type=MESH)

        mine = pltpu.make_async_copy(x_ref, slot(*me), local_sems.at[0])
        mine.start()
        first = [copy(0, me, sibling, src=x_ref)]
        first += [copy(1 + j, me, (*chip, c), src=x_ref) for j, chip in enumerate(chips)]
        for cp in first:
            cp.start()
        passed = [copy(4 + j, (*chip, c), sibling) for j, chip in enumerate(chips)]
        for j, chip in enumerate(chips):
            copy(1 + j, (*chip, c), me).wait_recv()
            passed[j].start()
        copy(0, sibling, me).wait_recv()
        for j, chip in enumerate(chips):
            copy(4 + j, (*chip, 1 - c), me).wait_recv()
        for cp in first + passed:
            cp.wait_send()
        mine.wait()

    return _comm_call(name, collective_id, peers, body, [shard],
                      [jax.ShapeDtypeStruct((N_DEV, R, C), shard.dtype)], (7, 7, 1))[0]


def _rs_sibling_exchange(name, collective_id, arrs):
    n = len(arrs)

    def body(in_refs, out_refs, send_sems, recv_sems):
        x, y, c = _mesh_pos()
        cps = []
        for k, (src, dst) in enumerate(zip(in_refs, out_refs)):
            src = src.at[:, 1 - c] if len(src.shape) == 4 else src
            cps.append(pltpu.make_async_remote_copy(src_ref=src, dst_ref=dst, send_sem=send_sems.at[k],
                                                    recv_sem=recv_sems.at[k], device_id=(x, y, 1 - c),
                                                    device_id_type=MESH))
        for cp in cps:
            cp.start()
        for cp in cps:
            cp.wait()

    out_shape = [jax.ShapeDtypeStruct(a.shape[:1] + a.shape[2:] if a.ndim == 4 else a.shape, a.dtype) for a in arrs]
    return _comm_call(name, collective_id, lambda x, y, c: [(x, y, 1 - c)], body, arrs, out_shape, (n, n))


def _rs_rows(r):
    return r // 2 if (r // 2) % 16 == 0 else r


def _drop_ref(body, idx):
    def wrapped(*refs):
        return body(*refs[:idx], *refs[idx + 1:])

    return wrapped


def _rs_chip_sum(name, ids, big, rbig, after=None):
    _, _, R, C = big.shape
    rows = _rs_rows(R)

    def body(ids_ref, mine_ref, theirs_ref, t16_ref, own_ref):
        p = pl.program_id(1)
        s = mine_ref[...] + theirs_ref[...]
        t16_ref[...] = s.astype(BF16)

        @pl.when(p == ids_ref[1])
        def _():
            own_ref[...] = s

    in_specs = [pl.BlockSpec((None, None, rows, C), lambda i, p, ids: (p, ids[0], i, 0)),
                pl.BlockSpec((None, rows, C), lambda i, p, ids: (p, i, 0))]
    args = [ids, big, rbig]
    if after is not None:
        body = _drop_ref(body, len(args))
        in_specs.append(ANY)
        args.append(after)
    grid_spec = pltpu.PrefetchScalarGridSpec(
        num_scalar_prefetch=1, grid=(R // rows, 4), in_specs=in_specs,
        out_specs=(pl.BlockSpec((None, rows, C), lambda i, p, ids: (p, i, 0)),
                   pl.BlockSpec((rows, C), lambda i, p, ids: (i, 0))))
    return pl.pallas_call(
        body, name=name, grid_spec=grid_spec,
        out_shape=(jax.ShapeDtypeStruct((4, R, C), BF16), jax.ShapeDtypeStruct((R, C), F32)),
        compiler_params=_cparams(("parallel", "arbitrary")),
    )(*args)


def _add2(name, a, b):
    def body(a_ref, b_ref, o_ref):
        o_ref[...] = a_ref[...] + b_ref[...]

    full = pl.BlockSpec(a.shape, lambda i: (0, 0))
    return pl.pallas_call(body, name=name, grid=(1,), in_specs=[full, full], out_specs=full,
                          out_shape=jax.ShapeDtypeStruct(a.shape, a.dtype))(a, b)


def _rs_chip_exchange(name, collective_id, arrs):
    n = len(arrs)

    def body(in_refs, out_refs, send_sems, recv_sems, local_sems):
        x, y, c = _mesh_pos()
        p_me = 2 * x + y
        chips = _other_chips(x, y)

        def part(ref, p):
            return ref.at[p] if len(ref.shape) == 3 else ref

        local = [pltpu.make_async_copy(part(src, p_me), dst.at[p_me], local_sems.at[k])
                 for k, (src, dst) in enumerate(zip(in_refs, out_refs))]
        for cp in local:
            cp.start()
        sends = []
        for j, (px, py) in enumerate(chips):
            for k, (src, dst) in enumerate(zip(in_refs, out_refs)):
                sends.append(pltpu.make_async_remote_copy(
                    src_ref=part(src, 2 * px + py), dst_ref=dst.at[p_me], send_sem=send_sems.at[n * j + k],
                    recv_sem=recv_sems.at[n * j + k], device_id=(px, py, c), device_id_type=MESH))
        for cp in sends:
            cp.start()
        for j, (px, py) in enumerate(chips):
            for k, (src, dst) in enumerate(zip(in_refs, out_refs)):
                pltpu.make_async_remote_copy(
                    src_ref=part(src, p_me), dst_ref=dst.at[2 * px + py], send_sem=send_sems.at[n * j + k],
                    recv_sem=recv_sems.at[n * j + k], device_id=(px, py, c), device_id_type=MESH).wait_recv()
        for cp in sends:
            cp.wait_send()
        for cp in local:
            cp.wait()

    out_shape = [jax.ShapeDtypeStruct((4,) + a.shape[-2:], a.dtype) for a in arrs]
    return _comm_call(name, collective_id, lambda x, y, c: [(*chip, c) for chip in _other_chips(x, y)], body, arrs,
                      out_shape, (3 * n, 3 * n, n))


def _rs_final_sum(name, ids, recv, own, rows, after=None):
    _, R, C = recv.shape
    assert R % rows == 0

    def body(ids_ref, r_ref, own_ref, o_ref):
        acc = None
        for p in range(4):
            term = jnp.where(ids_ref[1] == p, own_ref[...], r_ref[p].astype(F32))
            acc = term if acc is None else acc + term
        o_ref[...] = acc

    in_specs = [pl.BlockSpec((4, rows, C), lambda i, ids: (0, i, 0)), pl.BlockSpec((rows, C), lambda i, ids: (i, 0))]
    args = [ids, recv, own]
    if after is not None:
        body = _drop_ref(body, len(args))
        in_specs.append(ANY)
        args.append(after)
    grid_spec = pltpu.PrefetchScalarGridSpec(
        num_scalar_prefetch=1, grid=(R // rows,), in_specs=in_specs,
        out_specs=pl.BlockSpec((rows, C), lambda i, ids: (i, 0)))
    return pl.pallas_call(
        body, name=name, grid_spec=grid_spec, out_shape=jax.ShapeDtypeStruct((R, C), F32),
        compiler_params=_cparams(("parallel",)),
    )(*args)


class _ReduceScatter:
    def __init__(self, tag, collective_id, big, small=None):
        x, y, c = _mesh_pos()
        self.tag, self.cid, self.small = tag, collective_id, small
        self.ids = jnp.stack([c, 2 * x + y]).astype(jnp.int32)
        self.big = big.reshape((4, 2) + big.shape[1:])
        self.got = _rs_sibling_exchange(tag + "_sibling", collective_id,
                                        [self.big] + ([] if small is None else [small]))

    def chip_sums(self, after=None):
        t16, self.own = _rs_chip_sum(self.tag + "_chip_sum", self.ids, self.big, self.got[0], after)
        arrs = [t16]
        if self.small is not None:
            self.ts = _add2(self.tag + "_chip_sum_small", self.small, self.got[1])
            arrs.append(self.ts)
        self.recv = _rs_chip_exchange(self.tag + "_chips", self.cid + 1, arrs)
        return t16

    def finish(self, after=None):
        out = _rs_final_sum(self.tag + "_final", self.ids, self.recv[0], self.own, _rs_rows(self.own.shape[0]), after)
        out_small = None
        if self.small is not None:
            out_small = _rs_final_sum(self.tag + "_final_small", self.ids, self.recv[1], self.ts, self.small.shape[0])
        return out, out_small


def _all_reduce_small(tag, small):
    x, y, c = _mesh_pos()
    ids = jnp.stack([c, 2 * x + y]).astype(jnp.int32)
    (theirs,) = _rs_sibling_exchange(tag + "_sibling", None, [small])
    ts = _add2(tag + "_chip_sum", small, theirs)
    (recv,) = _rs_chip_exchange(tag + "_chips", None, [ts])
    return _rs_final_sum(tag + "_final", ids, recv, ts, small.shape[0])


def _adamw(name, w, g, m, v):
    R, C = w.shape
    tr = max(t for t in range(8, 513, 8) if R % t == 0)
    c1 = 1.0 - ADAM_B1 ** ADAM_STEP
    c2 = 1.0 - ADAM_B2 ** ADAM_STEP

    def body(w_ref, g_ref, m_ref, v_ref, d_ref, nm_ref, nv_ref):
        gv = g_ref[...]
        nm = ADAM_B1 * m_ref[...] + (1.0 - ADAM_B1) * gv
        nv = ADAM_B2 * v_ref[...] + (1.0 - ADAM_B2) * (gv * gv)
        d_ref[...] = -ADAM_LR * ((nm / c1) / (jnp.sqrt(nv / c2) + ADAM_EPS) + ADAM_WD * w_ref[...])
        nm_ref[...] = nm
        nv_ref[...] = nv

    blk = pl.BlockSpec((tr, C), lambda i: (i, 0))
    shp = jax.ShapeDtypeStruct((R, C), F32)
    return pl.pallas_call(
        body, name=name, grid=(R // tr,), in_specs=[blk] * 4, out_specs=(blk,) * 3, out_shape=(shp,) * 3,
        compiler_params=_cparams(("parallel",)),
    )(w, g, m, v)


def _adamw_t(name, w, g_t, m, v):
    L, R, C = w.shape
    tr = 256
    c1 = 1.0 - ADAM_B1 ** ADAM_STEP
    c2 = 1.0 - ADAM_B2 ** ADAM_STEP

    def body(w_ref, g_ref, m_ref, v_ref, go_ref, d_ref, nm_ref, nv_ref):
        gv = g_ref[...].T
        go_ref[...] = gv
        nm = ADAM_B1 * m_ref[...] + (1.0 - ADAM_B1) * gv
        nv = ADAM_B2 * v_ref[...] + (1.0 - ADAM_B2) * (gv * gv)
        d_ref[...] = -ADAM_LR * ((nm / c1) / (jnp.sqrt(nv / c2) + ADAM_EPS) + ADAM_WD * w_ref[...])
        nm_ref[...] = nm
        nv_ref[...] = nv

    blk = pl.BlockSpec((None, tr, C), lambda l, i: (l, i, 0))
    blk_t = pl.BlockSpec((None, C, tr), lambda l, i: (l, 0, i))
    shp = jax.ShapeDtypeStruct((L, R, C), F32)
    return pl.pallas_call(
        body, name=name, grid=(L, R // tr), in_specs=[blk, blk_t, blk, blk], out_specs=(blk,) * 4,
        out_shape=(shp,) * 4, compiler_params=_cparams(("parallel", "parallel")),
    )(w, g_t, m, v)


REST_FF1 = 0
REST_FF2 = 512
REST_WO = 1024
REST_UP = 1152
REST_ATTN = 1280
REST_ROWS = 1312
IN_ROWS = IN_COLS // N_DEV
CONV_BITS_ROWS = 16
SHARD_ROWS = IN_ROWS + REST_ROWS + CONV_BITS_ROWS
SMALL = (("norm_mix", (DEPTH, D)), ("b_gate", (DEPTH, 3 * D)), ("pool_mix", (DEPTH, 4, POOL_G, POOL_G)),
         ("pool_scale", (DEPTH, POOL_W)), ("conv_w", (DEPTH, 3, CONV_W)), ("q_gain", (DEPTH, HEAD)),
         ("k_gain", (DEPTH, HEAD)), ("norm_mlp", (DEPTH, D)))


def _pack_weight_shards(w, l):
    b = lambda a: a.astype(BF16)
    conv = lax.bitcast_convert_type(w["conv_w"][l], BF16).reshape(3, 128)
    conv = jnp.pad(conv, ((0, CONV_BITS_ROWS - 3), (0, D - 128)))
    rest = jnp.concatenate([
        b(w["w_ff1"][l].T), b(w["w_ff2"][l]), b(w["w_o"][l]),
        jnp.concatenate([b(w["w_pool_up"][l].T), b(w["w_conv_out"][l].T)], axis=1),
        b(w["w_attn_up"][l].T).reshape(REST_ROWS - REST_ATTN, D), conv], axis=0)
    return b(w["w_in"][l].T), rest


def _unpack_gathered(g_in, g_rest, small_w, l):
    rest = g_rest[:, :REST_ROWS]
    take = lambda r0, rows, c0=0, cols=D: rest[:, r0:r0 + rows, c0:c0 + cols].reshape(N_DEV * rows, cols)
    conv = g_rest[:, REST_ROWS:REST_ROWS + 3, :128].reshape(N_DEV, 3, CONV_W // N_DEV, 2)
    conv = lax.bitcast_convert_type(conv, F32)
    p = {
        "w_in_t": g_in.reshape(IN_COLS, D),
        "w_ff1_t": take(REST_FF1, 512), "w_ff2": take(REST_FF2, 512), "w_o": take(REST_WO, 128),
        "w_pool_up_t": take(REST_UP, 128, 0, POOL_W), "w_conv_out_t": take(REST_UP, 128, POOL_W, CONV_W),
        "w_attn_up_t": rest[:, REST_ATTN:].reshape(D, ATTN_OUT),
        "conv_w": jnp.transpose(conv, (1, 0, 2)).reshape(3, CONV_W),
        "pool_mix": small_w["pool_mix"][l],
    }
    for name in ("norm_mix", "b_gate", "pool_scale", "q_gain", "k_gain", "norm_mlp"):
        p[name] = small_w[name][l][None, :]
    return p


def _pack_small_grads(grads):
    flat = jnp.concatenate([jnp.stack([grads[l][name] for l in range(DEPTH)]).reshape(-1) for name, _ in SMALL])
    return jnp.pad(flat, (0, (-flat.shape[0]) % (8 * 128))).reshape(-1, 128)


def _unpack_grads(in_t, rest, small, dev):
    out = {
        "w_in_t": jnp.stack(in_t),
        "w_ff1_t": jnp.stack([a[REST_FF1:REST_FF1 + 512] for a in rest]),
        "w_ff2": jnp.stack([a[REST_FF2:REST_FF2 + 512] for a in rest]),
        "w_o": jnp.stack([a[REST_WO:REST_WO + 128] for a in rest]),
        "w_pool_up": jnp.stack([a[REST_UP:REST_UP + 128, :POOL_W].T for a in rest]),
        "w_conv_out": jnp.stack([a[REST_UP:REST_UP + 128, POOL_W:].T for a in rest]),
        "w_attn_up": jnp.stack([a[REST_ATTN:].reshape(D // N_DEV, ATTN_OUT).T for a in rest]),
    }
    flat = small.reshape(-1)
    off = 0
    for name, shp in SMALL:
        n = 1
        for s in shp:
            n *= s
        out[name] = flat[off:off + n].reshape(shp)
        off += n
    width = CONV_W // N_DEV
    out["conv_w"] = lax.dynamic_slice_in_dim(out["conv_w"], dev * width, width, axis=2)
    return out


def _pack_small(arrs):
    flat = jnp.concatenate([a.reshape(-1) for a in arrs])
    pad = (-flat.shape[0]) % (8 * 128)
    return jnp.pad(flat, (0, pad)).reshape(-1, 128)


def _unpack_small(packed, like):
    flat = packed.reshape(-1)
    out, off = [], 0
    for a in like:
        out.append(flat[off:off + a.size].reshape(a.shape))
        off += a.size
    return out


WEIGHTS = ("norm_mix", "w_in", "b_gate", "pool_mix", "pool_scale", "conv_w", "q_gain", "k_gain", "w_pool_up",
           "w_conv_out", "w_attn_up", "w_o", "norm_mlp", "w_ff1", "w_ff2")


def kernel(x, norm_mix, w_in, b_gate, pool_mix, pool_scale, conv_w, q_gain, k_gain, w_pool_up, w_conv_out, w_attn_up, w_o, norm_mlp, w_ff1, w_ff2, loss_target, m_norm_mix, m_w_in, m_b_gate, m_pool_mix, m_pool_scale, m_conv_w, m_q_gain, m_k_gain, m_w_pool_up, m_w_conv_out, m_w_attn_up, m_w_o, m_norm_mlp, m_w_ff1, m_w_ff2, v_norm_mix, v_w_in, v_b_gate, v_pool_mix, v_pool_scale, v_conv_w, v_q_gain, v_k_gain, v_w_pool_up, v_w_conv_out, v_w_attn_up, v_w_o, v_norm_mlp, v_w_ff1, v_w_ff2):
    w = dict(zip(WEIGHTS, (norm_mix, w_in, b_gate, pool_mix, pool_scale, conv_w, q_gain, k_gain, w_pool_up,
                           w_conv_out, w_attn_up, w_o, norm_mlp, w_ff1, w_ff2)))
    m = dict(zip(WEIGHTS, (m_norm_mix, m_w_in, m_b_gate, m_pool_mix, m_pool_scale, m_conv_w, m_q_gain, m_k_gain,
                           m_w_pool_up, m_w_conv_out, m_w_attn_up, m_w_o, m_norm_mlp, m_w_ff1, m_w_ff2)))
    v = dict(zip(WEIGHTS, (v_norm_mix, v_w_in, v_b_gate, v_pool_mix, v_pool_scale, v_conv_w, v_q_gain, v_k_gain,
                           v_w_pool_up, v_w_conv_out, v_w_attn_up, v_w_o, v_norm_mlp, v_w_ff1, v_w_ff2)))
    xi, yi, ci = _mesh_pos()
    dev = 4 * xi + 2 * yi + ci

    params = []
    for l in range(DEPTH):
        s_in, s_rest = _pack_weight_shards(w, l)
        g_in = _all_gather(f"gather_in{l}", 1 + 2 * l, s_in)
        g_rest = _all_gather(f"gather_rest{l}", 2 + 2 * l, s_rest)
        params.append(_unpack_gathered(g_in, g_rest, w, l))
    saved = []
    act = x[0]
    for l in range(DEPTH):
        act, s = _layer_fwd(l, act, params[l])
        saved.append(s)
    dy, loss = _loss_head("loss_head", act, loss_target[0])
    total = lax.psum(loss[0, 0], ("x", "y", "c"))
    in_t, rest, grads, pending = [None] * DEPTH, [None] * DEPTH, [None] * DEPTH, None
    start = total.reshape(1, 1)
    for l in reversed(range(DEPTH)):
        dy, grads[l], rest[l], done, pending = _layer_bwd(l, dy, params[l], saved[l], pending, 5 + 4 * l, l == 0,
                                                          start)
        start = None
        if done is not None:
            in_t[l + 1] = done[0]
    dx = dy
    g_small = _all_reduce_small("ar_small", _pack_small_grads(grads))
    in_t[0], _ = pending.finish(after=g_small)
    g = _unpack_grads(in_t, rest, g_small, dev)

    delta, new_m, new_v = {}, {}, {}
    for name in ("w_in", "w_ff1"):
        g[name], delta[name], new_m[name], new_v[name] = _adamw_t("adamw_" + name, w[name], g[name + "_t"],
                                                                  m[name], v[name])
    for name in ("w_pool_up", "w_conv_out", "w_attn_up", "w_o", "w_ff2"):
        shp = w[name].shape
        two_d = (shp[0] * shp[1], shp[2])
        d_, m_, v_ = _adamw("adamw_" + name, w[name].reshape(two_d), g[name].reshape(two_d),
                            m[name].reshape(two_d), v[name].reshape(two_d))
        delta[name], new_m[name], new_v[name] = d_.reshape(shp), m_.reshape(shp), v_.reshape(shp)
    small_names = [name for name, _ in SMALL]
    packed = [_pack_small([src[name] for name in small_names]) for src in (w, g, m, v)]
    outs = _adamw("adamw_small", *packed)
    for dst, arr in zip((delta, new_m, new_v), outs):
        for name, val in zip(small_names, _unpack_small(arr, [w[name] for name in small_names])):
            dst[name] = val

    return (total, dx[None], *[g[n] for n in WEIGHTS], *[delta[n] for n in WEIGHTS],
            *[new_m[n] for n in WEIGHTS], *[new_v[n] for n in WEIGHTS])
```

```python
import functools

import jax
import jax.numpy as jnp
from jax import lax
from jax.experimental import pallas as pl
from jax.experimental.pallas import tpu as pltpu
from jax.experimental.pallas import tpu_sc as plsc

F32 = jnp.float32
BF16 = jnp.bfloat16
MESH = pl.DeviceIdType.MESH

D = 1024
SEQ = 4096
DEPTH = 2
N_DEV = 8
POOL_WINDOWS = (2, 4, 8, 16)
POOL_W = 512
POOL_G = 128
CONV_W = 512
HEAD = 64
ATTN_GROUPS = ((128, 1), (512, 4), (2048, 16))
HPG = 4
ATTN_W = 768
ATTN_OUT = 256
ATTN_BLK = 128
D_FF = 4096
EPS = 1e-6
MASK = -1e30
OFF_POOL = 0
OFF_CB = 512
OFF_CC = 1024
OFF_CX = 1536
OFF_Q = 2048
OFF_K = 2816
OFF_V = 3584
OFF_GATE = 4352
IN_COLS = 7424
ADAM_LR = 0.001
ADAM_B1 = 0.9
ADAM_B2 = 0.999
ADAM_EPS = 1e-08
ADAM_WD = 0.01
ADAM_STEP = 10

ROW_TILE = 512
SEQ_CHUNK = 256
HALO = 16
VMEM_LIMIT = 56 * 1024 * 1024


def _cparams(sem=None):
    return pltpu.CompilerParams(dimension_semantics=sem, vmem_limit_bytes=VMEM_LIMIT)


def _call(body, args, in_specs, after=None, **kw):
    if after is not None:
        n, inner = len(args), body

        def body(*refs):
            return inner(*refs[:n], *refs[n + 1:])

        args = list(args) + [after]
        in_specs = list(in_specs) + [pl.BlockSpec(memory_space=pl.ANY)]
    return pl.pallas_call(body, in_specs=in_specs, **kw)(*args)


_DIMS = {"nn": (((1,), (0,)), ((), ())), "nt": (((1,), (1,)), ((), ())), "tn": (((0,), (0,)), ((), ()))}


def _matmul(name, a, b, mode, tm, tn, tk, out_dtypes=(F32,), extras=(), epilogue=None, into=None, after=None):
    if mode == "tn":
        K, M = a.shape
    else:
        M, K = a.shape
    N = b.shape[0] if mode == "nt" else b.shape[1]
    assert M % tm == 0 and N % tn == 0 and K % tk == 0, (name, M, N, K, tm, tn, tk)
    nk = K // tk
    n_extra = len(extras)
    n_out = len(out_dtypes)
    dims = _DIMS[mode]
    n_alias = 0 if into is None or isinstance(into[0], jax.ShapeDtypeStruct) else 1

    def body(a_ref, b_ref, *rest):
        extra_refs = rest[:n_extra]
        out_refs = rest[n_extra + n_alias:n_extra + n_alias + n_out]

        def finish(acc):
            if epilogue is None:
                res = (acc,)
            else:
                res = epilogue(acc, *[r[...] for r in extra_refs])
            for o_ref, v in zip(out_refs, res):
                o_ref[...] = v.astype(o_ref.dtype)

        part = lax.dot_general(a_ref[...].astype(BF16), b_ref[...].astype(BF16), dims,
                               preferred_element_type=F32)
        if nk == 1:
            finish(part)
        else:
            acc_ref = rest[-1]
            k = pl.program_id(2)

            @pl.when(k == 0)
            def _():
                acc_ref[...] = part

            @pl.when(k > 0)
            def _():
                acc_ref[...] += part

            @pl.when(k == nk - 1)
            def _():
                finish(acc_ref[...])

    if mode == "tn":
        a_spec = pl.BlockSpec((tk, tm), lambda i, j, k: (k, i))
    else:
        a_spec = pl.BlockSpec((tm, tk), lambda i, j, k: (i, k))
    if mode == "nt":
        b_spec = pl.BlockSpec((tn, tk), lambda i, j, k: (j, k))
    else:
        b_spec = pl.BlockSpec((tk, tn), lambda i, j, k: (k, j))
    in_specs = [a_spec, b_spec]
    args = [a, b]
    for arr, kind in extras:
        if kind == "mn":
            in_specs.append(pl.BlockSpec((tm, tn), lambda i, j, k: (i, j)))
        else:
            in_specs.append(pl.BlockSpec((1, tn), lambda i, j, k: (0, j)))
        args.append(arr)
    out_shape = tuple(jax.ShapeDtypeStruct((M, N), dt) for dt in out_dtypes)
    out_specs = tuple(pl.BlockSpec((tm, tn), lambda i, j, k: (i, j)) for _ in out_dtypes)
    aliases = {}
    if into is not None:
        buf, row0, col0 = into
        assert n_out == 1 and (M // N_DEV) % tm == 0 and row0 % tm == 0 and col0 % tn == 0
        per_dev = M // N_DEV // tm
        out_shape = (jax.ShapeDtypeStruct(buf.shape, buf.dtype),)
        out_specs = (pl.BlockSpec((None, tm, tn), lambda i, j, k: (i // per_dev, row0 // tm + i % per_dev,
                                                                   col0 // tn + j)),)
        if not isinstance(buf, jax.ShapeDtypeStruct):
            aliases = {len(args): 0}
            in_specs.append(pl.BlockSpec(memory_space=pl.ANY))
            args.append(buf)
    scratch = [] if nk == 1 else [pltpu.VMEM((tm, tn), F32)]
    res = _call(
        body, args, in_specs, after, name=name, grid=(M // tm, N // tn, nk), out_specs=out_specs,
        out_shape=out_shape, scratch_shapes=scratch, input_output_aliases=aliases,
        compiler_params=_cparams(("parallel", "parallel", "arbitrary")))
    return res if n_out > 1 else res[0]


def _rmsnorm_fwd(name, x, gain):
    S_, D_ = x.shape

    def body(x_ref, g_ref, h_ref):
        xf = x_ref[...]
        ms = jnp.mean(xf * xf, axis=-1, keepdims=True)
        h_ref[...] = (xf * lax.rsqrt(ms + EPS) * g_ref[...]).astype(BF16)

    return pl.pallas_call(
        body, name=name, grid=(S_ // ROW_TILE,),
        in_specs=[pl.BlockSpec((ROW_TILE, D_), lambda i: (i, 0)), pl.BlockSpec((1, D_), lambda i: (0, 0))],
        out_specs=pl.BlockSpec((ROW_TILE, D_), lambda i: (i, 0)),
        out_shape=jax.ShapeDtypeStruct((S_, D_), BF16),
        compiler_params=_cparams(("parallel",)),
    )(x, gain)


def _rmsnorm_bwd(name, x, gain, dh, dres, after=None):
    S_, D_ = x.shape
    n = S_ // ROW_TILE

    def body(x_ref, g_ref, dh_ref, dres_ref, dx_ref, dg_ref, acc_ref):
        i = pl.program_id(0)
        xf = x_ref[...]
        rstd = lax.rsqrt(jnp.mean(xf * xf, axis=-1, keepdims=True) + EPS)
        xhat = xf * rstd
        dhv = dh_ref[...]
        dxhat = dhv * g_ref[...]
        c = jnp.mean(dxhat * xhat, axis=-1, keepdims=True)
        dx_ref[...] = dres_ref[...] + rstd * (dxhat - xhat * c)
        part = jnp.sum((dhv * xhat).reshape(ROW_TILE // 8, 8, D_), axis=0)

        @pl.when(i == 0)
        def _():
            acc_ref[...] = part

        @pl.when(i > 0)
        def _():
            acc_ref[...] += part

        @pl.when(i == n - 1)
        def _():
            dg_ref[...] = jnp.sum(acc_ref[...], axis=0, keepdims=True)

    row = pl.BlockSpec((ROW_TILE, D_), lambda i: (i, 0))
    vec = pl.BlockSpec((1, D_), lambda i: (0, 0))
    return _call(
        body, [x, gain, dh, dres], [row, vec, row, row], after, name=name, grid=(n,), out_specs=(row, vec),
        out_shape=(jax.ShapeDtypeStruct((S_, D_), F32), jax.ShapeDtypeStruct((1, D_), F32)),
        scratch_shapes=[pltpu.VMEM((8, D_), F32)],
        compiler_params=_cparams(("arbitrary",)))


def _loss_head(name, y, target):
    S_, D_ = y.shape
    n = S_ // ROW_TILE

    def body(y_ref, t_ref, dy_ref, l_ref, acc_ref):
        i = pl.program_id(0)
        e = y_ref[...] - t_ref[...]
        dy_ref[...] = e * (1.0 / D_)
        part = jnp.sum((e * e).reshape(ROW_TILE // 8, 8, D_), axis=0)

        @pl.when(i == 0)
        def _():
            acc_ref[...] = part

        @pl.when(i > 0)
        def _():
            acc_ref[...] += part

        @pl.when(i == n - 1)
        def _():
            s = jnp.sum(acc_ref[...], axis=1, keepdims=True)
            l_ref[...] = jnp.sum(s, axis=0, keepdims=True) * (0.5 / D_)

    row = pl.BlockSpec((ROW_TILE, D_), lambda i: (i, 0))
    return pl.pallas_call(
        body, name=name, grid=(n,), in_specs=[row, row],
        out_specs=(row, pl.BlockSpec((1, 1), lambda i: (0, 0))),
        out_shape=(jax.ShapeDtypeStruct((S_, D_), F32), jax.ShapeDtypeStruct((1, 1), F32)),
        scratch_shapes=[pltpu.VMEM((8, D_), F32)],
        compiler_params=_cparams(("arbitrary",)),
    )(y, target)


def _rows_with_halo(ref, cols, i, n_chunks, before, after):
    r0 = pl.multiple_of(i * SEQ_CHUNK, SEQ_CHUNK)
    parts = []
    if before:
        h0 = pl.multiple_of(jnp.maximum(r0 - HALO, 0), 8)
        halo = ref[pl.ds(h0, HALO), cols]
        parts.append(jnp.where(i > 0, halo, jnp.zeros_like(halo)))
    parts.append(ref[pl.ds(r0, SEQ_CHUNK), cols])
    if after:
        a0 = pl.multiple_of(jnp.minimum(r0 + SEQ_CHUNK, (n_chunks - 1) * SEQ_CHUNK + SEQ_CHUNK - HALO), 8)
        halo = ref[pl.ds(a0, HALO), cols]
        parts.append(jnp.where(i < n_chunks - 1, halo, jnp.zeros_like(halo)))
    return parts[0] if len(parts) == 1 else jnp.concatenate(parts, axis=0)


def _shift_down(v, k):
    return pltpu.roll(v, k, 0)


def _shift_up(v, k):
    return pltpu.roll(v, v.shape[0] - k, 0)


def _pool_diff(xx, w, t_main):
    s = xx
    k = 1
    while k < w:
        s = s + _shift_down(s, k)
        k *= 2
    cnt = jnp.minimum(t_main + 1, w).astype(F32)
    return s[HALO:] / cnt - xx[HALO:]


def _pool_fwd(name, z, pool_mix, pool_scale):
    S_ = z.shape[0]
    n_chunks = S_ // SEQ_CHUNK

    def body(u_ref, mix_ref, sc_ref, y_ref):
        for g, w in enumerate(POOL_WINDOWS):
            cols = slice(g * POOL_G, (g + 1) * POOL_G)
            mixg = mix_ref[g].astype(BF16)
            scg = sc_ref[:, cols]

            def chunk(i, carry, cols=cols, mixg=mixg, scg=scg, w=w):
                r0 = pl.multiple_of(i * SEQ_CHUNK, SEQ_CHUNK)
                xx = _rows_with_halo(u_ref, cols, i, n_chunks, True, False)
                t = r0 + lax.broadcasted_iota(jnp.int32, (SEQ_CHUNK, POOL_G), 0)
                d = _pool_diff(xx, w, t)
                y = jnp.dot(d.astype(BF16), mixg, preferred_element_type=F32) * scg
                y_ref[pl.ds(r0, SEQ_CHUNK), cols] = y.astype(BF16)
                return carry

            lax.fori_loop(0, n_chunks, chunk, 0)

    return pl.pallas_call(
        body, name=name, grid=(1,),
        in_specs=[pl.BlockSpec((S_, POOL_W), lambda i: (0, 0)),
                  pl.BlockSpec((4, POOL_G, POOL_G), lambda i: (0, 0, 0)),
                  pl.BlockSpec((1, POOL_W), lambda i: (0, 0))],
        out_specs=pl.BlockSpec((S_, POOL_W), lambda i: (0, 0)),
        out_shape=jax.ShapeDtypeStruct((S_, POOL_W), BF16),
        compiler_params=_cparams(("arbitrary",)),
    )(z, pool_mix, pool_scale)


def _pool_bwd(name, z, dy, pool_mix, pool_scale, after=None):
    S_ = z.shape[0]
    n_chunks = S_ // SEQ_CHUNK
    rows_a = SEQ_CHUNK + HALO

    def body(u_ref, dy_ref, mix_ref, sc_ref, du_ref, dmix_ref, dsc_ref):
        for g, w in enumerate(POOL_WINDOWS):
            cols = slice(g * POOL_G, (g + 1) * POOL_G)
            mixg = mix_ref[g].astype(BF16)
            scg = sc_ref[:, cols]

            def chunk(i, carry, cols=cols, mixg=mixg, scg=scg, w=w):
                dmix_acc, dsc_acc = carry
                r0 = pl.multiple_of(i * SEQ_CHUNK, SEQ_CHUNK)
                xx = _rows_with_halo(u_ref, cols, i, n_chunks, True, False)
                t = r0 + lax.broadcasted_iota(jnp.int32, (SEQ_CHUNK, POOL_G), 0)
                d = _pool_diff(xx, w, t).astype(BF16)
                ypre = jnp.dot(d, mixg, preferred_element_type=F32)
                dyy = _rows_with_halo(dy_ref, cols, i, n_chunks, False, True)
                dys = (dyy * scg).astype(BF16)
                dsc_acc = dsc_acc + jnp.sum((dyy[:SEQ_CHUNK] * ypre).reshape(SEQ_CHUNK // 8, 8, POOL_G), axis=0)
                dmix_acc = dmix_acc + lax.dot_general(d, dys[:SEQ_CHUNK], _DIMS["tn"], preferred_element_type=F32)
                dd = lax.dot_general(dys, mixg, _DIMS["nt"], preferred_element_type=F32)
                ta = r0 + lax.broadcasted_iota(jnp.int32, (rows_a, POOL_G), 0)
                f = dd / jnp.minimum(ta + 1, w).astype(F32)
                k = 1
                while k < w:
                    f = f + _shift_up(f, k)
                    k *= 2
                du_ref[pl.ds(r0, SEQ_CHUNK), cols] = (f[:SEQ_CHUNK] - dd[:SEQ_CHUNK]).astype(BF16)
                return dmix_acc, dsc_acc

            dmix_acc, dsc_acc = lax.fori_loop(
                0, n_chunks, chunk, (jnp.zeros((POOL_G, POOL_G), F32), jnp.zeros((8, POOL_G), F32)))
            dmix_ref[g] = dmix_acc
            dsc_ref[:, cols] = jnp.sum(dsc_acc, axis=0, keepdims=True)

    full = pl.BlockSpec((S_, POOL_W), lambda i: (0, 0))
    mix_spec = pl.BlockSpec((4, POOL_G, POOL_G), lambda i: (0, 0, 0))
    vec = pl.BlockSpec((1, POOL_W), lambda i: (0, 0))
    return _call(
        body, [z, dy, pool_mix, pool_scale], [full, full, mix_spec, vec], after, name=name, grid=(1,),
        out_specs=(full, mix_spec, vec),
        out_shape=(jax.ShapeDtypeStruct((S_, POOL_W), BF16), jax.ShapeDtypeStruct((4, POOL_G, POOL_G), F32),
                   jax.ShapeDtypeStruct((1, POOL_W), F32)),
        compiler_params=_cparams(("arbitrary",)))


def _conv_specs(S_):
    slab = lambda off: pl.BlockSpec((S_, 128), lambda c, off=off: (0, off // 128 + c))
    return slab(OFF_CB), slab(OFF_CC), slab(OFF_CX)


def _conv_fwd(name, z, conv_w):
    S_ = z.shape[0]
    n_chunks = S_ // SEQ_CHUNK
    col = slice(0, 128)

    def body(b_ref, c_ref, x_ref, w_ref, y_ref):
        w0, w1, w2 = w_ref[0:1, :], w_ref[1:2, :], w_ref[2:3, :]

        def chunk(i, carry):
            r0 = pl.multiple_of(i * SEQ_CHUNK, SEQ_CHUNK)
            u = _rows_with_halo(c_ref, col, i, n_chunks, True, False) * _rows_with_halo(x_ref, col, i, n_chunks, True, False)
            y = w2 * u + w1 * _shift_down(u, 1) + w0 * _shift_down(u, 2)
            y_ref[pl.ds(r0, SEQ_CHUNK), :] = (b_ref[pl.ds(r0, SEQ_CHUNK), :] * y[HALO:]).astype(BF16)
            return carry

        lax.fori_loop(0, n_chunks, chunk, 0)

    sb, sc, sx = _conv_specs(S_)
    return pl.pallas_call(
        body, name=name, grid=(CONV_W // 128,),
        in_specs=[sb, sc, sx, pl.BlockSpec((3, 128), lambda c: (0, c))],
        out_specs=pl.BlockSpec((S_, 128), lambda c: (0, c)),
        out_shape=jax.ShapeDtypeStruct((S_, CONV_W), BF16),
        compiler_params=_cparams(("parallel",)),
    )(z, z, z, conv_w)


def _conv_bwd(name, z, dy, conv_w, after=None):
    S_ = z.shape[0]
    n_chunks = S_ // SEQ_CHUNK
    col = slice(0, 128)
    lo, hi = HALO, HALO + SEQ_CHUNK

    def body(b_ref, c_ref, x_ref, dy_ref, w_ref, db_ref, dc_ref, dx_ref, dw_ref):
        w0, w1, w2 = w_ref[0:1, :], w_ref[1:2, :], w_ref[2:3, :]

        def chunk(i, carry):
            a0, a1, a2 = carry
            r0 = pl.multiple_of(i * SEQ_CHUNK, SEQ_CHUNK)
            cc = _rows_with_halo(c_ref, col, i, n_chunks, True, True)
            xx = _rows_with_halo(x_ref, col, i, n_chunks, True, True)
            bb = _rows_with_halo(b_ref, col, i, n_chunks, True, True)
            dyy = _rows_with_halo(dy_ref, col, i, n_chunks, True, True)
            u = cc * xx
            u1 = _shift_down(u, 1)
            u2 = _shift_down(u, 2)
            y = w2 * u + w1 * u1 + w0 * u2
            dyv = dyy * bb
            du = w2 * dyv + w1 * _shift_up(dyv, 1) + w0 * _shift_up(dyv, 2)
            db_ref[pl.ds(r0, SEQ_CHUNK), :] = (dyy[lo:hi] * y[lo:hi]).astype(BF16)
            dc_ref[pl.ds(r0, SEQ_CHUNK), :] = (du[lo:hi] * xx[lo:hi]).astype(BF16)
            dx_ref[pl.ds(r0, SEQ_CHUNK), :] = (du[lo:hi] * cc[lo:hi]).astype(BF16)
            red = lambda v: jnp.sum(v.reshape(SEQ_CHUNK // 8, 8, 128), axis=0)
            dm = dyv[lo:hi]
            return a0 + red(dm * u2[lo:hi]), a1 + red(dm * u1[lo:hi]), a2 + red(dm * u[lo:hi])

        zero = jnp.zeros((8, 128), F32)
        a0, a1, a2 = lax.fori_loop(0, n_chunks, chunk, (zero, zero, zero))
        dw_ref[0:1, :] = jnp.sum(a0, axis=0, keepdims=True)
        dw_ref[1:2, :] = jnp.sum(a1, axis=0, keepdims=True)
        dw_ref[2:3, :] = jnp.sum(a2, axis=0, keepdims=True)

    sb, sc, sx = _conv_specs(S_)
    slab = pl.BlockSpec((S_, 128), lambda c: (0, c))
    wspec = pl.BlockSpec((3, 128), lambda c: (0, c))
    act = jax.ShapeDtypeStruct((S_, CONV_W), BF16)
    return _call(
        body, [z, z, z, dy, conv_w], [sb, sc, sx, slab, wspec], after, name=name, grid=(CONV_W // 128,),
        out_specs=(slab, slab, slab, wspec),
        out_shape=(act, act, act, jax.ShapeDtypeStruct((3, CONV_W), F32)),
        compiler_params=_cparams(("parallel",)))


def _head_ones(pw):
    a = lax.broadcasted_iota(jnp.int32, (pw, pw), 0) // HEAD
    b = lax.broadcasted_iota(jnp.int32, (pw, pw), 1) // HEAD
    return (a == b).astype(BF16)


def _head_sum(v, ones):
    hi = v.astype(BF16)
    lo = (v - hi.astype(F32)).astype(BF16)
    return jnp.dot(hi, ones, preferred_element_type=F32) + jnp.dot(lo, ones, preferred_element_type=F32)


def _head_norm(v, gain, ones):
    rstd = lax.rsqrt(_head_sum(v * v, ones) * (1.0 / HEAD) + EPS)
    xhat = v * rstd
    return xhat * gain, xhat, rstd


def _head_norm_bwd(dy, xhat, rstd, gain, ones):
    dxhat = dy * gain
    c = _head_sum(dxhat * xhat, ones) * (1.0 / HEAD)
    dv = rstd * (dxhat - xhat * c)
    dg = jnp.sum((dy * xhat).reshape(dy.shape[0] // 8, 8, dy.shape[1]), axis=0)
    return dv, dg


def _head_masks(pw):
    lane_head = lax.broadcasted_iota(jnp.int32, (1, pw), 1) // HEAD
    return [lane_head == h for h in range(pw // HEAD)]


def _only(mask, v):
    return jnp.where(mask, v, jnp.zeros_like(v))


def _attn_specs(S_, g, dil):
    rows = ATTN_BLK * dil
    nb = S_ // rows
    pw = 128 if dil > 1 else ATTN_OUT
    cq, ck, cv = ((OFF_Q + g * ATTN_OUT) // pw, (OFF_K + g * ATTN_OUT) // pw, (OFF_V + g * ATTN_OUT) // pw)
    return rows, nb, pw, pw // HEAD, ATTN_OUT // pw, cq, ck, cv


ATTN_BATCH = 4


def _attn_group(dil):
    return 4 if dil == 1 else 1


def _attn_block_specs(rows, grp, pw, last=None):
    step = (lambda n: n) if last is None else (lambda n: jnp.minimum(n, last))
    cur = lambda c: pl.BlockSpec((rows * grp, pw), lambda hp, n, c=c: (step(n), c + hp))
    prev = lambda c: pl.BlockSpec((rows, pw), lambda hp, n, c=c: (jnp.maximum(step(n) * grp - 1, 0), c + hp))
    return cur, prev


def _band_mask(has_prev):
    qi = lax.broadcasted_iota(jnp.int32, (ATTN_BLK, 2 * ATTN_BLK), 0)
    ki = lax.broadcasted_iota(jnp.int32, (ATTN_BLK, 2 * ATTN_BLK), 1)
    in_prev = jnp.logical_and(ki < ATTN_BLK, ki >= qi)
    if has_prev is not True:
        in_prev = jnp.logical_and(in_prev, has_prev)
    return jnp.logical_or(in_prev, jnp.logical_and(ki >= ATTN_BLK, ki - ATTN_BLK <= qi))


def _rows_of(ref, r, dil):
    if dil == 1:
        return ref[r * ATTN_BLK:(r + 1) * ATTN_BLK, :]
    return ref[pl.ds(r, ATTN_BLK, stride=dil), :]


def _put_rows(ref, r, dil, val):
    if dil == 1:
        ref[r * ATTN_BLK:(r + 1) * ATTN_BLK, :] = val.astype(ref.dtype)
    else:
        ref[pl.ds(r, ATTN_BLK, stride=dil), :] = val.astype(ref.dtype)


def _attn_fwd(name, z, q_gain, k_gain, g, dil):
    S_ = z.shape[0]
    rows, nb, pw, heads, npairs, cq, ck, cv = _attn_specs(S_, g, dil)
    scale = HEAD ** -0.5

    grp = _attn_group(dil)
    nsteps = nb // grp

    def body(q_ref, k_ref, kp_ref, v_ref, vp_ref, gq_ref, gk_ref, o_ref, l_ref):
        n = pl.program_id(1)
        ones, hmask = _head_ones(pw), _head_masks(pw)
        gq, gk = jnp.tile(gq_ref[...], (1, heads)), jnp.tile(gk_ref[...], (1, heads))
        mask_first, mask_rest = _band_mask(n > 0), _band_mask(True)
        for r0 in range(0, dil * grp, ATTN_BATCH):
            rs = range(r0, min(r0 + ATTN_BATCH, dil * grp))
            qn, kn, vv, s, p = {}, {}, {}, {}, {}
            kcn = {}
            for r in rs:
                q, kc, vc = _rows_of(q_ref, r, dil), _rows_of(k_ref, r, dil), _rows_of(v_ref, r, dil)
                kcn[r] = _head_norm(kc, gk, ones)[0]
                if dil == 1 and r > 0:
                    kpn = kcn[r - 1] if r - 1 in kcn else _head_norm(_rows_of(k_ref, r - 1, dil), gk, ones)[0]
                    vp = _rows_of(v_ref, r - 1, dil)
                else:
                    kpn, vp = _head_norm(_rows_of(kp_ref, r, dil), gk, ones)[0], _rows_of(vp_ref, r, dil)
                qn[r] = _head_norm(q, gq, ones)[0].astype(BF16)
                kn[r] = jnp.concatenate([kpn, kcn[r]], axis=0).astype(BF16)
                vv[r] = jnp.concatenate([vp, vc], axis=0).astype(BF16)
            keys = [(r, h) for r in rs for h in range(heads)]
            for r, h in keys:
                s[r, h] = lax.dot_general(_only(hmask[h], qn[r]), kn[r], _DIMS["nt"],
                                          preferred_element_type=F32) * scale
            lse, den = {}, {}
            for key in keys:
                mask = mask_rest if (dil == 1 and key[0] > 0) else mask_first
                sm = jnp.where(mask, s[key], MASK)
                m = jnp.max(sm, axis=-1, keepdims=True)
                e = jnp.exp(sm - m)
                den[key] = jnp.sum(e, axis=-1, keepdims=True)
                p[key] = e.astype(BF16)
                lse[key] = m + jnp.log(den[key])
            for r in rs:
                out = jnp.zeros((ATTN_BLK, pw), F32)
                lse_all = jnp.zeros((ATTN_BLK, pw), F32)
                for h in range(heads):
                    out = jnp.where(hmask[h], jnp.dot(p[r, h], vv[r], preferred_element_type=F32) / den[r, h], out)
                    lse_all = jnp.where(hmask[h], lse[r, h], lse_all)
                _put_rows(o_ref, r, dil, out)
                _put_rows(l_ref, r, dil, lse_all)

    cur, prev = _attn_block_specs(rows, grp, pw)
    gspec = pl.BlockSpec((1, HEAD), lambda hp, n: (0, 0))
    shp = jax.ShapeDtypeStruct((S_, ATTN_OUT), F32)
    return pl.pallas_call(
        body, name=name, grid=(npairs, nsteps),
        in_specs=[cur(cq), cur(ck), prev(ck), cur(cv), prev(cv), gspec, gspec],
        out_specs=(cur(0), cur(0)), out_shape=(shp, shp),
        compiler_params=_cparams(("parallel", "parallel")),
    )(z, z, z, z, z, q_gain, k_gain)


def _attn_combine(name, os_, ls_):
    S_ = os_[0].shape[0]

    def body(o0, o1, o2, l0, l1, l2, o_ref, l_ref):
        a, b, c = l0[...], l1[...], l2[...]
        m = jnp.maximum(jnp.maximum(a, b), c)
        ea, eb, ec = jnp.exp(a - m), jnp.exp(b - m), jnp.exp(c - m)
        zsum = ea + eb + ec
        o_ref[...] = (ea * o0[...] + eb * o1[...] + ec * o2[...]) / zsum
        l_ref[...] = m + jnp.log(zsum)

    row = pl.BlockSpec((ROW_TILE, ATTN_OUT), lambda i: (i, 0))
    shp = jax.ShapeDtypeStruct((S_, ATTN_OUT), F32)
    return pl.pallas_call(
        body, name=name, grid=(S_ // ROW_TILE,), in_specs=[row] * 6, out_specs=(row, row), out_shape=(shp, shp),
        compiler_params=_cparams(("parallel",)),
    )(*os_, *ls_)


def _attn_bwd(name, z, q_gain, k_gain, do, o, lse, g, dil, after=None):
    S_ = z.shape[0]
    rows, nb, pw, heads, npairs, cq, ck, cv = _attn_specs(S_, g, dil)
    scale = HEAD ** -0.5

    def body(q_ref, kc_ref, kp_ref, vc_ref, vp_ref, gq_ref, gk_ref, do_ref, o_ref, l_ref,
             dq_ref, dk_ref, dv_ref, dgq_ref, dgk_ref, ck_ref, cvv_ref, gq_acc, gk_acc):
        hp = pl.program_id(0)
        n = pl.program_id(1)
        live = n < nb
        mask = jnp.logical_and(_band_mask(n > 0), live)
        ones, hmask = _head_ones(pw), _head_masks(pw)
        gq, gk = jnp.tile(gq_ref[...], (1, heads)), jnp.tile(gk_ref[...], (1, heads))

        @pl.when(n == 0)
        def _():
            ck_ref[...] = jnp.zeros_like(ck_ref)
            cvv_ref[...] = jnp.zeros_like(cvv_ref)

        @pl.when(jnp.logical_and(n == 0, hp == 0))
        def _():
            gq_acc[...] = jnp.zeros_like(gq_acc)
            gk_acc[...] = jnp.zeros_like(gk_acc)

        dgq = jnp.zeros((8, pw), F32)
        dgk = jnp.zeros((8, pw), F32)
        for r0 in range(0, dil, ATTN_BATCH):
            rs = range(r0, min(r0 + ATTN_BATCH, dil))
            keys = [(r, h) for r in rs for h in range(heads)]
            qn, qhat, qrstd, kn, kphat, kprstd, vv, dob, delta, lse = ({} for _ in range(10))
            for r in rs:
                q, kc, kp = _rows_of(q_ref, r, dil), _rows_of(kc_ref, r, dil), _rows_of(kp_ref, r, dil)
                dov = _rows_of(do_ref, r, dil)
                qn_f, qhat[r], qrstd[r] = _head_norm(q, gq, ones)
                kpn, kphat[r], kprstd[r] = _head_norm(kp, gk, ones)
                qn[r] = qn_f.astype(BF16)
                kn[r] = jnp.concatenate([kpn, _head_norm(kc, gk, ones)[0]], axis=0).astype(BF16)
                vv[r] = jnp.concatenate([_rows_of(vp_ref, r, dil), _rows_of(vc_ref, r, dil)], axis=0).astype(BF16)
                dob[r] = dov.astype(BF16)
                delta[r] = _head_sum(dov * _rows_of(o_ref, r, dil), ones)
                lse[r] = _rows_of(l_ref, r, dil)
            s, dp = {}, {}
            for r, h in keys:
                s[r, h] = lax.dot_general(_only(hmask[h], qn[r]), kn[r], _DIMS["nt"],
                                          preferred_element_type=F32) * scale
                dp[r, h] = lax.dot_general(_only(hmask[h], dob[r]), vv[r], _DIMS["nt"], preferred_element_type=F32)
            p, ds = {}, {}
            for r, h in keys:
                col = slice(h * HEAD, h * HEAD + 1)
                pk = jnp.where(mask, jnp.exp(jnp.where(mask, s[r, h], MASK) - lse[r][:, col]), 0.0)
                ds[r, h] = (pk * (dp[r, h] - delta[r][:, col]) * scale).astype(BF16)
                p[r, h] = pk.astype(BF16)
            dqn, dkn, dvv = {}, {}, {}
            for r in rs:
                dqn[r] = jnp.zeros((ATTN_BLK, pw), F32)
                dkn[r] = jnp.zeros((2 * ATTN_BLK, pw), F32)
                dvv[r] = jnp.zeros((2 * ATTN_BLK, pw), F32)
                for h in range(heads):
                    dqn[r] = jnp.where(hmask[h], jnp.dot(ds[r, h], kn[r], preferred_element_type=F32), dqn[r])
                    dkn[r] = jnp.where(hmask[h], lax.dot_general(ds[r, h], qn[r], _DIMS["tn"],
                                                                 preferred_element_type=F32), dkn[r])
                    dvv[r] = jnp.where(hmask[h], lax.dot_general(p[r, h], dob[r], _DIMS["tn"],
                                                                 preferred_element_type=F32), dvv[r])
            for r in rs:
                dq_all, dg = _head_norm_bwd(dqn[r], qhat[r], qrstd[r], gq, ones)
                dgq = dgq + dg
                dk_all, dg = _head_norm_bwd(_rows_of(ck_ref, r, dil) + dkn[r][:ATTN_BLK], kphat[r], kprstd[r], gk, ones)
                dgk = dgk + dg
                dv_all = _rows_of(cvv_ref, r, dil) + dvv[r][:ATTN_BLK]

                @pl.when(live)
                def _(dq_all=dq_all, r=r):
                    _put_rows(dq_ref, r, dil, dq_all)

                _put_rows(dk_ref, r, dil, dk_all)
                _put_rows(dv_ref, r, dil, dv_all)
                _put_rows(ck_ref, r, dil, dkn[r][ATTN_BLK:])
                _put_rows(cvv_ref, r, dil, dvv[r][ATTN_BLK:])
        gq_acc[...] += dgq
        gk_acc[...] += dgk

        @pl.when(jnp.logical_and(n == nb, hp == npairs - 1))
        def _():
            fold = lambda a: sum(a[:, h * HEAD:(h + 1) * HEAD] for h in range(heads))
            dgq_ref[...] = jnp.sum(fold(gq_acc[...]), axis=0, keepdims=True)
            dgk_ref[...] = jnp.sum(fold(gk_acc[...]), axis=0, keepdims=True)

    last = nb - 1
    cur = lambda c: pl.BlockSpec((rows, pw), lambda hp, n, c=c: (jnp.minimum(n, last), c + hp))
    prev = lambda c: pl.BlockSpec((rows, pw), lambda hp, n, c=c: (jnp.maximum(n - 1, 0), c + hp))
    gspec = pl.BlockSpec((1, HEAD), lambda hp, n: (0, 0))
    act = jax.ShapeDtypeStruct((S_, ATTN_OUT), F32)
    vec = jax.ShapeDtypeStruct((1, HEAD), F32)
    return _call(
        body, [z, z, z, z, z, q_gain, k_gain, do, o, lse],
        [cur(cq), cur(ck), prev(ck), cur(cv), prev(cv), gspec, gspec, cur(0), cur(0), cur(0)], after,
        name=name, grid=(npairs, nb + 1),
        out_specs=(cur(0), prev(0), prev(0), gspec, gspec),
        out_shape=(act, act, act, vec, vec),
        scratch_shapes=[pltpu.VMEM((rows, pw), F32), pltpu.VMEM((rows, pw), F32),
                        pltpu.VMEM((8, pw), F32), pltpu.VMEM((8, pw), F32)],
        compiler_params=_cparams(("arbitrary", "arbitrary")))


MIX_TN = 256
MIX_TM = 2048


def _sigmoid(v):
    return 1.0 / (1.0 + jnp.exp(-v))


def _mix_fwd(name, z, b_gate, ys, ws):
    S_ = z.shape[0]
    tm, tn = MIX_TM, MIX_TN
    gblk = OFF_GATE // tn

    def body(yp, yc, ya, wp, wc, wa, g0, g1, g2, b0, b1, b2, m_ref):
        acc = None
        for y_ref, w_ref, g_ref, b_ref in ((yp, wp, g0, b0), (yc, wc, g1, b1), (ya, wa, g2, b2)):
            u = lax.dot_general(y_ref[...].astype(BF16), w_ref[...], _DIMS["nt"], preferred_element_type=F32)
            t = _sigmoid(g_ref[...] + b_ref[...]) * u
            acc = t if acc is None else acc + t
        m_ref[...] = acc.astype(BF16)

    yspec = lambda w: pl.BlockSpec((tm, w), lambda i, j: (i, 0))
    wspec = lambda w: pl.BlockSpec((tn, w), lambda i, j: (j, 0))
    gspec = lambda b: pl.BlockSpec((tm, tn), lambda i, j, b=b: (i, gblk + b * (D // tn) + j))
    bspec = lambda b: pl.BlockSpec((1, tn), lambda i, j, b=b: (0, b * (D // tn) + j))
    return pl.pallas_call(
        body, name=name, grid=(S_ // tm, D // tn),
        in_specs=[yspec(POOL_W), yspec(CONV_W), yspec(ATTN_OUT), wspec(POOL_W), wspec(CONV_W), wspec(ATTN_OUT),
                  gspec(0), gspec(1), gspec(2), bspec(0), bspec(1), bspec(2)],
        out_specs=pl.BlockSpec((tm, tn), lambda i, j: (i, j)),
        out_shape=jax.ShapeDtypeStruct((S_, D), BF16),
        compiler_params=_cparams(("parallel", "parallel")),
    )(*ys, *ws, z, z, z, b_gate, b_gate, b_gate)


def _mix_bwd(name, z, b_gate, y, w, dmerged, branch, after=None):
    S_ = z.shape[0]
    tm, tn = MIX_TM, MIX_TN
    width = y.shape[1]
    gblk = OFF_GATE // tn + branch * (D // tn)
    ni = S_ // tm

    def body(y_ref, w_ref, g_ref, b_ref, dm_ref, du_ref, dg_ref, db_ref, acc_ref):
        i = pl.program_id(1)
        u = lax.dot_general(y_ref[...].astype(BF16), w_ref[...], _DIMS["nt"], preferred_element_type=F32)
        sg = _sigmoid(g_ref[...] + b_ref[...])
        dm = dm_ref[...]
        du_ref[...] = (sg * dm).astype(BF16)
        dpre = dm * u * sg * (1.0 - sg)
        dg_ref[...] = dpre.astype(BF16)
        part = jnp.sum(dpre.reshape(tm // 8, 8, tn), axis=0)

        @pl.when(i == 0)
        def _():
            acc_ref[...] = part

        @pl.when(i > 0)
        def _():
            acc_ref[...] += part

        @pl.when(i == ni - 1)
        def _():
            db_ref[...] = jnp.sum(acc_ref[...], axis=0, keepdims=True)

    blk = pl.BlockSpec((tm, tn), lambda j, i: (i, j))
    vec = pl.BlockSpec((1, tn), lambda j, i: (0, j))
    act = jax.ShapeDtypeStruct((S_, D), BF16)
    return _call(
        body, [y, w, z, b_gate, dmerged],
        [pl.BlockSpec((tm, width), lambda j, i: (i, 0)), pl.BlockSpec((tn, width), lambda j, i: (j, 0)),
         pl.BlockSpec((tm, tn), lambda j, i: (i, gblk + j)),
         pl.BlockSpec((1, tn), lambda j, i: (0, branch * (D // tn) + j)), blk], after,
        name=name, grid=(D // tn, ni),
        out_specs=(blk, blk, vec), out_shape=(act, act, jax.ShapeDtypeStruct((1, D), F32)),
        scratch_shapes=[pltpu.VMEM((8, tn), F32)],
        compiler_params=_cparams(("parallel", "arbitrary")))


def _relu2_epilogue(acc):
    r = jnp.maximum(acc, 0.0)
    return acc, r * r


def _layer_fwd(l, x, p):
    t = f"l{l}_"
    h = _rmsnorm_fwd(t + "norm_mix", x, p["norm_mix"])
    z = _matmul(t + "in_proj", h, p["w_in_t"], "nt", 512, 3712, 1024)
    y_pool = _pool_fwd(t + "pool", z, p["pool_mix"], p["pool_scale"])
    y_conv = _conv_fwd(t + "conv", z, p["conv_w"])
    os_, ls_ = [], []
    for g, (_, dil) in enumerate(ATTN_GROUPS):
        o_g, l_g = _attn_fwd(t + f"attn{g}", z, p["q_gain"], p["k_gain"], g, dil)
        os_.append(o_g)
        ls_.append(l_g)
    y_attn, lse = _attn_combine(t + "attn_mix", os_, ls_)
    merged = _mix_fwd(t + "merge", z, p["b_gate"], (y_pool, y_conv, y_attn),
                      (p["w_pool_up_t"], p["w_conv_out_t"], p["w_attn_up_t"]))
    x1 = _matmul(t + "out_proj", merged, p["w_o"], "nn", 1024, 1024, 1024,
                 extras=((x, "mn"),), epilogue=lambda acc, xr: (xr + acc,))
    h2 = _rmsnorm_fwd(t + "norm_mlp", x1, p["norm_mlp"])
    a, r = _matmul(t + "ff1", h2, p["w_ff1_t"], "nt", 1024, 1024, 1024, out_dtypes=(F32, BF16),
                   epilogue=_relu2_epilogue)
    x2 = _matmul(t + "ff2", r, p["w_ff2"], "nn", 1024, 1024, 1024,
                 extras=((x1, "mn"),), epilogue=lambda acc, xr: (xr + acc,))
    saved = dict(x=x, h=h, z=z, y_pool=y_pool, y_conv=y_conv, y_attn=y_attn, lse=lse, merged=merged,
                 x1=x1, h2=h2, a=a, r=r)
    return x2, saved


def _layer_bwd(l, dx2, p, s, pending, collective_id, last, start_after=None):
    t = f"l{l}_b_"
    g = {}
    rest = jax.ShapeDtypeStruct((N_DEV, REST_ROWS, D), F32)
    da = _matmul(t + "d_ff2_in", dx2, p["w_ff2"], "nt", 1024, 1024, 1024, out_dtypes=(BF16,),
                 extras=((s["a"], "mn"),), epilogue=lambda acc, a: (acc * (2.0 * jnp.maximum(a, 0.0)),),
                 after=start_after)
    rest = _matmul(t + "dw_ff2", s["r"], dx2, "tn", 512, 1024, 4096, into=(rest, REST_FF2, 0), after=da)
    tok = rest if pending is None else pending.chip_sums(after=rest)
    dh2 = _matmul(t + "d_ff1_in", da, p["w_ff1_t"], "nn", 1024, 1024, 1024, after=tok)
    rest = _matmul(t + "dw_ff1", da, s["h2"], "tn", 512, 1024, 4096, into=(rest, REST_FF1, 0), after=dh2)
    dx1, g["norm_mlp"] = _rmsnorm_bwd(t + "norm_mlp", s["x1"], p["norm_mlp"], dh2, dx2, after=rest)
    dmerged = _matmul(t + "d_out_proj_in", dx1, p["w_o"], "nt", 1024, 1024, 1024)
    rest = _matmul(t + "dw_o", s["merged"], dx1, "tn", 128, 1024, 4096, into=(rest, REST_WO, 0), after=dmerged)
    ys = (s["y_pool"], s["y_conv"], s["y_attn"])
    names = ("w_pool_up_t", "w_conv_out_t", "w_attn_up_t")
    dys, dgates, dbs = [], [], []
    tok = rest
    for b in range(3):
        du, dgz, db = _mix_bwd(t + f"merge{b}", s["z"], p["b_gate"], ys[b], p[names[b]], dmerged, b, after=tok)
        width = ys[b].shape[1]
        dys.append(_matmul(t + f"d_up{b}_in", du, p[names[b]], "nn", 512, width, 1024))
        if b < 2:
            rest = _matmul(t + f"dw_up{b}", du, ys[b], "tn", 128, width, 4096, into=(rest, REST_UP, b * width),
                           after=dys[b])
            tok = rest
        else:
            d_attn_up_t = _matmul(t + f"dw_up{b}", du, ys[b], "tn", 128, width, 4096, after=dys[b])
            rest = rest.at[:, REST_ATTN:REST_ROWS, :].set(d_attn_up_t.reshape(N_DEV, REST_ROWS - REST_ATTN, D))
        dgates.append(dgz)
        dbs.append(db)
    g["b_gate"] = jnp.concatenate(dbs, axis=1)
    rs_rest = _ReduceScatter(f"rs_rest{l}", collective_id, rest)
    pending_sum = None if pending is None else pending.finish(after=rest)
    du_pool, g["pool_mix"], g["pool_scale"] = _pool_bwd(t + "pool", s["z"], dys[0], p["pool_mix"], p["pool_scale"],
                                                        after=rest if pending is None else pending_sum[0])
    dcb, dcc, dcx, g["conv_w"] = _conv_bwd(t + "conv", s["z"], dys[1], p["conv_w"], after=du_pool)
    tok = rs_rest.chip_sums(after=dcb)
    dqs, dks, dvs = [], [], []
    gq = gk = None
    for gi, (_, dil) in enumerate(ATTN_GROUPS):
        dq, dk, dv, dgq, dgk = _attn_bwd(t + f"attn{gi}", s["z"], p["q_gain"], p["k_gain"], dys[2], s["y_attn"],
                                         s["lse"], gi, dil, after=tok)
        tok = dq
        dqs.append(dq)
        dks.append(dk)
        dvs.append(dv)
        gq = dgq if gq is None else gq + dgq
        gk = dgk if gk is None else gk + dgk
    g["q_gain"], g["k_gain"] = gq, gk
    dz = jnp.concatenate([du_pool, dcb, dcc, dcx] + [v.astype(BF16) for v in dqs + dks + dvs] + dgates, axis=1)
    in_t = _matmul(t + "dw_in", dz, s["h"], "tn", 256, 1024, 4096)
    rs_in = _ReduceScatter(f"rs_in{l}", collective_id + 2, in_t.reshape(N_DEV, IN_ROWS, D))
    dh = _matmul(t + "d_in_proj_in", dz, p["w_in_t"], "nn", 512, 1024, 3712, after=in_t)
    tok = rs_in.chip_sums(after=dh) if last else dh
    rest_sum, _ = rs_rest.finish(after=tok)
    dx, g["norm_mix"] = _rmsnorm_bwd(t + "norm_mix", s["x"], p["norm_mix"], dh, dx1, after=rest_sum)
    return dx, g, rest_sum, pending_sum, rs_in


ANY = pl.BlockSpec(memory_space=pl.ANY)


def _mesh_pos():
    return lax.axis_index("x"), lax.axis_index("y"), lax.axis_index("c")


def _other_chips(x, y):
    return [(1 - x, y), (x, 1 - y), (1 - x, 1 - y)]


def _comm_call(name, collective_id, peers, body, arrs, out_shape, sem_counts):
    n_in, n_out = len(arrs), len(out_shape)
    if collective_id is None:
        def tc_body(*refs):
            body(refs[:n_in], refs[n_in:n_in + n_out], *refs[n_in + n_out:])

        return pl.pallas_call(
            tc_body, name=name, out_shape=tuple(out_shape), in_specs=[ANY] * n_in, out_specs=(ANY,) * n_out,
            scratch_shapes=[pltpu.SemaphoreType.DMA((n,)) for n in sem_counts])(*arrs)

    def seq_body(*refs):
        barrier = pltpu.get_barrier_semaphore()
        ps = peers(*_mesh_pos())
        for p in ps:
            pl.semaphore_signal(barrier, inc=1, device_id=p, device_id_type=MESH)
        pl.semaphore_wait(barrier, len(ps))
        body(refs[:n_in], refs[n_in:n_in + n_out], *refs[n_in + n_out:])

    return pl.kernel(
        seq_body, out_type=tuple(out_shape), mesh=plsc.ScalarSubcoreMesh(axis_name="seq", num_cores=1), name=name,
        scratch_types=[pltpu.SemaphoreType.DMA((n,)) for n in sem_counts],
        compiler_params=pltpu.CompilerParams(collective_id=collective_id),
    )(*arrs)


def _all_gather(name, collective_id, shard):
    R, C = shard.shape

    def peers(x, y, c):
        return [(x, y, 1 - c)] + [(*chip, c) for chip in _other_chips(x, y)]

    def body(in_refs, out_refs, send_sems, recv_sems, local_sems):
        (x_ref,), (out_ref,) = in_refs, out_refs
        x, y, c = _mesh_pos()
        me, sibling = (x, y, c), (x, y, 1 - c)
        chips = _other_chips(x, y)

        def slot(px, py, pc):
            return out_ref.at[4 * px + 2 * py + pc]

        def copy(k, block, to, src=None):
            return pltpu.make_async_remote_copy(
                src_ref=slot(*block) if src is None else src, dst_ref=slot(*block),
                send_sem=send_sems.at[k], recv_sem=recv_sems.at[k], device_id=to, device_id_type=MESH)

        mine = pltpu.make_async_copy(x_ref, slot(*me), local_sems.at[0])
        mine.start()
        first = [copy(0, me, sibling, src=x_ref)]
        first += [copy(1 + j, me, (*chip, c), src=x_ref) for j, chip in enumerate(chips)]
        for cp in first:
            cp.start()
        passed = [copy(4 + j, (*chip, c), sibling) for j, chip in enumerate(chips)]
        for j, chip in enumerate(chips):
            copy(1 + j, (*chip, c), me).wait_recv()
            passed[j].start()
        copy(0, sibling, me).wait_recv()
        for j, chip in enumerate(chips):
            copy(4 + j, (*chip, 1 - c), me).wait_recv()
        for cp in first + passed:
            cp.wait_send()
        mine.wait()

    return _comm_call(name, collective_id, peers, body, [shard],
                      [jax.ShapeDtypeStruct((N_DEV, R, C), shard.dtype)], (7, 7, 1))[0]


def _rs_sibling_exchange(name, collective_id, arrs):
    n = len(arrs)

    def body(in_refs, out_refs, send_sems, recv_sems):
        x, y, c = _mesh_pos()
        cps = []
        for k, (src, dst) in enumerate(zip(in_refs, out_refs)):
            src = src.at[:, 1 - c] if len(src.shape) == 4 else src
            cps.append(pltpu.make_async_remote_copy(src_ref=src, dst_ref=dst, send_sem=send_sems.at[k],
                                                    recv_sem=recv_sems.at[k], device_id=(x, y, 1 - c),
                                                    device_id_type=MESH))
        for cp in cps:
            cp.start()
        for cp in cps:
            cp.wait()

    out_shape = [jax.ShapeDtypeStruct(a.shape[:1] + a.shape[2:] if a.ndim == 4 else a.shape, a.dtype) for a in arrs]
    return _comm_call(name, collective_id, lambda x, y, c: [(x, y, 1 - c)], body, arrs, out_shape, (n, n))


def _rs_rows(r):
    return r // 2 if (r // 2) % 16 == 0 else r


def _drop_ref(body, idx):
    def wrapped(*refs):
        return body(*refs[:idx], *refs[idx + 1:])

    return wrapped


def _rs_chip_sum(name, ids, big, rbig, after=None):
    _, _, R, C = big.shape
    rows = _rs_rows(R)

    def body(ids_ref, mine_ref, theirs_ref, t16_ref, own_ref):
        p = pl.program_id(1)
        s = mine_ref[...] + theirs_ref[...]
        t16_ref[...] = s.astype(BF16)

        @pl.when(p == ids_ref[1])
        def _():
            own_ref[...] = s

    in_specs = [pl.BlockSpec((None, None, rows, C), lambda i, p, ids: (p, ids[0], i, 0)),
                pl.BlockSpec((None, rows, C), lambda i, p, ids: (p, i, 0))]
    args = [ids, big, rbig]
    if after is not None:
        body = _drop_ref(body, len(args))
        in_specs.append(ANY)
        args.append(after)
    grid_spec = pltpu.PrefetchScalarGridSpec(
        num_scalar_prefetch=1, grid=(R // rows, 4), in_specs=in_specs,
        out_specs=(pl.BlockSpec((None, rows, C), lambda i, p, ids: (p, i, 0)),
                   pl.BlockSpec((rows, C), lambda i, p, ids: (i, 0))))
    return pl.pallas_call(
        body, name=name, grid_spec=grid_spec,
        out_shape=(jax.ShapeDtypeStruct((4, R, C), BF16), jax.ShapeDtypeStruct((R, C), F32)),
        compiler_params=_cparams(("parallel", "arbitrary")),
    )(*args)


def _add2(name, a, b):
    def body(a_ref, b_ref, o_ref):
        o_ref[...] = a_ref[...] + b_ref[...]

    full = pl.BlockSpec(a.shape, lambda i: (0, 0))
    return pl.pallas_call(body, name=name, grid=(1,), in_specs=[full, full], out_specs=full,
                          out_shape=jax.ShapeDtypeStruct(a.shape, a.dtype))(a, b)


def _rs_chip_exchange(name, collective_id, arrs):
    n = len(arrs)

    def body(in_refs, out_refs, send_sems, recv_sems, local_sems):
        x, y, c = _mesh_pos()
        p_me = 2 * x + y
        chips = _other_chips(x, y)

        def part(ref, p):
            return ref.at[p] if len(ref.shape) == 3 else ref

        local = [pltpu.make_async_copy(part(src, p_me), dst.at[p_me], local_sems.at[k])
                 for k, (src, dst) in enumerate(zip(in_refs, out_refs))]
        for cp in local:
            cp.start()
        sends = []
        for j, (px, py) in enumerate(chips):
            for k, (src, dst) in enumerate(zip(in_refs, out_refs)):
                sends.append(pltpu.make_async_remote_copy(
                    src_ref=part(src, 2 * px + py), dst_ref=dst.at[p_me], send_sem=send_sems.at[n * j + k],
                    recv_sem=recv_sems.at[n * j + k], device_id=(px, py, c), device_id_type=MESH))
        for cp in sends:
            cp.start()
        for j, (px, py) in enumerate(chips):
            for k, (src, dst) in enumerate(zip(in_refs, out_refs)):
                pltpu.make_async_remote_copy(
                    src_ref=part(src, p_me), dst_ref=dst.at[2 * px + py], send_sem=send_sems.at[n * j + k],
                    recv_sem=recv_sems.at[n * j + k], device_id=(px, py, c), device_id_type=MESH).wait_recv()
        for cp in sends:
            cp.wait_send()
        for cp in local:
            cp.wait()

    out_shape = [jax.ShapeDtypeStruct((4,) + a.shape[-2:], a.dtype) for a in arrs]
    return _comm_call(name, collective_id, lambda x, y, c: [(*chip, c) for chip in _other_chips(x, y)], body, arrs,
                      out_shape, (3 * n, 3 * n, n))


def _rs_final_sum(name, ids, recv, own, rows, after=None):
    _, R, C = recv.shape
    assert R % rows == 0

    def body(ids_ref, r_ref, own_ref, o_ref):
        acc = None
        for p in range(4):
            term = jnp.where(ids_ref[1] == p, own_ref[...], r_ref[p].astype(F32))
            acc = term if acc is None else acc + term
        o_ref[...] = acc

    in_specs = [pl.BlockSpec((4, rows, C), lambda i, ids: (0, i, 0)), pl.BlockSpec((rows, C), lambda i, ids: (i, 0))]
    args = [ids, recv, own]
    if after is not None:
        body = _drop_ref(body, len(args))
        in_specs.append(ANY)
        args.append(after)
    grid_spec = pltpu.PrefetchScalarGridSpec(
        num_scalar_prefetch=1, grid=(R // rows,), in_specs=in_specs,
        out_specs=pl.BlockSpec((rows, C), lambda i, ids: (i, 0)))
    return pl.pallas_call(
        body, name=name, grid_spec=grid_spec, out_shape=jax.ShapeDtypeStruct((R, C), F32),
        compiler_params=_cparams(("parallel",)),
    )(*args)


class _ReduceScatter:
    def __init__(self, tag, collective_id, big, small=None):
        x, y, c = _mesh_pos()
        self.tag, self.cid, self.small = tag, collective_id, small
        self.ids = jnp.stack([c, 2 * x + y]).astype(jnp.int32)
        self.big = big.reshape((4, 2) + big.shape[1:])
        self.got = _rs_sibling_exchange(tag + "_sibling", collective_id,
                                        [self.big] + ([] if small is None else [small]))

    def chip_sums(self, after=None):
        t16, self.own = _rs_chip_sum(self.tag + "_chip_sum", self.ids, self.big, self.got[0], after)
        arrs = [t16]
        if self.small is not None:
            self.ts = _add2(self.tag + "_chip_sum_small", self.small, self.got[1])
            arrs.append(self.ts)
        self.recv = _rs_chip_exchange(self.tag + "_chips", self.cid + 1, arrs)
        return t16

    def finish(self, after=None):
        out = _rs_final_sum(self.tag + "_final", self.ids, self.recv[0], self.own, _rs_rows(self.own.shape[0]), after)
        out_small = None
        if self.small is not None:
            out_small = _rs_final_sum(self.tag + "_final_small", self.ids, self.recv[1], self.ts, self.small.shape[0])
        return out, out_small


def _all_reduce_small(tag, small):
    x, y, c = _mesh_pos()
    ids = jnp.stack([c, 2 * x + y]).astype(jnp.int32)
    (theirs,) = _rs_sibling_exchange(tag + "_sibling", None, [small])
    ts = _add2(tag + "_chip_sum", small, theirs)
    (recv,) = _rs_chip_exchange(tag + "_chips", None, [ts])
    return _rs_final_sum(tag + "_final", ids, recv, ts, small.shape[0])


def _adamw(name, w, g, m, v):
    R, C = w.shape
    tr = max(t for t in range(8, 513, 8) if R % t == 0)
    c1 = 1.0 - ADAM_B1 ** ADAM_STEP
    c2 = 1.0 - ADAM_B2 ** ADAM_STEP

    def body(w_ref, g_ref, m_ref, v_ref, d_ref, nm_ref, nv_ref):
        gv = g_ref[...]
        nm = ADAM_B1 * m_ref[...] + (1.0 - ADAM_B1) * gv
        nv = ADAM_B2 * v_ref[...] + (1.0 - ADAM_B2) * (gv * gv)
        d_ref[...] = -ADAM_LR * ((nm / c1) / (jnp.sqrt(nv / c2) + ADAM_EPS) + ADAM_WD * w_ref[...])
        nm_ref[...] = nm
        nv_ref[...] = nv

    blk = pl.BlockSpec((tr, C), lambda i: (i, 0))
    shp = jax.ShapeDtypeStruct((R, C), F32)
    return pl.pallas_call(
        body, name=name, grid=(R // tr,), in_specs=[blk] * 4, out_specs=(blk,) * 3, out_shape=(shp,) * 3,
        compiler_params=_cparams(("parallel",)),
    )(w, g, m, v)


def _adamw_t(name, w, g_t, m, v):
    L, R, C = w.shape
    tr = 256
    c1 = 1.0 - ADAM_B1 ** ADAM_STEP
    c2 = 1.0 - ADAM_B2 ** ADAM_STEP

    def body(w_ref, g_ref, m_ref, v_ref, go_ref, d_ref, nm_ref, nv_ref):
        gv = g_ref[...].T
        go_ref[...] = gv
        nm = ADAM_B1 * m_ref[...] + (1.0 - ADAM_B1) * gv
        nv = ADAM_B2 * v_ref[...] + (1.0 - ADAM_B2) * (gv * gv)
        d_ref[...] = -ADAM_LR * ((nm / c1) / (jnp.sqrt(nv / c2) + ADAM_EPS) + ADAM_WD * w_ref[...])
        nm_ref[...] = nm
        nv_ref[...] = nv

    blk = pl.BlockSpec((None, tr, C), lambda l, i: (l, i, 0))
    blk_t = pl.BlockSpec((None, C, tr), lambda l, i: (l, 0, i))
    shp = jax.ShapeDtypeStruct((L, R, C), F32)
    return pl.pallas_call(
        body, name=name, grid=(L, R // tr), in_specs=[blk, blk_t, blk, blk], out_specs=(blk,) * 4,
        out_shape=(shp,) * 4, compiler_params=_cparams(("parallel", "parallel")),
    )(w, g_t, m, v)


REST_FF1 = 0
REST_FF2 = 512
REST_WO = 1024
REST_UP = 1152
REST_ATTN = 1280
REST_ROWS = 1312
IN_ROWS = IN_COLS // N_DEV
CONV_BITS_ROWS = 16
SHARD_ROWS = IN_ROWS + REST_ROWS + CONV_BITS_ROWS
SMALL = (("norm_mix", (DEPTH, D)), ("b_gate", (DEPTH, 3 * D)), ("pool_mix", (DEPTH, 4, POOL_G, POOL_G)),
         ("pool_scale", (DEPTH, POOL_W)), ("conv_w", (DEPTH, 3, CONV_W)), ("q_gain", (DEPTH, HEAD)),
         ("k_gain", (DEPTH, HEAD)), ("norm_mlp", (DEPTH, D)))


def _pack_weight_shards(w, l):
    b = lambda a: a.astype(BF16)
    conv = lax.bitcast_convert_type(w["conv_w"][l], BF16).reshape(3, 128)
    conv = jnp.pad(conv, ((0, CONV_BITS_ROWS - 3), (0, D - 128)))
    rest = jnp.concatenate([
        b(w["w_ff1"][l].T), b(w["w_ff2"][l]), b(w["w_o"][l]),
        jnp.concatenate([b(w["w_pool_up"][l].T), b(w["w_conv_out"][l].T)], axis=1),
        b(w["w_attn_up"][l].T).reshape(REST_ROWS - REST_ATTN, D), conv], axis=0)
    return b(w["w_in"][l].T), rest


def _unpack_gathered(g_in, g_rest, small_w, l):
    rest = g_rest[:, :REST_ROWS]
    take = lambda r0, rows, c0=0, cols=D: rest[:, r0:r0 + rows, c0:c0 + cols].reshape(N_DEV * rows, cols)
    conv = g_rest[:, REST_ROWS:REST_ROWS + 3, :128].reshape(N_DEV, 3, CONV_W // N_DEV, 2)
    conv = lax.bitcast_convert_type(conv, F32)
    p = {
        "w_in_t": g_in.reshape(IN_COLS, D),
        "w_ff1_t": take(REST_FF1, 512), "w_ff2": take(REST_FF2, 512), "w_o": take(REST_WO, 128),
        "w_pool_up_t": take(REST_UP, 128, 0, POOL_W), "w_conv_out_t": take(REST_UP, 128, POOL_W, CONV_W),
        "w_attn_up_t": rest[:, REST_ATTN:].reshape(D, ATTN_OUT),
        "conv_w": jnp.transpose(conv, (1, 0, 2)).reshape(3, CONV_W),
        "pool_mix": small_w["pool_mix"][l],
    }
    for name in ("norm_mix", "b_gate", "pool_scale", "q_gain", "k_gain", "norm_mlp"):
        p[name] = small_w[name][l][None, :]
    return p


def _pack_small_grads(grads):
    flat = jnp.concatenate([jnp.stack([grads[l][name] for l in range(DEPTH)]).reshape(-1) for name, _ in SMALL])
    return jnp.pad(flat, (0, (-flat.shape[0]) % (8 * 128))).reshape(-1, 128)


def _unpack_grads(in_t, rest, small, dev):
    out = {
        "w_in_t": jnp.stack(in_t),
        "w_ff1_t": jnp.stack([a[REST_FF1:REST_FF1 + 512] for a in rest]),
        "w_ff2": jnp.stack([a[REST_FF2:REST_FF2 + 512] for a in rest]),
        "w_o": jnp.stack([a[REST_WO:REST_WO + 128] for a in rest]),
        "w_pool_up": jnp.stack([a[REST_UP:REST_UP + 128, :POOL_W].T for a in rest]),
        "w_conv_out": jnp.stack([a[REST_UP:REST_UP + 128, POOL_W:].T for a in rest]),
        "w_attn_up": jnp.stack([a[REST_ATTN:].reshape(D // N_DEV, ATTN_OUT).T for a in rest]),
    }
    flat = small.reshape(-1)
    off = 0
    for name, shp in SMALL:
        n = 1
        for s in shp:
            n *= s
        out[name] = flat[off:off + n].reshape(shp)
        off += n
    width = CONV_W // N_DEV
    out["conv_w"] = lax.dynamic_slice_in_dim(out["conv_w"], dev * width, width, axis=2)
    return out


def _pack_small(arrs):
    flat = jnp.concatenate([a.reshape(-1) for a in arrs])
    pad = (-flat.shape[0]) % (8 * 128)
    return jnp.pad(flat, (0, pad)).reshape(-1, 128)


def _unpack_small(packed, like):
    flat = packed.reshape(-1)
    out, off = [], 0
    for a in like:
        out.append(flat[off:off + a.size].reshape(a.shape))
        off += a.size
    return out


WEIGHTS = ("norm_mix", "w_in", "b_gate", "pool_mix", "pool_scale", "conv_w", "q_gain", "k_gain", "w_pool_up",
           "w_conv_out", "w_attn_up", "w_o", "norm_mlp", "w_ff1", "w_ff2")


def kernel(x, norm_mix, w_in, b_gate, pool_mix, pool_scale, conv_w, q_gain, k_gain, w_pool_up, w_conv_out, w_attn_up, w_o, norm_mlp, w_ff1, w_ff2, loss_target, m_norm_mix, m_w_in, m_b_gate, m_pool_mix, m_pool_scale, m_conv_w, m_q_gain, m_k_gain, m_w_pool_up, m_w_conv_out, m_w_attn_up, m_w_o, m_norm_mlp, m_w_ff1, m_w_ff2, v_norm_mix, v_w_in, v_b_gate, v_pool_mix, v_pool_scale, v_conv_w, v_q_gain, v_k_gain, v_w_pool_up, v_w_conv_out, v_w_attn_up, v_w_o, v_norm_mlp, v_w_ff1, v_w_ff2):
    w = dict(zip(WEIGHTS, (norm_mix, w_in, b_gate, pool_mix, pool_scale, conv_w, q_gain, k_gain, w_pool_up,
                           w_conv_out, w_attn_up, w_o, norm_mlp, w_ff1, w_ff2)))
    m = dict(zip(WEIGHTS, (m_norm_mix, m_w_in, m_b_gate, m_pool_mix, m_pool_scale, m_conv_w, m_q_gain, m_k_gain,
                           m_w_pool_up, m_w_conv_out, m_w_attn_up, m_w_o, m_norm_mlp, m_w_ff1, m_w_ff2)))
    v = dict(zip(WEIGHTS, (v_norm_mix, v_w_in, v_b_gate, v_pool_mix, v_pool_scale, v_conv_w, v_q_gain, v_k_gain,
                           v_w_pool_up, v_w_conv_out, v_w_attn_up, v_w_o, v_norm_mlp, v_w_ff1, v_w_ff2)))
    xi, yi, ci = _mesh_pos()
    dev = 4 * xi + 2 * yi + ci

    params = []
    for l in range(DEPTH):
        s_in, s_rest = _pack_weight_shards(w, l)
        g_in = _all_gather(f"gather_in{l}", 1 + 2 * l, s_in)
        g_rest = _all_gather(f"gather_rest{l}", 2 + 2 * l, s_rest)
        params.append(_unpack_gathered(g_in, g_rest, w, l))
    saved = []
    act = x[0]
    for l in range(DEPTH):
        act, s = _layer_fwd(l, act, params[l])
        saved.append(s)
    dy, loss = _loss_head("loss_head", act, loss_target[0])
    total = lax.psum(loss[0, 0], ("x", "y", "c"))
    in_t, rest, grads, pending = [None] * DEPTH, [None] * DEPTH, [None] * DEPTH, None
    start = total.reshape(1, 1)
    for l in reversed(range(DEPTH)):
        dy, grads[l], rest[l], done, pending = _layer_bwd(l, dy, params[l], saved[l], pending, 5 + 4 * l, l == 0,
                                                          start)
        start = None
        if done is not None:
            in_t[l + 1] = done[0]
    dx = dy
    g_small = _all_reduce_small("ar_small", _pack_small_grads(grads))
    in_t[0], _ = pending.finish(after=g_small)
    g = _unpack_grads(in_t, rest, g_small, dev)

    delta, new_m, new_v = {}, {}, {}
    for name in ("w_in", "w_ff1"):
        g[name], delta[name], new_m[name], new_v[name] = _adamw_t("adamw_" + name, w[name], g[name + "_t"],
                                                                  m[name], v[name])
    for name in ("w_pool_up", "w_conv_out", "w_attn_up", "w_o", "w_ff2"):
        shp = w[name].shape
        two_d = (shp[0] * shp[1], shp[2])
        d_, m_, v_ = _adamw("adamw_" + name, w[name].reshape(two_d), g[name].reshape(two_d),
                            m[name].reshape(two_d), v[name].reshape(two_d))
        delta[name], new_m[name], new_v[name] = d_.reshape(shp), m_.reshape(shp), v_.reshape(shp)
    small_names = [name for name, _ in SMALL]
    packed = [_pack_small([src[name] for name in small_names]) for src in (w, g, m, v)]
    outs = _adamw("adamw_small", *packed)
    for dst, arr in zip((delta, new_m, new_v), outs):
        for name, val in zip(small_names, _unpack_small(arr, [w[name] for name in small_names])):
            dst[name] = val

    return (total, dx[None], *[g[n] for n in WEIGHTS], *[delta[n] for n in WEIGHTS],
            *[new_m[n] for n in WEIGHTS], *[new_v[n] for n in WEIGHTS])
```

```python
import functools

import jax
import jax.numpy as jnp
from jax import lax
from jax.experimental import pallas as pl
from jax.experimental.pallas import tpu as pltpu
from jax.experimental.pallas import tpu_sc as plsc

F32 = jnp.float32
BF16 = jnp.bfloat16
MESH = pl.DeviceIdType.MESH

D = 1024
SEQ = 4096
DEPTH = 2
N_DEV = 8
POOL_WINDOWS = (2, 4, 8, 16)
POOL_W = 512
POOL_G = 128
CONV_W = 512
HEAD = 64
ATTN_GROUPS = ((128, 1), (512, 4), (2048, 16))
HPG = 4
ATTN_W = 768
ATTN_OUT = 256
ATTN_BLK = 128
D_FF = 4096
EPS = 1e-6
MASK = -1e30
OFF_POOL = 0
OFF_CB = 512
OFF_CC = 1024
OFF_CX = 1536
OFF_Q = 2048
OFF_K = 2816
OFF_V = 3584
OFF_GATE = 4352
IN_COLS = 7424
ADAM_LR = 0.001
ADAM_B1 = 0.9
ADAM_B2 = 0.999
ADAM_EPS = 1e-08
ADAM_WD = 0.01
ADAM_STEP = 10

ROW_TILE = 512
SEQ_CHUNK = 256
HALO = 16
VMEM_LIMIT = 56 * 1024 * 1024


def _cparams(sem=None):
    return pltpu.CompilerParams(dimension_semantics=sem, vmem_limit_bytes=VMEM_LIMIT)


def _call(body, args, in_specs, after=None, **kw):
    if after is not None:
        n, inner = len(args), body

        def body(*refs):
            return inner(*refs[:n], *refs[n + 1:])

        args = list(args) + [after]
        in_specs = list(in_specs) + [pl.BlockSpec(memory_space=pl.ANY)]
    return pl.pallas_call(body, in_specs=in_specs, **kw)(*args)


_DIMS = {"nn": (((1,), (0,)), ((), ())), "nt": (((1,), (1,)), ((), ())), "tn": (((0,), (0,)), ((), ()))}


def _matmul(name, a, b, mode, tm, tn, tk, out_dtypes=(F32,), extras=(), epilogue=None, into=None, after=None):
    if mode == "tn":
        K, M = a.shape
    else:
        M, K = a.shape
    N = b.shape[0] if mode == "nt" else b.shape[1]
    assert M % tm == 0 and N % tn == 0 and K % tk == 0, (name, M, N, K, tm, tn, tk)
    nk = K // tk
    n_extra = len(extras)
    n_out = len(out_dtypes)
    dims = _DIMS[mode]
    n_alias = 0 if into is None or isinstance(into[0], jax.ShapeDtypeStruct) else 1

    def body(a_ref, b_ref, *rest):
        extra_refs = rest[:n_extra]
        out_refs = rest[n_extra + n_alias:n_extra + n_alias + n_out]

        def finish(acc):
            if epilogue is None:
                res = (acc,)
            else:
                res = epilogue(acc, *[r[...] for r in extra_refs])
            for o_ref, v in zip(out_refs, res):
                o_ref[...] = v.astype(o_ref.dtype)

        part = lax.dot_general(a_ref[...].astype(BF16), b_ref[...].astype(BF16), dims,
                               preferred_element_type=F32)
        if nk == 1:
            finish(part)
        else:
            acc_ref = rest[-1]
            k = pl.program_id(2)

            @pl.when(k == 0)
            def _():
                acc_ref[...] = part

            @pl.when(k > 0)
            def _():
                acc_ref[...] += part

            @pl.when(k == nk - 1)
            def _():
                finish(acc_ref[...])

    if mode == "tn":
        a_spec = pl.BlockSpec((tk, tm), lambda i, j, k: (k, i))
    else:
        a_spec = pl.BlockSpec((tm, tk), lambda i, j, k: (i, k))
    if mode == "nt":
        b_spec = pl.BlockSpec((tn, tk), lambda i, j, k: (j, k))
    else:
        b_spec = pl.BlockSpec((tk, tn), lambda i, j, k: (k, j))
    in_specs = [a_spec, b_spec]
    args = [a, b]
    for arr, kind in extras:
        if kind == "mn":
            in_specs.append(pl.BlockSpec((tm, tn), lambda i, j, k: (i, j)))
        else:
            in_specs.append(pl.BlockSpec((1, tn), lambda i, j, k: (0, j)))
        args.append(arr)
    out_shape = tuple(jax.ShapeDtypeStruct((M, N), dt) for dt in out_dtypes)
    out_specs = tuple(pl.BlockSpec((tm, tn), lambda i, j, k: (i, j)) for _ in out_dtypes)
    aliases = {}
    if into is not None:
        buf, row0, col0 = into
        assert n_out == 1 and (M // N_DEV) % tm == 0 and row0 % tm == 0 and col0 % tn == 0
        per_dev = M // N_DEV // tm
        out_shape = (jax.ShapeDtypeStruct(buf.shape, buf.dtype),)
        out_specs = (pl.BlockSpec((None, tm, tn), lambda i, j, k: (i // per_dev, row0 // tm + i % per_dev,
                                                                   col0 // tn + j)),)
        if not isinstance(buf, jax.ShapeDtypeStruct):
            aliases = {len(args): 0}
            in_specs.append(pl.BlockSpec(memory_space=pl.ANY))
            args.append(buf)
    scratch = [] if nk == 1 else [pltpu.VMEM((tm, tn), F32)]
    res = _call(
        body, args, in_specs, after, name=name, grid=(M // tm, N // tn, nk), out_specs=out_specs,
        out_shape=out_shape, scratch_shapes=scratch, input_output_aliases=aliases,
        compiler_params=_cparams(("parallel", "parallel", "arbitrary")))
    return res if n_out > 1 else res[0]


def _rmsnorm_fwd(name, x, gain):
    S_, D_ = x.shape

    def body(x_ref, g_ref, h_ref):
        xf = x_ref[...]
        ms = jnp.mean(xf * xf, axis=-1, keepdims=True)
        h_ref[...] = (xf * lax.rsqrt(ms + EPS) * g_ref[...]).astype(BF16)

    return pl.pallas_call(
        body, name=name, grid=(S_ // ROW_TILE,),
        in_specs=[pl.BlockSpec((ROW_TILE, D_), lambda i: (i, 0)), pl.BlockSpec((1, D_), lambda i: (0, 0))],
        out_specs=pl.BlockSpec((ROW_TILE, D_), lambda i: (i, 0)),
        out_shape=jax.ShapeDtypeStruct((S_, D_), BF16),
        compiler_params=_cparams(("parallel",)),
    )(x, gain)


def _rmsnorm_bwd(name, x, gain, dh, dres, after=None):
    S_, D_ = x.shape
    n = S_ // ROW_TILE

    def body(x_ref, g_ref, dh_ref, dres_ref, dx_ref, dg_ref, acc_ref):
        i = pl.program_id(0)
        xf = x_ref[...]
        rstd = lax.rsqrt(jnp.mean(xf * xf, axis=-1, keepdims=True) + EPS)
        xhat = xf * rstd
        dhv = dh_ref[...]
        dxhat = dhv * g_ref[...]
        c = jnp.mean(dxhat * xhat, axis=-1, keepdims=True)
        dx_ref[...] = dres_ref[...] + rstd * (dxhat - xhat * c)
        part = jnp.sum((dhv * xhat).reshape(ROW_TILE // 8, 8, D_), axis=0)

        @pl.when(i == 0)
        def _():
            acc_ref[...] = part

        @pl.when(i > 0)
        def _():
            acc_ref[...] += part

        @pl.when(i == n - 1)
        def _():
            dg_ref[...] = jnp.sum(acc_ref[...], axis=0, keepdims=True)

    row = pl.BlockSpec((ROW_TILE, D_), lambda i: (i, 0))
    vec = pl.BlockSpec((1, D_), lambda i: (0, 0))
    return _call(
        body, [x, gain, dh, dres], [row, vec, row, row], after, name=name, grid=(n,), out_specs=(row, vec),
        out_shape=(jax.ShapeDtypeStruct((S_, D_), F32), jax.ShapeDtypeStruct((1, D_), F32)),
        scratch_shapes=[pltpu.VMEM((8, D_), F32)],
        compiler_params=_cparams(("arbitrary",)))


def _loss_head(name, y, target):
    S_, D_ = y.shape
    n = S_ // ROW_TILE

    def body(y_ref, t_ref, dy_ref, l_ref, acc_ref):
        i = pl.program_id(0)
        e = y_ref[...] - t_ref[...]
        dy_ref[...] = e * (1.0 / D_)
        part = jnp.sum((e * e).reshape(ROW_TILE // 8, 8, D_), axis=0)

        @pl.when(i == 0)
        def _():
            acc_ref[...] = part

        @pl.when(i > 0)
        def _():
            acc_ref[...] += part

        @pl.when(i == n - 1)
        def _():
            s = jnp.sum(acc_ref[...], axis=1, keepdims=True)
            l_ref[...] = jnp.sum(s, axis=0, keepdims=True) * (0.5 / D_)

    row = pl.BlockSpec((ROW_TILE, D_), lambda i: (i, 0))
    return pl.pallas_call(
        body, name=name, grid=(n,), in_specs=[row, row],
        out_specs=(row, pl.BlockSpec((1, 1), lambda i: (0, 0))),
        out_shape=(jax.ShapeDtypeStruct((S_, D_), F32), jax.ShapeDtypeStruct((1, 1), F32)),
        scratch_shapes=[pltpu.VMEM((8, D_), F32)],
        compiler_params=_cparams(("arbitrary",)),
    )(y, target)


def _rows_with_halo(ref, cols, i, n_chunks, before, after):
    r0 = pl.multiple_of(i * SEQ_CHUNK, SEQ_CHUNK)
    parts = []
    if before:
        h0 = pl.multiple_of(jnp.maximum(r0 - HALO, 0), 8)
        halo = ref[pl.ds(h0, HALO), cols]
        parts.append(jnp.where(i > 0, halo, jnp.zeros_like(halo)))
    parts.append(ref[pl.ds(r0, SEQ_CHUNK), cols])
    if after:
        a0 = pl.multiple_of(jnp.minimum(r0 + SEQ_CHUNK, (n_chunks - 1) * SEQ_CHUNK + SEQ_CHUNK - HALO), 8)
        halo = ref[pl.ds(a0, HALO), cols]
        parts.append(jnp.where(i < n_chunks - 1, halo, jnp.zeros_like(halo)))
    return parts[0] if len(parts) == 1 else jnp.concatenate(parts, axis=0)


def _shift_down(v, k):
    return pltpu.roll(v, k, 0)


def _shift_up(v, k):
    return pltpu.roll(v, v.shape[0] - k, 0)


def _pool_diff(xx, w, t_main):
    s = xx
    k = 1
    while k < w:
        s = s + _shift_down(s, k)
        k *= 2
    cnt = jnp.minimum(t_main + 1, w).astype(F32)
    return s[HALO:] / cnt - xx[HALO:]


def _pool_fwd(name, z, pool_mix, pool_scale):
    S_ = z.shape[0]
    n_chunks = S_ // SEQ_CHUNK

    cols = slice(0, POOL_G)

    def body(u_ref, mix_ref, sc_ref, y_ref):
        mixg = mix_ref[...].astype(BF16)
        scg = sc_ref[...]
        for g, w in enumerate(POOL_WINDOWS):
            @pl.when(pl.program_id(0) == g)
            def _(w=w):
                def chunk(i, carry):
                    r0 = pl.multiple_of(i * SEQ_CHUNK, SEQ_CHUNK)
                    xx = _rows_with_halo(u_ref, cols, i, n_chunks, True, False)
                    t = r0 + lax.broadcasted_iota(jnp.int32, (SEQ_CHUNK, POOL_G), 0)
                    d = _pool_diff(xx, w, t)
                    y = jnp.dot(d.astype(BF16), mixg, preferred_element_type=F32) * scg
                    y_ref[pl.ds(r0, SEQ_CHUNK), :] = y.astype(BF16)
                    return carry

                lax.fori_loop(0, n_chunks, chunk, 0)

    slab = pl.BlockSpec((S_, POOL_G), lambda g: (0, g))
    return pl.pallas_call(
        body, name=name, grid=(len(POOL_WINDOWS),),
        in_specs=[slab, pl.BlockSpec((None, POOL_G, POOL_G), lambda g: (g, 0, 0)),
                  pl.BlockSpec((1, POOL_G), lambda g: (0, g))],
        out_specs=slab, out_shape=jax.ShapeDtypeStruct((S_, POOL_W), BF16),
        compiler_params=_cparams(("parallel",)),
    )(z, pool_mix, pool_scale)


def _pool_bwd(name, z, dy, pool_mix, pool_scale, after=None):
    S_ = z.shape[0]
    n_chunks = S_ // SEQ_CHUNK
    rows_a = SEQ_CHUNK + HALO

    cols = slice(0, POOL_G)

    def body(u_ref, dy_ref, mix_ref, sc_ref, du_ref, dmix_ref, dsc_ref):
        mixg = mix_ref[...].astype(BF16)
        scg = sc_ref[...]
        for g, w in enumerate(POOL_WINDOWS):
            @pl.when(pl.program_id(0) == g)
            def _(w=w):
                def chunk(i, carry):
                    dmix_acc, dsc_acc = carry
                    r0 = pl.multiple_of(i * SEQ_CHUNK, SEQ_CHUNK)
                    xx = _rows_with_halo(u_ref, cols, i, n_chunks, True, False)
                    t = r0 + lax.broadcasted_iota(jnp.int32, (SEQ_CHUNK, POOL_G), 0)
                    d = _pool_diff(xx, w, t).astype(BF16)
                    ypre = jnp.dot(d, mixg, preferred_element_type=F32)
                    dyy = _rows_with_halo(dy_ref, cols, i, n_chunks, False, True)
                    dys = (dyy * scg).astype(BF16)
                    dsc_acc = dsc_acc + jnp.sum((dyy[:SEQ_CHUNK] * ypre).reshape(SEQ_CHUNK // 8, 8, POOL_G), axis=0)
                    dmix_acc = dmix_acc + lax.dot_general(d, dys[:SEQ_CHUNK], _DIMS["tn"],
                                                          preferred_element_type=F32)
                    dd = lax.dot_general(dys, mixg, _DIMS["nt"], preferred_element_type=F32)
                    ta = r0 + lax.broadcasted_iota(jnp.int32, (rows_a, POOL_G), 0)
                    f = dd / jnp.minimum(ta + 1, w).astype(F32)
                    k = 1
                    while k < w:
                        f = f + _shift_up(f, k)
                        k *= 2
                    du_ref[pl.ds(r0, SEQ_CHUNK), :] = (f[:SEQ_CHUNK] - dd[:SEQ_CHUNK]).astype(BF16)
                    return dmix_acc, dsc_acc

                dmix_acc, dsc_acc = lax.fori_loop(
                    0, n_chunks, chunk, (jnp.zeros((POOL_G, POOL_G), F32), jnp.zeros((8, POOL_G), F32)))
                dmix_ref[...] = dmix_acc
                dsc_ref[...] = jnp.sum(dsc_acc, axis=0, keepdims=True)

    slab = pl.BlockSpec((S_, POOL_G), lambda g: (0, g))
    mix_spec = pl.BlockSpec((None, POOL_G, POOL_G), lambda g: (g, 0, 0))
    vec = pl.BlockSpec((1, POOL_G), lambda g: (0, g))
    return _call(
        body, [z, dy, pool_mix, pool_scale], [slab, slab, mix_spec, vec], after, name=name,
        grid=(len(POOL_WINDOWS),), out_specs=(slab, mix_spec, vec),
        out_shape=(jax.ShapeDtypeStruct((S_, POOL_W), BF16), jax.ShapeDtypeStruct((4, POOL_G, POOL_G), F32),
                   jax.ShapeDtypeStruct((1, POOL_W), F32)),
        compiler_params=_cparams(("parallel",)))


def _conv_specs(S_):
    slab = lambda off: pl.BlockSpec((S_, 128), lambda c, off=off: (0, off // 128 + c))
    return slab(OFF_CB), slab(OFF_CC), slab(OFF_CX)


def _conv_fwd(name, z, conv_w):
    S_ = z.shape[0]
    n_chunks = S_ // SEQ_CHUNK
    col = slice(0, 128)

    def body(b_ref, c_ref, x_ref, w_ref, y_ref):
        w0, w1, w2 = w_ref[0:1, :], w_ref[1:2, :], w_ref[2:3, :]

        def chunk(i, carry):
            r0 = pl.multiple_of(i * SEQ_CHUNK, SEQ_CHUNK)
            u = _rows_with_halo(c_ref, col, i, n_chunks, True, False) * _rows_with_halo(x_ref, col, i, n_chunks, True, False)
            y = w2 * u + w1 * _shift_down(u, 1) + w0 * _shift_down(u, 2)
            y_ref[pl.ds(r0, SEQ_CHUNK), :] = (b_ref[pl.ds(r0, SEQ_CHUNK), :] * y[HALO:]).astype(BF16)
            return carry

        lax.fori_loop(0, n_chunks, chunk, 0)

    sb, sc, sx = _conv_specs(S_)
    return pl.pallas_call(
        body, name=name, grid=(CONV_W // 128,),
        in_specs=[sb, sc, sx, pl.BlockSpec((3, 128), lambda c: (0, c))],
        out_specs=pl.BlockSpec((S_, 128), lambda c: (0, c)),
        out_shape=jax.ShapeDtypeStruct((S_, CONV_W), BF16),
        compiler_params=_cparams(("parallel",)),
    )(z, z, z, conv_w)


def _conv_bwd(name, z, dy, conv_w, after=None):
    S_ = z.shape[0]
    n_chunks = S_ // SEQ_CHUNK
    col = slice(0, 128)
    lo, hi = HALO, HALO + SEQ_CHUNK

    def body(b_ref, c_ref, x_ref, dy_ref, w_ref, db_ref, dc_ref, dx_ref, dw_ref):
        w0, w1, w2 = w_ref[0:1, :], w_ref[1:2, :], w_ref[2:3, :]

        def chunk(i, carry):
            a0, a1, a2 = carry
            r0 = pl.multiple_of(i * SEQ_CHUNK, SEQ_CHUNK)
            cc = _rows_with_halo(c_ref, col, i, n_chunks, True, True)
            xx = _rows_with_halo(x_ref, col, i, n_chunks, True, True)
            bb = _rows_with_halo(b_ref, col, i, n_chunks, True, True)
            dyy = _rows_with_halo(dy_ref, col, i, n_chunks, True, True)
            u = cc * xx
            u1 = _shift_down(u, 1)
            u2 = _shift_down(u, 2)
            y = w2 * u + w1 * u1 + w0 * u2
            dyv = dyy * bb
            du = w2 * dyv + w1 * _shift_up(dyv, 1) + w0 * _shift_up(dyv, 2)
            db_ref[pl.ds(r0, SEQ_CHUNK), :] = (dyy[lo:hi] * y[lo:hi]).astype(BF16)
            dc_ref[pl.ds(r0, SEQ_CHUNK), :] = (du[lo:hi] * xx[lo:hi]).astype(BF16)
            dx_ref[pl.ds(r0, SEQ_CHUNK), :] = (du[lo:hi] * cc[lo:hi]).astype(BF16)
            red = lambda v: jnp.sum(v.reshape(SEQ_CHUNK // 8, 8, 128), axis=0)
            dm = dyv[lo:hi]
            return a0 + red(dm * u2[lo:hi]), a1 + red(dm * u1[lo:hi]), a2 + red(dm * u[lo:hi])

        zero = jnp.zeros((8, 128), F32)
        a0, a1, a2 = lax.fori_loop(0, n_chunks, chunk, (zero, zero, zero))
        dw_ref[0:1, :] = jnp.sum(a0, axis=0, keepdims=True)
        dw_ref[1:2, :] = jnp.sum(a1, axis=0, keepdims=True)
        dw_ref[2:3, :] = jnp.sum(a2, axis=0, keepdims=True)

    sb, sc, sx = _conv_specs(S_)
    slab = pl.BlockSpec((S_, 128), lambda c: (0, c))
    wspec = pl.BlockSpec((3, 128), lambda c: (0, c))
    act = jax.ShapeDtypeStruct((S_, CONV_W), BF16)
    return _call(
        body, [z, z, z, dy, conv_w], [sb, sc, sx, slab, wspec], after, name=name, grid=(CONV_W // 128,),
        out_specs=(slab, slab, slab, wspec),
        out_shape=(act, act, act, jax.ShapeDtypeStruct((3, CONV_W), F32)),
        compiler_params=_cparams(("parallel",)))


def _head_ones(pw):
    a = lax.broadcasted_iota(jnp.int32, (pw, pw), 0) // HEAD
    b = lax.broadcasted_iota(jnp.int32, (pw, pw), 1) // HEAD
    return (a == b).astype(BF16)


def _head_sum(v, ones):
    hi = v.astype(BF16)
    lo = (v - hi.astype(F32)).astype(BF16)
    return jnp.dot(hi, ones, preferred_element_type=F32) + jnp.dot(lo, ones, preferred_element_type=F32)


def _head_norm(v, gain, ones):
    rstd = lax.rsqrt(_head_sum(v * v, ones) * (1.0 / HEAD) + EPS)
    xhat = v * rstd
    return xhat * gain, xhat, rstd


def _head_norm_bwd(dy, xhat, rstd, gain, ones):
    dxhat = dy * gain
    c = _head_sum(dxhat * xhat, ones) * (1.0 / HEAD)
    dv = rstd * (dxhat - xhat * c)
    dg = jnp.sum((dy * xhat).reshape(dy.shape[0] // 8, 8, dy.shape[1]), axis=0)
    return dv, dg


def _head_masks(pw):
    lane_head = lax.broadcasted_iota(jnp.int32, (1, pw), 1) // HEAD
    return [lane_head == h for h in range(pw // HEAD)]


def _only(mask, v):
    return jnp.where(mask, v, jnp.zeros_like(v))


def _attn_specs(S_, g, dil):
    rows = ATTN_BLK * dil
    nb = S_ // rows
    pw = 128 if dil > 1 else ATTN_OUT
    cq, ck, cv = ((OFF_Q + g * ATTN_OUT) // pw, (OFF_K + g * ATTN_OUT) // pw, (OFF_V + g * ATTN_OUT) // pw)
    return rows, nb, pw, pw // HEAD, ATTN_OUT // pw, cq, ck, cv


ATTN_BATCH = 4


def _attn_group(dil):
    return 4 if dil == 1 else 1


def _attn_block_specs(rows, grp, pw, last=None):
    step = (lambda n: n) if last is None else (lambda n: jnp.minimum(n, last))
    cur = lambda c: pl.BlockSpec((rows * grp, pw), lambda hp, n, c=c: (step(n), c + hp))
    prev = lambda c: pl.BlockSpec((rows, pw), lambda hp, n, c=c: (jnp.maximum(step(n) * grp - 1, 0), c + hp))
    return cur, prev


def _band_mask(has_prev):
    qi = lax.broadcasted_iota(jnp.int32, (ATTN_BLK, 2 * ATTN_BLK), 0)
    ki = lax.broadcasted_iota(jnp.int32, (ATTN_BLK, 2 * ATTN_BLK), 1)
    in_prev = jnp.logical_and(ki < ATTN_BLK, ki >= qi)
    if has_prev is not True:
        in_prev = jnp.logical_and(in_prev, has_prev)
    return jnp.logical_or(in_prev, jnp.logical_and(ki >= ATTN_BLK, ki - ATTN_BLK <= qi))


def _rows_of(ref, r, dil):
    if dil == 1:
        return ref[r * ATTN_BLK:(r + 1) * ATTN_BLK, :]
    return ref[pl.ds(r, ATTN_BLK, stride=dil), :]


def _put_rows(ref, r, dil, val):
    if dil == 1:
        ref[r * ATTN_BLK:(r + 1) * ATTN_BLK, :] = val.astype(ref.dtype)
    else:
        ref[pl.ds(r, ATTN_BLK, stride=dil), :] = val.astype(ref.dtype)


def _attn_fwd(name, z, q_gain, k_gain, g, dil):
    S_ = z.shape[0]
    rows, nb, pw, heads, npairs, cq, ck, cv = _attn_specs(S_, g, dil)
    scale = HEAD ** -0.5

    grp = _attn_group(dil)
    nsteps = nb // grp

    def body(q_ref, k_ref, kp_ref, v_ref, vp_ref, gq_ref, gk_ref, o_ref, l_ref):
        n = pl.program_id(1)
        ones, hmask = _head_ones(pw), _head_masks(pw)
        gq, gk = jnp.tile(gq_ref[...], (1, heads)), jnp.tile(gk_ref[...], (1, heads))
        mask_first, mask_rest = _band_mask(n > 0), _band_mask(True)
        for r0 in range(0, dil * grp, ATTN_BATCH):
            rs = range(r0, min(r0 + ATTN_BATCH, dil * grp))
            qn, kn, vv, s, p = {}, {}, {}, {}, {}
            kcn = {}
            for r in rs:
                q, kc, vc = _rows_of(q_ref, r, dil), _rows_of(k_ref, r, dil), _rows_of(v_ref, r, dil)
                kcn[r] = _head_norm(kc, gk, ones)[0]
                if dil == 1 and r > 0:
                    kpn = kcn[r - 1] if r - 1 in kcn else _head_norm(_rows_of(k_ref, r - 1, dil), gk, ones)[0]
                    vp = _rows_of(v_ref, r - 1, dil)
                else:
                    kpn, vp = _head_norm(_rows_of(kp_ref, r, dil), gk, ones)[0], _rows_of(vp_ref, r, dil)
                qn[r] = _head_norm(q, gq, ones)[0].astype(BF16)
                kn[r] = jnp.concatenate([kpn, kcn[r]], axis=0).astype(BF16)
                vv[r] = jnp.concatenate([vp, vc], axis=0).astype(BF16)
            keys = [(r, h) for r in rs for h in range(heads)]
            for r, h in keys:
                s[r, h] = lax.dot_general(_only(hmask[h], qn[r]), kn[r], _DIMS["nt"],
                                          preferred_element_type=F32) * scale
            lse, den = {}, {}
            for key in keys:
                mask = mask_rest if (dil == 1 and key[0] > 0) else mask_first
                sm = jnp.where(mask, s[key], MASK)
                m = jnp.max(sm, axis=-1, keepdims=True)
                e = jnp.exp(sm - m)
                den[key] = jnp.sum(e, axis=-1, keepdims=True)
                p[key] = e.astype(BF16)
                lse[key] = m + jnp.log(den[key])
            for r in rs:
                out = jnp.zeros((ATTN_BLK, pw), F32)
                lse_all = jnp.zeros((ATTN_BLK, pw), F32)
                for h in range(heads):
                    out = jnp.where(hmask[h], jnp.dot(p[r, h], vv[r], preferred_element_type=F32) / den[r, h], out)
                    lse_all = jnp.where(hmask[h], lse[r, h], lse_all)
                _put_rows(o_ref, r, dil, out)
                _put_rows(l_ref, r, dil, lse_all)

    cur, prev = _attn_block_specs(rows, grp, pw)
    gspec = pl.BlockSpec((1, HEAD), lambda hp, n: (0, 0))
    shp = jax.ShapeDtypeStruct((S_, ATTN_OUT), F32)
    return pl.pallas_call(
        body, name=name, grid=(npairs, nsteps),
        in_specs=[cur(cq), cur(ck), prev(ck), cur(cv), prev(cv), gspec, gspec],
        out_specs=(cur(0), cur(0)), out_shape=(shp, shp),
        compiler_params=_cparams(("parallel", "parallel")),
    )(z, z, z, z, z, q_gain, k_gain)


def _attn_combine(name, os_, ls_):
    S_ = os_[0].shape[0]

    def body(o0, o1, o2, l0, l1, l2, o_ref, l_ref):
        a, b, c = l0[...], l1[...], l2[...]
        m = jnp.maximum(jnp.maximum(a, b), c)
        ea, eb, ec = jnp.exp(a - m), jnp.exp(b - m), jnp.exp(c - m)
        zsum = ea + eb + ec
        o_ref[...] = (ea * o0[...] + eb * o1[...] + ec * o2[...]) / zsum
        l_ref[...] = m + jnp.log(zsum)

    row = pl.BlockSpec((ROW_TILE, ATTN_OUT), lambda i: (i, 0))
    shp = jax.ShapeDtypeStruct((S_, ATTN_OUT), F32)
    return pl.pallas_call(
        body, name=name, grid=(S_ // ROW_TILE,), in_specs=[row] * 6, out_specs=(row, row), out_shape=(shp, shp),
        compiler_params=_cparams(("parallel",)),
    )(*os_, *ls_)


def _attn_bwd(name, z, q_gain, k_gain, do, o, lse, g, dil, after=None):
    S_ = z.shape[0]
    rows, nb, pw, heads, npairs, cq, ck, cv = _attn_specs(S_, g, dil)
    scale = HEAD ** -0.5

    def body(q_ref, kc_ref, kp_ref, vc_ref, vp_ref, gq_ref, gk_ref, do_ref, o_ref, l_ref,
             dq_ref, dk_ref, dv_ref, dgq_ref, dgk_ref, ck_ref, cvv_ref, gq_acc, gk_acc):
        hp = pl.program_id(0)
        n = pl.program_id(1)
        live = n < nb
        mask = jnp.logical_and(_band_mask(n > 0), live)
        ones, hmask = _head_ones(pw), _head_masks(pw)
        gq, gk = jnp.tile(gq_ref[...], (1, heads)), jnp.tile(gk_ref[...], (1, heads))

        @pl.when(n == 0)
        def _():
            ck_ref[...] = jnp.zeros_like(ck_ref)
            cvv_ref[...] = jnp.zeros_like(cvv_ref)

        @pl.when(jnp.logical_and(n == 0, hp == 0))
        def _():
            gq_acc[...] = jnp.zeros_like(gq_acc)
            gk_acc[...] = jnp.zeros_like(gk_acc)

        dgq = jnp.zeros((8, pw), F32)
        dgk = jnp.zeros((8, pw), F32)
        for r0 in range(0, dil, ATTN_BATCH):
            rs = range(r0, min(r0 + ATTN_BATCH, dil))
            keys = [(r, h) for r in rs for h in range(heads)]
            qn, qhat, qrstd, kn, kphat, kprstd, vv, dob, delta, lse = ({} for _ in range(10))
            for r in rs:
                q, kc, kp = _rows_of(q_ref, r, dil), _rows_of(kc_ref, r, dil), _rows_of(kp_ref, r, dil)
                dov = _rows_of(do_ref, r, dil)
                qn_f, qhat[r], qrstd[r] = _head_norm(q, gq, ones)
                kpn, kphat[r], kprstd[r] = _head_norm(kp, gk, ones)
                qn[r] = qn_f.astype(BF16)
                kn[r] = jnp.concatenate([kpn, _head_norm(kc, gk, ones)[0]], axis=0).astype(BF16)
                vv[r] = jnp.concatenate([_rows_of(vp_ref, r, dil), _rows_of(vc_ref, r, dil)], axis=0).astype(BF16)
                dob[r] = dov.astype(BF16)
                delta[r] = _head_sum(dov * _rows_of(o_ref, r, dil), ones)
                lse[r] = _rows_of(l_ref, r, dil)
            s, dp = {}, {}
            for r, h in keys:
                s[r, h] = lax.dot_general(_only(hmask[h], qn[r]), kn[r], _DIMS["nt"],
                                          preferred_element_type=F32) * scale
                dp[r, h] = lax.dot_general(_only(hmask[h], dob[r]), vv[r], _DIMS["nt"], preferred_element_type=F32)
            p, ds = {}, {}
            for r, h in keys:
                col = slice(h * HEAD, h * HEAD + 1)
                pk = jnp.where(mask, jnp.exp(jnp.where(mask, s[r, h], MASK) - lse[r][:, col]), 0.0)
                ds[r, h] = (pk * (dp[r, h] - delta[r][:, col]) * scale).astype(BF16)
                p[r, h] = pk.astype(BF16)
            dqn, dkn, dvv = {}, {}, {}
            for r in rs:
                dqn[r] = jnp.zeros((ATTN_BLK, pw), F32)
                dkn[r] = jnp.zeros((2 * ATTN_BLK, pw), F32)
                dvv[r] = jnp.zeros((2 * ATTN_BLK, pw), F32)
                for h in range(heads):
                    dqn[r] = jnp.where(hmask[h], jnp.dot(ds[r, h], kn[r], preferred_element_type=F32), dqn[r])
                    dkn[r] = jnp.where(hmask[h], lax.dot_general(ds[r, h], qn[r], _DIMS["tn"],
                                                                 preferred_element_type=F32), dkn[r])
                    dvv[r] = jnp.where(hmask[h], lax.dot_general(p[r, h], dob[r], _DIMS["tn"],
                                                                 preferred_element_type=F32), dvv[r])
            for r in rs:
                dq_all, dg = _head_norm_bwd(dqn[r], qhat[r], qrstd[r], gq, ones)
                dgq = dgq + dg
                dk_all, dg = _head_norm_bwd(_rows_of(ck_ref, r, dil) + dkn[r][:ATTN_BLK], kphat[r], kprstd[r], gk, ones)
                dgk = dgk + dg
                dv_all = _rows_of(cvv_ref, r, dil) + dvv[r][:ATTN_BLK]

                @pl.when(live)
                def _(dq_all=dq_all, r=r):
                    _put_rows(dq_ref, r, dil, dq_all)

                _put_rows(dk_ref, r, dil, dk_all)
                _put_rows(dv_ref, r, dil, dv_all)
                _put_rows(ck_ref, r, dil, dkn[r][ATTN_BLK:])
                _put_rows(cvv_ref, r, dil, dvv[r][ATTN_BLK:])
        gq_acc[...] += dgq
        gk_acc[...] += dgk

        @pl.when(jnp.logical_and(n == nb, hp == npairs - 1))
        def _():
            fold = lambda a: sum(a[:, h * HEAD:(h + 1) * HEAD] for h in range(heads))
            dgq_ref[...] = jnp.sum(fold(gq_acc[...]), axis=0, keepdims=True)
            dgk_ref[...] = jnp.sum(fold(gk_acc[...]), axis=0, keepdims=True)

    last = nb - 1
    cur = lambda c: pl.BlockSpec((rows, pw), lambda hp, n, c=c: (jnp.minimum(n, last), c + hp))
    prev = lambda c: pl.BlockSpec((rows, pw), lambda hp, n, c=c: (jnp.maximum(n - 1, 0), c + hp))
    gspec = pl.BlockSpec((1, HEAD), lambda hp, n: (0, 0))
    act = jax.ShapeDtypeStruct((S_, ATTN_OUT), F32)
    vec = jax.ShapeDtypeStruct((1, HEAD), F32)
    return _call(
        body, [z, z, z, z, z, q_gain, k_gain, do, o, lse],
        [cur(cq), cur(ck), prev(ck), cur(cv), prev(cv), gspec, gspec, cur(0), cur(0), cur(0)], after,
        name=name, grid=(npairs, nb + 1),
        out_specs=(cur(0), prev(0), prev(0), gspec, gspec),
        out_shape=(act, act, act, vec, vec),
        scratch_shapes=[pltpu.VMEM((rows, pw), F32), pltpu.VMEM((rows, pw), F32),
                        pltpu.VMEM((8, pw), F32), pltpu.VMEM((8, pw), F32)],
        compiler_params=_cparams(("arbitrary", "arbitrary")))


MIX_TN = 256
MIX_TM = 2048


def _sigmoid(v):
    return 1.0 / (1.0 + jnp.exp(-v))


def _mix_fwd(name, z, b_gate, ys, ws):
    S_ = z.shape[0]
    tm, tn = MIX_TM, MIX_TN
    gblk = OFF_GATE // tn

    def body(yp, yc, ya, wp, wc, wa, g0, g1, g2, b0, b1, b2, m_ref):
        acc = None
        for y_ref, w_ref, g_ref, b_ref in ((yp, wp, g0, b0), (yc, wc, g1, b1), (ya, wa, g2, b2)):
            u = lax.dot_general(y_ref[...].astype(BF16), w_ref[...], _DIMS["nt"], preferred_element_type=F32)
            t = _sigmoid(g_ref[...] + b_ref[...]) * u
            acc = t if acc is None else acc + t
        m_ref[...] = acc.astype(BF16)

    yspec = lambda w: pl.BlockSpec((tm, w), lambda i, j: (i, 0))
    wspec = lambda w: pl.BlockSpec((tn, w), lambda i, j: (j, 0))
    gspec = lambda b: pl.BlockSpec((tm, tn), lambda i, j, b=b: (i, gblk + b * (D // tn) + j))
    bspec = lambda b: pl.BlockSpec((1, tn), lambda i, j, b=b: (0, b * (D // tn) + j))
    return pl.pallas_call(
        body, name=name, grid=(S_ // tm, D // tn),
        in_specs=[yspec(POOL_W), yspec(CONV_W), yspec(ATTN_OUT), wspec(POOL_W), wspec(CONV_W), wspec(ATTN_OUT),
                  gspec(0), gspec(1), gspec(2), bspec(0), bspec(1), bspec(2)],
        out_specs=pl.BlockSpec((tm, tn), lambda i, j: (i, j)),
        out_shape=jax.ShapeDtypeStruct((S_, D), BF16),
        compiler_params=_cparams(("parallel", "parallel")),
    )(*ys, *ws, z, z, z, b_gate, b_gate, b_gate)


def _mix_bwd(name, z, b_gate, y, w, dmerged, branch, after=None):
    S_ = z.shape[0]
    tm, tn = MIX_TM, MIX_TN
    width = y.shape[1]
    gblk = OFF_GATE // tn + branch * (D // tn)
    ni = S_ // tm

    def body(y_ref, w_ref, g_ref, b_ref, dm_ref, du_ref, dg_ref, db_ref, acc_ref):
        i = pl.program_id(1)
        u = lax.dot_general(y_ref[...].astype(BF16), w_ref[...], _DIMS["nt"], preferred_element_type=F32)
        sg = _sigmoid(g_ref[...] + b_ref[...])
        dm = dm_ref[...]
        du_ref[...] = (sg * dm).astype(BF16)
        dpre = dm * u * sg * (1.0 - sg)
        dg_ref[...] = dpre.astype(BF16)
        part = jnp.sum(dpre.reshape(tm // 8, 8, tn), axis=0)

        @pl.when(i == 0)
        def _():
            acc_ref[...] = part

        @pl.when(i > 0)
        def _():
            acc_ref[...] += part

        @pl.when(i == ni - 1)
        def _():
            db_ref[...] = jnp.sum(acc_ref[...], axis=0, keepdims=True)

    blk = pl.BlockSpec((tm, tn), lambda j, i: (i, j))
    vec = pl.BlockSpec((1, tn), lambda j, i: (0, j))
    act = jax.ShapeDtypeStruct((S_, D), BF16)
    return _call(
        body, [y, w, z, b_gate, dmerged],
        [pl.BlockSpec((tm, width), lambda j, i: (i, 0)), pl.BlockSpec((tn, width), lambda j, i: (j, 0)),
         pl.BlockSpec((tm, tn), lambda j, i: (i, gblk + j)),
         pl.BlockSpec((1, tn), lambda j, i: (0, branch * (D // tn) + j)), blk], after,
        name=name, grid=(D // tn, ni),
        out_specs=(blk, blk, vec), out_shape=(act, act, jax.ShapeDtypeStruct((1, D), F32)),
        scratch_shapes=[pltpu.VMEM((8, tn), F32)],
        compiler_params=_cparams(("parallel", "arbitrary")))


def _relu2_epilogue(acc):
    r = jnp.maximum(acc, 0.0)
    return acc, r * r


def _residual_norm_epilogue(acc, xr, gain):
    x = xr + acc
    ms = jnp.mean(x * x, axis=-1, keepdims=True)
    return x, x * lax.rsqrt(ms + EPS) * gain


def _layer_fwd(l, x, h, p, next_gain):
    t = f"l{l}_"
    z = _matmul(t + "in_proj", h, p["w_in_t"], "nt", 512, 3712, 1024)
    y_pool = _pool_fwd(t + "pool", z, p["pool_mix"], p["pool_scale"])
    y_conv = _conv_fwd(t + "conv", z, p["conv_w"])
    os_, ls_ = [], []
    for g, (_, dil) in enumerate(ATTN_GROUPS):
        o_g, l_g = _attn_fwd(t + f"attn{g}", z, p["q_gain"], p["k_gain"], g, dil)
        os_.append(o_g)
        ls_.append(l_g)
    y_attn, lse = _attn_combine(t + "attn_mix", os_, ls_)
    merged = _mix_fwd(t + "merge", z, p["b_gate"], (y_pool, y_conv, y_attn),
                      (p["w_pool_up_t"], p["w_conv_out_t"], p["w_attn_up_t"]))
    x1, h2 = _matmul(t + "out_proj", merged, p["w_o"], "nn", 1024, D, 1024, out_dtypes=(F32, BF16),
                     extras=((x, "mn"), (p["norm_mlp"], "n")), epilogue=_residual_norm_epilogue)
    a, r = _matmul(t + "ff1", h2, p["w_ff1_t"], "nt", 1024, 1024, 1024, out_dtypes=(F32, BF16),
                   epilogue=_relu2_epilogue)
    if next_gain is None:
        x2 = _matmul(t + "ff2", r, p["w_ff2"], "nn", 1024, D, 1024,
                     extras=((x1, "mn"),), epilogue=lambda acc, xr: (xr + acc,))
        h_out = None
    else:
        x2, h_out = _matmul(t + "ff2", r, p["w_ff2"], "nn", 1024, D, 1024, out_dtypes=(F32, BF16),
                            extras=((x1, "mn"), (next_gain, "n")), epilogue=_residual_norm_epilogue)
    saved = dict(x=x, h=h, z=z, y_pool=y_pool, y_conv=y_conv, y_attn=y_attn, lse=lse, merged=merged,
                 x1=x1, h2=h2, a=a, r=r)
    return x2, h_out, saved


def _layer_bwd(l, dx2, p, s, pending, collective_id, last, start_after=None):
    t = f"l{l}_b_"
    g = {}
    rest = jax.ShapeDtypeStruct((N_DEV, REST_ROWS, D), F32)
    da = _matmul(t + "d_ff2_in", dx2, p["w_ff2"], "nt", 1024, 1024, 1024, out_dtypes=(BF16,),
                 extras=((s["a"], "mn"),), epilogue=lambda acc, a: (acc * (2.0 * jnp.maximum(a, 0.0)),),
                 after=start_after)
    rest = _matmul(t + "dw_ff2", s["r"], dx2, "tn", 512, 1024, 4096, into=(rest, REST_FF2, 0), after=da)
    tok = rest if pending is None else pending.chip_sums(after=rest)
    dh2 = _matmul(t + "d_ff1_in", da, p["w_ff1_t"], "nn", 1024, 1024, 1024, after=tok)
    rest = _matmul(t + "dw_ff1", da, s["h2"], "tn", 512, 1024, 4096, into=(rest, REST_FF1, 0), after=dh2)
    dx1, g["norm_mlp"] = _rmsnorm_bwd(t + "norm_mlp", s["x1"], p["norm_mlp"], dh2, dx2, after=rest)
    dmerged = _matmul(t + "d_out_proj_in", dx1, p["w_o"], "nt", 1024, 1024, 1024)
    rest = _matmul(t + "dw_o", s["merged"], dx1, "tn", 128, 1024, 4096, into=(rest, REST_WO, 0), after=dmerged)
    ys = (s["y_pool"], s["y_conv"], s["y_attn"])
    names = ("w_pool_up_t", "w_conv_out_t", "w_attn_up_t")
    dys, dgates, dbs = [], [], []
    tok = rest
    for b in range(3):
        du, dgz, db = _mix_bwd(t + f"merge{b}", s["z"], p["b_gate"], ys[b], p[names[b]], dmerged, b, after=tok)
        width = ys[b].shape[1]
        dys.append(_matmul(t + f"d_up{b}_in", du, p[names[b]], "nn", 512, width, 1024))
        if b < 2:
            rest = _matmul(t + f"dw_up{b}", du, ys[b], "tn", 128, width, 4096, into=(rest, REST_UP, b * width),
                           after=dys[b])
            tok = rest
        else:
            d_attn_up_t = _matmul(t + f"dw_up{b}", du, ys[b], "tn", 128, width, 4096, after=dys[b])
            rest = rest.at[:, REST_ATTN:REST_ROWS, :].set(d_attn_up_t.reshape(N_DEV, REST_ROWS - REST_ATTN, D))
        dgates.append(dgz)
        dbs.append(db)
    g["b_gate"] = jnp.concatenate(dbs, axis=1)
    rs_rest = _ReduceScatter(f"rs_rest{l}", collective_id, rest)
    pending_sum = None if pending is None else pending.finish(after=rest)
    du_pool, g["pool_mix"], g["pool_scale"] = _pool_bwd(t + "pool", s["z"], dys[0], p["pool_mix"], p["pool_scale"],
                                                        after=rest if pending is None else pending_sum[0])
    dcb, dcc, dcx, g["conv_w"] = _conv_bwd(t + "conv", s["z"], dys[1], p["conv_w"], after=du_pool)
    tok = rs_rest.chip_sums(after=dcb)
    dqs, dks, dvs = [], [], []
    gq = gk = None
    for gi, (_, dil) in enumerate(ATTN_GROUPS):
        dq, dk, dv, dgq, dgk = _attn_bwd(t + f"attn{gi}", s["z"], p["q_gain"], p["k_gain"], dys[2], s["y_attn"],
                                         s["lse"], gi, dil, after=tok)
        tok = dq
        dqs.append(dq)
        dks.append(dk)
        dvs.append(dv)
        gq = dgq if gq is None else gq + dgq
        gk = dgk if gk is None else gk + dgk
    g["q_gain"], g["k_gain"] = gq, gk
    dz = jnp.concatenate([du_pool, dcb, dcc, dcx] + [v.astype(BF16) for v in dqs + dks + dvs] + dgates, axis=1)
    in_t = _matmul(t + "dw_in", dz, s["h"], "tn", 256, 1024, 4096)
    rs_in = _ReduceScatter(f"rs_in{l}", collective_id + 2, in_t.reshape(N_DEV, IN_ROWS, D))
    dh = _matmul(t + "d_in_proj_in", dz, p["w_in_t"], "nn", 512, 1024, 3712, after=in_t)
    tok = rs_in.chip_sums(after=dh) if last else dh
    rest_sum, _ = rs_rest.finish(after=tok)
    dx, g["norm_mix"] = _rmsnorm_bwd(t + "norm_mix", s["x"], p["norm_mix"], dh, dx1, after=rest_sum)
    return dx, g, rest_sum, pending_sum, rs_in


ANY = pl.BlockSpec(memory_space=pl.ANY)


def _mesh_pos():
    return lax.axis_index("x"), lax.axis_index("y"), lax.axis_index("c")


def _other_chips(x, y):
    return [(1 - x, y), (x, 1 - y), (1 - x, 1 - y)]


def _comm_call(name, collective_id, peers, body, arrs, out_shape, sem_counts):
    n_in, n_out = len(arrs), len(out_shape)
    if collective_id is None:
        def tc_body(*refs):
            body(refs[:n_in], refs[n_in:n_in + n_out], *refs[n_in + n_out:])

        return pl.pallas_call(
            tc_body, name=name, out_shape=tuple(out_shape), in_specs=[ANY] * n_in, out_specs=(ANY,) * n_out,
            scratch_shapes=[pltpu.SemaphoreType.DMA((n,)) for n in sem_counts])(*arrs)

    def seq_body(*refs):
        barrier = pltpu.get_barrier_semaphore()
        ps = peers(*_mesh_pos())
        for p in ps:
            pl.semaphore_signal(barrier, inc=1, device_id=p, device_id_type=MESH)
        pl.semaphore_wait(barrier, len(ps))
        body(refs[:n_in], refs[n_in:n_in + n_out], *refs[n_in + n_out:])

    return pl.kernel(
        seq_body, out_type=tuple(out_shape), mesh=plsc.ScalarSubcoreMesh(axis_name="seq", num_cores=1), name=name,
        scratch_types=[pltpu.SemaphoreType.DMA((n,)) for n in sem_counts],
        compiler_params=pltpu.CompilerParams(collective_id=collective_id),
    )(*arrs)


def _all_gather(name, collective_id, shard):
    R, C = shard.shape

    def peers(x, y, c):
        return [(x, y, 1 - c)] + [(*chip, c) for chip in _other_chips(x, y)]

    def body(in_refs, out_refs, send_sems, recv_sems, local_sems):
        (x_ref,), (out_ref,) = in_refs, out_refs
        x, y, c = _mesh_pos()
        me, sibling = (x, y, c), (x, y, 1 - c)
        chips = _other_chips(x, y)

        def slot(px, py, pc):
            return out_ref.at[4 * px + 2 * py + pc]

        def copy(k, block, to, src=None):
            return pltpu.make_async_remote_copy(
                src_ref=slot(*block) if src is None else src, dst_ref=slot(*block),
                send_sem=send_sems.at[k], recv_sem=recv_sems.at[k], device_id=to, device_id_type=MESH)

        mine = pltpu.make_async_copy(x_ref, slot(*me), local_sems.at[0])
        mine.start()
        first = [copy(0, me, sibling, src=x_ref)]
        first += [copy(1 + j, me, (*chip, c), src=x_ref) for j, chip in enumerate(chips)]
        for cp in first:
            cp.start()
        passed = [copy(4 + j, (*chip, c), sibling) for j, chip in enumerate(chips)]
        for j, chip in enumerate(chips):
            copy(1 + j, (*chip, c), me).wait_recv()
            passed[j].start()
        copy(0, sibling, me).wait_recv()
        for j, chip in enumerate(chips):
            copy(4 + j, (*chip, 1 - c), me).wait_recv()
        for cp in first + passed:
            cp.wait_send()
        mine.wait()

    return _comm_call(name, collective_id, peers, body, [shard],
                      [jax.ShapeDtypeStruct((N_DEV, R, C), shard.dtype)], (7, 7, 1))[0]


def _rs_sibling_exchange(name, collective_id, arrs):
    n = len(arrs)

    def body(in_refs, out_refs, send_sems, recv_sems):
        x, y, c = _mesh_pos()
        cps = []
        for k, (src, dst) in enumerate(zip(in_refs, out_refs)):
            src = src.at[:, 1 - c] if len(src.shape) == 4 else src
            cps.append(pltpu.make_async_remote_copy(src_ref=src, dst_ref=dst, send_sem=send_sems.at[k],
                                                    recv_sem=recv_sems.at[k], device_id=(x, y, 1 - c),
                                                    device_id_type=MESH))
        for cp in cps:
            cp.start()
        for cp in cps:
            cp.wait()

    out_shape = [jax.ShapeDtypeStruct(a.shape[:1] + a.shape[2:] if a.ndim == 4 else a.shape, a.dtype) for a in arrs]
    return _comm_call(name, collective_id, lambda x, y, c: [(x, y, 1 - c)], body, arrs, out_shape, (n, n))


def _rs_rows(r):
    return r // 2 if (r // 2) % 16 == 0 else r


def _drop_ref(body, idx):
    def wrapped(*refs):
        return body(*refs[:idx], *refs[idx + 1:])

    return wrapped


def _rs_chip_sum(name, ids, big, rbig, after=None):
    _, _, R, C = big.shape
    rows = _rs_rows(R)

    def body(ids_ref, mine_ref, theirs_ref, t16_ref, own_ref):
        p = pl.program_id(1)
        s = mine_ref[...] + theirs_ref[...]
        t16_ref[...] = s.astype(BF16)

        @pl.when(p == ids_ref[1])
        def _():
            own_ref[...] = s

    in_specs = [pl.BlockSpec((None, None, rows, C), lambda i, p, ids: (p, ids[0], i, 0)),
                pl.BlockSpec((None, rows, C), lambda i, p, ids: (p, i, 0))]
    args = [ids, big, rbig]
    if after is not None:
        body = _drop_ref(body, len(args))
        in_specs.append(ANY)
        args.append(after)
    grid_spec = pltpu.PrefetchScalarGridSpec(
        num_scalar_prefetch=1, grid=(R // rows, 4), in_specs=in_specs,
        out_specs=(pl.BlockSpec((None, rows, C), lambda i, p, ids: (p, i, 0)),
                   pl.BlockSpec((rows, C), lambda i, p, ids: (i, 0))))
    return pl.pallas_call(
        body, name=name, grid_spec=grid_spec,
        out_shape=(jax.ShapeDtypeStruct((4, R, C), BF16), jax.ShapeDtypeStruct((R, C), F32)),
        compiler_params=_cparams(("parallel", "arbitrary")),
    )(*args)


def _add2(name, a, b):
    def body(a_ref, b_ref, o_ref):
        o_ref[...] = a_ref[...] + b_ref[...]

    full = pl.BlockSpec(a.shape, lambda i: (0, 0))
    return pl.pallas_call(body, name=name, grid=(1,), in_specs=[full, full], out_specs=full,
                          out_shape=jax.ShapeDtypeStruct(a.shape, a.dtype))(a, b)


def _rs_chip_exchange(name, collective_id, arrs):
    n = len(arrs)

    def body(in_refs, out_refs, send_sems, recv_sems, local_sems):
        x, y, c = _mesh_pos()
        p_me = 2 * x + y
        chips = _other_chips(x, y)

        def part(ref, p):
            return ref.at[p] if len(ref.shape) == 3 else ref

        local = [pltpu.make_async_copy(part(src, p_me), dst.at[p_me], local_sems.at[k])
                 for k, (src, dst) in enumerate(zip(in_refs, out_refs))]
        for cp in local:
            cp.start()
        sends = []
        for j, (px, py) in enumerate(chips):
            for k, (src, dst) in enumerate(zip(in_refs, out_refs)):
                sends.append(pltpu.make_async_remote_copy(
                    src_ref=part(src, 2 * px + py), dst_ref=dst.at[p_me], send_sem=send_sems.at[n * j + k],
                    recv_sem=recv_sems.at[n * j + k], device_id=(px, py, c), device_id_type=MESH))
        for cp in sends:
            cp.start()
        for j, (px, py) in enumerate(chips):
            for k, (src, dst) in enumerate(zip(in_refs, out_refs)):
                pltpu.make_async_remote_copy(
                    src_ref=part(src, p_me), dst_ref=dst.at[2 * px + py], send_sem=send_sems.at[n * j + k],
                    recv_sem=recv_sems.at[n * j + k], device_id=(px, py, c), device_id_type=MESH).wait_recv()
        for cp in sends:
            cp.wait_send()
        for cp in local:
            cp.wait()

    out_shape = [jax.ShapeDtypeStruct((4,) + a.shape[-2:], a.dtype) for a in arrs]
    return _comm_call(name, collective_id, lambda x, y, c: [(*chip, c) for chip in _other_chips(x, y)], body, arrs,
                      out_shape, (3 * n, 3 * n, n))


def _rs_final_sum(name, ids, recv, own, rows, after=None):
    _, R, C = recv.shape
    assert R % rows == 0

    def body(ids_ref, r_ref, own_ref, o_ref):
        acc = None
        for p in range(4):
            term = jnp.where(ids_ref[1] == p, own_ref[...], r_ref[p].astype(F32))
            acc = term if acc is None else acc + term
        o_ref[...] = acc

    in_specs = [pl.BlockSpec((4, rows, C), lambda i, ids: (0, i, 0)), pl.BlockSpec((rows, C), lambda i, ids: (i, 0))]
    args = [ids, recv, own]
    if after is not None:
        body = _drop_ref(body, len(args))
        in_specs.append(ANY)
        args.append(after)
    grid_spec = pltpu.PrefetchScalarGridSpec(
        num_scalar_prefetch=1, grid=(R // rows,), in_specs=in_specs,
        out_specs=pl.BlockSpec((rows, C), lambda i, ids: (i, 0)))
    return pl.pallas_call(
        body, name=name, grid_spec=grid_spec, out_shape=jax.ShapeDtypeStruct((R, C), F32),
        compiler_params=_cparams(("parallel",)),
    )(*args)


class _ReduceScatter:
    def __init__(self, tag, collective_id, big, small=None):
        x, y, c = _mesh_pos()
        self.tag, self.cid, self.small = tag, collective_id, small
        self.ids = jnp.stack([c, 2 * x + y]).astype(jnp.int32)
        self.big = big.reshape((4, 2) + big.shape[1:])
        self.got = _rs_sibling_exchange(tag + "_sibling", collective_id,
                                        [self.big] + ([] if small is None else [small]))

    def chip_sums(self, after=None):
        t16, self.own = _rs_chip_sum(self.tag + "_chip_sum", self.ids, self.big, self.got[0], after)
        arrs = [t16]
        if self.small is not None:
            self.ts = _add2(self.tag + "_chip_sum_small", self.small, self.got[1])
            arrs.append(self.ts)
        self.recv = _rs_chip_exchange(self.tag + "_chips", self.cid + 1, arrs)
        return t16

    def finish(self, after=None):
        out = _rs_final_sum(self.tag + "_final", self.ids, self.recv[0], self.own, _rs_rows(self.own.shape[0]), after)
        out_small = None
        if self.small is not None:
            out_small = _rs_final_sum(self.tag + "_final_small", self.ids, self.recv[1], self.ts, self.small.shape[0])
        return out, out_small


def _all_reduce_small(tag, small):
    x, y, c = _mesh_pos()
    ids = jnp.stack([c, 2 * x + y]).astype(jnp.int32)
    (theirs,) = _rs_sibling_exchange(tag + "_sibling", None, [small])
    ts = _add2(tag + "_chip_sum", small, theirs)
    (recv,) = _rs_chip_exchange(tag + "_chips", None, [ts])
    return _rs_final_sum(tag + "_final", ids, recv, ts, small.shape[0])


def _adamw(name, w, g, m, v):
    R, C = w.shape
    tr = max(t for t in range(8, 513, 8) if R % t == 0)
    c1 = 1.0 - ADAM_B1 ** ADAM_STEP
    c2 = 1.0 - ADAM_B2 ** ADAM_STEP

    def body(w_ref, g_ref, m_ref, v_ref, d_ref, nm_ref, nv_ref):
        gv = g_ref[...]
        nm = ADAM_B1 * m_ref[...] + (1.0 - ADAM_B1) * gv
        nv = ADAM_B2 * v_ref[...] + (1.0 - ADAM_B2) * (gv * gv)
        d_ref[...] = -ADAM_LR * ((nm / c1) / (jnp.sqrt(nv / c2) + ADAM_EPS) + ADAM_WD * w_ref[...])
        nm_ref[...] = nm
        nv_ref[...] = nv

    blk = pl.BlockSpec((tr, C), lambda i: (i, 0))
    shp = jax.ShapeDtypeStruct((R, C), F32)
    return pl.pallas_call(
        body, name=name, grid=(R // tr,), in_specs=[blk] * 4, out_specs=(blk,) * 3, out_shape=(shp,) * 3,
        compiler_params=_cparams(("parallel",)),
    )(w, g, m, v)


def _adamw_t(name, w, g_t, m, v):
    L, R, C = w.shape
    tr = 256
    c1 = 1.0 - ADAM_B1 ** ADAM_STEP
    c2 = 1.0 - ADAM_B2 ** ADAM_STEP

    def body(w_ref, g_ref, m_ref, v_ref, go_ref, d_ref, nm_ref, nv_ref):
        gv = g_ref[...].T
        go_ref[...] = gv
        nm = ADAM_B1 * m_ref[...] + (1.0 - ADAM_B1) * gv
        nv = ADAM_B2 * v_ref[...] + (1.0 - ADAM_B2) * (gv * gv)
        d_ref[...] = -ADAM_LR * ((nm / c1) / (jnp.sqrt(nv / c2) + ADAM_EPS) + ADAM_WD * w_ref[...])
        nm_ref[...] = nm
        nv_ref[...] = nv

    blk = pl.BlockSpec((None, tr, C), lambda l, i: (l, i, 0))
    blk_t = pl.BlockSpec((None, C, tr), lambda l, i: (l, 0, i))
    shp = jax.ShapeDtypeStruct((L, R, C), F32)
    return pl.pallas_call(
        body, name=name, grid=(L, R // tr), in_specs=[blk, blk_t, blk, blk], out_specs=(blk,) * 4,
        out_shape=(shp,) * 4, compiler_params=_cparams(("parallel", "parallel")),
    )(w, g_t, m, v)


REST_FF1 = 0
REST_FF2 = 512
REST_WO = 1024
REST_UP = 1152
REST_ATTN = 1280
REST_ROWS = 1312
IN_ROWS = IN_COLS // N_DEV
CONV_BITS_ROWS = 16
SHARD_ROWS = IN_ROWS + REST_ROWS + CONV_BITS_ROWS
SMALL = (("norm_mix", (DEPTH, D)), ("b_gate", (DEPTH, 3 * D)), ("pool_mix", (DEPTH, 4, POOL_G, POOL_G)),
         ("pool_scale", (DEPTH, POOL_W)), ("conv_w", (DEPTH, 3, CONV_W)), ("q_gain", (DEPTH, HEAD)),
         ("k_gain", (DEPTH, HEAD)), ("norm_mlp", (DEPTH, D)))


def _pack_weight_shards(w, l):
    b = lambda a: a.astype(BF16)
    conv = lax.bitcast_convert_type(w["conv_w"][l], BF16).reshape(3, 128)
    conv = jnp.pad(conv, ((0, CONV_BITS_ROWS - 3), (0, D - 128)))
    rest = jnp.concatenate([
        b(w["w_ff1"][l].T), b(w["w_ff2"][l]), b(w["w_o"][l]),
        jnp.concatenate([b(w["w_pool_up"][l].T), b(w["w_conv_out"][l].T)], axis=1),
        b(w["w_attn_up"][l].T).reshape(REST_ROWS - REST_ATTN, D), conv], axis=0)
    return b(w["w_in"][l].T), rest


def _unpack_gathered(g_in, g_rest, small_w, l):
    rest = g_rest[:, :REST_ROWS]
    take = lambda r0, rows, c0=0, cols=D: rest[:, r0:r0 + rows, c0:c0 + cols].reshape(N_DEV * rows, cols)
    conv = g_rest[:, REST_ROWS:REST_ROWS + 3, :128].reshape(N_DEV, 3, CONV_W // N_DEV, 2)
    conv = lax.bitcast_convert_type(conv, F32)
    p = {
        "w_in_t": g_in.reshape(IN_COLS, D),
        "w_ff1_t": take(REST_FF1, 512), "w_ff2": take(REST_FF2, 512), "w_o": take(REST_WO, 128),
        "w_pool_up_t": take(REST_UP, 128, 0, POOL_W), "w_conv_out_t": take(REST_UP, 128, POOL_W, CONV_W),
        "w_attn_up_t": rest[:, REST_ATTN:].reshape(D, ATTN_OUT),
        "conv_w": jnp.transpose(conv, (1, 0, 2)).reshape(3, CONV_W),
        "pool_mix": small_w["pool_mix"][l],
    }
    for name in ("norm_mix", "b_gate", "pool_scale", "q_gain", "k_gain", "norm_mlp"):
        p[name] = small_w[name][l][None, :]
    return p


def _pack_small_grads(grads):
    flat = jnp.concatenate([jnp.stack([grads[l][name] for l in range(DEPTH)]).reshape(-1) for name, _ in SMALL])
    return jnp.pad(flat, (0, (-flat.shape[0]) % (8 * 128))).reshape(-1, 128)


def _unpack_grads(in_t, rest, small, dev):
    out = {
        "w_in_t": jnp.stack(in_t),
        "w_ff1_t": jnp.stack([a[REST_FF1:REST_FF1 + 512] for a in rest]),
        "w_ff2": jnp.stack([a[REST_FF2:REST_FF2 + 512] for a in rest]),
        "w_o": jnp.stack([a[REST_WO:REST_WO + 128] for a in rest]),
        "w_pool_up": jnp.stack([a[REST_UP:REST_UP + 128, :POOL_W].T for a in rest]),
        "w_conv_out": jnp.stack([a[REST_UP:REST_UP + 128, POOL_W:].T for a in rest]),
        "w_attn_up": jnp.stack([a[REST_ATTN:].reshape(D // N_DEV, ATTN_OUT).T for a in rest]),
    }
    flat = small.reshape(-1)
    off = 0
    for name, shp in SMALL:
        n = 1
        for s in shp:
            n *= s
        out[name] = flat[off:off + n].reshape(shp)
        off += n
    width = CONV_W // N_DEV
    out["conv_w"] = lax.dynamic_slice_in_dim(out["conv_w"], dev * width, width, axis=2)
    return out


def _pack_small(arrs):
    flat = jnp.concatenate([a.reshape(-1) for a in arrs])
    pad = (-flat.shape[0]) % (8 * 128)
    return jnp.pad(flat, (0, pad)).reshape(-1, 128)


def _unpack_small(packed, like):
    flat = packed.reshape(-1)
    out, off = [], 0
    for a in like:
        out.append(flat[off:off + a.size].reshape(a.shape))
        off += a.size
    return out


WEIGHTS = ("norm_mix", "w_in", "b_gate", "pool_mix", "pool_scale", "conv_w", "q_gain", "k_gain", "w_pool_up",
           "w_conv_out", "w_attn_up", "w_o", "norm_mlp", "w_ff1", "w_ff2")


def kernel(x, norm_mix, w_in, b_gate, pool_mix, pool_scale, conv_w, q_gain, k_gain, w_pool_up, w_conv_out, w_attn_up, w_o, norm_mlp, w_ff1, w_ff2, loss_target, m_norm_mix, m_w_in, m_b_gate, m_pool_mix, m_pool_scale, m_conv_w, m_q_gain, m_k_gain, m_w_pool_up, m_w_conv_out, m_w_attn_up, m_w_o, m_norm_mlp, m_w_ff1, m_w_ff2, v_norm_mix, v_w_in, v_b_gate, v_pool_mix, v_pool_scale, v_conv_w, v_q_gain, v_k_gain, v_w_pool_up, v_w_conv_out, v_w_attn_up, v_w_o, v_norm_mlp, v_w_ff1, v_w_ff2):
    w = dict(zip(WEIGHTS, (norm_mix, w_in, b_gate, pool_mix, pool_scale, conv_w, q_gain, k_gain, w_pool_up,
                           w_conv_out, w_attn_up, w_o, norm_mlp, w_ff1, w_ff2)))
    m = dict(zip(WEIGHTS, (m_norm_mix, m_w_in, m_b_gate, m_pool_mix, m_pool_scale, m_conv_w, m_q_gain, m_k_gain,
                           m_w_pool_up, m_w_conv_out, m_w_attn_up, m_w_o, m_norm_mlp, m_w_ff1, m_w_ff2)))
    v = dict(zip(WEIGHTS, (v_norm_mix, v_w_in, v_b_gate, v_pool_mix, v_pool_scale, v_conv_w, v_q_gain, v_k_gain,
                           v_w_pool_up, v_w_conv_out, v_w_attn_up, v_w_o, v_norm_mlp, v_w_ff1, v_w_ff2)))
    xi, yi, ci = _mesh_pos()
    dev = 4 * xi + 2 * yi + ci

    params = []
    for l in range(DEPTH):
        s_in, s_rest = _pack_weight_shards(w, l)
        g_in = _all_gather(f"gather_in{l}", 1 + 2 * l, s_in)
        g_rest = _all_gather(f"gather_rest{l}", 2 + 2 * l, s_rest)
        params.append(_unpack_gathered(g_in, g_rest, w, l))
    saved = []
    act = x[0]
    h = _rmsnorm_fwd("l0_norm_mix", act, params[0]["norm_mix"])
    for l in range(DEPTH):
        act, h, s = _layer_fwd(l, act, h, params[l], params[l + 1]["norm_mix"] if l + 1 < DEPTH else None)
        saved.append(s)
    dy, loss = _loss_head("loss_head", act, loss_target[0])
    total = lax.psum(loss[0, 0], ("x", "y", "c"))
    in_t, rest, grads, pending = [None] * DEPTH, [None] * DEPTH, [None] * DEPTH, None
    start = total.reshape(1, 1)
    for l in reversed(range(DEPTH)):
        dy, grads[l], rest[l], done, pending = _layer_bwd(l, dy, params[l], saved[l], pending, 5 + 4 * l, l == 0,
                                                          start)
        start = None
        if done is not None:
            in_t[l + 1] = done[0]
    dx = dy
    g_small = _all_reduce_small("ar_small", _pack_small_grads(grads))
    in_t[0], _ = pending.finish(after=g_small)
    g = _unpack_grads(in_t, rest, g_small, dev)

    delta, new_m, new_v = {}, {}, {}
    for name in ("w_in", "w_ff1"):
        t = lambda a: jnp.swapaxes(a, 1, 2)
        g_t = g[name + "_t"]
        two_d = (g_t.shape[0] * g_t.shape[1], g_t.shape[2])
        d_, m_, v_ = _adamw("adamw_" + name, t(w[name]).reshape(two_d), g_t.reshape(two_d),
                            t(m[name]).reshape(two_d), t(v[name]).reshape(two_d))
        g[name] = t(g_t)
        delta[name], new_m[name], new_v[name] = (t(a.reshape(g_t.shape)) for a in (d_, m_, v_))
    for name in ("w_pool_up", "w_conv_out", "w_attn_up", "w_o", "w_ff2"):
        shp = w[name].shape
        two_d = (shp[0] * shp[1], shp[2])
        d_, m_, v_ = _adamw("adamw_" + name, w[name].reshape(two_d), g[name].reshape(two_d),
                            m[name].reshape(two_d), v[name].reshape(two_d))
        delta[name], new_m[name], new_v[name] = d_.reshape(shp), m_.reshape(shp), v_.reshape(shp)
    small_names = [name for name, _ in SMALL]
    packed = [_pack_small([src[name] for name in small_names]) for src in (w, g, m, v)]
    outs = _adamw("adamw_small", *packed)
    for dst, arr in zip((delta, new_m, new_v), outs):
        for name, val in zip(small_names, _unpack_small(arr, [w[name] for name in small_names])):
            dst[name] = val

    return (total, dx[None], *[g[n] for n in WEIGHTS], *[delta[n] for n in WEIGHTS],
            *[new_m[n] for n in WEIGHTS], *[new_v[n] for n in WEIGHTS])
```

```python
import functools

import jax
import jax.numpy as jnp
from jax import lax
from jax.experimental import pallas as pl
from jax.experimental.pallas import tpu as pltpu
from jax.experimental.pallas import tpu_sc as plsc

F32 = jnp.float32
BF16 = jnp.bfloat16
MESH = pl.DeviceIdType.MESH

D = 1024
SEQ = 4096
DEPTH = 2
N_DEV = 8
POOL_WINDOWS = (2, 4, 8, 16)
POOL_W = 512
POOL_G = 128
CONV_W = 512
HEAD = 64
ATTN_GROUPS = ((128, 1), (512, 4), (2048, 16))
HPG = 4
ATTN_W = 768
ATTN_OUT = 256
ATTN_BLK = 128
D_FF = 4096
EPS = 1e-6
MASK = -1e30
OFF_POOL = 0
OFF_CB = 512
OFF_CC = 1024
OFF_CX = 1536
OFF_Q = 2048
OFF_K = 2816
OFF_V = 3584
OFF_GATE = 4352
IN_COLS = 7424
ADAM_LR = 0.001
ADAM_B1 = 0.9
ADAM_B2 = 0.999
ADAM_EPS = 1e-08
ADAM_WD = 0.01
ADAM_STEP = 10

ROW_TILE = 512
SEQ_CHUNK = 256
HALO = 16
VMEM_LIMIT = 56 * 1024 * 1024


def _cparams(sem=None):
    return pltpu.CompilerParams(dimension_semantics=sem, vmem_limit_bytes=VMEM_LIMIT)


def _call(body, args, in_specs, after=None, **kw):
    if after is not None:
        n, inner = len(args), body

        def body(*refs):
            return inner(*refs[:n], *refs[n + 1:])

        args = list(args) + [after]
        in_specs = list(in_specs) + [pl.BlockSpec(memory_space=pl.ANY)]
    return pl.pallas_call(body, in_specs=in_specs, **kw)(*args)


_DIMS = {"nn": (((1,), (0,)), ((), ())), "nt": (((1,), (1,)), ((), ())), "tn": (((0,), (0,)), ((), ()))}


def _matmul(name, a, b, mode, tm, tn, tk, out_dtypes=(F32,), extras=(), epilogue=None, into=None, after=None):
    if mode == "tn":
        K, M = a.shape
    else:
        M, K = a.shape
    N = b.shape[0] if mode == "nt" else b.shape[1]
    assert M % tm == 0 and N % tn == 0 and K % tk == 0, (name, M, N, K, tm, tn, tk)
    nk = K // tk
    n_extra = len(extras)
    n_out = len(out_dtypes)
    dims = _DIMS[mode]
    n_alias = 0 if into is None or isinstance(into[0], jax.ShapeDtypeStruct) else 1

    def body(a_ref, b_ref, *rest):
        extra_refs = rest[:n_extra]
        out_refs = rest[n_extra + n_alias:n_extra + n_alias + n_out]

        def finish(acc):
            if epilogue is None:
                res = (acc,)
            else:
                res = epilogue(acc, *[r[...] for r in extra_refs])
            for o_ref, v in zip(out_refs, res):
                o_ref[...] = v.astype(o_ref.dtype)

        part = lax.dot_general(a_ref[...].astype(BF16), b_ref[...].astype(BF16), dims,
                               preferred_element_type=F32)
        if nk == 1:
            finish(part)
        else:
            acc_ref = rest[-1]
            k = pl.program_id(2)

            @pl.when(k == 0)
            def _():
                acc_ref[...] = part

            @pl.when(k > 0)
            def _():
                acc_ref[...] += part

            @pl.when(k == nk - 1)
            def _():
                finish(acc_ref[...])

    if mode == "tn":
        a_spec = pl.BlockSpec((tk, tm), lambda i, j, k: (k, i))
    else:
        a_spec = pl.BlockSpec((tm, tk), lambda i, j, k: (i, k))
    if mode == "nt":
        b_spec = pl.BlockSpec((tn, tk), lambda i, j, k: (j, k))
    else:
        b_spec = pl.BlockSpec((tk, tn), lambda i, j, k: (k, j))
    in_specs = [a_spec, b_spec]
    args = [a, b]
    for arr, kind in extras:
        if kind == "mn":
            in_specs.append(pl.BlockSpec((tm, tn), lambda i, j, k: (i, j)))
        else:
            in_specs.append(pl.BlockSpec((1, tn), lambda i, j, k: (0, j)))
        args.append(arr)
    out_shape = tuple(jax.ShapeDtypeStruct((M, N), dt) for dt in out_dtypes)
    out_specs = tuple(pl.BlockSpec((tm, tn), lambda i, j, k: (i, j)) for _ in out_dtypes)
    aliases = {}
    if into is not None:
        buf, row0, col0 = into
        assert n_out == 1 and (M // N_DEV) % tm == 0 and row0 % tm == 0 and col0 % tn == 0
        per_dev = M // N_DEV // tm
        out_shape = (jax.ShapeDtypeStruct(buf.shape, buf.dtype),)
        out_specs = (pl.BlockSpec((None, tm, tn), lambda i, j, k: (i // per_dev, row0 // tm + i % per_dev,
                                                                   col0 // tn + j)),)
        if not isinstance(buf, jax.ShapeDtypeStruct):
            aliases = {len(args): 0}
            in_specs.append(pl.BlockSpec(memory_space=pl.ANY))
            args.append(buf)
    scratch = [] if nk == 1 else [pltpu.VMEM((tm, tn), F32)]
    res = _call(
        body, args, in_specs, after, name=name, grid=(M // tm, N // tn, nk), out_specs=out_specs,
        out_shape=out_shape, scratch_shapes=scratch, input_output_aliases=aliases,
        compiler_params=_cparams(("parallel", "parallel", "arbitrary")))
    return res if n_out > 1 else res[0]


def _rmsnorm_fwd(name, x, gain):
    S_, D_ = x.shape

    def body(x_ref, g_ref, h_ref):
        xf = x_ref[...]
        ms = jnp.mean(xf * xf, axis=-1, keepdims=True)
        h_ref[...] = (xf * lax.rsqrt(ms + EPS) * g_ref[...]).astype(BF16)

    return pl.pallas_call(
        body, name=name, grid=(S_ // ROW_TILE,),
        in_specs=[pl.BlockSpec((ROW_TILE, D_), lambda i: (i, 0)), pl.BlockSpec((1, D_), lambda i: (0, 0))],
        out_specs=pl.BlockSpec((ROW_TILE, D_), lambda i: (i, 0)),
        out_shape=jax.ShapeDtypeStruct((S_, D_), BF16),
        compiler_params=_cparams(("parallel",)),
    )(x, gain)


def _rmsnorm_bwd(name, x, gain, dh, dres, after=None):
    S_, D_ = x.shape
    n = S_ // ROW_TILE

    def body(x_ref, g_ref, dh_ref, dres_ref, dx_ref, dg_ref, acc_ref):
        i = pl.program_id(0)
        xf = x_ref[...]
        rstd = lax.rsqrt(jnp.mean(xf * xf, axis=-1, keepdims=True) + EPS)
        xhat = xf * rstd
        dhv = dh_ref[...]
        dxhat = dhv * g_ref[...]
        c = jnp.mean(dxhat * xhat, axis=-1, keepdims=True)
        dx_ref[...] = dres_ref[...] + rstd * (dxhat - xhat * c)
        part = jnp.sum((dhv * xhat).reshape(ROW_TILE // 8, 8, D_), axis=0)

        @pl.when(i == 0)
        def _():
            acc_ref[...] = part

        @pl.when(i > 0)
        def _():
            acc_ref[...] += part

        @pl.when(i == n - 1)
        def _():
            dg_ref[...] = jnp.sum(acc_ref[...], axis=0, keepdims=True)

    row = pl.BlockSpec((ROW_TILE, D_), lambda i: (i, 0))
    vec = pl.BlockSpec((1, D_), lambda i: (0, 0))
    return _call(
        body, [x, gain, dh, dres], [row, vec, row, row], after, name=name, grid=(n,), out_specs=(row, vec),
        out_shape=(jax.ShapeDtypeStruct((S_, D_), F32), jax.ShapeDtypeStruct((1, D_), F32)),
        scratch_shapes=[pltpu.VMEM((8, D_), F32)],
        compiler_params=_cparams(("arbitrary",)))


def _loss_head(name, y, target):
    S_, D_ = y.shape
    n = S_ // ROW_TILE

    def body(y_ref, t_ref, dy_ref, l_ref, acc_ref):
        i = pl.program_id(0)
        e = y_ref[...] - t_ref[...]
        dy_ref[...] = e * (1.0 / D_)
        part = jnp.sum((e * e).reshape(ROW_TILE // 8, 8, D_), axis=0)

        @pl.when(i == 0)
        def _():
            acc_ref[...] = part

        @pl.when(i > 0)
        def _():
            acc_ref[...] += part

        @pl.when(i == n - 1)
        def _():
            s = jnp.sum(acc_ref[...], axis=1, keepdims=True)
            l_ref[...] = jnp.sum(s, axis=0, keepdims=True) * (0.5 / D_)

    row = pl.BlockSpec((ROW_TILE, D_), lambda i: (i, 0))
    return pl.pallas_call(
        body, name=name, grid=(n,), in_specs=[row, row],
        out_specs=(row, pl.BlockSpec((1, 1), lambda i: (0, 0))),
        out_shape=(jax.ShapeDtypeStruct((S_, D_), F32), jax.ShapeDtypeStruct((1, 1), F32)),
        scratch_shapes=[pltpu.VMEM((8, D_), F32)],
        compiler_params=_cparams(("arbitrary",)),
    )(y, target)


def _rows_with_halo(ref, cols, i, n_chunks, before, after):
    r0 = pl.multiple_of(i * SEQ_CHUNK, SEQ_CHUNK)
    parts = []
    if before:
        h0 = pl.multiple_of(jnp.maximum(r0 - HALO, 0), 8)
        halo = ref[pl.ds(h0, HALO), cols]
        parts.append(jnp.where(i > 0, halo, jnp.zeros_like(halo)))
    parts.append(ref[pl.ds(r0, SEQ_CHUNK), cols])
    if after:
        a0 = pl.multiple_of(jnp.minimum(r0 + SEQ_CHUNK, (n_chunks - 1) * SEQ_CHUNK + SEQ_CHUNK - HALO), 8)
        halo = ref[pl.ds(a0, HALO), cols]
        parts.append(jnp.where(i < n_chunks - 1, halo, jnp.zeros_like(halo)))
    return parts[0] if len(parts) == 1 else jnp.concatenate(parts, axis=0)


def _shift_down(v, k):
    return pltpu.roll(v, k, 0)


def _shift_up(v, k):
    return pltpu.roll(v, v.shape[0] - k, 0)


def _pool_diff(xx, w, t_main):
    s = xx
    k = 1
    while k < w:
        s = s + _shift_down(s, k)
        k *= 2
    cnt = jnp.minimum(t_main + 1, w).astype(F32)
    return s[HALO:] / cnt - xx[HALO:]


def _pool_fwd(name, z, pool_mix, pool_scale):
    S_ = z.shape[0]
    n_chunks = S_ // SEQ_CHUNK

    cols = slice(0, POOL_G)

    def body(u_ref, mix_ref, sc_ref, y_ref):
        mixg = mix_ref[...].astype(BF16)
        scg = sc_ref[...]
        for g, w in enumerate(POOL_WINDOWS):
            @pl.when(pl.program_id(0) == g)
            def _(w=w):
                def chunk(i, carry):
                    r0 = pl.multiple_of(i * SEQ_CHUNK, SEQ_CHUNK)
                    xx = _rows_with_halo(u_ref, cols, i, n_chunks, True, False)
                    t = r0 + lax.broadcasted_iota(jnp.int32, (SEQ_CHUNK, POOL_G), 0)
                    d = _pool_diff(xx, w, t)
                    y = jnp.dot(d.astype(BF16), mixg, preferred_element_type=F32) * scg
                    y_ref[pl.ds(r0, SEQ_CHUNK), :] = y.astype(BF16)
                    return carry

                lax.fori_loop(0, n_chunks, chunk, 0)

    slab = pl.BlockSpec((S_, POOL_G), lambda g: (0, g))
    return pl.pallas_call(
        body, name=name, grid=(len(POOL_WINDOWS),),
        in_specs=[slab, pl.BlockSpec((None, POOL_G, POOL_G), lambda g: (g, 0, 0)),
                  pl.BlockSpec((1, POOL_G), lambda g: (0, g))],
        out_specs=slab, out_shape=jax.ShapeDtypeStruct((S_, POOL_W), BF16),
        compiler_params=_cparams(("parallel",)),
    )(z, pool_mix, pool_scale)


def _pool_bwd(name, z, dy, pool_mix, pool_scale, after=None):
    S_ = z.shape[0]
    n_chunks = S_ // SEQ_CHUNK
    rows_a = SEQ_CHUNK + HALO

    cols = slice(0, POOL_G)

    def body(u_ref, dy_ref, mix_ref, sc_ref, du_ref, dmix_ref, dsc_ref):
        mixg = mix_ref[...].astype(BF16)
        scg = sc_ref[...]
        for g, w in enumerate(POOL_WINDOWS):
            @pl.when(pl.program_id(0) == g)
            def _(w=w):
                def chunk(i, carry):
                    dmix_acc, dsc_acc = carry
                    r0 = pl.multiple_of(i * SEQ_CHUNK, SEQ_CHUNK)
                    xx = _rows_with_halo(u_ref, cols, i, n_chunks, True, False)
                    t = r0 + lax.broadcasted_iota(jnp.int32, (SEQ_CHUNK, POOL_G), 0)
                    d = _pool_diff(xx, w, t).astype(BF16)
                    ypre = jnp.dot(d, mixg, preferred_element_type=F32)
                    dyy = _rows_with_halo(dy_ref, cols, i, n_chunks, False, True)
                    dys = (dyy * scg).astype(BF16)
                    dsc_acc = dsc_acc + jnp.sum((dyy[:SEQ_CHUNK] * ypre).reshape(SEQ_CHUNK // 8, 8, POOL_G), axis=0)
                    dmix_acc = dmix_acc + lax.dot_general(d, dys[:SEQ_CHUNK], _DIMS["tn"],
                                                          preferred_element_type=F32)
                    dd = lax.dot_general(dys, mixg, _DIMS["nt"], preferred_element_type=F32)
                    ta = r0 + lax.broadcasted_iota(jnp.int32, (rows_a, POOL_G), 0)
                    f = dd / jnp.minimum(ta + 1, w).astype(F32)
                    k = 1
                    while k < w:
                        f = f + _shift_up(f, k)
                        k *= 2
                    du_ref[pl.ds(r0, SEQ_CHUNK), :] = (f[:SEQ_CHUNK] - dd[:SEQ_CHUNK]).astype(BF16)
                    return dmix_acc, dsc_acc

                dmix_acc, dsc_acc = lax.fori_loop(
                    0, n_chunks, chunk, (jnp.zeros((POOL_G, POOL_G), F32), jnp.zeros((8, POOL_G), F32)))
                dmix_ref[...] = dmix_acc
                dsc_ref[...] = jnp.sum(dsc_acc, axis=0, keepdims=True)

    slab = pl.BlockSpec((S_, POOL_G), lambda g: (0, g))
    mix_spec = pl.BlockSpec((None, POOL_G, POOL_G), lambda g: (g, 0, 0))
    vec = pl.BlockSpec((1, POOL_G), lambda g: (0, g))
    return _call(
        body, [z, dy, pool_mix, pool_scale], [slab, slab, mix_spec, vec], after, name=name,
        grid=(len(POOL_WINDOWS),), out_specs=(slab, mix_spec, vec),
        out_shape=(jax.ShapeDtypeStruct((S_, POOL_W), BF16), jax.ShapeDtypeStruct((4, POOL_G, POOL_G), F32),
                   jax.ShapeDtypeStruct((1, POOL_W), F32)),
        compiler_params=_cparams(("parallel",)))


def _conv_specs(S_):
    slab = lambda off: pl.BlockSpec((S_, 128), lambda c, off=off: (0, off // 128 + c))
    return slab(OFF_CB), slab(OFF_CC), slab(OFF_CX)


def _conv_fwd(name, z, conv_w):
    S_ = z.shape[0]
    n_chunks = S_ // SEQ_CHUNK
    col = slice(0, 128)

    def body(b_ref, c_ref, x_ref, w_ref, y_ref):
        w0, w1, w2 = w_ref[0:1, :], w_ref[1:2, :], w_ref[2:3, :]

        def chunk(i, carry):
            r0 = pl.multiple_of(i * SEQ_CHUNK, SEQ_CHUNK)
            u = _rows_with_halo(c_ref, col, i, n_chunks, True, False) * _rows_with_halo(x_ref, col, i, n_chunks, True, False)
            y = w2 * u + w1 * _shift_down(u, 1) + w0 * _shift_down(u, 2)
            y_ref[pl.ds(r0, SEQ_CHUNK), :] = (b_ref[pl.ds(r0, SEQ_CHUNK), :] * y[HALO:]).astype(BF16)
            return carry

        lax.fori_loop(0, n_chunks, chunk, 0)

    sb, sc, sx = _conv_specs(S_)
    return pl.pallas_call(
        body, name=name, grid=(CONV_W // 128,),
        in_specs=[sb, sc, sx, pl.BlockSpec((3, 128), lambda c: (0, c))],
        out_specs=pl.BlockSpec((S_, 128), lambda c: (0, c)),
        out_shape=jax.ShapeDtypeStruct((S_, CONV_W), BF16),
        compiler_params=_cparams(("parallel",)),
    )(z, z, z, conv_w)


def _conv_bwd(name, z, dy, conv_w, after=None):
    S_ = z.shape[0]
    n_chunks = S_ // SEQ_CHUNK
    col = slice(0, 128)
    lo, hi = HALO, HALO + SEQ_CHUNK

    def body(b_ref, c_ref, x_ref, dy_ref, w_ref, db_ref, dc_ref, dx_ref, dw_ref):
        w0, w1, w2 = w_ref[0:1, :], w_ref[1:2, :], w_ref[2:3, :]

        def chunk(i, carry):
            a0, a1, a2 = carry
            r0 = pl.multiple_of(i * SEQ_CHUNK, SEQ_CHUNK)
            cc = _rows_with_halo(c_ref, col, i, n_chunks, True, True)
            xx = _rows_with_halo(x_ref, col, i, n_chunks, True, True)
            bb = _rows_with_halo(b_ref, col, i, n_chunks, True, True)
            dyy = _rows_with_halo(dy_ref, col, i, n_chunks, True, True)
            u = cc * xx
            u1 = _shift_down(u, 1)
            u2 = _shift_down(u, 2)
            y = w2 * u + w1 * u1 + w0 * u2
            dyv = dyy * bb
            du = w2 * dyv + w1 * _shift_up(dyv, 1) + w0 * _shift_up(dyv, 2)
            db_ref[pl.ds(r0, SEQ_CHUNK), :] = (dyy[lo:hi] * y[lo:hi]).astype(BF16)
            dc_ref[pl.ds(r0, SEQ_CHUNK), :] = (du[lo:hi] * xx[lo:hi]).astype(BF16)
            dx_ref[pl.ds(r0, SEQ_CHUNK), :] = (du[lo:hi] * cc[lo:hi]).astype(BF16)
            red = lambda v: jnp.sum(v.reshape(SEQ_CHUNK // 8, 8, 128), axis=0)
            dm = dyv[lo:hi]
            return a0 + red(dm * u2[lo:hi]), a1 + red(dm * u1[lo:hi]), a2 + red(dm * u[lo:hi])

        zero = jnp.zeros((8, 128), F32)
        a0, a1, a2 = lax.fori_loop(0, n_chunks, chunk, (zero, zero, zero))
        dw_ref[0:1, :] = jnp.sum(a0, axis=0, keepdims=True)
        dw_ref[1:2, :] = jnp.sum(a1, axis=0, keepdims=True)
        dw_ref[2:3, :] = jnp.sum(a2, axis=0, keepdims=True)

    sb, sc, sx = _conv_specs(S_)
    slab = pl.BlockSpec((S_, 128), lambda c: (0, c))
    wspec = pl.BlockSpec((3, 128), lambda c: (0, c))
    act = jax.ShapeDtypeStruct((S_, CONV_W), BF16)
    return _call(
        body, [z, z, z, dy, conv_w], [sb, sc, sx, slab, wspec], after, name=name, grid=(CONV_W // 128,),
        out_specs=(slab, slab, slab, wspec),
        out_shape=(act, act, act, jax.ShapeDtypeStruct((3, CONV_W), F32)),
        compiler_params=_cparams(("parallel",)))


def _head_ones(pw):
    a = lax.broadcasted_iota(jnp.int32, (pw, pw), 0) // HEAD
    b = lax.broadcasted_iota(jnp.int32, (pw, pw), 1) // HEAD
    return (a == b).astype(BF16)


def _head_sum(v, ones):
    hi = v.astype(BF16)
    lo = (v - hi.astype(F32)).astype(BF16)
    return jnp.dot(hi, ones, preferred_element_type=F32) + jnp.dot(lo, ones, preferred_element_type=F32)


def _head_norm(v, gain, ones):
    rstd = lax.rsqrt(_head_sum(v * v, ones) * (1.0 / HEAD) + EPS)
    xhat = v * rstd
    return xhat * gain, xhat, rstd


def _head_norm_bwd(dy, xhat, rstd, gain, ones):
    dxhat = dy * gain
    c = _head_sum(dxhat * xhat, ones) * (1.0 / HEAD)
    dv = rstd * (dxhat - xhat * c)
    dg = jnp.sum((dy * xhat).reshape(dy.shape[0] // 8, 8, dy.shape[1]), axis=0)
    return dv, dg


def _head_masks(pw):
    lane_head = lax.broadcasted_iota(jnp.int32, (1, pw), 1) // HEAD
    return [lane_head == h for h in range(pw // HEAD)]


def _only(mask, v):
    return jnp.where(mask, v, jnp.zeros_like(v))


def _attn_specs(S_, g, dil):
    rows = ATTN_BLK * dil
    nb = S_ // rows
    pw = 128 if dil > 1 else ATTN_OUT
    cq, ck, cv = ((OFF_Q + g * ATTN_OUT) // pw, (OFF_K + g * ATTN_OUT) // pw, (OFF_V + g * ATTN_OUT) // pw)
    return rows, nb, pw, pw // HEAD, ATTN_OUT // pw, cq, ck, cv


ATTN_BATCH = 4


def _attn_group(dil):
    return 4 if dil == 1 else 1


def _attn_block_specs(rows, grp, pw, last=None):
    step = (lambda n: n) if last is None else (lambda n: jnp.minimum(n, last))
    cur = lambda c: pl.BlockSpec((rows * grp, pw), lambda hp, n, c=c: (step(n), c + hp))
    prev = lambda c: pl.BlockSpec((rows, pw), lambda hp, n, c=c: (jnp.maximum(step(n) * grp - 1, 0), c + hp))
    return cur, prev


def _band_mask(has_prev):
    qi = lax.broadcasted_iota(jnp.int32, (ATTN_BLK, 2 * ATTN_BLK), 0)
    ki = lax.broadcasted_iota(jnp.int32, (ATTN_BLK, 2 * ATTN_BLK), 1)
    in_prev = jnp.logical_and(ki < ATTN_BLK, ki >= qi)
    if has_prev is not True:
        in_prev = jnp.logical_and(in_prev, has_prev)
    return jnp.logical_or(in_prev, jnp.logical_and(ki >= ATTN_BLK, ki - ATTN_BLK <= qi))


def _rows_of(ref, r, dil):
    if dil == 1:
        return ref[r * ATTN_BLK:(r + 1) * ATTN_BLK, :]
    return ref[pl.ds(r, ATTN_BLK, stride=dil), :]


def _put_rows(ref, r, dil, val):
    if dil == 1:
        ref[r * ATTN_BLK:(r + 1) * ATTN_BLK, :] = val.astype(ref.dtype)
    else:
        ref[pl.ds(r, ATTN_BLK, stride=dil), :] = val.astype(ref.dtype)


def _attn_fwd(name, z, q_gain, k_gain, g, dil):
    S_ = z.shape[0]
    rows, nb, pw, heads, npairs, cq, ck, cv = _attn_specs(S_, g, dil)
    scale = HEAD ** -0.5

    grp = _attn_group(dil)
    nsteps = nb // grp

    def body(q_ref, k_ref, kp_ref, v_ref, vp_ref, gq_ref, gk_ref, o_ref, l_ref):
        n = pl.program_id(1)
        ones, hmask = _head_ones(pw), _head_masks(pw)
        gq, gk = jnp.tile(gq_ref[...], (1, heads)), jnp.tile(gk_ref[...], (1, heads))
        mask_first, mask_rest = _band_mask(n > 0), _band_mask(True)
        for r0 in range(0, dil * grp, ATTN_BATCH):
            rs = range(r0, min(r0 + ATTN_BATCH, dil * grp))
            qn, kn, vv, s, p = {}, {}, {}, {}, {}
            kcn = {}
            for r in rs:
                q, kc, vc = _rows_of(q_ref, r, dil), _rows_of(k_ref, r, dil), _rows_of(v_ref, r, dil)
                kcn[r] = _head_norm(kc, gk, ones)[0]
                if dil == 1 and r > 0:
                    kpn = kcn[r - 1] if r - 1 in kcn else _head_norm(_rows_of(k_ref, r - 1, dil), gk, ones)[0]
                    vp = _rows_of(v_ref, r - 1, dil)
                else:
                    kpn, vp = _head_norm(_rows_of(kp_ref, r, dil), gk, ones)[0], _rows_of(vp_ref, r, dil)
                qn[r] = _head_norm(q, gq, ones)[0].astype(BF16)
                kn[r] = jnp.concatenate([kpn, kcn[r]], axis=0).astype(BF16)
                vv[r] = jnp.concatenate([vp, vc], axis=0).astype(BF16)
            keys = [(r, h) for r in rs for h in range(heads)]
            for r, h in keys:
                s[r, h] = lax.dot_general(_only(hmask[h], qn[r]), kn[r], _DIMS["nt"],
                                          preferred_element_type=F32) * scale
            lse, den = {}, {}
            for key in keys:
                mask = mask_rest if (dil == 1 and key[0] > 0) else mask_first
                sm = jnp.where(mask, s[key], MASK)
                m = jnp.max(sm, axis=-1, keepdims=True)
                e = jnp.exp(sm - m)
                den[key] = jnp.sum(e, axis=-1, keepdims=True)
                p[key] = e.astype(BF16)
                lse[key] = m + jnp.log(den[key])
            for r in rs:
                out = jnp.zeros((ATTN_BLK, pw), F32)
                lse_all = jnp.zeros((ATTN_BLK, pw), F32)
                for h in range(heads):
                    out = jnp.where(hmask[h], jnp.dot(p[r, h], vv[r], preferred_element_type=F32) / den[r, h], out)
                    lse_all = jnp.where(hmask[h], lse[r, h], lse_all)
                _put_rows(o_ref, r, dil, out)
                _put_rows(l_ref, r, dil, lse_all)

    cur, prev = _attn_block_specs(rows, grp, pw)
    gspec = pl.BlockSpec((1, HEAD), lambda hp, n: (0, 0))
    shp = jax.ShapeDtypeStruct((S_, ATTN_OUT), F32)
    return pl.pallas_call(
        body, name=name, grid=(npairs, nsteps),
        in_specs=[cur(cq), cur(ck), prev(ck), cur(cv), prev(cv), gspec, gspec],
        out_specs=(cur(0), cur(0)), out_shape=(shp, shp),
        compiler_params=_cparams(("parallel", "parallel")),
    )(z, z, z, z, z, q_gain, k_gain)


def _attn_combine(name, os_, ls_):
    S_ = os_[0].shape[0]

    def body(o0, o1, o2, l0, l1, l2, o_ref, l_ref):
        a, b, c = l0[...], l1[...], l2[...]
        m = jnp.maximum(jnp.maximum(a, b), c)
        ea, eb, ec = jnp.exp(a - m), jnp.exp(b - m), jnp.exp(c - m)
        zsum = ea + eb + ec
        o_ref[...] = (ea * o0[...] + eb * o1[...] + ec * o2[...]) / zsum
        l_ref[...] = m + jnp.log(zsum)

    row = pl.BlockSpec((ROW_TILE, ATTN_OUT), lambda i: (i, 0))
    shp = jax.ShapeDtypeStruct((S_, ATTN_OUT), F32)
    return pl.pallas_call(
        body, name=name, grid=(S_ // ROW_TILE,), in_specs=[row] * 6, out_specs=(row, row), out_shape=(shp, shp),
        compiler_params=_cparams(("parallel",)),
    )(*os_, *ls_)


def _attn_bwd(name, z, q_gain, k_gain, do, o, lse, g, dil, after=None):
    S_ = z.shape[0]
    rows, nb, pw, heads, npairs, cq, ck, cv = _attn_specs(S_, g, dil)
    scale = HEAD ** -0.5

    def body(q_ref, kc_ref, kp_ref, vc_ref, vp_ref, gq_ref, gk_ref, do_ref, o_ref, l_ref,
             dq_ref, dk_ref, dv_ref, dgq_ref, dgk_ref, ck_ref, cvv_ref, gq_acc, gk_acc):
        hp = pl.program_id(0)
        n = pl.program_id(1)
        live = n < nb
        mask = jnp.logical_and(_band_mask(n > 0), live)
        ones, hmask = _head_ones(pw), _head_masks(pw)
        gq, gk = jnp.tile(gq_ref[...], (1, heads)), jnp.tile(gk_ref[...], (1, heads))

        @pl.when(n == 0)
        def _():
            ck_ref[...] = jnp.zeros_like(ck_ref)
            cvv_ref[...] = jnp.zeros_like(cvv_ref)

        @pl.when(jnp.logical_and(n == 0, hp == 0))
        def _():
            gq_acc[...] = jnp.zeros_like(gq_acc)
            gk_acc[...] = jnp.zeros_like(gk_acc)

        dgq = jnp.zeros((8, pw), F32)
        dgk = jnp.zeros((8, pw), F32)
        for r0 in range(0, dil, ATTN_BATCH):
            rs = range(r0, min(r0 + ATTN_BATCH, dil))
            keys = [(r, h) for r in rs for h in range(heads)]
            qn, qhat, qrstd, kn, kphat, kprstd, vv, dob, delta, lse = ({} for _ in range(10))
            for r in rs:
                q, kc, kp = _rows_of(q_ref, r, dil), _rows_of(kc_ref, r, dil), _rows_of(kp_ref, r, dil)
                dov = _rows_of(do_ref, r, dil)
                qn_f, qhat[r], qrstd[r] = _head_norm(q, gq, ones)
                kpn, kphat[r], kprstd[r] = _head_norm(kp, gk, ones)
                qn[r] = qn_f.astype(BF16)
                kn[r] = jnp.concatenate([kpn, _head_norm(kc, gk, ones)[0]], axis=0).astype(BF16)
                vv[r] = jnp.concatenate([_rows_of(vp_ref, r, dil), _rows_of(vc_ref, r, dil)], axis=0).astype(BF16)
                dob[r] = dov.astype(BF16)
                delta[r] = _head_sum(dov * _rows_of(o_ref, r, dil), ones)
                lse[r] = _rows_of(l_ref, r, dil)
            s, dp = {}, {}
            for r, h in keys:
                s[r, h] = lax.dot_general(_only(hmask[h], qn[r]), kn[r], _DIMS["nt"],
                                          preferred_element_type=F32) * scale
                dp[r, h] = lax.dot_general(_only(hmask[h], dob[r]), vv[r], _DIMS["nt"], preferred_element_type=F32)
            p, ds = {}, {}
            for r, h in keys:
                col = slice(h * HEAD, h * HEAD + 1)
                pk = jnp.where(mask, jnp.exp(jnp.where(mask, s[r, h], MASK) - lse[r][:, col]), 0.0)
                ds[r, h] = (pk * (dp[r, h] - delta[r][:, col]) * scale).astype(BF16)
                p[r, h] = pk.astype(BF16)
            dqn, dkn, dvv = {}, {}, {}
            for r in rs:
                dqn[r] = jnp.zeros((ATTN_BLK, pw), F32)
                dkn[r] = jnp.zeros((2 * ATTN_BLK, pw), F32)
                dvv[r] = jnp.zeros((2 * ATTN_BLK, pw), F32)
                for h in range(heads):
                    dqn[r] = jnp.where(hmask[h], jnp.dot(ds[r, h], kn[r], preferred_element_type=F32), dqn[r])
                    dkn[r] = jnp.where(hmask[h], lax.dot_general(ds[r, h], qn[r], _DIMS["tn"],
                                                                 preferred_element_type=F32), dkn[r])
                    dvv[r] = jnp.where(hmask[h], lax.dot_general(p[r, h], dob[r], _DIMS["tn"],
                                                                 preferred_element_type=F32), dvv[r])
            for r in rs:
                dq_all, dg = _head_norm_bwd(dqn[r], qhat[r], qrstd[r], gq, ones)
                dgq = dgq + dg
                dk_all, dg = _head_norm_bwd(_rows_of(ck_ref, r, dil) + dkn[r][:ATTN_BLK], kphat[r], kprstd[r], gk, ones)
                dgk = dgk + dg
                dv_all = _rows_of(cvv_ref, r, dil) + dvv[r][:ATTN_BLK]

                @pl.when(live)
                def _(dq_all=dq_all, r=r):
                    _put_rows(dq_ref, r, dil, dq_all)

                _put_rows(dk_ref, r, dil, dk_all)
                _put_rows(dv_ref, r, dil, dv_all)
                _put_rows(ck_ref, r, dil, dkn[r][ATTN_BLK:])
                _put_rows(cvv_ref, r, dil, dvv[r][ATTN_BLK:])
        gq_acc[...] += dgq
        gk_acc[...] += dgk

        @pl.when(jnp.logical_and(n == nb, hp == npairs - 1))
        def _():
            fold = lambda a: sum(a[:, h * HEAD:(h + 1) * HEAD] for h in range(heads))
            dgq_ref[...] = jnp.sum(fold(gq_acc[...]), axis=0, keepdims=True)
            dgk_ref[...] = jnp.sum(fold(gk_acc[...]), axis=0, keepdims=True)

    last = nb - 1
    cur = lambda c: pl.BlockSpec((rows, pw), lambda hp, n, c=c: (jnp.minimum(n, last), c + hp))
    prev = lambda c: pl.BlockSpec((rows, pw), lambda hp, n, c=c: (jnp.maximum(n - 1, 0), c + hp))
    gspec = pl.BlockSpec((1, HEAD), lambda hp, n: (0, 0))
    act = jax.ShapeDtypeStruct((S_, ATTN_OUT), F32)
    vec = jax.ShapeDtypeStruct((1, HEAD), F32)
    return _call(
        body, [z, z, z, z, z, q_gain, k_gain, do, o, lse],
        [cur(cq), cur(ck), prev(ck), cur(cv), prev(cv), gspec, gspec, cur(0), cur(0), cur(0)], after,
        name=name, grid=(npairs, nb + 1),
        out_specs=(cur(0), prev(0), prev(0), gspec, gspec),
        out_shape=(act, act, act, vec, vec),
        scratch_shapes=[pltpu.VMEM((rows, pw), F32), pltpu.VMEM((rows, pw), F32),
                        pltpu.VMEM((8, pw), F32), pltpu.VMEM((8, pw), F32)],
        compiler_params=_cparams(("arbitrary", "arbitrary")))


MIX_TN = 256
MIX_TM = 2048


def _sigmoid(v):
    return 1.0 / (1.0 + jnp.exp(-v))


def _mix_fwd(name, z, b_gate, ys, ws):
    S_ = z.shape[0]
    tm, tn = MIX_TM, MIX_TN
    gblk = OFF_GATE // tn

    def body(yp, yc, ya, wp, wc, wa, g0, g1, g2, b0, b1, b2, m_ref):
        acc = None
        for y_ref, w_ref, g_ref, b_ref in ((yp, wp, g0, b0), (yc, wc, g1, b1), (ya, wa, g2, b2)):
            u = lax.dot_general(y_ref[...].astype(BF16), w_ref[...], _DIMS["nt"], preferred_element_type=F32)
            t = _sigmoid(g_ref[...] + b_ref[...]) * u
            acc = t if acc is None else acc + t
        m_ref[...] = acc.astype(BF16)

    yspec = lambda w: pl.BlockSpec((tm, w), lambda i, j: (i, 0))
    wspec = lambda w: pl.BlockSpec((tn, w), lambda i, j: (j, 0))
    gspec = lambda b: pl.BlockSpec((tm, tn), lambda i, j, b=b: (i, gblk + b * (D // tn) + j))
    bspec = lambda b: pl.BlockSpec((1, tn), lambda i, j, b=b: (0, b * (D // tn) + j))
    return pl.pallas_call(
        body, name=name, grid=(S_ // tm, D // tn),
        in_specs=[yspec(POOL_W), yspec(CONV_W), yspec(ATTN_OUT), wspec(POOL_W), wspec(CONV_W), wspec(ATTN_OUT),
                  gspec(0), gspec(1), gspec(2), bspec(0), bspec(1), bspec(2)],
        out_specs=pl.BlockSpec((tm, tn), lambda i, j: (i, j)),
        out_shape=jax.ShapeDtypeStruct((S_, D), BF16),
        compiler_params=_cparams(("parallel", "parallel")),
    )(*ys, *ws, z, z, z, b_gate, b_gate, b_gate)


def _mix_bwd(name, z, b_gate, y, w, dmerged, branch, after=None):
    S_ = z.shape[0]
    tm, tn = MIX_TM, MIX_TN
    width = y.shape[1]
    gblk = OFF_GATE // tn + branch * (D // tn)
    ni = S_ // tm

    def body(y_ref, w_ref, g_ref, b_ref, dm_ref, du_ref, dg_ref, db_ref, acc_ref):
        i = pl.program_id(1)
        u = lax.dot_general(y_ref[...].astype(BF16), w_ref[...], _DIMS["nt"], preferred_element_type=F32)
        sg = _sigmoid(g_ref[...] + b_ref[...])
        dm = dm_ref[...]
        du_ref[...] = (sg * dm).astype(BF16)
        dpre = dm * u * sg * (1.0 - sg)
        dg_ref[...] = dpre.astype(BF16)
        part = jnp.sum(dpre.reshape(tm // 8, 8, tn), axis=0)

        @pl.when(i == 0)
        def _():
            acc_ref[...] = part

        @pl.when(i > 0)
        def _():
            acc_ref[...] += part

        @pl.when(i == ni - 1)
        def _():
            db_ref[...] = jnp.sum(acc_ref[...], axis=0, keepdims=True)

    blk = pl.BlockSpec((tm, tn), lambda j, i: (i, j))
    vec = pl.BlockSpec((1, tn), lambda j, i: (0, j))
    act = jax.ShapeDtypeStruct((S_, D), BF16)
    return _call(
        body, [y, w, z, b_gate, dmerged],
        [pl.BlockSpec((tm, width), lambda j, i: (i, 0)), pl.BlockSpec((tn, width), lambda j, i: (j, 0)),
         pl.BlockSpec((tm, tn), lambda j, i: (i, gblk + j)),
         pl.BlockSpec((1, tn), lambda j, i: (0, branch * (D // tn) + j)), blk], after,
        name=name, grid=(D // tn, ni),
        out_specs=(blk, blk, vec), out_shape=(act, act, jax.ShapeDtypeStruct((1, D), F32)),
        scratch_shapes=[pltpu.VMEM((8, tn), F32)],
        compiler_params=_cparams(("parallel", "arbitrary")))


def _relu2_epilogue(acc):
    r = jnp.maximum(acc, 0.0)
    return acc, r * r


def _residual_norm_epilogue(acc, xr, gain):
    x = xr + acc
    ms = jnp.mean(x * x, axis=-1, keepdims=True)
    return x, x * lax.rsqrt(ms + EPS) * gain


def _layer_fwd(l, x, h, p, next_gain):
    t = f"l{l}_"
    z = _matmul(t + "in_proj", h, p["w_in_t"], "nt", 512, 3712, 1024)
    y_pool = _pool_fwd(t + "pool", z, p["pool_mix"], p["pool_scale"])
    y_conv = _conv_fwd(t + "conv", z, p["conv_w"])
    os_, ls_ = [], []
    for g, (_, dil) in enumerate(ATTN_GROUPS):
        o_g, l_g = _attn_fwd(t + f"attn{g}", z, p["q_gain"], p["k_gain"], g, dil)
        os_.append(o_g)
        ls_.append(l_g)
    y_attn, lse = _attn_combine(t + "attn_mix", os_, ls_)
    merged = _mix_fwd(t + "merge", z, p["b_gate"], (y_pool, y_conv, y_attn),
                      (p["w_pool_up_t"], p["w_conv_out_t"], p["w_attn_up_t"]))
    x1, h2 = _matmul(t + "out_proj", merged, p["w_o"], "nn", 1024, D, 1024, out_dtypes=(F32, BF16),
                     extras=((x, "mn"), (p["norm_mlp"], "n")), epilogue=_residual_norm_epilogue)
    a, r = _matmul(t + "ff1", h2, p["w_ff1_t"], "nt", 1024, 1024, 1024, out_dtypes=(F32, BF16),
                   epilogue=_relu2_epilogue)
    if next_gain is None:
        x2 = _matmul(t + "ff2", r, p["w_ff2"], "nn", 1024, D, 1024,
                     extras=((x1, "mn"),), epilogue=lambda acc, xr: (xr + acc,))
        h_out = None
    else:
        x2, h_out = _matmul(t + "ff2", r, p["w_ff2"], "nn", 1024, D, 1024, out_dtypes=(F32, BF16),
                            extras=((x1, "mn"), (next_gain, "n")), epilogue=_residual_norm_epilogue)
    saved = dict(x=x, h=h, z=z, y_pool=y_pool, y_conv=y_conv, y_attn=y_attn, lse=lse, merged=merged,
                 x1=x1, h2=h2, a=a, r=r)
    return x2, h_out, saved


def _layer_bwd(l, dx2, p, s, pending, collective_id, last, start_after=None):
    t = f"l{l}_b_"
    g = {}
    rest = jax.ShapeDtypeStruct((N_DEV, REST_ROWS, D), F32)
    da = _matmul(t + "d_ff2_in", dx2, p["w_ff2"], "nt", 1024, 1024, 1024, out_dtypes=(BF16,),
                 extras=((s["a"], "mn"),), epilogue=lambda acc, a: (acc * (2.0 * jnp.maximum(a, 0.0)),),
                 after=start_after)
    rest = _matmul(t + "dw_ff2", s["r"], dx2, "tn", 512, 1024, 4096, into=(rest, REST_FF2, 0), after=da)
    tok = rest if pending is None else pending.chip_sums(after=rest)
    dh2 = _matmul(t + "d_ff1_in", da, p["w_ff1_t"], "nn", 1024, 1024, 1024, after=tok)
    rest = _matmul(t + "dw_ff1", da, s["h2"], "tn", 512, 1024, 4096, into=(rest, REST_FF1, 0), after=dh2)
    dx1, g["norm_mlp"] = _rmsnorm_bwd(t + "norm_mlp", s["x1"], p["norm_mlp"], dh2, dx2, after=rest)
    dmerged = _matmul(t + "d_out_proj_in", dx1, p["w_o"], "nt", 1024, 1024, 1024)
    rest = _matmul(t + "dw_o", s["merged"], dx1, "tn", 128, 1024, 4096, into=(rest, REST_WO, 0), after=dmerged)
    ys = (s["y_pool"], s["y_conv"], s["y_attn"])
    names = ("w_pool_up_t", "w_conv_out_t", "w_attn_up_t")
    dys, dgates, dbs = [], [], []
    tok = rest
    for b in range(3):
        du, dgz, db = _mix_bwd(t + f"merge{b}", s["z"], p["b_gate"], ys[b], p[names[b]], dmerged, b, after=tok)
        width = ys[b].shape[1]
        dys.append(_matmul(t + f"d_up{b}_in", du, p[names[b]], "nn", 512, width, 1024))
        if b < 2:
            rest = _matmul(t + f"dw_up{b}", du, ys[b], "tn", 128, width, 4096, into=(rest, REST_UP, b * width),
                           after=dys[b])
            tok = rest
        else:
            d_attn_up_t = _matmul(t + f"dw_up{b}", du, ys[b], "tn", 128, width, 4096, after=dys[b])
            rest = rest.at[:, REST_ATTN:REST_ROWS, :].set(d_attn_up_t.reshape(N_DEV, REST_ROWS - REST_ATTN, D))
        dgates.append(dgz)
        dbs.append(db)
    g["b_gate"] = jnp.concatenate(dbs, axis=1)
    rs_rest = _ReduceScatter(f"rs_rest{l}", collective_id, rest)
    pending_sum = None if pending is None else pending.finish(after=rest)
    du_pool, g["pool_mix"], g["pool_scale"] = _pool_bwd(t + "pool", s["z"], dys[0], p["pool_mix"], p["pool_scale"],
                                                        after=rest if pending is None else pending_sum[0])
    dcb, dcc, dcx, g["conv_w"] = _conv_bwd(t + "conv", s["z"], dys[1], p["conv_w"], after=du_pool)
    tok = rs_rest.chip_sums(after=dcb)
    dqs, dks, dvs = [], [], []
    gq = gk = None
    for gi, (_, dil) in enumerate(ATTN_GROUPS):
        dq, dk, dv, dgq, dgk = _attn_bwd(t + f"attn{gi}", s["z"], p["q_gain"], p["k_gain"], dys[2], s["y_attn"],
                                         s["lse"], gi, dil, after=tok)
        tok = dq
        dqs.append(dq)
        dks.append(dk)
        dvs.append(dv)
        gq = dgq if gq is None else gq + dgq
        gk = dgk if gk is None else gk + dgk
    g["q_gain"], g["k_gain"] = gq, gk
    rest_sum, _ = rs_rest.finish(after=tok)
    dz = jnp.concatenate([du_pool, dcb, dcc, dcx] + [v.astype(BF16) for v in dqs + dks + dvs] + dgates, axis=1)
    in_t = _matmul(t + "dw_in", dz, s["h"], "tn", 256, 1024, 4096, after=rest_sum)
    rs_in = _ReduceScatter(f"rs_in{l}", collective_id + 2, in_t.reshape(N_DEV, IN_ROWS, D))
    dh = _matmul(t + "d_in_proj_in", dz, p["w_in_t"], "nn", 512, 1024, 3712, after=in_t)
    tok = rs_in.chip_sums(after=dh) if last else dh
    dx, g["norm_mix"] = _rmsnorm_bwd(t + "norm_mix", s["x"], p["norm_mix"], dh, dx1, after=tok)
    return dx, g, rest_sum, pending_sum, rs_in


ANY = pl.BlockSpec(memory_space=pl.ANY)


def _mesh_pos():
    return lax.axis_index("x"), lax.axis_index("y"), lax.axis_index("c")


def _other_chips(x, y):
    return [(1 - x, y), (x, 1 - y), (1 - x, 1 - y)]


def _comm_call(name, collective_id, peers, body, arrs, out_shape, sem_counts):
    n_in, n_out = len(arrs), len(out_shape)
    if collective_id is None:
        def tc_body(*refs):
            body(refs[:n_in], refs[n_in:n_in + n_out], *refs[n_in + n_out:])

        return pl.pallas_call(
            tc_body, name=name, out_shape=tuple(out_shape), in_specs=[ANY] * n_in, out_specs=(ANY,) * n_out,
            scratch_shapes=[pltpu.SemaphoreType.DMA((n,)) for n in sem_counts])(*arrs)

    def seq_body(*refs):
        barrier = pltpu.get_barrier_semaphore()
        ps = peers(*_mesh_pos())
        for p in ps:
            pl.semaphore_signal(barrier, inc=1, device_id=p, device_id_type=MESH)
        pl.semaphore_wait(barrier, len(ps))
        body(refs[:n_in], refs[n_in:n_in + n_out], *refs[n_in + n_out:])

    return pl.kernel(
        seq_body, out_type=tuple(out_shape), mesh=plsc.ScalarSubcoreMesh(axis_name="seq", num_cores=1), name=name,
        scratch_types=[pltpu.SemaphoreType.DMA((n,)) for n in sem_counts],
        compiler_params=pltpu.CompilerParams(collective_id=collective_id),
    )(*arrs)


def _all_gather(name, collective_id, shard):
    R, C = shard.shape

    def peers(x, y, c):
        return [(x, y, 1 - c)] + [(*chip, c) for chip in _other_chips(x, y)]

    def body(in_refs, out_refs, send_sems, recv_sems, local_sems):
        (x_ref,), (out_ref,) = in_refs, out_refs
        x, y, c = _mesh_pos()
        me, sibling = (x, y, c), (x, y, 1 - c)
        chips = _other_chips(x, y)

        def slot(px, py, pc):
            return out_ref.at[4 * px + 2 * py + pc]

        def copy(k, block, to, src=None):
            return pltpu.make_async_remote_copy(
                src_ref=slot(*block) if src is None else src, dst_ref=slot(*block),
                send_sem=send_sems.at[k], recv_sem=recv_sems.at[k], device_id=to, device_id_type=MESH)

        mine = pltpu.make_async_copy(x_ref, slot(*me), local_sems.at[0])
        mine.start()
        first = [copy(0, me, sibling, src=x_ref)]
        first += [copy(1 + j, me, (*chip, c), src=x_ref) for j, chip in enumerate(chips)]
        for cp in first:
            cp.start()
        passed = [copy(4 + j, (*chip, c), sibling) for j, chip in enumerate(chips)]
        for j, chip in enumerate(chips):
            copy(1 + j, (*chip, c), me).wait_recv()
            passed[j].start()
        copy(0, sibling, me).wait_recv()
        for j, chip in enumerate(chips):
            copy(4 + j, (*chip, 1 - c), me).wait_recv()
        for cp in first + passed:
            cp.wait_send()
        mine.wait()

    return _comm_call(name, collective_id, peers, body, [shard],
                      [jax.ShapeDtypeStruct((N_DEV, R, C), shard.dtype)], (7, 7, 1))[0]


def _rs_sibling_exchange(name, collective_id, arrs):
    n = len(arrs)

    def body(in_refs, out_refs, send_sems, recv_sems):
        x, y, c = _mesh_pos()
        cps = []
        for k, (src, dst) in enumerate(zip(in_refs, out_refs)):
            src = src.at[:, 1 - c] if len(src.shape) == 4 else src
            cps.append(pltpu.make_async_remote_copy(src_ref=src, dst_ref=dst, send_sem=send_sems.at[k],
                                                    recv_sem=recv_sems.at[k], device_id=(x, y, 1 - c),
                                                    device_id_type=MESH))
        for cp in cps:
            cp.start()
        for cp in cps:
            cp.wait()

    out_shape = [jax.ShapeDtypeStruct(a.shape[:1] + a.shape[2:] if a.ndim == 4 else a.shape, a.dtype) for a in arrs]
    return _comm_call(name, collective_id, lambda x, y, c: [(x, y, 1 - c)], body, arrs, out_shape, (n, n))


def _rs_rows(r):
    return r // 2 if (r // 2) % 16 == 0 else r


def _drop_ref(body, idx):
    def wrapped(*refs):
        return body(*refs[:idx], *refs[idx + 1:])

    return wrapped


def _rs_chip_sum(name, ids, big, rbig, after=None):
    _, _, R, C = big.shape
    rows = _rs_rows(R)

    def body(ids_ref, mine_ref, theirs_ref, t16_ref, own_ref):
        p = pl.program_id(1)
        s = mine_ref[...] + theirs_ref[...]
        t16_ref[...] = s.astype(BF16)

        @pl.when(p == ids_ref[1])
        def _():
            own_ref[...] = s

    in_specs = [pl.BlockSpec((None, None, rows, C), lambda i, p, ids: (p, ids[0], i, 0)),
                pl.BlockSpec((None, rows, C), lambda i, p, ids: (p, i, 0))]
    args = [ids, big, rbig]
    if after is not None:
        body = _drop_ref(body, len(args))
        in_specs.append(ANY)
        args.append(after)
    grid_spec = pltpu.PrefetchScalarGridSpec(
        num_scalar_prefetch=1, grid=(R // rows, 4), in_specs=in_specs,
        out_specs=(pl.BlockSpec((None, rows, C), lambda i, p, ids: (p, i, 0)),
                   pl.BlockSpec((rows, C), lambda i, p, ids: (i, 0))))
    return pl.pallas_call(
        body, name=name, grid_spec=grid_spec,
        out_shape=(jax.ShapeDtypeStruct((4, R, C), BF16), jax.ShapeDtypeStruct((R, C), F32)),
        compiler_params=_cparams(("parallel", "arbitrary")),
    )(*args)


def _add2(name, a, b):
    def body(a_ref, b_ref, o_ref):
        o_ref[...] = a_ref[...] + b_ref[...]

    full = pl.BlockSpec(a.shape, lambda i: (0, 0))
    return pl.pallas_call(body, name=name, grid=(1,), in_specs=[full, full], out_specs=full,
                          out_shape=jax.ShapeDtypeStruct(a.shape, a.dtype))(a, b)


def _rs_chip_exchange(name, collective_id, arrs):
    n = len(arrs)

    def body(in_refs, out_refs, send_sems, recv_sems, local_sems):
        x, y, c = _mesh_pos()
        p_me = 2 * x + y
        chips = _other_chips(x, y)

        def part(ref, p):
            return ref.at[p] if len(ref.shape) == 3 else ref

        local = [pltpu.make_async_copy(part(src, p_me), dst.at[p_me], local_sems.at[k])
                 for k, (src, dst) in enumerate(zip(in_refs, out_refs))]
        for cp in local:
            cp.start()
        sends = []
        for j, (px, py) in enumerate(chips):
            for k, (src, dst) in enumerate(zip(in_refs, out_refs)):
                sends.append(pltpu.make_async_remote_copy(
                    src_ref=part(src, 2 * px + py), dst_ref=dst.at[p_me], send_sem=send_sems.at[n * j + k],
                    recv_sem=recv_sems.at[n * j + k], device_id=(px, py, c), device_id_type=MESH))
        for cp in sends:
            cp.start()
        for j, (px, py) in enumerate(chips):
            for k, (src, dst) in enumerate(zip(in_refs, out_refs)):
                pltpu.make_async_remote_copy(
                    src_ref=part(src, p_me), dst_ref=dst.at[2 * px + py], send_sem=send_sems.at[n * j + k],
                    recv_sem=recv_sems.at[n * j + k], device_id=(px, py, c), device_id_type=MESH).wait_recv()
        for cp in sends:
            cp.wait_send()
        for cp in local:
            cp.wait()

    out_shape = [jax.ShapeDtypeStruct((4,) + a.shape[-2:], a.dtype) for a in arrs]
    return _comm_call(name, collective_id, lambda x, y, c: [(*chip, c) for chip in _other_chips(x, y)], body, arrs,
                      out_shape, (3 * n, 3 * n, n))


def _rs_final_sum(name, ids, recv, own, rows, after=None):
    _, R, C = recv.shape
    assert R % rows == 0

    def body(ids_ref, r_ref, own_ref, o_ref):
        acc = None
        for p in range(4):
            term = jnp.where(ids_ref[1] == p, own_ref[...], r_ref[p].astype(F32))
            acc = term if acc is None else acc + term
        o_ref[...] = acc

    in_specs = [pl.BlockSpec((4, rows, C), lambda i, ids: (0, i, 0)), pl.BlockSpec((rows, C), lambda i, ids: (i, 0))]
    args = [ids, recv, own]
    if after is not None:
        body = _drop_ref(body, len(args))
        in_specs.append(ANY)
        args.append(after)
    grid_spec = pltpu.PrefetchScalarGridSpec(
        num_scalar_prefetch=1, grid=(R // rows,), in_specs=in_specs,
        out_specs=pl.BlockSpec((rows, C), lambda i, ids: (i, 0)))
    return pl.pallas_call(
        body, name=name, grid_spec=grid_spec, out_shape=jax.ShapeDtypeStruct((R, C), F32),
        compiler_params=_cparams(("parallel",)),
    )(*args)


class _ReduceScatter:
    def __init__(self, tag, collective_id, big, small=None):
        x, y, c = _mesh_pos()
        self.tag, self.cid, self.small = tag, collective_id, small
        self.ids = jnp.stack([c, 2 * x + y]).astype(jnp.int32)
        self.big = big.reshape((4, 2) + big.shape[1:])
        self.got = _rs_sibling_exchange(tag + "_sibling", collective_id,
                                        [self.big] + ([] if small is None else [small]))

    def chip_sums(self, after=None):
        t16, self.own = _rs_chip_sum(self.tag + "_chip_sum", self.ids, self.big, self.got[0], after)
        arrs = [t16]
        if self.small is not None:
            self.ts = _add2(self.tag + "_chip_sum_small", self.small, self.got[1])
            arrs.append(self.ts)
        self.recv = _rs_chip_exchange(self.tag + "_chips", self.cid + 1, arrs)
        return t16

    def finish(self, after=None):
        out = _rs_final_sum(self.tag + "_final", self.ids, self.recv[0], self.own, _rs_rows(self.own.shape[0]), after)
        out_small = None
        if self.small is not None:
            out_small = _rs_final_sum(self.tag + "_final_small", self.ids, self.recv[1], self.ts, self.small.shape[0])
        return out, out_small


def _all_reduce_small(tag, small):
    x, y, c = _mesh_pos()
    ids = jnp.stack([c, 2 * x + y]).astype(jnp.int32)
    (theirs,) = _rs_sibling_exchange(tag + "_sibling", None, [small])
    ts = _add2(tag + "_chip_sum", small, theirs)
    (recv,) = _rs_chip_exchange(tag + "_chips", None, [ts])
    return _rs_final_sum(tag + "_final", ids, recv, ts, small.shape[0])


def _adamw(name, w, g, m, v):
    R, C = w.shape
    tr = max(t for t in range(8, 513, 8) if R % t == 0)
    c1 = 1.0 - ADAM_B1 ** ADAM_STEP
    c2 = 1.0 - ADAM_B2 ** ADAM_STEP

    def body(w_ref, g_ref, m_ref, v_ref, d_ref, nm_ref, nv_ref):
        gv = g_ref[...]
        nm = ADAM_B1 * m_ref[...] + (1.0 - ADAM_B1) * gv
        nv = ADAM_B2 * v_ref[...] + (1.0 - ADAM_B2) * (gv * gv)
        d_ref[...] = -ADAM_LR * ((nm / c1) / (jnp.sqrt(nv / c2) + ADAM_EPS) + ADAM_WD * w_ref[...])
        nm_ref[...] = nm
        nv_ref[...] = nv

    blk = pl.BlockSpec((tr, C), lambda i: (i, 0))
    shp = jax.ShapeDtypeStruct((R, C), F32)
    return pl.pallas_call(
        body, name=name, grid=(R // tr,), in_specs=[blk] * 4, out_specs=(blk,) * 3, out_shape=(shp,) * 3,
        compiler_params=_cparams(("parallel",)),
    )(w, g, m, v)


def _adamw_t(name, w, g_t, m, v):
    L, R, C = w.shape
    tr = 256
    c1 = 1.0 - ADAM_B1 ** ADAM_STEP
    c2 = 1.0 - ADAM_B2 ** ADAM_STEP

    def body(w_ref, g_ref, m_ref, v_ref, go_ref, d_ref, nm_ref, nv_ref):
        gv = g_ref[...].T
        go_ref[...] = gv
        nm = ADAM_B1 * m_ref[...] + (1.0 - ADAM_B1) * gv
        nv = ADAM_B2 * v_ref[...] + (1.0 - ADAM_B2) * (gv * gv)
        d_ref[...] = -ADAM_LR * ((nm / c1) / (jnp.sqrt(nv / c2) + ADAM_EPS) + ADAM_WD * w_ref[...])
        nm_ref[...] = nm
        nv_ref[...] = nv

    blk = pl.BlockSpec((None, tr, C), lambda l, i: (l, i, 0))
    blk_t = pl.BlockSpec((None, C, tr), lambda l, i: (l, 0, i))
    shp = jax.ShapeDtypeStruct((L, R, C), F32)
    return pl.pallas_call(
        body, name=name, grid=(L, R // tr), in_specs=[blk, blk_t, blk, blk], out_specs=(blk,) * 4,
        out_shape=(shp,) * 4, compiler_params=_cparams(("parallel", "parallel")),
    )(w, g_t, m, v)


REST_FF1 = 0
REST_FF2 = 512
REST_WO = 1024
REST_UP = 1152
REST_ATTN = 1280
REST_ROWS = 1312
IN_ROWS = IN_COLS // N_DEV
CONV_BITS_ROWS = 16
SHARD_ROWS = IN_ROWS + REST_ROWS + CONV_BITS_ROWS
SMALL = (("norm_mix", (DEPTH, D)), ("b_gate", (DEPTH, 3 * D)), ("pool_mix", (DEPTH, 4, POOL_G, POOL_G)),
         ("pool_scale", (DEPTH, POOL_W)), ("conv_w", (DEPTH, 3, CONV_W)), ("q_gain", (DEPTH, HEAD)),
         ("k_gain", (DEPTH, HEAD)), ("norm_mlp", (DEPTH, D)))


def _pack_weight_shards(w, l):
    b = lambda a: a.astype(BF16)
    conv = lax.bitcast_convert_type(w["conv_w"][l], BF16).reshape(3, 128)
    conv = jnp.pad(conv, ((0, CONV_BITS_ROWS - 3), (0, D - 128)))
    rest = jnp.concatenate([
        b(w["w_ff1"][l].T), b(w["w_ff2"][l]), b(w["w_o"][l]),
        jnp.concatenate([b(w["w_pool_up"][l].T), b(w["w_conv_out"][l].T)], axis=1),
        b(w["w_attn_up"][l].T).reshape(REST_ROWS - REST_ATTN, D), conv], axis=0)
    return b(w["w_in"][l].T), rest


def _unpack_gathered(g_in, g_rest, small_w, l):
    rest = g_rest[:, :REST_ROWS]
    take = lambda r0, rows, c0=0, cols=D: rest[:, r0:r0 + rows, c0:c0 + cols].reshape(N_DEV * rows, cols)
    conv = g_rest[:, REST_ROWS:REST_ROWS + 3, :128].reshape(N_DEV, 3, CONV_W // N_DEV, 2)
    conv = lax.bitcast_convert_type(conv, F32)
    p = {
        "w_in_t": g_in.reshape(IN_COLS, D),
        "w_ff1_t": take(REST_FF1, 512), "w_ff2": take(REST_FF2, 512), "w_o": take(REST_WO, 128),
        "w_pool_up_t": take(REST_UP, 128, 0, POOL_W), "w_conv_out_t": take(REST_UP, 128, POOL_W, CONV_W),
        "w_attn_up_t": rest[:, REST_ATTN:].reshape(D, ATTN_OUT),
        "conv_w": jnp.transpose(conv, (1, 0, 2)).reshape(3, CONV_W),
        "pool_mix": small_w["pool_mix"][l],
    }
    for name in ("norm_mix", "b_gate", "pool_scale", "q_gain", "k_gain", "norm_mlp"):
        p[name] = small_w[name][l][None, :]
    return p


def _pack_small_grads(grads):
    flat = jnp.concatenate([jnp.stack([grads[l][name] for l in range(DEPTH)]).reshape(-1) for name, _ in SMALL])
    return jnp.pad(flat, (0, (-flat.shape[0]) % (8 * 128))).reshape(-1, 128)


def _unpack_grads(in_t, rest, small, dev):
    out = {
        "w_in_t": jnp.stack(in_t),
        "w_ff1_t": jnp.stack([a[REST_FF1:REST_FF1 + 512] for a in rest]),
        "w_ff2": jnp.stack([a[REST_FF2:REST_FF2 + 512] for a in rest]),
        "w_o": jnp.stack([a[REST_WO:REST_WO + 128] for a in rest]),
        "w_pool_up": jnp.stack([a[REST_UP:REST_UP + 128, :POOL_W].T for a in rest]),
        "w_conv_out": jnp.stack([a[REST_UP:REST_UP + 128, POOL_W:].T for a in rest]),
        "w_attn_up": jnp.stack([a[REST_ATTN:].reshape(D // N_DEV, ATTN_OUT).T for a in rest]),
    }
    flat = small.reshape(-1)
    off = 0
    for name, shp in SMALL:
        n = 1
        for s in shp:
            n *= s
        out[name] = flat[off:off + n].reshape(shp)
        off += n
    width = CONV_W // N_DEV
    out["conv_w"] = lax.dynamic_slice_in_dim(out["conv_w"], dev * width, width, axis=2)
    return out


def _pack_small(arrs):
    flat = jnp.concatenate([a.reshape(-1) for a in arrs])
    pad = (-flat.shape[0]) % (8 * 128)
    return jnp.pad(flat, (0, pad)).reshape(-1, 128)


def _unpack_small(packed, like):
    flat = packed.reshape(-1)
    out, off = [], 0
    for a in like:
        out.append(flat[off:off + a.size].reshape(a.shape))
        off += a.size
    return out


WEIGHTS = ("norm_mix", "w_in", "b_gate", "pool_mix", "pool_scale", "conv_w", "q_gain", "k_gain", "w_pool_up",
           "w_conv_out", "w_attn_up", "w_o", "norm_mlp", "w_ff1", "w_ff2")


def kernel(x, norm_mix, w_in, b_gate, pool_mix, pool_scale, conv_w, q_gain, k_gain, w_pool_up, w_conv_out, w_attn_up, w_o, norm_mlp, w_ff1, w_ff2, loss_target, m_norm_mix, m_w_in, m_b_gate, m_pool_mix, m_pool_scale, m_conv_w, m_q_gain, m_k_gain, m_w_pool_up, m_w_conv_out, m_w_attn_up, m_w_o, m_norm_mlp, m_w_ff1, m_w_ff2, v_norm_mix, v_w_in, v_b_gate, v_pool_mix, v_pool_scale, v_conv_w, v_q_gain, v_k_gain, v_w_pool_up, v_w_conv_out, v_w_attn_up, v_w_o, v_norm_mlp, v_w_ff1, v_w_ff2):
    w = dict(zip(WEIGHTS, (norm_mix, w_in, b_gate, pool_mix, pool_scale, conv_w, q_gain, k_gain, w_pool_up,
                           w_conv_out, w_attn_up, w_o, norm_mlp, w_ff1, w_ff2)))
    m = dict(zip(WEIGHTS, (m_norm_mix, m_w_in, m_b_gate, m_pool_mix, m_pool_scale, m_conv_w, m_q_gain, m_k_gain,
                           m_w_pool_up, m_w_conv_out, m_w_attn_up, m_w_o, m_norm_mlp, m_w_ff1, m_w_ff2)))
    v = dict(zip(WEIGHTS, (v_norm_mix, v_w_in, v_b_gate, v_pool_mix, v_pool_scale, v_conv_w, v_q_gain, v_k_gain,
                           v_w_pool_up, v_w_conv_out, v_w_attn_up, v_w_o, v_norm_mlp, v_w_ff1, v_w_ff2)))
    xi, yi, ci = _mesh_pos()
    dev = 4 * xi + 2 * yi + ci

    params = []
    for l in range(DEPTH):
        s_in, s_rest = _pack_weight_shards(w, l)
        g_in = _all_gather(f"gather_in{l}", 1 + 2 * l, s_in)
        g_rest = _all_gather(f"gather_rest{l}", 2 + 2 * l, s_rest)
        params.append(_unpack_gathered(g_in, g_rest, w, l))
    saved = []
    act = x[0]
    h = _rmsnorm_fwd("l0_norm_mix", act, params[0]["norm_mix"])
    for l in range(DEPTH):
        act, h, s = _layer_fwd(l, act, h, params[l], params[l + 1]["norm_mix"] if l + 1 < DEPTH else None)
        saved.append(s)
    dy, loss = _loss_head("loss_head", act, loss_target[0])
    total = lax.psum(loss[0, 0], ("x", "y", "c"))
    in_t, rest, grads, pending = [None] * DEPTH, [None] * DEPTH, [None] * DEPTH, None
    start = total.reshape(1, 1)
    for l in reversed(range(DEPTH)):
        dy, grads[l], rest[l], done, pending = _layer_bwd(l, dy, params[l], saved[l], pending, 5 + 4 * l, l == 0,
                                                          start)
        start = None
        if done is not None:
            in_t[l + 1] = done[0]
    dx = dy
    g_small = _all_reduce_small("ar_small", _pack_small_grads(grads))
    in_t[0], _ = pending.finish(after=g_small)
    g = _unpack_grads(in_t, rest, g_small, dev)

    delta, new_m, new_v = {}, {}, {}
    for name in ("w_in", "w_ff1"):
        t = lambda a: jnp.swapaxes(a, 1, 2)
        g_t = g[name + "_t"]
        two_d = (g_t.shape[0] * g_t.shape[1], g_t.shape[2])
        d_, m_, v_ = _adamw("adamw_" + name, t(w[name]).reshape(two_d), g_t.reshape(two_d),
                            t(m[name]).reshape(two_d), t(v[name]).reshape(two_d))
        g[name] = t(g_t)
        delta[name], new_m[name], new_v[name] = (t(a.reshape(g_t.shape)) for a in (d_, m_, v_))
    for name in ("w_pool_up", "w_conv_out", "w_attn_up", "w_o", "w_ff2"):
        shp = w[name].shape
        two_d = (shp[0] * shp[1], shp[2])
        d_, m_, v_ = _adamw("adamw_" + name, w[name].reshape(two_d), g[name].reshape(two_d),
                            m[name].reshape(two_d), v[name].reshape(two_d))
        delta[name], new_m[name], new_v[name] = d_.reshape(shp), m_.reshape(shp), v_.reshape(shp)
    small_names = [name for name, _ in SMALL]
    packed = [_pack_small([src[name] for name in small_names]) for src in (w, g, m, v)]
    outs = _adamw("adamw_small", *packed)
    for dst, arr in zip((delta, new_m, new_v), outs):
        for name, val in zip(small_names, _unpack_small(arr, [w[name] for name in small_names])):
            dst[name] = val

    return (total, dx[None], *[g[n] for n in WEIGHTS], *[delta[n] for n in WEIGHTS],
            *[new_m[n] for n in WEIGHTS], *[new_v[n] for n in WEIGHTS])
```

```python
import functools

import jax
import jax.numpy as jnp
from jax import lax
from jax.experimental import pallas as pl
from jax.experimental.pallas import tpu as pltpu
from jax.experimental.pallas import tpu_sc as plsc

F32 = jnp.float32
BF16 = jnp.bfloat16
MESH = pl.DeviceIdType.MESH

D = 1024
SEQ = 4096
DEPTH = 2
N_DEV = 8
POOL_WINDOWS = (2, 4, 8, 16)
POOL_W = 512
POOL_G = 128
CONV_W = 512
HEAD = 64
ATTN_GROUPS = ((128, 1), (512, 4), (2048, 16))
HPG = 4
ATTN_W = 768
ATTN_OUT = 256
ATTN_BLK = 128
D_FF = 4096
EPS = 1e-6
MASK = -1e30
OFF_POOL = 0
OFF_CB = 512
OFF_CC = 1024
OFF_CX = 1536
OFF_Q = 2048
OFF_K = 2816
OFF_V = 3584
OFF_GATE = 4352
IN_COLS = 7424
ADAM_LR = 0.001
ADAM_B1 = 0.9
ADAM_B2 = 0.999
ADAM_EPS = 1e-08
ADAM_WD = 0.01
ADAM_STEP = 10

ROW_TILE = 512
SEQ_CHUNK = 256
HALO = 16
VMEM_LIMIT = 56 * 1024 * 1024


def _cparams(sem=None):
    return pltpu.CompilerParams(dimension_semantics=sem, vmem_limit_bytes=VMEM_LIMIT)


def _call(body, args, in_specs, after=None, **kw):
    if after is not None:
        n, inner = len(args), body

        def body(*refs):
            return inner(*refs[:n], *refs[n + 1:])

        args = list(args) + [after]
        in_specs = list(in_specs) + [pl.BlockSpec(memory_space=pl.ANY)]
    return pl.pallas_call(body, in_specs=in_specs, **kw)(*args)


_DIMS = {"nn": (((1,), (0,)), ((), ())), "nt": (((1,), (1,)), ((), ())), "tn": (((0,), (0,)), ((), ()))}


def _matmul(name, a, b, mode, tm, tn, tk, out_dtypes=(F32,), extras=(), epilogue=None, into=None, after=None):
    if mode == "tn":
        K, M = a.shape
    else:
        M, K = a.shape
    N = b.shape[0] if mode == "nt" else b.shape[1]
    assert M % tm == 0 and N % tn == 0 and K % tk == 0, (name, M, N, K, tm, tn, tk)
    nk = K // tk
    n_extra = len(extras)
    n_out = len(out_dtypes)
    dims = _DIMS[mode]
    n_alias = 0 if into is None or isinstance(into[0], jax.ShapeDtypeStruct) else 1

    def body(a_ref, b_ref, *rest):
        extra_refs = rest[:n_extra]
        out_refs = rest[n_extra + n_alias:n_extra + n_alias + n_out]

        def finish(acc):
            if epilogue is None:
                res = (acc,)
            else:
                res = epilogue(acc, *[r[...] for r in extra_refs])
            for o_ref, v in zip(out_refs, res):
                o_ref[...] = v.astype(o_ref.dtype)

        part = lax.dot_general(a_ref[...].astype(BF16), b_ref[...].astype(BF16), dims,
                               preferred_element_type=F32)
        if nk == 1:
            finish(part)
        else:
            acc_ref = rest[-1]
            k = pl.program_id(2)

            @pl.when(k == 0)
            def _():
                acc_ref[...] = part

            @pl.when(k > 0)
            def _():
                acc_ref[...] += part

            @pl.when(k == nk - 1)
            def _():
                finish(acc_ref[...])

    if mode == "tn":
        a_spec = pl.BlockSpec((tk, tm), lambda i, j, k: (k, i))
    else:
        a_spec = pl.BlockSpec((tm, tk), lambda i, j, k: (i, k))
    if mode == "nt":
        b_spec = pl.BlockSpec((tn, tk), lambda i, j, k: (j, k))
    else:
        b_spec = pl.BlockSpec((tk, tn), lambda i, j, k: (k, j))
    in_specs = [a_spec, b_spec]
    args = [a, b]
    for arr, kind in extras:
        if kind == "mn":
            in_specs.append(pl.BlockSpec((tm, tn), lambda i, j, k: (i, j)))
        else:
            in_specs.append(pl.BlockSpec((1, tn), lambda i, j, k: (0, j)))
        args.append(arr)
    part = [isinstance(dt, tuple) for dt in out_dtypes]
    out_shape = tuple(jax.ShapeDtypeStruct((M // tm * 8, N), dt[0]) if p else jax.ShapeDtypeStruct((M, N), dt)
                      for dt, p in zip(out_dtypes, part))
    out_specs = tuple(pl.BlockSpec((8 if p else tm, tn), lambda i, j, k: (i, j)) for p in part)
    aliases = {}
    if into is not None:
        buf, row0, col0 = into
        assert n_out == 1 and (M // N_DEV) % tm == 0 and row0 % tm == 0 and col0 % tn == 0
        per_dev = M // N_DEV // tm
        out_shape = (jax.ShapeDtypeStruct(buf.shape, buf.dtype),)
        out_specs = (pl.BlockSpec((None, tm, tn), lambda i, j, k: (i // per_dev, row0 // tm + i % per_dev,
                                                                   col0 // tn + j)),)
        if not isinstance(buf, jax.ShapeDtypeStruct):
            aliases = {len(args): 0}
            in_specs.append(pl.BlockSpec(memory_space=pl.ANY))
            args.append(buf)
    scratch = [] if nk == 1 else [pltpu.VMEM((tm, tn), F32)]
    res = _call(
        body, args, in_specs, after, name=name, grid=(M // tm, N // tn, nk), out_specs=out_specs,
        out_shape=out_shape, scratch_shapes=scratch, input_output_aliases=aliases,
        compiler_params=_cparams(("parallel", "parallel", "arbitrary")))
    return res if n_out > 1 else res[0]


def _rmsnorm_fwd(name, x, gain):
    S_, D_ = x.shape

    def body(x_ref, g_ref, h_ref):
        xf = x_ref[...]
        ms = jnp.mean(xf * xf, axis=-1, keepdims=True)
        h_ref[...] = (xf * lax.rsqrt(ms + EPS) * g_ref[...]).astype(BF16)

    return pl.pallas_call(
        body, name=name, grid=(S_ // ROW_TILE,),
        in_specs=[pl.BlockSpec((ROW_TILE, D_), lambda i: (i, 0)), pl.BlockSpec((1, D_), lambda i: (0, 0))],
        out_specs=pl.BlockSpec((ROW_TILE, D_), lambda i: (i, 0)),
        out_shape=jax.ShapeDtypeStruct((S_, D_), BF16),
        compiler_params=_cparams(("parallel",)),
    )(x, gain)


def _rmsnorm_bwd(name, x, gain, dh, dres, after=None):
    S_, D_ = x.shape
    n = S_ // ROW_TILE

    def body(x_ref, g_ref, dh_ref, dres_ref, dx_ref, dg_ref, acc_ref):
        i = pl.program_id(0)
        xf = x_ref[...]
        rstd = lax.rsqrt(jnp.mean(xf * xf, axis=-1, keepdims=True) + EPS)
        xhat = xf * rstd
        dhv = dh_ref[...]
        dxhat = dhv * g_ref[...]
        c = jnp.mean(dxhat * xhat, axis=-1, keepdims=True)
        dx_ref[...] = dres_ref[...] + rstd * (dxhat - xhat * c)
        part = jnp.sum((dhv * xhat).reshape(ROW_TILE // 8, 8, D_), axis=0)

        @pl.when(i == 0)
        def _():
            acc_ref[...] = part

        @pl.when(i > 0)
        def _():
            acc_ref[...] += part

        @pl.when(i == n - 1)
        def _():
            dg_ref[...] = jnp.sum(acc_ref[...], axis=0, keepdims=True)

    row = pl.BlockSpec((ROW_TILE, D_), lambda i: (i, 0))
    vec = pl.BlockSpec((1, D_), lambda i: (0, 0))
    return _call(
        body, [x, gain, dh, dres], [row, vec, row, row], after, name=name, grid=(n,), out_specs=(row, vec),
        out_shape=(jax.ShapeDtypeStruct((S_, D_), F32), jax.ShapeDtypeStruct((1, D_), F32)),
        scratch_shapes=[pltpu.VMEM((8, D_), F32)],
        compiler_params=_cparams(("arbitrary",)))


def _loss_head(name, y, target):
    S_, D_ = y.shape
    n = S_ // ROW_TILE

    def body(y_ref, t_ref, dy_ref, l_ref, acc_ref):
        i = pl.program_id(0)
        e = y_ref[...] - t_ref[...]
        dy_ref[...] = e * (1.0 / D_)
        part = jnp.sum((e * e).reshape(ROW_TILE // 8, 8, D_), axis=0)

        @pl.when(i == 0)
        def _():
            acc_ref[...] = part

        @pl.when(i > 0)
        def _():
            acc_ref[...] += part

        @pl.when(i == n - 1)
        def _():
            s = jnp.sum(acc_ref[...], axis=1, keepdims=True)
            l_ref[...] = jnp.sum(s, axis=0, keepdims=True) * (0.5 / D_)

    row = pl.BlockSpec((ROW_TILE, D_), lambda i: (i, 0))
    return pl.pallas_call(
        body, name=name, grid=(n,), in_specs=[row, row],
        out_specs=(row, pl.BlockSpec((1, 1), lambda i: (0, 0))),
        out_shape=(jax.ShapeDtypeStruct((S_, D_), F32), jax.ShapeDtypeStruct((1, 1), F32)),
        scratch_shapes=[pltpu.VMEM((8, D_), F32)],
        compiler_params=_cparams(("arbitrary",)),
    )(y, target)


def _rows_with_halo(ref, cols, i, n_chunks, before, after):
    r0 = pl.multiple_of(i * SEQ_CHUNK, SEQ_CHUNK)
    parts = []
    if before:
        h0 = pl.multiple_of(jnp.maximum(r0 - HALO, 0), 8)
        halo = ref[pl.ds(h0, HALO), cols]
        parts.append(jnp.where(i > 0, halo, jnp.zeros_like(halo)))
    parts.append(ref[pl.ds(r0, SEQ_CHUNK), cols])
    if after:
        a0 = pl.multiple_of(jnp.minimum(r0 + SEQ_CHUNK, (n_chunks - 1) * SEQ_CHUNK + SEQ_CHUNK - HALO), 8)
        halo = ref[pl.ds(a0, HALO), cols]
        parts.append(jnp.where(i < n_chunks - 1, halo, jnp.zeros_like(halo)))
    return parts[0] if len(parts) == 1 else jnp.concatenate(parts, axis=0)


def _shift_down(v, k):
    return pltpu.roll(v, k, 0)


def _shift_up(v, k):
    return pltpu.roll(v, v.shape[0] - k, 0)


def _pool_diff(xx, w, t_main):
    s = xx
    k = 1
    while k < w:
        s = s + _shift_down(s, k)
        k *= 2
    cnt = jnp.minimum(t_main + 1, w).astype(F32)
    return s[HALO:] / cnt - xx[HALO:]


def _pool_fwd(name, z, pool_mix, pool_scale):
    S_ = z.shape[0]
    n_chunks = S_ // SEQ_CHUNK

    cols = slice(0, POOL_G)

    def body(u_ref, mix_ref, sc_ref, y_ref):
        mixg = mix_ref[...].astype(BF16)
        scg = sc_ref[...]
        for g, w in enumerate(POOL_WINDOWS):
            @pl.when(pl.program_id(0) == g)
            def _(w=w):
                def chunk(i, carry):
                    r0 = pl.multiple_of(i * SEQ_CHUNK, SEQ_CHUNK)
                    xx = _rows_with_halo(u_ref, cols, i, n_chunks, True, False)
                    t = r0 + lax.broadcasted_iota(jnp.int32, (SEQ_CHUNK, POOL_G), 0)
                    d = _pool_diff(xx, w, t)
                    y = jnp.dot(d.astype(BF16), mixg, preferred_element_type=F32) * scg
                    y_ref[pl.ds(r0, SEQ_CHUNK), :] = y.astype(BF16)
                    return carry

                lax.fori_loop(0, n_chunks, chunk, 0)

    slab = pl.BlockSpec((S_, POOL_G), lambda g: (0, g))
    return pl.pallas_call(
        body, name=name, grid=(len(POOL_WINDOWS),),
        in_specs=[slab, pl.BlockSpec((None, POOL_G, POOL_G), lambda g: (g, 0, 0)),
                  pl.BlockSpec((1, POOL_G), lambda g: (0, g))],
        out_specs=slab, out_shape=jax.ShapeDtypeStruct((S_, POOL_W), BF16),
        compiler_params=_cparams(("parallel",)),
    )(z, pool_mix, pool_scale)


def _pool_bwd(name, z, dy, pool_mix, pool_scale, after=None):
    S_ = z.shape[0]
    n_chunks = S_ // SEQ_CHUNK
    rows_a = SEQ_CHUNK + HALO

    cols = slice(0, POOL_G)

    def body(u_ref, dy_ref, mix_ref, sc_ref, du_ref, dmix_ref, dsc_ref):
        mixg = mix_ref[...].astype(BF16)
        scg = sc_ref[...]
        for g, w in enumerate(POOL_WINDOWS):
            @pl.when(pl.program_id(0) == g)
            def _(w=w):
                def chunk(i, carry):
                    dmix_acc, dsc_acc = carry
                    r0 = pl.multiple_of(i * SEQ_CHUNK, SEQ_CHUNK)
                    xx = _rows_with_halo(u_ref, cols, i, n_chunks, True, False)
                    t = r0 + lax.broadcasted_iota(jnp.int32, (SEQ_CHUNK, POOL_G), 0)
                    d = _pool_diff(xx, w, t).astype(BF16)
                    ypre = jnp.dot(d, mixg, preferred_element_type=F32)
                    dyy = _rows_with_halo(dy_ref, cols, i, n_chunks, False, True)
                    dys = (dyy * scg).astype(BF16)
                    dsc_acc = dsc_acc + jnp.sum((dyy[:SEQ_CHUNK] * ypre).reshape(SEQ_CHUNK // 8, 8, POOL_G), axis=0)
                    dmix_acc = dmix_acc + lax.dot_general(d, dys[:SEQ_CHUNK], _DIMS["tn"],
                                                          preferred_element_type=F32)
                    dd = lax.dot_general(dys, mixg, _DIMS["nt"], preferred_element_type=F32)
                    ta = r0 + lax.broadcasted_iota(jnp.int32, (rows_a, POOL_G), 0)
                    f = dd / jnp.minimum(ta + 1, w).astype(F32)
                    k = 1
                    while k < w:
                        f = f + _shift_up(f, k)
                        k *= 2
                    du_ref[pl.ds(r0, SEQ_CHUNK), :] = (f[:SEQ_CHUNK] - dd[:SEQ_CHUNK]).astype(BF16)
                    return dmix_acc, dsc_acc

                dmix_acc, dsc_acc = lax.fori_loop(
                    0, n_chunks, chunk, (jnp.zeros((POOL_G, POOL_G), F32), jnp.zeros((8, POOL_G), F32)))
                dmix_ref[...] = dmix_acc
                dsc_ref[...] = jnp.sum(dsc_acc, axis=0, keepdims=True)

    slab = pl.BlockSpec((S_, POOL_G), lambda g: (0, g))
    mix_spec = pl.BlockSpec((None, POOL_G, POOL_G), lambda g: (g, 0, 0))
    vec = pl.BlockSpec((1, POOL_G), lambda g: (0, g))
    return _call(
        body, [z, dy, pool_mix, pool_scale], [slab, slab, mix_spec, vec], after, name=name,
        grid=(len(POOL_WINDOWS),), out_specs=(slab, mix_spec, vec),
        out_shape=(jax.ShapeDtypeStruct((S_, POOL_W), BF16), jax.ShapeDtypeStruct((4, POOL_G, POOL_G), F32),
                   jax.ShapeDtypeStruct((1, POOL_W), F32)),
        compiler_params=_cparams(("parallel",)))


def _conv_specs(S_):
    slab = lambda off: pl.BlockSpec((S_, 128), lambda c, off=off: (0, off // 128 + c))
    return slab(OFF_CB), slab(OFF_CC), slab(OFF_CX)


def _conv_fwd(name, z, conv_w):
    S_ = z.shape[0]
    n_chunks = S_ // SEQ_CHUNK
    col = slice(0, 128)

    def body(b_ref, c_ref, x_ref, w_ref, y_ref):
        w0, w1, w2 = w_ref[0:1, :], w_ref[1:2, :], w_ref[2:3, :]

        def chunk(i, carry):
            r0 = pl.multiple_of(i * SEQ_CHUNK, SEQ_CHUNK)
            u = _rows_with_halo(c_ref, col, i, n_chunks, True, False) * _rows_with_halo(x_ref, col, i, n_chunks, True, False)
            y = w2 * u + w1 * _shift_down(u, 1) + w0 * _shift_down(u, 2)
            y_ref[pl.ds(r0, SEQ_CHUNK), :] = (b_ref[pl.ds(r0, SEQ_CHUNK), :] * y[HALO:]).astype(BF16)
            return carry

        lax.fori_loop(0, n_chunks, chunk, 0)

    sb, sc, sx = _conv_specs(S_)
    return pl.pallas_call(
        body, name=name, grid=(CONV_W // 128,),
        in_specs=[sb, sc, sx, pl.BlockSpec((3, 128), lambda c: (0, c))],
        out_specs=pl.BlockSpec((S_, 128), lambda c: (0, c)),
        out_shape=jax.ShapeDtypeStruct((S_, CONV_W), BF16),
        compiler_params=_cparams(("parallel",)),
    )(z, z, z, conv_w)


def _conv_bwd(name, z, dy, conv_w, after=None):
    S_ = z.shape[0]
    n_chunks = S_ // SEQ_CHUNK
    col = slice(0, 128)
    lo, hi = HALO, HALO + SEQ_CHUNK

    def body(b_ref, c_ref, x_ref, dy_ref, w_ref, db_ref, dc_ref, dx_ref, dw_ref):
        w0, w1, w2 = w_ref[0:1, :], w_ref[1:2, :], w_ref[2:3, :]

        def chunk(i, carry):
            a0, a1, a2 = carry
            r0 = pl.multiple_of(i * SEQ_CHUNK, SEQ_CHUNK)
            cc = _rows_with_halo(c_ref, col, i, n_chunks, True, True)
            xx = _rows_with_halo(x_ref, col, i, n_chunks, True, True)
            bb = _rows_with_halo(b_ref, col, i, n_chunks, True, True)
            dyy = _rows_with_halo(dy_ref, col, i, n_chunks, True, True)
            u = cc * xx
            u1 = _shift_down(u, 1)
            u2 = _shift_down(u, 2)
            y = w2 * u + w1 * u1 + w0 * u2
            dyv = dyy * bb
            du = w2 * dyv + w1 * _shift_up(dyv, 1) + w0 * _shift_up(dyv, 2)
            db_ref[pl.ds(r0, SEQ_CHUNK), :] = (dyy[lo:hi] * y[lo:hi]).astype(BF16)
            dc_ref[pl.ds(r0, SEQ_CHUNK), :] = (du[lo:hi] * xx[lo:hi]).astype(BF16)
            dx_ref[pl.ds(r0, SEQ_CHUNK), :] = (du[lo:hi] * cc[lo:hi]).astype(BF16)
            red = lambda v: jnp.sum(v.reshape(SEQ_CHUNK // 8, 8, 128), axis=0)
            dm = dyv[lo:hi]
            return a0 + red(dm * u2[lo:hi]), a1 + red(dm * u1[lo:hi]), a2 + red(dm * u[lo:hi])

        zero = jnp.zeros((8, 128), F32)
        a0, a1, a2 = lax.fori_loop(0, n_chunks, chunk, (zero, zero, zero))
        dw_ref[0:1, :] = jnp.sum(a0, axis=0, keepdims=True)
        dw_ref[1:2, :] = jnp.sum(a1, axis=0, keepdims=True)
        dw_ref[2:3, :] = jnp.sum(a2, axis=0, keepdims=True)

    sb, sc, sx = _conv_specs(S_)
    slab = pl.BlockSpec((S_, 128), lambda c: (0, c))
    wspec = pl.BlockSpec((3, 128), lambda c: (0, c))
    act = jax.ShapeDtypeStruct((S_, CONV_W), BF16)
    return _call(
        body, [z, z, z, dy, conv_w], [sb, sc, sx, slab, wspec], after, name=name, grid=(CONV_W // 128,),
        out_specs=(slab, slab, slab, wspec),
        out_shape=(act, act, act, jax.ShapeDtypeStruct((3, CONV_W), F32)),
        compiler_params=_cparams(("parallel",)))


def _head_ones(pw):
    a = lax.broadcasted_iota(jnp.int32, (pw, pw), 0) // HEAD
    b = lax.broadcasted_iota(jnp.int32, (pw, pw), 1) // HEAD
    return (a == b).astype(BF16)


def _head_sum(v, ones):
    hi = v.astype(BF16)
    lo = (v - hi.astype(F32)).astype(BF16)
    return jnp.dot(hi, ones, preferred_element_type=F32) + jnp.dot(lo, ones, preferred_element_type=F32)


def _head_norm(v, gain, ones):
    rstd = lax.rsqrt(_head_sum(v * v, ones) * (1.0 / HEAD) + EPS)
    xhat = v * rstd
    return xhat * gain, xhat, rstd


def _head_norm_bwd(dy, xhat, rstd, gain, ones):
    dxhat = dy * gain
    c = _head_sum(dxhat * xhat, ones) * (1.0 / HEAD)
    dv = rstd * (dxhat - xhat * c)
    dg = jnp.sum((dy * xhat).reshape(dy.shape[0] // 8, 8, dy.shape[1]), axis=0)
    return dv, dg


def _head_masks(pw):
    lane_head = lax.broadcasted_iota(jnp.int32, (1, pw), 1) // HEAD
    return [lane_head == h for h in range(pw // HEAD)]


def _only(mask, v):
    return jnp.where(mask, v, jnp.zeros_like(v))


def _attn_specs(S_, g, dil):
    rows = ATTN_BLK * dil
    nb = S_ // rows
    pw = 128 if dil > 1 else ATTN_OUT
    cq, ck, cv = ((OFF_Q + g * ATTN_OUT) // pw, (OFF_K + g * ATTN_OUT) // pw, (OFF_V + g * ATTN_OUT) // pw)
    return rows, nb, pw, pw // HEAD, ATTN_OUT // pw, cq, ck, cv


ATTN_BATCH = 4


def _attn_group(dil):
    return 4 if dil == 1 else 1


def _attn_block_specs(rows, grp, pw, last=None):
    step = (lambda n: n) if last is None else (lambda n: jnp.minimum(n, last))
    cur = lambda c: pl.BlockSpec((rows * grp, pw), lambda hp, n, c=c: (step(n), c + hp))
    prev = lambda c: pl.BlockSpec((rows, pw), lambda hp, n, c=c: (jnp.maximum(step(n) * grp - 1, 0), c + hp))
    return cur, prev


def _band_mask(has_prev):
    qi = lax.broadcasted_iota(jnp.int32, (ATTN_BLK, 2 * ATTN_BLK), 0)
    ki = lax.broadcasted_iota(jnp.int32, (ATTN_BLK, 2 * ATTN_BLK), 1)
    in_prev = jnp.logical_and(ki < ATTN_BLK, ki >= qi)
    if has_prev is not True:
        in_prev = jnp.logical_and(in_prev, has_prev)
    return jnp.logical_or(in_prev, jnp.logical_and(ki >= ATTN_BLK, ki - ATTN_BLK <= qi))


def _rows_of(ref, r, dil):
    if dil == 1:
        return ref[r * ATTN_BLK:(r + 1) * ATTN_BLK, :]
    return ref[pl.ds(r, ATTN_BLK, stride=dil), :]


def _put_rows(ref, r, dil, val):
    if dil == 1:
        ref[r * ATTN_BLK:(r + 1) * ATTN_BLK, :] = val.astype(ref.dtype)
    else:
        ref[pl.ds(r, ATTN_BLK, stride=dil), :] = val.astype(ref.dtype)


def _attn_fwd(name, z, q_gain, k_gain, g, dil):
    S_ = z.shape[0]
    rows, nb, pw, heads, npairs, cq, ck, cv = _attn_specs(S_, g, dil)
    scale = HEAD ** -0.5

    grp = _attn_group(dil)
    nsteps = nb // grp

    def body(q_ref, k_ref, kp_ref, v_ref, vp_ref, gq_ref, gk_ref, o_ref, l_ref):
        n = pl.program_id(1)
        ones, hmask = _head_ones(pw), _head_masks(pw)
        gq, gk = jnp.tile(gq_ref[...], (1, heads)), jnp.tile(gk_ref[...], (1, heads))
        mask_first, mask_rest = _band_mask(n > 0), _band_mask(True)
        for r0 in range(0, dil * grp, ATTN_BATCH):
            rs = range(r0, min(r0 + ATTN_BATCH, dil * grp))
            qn, kn, vv, s, p = {}, {}, {}, {}, {}
            kcn = {}
            for r in rs:
                q, kc, vc = _rows_of(q_ref, r, dil), _rows_of(k_ref, r, dil), _rows_of(v_ref, r, dil)
                kcn[r] = _head_norm(kc, gk, ones)[0]
                if dil == 1 and r > 0:
                    kpn = kcn[r - 1] if r - 1 in kcn else _head_norm(_rows_of(k_ref, r - 1, dil), gk, ones)[0]
                    vp = _rows_of(v_ref, r - 1, dil)
                else:
                    kpn, vp = _head_norm(_rows_of(kp_ref, r, dil), gk, ones)[0], _rows_of(vp_ref, r, dil)
                qn[r] = _head_norm(q, gq, ones)[0].astype(BF16)
                kn[r] = jnp.concatenate([kpn, kcn[r]], axis=0).astype(BF16)
                vv[r] = jnp.concatenate([vp, vc], axis=0).astype(BF16)
            keys = [(r, h) for r in rs for h in range(heads)]
            for r, h in keys:
                s[r, h] = lax.dot_general(_only(hmask[h], qn[r]), kn[r], _DIMS["nt"],
                                          preferred_element_type=F32) * scale
            lse, den = {}, {}
            for key in keys:
                mask = mask_rest if (dil == 1 and key[0] > 0) else mask_first
                sm = jnp.where(mask, s[key], MASK)
                m = jnp.max(sm, axis=-1, keepdims=True)
                e = jnp.exp(sm - m)
                den[key] = jnp.sum(e, axis=-1, keepdims=True)
                p[key] = e.astype(BF16)
                lse[key] = m + jnp.log(den[key])
            for r in rs:
                out = jnp.zeros((ATTN_BLK, pw), F32)
                lse_all = jnp.zeros((ATTN_BLK, pw), F32)
                for h in range(heads):
                    out = jnp.where(hmask[h], jnp.dot(p[r, h], vv[r], preferred_element_type=F32) / den[r, h], out)
                    lse_all = jnp.where(hmask[h], lse[r, h], lse_all)
                _put_rows(o_ref, r, dil, out)
                _put_rows(l_ref, r, dil, lse_all)

    cur, prev = _attn_block_specs(rows, grp, pw)
    gspec = pl.BlockSpec((1, HEAD), lambda hp, n: (0, 0))
    shp = jax.ShapeDtypeStruct((S_, ATTN_OUT), F32)
    return pl.pallas_call(
        body, name=name, grid=(npairs, nsteps),
        in_specs=[cur(cq), cur(ck), prev(ck), cur(cv), prev(cv), gspec, gspec],
        out_specs=(cur(0), cur(0)), out_shape=(shp, shp),
        compiler_params=_cparams(("parallel", "parallel")),
    )(z, z, z, z, z, q_gain, k_gain)


def _attn_combine(name, os_, ls_):
    S_ = os_[0].shape[0]

    def body(o0, o1, o2, l0, l1, l2, o_ref, l_ref):
        a, b, c = l0[...], l1[...], l2[...]
        m = jnp.maximum(jnp.maximum(a, b), c)
        ea, eb, ec = jnp.exp(a - m), jnp.exp(b - m), jnp.exp(c - m)
        zsum = ea + eb + ec
        o_ref[...] = (ea * o0[...] + eb * o1[...] + ec * o2[...]) / zsum
        l_ref[...] = m + jnp.log(zsum)

    row = pl.BlockSpec((ROW_TILE, ATTN_OUT), lambda i: (i, 0))
    shp = jax.ShapeDtypeStruct((S_, ATTN_OUT), F32)
    return pl.pallas_call(
        body, name=name, grid=(S_ // ROW_TILE,), in_specs=[row] * 6, out_specs=(row, row), out_shape=(shp, shp),
        compiler_params=_cparams(("parallel",)),
    )(*os_, *ls_)


def _attn_bwd(name, z, q_gain, k_gain, do, o, lse, g, dil, after=None):
    S_ = z.shape[0]
    rows, nb, pw, heads, npairs, cq, ck, cv = _attn_specs(S_, g, dil)
    scale = HEAD ** -0.5

    def body(q_ref, kc_ref, kp_ref, vc_ref, vp_ref, gq_ref, gk_ref, do_ref, o_ref, l_ref,
             dq_ref, dk_ref, dv_ref, dgq_ref, dgk_ref, ck_ref, cvv_ref, gq_acc, gk_acc):
        hp = pl.program_id(0)
        n = pl.program_id(1)
        live = n < nb
        mask = jnp.logical_and(_band_mask(n > 0), live)
        ones, hmask = _head_ones(pw), _head_masks(pw)
        gq, gk = jnp.tile(gq_ref[...], (1, heads)), jnp.tile(gk_ref[...], (1, heads))

        @pl.when(n == 0)
        def _():
            ck_ref[...] = jnp.zeros_like(ck_ref)
            cvv_ref[...] = jnp.zeros_like(cvv_ref)

        @pl.when(jnp.logical_and(n == 0, hp == 0))
        def _():
            gq_acc[...] = jnp.zeros_like(gq_acc)
            gk_acc[...] = jnp.zeros_like(gk_acc)

        dgq = jnp.zeros((8, pw), F32)
        dgk = jnp.zeros((8, pw), F32)
        for r0 in range(0, dil, ATTN_BATCH):
            rs = range(r0, min(r0 + ATTN_BATCH, dil))
            keys = [(r, h) for r in rs for h in range(heads)]
            qn, qhat, qrstd, kn, kphat, kprstd, vv, dob, delta, lse = ({} for _ in range(10))
            for r in rs:
                q, kc, kp = _rows_of(q_ref, r, dil), _rows_of(kc_ref, r, dil), _rows_of(kp_ref, r, dil)
                dov = _rows_of(do_ref, r, dil)
                qn_f, qhat[r], qrstd[r] = _head_norm(q, gq, ones)
                kpn, kphat[r], kprstd[r] = _head_norm(kp, gk, ones)
                qn[r] = qn_f.astype(BF16)
                kn[r] = jnp.concatenate([kpn, _head_norm(kc, gk, ones)[0]], axis=0).astype(BF16)
                vv[r] = jnp.concatenate([_rows_of(vp_ref, r, dil), _rows_of(vc_ref, r, dil)], axis=0).astype(BF16)
                dob[r] = dov.astype(BF16)
                delta[r] = _head_sum(dov * _rows_of(o_ref, r, dil), ones)
                lse[r] = _rows_of(l_ref, r, dil)
            s, dp = {}, {}
            for r, h in keys:
                s[r, h] = lax.dot_general(_only(hmask[h], qn[r]), kn[r], _DIMS["nt"],
                                          preferred_element_type=F32) * scale
                dp[r, h] = lax.dot_general(_only(hmask[h], dob[r]), vv[r], _DIMS["nt"], preferred_element_type=F32)
            p, ds = {}, {}
            for r, h in keys:
                col = slice(h * HEAD, h * HEAD + 1)
                pk = jnp.where(mask, jnp.exp(jnp.where(mask, s[r, h], MASK) - lse[r][:, col]), 0.0)
                ds[r, h] = (pk * (dp[r, h] - delta[r][:, col]) * scale).astype(BF16)
                p[r, h] = pk.astype(BF16)
            dqn, dkn, dvv = {}, {}, {}
            for r in rs:
                dqn[r] = jnp.zeros((ATTN_BLK, pw), F32)
                dkn[r] = jnp.zeros((2 * ATTN_BLK, pw), F32)
                dvv[r] = jnp.zeros((2 * ATTN_BLK, pw), F32)
                for h in range(heads):
                    dqn[r] = jnp.where(hmask[h], jnp.dot(ds[r, h], kn[r], preferred_element_type=F32), dqn[r])
                    dkn[r] = jnp.where(hmask[h], lax.dot_general(ds[r, h], qn[r], _DIMS["tn"],
                                                                 preferred_element_type=F32), dkn[r])
                    dvv[r] = jnp.where(hmask[h], lax.dot_general(p[r, h], dob[r], _DIMS["tn"],
                                                                 preferred_element_type=F32), dvv[r])
            for r in rs:
                dq_all, dg = _head_norm_bwd(dqn[r], qhat[r], qrstd[r], gq, ones)
                dgq = dgq + dg
                dk_all, dg = _head_norm_bwd(_rows_of(ck_ref, r, dil) + dkn[r][:ATTN_BLK], kphat[r], kprstd[r], gk, ones)
                dgk = dgk + dg
                dv_all = _rows_of(cvv_ref, r, dil) + dvv[r][:ATTN_BLK]

                @pl.when(live)
                def _(dq_all=dq_all, r=r):
                    _put_rows(dq_ref, r, dil, dq_all)

                _put_rows(dk_ref, r, dil, dk_all)
                _put_rows(dv_ref, r, dil, dv_all)
                _put_rows(ck_ref, r, dil, dkn[r][ATTN_BLK:])
                _put_rows(cvv_ref, r, dil, dvv[r][ATTN_BLK:])
        gq_acc[...] += dgq
        gk_acc[...] += dgk

        @pl.when(jnp.logical_and(n == nb, hp == npairs - 1))
        def _():
            fold = lambda a: sum(a[:, h * HEAD:(h + 1) * HEAD] for h in range(heads))
            dgq_ref[...] = jnp.sum(fold(gq_acc[...]), axis=0, keepdims=True)
            dgk_ref[...] = jnp.sum(fold(gk_acc[...]), axis=0, keepdims=True)

    last = nb - 1
    cur = lambda c: pl.BlockSpec((rows, pw), lambda hp, n, c=c: (jnp.minimum(n, last), c + hp))
    prev = lambda c: pl.BlockSpec((rows, pw), lambda hp, n, c=c: (jnp.maximum(n - 1, 0), c + hp))
    gspec = pl.BlockSpec((1, HEAD), lambda hp, n: (0, 0))
    act = jax.ShapeDtypeStruct((S_, ATTN_OUT), F32)
    vec = jax.ShapeDtypeStruct((1, HEAD), F32)
    return _call(
        body, [z, z, z, z, z, q_gain, k_gain, do, o, lse],
        [cur(cq), cur(ck), prev(ck), cur(cv), prev(cv), gspec, gspec, cur(0), cur(0), cur(0)], after,
        name=name, grid=(npairs, nb + 1),
        out_specs=(cur(0), prev(0), prev(0), gspec, gspec),
        out_shape=(act, act, act, vec, vec),
        scratch_shapes=[pltpu.VMEM((rows, pw), F32), pltpu.VMEM((rows, pw), F32),
                        pltpu.VMEM((8, pw), F32), pltpu.VMEM((8, pw), F32)],
        compiler_params=_cparams(("arbitrary", "arbitrary")))


MIX_TN = 256
MIX_TM = 2048


def _sigmoid(v):
    return 1.0 / (1.0 + jnp.exp(-v))


def _mix_fwd(name, z, b_gate, ys, ws):
    S_ = z.shape[0]
    tm, tn = MIX_TM, MIX_TN
    gblk = OFF_GATE // tn

    def body(yp, yc, ya, wp, wc, wa, g0, g1, g2, b0, b1, b2, m_ref):
        acc = None
        for y_ref, w_ref, g_ref, b_ref in ((yp, wp, g0, b0), (yc, wc, g1, b1), (ya, wa, g2, b2)):
            u = lax.dot_general(y_ref[...].astype(BF16), w_ref[...], _DIMS["nt"], preferred_element_type=F32)
            t = _sigmoid(g_ref[...] + b_ref[...]) * u
            acc = t if acc is None else acc + t
        m_ref[...] = acc.astype(BF16)

    yspec = lambda w: pl.BlockSpec((tm, w), lambda i, j: (i, 0))
    wspec = lambda w: pl.BlockSpec((tn, w), lambda i, j: (j, 0))
    gspec = lambda b: pl.BlockSpec((tm, tn), lambda i, j, b=b: (i, gblk + b * (D // tn) + j))
    bspec = lambda b: pl.BlockSpec((1, tn), lambda i, j, b=b: (0, b * (D // tn) + j))
    return pl.pallas_call(
        body, name=name, grid=(S_ // tm, D // tn),
        in_specs=[yspec(POOL_W), yspec(CONV_W), yspec(ATTN_OUT), wspec(POOL_W), wspec(CONV_W), wspec(ATTN_OUT),
                  gspec(0), gspec(1), gspec(2), bspec(0), bspec(1), bspec(2)],
        out_specs=pl.BlockSpec((tm, tn), lambda i, j: (i, j)),
        out_shape=jax.ShapeDtypeStruct((S_, D), BF16),
        compiler_params=_cparams(("parallel", "parallel")),
    )(*ys, *ws, z, z, z, b_gate, b_gate, b_gate)


def _mix_bwd(name, z, b_gate, y, w, dmerged, branch, after=None):
    S_ = z.shape[0]
    tm, tn = MIX_TM, MIX_TN
    width = y.shape[1]
    gblk = OFF_GATE // tn + branch * (D // tn)
    nj = D // tn

    def body(y_ref, w_ref, g_ref, b_ref, dm_ref, dy_ref, dw_ref, dg_ref, db_ref, acc_ref):
        i, j = pl.program_id(0), pl.program_id(1)
        yb = y_ref[...].astype(BF16)
        u = lax.dot_general(yb, w_ref[...], _DIMS["nt"], preferred_element_type=F32)
        sg = _sigmoid(g_ref[...] + b_ref[...])
        dm = dm_ref[...]
        du = (sg * dm).astype(BF16)
        dpre = dm * u * sg * (1.0 - sg)
        dg_ref[...] = dpre.astype(BF16)
        cols = pl.ds(pl.multiple_of(j * tn, tn), tn)
        db_ref[...] = _rows8(dpre)
        d_w = lax.dot_general(du, yb, _DIMS["tn"], preferred_element_type=F32)
        d_y = jnp.dot(du, w_ref[...], preferred_element_type=F32)

        @pl.when(i == 0)
        def _():
            dw_ref[cols, :] = d_w

        @pl.when(i > 0)
        def _():
            dw_ref[cols, :] += d_w

        @pl.when(j == 0)
        def _():
            acc_ref[...] = d_y

        @pl.when(j > 0)
        def _():
            acc_ref[...] += d_y

        @pl.when(j == nj - 1)
        def _():
            dy_ref[...] = acc_ref[...]

    rows = pl.BlockSpec((tm, width), lambda i, j: (i, 0))
    blk = pl.BlockSpec((tm, tn), lambda i, j: (i, j))
    return _call(
        body, [y, w, z, b_gate, dmerged],
        [rows, pl.BlockSpec((tn, width), lambda i, j: (j, 0)), pl.BlockSpec((tm, tn), lambda i, j: (i, gblk + j)),
         pl.BlockSpec((1, tn), lambda i, j: (0, branch * nj + j)), blk], after,
        name=name, grid=(S_ // tm, nj),
        out_specs=(rows, pl.BlockSpec((D, width), lambda i, j: (0, 0)), blk, pl.BlockSpec((8, tn), lambda i, j: (i, j))),
        out_shape=(jax.ShapeDtypeStruct((S_, width), F32), jax.ShapeDtypeStruct((D, width), F32),
                   jax.ShapeDtypeStruct((S_, D), BF16), jax.ShapeDtypeStruct((S_ // tm * 8, D), F32)),
        scratch_shapes=[pltpu.VMEM((tm, width), F32)],
        compiler_params=_cparams(("arbitrary", "arbitrary")))


def _relu2_epilogue(acc):
    r = jnp.maximum(acc, 0.0)
    return acc, r * r


def _residual_norm_epilogue(acc, xr, gain):
    x = xr + acc
    ms = jnp.mean(x * x, axis=-1, keepdims=True)
    return x, x * lax.rsqrt(ms + EPS) * gain


def _fold_rows(name, part, lanes=False, scale=1.0):
    R, N = part.shape

    def body(p_ref, o_ref):
        s = jnp.sum(p_ref[...], axis=0, keepdims=True)
        if lanes:
            s = jnp.sum(s, axis=1, keepdims=True)
        o_ref[...] = s * scale

    out_n = 1 if lanes else N
    return pl.pallas_call(
        body, name=name, grid=(1,), in_specs=[pl.BlockSpec((R, N), lambda i: (0, 0))],
        out_specs=pl.BlockSpec((1, out_n), lambda i: (0, 0)), out_shape=jax.ShapeDtypeStruct((1, out_n), F32))(part)


def _rows8(v):
    return jnp.sum(v.reshape(v.shape[0] // 8, 8, v.shape[1]), axis=0)


def _residual_loss_epilogue(acc, xr, target):
    e = xr + acc - target
    return e * (1.0 / D), _rows8(e * e)


def _norm_bwd_epilogue(acc, x, dres, gain):
    rstd = lax.rsqrt(jnp.mean(x * x, axis=-1, keepdims=True) + EPS)
    xhat = x * rstd
    dxhat = acc * gain
    c = jnp.mean(dxhat * xhat, axis=-1, keepdims=True)
    return dres + rstd * (dxhat - xhat * c), _rows8(acc * xhat)


def _layer_fwd(l, x, h, p, next_gain, target=None):
    t = f"l{l}_"
    z = _matmul(t + "in_proj", h, p["w_in_t"], "nt", 512, 3712, 1024)
    y_pool = _pool_fwd(t + "pool", z, p["pool_mix"], p["pool_scale"])
    y_conv = _conv_fwd(t + "conv", z, p["conv_w"])
    os_, ls_ = [], []
    for g, (_, dil) in enumerate(ATTN_GROUPS):
        o_g, l_g = _attn_fwd(t + f"attn{g}", z, p["q_gain"], p["k_gain"], g, dil)
        os_.append(o_g)
        ls_.append(l_g)
    y_attn, lse = _attn_combine(t + "attn_mix", os_, ls_)
    merged = _mix_fwd(t + "merge", z, p["b_gate"], (y_pool, y_conv, y_attn),
                      (p["w_pool_up_t"], p["w_conv_out_t"], p["w_attn_up_t"]))
    x1, h2 = _matmul(t + "out_proj", merged, p["w_o"], "nn", 1024, D, 1024, out_dtypes=(F32, BF16),
                     extras=((x, "mn"), (p["norm_mlp"], "n")), epilogue=_residual_norm_epilogue)
    a, r = _matmul(t + "ff1", h2, p["w_ff1_t"], "nt", 1024, 1024, 1024, out_dtypes=(F32, BF16),
                   epilogue=_relu2_epilogue)
    if next_gain is None:
        x2, h_out = _matmul(t + "ff2", r, p["w_ff2"], "nn", 1024, D, 1024, out_dtypes=(F32, (F32, "rows8")),
                            extras=((x1, "mn"), (target, "mn")), epilogue=_residual_loss_epilogue)
    else:
        x2, h_out = _matmul(t + "ff2", r, p["w_ff2"], "nn", 1024, D, 1024, out_dtypes=(F32, BF16),
                            extras=((x1, "mn"), (next_gain, "n")), epilogue=_residual_norm_epilogue)
    saved = dict(x=x, h=h, z=z, y_pool=y_pool, y_conv=y_conv, y_attn=y_attn, lse=lse, merged=merged,
                 x1=x1, h2=h2, a=a, r=r)
    return x2, h_out, saved


def _layer_bwd(l, dx2, p, s, pending, collective_id, last, start_after=None):
    t = f"l{l}_b_"
    g = {}
    rest = jax.ShapeDtypeStruct((N_DEV, REST_ROWS, D), F32)
    da = _matmul(t + "d_ff2_in", dx2, p["w_ff2"], "nt", 1024, 1024, 1024, out_dtypes=(BF16,),
                 extras=((s["a"], "mn"),), epilogue=lambda acc, a: (acc * (2.0 * jnp.maximum(a, 0.0)),),
                 after=start_after)
    rest = _matmul(t + "dw_ff2", s["r"], dx2, "tn", 512, 1024, 4096, into=(rest, REST_FF2, 0), after=da)
    tok = rest if pending is None else pending.chip_sums(after=rest)
    dx1, part = _matmul(t + "d_ff1_in", da, p["w_ff1_t"], "nn", 1024, D, 1024, out_dtypes=(F32, (F32, "rows8")),
                        extras=((s["x1"], "mn"), (dx2, "mn"), (p["norm_mlp"], "n")), epilogue=_norm_bwd_epilogue,
                        after=tok)
    g["norm_mlp"] = _fold_rows(t + "d_norm_mlp", part)
    rest = _matmul(t + "dw_ff1", da, s["h2"], "tn", 512, 1024, 4096, into=(rest, REST_FF1, 0), after=dx1)
    dmerged = _matmul(t + "d_out_proj_in", dx1, p["w_o"], "nt", 1024, 1024, 1024)
    rest = _matmul(t + "dw_o", s["merged"], dx1, "tn", 128, 1024, 4096, into=(rest, REST_WO, 0), after=dmerged)
    ys = (s["y_pool"], s["y_conv"], s["y_attn"])
    names = ("w_pool_up_t", "w_conv_out_t", "w_attn_up_t")
    dys, dgates, dbs = [], [], []
    tok = rest
    for b in range(3):
        dy_b, dw_t, dgz, db = _mix_bwd(t + f"merge{b}", s["z"], p["b_gate"], ys[b], p[names[b]], dmerged, b,
                                       after=tok)
        tok = dy_b
        width = ys[b].shape[1]
        if b < 2:
            rest = rest.at[:, REST_UP:REST_ATTN, b * width:(b + 1) * width].set(
                dw_t.reshape(N_DEV, REST_ATTN - REST_UP, width))
        else:
            rest = rest.at[:, REST_ATTN:REST_ROWS, :].set(dw_t.reshape(N_DEV, REST_ROWS - REST_ATTN, D))
        dys.append(dy_b)
        dgates.append(dgz)
        dbs.append(_fold_rows(t + f"d_b_gate{b}", db))
    g["b_gate"] = jnp.concatenate(dbs, axis=1)
    rs_rest = _ReduceScatter(f"rs_rest{l}", collective_id, rest)
    pending_sum = None if pending is None else pending.finish(after=rest)
    du_pool, g["pool_mix"], g["pool_scale"] = _pool_bwd(t + "pool", s["z"], dys[0], p["pool_mix"], p["pool_scale"],
                                                        after=rest if pending is None else pending_sum[0])
    dcb, dcc, dcx, g["conv_w"] = _conv_bwd(t + "conv", s["z"], dys[1], p["conv_w"], after=du_pool)
    tok = rs_rest.chip_sums(after=dcb)
    dqs, dks, dvs = [], [], []
    gq = gk = None
    for gi, (_, dil) in enumerate(ATTN_GROUPS):
        dq, dk, dv, dgq, dgk = _attn_bwd(t + f"attn{gi}", s["z"], p["q_gain"], p["k_gain"], dys[2], s["y_attn"],
                                         s["lse"], gi, dil, after=tok)
        tok = dq
        dqs.append(dq)
        dks.append(dk)
        dvs.append(dv)
        gq = dgq if gq is None else gq + dgq
        gk = dgk if gk is None else gk + dgk
    g["q_gain"], g["k_gain"] = gq, gk
    rest_sum, _ = rs_rest.finish(after=tok)
    dz = jnp.concatenate([du_pool, dcb, dcc, dcx] + [v.astype(BF16) for v in dqs + dks + dvs] + dgates, axis=1)
    in_t = _matmul(t + "dw_in", dz, s["h"], "tn", 256, 1024, 4096, after=rest_sum)
    rs_in = _ReduceScatter(f"rs_in{l}", collective_id + 2, in_t.reshape(N_DEV, IN_ROWS, D))
    dx, part = _matmul(t + "d_in_proj_in", dz, p["w_in_t"], "nn", 512, D, 3712, out_dtypes=(F32, (F32, "rows8")),
                       extras=((s["x"], "mn"), (dx1, "mn"), (p["norm_mix"], "n")), epilogue=_norm_bwd_epilogue,
                       after=in_t)
    g["norm_mix"] = _fold_rows(t + "d_norm_mix", part)
    if last:
        rs_in.chip_sums(after=dx)
    return dx, g, rest_sum, pending_sum, rs_in


ANY = pl.BlockSpec(memory_space=pl.ANY)


def _mesh_pos():
    return lax.axis_index("x"), lax.axis_index("y"), lax.axis_index("c")


def _other_chips(x, y):
    return [(1 - x, y), (x, 1 - y), (1 - x, 1 - y)]


def _comm_call(name, collective_id, peers, body, arrs, out_shape, sem_counts):
    n_in, n_out = len(arrs), len(out_shape)
    if collective_id is None:
        def tc_body(*refs):
            body(refs[:n_in], refs[n_in:n_in + n_out], *refs[n_in + n_out:])

        return pl.pallas_call(
            tc_body, name=name, out_shape=tuple(out_shape), in_specs=[ANY] * n_in, out_specs=(ANY,) * n_out,
            scratch_shapes=[pltpu.SemaphoreType.DMA((n,)) for n in sem_counts])(*arrs)

    def seq_body(*refs):
        barrier = pltpu.get_barrier_semaphore()
        ps = peers(*_mesh_pos())
        for p in ps:
            pl.semaphore_signal(barrier, inc=1, device_id=p, device_id_type=MESH)
        pl.semaphore_wait(barrier, len(ps))
        body(refs[:n_in], refs[n_in:n_in + n_out], *refs[n_in + n_out:])

    return pl.kernel(
        seq_body, out_type=tuple(out_shape), mesh=plsc.ScalarSubcoreMesh(axis_name="seq", num_cores=1), name=name,
        scratch_types=[pltpu.SemaphoreType.DMA((n,)) for n in sem_counts],
        compiler_params=pltpu.CompilerParams(collective_id=collective_id),
    )(*arrs)


def _all_gather(name, collective_id, shard):
    R, C = shard.shape

    def peers(x, y, c):
        return [(x, y, 1 - c)] + [(*chip, c) for chip in _other_chips(x, y)]

    def body(in_refs, out_refs, send_sems, recv_sems, local_sems):
        (x_ref,), (out_ref,) = in_refs, out_refs
        x, y, c = _mesh_pos()
        me, sibling = (x, y, c), (x, y, 1 - c)
        chips = _other_chips(x, y)

        def slot(px, py, pc):
            return out_ref.at[4 * px + 2 * py + pc]

        def copy(k, block, to, src=None):
            return pltpu.make_async_remote_copy(
                src_ref=slot(*block) if src is None else src, dst_ref=slot(*block),
                send_sem=send_sems.at[k], recv_sem=recv_sems.at[k], device_id=to, device_id_type=MESH)

        mine = pltpu.make_async_copy(x_ref, slot(*me), local_sems.at[0])
        mine.start()
        first = [copy(0, me, sibling, src=x_ref)]
        first += [copy(1 + j, me, (*chip, c), src=x_ref) for j, chip in enumerate(chips)]
        for cp in first:
            cp.start()
        passed = [copy(4 + j, (*chip, c), sibling) for j, chip in enumerate(chips)]
        for j, chip in enumerate(chips):
            copy(1 + j, (*chip, c), me).wait_recv()
            passed[j].start()
        copy(0, sibling, me).wait_recv()
        for j, chip in enumerate(chips):
            copy(4 + j, (*chip, 1 - c), me).wait_recv()
        for cp in first + passed:
            cp.wait_send()
        mine.wait()

    return _comm_call(name, collective_id, peers, body, [shard],
                      [jax.ShapeDtypeStruct((N_DEV, R, C), shard.dtype)], (7, 7, 1))[0]


def _rs_sibling_exchange(name, collective_id, arrs):
    n = len(arrs)

    def body(in_refs, out_refs, send_sems, recv_sems):
        x, y, c = _mesh_pos()
        cps = []
        for k, (src, dst) in enumerate(zip(in_refs, out_refs)):
            src = src.at[:, 1 - c] if len(src.shape) == 4 else src
            cps.append(pltpu.make_async_remote_copy(src_ref=src, dst_ref=dst, send_sem=send_sems.at[k],
                                                    recv_sem=recv_sems.at[k], device_id=(x, y, 1 - c),
                                                    device_id_type=MESH))
        for cp in cps:
            cp.start()
        for cp in cps:
            cp.wait()

    out_shape = [jax.ShapeDtypeStruct(a.shape[:1] + a.shape[2:] if a.ndim == 4 else a.shape, a.dtype) for a in arrs]
    return _comm_call(name, collective_id, lambda x, y, c: [(x, y, 1 - c)], body, arrs, out_shape, (n, n))


def _rs_rows(r):
    return r // 2 if (r // 2) % 16 == 0 else r


def _drop_ref(body, idx):
    def wrapped(*refs):
        return body(*refs[:idx], *refs[idx + 1:])

    return wrapped


def _rs_chip_sum(name, ids, big, rbig, after=None):
    _, _, R, C = big.shape
    rows = _rs_rows(R)

    def body(ids_ref, mine_ref, theirs_ref, t16_ref, own_ref):
        p = pl.program_id(1)
        s = mine_ref[...] + theirs_ref[...]
        t16_ref[...] = s.astype(BF16)

        @pl.when(p == ids_ref[1])
        def _():
            own_ref[...] = s

    in_specs = [pl.BlockSpec((None, None, rows, C), lambda i, p, ids: (p, ids[0], i, 0)),
                pl.BlockSpec((None, rows, C), lambda i, p, ids: (p, i, 0))]
    args = [ids, big, rbig]
    if after is not None:
        body = _drop_ref(body, len(args))
        in_specs.append(ANY)
        args.append(after)
    grid_spec = pltpu.PrefetchScalarGridSpec(
        num_scalar_prefetch=1, grid=(R // rows, 4), in_specs=in_specs,
        out_specs=(pl.BlockSpec((None, rows, C), lambda i, p, ids: (p, i, 0)),
                   pl.BlockSpec((rows, C), lambda i, p, ids: (i, 0))))
    return pl.pallas_call(
        body, name=name, grid_spec=grid_spec,
        out_shape=(jax.ShapeDtypeStruct((4, R, C), BF16), jax.ShapeDtypeStruct((R, C), F32)),
        compiler_params=_cparams(("parallel", "arbitrary")),
    )(*args)


def _add2(name, a, b):
    def body(a_ref, b_ref, o_ref):
        o_ref[...] = a_ref[...] + b_ref[...]

    full = pl.BlockSpec(a.shape, lambda i: (0, 0))
    return pl.pallas_call(body, name=name, grid=(1,), in_specs=[full, full], out_specs=full,
                          out_shape=jax.ShapeDtypeStruct(a.shape, a.dtype))(a, b)


def _rs_chip_exchange(name, collective_id, arrs):
    n = len(arrs)

    def body(in_refs, out_refs, send_sems, recv_sems, local_sems):
        x, y, c = _mesh_pos()
        p_me = 2 * x + y
        chips = _other_chips(x, y)

        def part(ref, p):
            return ref.at[p] if len(ref.shape) == 3 else ref

        local = [pltpu.make_async_copy(part(src, p_me), dst.at[p_me], local_sems.at[k])
                 for k, (src, dst) in enumerate(zip(in_refs, out_refs))]
        for cp in local:
            cp.start()
        sends = []
        for j, (px, py) in enumerate(chips):
            for k, (src, dst) in enumerate(zip(in_refs, out_refs)):
                sends.append(pltpu.make_async_remote_copy(
                    src_ref=part(src, 2 * px + py), dst_ref=dst.at[p_me], send_sem=send_sems.at[n * j + k],
                    recv_sem=recv_sems.at[n * j + k], device_id=(px, py, c), device_id_type=MESH))
        for cp in sends:
            cp.start()
        for j, (px, py) in enumerate(chips):
            for k, (src, dst) in enumerate(zip(in_refs, out_refs)):
                pltpu.make_async_remote_copy(
                    src_ref=part(src, p_me), dst_ref=dst.at[2 * px + py], send_sem=send_sems.at[n * j + k],
                    recv_sem=recv_sems.at[n * j + k], device_id=(px, py, c), device_id_type=MESH).wait_recv()
        for cp in sends:
            cp.wait_send()
        for cp in local:
            cp.wait()

    out_shape = [jax.ShapeDtypeStruct((4,) + a.shape[-2:], a.dtype) for a in arrs]
    return _comm_call(name, collective_id, lambda x, y, c: [(*chip, c) for chip in _other_chips(x, y)], body, arrs,
                      out_shape, (3 * n, 3 * n, n))


def _rs_final_sum(name, ids, recv, own, rows, after=None):
    _, R, C = recv.shape
    assert R % rows == 0

    def body(ids_ref, r_ref, own_ref, o_ref):
        acc = None
        for p in range(4):
            term = jnp.where(ids_ref[1] == p, own_ref[...], r_ref[p].astype(F32))
            acc = term if acc is None else acc + term
        o_ref[...] = acc

    in_specs = [pl.BlockSpec((4, rows, C), lambda i, ids: (0, i, 0)), pl.BlockSpec((rows, C), lambda i, ids: (i, 0))]
    args = [ids, recv, own]
    if after is not None:
        body = _drop_ref(body, len(args))
        in_specs.append(ANY)
        args.append(after)
    grid_spec = pltpu.PrefetchScalarGridSpec(
        num_scalar_prefetch=1, grid=(R // rows,), in_specs=in_specs,
        out_specs=pl.BlockSpec((rows, C), lambda i, ids: (i, 0)))
    return pl.pallas_call(
        body, name=name, grid_spec=grid_spec, out_shape=jax.ShapeDtypeStruct((R, C), F32),
        compiler_params=_cparams(("parallel",)),
    )(*args)


class _ReduceScatter:
    def __init__(self, tag, collective_id, big, small=None):
        x, y, c = _mesh_pos()
        self.tag, self.cid, self.small = tag, collective_id, small
        self.ids = jnp.stack([c, 2 * x + y]).astype(jnp.int32)
        self.big = big.reshape((4, 2) + big.shape[1:])
        self.got = _rs_sibling_exchange(tag + "_sibling", collective_id,
                                        [self.big] + ([] if small is None else [small]))

    def chip_sums(self, after=None):
        t16, self.own = _rs_chip_sum(self.tag + "_chip_sum", self.ids, self.big, self.got[0], after)
        arrs = [t16]
        if self.small is not None:
            self.ts = _add2(self.tag + "_chip_sum_small", self.small, self.got[1])
            arrs.append(self.ts)
        self.recv = _rs_chip_exchange(self.tag + "_chips", self.cid + 1, arrs)
        return t16

    def finish(self, after=None):
        out = _rs_final_sum(self.tag + "_final", self.ids, self.recv[0], self.own, _rs_rows(self.own.shape[0]), after)
        out_small = None
        if self.small is not None:
            out_small = _rs_final_sum(self.tag + "_final_small", self.ids, self.recv[1], self.ts, self.small.shape[0])
        return out, out_small


def _all_reduce_small(tag, small):
    x, y, c = _mesh_pos()
    ids = jnp.stack([c, 2 * x + y]).astype(jnp.int32)
    (theirs,) = _rs_sibling_exchange(tag + "_sibling", None, [small])
    ts = _add2(tag + "_chip_sum", small, theirs)
    (recv,) = _rs_chip_exchange(tag + "_chips", None, [ts])
    return _rs_final_sum(tag + "_final", ids, recv, ts, small.shape[0])


def _adamw(name, w, g, m, v):
    R, C = w.shape
    tr = max(t for t in range(8, 513, 8) if R % t == 0)
    c1 = 1.0 - ADAM_B1 ** ADAM_STEP
    c2 = 1.0 - ADAM_B2 ** ADAM_STEP

    def body(w_ref, g_ref, m_ref, v_ref, d_ref, nm_ref, nv_ref):
        gv = g_ref[...]
        nm = ADAM_B1 * m_ref[...] + (1.0 - ADAM_B1) * gv
        nv = ADAM_B2 * v_ref[...] + (1.0 - ADAM_B2) * (gv * gv)
        d_ref[...] = -ADAM_LR * ((nm / c1) / (jnp.sqrt(nv / c2) + ADAM_EPS) + ADAM_WD * w_ref[...])
        nm_ref[...] = nm
        nv_ref[...] = nv

    blk = pl.BlockSpec((tr, C), lambda i: (i, 0))
    shp = jax.ShapeDtypeStruct((R, C), F32)
    return pl.pallas_call(
        body, name=name, grid=(R // tr,), in_specs=[blk] * 4, out_specs=(blk,) * 3, out_shape=(shp,) * 3,
        compiler_params=_cparams(("parallel",)),
    )(w, g, m, v)


def _adamw_t(name, w, g_t, m, v):
    L, R, C = w.shape
    tr = 256
    c1 = 1.0 - ADAM_B1 ** ADAM_STEP
    c2 = 1.0 - ADAM_B2 ** ADAM_STEP

    def body(w_ref, g_ref, m_ref, v_ref, go_ref, d_ref, nm_ref, nv_ref):
        gv = g_ref[...].T
        go_ref[...] = gv
        nm = ADAM_B1 * m_ref[...] + (1.0 - ADAM_B1) * gv
        nv = ADAM_B2 * v_ref[...] + (1.0 - ADAM_B2) * (gv * gv)
        d_ref[...] = -ADAM_LR * ((nm / c1) / (jnp.sqrt(nv / c2) + ADAM_EPS) + ADAM_WD * w_ref[...])
        nm_ref[...] = nm
        nv_ref[...] = nv

    blk = pl.BlockSpec((None, tr, C), lambda l, i: (l, i, 0))
    blk_t = pl.BlockSpec((None, C, tr), lambda l, i: (l, 0, i))
    shp = jax.ShapeDtypeStruct((L, R, C), F32)
    return pl.pallas_call(
        body, name=name, grid=(L, R // tr), in_specs=[blk, blk_t, blk, blk], out_specs=(blk,) * 4,
        out_shape=(shp,) * 4, compiler_params=_cparams(("parallel", "parallel")),
    )(w, g_t, m, v)


REST_FF1 = 0
REST_FF2 = 512
REST_WO = 1024
REST_UP = 1152
REST_ATTN = 1280
REST_ROWS = 1312
IN_ROWS = IN_COLS // N_DEV
CONV_BITS_ROWS = 16
SHARD_ROWS = IN_ROWS + REST_ROWS + CONV_BITS_ROWS
SMALL = (("norm_mix", (DEPTH, D)), ("b_gate", (DEPTH, 3 * D)), ("pool_mix", (DEPTH, 4, POOL_G, POOL_G)),
         ("pool_scale", (DEPTH, POOL_W)), ("conv_w", (DEPTH, 3, CONV_W)), ("q_gain", (DEPTH, HEAD)),
         ("k_gain", (DEPTH, HEAD)), ("norm_mlp", (DEPTH, D)))


def _pack_weight_shards(w, l):
    b = lambda a: a.astype(BF16)
    conv = lax.bitcast_convert_type(w["conv_w"][l], BF16).reshape(3, 128)
    conv = jnp.pad(conv, ((0, CONV_BITS_ROWS - 3), (0, D - 128)))
    rest = jnp.concatenate([
        b(w["w_ff1"][l].T), b(w["w_ff2"][l]), b(w["w_o"][l]),
        jnp.concatenate([b(w["w_pool_up"][l].T), b(w["w_conv_out"][l].T)], axis=1),
        b(w["w_attn_up"][l].T).reshape(REST_ROWS - REST_ATTN, D), conv], axis=0)
    return b(w["w_in"][l].T), rest


def _unpack_gathered(g_in, g_rest, small_w, l):
    rest = g_rest[:, :REST_ROWS]
    take = lambda r0, rows, c0=0, cols=D: rest[:, r0:r0 + rows, c0:c0 + cols].reshape(N_DEV * rows, cols)
    conv = g_rest[:, REST_ROWS:REST_ROWS + 3, :128].reshape(N_DEV, 3, CONV_W // N_DEV, 2)
    conv = lax.bitcast_convert_type(conv, F32)
    p = {
        "w_in_t": g_in.reshape(IN_COLS, D),
        "w_ff1_t": take(REST_FF1, 512), "w_ff2": take(REST_FF2, 512), "w_o": take(REST_WO, 128),
        "w_pool_up_t": take(REST_UP, 128, 0, POOL_W), "w_conv_out_t": take(REST_UP, 128, POOL_W, CONV_W),
        "w_attn_up_t": rest[:, REST_ATTN:].reshape(D, ATTN_OUT),
        "conv_w": jnp.transpose(conv, (1, 0, 2)).reshape(3, CONV_W),
        "pool_mix": small_w["pool_mix"][l],
    }
    for name in ("norm_mix", "b_gate", "pool_scale", "q_gain", "k_gain", "norm_mlp"):
        p[name] = small_w[name][l][None, :]
    return p


def _pack_small_grads(grads):
    flat = jnp.concatenate([jnp.stack([grads[l][name] for l in range(DEPTH)]).reshape(-1) for name, _ in SMALL])
    return jnp.pad(flat, (0, (-flat.shape[0]) % (8 * 128))).reshape(-1, 128)


def _unpack_grads(in_t, rest, small, dev):
    out = {
        "w_in_t": jnp.stack(in_t),
        "w_ff1_t": jnp.stack([a[REST_FF1:REST_FF1 + 512] for a in rest]),
        "w_ff2": jnp.stack([a[REST_FF2:REST_FF2 + 512] for a in rest]),
        "w_o": jnp.stack([a[REST_WO:REST_WO + 128] for a in rest]),
        "w_pool_up": jnp.stack([a[REST_UP:REST_UP + 128, :POOL_W].T for a in rest]),
        "w_conv_out": jnp.stack([a[REST_UP:REST_UP + 128, POOL_W:].T for a in rest]),
        "w_attn_up": jnp.stack([a[REST_ATTN:].reshape(D // N_DEV, ATTN_OUT).T for a in rest]),
    }
    flat = small.reshape(-1)
    off = 0
    for name, shp in SMALL:
        n = 1
        for s in shp:
            n *= s
        out[name] = flat[off:off + n].reshape(shp)
        off += n
    width = CONV_W // N_DEV
    out["conv_w"] = lax.dynamic_slice_in_dim(out["conv_w"], dev * width, width, axis=2)
    return out


def _pack_small(arrs):
    flat = jnp.concatenate([a.reshape(-1) for a in arrs])
    pad = (-flat.shape[0]) % (8 * 128)
    return jnp.pad(flat, (0, pad)).reshape(-1, 128)


def _unpack_small(packed, like):
    flat = packed.reshape(-1)
    out, off = [], 0
    for a in like:
        out.append(flat[off:off + a.size].reshape(a.shape))
        off += a.size
    return out


WEIGHTS = ("norm_mix", "w_in", "b_gate", "pool_mix", "pool_scale", "conv_w", "q_gain", "k_gain", "w_pool_up",
           "w_conv_out", "w_attn_up", "w_o", "norm_mlp", "w_ff1", "w_ff2")


def kernel(x, norm_mix, w_in, b_gate, pool_mix, pool_scale, conv_w, q_gain, k_gain, w_pool_up, w_conv_out, w_attn_up, w_o, norm_mlp, w_ff1, w_ff2, loss_target, m_norm_mix, m_w_in, m_b_gate, m_pool_mix, m_pool_scale, m_conv_w, m_q_gain, m_k_gain, m_w_pool_up, m_w_conv_out, m_w_attn_up, m_w_o, m_norm_mlp, m_w_ff1, m_w_ff2, v_norm_mix, v_w_in, v_b_gate, v_pool_mix, v_pool_scale, v_conv_w, v_q_gain, v_k_gain, v_w_pool_up, v_w_conv_out, v_w_attn_up, v_w_o, v_norm_mlp, v_w_ff1, v_w_ff2):
    w = dict(zip(WEIGHTS, (norm_mix, w_in, b_gate, pool_mix, pool_scale, conv_w, q_gain, k_gain, w_pool_up,
                           w_conv_out, w_attn_up, w_o, norm_mlp, w_ff1, w_ff2)))
    m = dict(zip(WEIGHTS, (m_norm_mix, m_w_in, m_b_gate, m_pool_mix, m_pool_scale, m_conv_w, m_q_gain, m_k_gain,
                           m_w_pool_up, m_w_conv_out, m_w_attn_up, m_w_o, m_norm_mlp, m_w_ff1, m_w_ff2)))
    v = dict(zip(WEIGHTS, (v_norm_mix, v_w_in, v_b_gate, v_pool_mix, v_pool_scale, v_conv_w, v_q_gain, v_k_gain,
                           v_w_pool_up, v_w_conv_out, v_w_attn_up, v_w_o, v_norm_mlp, v_w_ff1, v_w_ff2)))
    xi, yi, ci = _mesh_pos()
    dev = 4 * xi + 2 * yi + ci

    params = []
    for l in range(DEPTH):
        s_in, s_rest = _pack_weight_shards(w, l)
        g_in = _all_gather(f"gather_in{l}", 1 + 2 * l, s_in)
        g_rest = _all_gather(f"gather_rest{l}", 2 + 2 * l, s_rest)
        params.append(_unpack_gathered(g_in, g_rest, w, l))
    saved = []
    act = x[0]
    h = _rmsnorm_fwd("l0_norm_mix", act, params[0]["norm_mix"])
    for l in range(DEPTH):
        act, h, s = _layer_fwd(l, act, h, params[l], params[l + 1]["norm_mix"] if l + 1 < DEPTH else None,
                               loss_target[0])
        saved.append(s)
    dy = act
    loss = _fold_rows("loss_sum", h, lanes=True, scale=0.5 / D)
    total = lax.psum(loss[0, 0], ("x", "y", "c"))
    in_t, rest, grads, pending = [None] * DEPTH, [None] * DEPTH, [None] * DEPTH, None
    start = total.reshape(1, 1)
    for l in reversed(range(DEPTH)):
        dy, grads[l], rest[l], done, pending = _layer_bwd(l, dy, params[l], saved[l], pending, 5 + 4 * l, l == 0,
                                                          start)
        start = None
        if done is not None:
            in_t[l + 1] = done[0]
    dx = dy
    g_small = _all_reduce_small("ar_small", _pack_small_grads(grads))
    in_t[0], _ = pending.finish(after=g_small)
    g = _unpack_grads(in_t, rest, g_small, dev)

    delta, new_m, new_v = {}, {}, {}
    for name in ("w_in", "w_ff1"):
        t = lambda a: jnp.swapaxes(a, 1, 2)
        g_t = g[name + "_t"]
        two_d = (g_t.shape[0] * g_t.shape[1], g_t.shape[2])
        d_, m_, v_ = _adamw("adamw_" + name, t(w[name]).reshape(two_d), g_t.reshape(two_d),
                            t(m[name]).reshape(two_d), t(v[name]).reshape(two_d))
        g[name] = t(g_t)
        delta[name], new_m[name], new_v[name] = (t(a.reshape(g_t.shape)) for a in (d_, m_, v_))
    for name in ("w_pool_up", "w_conv_out", "w_attn_up", "w_o", "w_ff2"):
        shp = w[name].shape
        two_d = (shp[0] * shp[1], shp[2])
        d_, m_, v_ = _adamw("adamw_" + name, w[name].reshape(two_d), g[name].reshape(two_d),
                            m[name].reshape(two_d), v[name].reshape(two_d))
        delta[name], new_m[name], new_v[name] = d_.reshape(shp), m_.reshape(shp), v_.reshape(shp)
    small_names = [name for name, _ in SMALL]
    packed = [_pack_small([src[name] for name in small_names]) for src in (w, g, m, v)]
    outs = _adamw("adamw_small", *packed)
    for dst, arr in zip((delta, new_m, new_v), outs):
        for name, val in zip(small_names, _unpack_small(arr, [w[name] for name in small_names])):
            dst[name] = val

    return (total, dx[None], *[g[n] for n in WEIGHTS], *[delta[n] for n in WEIGHTS],
            *[new_m[n] for n in WEIGHTS], *[new_v[n] for n in WEIGHTS])
```

```python
import functools

import jax
import jax.numpy as jnp
from jax import lax
from jax.experimental import pallas as pl
from jax.experimental.pallas import tpu as pltpu
from jax.experimental.pallas import tpu_sc as plsc

F32 = jnp.float32
BF16 = jnp.bfloat16
MESH = pl.DeviceIdType.MESH

D = 1024
SEQ = 4096
DEPTH = 2
N_DEV = 8
POOL_WINDOWS = (2, 4, 8, 16)
POOL_W = 512
POOL_G = 128
CONV_W = 512
HEAD = 64
ATTN_GROUPS = ((128, 1), (512, 4), (2048, 16))
HPG = 4
ATTN_W = 768
ATTN_OUT = 256
ATTN_BLK = 128
D_FF = 4096
EPS = 1e-6
MASK = -1e30
OFF_POOL = 0
OFF_CB = 512
OFF_CC = 1024
OFF_CX = 1536
OFF_Q = 2048
OFF_K = 2816
OFF_V = 3584
OFF_GATE = 4352
IN_COLS = 7424
ADAM_LR = 0.001
ADAM_B1 = 0.9
ADAM_B2 = 0.999
ADAM_EPS = 1e-08
ADAM_WD = 0.01
ADAM_STEP = 10

ROW_TILE = 512
SEQ_CHUNK = 256
HALO = 16
VMEM_LIMIT = 56 * 1024 * 1024


def _cparams(sem=None):
    return pltpu.CompilerParams(dimension_semantics=sem, vmem_limit_bytes=VMEM_LIMIT)


def _call(body, args, in_specs, after=None, **kw):
    if after is not None:
        n, inner = len(args), body

        def body(*refs):
            return inner(*refs[:n], *refs[n + 1:])

        args = list(args) + [after]
        in_specs = list(in_specs) + [pl.BlockSpec(memory_space=pl.ANY)]
    return pl.pallas_call(body, in_specs=in_specs, **kw)(*args)


_DIMS = {"nn": (((1,), (0,)), ((), ())), "nt": (((1,), (1,)), ((), ())), "tn": (((0,), (0,)), ((), ()))}


def _matmul(name, a, b, mode, tm, tn, tk, out_dtypes=(F32,), extras=(), epilogue=None, into=None, after=None):
    if mode == "tn":
        K, M = a.shape
    else:
        M, K = a.shape
    N = b.shape[0] if mode == "nt" else b.shape[1]
    assert M % tm == 0 and N % tn == 0 and K % tk == 0, (name, M, N, K, tm, tn, tk)
    nk = K // tk
    n_extra = len(extras)
    n_out = len(out_dtypes)
    dims = _DIMS[mode]
    n_alias = 0 if into is None or isinstance(into[0], jax.ShapeDtypeStruct) else 1

    def body(a_ref, b_ref, *rest):
        extra_refs = rest[:n_extra]
        out_refs = rest[n_extra + n_alias:n_extra + n_alias + n_out]

        def finish(acc):
            if epilogue is None:
                res = (acc,)
            else:
                res = epilogue(acc, *[r[...] for r in extra_refs])
            for o_ref, v in zip(out_refs, res):
                o_ref[...] = v.astype(o_ref.dtype)

        part = lax.dot_general(a_ref[...].astype(BF16), b_ref[...].astype(BF16), dims,
                               preferred_element_type=F32)
        if nk == 1:
            finish(part)
        else:
            acc_ref = rest[-1]
            k = pl.program_id(2)

            @pl.when(k == 0)
            def _():
                acc_ref[...] = part

            @pl.when(k > 0)
            def _():
                acc_ref[...] += part

            @pl.when(k == nk - 1)
            def _():
                finish(acc_ref[...])

    if mode == "tn":
        a_spec = pl.BlockSpec((tk, tm), lambda i, j, k: (k, i))
    else:
        a_spec = pl.BlockSpec((tm, tk), lambda i, j, k: (i, k))
    if mode == "nt":
        b_spec = pl.BlockSpec((tn, tk), lambda i, j, k: (j, k))
    else:
        b_spec = pl.BlockSpec((tk, tn), lambda i, j, k: (k, j))
    in_specs = [a_spec, b_spec]
    args = [a, b]
    for arr, kind in extras:
        if kind == "mn":
            in_specs.append(pl.BlockSpec((tm, tn), lambda i, j, k: (i, j)))
        else:
            in_specs.append(pl.BlockSpec((1, tn), lambda i, j, k: (0, j)))
        args.append(arr)
    part = [isinstance(dt, tuple) for dt in out_dtypes]
    out_shape = tuple(jax.ShapeDtypeStruct((M // tm * 8, N), dt[0]) if p else jax.ShapeDtypeStruct((M, N), dt)
                      for dt, p in zip(out_dtypes, part))
    out_specs = tuple(pl.BlockSpec((8 if p else tm, tn), lambda i, j, k: (i, j)) for p in part)
    aliases = {}
    if into is not None:
        buf, row0, col0 = into
        assert n_out == 1 and (M // N_DEV) % tm == 0 and row0 % tm == 0 and col0 % tn == 0
        per_dev = M // N_DEV // tm
        out_shape = (jax.ShapeDtypeStruct(buf.shape, buf.dtype),)
        out_specs = (pl.BlockSpec((None, tm, tn), lambda i, j, k: (i // per_dev, row0 // tm + i % per_dev,
                                                                   col0 // tn + j)),)
        if not isinstance(buf, jax.ShapeDtypeStruct):
            aliases = {len(args): 0}
            in_specs.append(pl.BlockSpec(memory_space=pl.ANY))
            args.append(buf)
    scratch = [] if nk == 1 else [pltpu.VMEM((tm, tn), F32)]
    res = _call(
        body, args, in_specs, after, name=name, grid=(M // tm, N // tn, nk), out_specs=out_specs,
        out_shape=out_shape, scratch_shapes=scratch, input_output_aliases=aliases,
        compiler_params=_cparams(("parallel", "parallel", "arbitrary")))
    return res if n_out > 1 else res[0]


def _rmsnorm_fwd(name, x, gain):
    S_, D_ = x.shape

    def body(x_ref, g_ref, h_ref):
        xf = x_ref[...]
        ms = jnp.mean(xf * xf, axis=-1, keepdims=True)
        h_ref[...] = (xf * lax.rsqrt(ms + EPS) * g_ref[...]).astype(BF16)

    return pl.pallas_call(
        body, name=name, grid=(S_ // ROW_TILE,),
        in_specs=[pl.BlockSpec((ROW_TILE, D_), lambda i: (i, 0)), pl.BlockSpec((1, D_), lambda i: (0, 0))],
        out_specs=pl.BlockSpec((ROW_TILE, D_), lambda i: (i, 0)),
        out_shape=jax.ShapeDtypeStruct((S_, D_), BF16),
        compiler_params=_cparams(("parallel",)),
    )(x, gain)


def _rmsnorm_bwd(name, x, gain, dh, dres, after=None):
    S_, D_ = x.shape
    n = S_ // ROW_TILE

    def body(x_ref, g_ref, dh_ref, dres_ref, dx_ref, dg_ref, acc_ref):
        i = pl.program_id(0)
        xf = x_ref[...]
        rstd = lax.rsqrt(jnp.mean(xf * xf, axis=-1, keepdims=True) + EPS)
        xhat = xf * rstd
        dhv = dh_ref[...]
        dxhat = dhv * g_ref[...]
        c = jnp.mean(dxhat * xhat, axis=-1, keepdims=True)
        dx_ref[...] = dres_ref[...] + rstd * (dxhat - xhat * c)
        part = jnp.sum((dhv * xhat).reshape(ROW_TILE // 8, 8, D_), axis=0)

        @pl.when(i == 0)
        def _():
            acc_ref[...] = part

        @pl.when(i > 0)
        def _():
            acc_ref[...] += part

        @pl.when(i == n - 1)
        def _():
            dg_ref[...] = jnp.sum(acc_ref[...], axis=0, keepdims=True)

    row = pl.BlockSpec((ROW_TILE, D_), lambda i: (i, 0))
    vec = pl.BlockSpec((1, D_), lambda i: (0, 0))
    return _call(
        body, [x, gain, dh, dres], [row, vec, row, row], after, name=name, grid=(n,), out_specs=(row, vec),
        out_shape=(jax.ShapeDtypeStruct((S_, D_), F32), jax.ShapeDtypeStruct((1, D_), F32)),
        scratch_shapes=[pltpu.VMEM((8, D_), F32)],
        compiler_params=_cparams(("arbitrary",)))


def _loss_head(name, y, target):
    S_, D_ = y.shape
    n = S_ // ROW_TILE

    def body(y_ref, t_ref, dy_ref, l_ref, acc_ref):
        i = pl.program_id(0)
        e = y_ref[...] - t_ref[...]
        dy_ref[...] = e * (1.0 / D_)
        part = jnp.sum((e * e).reshape(ROW_TILE // 8, 8, D_), axis=0)

        @pl.when(i == 0)
        def _():
            acc_ref[...] = part

        @pl.when(i > 0)
        def _():
            acc_ref[...] += part

        @pl.when(i == n - 1)
        def _():
            s = jnp.sum(acc_ref[...], axis=1, keepdims=True)
            l_ref[...] = jnp.sum(s, axis=0, keepdims=True) * (0.5 / D_)

    row = pl.BlockSpec((ROW_TILE, D_), lambda i: (i, 0))
    return pl.pallas_call(
        body, name=name, grid=(n,), in_specs=[row, row],
        out_specs=(row, pl.BlockSpec((1, 1), lambda i: (0, 0))),
        out_shape=(jax.ShapeDtypeStruct((S_, D_), F32), jax.ShapeDtypeStruct((1, 1), F32)),
        scratch_shapes=[pltpu.VMEM((8, D_), F32)],
        compiler_params=_cparams(("arbitrary",)),
    )(y, target)


def _rows_with_halo(ref, cols, i, n_chunks, before, after):
    r0 = pl.multiple_of(i * SEQ_CHUNK, SEQ_CHUNK)
    parts = []
    if before:
        h0 = pl.multiple_of(jnp.maximum(r0 - HALO, 0), 8)
        halo = ref[pl.ds(h0, HALO), cols]
        parts.append(jnp.where(i > 0, halo, jnp.zeros_like(halo)))
    parts.append(ref[pl.ds(r0, SEQ_CHUNK), cols])
    if after:
        a0 = pl.multiple_of(jnp.minimum(r0 + SEQ_CHUNK, (n_chunks - 1) * SEQ_CHUNK + SEQ_CHUNK - HALO), 8)
        halo = ref[pl.ds(a0, HALO), cols]
        parts.append(jnp.where(i < n_chunks - 1, halo, jnp.zeros_like(halo)))
    return parts[0] if len(parts) == 1 else jnp.concatenate(parts, axis=0)


def _shift_down(v, k):
    return pltpu.roll(v, k, 0)


def _shift_up(v, k):
    return pltpu.roll(v, v.shape[0] - k, 0)


def _pool_diff(xx, w, t_main):
    s = xx
    k = 1
    while k < w:
        s = s + _shift_down(s, k)
        k *= 2
    cnt = jnp.minimum(t_main + 1, w).astype(F32)
    return s[HALO:] / cnt - xx[HALO:]


def _pool_fwd(name, z, pool_mix, pool_scale):
    S_ = z.shape[0]
    n_chunks = S_ // SEQ_CHUNK

    cols = slice(0, POOL_G)

    def body(u_ref, mix_ref, sc_ref, y_ref):
        mixg = mix_ref[...].astype(BF16)
        scg = sc_ref[...]
        for g, w in enumerate(POOL_WINDOWS):
            @pl.when(pl.program_id(0) == g)
            def _(w=w):
                def chunk(i, carry):
                    r0 = pl.multiple_of(i * SEQ_CHUNK, SEQ_CHUNK)
                    xx = _rows_with_halo(u_ref, cols, i, n_chunks, True, False)
                    t = r0 + lax.broadcasted_iota(jnp.int32, (SEQ_CHUNK, POOL_G), 0)
                    d = _pool_diff(xx, w, t)
                    y = jnp.dot(d.astype(BF16), mixg, preferred_element_type=F32) * scg
                    y_ref[pl.ds(r0, SEQ_CHUNK), :] = y.astype(BF16)
                    return carry

                lax.fori_loop(0, n_chunks, chunk, 0)

    slab = pl.BlockSpec((S_, POOL_G), lambda g: (0, g))
    return pl.pallas_call(
        body, name=name, grid=(len(POOL_WINDOWS),),
        in_specs=[slab, pl.BlockSpec((None, POOL_G, POOL_G), lambda g: (g, 0, 0)),
                  pl.BlockSpec((1, POOL_G), lambda g: (0, g))],
        out_specs=slab, out_shape=jax.ShapeDtypeStruct((S_, POOL_W), BF16),
        compiler_params=_cparams(("parallel",)),
    )(z, pool_mix, pool_scale)


def _pool_bwd(name, z, dy, pool_mix, pool_scale, dz, after=None):
    S_ = z.shape[0]
    n_chunks = S_ // SEQ_CHUNK
    rows_a = SEQ_CHUNK + HALO

    cols = slice(0, POOL_G)

    def body(u_ref, dy_ref, mix_ref, sc_ref, du_ref, dmix_ref, dsc_ref):
        mixg = mix_ref[...].astype(BF16)
        scg = sc_ref[...]
        for g, w in enumerate(POOL_WINDOWS):
            @pl.when(pl.program_id(0) == g)
            def _(w=w):
                def chunk(i, carry):
                    dmix_acc, dsc_acc = carry
                    r0 = pl.multiple_of(i * SEQ_CHUNK, SEQ_CHUNK)
                    xx = _rows_with_halo(u_ref, cols, i, n_chunks, True, False)
                    t = r0 + lax.broadcasted_iota(jnp.int32, (SEQ_CHUNK, POOL_G), 0)
                    d = _pool_diff(xx, w, t).astype(BF16)
                    ypre = jnp.dot(d, mixg, preferred_element_type=F32)
                    dyy = _rows_with_halo(dy_ref, cols, i, n_chunks, False, True)
                    dys = (dyy * scg).astype(BF16)
                    dsc_acc = dsc_acc + jnp.sum((dyy[:SEQ_CHUNK] * ypre).reshape(SEQ_CHUNK // 8, 8, POOL_G), axis=0)
                    dmix_acc = dmix_acc + lax.dot_general(d, dys[:SEQ_CHUNK], _DIMS["tn"],
                                                          preferred_element_type=F32)
                    dd = lax.dot_general(dys, mixg, _DIMS["nt"], preferred_element_type=F32)
                    ta = r0 + lax.broadcasted_iota(jnp.int32, (rows_a, POOL_G), 0)
                    f = dd / jnp.minimum(ta + 1, w).astype(F32)
                    k = 1
                    while k < w:
                        f = f + _shift_up(f, k)
                        k *= 2
                    du_ref[pl.ds(r0, SEQ_CHUNK), :] = (f[:SEQ_CHUNK] - dd[:SEQ_CHUNK]).astype(BF16)
                    return dmix_acc, dsc_acc

                dmix_acc, dsc_acc = lax.fori_loop(
                    0, n_chunks, chunk, (jnp.zeros((POOL_G, POOL_G), F32), jnp.zeros((8, POOL_G), F32)))
                dmix_ref[...] = dmix_acc
                dsc_ref[...] = jnp.sum(dsc_acc, axis=0, keepdims=True)

    slab = pl.BlockSpec((S_, POOL_G), lambda g: (0, g))
    mix_spec = pl.BlockSpec((None, POOL_G, POOL_G), lambda g: (g, 0, 0))
    vec = pl.BlockSpec((1, POOL_G), lambda g: (0, g))
    return _call(
        _drop_ref(body, 4), [z, dy, pool_mix, pool_scale, dz], [slab, slab, mix_spec, vec, ANY], after, name=name,
        grid=(len(POOL_WINDOWS),), out_specs=(slab, mix_spec, vec),
        out_shape=(jax.ShapeDtypeStruct(dz.shape, dz.dtype), jax.ShapeDtypeStruct((4, POOL_G, POOL_G), F32),
                   jax.ShapeDtypeStruct((1, POOL_W), F32)),
        input_output_aliases={4: 0}, compiler_params=_cparams(("parallel",)))


def _conv_specs(S_):
    slab = lambda off: pl.BlockSpec((S_, 128), lambda c, off=off: (0, off // 128 + c))
    return slab(OFF_CB), slab(OFF_CC), slab(OFF_CX)


def _conv_fwd(name, z, conv_w):
    S_ = z.shape[0]
    n_chunks = S_ // SEQ_CHUNK
    col = slice(0, 128)

    def body(b_ref, c_ref, x_ref, w_ref, y_ref):
        w0, w1, w2 = w_ref[0:1, :], w_ref[1:2, :], w_ref[2:3, :]

        def chunk(i, carry):
            r0 = pl.multiple_of(i * SEQ_CHUNK, SEQ_CHUNK)
            u = _rows_with_halo(c_ref, col, i, n_chunks, True, False) * _rows_with_halo(x_ref, col, i, n_chunks, True, False)
            y = w2 * u + w1 * _shift_down(u, 1) + w0 * _shift_down(u, 2)
            y_ref[pl.ds(r0, SEQ_CHUNK), :] = (b_ref[pl.ds(r0, SEQ_CHUNK), :] * y[HALO:]).astype(BF16)
            return carry

        lax.fori_loop(0, n_chunks, chunk, 0)

    sb, sc, sx = _conv_specs(S_)
    return pl.pallas_call(
        body, name=name, grid=(CONV_W // 128,),
        in_specs=[sb, sc, sx, pl.BlockSpec((3, 128), lambda c: (0, c))],
        out_specs=pl.BlockSpec((S_, 128), lambda c: (0, c)),
        out_shape=jax.ShapeDtypeStruct((S_, CONV_W), BF16),
        compiler_params=_cparams(("parallel",)),
    )(z, z, z, conv_w)


def _conv_bwd(name, z, dy, conv_w, after=None):
    S_ = z.shape[0]
    n_chunks = S_ // SEQ_CHUNK
    col = slice(0, 128)
    lo, hi = HALO, HALO + SEQ_CHUNK

    def body(b_ref, c_ref, x_ref, dy_ref, w_ref, db_ref, dc_ref, dx_ref, dw_ref):
        w0, w1, w2 = w_ref[0:1, :], w_ref[1:2, :], w_ref[2:3, :]

        def chunk(i, carry):
            a0, a1, a2 = carry
            r0 = pl.multiple_of(i * SEQ_CHUNK, SEQ_CHUNK)
            cc = _rows_with_halo(c_ref, col, i, n_chunks, True, True)
            xx = _rows_with_halo(x_ref, col, i, n_chunks, True, True)
            bb = _rows_with_halo(b_ref, col, i, n_chunks, True, True)
            dyy = _rows_with_halo(dy_ref, col, i, n_chunks, True, True)
            u = cc * xx
            u1 = _shift_down(u, 1)
            u2 = _shift_down(u, 2)
            y = w2 * u + w1 * u1 + w0 * u2
            dyv = dyy * bb
            du = w2 * dyv + w1 * _shift_up(dyv, 1) + w0 * _shift_up(dyv, 2)
            db_ref[pl.ds(r0, SEQ_CHUNK), :] = (dyy[lo:hi] * y[lo:hi]).astype(BF16)
            dc_ref[pl.ds(r0, SEQ_CHUNK), :] = (du[lo:hi] * xx[lo:hi]).astype(BF16)
            dx_ref[pl.ds(r0, SEQ_CHUNK), :] = (du[lo:hi] * cc[lo:hi]).astype(BF16)
            red = lambda v: jnp.sum(v.reshape(SEQ_CHUNK // 8, 8, 128), axis=0)
            dm = dyv[lo:hi]
            return a0 + red(dm * u2[lo:hi]), a1 + red(dm * u1[lo:hi]), a2 + red(dm * u[lo:hi])

        zero = jnp.zeros((8, 128), F32)
        a0, a1, a2 = lax.fori_loop(0, n_chunks, chunk, (zero, zero, zero))
        dw_ref[0:1, :] = jnp.sum(a0, axis=0, keepdims=True)
        dw_ref[1:2, :] = jnp.sum(a1, axis=0, keepdims=True)
        dw_ref[2:3, :] = jnp.sum(a2, axis=0, keepdims=True)

    sb, sc, sx = _conv_specs(S_)
    slab = pl.BlockSpec((S_, 128), lambda c: (0, c))
    wspec = pl.BlockSpec((3, 128), lambda c: (0, c))
    act = jax.ShapeDtypeStruct((S_, CONV_W), BF16)
    return _call(
        body, [z, z, z, dy, conv_w], [sb, sc, sx, slab, wspec], after, name=name, grid=(CONV_W // 128,),
        out_specs=(slab, slab, slab, wspec),
        out_shape=(act, act, act, jax.ShapeDtypeStruct((3, CONV_W), F32)),
        compiler_params=_cparams(("parallel",)))


def _head_ones(pw):
    a = lax.broadcasted_iota(jnp.int32, (pw, pw), 0) // HEAD
    b = lax.broadcasted_iota(jnp.int32, (pw, pw), 1) // HEAD
    return (a == b).astype(BF16)


def _head_sum(v, ones):
    hi = v.astype(BF16)
    lo = (v - hi.astype(F32)).astype(BF16)
    return jnp.dot(hi, ones, preferred_element_type=F32) + jnp.dot(lo, ones, preferred_element_type=F32)


def _head_norm(v, gain, ones):
    rstd = lax.rsqrt(_head_sum(v * v, ones) * (1.0 / HEAD) + EPS)
    xhat = v * rstd
    return xhat * gain, xhat, rstd


def _head_norm_bwd(dy, xhat, rstd, gain, ones):
    dxhat = dy * gain
    c = _head_sum(dxhat * xhat, ones) * (1.0 / HEAD)
    dv = rstd * (dxhat - xhat * c)
    dg = jnp.sum((dy * xhat).reshape(dy.shape[0] // 8, 8, dy.shape[1]), axis=0)
    return dv, dg


def _head_masks(pw):
    lane_head = lax.broadcasted_iota(jnp.int32, (1, pw), 1) // HEAD
    return [lane_head == h for h in range(pw // HEAD)]


def _only(mask, v):
    return jnp.where(mask, v, jnp.zeros_like(v))


def _attn_specs(S_, g, dil):
    rows = ATTN_BLK * dil
    nb = S_ // rows
    pw = 128 if dil > 1 else ATTN_OUT
    cq, ck, cv = ((OFF_Q + g * ATTN_OUT) // pw, (OFF_K + g * ATTN_OUT) // pw, (OFF_V + g * ATTN_OUT) // pw)
    return rows, nb, pw, pw // HEAD, ATTN_OUT // pw, cq, ck, cv


ATTN_BATCH = 4


def _attn_group(dil):
    return 4 if dil == 1 else 1


def _attn_block_specs(rows, grp, pw, last=None):
    step = (lambda n: n) if last is None else (lambda n: jnp.minimum(n, last))
    cur = lambda c: pl.BlockSpec((rows * grp, pw), lambda hp, n, c=c: (step(n), c + hp))
    prev = lambda c: pl.BlockSpec((rows, pw), lambda hp, n, c=c: (jnp.maximum(step(n) * grp - 1, 0), c + hp))
    return cur, prev


def _band_mask(has_prev):
    qi = lax.broadcasted_iota(jnp.int32, (ATTN_BLK, 2 * ATTN_BLK), 0)
    ki = lax.broadcasted_iota(jnp.int32, (ATTN_BLK, 2 * ATTN_BLK), 1)
    in_prev = jnp.logical_and(ki < ATTN_BLK, ki >= qi)
    if has_prev is not True:
        in_prev = jnp.logical_and(in_prev, has_prev)
    return jnp.logical_or(in_prev, jnp.logical_and(ki >= ATTN_BLK, ki - ATTN_BLK <= qi))


def _rows_of(ref, r, dil):
    if dil == 1:
        return ref[r * ATTN_BLK:(r + 1) * ATTN_BLK, :]
    return ref[pl.ds(r, ATTN_BLK, stride=dil), :]


def _put_rows(ref, r, dil, val):
    if dil == 1:
        ref[r * ATTN_BLK:(r + 1) * ATTN_BLK, :] = val.astype(ref.dtype)
    else:
        ref[pl.ds(r, ATTN_BLK, stride=dil), :] = val.astype(ref.dtype)


def _attn_fwd(name, z, q_gain, k_gain, g, dil):
    S_ = z.shape[0]
    rows, nb, pw, heads, npairs, cq, ck, cv = _attn_specs(S_, g, dil)
    scale = HEAD ** -0.5

    grp = _attn_group(dil)
    nsteps = nb // grp

    def body(q_ref, k_ref, kp_ref, v_ref, vp_ref, gq_ref, gk_ref, o_ref, l_ref):
        n = pl.program_id(1)
        ones, hmask = _head_ones(pw), _head_masks(pw)
        gq, gk = jnp.tile(gq_ref[...], (1, heads)), jnp.tile(gk_ref[...], (1, heads))
        mask_first, mask_rest = _band_mask(n > 0), _band_mask(True)
        for r0 in range(0, dil * grp, ATTN_BATCH):
            rs = range(r0, min(r0 + ATTN_BATCH, dil * grp))
            qn, kn, vv, s, p = {}, {}, {}, {}, {}
            kcn = {}
            for r in rs:
                q, kc, vc = _rows_of(q_ref, r, dil), _rows_of(k_ref, r, dil), _rows_of(v_ref, r, dil)
                kcn[r] = _head_norm(kc, gk, ones)[0]
                if dil == 1 and r > 0:
                    kpn = kcn[r - 1] if r - 1 in kcn else _head_norm(_rows_of(k_ref, r - 1, dil), gk, ones)[0]
                    vp = _rows_of(v_ref, r - 1, dil)
                else:
                    kpn, vp = _head_norm(_rows_of(kp_ref, r, dil), gk, ones)[0], _rows_of(vp_ref, r, dil)
                qn[r] = _head_norm(q, gq, ones)[0].astype(BF16)
                kn[r] = jnp.concatenate([kpn, kcn[r]], axis=0).astype(BF16)
                vv[r] = jnp.concatenate([vp, vc], axis=0).astype(BF16)
            keys = [(r, h) for r in rs for h in range(heads)]
            for r, h in keys:
                s[r, h] = lax.dot_general(_only(hmask[h], qn[r]), kn[r], _DIMS["nt"],
                                          preferred_element_type=F32) * scale
            lse, den = {}, {}
            for key in keys:
                mask = mask_rest if (dil == 1 and key[0] > 0) else mask_first
                sm = jnp.where(mask, s[key], MASK)
                m = jnp.max(sm, axis=-1, keepdims=True)
                e = jnp.exp(sm - m)
                den[key] = jnp.sum(e, axis=-1, keepdims=True)
                p[key] = e.astype(BF16)
                lse[key] = m + jnp.log(den[key])
            for r in rs:
                out = jnp.zeros((ATTN_BLK, pw), F32)
                lse_all = jnp.zeros((ATTN_BLK, pw), F32)
                for h in range(heads):
                    out = jnp.where(hmask[h], jnp.dot(p[r, h], vv[r], preferred_element_type=F32) / den[r, h], out)
                    lse_all = jnp.where(hmask[h], lse[r, h], lse_all)
                _put_rows(o_ref, r, dil, out)
                _put_rows(l_ref, r, dil, lse_all)

    cur, prev = _attn_block_specs(rows, grp, pw)
    gspec = pl.BlockSpec((1, HEAD), lambda hp, n: (0, 0))
    shp = jax.ShapeDtypeStruct((S_, ATTN_OUT), F32)
    return pl.pallas_call(
        body, name=name, grid=(npairs, nsteps),
        in_specs=[cur(cq), cur(ck), prev(ck), cur(cv), prev(cv), gspec, gspec],
        out_specs=(cur(0), cur(0)), out_shape=(shp, shp),
        compiler_params=_cparams(("parallel", "parallel")),
    )(z, z, z, z, z, q_gain, k_gain)


def _attn_combine(name, os_, ls_):
    S_ = os_[0].shape[0]

    def body(o0, o1, o2, l0, l1, l2, o_ref, l_ref):
        a, b, c = l0[...], l1[...], l2[...]
        m = jnp.maximum(jnp.maximum(a, b), c)
        ea, eb, ec = jnp.exp(a - m), jnp.exp(b - m), jnp.exp(c - m)
        zsum = ea + eb + ec
        o_ref[...] = (ea * o0[...] + eb * o1[...] + ec * o2[...]) / zsum
        l_ref[...] = m + jnp.log(zsum)

    row = pl.BlockSpec((ROW_TILE, ATTN_OUT), lambda i: (i, 0))
    shp = jax.ShapeDtypeStruct((S_, ATTN_OUT), F32)
    return pl.pallas_call(
        body, name=name, grid=(S_ // ROW_TILE,), in_specs=[row] * 6, out_specs=(row, row), out_shape=(shp, shp),
        compiler_params=_cparams(("parallel",)),
    )(*os_, *ls_)


def _attn_bwd(name, z, q_gain, k_gain, do, o, lse, g, dil, after=None):
    S_ = z.shape[0]
    rows, nb, pw, heads, npairs, cq, ck, cv = _attn_specs(S_, g, dil)
    scale = HEAD ** -0.5

    def body(q_ref, kc_ref, kp_ref, vc_ref, vp_ref, gq_ref, gk_ref, do_ref, o_ref, l_ref,
             dq_ref, dk_ref, dv_ref, dgq_ref, dgk_ref, ck_ref, cvv_ref, gq_acc, gk_acc):
        hp = pl.program_id(0)
        n = pl.program_id(1)
        live = n < nb
        mask = jnp.logical_and(_band_mask(n > 0), live)
        ones, hmask = _head_ones(pw), _head_masks(pw)
        gq, gk = jnp.tile(gq_ref[...], (1, heads)), jnp.tile(gk_ref[...], (1, heads))

        @pl.when(n == 0)
        def _():
            ck_ref[...] = jnp.zeros_like(ck_ref)
            cvv_ref[...] = jnp.zeros_like(cvv_ref)

        @pl.when(jnp.logical_and(n == 0, hp == 0))
        def _():
            gq_acc[...] = jnp.zeros_like(gq_acc)
            gk_acc[...] = jnp.zeros_like(gk_acc)

        dgq = jnp.zeros((8, pw), F32)
        dgk = jnp.zeros((8, pw), F32)
        for r0 in range(0, dil, ATTN_BATCH):
            rs = range(r0, min(r0 + ATTN_BATCH, dil))
            keys = [(r, h) for r in rs for h in range(heads)]
            qn, qhat, qrstd, kn, kphat, kprstd, vv, dob, delta, lse = ({} for _ in range(10))
            for r in rs:
                q, kc, kp = _rows_of(q_ref, r, dil), _rows_of(kc_ref, r, dil), _rows_of(kp_ref, r, dil)
                dov = _rows_of(do_ref, r, dil)
                qn_f, qhat[r], qrstd[r] = _head_norm(q, gq, ones)
                kpn, kphat[r], kprstd[r] = _head_norm(kp, gk, ones)
                qn[r] = qn_f.astype(BF16)
                kn[r] = jnp.concatenate([kpn, _head_norm(kc, gk, ones)[0]], axis=0).astype(BF16)
                vv[r] = jnp.concatenate([_rows_of(vp_ref, r, dil), _rows_of(vc_ref, r, dil)], axis=0).astype(BF16)
                dob[r] = dov.astype(BF16)
                delta[r] = _head_sum(dov * _rows_of(o_ref, r, dil), ones)
                lse[r] = _rows_of(l_ref, r, dil)
            s, dp = {}, {}
            for r, h in keys:
                s[r, h] = lax.dot_general(_only(hmask[h], qn[r]), kn[r], _DIMS["nt"],
                                          preferred_element_type=F32) * scale
                dp[r, h] = lax.dot_general(_only(hmask[h], dob[r]), vv[r], _DIMS["nt"], preferred_element_type=F32)
            p, ds = {}, {}
            for r, h in keys:
                col = slice(h * HEAD, h * HEAD + 1)
                pk = jnp.where(mask, jnp.exp(jnp.where(mask, s[r, h], MASK) - lse[r][:, col]), 0.0)
                ds[r, h] = (pk * (dp[r, h] - delta[r][:, col]) * scale).astype(BF16)
                p[r, h] = pk.astype(BF16)
            dqn, dkn, dvv = {}, {}, {}
            for r in rs:
                dqn[r] = jnp.zeros((ATTN_BLK, pw), F32)
                dkn[r] = jnp.zeros((2 * ATTN_BLK, pw), F32)
                dvv[r] = jnp.zeros((2 * ATTN_BLK, pw), F32)
                for h in range(heads):
                    dqn[r] = jnp.where(hmask[h], jnp.dot(ds[r, h], kn[r], preferred_element_type=F32), dqn[r])
                    dkn[r] = jnp.where(hmask[h], lax.dot_general(ds[r, h], qn[r], _DIMS["tn"],
                                                                 preferred_element_type=F32), dkn[r])
                    dvv[r] = jnp.where(hmask[h], lax.dot_general(p[r, h], dob[r], _DIMS["tn"],
                                                                 preferred_element_type=F32), dvv[r])
            for r in rs:
                dq_all, dg = _head_norm_bwd(dqn[r], qhat[r], qrstd[r], gq, ones)
                dgq = dgq + dg
                dk_all, dg = _head_norm_bwd(_rows_of(ck_ref, r, dil) + dkn[r][:ATTN_BLK], kphat[r], kprstd[r], gk, ones)
                dgk = dgk + dg
                dv_all = _rows_of(cvv_ref, r, dil) + dvv[r][:ATTN_BLK]

                @pl.when(live)
                def _(dq_all=dq_all, r=r):
                    _put_rows(dq_ref, r, dil, dq_all)

                _put_rows(dk_ref, r, dil, dk_all)
                _put_rows(dv_ref, r, dil, dv_all)
                _put_rows(ck_ref, r, dil, dkn[r][ATTN_BLK:])
                _put_rows(cvv_ref, r, dil, dvv[r][ATTN_BLK:])
        gq_acc[...] += dgq
        gk_acc[...] += dgk

        @pl.when(jnp.logical_and(n == nb, hp == npairs - 1))
        def _():
            fold = lambda a: sum(a[:, h * HEAD:(h + 1) * HEAD] for h in range(heads))
            dgq_ref[...] = jnp.sum(fold(gq_acc[...]), axis=0, keepdims=True)
            dgk_ref[...] = jnp.sum(fold(gk_acc[...]), axis=0, keepdims=True)

    last = nb - 1
    cur = lambda c: pl.BlockSpec((rows, pw), lambda hp, n, c=c: (jnp.minimum(n, last), c + hp))
    prev = lambda c: pl.BlockSpec((rows, pw), lambda hp, n, c=c: (jnp.maximum(n - 1, 0), c + hp))
    gspec = pl.BlockSpec((1, HEAD), lambda hp, n: (0, 0))
    act = jax.ShapeDtypeStruct((S_, ATTN_OUT), F32)
    vec = jax.ShapeDtypeStruct((1, HEAD), F32)
    return _call(
        body, [z, z, z, z, z, q_gain, k_gain, do, o, lse],
        [cur(cq), cur(ck), prev(ck), cur(cv), prev(cv), gspec, gspec, cur(0), cur(0), cur(0)], after,
        name=name, grid=(npairs, nb + 1),
        out_specs=(cur(0), prev(0), prev(0), gspec, gspec),
        out_shape=(act, act, act, vec, vec),
        scratch_shapes=[pltpu.VMEM((rows, pw), F32), pltpu.VMEM((rows, pw), F32),
                        pltpu.VMEM((8, pw), F32), pltpu.VMEM((8, pw), F32)],
        compiler_params=_cparams(("arbitrary", "arbitrary")))


MIX_TN = 256
MIX_TM = 2048


def _sigmoid(v):
    return 1.0 / (1.0 + jnp.exp(-v))


def _mix_fwd(name, z, b_gate, ys, ws):
    S_ = z.shape[0]
    tm, tn = MIX_TM, MIX_TN
    gblk = OFF_GATE // tn

    def body(yp, yc, ya, wp, wc, wa, g0, g1, g2, b0, b1, b2, m_ref):
        acc = None
        for y_ref, w_ref, g_ref, b_ref in ((yp, wp, g0, b0), (yc, wc, g1, b1), (ya, wa, g2, b2)):
            u = lax.dot_general(y_ref[...].astype(BF16), w_ref[...], _DIMS["nt"], preferred_element_type=F32)
            t = _sigmoid(g_ref[...] + b_ref[...]) * u
            acc = t if acc is None else acc + t
        m_ref[...] = acc.astype(BF16)

    yspec = lambda w: pl.BlockSpec((tm, w), lambda i, j: (i, 0))
    wspec = lambda w: pl.BlockSpec((tn, w), lambda i, j: (j, 0))
    gspec = lambda b: pl.BlockSpec((tm, tn), lambda i, j, b=b: (i, gblk + b * (D // tn) + j))
    bspec = lambda b: pl.BlockSpec((1, tn), lambda i, j, b=b: (0, b * (D // tn) + j))
    return pl.pallas_call(
        body, name=name, grid=(S_ // tm, D // tn),
        in_specs=[yspec(POOL_W), yspec(CONV_W), yspec(ATTN_OUT), wspec(POOL_W), wspec(CONV_W), wspec(ATTN_OUT),
                  gspec(0), gspec(1), gspec(2), bspec(0), bspec(1), bspec(2)],
        out_specs=pl.BlockSpec((tm, tn), lambda i, j: (i, j)),
        out_shape=jax.ShapeDtypeStruct((S_, D), BF16),
        compiler_params=_cparams(("parallel", "parallel")),
    )(*ys, *ws, z, z, z, b_gate, b_gate, b_gate)


def _mix_bwd(name, z, b_gate, y, w, dmerged, branch, dz, after=None):
    S_ = z.shape[0]
    tm, tn = MIX_TM, MIX_TN
    width = y.shape[1]
    gblk = OFF_GATE // tn + branch * (D // tn)
    nj = D // tn

    def body(y_ref, w_ref, g_ref, b_ref, dm_ref, dy_ref, dw_ref, dg_ref, db_ref, acc_ref):
        i, j = pl.program_id(0), pl.program_id(1)
        yb = y_ref[...].astype(BF16)
        u = lax.dot_general(yb, w_ref[...], _DIMS["nt"], preferred_element_type=F32)
        sg = _sigmoid(g_ref[...] + b_ref[...])
        dm = dm_ref[...]
        du = (sg * dm).astype(BF16)
        dpre = dm * u * sg * (1.0 - sg)
        dg_ref[...] = dpre.astype(BF16)
        cols = pl.ds(pl.multiple_of(j * tn, tn), tn)
        db_ref[...] = _rows8(dpre)
        d_w = lax.dot_general(du, yb, _DIMS["tn"], preferred_element_type=F32)
        d_y = jnp.dot(du, w_ref[...], preferred_element_type=F32)

        @pl.when(i == 0)
        def _():
            dw_ref[cols, :] = d_w

        @pl.when(i > 0)
        def _():
            dw_ref[cols, :] += d_w

        @pl.when(j == 0)
        def _():
            acc_ref[...] = d_y

        @pl.when(j > 0)
        def _():
            acc_ref[...] += d_y

        @pl.when(j == nj - 1)
        def _():
            dy_ref[...] = acc_ref[...]

    rows = pl.BlockSpec((tm, width), lambda i, j: (i, 0))
    blk = pl.BlockSpec((tm, tn), lambda i, j: (i, j))
    gate = pl.BlockSpec((tm, tn), lambda i, j: (i, gblk + j))
    args = [y, w, z, b_gate, dmerged]
    in_specs = [rows, pl.BlockSpec((tn, width), lambda i, j: (j, 0)), gate,
                pl.BlockSpec((1, tn), lambda i, j: (0, branch * nj + j)), blk]
    aliases = {}
    if not isinstance(dz, jax.ShapeDtypeStruct):
        body = _drop_ref(body, len(args))
        aliases = {len(args): 2}
        args.append(dz)
        in_specs.append(ANY)
    return _call(
        body, args, in_specs, after, name=name, grid=(S_ // tm, nj),
        out_specs=(rows, pl.BlockSpec((D, width), lambda i, j: (0, 0)), gate, pl.BlockSpec((8, tn), lambda i, j: (i, j))),
        out_shape=(jax.ShapeDtypeStruct((S_, width), F32), jax.ShapeDtypeStruct((D, width), F32),
                   jax.ShapeDtypeStruct(dz.shape, dz.dtype), jax.ShapeDtypeStruct((S_ // tm * 8, D), F32)),
        scratch_shapes=[pltpu.VMEM((tm, width), F32)], input_output_aliases=aliases,
        compiler_params=_cparams(("arbitrary", "arbitrary")))


def _relu2_epilogue(acc):
    r = jnp.maximum(acc, 0.0)
    return (r * r,)


def _relu2_bwd_epilogue(acc, r):
    return (acc * (2.0 * jnp.sqrt(r.astype(F32))),)


def _residual_norm_epilogue(acc, xr, gain):
    x = xr + acc
    ms = jnp.mean(x * x, axis=-1, keepdims=True)
    return x, x * lax.rsqrt(ms + EPS) * gain


def _fold_rows(name, part, lanes=False, scale=1.0):
    R, N = part.shape

    def body(p_ref, o_ref):
        s = jnp.sum(p_ref[...], axis=0, keepdims=True)
        if lanes:
            s = jnp.sum(s, axis=1, keepdims=True)
        o_ref[...] = s * scale

    out_n = 1 if lanes else N
    return pl.pallas_call(
        body, name=name, grid=(1,), in_specs=[pl.BlockSpec((R, N), lambda i: (0, 0))],
        out_specs=pl.BlockSpec((1, out_n), lambda i: (0, 0)), out_shape=jax.ShapeDtypeStruct((1, out_n), F32))(part)


def _rows8(v):
    return jnp.sum(v.reshape(v.shape[0] // 8, 8, v.shape[1]), axis=0)


def _residual_loss_epilogue(acc, xr, target):
    e = xr + acc - target
    return e * (1.0 / D), _rows8(e * e)


def _norm_bwd_epilogue(acc, x, dres, gain):
    rstd = lax.rsqrt(jnp.mean(x * x, axis=-1, keepdims=True) + EPS)
    xhat = x * rstd
    dxhat = acc * gain
    c = jnp.mean(dxhat * xhat, axis=-1, keepdims=True)
    return dres + rstd * (dxhat - xhat * c), _rows8(acc * xhat)


def _layer_fwd(l, x, h, p, next_gain, target=None):
    t = f"l{l}_"
    z = _matmul(t + "in_proj", h, p["w_in_t"], "nt", 512, 3712, 1024)
    y_pool = _pool_fwd(t + "pool", z, p["pool_mix"], p["pool_scale"])
    y_conv = _conv_fwd(t + "conv", z, p["conv_w"])
    os_, ls_ = [], []
    for g, (_, dil) in enumerate(ATTN_GROUPS):
        o_g, l_g = _attn_fwd(t + f"attn{g}", z, p["q_gain"], p["k_gain"], g, dil)
        os_.append(o_g)
        ls_.append(l_g)
    y_attn, lse = _attn_combine(t + "attn_mix", os_, ls_)
    merged = _mix_fwd(t + "merge", z, p["b_gate"], (y_pool, y_conv, y_attn),
                      (p["w_pool_up_t"], p["w_conv_out_t"], p["w_attn_up_t"]))
    x1, h2 = _matmul(t + "out_proj", merged, p["w_o"], "nn", 1024, D, 1024, out_dtypes=(F32, BF16),
                     extras=((x, "mn"), (p["norm_mlp"], "n")), epilogue=_residual_norm_epilogue)
    r = _matmul(t + "ff1", h2, p["w_ff1_t"], "nt", 1024, 1024, 1024, out_dtypes=(BF16,), epilogue=_relu2_epilogue)
    if next_gain is None:
        x2, h_out = _matmul(t + "ff2", r, p["w_ff2"], "nn", 1024, D, 1024, out_dtypes=(F32, (F32, "rows8")),
                            extras=((x1, "mn"), (target, "mn")), epilogue=_residual_loss_epilogue)
    else:
        x2, h_out = _matmul(t + "ff2", r, p["w_ff2"], "nn", 1024, D, 1024, out_dtypes=(F32, BF16),
                            extras=((x1, "mn"), (next_gain, "n")), epilogue=_residual_norm_epilogue)
    saved = dict(x=x, h=h, z=z, y_pool=y_pool, y_conv=y_conv, y_attn=y_attn, lse=lse, merged=merged,
                 x1=x1, h2=h2, r=r)
    return x2, h_out, saved


def _layer_bwd(l, dx2, p, s, pending, collective_id, last, start_after=None):
    t = f"l{l}_b_"
    g = {}
    rest = jax.ShapeDtypeStruct((N_DEV, REST_ROWS, D), F32)
    da = _matmul(t + "d_ff2_in", dx2, p["w_ff2"], "nt", 1024, 1024, 1024, out_dtypes=(BF16,),
                 extras=((s["r"], "mn"),), epilogue=_relu2_bwd_epilogue,
                 after=start_after)
    rest = _matmul(t + "dw_ff2", s["r"], dx2, "tn", 512, 1024, 4096, into=(rest, REST_FF2, 0), after=da)
    tok = rest if pending is None else pending.chip_sums(after=rest)
    dx1, part = _matmul(t + "d_ff1_in", da, p["w_ff1_t"], "nn", 1024, D, 1024, out_dtypes=(F32, (F32, "rows8")),
                        extras=((s["x1"], "mn"), (dx2, "mn"), (p["norm_mlp"], "n")), epilogue=_norm_bwd_epilogue,
                        after=tok)
    g["norm_mlp"] = _fold_rows(t + "d_norm_mlp", part)
    rest = _matmul(t + "dw_ff1", da, s["h2"], "tn", 512, 1024, 4096, into=(rest, REST_FF1, 0), after=dx1)
    dmerged = _matmul(t + "d_out_proj_in", dx1, p["w_o"], "nt", 1024, 1024, 1024)
    rest = _matmul(t + "dw_o", s["merged"], dx1, "tn", 128, 1024, 4096, into=(rest, REST_WO, 0), after=dmerged)
    ys = (s["y_pool"], s["y_conv"], s["y_attn"])
    names = ("w_pool_up_t", "w_conv_out_t", "w_attn_up_t")
    dys, dbs = [], []
    tok = rest
    dz = jax.ShapeDtypeStruct((dx2.shape[0], IN_COLS), BF16)
    for b in range(3):
        dy_b, dw_t, dz, db = _mix_bwd(t + f"merge{b}", s["z"], p["b_gate"], ys[b], p[names[b]], dmerged, b, dz,
                                      after=tok)
        tok = dy_b
        width = ys[b].shape[1]
        if b < 2:
            rest = rest.at[:, REST_UP:REST_ATTN, b * width:(b + 1) * width].set(
                dw_t.reshape(N_DEV, REST_ATTN - REST_UP, width))
        else:
            rest = rest.at[:, REST_ATTN:REST_ROWS, :].set(dw_t.reshape(N_DEV, REST_ROWS - REST_ATTN, D))
        dys.append(dy_b)
        dbs.append(_fold_rows(t + f"d_b_gate{b}", db))
    g["b_gate"] = jnp.concatenate(dbs, axis=1)
    rs_rest = _ReduceScatter(f"rs_rest{l}", collective_id, rest)
    pending_sum = None if pending is None else pending.finish(after=rest)
    dz, g["pool_mix"], g["pool_scale"] = _pool_bwd(t + "pool", s["z"], dys[0], p["pool_mix"], p["pool_scale"], dz,
                                                   after=rest if pending is None else pending_sum[0])
    dcb, dcc, dcx, g["conv_w"] = _conv_bwd(t + "conv", s["z"], dys[1], p["conv_w"], after=dz)
    tok = rs_rest.chip_sums(after=dcb)
    dqs, dks, dvs = [], [], []
    gq = gk = None
    for gi, (_, dil) in enumerate(ATTN_GROUPS):
        dq, dk, dv, dgq, dgk = _attn_bwd(t + f"attn{gi}", s["z"], p["q_gain"], p["k_gain"], dys[2], s["y_attn"],
                                         s["lse"], gi, dil, after=tok)
        tok = dq
        dqs.append(dq)
        dks.append(dk)
        dvs.append(dv)
        gq = dgq if gq is None else gq + dgq
        gk = dgk if gk is None else gk + dgk
    g["q_gain"], g["k_gain"] = gq, gk
    rest_sum, _ = rs_rest.finish(after=tok)
    col = OFF_CB
    for piece in [dcb, dcc, dcx] + dqs + dks + dvs:
        dz = lax.dynamic_update_slice(dz, piece.astype(BF16), (0, col))
        col += piece.shape[1]
    in_t = _matmul(t + "dw_in", dz, s["h"], "tn", 256, 1024, 4096, after=rest_sum)
    rs_in = _ReduceScatter(f"rs_in{l}", collective_id + 2, in_t.reshape(N_DEV, IN_ROWS, D))
    dx, part = _matmul(t + "d_in_proj_in", dz, p["w_in_t"], "nn", 512, D, 3712, out_dtypes=(F32, (F32, "rows8")),
                       extras=((s["x"], "mn"), (dx1, "mn"), (p["norm_mix"], "n")), epilogue=_norm_bwd_epilogue,
                       after=in_t)
    g["norm_mix"] = _fold_rows(t + "d_norm_mix", part)
    if last:
        rs_in.chip_sums(after=dx)
    return dx, g, rest_sum, pending_sum, rs_in


ANY = pl.BlockSpec(memory_space=pl.ANY)


def _mesh_pos():
    return lax.axis_index("x"), lax.axis_index("y"), lax.axis_index("c")


def _other_chips(x, y):
    return [(1 - x, y), (x, 1 - y), (1 - x, 1 - y)]


def _comm_call(name, collective_id, peers, body, arrs, out_shape, sem_counts):
    n_in, n_out = len(arrs), len(out_shape)
    if collective_id is None:
        def tc_body(*refs):
            body(refs[:n_in], refs[n_in:n_in + n_out], *refs[n_in + n_out:])

        return pl.pallas_call(
            tc_body, name=name, out_shape=tuple(out_shape), in_specs=[ANY] * n_in, out_specs=(ANY,) * n_out,
            scratch_shapes=[pltpu.SemaphoreType.DMA((n,)) for n in sem_counts])(*arrs)

    def seq_body(*refs):
        barrier = pltpu.get_barrier_semaphore()
        ps = peers(*_mesh_pos())
        for p in ps:
            pl.semaphore_signal(barrier, inc=1, device_id=p, device_id_type=MESH)
        pl.semaphore_wait(barrier, len(ps))
        body(refs[:n_in], refs[n_in:n_in + n_out], *refs[n_in + n_out:])

    return pl.kernel(
        seq_body, out_type=tuple(out_shape), mesh=plsc.ScalarSubcoreMesh(axis_name="seq", num_cores=1), name=name,
        scratch_types=[pltpu.SemaphoreType.DMA((n,)) for n in sem_counts],
        compiler_params=pltpu.CompilerParams(collective_id=collective_id),
    )(*arrs)


def _all_gather(name, collective_id, shard):
    R, C = shard.shape

    def peers(x, y, c):
        return [(x, y, 1 - c)] + [(*chip, c) for chip in _other_chips(x, y)]

    def body(in_refs, out_refs, send_sems, recv_sems, local_sems):
        (x_ref,), (out_ref,) = in_refs, out_refs
        x, y, c = _mesh_pos()
        me, sibling = (x, y, c), (x, y, 1 - c)
        chips = _other_chips(x, y)

        def slot(px, py, pc):
            return out_ref.at[4 * px + 2 * py + pc]

        def copy(k, block, to, src=None):
            return pltpu.make_async_remote_copy(
                src_ref=slot(*block) if src is None else src, dst_ref=slot(*block),
                send_sem=send_sems.at[k], recv_sem=recv_sems.at[k], device_id=to, device_id_type=MESH)

        mine = pltpu.make_async_copy(x_ref, slot(*me), local_sems.at[0])
        mine.start()
        first = [copy(0, me, sibling, src=x_ref)]
        first += [copy(1 + j, me, (*chip, c), src=x_ref) for j, chip in enumerate(chips)]
        for cp in first:
            cp.start()
        passed = [copy(4 + j, (*chip, c), sibling) for j, chip in enumerate(chips)]
        for j, chip in enumerate(chips):
            copy(1 + j, (*chip, c), me).wait_recv()
            passed[j].start()
        copy(0, sibling, me).wait_recv()
        for j, chip in enumerate(chips):
            copy(4 + j, (*chip, 1 - c), me).wait_recv()
        for cp in first + passed:
            cp.wait_send()
        mine.wait()

    return _comm_call(name, collective_id, peers, body, [shard],
                      [jax.ShapeDtypeStruct((N_DEV, R, C), shard.dtype)], (7, 7, 1))[0]


def _rs_sibling_exchange(name, collective_id, arrs):
    n = len(arrs)

    def body(in_refs, out_refs, send_sems, recv_sems):
        x, y, c = _mesh_pos()
        cps = []
        for k, (src, dst) in enumerate(zip(in_refs, out_refs)):
            src = src.at[:, 1 - c] if len(src.shape) == 4 else src
            cps.append(pltpu.make_async_remote_copy(src_ref=src, dst_ref=dst, send_sem=send_sems.at[k],
                                                    recv_sem=recv_sems.at[k], device_id=(x, y, 1 - c),
                                                    device_id_type=MESH))
        for cp in cps:
            cp.start()
        for cp in cps:
            cp.wait()

    out_shape = [jax.ShapeDtypeStruct(a.shape[:1] + a.shape[2:] if a.ndim == 4 else a.shape, a.dtype) for a in arrs]
    return _comm_call(name, collective_id, lambda x, y, c: [(x, y, 1 - c)], body, arrs, out_shape, (n, n))


def _rs_rows(r):
    return r // 2 if (r // 2) % 16 == 0 else r


def _drop_ref(body, idx):
    def wrapped(*refs):
        return body(*refs[:idx], *refs[idx + 1:])

    return wrapped


def _rs_chip_sum(name, ids, big, rbig, after=None):
    _, _, R, C = big.shape
    rows = _rs_rows(R)

    def body(ids_ref, mine_ref, theirs_ref, t16_ref, own_ref):
        p = pl.program_id(1)
        s = mine_ref[...] + theirs_ref[...]
        t16_ref[...] = s.astype(BF16)

        @pl.when(p == ids_ref[1])
        def _():
            own_ref[...] = s

    in_specs = [pl.BlockSpec((None, None, rows, C), lambda i, p, ids: (p, ids[0], i, 0)),
                pl.BlockSpec((None, rows, C), lambda i, p, ids: (p, i, 0))]
    args = [ids, big, rbig]
    if after is not None:
        body = _drop_ref(body, len(args))
        in_specs.append(ANY)
        args.append(after)
    grid_spec = pltpu.PrefetchScalarGridSpec(
        num_scalar_prefetch=1, grid=(R // rows, 4), in_specs=in_specs,
        out_specs=(pl.BlockSpec((None, rows, C), lambda i, p, ids: (p, i, 0)),
                   pl.BlockSpec((rows, C), lambda i, p, ids: (i, 0))))
    return pl.pallas_call(
        body, name=name, grid_spec=grid_spec,
        out_shape=(jax.ShapeDtypeStruct((4, R, C), BF16), jax.ShapeDtypeStruct((R, C), F32)),
        compiler_params=_cparams(("parallel", "arbitrary")),
    )(*args)


def _add2(name, a, b):
    def body(a_ref, b_ref, o_ref):
        o_ref[...] = a_ref[...] + b_ref[...]

    full = pl.BlockSpec(a.shape, lambda i: (0, 0))
    return pl.pallas_call(body, name=name, grid=(1,), in_specs=[full, full], out_specs=full,
                          out_shape=jax.ShapeDtypeStruct(a.shape, a.dtype))(a, b)


def _rs_chip_exchange(name, collective_id, arrs):
    n = len(arrs)

    def body(in_refs, out_refs, send_sems, recv_sems, local_sems):
        x, y, c = _mesh_pos()
        p_me = 2 * x + y
        chips = _other_chips(x, y)

        def part(ref, p):
            return ref.at[p] if len(ref.shape) == 3 else ref

        local = [pltpu.make_async_copy(part(src, p_me), dst.at[p_me], local_sems.at[k])
                 for k, (src, dst) in enumerate(zip(in_refs, out_refs))]
        for cp in local:
            cp.start()
        sends = []
        for j, (px, py) in enumerate(chips):
            for k, (src, dst) in enumerate(zip(in_refs, out_refs)):
                sends.append(pltpu.make_async_remote_copy(
                    src_ref=part(src, 2 * px + py), dst_ref=dst.at[p_me], send_sem=send_sems.at[n * j + k],
                    recv_sem=recv_sems.at[n * j + k], device_id=(px, py, c), device_id_type=MESH))
        for cp in sends:
            cp.start()
        for j, (px, py) in enumerate(chips):
            for k, (src, dst) in enumerate(zip(in_refs, out_refs)):
                pltpu.make_async_remote_copy(
                    src_ref=part(src, p_me), dst_ref=dst.at[2 * px + py], send_sem=send_sems.at[n * j + k],
                    recv_sem=recv_sems.at[n * j + k], device_id=(px, py, c), device_id_type=MESH).wait_recv()
        for cp in sends:
            cp.wait_send()
        for cp in local:
            cp.wait()

    out_shape = [jax.ShapeDtypeStruct((4,) + a.shape[-2:], a.dtype) for a in arrs]
    return _comm_call(name, collective_id, lambda x, y, c: [(*chip, c) for chip in _other_chips(x, y)], body, arrs,
                      out_shape, (3 * n, 3 * n, n))


def _rs_final_sum(name, ids, recv, own, rows, after=None):
    _, R, C = recv.shape
    assert R % rows == 0

    def body(ids_ref, r_ref, own_ref, o_ref):
        acc = None
        for p in range(4):
            term = jnp.where(ids_ref[1] == p, own_ref[...], r_ref[p].astype(F32))
            acc = term if acc is None else acc + term
        o_ref[...] = acc

    in_specs = [pl.BlockSpec((4, rows, C), lambda i, ids: (0, i, 0)), pl.BlockSpec((rows, C), lambda i, ids: (i, 0))]
    args = [ids, recv, own]
    if after is not None:
        body = _drop_ref(body, len(args))
        in_specs.append(ANY)
        args.append(after)
    grid_spec = pltpu.PrefetchScalarGridSpec(
        num_scalar_prefetch=1, grid=(R // rows,), in_specs=in_specs,
        out_specs=pl.BlockSpec((rows, C), lambda i, ids: (i, 0)))
    return pl.pallas_call(
        body, name=name, grid_spec=grid_spec, out_shape=jax.ShapeDtypeStruct((R, C), F32),
        compiler_params=_cparams(("parallel",)),
    )(*args)


class _ReduceScatter:
    def __init__(self, tag, collective_id, big, small=None):
        x, y, c = _mesh_pos()
        self.tag, self.cid, self.small = tag, collective_id, small
        self.ids = jnp.stack([c, 2 * x + y]).astype(jnp.int32)
        self.big = big.reshape((4, 2) + big.shape[1:])
        self.got = _rs_sibling_exchange(tag + "_sibling", collective_id,
                                        [self.big] + ([] if small is None else [small]))

    def chip_sums(self, after=None):
        t16, self.own = _rs_chip_sum(self.tag + "_chip_sum", self.ids, self.big, self.got[0], after)
        arrs = [t16]
        if self.small is not None:
            self.ts = _add2(self.tag + "_chip_sum_small", self.small, self.got[1])
            arrs.append(self.ts)
        self.recv = _rs_chip_exchange(self.tag + "_chips", self.cid + 1, arrs)
        return t16

    def finish(self, after=None):
        out = _rs_final_sum(self.tag + "_final", self.ids, self.recv[0], self.own, _rs_rows(self.own.shape[0]), after)
        out_small = None
        if self.small is not None:
            out_small = _rs_final_sum(self.tag + "_final_small", self.ids, self.recv[1], self.ts, self.small.shape[0])
        return out, out_small


def _all_reduce_small(tag, small):
    x, y, c = _mesh_pos()
    ids = jnp.stack([c, 2 * x + y]).astype(jnp.int32)
    (theirs,) = _rs_sibling_exchange(tag + "_sibling", None, [small])
    ts = _add2(tag + "_chip_sum", small, theirs)
    (recv,) = _rs_chip_exchange(tag + "_chips", None, [ts])
    return _rs_final_sum(tag + "_final", ids, recv, ts, small.shape[0])


def _adamw(name, w, g, m, v):
    R, C = w.shape
    tr = max(t for t in range(8, 513, 8) if R % t == 0)
    c1 = 1.0 - ADAM_B1 ** ADAM_STEP
    c2 = 1.0 - ADAM_B2 ** ADAM_STEP

    def body(w_ref, g_ref, m_ref, v_ref, d_ref, nm_ref, nv_ref):
        gv = g_ref[...]
        nm = ADAM_B1 * m_ref[...] + (1.0 - ADAM_B1) * gv
        nv = ADAM_B2 * v_ref[...] + (1.0 - ADAM_B2) * (gv * gv)
        d_ref[...] = -ADAM_LR * ((nm / c1) / (jnp.sqrt(nv / c2) + ADAM_EPS) + ADAM_WD * w_ref[...])
        nm_ref[...] = nm
        nv_ref[...] = nv

    blk = pl.BlockSpec((tr, C), lambda i: (i, 0))
    shp = jax.ShapeDtypeStruct((R, C), F32)
    return pl.pallas_call(
        body, name=name, grid=(R // tr,), in_specs=[blk] * 4, out_specs=(blk,) * 3, out_shape=(shp,) * 3,
        compiler_params=_cparams(("parallel",)),
    )(w, g, m, v)


def _adamw_t(name, w, g_t, m, v):
    L, R, C = w.shape
    tr = 256
    c1 = 1.0 - ADAM_B1 ** ADAM_STEP
    c2 = 1.0 - ADAM_B2 ** ADAM_STEP

    def body(w_ref, g_ref, m_ref, v_ref, go_ref, d_ref, nm_ref, nv_ref):
        gv = g_ref[...].T
        go_ref[...] = gv
        nm = ADAM_B1 * m_ref[...] + (1.0 - ADAM_B1) * gv
        nv = ADAM_B2 * v_ref[...] + (1.0 - ADAM_B2) * (gv * gv)
        d_ref[...] = -ADAM_LR * ((nm / c1) / (jnp.sqrt(nv / c2) + ADAM_EPS) + ADAM_WD * w_ref[...])
        nm_ref[...] = nm
        nv_ref[...] = nv

    blk = pl.BlockSpec((None, tr, C), lambda l, i: (l, i, 0))
    blk_t = pl.BlockSpec((None, C, tr), lambda l, i: (l, 0, i))
    shp = jax.ShapeDtypeStruct((L, R, C), F32)
    return pl.pallas_call(
        body, name=name, grid=(L, R // tr), in_specs=[blk, blk_t, blk, blk], out_specs=(blk,) * 4,
        out_shape=(shp,) * 4, compiler_params=_cparams(("parallel", "parallel")),
    )(w, g_t, m, v)


REST_FF1 = 0
REST_FF2 = 512
REST_WO = 1024
REST_UP = 1152
REST_ATTN = 1280
REST_ROWS = 1312
IN_ROWS = IN_COLS // N_DEV
CONV_BITS_ROWS = 16
SHARD_ROWS = IN_ROWS + REST_ROWS + CONV_BITS_ROWS
SMALL = (("norm_mix", (DEPTH, D)), ("b_gate", (DEPTH, 3 * D)), ("pool_mix", (DEPTH, 4, POOL_G, POOL_G)),
         ("pool_scale", (DEPTH, POOL_W)), ("conv_w", (DEPTH, 3, CONV_W)), ("q_gain", (DEPTH, HEAD)),
         ("k_gain", (DEPTH, HEAD)), ("norm_mlp", (DEPTH, D)))


def _pack_weight_shards(w, l):
    b = lambda a: a.astype(BF16)
    conv = lax.bitcast_convert_type(w["conv_w"][l], BF16).reshape(3, 128)
    conv = jnp.pad(conv, ((0, CONV_BITS_ROWS - 3), (0, D - 128)))
    rest = jnp.concatenate([
        b(w["w_ff1"][l].T), b(w["w_ff2"][l]), b(w["w_o"][l]),
        jnp.concatenate([b(w["w_pool_up"][l].T), b(w["w_conv_out"][l].T)], axis=1),
        b(w["w_attn_up"][l].T).reshape(REST_ROWS - REST_ATTN, D), conv], axis=0)
    return b(w["w_in"][l].T), rest


def _unpack_gathered(g_in, g_rest, small_w, l):
    rest = g_rest[:, :REST_ROWS]
    take = lambda r0, rows, c0=0, cols=D: rest[:, r0:r0 + rows, c0:c0 + cols].reshape(N_DEV * rows, cols)
    conv = g_rest[:, REST_ROWS:REST_ROWS + 3, :128].reshape(N_DEV, 3, CONV_W // N_DEV, 2)
    conv = lax.bitcast_convert_type(conv, F32)
    p = {
        "w_in_t": g_in.reshape(IN_COLS, D),
        "w_ff1_t": take(REST_FF1, 512), "w_ff2": take(REST_FF2, 512), "w_o": take(REST_WO, 128),
        "w_pool_up_t": take(REST_UP, 128, 0, POOL_W), "w_conv_out_t": take(REST_UP, 128, POOL_W, CONV_W),
        "w_attn_up_t": rest[:, REST_ATTN:].reshape(D, ATTN_OUT),
        "conv_w": jnp.transpose(conv, (1, 0, 2)).reshape(3, CONV_W),
        "pool_mix": small_w["pool_mix"][l],
    }
    for name in ("norm_mix", "b_gate", "pool_scale", "q_gain", "k_gain", "norm_mlp"):
        p[name] = small_w[name][l][None, :]
    return p


def _pack_small_grads(grads):
    flat = jnp.concatenate([jnp.stack([grads[l][name] for l in range(DEPTH)]).reshape(-1) for name, _ in SMALL])
    return jnp.pad(flat, (0, (-flat.shape[0]) % (8 * 128))).reshape(-1, 128)


def _unpack_grads(in_t, rest, small, dev):
    out = {
        "w_in_t": jnp.stack(in_t),
        "w_ff1_t": jnp.stack([a[REST_FF1:REST_FF1 + 512] for a in rest]),
        "w_ff2": jnp.stack([a[REST_FF2:REST_FF2 + 512] for a in rest]),
        "w_o": jnp.stack([a[REST_WO:REST_WO + 128] for a in rest]),
        "w_pool_up": jnp.stack([a[REST_UP:REST_UP + 128, :POOL_W].T for a in rest]),
        "w_conv_out": jnp.stack([a[REST_UP:REST_UP + 128, POOL_W:].T for a in rest]),
        "w_attn_up": jnp.stack([a[REST_ATTN:].reshape(D // N_DEV, ATTN_OUT).T for a in rest]),
    }
    flat = small.reshape(-1)
    off = 0
    for name, shp in SMALL:
        n = 1
        for s in shp:
            n *= s
        out[name] = flat[off:off + n].reshape(shp)
        off += n
    width = CONV_W // N_DEV
    out["conv_w"] = lax.dynamic_slice_in_dim(out["conv_w"], dev * width, width, axis=2)
    return out


def _pack_small(arrs):
    flat = jnp.concatenate([a.reshape(-1) for a in arrs])
    pad = (-flat.shape[0]) % (8 * 128)
    return jnp.pad(flat, (0, pad)).reshape(-1, 128)


def _unpack_small(packed, like):
    flat = packed.reshape(-1)
    out, off = [], 0
    for a in like:
        out.append(flat[off:off + a.size].reshape(a.shape))
        off += a.size
    return out


WEIGHTS = ("norm_mix", "w_in", "b_gate", "pool_mix", "pool_scale", "conv_w", "q_gain", "k_gain", "w_pool_up",
           "w_conv_out", "w_attn_up", "w_o", "norm_mlp", "w_ff1", "w_ff2")


def kernel(x, norm_mix, w_in, b_gate, pool_mix, pool_scale, conv_w, q_gain, k_gain, w_pool_up, w_conv_out, w_attn_up, w_o, norm_mlp, w_ff1, w_ff2, loss_target, m_norm_mix, m_w_in, m_b_gate, m_pool_mix, m_pool_scale, m_conv_w, m_q_gain, m_k_gain, m_w_pool_up, m_w_conv_out, m_w_attn_up, m_w_o, m_norm_mlp, m_w_ff1, m_w_ff2, v_norm_mix, v_w_in, v_b_gate, v_pool_mix, v_pool_scale, v_conv_w, v_q_gain, v_k_gain, v_w_pool_up, v_w_conv_out, v_w_attn_up, v_w_o, v_norm_mlp, v_w_ff1, v_w_ff2):
    w = dict(zip(WEIGHTS, (norm_mix, w_in, b_gate, pool_mix, pool_scale, conv_w, q_gain, k_gain, w_pool_up,
                           w_conv_out, w_attn_up, w_o, norm_mlp, w_ff1, w_ff2)))
    m = dict(zip(WEIGHTS, (m_norm_mix, m_w_in, m_b_gate, m_pool_mix, m_pool_scale, m_conv_w, m_q_gain, m_k_gain,
                           m_w_pool_up, m_w_conv_out, m_w_attn_up, m_w_o, m_norm_mlp, m_w_ff1, m_w_ff2)))
    v = dict(zip(WEIGHTS, (v_norm_mix, v_w_in, v_b_gate, v_pool_mix, v_pool_scale, v_conv_w, v_q_gain, v_k_gain,
                           v_w_pool_up, v_w_conv_out, v_w_attn_up, v_w_o, v_norm_mlp, v_w_ff1, v_w_ff2)))
    xi, yi, ci = _mesh_pos()
    dev = 4 * xi + 2 * yi + ci

    params = []
    for l in range(DEPTH):
        s_in, s_rest = _pack_weight_shards(w, l)
        g_in = _all_gather(f"gather_in{l}", 1 + 2 * l, s_in)
        g_rest = _all_gather(f"gather_rest{l}", 2 + 2 * l, s_rest)
        params.append(_unpack_gathered(g_in, g_rest, w, l))
    saved = []
    act = x[0]
    h = _rmsnorm_fwd("l0_norm_mix", act, params[0]["norm_mix"])
    for l in range(DEPTH):
        act, h, s = _layer_fwd(l, act, h, params[l], params[l + 1]["norm_mix"] if l + 1 < DEPTH else None,
                               loss_target[0])
        saved.append(s)
    dy = act
    loss = _fold_rows("loss_sum", h, lanes=True, scale=0.5 / D)
    total = lax.psum(loss[0, 0], ("x", "y", "c"))
    in_t, rest, grads, pending = [None] * DEPTH, [None] * DEPTH, [None] * DEPTH, None
    start = total.reshape(1, 1)
    for l in reversed(range(DEPTH)):
        dy, grads[l], rest[l], done, pending = _layer_bwd(l, dy, params[l], saved[l], pending, 5 + 4 * l, l == 0,
                                                          start)
        start = None
        if done is not None:
            in_t[l + 1] = done[0]
    dx = dy
    g_small = _all_reduce_small("ar_small", _pack_small_grads(grads))
    in_t[0], _ = pending.finish(after=g_small)
    g = _unpack_grads(in_t, rest, g_small, dev)

    delta, new_m, new_v = {}, {}, {}
    for name in ("w_in", "w_ff1"):
        t = lambda a: jnp.swapaxes(a, 1, 2)
        g_t = g[name + "_t"]
        two_d = (g_t.shape[0] * g_t.shape[1], g_t.shape[2])
        d_, m_, v_ = _adamw("adamw_" + name, t(w[name]).reshape(two_d), g_t.reshape(two_d),
                            t(m[name]).reshape(two_d), t(v[name]).reshape(two_d))
        g[name] = t(g_t)
        delta[name], new_m[name], new_v[name] = (t(a.reshape(g_t.shape)) for a in (d_, m_, v_))
    for name in ("w_pool_up", "w_conv_out", "w_attn_up", "w_o", "w_ff2"):
        shp = w[name].shape
        two_d = (shp[0] * shp[1], shp[2])
        d_, m_, v_ = _adamw("adamw_" + name, w[name].reshape(two_d), g[name].reshape(two_d),
                            m[name].reshape(two_d), v[name].reshape(two_d))
        delta[name], new_m[name], new_v[name] = d_.reshape(shp), m_.reshape(shp), v_.reshape(shp)
    small_names = [name for name, _ in SMALL]
    packed = [_pack_small([src[name] for name in small_names]) for src in (w, g, m, v)]
    outs = _adamw("adamw_small", *packed)
    for dst, arr in zip((delta, new_m, new_v), outs):
        for name, val in zip(small_names, _unpack_small(arr, [w[name] for name in small_names])):
            dst[name] = val

    return (total, dx[None], *[g[n] for n in WEIGHTS], *[delta[n] for n in WEIGHTS],
            *[new_m[n] for n in WEIGHTS], *[new_v[n] for n in WEIGHTS])
```

```python
import functools

import jax
import jax.numpy as jnp
from jax import lax
from jax.experimental import pallas as pl
from jax.experimental.pallas import tpu as pltpu
from jax.experimental.pallas import tpu_sc as plsc

F32 = jnp.float32
BF16 = jnp.bfloat16
MESH = pl.DeviceIdType.MESH

D = 1024
SEQ = 4096
DEPTH = 2
N_DEV = 8
POOL_WINDOWS = (2, 4, 8, 16)
POOL_W = 512
POOL_G = 128
CONV_W = 512
HEAD = 64
ATTN_GROUPS = ((128, 1), (512, 4), (2048, 16))
HPG = 4
ATTN_W = 768
ATTN_OUT = 256
ATTN_BLK = 128
D_FF = 4096
EPS = 1e-6
MASK = -1e30
OFF_POOL = 0
OFF_CB = 512
OFF_CC = 1024
OFF_CX = 1536
OFF_Q = 2048
OFF_K = 2816
OFF_V = 3584
OFF_GATE = 4352
IN_COLS = 7424
ADAM_LR = 0.001
ADAM_B1 = 0.9
ADAM_B2 = 0.999
ADAM_EPS = 1e-08
ADAM_WD = 0.01
ADAM_STEP = 10

ROW_TILE = 512
SEQ_CHUNK = 256
HALO = 16
VMEM_LIMIT = 56 * 1024 * 1024


def _cparams(sem=None):
    return pltpu.CompilerParams(dimension_semantics=sem, vmem_limit_bytes=VMEM_LIMIT)


def _call(body, args, in_specs, after=None, **kw):
    if after is not None:
        after = list(after) if isinstance(after, (list, tuple)) else [after]
        n, k, inner = len(args), len(after), body

        def body(*refs):
            return inner(*refs[:n], *refs[n + k:])

        args = list(args) + after
        in_specs = list(in_specs) + [pl.BlockSpec(memory_space=pl.ANY)] * k
    return pl.pallas_call(body, in_specs=in_specs, **kw)(*args)


_DIMS = {"nn": (((1,), (0,)), ((), ())), "nt": (((1,), (1,)), ((), ())), "tn": (((0,), (0,)), ((), ()))}


def _matmul(name, a, b, mode, tm, tn, tk, out_dtypes=(F32,), extras=(), epilogue=None, into=None, after=None):
    if mode == "tn":
        K, M = a.shape
    else:
        M, K = a.shape
    N = b.shape[0] if mode == "nt" else b.shape[1]
    assert M % tm == 0 and N % tn == 0 and K % tk == 0, (name, M, N, K, tm, tn, tk)
    nk = K // tk
    n_extra = len(extras)
    n_out = len(out_dtypes)
    dims = _DIMS[mode]
    n_alias = 0 if into is None or isinstance(into[0], jax.ShapeDtypeStruct) else 1

    def body(a_ref, b_ref, *rest):
        extra_refs = rest[:n_extra]
        out_refs = rest[n_extra + n_alias:n_extra + n_alias + n_out]

        def finish(acc):
            if epilogue is None:
                res = (acc,)
            else:
                res = epilogue(acc, *[r[...] for r in extra_refs])
            for o_ref, v in zip(out_refs, res):
                o_ref[...] = v.astype(o_ref.dtype)

        part = lax.dot_general(a_ref[...].astype(BF16), b_ref[...].astype(BF16), dims,
                               preferred_element_type=F32)
        if nk == 1:
            finish(part)
        else:
            acc_ref = rest[-1]
            k = pl.program_id(2)

            @pl.when(k == 0)
            def _():
                acc_ref[...] = part

            @pl.when(k > 0)
            def _():
                acc_ref[...] += part

            @pl.when(k == nk - 1)
            def _():
                finish(acc_ref[...])

    if mode == "tn":
        a_spec = pl.BlockSpec((tk, tm), lambda i, j, k: (k, i))
    else:
        a_spec = pl.BlockSpec((tm, tk), lambda i, j, k: (i, k))
    if mode == "nt":
        b_spec = pl.BlockSpec((tn, tk), lambda i, j, k: (j, k))
    else:
        b_spec = pl.BlockSpec((tk, tn), lambda i, j, k: (k, j))
    in_specs = [a_spec, b_spec]
    args = [a, b]
    for arr, kind in extras:
        if kind == "mn":
            in_specs.append(pl.BlockSpec((tm, tn), lambda i, j, k: (i, j)))
        else:
            in_specs.append(pl.BlockSpec((1, tn), lambda i, j, k: (0, j)))
        args.append(arr)
    part = [isinstance(dt, tuple) for dt in out_dtypes]
    out_shape = tuple(jax.ShapeDtypeStruct((M // tm * 8, N), dt[0]) if p else jax.ShapeDtypeStruct((M, N), dt)
                      for dt, p in zip(out_dtypes, part))
    out_specs = tuple(pl.BlockSpec((8 if p else tm, tn), lambda i, j, k: (i, j)) for p in part)
    aliases = {}
    if into is not None:
        buf, row0, col0 = into
        assert n_out == 1 and (M // N_DEV) % tm == 0 and row0 % tm == 0 and col0 % tn == 0
        per_dev = M // N_DEV // tm
        out_shape = (jax.ShapeDtypeStruct(buf.shape, buf.dtype),)
        out_specs = (pl.BlockSpec((None, tm, tn), lambda i, j, k: (i // per_dev, row0 // tm + i % per_dev,
                                                                   col0 // tn + j)),)
        if not isinstance(buf, jax.ShapeDtypeStruct):
            aliases = {len(args): 0}
            in_specs.append(pl.BlockSpec(memory_space=pl.ANY))
            args.append(buf)
    scratch = [] if nk == 1 else [pltpu.VMEM((tm, tn), F32)]
    res = _call(
        body, args, in_specs, after, name=name, grid=(M // tm, N // tn, nk), out_specs=out_specs,
        out_shape=out_shape, scratch_shapes=scratch, input_output_aliases=aliases,
        compiler_params=_cparams(("parallel", "parallel", "arbitrary")))
    return res if n_out > 1 else res[0]


def _rmsnorm_fwd(name, x, gain):
    S_, D_ = x.shape

    def body(x_ref, g_ref, h_ref):
        xf = x_ref[...]
        ms = jnp.mean(xf * xf, axis=-1, keepdims=True)
        h_ref[...] = (xf * lax.rsqrt(ms + EPS) * g_ref[...]).astype(BF16)

    return pl.pallas_call(
        body, name=name, grid=(S_ // ROW_TILE,),
        in_specs=[pl.BlockSpec((ROW_TILE, D_), lambda i: (i, 0)), pl.BlockSpec((1, D_), lambda i: (0, 0))],
        out_specs=pl.BlockSpec((ROW_TILE, D_), lambda i: (i, 0)),
        out_shape=jax.ShapeDtypeStruct((S_, D_), BF16),
        compiler_params=_cparams(("parallel",)),
    )(x, gain)


def _rmsnorm_bwd(name, x, gain, dh, dres, after=None):
    S_, D_ = x.shape
    n = S_ // ROW_TILE

    def body(x_ref, g_ref, dh_ref, dres_ref, dx_ref, dg_ref, acc_ref):
        i = pl.program_id(0)
        xf = x_ref[...]
        rstd = lax.rsqrt(jnp.mean(xf * xf, axis=-1, keepdims=True) + EPS)
        xhat = xf * rstd
        dhv = dh_ref[...]
        dxhat = dhv * g_ref[...]
        c = jnp.mean(dxhat * xhat, axis=-1, keepdims=True)
        dx_ref[...] = dres_ref[...] + rstd * (dxhat - xhat * c)
        part = jnp.sum((dhv * xhat).reshape(ROW_TILE // 8, 8, D_), axis=0)

        @pl.when(i == 0)
        def _():
            acc_ref[...] = part

        @pl.when(i > 0)
        def _():
            acc_ref[...] += part

        @pl.when(i == n - 1)
        def _():
            dg_ref[...] = jnp.sum(acc_ref[...], axis=0, keepdims=True)

    row = pl.BlockSpec((ROW_TILE, D_), lambda i: (i, 0))
    vec = pl.BlockSpec((1, D_), lambda i: (0, 0))
    return _call(
        body, [x, gain, dh, dres], [row, vec, row, row], after, name=name, grid=(n,), out_specs=(row, vec),
        out_shape=(jax.ShapeDtypeStruct((S_, D_), F32), jax.ShapeDtypeStruct((1, D_), F32)),
        scratch_shapes=[pltpu.VMEM((8, D_), F32)],
        compiler_params=_cparams(("arbitrary",)))


def _loss_head(name, y, target):
    S_, D_ = y.shape
    n = S_ // ROW_TILE

    def body(y_ref, t_ref, dy_ref, l_ref, acc_ref):
        i = pl.program_id(0)
        e = y_ref[...] - t_ref[...]
        dy_ref[...] = e * (1.0 / D_)
        part = jnp.sum((e * e).reshape(ROW_TILE // 8, 8, D_), axis=0)

        @pl.when(i == 0)
        def _():
            acc_ref[...] = part

        @pl.when(i > 0)
        def _():
            acc_ref[...] += part

        @pl.when(i == n - 1)
        def _():
            s = jnp.sum(acc_ref[...], axis=1, keepdims=True)
            l_ref[...] = jnp.sum(s, axis=0, keepdims=True) * (0.5 / D_)

    row = pl.BlockSpec((ROW_TILE, D_), lambda i: (i, 0))
    return pl.pallas_call(
        body, name=name, grid=(n,), in_specs=[row, row],
        out_specs=(row, pl.BlockSpec((1, 1), lambda i: (0, 0))),
        out_shape=(jax.ShapeDtypeStruct((S_, D_), F32), jax.ShapeDtypeStruct((1, 1), F32)),
        scratch_shapes=[pltpu.VMEM((8, D_), F32)],
        compiler_params=_cparams(("arbitrary",)),
    )(y, target)


def _rows_with_halo(ref, cols, i, n_chunks, before, after):
    r0 = pl.multiple_of(i * SEQ_CHUNK, SEQ_CHUNK)
    parts = []
    if before:
        h0 = pl.multiple_of(jnp.maximum(r0 - HALO, 0), 8)
        halo = ref[pl.ds(h0, HALO), cols]
        parts.append(jnp.where(i > 0, halo, jnp.zeros_like(halo)))
    parts.append(ref[pl.ds(r0, SEQ_CHUNK), cols])
    if after:
        a0 = pl.multiple_of(jnp.minimum(r0 + SEQ_CHUNK, (n_chunks - 1) * SEQ_CHUNK + SEQ_CHUNK - HALO), 8)
        halo = ref[pl.ds(a0, HALO), cols]
        parts.append(jnp.where(i < n_chunks - 1, halo, jnp.zeros_like(halo)))
    return parts[0] if len(parts) == 1 else jnp.concatenate(parts, axis=0)


def _shift_down(v, k):
    return pltpu.roll(v, k, 0)


def _shift_up(v, k):
    return pltpu.roll(v, v.shape[0] - k, 0)


def _pool_diff(xx, w, t_main):
    s = xx
    k = 1
    while k < w:
        s = s + _shift_down(s, k)
        k *= 2
    cnt = jnp.minimum(t_main + 1, w).astype(F32)
    return s[HALO:] / cnt - xx[HALO:]


def _pool_fwd(name, z, pool_mix, pool_scale):
    S_ = z.shape[0]
    n_chunks = S_ // SEQ_CHUNK

    cols = slice(0, POOL_G)

    def body(u_ref, mix_ref, sc_ref, y_ref):
        mixg = mix_ref[...].astype(BF16)
        scg = sc_ref[...]
        for g, w in enumerate(POOL_WINDOWS):
            @pl.when(pl.program_id(0) == g)
            def _(w=w):
                def chunk(i, carry):
                    r0 = pl.multiple_of(i * SEQ_CHUNK, SEQ_CHUNK)
                    xx = _rows_with_halo(u_ref, cols, i, n_chunks, True, False)
                    t = r0 + lax.broadcasted_iota(jnp.int32, (SEQ_CHUNK, POOL_G), 0)
                    d = _pool_diff(xx, w, t)
                    y = jnp.dot(d.astype(BF16), mixg, preferred_element_type=F32) * scg
                    y_ref[pl.ds(r0, SEQ_CHUNK), :] = y.astype(BF16)
                    return carry

                lax.fori_loop(0, n_chunks, chunk, 0)

    slab = pl.BlockSpec((S_, POOL_G), lambda g: (0, g))
    return pl.pallas_call(
        body, name=name, grid=(len(POOL_WINDOWS),),
        in_specs=[slab, pl.BlockSpec((None, POOL_G, POOL_G), lambda g: (g, 0, 0)),
                  pl.BlockSpec((1, POOL_G), lambda g: (0, g))],
        out_specs=slab, out_shape=jax.ShapeDtypeStruct((S_, POOL_W), BF16),
        compiler_params=_cparams(("parallel",)),
    )(z, pool_mix, pool_scale)


def _pool_bwd(name, z, dy, pool_mix, pool_scale, dz, after=None):
    S_ = z.shape[0]
    n_chunks = S_ // SEQ_CHUNK
    rows_a = SEQ_CHUNK + HALO

    cols = slice(0, POOL_G)

    def body(u_ref, dy_ref, mix_ref, sc_ref, du_ref, dmix_ref, dsc_ref):
        mixg = mix_ref[...].astype(BF16)
        scg = sc_ref[...]
        for g, w in enumerate(POOL_WINDOWS):
            @pl.when(pl.program_id(0) == g)
            def _(w=w):
                def chunk(i, carry):
                    dmix_acc, dsc_acc = carry
                    r0 = pl.multiple_of(i * SEQ_CHUNK, SEQ_CHUNK)
                    xx = _rows_with_halo(u_ref, cols, i, n_chunks, True, False)
                    t = r0 + lax.broadcasted_iota(jnp.int32, (SEQ_CHUNK, POOL_G), 0)
                    d = _pool_diff(xx, w, t).astype(BF16)
                    ypre = jnp.dot(d, mixg, preferred_element_type=F32)
                    dyy = _rows_with_halo(dy_ref, cols, i, n_chunks, False, True)
                    dys = (dyy * scg).astype(BF16)
                    dsc_acc = dsc_acc + jnp.sum((dyy[:SEQ_CHUNK] * ypre).reshape(SEQ_CHUNK // 8, 8, POOL_G), axis=0)
                    dmix_acc = dmix_acc + lax.dot_general(d, dys[:SEQ_CHUNK], _DIMS["tn"],
                                                          preferred_element_type=F32)
                    dd = lax.dot_general(dys, mixg, _DIMS["nt"], preferred_element_type=F32)
                    ta = r0 + lax.broadcasted_iota(jnp.int32, (rows_a, POOL_G), 0)
                    f = dd / jnp.minimum(ta + 1, w).astype(F32)
                    k = 1
                    while k < w:
                        f = f + _shift_up(f, k)
                        k *= 2
                    du_ref[pl.ds(r0, SEQ_CHUNK), :] = (f[:SEQ_CHUNK] - dd[:SEQ_CHUNK]).astype(BF16)
                    return dmix_acc, dsc_acc

                dmix_acc, dsc_acc = lax.fori_loop(
                    0, n_chunks, chunk, (jnp.zeros((POOL_G, POOL_G), F32), jnp.zeros((8, POOL_G), F32)))
                dmix_ref[...] = dmix_acc
                dsc_ref[...] = jnp.sum(dsc_acc, axis=0, keepdims=True)

    slab = pl.BlockSpec((S_, POOL_G), lambda g: (0, g))
    mix_spec = pl.BlockSpec((None, POOL_G, POOL_G), lambda g: (g, 0, 0))
    vec = pl.BlockSpec((1, POOL_G), lambda g: (0, g))
    return _call(
        _drop_ref(body, 4), [z, dy, pool_mix, pool_scale, dz], [slab, slab, mix_spec, vec, ANY], after, name=name,
        grid=(len(POOL_WINDOWS),), out_specs=(slab, mix_spec, vec),
        out_shape=(jax.ShapeDtypeStruct(dz.shape, dz.dtype), jax.ShapeDtypeStruct((4, POOL_G, POOL_G), F32),
                   jax.ShapeDtypeStruct((1, POOL_W), F32)),
        input_output_aliases={4: 0}, compiler_params=_cparams(("parallel",)))


def _conv_specs(S_):
    slab = lambda off: pl.BlockSpec((S_, 128), lambda c, off=off: (0, off // 128 + c))
    return slab(OFF_CB), slab(OFF_CC), slab(OFF_CX)


def _conv_fwd(name, z, conv_w):
    S_ = z.shape[0]
    n_chunks = S_ // SEQ_CHUNK
    col = slice(0, 128)

    def body(b_ref, c_ref, x_ref, w_ref, y_ref):
        w0, w1, w2 = w_ref[0:1, :], w_ref[1:2, :], w_ref[2:3, :]

        def chunk(i, carry):
            r0 = pl.multiple_of(i * SEQ_CHUNK, SEQ_CHUNK)
            u = _rows_with_halo(c_ref, col, i, n_chunks, True, False) * _rows_with_halo(x_ref, col, i, n_chunks, True, False)
            y = w2 * u + w1 * _shift_down(u, 1) + w0 * _shift_down(u, 2)
            y_ref[pl.ds(r0, SEQ_CHUNK), :] = (b_ref[pl.ds(r0, SEQ_CHUNK), :] * y[HALO:]).astype(BF16)
            return carry

        lax.fori_loop(0, n_chunks, chunk, 0)

    sb, sc, sx = _conv_specs(S_)
    return pl.pallas_call(
        body, name=name, grid=(CONV_W // 128,),
        in_specs=[sb, sc, sx, pl.BlockSpec((3, 128), lambda c: (0, c))],
        out_specs=pl.BlockSpec((S_, 128), lambda c: (0, c)),
        out_shape=jax.ShapeDtypeStruct((S_, CONV_W), BF16),
        compiler_params=_cparams(("parallel",)),
    )(z, z, z, conv_w)


def _conv_bwd(name, z, dy, conv_w, after=None):
    S_ = z.shape[0]
    n_chunks = S_ // SEQ_CHUNK
    col = slice(0, 128)
    lo, hi = HALO, HALO + SEQ_CHUNK

    def body(b_ref, c_ref, x_ref, dy_ref, w_ref, db_ref, dc_ref, dx_ref, dw_ref):
        w0, w1, w2 = w_ref[0:1, :], w_ref[1:2, :], w_ref[2:3, :]

        def chunk(i, carry):
            a0, a1, a2 = carry
            r0 = pl.multiple_of(i * SEQ_CHUNK, SEQ_CHUNK)
            cc = _rows_with_halo(c_ref, col, i, n_chunks, True, True)
            xx = _rows_with_halo(x_ref, col, i, n_chunks, True, True)
            bb = _rows_with_halo(b_ref, col, i, n_chunks, True, True)
            dyy = _rows_with_halo(dy_ref, col, i, n_chunks, True, True)
            u = cc * xx
            u1 = _shift_down(u, 1)
            u2 = _shift_down(u, 2)
            y = w2 * u + w1 * u1 + w0 * u2
            dyv = dyy * bb
            du = w2 * dyv + w1 * _shift_up(dyv, 1) + w0 * _shift_up(dyv, 2)
            db_ref[pl.ds(r0, SEQ_CHUNK), :] = (dyy[lo:hi] * y[lo:hi]).astype(BF16)
            dc_ref[pl.ds(r0, SEQ_CHUNK), :] = (du[lo:hi] * xx[lo:hi]).astype(BF16)
            dx_ref[pl.ds(r0, SEQ_CHUNK), :] = (du[lo:hi] * cc[lo:hi]).astype(BF16)
            red = lambda v: jnp.sum(v.reshape(SEQ_CHUNK // 8, 8, 128), axis=0)
            dm = dyv[lo:hi]
            return a0 + red(dm * u2[lo:hi]), a1 + red(dm * u1[lo:hi]), a2 + red(dm * u[lo:hi])

        zero = jnp.zeros((8, 128), F32)
        a0, a1, a2 = lax.fori_loop(0, n_chunks, chunk, (zero, zero, zero))
        dw_ref[0:1, :] = jnp.sum(a0, axis=0, keepdims=True)
        dw_ref[1:2, :] = jnp.sum(a1, axis=0, keepdims=True)
        dw_ref[2:3, :] = jnp.sum(a2, axis=0, keepdims=True)

    sb, sc, sx = _conv_specs(S_)
    slab = pl.BlockSpec((S_, 128), lambda c: (0, c))
    wspec = pl.BlockSpec((3, 128), lambda c: (0, c))
    act = jax.ShapeDtypeStruct((S_, CONV_W), BF16)
    return _call(
        body, [z, z, z, dy, conv_w], [sb, sc, sx, slab, wspec], after, name=name, grid=(CONV_W // 128,),
        out_specs=(slab, slab, slab, wspec),
        out_shape=(act, act, act, jax.ShapeDtypeStruct((3, CONV_W), F32)),
        compiler_params=_cparams(("parallel",)))


def _head_ones(pw):
    a = lax.broadcasted_iota(jnp.int32, (pw, pw), 0) // HEAD
    b = lax.broadcasted_iota(jnp.int32, (pw, pw), 1) // HEAD
    return (a == b).astype(BF16)


def _head_sum(v, ones):
    hi = v.astype(BF16)
    lo = (v - hi.astype(F32)).astype(BF16)
    return jnp.dot(hi, ones, preferred_element_type=F32) + jnp.dot(lo, ones, preferred_element_type=F32)


def _head_norm(v, gain, ones):
    rstd = lax.rsqrt(_head_sum(v * v, ones) * (1.0 / HEAD) + EPS)
    xhat = v * rstd
    return xhat * gain, xhat, rstd


def _head_norm_bwd(dy, xhat, rstd, gain, ones):
    dxhat = dy * gain
    c = _head_sum(dxhat * xhat, ones) * (1.0 / HEAD)
    dv = rstd * (dxhat - xhat * c)
    dg = jnp.sum((dy * xhat).reshape(dy.shape[0] // 8, 8, dy.shape[1]), axis=0)
    return dv, dg


def _head_masks(pw):
    lane_head = lax.broadcasted_iota(jnp.int32, (1, pw), 1) // HEAD
    return [lane_head == h for h in range(pw // HEAD)]


def _only(mask, v):
    return jnp.where(mask, v, jnp.zeros_like(v))


def _attn_specs(S_, g, dil):
    rows = ATTN_BLK * dil
    nb = S_ // rows
    pw = 128 if dil > 1 else ATTN_OUT
    cq, ck, cv = ((OFF_Q + g * ATTN_OUT) // pw, (OFF_K + g * ATTN_OUT) // pw, (OFF_V + g * ATTN_OUT) // pw)
    return rows, nb, pw, pw // HEAD, ATTN_OUT // pw, cq, ck, cv


ATTN_BATCH = 4


def _attn_group(dil):
    return 4 if dil == 1 else 1


def _attn_block_specs(rows, grp, pw, last=None):
    step = (lambda n: n) if last is None else (lambda n: jnp.minimum(n, last))
    cur = lambda c: pl.BlockSpec((rows * grp, pw), lambda hp, n, c=c: (step(n), c + hp))
    prev = lambda c: pl.BlockSpec((rows, pw), lambda hp, n, c=c: (jnp.maximum(step(n) * grp - 1, 0), c + hp))
    return cur, prev


def _band_mask(has_prev):
    qi = lax.broadcasted_iota(jnp.int32, (ATTN_BLK, 2 * ATTN_BLK), 0)
    ki = lax.broadcasted_iota(jnp.int32, (ATTN_BLK, 2 * ATTN_BLK), 1)
    in_prev = jnp.logical_and(ki < ATTN_BLK, ki >= qi)
    if has_prev is not True:
        in_prev = jnp.logical_and(in_prev, has_prev)
    return jnp.logical_or(in_prev, jnp.logical_and(ki >= ATTN_BLK, ki - ATTN_BLK <= qi))


def _rows_of(ref, r, dil):
    if dil == 1:
        return ref[r * ATTN_BLK:(r + 1) * ATTN_BLK, :]
    return ref[pl.ds(r, ATTN_BLK, stride=dil), :]


def _put_rows(ref, r, dil, val):
    if dil == 1:
        ref[r * ATTN_BLK:(r + 1) * ATTN_BLK, :] = val.astype(ref.dtype)
    else:
        ref[pl.ds(r, ATTN_BLK, stride=dil), :] = val.astype(ref.dtype)


def _attn_fwd(name, z, q_gain, k_gain, g, dil):
    S_ = z.shape[0]
    rows, nb, pw, heads, npairs, cq, ck, cv = _attn_specs(S_, g, dil)
    scale = HEAD ** -0.5

    grp = _attn_group(dil)
    nsteps = nb // grp

    def body(q_ref, k_ref, kp_ref, v_ref, vp_ref, gq_ref, gk_ref, o_ref, l_ref):
        n = pl.program_id(1)
        ones, hmask = _head_ones(pw), _head_masks(pw)
        gq, gk = jnp.tile(gq_ref[...], (1, heads)), jnp.tile(gk_ref[...], (1, heads))
        mask_first, mask_rest = _band_mask(n > 0), _band_mask(True)
        for r0 in range(0, dil * grp, ATTN_BATCH):
            rs = range(r0, min(r0 + ATTN_BATCH, dil * grp))
            qn, kn, vv, s, p = {}, {}, {}, {}, {}
            kcn = {}
            for r in rs:
                q, kc, vc = _rows_of(q_ref, r, dil), _rows_of(k_ref, r, dil), _rows_of(v_ref, r, dil)
                kcn[r] = _head_norm(kc, gk, ones)[0]
                if dil == 1 and r > 0:
                    kpn = kcn[r - 1] if r - 1 in kcn else _head_norm(_rows_of(k_ref, r - 1, dil), gk, ones)[0]
                    vp = _rows_of(v_ref, r - 1, dil)
                else:
                    kpn, vp = _head_norm(_rows_of(kp_ref, r, dil), gk, ones)[0], _rows_of(vp_ref, r, dil)
                qn[r] = _head_norm(q, gq, ones)[0].astype(BF16)
                kn[r] = jnp.concatenate([kpn, kcn[r]], axis=0).astype(BF16)
                vv[r] = jnp.concatenate([vp, vc], axis=0).astype(BF16)
            keys = [(r, h) for r in rs for h in range(heads)]
            for r, h in keys:
                s[r, h] = lax.dot_general(_only(hmask[h], qn[r]), kn[r], _DIMS["nt"],
                                          preferred_element_type=F32) * scale
            lse, den = {}, {}
            for key in keys:
                mask = mask_rest if (dil == 1 and key[0] > 0) else mask_first
                sm = jnp.where(mask, s[key], MASK)
                m = jnp.max(sm, axis=-1, keepdims=True)
                e = jnp.exp(sm - m)
                den[key] = jnp.sum(e, axis=-1, keepdims=True)
                p[key] = e.astype(BF16)
                lse[key] = m + jnp.log(den[key])
            for r in rs:
                out = jnp.zeros((ATTN_BLK, pw), F32)
                lse_all = jnp.zeros((ATTN_BLK, pw), F32)
                for h in range(heads):
                    out = jnp.where(hmask[h], jnp.dot(p[r, h], vv[r], preferred_element_type=F32) / den[r, h], out)
                    lse_all = jnp.where(hmask[h], lse[r, h], lse_all)
                _put_rows(o_ref, r, dil, out)
                _put_rows(l_ref, r, dil, lse_all)

    cur, prev = _attn_block_specs(rows, grp, pw)
    gspec = pl.BlockSpec((1, HEAD), lambda hp, n: (0, 0))
    shp = jax.ShapeDtypeStruct((S_, ATTN_OUT), F32)
    return pl.pallas_call(
        body, name=name, grid=(npairs, nsteps),
        in_specs=[cur(cq), cur(ck), prev(ck), cur(cv), prev(cv), gspec, gspec],
        out_specs=(cur(0), cur(0)), out_shape=(shp, shp),
        compiler_params=_cparams(("parallel", "parallel")),
    )(z, z, z, z, z, q_gain, k_gain)


def _attn_combine(name, os_, ls_):
    S_ = os_[0].shape[0]

    def body(o0, o1, o2, l0, l1, l2, o_ref, l_ref):
        a, b, c = l0[...], l1[...], l2[...]
        m = jnp.maximum(jnp.maximum(a, b), c)
        ea, eb, ec = jnp.exp(a - m), jnp.exp(b - m), jnp.exp(c - m)
        zsum = ea + eb + ec
        o_ref[...] = (ea * o0[...] + eb * o1[...] + ec * o2[...]) / zsum
        l_ref[...] = m + jnp.log(zsum)

    row = pl.BlockSpec((ROW_TILE, ATTN_OUT), lambda i: (i, 0))
    shp = jax.ShapeDtypeStruct((S_, ATTN_OUT), F32)
    return pl.pallas_call(
        body, name=name, grid=(S_ // ROW_TILE,), in_specs=[row] * 6, out_specs=(row, row), out_shape=(shp, shp),
        compiler_params=_cparams(("parallel",)),
    )(*os_, *ls_)


def _attn_bwd(name, z, q_gain, k_gain, do, o, lse, g, dil, after=None):
    S_ = z.shape[0]
    rows, nb, pw, heads, npairs, cq, ck, cv = _attn_specs(S_, g, dil)
    scale = HEAD ** -0.5

    def body(q_ref, kc_ref, kp_ref, vc_ref, vp_ref, gq_ref, gk_ref, do_ref, o_ref, l_ref,
             dq_ref, dk_ref, dv_ref, dgq_ref, dgk_ref, ck_ref, cvv_ref, gq_acc, gk_acc):
        hp = pl.program_id(0)
        n = pl.program_id(1)
        live = n < nb
        mask = jnp.logical_and(_band_mask(n > 0), live)
        ones, hmask = _head_ones(pw), _head_masks(pw)
        gq, gk = jnp.tile(gq_ref[...], (1, heads)), jnp.tile(gk_ref[...], (1, heads))

        @pl.when(n == 0)
        def _():
            ck_ref[...] = jnp.zeros_like(ck_ref)
            cvv_ref[...] = jnp.zeros_like(cvv_ref)

        @pl.when(jnp.logical_and(n == 0, hp == 0))
        def _():
            gq_acc[...] = jnp.zeros_like(gq_acc)
            gk_acc[...] = jnp.zeros_like(gk_acc)

        dgq = jnp.zeros((8, pw), F32)
        dgk = jnp.zeros((8, pw), F32)
        for r0 in range(0, dil, ATTN_BATCH):
            rs = range(r0, min(r0 + ATTN_BATCH, dil))
            keys = [(r, h) for r in rs for h in range(heads)]
            qn, qhat, qrstd, kn, kphat, kprstd, vv, dob, delta, lse = ({} for _ in range(10))
            for r in rs:
                q, kc, kp = _rows_of(q_ref, r, dil), _rows_of(kc_ref, r, dil), _rows_of(kp_ref, r, dil)
                dov = _rows_of(do_ref, r, dil)
                qn_f, qhat[r], qrstd[r] = _head_norm(q, gq, ones)
                kpn, kphat[r], kprstd[r] = _head_norm(kp, gk, ones)
                qn[r] = qn_f.astype(BF16)
                kn[r] = jnp.concatenate([kpn, _head_norm(kc, gk, ones)[0]], axis=0).astype(BF16)
                vv[r] = jnp.concatenate([_rows_of(vp_ref, r, dil), _rows_of(vc_ref, r, dil)], axis=0).astype(BF16)
                dob[r] = dov.astype(BF16)
                delta[r] = _head_sum(dov * _rows_of(o_ref, r, dil), ones)
                lse[r] = _rows_of(l_ref, r, dil)
            s, dp = {}, {}
            for r, h in keys:
                s[r, h] = lax.dot_general(_only(hmask[h], qn[r]), kn[r], _DIMS["nt"],
                                          preferred_element_type=F32) * scale
                dp[r, h] = lax.dot_general(_only(hmask[h], dob[r]), vv[r], _DIMS["nt"], preferred_element_type=F32)
            p, ds = {}, {}
            for r, h in keys:
                col = slice(h * HEAD, h * HEAD + 1)
                pk = jnp.where(mask, jnp.exp(jnp.where(mask, s[r, h], MASK) - lse[r][:, col]), 0.0)
                ds[r, h] = (pk * (dp[r, h] - delta[r][:, col]) * scale).astype(BF16)
                p[r, h] = pk.astype(BF16)
            dqn, dkn, dvv = {}, {}, {}
            for r in rs:
                dqn[r] = jnp.zeros((ATTN_BLK, pw), F32)
                dkn[r] = jnp.zeros((2 * ATTN_BLK, pw), F32)
                dvv[r] = jnp.zeros((2 * ATTN_BLK, pw), F32)
                for h in range(heads):
                    dqn[r] = jnp.where(hmask[h], jnp.dot(ds[r, h], kn[r], preferred_element_type=F32), dqn[r])
                    dkn[r] = jnp.where(hmask[h], lax.dot_general(ds[r, h], qn[r], _DIMS["tn"],
                                                                 preferred_element_type=F32), dkn[r])
                    dvv[r] = jnp.where(hmask[h], lax.dot_general(p[r, h], dob[r], _DIMS["tn"],
                                                                 preferred_element_type=F32), dvv[r])
            for r in rs:
                dq_all, dg = _head_norm_bwd(dqn[r], qhat[r], qrstd[r], gq, ones)
                dgq = dgq + dg
                dk_all, dg = _head_norm_bwd(_rows_of(ck_ref, r, dil) + dkn[r][:ATTN_BLK], kphat[r], kprstd[r], gk, ones)
                dgk = dgk + dg
                dv_all = _rows_of(cvv_ref, r, dil) + dvv[r][:ATTN_BLK]

                @pl.when(live)
                def _(dq_all=dq_all, r=r):
                    _put_rows(dq_ref, r, dil, dq_all)

                _put_rows(dk_ref, r, dil, dk_all)
                _put_rows(dv_ref, r, dil, dv_all)
                _put_rows(ck_ref, r, dil, dkn[r][ATTN_BLK:])
                _put_rows(cvv_ref, r, dil, dvv[r][ATTN_BLK:])
        gq_acc[...] += dgq
        gk_acc[...] += dgk

        @pl.when(jnp.logical_and(n == nb, hp == npairs - 1))
        def _():
            fold = lambda a: sum(a[:, h * HEAD:(h + 1) * HEAD] for h in range(heads))
            dgq_ref[...] = jnp.sum(fold(gq_acc[...]), axis=0, keepdims=True)
            dgk_ref[...] = jnp.sum(fold(gk_acc[...]), axis=0, keepdims=True)

    last = nb - 1
    cur = lambda c: pl.BlockSpec((rows, pw), lambda hp, n, c=c: (jnp.minimum(n, last), c + hp))
    prev = lambda c: pl.BlockSpec((rows, pw), lambda hp, n, c=c: (jnp.maximum(n - 1, 0), c + hp))
    gspec = pl.BlockSpec((1, HEAD), lambda hp, n: (0, 0))
    act = jax.ShapeDtypeStruct((S_, ATTN_OUT), F32)
    vec = jax.ShapeDtypeStruct((1, HEAD), F32)
    return _call(
        body, [z, z, z, z, z, q_gain, k_gain, do, o, lse],
        [cur(cq), cur(ck), prev(ck), cur(cv), prev(cv), gspec, gspec, cur(0), cur(0), cur(0)], after,
        name=name, grid=(npairs, nb + 1),
        out_specs=(cur(0), prev(0), prev(0), gspec, gspec),
        out_shape=(act, act, act, vec, vec),
        scratch_shapes=[pltpu.VMEM((rows, pw), F32), pltpu.VMEM((rows, pw), F32),
                        pltpu.VMEM((8, pw), F32), pltpu.VMEM((8, pw), F32)],
        compiler_params=_cparams(("arbitrary", "arbitrary")))


MIX_TN = 256
MIX_TM = 2048


def _sigmoid(v):
    return 1.0 / (1.0 + jnp.exp(-v))


def _mix_fwd(name, z, b_gate, ys, ws):
    S_ = z.shape[0]
    tm, tn = MIX_TM, MIX_TN
    gblk = OFF_GATE // tn

    def body(yp, yc, ya, wp, wc, wa, g0, g1, g2, b0, b1, b2, m_ref):
        acc = None
        for y_ref, w_ref, g_ref, b_ref in ((yp, wp, g0, b0), (yc, wc, g1, b1), (ya, wa, g2, b2)):
            u = lax.dot_general(y_ref[...].astype(BF16), w_ref[...], _DIMS["nt"], preferred_element_type=F32)
            t = _sigmoid(g_ref[...] + b_ref[...]) * u
            acc = t if acc is None else acc + t
        m_ref[...] = acc.astype(BF16)

    yspec = lambda w: pl.BlockSpec((tm, w), lambda i, j: (i, 0))
    wspec = lambda w: pl.BlockSpec((tn, w), lambda i, j: (j, 0))
    gspec = lambda b: pl.BlockSpec((tm, tn), lambda i, j, b=b: (i, gblk + b * (D // tn) + j))
    bspec = lambda b: pl.BlockSpec((1, tn), lambda i, j, b=b: (0, b * (D // tn) + j))
    return pl.pallas_call(
        body, name=name, grid=(S_ // tm, D // tn),
        in_specs=[yspec(POOL_W), yspec(CONV_W), yspec(ATTN_OUT), wspec(POOL_W), wspec(CONV_W), wspec(ATTN_OUT),
                  gspec(0), gspec(1), gspec(2), bspec(0), bspec(1), bspec(2)],
        out_specs=pl.BlockSpec((tm, tn), lambda i, j: (i, j)),
        out_shape=jax.ShapeDtypeStruct((S_, D), BF16),
        compiler_params=_cparams(("parallel", "parallel")),
    )(*ys, *ws, z, z, z, b_gate, b_gate, b_gate)


def _mix_bwd(name, z, b_gate, y, w, dmerged, branch, dz, after=None):
    S_ = z.shape[0]
    tm, tn = MIX_TM, MIX_TN
    width = y.shape[1]
    gblk = OFF_GATE // tn + branch * (D // tn)
    nj = D // tn

    def body(y_ref, w_ref, g_ref, b_ref, dm_ref, dy_ref, dw_ref, dg_ref, db_ref, acc_ref):
        i, j = pl.program_id(0), pl.program_id(1)
        yb = y_ref[...].astype(BF16)
        u = lax.dot_general(yb, w_ref[...], _DIMS["nt"], preferred_element_type=F32)
        sg = _sigmoid(g_ref[...] + b_ref[...])
        dm = dm_ref[...]
        du = (sg * dm).astype(BF16)
        dpre = dm * u * sg * (1.0 - sg)
        dg_ref[...] = dpre.astype(BF16)
        cols = pl.ds(pl.multiple_of(j * tn, tn), tn)
        db_ref[...] = _rows8(dpre)
        d_w = lax.dot_general(du, yb, _DIMS["tn"], preferred_element_type=F32)
        d_y = jnp.dot(du, w_ref[...], preferred_element_type=F32)

        @pl.when(i == 0)
        def _():
            dw_ref[cols, :] = d_w

        @pl.when(i > 0)
        def _():
            dw_ref[cols, :] += d_w

        @pl.when(j == 0)
        def _():
            acc_ref[...] = d_y

        @pl.when(j > 0)
        def _():
            acc_ref[...] += d_y

        @pl.when(j == nj - 1)
        def _():
            dy_ref[...] = acc_ref[...]

    rows = pl.BlockSpec((tm, width), lambda i, j: (i, 0))
    blk = pl.BlockSpec((tm, tn), lambda i, j: (i, j))
    gate = pl.BlockSpec((tm, tn), lambda i, j: (i, gblk + j))
    args = [y, w, z, b_gate, dmerged]
    in_specs = [rows, pl.BlockSpec((tn, width), lambda i, j: (j, 0)), gate,
                pl.BlockSpec((1, tn), lambda i, j: (0, branch * nj + j)), blk]
    aliases = {}
    if not isinstance(dz, jax.ShapeDtypeStruct):
        body = _drop_ref(body, len(args))
        aliases = {len(args): 2}
        args.append(dz)
        in_specs.append(ANY)
    return _call(
        body, args, in_specs, after, name=name, grid=(S_ // tm, nj),
        out_specs=(rows, pl.BlockSpec((D, width), lambda i, j: (0, 0)), gate, pl.BlockSpec((8, tn), lambda i, j: (i, j))),
        out_shape=(jax.ShapeDtypeStruct((S_, width), F32), jax.ShapeDtypeStruct((D, width), F32),
                   jax.ShapeDtypeStruct(dz.shape, dz.dtype), jax.ShapeDtypeStruct((S_ // tm * 8, D), F32)),
        scratch_shapes=[pltpu.VMEM((tm, width), F32)], input_output_aliases=aliases,
        compiler_params=_cparams(("arbitrary", "arbitrary")))


def _relu2_epilogue(acc):
    r = jnp.maximum(acc, 0.0)
    return (r * r,)


def _relu2_bwd_epilogue(acc, r):
    return (acc * (2.0 * jnp.sqrt(r.astype(F32))),)


def _residual_norm_epilogue(acc, xr, gain):
    x = xr + acc
    ms = jnp.mean(x * x, axis=-1, keepdims=True)
    return x, x * lax.rsqrt(ms + EPS) * gain


def _fold_rows(name, part, lanes=False, scale=1.0):
    R, N = part.shape

    def body(p_ref, o_ref):
        s = jnp.sum(p_ref[...], axis=0, keepdims=True)
        if lanes:
            s = jnp.sum(s, axis=1, keepdims=True)
        o_ref[...] = s * scale

    out_n = 1 if lanes else N
    return pl.pallas_call(
        body, name=name, grid=(1,), in_specs=[pl.BlockSpec((R, N), lambda i: (0, 0))],
        out_specs=pl.BlockSpec((1, out_n), lambda i: (0, 0)), out_shape=jax.ShapeDtypeStruct((1, out_n), F32))(part)


def _rows8(v):
    return jnp.sum(v.reshape(v.shape[0] // 8, 8, v.shape[1]), axis=0)


def _residual_loss_epilogue(acc, xr, target):
    e = xr + acc - target
    return e * (1.0 / D), _rows8(e * e)


def _norm_bwd_epilogue(acc, x, dres, gain):
    rstd = lax.rsqrt(jnp.mean(x * x, axis=-1, keepdims=True) + EPS)
    xhat = x * rstd
    dxhat = acc * gain
    c = jnp.mean(dxhat * xhat, axis=-1, keepdims=True)
    return dres + rstd * (dxhat - xhat * c), _rows8(acc * xhat)


def _layer_fwd(l, x, h, p, next_gain, target=None, early=None):
    t = f"l{l}_"
    z = _matmul(t + "in_proj", h, p["w_in_t"], "nt", 512, 3712, 1024, after=early)
    y_pool = _pool_fwd(t + "pool", z, p["pool_mix"], p["pool_scale"])
    y_conv = _conv_fwd(t + "conv", z, p["conv_w"])
    os_, ls_ = [], []
    for g, (_, dil) in enumerate(ATTN_GROUPS):
        o_g, l_g = _attn_fwd(t + f"attn{g}", z, p["q_gain"], p["k_gain"], g, dil)
        os_.append(o_g)
        ls_.append(l_g)
    y_attn, lse = _attn_combine(t + "attn_mix", os_, ls_)
    merged = _mix_fwd(t + "merge", z, p["b_gate"], (y_pool, y_conv, y_attn),
                      (p["w_pool_up_t"], p["w_conv_out_t"], p["w_attn_up_t"]))
    x1, h2 = _matmul(t + "out_proj", merged, p["w_o"], "nn", 1024, D, 1024, out_dtypes=(F32, BF16),
                     extras=((x, "mn"), (p["norm_mlp"], "n")), epilogue=_residual_norm_epilogue)
    r = _matmul(t + "ff1", h2, p["w_ff1_t"], "nt", 1024, 1024, 1024, out_dtypes=(BF16,), epilogue=_relu2_epilogue)
    if next_gain is None:
        x2, h_out = _matmul(t + "ff2", r, p["w_ff2"], "nn", 512, D, D_FF, out_dtypes=(F32, (F32, "rows8")),
                            extras=((x1, "mn"), (target, "mn")), epilogue=_residual_loss_epilogue)
    else:
        x2, h_out = _matmul(t + "ff2", r, p["w_ff2"], "nn", 512, D, D_FF, out_dtypes=(F32, BF16),
                            extras=((x1, "mn"), (next_gain, "n")), epilogue=_residual_norm_epilogue)
    saved = dict(x=x, h=h, z=z, y_pool=y_pool, y_conv=y_conv, y_attn=y_attn, lse=lse, merged=merged,
                 x1=x1, h2=h2, r=r)
    return x2, h_out, saved


def _layer_bwd(l, dx2, p, s, pending, collective_id, last, start_after=None):
    t = f"l{l}_b_"
    g = {}
    rest = jax.ShapeDtypeStruct((N_DEV, REST_ROWS, D), F32)
    da = _matmul(t + "d_ff2_in", dx2, p["w_ff2"], "nt", 1024, 1024, 1024, out_dtypes=(BF16,),
                 extras=((s["r"], "mn"),), epilogue=_relu2_bwd_epilogue,
                 after=start_after)
    rest = _matmul(t + "dw_ff2", s["r"], dx2, "tn", 512, 1024, 4096, into=(rest, REST_FF2, 0), after=da)
    tok = rest if pending is None else pending.chip_sums(after=rest)
    dx1, part = _matmul(t + "d_ff1_in", da, p["w_ff1_t"], "nn", 512, D, D_FF, out_dtypes=(F32, (F32, "rows8")),
                        extras=((s["x1"], "mn"), (dx2, "mn"), (p["norm_mlp"], "n")), epilogue=_norm_bwd_epilogue,
                        after=tok)
    g["norm_mlp"] = _fold_rows(t + "d_norm_mlp", part)
    rest = _matmul(t + "dw_ff1", da, s["h2"], "tn", 512, 1024, 4096, into=(rest, REST_FF1, 0), after=dx1)
    dmerged = _matmul(t + "d_out_proj_in", dx1, p["w_o"], "nt", 1024, 1024, 1024)
    rest = _matmul(t + "dw_o", s["merged"], dx1, "tn", 128, 1024, 4096, into=(rest, REST_WO, 0), after=dmerged)
    ys = (s["y_pool"], s["y_conv"], s["y_attn"])
    names = ("w_pool_up_t", "w_conv_out_t", "w_attn_up_t")
    dys, dbs = [], []
    tok = rest
    dz = jax.ShapeDtypeStruct((dx2.shape[0], IN_COLS), BF16)
    for b in range(3):
        dy_b, dw_t, dz, db = _mix_bwd(t + f"merge{b}", s["z"], p["b_gate"], ys[b], p[names[b]], dmerged, b, dz,
                                      after=tok)
        tok = dy_b
        width = ys[b].shape[1]
        if b < 2:
            rest = rest.at[:, REST_UP:REST_ATTN, b * width:(b + 1) * width].set(
                dw_t.reshape(N_DEV, REST_ATTN - REST_UP, width))
        else:
            rest = rest.at[:, REST_ATTN:REST_ROWS, :].set(dw_t.reshape(N_DEV, REST_ROWS - REST_ATTN, D))
        dys.append(dy_b)
        dbs.append(_fold_rows(t + f"d_b_gate{b}", db))
    g["b_gate"] = jnp.concatenate(dbs, axis=1)
    rs_rest = _ReduceScatter(f"rs_rest{l}", collective_id, rest)
    pending_sum = None if pending is None else pending.finish(after=rest)
    dz, g["pool_mix"], g["pool_scale"] = _pool_bwd(t + "pool", s["z"], dys[0], p["pool_mix"], p["pool_scale"], dz,
                                                   after=rest if pending is None else pending_sum[0])
    dcb, dcc, dcx, g["conv_w"] = _conv_bwd(t + "conv", s["z"], dys[1], p["conv_w"], after=dz)
    tok = rs_rest.chip_sums(after=dcb)
    dqs, dks, dvs = [], [], []
    gq = gk = None
    for gi, (_, dil) in enumerate(ATTN_GROUPS):
        dq, dk, dv, dgq, dgk = _attn_bwd(t + f"attn{gi}", s["z"], p["q_gain"], p["k_gain"], dys[2], s["y_attn"],
                                         s["lse"], gi, dil, after=tok)
        tok = dq
        dqs.append(dq)
        dks.append(dk)
        dvs.append(dv)
        gq = dgq if gq is None else gq + dgq
        gk = dgk if gk is None else gk + dgk
    g["q_gain"], g["k_gain"] = gq, gk
    rest_sum, _ = rs_rest.finish(after=tok)
    col = OFF_CB
    for piece in [dcb, dcc, dcx] + dqs + dks + dvs:
        dz = lax.dynamic_update_slice(dz, piece.astype(BF16), (0, col))
        col += piece.shape[1]
    in_t = _matmul(t + "dw_in", dz, s["h"], "tn", 256, 1024, 4096, after=rest_sum)
    rs_in = _ReduceScatter(f"rs_in{l}", collective_id + 2, in_t.reshape(N_DEV, IN_ROWS, D))
    dx, part = _matmul(t + "d_in_proj_in", dz, p["w_in_t"], "nn", 256, D, IN_COLS, out_dtypes=(F32, (F32, "rows8")),
                       extras=((s["x"], "mn"), (dx1, "mn"), (p["norm_mix"], "n")), epilogue=_norm_bwd_epilogue,
                       after=in_t)
    g["norm_mix"] = _fold_rows(t + "d_norm_mix", part)
    if last:
        rs_in.chip_sums(after=dx)
    return dx, g, rest_sum, pending_sum, rs_in


ANY = pl.BlockSpec(memory_space=pl.ANY)


def _mesh_pos():
    return lax.axis_index("x"), lax.axis_index("y"), lax.axis_index("c")


def _other_chips(x, y):
    return [(1 - x, y), (x, 1 - y), (1 - x, 1 - y)]


def _comm_call(name, collective_id, peers, body, arrs, out_shape, sem_counts):
    n_in, n_out = len(arrs), len(out_shape)
    if collective_id is None:
        def tc_body(*refs):
            body(refs[:n_in], refs[n_in:n_in + n_out], *refs[n_in + n_out:])

        return pl.pallas_call(
            tc_body, name=name, out_shape=tuple(out_shape), in_specs=[ANY] * n_in, out_specs=(ANY,) * n_out,
            scratch_shapes=[pltpu.SemaphoreType.DMA((n,)) for n in sem_counts])(*arrs)

    def seq_body(*refs):
        barrier = pltpu.get_barrier_semaphore()
        ps = peers(*_mesh_pos())
        for p in ps:
            pl.semaphore_signal(barrier, inc=1, device_id=p, device_id_type=MESH)
        pl.semaphore_wait(barrier, len(ps))
        body(refs[:n_in], refs[n_in:n_in + n_out], *refs[n_in + n_out:])

    return pl.kernel(
        seq_body, out_type=tuple(out_shape), mesh=plsc.ScalarSubcoreMesh(axis_name="seq", num_cores=1), name=name,
        scratch_types=[pltpu.SemaphoreType.DMA((n,)) for n in sem_counts],
        compiler_params=pltpu.CompilerParams(collective_id=collective_id),
    )(*arrs)


def _all_gather(name, collective_id, shard):
    R, C = shard.shape

    def peers(x, y, c):
        return [(x, y, 1 - c)] + [(*chip, c) for chip in _other_chips(x, y)]

    def body(in_refs, out_refs, send_sems, recv_sems, local_sems):
        (x_ref,), (out_ref,) = in_refs, out_refs
        x, y, c = _mesh_pos()
        me, sibling = (x, y, c), (x, y, 1 - c)
        chips = _other_chips(x, y)

        def slot(px, py, pc):
            return out_ref.at[4 * px + 2 * py + pc]

        def copy(k, block, to, src=None):
            return pltpu.make_async_remote_copy(
                src_ref=slot(*block) if src is None else src, dst_ref=slot(*block),
                send_sem=send_sems.at[k], recv_sem=recv_sems.at[k], device_id=to, device_id_type=MESH)

        mine = pltpu.make_async_copy(x_ref, slot(*me), local_sems.at[0])
        mine.start()
        first = [copy(0, me, sibling, src=x_ref)]
        first += [copy(1 + j, me, (*chip, c), src=x_ref) for j, chip in enumerate(chips)]
        for cp in first:
            cp.start()
        passed = [copy(4 + j, (*chip, c), sibling) for j, chip in enumerate(chips)]
        for j, chip in enumerate(chips):
            copy(1 + j, (*chip, c), me).wait_recv()
            passed[j].start()
        copy(0, sibling, me).wait_recv()
        for j, chip in enumerate(chips):
            copy(4 + j, (*chip, 1 - c), me).wait_recv()
        for cp in first + passed:
            cp.wait_send()
        mine.wait()

    return _comm_call(name, collective_id, peers, body, [shard],
                      [jax.ShapeDtypeStruct((N_DEV, R, C), shard.dtype)], (7, 7, 1))[0]


def _rs_sibling_exchange(name, collective_id, arrs):
    n = len(arrs)

    def body(in_refs, out_refs, send_sems, recv_sems):
        x, y, c = _mesh_pos()
        cps = []
        for k, (src, dst) in enumerate(zip(in_refs, out_refs)):
            src = src.at[:, 1 - c] if len(src.shape) == 4 else src
            cps.append(pltpu.make_async_remote_copy(src_ref=src, dst_ref=dst, send_sem=send_sems.at[k],
                                                    recv_sem=recv_sems.at[k], device_id=(x, y, 1 - c),
                                                    device_id_type=MESH))
        for cp in cps:
            cp.start()
        for cp in cps:
            cp.wait()

    out_shape = [jax.ShapeDtypeStruct(a.shape[:1] + a.shape[2:] if a.ndim == 4 else a.shape, a.dtype) for a in arrs]
    return _comm_call(name, collective_id, lambda x, y, c: [(x, y, 1 - c)], body, arrs, out_shape, (n, n))


def _rs_rows(r):
    return r // 2 if (r // 2) % 16 == 0 else r


def _drop_ref(body, idx):
    def wrapped(*refs):
        return body(*refs[:idx], *refs[idx + 1:])

    return wrapped


def _rs_chip_sum(name, ids, big, rbig, after=None):
    _, _, R, C = big.shape
    rows = _rs_rows(R)

    def body(ids_ref, mine_ref, theirs_ref, t16_ref, own_ref):
        p = pl.program_id(1)
        s = mine_ref[...] + theirs_ref[...]
        t16_ref[...] = s.astype(BF16)

        @pl.when(p == ids_ref[1])
        def _():
            own_ref[...] = s

    in_specs = [pl.BlockSpec((None, None, rows, C), lambda i, p, ids: (p, ids[0], i, 0)),
                pl.BlockSpec((None, rows, C), lambda i, p, ids: (p, i, 0))]
    args = [ids, big, rbig]
    if after is not None:
        body = _drop_ref(body, len(args))
        in_specs.append(ANY)
        args.append(after)
    grid_spec = pltpu.PrefetchScalarGridSpec(
        num_scalar_prefetch=1, grid=(R // rows, 4), in_specs=in_specs,
        out_specs=(pl.BlockSpec((None, rows, C), lambda i, p, ids: (p, i, 0)),
                   pl.BlockSpec((rows, C), lambda i, p, ids: (i, 0))))
    return pl.pallas_call(
        body, name=name, grid_spec=grid_spec,
        out_shape=(jax.ShapeDtypeStruct((4, R, C), BF16), jax.ShapeDtypeStruct((R, C), F32)),
        compiler_params=_cparams(("parallel", "arbitrary")),
    )(*args)


def _add2(name, a, b):
    def body(a_ref, b_ref, o_ref):
        o_ref[...] = a_ref[...] + b_ref[...]

    full = pl.BlockSpec(a.shape, lambda i: (0, 0))
    return pl.pallas_call(body, name=name, grid=(1,), in_specs=[full, full], out_specs=full,
                          out_shape=jax.ShapeDtypeStruct(a.shape, a.dtype))(a, b)


def _rs_chip_exchange(name, collective_id, arrs):
    n = len(arrs)

    def body(in_refs, out_refs, send_sems, recv_sems, local_sems):
        x, y, c = _mesh_pos()
        p_me = 2 * x + y
        chips = _other_chips(x, y)

        def part(ref, p):
            return ref.at[p] if len(ref.shape) == 3 else ref

        local = [pltpu.make_async_copy(part(src, p_me), dst.at[p_me], local_sems.at[k])
                 for k, (src, dst) in enumerate(zip(in_refs, out_refs))]
        for cp in local:
            cp.start()
        sends = []
        for j, (px, py) in enumerate(chips):
            for k, (src, dst) in enumerate(zip(in_refs, out_refs)):
                sends.append(pltpu.make_async_remote_copy(
                    src_ref=part(src, 2 * px + py), dst_ref=dst.at[p_me], send_sem=send_sems.at[n * j + k],
                    recv_sem=recv_sems.at[n * j + k], device_id=(px, py, c), device_id_type=MESH))
        for cp in sends:
            cp.start()
        for j, (px, py) in enumerate(chips):
            for k, (src, dst) in enumerate(zip(in_refs, out_refs)):
                pltpu.make_async_remote_copy(
                    src_ref=part(src, p_me), dst_ref=dst.at[2 * px + py], send_sem=send_sems.at[n * j + k],
                    recv_sem=recv_sems.at[n * j + k], device_id=(px, py, c), device_id_type=MESH).wait_recv()
        for cp in sends:
            cp.wait_send()
        for cp in local:
            cp.wait()

    out_shape = [jax.ShapeDtypeStruct((4,) + a.shape[-2:], a.dtype) for a in arrs]
    return _comm_call(name, collective_id, lambda x, y, c: [(*chip, c) for chip in _other_chips(x, y)], body, arrs,
                      out_shape, (3 * n, 3 * n, n))


def _rs_final_sum(name, ids, recv, own, rows, after=None):
    _, R, C = recv.shape
    assert R % rows == 0

    def body(ids_ref, r_ref, own_ref, o_ref):
        acc = None
        for p in range(4):
            term = jnp.where(ids_ref[1] == p, own_ref[...], r_ref[p].astype(F32))
            acc = term if acc is None else acc + term
        o_ref[...] = acc

    in_specs = [pl.BlockSpec((4, rows, C), lambda i, ids: (0, i, 0)), pl.BlockSpec((rows, C), lambda i, ids: (i, 0))]
    args = [ids, recv, own]
    if after is not None:
        body = _drop_ref(body, len(args))
        in_specs.append(ANY)
        args.append(after)
    grid_spec = pltpu.PrefetchScalarGridSpec(
        num_scalar_prefetch=1, grid=(R // rows,), in_specs=in_specs,
        out_specs=pl.BlockSpec((rows, C), lambda i, ids: (i, 0)))
    return pl.pallas_call(
        body, name=name, grid_spec=grid_spec, out_shape=jax.ShapeDtypeStruct((R, C), F32),
        compiler_params=_cparams(("parallel",)),
    )(*args)


class _ReduceScatter:
    def __init__(self, tag, collective_id, big, small=None):
        x, y, c = _mesh_pos()
        self.tag, self.cid, self.small = tag, collective_id, small
        self.ids = jnp.stack([c, 2 * x + y]).astype(jnp.int32)
        self.big = big.reshape((4, 2) + big.shape[1:])
        self.got = _rs_sibling_exchange(tag + "_sibling", collective_id,
                                        [self.big] + ([] if small is None else [small]))

    def chip_sums(self, after=None):
        t16, self.own = _rs_chip_sum(self.tag + "_chip_sum", self.ids, self.big, self.got[0], after)
        arrs = [t16]
        if self.small is not None:
            self.ts = _add2(self.tag + "_chip_sum_small", self.small, self.got[1])
            arrs.append(self.ts)
        self.recv = _rs_chip_exchange(self.tag + "_chips", self.cid + 1, arrs)
        return t16

    def finish(self, after=None):
        out = _rs_final_sum(self.tag + "_final", self.ids, self.recv[0], self.own, _rs_rows(self.own.shape[0]), after)
        out_small = None
        if self.small is not None:
            out_small = _rs_final_sum(self.tag + "_final_small", self.ids, self.recv[1], self.ts, self.small.shape[0])
        return out, out_small


def _all_reduce_small(tag, small):
    x, y, c = _mesh_pos()
    ids = jnp.stack([c, 2 * x + y]).astype(jnp.int32)
    (theirs,) = _rs_sibling_exchange(tag + "_sibling", None, [small])
    ts = _add2(tag + "_chip_sum", small, theirs)
    (recv,) = _rs_chip_exchange(tag + "_chips", None, [ts])
    return _rs_final_sum(tag + "_final", ids, recv, ts, small.shape[0])


def _adamw(name, w, g, m, v):
    R, C = w.shape
    tr = max(t for t in range(8, 513, 8) if R % t == 0)
    c1 = 1.0 - ADAM_B1 ** ADAM_STEP
    c2 = 1.0 - ADAM_B2 ** ADAM_STEP

    def body(w_ref, g_ref, m_ref, v_ref, d_ref, nm_ref, nv_ref):
        gv = g_ref[...]
        nm = ADAM_B1 * m_ref[...] + (1.0 - ADAM_B1) * gv
        nv = ADAM_B2 * v_ref[...] + (1.0 - ADAM_B2) * (gv * gv)
        d_ref[...] = -ADAM_LR * ((nm / c1) / (jnp.sqrt(nv / c2) + ADAM_EPS) + ADAM_WD * w_ref[...])
        nm_ref[...] = nm
        nv_ref[...] = nv

    blk = pl.BlockSpec((tr, C), lambda i: (i, 0))
    shp = jax.ShapeDtypeStruct((R, C), F32)
    return pl.pallas_call(
        body, name=name, grid=(R // tr,), in_specs=[blk] * 4, out_specs=(blk,) * 3, out_shape=(shp,) * 3,
        compiler_params=_cparams(("parallel",)),
    )(w, g, m, v)


def _adamw_t(name, w, g_t, m, v):
    L, R, C = w.shape
    tr = 256
    c1 = 1.0 - ADAM_B1 ** ADAM_STEP
    c2 = 1.0 - ADAM_B2 ** ADAM_STEP

    def body(w_ref, g_ref, m_ref, v_ref, go_ref, d_ref, nm_ref, nv_ref):
        gv = g_ref[...].T
        go_ref[...] = gv
        nm = ADAM_B1 * m_ref[...] + (1.0 - ADAM_B1) * gv
        nv = ADAM_B2 * v_ref[...] + (1.0 - ADAM_B2) * (gv * gv)
        d_ref[...] = -ADAM_LR * ((nm / c1) / (jnp.sqrt(nv / c2) + ADAM_EPS) + ADAM_WD * w_ref[...])
        nm_ref[...] = nm
        nv_ref[...] = nv

    blk = pl.BlockSpec((None, tr, C), lambda l, i: (l, i, 0))
    blk_t = pl.BlockSpec((None, C, tr), lambda l, i: (l, 0, i))
    shp = jax.ShapeDtypeStruct((L, R, C), F32)
    return pl.pallas_call(
        body, name=name, grid=(L, R // tr), in_specs=[blk, blk_t, blk, blk], out_specs=(blk,) * 4,
        out_shape=(shp,) * 4, compiler_params=_cparams(("parallel", "parallel")),
    )(w, g_t, m, v)


REST_FF1 = 0
REST_FF2 = 512
REST_WO = 1024
REST_UP = 1152
REST_ATTN = 1280
REST_ROWS = 1312
IN_ROWS = IN_COLS // N_DEV
CONV_BITS_ROWS = 16
SHARD_ROWS = IN_ROWS + REST_ROWS + CONV_BITS_ROWS
SMALL = (("norm_mix", (DEPTH, D)), ("b_gate", (DEPTH, 3 * D)), ("pool_mix", (DEPTH, 4, POOL_G, POOL_G)),
         ("pool_scale", (DEPTH, POOL_W)), ("conv_w", (DEPTH, 3, CONV_W)), ("q_gain", (DEPTH, HEAD)),
         ("k_gain", (DEPTH, HEAD)), ("norm_mlp", (DEPTH, D)))


def _pack_weight_shards(w, l):
    b = lambda a: a.astype(BF16)
    conv = lax.bitcast_convert_type(w["conv_w"][l], BF16).reshape(3, 128)
    conv = jnp.pad(conv, ((0, CONV_BITS_ROWS - 3), (0, D - 128)))
    rest = jnp.concatenate([
        b(w["w_ff1"][l].T), b(w["w_ff2"][l]), b(w["w_o"][l]),
        jnp.concatenate([b(w["w_pool_up"][l].T), b(w["w_conv_out"][l].T)], axis=1),
        b(w["w_attn_up"][l].T).reshape(REST_ROWS - REST_ATTN, D), conv], axis=0)
    return b(w["w_in"][l].T), rest


def _unpack_gathered(g_in, g_rest, small_w, l):
    rest = g_rest[:, :REST_ROWS]
    take = lambda r0, rows, c0=0, cols=D: rest[:, r0:r0 + rows, c0:c0 + cols].reshape(N_DEV * rows, cols)
    conv = g_rest[:, REST_ROWS:REST_ROWS + 3, :128].reshape(N_DEV, 3, CONV_W // N_DEV, 2)
    conv = lax.bitcast_convert_type(conv, F32)
    p = {
        "w_in_t": g_in.reshape(IN_COLS, D),
        "w_ff1_t": take(REST_FF1, 512), "w_ff2": take(REST_FF2, 512), "w_o": take(REST_WO, 128),
        "w_pool_up_t": take(REST_UP, 128, 0, POOL_W), "w_conv_out_t": take(REST_UP, 128, POOL_W, CONV_W),
        "w_attn_up_t": rest[:, REST_ATTN:].reshape(D, ATTN_OUT),
        "conv_w": jnp.transpose(conv, (1, 0, 2)).reshape(3, CONV_W),
        "pool_mix": small_w["pool_mix"][l],
    }
    for name in ("norm_mix", "b_gate", "pool_scale", "q_gain", "k_gain", "norm_mlp"):
        p[name] = small_w[name][l][None, :]
    return p


def _pack_small_grads(grads):
    flat = jnp.concatenate([jnp.stack([grads[l][name] for l in range(DEPTH)]).reshape(-1) for name, _ in SMALL])
    return jnp.pad(flat, (0, (-flat.shape[0]) % (8 * 128))).reshape(-1, 128)


def _unpack_grads(in_t, rest, small, dev):
    out = {
        "w_in_t": jnp.stack(in_t),
        "w_ff1_t": jnp.stack([a[REST_FF1:REST_FF1 + 512] for a in rest]),
        "w_ff2": jnp.stack([a[REST_FF2:REST_FF2 + 512] for a in rest]),
        "w_o": jnp.stack([a[REST_WO:REST_WO + 128] for a in rest]),
        "w_pool_up": jnp.stack([a[REST_UP:REST_UP + 128, :POOL_W].T for a in rest]),
        "w_conv_out": jnp.stack([a[REST_UP:REST_UP + 128, POOL_W:].T for a in rest]),
        "w_attn_up": jnp.stack([a[REST_ATTN:].reshape(D // N_DEV, ATTN_OUT).T for a in rest]),
    }
    out.update(_unpack_small_like_grads(small, dev))
    return out


def _unpack_small_like_grads(small, dev):
    out = {}
    flat = small.reshape(-1)
    off = 0
    for name, shp in SMALL:
        n = 1
        for s in shp:
            n *= s
        out[name] = flat[off:off + n].reshape(shp)
        off += n
    width = CONV_W // N_DEV
    out["conv_w"] = lax.dynamic_slice_in_dim(out["conv_w"], dev * width, width, axis=2)
    return out


def _pack_small_like_grads(src, dev):
    parts = []
    for name, shp in SMALL:
        a = src[name]
        if name == "conv_w":
            a = lax.dynamic_update_slice_in_dim(jnp.zeros(shp, F32), a, dev * (CONV_W // N_DEV), axis=2)
        parts.append(a.reshape(-1))
    flat = jnp.concatenate(parts)
    return jnp.pad(flat, (0, (-flat.shape[0]) % (8 * 128))).reshape(-1, 128)


WEIGHTS = ("norm_mix", "w_in", "b_gate", "pool_mix", "pool_scale", "conv_w", "q_gain", "k_gain", "w_pool_up",
           "w_conv_out", "w_attn_up", "w_o", "norm_mlp", "w_ff1", "w_ff2")


def kernel(x, norm_mix, w_in, b_gate, pool_mix, pool_scale, conv_w, q_gain, k_gain, w_pool_up, w_conv_out, w_attn_up, w_o, norm_mlp, w_ff1, w_ff2, loss_target, m_norm_mix, m_w_in, m_b_gate, m_pool_mix, m_pool_scale, m_conv_w, m_q_gain, m_k_gain, m_w_pool_up, m_w_conv_out, m_w_attn_up, m_w_o, m_norm_mlp, m_w_ff1, m_w_ff2, v_norm_mix, v_w_in, v_b_gate, v_pool_mix, v_pool_scale, v_conv_w, v_q_gain, v_k_gain, v_w_pool_up, v_w_conv_out, v_w_attn_up, v_w_o, v_norm_mlp, v_w_ff1, v_w_ff2):
    w = dict(zip(WEIGHTS, (norm_mix, w_in, b_gate, pool_mix, pool_scale, conv_w, q_gain, k_gain, w_pool_up,
                           w_conv_out, w_attn_up, w_o, norm_mlp, w_ff1, w_ff2)))
    m = dict(zip(WEIGHTS, (m_norm_mix, m_w_in, m_b_gate, m_pool_mix, m_pool_scale, m_conv_w, m_q_gain, m_k_gain,
                           m_w_pool_up, m_w_conv_out, m_w_attn_up, m_w_o, m_norm_mlp, m_w_ff1, m_w_ff2)))
    v = dict(zip(WEIGHTS, (v_norm_mix, v_w_in, v_b_gate, v_pool_mix, v_pool_scale, v_conv_w, v_q_gain, v_k_gain,
                           v_w_pool_up, v_w_conv_out, v_w_attn_up, v_w_o, v_norm_mlp, v_w_ff1, v_w_ff2)))
    xi, yi, ci = _mesh_pos()
    dev = 4 * xi + 2 * yi + ci

    params = []
    for l in range(DEPTH):
        s_in, s_rest = _pack_weight_shards(w, l)
        g_in = _all_gather(f"gather_in{l}", 1 + 2 * l, s_in)
        g_rest = _all_gather(f"gather_rest{l}", 2 + 2 * l, s_rest)
        params.append(_unpack_gathered(g_in, g_rest, w, l))
    saved = []
    act = x[0]
    h = _rmsnorm_fwd("l0_norm_mix", act, params[0]["norm_mix"])
    w_in_t, m_in_t, v_in_t = (jnp.swapaxes(src["w_in"], 1, 2) for src in (w, m, v))
    small_packs = [_pack_small_like_grads(src, dev) for src in (w, m, v)]
    early = [w_in_t, m_in_t, v_in_t] + small_packs
    for l in range(DEPTH):
        act, h, s = _layer_fwd(l, act, h, params[l], params[l + 1]["norm_mix"] if l + 1 < DEPTH else None,
                               loss_target[0], early if l == 0 else None)
        saved.append(s)
    dy = act
    loss = _fold_rows("loss_sum", h, lanes=True, scale=0.5 / D)
    total = lax.psum(loss[0, 0], ("x", "y", "c"))
    in_t, rest, grads, pending = [None] * DEPTH, [None] * DEPTH, [None] * DEPTH, None
    start = total.reshape(1, 1)
    for l in reversed(range(DEPTH)):
        dy, grads[l], rest[l], done, pending = _layer_bwd(l, dy, params[l], saved[l], pending, 5 + 4 * l, l == 0,
                                                          start)
        start = None
        if done is not None:
            in_t[l + 1] = done[0]
    dx = dy
    g_small = _all_reduce_small("ar_small", _pack_small_grads(grads))
    in_t[0], _ = pending.finish(after=g_small)
    g = _unpack_grads(in_t, rest, g_small, dev)

    delta, new_m, new_v = {}, {}, {}
    g_t = g["w_in_t"]
    two_d = (g_t.shape[0] * g_t.shape[1], g_t.shape[2])
    d_, m_, v_ = _adamw("adamw_w_in", *[a.reshape(two_d) for a in (w_in_t, g_t, m_in_t, v_in_t)])
    g["w_in"] = jnp.swapaxes(g_t, 1, 2)
    delta["w_in"], new_m["w_in"], new_v["w_in"] = (jnp.swapaxes(a.reshape(g_t.shape), 1, 2) for a in (d_, m_, v_))
    g["w_ff1"], delta["w_ff1"], new_m["w_ff1"], new_v["w_ff1"] = _adamw_t("adamw_w_ff1", w["w_ff1"], g["w_ff1_t"],
                                                                          m["w_ff1"], v["w_ff1"])
    for name in ("w_pool_up", "w_conv_out", "w_attn_up", "w_o", "w_ff2"):
        shp = w[name].shape
        two_d = (shp[0] * shp[1], shp[2])
        d_, m_, v_ = _adamw("adamw_" + name, w[name].reshape(two_d), g[name].reshape(two_d),
                            m[name].reshape(two_d), v[name].reshape(two_d))
        delta[name], new_m[name], new_v[name] = d_.reshape(shp), m_.reshape(shp), v_.reshape(shp)
    outs = _adamw("adamw_small", small_packs[0], g_small, small_packs[1], small_packs[2])
    for dst, arr in zip((delta, new_m, new_v), outs):
        dst.update(_unpack_small_like_grads(arr, dev))

    return (total, dx[None], *[g[n] for n in WEIGHTS], *[delta[n] for n in WEIGHTS],
            *[new_m[n] for n in WEIGHTS], *[new_v[n] for n in WEIGHTS])
```

```python
import functools

import jax
import jax.numpy as jnp
from jax import lax
from jax.experimental import pallas as pl
from jax.experimental.pallas import tpu as pltpu
from jax.experimental.pallas import tpu_sc as plsc

F32 = jnp.float32
BF16 = jnp.bfloat16
MESH = pl.DeviceIdType.MESH

D = 1024
SEQ = 4096
DEPTH = 2
N_DEV = 8
POOL_WINDOWS = (2, 4, 8, 16)
POOL_W = 512
POOL_G = 128
CONV_W = 512
HEAD = 64
ATTN_GROUPS = ((128, 1), (512, 4), (2048, 16))
HPG = 4
ATTN_W = 768
ATTN_OUT = 256
ATTN_BLK = 128
D_FF = 4096
EPS = 1e-6
MASK = -1e30
OFF_POOL = 0
OFF_CB = 512
OFF_CC = 1024
OFF_CX = 1536
OFF_Q = 2048
OFF_K = 2816
OFF_V = 3584
OFF_GATE = 4352
IN_COLS = 7424
ADAM_LR = 0.001
ADAM_B1 = 0.9
ADAM_B2 = 0.999
ADAM_EPS = 1e-08
ADAM_WD = 0.01
ADAM_STEP = 10

ROW_TILE = 512
SEQ_CHUNK = 256
HALO = 16
VMEM_LIMIT = 56 * 1024 * 1024


def _cparams(sem=None):
    return pltpu.CompilerParams(dimension_semantics=sem, vmem_limit_bytes=VMEM_LIMIT)


def _call(body, args, in_specs, after=None, **kw):
    if after is not None:
        after = list(after) if isinstance(after, (list, tuple)) else [after]
        n, k, inner = len(args), len(after), body

        def body(*refs):
            return inner(*refs[:n], *refs[n + k:])

        args = list(args) + after
        in_specs = list(in_specs) + [pl.BlockSpec(memory_space=pl.ANY)] * k
    return pl.pallas_call(body, in_specs=in_specs, **kw)(*args)


_DIMS = {"nn": (((1,), (0,)), ((), ())), "nt": (((1,), (1,)), ((), ())), "tn": (((0,), (0,)), ((), ()))}


def _matmul(name, a, b, mode, tm, tn, tk, out_dtypes=(F32,), extras=(), epilogue=None, into=None, after=None):
    if mode == "tn":
        K, M = a.shape
    else:
        M, K = a.shape
    N = b.shape[0] if mode == "nt" else b.shape[1]
    assert M % tm == 0 and N % tn == 0 and K % tk == 0, (name, M, N, K, tm, tn, tk)
    nk = K // tk
    n_extra = len(extras)
    n_out = len(out_dtypes)
    dims = _DIMS[mode]
    n_alias = 0 if into is None or isinstance(into[0], jax.ShapeDtypeStruct) else 1

    def body(a_ref, b_ref, *rest):
        extra_refs = rest[:n_extra]
        out_refs = rest[n_extra + n_alias:n_extra + n_alias + n_out]

        def finish(acc):
            if epilogue is None:
                res = (acc,)
            else:
                res = epilogue(acc, *[r[...] for r in extra_refs])
            for o_ref, v in zip(out_refs, res):
                o_ref[...] = v.astype(o_ref.dtype)

        part = lax.dot_general(a_ref[...].astype(BF16), b_ref[...].astype(BF16), dims,
                               preferred_element_type=F32)
        if nk == 1:
            finish(part)
        else:
            acc_ref = rest[-1]
            k = pl.program_id(2)

            @pl.when(k == 0)
            def _():
                acc_ref[...] = part

            @pl.when(k > 0)
            def _():
                acc_ref[...] += part

            @pl.when(k == nk - 1)
            def _():
                finish(acc_ref[...])

    if mode == "tn":
        a_spec = pl.BlockSpec((tk, tm), lambda i, j, k: (k, i))
    else:
        a_spec = pl.BlockSpec((tm, tk), lambda i, j, k: (i, k))
    if mode == "nt":
        b_spec = pl.BlockSpec((tn, tk), lambda i, j, k: (j, k))
    else:
        b_spec = pl.BlockSpec((tk, tn), lambda i, j, k: (k, j))
    in_specs = [a_spec, b_spec]
    args = [a, b]
    for arr, kind in extras:
        if kind == "mn":
            in_specs.append(pl.BlockSpec((tm, tn), lambda i, j, k: (i, j)))
        else:
            in_specs.append(pl.BlockSpec((1, tn), lambda i, j, k: (0, j)))
        args.append(arr)
    part = [isinstance(dt, tuple) for dt in out_dtypes]
    out_shape = tuple(jax.ShapeDtypeStruct((M // tm * 8, N), dt[0]) if p else jax.ShapeDtypeStruct((M, N), dt)
                      for dt, p in zip(out_dtypes, part))
    out_specs = tuple(pl.BlockSpec((8 if p else tm, tn), lambda i, j, k: (i, j)) for p in part)
    aliases = {}
    if into is not None:
        buf, row0, col0 = into
        assert n_out == 1 and (M // N_DEV) % tm == 0 and row0 % tm == 0 and col0 % tn == 0
        per_dev = M // N_DEV // tm
        out_shape = (jax.ShapeDtypeStruct(buf.shape, buf.dtype),)
        out_specs = (pl.BlockSpec((None, tm, tn), lambda i, j, k: (i // per_dev, row0 // tm + i % per_dev,
                                                                   col0 // tn + j)),)
        if not isinstance(buf, jax.ShapeDtypeStruct):
            aliases = {len(args): 0}
            in_specs.append(pl.BlockSpec(memory_space=pl.ANY))
            args.append(buf)
    scratch = [] if nk == 1 else [pltpu.VMEM((tm, tn), F32)]
    res = _call(
        body, args, in_specs, after, name=name, grid=(M // tm, N // tn, nk), out_specs=out_specs,
        out_shape=out_shape, scratch_shapes=scratch, input_output_aliases=aliases,
        compiler_params=_cparams(("parallel", "parallel", "arbitrary")))
    return res if n_out > 1 else res[0]


def _rmsnorm_fwd(name, x, gain):
    S_, D_ = x.shape

    def body(x_ref, g_ref, h_ref):
        xf = x_ref[...]
        ms = jnp.mean(xf * xf, axis=-1, keepdims=True)
        h_ref[...] = (xf * lax.rsqrt(ms + EPS) * g_ref[...]).astype(BF16)

    return pl.pallas_call(
        body, name=name, grid=(S_ // ROW_TILE,),
        in_specs=[pl.BlockSpec((ROW_TILE, D_), lambda i: (i, 0)), pl.BlockSpec((1, D_), lambda i: (0, 0))],
        out_specs=pl.BlockSpec((ROW_TILE, D_), lambda i: (i, 0)),
        out_shape=jax.ShapeDtypeStruct((S_, D_), BF16),
        compiler_params=_cparams(("parallel",)),
    )(x, gain)


def _rmsnorm_bwd(name, x, gain, dh, dres, after=None):
    S_, D_ = x.shape
    n = S_ // ROW_TILE

    def body(x_ref, g_ref, dh_ref, dres_ref, dx_ref, dg_ref, acc_ref):
        i = pl.program_id(0)
        xf = x_ref[...]
        rstd = lax.rsqrt(jnp.mean(xf * xf, axis=-1, keepdims=True) + EPS)
        xhat = xf * rstd
        dhv = dh_ref[...]
        dxhat = dhv * g_ref[...]
        c = jnp.mean(dxhat * xhat, axis=-1, keepdims=True)
        dx_ref[...] = dres_ref[...] + rstd * (dxhat - xhat * c)
        part = jnp.sum((dhv * xhat).reshape(ROW_TILE // 8, 8, D_), axis=0)

        @pl.when(i == 0)
        def _():
            acc_ref[...] = part

        @pl.when(i > 0)
        def _():
            acc_ref[...] += part

        @pl.when(i == n - 1)
        def _():
            dg_ref[...] = jnp.sum(acc_ref[...], axis=0, keepdims=True)

    row = pl.BlockSpec((ROW_TILE, D_), lambda i: (i, 0))
    vec = pl.BlockSpec((1, D_), lambda i: (0, 0))
    return _call(
        body, [x, gain, dh, dres], [row, vec, row, row], after, name=name, grid=(n,), out_specs=(row, vec),
        out_shape=(jax.ShapeDtypeStruct((S_, D_), F32), jax.ShapeDtypeStruct((1, D_), F32)),
        scratch_shapes=[pltpu.VMEM((8, D_), F32)],
        compiler_params=_cparams(("arbitrary",)))


def _loss_head(name, y, target):
    S_, D_ = y.shape
    n = S_ // ROW_TILE

    def body(y_ref, t_ref, dy_ref, l_ref, acc_ref):
        i = pl.program_id(0)
        e = y_ref[...] - t_ref[...]
        dy_ref[...] = e * (1.0 / D_)
        part = jnp.sum((e * e).reshape(ROW_TILE // 8, 8, D_), axis=0)

        @pl.when(i == 0)
        def _():
            acc_ref[...] = part

        @pl.when(i > 0)
        def _():
            acc_ref[...] += part

        @pl.when(i == n - 1)
        def _():
            s = jnp.sum(acc_ref[...], axis=1, keepdims=True)
            l_ref[...] = jnp.sum(s, axis=0, keepdims=True) * (0.5 / D_)

    row = pl.BlockSpec((ROW_TILE, D_), lambda i: (i, 0))
    return pl.pallas_call(
        body, name=name, grid=(n,), in_specs=[row, row],
        out_specs=(row, pl.BlockSpec((1, 1), lambda i: (0, 0))),
        out_shape=(jax.ShapeDtypeStruct((S_, D_), F32), jax.ShapeDtypeStruct((1, 1), F32)),
        scratch_shapes=[pltpu.VMEM((8, D_), F32)],
        compiler_params=_cparams(("arbitrary",)),
    )(y, target)


def _rows_with_halo(ref, cols, i, n_chunks, before, after):
    r0 = pl.multiple_of(i * SEQ_CHUNK, SEQ_CHUNK)
    parts = []
    if before:
        h0 = pl.multiple_of(jnp.maximum(r0 - HALO, 0), 8)
        halo = ref[pl.ds(h0, HALO), cols]
        parts.append(jnp.where(i > 0, halo, jnp.zeros_like(halo)))
    parts.append(ref[pl.ds(r0, SEQ_CHUNK), cols])
    if after:
        a0 = pl.multiple_of(jnp.minimum(r0 + SEQ_CHUNK, (n_chunks - 1) * SEQ_CHUNK + SEQ_CHUNK - HALO), 8)
        halo = ref[pl.ds(a0, HALO), cols]
        parts.append(jnp.where(i < n_chunks - 1, halo, jnp.zeros_like(halo)))
    return parts[0] if len(parts) == 1 else jnp.concatenate(parts, axis=0)


def _shift_down(v, k):
    return pltpu.roll(v, k, 0)


def _shift_up(v, k):
    return pltpu.roll(v, v.shape[0] - k, 0)


def _pool_diff(xx, w, t_main):
    s = xx
    k = 1
    while k < w:
        s = s + _shift_down(s, k)
        k *= 2
    cnt = jnp.minimum(t_main + 1, w).astype(F32)
    return s[HALO:] / cnt - xx[HALO:]


def _pool_fwd(name, z, pool_mix, pool_scale):
    S_ = z.shape[0]
    n_chunks = S_ // SEQ_CHUNK

    cols = slice(0, POOL_G)

    def body(u_ref, mix_ref, sc_ref, y_ref):
        mixg = mix_ref[...].astype(BF16)
        scg = sc_ref[...]
        for g, w in enumerate(POOL_WINDOWS):
            @pl.when(pl.program_id(0) == g)
            def _(w=w):
                def chunk(i, carry):
                    r0 = pl.multiple_of(i * SEQ_CHUNK, SEQ_CHUNK)
                    xx = _rows_with_halo(u_ref, cols, i, n_chunks, True, False)
                    t = r0 + lax.broadcasted_iota(jnp.int32, (SEQ_CHUNK, POOL_G), 0)
                    d = _pool_diff(xx, w, t)
                    y = jnp.dot(d.astype(BF16), mixg, preferred_element_type=F32) * scg
                    y_ref[pl.ds(r0, SEQ_CHUNK), :] = y.astype(BF16)
                    return carry

                lax.fori_loop(0, n_chunks, chunk, 0)

    slab = pl.BlockSpec((S_, POOL_G), lambda g: (0, g))
    return pl.pallas_call(
        body, name=name, grid=(len(POOL_WINDOWS),),
        in_specs=[slab, pl.BlockSpec((None, POOL_G, POOL_G), lambda g: (g, 0, 0)),
                  pl.BlockSpec((1, POOL_G), lambda g: (0, g))],
        out_specs=slab, out_shape=jax.ShapeDtypeStruct((S_, POOL_W), BF16),
        compiler_params=_cparams(("parallel",)),
    )(z, pool_mix, pool_scale)


def _pool_bwd(name, z, dy, pool_mix, pool_scale, dz, after=None):
    S_ = z.shape[0]
    n_chunks = S_ // SEQ_CHUNK
    rows_a = SEQ_CHUNK + HALO

    cols = slice(0, POOL_G)

    def body(u_ref, dy_ref, mix_ref, sc_ref, du_ref, dmix_ref, dsc_ref):
        mixg = mix_ref[...].astype(BF16)
        scg = sc_ref[...]
        for g, w in enumerate(POOL_WINDOWS):
            @pl.when(pl.program_id(0) == g)
            def _(w=w):
                def chunk(i, carry):
                    dmix_acc, dsc_acc = carry
                    r0 = pl.multiple_of(i * SEQ_CHUNK, SEQ_CHUNK)
                    xx = _rows_with_halo(u_ref, cols, i, n_chunks, True, False)
                    t = r0 + lax.broadcasted_iota(jnp.int32, (SEQ_CHUNK, POOL_G), 0)
                    d = _pool_diff(xx, w, t).astype(BF16)
                    ypre = jnp.dot(d, mixg, preferred_element_type=F32)
                    dyy = _rows_with_halo(dy_ref, cols, i, n_chunks, False, True)
                    dys = (dyy * scg).astype(BF16)
                    dsc_acc = dsc_acc + jnp.sum((dyy[:SEQ_CHUNK] * ypre).reshape(SEQ_CHUNK // 8, 8, POOL_G), axis=0)
                    dmix_acc = dmix_acc + lax.dot_general(d, dys[:SEQ_CHUNK], _DIMS["tn"],
                                                          preferred_element_type=F32)
                    dd = lax.dot_general(dys, mixg, _DIMS["nt"], preferred_element_type=F32)
                    ta = r0 + lax.broadcasted_iota(jnp.int32, (rows_a, POOL_G), 0)
                    f = dd / jnp.minimum(ta + 1, w).astype(F32)
                    k = 1
                    while k < w:
                        f = f + _shift_up(f, k)
                        k *= 2
                    du_ref[pl.ds(r0, SEQ_CHUNK), :] = (f[:SEQ_CHUNK] - dd[:SEQ_CHUNK]).astype(BF16)
                    return dmix_acc, dsc_acc

                dmix_acc, dsc_acc = lax.fori_loop(
                    0, n_chunks, chunk, (jnp.zeros((POOL_G, POOL_G), F32), jnp.zeros((8, POOL_G), F32)))
                dmix_ref[...] = dmix_acc
                dsc_ref[...] = jnp.sum(dsc_acc, axis=0, keepdims=True)

    slab = pl.BlockSpec((S_, POOL_G), lambda g: (0, g))
    mix_spec = pl.BlockSpec((None, POOL_G, POOL_G), lambda g: (g, 0, 0))
    vec = pl.BlockSpec((1, POOL_G), lambda g: (0, g))
    return _call(
        _drop_ref(body, 4), [z, dy, pool_mix, pool_scale, dz], [slab, slab, mix_spec, vec, ANY], after, name=name,
        grid=(len(POOL_WINDOWS),), out_specs=(slab, mix_spec, vec),
        out_shape=(jax.ShapeDtypeStruct(dz.shape, dz.dtype), jax.ShapeDtypeStruct((4, POOL_G, POOL_G), F32),
                   jax.ShapeDtypeStruct((1, POOL_W), F32)),
        input_output_aliases={4: 0}, compiler_params=_cparams(("parallel",)))


def _conv_specs(S_):
    slab = lambda off: pl.BlockSpec((S_, 128), lambda c, off=off: (0, off // 128 + c))
    return slab(OFF_CB), slab(OFF_CC), slab(OFF_CX)


def _conv_fwd(name, z, conv_w, after=None):
    S_ = z.shape[0]
    n_chunks = S_ // SEQ_CHUNK
    col = slice(0, 128)

    def body(b_ref, c_ref, x_ref, w_ref, y_ref):
        w0, w1, w2 = w_ref[0:1, :], w_ref[1:2, :], w_ref[2:3, :]

        def chunk(i, carry):
            r0 = pl.multiple_of(i * SEQ_CHUNK, SEQ_CHUNK)
            u = _rows_with_halo(c_ref, col, i, n_chunks, True, False) * _rows_with_halo(x_ref, col, i, n_chunks, True, False)
            y = w2 * u + w1 * _shift_down(u, 1) + w0 * _shift_down(u, 2)
            y_ref[pl.ds(r0, SEQ_CHUNK), :] = (b_ref[pl.ds(r0, SEQ_CHUNK), :] * y[HALO:]).astype(BF16)
            return carry

        lax.fori_loop(0, n_chunks, chunk, 0)

    sb, sc, sx = _conv_specs(S_)
    return _call(
        body, [z, z, z, conv_w], [sb, sc, sx, pl.BlockSpec((3, 128), lambda c: (0, c))], after,
        name=name, grid=(CONV_W // 128,),
        out_specs=pl.BlockSpec((S_, 128), lambda c: (0, c)),
        out_shape=jax.ShapeDtypeStruct((S_, CONV_W), BF16),
        compiler_params=_cparams(("parallel",)))


def _conv_bwd(name, z, dy, conv_w, after=None):
    S_ = z.shape[0]
    n_chunks = S_ // SEQ_CHUNK
    col = slice(0, 128)
    lo, hi = HALO, HALO + SEQ_CHUNK

    def body(b_ref, c_ref, x_ref, dy_ref, w_ref, db_ref, dc_ref, dx_ref, dw_ref):
        w0, w1, w2 = w_ref[0:1, :], w_ref[1:2, :], w_ref[2:3, :]

        def chunk(i, carry):
            a0, a1, a2 = carry
            r0 = pl.multiple_of(i * SEQ_CHUNK, SEQ_CHUNK)
            cc = _rows_with_halo(c_ref, col, i, n_chunks, True, True)
            xx = _rows_with_halo(x_ref, col, i, n_chunks, True, True)
            bb = _rows_with_halo(b_ref, col, i, n_chunks, True, True)
            dyy = _rows_with_halo(dy_ref, col, i, n_chunks, True, True)
            u = cc * xx
            u1 = _shift_down(u, 1)
            u2 = _shift_down(u, 2)
            y = w2 * u + w1 * u1 + w0 * u2
            dyv = dyy * bb
            du = w2 * dyv + w1 * _shift_up(dyv, 1) + w0 * _shift_up(dyv, 2)
            db_ref[pl.ds(r0, SEQ_CHUNK), :] = (dyy[lo:hi] * y[lo:hi]).astype(BF16)
            dc_ref[pl.ds(r0, SEQ_CHUNK), :] = (du[lo:hi] * xx[lo:hi]).astype(BF16)
            dx_ref[pl.ds(r0, SEQ_CHUNK), :] = (du[lo:hi] * cc[lo:hi]).astype(BF16)
            red = lambda v: jnp.sum(v.reshape(SEQ_CHUNK // 8, 8, 128), axis=0)
            dm = dyv[lo:hi]
            return a0 + red(dm * u2[lo:hi]), a1 + red(dm * u1[lo:hi]), a2 + red(dm * u[lo:hi])

        zero = jnp.zeros((8, 128), F32)
        a0, a1, a2 = lax.fori_loop(0, n_chunks, chunk, (zero, zero, zero))
        dw_ref[0:1, :] = jnp.sum(a0, axis=0, keepdims=True)
        dw_ref[1:2, :] = jnp.sum(a1, axis=0, keepdims=True)
        dw_ref[2:3, :] = jnp.sum(a2, axis=0, keepdims=True)

    sb, sc, sx = _conv_specs(S_)
    slab = pl.BlockSpec((S_, 128), lambda c: (0, c))
    wspec = pl.BlockSpec((3, 128), lambda c: (0, c))
    act = jax.ShapeDtypeStruct((S_, CONV_W), BF16)
    return _call(
        body, [z, z, z, dy, conv_w], [sb, sc, sx, slab, wspec], after, name=name, grid=(CONV_W // 128,),
        out_specs=(slab, slab, slab, wspec),
        out_shape=(act, act, act, jax.ShapeDtypeStruct((3, CONV_W), F32)),
        compiler_params=_cparams(("parallel",)))


def _head_ones(pw):
    a = lax.broadcasted_iota(jnp.int32, (pw, pw), 0) // HEAD
    b = lax.broadcasted_iota(jnp.int32, (pw, pw), 1) // HEAD
    return (a == b).astype(BF16)


def _head_sum(v, ones):
    hi = v.astype(BF16)
    lo = (v - hi.astype(F32)).astype(BF16)
    return jnp.dot(hi, ones, preferred_element_type=F32) + jnp.dot(lo, ones, preferred_element_type=F32)


def _head_norm(v, gain, ones):
    rstd = lax.rsqrt(_head_sum(v * v, ones) * (1.0 / HEAD) + EPS)
    xhat = v * rstd
    return xhat * gain, xhat, rstd


def _head_norm_bwd(dy, xhat, rstd, gain, ones):
    dxhat = dy * gain
    c = _head_sum(dxhat * xhat, ones) * (1.0 / HEAD)
    dv = rstd * (dxhat - xhat * c)
    dg = jnp.sum((dy * xhat).reshape(dy.shape[0] // 8, 8, dy.shape[1]), axis=0)
    return dv, dg


def _head_masks(pw):
    lane_head = lax.broadcasted_iota(jnp.int32, (1, pw), 1) // HEAD
    return [lane_head == h for h in range(pw // HEAD)]


def _only(mask, v):
    return jnp.where(mask, v, jnp.zeros_like(v))


def _attn_specs(S_, g, dil):
    rows = ATTN_BLK * dil
    nb = S_ // rows
    pw = 128 if dil > 1 else ATTN_OUT
    cq, ck, cv = ((OFF_Q + g * ATTN_OUT) // pw, (OFF_K + g * ATTN_OUT) // pw, (OFF_V + g * ATTN_OUT) // pw)
    return rows, nb, pw, pw // HEAD, ATTN_OUT // pw, cq, ck, cv


ATTN_BATCH = 4


def _attn_group(dil):
    return 4 if dil == 1 else 1


def _attn_block_specs(rows, grp, pw, last=None):
    step = (lambda n: n) if last is None else (lambda n: jnp.minimum(n, last))
    cur = lambda c: pl.BlockSpec((rows * grp, pw), lambda hp, n, c=c: (step(n), c + hp))
    prev = lambda c: pl.BlockSpec((rows, pw), lambda hp, n, c=c: (jnp.maximum(step(n) * grp - 1, 0), c + hp))
    return cur, prev


def _band_mask(has_prev):
    qi = lax.broadcasted_iota(jnp.int32, (ATTN_BLK, 2 * ATTN_BLK), 0)
    ki = lax.broadcasted_iota(jnp.int32, (ATTN_BLK, 2 * ATTN_BLK), 1)
    in_prev = jnp.logical_and(ki < ATTN_BLK, ki >= qi)
    if has_prev is not True:
        in_prev = jnp.logical_and(in_prev, has_prev)
    return jnp.logical_or(in_prev, jnp.logical_and(ki >= ATTN_BLK, ki - ATTN_BLK <= qi))


def _rows_of(ref, r, dil):
    if dil == 1:
        return ref[r * ATTN_BLK:(r + 1) * ATTN_BLK, :]
    return ref[pl.ds(r, ATTN_BLK, stride=dil), :]


def _put_rows(ref, r, dil, val):
    if dil == 1:
        ref[r * ATTN_BLK:(r + 1) * ATTN_BLK, :] = val.astype(ref.dtype)
    else:
        ref[pl.ds(r, ATTN_BLK, stride=dil), :] = val.astype(ref.dtype)


def _attn_fwd(name, z, q_gain, k_gain, g, dil):
    S_ = z.shape[0]
    rows, nb, pw, heads, npairs, cq, ck, cv = _attn_specs(S_, g, dil)
    scale = HEAD ** -0.5

    grp = _attn_group(dil)
    nsteps = nb // grp

    def body(q_ref, k_ref, kp_ref, v_ref, vp_ref, gq_ref, gk_ref, o_ref, l_ref):
        n = pl.program_id(1)
        ones, hmask = _head_ones(pw), _head_masks(pw)
        gq, gk = jnp.tile(gq_ref[...], (1, heads)), jnp.tile(gk_ref[...], (1, heads))
        mask_first, mask_rest = _band_mask(n > 0), _band_mask(True)
        for r0 in range(0, dil * grp, ATTN_BATCH):
            rs = range(r0, min(r0 + ATTN_BATCH, dil * grp))
            qn, kn, vv, s, p = {}, {}, {}, {}, {}
            kcn = {}
            for r in rs:
                q, kc, vc = _rows_of(q_ref, r, dil), _rows_of(k_ref, r, dil), _rows_of(v_ref, r, dil)
                kcn[r] = _head_norm(kc, gk, ones)[0]
                if dil == 1 and r > 0:
                    kpn = kcn[r - 1] if r - 1 in kcn else _head_norm(_rows_of(k_ref, r - 1, dil), gk, ones)[0]
                    vp = _rows_of(v_ref, r - 1, dil)
                else:
                    kpn, vp = _head_norm(_rows_of(kp_ref, r, dil), gk, ones)[0], _rows_of(vp_ref, r, dil)
                qn[r] = _head_norm(q, gq, ones)[0].astype(BF16)
                kn[r] = jnp.concatenate([kpn, kcn[r]], axis=0).astype(BF16)
                vv[r] = jnp.concatenate([vp, vc], axis=0).astype(BF16)
            keys = [(r, h) for r in rs for h in range(heads)]
            for r, h in keys:
                s[r, h] = lax.dot_general(_only(hmask[h], qn[r]), kn[r], _DIMS["nt"],
                                          preferred_element_type=F32) * scale
            lse, den = {}, {}
            for key in keys:
                mask = mask_rest if (dil == 1 and key[0] > 0) else mask_first
                sm = jnp.where(mask, s[key], MASK)
                m = jnp.max(sm, axis=-1, keepdims=True)
                e = jnp.exp(sm - m)
                den[key] = jnp.sum(e, axis=-1, keepdims=True)
                p[key] = e.astype(BF16)
                lse[key] = m + jnp.log(den[key])
            for r in rs:
                out = jnp.zeros((ATTN_BLK, pw), F32)
                lse_all = jnp.zeros((ATTN_BLK, pw), F32)
                for h in range(heads):
                    out = jnp.where(hmask[h], jnp.dot(p[r, h], vv[r], preferred_element_type=F32) / den[r, h], out)
                    lse_all = jnp.where(hmask[h], lse[r, h], lse_all)
                _put_rows(o_ref, r, dil, out)
                _put_rows(l_ref, r, dil, lse_all)

    cur, prev = _attn_block_specs(rows, grp, pw)
    gspec = pl.BlockSpec((1, HEAD), lambda hp, n: (0, 0))
    shp = jax.ShapeDtypeStruct((S_, ATTN_OUT), F32)
    return pl.pallas_call(
        body, name=name, grid=(npairs, nsteps),
        in_specs=[cur(cq), cur(ck), prev(ck), cur(cv), prev(cv), gspec, gspec],
        out_specs=(cur(0), cur(0)), out_shape=(shp, shp),
        compiler_params=_cparams(("parallel", "parallel")),
    )(z, z, z, z, z, q_gain, k_gain)


def _attn_combine(name, os_, ls_):
    S_ = os_[0].shape[0]

    def body(o0, o1, o2, l0, l1, l2, o_ref, l_ref):
        a, b, c = l0[...], l1[...], l2[...]
        m = jnp.maximum(jnp.maximum(a, b), c)
        ea, eb, ec = jnp.exp(a - m), jnp.exp(b - m), jnp.exp(c - m)
        zsum = ea + eb + ec
        o_ref[...] = (ea * o0[...] + eb * o1[...] + ec * o2[...]) / zsum
        l_ref[...] = m + jnp.log(zsum)

    row = pl.BlockSpec((ROW_TILE, ATTN_OUT), lambda i: (i, 0))
    shp = jax.ShapeDtypeStruct((S_, ATTN_OUT), F32)
    return pl.pallas_call(
        body, name=name, grid=(S_ // ROW_TILE,), in_specs=[row] * 6, out_specs=(row, row), out_shape=(shp, shp),
        compiler_params=_cparams(("parallel",)),
    )(*os_, *ls_)


def _attn_bwd(name, z, q_gain, k_gain, do, o, lse, g, dil, after=None):
    S_ = z.shape[0]
    rows, nb, pw, heads, npairs, cq, ck, cv = _attn_specs(S_, g, dil)
    scale = HEAD ** -0.5

    def body(q_ref, kc_ref, kp_ref, vc_ref, vp_ref, gq_ref, gk_ref, do_ref, o_ref, l_ref,
             dq_ref, dk_ref, dv_ref, dgq_ref, dgk_ref, ck_ref, cvv_ref, gq_acc, gk_acc):
        hp = pl.program_id(0)
        n = pl.program_id(1)
        live = n < nb
        mask = jnp.logical_and(_band_mask(n > 0), live)
        ones, hmask = _head_ones(pw), _head_masks(pw)
        gq, gk = jnp.tile(gq_ref[...], (1, heads)), jnp.tile(gk_ref[...], (1, heads))

        @pl.when(n == 0)
        def _():
            ck_ref[...] = jnp.zeros_like(ck_ref)
            cvv_ref[...] = jnp.zeros_like(cvv_ref)

        @pl.when(jnp.logical_and(n == 0, hp == 0))
        def _():
            gq_acc[...] = jnp.zeros_like(gq_acc)
            gk_acc[...] = jnp.zeros_like(gk_acc)

        dgq = jnp.zeros((8, pw), F32)
        dgk = jnp.zeros((8, pw), F32)
        for r0 in range(0, dil, ATTN_BATCH):
            rs = range(r0, min(r0 + ATTN_BATCH, dil))
            keys = [(r, h) for r in rs for h in range(heads)]
            qn, qhat, qrstd, kn, kphat, kprstd, vv, dob, delta, lse = ({} for _ in range(10))
            for r in rs:
                q, kc, kp = _rows_of(q_ref, r, dil), _rows_of(kc_ref, r, dil), _rows_of(kp_ref, r, dil)
                dov = _rows_of(do_ref, r, dil)
                qn_f, qhat[r], qrstd[r] = _head_norm(q, gq, ones)
                kpn, kphat[r], kprstd[r] = _head_norm(kp, gk, ones)
                qn[r] = qn_f.astype(BF16)
                kn[r] = jnp.concatenate([kpn, _head_norm(kc, gk, ones)[0]], axis=0).astype(BF16)
                vv[r] = jnp.concatenate([_rows_of(vp_ref, r, dil), _rows_of(vc_ref, r, dil)], axis=0).astype(BF16)
                dob[r] = dov.astype(BF16)
                delta[r] = _head_sum(dov * _rows_of(o_ref, r, dil), ones)
                lse[r] = _rows_of(l_ref, r, dil)
            s, dp = {}, {}
            for r, h in keys:
                s[r, h] = lax.dot_general(_only(hmask[h], qn[r]), kn[r], _DIMS["nt"],
                                          preferred_element_type=F32) * scale
                dp[r, h] = lax.dot_general(_only(hmask[h], dob[r]), vv[r], _DIMS["nt"], preferred_element_type=F32)
            p, ds = {}, {}
            for r, h in keys:
                col = slice(h * HEAD, h * HEAD + 1)
                pk = jnp.where(mask, jnp.exp(jnp.where(mask, s[r, h], MASK) - lse[r][:, col]), 0.0)
                ds[r, h] = (pk * (dp[r, h] - delta[r][:, col]) * scale).astype(BF16)
                p[r, h] = pk.astype(BF16)
            dqn, dkn, dvv = {}, {}, {}
            for r in rs:
                dqn[r] = jnp.zeros((ATTN_BLK, pw), F32)
                dkn[r] = jnp.zeros((2 * ATTN_BLK, pw), F32)
                dvv[r] = jnp.zeros((2 * ATTN_BLK, pw), F32)
                for h in range(heads):
                    dqn[r] = jnp.where(hmask[h], jnp.dot(ds[r, h], kn[r], preferred_element_type=F32), dqn[r])
                    dkn[r] = jnp.where(hmask[h], lax.dot_general(ds[r, h], qn[r], _DIMS["tn"],
                                                                 preferred_element_type=F32), dkn[r])
                    dvv[r] = jnp.where(hmask[h], lax.dot_general(p[r, h], dob[r], _DIMS["tn"],
                                                                 preferred_element_type=F32), dvv[r])
            for r in rs:
                dq_all, dg = _head_norm_bwd(dqn[r], qhat[r], qrstd[r], gq, ones)
                dgq = dgq + dg
                dk_all, dg = _head_norm_bwd(_rows_of(ck_ref, r, dil) + dkn[r][:ATTN_BLK], kphat[r], kprstd[r], gk, ones)
                dgk = dgk + dg
                dv_all = _rows_of(cvv_ref, r, dil) + dvv[r][:ATTN_BLK]

                @pl.when(live)
                def _(dq_all=dq_all, r=r):
                    _put_rows(dq_ref, r, dil, dq_all)

                _put_rows(dk_ref, r, dil, dk_all)
                _put_rows(dv_ref, r, dil, dv_all)
                _put_rows(ck_ref, r, dil, dkn[r][ATTN_BLK:])
                _put_rows(cvv_ref, r, dil, dvv[r][ATTN_BLK:])
        gq_acc[...] += dgq
        gk_acc[...] += dgk

        @pl.when(jnp.logical_and(n == nb, hp == npairs - 1))
        def _():
            fold = lambda a: sum(a[:, h * HEAD:(h + 1) * HEAD] for h in range(heads))
            dgq_ref[...] = jnp.sum(fold(gq_acc[...]), axis=0, keepdims=True)
            dgk_ref[...] = jnp.sum(fold(gk_acc[...]), axis=0, keepdims=True)

    last = nb - 1
    cur = lambda c: pl.BlockSpec((rows, pw), lambda hp, n, c=c: (jnp.minimum(n, last), c + hp))
    prev = lambda c: pl.BlockSpec((rows, pw), lambda hp, n, c=c: (jnp.maximum(n - 1, 0), c + hp))
    gspec = pl.BlockSpec((1, HEAD), lambda hp, n: (0, 0))
    act = jax.ShapeDtypeStruct((S_, ATTN_OUT), F32)
    vec = jax.ShapeDtypeStruct((1, HEAD), F32)
    return _call(
        body, [z, z, z, z, z, q_gain, k_gain, do, o, lse],
        [cur(cq), cur(ck), prev(ck), cur(cv), prev(cv), gspec, gspec, cur(0), cur(0), cur(0)], after,
        name=name, grid=(npairs, nb + 1),
        out_specs=(cur(0), prev(0), prev(0), gspec, gspec),
        out_shape=(act, act, act, vec, vec),
        scratch_shapes=[pltpu.VMEM((rows, pw), F32), pltpu.VMEM((rows, pw), F32),
                        pltpu.VMEM((8, pw), F32), pltpu.VMEM((8, pw), F32)],
        compiler_params=_cparams(("arbitrary", "arbitrary")))


MIX_TN = 256
MIX_TM = 2048


def _sigmoid(v):
    return 1.0 / (1.0 + jnp.exp(-v))


def _mix_fwd(name, z, b_gate, ys, ws):
    S_ = z.shape[0]
    tm, tn = MIX_TM, MIX_TN
    gblk = OFF_GATE // tn

    def body(yp, yc, ya, wp, wc, wa, g0, g1, g2, b0, b1, b2, m_ref):
        acc = None
        for y_ref, w_ref, g_ref, b_ref in ((yp, wp, g0, b0), (yc, wc, g1, b1), (ya, wa, g2, b2)):
            u = lax.dot_general(y_ref[...].astype(BF16), w_ref[...], _DIMS["nt"], preferred_element_type=F32)
            t = _sigmoid(g_ref[...] + b_ref[...]) * u
            acc = t if acc is None else acc + t
        m_ref[...] = acc.astype(BF16)

    yspec = lambda w: pl.BlockSpec((tm, w), lambda i, j: (i, 0))
    wspec = lambda w: pl.BlockSpec((tn, w), lambda i, j: (j, 0))
    gspec = lambda b: pl.BlockSpec((tm, tn), lambda i, j, b=b: (i, gblk + b * (D // tn) + j))
    bspec = lambda b: pl.BlockSpec((1, tn), lambda i, j, b=b: (0, b * (D // tn) + j))
    return pl.pallas_call(
        body, name=name, grid=(S_ // tm, D // tn),
        in_specs=[yspec(POOL_W), yspec(CONV_W), yspec(ATTN_OUT), wspec(POOL_W), wspec(CONV_W), wspec(ATTN_OUT),
                  gspec(0), gspec(1), gspec(2), bspec(0), bspec(1), bspec(2)],
        out_specs=pl.BlockSpec((tm, tn), lambda i, j: (i, j)),
        out_shape=jax.ShapeDtypeStruct((S_, D), BF16),
        compiler_params=_cparams(("parallel", "parallel")),
    )(*ys, *ws, z, z, z, b_gate, b_gate, b_gate)


def _mix_bwd(name, z, b_gate, y, w, dmerged, branch, dz, after=None):
    S_ = z.shape[0]
    tm, tn = MIX_TM, MIX_TN
    width = y.shape[1]
    gblk = OFF_GATE // tn + branch * (D // tn)
    nj = D // tn

    def body(y_ref, w_ref, g_ref, b_ref, dm_ref, dy_ref, dw_ref, dg_ref, db_ref, acc_ref):
        i, j = pl.program_id(0), pl.program_id(1)
        yb = y_ref[...].astype(BF16)
        u = lax.dot_general(yb, w_ref[...], _DIMS["nt"], preferred_element_type=F32)
        sg = _sigmoid(g_ref[...] + b_ref[...])
        dm = dm_ref[...]
        du = (sg * dm).astype(BF16)
        dpre = dm * u * sg * (1.0 - sg)
        dg_ref[...] = dpre.astype(BF16)
        cols = pl.ds(pl.multiple_of(j * tn, tn), tn)
        db_ref[...] = _rows8(dpre)
        d_w = lax.dot_general(du, yb, _DIMS["tn"], preferred_element_type=F32)
        d_y = jnp.dot(du, w_ref[...], preferred_element_type=F32)

        @pl.when(i == 0)
        def _():
            dw_ref[cols, :] = d_w

        @pl.when(i > 0)
        def _():
            dw_ref[cols, :] += d_w

        @pl.when(j == 0)
        def _():
            acc_ref[...] = d_y

        @pl.when(j > 0)
        def _():
            acc_ref[...] += d_y

        @pl.when(j == nj - 1)
        def _():
            dy_ref[...] = acc_ref[...]

    rows = pl.BlockSpec((tm, width), lambda i, j: (i, 0))
    blk = pl.BlockSpec((tm, tn), lambda i, j: (i, j))
    gate = pl.BlockSpec((tm, tn), lambda i, j: (i, gblk + j))
    args = [y, w, z, b_gate, dmerged]
    in_specs = [rows, pl.BlockSpec((tn, width), lambda i, j: (j, 0)), gate,
                pl.BlockSpec((1, tn), lambda i, j: (0, branch * nj + j)), blk]
    aliases = {}
    if not isinstance(dz, jax.ShapeDtypeStruct):
        body = _drop_ref(body, len(args))
        aliases = {len(args): 2}
        args.append(dz)
        in_specs.append(ANY)
    return _call(
        body, args, in_specs, after, name=name, grid=(S_ // tm, nj),
        out_specs=(rows, pl.BlockSpec((D, width), lambda i, j: (0, 0)), gate, pl.BlockSpec((8, tn), lambda i, j: (i, j))),
        out_shape=(jax.ShapeDtypeStruct((S_, width), F32), jax.ShapeDtypeStruct((D, width), F32),
                   jax.ShapeDtypeStruct(dz.shape, dz.dtype), jax.ShapeDtypeStruct((S_ // tm * 8, D), F32)),
        scratch_shapes=[pltpu.VMEM((tm, width), F32)], input_output_aliases=aliases,
        compiler_params=_cparams(("arbitrary", "arbitrary")))


def _relu2_epilogue(acc):
    r = jnp.maximum(acc, 0.0)
    return (r * r,)


def _relu2_bwd_epilogue(acc, r):
    return (acc * (2.0 * jnp.sqrt(r.astype(F32))),)


def _residual_norm_epilogue(acc, xr, gain):
    x = xr + acc
    ms = jnp.mean(x * x, axis=-1, keepdims=True)
    return x, x * lax.rsqrt(ms + EPS) * gain


def _fold_rows(name, part, lanes=False, scale=1.0):
    R, N = part.shape

    def body(p_ref, o_ref):
        s = jnp.sum(p_ref[...], axis=0, keepdims=True)
        if lanes:
            s = jnp.sum(s, axis=1, keepdims=True)
        o_ref[...] = s * scale

    out_n = 1 if lanes else N
    return pl.pallas_call(
        body, name=name, grid=(1,), in_specs=[pl.BlockSpec((R, N), lambda i: (0, 0))],
        out_specs=pl.BlockSpec((1, out_n), lambda i: (0, 0)), out_shape=jax.ShapeDtypeStruct((1, out_n), F32))(part)


def _rows8(v):
    return jnp.sum(v.reshape(v.shape[0] // 8, 8, v.shape[1]), axis=0)


def _residual_loss_epilogue(acc, xr, target):
    e = xr + acc - target
    return e * (1.0 / D), _rows8(e * e)


def _norm_bwd_epilogue(acc, x, dres, gain):
    rstd = lax.rsqrt(jnp.mean(x * x, axis=-1, keepdims=True) + EPS)
    xhat = x * rstd
    dxhat = acc * gain
    c = jnp.mean(dxhat * xhat, axis=-1, keepdims=True)
    return dres + rstd * (dxhat - xhat * c), _rows8(acc * xhat)


def _layer_fwd(l, x, h, p, next_gain, target=None, early=None):
    t = f"l{l}_"
    z = _matmul(t + "in_proj", h, p["w_in_t"], "nt", 512, 3712, 1024, after=early)
    y_pool = _pool_fwd(t + "pool", z, p["pool_mix"], p["pool_scale"])
    os_, ls_ = [], []
    for g, (_, dil) in enumerate(ATTN_GROUPS):
        o_g, l_g = _attn_fwd(t + f"attn{g}", z, p["q_gain"], p["k_gain"], g, dil)
        os_.append(o_g)
        ls_.append(l_g)
    y_attn, lse = _attn_combine(t + "attn_mix", os_, ls_)
    y_conv = _conv_fwd(t + "conv", z, p["conv_w"], after=y_attn)
    merged = _mix_fwd(t + "merge", z, p["b_gate"], (y_pool, y_conv, y_attn),
                      (p["w_pool_up_t"], p["w_conv_out_t"], p["w_attn_up_t"]))
    x1, h2 = _matmul(t + "out_proj", merged, p["w_o"], "nn", 1024, D, 1024, out_dtypes=(F32, BF16),
                     extras=((x, "mn"), (p["norm_mlp"], "n")), epilogue=_residual_norm_epilogue)
    r = _matmul(t + "ff1", h2, p["w_ff1_t"], "nt", 1024, 1024, 1024, out_dtypes=(BF16,), epilogue=_relu2_epilogue)
    if next_gain is None:
        x2, h_out = _matmul(t + "ff2", r, p["w_ff2"], "nn", 512, D, D_FF, out_dtypes=(F32, (F32, "rows8")),
                            extras=((x1, "mn"), (target, "mn")), epilogue=_residual_loss_epilogue)
    else:
        x2, h_out = _matmul(t + "ff2", r, p["w_ff2"], "nn", 512, D, D_FF, out_dtypes=(F32, BF16),
                            extras=((x1, "mn"), (next_gain, "n")), epilogue=_residual_norm_epilogue)
    saved = dict(x=x, h=h, z=z, y_pool=y_pool, y_conv=y_conv, y_attn=y_attn, lse=lse, merged=merged,
                 x1=x1, h2=h2, r=r)
    return x2, h_out, saved


def _layer_bwd(l, dx2, p, s, pending, collective_id, last, start_after=None):
    t = f"l{l}_b_"
    g = {}
    rest = jax.ShapeDtypeStruct((N_DEV, REST_ROWS, D), F32)
    da = _matmul(t + "d_ff2_in", dx2, p["w_ff2"], "nt", 1024, 1024, 1024, out_dtypes=(BF16,),
                 extras=((s["r"], "mn"),), epilogue=_relu2_bwd_epilogue,
                 after=start_after)
    rest = _matmul(t + "dw_ff2", s["r"], dx2, "tn", 512, 1024, 4096, into=(rest, REST_FF2, 0), after=da)
    tok = rest if pending is None else pending.chip_sums(after=rest)
    dx1, part = _matmul(t + "d_ff1_in", da, p["w_ff1_t"], "nn", 512, D, D_FF, out_dtypes=(F32, (F32, "rows8")),
                        extras=((s["x1"], "mn"), (dx2, "mn"), (p["norm_mlp"], "n")), epilogue=_norm_bwd_epilogue,
                        after=tok)
    g["norm_mlp"] = _fold_rows(t + "d_norm_mlp", part)
    rest = _matmul(t + "dw_ff1", da, s["h2"], "tn", 512, 1024, 4096, into=(rest, REST_FF1, 0), after=dx1)
    dmerged = _matmul(t + "d_out_proj_in", dx1, p["w_o"], "nt", 1024, 1024, 1024)
    rest = _matmul(t + "dw_o", s["merged"], dx1, "tn", 128, 1024, 4096, into=(rest, REST_WO, 0), after=dmerged)
    ys = (s["y_pool"], s["y_conv"], s["y_attn"])
    names = ("w_pool_up_t", "w_conv_out_t", "w_attn_up_t")
    dys, dbs = [], []
    tok = rest
    dz = jax.ShapeDtypeStruct((dx2.shape[0], IN_COLS), BF16)
    for b in range(3):
        dy_b, dw_t, dz, db = _mix_bwd(t + f"merge{b}", s["z"], p["b_gate"], ys[b], p[names[b]], dmerged, b, dz,
                                      after=tok)
        tok = dy_b
        width = ys[b].shape[1]
        if b < 2:
            rest = rest.at[:, REST_UP:REST_ATTN, b * width:(b + 1) * width].set(
                dw_t.reshape(N_DEV, REST_ATTN - REST_UP, width))
        else:
            rest = rest.at[:, REST_ATTN:REST_ROWS, :].set(dw_t.reshape(N_DEV, REST_ROWS - REST_ATTN, D))
        dys.append(dy_b)
        dbs.append(_fold_rows(t + f"d_b_gate{b}", db))
    g["b_gate"] = jnp.concatenate(dbs, axis=1)
    rs_rest = _ReduceScatter(f"rs_rest{l}", collective_id, rest)
    pending_sum = None if pending is None else pending.finish(after=rest)
    dz, g["pool_mix"], g["pool_scale"] = _pool_bwd(t + "pool", s["z"], dys[0], p["pool_mix"], p["pool_scale"], dz,
                                                   after=rest if pending is None else pending_sum[0])
    dcb, dcc, dcx, g["conv_w"] = _conv_bwd(t + "conv", s["z"], dys[1], p["conv_w"], after=dz)
    tok = rs_rest.chip_sums(after=dcb)
    dqs, dks, dvs = [], [], []
    gq = gk = None
    for gi, (_, dil) in enumerate(ATTN_GROUPS):
        dq, dk, dv, dgq, dgk = _attn_bwd(t + f"attn{gi}", s["z"], p["q_gain"], p["k_gain"], dys[2], s["y_attn"],
                                         s["lse"], gi, dil, after=tok)
        tok = dq
        dqs.append(dq)
        dks.append(dk)
        dvs.append(dv)
        gq = dgq if gq is None else gq + dgq
        gk = dgk if gk is None else gk + dgk
    g["q_gain"], g["k_gain"] = gq, gk
    rest_sum, _ = rs_rest.finish(after=tok)
    col = OFF_CB
    for piece in [dcb, dcc, dcx] + dqs + dks + dvs:
        dz = lax.dynamic_update_slice(dz, piece.astype(BF16), (0, col))
        col += piece.shape[1]
    in_t = _matmul(t + "dw_in", dz, s["h"], "tn", 256, 1024, 4096, after=rest_sum)
    rs_in = _ReduceScatter(f"rs_in{l}", collective_id + 2, in_t.reshape(N_DEV, IN_ROWS, D))
    dx, part = _matmul(t + "d_in_proj_in", dz, p["w_in_t"], "nn", 256, D, IN_COLS, out_dtypes=(F32, (F32, "rows8")),
                       extras=((s["x"], "mn"), (dx1, "mn"), (p["norm_mix"], "n")), epilogue=_norm_bwd_epilogue,
                       after=in_t)
    g["norm_mix"] = _fold_rows(t + "d_norm_mix", part)
    if last:
        rs_in.chip_sums(after=dx)
    return dx, g, rest_sum, pending_sum, rs_in


ANY = pl.BlockSpec(memory_space=pl.ANY)


def _mesh_pos():
    return lax.axis_index("x"), lax.axis_index("y"), lax.axis_index("c")


def _other_chips(x, y):
    return [(1 - x, y), (x, 1 - y), (1 - x, 1 - y)]


def _comm_call(name, collective_id, peers, body, arrs, out_shape, sem_counts, after=None):
    n_in, n_out = len(arrs), len(out_shape)
    if collective_id is None:
        def tc_body(*refs):
            body(refs[:n_in], refs[n_in:n_in + n_out], *refs[n_in + n_out:])

        return pl.pallas_call(
            tc_body, name=name, out_shape=tuple(out_shape), in_specs=[ANY] * n_in, out_specs=(ANY,) * n_out,
            scratch_shapes=[pltpu.SemaphoreType.DMA((n,)) for n in sem_counts])(*arrs)

    n_after = 0 if after is None else 1

    def seq_body(*refs):
        barrier = pltpu.get_barrier_semaphore()
        ps = peers(*_mesh_pos())
        for p in ps:
            pl.semaphore_signal(barrier, inc=1, device_id=p, device_id_type=MESH)
        pl.semaphore_wait(barrier, len(ps))
        outs = refs[n_in + n_after:]
        body(refs[:n_in], outs[:n_out], *outs[n_out:])

    return pl.kernel(
        seq_body, out_type=tuple(out_shape), mesh=plsc.ScalarSubcoreMesh(axis_name="seq", num_cores=1), name=name,
        scratch_types=[pltpu.SemaphoreType.DMA((n,)) for n in sem_counts],
        compiler_params=pltpu.CompilerParams(collective_id=collective_id),
    )(*arrs, *([] if after is None else [after]))


def _all_gather(name, collective_id, shard, after=None):
    R, C = shard.shape

    def peers(x, y, c):
        return [(x, y, 1 - c)] + [(*chip, c) for chip in _other_chips(x, y)]

    def body(in_refs, out_refs, send_sems, recv_sems, local_sems):
        (x_ref,), (out_ref,) = in_refs, out_refs
        x, y, c = _mesh_pos()
        me, sibling = (x, y, c), (x, y, 1 - c)
        chips = _other_chips(x, y)

        def slot(px, py, pc):
            return out_ref.at[4 * px + 2 * py + pc]

        def copy(k, block, to, src=None):
            return pltpu.make_async_remote_copy(
                src_ref=slot(*block) if src is None else src, dst_ref=slot(*block),
                send_sem=send_sems.at[k], recv_sem=recv_sems.at[k], device_id=to, device_id_type=MESH)

        mine = pltpu.make_async_copy(x_ref, slot(*me), local_sems.at[0])
        mine.start()
        first = [copy(0, me, sibling, src=x_ref)]
        first += [copy(1 + j, me, (*chip, c), src=x_ref) for j, chip in enumerate(chips)]
        for cp in first:
            cp.start()
        passed = [copy(4 + j, (*chip, c), sibling) for j, chip in enumerate(chips)]
        for j, chip in enumerate(chips):
            copy(1 + j, (*chip, c), me).wait_recv()
            passed[j].start()
        copy(0, sibling, me).wait_recv()
        for j, chip in enumerate(chips):
            copy(4 + j, (*chip, 1 - c), me).wait_recv()
        for cp in first + passed:
            cp.wait_send()
        mine.wait()

    return _comm_call(name, collective_id, peers, body, [shard],
                      [jax.ShapeDtypeStruct((N_DEV, R, C), shard.dtype)], (7, 7, 1), after)[0]


def _rs_sibling_exchange(name, collective_id, arrs):
    n = len(arrs)

    def body(in_refs, out_refs, send_sems, recv_sems):
        x, y, c = _mesh_pos()
        cps = []
        for k, (src, dst) in enumerate(zip(in_refs, out_refs)):
            src = src.at[:, 1 - c] if len(src.shape) == 4 else src
            cps.append(pltpu.make_async_remote_copy(src_ref=src, dst_ref=dst, send_sem=send_sems.at[k],
                                                    recv_sem=recv_sems.at[k], device_id=(x, y, 1 - c),
                                                    device_id_type=MESH))
        for cp in cps:
            cp.start()
        for cp in cps:
            cp.wait()

    out_shape = [jax.ShapeDtypeStruct(a.shape[:1] + a.shape[2:] if a.ndim == 4 else a.shape, a.dtype) for a in arrs]
    return _comm_call(name, collective_id, lambda x, y, c: [(x, y, 1 - c)], body, arrs, out_shape, (n, n))


def _rs_rows(r):
    return r // 2 if (r // 2) % 16 == 0 else r


def _drop_ref(body, idx):
    def wrapped(*refs):
        return body(*refs[:idx], *refs[idx + 1:])

    return wrapped


def _rs_chip_sum(name, ids, big, rbig, after=None):
    _, _, R, C = big.shape
    rows = _rs_rows(R)

    def body(ids_ref, mine_ref, theirs_ref, t16_ref, own_ref):
        p = pl.program_id(1)
        s = mine_ref[...] + theirs_ref[...]
        t16_ref[...] = s.astype(BF16)

        @pl.when(p == ids_ref[1])
        def _():
            own_ref[...] = s

    in_specs = [pl.BlockSpec((None, None, rows, C), lambda i, p, ids: (p, ids[0], i, 0)),
                pl.BlockSpec((None, rows, C), lambda i, p, ids: (p, i, 0))]
    args = [ids, big, rbig]
    if after is not None:
        body = _drop_ref(body, len(args))
        in_specs.append(ANY)
        args.append(after)
    grid_spec = pltpu.PrefetchScalarGridSpec(
        num_scalar_prefetch=1, grid=(R // rows, 4), in_specs=in_specs,
        out_specs=(pl.BlockSpec((None, rows, C), lambda i, p, ids: (p, i, 0)),
                   pl.BlockSpec((rows, C), lambda i, p, ids: (i, 0))))
    return pl.pallas_call(
        body, name=name, grid_spec=grid_spec,
        out_shape=(jax.ShapeDtypeStruct((4, R, C), BF16), jax.ShapeDtypeStruct((R, C), F32)),
        compiler_params=_cparams(("parallel", "arbitrary")),
    )(*args)


def _add2(name, a, b):
    def body(a_ref, b_ref, o_ref):
        o_ref[...] = a_ref[...] + b_ref[...]

    full = pl.BlockSpec(a.shape, lambda i: (0, 0))
    return pl.pallas_call(body, name=name, grid=(1,), in_specs=[full, full], out_specs=full,
                          out_shape=jax.ShapeDtypeStruct(a.shape, a.dtype))(a, b)


def _rs_chip_exchange(name, collective_id, arrs):
    n = len(arrs)

    def body(in_refs, out_refs, send_sems, recv_sems, local_sems):
        x, y, c = _mesh_pos()
        p_me = 2 * x + y
        chips = _other_chips(x, y)

        def part(ref, p):
            return ref.at[p] if len(ref.shape) == 3 else ref

        local = [pltpu.make_async_copy(part(src, p_me), dst.at[p_me], local_sems.at[k])
                 for k, (src, dst) in enumerate(zip(in_refs, out_refs))]
        for cp in local:
            cp.start()
        sends = []
        for j, (px, py) in enumerate(chips):
            for k, (src, dst) in enumerate(zip(in_refs, out_refs)):
                sends.append(pltpu.make_async_remote_copy(
                    src_ref=part(src, 2 * px + py), dst_ref=dst.at[p_me], send_sem=send_sems.at[n * j + k],
                    recv_sem=recv_sems.at[n * j + k], device_id=(px, py, c), device_id_type=MESH))
        for cp in sends:
            cp.start()
        for j, (px, py) in enumerate(chips):
            for k, (src, dst) in enumerate(zip(in_refs, out_refs)):
                pltpu.make_async_remote_copy(
                    src_ref=part(src, p_me), dst_ref=dst.at[2 * px + py], send_sem=send_sems.at[n * j + k],
                    recv_sem=recv_sems.at[n * j + k], device_id=(px, py, c), device_id_type=MESH).wait_recv()
        for cp in sends:
            cp.wait_send()
        for cp in local:
            cp.wait()

    out_shape = [jax.ShapeDtypeStruct((4,) + a.shape[-2:], a.dtype) for a in arrs]
    return _comm_call(name, collective_id, lambda x, y, c: [(*chip, c) for chip in _other_chips(x, y)], body, arrs,
                      out_shape, (3 * n, 3 * n, n))


def _rs_final_sum(name, ids, recv, own, rows, after=None):
    _, R, C = recv.shape
    assert R % rows == 0

    def body(ids_ref, r_ref, own_ref, o_ref):
        acc = None
        for p in range(4):
            term = jnp.where(ids_ref[1] == p, own_ref[...], r_ref[p].astype(F32))
            acc = term if acc is None else acc + term
        o_ref[...] = acc

    in_specs = [pl.BlockSpec((4, rows, C), lambda i, ids: (0, i, 0)), pl.BlockSpec((rows, C), lambda i, ids: (i, 0))]
    args = [ids, recv, own]
    if after is not None:
        body = _drop_ref(body, len(args))
        in_specs.append(ANY)
        args.append(after)
    grid_spec = pltpu.PrefetchScalarGridSpec(
        num_scalar_prefetch=1, grid=(R // rows,), in_specs=in_specs,
        out_specs=pl.BlockSpec((rows, C), lambda i, ids: (i, 0)))
    return pl.pallas_call(
        body, name=name, grid_spec=grid_spec, out_shape=jax.ShapeDtypeStruct((R, C), F32),
        compiler_params=_cparams(("parallel",)),
    )(*args)


class _ReduceScatter:
    def __init__(self, tag, collective_id, big, small=None):
        x, y, c = _mesh_pos()
        self.tag, self.cid, self.small = tag, collective_id, small
        self.ids = jnp.stack([c, 2 * x + y]).astype(jnp.int32)
        self.big = big.reshape((4, 2) + big.shape[1:])
        self.got = _rs_sibling_exchange(tag + "_sibling", collective_id,
                                        [self.big] + ([] if small is None else [small]))

    def chip_sums(self, after=None):
        t16, self.own = _rs_chip_sum(self.tag + "_chip_sum", self.ids, self.big, self.got[0], after)
        arrs = [t16]
        if self.small is not None:
            self.ts = _add2(self.tag + "_chip_sum_small", self.small, self.got[1])
            arrs.append(self.ts)
        self.recv = _rs_chip_exchange(self.tag + "_chips", self.cid + 1, arrs)
        return t16

    def finish(self, after=None):
        out = _rs_final_sum(self.tag + "_final", self.ids, self.recv[0], self.own, _rs_rows(self.own.shape[0]), after)
        out_small = None
        if self.small is not None:
            out_small = _rs_final_sum(self.tag + "_final_small", self.ids, self.recv[1], self.ts, self.small.shape[0])
        return out, out_small


def _all_reduce_small(tag, small):
    x, y, c = _mesh_pos()
    ids = jnp.stack([c, 2 * x + y]).astype(jnp.int32)
    (theirs,) = _rs_sibling_exchange(tag + "_sibling", None, [small])
    ts = _add2(tag + "_chip_sum", small, theirs)
    (recv,) = _rs_chip_exchange(tag + "_chips", None, [ts])
    return _rs_final_sum(tag + "_final", ids, recv, ts, small.shape[0])


def _adamw(name, w, g, m, v):
    R, C = w.shape
    tr = max(t for t in range(8, 513, 8) if R % t == 0)
    c1 = 1.0 - ADAM_B1 ** ADAM_STEP
    c2 = 1.0 - ADAM_B2 ** ADAM_STEP

    def body(w_ref, g_ref, m_ref, v_ref, d_ref, nm_ref, nv_ref):
        gv = g_ref[...]
        nm = ADAM_B1 * m_ref[...] + (1.0 - ADAM_B1) * gv
        nv = ADAM_B2 * v_ref[...] + (1.0 - ADAM_B2) * (gv * gv)
        d_ref[...] = -ADAM_LR * ((nm / c1) / (jnp.sqrt(nv / c2) + ADAM_EPS) + ADAM_WD * w_ref[...])
        nm_ref[...] = nm
        nv_ref[...] = nv

    blk = pl.BlockSpec((tr, C), lambda i: (i, 0))
    shp = jax.ShapeDtypeStruct((R, C), F32)
    return pl.pallas_call(
        body, name=name, grid=(R // tr,), in_specs=[blk] * 4, out_specs=(blk,) * 3, out_shape=(shp,) * 3,
        compiler_params=_cparams(("parallel",)),
    )(w, g, m, v)


def _adamw_t(name, w, g_t, m, v):
    L, R, C = w.shape
    tr = 256
    c1 = 1.0 - ADAM_B1 ** ADAM_STEP
    c2 = 1.0 - ADAM_B2 ** ADAM_STEP

    def body(w_ref, g_ref, m_ref, v_ref, go_ref, d_ref, nm_ref, nv_ref):
        gv = g_ref[...].T
        go_ref[...] = gv
        nm = ADAM_B1 * m_ref[...] + (1.0 - ADAM_B1) * gv
        nv = ADAM_B2 * v_ref[...] + (1.0 - ADAM_B2) * (gv * gv)
        d_ref[...] = -ADAM_LR * ((nm / c1) / (jnp.sqrt(nv / c2) + ADAM_EPS) + ADAM_WD * w_ref[...])
        nm_ref[...] = nm
        nv_ref[...] = nv

    blk = pl.BlockSpec((None, tr, C), lambda l, i: (l, i, 0))
    blk_t = pl.BlockSpec((None, C, tr), lambda l, i: (l, 0, i))
    shp = jax.ShapeDtypeStruct((L, R, C), F32)
    return pl.pallas_call(
        body, name=name, grid=(L, R // tr), in_specs=[blk, blk_t, blk, blk], out_specs=(blk,) * 4,
        out_shape=(shp,) * 4, compiler_params=_cparams(("parallel", "parallel")),
    )(w, g_t, m, v)


REST_FF1 = 0
REST_FF2 = 512
REST_WO = 1024
REST_UP = 1152
REST_ATTN = 1280
REST_ROWS = 1312
IN_ROWS = IN_COLS // N_DEV
CONV_BITS_ROWS = 16
SHARD_ROWS = IN_ROWS + REST_ROWS + CONV_BITS_ROWS
SMALL = (("norm_mix", (DEPTH, D)), ("b_gate", (DEPTH, 3 * D)), ("pool_mix", (DEPTH, 4, POOL_G, POOL_G)),
         ("pool_scale", (DEPTH, POOL_W)), ("conv_w", (DEPTH, 3, CONV_W)), ("q_gain", (DEPTH, HEAD)),
         ("k_gain", (DEPTH, HEAD)), ("norm_mlp", (DEPTH, D)))


def _pack_weight_shards(w, l):
    b = lambda a: a.astype(BF16)
    conv = lax.bitcast_convert_type(w["conv_w"][l], BF16).reshape(3, 128)
    conv = jnp.pad(conv, ((0, CONV_BITS_ROWS - 3), (0, D - 128)))
    rest = jnp.concatenate([
        b(w["w_ff1"][l].T), b(w["w_ff2"][l]), b(w["w_o"][l]),
        jnp.concatenate([b(w["w_pool_up"][l].T), b(w["w_conv_out"][l].T)], axis=1),
        b(w["w_attn_up"][l].T).reshape(REST_ROWS - REST_ATTN, D), conv], axis=0)
    return b(w["w_in"][l].T), rest


def _unpack_gathered(g_in, g_rest, small_w, l):
    rest = g_rest[:, :REST_ROWS]
    take = lambda r0, rows, c0=0, cols=D: rest[:, r0:r0 + rows, c0:c0 + cols].reshape(N_DEV * rows, cols)
    conv = g_rest[:, REST_ROWS:REST_ROWS + 3, :128].reshape(N_DEV, 3, CONV_W // N_DEV, 2)
    conv = lax.bitcast_convert_type(conv, F32)
    p = {
        "w_in_t": g_in.reshape(IN_COLS, D),
        "w_ff1_t": take(REST_FF1, 512), "w_ff2": take(REST_FF2, 512), "w_o": take(REST_WO, 128),
        "w_pool_up_t": take(REST_UP, 128, 0, POOL_W), "w_conv_out_t": take(REST_UP, 128, POOL_W, CONV_W),
        "w_attn_up_t": rest[:, REST_ATTN:].reshape(D, ATTN_OUT),
        "conv_w": jnp.transpose(conv, (1, 0, 2)).reshape(3, CONV_W),
        "pool_mix": small_w["pool_mix"][l],
    }
    for name in ("norm_mix", "b_gate", "pool_scale", "q_gain", "k_gain", "norm_mlp"):
        p[name] = small_w[name][l][None, :]
    return p


def _pack_small_grads(grads):
    flat = jnp.concatenate([jnp.stack([grads[l][name] for l in range(DEPTH)]).reshape(-1) for name, _ in SMALL])
    return jnp.pad(flat, (0, (-flat.shape[0]) % (8 * 128))).reshape(-1, 128)


def _unpack_grads(in_t, rest, small, dev):
    out = {
        "w_in_t": jnp.stack(in_t),
        "w_ff1_t": jnp.stack([a[REST_FF1:REST_FF1 + 512] for a in rest]),
        "w_ff2": jnp.stack([a[REST_FF2:REST_FF2 + 512] for a in rest]),
        "w_o": jnp.stack([a[REST_WO:REST_WO + 128] for a in rest]),
        "w_pool_up": jnp.stack([a[REST_UP:REST_UP + 128, :POOL_W].T for a in rest]),
        "w_conv_out": jnp.stack([a[REST_UP:REST_UP + 128, POOL_W:].T for a in rest]),
        "w_attn_up": jnp.stack([a[REST_ATTN:].reshape(D // N_DEV, ATTN_OUT).T for a in rest]),
    }
    out.update(_unpack_small_like_grads(small, dev))
    return out


def _unpack_small_like_grads(small, dev):
    out = {}
    flat = small.reshape(-1)
    off = 0
    for name, shp in SMALL:
        n = 1
        for s in shp:
            n *= s
        out[name] = flat[off:off + n].reshape(shp)
        off += n
    width = CONV_W // N_DEV
    out["conv_w"] = lax.dynamic_slice_in_dim(out["conv_w"], dev * width, width, axis=2)
    return out


def _pack_small_like_grads(src, dev):
    parts = []
    for name, shp in SMALL:
        a = src[name]
        if name == "conv_w":
            a = lax.dynamic_update_slice_in_dim(jnp.zeros(shp, F32), a, dev * (CONV_W // N_DEV), axis=2)
        parts.append(a.reshape(-1))
    flat = jnp.concatenate(parts)
    return jnp.pad(flat, (0, (-flat.shape[0]) % (8 * 128))).reshape(-1, 128)


WEIGHTS = ("norm_mix", "w_in", "b_gate", "pool_mix", "pool_scale", "conv_w", "q_gain", "k_gain", "w_pool_up",
           "w_conv_out", "w_attn_up", "w_o", "norm_mlp", "w_ff1", "w_ff2")


def kernel(x, norm_mix, w_in, b_gate, pool_mix, pool_scale, conv_w, q_gain, k_gain, w_pool_up, w_conv_out, w_attn_up, w_o, norm_mlp, w_ff1, w_ff2, loss_target, m_norm_mix, m_w_in, m_b_gate, m_pool_mix, m_pool_scale, m_conv_w, m_q_gain, m_k_gain, m_w_pool_up, m_w_conv_out, m_w_attn_up, m_w_o, m_norm_mlp, m_w_ff1, m_w_ff2, v_norm_mix, v_w_in, v_b_gate, v_pool_mix, v_pool_scale, v_conv_w, v_q_gain, v_k_gain, v_w_pool_up, v_w_conv_out, v_w_attn_up, v_w_o, v_norm_mlp, v_w_ff1, v_w_ff2):
    w = dict(zip(WEIGHTS, (norm_mix, w_in, b_gate, pool_mix, pool_scale, conv_w, q_gain, k_gain, w_pool_up,
                           w_conv_out, w_attn_up, w_o, norm_mlp, w_ff1, w_ff2)))
    m = dict(zip(WEIGHTS, (m_norm_mix, m_w_in, m_b_gate, m_pool_mix, m_pool_scale, m_conv_w, m_q_gain, m_k_gain,
                           m_w_pool_up, m_w_conv_out, m_w_attn_up, m_w_o, m_norm_mlp, m_w_ff1, m_w_ff2)))
    v = dict(zip(WEIGHTS, (v_norm_mix, v_w_in, v_b_gate, v_pool_mix, v_pool_scale, v_conv_w, v_q_gain, v_k_gain,
                           v_w_pool_up, v_w_conv_out, v_w_attn_up, v_w_o, v_norm_mlp, v_w_ff1, v_w_ff2)))
    xi, yi, ci = _mesh_pos()
    dev = 4 * xi + 2 * yi + ci

    saved, params = [], []
    act = x[0]
    h = _rmsnorm_fwd("l0_norm_mix", act, w["norm_mix"][0][None])
    w_in_t, m_in_t, v_in_t = (jnp.swapaxes(src["w_in"], 1, 2) for src in (w, m, v))
    small_packs = [_pack_small_like_grads(src, dev) for src in (w, m, v)]
    for l in range(DEPTH):
        s_in, s_rest = _pack_weight_shards(w, l)
        g_in = _all_gather(f"gather_in{l}", 1 + 2 * l, s_in)
        g_rest = _all_gather(f"gather_rest{l}", 2 + 2 * l, s_rest)
        params.append(_unpack_gathered(g_in, g_rest, w, l))
    for l in range(DEPTH):
        act, h, s = _layer_fwd(l, act, h, params[l], w["norm_mix"][l + 1][None] if l + 1 < DEPTH else None,
                               loss_target[0])
        saved.append(s)
    dy = act
    loss = _fold_rows("loss_sum", h, lanes=True, scale=0.5 / D)
    total = lax.psum(loss[0, 0], ("x", "y", "c"))
    in_t, rest, grads, pending = [None] * DEPTH, [None] * DEPTH, [None] * DEPTH, None
    start = total.reshape(1, 1)
    for l in reversed(range(DEPTH)):
        dy, grads[l], rest[l], done, pending = _layer_bwd(l, dy, params[l], saved[l], pending, 5 + 4 * l, l == 0,
                                                          start)
        start = None
        if done is not None:
            in_t[l + 1] = done[0]
    dx = dy
    g_small = _all_reduce_small("ar_small", _pack_small_grads(grads))
    in_t[0], _ = pending.finish(after=g_small)
    g = _unpack_grads(in_t, rest, g_small, dev)

    delta, new_m, new_v = {}, {}, {}
    g_t = g["w_in_t"]
    two_d = (g_t.shape[0] * g_t.shape[1], g_t.shape[2])
    d_, m_, v_ = _adamw("adamw_w_in", *[a.reshape(two_d) for a in (w_in_t, g_t, m_in_t, v_in_t)])
    g["w_in"] = jnp.swapaxes(g_t, 1, 2)
    delta["w_in"], new_m["w_in"], new_v["w_in"] = (jnp.swapaxes(a.reshape(g_t.shape), 1, 2) for a in (d_, m_, v_))
    g["w_ff1"], delta["w_ff1"], new_m["w_ff1"], new_v["w_ff1"] = _adamw_t("adamw_w_ff1", w["w_ff1"], g["w_ff1_t"],
                                                                          m["w_ff1"], v["w_ff1"])
    for name in ("w_pool_up", "w_conv_out", "w_attn_up", "w_o", "w_ff2"):
        shp = w[name].shape
        two_d = (shp[0] * shp[1], shp[2])
        d_, m_, v_ = _adamw("adamw_" + name, w[name].reshape(two_d), g[name].reshape(two_d),
                            m[name].reshape(two_d), v[name].reshape(two_d))
        delta[name], new_m[name], new_v[name] = d_.reshape(shp), m_.reshape(shp), v_.reshape(shp)
    outs = _adamw("adamw_small", small_packs[0], g_small, small_packs[1], small_packs[2])
    for dst, arr in zip((delta, new_m, new_v), outs):
        dst.update(_unpack_small_like_grads(arr, dev))

    return (total, dx[None], *[g[n] for n in WEIGHTS], *[delta[n] for n in WEIGHTS],
            *[new_m[n] for n in WEIGHTS], *[new_v[n] for n in WEIGHTS])
```

```python
import functools

import jax
import jax.numpy as jnp
from jax import lax
from jax.experimental import pallas as pl
from jax.experimental.pallas import tpu as pltpu
from jax.experimental.pallas import tpu_sc as plsc

F32 = jnp.float32
BF16 = jnp.bfloat16
MESH = pl.DeviceIdType.MESH

D = 1024
SEQ = 4096
DEPTH = 2
N_DEV = 8
POOL_WINDOWS = (2, 4, 8, 16)
POOL_W = 512
POOL_G = 128
CONV_W = 512
HEAD = 64
ATTN_GROUPS = ((128, 1), (512, 4), (2048, 16))
HPG = 4
ATTN_W = 768
ATTN_OUT = 256
ATTN_BLK = 128
D_FF = 4096
EPS = 1e-6
MASK = -1e30
OFF_POOL = 0
OFF_CB = 512
OFF_CC = 1024
OFF_CX = 1536
OFF_Q = 2048
OFF_K = 2816
OFF_V = 3584
OFF_GATE = 4352
IN_COLS = 7424
ADAM_LR = 0.001
ADAM_B1 = 0.9
ADAM_B2 = 0.999
ADAM_EPS = 1e-08
ADAM_WD = 0.01
ADAM_STEP = 10

ROW_TILE = 512
SEQ_CHUNK = 256
HALO = 16
VMEM_LIMIT = 56 * 1024 * 1024


def _cparams(sem=None):
    return pltpu.CompilerParams(dimension_semantics=sem, vmem_limit_bytes=VMEM_LIMIT)


def _call(body, args, in_specs, after=None, **kw):
    if after is not None:
        after = list(after) if isinstance(after, (list, tuple)) else [after]
        n, k, inner = len(args), len(after), body

        def body(*refs):
            return inner(*refs[:n], *refs[n + k:])

        args = list(args) + after
        in_specs = list(in_specs) + [pl.BlockSpec(memory_space=pl.ANY)] * k
    return pl.pallas_call(body, in_specs=in_specs, **kw)(*args)


_DIMS = {"nn": (((1,), (0,)), ((), ())), "nt": (((1,), (1,)), ((), ())), "tn": (((0,), (0,)), ((), ()))}


def _matmul(name, a, b, mode, tm, tn, tk, out_dtypes=(F32,), extras=(), epilogue=None, into=None, after=None):
    if mode == "tn":
        K, M = a.shape
    else:
        M, K = a.shape
    N = b.shape[0] if mode == "nt" else b.shape[1]
    assert M % tm == 0 and N % tn == 0 and K % tk == 0, (name, M, N, K, tm, tn, tk)
    nk = K // tk
    n_extra = len(extras)
    n_out = len(out_dtypes)
    dims = _DIMS[mode]
    n_alias = 0 if into is None or isinstance(into[0], jax.ShapeDtypeStruct) else 1

    def body(a_ref, b_ref, *rest):
        extra_refs = rest[:n_extra]
        out_refs = rest[n_extra + n_alias:n_extra + n_alias + n_out]

        def finish(acc):
            if epilogue is None:
                res = (acc,)
            else:
                res = epilogue(acc, *[r[...] for r in extra_refs])
            for o_ref, v in zip(out_refs, res):
                o_ref[...] = v.astype(o_ref.dtype)

        part = lax.dot_general(a_ref[...].astype(BF16), b_ref[...].astype(BF16), dims,
                               preferred_element_type=F32)
        if nk == 1:
            finish(part)
        else:
            acc_ref = rest[-1]
            k = pl.program_id(2)

            @pl.when(k == 0)
            def _():
                acc_ref[...] = part

            @pl.when(k > 0)
            def _():
                acc_ref[...] += part

            @pl.when(k == nk - 1)
            def _():
                finish(acc_ref[...])

    if mode == "tn":
        a_spec = pl.BlockSpec((tk, tm), lambda i, j, k: (k, i))
    else:
        a_spec = pl.BlockSpec((tm, tk), lambda i, j, k: (i, k))
    if mode == "nt":
        b_spec = pl.BlockSpec((tn, tk), lambda i, j, k: (j, k))
    else:
        b_spec = pl.BlockSpec((tk, tn), lambda i, j, k: (k, j))
    in_specs = [a_spec, b_spec]
    args = [a, b]
    for arr, kind in extras:
        if kind == "mn":
            in_specs.append(pl.BlockSpec((tm, tn), lambda i, j, k: (i, j)))
        else:
            in_specs.append(pl.BlockSpec((1, tn), lambda i, j, k: (0, j)))
        args.append(arr)
    part = [isinstance(dt, tuple) for dt in out_dtypes]
    out_shape = tuple(jax.ShapeDtypeStruct((M // tm * 8, N), dt[0]) if p else jax.ShapeDtypeStruct((M, N), dt)
                      for dt, p in zip(out_dtypes, part))
    out_specs = tuple(pl.BlockSpec((8 if p else tm, tn), lambda i, j, k: (i, j)) for p in part)
    aliases = {}
    if into is not None:
        buf, row0, col0 = into
        assert n_out == 1 and (M // N_DEV) % tm == 0 and row0 % tm == 0 and col0 % tn == 0
        per_dev = M // N_DEV // tm
        out_shape = (jax.ShapeDtypeStruct(buf.shape, buf.dtype),)
        out_specs = (pl.BlockSpec((None, tm, tn), lambda i, j, k: (i // per_dev, row0 // tm + i % per_dev,
                                                                   col0 // tn + j)),)
        if not isinstance(buf, jax.ShapeDtypeStruct):
            aliases = {len(args): 0}
            in_specs.append(pl.BlockSpec(memory_space=pl.ANY))
            args.append(buf)
    scratch = [] if nk == 1 else [pltpu.VMEM((tm, tn), F32)]
    res = _call(
        body, args, in_specs, after, name=name, grid=(M // tm, N // tn, nk), out_specs=out_specs,
        out_shape=out_shape, scratch_shapes=scratch, input_output_aliases=aliases,
        compiler_params=_cparams(("parallel", "parallel", "arbitrary")))
    return res if n_out > 1 else res[0]


def _rmsnorm_fwd(name, x, gain, after=None):
    S_, D_ = x.shape

    def body(x_ref, g_ref, h_ref):
        xf = x_ref[...]
        ms = jnp.mean(xf * xf, axis=-1, keepdims=True)
        h_ref[...] = (xf * lax.rsqrt(ms + EPS) * g_ref[...]).astype(BF16)

    return _call(
        body, [x, gain], [pl.BlockSpec((ROW_TILE, D_), lambda i: (i, 0)), pl.BlockSpec((1, D_), lambda i: (0, 0))],
        after, name=name, grid=(S_ // ROW_TILE,),
        out_specs=pl.BlockSpec((ROW_TILE, D_), lambda i: (i, 0)),
        out_shape=jax.ShapeDtypeStruct((S_, D_), BF16),
        compiler_params=_cparams(("parallel",)))


def _rmsnorm_bwd(name, x, gain, dh, dres, after=None):
    S_, D_ = x.shape
    n = S_ // ROW_TILE

    def body(x_ref, g_ref, dh_ref, dres_ref, dx_ref, dg_ref, acc_ref):
        i = pl.program_id(0)
        xf = x_ref[...]
        rstd = lax.rsqrt(jnp.mean(xf * xf, axis=-1, keepdims=True) + EPS)
        xhat = xf * rstd
        dhv = dh_ref[...]
        dxhat = dhv * g_ref[...]
        c = jnp.mean(dxhat * xhat, axis=-1, keepdims=True)
        dx_ref[...] = dres_ref[...] + rstd * (dxhat - xhat * c)
        part = jnp.sum((dhv * xhat).reshape(ROW_TILE // 8, 8, D_), axis=0)

        @pl.when(i == 0)
        def _():
            acc_ref[...] = part

        @pl.when(i > 0)
        def _():
            acc_ref[...] += part

        @pl.when(i == n - 1)
        def _():
            dg_ref[...] = jnp.sum(acc_ref[...], axis=0, keepdims=True)

    row = pl.BlockSpec((ROW_TILE, D_), lambda i: (i, 0))
    vec = pl.BlockSpec((1, D_), lambda i: (0, 0))
    return _call(
        body, [x, gain, dh, dres], [row, vec, row, row], after, name=name, grid=(n,), out_specs=(row, vec),
        out_shape=(jax.ShapeDtypeStruct((S_, D_), F32), jax.ShapeDtypeStruct((1, D_), F32)),
        scratch_shapes=[pltpu.VMEM((8, D_), F32)],
        compiler_params=_cparams(("arbitrary",)))


def _loss_head(name, y, target):
    S_, D_ = y.shape
    n = S_ // ROW_TILE

    def body(y_ref, t_ref, dy_ref, l_ref, acc_ref):
        i = pl.program_id(0)
        e = y_ref[...] - t_ref[...]
        dy_ref[...] = e * (1.0 / D_)
        part = jnp.sum((e * e).reshape(ROW_TILE // 8, 8, D_), axis=0)

        @pl.when(i == 0)
        def _():
            acc_ref[...] = part

        @pl.when(i > 0)
        def _():
            acc_ref[...] += part

        @pl.when(i == n - 1)
        def _():
            s = jnp.sum(acc_ref[...], axis=1, keepdims=True)
            l_ref[...] = jnp.sum(s, axis=0, keepdims=True) * (0.5 / D_)

    row = pl.BlockSpec((ROW_TILE, D_), lambda i: (i, 0))
    return pl.pallas_call(
        body, name=name, grid=(n,), in_specs=[row, row],
        out_specs=(row, pl.BlockSpec((1, 1), lambda i: (0, 0))),
        out_shape=(jax.ShapeDtypeStruct((S_, D_), F32), jax.ShapeDtypeStruct((1, 1), F32)),
        scratch_shapes=[pltpu.VMEM((8, D_), F32)],
        compiler_params=_cparams(("arbitrary",)),
    )(y, target)


def _rows_with_halo(ref, cols, i, n_chunks, before, after):
    r0 = pl.multiple_of(i * SEQ_CHUNK, SEQ_CHUNK)
    parts = []
    if before:
        h0 = pl.multiple_of(jnp.maximum(r0 - HALO, 0), 8)
        halo = ref[pl.ds(h0, HALO), cols]
        parts.append(jnp.where(i > 0, halo, jnp.zeros_like(halo)))
    parts.append(ref[pl.ds(r0, SEQ_CHUNK), cols])
    if after:
        a0 = pl.multiple_of(jnp.minimum(r0 + SEQ_CHUNK, (n_chunks - 1) * SEQ_CHUNK + SEQ_CHUNK - HALO), 8)
        halo = ref[pl.ds(a0, HALO), cols]
        parts.append(jnp.where(i < n_chunks - 1, halo, jnp.zeros_like(halo)))
    return parts[0] if len(parts) == 1 else jnp.concatenate(parts, axis=0)


def _shift_down(v, k):
    return pltpu.roll(v, k, 0)


def _shift_up(v, k):
    return pltpu.roll(v, v.shape[0] - k, 0)


def _pool_diff(xx, w, t_main):
    s = xx
    k = 1
    while k < w:
        s = s + _shift_down(s, k)
        k *= 2
    cnt = jnp.minimum(t_main + 1, w).astype(F32)
    return s[HALO:] / cnt - xx[HALO:]


def _pool_fwd(name, z, pool_mix, pool_scale):
    S_ = z.shape[0]
    n_chunks = S_ // SEQ_CHUNK

    cols = slice(0, POOL_G)

    def body(u_ref, mix_ref, sc_ref, y_ref):
        mixg = mix_ref[...].astype(BF16)
        scg = sc_ref[...]
        for g, w in enumerate(POOL_WINDOWS):
            @pl.when(pl.program_id(0) == g)
            def _(w=w):
                def chunk(i, carry):
                    r0 = pl.multiple_of(i * SEQ_CHUNK, SEQ_CHUNK)
                    xx = _rows_with_halo(u_ref, cols, i, n_chunks, True, False)
                    t = r0 + lax.broadcasted_iota(jnp.int32, (SEQ_CHUNK, POOL_G), 0)
                    d = _pool_diff(xx, w, t)
                    y = jnp.dot(d.astype(BF16), mixg, preferred_element_type=F32) * scg
                    y_ref[pl.ds(r0, SEQ_CHUNK), :] = y.astype(BF16)
                    return carry

                lax.fori_loop(0, n_chunks, chunk, 0)

    slab = pl.BlockSpec((S_, POOL_G), lambda g: (0, g))
    return pl.pallas_call(
        body, name=name, grid=(len(POOL_WINDOWS),),
        in_specs=[slab, pl.BlockSpec((None, POOL_G, POOL_G), lambda g: (g, 0, 0)),
                  pl.BlockSpec((1, POOL_G), lambda g: (0, g))],
        out_specs=slab, out_shape=jax.ShapeDtypeStruct((S_, POOL_W), BF16),
        compiler_params=_cparams(("parallel",)),
    )(z, pool_mix, pool_scale)


def _pool_bwd(name, z, dy, pool_mix, pool_scale, dz, after=None):
    S_ = z.shape[0]
    n_chunks = S_ // SEQ_CHUNK
    rows_a = SEQ_CHUNK + HALO

    cols = slice(0, POOL_G)

    def body(u_ref, dy_ref, mix_ref, sc_ref, du_ref, dmix_ref, dsc_ref):
        mixg = mix_ref[...].astype(BF16)
        scg = sc_ref[...]
        for g, w in enumerate(POOL_WINDOWS):
            @pl.when(pl.program_id(0) == g)
            def _(w=w):
                def chunk(i, carry):
                    dmix_acc, dsc_acc = carry
                    r0 = pl.multiple_of(i * SEQ_CHUNK, SEQ_CHUNK)
                    xx = _rows_with_halo(u_ref, cols, i, n_chunks, True, False)
                    t = r0 + lax.broadcasted_iota(jnp.int32, (SEQ_CHUNK, POOL_G), 0)
                    d = _pool_diff(xx, w, t).astype(BF16)
                    ypre = jnp.dot(d, mixg, preferred_element_type=F32)
                    dyy = _rows_with_halo(dy_ref, cols, i, n_chunks, False, True)
                    dys = (dyy * scg).astype(BF16)
                    dsc_acc = dsc_acc + jnp.sum((dyy[:SEQ_CHUNK] * ypre).reshape(SEQ_CHUNK // 8, 8, POOL_G), axis=0)
                    dmix_acc = dmix_acc + lax.dot_general(d, dys[:SEQ_CHUNK], _DIMS["tn"],
                                                          preferred_element_type=F32)
                    dd = lax.dot_general(dys, mixg, _DIMS["nt"], preferred_element_type=F32)
                    ta = r0 + lax.broadcasted_iota(jnp.int32, (rows_a, POOL_G), 0)
                    f = dd / jnp.minimum(ta + 1, w).astype(F32)
                    k = 1
                    while k < w:
                        f = f + _shift_up(f, k)
                        k *= 2
                    du_ref[pl.ds(r0, SEQ_CHUNK), :] = (f[:SEQ_CHUNK] - dd[:SEQ_CHUNK]).astype(BF16)
                    return dmix_acc, dsc_acc

                dmix_acc, dsc_acc = lax.fori_loop(
                    0, n_chunks, chunk, (jnp.zeros((POOL_G, POOL_G), F32), jnp.zeros((8, POOL_G), F32)))
                dmix_ref[...] = dmix_acc
                dsc_ref[...] = jnp.sum(dsc_acc, axis=0, keepdims=True)

    slab = pl.BlockSpec((S_, POOL_G), lambda g: (0, g))
    mix_spec = pl.BlockSpec((None, POOL_G, POOL_G), lambda g: (g, 0, 0))
    vec = pl.BlockSpec((1, POOL_G), lambda g: (0, g))
    return _call(
        _drop_ref(body, 4), [z, dy, pool_mix, pool_scale, dz], [slab, slab, mix_spec, vec, ANY], after, name=name,
        grid=(len(POOL_WINDOWS),), out_specs=(slab, mix_spec, vec),
        out_shape=(jax.ShapeDtypeStruct(dz.shape, dz.dtype), jax.ShapeDtypeStruct((4, POOL_G, POOL_G), F32),
                   jax.ShapeDtypeStruct((1, POOL_W), F32)),
        input_output_aliases={4: 0}, compiler_params=_cparams(("parallel",)))


def _conv_specs(S_):
    slab = lambda off: pl.BlockSpec((S_, 128), lambda c, off=off: (0, off // 128 + c))
    return slab(OFF_CB), slab(OFF_CC), slab(OFF_CX)


def _conv_fwd(name, z, conv_w, after=None):
    S_ = z.shape[0]
    n_chunks = S_ // SEQ_CHUNK
    col = slice(0, 128)

    def body(b_ref, c_ref, x_ref, w_ref, y_ref):
        w0, w1, w2 = w_ref[0:1, :], w_ref[1:2, :], w_ref[2:3, :]

        def chunk(i, carry):
            r0 = pl.multiple_of(i * SEQ_CHUNK, SEQ_CHUNK)
            u = _rows_with_halo(c_ref, col, i, n_chunks, True, False) * _rows_with_halo(x_ref, col, i, n_chunks, True, False)
            y = w2 * u + w1 * _shift_down(u, 1) + w0 * _shift_down(u, 2)
            y_ref[pl.ds(r0, SEQ_CHUNK), :] = (b_ref[pl.ds(r0, SEQ_CHUNK), :] * y[HALO:]).astype(BF16)
            return carry

        lax.fori_loop(0, n_chunks, chunk, 0)

    sb, sc, sx = _conv_specs(S_)
    return _call(
        body, [z, z, z, conv_w], [sb, sc, sx, pl.BlockSpec((3, 128), lambda c: (0, c))], after,
        name=name, grid=(CONV_W // 128,),
        out_specs=pl.BlockSpec((S_, 128), lambda c: (0, c)),
        out_shape=jax.ShapeDtypeStruct((S_, CONV_W), BF16),
        compiler_params=_cparams(("parallel",)))


def _conv_bwd(name, z, dy, conv_w, after=None):
    S_ = z.shape[0]
    n_chunks = S_ // SEQ_CHUNK
    col = slice(0, 128)
    lo, hi = HALO, HALO + SEQ_CHUNK

    def body(b_ref, c_ref, x_ref, dy_ref, w_ref, db_ref, dc_ref, dx_ref, dw_ref):
        w0, w1, w2 = w_ref[0:1, :], w_ref[1:2, :], w_ref[2:3, :]

        def chunk(i, carry):
            a0, a1, a2 = carry
            r0 = pl.multiple_of(i * SEQ_CHUNK, SEQ_CHUNK)
            cc = _rows_with_halo(c_ref, col, i, n_chunks, True, True)
            xx = _rows_with_halo(x_ref, col, i, n_chunks, True, True)
            bb = _rows_with_halo(b_ref, col, i, n_chunks, True, True)
            dyy = _rows_with_halo(dy_ref, col, i, n_chunks, True, True)
            u = cc * xx
            u1 = _shift_down(u, 1)
            u2 = _shift_down(u, 2)
            y = w2 * u + w1 * u1 + w0 * u2
            dyv = dyy * bb
            du = w2 * dyv + w1 * _shift_up(dyv, 1) + w0 * _shift_up(dyv, 2)
            db_ref[pl.ds(r0, SEQ_CHUNK), :] = (dyy[lo:hi] * y[lo:hi]).astype(BF16)
            dc_ref[pl.ds(r0, SEQ_CHUNK), :] = (du[lo:hi] * xx[lo:hi]).astype(BF16)
            dx_ref[pl.ds(r0, SEQ_CHUNK), :] = (du[lo:hi] * cc[lo:hi]).astype(BF16)
            red = lambda v: jnp.sum(v.reshape(SEQ_CHUNK // 8, 8, 128), axis=0)
            dm = dyv[lo:hi]
            return a0 + red(dm * u2[lo:hi]), a1 + red(dm * u1[lo:hi]), a2 + red(dm * u[lo:hi])

        zero = jnp.zeros((8, 128), F32)
        a0, a1, a2 = lax.fori_loop(0, n_chunks, chunk, (zero, zero, zero))
        dw_ref[0:1, :] = jnp.sum(a0, axis=0, keepdims=True)
        dw_ref[1:2, :] = jnp.sum(a1, axis=0, keepdims=True)
        dw_ref[2:3, :] = jnp.sum(a2, axis=0, keepdims=True)

    sb, sc, sx = _conv_specs(S_)
    slab = pl.BlockSpec((S_, 128), lambda c: (0, c))
    wspec = pl.BlockSpec((3, 128), lambda c: (0, c))
    act = jax.ShapeDtypeStruct((S_, CONV_W), BF16)
    return _call(
        body, [z, z, z, dy, conv_w], [sb, sc, sx, slab, wspec], after, name=name, grid=(CONV_W // 128,),
        out_specs=(slab, slab, slab, wspec),
        out_shape=(act, act, act, jax.ShapeDtypeStruct((3, CONV_W), F32)),
        compiler_params=_cparams(("parallel",)))


def _head_ones(pw):
    a = lax.broadcasted_iota(jnp.int32, (pw, pw), 0) // HEAD
    b = lax.broadcasted_iota(jnp.int32, (pw, pw), 1) // HEAD
    return (a == b).astype(BF16)


def _head_sum(v, ones):
    hi = v.astype(BF16)
    lo = (v - hi.astype(F32)).astype(BF16)
    return jnp.dot(hi, ones, preferred_element_type=F32) + jnp.dot(lo, ones, preferred_element_type=F32)


def _head_norm(v, gain, ones):
    rstd = lax.rsqrt(_head_sum(v * v, ones) * (1.0 / HEAD) + EPS)
    xhat = v * rstd
    return xhat * gain, xhat, rstd


def _head_norm_bwd(dy, xhat, rstd, gain, ones):
    dxhat = dy * gain
    c = _head_sum(dxhat * xhat, ones) * (1.0 / HEAD)
    dv = rstd * (dxhat - xhat * c)
    dg = jnp.sum((dy * xhat).reshape(dy.shape[0] // 8, 8, dy.shape[1]), axis=0)
    return dv, dg


def _head_masks(pw):
    lane_head = lax.broadcasted_iota(jnp.int32, (1, pw), 1) // HEAD
    return [lane_head == h for h in range(pw // HEAD)]


def _only(mask, v):
    return jnp.where(mask, v, jnp.zeros_like(v))


def _attn_specs(S_, g, dil):
    rows = ATTN_BLK * dil
    nb = S_ // rows
    pw = 128 if dil > 1 else ATTN_OUT
    cq, ck, cv = ((OFF_Q + g * ATTN_OUT) // pw, (OFF_K + g * ATTN_OUT) // pw, (OFF_V + g * ATTN_OUT) // pw)
    return rows, nb, pw, pw // HEAD, ATTN_OUT // pw, cq, ck, cv


ATTN_BATCH = 4


def _attn_group(dil):
    return 4 if dil == 1 else 1


def _attn_block_specs(rows, grp, pw, last=None):
    step = (lambda n: n) if last is None else (lambda n: jnp.minimum(n, last))
    cur = lambda c: pl.BlockSpec((rows * grp, pw), lambda hp, n, c=c: (step(n), c + hp))
    prev = lambda c: pl.BlockSpec((rows, pw), lambda hp, n, c=c: (jnp.maximum(step(n) * grp - 1, 0), c + hp))
    return cur, prev


def _band_mask(has_prev):
    qi = lax.broadcasted_iota(jnp.int32, (ATTN_BLK, 2 * ATTN_BLK), 0)
    ki = lax.broadcasted_iota(jnp.int32, (ATTN_BLK, 2 * ATTN_BLK), 1)
    in_prev = jnp.logical_and(ki < ATTN_BLK, ki >= qi)
    if has_prev is not True:
        in_prev = jnp.logical_and(in_prev, has_prev)
    return jnp.logical_or(in_prev, jnp.logical_and(ki >= ATTN_BLK, ki - ATTN_BLK <= qi))


def _rows_of(ref, r, dil):
    if dil == 1:
        return ref[r * ATTN_BLK:(r + 1) * ATTN_BLK, :]
    return ref[pl.ds(r, ATTN_BLK, stride=dil), :]


def _put_rows(ref, r, dil, val):
    if dil == 1:
        ref[r * ATTN_BLK:(r + 1) * ATTN_BLK, :] = val.astype(ref.dtype)
    else:
        ref[pl.ds(r, ATTN_BLK, stride=dil), :] = val.astype(ref.dtype)


def _attn_fwd(name, z, q_gain, k_gain, g, dil):
    S_ = z.shape[0]
    rows, nb, pw, heads, npairs, cq, ck, cv = _attn_specs(S_, g, dil)
    scale = HEAD ** -0.5

    grp = _attn_group(dil)
    nsteps = nb // grp

    def body(q_ref, k_ref, kp_ref, v_ref, vp_ref, gq_ref, gk_ref, o_ref, l_ref):
        n = pl.program_id(1)
        ones, hmask = _head_ones(pw), _head_masks(pw)
        gq, gk = jnp.tile(gq_ref[...], (1, heads)), jnp.tile(gk_ref[...], (1, heads))
        mask_first, mask_rest = _band_mask(n > 0), _band_mask(True)
        for r0 in range(0, dil * grp, ATTN_BATCH):
            rs = range(r0, min(r0 + ATTN_BATCH, dil * grp))
            qn, kn, vv, s, p = {}, {}, {}, {}, {}
            kcn = {}
            for r in rs:
                q, kc, vc = _rows_of(q_ref, r, dil), _rows_of(k_ref, r, dil), _rows_of(v_ref, r, dil)
                kcn[r] = _head_norm(kc, gk, ones)[0]
                if dil == 1 and r > 0:
                    kpn = kcn[r - 1] if r - 1 in kcn else _head_norm(_rows_of(k_ref, r - 1, dil), gk, ones)[0]
                    vp = _rows_of(v_ref, r - 1, dil)
                else:
                    kpn, vp = _head_norm(_rows_of(kp_ref, r, dil), gk, ones)[0], _rows_of(vp_ref, r, dil)
                qn[r] = _head_norm(q, gq, ones)[0].astype(BF16)
                kn[r] = jnp.concatenate([kpn, kcn[r]], axis=0).astype(BF16)
                vv[r] = jnp.concatenate([vp, vc], axis=0).astype(BF16)
            keys = [(r, h) for r in rs for h in range(heads)]
            for r, h in keys:
                s[r, h] = lax.dot_general(_only(hmask[h], qn[r]), kn[r], _DIMS["nt"],
                                          preferred_element_type=F32) * scale
            lse, den = {}, {}
            for key in keys:
                mask = mask_rest if (dil == 1 and key[0] > 0) else mask_first
                sm = jnp.where(mask, s[key], MASK)
                m = jnp.max(sm, axis=-1, keepdims=True)
                e = jnp.exp(sm - m)
                den[key] = jnp.sum(e, axis=-1, keepdims=True)
                p[key] = e.astype(BF16)
                lse[key] = m + jnp.log(den[key])
            for r in rs:
                out = jnp.zeros((ATTN_BLK, pw), F32)
                lse_all = jnp.zeros((ATTN_BLK, pw), F32)
                for h in range(heads):
                    out = jnp.where(hmask[h], jnp.dot(p[r, h], vv[r], preferred_element_type=F32) / den[r, h], out)
                    lse_all = jnp.where(hmask[h], lse[r, h], lse_all)
                _put_rows(o_ref, r, dil, out)
                _put_rows(l_ref, r, dil, lse_all)

    cur, prev = _attn_block_specs(rows, grp, pw)
    gspec = pl.BlockSpec((1, HEAD), lambda hp, n: (0, 0))
    shp = jax.ShapeDtypeStruct((S_, ATTN_OUT), F32)
    return pl.pallas_call(
        body, name=name, grid=(npairs, nsteps),
        in_specs=[cur(cq), cur(ck), prev(ck), cur(cv), prev(cv), gspec, gspec],
        out_specs=(cur(0), cur(0)), out_shape=(shp, shp),
        compiler_params=_cparams(("parallel", "parallel")),
    )(z, z, z, z, z, q_gain, k_gain)


def _attn_combine(name, os_, ls_):
    S_ = os_[0].shape[0]

    def body(o0, o1, o2, l0, l1, l2, o_ref, l_ref):
        a, b, c = l0[...], l1[...], l2[...]
        m = jnp.maximum(jnp.maximum(a, b), c)
        ea, eb, ec = jnp.exp(a - m), jnp.exp(b - m), jnp.exp(c - m)
        zsum = ea + eb + ec
        o_ref[...] = (ea * o0[...] + eb * o1[...] + ec * o2[...]) / zsum
        l_ref[...] = m + jnp.log(zsum)

    row = pl.BlockSpec((ROW_TILE, ATTN_OUT), lambda i: (i, 0))
    shp = jax.ShapeDtypeStruct((S_, ATTN_OUT), F32)
    return pl.pallas_call(
        body, name=name, grid=(S_ // ROW_TILE,), in_specs=[row] * 6, out_specs=(row, row), out_shape=(shp, shp),
        compiler_params=_cparams(("parallel",)),
    )(*os_, *ls_)


def _attn_bwd(name, z, q_gain, k_gain, do, o, lse, g, dil, after=None):
    S_ = z.shape[0]
    rows, nb, pw, heads, npairs, cq, ck, cv = _attn_specs(S_, g, dil)
    scale = HEAD ** -0.5

    def body(q_ref, kc_ref, kp_ref, vc_ref, vp_ref, gq_ref, gk_ref, do_ref, o_ref, l_ref,
             dq_ref, dk_ref, dv_ref, dgq_ref, dgk_ref, ck_ref, cvv_ref, gq_acc, gk_acc):
        hp = pl.program_id(0)
        n = pl.program_id(1)
        live = n < nb
        mask = jnp.logical_and(_band_mask(n > 0), live)
        ones, hmask = _head_ones(pw), _head_masks(pw)
        gq, gk = jnp.tile(gq_ref[...], (1, heads)), jnp.tile(gk_ref[...], (1, heads))

        @pl.when(n == 0)
        def _():
            ck_ref[...] = jnp.zeros_like(ck_ref)
            cvv_ref[...] = jnp.zeros_like(cvv_ref)

        @pl.when(jnp.logical_and(n == 0, hp == 0))
        def _():
            gq_acc[...] = jnp.zeros_like(gq_acc)
            gk_acc[...] = jnp.zeros_like(gk_acc)

        dgq = jnp.zeros((8, pw), F32)
        dgk = jnp.zeros((8, pw), F32)
        for r0 in range(0, dil, ATTN_BATCH):
            rs = range(r0, min(r0 + ATTN_BATCH, dil))
            keys = [(r, h) for r in rs for h in range(heads)]
            qn, qhat, qrstd, kn, kphat, kprstd, vv, dob, delta, lse = ({} for _ in range(10))
            for r in rs:
                q, kc, kp = _rows_of(q_ref, r, dil), _rows_of(kc_ref, r, dil), _rows_of(kp_ref, r, dil)
                dov = _rows_of(do_ref, r, dil)
                qn_f, qhat[r], qrstd[r] = _head_norm(q, gq, ones)
                kpn, kphat[r], kprstd[r] = _head_norm(kp, gk, ones)
                qn[r] = qn_f.astype(BF16)
                kn[r] = jnp.concatenate([kpn, _head_norm(kc, gk, ones)[0]], axis=0).astype(BF16)
                vv[r] = jnp.concatenate([_rows_of(vp_ref, r, dil), _rows_of(vc_ref, r, dil)], axis=0).astype(BF16)
                dob[r] = dov.astype(BF16)
                delta[r] = _head_sum(dov * _rows_of(o_ref, r, dil), ones)
                lse[r] = _rows_of(l_ref, r, dil)
            s, dp = {}, {}
            for r, h in keys:
                s[r, h] = lax.dot_general(_only(hmask[h], qn[r]), kn[r], _DIMS["nt"],
                                          preferred_element_type=F32) * scale
                dp[r, h] = lax.dot_general(_only(hmask[h], dob[r]), vv[r], _DIMS["nt"], preferred_element_type=F32)
            p, ds = {}, {}
            for r, h in keys:
                col = slice(h * HEAD, h * HEAD + 1)
                pk = jnp.where(mask, jnp.exp(jnp.where(mask, s[r, h], MASK) - lse[r][:, col]), 0.0)
                ds[r, h] = (pk * (dp[r, h] - delta[r][:, col]) * scale).astype(BF16)
                p[r, h] = pk.astype(BF16)
            dqn, dkn, dvv = {}, {}, {}
            for r in rs:
                dqn[r] = jnp.zeros((ATTN_BLK, pw), F32)
                dkn[r] = jnp.zeros((2 * ATTN_BLK, pw), F32)
                dvv[r] = jnp.zeros((2 * ATTN_BLK, pw), F32)
                for h in range(heads):
                    dqn[r] = jnp.where(hmask[h], jnp.dot(ds[r, h], kn[r], preferred_element_type=F32), dqn[r])
                    dkn[r] = jnp.where(hmask[h], lax.dot_general(ds[r, h], qn[r], _DIMS["tn"],
                                                                 preferred_element_type=F32), dkn[r])
                    dvv[r] = jnp.where(hmask[h], lax.dot_general(p[r, h], dob[r], _DIMS["tn"],
                                                                 preferred_element_type=F32), dvv[r])
            for r in rs:
                dq_all, dg = _head_norm_bwd(dqn[r], qhat[r], qrstd[r], gq, ones)
                dgq = dgq + dg
                dk_all, dg = _head_norm_bwd(_rows_of(ck_ref, r, dil) + dkn[r][:ATTN_BLK], kphat[r], kprstd[r], gk, ones)
                dgk = dgk + dg
                dv_all = _rows_of(cvv_ref, r, dil) + dvv[r][:ATTN_BLK]

                @pl.when(live)
                def _(dq_all=dq_all, r=r):
                    _put_rows(dq_ref, r, dil, dq_all)

                _put_rows(dk_ref, r, dil, dk_all)
                _put_rows(dv_ref, r, dil, dv_all)
                _put_rows(ck_ref, r, dil, dkn[r][ATTN_BLK:])
                _put_rows(cvv_ref, r, dil, dvv[r][ATTN_BLK:])
        gq_acc[...] += dgq
        gk_acc[...] += dgk

        @pl.when(jnp.logical_and(n == nb, hp == npairs - 1))
        def _():
            fold = lambda a: sum(a[:, h * HEAD:(h + 1) * HEAD] for h in range(heads))
            dgq_ref[...] = jnp.sum(fold(gq_acc[...]), axis=0, keepdims=True)
            dgk_ref[...] = jnp.sum(fold(gk_acc[...]), axis=0, keepdims=True)

    last = nb - 1
    cur = lambda c: pl.BlockSpec((rows, pw), lambda hp, n, c=c: (jnp.minimum(n, last), c + hp))
    prev = lambda c: pl.BlockSpec((rows, pw), lambda hp, n, c=c: (jnp.maximum(n - 1, 0), c + hp))
    gspec = pl.BlockSpec((1, HEAD), lambda hp, n: (0, 0))
    act = jax.ShapeDtypeStruct((S_, ATTN_OUT), F32)
    vec = jax.ShapeDtypeStruct((1, HEAD), F32)
    return _call(
        body, [z, z, z, z, z, q_gain, k_gain, do, o, lse],
        [cur(cq), cur(ck), prev(ck), cur(cv), prev(cv), gspec, gspec, cur(0), cur(0), cur(0)], after,
        name=name, grid=(npairs, nb + 1),
        out_specs=(cur(0), prev(0), prev(0), gspec, gspec),
        out_shape=(act, act, act, vec, vec),
        scratch_shapes=[pltpu.VMEM((rows, pw), F32), pltpu.VMEM((rows, pw), F32),
                        pltpu.VMEM((8, pw), F32), pltpu.VMEM((8, pw), F32)],
        compiler_params=_cparams(("arbitrary", "arbitrary")))


MIX_TN = 256
MIX_TM = 2048


def _sigmoid(v):
    return 1.0 / (1.0 + jnp.exp(-v))


def _mix_fwd(name, z, b_gate, ys, ws):
    S_ = z.shape[0]
    tm, tn = MIX_TM, MIX_TN
    gblk = OFF_GATE // tn

    def body(yp, yc, ya, wp, wc, wa, g0, g1, g2, b0, b1, b2, m_ref):
        acc = None
        for y_ref, w_ref, g_ref, b_ref in ((yp, wp, g0, b0), (yc, wc, g1, b1), (ya, wa, g2, b2)):
            u = lax.dot_general(y_ref[...].astype(BF16), w_ref[...], _DIMS["nt"], preferred_element_type=F32)
            t = _sigmoid(g_ref[...] + b_ref[...]) * u
            acc = t if acc is None else acc + t
        m_ref[...] = acc.astype(BF16)

    yspec = lambda w: pl.BlockSpec((tm, w), lambda i, j: (i, 0))
    wspec = lambda w: pl.BlockSpec((tn, w), lambda i, j: (j, 0))
    gspec = lambda b: pl.BlockSpec((tm, tn), lambda i, j, b=b: (i, gblk + b * (D // tn) + j))
    bspec = lambda b: pl.BlockSpec((1, tn), lambda i, j, b=b: (0, b * (D // tn) + j))
    return pl.pallas_call(
        body, name=name, grid=(S_ // tm, D // tn),
        in_specs=[yspec(POOL_W), yspec(CONV_W), yspec(ATTN_OUT), wspec(POOL_W), wspec(CONV_W), wspec(ATTN_OUT),
                  gspec(0), gspec(1), gspec(2), bspec(0), bspec(1), bspec(2)],
        out_specs=pl.BlockSpec((tm, tn), lambda i, j: (i, j)),
        out_shape=jax.ShapeDtypeStruct((S_, D), BF16),
        compiler_params=_cparams(("parallel", "parallel")),
    )(*ys, *ws, z, z, z, b_gate, b_gate, b_gate)


def _mix_bwd(name, z, b_gate, y, w, dmerged, branch, dz, after=None):
    S_ = z.shape[0]
    tm, tn = MIX_TM, MIX_TN
    width = y.shape[1]
    gblk = OFF_GATE // tn + branch * (D // tn)
    nj = D // tn

    def body(y_ref, w_ref, g_ref, b_ref, dm_ref, dy_ref, dw_ref, dg_ref, db_ref, acc_ref):
        i, j = pl.program_id(0), pl.program_id(1)
        yb = y_ref[...].astype(BF16)
        u = lax.dot_general(yb, w_ref[...], _DIMS["nt"], preferred_element_type=F32)
        sg = _sigmoid(g_ref[...] + b_ref[...])
        dm = dm_ref[...]
        du = (sg * dm).astype(BF16)
        dpre = dm * u * sg * (1.0 - sg)
        dg_ref[...] = dpre.astype(BF16)
        cols = pl.ds(pl.multiple_of(j * tn, tn), tn)
        db_ref[...] = _rows8(dpre)
        d_w = lax.dot_general(du, yb, _DIMS["tn"], preferred_element_type=F32)
        d_y = jnp.dot(du, w_ref[...], preferred_element_type=F32)

        @pl.when(i == 0)
        def _():
            dw_ref[cols, :] = d_w

        @pl.when(i > 0)
        def _():
            dw_ref[cols, :] += d_w

        @pl.when(j == 0)
        def _():
            acc_ref[...] = d_y

        @pl.when(j > 0)
        def _():
            acc_ref[...] += d_y

        @pl.when(j == nj - 1)
        def _():
            dy_ref[...] = acc_ref[...]

    rows = pl.BlockSpec((tm, width), lambda i, j: (i, 0))
    blk = pl.BlockSpec((tm, tn), lambda i, j: (i, j))
    gate = pl.BlockSpec((tm, tn), lambda i, j: (i, gblk + j))
    args = [y, w, z, b_gate, dmerged]
    in_specs = [rows, pl.BlockSpec((tn, width), lambda i, j: (j, 0)), gate,
                pl.BlockSpec((1, tn), lambda i, j: (0, branch * nj + j)), blk]
    aliases = {}
    if not isinstance(dz, jax.ShapeDtypeStruct):
        body = _drop_ref(body, len(args))
        aliases = {len(args): 2}
        args.append(dz)
        in_specs.append(ANY)
    return _call(
        body, args, in_specs, after, name=name, grid=(S_ // tm, nj),
        out_specs=(rows, pl.BlockSpec((D, width), lambda i, j: (0, 0)), gate, pl.BlockSpec((8, tn), lambda i, j: (i, j))),
        out_shape=(jax.ShapeDtypeStruct((S_, width), F32), jax.ShapeDtypeStruct((D, width), F32),
                   jax.ShapeDtypeStruct(dz.shape, dz.dtype), jax.ShapeDtypeStruct((S_ // tm * 8, D), F32)),
        scratch_shapes=[pltpu.VMEM((tm, width), F32)], input_output_aliases=aliases,
        compiler_params=_cparams(("arbitrary", "arbitrary")))


def _relu2_epilogue(acc):
    r = jnp.maximum(acc, 0.0)
    return (r * r,)


def _relu2_bwd_epilogue(acc, r):
    return (acc * (2.0 * jnp.sqrt(r.astype(F32))),)


def _residual_norm_epilogue(acc, xr, gain):
    x = xr + acc
    ms = jnp.mean(x * x, axis=-1, keepdims=True)
    return x, x * lax.rsqrt(ms + EPS) * gain


def _fold_rows(name, part, lanes=False, scale=1.0):
    R, N = part.shape

    def body(p_ref, o_ref):
        s = jnp.sum(p_ref[...], axis=0, keepdims=True)
        if lanes:
            s = jnp.sum(s, axis=1, keepdims=True)
        o_ref[...] = s * scale

    out_n = 1 if lanes else N
    return pl.pallas_call(
        body, name=name, grid=(1,), in_specs=[pl.BlockSpec((R, N), lambda i: (0, 0))],
        out_specs=pl.BlockSpec((1, out_n), lambda i: (0, 0)), out_shape=jax.ShapeDtypeStruct((1, out_n), F32))(part)


def _rows8(v):
    return jnp.sum(v.reshape(v.shape[0] // 8, 8, v.shape[1]), axis=0)


def _residual_loss_epilogue(acc, xr, target):
    e = xr + acc - target
    return e * (1.0 / D), _rows8(e * e)


def _norm_bwd_epilogue(acc, x, dres, gain):
    rstd = lax.rsqrt(jnp.mean(x * x, axis=-1, keepdims=True) + EPS)
    xhat = x * rstd
    dxhat = acc * gain
    c = jnp.mean(dxhat * xhat, axis=-1, keepdims=True)
    return dres + rstd * (dxhat - xhat * c), _rows8(acc * xhat)


def _layer_fwd(l, x, h, p, next_gain, target=None, early=None):
    t = f"l{l}_"
    z = _matmul(t + "in_proj", h, p["w_in_t"], "nt", 512, 3712, 1024, after=early)
    y_pool = _pool_fwd(t + "pool", z, p["pool_mix"], p["pool_scale"])
    os_, ls_ = [], []
    for g, (_, dil) in enumerate(ATTN_GROUPS):
        o_g, l_g = _attn_fwd(t + f"attn{g}", z, p["q_gain"], p["k_gain"], g, dil)
        os_.append(o_g)
        ls_.append(l_g)
    y_attn, lse = _attn_combine(t + "attn_mix", os_, ls_)
    y_conv = _conv_fwd(t + "conv", z, p["conv_w"], after=y_attn)
    merged = _mix_fwd(t + "merge", z, p["b_gate"], (y_pool, y_conv, y_attn),
                      (p["w_pool_up_t"], p["w_conv_out_t"], p["w_attn_up_t"]))
    x1, h2 = _matmul(t + "out_proj", merged, p["w_o"], "nn", 1024, D, 1024, out_dtypes=(F32, BF16),
                     extras=((x, "mn"), (p["norm_mlp"], "n")), epilogue=_residual_norm_epilogue)
    r = _matmul(t + "ff1", h2, p["w_ff1_t"], "nt", 1024, 1024, 1024, out_dtypes=(BF16,), epilogue=_relu2_epilogue)
    if next_gain is None:
        x2, h_out = _matmul(t + "ff2", r, p["w_ff2"], "nn", 512, D, D_FF, out_dtypes=(F32, (F32, "rows8")),
                            extras=((x1, "mn"), (target, "mn")), epilogue=_residual_loss_epilogue)
    else:
        x2, h_out = _matmul(t + "ff2", r, p["w_ff2"], "nn", 512, D, D_FF, out_dtypes=(F32, BF16),
                            extras=((x1, "mn"), (next_gain, "n")), epilogue=_residual_norm_epilogue)
    saved = dict(x=x, h=h, z=z, y_pool=y_pool, y_conv=y_conv, y_attn=y_attn, lse=lse, merged=merged,
                 x1=x1, h2=h2, r=r)
    return x2, h_out, saved


def _layer_bwd(l, dx2, p, s, pending, collective_id, last, start_after=None, small_fn=None):
    t = f"l{l}_b_"
    g = {}
    rest = jax.ShapeDtypeStruct((N_DEV, REST_ROWS, D), F32)
    da = _matmul(t + "d_ff2_in", dx2, p["w_ff2"], "nt", 1024, 1024, 1024, out_dtypes=(BF16,),
                 extras=((s["r"], "mn"),), epilogue=_relu2_bwd_epilogue,
                 after=start_after)
    rest = _matmul(t + "dw_ff2", s["r"], dx2, "tn", 512, 1024, 4096, into=(rest, REST_FF2, 0), after=da)
    tok = rest if pending is None else pending.chip_sums(after=rest)
    dx1, part = _matmul(t + "d_ff1_in", da, p["w_ff1_t"], "nn", 512, D, D_FF, out_dtypes=(F32, (F32, "rows8")),
                        extras=((s["x1"], "mn"), (dx2, "mn"), (p["norm_mlp"], "n")), epilogue=_norm_bwd_epilogue,
                        after=tok)
    g["norm_mlp"] = _fold_rows(t + "d_norm_mlp", part)
    rest = _matmul(t + "dw_ff1", da, s["h2"], "tn", 512, 1024, 4096, into=(rest, REST_FF1, 0), after=dx1)
    dmerged = _matmul(t + "d_out_proj_in", dx1, p["w_o"], "nt", 1024, 1024, 1024)
    rest = _matmul(t + "dw_o", s["merged"], dx1, "tn", 128, 1024, 4096, into=(rest, REST_WO, 0), after=dmerged)
    ys = (s["y_pool"], s["y_conv"], s["y_attn"])
    names = ("w_pool_up_t", "w_conv_out_t", "w_attn_up_t")
    dys, dbs = [], []
    tok = rest
    dz = jax.ShapeDtypeStruct((dx2.shape[0], IN_COLS), BF16)
    for b in range(3):
        dy_b, dw_t, dz, db = _mix_bwd(t + f"merge{b}", s["z"], p["b_gate"], ys[b], p[names[b]], dmerged, b, dz,
                                      after=tok)
        tok = dy_b
        width = ys[b].shape[1]
        if b < 2:
            rest = rest.at[:, REST_UP:REST_ATTN, b * width:(b + 1) * width].set(
                dw_t.reshape(N_DEV, REST_ATTN - REST_UP, width))
        else:
            rest = rest.at[:, REST_ATTN:REST_ROWS, :].set(dw_t.reshape(N_DEV, REST_ROWS - REST_ATTN, D))
        dys.append(dy_b)
        dbs.append(_fold_rows(t + f"d_b_gate{b}", db))
    g["b_gate"] = jnp.concatenate(dbs, axis=1)
    rs_rest = _ReduceScatter(f"rs_rest{l}", collective_id, rest)
    pending_sum = None if pending is None else pending.finish(after=rest)
    dz, g["pool_mix"], g["pool_scale"] = _pool_bwd(t + "pool", s["z"], dys[0], p["pool_mix"], p["pool_scale"], dz,
                                                   after=rest if pending is None else pending_sum[0])
    dcb, dcc, dcx, g["conv_w"] = _conv_bwd(t + "conv", s["z"], dys[1], p["conv_w"], after=dz)
    tok = rs_rest.chip_sums(after=dcb)
    dqs, dks, dvs = [], [], []
    gq = gk = None
    for gi, (_, dil) in enumerate(ATTN_GROUPS):
        dq, dk, dv, dgq, dgk = _attn_bwd(t + f"attn{gi}", s["z"], p["q_gain"], p["k_gain"], dys[2], s["y_attn"],
                                         s["lse"], gi, dil, after=tok)
        tok = dq
        dqs.append(dq)
        dks.append(dk)
        dvs.append(dv)
        gq = dgq if gq is None else gq + dgq
        gk = dgk if gk is None else gk + dgk
    g["q_gain"], g["k_gain"] = gq, gk
    rest_sum, _ = rs_rest.finish(after=tok)
    col = OFF_CB
    for piece in [dcb, dcc, dcx] + dqs + dks + dvs:
        dz = lax.dynamic_update_slice(dz, piece.astype(BF16), (0, col))
        col += piece.shape[1]
    in_t = _matmul(t + "dw_in", dz, s["h"], "tn", 256, 1024, 4096, after=rest_sum)
    rs_in = _ReduceScatter(f"rs_in{l}", collective_id + 2, in_t.reshape(N_DEV, IN_ROWS, D))
    dx, part = _matmul(t + "d_in_proj_in", dz, p["w_in_t"], "nn", 256, D, IN_COLS, out_dtypes=(F32, (F32, "rows8")),
                       extras=((s["x"], "mn"), (dx1, "mn"), (p["norm_mix"], "n")), epilogue=_norm_bwd_epilogue,
                       after=in_t)
    g["norm_mix"] = _fold_rows(t + "d_norm_mix", part)
    if last:
        rs_in.chip_sums(after=dx)
    return dx, g, rest_sum, pending_sum, rs_in


ANY = pl.BlockSpec(memory_space=pl.ANY)


def _mesh_pos():
    return lax.axis_index("x"), lax.axis_index("y"), lax.axis_index("c")


def _other_chips(x, y):
    return [(1 - x, y), (x, 1 - y), (1 - x, 1 - y)]


def _comm_call(name, collective_id, peers, body, arrs, out_shape, sem_counts, after=None):
    n_in, n_out = len(arrs), len(out_shape)
    if collective_id is None:
        def tc_body(*refs):
            body(refs[:n_in], refs[n_in:n_in + n_out], *refs[n_in + n_out:])

        return pl.pallas_call(
            tc_body, name=name, out_shape=tuple(out_shape), in_specs=[ANY] * n_in, out_specs=(ANY,) * n_out,
            scratch_shapes=[pltpu.SemaphoreType.DMA((n,)) for n in sem_counts])(*arrs)

    n_after = 0 if after is None else 1

    def seq_body(*refs):
        barrier = pltpu.get_barrier_semaphore()
        ps = peers(*_mesh_pos())
        for p in ps:
            pl.semaphore_signal(barrier, inc=1, device_id=p, device_id_type=MESH)
        pl.semaphore_wait(barrier, len(ps))
        outs = refs[n_in + n_after:]
        body(refs[:n_in], outs[:n_out], *outs[n_out:])

    return pl.kernel(
        seq_body, out_type=tuple(out_shape), mesh=plsc.ScalarSubcoreMesh(axis_name="seq", num_cores=1), name=name,
        scratch_types=[pltpu.SemaphoreType.DMA((n,)) for n in sem_counts],
        compiler_params=pltpu.CompilerParams(collective_id=collective_id),
    )(*arrs, *([] if after is None else [after]))


def _ordered(name, x, after):
    def body(x_ref, after_ref, o_ref):
        del x_ref, after_ref, o_ref

    return pl.pallas_call(
        body, name=name, out_shape=jax.ShapeDtypeStruct(x.shape, x.dtype), in_specs=[ANY, ANY], out_specs=ANY,
        input_output_aliases={0: 0})(x, after)


def _all_gather(name, collective_id, shard, after=None):
    R, C = shard.shape

    def peers(x, y, c):
        return [(x, y, 1 - c)] + [(*chip, c) for chip in _other_chips(x, y)]

    def body(in_refs, out_refs, send_sems, recv_sems, local_sems):
        (x_ref,), (out_ref,) = in_refs, out_refs
        x, y, c = _mesh_pos()
        me, sibling = (x, y, c), (x, y, 1 - c)
        chips = _other_chips(x, y)

        def slot(px, py, pc):
            return out_ref.at[4 * px + 2 * py + pc]

        def copy(k, block, to, src=None):
            return pltpu.make_async_remote_copy(
                src_ref=slot(*block) if src is None else src, dst_ref=slot(*block),
                send_sem=send_sems.at[k], recv_sem=recv_sems.at[k], device_id=to, device_id_type=MESH)

        mine = pltpu.make_async_copy(x_ref, slot(*me), local_sems.at[0])
        mine.start()
        first = [copy(0, me, sibling, src=x_ref)]
        first += [copy(1 + j, me, (*chip, c), src=x_ref) for j, chip in enumerate(chips)]
        for cp in first:
            cp.start()
        passed = [copy(4 + j, (*chip, c), sibling) for j, chip in enumerate(chips)]
        for j, chip in enumerate(chips):
            copy(1 + j, (*chip, c), me).wait_recv()
            passed[j].start()
        copy(0, sibling, me).wait_recv()
        for j, chip in enumerate(chips):
            copy(4 + j, (*chip, 1 - c), me).wait_recv()
        for cp in first + passed:
            cp.wait_send()
        mine.wait()

    return _comm_call(name, collective_id, peers, body, [shard],
                      [jax.ShapeDtypeStruct((N_DEV, R, C), shard.dtype)], (7, 7, 1), after)[0]


def _rs_sibling_exchange(name, collective_id, arrs):
    n = len(arrs)

    def body(in_refs, out_refs, send_sems, recv_sems):
        x, y, c = _mesh_pos()
        cps = []
        for k, (src, dst) in enumerate(zip(in_refs, out_refs)):
            src = src.at[:, 1 - c] if len(src.shape) == 4 else src
            cps.append(pltpu.make_async_remote_copy(src_ref=src, dst_ref=dst, send_sem=send_sems.at[k],
                                                    recv_sem=recv_sems.at[k], device_id=(x, y, 1 - c),
                                                    device_id_type=MESH))
        for cp in cps:
            cp.start()
        for cp in cps:
            cp.wait()

    out_shape = [jax.ShapeDtypeStruct(a.shape[:1] + a.shape[2:] if a.ndim == 4 else a.shape, a.dtype) for a in arrs]
    return _comm_call(name, collective_id, lambda x, y, c: [(x, y, 1 - c)], body, arrs, out_shape, (n, n))


def _rs_rows(r):
    return r // 2 if (r // 2) % 16 == 0 else r


def _drop_ref(body, idx):
    def wrapped(*refs):
        return body(*refs[:idx], *refs[idx + 1:])

    return wrapped


def _rs_chip_sum(name, ids, big, rbig, after=None):
    _, _, R, C = big.shape
    rows = _rs_rows(R)

    def body(ids_ref, mine_ref, theirs_ref, t16_ref, own_ref):
        p = pl.program_id(1)
        s = mine_ref[...] + theirs_ref[...]
        t16_ref[...] = s.astype(BF16)

        @pl.when(p == ids_ref[1])
        def _():
            own_ref[...] = s

    in_specs = [pl.BlockSpec((None, None, rows, C), lambda i, p, ids: (p, ids[0], i, 0)),
                pl.BlockSpec((None, rows, C), lambda i, p, ids: (p, i, 0))]
    args = [ids, big, rbig]
    if after is not None:
        body = _drop_ref(body, len(args))
        in_specs.append(ANY)
        args.append(after)
    grid_spec = pltpu.PrefetchScalarGridSpec(
        num_scalar_prefetch=1, grid=(R // rows, 4), in_specs=in_specs,
        out_specs=(pl.BlockSpec((None, rows, C), lambda i, p, ids: (p, i, 0)),
                   pl.BlockSpec((rows, C), lambda i, p, ids: (i, 0))))
    return pl.pallas_call(
        body, name=name, grid_spec=grid_spec,
        out_shape=(jax.ShapeDtypeStruct((4, R, C), BF16), jax.ShapeDtypeStruct((R, C), F32)),
        compiler_params=_cparams(("parallel", "arbitrary")),
    )(*args)


def _add2(name, a, b):
    def body(a_ref, b_ref, o_ref):
        o_ref[...] = a_ref[...] + b_ref[...]

    full = pl.BlockSpec(a.shape, lambda i: (0, 0))
    return pl.pallas_call(body, name=name, grid=(1,), in_specs=[full, full], out_specs=full,
                          out_shape=jax.ShapeDtypeStruct(a.shape, a.dtype))(a, b)


def _rs_chip_exchange(name, collective_id, arrs):
    n = len(arrs)

    def body(in_refs, out_refs, send_sems, recv_sems, local_sems):
        x, y, c = _mesh_pos()
        p_me = 2 * x + y
        chips = _other_chips(x, y)

        def part(ref, p):
            return ref.at[p] if len(ref.shape) == 3 else ref

        local = [pltpu.make_async_copy(part(src, p_me), dst.at[p_me], local_sems.at[k])
                 for k, (src, dst) in enumerate(zip(in_refs, out_refs))]
        for cp in local:
            cp.start()
        sends = []
        for j, (px, py) in enumerate(chips):
            for k, (src, dst) in enumerate(zip(in_refs, out_refs)):
                sends.append(pltpu.make_async_remote_copy(
                    src_ref=part(src, 2 * px + py), dst_ref=dst.at[p_me], send_sem=send_sems.at[n * j + k],
                    recv_sem=recv_sems.at[n * j + k], device_id=(px, py, c), device_id_type=MESH))
        for cp in sends:
            cp.start()
        for j, (px, py) in enumerate(chips):
            for k, (src, dst) in enumerate(zip(in_refs, out_refs)):
                pltpu.make_async_remote_copy(
                    src_ref=part(src, p_me), dst_ref=dst.at[2 * px + py], send_sem=send_sems.at[n * j + k],
                    recv_sem=recv_sems.at[n * j + k], device_id=(px, py, c), device_id_type=MESH).wait_recv()
        for cp in sends:
            cp.wait_send()
        for cp in local:
            cp.wait()

    out_shape = [jax.ShapeDtypeStruct((4,) + a.shape[-2:], a.dtype) for a in arrs]
    return _comm_call(name, collective_id, lambda x, y, c: [(*chip, c) for chip in _other_chips(x, y)], body, arrs,
                      out_shape, (3 * n, 3 * n, n))


def _rs_final_sum(name, ids, recv, own, rows, after=None):
    _, R, C = recv.shape
    assert R % rows == 0

    def body(ids_ref, r_ref, own_ref, o_ref):
        acc = None
        for p in range(4):
            term = jnp.where(ids_ref[1] == p, own_ref[...], r_ref[p].astype(F32))
            acc = term if acc is None else acc + term
        o_ref[...] = acc

    in_specs = [pl.BlockSpec((4, rows, C), lambda i, ids: (0, i, 0)), pl.BlockSpec((rows, C), lambda i, ids: (i, 0))]
    args = [ids, recv, own]
    if after is not None:
        body = _drop_ref(body, len(args))
        in_specs.append(ANY)
        args.append(after)
    grid_spec = pltpu.PrefetchScalarGridSpec(
        num_scalar_prefetch=1, grid=(R // rows,), in_specs=in_specs,
        out_specs=pl.BlockSpec((rows, C), lambda i, ids: (i, 0)))
    return pl.pallas_call(
        body, name=name, grid_spec=grid_spec, out_shape=jax.ShapeDtypeStruct((R, C), F32),
        compiler_params=_cparams(("parallel",)),
    )(*args)


class _ReduceScatter:
    def __init__(self, tag, collective_id, big, small=None):
        x, y, c = _mesh_pos()
        self.tag, self.cid, self.small = tag, collective_id, small
        self.ids = jnp.stack([c, 2 * x + y]).astype(jnp.int32)
        self.big = big.reshape((4, 2) + big.shape[1:])
        self.got = _rs_sibling_exchange(tag + "_sibling", collective_id,
                                        [self.big] + ([] if small is None else [small]))

    def chip_sums(self, after=None):
        t16, self.own = _rs_chip_sum(self.tag + "_chip_sum", self.ids, self.big, self.got[0], after)
        arrs = [t16]
        if self.small is not None:
            self.ts = _add2(self.tag + "_chip_sum_small", self.small, self.got[1])
            arrs.append(self.ts)
        self.recv = _rs_chip_exchange(self.tag + "_chips", self.cid + 1, arrs)
        return t16

    def finish(self, after=None):
        out = _rs_final_sum(self.tag + "_final", self.ids, self.recv[0], self.own, _rs_rows(self.own.shape[0]), after)
        out_small = None
        if self.small is not None:
            out_small = _rs_final_sum(self.tag + "_final_small", self.ids, self.recv[1], self.ts, self.small.shape[0])
        return out, out_small


def _all_reduce_small(tag, small):
    x, y, c = _mesh_pos()
    ids = jnp.stack([c, 2 * x + y]).astype(jnp.int32)
    (theirs,) = _rs_sibling_exchange(tag + "_sibling", None, [small])
    ts = _add2(tag + "_chip_sum", small, theirs)
    (recv,) = _rs_chip_exchange(tag + "_chips", None, [ts])
    return _rs_final_sum(tag + "_final", ids, recv, ts, small.shape[0])


def _adamw(name, w, g, m, v):
    R, C = w.shape
    tiles = [t for t in range(8, 513, 8) if R % t == 0]
    tr = max(tiles) if tiles else R
    c1 = 1.0 - ADAM_B1 ** ADAM_STEP
    c2 = 1.0 - ADAM_B2 ** ADAM_STEP

    def body(w_ref, g_ref, m_ref, v_ref, d_ref, nm_ref, nv_ref):
        gv = g_ref[...]
        nm = ADAM_B1 * m_ref[...] + (1.0 - ADAM_B1) * gv
        nv = ADAM_B2 * v_ref[...] + (1.0 - ADAM_B2) * (gv * gv)
        d_ref[...] = -ADAM_LR * ((nm / c1) / (jnp.sqrt(nv / c2) + ADAM_EPS) + ADAM_WD * w_ref[...])
        nm_ref[...] = nm
        nv_ref[...] = nv

    blk = pl.BlockSpec((tr, C), lambda i: (i, 0))
    shp = jax.ShapeDtypeStruct((R, C), F32)
    return pl.pallas_call(
        body, name=name, grid=(R // tr,), in_specs=[blk] * 4, out_specs=(blk,) * 3, out_shape=(shp,) * 3,
        compiler_params=_cparams(("parallel",)),
    )(w, g, m, v)


def _adamw_t(name, w, g_t, m, v):
    L, R, C = w.shape
    tr = 256
    c1 = 1.0 - ADAM_B1 ** ADAM_STEP
    c2 = 1.0 - ADAM_B2 ** ADAM_STEP

    def body(w_ref, g_ref, m_ref, v_ref, go_ref, d_ref, nm_ref, nv_ref):
        gv = g_ref[...].T
        go_ref[...] = gv
        nm = ADAM_B1 * m_ref[...] + (1.0 - ADAM_B1) * gv
        nv = ADAM_B2 * v_ref[...] + (1.0 - ADAM_B2) * (gv * gv)
        d_ref[...] = -ADAM_LR * ((nm / c1) / (jnp.sqrt(nv / c2) + ADAM_EPS) + ADAM_WD * w_ref[...])
        nm_ref[...] = nm
        nv_ref[...] = nv

    blk = pl.BlockSpec((None, tr, C), lambda l, i: (l, i, 0))
    blk_t = pl.BlockSpec((None, C, tr), lambda l, i: (l, 0, i))
    shp = jax.ShapeDtypeStruct((L, R, C), F32)
    return pl.pallas_call(
        body, name=name, grid=(L, R // tr), in_specs=[blk, blk_t, blk, blk], out_specs=(blk,) * 4,
        out_shape=(shp,) * 4, compiler_params=_cparams(("parallel", "parallel")),
    )(w, g_t, m, v)


REST_FF1 = 0
REST_FF2 = 512
REST_WO = 1024
REST_UP = 1152
REST_ATTN = 1280
REST_ROWS = 1312
IN_ROWS = IN_COLS // N_DEV
CONV_BITS_ROWS = 16
SHARD_ROWS = IN_ROWS + REST_ROWS + CONV_BITS_ROWS
SMALL = (("norm_mix", (DEPTH, D)), ("b_gate", (DEPTH, 3 * D)), ("pool_mix", (DEPTH, 4, POOL_G, POOL_G)),
         ("pool_scale", (DEPTH, POOL_W)), ("conv_w", (DEPTH, 3, CONV_W)), ("q_gain", (DEPTH, HEAD)),
         ("k_gain", (DEPTH, HEAD)), ("norm_mlp", (DEPTH, D)))


def _pack_weight_shards(w, l):
    b = lambda a: a.astype(BF16)
    rest = jnp.concatenate([
        b(w["w_ff1"][l].T), b(w["w_ff2"][l]), b(w["w_o"][l]),
        jnp.concatenate([b(w["w_pool_up"][l].T), b(w["w_conv_out"][l].T)], axis=1),
        b(w["w_attn_up"][l].T).reshape(REST_ROWS - REST_ATTN, D)], axis=0)
    return b(w["w_in"][l].T), rest


def _unpack_gathered(g_in, g_rest, g_conv, small_w, l):
    rest = g_rest
    take = lambda r0, rows, c0=0, cols=D: rest[:, r0:r0 + rows, c0:c0 + cols].reshape(N_DEV * rows, cols)
    conv = g_conv[:, 3 * l:3 * l + 3, :CONV_W // N_DEV]
    p = {
        "w_in_t": g_in.reshape(IN_COLS, D),
        "w_ff1_t": take(REST_FF1, 512), "w_ff2": take(REST_FF2, 512), "w_o": take(REST_WO, 128),
        "w_pool_up_t": take(REST_UP, 128, 0, POOL_W), "w_conv_out_t": take(REST_UP, 128, POOL_W, CONV_W),
        "w_attn_up_t": rest[:, REST_ATTN:].reshape(D, ATTN_OUT),
        "conv_w": jnp.transpose(conv, (1, 0, 2)).reshape(3, CONV_W),
        "pool_mix": small_w["pool_mix"][l],
    }
    for name in ("norm_mix", "b_gate", "pool_scale", "q_gain", "k_gain", "norm_mlp"):
        p[name] = small_w[name][l][None, :]
    return p


def _pack_small_grads(grads):
    flat = jnp.concatenate([jnp.stack([grads[l][name] for l in range(DEPTH)]).reshape(-1) for name, _ in SMALL])
    return jnp.pad(flat, (0, (-flat.shape[0]) % (8 * 128))).reshape(-1, 128)


def _unpack_grads(in_t, rest, small, dev):
    out = {
        "w_in_t": jnp.stack(in_t),
        "w_ff1_t": jnp.stack([a[REST_FF1:REST_FF1 + 512] for a in rest]),
        "w_ff2": jnp.stack([a[REST_FF2:REST_FF2 + 512] for a in rest]),
        "w_o": jnp.stack([a[REST_WO:REST_WO + 128] for a in rest]),
        "w_pool_up": jnp.stack([a[REST_UP:REST_UP + 128, :POOL_W].T for a in rest]),
        "w_conv_out": jnp.stack([a[REST_UP:REST_UP + 128, POOL_W:].T for a in rest]),
        "w_attn_up": jnp.stack([a[REST_ATTN:].reshape(D // N_DEV, ATTN_OUT).T for a in rest]),
    }
    out.update(_unpack_small_like_grads(small, dev))
    return out


def _unpack_small_like_grads(small, dev):
    out = {}
    flat = small.reshape(-1)
    off = 0
    for name, shp in SMALL:
        n = 1
        for s in shp:
            n *= s
        out[name] = flat[off:off + n].reshape(shp)
        off += n
    width = CONV_W // N_DEV
    out["conv_w"] = lax.dynamic_slice_in_dim(out["conv_w"], dev * width, width, axis=2)
    return out


def _pack_small_like_grads(src, dev):
    parts = []
    for name, shp in SMALL:
        a = src[name]
        if name == "conv_w":
            a = lax.dynamic_update_slice_in_dim(jnp.zeros(shp, F32), a, dev * (CONV_W // N_DEV), axis=2)
        parts.append(a.reshape(-1))
    flat = jnp.concatenate(parts)
    return jnp.pad(flat, (0, (-flat.shape[0]) % (8 * 128))).reshape(-1, 128)


WEIGHTS = ("norm_mix", "w_in", "b_gate", "pool_mix", "pool_scale", "conv_w", "q_gain", "k_gain", "w_pool_up",
           "w_conv_out", "w_attn_up", "w_o", "norm_mlp", "w_ff1", "w_ff2")


def kernel(x, norm_mix, w_in, b_gate, pool_mix, pool_scale, conv_w, q_gain, k_gain, w_pool_up, w_conv_out, w_attn_up, w_o, norm_mlp, w_ff1, w_ff2, loss_target, m_norm_mix, m_w_in, m_b_gate, m_pool_mix, m_pool_scale, m_conv_w, m_q_gain, m_k_gain, m_w_pool_up, m_w_conv_out, m_w_attn_up, m_w_o, m_norm_mlp, m_w_ff1, m_w_ff2, v_norm_mix, v_w_in, v_b_gate, v_pool_mix, v_pool_scale, v_conv_w, v_q_gain, v_k_gain, v_w_pool_up, v_w_conv_out, v_w_attn_up, v_w_o, v_norm_mlp, v_w_ff1, v_w_ff2):
    w = dict(zip(WEIGHTS, (norm_mix, w_in, b_gate, pool_mix, pool_scale, conv_w, q_gain, k_gain, w_pool_up,
                           w_conv_out, w_attn_up, w_o, norm_mlp, w_ff1, w_ff2)))
    m = dict(zip(WEIGHTS, (m_norm_mix, m_w_in, m_b_gate, m_pool_mix, m_pool_scale, m_conv_w, m_q_gain, m_k_gain,
                           m_w_pool_up, m_w_conv_out, m_w_attn_up, m_w_o, m_norm_mlp, m_w_ff1, m_w_ff2)))
    v = dict(zip(WEIGHTS, (v_norm_mix, v_w_in, v_b_gate, v_pool_mix, v_pool_scale, v_conv_w, v_q_gain, v_k_gain,
                           v_w_pool_up, v_w_conv_out, v_w_attn_up, v_w_o, v_norm_mlp, v_w_ff1, v_w_ff2)))
    xi, yi, ci = _mesh_pos()
    dev = 4 * xi + 2 * yi + ci

    saved, params = [], []
    act = x[0]
    w_in_t, m_in_t, v_in_t = (jnp.swapaxes(src["w_in"], 1, 2) for src in (w, m, v))
    h = _rmsnorm_fwd("l0_norm_mix", act, w["norm_mix"][0][None])
    conv_shard = jnp.pad(w["conv_w"].reshape(DEPTH * 3, CONV_W // N_DEV), ((0, 8 - DEPTH * 3), (0, 128 - CONV_W // N_DEV)))
    g_conv = _all_gather("gather_conv", None, conv_shard)
    for l in range(DEPTH):
        s_in, s_rest = _pack_weight_shards(w, l)
        g_in = _all_gather(f"gather_in{l}", 1 + 2 * l, s_in)
        g_rest = _all_gather(f"gather_rest{l}", 2 + 2 * l, s_rest)
        params.append(_unpack_gathered(g_in, g_rest, g_conv, w, l))
    for l in range(DEPTH):
        act, h, s = _layer_fwd(l, act, h, params[l], w["norm_mix"][l + 1][None] if l + 1 < DEPTH else None,
                               loss_target[0])
        saved.append(s)
    dy = act
    loss = _fold_rows("loss_sum", h, lanes=True, scale=0.5 / D)
    total = lax.psum(loss[0, 0], ("x", "y", "c"))
    in_t, rest, grads, pending = [None] * DEPTH, [None] * DEPTH, [None] * DEPTH, None
    start = total.reshape(1, 1)
    for l in reversed(range(DEPTH)):
        dy, grads[l], rest[l], done, pending = _layer_bwd(l, dy, params[l], saved[l], pending, 5 + 4 * l, l == 0,
                                                          start)
        start = None
        if done is not None:
            in_t[l + 1] = done[0]
    dx = dy
    g_small = _all_reduce_small("ar_small", _pack_small_grads(grads))
    in_t[0], _ = pending.finish(after=g_small)
    g = _unpack_grads(in_t, rest, g_small, dev)

    delta, new_m, new_v = {}, {}, {}
    g_t = g["w_in_t"]
    two_d = (g_t.shape[0] * g_t.shape[1], g_t.shape[2])
    d_, m_, v_ = _adamw("adamw_w_in", *[a.reshape(two_d) for a in (w_in_t, g_t, m_in_t, v_in_t)])
    g["w_in"] = jnp.swapaxes(g_t, 1, 2)
    delta["w_in"], new_m["w_in"], new_v["w_in"] = (jnp.swapaxes(a.reshape(g_t.shape), 1, 2) for a in (d_, m_, v_))
    g["w_ff1"], delta["w_ff1"], new_m["w_ff1"], new_v["w_ff1"] = _adamw_t("adamw_w_ff1", w["w_ff1"], g["w_ff1_t"],
                                                                          m["w_ff1"], v["w_ff1"])
    for name in ("w_pool_up", "w_conv_out", "w_attn_up", "w_o", "w_ff2"):
        shp = w[name].shape
        two_d = (shp[0] * shp[1], shp[2])
        d_, m_, v_ = _adamw("adamw_" + name, w[name].reshape(two_d), g[name].reshape(two_d),
                            m[name].reshape(two_d), v[name].reshape(two_d))
        delta[name], new_m[name], new_v[name] = d_.reshape(shp), m_.reshape(shp), v_.reshape(shp)
    for name, shp in SMALL:
        shp = w[name].shape
        two_d = (-1, shp[-1])
        d_, m_, v_ = _adamw("adamw_" + name, w[name].reshape(two_d), g[name].reshape(two_d),
                            m[name].reshape(two_d), v[name].reshape(two_d))
        delta[name], new_m[name], new_v[name] = d_.reshape(shp), m_.reshape(shp), v_.reshape(shp)

    return (total, dx[None], *[g[n] for n in WEIGHTS], *[delta[n] for n in WEIGHTS],
            *[new_m[n] for n in WEIGHTS], *[new_v[n] for n in WEIGHTS])
```

```python
import functools

import jax
import jax.numpy as jnp
from jax import lax
from jax.experimental import pallas as pl
from jax.experimental.pallas import tpu as pltpu
from jax.experimental.pallas import tpu_sc as plsc

F32 = jnp.float32
BF16 = jnp.bfloat16
MESH = pl.DeviceIdType.MESH

D = 1024
SEQ = 4096
DEPTH = 2
N_DEV = 8
POOL_WINDOWS = (2, 4, 8, 16)
POOL_W = 512
POOL_G = 128
CONV_W = 512
HEAD = 64
ATTN_GROUPS = ((128, 1), (512, 4), (2048, 16))
HPG = 4
ATTN_W = 768
ATTN_OUT = 256
ATTN_BLK = 128
D_FF = 4096
EPS = 1e-6
MASK = -1e30
OFF_POOL = 0
OFF_CB = 512
OFF_CC = 1024
OFF_CX = 1536
OFF_Q = 2048
OFF_K = 2816
OFF_V = 3584
OFF_GATE = 4352
IN_COLS = 7424
ADAM_LR = 0.001
ADAM_B1 = 0.9
ADAM_B2 = 0.999
ADAM_EPS = 1e-08
ADAM_WD = 0.01
ADAM_STEP = 10

ROW_TILE = 512
SEQ_CHUNK = 256
HALO = 16
VMEM_LIMIT = 56 * 1024 * 1024


def _cparams(sem=None):
    return pltpu.CompilerParams(dimension_semantics=sem, vmem_limit_bytes=VMEM_LIMIT)


def _call(body, args, in_specs, after=None, **kw):
    if after is not None:
        after = list(after) if isinstance(after, (list, tuple)) else [after]
        n, k, inner = len(args), len(after), body

        def body(*refs):
            return inner(*refs[:n], *refs[n + k:])

        args = list(args) + after
        in_specs = list(in_specs) + [pl.BlockSpec(memory_space=pl.ANY)] * k
    return pl.pallas_call(body, in_specs=in_specs, **kw)(*args)


_DIMS = {"nn": (((1,), (0,)), ((), ())), "nt": (((1,), (1,)), ((), ())), "tn": (((0,), (0,)), ((), ()))}


def _matmul(name, a, b, mode, tm, tn, tk, out_dtypes=(F32,), extras=(), epilogue=None, into=None, after=None):
    if mode == "tn":
        K, M = a.shape
    else:
        M, K = a.shape
    N = b.shape[0] if mode == "nt" else b.shape[1]
    assert M % tm == 0 and N % tn == 0 and K % tk == 0, (name, M, N, K, tm, tn, tk)
    nk = K // tk
    n_extra = len(extras)
    n_out = len(out_dtypes)
    dims = _DIMS[mode]
    n_alias = 0 if into is None or isinstance(into[0], jax.ShapeDtypeStruct) else 1

    def body(a_ref, b_ref, *rest):
        extra_refs = rest[:n_extra]
        out_refs = rest[n_extra + n_alias:n_extra + n_alias + n_out]

        def finish(acc):
            if epilogue is None:
                res = (acc,)
            else:
                res = epilogue(acc, *[r[...] for r in extra_refs])
            for o_ref, v in zip(out_refs, res):
                o_ref[...] = v.astype(o_ref.dtype)

        part = lax.dot_general(a_ref[...].astype(BF16), b_ref[...].astype(BF16), dims,
                               preferred_element_type=F32)
        if nk == 1:
            finish(part)
        else:
            acc_ref = rest[-1]
            k = pl.program_id(2)

            @pl.when(k == 0)
            def _():
                acc_ref[...] = part

            @pl.when(k > 0)
            def _():
                acc_ref[...] += part

            @pl.when(k == nk - 1)
            def _():
                finish(acc_ref[...])

    if mode == "tn":
        a_spec = pl.BlockSpec((tk, tm), lambda i, j, k: (k, i))
    else:
        a_spec = pl.BlockSpec((tm, tk), lambda i, j, k: (i, k))
    if mode == "nt":
        b_spec = pl.BlockSpec((tn, tk), lambda i, j, k: (j, k))
    else:
        b_spec = pl.BlockSpec((tk, tn), lambda i, j, k: (k, j))
    in_specs = [a_spec, b_spec]
    args = [a, b]
    for arr, kind in extras:
        if kind == "mn":
            in_specs.append(pl.BlockSpec((tm, tn), lambda i, j, k: (i, j)))
        else:
            in_specs.append(pl.BlockSpec((1, tn), lambda i, j, k: (0, j)))
        args.append(arr)
    part = [isinstance(dt, tuple) for dt in out_dtypes]
    out_shape = tuple(jax.ShapeDtypeStruct((M // tm * 8, N), dt[0]) if p else jax.ShapeDtypeStruct((M, N), dt)
                      for dt, p in zip(out_dtypes, part))
    out_specs = tuple(pl.BlockSpec((8 if p else tm, tn), lambda i, j, k: (i, j)) for p in part)
    aliases = {}
    if into is not None:
        buf, row0, col0 = into
        assert n_out == 1 and (M // N_DEV) % tm == 0 and row0 % tm == 0 and col0 % tn == 0
        per_dev = M // N_DEV // tm
        out_shape = (jax.ShapeDtypeStruct(buf.shape, buf.dtype),)
        out_specs = (pl.BlockSpec((None, tm, tn), lambda i, j, k: (i // per_dev, row0 // tm + i % per_dev,
                                                                   col0 // tn + j)),)
        if not isinstance(buf, jax.ShapeDtypeStruct):
            aliases = {len(args): 0}
            in_specs.append(pl.BlockSpec(memory_space=pl.ANY))
            args.append(buf)
    scratch = [] if nk == 1 else [pltpu.VMEM((tm, tn), F32)]
    res = _call(
        body, args, in_specs, after, name=name, grid=(M // tm, N // tn, nk), out_specs=out_specs,
        out_shape=out_shape, scratch_shapes=scratch, input_output_aliases=aliases,
        compiler_params=_cparams(("parallel", "parallel", "arbitrary")))
    return res if n_out > 1 else res[0]


def _rmsnorm_fwd(name, x, gain, after=None):
    S_, D_ = x.shape

    def body(x_ref, g_ref, h_ref):
        xf = x_ref[...]
        ms = jnp.mean(xf * xf, axis=-1, keepdims=True)
        h_ref[...] = (xf * lax.rsqrt(ms + EPS) * g_ref[...]).astype(BF16)

    return _call(
        body, [x, gain], [pl.BlockSpec((ROW_TILE, D_), lambda i: (i, 0)), pl.BlockSpec((1, D_), lambda i: (0, 0))],
        after, name=name, grid=(S_ // ROW_TILE,),
        out_specs=pl.BlockSpec((ROW_TILE, D_), lambda i: (i, 0)),
        out_shape=jax.ShapeDtypeStruct((S_, D_), BF16),
        compiler_params=_cparams(("parallel",)))


def _rmsnorm_bwd(name, x, gain, dh, dres, after=None):
    S_, D_ = x.shape
    n = S_ // ROW_TILE

    def body(x_ref, g_ref, dh_ref, dres_ref, dx_ref, dg_ref, acc_ref):
        i = pl.program_id(0)
        xf = x_ref[...]
        rstd = lax.rsqrt(jnp.mean(xf * xf, axis=-1, keepdims=True) + EPS)
        xhat = xf * rstd
        dhv = dh_ref[...]
        dxhat = dhv * g_ref[...]
        c = jnp.mean(dxhat * xhat, axis=-1, keepdims=True)
        dx_ref[...] = dres_ref[...] + rstd * (dxhat - xhat * c)
        part = jnp.sum((dhv * xhat).reshape(ROW_TILE // 8, 8, D_), axis=0)

        @pl.when(i == 0)
        def _():
            acc_ref[...] = part

        @pl.when(i > 0)
        def _():
            acc_ref[...] += part

        @pl.when(i == n - 1)
        def _():
            dg_ref[...] = jnp.sum(acc_ref[...], axis=0, keepdims=True)

    row = pl.BlockSpec((ROW_TILE, D_), lambda i: (i, 0))
    vec = pl.BlockSpec((1, D_), lambda i: (0, 0))
    return _call(
        body, [x, gain, dh, dres], [row, vec, row, row], after, name=name, grid=(n,), out_specs=(row, vec),
        out_shape=(jax.ShapeDtypeStruct((S_, D_), F32), jax.ShapeDtypeStruct((1, D_), F32)),
        scratch_shapes=[pltpu.VMEM((8, D_), F32)],
        compiler_params=_cparams(("arbitrary",)))


def _loss_head(name, y, target):
    S_, D_ = y.shape
    n = S_ // ROW_TILE

    def body(y_ref, t_ref, dy_ref, l_ref, acc_ref):
        i = pl.program_id(0)
        e = y_ref[...] - t_ref[...]
        dy_ref[...] = e * (1.0 / D_)
        part = jnp.sum((e * e).reshape(ROW_TILE // 8, 8, D_), axis=0)

        @pl.when(i == 0)
        def _():
            acc_ref[...] = part

        @pl.when(i > 0)
        def _():
            acc_ref[...] += part

        @pl.when(i == n - 1)
        def _():
            s = jnp.sum(acc_ref[...], axis=1, keepdims=True)
            l_ref[...] = jnp.sum(s, axis=0, keepdims=True) * (0.5 / D_)

    row = pl.BlockSpec((ROW_TILE, D_), lambda i: (i, 0))
    return pl.pallas_call(
        body, name=name, grid=(n,), in_specs=[row, row],
        out_specs=(row, pl.BlockSpec((1, 1), lambda i: (0, 0))),
        out_shape=(jax.ShapeDtypeStruct((S_, D_), F32), jax.ShapeDtypeStruct((1, 1), F32)),
        scratch_shapes=[pltpu.VMEM((8, D_), F32)],
        compiler_params=_cparams(("arbitrary",)),
    )(y, target)


def _rows_with_halo(ref, cols, i, n_chunks, before, after):
    r0 = pl.multiple_of(i * SEQ_CHUNK, SEQ_CHUNK)
    parts = []
    if before:
        h0 = pl.multiple_of(jnp.maximum(r0 - HALO, 0), 8)
        halo = ref[pl.ds(h0, HALO), cols]
        parts.append(jnp.where(i > 0, halo, jnp.zeros_like(halo)))
    parts.append(ref[pl.ds(r0, SEQ_CHUNK), cols])
    if after:
        a0 = pl.multiple_of(jnp.minimum(r0 + SEQ_CHUNK, (n_chunks - 1) * SEQ_CHUNK + SEQ_CHUNK - HALO), 8)
        halo = ref[pl.ds(a0, HALO), cols]
        parts.append(jnp.where(i < n_chunks - 1, halo, jnp.zeros_like(halo)))
    return parts[0] if len(parts) == 1 else jnp.concatenate(parts, axis=0)


def _shift_down(v, k):
    return pltpu.roll(v, k, 0)


def _shift_up(v, k):
    return pltpu.roll(v, v.shape[0] - k, 0)


def _pool_diff(xx, w, t_main):
    s = xx
    k = 1
    while k < w:
        s = s + _shift_down(s, k)
        k *= 2
    cnt = jnp.minimum(t_main + 1, w).astype(F32)
    return s[HALO:] / cnt - xx[HALO:]


def _pool_fwd(name, z, pool_mix, pool_scale):
    S_ = z.shape[0]
    n_chunks = S_ // SEQ_CHUNK

    cols = slice(0, POOL_G)

    def body(u_ref, mix_ref, sc_ref, y_ref):
        mixg = mix_ref[...].astype(BF16)
        scg = sc_ref[...]
        for g, w in enumerate(POOL_WINDOWS):
            @pl.when(pl.program_id(0) == g)
            def _(w=w):
                def chunk(i, carry):
                    r0 = pl.multiple_of(i * SEQ_CHUNK, SEQ_CHUNK)
                    xx = _rows_with_halo(u_ref, cols, i, n_chunks, True, False)
                    t = r0 + lax.broadcasted_iota(jnp.int32, (SEQ_CHUNK, POOL_G), 0)
                    d = _pool_diff(xx, w, t)
                    y = jnp.dot(d.astype(BF16), mixg, preferred_element_type=F32) * scg
                    y_ref[pl.ds(r0, SEQ_CHUNK), :] = y.astype(BF16)
                    return carry

                lax.fori_loop(0, n_chunks, chunk, 0)

    slab = pl.BlockSpec((S_, POOL_G), lambda g: (0, g))
    return pl.pallas_call(
        body, name=name, grid=(len(POOL_WINDOWS),),
        in_specs=[slab, pl.BlockSpec((None, POOL_G, POOL_G), lambda g: (g, 0, 0)),
                  pl.BlockSpec((1, POOL_G), lambda g: (0, g))],
        out_specs=slab, out_shape=jax.ShapeDtypeStruct((S_, POOL_W), BF16),
        compiler_params=_cparams(("parallel",)),
    )(z, pool_mix, pool_scale)


def _pool_bwd(name, z, dy, pool_mix, pool_scale, dz, after=None):
    S_ = z.shape[0]
    n_chunks = S_ // SEQ_CHUNK
    rows_a = SEQ_CHUNK + HALO

    cols = slice(0, POOL_G)

    def body(u_ref, dy_ref, mix_ref, sc_ref, du_ref, dmix_ref, dsc_ref):
        mixg = mix_ref[...].astype(BF16)
        scg = sc_ref[...]
        for g, w in enumerate(POOL_WINDOWS):
            @pl.when(pl.program_id(0) == g)
            def _(w=w):
                def chunk(i, carry):
                    dmix_acc, dsc_acc = carry
                    r0 = pl.multiple_of(i * SEQ_CHUNK, SEQ_CHUNK)
                    xx = _rows_with_halo(u_ref, cols, i, n_chunks, True, False)
                    t = r0 + lax.broadcasted_iota(jnp.int32, (SEQ_CHUNK, POOL_G), 0)
                    d = _pool_diff(xx, w, t).astype(BF16)
                    ypre = jnp.dot(d, mixg, preferred_element_type=F32)
                    dyy = _rows_with_halo(dy_ref, cols, i, n_chunks, False, True)
                    dys = (dyy * scg).astype(BF16)
                    dsc_acc = dsc_acc + jnp.sum((dyy[:SEQ_CHUNK] * ypre).reshape(SEQ_CHUNK // 8, 8, POOL_G), axis=0)
                    dmix_acc = dmix_acc + lax.dot_general(d, dys[:SEQ_CHUNK], _DIMS["tn"],
                                                          preferred_element_type=F32)
                    dd = lax.dot_general(dys, mixg, _DIMS["nt"], preferred_element_type=F32)
                    ta = r0 + lax.broadcasted_iota(jnp.int32, (rows_a, POOL_G), 0)
                    f = dd / jnp.minimum(ta + 1, w).astype(F32)
                    k = 1
                    while k < w:
                        f = f + _shift_up(f, k)
                        k *= 2
                    du_ref[pl.ds(r0, SEQ_CHUNK), :] = (f[:SEQ_CHUNK] - dd[:SEQ_CHUNK]).astype(BF16)
                    return dmix_acc, dsc_acc

                dmix_acc, dsc_acc = lax.fori_loop(
                    0, n_chunks, chunk, (jnp.zeros((POOL_G, POOL_G), F32), jnp.zeros((8, POOL_G), F32)))
                dmix_ref[...] = dmix_acc
                dsc_ref[...] = jnp.sum(dsc_acc, axis=0, keepdims=True)

    slab = pl.BlockSpec((S_, POOL_G), lambda g: (0, g))
    mix_spec = pl.BlockSpec((None, POOL_G, POOL_G), lambda g: (g, 0, 0))
    vec = pl.BlockSpec((1, POOL_G), lambda g: (0, g))
    return _call(
        _drop_ref(body, 4), [z, dy, pool_mix, pool_scale, dz], [slab, slab, mix_spec, vec, ANY], after, name=name,
        grid=(len(POOL_WINDOWS),), out_specs=(slab, mix_spec, vec),
        out_shape=(jax.ShapeDtypeStruct(dz.shape, dz.dtype), jax.ShapeDtypeStruct((4, POOL_G, POOL_G), F32),
                   jax.ShapeDtypeStruct((1, POOL_W), F32)),
        input_output_aliases={4: 0}, compiler_params=_cparams(("parallel",)))


def _conv_specs(S_):
    slab = lambda off: pl.BlockSpec((S_, 128), lambda c, off=off: (0, off // 128 + c))
    return slab(OFF_CB), slab(OFF_CC), slab(OFF_CX)


def _conv_fwd(name, z, conv_w, after=None):
    S_ = z.shape[0]
    n_chunks = S_ // SEQ_CHUNK
    col = slice(0, 128)

    def body(b_ref, c_ref, x_ref, w_ref, y_ref):
        w0, w1, w2 = w_ref[0:1, :], w_ref[1:2, :], w_ref[2:3, :]

        def chunk(i, carry):
            r0 = pl.multiple_of(i * SEQ_CHUNK, SEQ_CHUNK)
            u = _rows_with_halo(c_ref, col, i, n_chunks, True, False) * _rows_with_halo(x_ref, col, i, n_chunks, True, False)
            y = w2 * u + w1 * _shift_down(u, 1) + w0 * _shift_down(u, 2)
            y_ref[pl.ds(r0, SEQ_CHUNK), :] = (b_ref[pl.ds(r0, SEQ_CHUNK), :] * y[HALO:]).astype(BF16)
            return carry

        lax.fori_loop(0, n_chunks, chunk, 0)

    sb, sc, sx = _conv_specs(S_)
    return _call(
        body, [z, z, z, conv_w], [sb, sc, sx, pl.BlockSpec((3, 128), lambda c: (0, c))], after,
        name=name, grid=(CONV_W // 128,),
        out_specs=pl.BlockSpec((S_, 128), lambda c: (0, c)),
        out_shape=jax.ShapeDtypeStruct((S_, CONV_W), BF16),
        compiler_params=_cparams(("parallel",)))


def _conv_bwd(name, z, dy, conv_w, after=None):
    S_ = z.shape[0]
    n_chunks = S_ // SEQ_CHUNK
    col = slice(0, 128)
    lo, hi = HALO, HALO + SEQ_CHUNK

    def body(b_ref, c_ref, x_ref, dy_ref, w_ref, db_ref, dc_ref, dx_ref, dw_ref):
        w0, w1, w2 = w_ref[0:1, :], w_ref[1:2, :], w_ref[2:3, :]

        def chunk(i, carry):
            a0, a1, a2 = carry
            r0 = pl.multiple_of(i * SEQ_CHUNK, SEQ_CHUNK)
            cc = _rows_with_halo(c_ref, col, i, n_chunks, True, True)
            xx = _rows_with_halo(x_ref, col, i, n_chunks, True, True)
            bb = _rows_with_halo(b_ref, col, i, n_chunks, True, True)
            dyy = _rows_with_halo(dy_ref, col, i, n_chunks, True, True)
            u = cc * xx
            u1 = _shift_down(u, 1)
            u2 = _shift_down(u, 2)
            y = w2 * u + w1 * u1 + w0 * u2
            dyv = dyy * bb
            du = w2 * dyv + w1 * _shift_up(dyv, 1) + w0 * _shift_up(dyv, 2)
            db_ref[pl.ds(r0, SEQ_CHUNK), :] = (dyy[lo:hi] * y[lo:hi]).astype(BF16)
            dc_ref[pl.ds(r0, SEQ_CHUNK), :] = (du[lo:hi] * xx[lo:hi]).astype(BF16)
            dx_ref[pl.ds(r0, SEQ_CHUNK), :] = (du[lo:hi] * cc[lo:hi]).astype(BF16)
            red = lambda v: jnp.sum(v.reshape(SEQ_CHUNK // 8, 8, 128), axis=0)
            dm = dyv[lo:hi]
            return a0 + red(dm * u2[lo:hi]), a1 + red(dm * u1[lo:hi]), a2 + red(dm * u[lo:hi])

        zero = jnp.zeros((8, 128), F32)
        a0, a1, a2 = lax.fori_loop(0, n_chunks, chunk, (zero, zero, zero))
        dw_ref[0:1, :] = jnp.sum(a0, axis=0, keepdims=True)
        dw_ref[1:2, :] = jnp.sum(a1, axis=0, keepdims=True)
        dw_ref[2:3, :] = jnp.sum(a2, axis=0, keepdims=True)

    sb, sc, sx = _conv_specs(S_)
    slab = pl.BlockSpec((S_, 128), lambda c: (0, c))
    wspec = pl.BlockSpec((3, 128), lambda c: (0, c))
    act = jax.ShapeDtypeStruct((S_, CONV_W), BF16)
    return _call(
        body, [z, z, z, dy, conv_w], [sb, sc, sx, slab, wspec], after, name=name, grid=(CONV_W // 128,),
        out_specs=(slab, slab, slab, wspec),
        out_shape=(act, act, act, jax.ShapeDtypeStruct((3, CONV_W), F32)),
        compiler_params=_cparams(("parallel",)))


def _head_ones(pw):
    a = lax.broadcasted_iota(jnp.int32, (pw, pw), 0) // HEAD
    b = lax.broadcasted_iota(jnp.int32, (pw, pw), 1) // HEAD
    return (a == b).astype(BF16)


def _head_sum(v, ones):
    hi = v.astype(BF16)
    lo = (v - hi.astype(F32)).astype(BF16)
    return jnp.dot(hi, ones, preferred_element_type=F32) + jnp.dot(lo, ones, preferred_element_type=F32)


def _head_norm(v, gain, ones):
    rstd = lax.rsqrt(_head_sum(v * v, ones) * (1.0 / HEAD) + EPS)
    xhat = v * rstd
    return xhat * gain, xhat, rstd


def _head_norm_bwd(dy, xhat, rstd, gain, ones):
    dxhat = dy * gain
    c = _head_sum(dxhat * xhat, ones) * (1.0 / HEAD)
    dv = rstd * (dxhat - xhat * c)
    dg = jnp.sum((dy * xhat).reshape(dy.shape[0] // 8, 8, dy.shape[1]), axis=0)
    return dv, dg


def _head_masks(pw):
    lane_head = lax.broadcasted_iota(jnp.int32, (1, pw), 1) // HEAD
    return [lane_head == h for h in range(pw // HEAD)]


def _only(mask, v):
    return jnp.where(mask, v, jnp.zeros_like(v))


def _attn_specs(S_, g, dil):
    rows = ATTN_BLK * dil
    nb = S_ // rows
    pw = 128 if dil > 1 else ATTN_OUT
    cq, ck, cv = ((OFF_Q + g * ATTN_OUT) // pw, (OFF_K + g * ATTN_OUT) // pw, (OFF_V + g * ATTN_OUT) // pw)
    return rows, nb, pw, pw // HEAD, ATTN_OUT // pw, cq, ck, cv


ATTN_BATCH = 4


def _attn_group(dil):
    return 4 if dil == 1 else 1


def _attn_block_specs(rows, grp, pw, last=None):
    step = (lambda n: n) if last is None else (lambda n: jnp.minimum(n, last))
    cur = lambda c: pl.BlockSpec((rows * grp, pw), lambda hp, n, c=c: (step(n), c + hp))
    prev = lambda c: pl.BlockSpec((rows, pw), lambda hp, n, c=c: (jnp.maximum(step(n) * grp - 1, 0), c + hp))
    return cur, prev


def _band_mask(has_prev):
    qi = lax.broadcasted_iota(jnp.int32, (ATTN_BLK, 2 * ATTN_BLK), 0)
    ki = lax.broadcasted_iota(jnp.int32, (ATTN_BLK, 2 * ATTN_BLK), 1)
    in_prev = jnp.logical_and(ki < ATTN_BLK, ki >= qi)
    if has_prev is not True:
        in_prev = jnp.logical_and(in_prev, has_prev)
    return jnp.logical_or(in_prev, jnp.logical_and(ki >= ATTN_BLK, ki - ATTN_BLK <= qi))


def _rows_of(ref, r, dil):
    if dil == 1:
        return ref[r * ATTN_BLK:(r + 1) * ATTN_BLK, :]
    return ref[pl.ds(r, ATTN_BLK, stride=dil), :]


def _put_rows(ref, r, dil, val):
    if dil == 1:
        ref[r * ATTN_BLK:(r + 1) * ATTN_BLK, :] = val.astype(ref.dtype)
    else:
        ref[pl.ds(r, ATTN_BLK, stride=dil), :] = val.astype(ref.dtype)


def _attn_fwd(name, z, q_gain, k_gain, g, dil):
    S_ = z.shape[0]
    rows, nb, pw, heads, npairs, cq, ck, cv = _attn_specs(S_, g, dil)
    scale = HEAD ** -0.5

    grp = _attn_group(dil)
    nsteps = nb // grp

    def body(q_ref, k_ref, kp_ref, v_ref, vp_ref, gq_ref, gk_ref, o_ref, l_ref):
        n = pl.program_id(1)
        ones, hmask = _head_ones(pw), _head_masks(pw)
        gq, gk = jnp.tile(gq_ref[...], (1, heads)), jnp.tile(gk_ref[...], (1, heads))
        mask_first, mask_rest = _band_mask(n > 0), _band_mask(True)
        for r0 in range(0, dil * grp, ATTN_BATCH):
            rs = range(r0, min(r0 + ATTN_BATCH, dil * grp))
            qn, kn, vv, s, p = {}, {}, {}, {}, {}
            kcn = {}
            for r in rs:
                q, kc, vc = _rows_of(q_ref, r, dil), _rows_of(k_ref, r, dil), _rows_of(v_ref, r, dil)
                kcn[r] = _head_norm(kc, gk, ones)[0]
                if dil == 1 and r > 0:
                    kpn = kcn[r - 1] if r - 1 in kcn else _head_norm(_rows_of(k_ref, r - 1, dil), gk, ones)[0]
                    vp = _rows_of(v_ref, r - 1, dil)
                else:
                    kpn, vp = _head_norm(_rows_of(kp_ref, r, dil), gk, ones)[0], _rows_of(vp_ref, r, dil)
                qn[r] = _head_norm(q, gq, ones)[0].astype(BF16)
                kn[r] = jnp.concatenate([kpn, kcn[r]], axis=0).astype(BF16)
                vv[r] = jnp.concatenate([vp, vc], axis=0).astype(BF16)
            keys = [(r, h) for r in rs for h in range(heads)]
            for r, h in keys:
                s[r, h] = lax.dot_general(_only(hmask[h], qn[r]), kn[r], _DIMS["nt"],
                                          preferred_element_type=F32) * scale
            lse, den = {}, {}
            for key in keys:
                mask = mask_rest if (dil == 1 and key[0] > 0) else mask_first
                sm = jnp.where(mask, s[key], MASK)
                m = jnp.max(sm, axis=-1, keepdims=True)
                e = jnp.exp(sm - m)
                den[key] = jnp.sum(e, axis=-1, keepdims=True)
                p[key] = e.astype(BF16)
                lse[key] = m + jnp.log(den[key])
            for r in rs:
                out = jnp.zeros((ATTN_BLK, pw), F32)
                lse_all = jnp.zeros((ATTN_BLK, pw), F32)
                for h in range(heads):
                    out = jnp.where(hmask[h], jnp.dot(p[r, h], vv[r], preferred_element_type=F32) / den[r, h], out)
                    lse_all = jnp.where(hmask[h], lse[r, h], lse_all)
                _put_rows(o_ref, r, dil, out)
                _put_rows(l_ref, r, dil, lse_all)

    cur, prev = _attn_block_specs(rows, grp, pw)
    gspec = pl.BlockSpec((1, HEAD), lambda hp, n: (0, 0))
    shp = jax.ShapeDtypeStruct((S_, ATTN_OUT), F32)
    return pl.pallas_call(
        body, name=name, grid=(npairs, nsteps),
        in_specs=[cur(cq), cur(ck), prev(ck), cur(cv), prev(cv), gspec, gspec],
        out_specs=(cur(0), cur(0)), out_shape=(shp, shp),
        compiler_params=_cparams(("parallel", "parallel")),
    )(z, z, z, z, z, q_gain, k_gain)


def _attn_combine(name, os_, ls_):
    S_ = os_[0].shape[0]

    def body(o0, o1, o2, l0, l1, l2, o_ref, l_ref):
        a, b, c = l0[...], l1[...], l2[...]
        m = jnp.maximum(jnp.maximum(a, b), c)
        ea, eb, ec = jnp.exp(a - m), jnp.exp(b - m), jnp.exp(c - m)
        zsum = ea + eb + ec
        o_ref[...] = (ea * o0[...] + eb * o1[...] + ec * o2[...]) / zsum
        l_ref[...] = m + jnp.log(zsum)

    row = pl.BlockSpec((ROW_TILE, ATTN_OUT), lambda i: (i, 0))
    shp = jax.ShapeDtypeStruct((S_, ATTN_OUT), F32)
    return pl.pallas_call(
        body, name=name, grid=(S_ // ROW_TILE,), in_specs=[row] * 6, out_specs=(row, row), out_shape=(shp, shp),
        compiler_params=_cparams(("parallel",)),
    )(*os_, *ls_)


def _attn_bwd(name, z, q_gain, k_gain, do, o, lse, g, dil, after=None):
    S_ = z.shape[0]
    rows, nb, pw, heads, npairs, cq, ck, cv = _attn_specs(S_, g, dil)
    scale = HEAD ** -0.5

    def body(q_ref, kc_ref, kp_ref, vc_ref, vp_ref, gq_ref, gk_ref, do_ref, o_ref, l_ref,
             dq_ref, dk_ref, dv_ref, dgq_ref, dgk_ref, ck_ref, cvv_ref, gq_acc, gk_acc):
        hp = pl.program_id(0)
        n = pl.program_id(1)
        live = n < nb
        mask = jnp.logical_and(_band_mask(n > 0), live)
        ones, hmask = _head_ones(pw), _head_masks(pw)
        gq, gk = jnp.tile(gq_ref[...], (1, heads)), jnp.tile(gk_ref[...], (1, heads))

        @pl.when(n == 0)
        def _():
            ck_ref[...] = jnp.zeros_like(ck_ref)
            cvv_ref[...] = jnp.zeros_like(cvv_ref)

        @pl.when(jnp.logical_and(n == 0, hp == 0))
        def _():
            gq_acc[...] = jnp.zeros_like(gq_acc)
            gk_acc[...] = jnp.zeros_like(gk_acc)

        dgq = jnp.zeros((8, pw), F32)
        dgk = jnp.zeros((8, pw), F32)
        for r0 in range(0, dil, ATTN_BATCH):
            rs = range(r0, min(r0 + ATTN_BATCH, dil))
            keys = [(r, h) for r in rs for h in range(heads)]
            qn, qhat, qrstd, kn, kphat, kprstd, vv, dob, delta, lse = ({} for _ in range(10))
            for r in rs:
                q, kc, kp = _rows_of(q_ref, r, dil), _rows_of(kc_ref, r, dil), _rows_of(kp_ref, r, dil)
                dov = _rows_of(do_ref, r, dil)
                qn_f, qhat[r], qrstd[r] = _head_norm(q, gq, ones)
                kpn, kphat[r], kprstd[r] = _head_norm(kp, gk, ones)
                qn[r] = qn_f.astype(BF16)
                kn[r] = jnp.concatenate([kpn, _head_norm(kc, gk, ones)[0]], axis=0).astype(BF16)
                vv[r] = jnp.concatenate([_rows_of(vp_ref, r, dil), _rows_of(vc_ref, r, dil)], axis=0).astype(BF16)
                dob[r] = dov.astype(BF16)
                delta[r] = _head_sum(dov * _rows_of(o_ref, r, dil), ones)
                lse[r] = _rows_of(l_ref, r, dil)
            s, dp = {}, {}
            for r, h in keys:
                s[r, h] = lax.dot_general(_only(hmask[h], qn[r]), kn[r], _DIMS["nt"],
                                          preferred_element_type=F32) * scale
                dp[r, h] = lax.dot_general(_only(hmask[h], dob[r]), vv[r], _DIMS["nt"], preferred_element_type=F32)
            p, ds = {}, {}
            for r, h in keys:
                col = slice(h * HEAD, h * HEAD + 1)
                pk = jnp.where(mask, jnp.exp(jnp.where(mask, s[r, h], MASK) - lse[r][:, col]), 0.0)
                ds[r, h] = (pk * (dp[r, h] - delta[r][:, col]) * scale).astype(BF16)
                p[r, h] = pk.astype(BF16)
            dqn, dkn, dvv = {}, {}, {}
            for r in rs:
                dqn[r] = jnp.zeros((ATTN_BLK, pw), F32)
                dkn[r] = jnp.zeros((2 * ATTN_BLK, pw), F32)
                dvv[r] = jnp.zeros((2 * ATTN_BLK, pw), F32)
                for h in range(heads):
                    dqn[r] = jnp.where(hmask[h], jnp.dot(ds[r, h], kn[r], preferred_element_type=F32), dqn[r])
                    dkn[r] = jnp.where(hmask[h], lax.dot_general(ds[r, h], qn[r], _DIMS["tn"],
                                                                 preferred_element_type=F32), dkn[r])
                    dvv[r] = jnp.where(hmask[h], lax.dot_general(p[r, h], dob[r], _DIMS["tn"],
                                                                 preferred_element_type=F32), dvv[r])
            for r in rs:
                dq_all, dg = _head_norm_bwd(dqn[r], qhat[r], qrstd[r], gq, ones)
                dgq = dgq + dg
                dk_all, dg = _head_norm_bwd(_rows_of(ck_ref, r, dil) + dkn[r][:ATTN_BLK], kphat[r], kprstd[r], gk, ones)
                dgk = dgk + dg
                dv_all = _rows_of(cvv_ref, r, dil) + dvv[r][:ATTN_BLK]

                @pl.when(live)
                def _(dq_all=dq_all, r=r):
                    _put_rows(dq_ref, r, dil, dq_all)

                _put_rows(dk_ref, r, dil, dk_all)
                _put_rows(dv_ref, r, dil, dv_all)
                _put_rows(ck_ref, r, dil, dkn[r][ATTN_BLK:])
                _put_rows(cvv_ref, r, dil, dvv[r][ATTN_BLK:])
        gq_acc[...] += dgq
        gk_acc[...] += dgk

        @pl.when(jnp.logical_and(n == nb, hp == npairs - 1))
        def _():
            fold = lambda a: sum(a[:, h * HEAD:(h + 1) * HEAD] for h in range(heads))
            dgq_ref[...] = jnp.sum(fold(gq_acc[...]), axis=0, keepdims=True)
            dgk_ref[...] = jnp.sum(fold(gk_acc[...]), axis=0, keepdims=True)

    last = nb - 1
    cur = lambda c: pl.BlockSpec((rows, pw), lambda hp, n, c=c: (jnp.minimum(n, last), c + hp))
    prev = lambda c: pl.BlockSpec((rows, pw), lambda hp, n, c=c: (jnp.maximum(n - 1, 0), c + hp))
    gspec = pl.BlockSpec((1, HEAD), lambda hp, n: (0, 0))
    act = jax.ShapeDtypeStruct((S_, ATTN_OUT), F32)
    vec = jax.ShapeDtypeStruct((1, HEAD), F32)
    return _call(
        body, [z, z, z, z, z, q_gain, k_gain, do, o, lse],
        [cur(cq), cur(ck), prev(ck), cur(cv), prev(cv), gspec, gspec, cur(0), cur(0), cur(0)], after,
        name=name, grid=(npairs, nb + 1),
        out_specs=(cur(0), prev(0), prev(0), gspec, gspec),
        out_shape=(act, act, act, vec, vec),
        scratch_shapes=[pltpu.VMEM((rows, pw), F32), pltpu.VMEM((rows, pw), F32),
                        pltpu.VMEM((8, pw), F32), pltpu.VMEM((8, pw), F32)],
        compiler_params=_cparams(("arbitrary", "arbitrary")))


MIX_TN = 256
MIX_TM = 2048


def _sigmoid(v):
    return 1.0 / (1.0 + jnp.exp(-v))


def _mix_fwd(name, z, b_gate, ys, ws):
    S_ = z.shape[0]
    tm, tn = MIX_TM, MIX_TN
    gblk = OFF_GATE // tn

    def body(yp, yc, ya, wp, wc, wa, g0, g1, g2, b0, b1, b2, m_ref):
        acc = None
        for y_ref, w_ref, g_ref, b_ref in ((yp, wp, g0, b0), (yc, wc, g1, b1), (ya, wa, g2, b2)):
            u = lax.dot_general(y_ref[...].astype(BF16), w_ref[...], _DIMS["nt"], preferred_element_type=F32)
            t = _sigmoid(g_ref[...] + b_ref[...]) * u
            acc = t if acc is None else acc + t
        m_ref[...] = acc.astype(BF16)

    yspec = lambda w: pl.BlockSpec((tm, w), lambda i, j: (i, 0))
    wspec = lambda w: pl.BlockSpec((tn, w), lambda i, j: (j, 0))
    gspec = lambda b: pl.BlockSpec((tm, tn), lambda i, j, b=b: (i, gblk + b * (D // tn) + j))
    bspec = lambda b: pl.BlockSpec((1, tn), lambda i, j, b=b: (0, b * (D // tn) + j))
    return pl.pallas_call(
        body, name=name, grid=(S_ // tm, D // tn),
        in_specs=[yspec(POOL_W), yspec(CONV_W), yspec(ATTN_OUT), wspec(POOL_W), wspec(CONV_W), wspec(ATTN_OUT),
                  gspec(0), gspec(1), gspec(2), bspec(0), bspec(1), bspec(2)],
        out_specs=pl.BlockSpec((tm, tn), lambda i, j: (i, j)),
        out_shape=jax.ShapeDtypeStruct((S_, D), BF16),
        compiler_params=_cparams(("parallel", "parallel")),
    )(*ys, *ws, z, z, z, b_gate, b_gate, b_gate)


def _mix_bwd(name, z, b_gate, y, w, dmerged, branch, dz, after=None):
    S_ = z.shape[0]
    tm, tn = MIX_TM, MIX_TN
    width = y.shape[1]
    gblk = OFF_GATE // tn + branch * (D // tn)
    nj = D // tn

    def body(y_ref, w_ref, g_ref, b_ref, dm_ref, dy_ref, dw_ref, dg_ref, db_ref, acc_ref):
        i, j = pl.program_id(0), pl.program_id(1)
        yb = y_ref[...].astype(BF16)
        u = lax.dot_general(yb, w_ref[...], _DIMS["nt"], preferred_element_type=F32)
        sg = _sigmoid(g_ref[...] + b_ref[...])
        dm = dm_ref[...]
        du = (sg * dm).astype(BF16)
        dpre = dm * u * sg * (1.0 - sg)
        dg_ref[...] = dpre.astype(BF16)
        cols = pl.ds(pl.multiple_of(j * tn, tn), tn)
        db_ref[...] = _rows8(dpre)
        d_w = lax.dot_general(du, yb, _DIMS["tn"], preferred_element_type=F32)
        d_y = jnp.dot(du, w_ref[...], preferred_element_type=F32)

        @pl.when(i == 0)
        def _():
            dw_ref[cols, :] = d_w

        @pl.when(i > 0)
        def _():
            dw_ref[cols, :] += d_w

        @pl.when(j == 0)
        def _():
            acc_ref[...] = d_y

        @pl.when(j > 0)
        def _():
            acc_ref[...] += d_y

        @pl.when(j == nj - 1)
        def _():
            dy_ref[...] = acc_ref[...]

    rows = pl.BlockSpec((tm, width), lambda i, j: (i, 0))
    blk = pl.BlockSpec((tm, tn), lambda i, j: (i, j))
    gate = pl.BlockSpec((tm, tn), lambda i, j: (i, gblk + j))
    args = [y, w, z, b_gate, dmerged]
    in_specs = [rows, pl.BlockSpec((tn, width), lambda i, j: (j, 0)), gate,
                pl.BlockSpec((1, tn), lambda i, j: (0, branch * nj + j)), blk]
    aliases = {}
    if not isinstance(dz, jax.ShapeDtypeStruct):
        body = _drop_ref(body, len(args))
        aliases = {len(args): 2}
        args.append(dz)
        in_specs.append(ANY)
    return _call(
        body, args, in_specs, after, name=name, grid=(S_ // tm, nj),
        out_specs=(rows, pl.BlockSpec((D, width), lambda i, j: (0, 0)), gate, pl.BlockSpec((8, tn), lambda i, j: (i, j))),
        out_shape=(jax.ShapeDtypeStruct((S_, width), F32), jax.ShapeDtypeStruct((D, width), F32),
                   jax.ShapeDtypeStruct(dz.shape, dz.dtype), jax.ShapeDtypeStruct((S_ // tm * 8, D), F32)),
        scratch_shapes=[pltpu.VMEM((tm, width), F32)], input_output_aliases=aliases,
        compiler_params=_cparams(("arbitrary", "arbitrary")))


def _relu2_epilogue(acc):
    r = jnp.maximum(acc, 0.0)
    return (r * r,)


def _relu2_bwd_epilogue(acc, r):
    return (acc * (2.0 * jnp.sqrt(r.astype(F32))),)


def _residual_norm_epilogue(acc, xr, gain):
    x = xr + acc
    ms = jnp.mean(x * x, axis=-1, keepdims=True)
    return x, x * lax.rsqrt(ms + EPS) * gain


def _fold_rows(name, part, lanes=False, scale=1.0):
    R, N = part.shape

    def body(p_ref, o_ref):
        s = jnp.sum(p_ref[...], axis=0, keepdims=True)
        if lanes:
            s = jnp.sum(s, axis=1, keepdims=True)
        o_ref[...] = s * scale

    out_n = 1 if lanes else N
    return pl.pallas_call(
        body, name=name, grid=(1,), in_specs=[pl.BlockSpec((R, N), lambda i: (0, 0))],
        out_specs=pl.BlockSpec((1, out_n), lambda i: (0, 0)), out_shape=jax.ShapeDtypeStruct((1, out_n), F32))(part)


def _rows8(v):
    return jnp.sum(v.reshape(v.shape[0] // 8, 8, v.shape[1]), axis=0)


def _residual_loss_epilogue(acc, xr, target):
    e = xr + acc - target
    return e * (1.0 / D), _rows8(e * e)


def _norm_bwd_epilogue(acc, x, dres, gain):
    rstd = lax.rsqrt(jnp.mean(x * x, axis=-1, keepdims=True) + EPS)
    xhat = x * rstd
    dxhat = acc * gain
    c = jnp.mean(dxhat * xhat, axis=-1, keepdims=True)
    return dres + rstd * (dxhat - xhat * c), _rows8(acc * xhat)


def _layer_fwd(l, x, h, p, next_gain, target=None, early=None):
    t = f"l{l}_"
    z = _matmul(t + "in_proj", h, p["w_in_t"], "nt", 512, 3712, 1024, after=early)
    y_pool = _pool_fwd(t + "pool", z, p["pool_mix"], p["pool_scale"])
    os_, ls_ = [], []
    for g, (_, dil) in enumerate(ATTN_GROUPS):
        o_g, l_g = _attn_fwd(t + f"attn{g}", z, p["q_gain"], p["k_gain"], g, dil)
        os_.append(o_g)
        ls_.append(l_g)
    y_attn, lse = _attn_combine(t + "attn_mix", os_, ls_)
    y_conv = _conv_fwd(t + "conv", z, p["conv_w"], after=y_attn)
    merged = _mix_fwd(t + "merge", z, p["b_gate"], (y_pool, y_conv, y_attn),
                      (p["w_pool_up_t"], p["w_conv_out_t"], p["w_attn_up_t"]))
    x1, h2 = _matmul(t + "out_proj", merged, p["w_o"], "nn", 1024, D, 1024, out_dtypes=(F32, BF16),
                     extras=((x, "mn"), (p["norm_mlp"], "n")), epilogue=_residual_norm_epilogue)
    r = _matmul(t + "ff1", h2, p["w_ff1_t"], "nt", 1024, 1024, 1024, out_dtypes=(BF16,), epilogue=_relu2_epilogue)
    if next_gain is None:
        x2, h_out = _matmul(t + "ff2", r, p["w_ff2"], "nn", 512, D, D_FF, out_dtypes=(F32, (F32, "rows8")),
                            extras=((x1, "mn"), (target, "mn")), epilogue=_residual_loss_epilogue)
    else:
        x2, h_out = _matmul(t + "ff2", r, p["w_ff2"], "nn", 512, D, D_FF, out_dtypes=(F32, BF16),
                            extras=((x1, "mn"), (next_gain, "n")), epilogue=_residual_norm_epilogue)
    saved = dict(x=x, h=h, z=z, y_pool=y_pool, y_conv=y_conv, y_attn=y_attn, lse=lse, merged=merged,
                 x1=x1, h2=h2, r=r)
    return x2, h_out, saved


def _layer_bwd(l, dx2, p, s, pending, collective_id, start_after=None):
    t = f"l{l}_b_"
    g = {}
    rest = jax.ShapeDtypeStruct((N_DEV, REST_ROWS, D), F32)
    da = _matmul(t + "d_ff2_in", dx2, p["w_ff2"], "nt", 1024, 1024, 1024, out_dtypes=(BF16,),
                 extras=((s["r"], "mn"),), epilogue=_relu2_bwd_epilogue,
                 after=start_after)
    rest = _matmul(t + "dw_ff2", s["r"], dx2, "tn", 512, 1024, 4096, into=(rest, REST_FF2, 0), after=da)
    tok = rest if pending is None else pending.chip_sums(after=rest)
    dx1, part = _matmul(t + "d_ff1_in", da, p["w_ff1_t"], "nn", 512, D, D_FF, out_dtypes=(F32, (F32, "rows8")),
                        extras=((s["x1"], "mn"), (dx2, "mn"), (p["norm_mlp"], "n")), epilogue=_norm_bwd_epilogue,
                        after=tok)
    g["norm_mlp"] = _fold_rows(t + "d_norm_mlp", part)
    rest = _matmul(t + "dw_ff1", da, s["h2"], "tn", 512, 1024, 4096, into=(rest, REST_FF1, 0), after=dx1)
    dmerged = _matmul(t + "d_out_proj_in", dx1, p["w_o"], "nt", 1024, 1024, 1024)
    rest = _matmul(t + "dw_o", s["merged"], dx1, "tn", 128, 1024, 4096, into=(rest, REST_WO, 0), after=dmerged)
    ys = (s["y_pool"], s["y_conv"], s["y_attn"])
    names = ("w_pool_up_t", "w_conv_out_t", "w_attn_up_t")
    dys, dbs = [], []
    tok = rest
    dz = jax.ShapeDtypeStruct((dx2.shape[0], IN_COLS), BF16)
    for b in range(3):
        dy_b, dw_t, dz, db = _mix_bwd(t + f"merge{b}", s["z"], p["b_gate"], ys[b], p[names[b]], dmerged, b, dz,
                                      after=tok)
        tok = dy_b
        width = ys[b].shape[1]
        if b < 2:
            rest = rest.at[:, REST_UP:REST_ATTN, b * width:(b + 1) * width].set(
                dw_t.reshape(N_DEV, REST_ATTN - REST_UP, width))
        else:
            rest = rest.at[:, REST_ATTN:REST_ROWS, :].set(dw_t.reshape(N_DEV, REST_ROWS - REST_ATTN, D))
        dys.append(dy_b)
        dbs.append(_fold_rows(t + f"d_b_gate{b}", db))
    g["b_gate"] = jnp.concatenate(dbs, axis=1)
    rs_rest = _ReduceScatter(f"rs_rest{l}", collective_id, rest)
    pending_sum = None if pending is None else pending.finish(after=rest)
    dz, g["pool_mix"], g["pool_scale"] = _pool_bwd(t + "pool", s["z"], dys[0], p["pool_mix"], p["pool_scale"], dz,
                                                   after=rest if pending is None else pending_sum[0])
    dcb, dcc, dcx, g["conv_w"] = _conv_bwd(t + "conv", s["z"], dys[1], p["conv_w"], after=dz)
    tok = rs_rest.chip_sums(after=dcb)
    dqs, dks, dvs = [], [], []
    gq = gk = None
    for gi, (_, dil) in enumerate(ATTN_GROUPS):
        dq, dk, dv, dgq, dgk = _attn_bwd(t + f"attn{gi}", s["z"], p["q_gain"], p["k_gain"], dys[2], s["y_attn"],
                                         s["lse"], gi, dil, after=tok)
        tok = dq
        dqs.append(dq)
        dks.append(dk)
        dvs.append(dv)
        gq = dgq if gq is None else gq + dgq
        gk = dgk if gk is None else gk + dgk
    g["q_gain"], g["k_gain"] = gq, gk
    rest_sum, _ = rs_rest.finish(after=tok)
    col = OFF_CB
    for piece in [dcb, dcc, dcx] + dqs + dks + dvs:
        dz = lax.dynamic_update_slice(dz, piece.astype(BF16), (0, col))
        col += piece.shape[1]
    in_t = _matmul(t + "dw_in", dz, s["h"], "tn", 256, 1024, 4096, after=rest_sum)
    rs_in = _ReduceScatter(f"rs_in{l}", collective_id + 2, in_t.reshape(N_DEV, IN_ROWS, D))
    dx, part = _matmul(t + "d_in_proj_in", dz, p["w_in_t"], "nn", 256, D, IN_COLS, out_dtypes=(F32, (F32, "rows8")),
                       extras=((s["x"], "mn"), (dx1, "mn"), (p["norm_mix"], "n")), epilogue=_norm_bwd_epilogue,
                       after=in_t)
    g["norm_mix"] = _fold_rows(t + "d_norm_mix", part)
    return dx, g, rest_sum, pending_sum, rs_in


ANY = pl.BlockSpec(memory_space=pl.ANY)


def _mesh_pos():
    return lax.axis_index("x"), lax.axis_index("y"), lax.axis_index("c")


def _other_chips(x, y):
    return [(1 - x, y), (x, 1 - y), (1 - x, 1 - y)]


def _comm_call(name, collective_id, peers, body, arrs, out_shape, sem_counts, after=None):
    n_in, n_out = len(arrs), len(out_shape)
    if collective_id is None:
        def tc_body(*refs):
            body(refs[:n_in], refs[n_in:n_in + n_out], *refs[n_in + n_out:])

        return pl.pallas_call(
            tc_body, name=name, out_shape=tuple(out_shape), in_specs=[ANY] * n_in, out_specs=(ANY,) * n_out,
            scratch_shapes=[pltpu.SemaphoreType.DMA((n,)) for n in sem_counts])(*arrs)

    n_after = 0 if after is None else 1

    def seq_body(*refs):
        barrier = pltpu.get_barrier_semaphore()
        ps = peers(*_mesh_pos())
        for p in ps:
            pl.semaphore_signal(barrier, inc=1, device_id=p, device_id_type=MESH)
        pl.semaphore_wait(barrier, len(ps))
        outs = refs[n_in + n_after:]
        body(refs[:n_in], outs[:n_out], *outs[n_out:])

    return pl.kernel(
        seq_body, out_type=tuple(out_shape), mesh=plsc.ScalarSubcoreMesh(axis_name="seq", num_cores=1), name=name,
        scratch_types=[pltpu.SemaphoreType.DMA((n,)) for n in sem_counts],
        compiler_params=pltpu.CompilerParams(collective_id=collective_id),
    )(*arrs, *([] if after is None else [after]))


def _ordered(name, x, after):
    def body(x_ref, after_ref, o_ref):
        del x_ref, after_ref, o_ref

    return pl.pallas_call(
        body, name=name, out_shape=jax.ShapeDtypeStruct(x.shape, x.dtype), in_specs=[ANY, ANY], out_specs=ANY,
        input_output_aliases={0: 0})(x, after)


def _all_gather(name, collective_id, shard, after=None):
    R, C = shard.shape

    def peers(x, y, c):
        return [(x, y, 1 - c)] + [(*chip, c) for chip in _other_chips(x, y)]

    def body(in_refs, out_refs, send_sems, recv_sems, local_sems):
        (x_ref,), (out_ref,) = in_refs, out_refs
        x, y, c = _mesh_pos()
        me, sibling = (x, y, c), (x, y, 1 - c)
        chips = _other_chips(x, y)

        def slot(px, py, pc):
            return out_ref.at[4 * px + 2 * py + pc]

        def copy(k, block, to, src=None):
            return pltpu.make_async_remote_copy(
                src_ref=slot(*block) if src is None else src, dst_ref=slot(*block),
                send_sem=send_sems.at[k], recv_sem=recv_sems.at[k], device_id=to, device_id_type=MESH)

        mine = pltpu.make_async_copy(x_ref, slot(*me), local_sems.at[0])
        mine.start()
        first = [copy(0, me, sibling, src=x_ref)]
        first += [copy(1 + j, me, (*chip, c), src=x_ref) for j, chip in enumerate(chips)]
        for cp in first:
            cp.start()
        passed = [copy(4 + j, (*chip, c), sibling) for j, chip in enumerate(chips)]
        for j, chip in enumerate(chips):
            copy(1 + j, (*chip, c), me).wait_recv()
            passed[j].start()
        copy(0, sibling, me).wait_recv()
        for j, chip in enumerate(chips):
            copy(4 + j, (*chip, 1 - c), me).wait_recv()
        for cp in first + passed:
            cp.wait_send()
        mine.wait()

    return _comm_call(name, collective_id, peers, body, [shard],
                      [jax.ShapeDtypeStruct((N_DEV, R, C), shard.dtype)], (7, 7, 1), after)[0]


def _rs_sibling_exchange(name, collective_id, arrs):
    n = len(arrs)

    def body(in_refs, out_refs, send_sems, recv_sems):
        x, y, c = _mesh_pos()
        cps = []
        for k, (src, dst) in enumerate(zip(in_refs, out_refs)):
            src = src.at[:, 1 - c] if len(src.shape) == 4 else src
            cps.append(pltpu.make_async_remote_copy(src_ref=src, dst_ref=dst, send_sem=send_sems.at[k],
                                                    recv_sem=recv_sems.at[k], device_id=(x, y, 1 - c),
                                                    device_id_type=MESH))
        for cp in cps:
            cp.start()
        for cp in cps:
            cp.wait()

    out_shape = [jax.ShapeDtypeStruct(a.shape[:1] + a.shape[2:] if a.ndim == 4 else a.shape, a.dtype) for a in arrs]
    return _comm_call(name, collective_id, lambda x, y, c: [(x, y, 1 - c)], body, arrs, out_shape, (n, n))


def _rs_rows(r):
    return r // 2 if (r // 2) % 16 == 0 else r


def _drop_ref(body, idx):
    def wrapped(*refs):
        return body(*refs[:idx], *refs[idx + 1:])

    return wrapped


def _rs_chip_sum(name, ids, big, rbig, after=None):
    _, _, R, C = big.shape
    rows = _rs_rows(R)

    def body(ids_ref, mine_ref, theirs_ref, t16_ref, own_ref):
        p = pl.program_id(1)
        s = mine_ref[...] + theirs_ref[...]
        t16_ref[...] = s.astype(BF16)

        @pl.when(p == ids_ref[1])
        def _():
            own_ref[...] = s

    in_specs = [pl.BlockSpec((None, None, rows, C), lambda i, p, ids: (p, ids[0], i, 0)),
                pl.BlockSpec((None, rows, C), lambda i, p, ids: (p, i, 0))]
    args = [ids, big, rbig]
    if after is not None:
        body = _drop_ref(body, len(args))
        in_specs.append(ANY)
        args.append(after)
    grid_spec = pltpu.PrefetchScalarGridSpec(
        num_scalar_prefetch=1, grid=(R // rows, 4), in_specs=in_specs,
        out_specs=(pl.BlockSpec((None, rows, C), lambda i, p, ids: (p, i, 0)),
                   pl.BlockSpec((rows, C), lambda i, p, ids: (i, 0))))
    return pl.pallas_call(
        body, name=name, grid_spec=grid_spec,
        out_shape=(jax.ShapeDtypeStruct((4, R, C), BF16), jax.ShapeDtypeStruct((R, C), F32)),
        compiler_params=_cparams(("parallel", "arbitrary")),
    )(*args)


def _add2(name, a, b):
    def body(a_ref, b_ref, o_ref):
        o_ref[...] = a_ref[...] + b_ref[...]

    full = pl.BlockSpec(a.shape, lambda i: (0, 0))
    return pl.pallas_call(body, name=name, grid=(1,), in_specs=[full, full], out_specs=full,
                          out_shape=jax.ShapeDtypeStruct(a.shape, a.dtype))(a, b)


def _rs_chip_exchange(name, collective_id, arrs):
    n = len(arrs)

    def body(in_refs, out_refs, send_sems, recv_sems, local_sems):
        x, y, c = _mesh_pos()
        p_me = 2 * x + y
        chips = _other_chips(x, y)

        def part(ref, p):
            return ref.at[p] if len(ref.shape) == 3 else ref

        local = [pltpu.make_async_copy(part(src, p_me), dst.at[p_me], local_sems.at[k])
                 for k, (src, dst) in enumerate(zip(in_refs, out_refs))]
        for cp in local:
            cp.start()
        sends = []
        for j, (px, py) in enumerate(chips):
            for k, (src, dst) in enumerate(zip(in_refs, out_refs)):
                sends.append(pltpu.make_async_remote_copy(
                    src_ref=part(src, 2 * px + py), dst_ref=dst.at[p_me], send_sem=send_sems.at[n * j + k],
                    recv_sem=recv_sems.at[n * j + k], device_id=(px, py, c), device_id_type=MESH))
        for cp in sends:
            cp.start()
        for j, (px, py) in enumerate(chips):
            for k, (src, dst) in enumerate(zip(in_refs, out_refs)):
                pltpu.make_async_remote_copy(
                    src_ref=part(src, p_me), dst_ref=dst.at[2 * px + py], send_sem=send_sems.at[n * j + k],
                    recv_sem=recv_sems.at[n * j + k], device_id=(px, py, c), device_id_type=MESH).wait_recv()
        for cp in sends:
            cp.wait_send()
        for cp in local:
            cp.wait()

    out_shape = [jax.ShapeDtypeStruct((4,) + a.shape[-2:], a.dtype) for a in arrs]
    return _comm_call(name, collective_id, lambda x, y, c: [(*chip, c) for chip in _other_chips(x, y)], body, arrs,
                      out_shape, (3 * n, 3 * n, n))


def _rs_final_sum(name, ids, recv, own, rows, after=None):
    _, R, C = recv.shape
    assert R % rows == 0

    def body(ids_ref, r_ref, own_ref, o_ref):
        acc = None
        for p in range(4):
            term = jnp.where(ids_ref[1] == p, own_ref[...], r_ref[p].astype(F32))
            acc = term if acc is None else acc + term
        o_ref[...] = acc

    in_specs = [pl.BlockSpec((4, rows, C), lambda i, ids: (0, i, 0)), pl.BlockSpec((rows, C), lambda i, ids: (i, 0))]
    args = [ids, recv, own]
    if after is not None:
        body = _drop_ref(body, len(args))
        in_specs.append(ANY)
        args.append(after)
    grid_spec = pltpu.PrefetchScalarGridSpec(
        num_scalar_prefetch=1, grid=(R // rows,), in_specs=in_specs,
        out_specs=pl.BlockSpec((rows, C), lambda i, ids: (i, 0)))
    return pl.pallas_call(
        body, name=name, grid_spec=grid_spec, out_shape=jax.ShapeDtypeStruct((R, C), F32),
        compiler_params=_cparams(("parallel",)),
    )(*args)


class _ReduceScatter:
    def __init__(self, tag, collective_id, big, small=None):
        x, y, c = _mesh_pos()
        self.tag, self.cid, self.small = tag, collective_id, small
        self.ids = jnp.stack([c, 2 * x + y]).astype(jnp.int32)
        self.big = big.reshape((4, 2) + big.shape[1:])
        self.got = _rs_sibling_exchange(tag + "_sibling", collective_id,
                                        [self.big] + ([] if small is None else [small]))

    def chip_sums(self, after=None):
        t16, self.own = _rs_chip_sum(self.tag + "_chip_sum", self.ids, self.big, self.got[0], after)
        arrs = [t16]
        if self.small is not None:
            self.ts = _add2(self.tag + "_chip_sum_small", self.small, self.got[1])
            arrs.append(self.ts)
        self.recv = _rs_chip_exchange(self.tag + "_chips", self.cid + 1, arrs)
        return t16

    def finish(self, after=None):
        out = _rs_final_sum(self.tag + "_final", self.ids, self.recv[0], self.own, _rs_rows(self.own.shape[0]), after)
        out_small = None
        if self.small is not None:
            out_small = _rs_final_sum(self.tag + "_final_small", self.ids, self.recv[1], self.ts, self.small.shape[0])
        return out, out_small


def _all_reduce_small(tag, small):
    x, y, c = _mesh_pos()
    ids = jnp.stack([c, 2 * x + y]).astype(jnp.int32)
    (theirs,) = _rs_sibling_exchange(tag + "_sibling", None, [small])
    ts = _add2(tag + "_chip_sum", small, theirs)
    (recv,) = _rs_chip_exchange(tag + "_chips", None, [ts])
    return _rs_final_sum(tag + "_final", ids, recv, ts, small.shape[0])


def _adamw(name, w, g, m, v):
    R, C = w.shape
    tiles = [t for t in range(8, 513, 8) if R % t == 0]
    tr = max(tiles) if tiles else R
    c1 = 1.0 - ADAM_B1 ** ADAM_STEP
    c2 = 1.0 - ADAM_B2 ** ADAM_STEP

    def body(w_ref, g_ref, m_ref, v_ref, d_ref, nm_ref, nv_ref):
        gv = g_ref[...]
        nm = ADAM_B1 * m_ref[...] + (1.0 - ADAM_B1) * gv
        nv = ADAM_B2 * v_ref[...] + (1.0 - ADAM_B2) * (gv * gv)
        d_ref[...] = -ADAM_LR * ((nm / c1) / (jnp.sqrt(nv / c2) + ADAM_EPS) + ADAM_WD * w_ref[...])
        nm_ref[...] = nm
        nv_ref[...] = nv

    blk = pl.BlockSpec((tr, C), lambda i: (i, 0))
    shp = jax.ShapeDtypeStruct((R, C), F32)
    return pl.pallas_call(
        body, name=name, grid=(R // tr,), in_specs=[blk] * 4, out_specs=(blk,) * 3, out_shape=(shp,) * 3,
        compiler_params=_cparams(("parallel",)),
    )(w, g, m, v)


def _adamw_t(name, w, g_t, m, v):
    L, R, C = w.shape
    tr = 256
    c1 = 1.0 - ADAM_B1 ** ADAM_STEP
    c2 = 1.0 - ADAM_B2 ** ADAM_STEP

    def body(w_ref, g_ref, m_ref, v_ref, go_ref, d_ref, nm_ref, nv_ref):
        gv = g_ref[...].T
        go_ref[...] = gv
        nm = ADAM_B1 * m_ref[...] + (1.0 - ADAM_B1) * gv
        nv = ADAM_B2 * v_ref[...] + (1.0 - ADAM_B2) * (gv * gv)
        d_ref[...] = -ADAM_LR * ((nm / c1) / (jnp.sqrt(nv / c2) + ADAM_EPS) + ADAM_WD * w_ref[...])
        nm_ref[...] = nm
        nv_ref[...] = nv

    blk = pl.BlockSpec((None, tr, C), lambda l, i: (l, i, 0))
    blk_t = pl.BlockSpec((None, C, tr), lambda l, i: (l, 0, i))
    shp = jax.ShapeDtypeStruct((L, R, C), F32)
    return pl.pallas_call(
        body, name=name, grid=(L, R // tr), in_specs=[blk, blk_t, blk, blk], out_specs=(blk,) * 4,
        out_shape=(shp,) * 4, compiler_params=_cparams(("parallel", "parallel")),
    )(w, g_t, m, v)


REST_FF1 = 0
REST_FF2 = 512
REST_WO = 1024
REST_UP = 1152
REST_ATTN = 1280
REST_ROWS = 1312
IN_ROWS = IN_COLS // N_DEV
CONV_BITS_ROWS = 16
SHARD_ROWS = IN_ROWS + REST_ROWS + CONV_BITS_ROWS
SMALL = (("norm_mix", (DEPTH, D)), ("b_gate", (DEPTH, 3 * D)), ("pool_mix", (DEPTH, 4, POOL_G, POOL_G)),
         ("pool_scale", (DEPTH, POOL_W)), ("conv_w", (DEPTH, 3, CONV_W)), ("q_gain", (DEPTH, HEAD)),
         ("k_gain", (DEPTH, HEAD)), ("norm_mlp", (DEPTH, D)))


def _pack_weight_shards(w, l):
    b = lambda a: a.astype(BF16)
    rest = jnp.concatenate([
        b(w["w_ff1"][l].T), b(w["w_ff2"][l]), b(w["w_o"][l]),
        jnp.concatenate([b(w["w_pool_up"][l].T), b(w["w_conv_out"][l].T)], axis=1),
        b(w["w_attn_up"][l].T).reshape(REST_ROWS - REST_ATTN, D)], axis=0)
    return b(w["w_in"][l].T), rest


def _unpack_gathered(g_in, g_rest, g_conv, small_w, l):
    rest = g_rest
    take = lambda r0, rows, c0=0, cols=D: rest[:, r0:r0 + rows, c0:c0 + cols].reshape(N_DEV * rows, cols)
    conv = g_conv[:, 3 * l:3 * l + 3, :CONV_W // N_DEV]
    p = {
        "w_in_t": g_in.reshape(IN_COLS, D),
        "w_ff1_t": take(REST_FF1, 512), "w_ff2": take(REST_FF2, 512), "w_o": take(REST_WO, 128),
        "w_pool_up_t": take(REST_UP, 128, 0, POOL_W), "w_conv_out_t": take(REST_UP, 128, POOL_W, CONV_W),
        "w_attn_up_t": rest[:, REST_ATTN:].reshape(D, ATTN_OUT),
        "conv_w": jnp.transpose(conv, (1, 0, 2)).reshape(3, CONV_W),
        "pool_mix": small_w["pool_mix"][l],
    }
    for name in ("norm_mix", "b_gate", "pool_scale", "q_gain", "k_gain", "norm_mlp"):
        p[name] = small_w[name][l][None, :]
    return p


def _pack_small_grads(grads):
    flat = jnp.concatenate([jnp.stack([grads[l][name] for l in range(DEPTH)]).reshape(-1) for name, _ in SMALL])
    return jnp.pad(flat, (0, (-flat.shape[0]) % (8 * 128))).reshape(-1, 128)


def _unpack_grads(rest, small, dev):
    out = {
        "w_ff1_t": jnp.stack([a[REST_FF1:REST_FF1 + 512] for a in rest]),
        "w_ff2": jnp.stack([a[REST_FF2:REST_FF2 + 512] for a in rest]),
        "w_o": jnp.stack([a[REST_WO:REST_WO + 128] for a in rest]),
        "w_pool_up": jnp.stack([a[REST_UP:REST_UP + 128, :POOL_W].T for a in rest]),
        "w_conv_out": jnp.stack([a[REST_UP:REST_UP + 128, POOL_W:].T for a in rest]),
        "w_attn_up": jnp.stack([a[REST_ATTN:].reshape(D // N_DEV, ATTN_OUT).T for a in rest]),
    }
    out.update(_unpack_small_like_grads(small, dev))
    return out


def _unpack_small_like_grads(small, dev):
    out = {}
    flat = small.reshape(-1)
    off = 0
    for name, shp in SMALL:
        n = 1
        for s in shp:
            n *= s
        out[name] = flat[off:off + n].reshape(shp)
        off += n
    width = CONV_W // N_DEV
    out["conv_w"] = lax.dynamic_slice_in_dim(out["conv_w"], dev * width, width, axis=2)
    return out


def _pack_small_like_grads(src, dev):
    parts = []
    for name, shp in SMALL:
        a = src[name]
        if name == "conv_w":
            a = lax.dynamic_update_slice_in_dim(jnp.zeros(shp, F32), a, dev * (CONV_W // N_DEV), axis=2)
        parts.append(a.reshape(-1))
    flat = jnp.concatenate(parts)
    return jnp.pad(flat, (0, (-flat.shape[0]) % (8 * 128))).reshape(-1, 128)


WEIGHTS = ("norm_mix", "w_in", "b_gate", "pool_mix", "pool_scale", "conv_w", "q_gain", "k_gain", "w_pool_up",
           "w_conv_out", "w_attn_up", "w_o", "norm_mlp", "w_ff1", "w_ff2")


def kernel(x, norm_mix, w_in, b_gate, pool_mix, pool_scale, conv_w, q_gain, k_gain, w_pool_up, w_conv_out, w_attn_up, w_o, norm_mlp, w_ff1, w_ff2, loss_target, m_norm_mix, m_w_in, m_b_gate, m_pool_mix, m_pool_scale, m_conv_w, m_q_gain, m_k_gain, m_w_pool_up, m_w_conv_out, m_w_attn_up, m_w_o, m_norm_mlp, m_w_ff1, m_w_ff2, v_norm_mix, v_w_in, v_b_gate, v_pool_mix, v_pool_scale, v_conv_w, v_q_gain, v_k_gain, v_w_pool_up, v_w_conv_out, v_w_attn_up, v_w_o, v_norm_mlp, v_w_ff1, v_w_ff2):
    w = dict(zip(WEIGHTS, (norm_mix, w_in, b_gate, pool_mix, pool_scale, conv_w, q_gain, k_gain, w_pool_up,
                           w_conv_out, w_attn_up, w_o, norm_mlp, w_ff1, w_ff2)))
    m = dict(zip(WEIGHTS, (m_norm_mix, m_w_in, m_b_gate, m_pool_mix, m_pool_scale, m_conv_w, m_q_gain, m_k_gain,
                           m_w_pool_up, m_w_conv_out, m_w_attn_up, m_w_o, m_norm_mlp, m_w_ff1, m_w_ff2)))
    v = dict(zip(WEIGHTS, (v_norm_mix, v_w_in, v_b_gate, v_pool_mix, v_pool_scale, v_conv_w, v_q_gain, v_k_gain,
                           v_w_pool_up, v_w_conv_out, v_w_attn_up, v_w_o, v_norm_mlp, v_w_ff1, v_w_ff2)))
    xi, yi, ci = _mesh_pos()
    dev = 4 * xi + 2 * yi + ci

    saved, params = [], []
    act = x[0]
    w_in_t, m_in_t, v_in_t = (jnp.swapaxes(src["w_in"], 1, 2) for src in (w, m, v))
    h = _rmsnorm_fwd("l0_norm_mix", act, w["norm_mix"][0][None])
    conv_shard = jnp.pad(w["conv_w"].reshape(DEPTH * 3, CONV_W // N_DEV), ((0, 8 - DEPTH * 3), (0, 128 - CONV_W // N_DEV)))
    g_conv = _all_gather("gather_conv", None, conv_shard)
    for l in range(DEPTH):
        s_in, s_rest = _pack_weight_shards(w, l)
        g_in = _all_gather(f"gather_in{l}", 1 + 2 * l, s_in)
        g_rest = _all_gather(f"gather_rest{l}", 2 + 2 * l, s_rest)
        params.append(_unpack_gathered(g_in, g_rest, g_conv, w, l))
    for l in range(DEPTH):
        act, h, s = _layer_fwd(l, act, h, params[l], w["norm_mix"][l + 1][None] if l + 1 < DEPTH else None,
                               loss_target[0])
        saved.append(s)
    dy = act
    loss = _fold_rows("loss_sum", h, lanes=True, scale=0.5 / D)
    total = lax.psum(loss[0, 0], ("x", "y", "c"))
    in_t, rest, grads, pending = [None] * DEPTH, [None] * DEPTH, [None] * DEPTH, None
    start = total.reshape(1, 1)
    for l in reversed(range(DEPTH)):
        dy, grads[l], rest[l], done, pending = _layer_bwd(l, dy, params[l], saved[l], pending, 5 + 4 * l, start)
        start = None
        if done is not None:
            in_t[l + 1] = done[0]
    dx = dy
    g_small = _all_reduce_small("ar_small", _pack_small_grads(grads))
    pending.chip_sums(after=g_small)
    g = _unpack_grads(rest, g_small, dev)

    delta, new_m, new_v = {}, {}, {}
    g["w_ff1"], delta["w_ff1"], new_m["w_ff1"], new_v["w_ff1"] = _adamw_t("adamw_w_ff1", w["w_ff1"], g["w_ff1_t"],
                                                                          m["w_ff1"], v["w_ff1"])
    for name in ("w_pool_up", "w_conv_out", "w_attn_up", "w_o", "w_ff2"):
        shp = w[name].shape
        two_d = (shp[0] * shp[1], shp[2])
        d_, m_, v_ = _adamw("adamw_" + name, w[name].reshape(two_d), g[name].reshape(two_d),
                            m[name].reshape(two_d), v[name].reshape(two_d))
        delta[name], new_m[name], new_v[name] = d_.reshape(shp), m_.reshape(shp), v_.reshape(shp)
    for name, shp in SMALL:
        shp = w[name].shape
        two_d = (-1, shp[-1])
        d_, m_, v_ = _adamw("adamw_" + name, w[name].reshape(two_d), g[name].reshape(two_d),
                            m[name].reshape(two_d), v[name].reshape(two_d))
        delta[name], new_m[name], new_v[name] = d_.reshape(shp), m_.reshape(shp), v_.reshape(shp)
    in_t[0], _ = pending.finish(after=new_v[SMALL[-1][0]])
    g_t = jnp.stack(in_t)
    two_d = (g_t.shape[0] * g_t.shape[1], g_t.shape[2])
    d_, m_, v_ = _adamw("adamw_w_in", *[a.reshape(two_d) for a in (w_in_t, g_t, m_in_t, v_in_t)])
    g["w_in"] = jnp.swapaxes(g_t, 1, 2)
    delta["w_in"], new_m["w_in"], new_v["w_in"] = (jnp.swapaxes(a.reshape(g_t.shape), 1, 2) for a in (d_, m_, v_))

    return (total, dx[None], *[g[n] for n in WEIGHTS], *[delta[n] for n in WEIGHTS],
            *[new_m[n] for n in WEIGHTS], *[new_v[n] for n in WEIGHTS])
```

```python
import functools

import jax
import jax.numpy as jnp
from jax import lax
from jax.experimental import pallas as pl
from jax.experimental.pallas import tpu as pltpu
from jax.experimental.pallas import tpu_sc as plsc

F32 = jnp.float32
BF16 = jnp.bfloat16
MESH = pl.DeviceIdType.MESH

D = 1024
SEQ = 4096
DEPTH = 2
N_DEV = 8
POOL_WINDOWS = (2, 4, 8, 16)
POOL_W = 512
POOL_G = 128
CONV_W = 512
HEAD = 64
ATTN_GROUPS = ((128, 1), (512, 4), (2048, 16))
HPG = 4
ATTN_W = 768
ATTN_OUT = 256
ATTN_BLK = 128
D_FF = 4096
EPS = 1e-6
MASK = -1e30
OFF_POOL = 0
OFF_CB = 512
OFF_CC = 1024
OFF_CX = 1536
OFF_Q = 2048
OFF_K = 2816
OFF_V = 3584
OFF_GATE = 4352
IN_COLS = 7424
ADAM_LR = 0.001
ADAM_B1 = 0.9
ADAM_B2 = 0.999
ADAM_EPS = 1e-08
ADAM_WD = 0.01
ADAM_STEP = 10

ROW_TILE = 512
SEQ_CHUNK = 256
HALO = 16
VMEM_LIMIT = 56 * 1024 * 1024


def _cparams(sem=None):
    return pltpu.CompilerParams(dimension_semantics=sem, vmem_limit_bytes=VMEM_LIMIT)


def _call(body, args, in_specs, after=None, **kw):
    if after is not None:
        after = list(after) if isinstance(after, (list, tuple)) else [after]
        n, k, inner = len(args), len(after), body

        def body(*refs):
            return inner(*refs[:n], *refs[n + k:])

        args = list(args) + after
        in_specs = list(in_specs) + [pl.BlockSpec(memory_space=pl.ANY)] * k
    return pl.pallas_call(body, in_specs=in_specs, **kw)(*args)


_DIMS = {"nn": (((1,), (0,)), ((), ())), "nt": (((1,), (1,)), ((), ())), "tn": (((0,), (0,)), ((), ()))}


def _matmul(name, a, b, mode, tm, tn, tk, out_dtypes=(F32,), extras=(), epilogue=None, into=None, after=None):
    if mode == "tn":
        K, M = a.shape
    else:
        M, K = a.shape
    N = b.shape[0] if mode == "nt" else b.shape[1]
    assert M % tm == 0 and N % tn == 0 and K % tk == 0, (name, M, N, K, tm, tn, tk)
    nk = K // tk
    n_extra = len(extras)
    n_out = len(out_dtypes)
    dims = _DIMS[mode]
    n_alias = 0 if into is None or isinstance(into[0], jax.ShapeDtypeStruct) else 1

    def body(a_ref, b_ref, *rest):
        extra_refs = rest[:n_extra]
        out_refs = rest[n_extra + n_alias:n_extra + n_alias + n_out]

        def finish(acc):
            if epilogue is None:
                res = (acc,)
            else:
                res = epilogue(acc, *[r[...] for r in extra_refs])
            for o_ref, v in zip(out_refs, res):
                o_ref[...] = v.astype(o_ref.dtype)

        part = lax.dot_general(a_ref[...].astype(BF16), b_ref[...].astype(BF16), dims,
                               preferred_element_type=F32)
        if nk == 1:
            finish(part)
        else:
            acc_ref = rest[-1]
            k = pl.program_id(2)

            @pl.when(k == 0)
            def _():
                acc_ref[...] = part

            @pl.when(k > 0)
            def _():
                acc_ref[...] += part

            @pl.when(k == nk - 1)
            def _():
                finish(acc_ref[...])

    if mode == "tn":
        a_spec = pl.BlockSpec((tk, tm), lambda i, j, k: (k, i))
    else:
        a_spec = pl.BlockSpec((tm, tk), lambda i, j, k: (i, k))
    if mode == "nt":
        b_spec = pl.BlockSpec((tn, tk), lambda i, j, k: (j, k))
    else:
        b_spec = pl.BlockSpec((tk, tn), lambda i, j, k: (k, j))
    in_specs = [a_spec, b_spec]
    args = [a, b]
    for arr, kind in extras:
        if kind == "mn":
            in_specs.append(pl.BlockSpec((tm, tn), lambda i, j, k: (i, j)))
        else:
            in_specs.append(pl.BlockSpec((1, tn), lambda i, j, k: (0, j)))
        args.append(arr)
    part = [isinstance(dt, tuple) for dt in out_dtypes]
    out_shape = tuple(jax.ShapeDtypeStruct((M // tm * 8, N), dt[0]) if p else jax.ShapeDtypeStruct((M, N), dt)
                      for dt, p in zip(out_dtypes, part))
    out_specs = tuple(pl.BlockSpec((8 if p else tm, tn), lambda i, j, k: (i, j)) for p in part)
    aliases = {}
    if into is not None:
        buf, row0, col0 = into
        assert n_out == 1 and (M // N_DEV) % tm == 0 and row0 % tm == 0 and col0 % tn == 0
        per_dev = M // N_DEV // tm
        out_shape = (jax.ShapeDtypeStruct(buf.shape, buf.dtype),)
        out_specs = (pl.BlockSpec((None, tm, tn), lambda i, j, k: (i // per_dev, row0 // tm + i % per_dev,
                                                                   col0 // tn + j)),)
        if not isinstance(buf, jax.ShapeDtypeStruct):
            aliases = {len(args): 0}
            in_specs.append(pl.BlockSpec(memory_space=pl.ANY))
            args.append(buf)
    scratch = [] if nk == 1 else [pltpu.VMEM((tm, tn), F32)]
    res = _call(
        body, args, in_specs, after, name=name, grid=(M // tm, N // tn, nk), out_specs=out_specs,
        out_shape=out_shape, scratch_shapes=scratch, input_output_aliases=aliases,
        compiler_params=_cparams(("parallel", "parallel", "arbitrary")))
    return res if n_out > 1 else res[0]


def _rmsnorm_fwd(name, x, gain, after=None):
    S_, D_ = x.shape

    def body(x_ref, g_ref, h_ref):
        xf = x_ref[...]
        ms = jnp.mean(xf * xf, axis=-1, keepdims=True)
        h_ref[...] = (xf * lax.rsqrt(ms + EPS) * g_ref[...]).astype(BF16)

    return _call(
        body, [x, gain], [pl.BlockSpec((ROW_TILE, D_), lambda i: (i, 0)), pl.BlockSpec((1, D_), lambda i: (0, 0))],
        after, name=name, grid=(S_ // ROW_TILE,),
        out_specs=pl.BlockSpec((ROW_TILE, D_), lambda i: (i, 0)),
        out_shape=jax.ShapeDtypeStruct((S_, D_), BF16),
        compiler_params=_cparams(("parallel",)))


def _rmsnorm_bwd(name, x, gain, dh, dres, after=None):
    S_, D_ = x.shape
    n = S_ // ROW_TILE

    def body(x_ref, g_ref, dh_ref, dres_ref, dx_ref, dg_ref, acc_ref):
        i = pl.program_id(0)
        xf = x_ref[...]
        rstd = lax.rsqrt(jnp.mean(xf * xf, axis=-1, keepdims=True) + EPS)
        xhat = xf * rstd
        dhv = dh_ref[...]
        dxhat = dhv * g_ref[...]
        c = jnp.mean(dxhat * xhat, axis=-1, keepdims=True)
        dx_ref[...] = dres_ref[...] + rstd * (dxhat - xhat * c)
        part = jnp.sum((dhv * xhat).reshape(ROW_TILE // 8, 8, D_), axis=0)

        @pl.when(i == 0)
        def _():
            acc_ref[...] = part

        @pl.when(i > 0)
        def _():
            acc_ref[...] += part

        @pl.when(i == n - 1)
        def _():
            dg_ref[...] = jnp.sum(acc_ref[...], axis=0, keepdims=True)

    row = pl.BlockSpec((ROW_TILE, D_), lambda i: (i, 0))
    vec = pl.BlockSpec((1, D_), lambda i: (0, 0))
    return _call(
        body, [x, gain, dh, dres], [row, vec, row, row], after, name=name, grid=(n,), out_specs=(row, vec),
        out_shape=(jax.ShapeDtypeStruct((S_, D_), F32), jax.ShapeDtypeStruct((1, D_), F32)),
        scratch_shapes=[pltpu.VMEM((8, D_), F32)],
        compiler_params=_cparams(("arbitrary",)))


def _loss_head(name, y, target):
    S_, D_ = y.shape
    n = S_ // ROW_TILE

    def body(y_ref, t_ref, dy_ref, l_ref, acc_ref):
        i = pl.program_id(0)
        e = y_ref[...] - t_ref[...]
        dy_ref[...] = e * (1.0 / D_)
        part = jnp.sum((e * e).reshape(ROW_TILE // 8, 8, D_), axis=0)

        @pl.when(i == 0)
        def _():
            acc_ref[...] = part

        @pl.when(i > 0)
        def _():
            acc_ref[...] += part

        @pl.when(i == n - 1)
        def _():
            s = jnp.sum(acc_ref[...], axis=1, keepdims=True)
            l_ref[...] = jnp.sum(s, axis=0, keepdims=True) * (0.5 / D_)

    row = pl.BlockSpec((ROW_TILE, D_), lambda i: (i, 0))
    return pl.pallas_call(
        body, name=name, grid=(n,), in_specs=[row, row],
        out_specs=(row, pl.BlockSpec((1, 1), lambda i: (0, 0))),
        out_shape=(jax.ShapeDtypeStruct((S_, D_), F32), jax.ShapeDtypeStruct((1, 1), F32)),
        scratch_shapes=[pltpu.VMEM((8, D_), F32)],
        compiler_params=_cparams(("arbitrary",)),
    )(y, target)


def _rows_with_halo(ref, cols, i, n_chunks, before, after):
    r0 = pl.multiple_of(i * SEQ_CHUNK, SEQ_CHUNK)
    parts = []
    if before:
        h0 = pl.multiple_of(jnp.maximum(r0 - HALO, 0), 8)
        halo = ref[pl.ds(h0, HALO), cols]
        parts.append(jnp.where(i > 0, halo, jnp.zeros_like(halo)))
    parts.append(ref[pl.ds(r0, SEQ_CHUNK), cols])
    if after:
        a0 = pl.multiple_of(jnp.minimum(r0 + SEQ_CHUNK, (n_chunks - 1) * SEQ_CHUNK + SEQ_CHUNK - HALO), 8)
        halo = ref[pl.ds(a0, HALO), cols]
        parts.append(jnp.where(i < n_chunks - 1, halo, jnp.zeros_like(halo)))
    return parts[0] if len(parts) == 1 else jnp.concatenate(parts, axis=0)


def _shift_down(v, k):
    return pltpu.roll(v, k, 0)


def _shift_up(v, k):
    return pltpu.roll(v, v.shape[0] - k, 0)


def _pool_diff(xx, w, t_main):
    s = xx
    k = 1
    while k < w:
        s = s + _shift_down(s, k)
        k *= 2
    cnt = jnp.minimum(t_main + 1, w).astype(F32)
    return s[HALO:] / cnt - xx[HALO:]


def _pool_fwd(name, z, pool_mix, pool_scale):
    S_ = z.shape[0]
    n_chunks = S_ // SEQ_CHUNK

    cols = slice(0, POOL_G)

    def body(u_ref, mix_ref, sc_ref, y_ref):
        mixg = mix_ref[...].astype(BF16)
        scg = sc_ref[...]
        for g, w in enumerate(POOL_WINDOWS):
            @pl.when(pl.program_id(0) == g)
            def _(w=w):
                def chunk(i, carry):
                    r0 = pl.multiple_of(i * SEQ_CHUNK, SEQ_CHUNK)
                    xx = _rows_with_halo(u_ref, cols, i, n_chunks, True, False)
                    t = r0 + lax.broadcasted_iota(jnp.int32, (SEQ_CHUNK, POOL_G), 0)
                    d = _pool_diff(xx, w, t)
                    y = jnp.dot(d.astype(BF16), mixg, preferred_element_type=F32) * scg
                    y_ref[pl.ds(r0, SEQ_CHUNK), :] = y.astype(BF16)
                    return carry

                lax.fori_loop(0, n_chunks, chunk, 0)

    slab = pl.BlockSpec((S_, POOL_G), lambda g: (0, g))
    return pl.pallas_call(
        body, name=name, grid=(len(POOL_WINDOWS),),
        in_specs=[slab, pl.BlockSpec((None, POOL_G, POOL_G), lambda g: (g, 0, 0)),
                  pl.BlockSpec((1, POOL_G), lambda g: (0, g))],
        out_specs=slab, out_shape=jax.ShapeDtypeStruct((S_, POOL_W), BF16),
        compiler_params=_cparams(("parallel",)),
    )(z, pool_mix, pool_scale)


def _pool_bwd(name, z, dy, pool_mix, pool_scale, dz, after=None):
    S_ = z.shape[0]
    n_chunks = S_ // SEQ_CHUNK
    rows_a = SEQ_CHUNK + HALO

    cols = slice(0, POOL_G)

    def body(u_ref, dy_ref, mix_ref, sc_ref, du_ref, dmix_ref, dsc_ref):
        mixg = mix_ref[...].astype(BF16)
        scg = sc_ref[...]
        for g, w in enumerate(POOL_WINDOWS):
            @pl.when(pl.program_id(0) == g)
            def _(w=w):
                def chunk(i, carry):
                    dmix_acc, dsc_acc = carry
                    r0 = pl.multiple_of(i * SEQ_CHUNK, SEQ_CHUNK)
                    xx = _rows_with_halo(u_ref, cols, i, n_chunks, True, False)
                    t = r0 + lax.broadcasted_iota(jnp.int32, (SEQ_CHUNK, POOL_G), 0)
                    d = _pool_diff(xx, w, t).astype(BF16)
                    ypre = jnp.dot(d, mixg, preferred_element_type=F32)
                    dyy = _rows_with_halo(dy_ref, cols, i, n_chunks, False, True)
                    dys = (dyy * scg).astype(BF16)
                    dsc_acc = dsc_acc + jnp.sum((dyy[:SEQ_CHUNK] * ypre).reshape(SEQ_CHUNK // 8, 8, POOL_G), axis=0)
                    dmix_acc = dmix_acc + lax.dot_general(d, dys[:SEQ_CHUNK], _DIMS["tn"],
                                                          preferred_element_type=F32)
                    dd = lax.dot_general(dys, mixg, _DIMS["nt"], preferred_element_type=F32)
                    ta = r0 + lax.broadcasted_iota(jnp.int32, (rows_a, POOL_G), 0)
                    f = dd / jnp.minimum(ta + 1, w).astype(F32)
                    k = 1
                    while k < w:
                        f = f + _shift_up(f, k)
                        k *= 2
                    du_ref[pl.ds(r0, SEQ_CHUNK), :] = (f[:SEQ_CHUNK] - dd[:SEQ_CHUNK]).astype(BF16)
                    return dmix_acc, dsc_acc

                dmix_acc, dsc_acc = lax.fori_loop(
                    0, n_chunks, chunk, (jnp.zeros((POOL_G, POOL_G), F32), jnp.zeros((8, POOL_G), F32)))
                dmix_ref[...] = dmix_acc
                dsc_ref[...] = jnp.sum(dsc_acc, axis=0, keepdims=True)

    slab = pl.BlockSpec((S_, POOL_G), lambda g: (0, g))
    mix_spec = pl.BlockSpec((None, POOL_G, POOL_G), lambda g: (g, 0, 0))
    vec = pl.BlockSpec((1, POOL_G), lambda g: (0, g))
    return _call(
        _drop_ref(body, 4), [z, dy, pool_mix, pool_scale, dz], [slab, slab, mix_spec, vec, ANY], after, name=name,
        grid=(len(POOL_WINDOWS),), out_specs=(slab, mix_spec, vec),
        out_shape=(jax.ShapeDtypeStruct(dz.shape, dz.dtype), jax.ShapeDtypeStruct((4, POOL_G, POOL_G), F32),
                   jax.ShapeDtypeStruct((1, POOL_W), F32)),
        input_output_aliases={4: 0}, compiler_params=_cparams(("parallel",)))


def _conv_specs(S_):
    slab = lambda off: pl.BlockSpec((S_, 128), lambda c, off=off: (0, off // 128 + c))
    return slab(OFF_CB), slab(OFF_CC), slab(OFF_CX)


def _conv_fwd(name, z, conv_w, after=None):
    S_ = z.shape[0]
    n_chunks = S_ // SEQ_CHUNK
    col = slice(0, 128)

    def body(b_ref, c_ref, x_ref, w_ref, y_ref):
        w0, w1, w2 = w_ref[0:1, :], w_ref[1:2, :], w_ref[2:3, :]

        def chunk(i, carry):
            r0 = pl.multiple_of(i * SEQ_CHUNK, SEQ_CHUNK)
            u = _rows_with_halo(c_ref, col, i, n_chunks, True, False) * _rows_with_halo(x_ref, col, i, n_chunks, True, False)
            y = w2 * u + w1 * _shift_down(u, 1) + w0 * _shift_down(u, 2)
            y_ref[pl.ds(r0, SEQ_CHUNK), :] = (b_ref[pl.ds(r0, SEQ_CHUNK), :] * y[HALO:]).astype(BF16)
            return carry

        lax.fori_loop(0, n_chunks, chunk, 0)

    sb, sc, sx = _conv_specs(S_)
    return _call(
        body, [z, z, z, conv_w], [sb, sc, sx, pl.BlockSpec((3, 128), lambda c: (0, c))], after,
        name=name, grid=(CONV_W // 128,),
        out_specs=pl.BlockSpec((S_, 128), lambda c: (0, c)),
        out_shape=jax.ShapeDtypeStruct((S_, CONV_W), BF16),
        compiler_params=_cparams(("parallel",)))


def _conv_bwd(name, z, dy, conv_w, after=None):
    S_ = z.shape[0]
    n_chunks = S_ // SEQ_CHUNK
    col = slice(0, 128)
    lo, hi = HALO, HALO + SEQ_CHUNK

    def body(b_ref, c_ref, x_ref, dy_ref, w_ref, db_ref, dc_ref, dx_ref, dw_ref):
        w0, w1, w2 = w_ref[0:1, :], w_ref[1:2, :], w_ref[2:3, :]

        def chunk(i, carry):
            a0, a1, a2 = carry
            r0 = pl.multiple_of(i * SEQ_CHUNK, SEQ_CHUNK)
            cc = _rows_with_halo(c_ref, col, i, n_chunks, True, True)
            xx = _rows_with_halo(x_ref, col, i, n_chunks, True, True)
            bb = _rows_with_halo(b_ref, col, i, n_chunks, True, True)
            dyy = _rows_with_halo(dy_ref, col, i, n_chunks, True, True)
            u = cc * xx
            u1 = _shift_down(u, 1)
            u2 = _shift_down(u, 2)
            y = w2 * u + w1 * u1 + w0 * u2
            dyv = dyy * bb
            du = w2 * dyv + w1 * _shift_up(dyv, 1) + w0 * _shift_up(dyv, 2)
            db_ref[pl.ds(r0, SEQ_CHUNK), :] = (dyy[lo:hi] * y[lo:hi]).astype(BF16)
            dc_ref[pl.ds(r0, SEQ_CHUNK), :] = (du[lo:hi] * xx[lo:hi]).astype(BF16)
            dx_ref[pl.ds(r0, SEQ_CHUNK), :] = (du[lo:hi] * cc[lo:hi]).astype(BF16)
            red = lambda v: jnp.sum(v.reshape(SEQ_CHUNK // 8, 8, 128), axis=0)
            dm = dyv[lo:hi]
            return a0 + red(dm * u2[lo:hi]), a1 + red(dm * u1[lo:hi]), a2 + red(dm * u[lo:hi])

        zero = jnp.zeros((8, 128), F32)
        a0, a1, a2 = lax.fori_loop(0, n_chunks, chunk, (zero, zero, zero))
        dw_ref[0:1, :] = jnp.sum(a0, axis=0, keepdims=True)
        dw_ref[1:2, :] = jnp.sum(a1, axis=0, keepdims=True)
        dw_ref[2:3, :] = jnp.sum(a2, axis=0, keepdims=True)

    sb, sc, sx = _conv_specs(S_)
    slab = pl.BlockSpec((S_, 128), lambda c: (0, c))
    wspec = pl.BlockSpec((3, 128), lambda c: (0, c))
    act = jax.ShapeDtypeStruct((S_, CONV_W), BF16)
    return _call(
        body, [z, z, z, dy, conv_w], [sb, sc, sx, slab, wspec], after, name=name, grid=(CONV_W // 128,),
        out_specs=(slab, slab, slab, wspec),
        out_shape=(act, act, act, jax.ShapeDtypeStruct((3, CONV_W), F32)),
        compiler_params=_cparams(("parallel",)))


def _head_ones(pw):
    a = lax.broadcasted_iota(jnp.int32, (pw, pw), 0) // HEAD
    b = lax.broadcasted_iota(jnp.int32, (pw, pw), 1) // HEAD
    return (a == b).astype(BF16)


def _head_sum(v, ones):
    hi = v.astype(BF16)
    lo = (v - hi.astype(F32)).astype(BF16)
    return jnp.dot(hi, ones, preferred_element_type=F32) + jnp.dot(lo, ones, preferred_element_type=F32)


def _head_norm(v, gain, ones):
    rstd = lax.rsqrt(_head_sum(v * v, ones) * (1.0 / HEAD) + EPS)
    xhat = v * rstd
    return xhat * gain, xhat, rstd


def _head_norm_bwd(dy, xhat, rstd, gain, ones):
    dxhat = dy * gain
    c = _head_sum(dxhat * xhat, ones) * (1.0 / HEAD)
    dv = rstd * (dxhat - xhat * c)
    dg = jnp.sum((dy * xhat).reshape(dy.shape[0] // 8, 8, dy.shape[1]), axis=0)
    return dv, dg


def _head_masks(pw):
    lane_head = lax.broadcasted_iota(jnp.int32, (1, pw), 1) // HEAD
    return [lane_head == h for h in range(pw // HEAD)]


def _only(mask, v):
    return jnp.where(mask, v, jnp.zeros_like(v))


def _attn_specs(S_, g, dil):
    rows = ATTN_BLK * dil
    nb = S_ // rows
    pw = 128 if dil > 1 else ATTN_OUT
    cq, ck, cv = ((OFF_Q + g * ATTN_OUT) // pw, (OFF_K + g * ATTN_OUT) // pw, (OFF_V + g * ATTN_OUT) // pw)
    return rows, nb, pw, pw // HEAD, ATTN_OUT // pw, cq, ck, cv


ATTN_BATCH = 4


def _attn_group(dil):
    return 4 if dil == 1 else 1


def _attn_block_specs(rows, grp, pw, last=None):
    step = (lambda n: n) if last is None else (lambda n: jnp.minimum(n, last))
    cur = lambda c: pl.BlockSpec((rows * grp, pw), lambda hp, n, c=c: (step(n), c + hp))
    prev = lambda c: pl.BlockSpec((rows, pw), lambda hp, n, c=c: (jnp.maximum(step(n) * grp - 1, 0), c + hp))
    return cur, prev


def _band_mask(has_prev):
    qi = lax.broadcasted_iota(jnp.int32, (ATTN_BLK, 2 * ATTN_BLK), 0)
    ki = lax.broadcasted_iota(jnp.int32, (ATTN_BLK, 2 * ATTN_BLK), 1)
    in_prev = jnp.logical_and(ki < ATTN_BLK, ki >= qi)
    if has_prev is not True:
        in_prev = jnp.logical_and(in_prev, has_prev)
    return jnp.logical_or(in_prev, jnp.logical_and(ki >= ATTN_BLK, ki - ATTN_BLK <= qi))


def _rows_of(ref, r, dil):
    if dil == 1:
        return ref[r * ATTN_BLK:(r + 1) * ATTN_BLK, :]
    return ref[pl.ds(r, ATTN_BLK, stride=dil), :]


def _put_rows(ref, r, dil, val):
    if dil == 1:
        ref[r * ATTN_BLK:(r + 1) * ATTN_BLK, :] = val.astype(ref.dtype)
    else:
        ref[pl.ds(r, ATTN_BLK, stride=dil), :] = val.astype(ref.dtype)


def _attn_fwd(name, z, q_gain, k_gain, g, dil):
    S_ = z.shape[0]
    rows, nb, pw, heads, npairs, cq, ck, cv = _attn_specs(S_, g, dil)
    scale = HEAD ** -0.5

    grp = _attn_group(dil)
    nsteps = nb // grp

    def body(q_ref, k_ref, kp_ref, v_ref, vp_ref, gq_ref, gk_ref, o_ref, l_ref):
        n = pl.program_id(1)
        ones, hmask = _head_ones(pw), _head_masks(pw)
        gq, gk = jnp.tile(gq_ref[...], (1, heads)), jnp.tile(gk_ref[...], (1, heads))
        mask_first, mask_rest = _band_mask(n > 0), _band_mask(True)
        for r0 in range(0, dil * grp, ATTN_BATCH):
            rs = range(r0, min(r0 + ATTN_BATCH, dil * grp))
            qn, kn, vv, s, p = {}, {}, {}, {}, {}
            kcn = {}
            for r in rs:
                q, kc, vc = _rows_of(q_ref, r, dil), _rows_of(k_ref, r, dil), _rows_of(v_ref, r, dil)
                kcn[r] = _head_norm(kc, gk, ones)[0]
                if dil == 1 and r > 0:
                    kpn = kcn[r - 1] if r - 1 in kcn else _head_norm(_rows_of(k_ref, r - 1, dil), gk, ones)[0]
                    vp = _rows_of(v_ref, r - 1, dil)
                else:
                    kpn, vp = _head_norm(_rows_of(kp_ref, r, dil), gk, ones)[0], _rows_of(vp_ref, r, dil)
                qn[r] = _head_norm(q, gq, ones)[0].astype(BF16)
                kn[r] = jnp.concatenate([kpn, kcn[r]], axis=0).astype(BF16)
                vv[r] = jnp.concatenate([vp, vc], axis=0).astype(BF16)
            keys = [(r, h) for r in rs for h in range(heads)]
            for r, h in keys:
                s[r, h] = lax.dot_general(_only(hmask[h], qn[r]), kn[r], _DIMS["nt"],
                                          preferred_element_type=F32) * scale
            lse, den = {}, {}
            for key in keys:
                mask = mask_rest if (dil == 1 and key[0] > 0) else mask_first
                sm = jnp.where(mask, s[key], MASK)
                m = jnp.max(sm, axis=-1, keepdims=True)
                e = jnp.exp(sm - m)
                den[key] = jnp.sum(e, axis=-1, keepdims=True)
                p[key] = e.astype(BF16)
                lse[key] = m + jnp.log(den[key])
            for r in rs:
                out = jnp.zeros((ATTN_BLK, pw), F32)
                lse_all = jnp.zeros((ATTN_BLK, pw), F32)
                for h in range(heads):
                    out = jnp.where(hmask[h], jnp.dot(p[r, h], vv[r], preferred_element_type=F32) / den[r, h], out)
                    lse_all = jnp.where(hmask[h], lse[r, h], lse_all)
                _put_rows(o_ref, r, dil, out)
                _put_rows(l_ref, r, dil, lse_all)

    cur, prev = _attn_block_specs(rows, grp, pw)
    gspec = pl.BlockSpec((1, HEAD), lambda hp, n: (0, 0))
    shp = jax.ShapeDtypeStruct((S_, ATTN_OUT), F32)
    return pl.pallas_call(
        body, name=name, grid=(npairs, nsteps),
        in_specs=[cur(cq), cur(ck), prev(ck), cur(cv), prev(cv), gspec, gspec],
        out_specs=(cur(0), cur(0)), out_shape=(shp, shp),
        compiler_params=_cparams(("parallel", "parallel")),
    )(z, z, z, z, z, q_gain, k_gain)


def _attn_combine(name, os_, ls_):
    S_ = os_[0].shape[0]

    def body(o0, o1, o2, l0, l1, l2, o_ref, l_ref):
        a, b, c = l0[...], l1[...], l2[...]
        m = jnp.maximum(jnp.maximum(a, b), c)
        ea, eb, ec = jnp.exp(a - m), jnp.exp(b - m), jnp.exp(c - m)
        zsum = ea + eb + ec
        o_ref[...] = (ea * o0[...] + eb * o1[...] + ec * o2[...]) / zsum
        l_ref[...] = m + jnp.log(zsum)

    row = pl.BlockSpec((ROW_TILE, ATTN_OUT), lambda i: (i, 0))
    shp = jax.ShapeDtypeStruct((S_, ATTN_OUT), F32)
    return pl.pallas_call(
        body, name=name, grid=(S_ // ROW_TILE,), in_specs=[row] * 6, out_specs=(row, row), out_shape=(shp, shp),
        compiler_params=_cparams(("parallel",)),
    )(*os_, *ls_)


def _attn_bwd(name, z, q_gain, k_gain, do, o, lse, g, dil, after=None):
    S_ = z.shape[0]
    rows, nb, pw, heads, npairs, cq, ck, cv = _attn_specs(S_, g, dil)
    scale = HEAD ** -0.5

    def body(q_ref, kc_ref, kp_ref, vc_ref, vp_ref, gq_ref, gk_ref, do_ref, o_ref, l_ref,
             dq_ref, dk_ref, dv_ref, dgq_ref, dgk_ref, ck_ref, cvv_ref, gq_acc, gk_acc):
        hp = pl.program_id(0)
        n = pl.program_id(1)
        ones = _head_ones(pw)
        gq, gk = jnp.tile(gq_ref[...], (1, heads)), jnp.tile(gk_ref[...], (1, heads))

        @pl.when(n == 0)
        def _():
            ck_ref[...] = jnp.zeros_like(ck_ref)
            cvv_ref[...] = jnp.zeros_like(cvv_ref)

        @pl.when(jnp.logical_and(n == 0, hp == 0))
        def _():
            gq_acc[...] = jnp.zeros_like(gq_acc)
            gk_acc[...] = jnp.zeros_like(gk_acc)

        pl.when(n < nb)(functools.partial(query_step, n, ones, gq, gk, q_ref, kc_ref, kp_ref, vc_ref, vp_ref, do_ref,
                                          o_ref, l_ref, dq_ref, dk_ref, dv_ref, ck_ref, cvv_ref, gq_acc, gk_acc))

        @pl.when(n == nb)
        def _():
            dgk = jnp.zeros((8, pw), F32)
            for r in range(dil):
                _, kphat, kprstd = _head_norm(_rows_of(kp_ref, r, dil), gk, ones)
                dk_all, dg = _head_norm_bwd(_rows_of(ck_ref, r, dil), kphat, kprstd, gk, ones)
                dgk = dgk + dg
                _put_rows(dk_ref, r, dil, dk_all)
                _put_rows(dv_ref, r, dil, _rows_of(cvv_ref, r, dil))
            gk_acc[...] += dgk

        @pl.when(jnp.logical_and(n == nb, hp == npairs - 1))
        def _():
            fold = lambda a: sum(a[:, h * HEAD:(h + 1) * HEAD] for h in range(heads))
            dgq_ref[...] = jnp.sum(fold(gq_acc[...]), axis=0, keepdims=True)
            dgk_ref[...] = jnp.sum(fold(gk_acc[...]), axis=0, keepdims=True)

    def query_step(n, ones, gq, gk, q_ref, kc_ref, kp_ref, vc_ref, vp_ref, do_ref, o_ref, l_ref,
                   dq_ref, dk_ref, dv_ref, ck_ref, cvv_ref, gq_acc, gk_acc):
        mask = _band_mask(n > 0)
        hmask = _head_masks(pw)
        dgq = jnp.zeros((8, pw), F32)
        dgk = jnp.zeros((8, pw), F32)
        for r0 in range(0, dil, ATTN_BATCH):
            rs = range(r0, min(r0 + ATTN_BATCH, dil))
            keys = [(r, h) for r in rs for h in range(heads)]
            qn, qhat, qrstd, kn, kphat, kprstd, vv, dob, delta, lse = ({} for _ in range(10))
            for r in rs:
                q, kc, kp = _rows_of(q_ref, r, dil), _rows_of(kc_ref, r, dil), _rows_of(kp_ref, r, dil)
                dov = _rows_of(do_ref, r, dil)
                qn_f, qhat[r], qrstd[r] = _head_norm(q, gq, ones)
                kpn, kphat[r], kprstd[r] = _head_norm(kp, gk, ones)
                qn[r] = qn_f.astype(BF16)
                kn[r] = jnp.concatenate([kpn, _head_norm(kc, gk, ones)[0]], axis=0).astype(BF16)
                vv[r] = jnp.concatenate([_rows_of(vp_ref, r, dil), _rows_of(vc_ref, r, dil)], axis=0).astype(BF16)
                dob[r] = dov.astype(BF16)
                delta[r] = _head_sum(dov * _rows_of(o_ref, r, dil), ones)
                lse[r] = _rows_of(l_ref, r, dil)
            s, dp = {}, {}
            for r, h in keys:
                s[r, h] = lax.dot_general(_only(hmask[h], qn[r]), kn[r], _DIMS["nt"],
                                          preferred_element_type=F32) * scale
                dp[r, h] = lax.dot_general(_only(hmask[h], dob[r]), vv[r], _DIMS["nt"], preferred_element_type=F32)
            p, ds = {}, {}
            for r, h in keys:
                col = slice(h * HEAD, h * HEAD + 1)
                pk = jnp.where(mask, jnp.exp(jnp.where(mask, s[r, h], MASK) - lse[r][:, col]), 0.0)
                ds[r, h] = (pk * (dp[r, h] - delta[r][:, col]) * scale).astype(BF16)
                p[r, h] = pk.astype(BF16)
            dqn, dkn, dvv = {}, {}, {}
            for r in rs:
                dqn[r] = jnp.zeros((ATTN_BLK, pw), F32)
                dkn[r] = jnp.zeros((2 * ATTN_BLK, pw), F32)
                dvv[r] = jnp.zeros((2 * ATTN_BLK, pw), F32)
                for h in range(heads):
                    dqn[r] = jnp.where(hmask[h], jnp.dot(ds[r, h], kn[r], preferred_element_type=F32), dqn[r])
                    dkn[r] = jnp.where(hmask[h], lax.dot_general(ds[r, h], qn[r], _DIMS["tn"],
                                                                 preferred_element_type=F32), dkn[r])
                    dvv[r] = jnp.where(hmask[h], lax.dot_general(p[r, h], dob[r], _DIMS["tn"],
                                                                 preferred_element_type=F32), dvv[r])
            for r in rs:
                dq_all, dg = _head_norm_bwd(dqn[r], qhat[r], qrstd[r], gq, ones)
                dgq = dgq + dg
                dk_all, dg = _head_norm_bwd(_rows_of(ck_ref, r, dil) + dkn[r][:ATTN_BLK], kphat[r], kprstd[r], gk, ones)
                dgk = dgk + dg
                dv_all = _rows_of(cvv_ref, r, dil) + dvv[r][:ATTN_BLK]
                _put_rows(dq_ref, r, dil, dq_all)
                _put_rows(dk_ref, r, dil, dk_all)
                _put_rows(dv_ref, r, dil, dv_all)
                _put_rows(ck_ref, r, dil, dkn[r][ATTN_BLK:])
                _put_rows(cvv_ref, r, dil, dvv[r][ATTN_BLK:])
        gq_acc[...] += dgq
        gk_acc[...] += dgk

    last = nb - 1
    cur = lambda c: pl.BlockSpec((rows, pw), lambda hp, n, c=c: (jnp.minimum(n, last), c + hp))
    prev = lambda c: pl.BlockSpec((rows, pw), lambda hp, n, c=c: (jnp.maximum(n - 1, 0), c + hp))
    gspec = pl.BlockSpec((1, HEAD), lambda hp, n: (0, 0))
    act = jax.ShapeDtypeStruct((S_, ATTN_OUT), F32)
    vec = jax.ShapeDtypeStruct((1, HEAD), F32)
    return _call(
        body, [z, z, z, z, z, q_gain, k_gain, do, o, lse],
        [cur(cq), cur(ck), prev(ck), cur(cv), prev(cv), gspec, gspec, cur(0), cur(0), cur(0)], after,
        name=name, grid=(npairs, nb + 1),
        out_specs=(cur(0), prev(0), prev(0), gspec, gspec),
        out_shape=(act, act, act, vec, vec),
        scratch_shapes=[pltpu.VMEM((rows, pw), F32), pltpu.VMEM((rows, pw), F32),
                        pltpu.VMEM((8, pw), F32), pltpu.VMEM((8, pw), F32)],
        compiler_params=_cparams(("arbitrary", "arbitrary")))


MIX_TN = 256
MIX_TM = 2048


def _sigmoid(v):
    return 1.0 / (1.0 + jnp.exp(-v))


def _mix_fwd(name, z, b_gate, ys, ws):
    S_ = z.shape[0]
    tm, tn = MIX_TM, MIX_TN
    gblk = OFF_GATE // tn

    def body(yp, yc, ya, wp, wc, wa, g0, g1, g2, b0, b1, b2, m_ref):
        acc = None
        for y_ref, w_ref, g_ref, b_ref in ((yp, wp, g0, b0), (yc, wc, g1, b1), (ya, wa, g2, b2)):
            u = lax.dot_general(y_ref[...].astype(BF16), w_ref[...], _DIMS["nt"], preferred_element_type=F32)
            t = _sigmoid(g_ref[...] + b_ref[...]) * u
            acc = t if acc is None else acc + t
        m_ref[...] = acc.astype(BF16)

    yspec = lambda w: pl.BlockSpec((tm, w), lambda i, j: (i, 0))
    wspec = lambda w: pl.BlockSpec((tn, w), lambda i, j: (j, 0))
    gspec = lambda b: pl.BlockSpec((tm, tn), lambda i, j, b=b: (i, gblk + b * (D // tn) + j))
    bspec = lambda b: pl.BlockSpec((1, tn), lambda i, j, b=b: (0, b * (D // tn) + j))
    return pl.pallas_call(
        body, name=name, grid=(S_ // tm, D // tn),
        in_specs=[yspec(POOL_W), yspec(CONV_W), yspec(ATTN_OUT), wspec(POOL_W), wspec(CONV_W), wspec(ATTN_OUT),
                  gspec(0), gspec(1), gspec(2), bspec(0), bspec(1), bspec(2)],
        out_specs=pl.BlockSpec((tm, tn), lambda i, j: (i, j)),
        out_shape=jax.ShapeDtypeStruct((S_, D), BF16),
        compiler_params=_cparams(("parallel", "parallel")),
    )(*ys, *ws, z, z, z, b_gate, b_gate, b_gate)


def _mix_bwd(name, z, b_gate, y, w, dmerged, branch, dz, after=None):
    S_ = z.shape[0]
    tm, tn = MIX_TM, MIX_TN
    width = y.shape[1]
    gblk = OFF_GATE // tn + branch * (D // tn)
    nj = D // tn

    def body(y_ref, w_ref, g_ref, b_ref, dm_ref, dy_ref, dw_ref, dg_ref, db_ref, acc_ref):
        i, j = pl.program_id(0), pl.program_id(1)
        yb = y_ref[...].astype(BF16)
        u = lax.dot_general(yb, w_ref[...], _DIMS["nt"], preferred_element_type=F32)
        sg = _sigmoid(g_ref[...] + b_ref[...])
        dm = dm_ref[...]
        du = (sg * dm).astype(BF16)
        dpre = dm * u * sg * (1.0 - sg)
        dg_ref[...] = dpre.astype(BF16)
        cols = pl.ds(pl.multiple_of(j * tn, tn), tn)
        db_ref[...] = _rows8(dpre)
        d_w = lax.dot_general(du, yb, _DIMS["tn"], preferred_element_type=F32)
        d_y = jnp.dot(du, w_ref[...], preferred_element_type=F32)

        @pl.when(i == 0)
        def _():
            dw_ref[cols, :] = d_w

        @pl.when(i > 0)
        def _():
            dw_ref[cols, :] += d_w

        @pl.when(j == 0)
        def _():
            acc_ref[...] = d_y

        @pl.when(j > 0)
        def _():
            acc_ref[...] += d_y

        @pl.when(j == nj - 1)
        def _():
            dy_ref[...] = acc_ref[...]

    rows = pl.BlockSpec((tm, width), lambda i, j: (i, 0))
    blk = pl.BlockSpec((tm, tn), lambda i, j: (i, j))
    gate = pl.BlockSpec((tm, tn), lambda i, j: (i, gblk + j))
    args = [y, w, z, b_gate, dmerged]
    in_specs = [rows, pl.BlockSpec((tn, width), lambda i, j: (j, 0)), gate,
                pl.BlockSpec((1, tn), lambda i, j: (0, branch * nj + j)), blk]
    aliases = {}
    if not isinstance(dz, jax.ShapeDtypeStruct):
        body = _drop_ref(body, len(args))
        aliases = {len(args): 2}
        args.append(dz)
        in_specs.append(ANY)
    return _call(
        body, args, in_specs, after, name=name, grid=(S_ // tm, nj),
        out_specs=(rows, pl.BlockSpec((D, width), lambda i, j: (0, 0)), gate, pl.BlockSpec((8, tn), lambda i, j: (i, j))),
        out_shape=(jax.ShapeDtypeStruct((S_, width), F32), jax.ShapeDtypeStruct((D, width), F32),
                   jax.ShapeDtypeStruct(dz.shape, dz.dtype), jax.ShapeDtypeStruct((S_ // tm * 8, D), F32)),
        scratch_shapes=[pltpu.VMEM((tm, width), F32)], input_output_aliases=aliases,
        compiler_params=_cparams(("arbitrary", "arbitrary")))


def _relu2_epilogue(acc):
    r = jnp.maximum(acc, 0.0)
    return (r * r,)


def _relu2_bwd_epilogue(acc, r):
    return (acc * (2.0 * jnp.sqrt(r.astype(F32))),)


def _residual_norm_epilogue(acc, xr, gain):
    x = xr + acc
    ms = jnp.mean(x * x, axis=-1, keepdims=True)
    return x, x * lax.rsqrt(ms + EPS) * gain


def _fold_rows(name, part, lanes=False, scale=1.0):
    R, N = part.shape

    def body(p_ref, o_ref):
        s = jnp.sum(p_ref[...], axis=0, keepdims=True)
        if lanes:
            s = jnp.sum(s, axis=1, keepdims=True)
        o_ref[...] = s * scale

    out_n = 1 if lanes else N
    return pl.pallas_call(
        body, name=name, grid=(1,), in_specs=[pl.BlockSpec((R, N), lambda i: (0, 0))],
        out_specs=pl.BlockSpec((1, out_n), lambda i: (0, 0)), out_shape=jax.ShapeDtypeStruct((1, out_n), F32))(part)


def _rows8(v):
    return jnp.sum(v.reshape(v.shape[0] // 8, 8, v.shape[1]), axis=0)


def _residual_loss_epilogue(acc, xr, target):
    e = xr + acc - target
    return e * (1.0 / D), _rows8(e * e)


def _norm_bwd_epilogue(acc, x, dres, gain):
    rstd = lax.rsqrt(jnp.mean(x * x, axis=-1, keepdims=True) + EPS)
    xhat = x * rstd
    dxhat = acc * gain
    c = jnp.mean(dxhat * xhat, axis=-1, keepdims=True)
    return dres + rstd * (dxhat - xhat * c), _rows8(acc * xhat)


def _layer_fwd(l, x, h, p, next_gain, target=None, early=None):
    t = f"l{l}_"
    z = _matmul(t + "in_proj", h, p["w_in_t"], "nt", 512, 3712, 1024, after=early)
    y_pool = _pool_fwd(t + "pool", z, p["pool_mix"], p["pool_scale"])
    os_, ls_ = [], []
    for g, (_, dil) in enumerate(ATTN_GROUPS):
        o_g, l_g = _attn_fwd(t + f"attn{g}", z, p["q_gain"], p["k_gain"], g, dil)
        os_.append(o_g)
        ls_.append(l_g)
    y_attn, lse = _attn_combine(t + "attn_mix", os_, ls_)
    y_conv = _conv_fwd(t + "conv", z, p["conv_w"], after=y_attn)
    merged = _mix_fwd(t + "merge", z, p["b_gate"], (y_pool, y_conv, y_attn),
                      (p["w_pool_up_t"], p["w_conv_out_t"], p["w_attn_up_t"]))
    x1, h2 = _matmul(t + "out_proj", merged, p["w_o"], "nn", 1024, D, 1024, out_dtypes=(F32, BF16),
                     extras=((x, "mn"), (p["norm_mlp"], "n")), epilogue=_residual_norm_epilogue)
    r = _matmul(t + "ff1", h2, p["w_ff1_t"], "nt", 1024, 1024, 1024, out_dtypes=(BF16,), epilogue=_relu2_epilogue)
    if next_gain is None:
        x2, h_out = _matmul(t + "ff2", r, p["w_ff2"], "nn", 512, D, D_FF, out_dtypes=(F32, (F32, "rows8")),
                            extras=((x1, "mn"), (target, "mn")), epilogue=_residual_loss_epilogue)
    else:
        x2, h_out = _matmul(t + "ff2", r, p["w_ff2"], "nn", 512, D, D_FF, out_dtypes=(F32, BF16),
                            extras=((x1, "mn"), (next_gain, "n")), epilogue=_residual_norm_epilogue)
    saved = dict(x=x, h=h, z=z, y_pool=y_pool, y_conv=y_conv, y_attn=y_attn, lse=lse, merged=merged,
                 x1=x1, h2=h2, r=r)
    return x2, h_out, saved


def _layer_bwd(l, dx2, p, s, pending, collective_id, start_after=None):
    t = f"l{l}_b_"
    g = {}
    rest = jax.ShapeDtypeStruct((N_DEV, REST_ROWS, D), F32)
    da = _matmul(t + "d_ff2_in", dx2, p["w_ff2"], "nt", 1024, 1024, 1024, out_dtypes=(BF16,),
                 extras=((s["r"], "mn"),), epilogue=_relu2_bwd_epilogue,
                 after=start_after)
    rest = _matmul(t + "dw_ff2", s["r"], dx2, "tn", 512, 1024, 4096, into=(rest, REST_FF2, 0), after=da)
    tok = rest if pending is None else pending.chip_sums(after=rest)
    dx1, part = _matmul(t + "d_ff1_in", da, p["w_ff1_t"], "nn", 512, D, D_FF, out_dtypes=(F32, (F32, "rows8")),
                        extras=((s["x1"], "mn"), (dx2, "mn"), (p["norm_mlp"], "n")), epilogue=_norm_bwd_epilogue,
                        after=tok)
    g["norm_mlp"] = _fold_rows(t + "d_norm_mlp", part)
    rest = _matmul(t + "dw_ff1", da, s["h2"], "tn", 512, 1024, 4096, into=(rest, REST_FF1, 0), after=dx1)
    dmerged = _matmul(t + "d_out_proj_in", dx1, p["w_o"], "nt", 1024, 1024, 1024)
    rest = _matmul(t + "dw_o", s["merged"], dx1, "tn", 128, 1024, 4096, into=(rest, REST_WO, 0), after=dmerged)
    ys = (s["y_pool"], s["y_conv"], s["y_attn"])
    names = ("w_pool_up_t", "w_conv_out_t", "w_attn_up_t")
    dys, dbs = [], []
    tok = rest
    dz = jax.ShapeDtypeStruct((dx2.shape[0], IN_COLS), BF16)
    for b in range(3):
        dy_b, dw_t, dz, db = _mix_bwd(t + f"merge{b}", s["z"], p["b_gate"], ys[b], p[names[b]], dmerged, b, dz,
                                      after=tok)
        tok = dy_b
        width = ys[b].shape[1]
        if b < 2:
            rest = rest.at[:, REST_UP:REST_ATTN, b * width:(b + 1) * width].set(
                dw_t.reshape(N_DEV, REST_ATTN - REST_UP, width))
        else:
            rest = rest.at[:, REST_ATTN:REST_ROWS, :].set(dw_t.reshape(N_DEV, REST_ROWS - REST_ATTN, D))
        dys.append(dy_b)
        dbs.append(_fold_rows(t + f"d_b_gate{b}", db))
    g["b_gate"] = jnp.concatenate(dbs, axis=1)
    rs_rest = _ReduceScatter(f"rs_rest{l}", collective_id, rest)
    pending_sum = None if pending is None else pending.finish(after=rest)
    dz, g["pool_mix"], g["pool_scale"] = _pool_bwd(t + "pool", s["z"], dys[0], p["pool_mix"], p["pool_scale"], dz,
                                                   after=rest if pending is None else pending_sum[0])
    dcb, dcc, dcx, g["conv_w"] = _conv_bwd(t + "conv", s["z"], dys[1], p["conv_w"], after=dz)
    tok = rs_rest.chip_sums(after=dcb)
    dqs, dks, dvs = [], [], []
    gq = gk = None
    for gi, (_, dil) in enumerate(ATTN_GROUPS):
        dq, dk, dv, dgq, dgk = _attn_bwd(t + f"attn{gi}", s["z"], p["q_gain"], p["k_gain"], dys[2], s["y_attn"],
                                         s["lse"], gi, dil, after=tok)
        tok = dq
        dqs.append(dq)
        dks.append(dk)
        dvs.append(dv)
        gq = dgq if gq is None else gq + dgq
        gk = dgk if gk is None else gk + dgk
    g["q_gain"], g["k_gain"] = gq, gk
    rest_sum, _ = rs_rest.finish(after=tok)
    col = OFF_CB
    for piece in [dcb, dcc, dcx] + dqs + dks + dvs:
        dz = lax.dynamic_update_slice(dz, piece.astype(BF16), (0, col))
        col += piece.shape[1]
    in_t = _matmul(t + "dw_in", dz, s["h"], "tn", 256, 1024, 4096, after=rest_sum)
    rs_in = _ReduceScatter(f"rs_in{l}", collective_id + 2, in_t.reshape(N_DEV, IN_ROWS, D))
    dx, part = _matmul(t + "d_in_proj_in", dz, p["w_in_t"], "nn", 256, D, IN_COLS, out_dtypes=(F32, (F32, "rows8")),
                       extras=((s["x"], "mn"), (dx1, "mn"), (p["norm_mix"], "n")), epilogue=_norm_bwd_epilogue,
                       after=in_t)
    g["norm_mix"] = _fold_rows(t + "d_norm_mix", part)
    return dx, g, rest_sum, pending_sum, rs_in


ANY = pl.BlockSpec(memory_space=pl.ANY)


def _mesh_pos():
    return lax.axis_index("x"), lax.axis_index("y"), lax.axis_index("c")


def _other_chips(x, y):
    return [(1 - x, y), (x, 1 - y), (1 - x, 1 - y)]


def _comm_call(name, collective_id, peers, body, arrs, out_shape, sem_counts, after=None):
    n_in, n_out = len(arrs), len(out_shape)
    if collective_id is None:
        def tc_body(*refs):
            body(refs[:n_in], refs[n_in:n_in + n_out], *refs[n_in + n_out:])

        return pl.pallas_call(
            tc_body, name=name, out_shape=tuple(out_shape), in_specs=[ANY] * n_in, out_specs=(ANY,) * n_out,
            scratch_shapes=[pltpu.SemaphoreType.DMA((n,)) for n in sem_counts])(*arrs)

    n_after = 0 if after is None else 1

    def seq_body(*refs):
        barrier = pltpu.get_barrier_semaphore()
        ps = peers(*_mesh_pos())
        for p in ps:
            pl.semaphore_signal(barrier, inc=1, device_id=p, device_id_type=MESH)
        pl.semaphore_wait(barrier, len(ps))
        outs = refs[n_in + n_after:]
        body(refs[:n_in], outs[:n_out], *outs[n_out:])

    return pl.kernel(
        seq_body, out_type=tuple(out_shape), mesh=plsc.ScalarSubcoreMesh(axis_name="seq", num_cores=1), name=name,
        scratch_types=[pltpu.SemaphoreType.DMA((n,)) for n in sem_counts],
        compiler_params=pltpu.CompilerParams(collective_id=collective_id),
    )(*arrs, *([] if after is None else [after]))


def _ordered(name, x, after):
    def body(x_ref, after_ref, o_ref):
        del x_ref, after_ref, o_ref

    return pl.pallas_call(
        body, name=name, out_shape=jax.ShapeDtypeStruct(x.shape, x.dtype), in_specs=[ANY, ANY], out_specs=ANY,
        input_output_aliases={0: 0})(x, after)


def _all_gather(name, collective_id, shard, after=None):
    R, C = shard.shape

    def peers(x, y, c):
        return [(x, y, 1 - c)] + [(*chip, c) for chip in _other_chips(x, y)]

    def body(in_refs, out_refs, send_sems, recv_sems, local_sems):
        (x_ref,), (out_ref,) = in_refs, out_refs
        x, y, c = _mesh_pos()
        me, sibling = (x, y, c), (x, y, 1 - c)
        chips = _other_chips(x, y)

        def slot(px, py, pc):
            return out_ref.at[4 * px + 2 * py + pc]

        def copy(k, block, to, src=None):
            return pltpu.make_async_remote_copy(
                src_ref=slot(*block) if src is None else src, dst_ref=slot(*block),
                send_sem=send_sems.at[k], recv_sem=recv_sems.at[k], device_id=to, device_id_type=MESH)

        mine = pltpu.make_async_copy(x_ref, slot(*me), local_sems.at[0])
        mine.start()
        first = [copy(0, me, sibling, src=x_ref)]
        first += [copy(1 + j, me, (*chip, c), src=x_ref) for j, chip in enumerate(chips)]
        for cp in first:
            cp.start()
        passed = [copy(4 + j, (*chip, c), sibling) for j, chip in enumerate(chips)]
        for j, chip in enumerate(chips):
            copy(1 + j, (*chip, c), me).wait_recv()
            passed[j].start()
        copy(0, sibling, me).wait_recv()
        for j, chip in enumerate(chips):
            copy(4 + j, (*chip, 1 - c), me).wait_recv()
        for cp in first + passed:
            cp.wait_send()
        mine.wait()

    return _comm_call(name, collective_id, peers, body, [shard],
                      [jax.ShapeDtypeStruct((N_DEV, R, C), shard.dtype)], (7, 7, 1), after)[0]


def _rs_sibling_exchange(name, collective_id, arrs):
    n = len(arrs)

    def body(in_refs, out_refs, send_sems, recv_sems):
        x, y, c = _mesh_pos()
        cps = []
        for k, (src, dst) in enumerate(zip(in_refs, out_refs)):
            src = src.at[:, 1 - c] if len(src.shape) == 4 else src
            cps.append(pltpu.make_async_remote_copy(src_ref=src, dst_ref=dst, send_sem=send_sems.at[k],
                                                    recv_sem=recv_sems.at[k], device_id=(x, y, 1 - c),
                                                    device_id_type=MESH))
        for cp in cps:
            cp.start()
        for cp in cps:
            cp.wait()

    out_shape = [jax.ShapeDtypeStruct(a.shape[:1] + a.shape[2:] if a.ndim == 4 else a.shape, a.dtype) for a in arrs]
    return _comm_call(name, collective_id, lambda x, y, c: [(x, y, 1 - c)], body, arrs, out_shape, (n, n))


def _rs_rows(r):
    return r // 2 if (r // 2) % 16 == 0 else r


def _drop_ref(body, idx):
    def wrapped(*refs):
        return body(*refs[:idx], *refs[idx + 1:])

    return wrapped


def _rs_chip_sum(name, ids, big, rbig, after=None):
    _, _, R, C = big.shape
    rows = _rs_rows(R)

    def body(ids_ref, mine_ref, theirs_ref, t16_ref, own_ref):
        p = pl.program_id(1)
        s = mine_ref[...] + theirs_ref[...]
        t16_ref[...] = s.astype(BF16)

        @pl.when(p == ids_ref[1])
        def _():
            own_ref[...] = s

    in_specs = [pl.BlockSpec((None, None, rows, C), lambda i, p, ids: (p, ids[0], i, 0)),
                pl.BlockSpec((None, rows, C), lambda i, p, ids: (p, i, 0))]
    args = [ids, big, rbig]
    if after is not None:
        body = _drop_ref(body, len(args))
        in_specs.append(ANY)
        args.append(after)
    grid_spec = pltpu.PrefetchScalarGridSpec(
        num_scalar_prefetch=1, grid=(R // rows, 4), in_specs=in_specs,
        out_specs=(pl.BlockSpec((None, rows, C), lambda i, p, ids: (p, i, 0)),
                   pl.BlockSpec((rows, C), lambda i, p, ids: (i, 0))))
    return pl.pallas_call(
        body, name=name, grid_spec=grid_spec,
        out_shape=(jax.ShapeDtypeStruct((4, R, C), BF16), jax.ShapeDtypeStruct((R, C), F32)),
        compiler_params=_cparams(("parallel", "arbitrary")),
    )(*args)


def _add2(name, a, b):
    def body(a_ref, b_ref, o_ref):
        o_ref[...] = a_ref[...] + b_ref[...]

    full = pl.BlockSpec(a.shape, lambda i: (0, 0))
    return pl.pallas_call(body, name=name, grid=(1,), in_specs=[full, full], out_specs=full,
                          out_shape=jax.ShapeDtypeStruct(a.shape, a.dtype))(a, b)


def _rs_chip_exchange(name, collective_id, arrs):
    n = len(arrs)

    def body(in_refs, out_refs, send_sems, recv_sems, local_sems):
        x, y, c = _mesh_pos()
        p_me = 2 * x + y
        chips = _other_chips(x, y)

        def part(ref, p):
            return ref.at[p] if len(ref.shape) == 3 else ref

        local = [pltpu.make_async_copy(part(src, p_me), dst.at[p_me], local_sems.at[k])
                 for k, (src, dst) in enumerate(zip(in_refs, out_refs))]
        for cp in local:
            cp.start()
        sends = []
        for j, (px, py) in enumerate(chips):
            for k, (src, dst) in enumerate(zip(in_refs, out_refs)):
                sends.append(pltpu.make_async_remote_copy(
                    src_ref=part(src, 2 * px + py), dst_ref=dst.at[p_me], send_sem=send_sems.at[n * j + k],
                    recv_sem=recv_sems.at[n * j + k], device_id=(px, py, c), device_id_type=MESH))
        for cp in sends:
            cp.start()
        for j, (px, py) in enumerate(chips):
            for k, (src, dst) in enumerate(zip(in_refs, out_refs)):
                pltpu.make_async_remote_copy(
                    src_ref=part(src, p_me), dst_ref=dst.at[2 * px + py], send_sem=send_sems.at[n * j + k],
                    recv_sem=recv_sems.at[n * j + k], device_id=(px, py, c), device_id_type=MESH).wait_recv()
        for cp in sends:
            cp.wait_send()
        for cp in local:
            cp.wait()

    out_shape = [jax.ShapeDtypeStruct((4,) + a.shape[-2:], a.dtype) for a in arrs]
    return _comm_call(name, collective_id, lambda x, y, c: [(*chip, c) for chip in _other_chips(x, y)], body, arrs,
                      out_shape, (3 * n, 3 * n, n))


def _rs_final_sum(name, ids, recv, own, rows, after=None):
    _, R, C = recv.shape
    assert R % rows == 0

    def body(ids_ref, r_ref, own_ref, o_ref):
        acc = None
        for p in range(4):
            term = jnp.where(ids_ref[1] == p, own_ref[...], r_ref[p].astype(F32))
            acc = term if acc is None else acc + term
        o_ref[...] = acc

    in_specs = [pl.BlockSpec((4, rows, C), lambda i, ids: (0, i, 0)), pl.BlockSpec((rows, C), lambda i, ids: (i, 0))]
    args = [ids, recv, own]
    if after is not None:
        body = _drop_ref(body, len(args))
        in_specs.append(ANY)
        args.append(after)
    grid_spec = pltpu.PrefetchScalarGridSpec(
        num_scalar_prefetch=1, grid=(R // rows,), in_specs=in_specs,
        out_specs=pl.BlockSpec((rows, C), lambda i, ids: (i, 0)))
    return pl.pallas_call(
        body, name=name, grid_spec=grid_spec, out_shape=jax.ShapeDtypeStruct((R, C), F32),
        compiler_params=_cparams(("parallel",)),
    )(*args)


class _ReduceScatter:
    def __init__(self, tag, collective_id, big, small=None):
        x, y, c = _mesh_pos()
        self.tag, self.cid, self.small = tag, collective_id, small
        self.ids = jnp.stack([c, 2 * x + y]).astype(jnp.int32)
        self.big = big.reshape((4, 2) + big.shape[1:])
        self.got = _rs_sibling_exchange(tag + "_sibling", collective_id,
                                        [self.big] + ([] if small is None else [small]))

    def chip_sums(self, after=None):
        t16, self.own = _rs_chip_sum(self.tag + "_chip_sum", self.ids, self.big, self.got[0], after)
        arrs = [t16]
        if self.small is not None:
            self.ts = _add2(self.tag + "_chip_sum_small", self.small, self.got[1])
            arrs.append(self.ts)
        self.recv = _rs_chip_exchange(self.tag + "_chips", self.cid + 1, arrs)
        return t16

    def finish(self, after=None):
        out = _rs_final_sum(self.tag + "_final", self.ids, self.recv[0], self.own, _rs_rows(self.own.shape[0]), after)
        out_small = None
        if self.small is not None:
            out_small = _rs_final_sum(self.tag + "_final_small", self.ids, self.recv[1], self.ts, self.small.shape[0])
        return out, out_small


def _all_reduce_small(tag, small):
    x, y, c = _mesh_pos()
    ids = jnp.stack([c, 2 * x + y]).astype(jnp.int32)
    (theirs,) = _rs_sibling_exchange(tag + "_sibling", None, [small])
    ts = _add2(tag + "_chip_sum", small, theirs)
    (recv,) = _rs_chip_exchange(tag + "_chips", None, [ts])
    return _rs_final_sum(tag + "_final", ids, recv, ts, small.shape[0])


def _adamw(name, w, g, m, v):
    R, C = w.shape
    tiles = [t for t in range(8, 513, 8) if R % t == 0]
    tr = max(tiles) if tiles else R
    c1 = 1.0 - ADAM_B1 ** ADAM_STEP
    c2 = 1.0 - ADAM_B2 ** ADAM_STEP

    def body(w_ref, g_ref, m_ref, v_ref, d_ref, nm_ref, nv_ref):
        gv = g_ref[...]
        nm = ADAM_B1 * m_ref[...] + (1.0 - ADAM_B1) * gv
        nv = ADAM_B2 * v_ref[...] + (1.0 - ADAM_B2) * (gv * gv)
        d_ref[...] = -ADAM_LR * ((nm / c1) / (jnp.sqrt(nv / c2) + ADAM_EPS) + ADAM_WD * w_ref[...])
        nm_ref[...] = nm
        nv_ref[...] = nv

    blk = pl.BlockSpec((tr, C), lambda i: (i, 0))
    shp = jax.ShapeDtypeStruct((R, C), F32)
    return pl.pallas_call(
        body, name=name, grid=(R // tr,), in_specs=[blk] * 4, out_specs=(blk,) * 3, out_shape=(shp,) * 3,
        compiler_params=_cparams(("parallel",)),
    )(w, g, m, v)


def _adamw_t(name, w, g_t, m, v):
    L, R, C = w.shape
    tr = 256
    c1 = 1.0 - ADAM_B1 ** ADAM_STEP
    c2 = 1.0 - ADAM_B2 ** ADAM_STEP

    def body(w_ref, g_ref, m_ref, v_ref, go_ref, d_ref, nm_ref, nv_ref):
        gv = g_ref[...].T
        go_ref[...] = gv
        nm = ADAM_B1 * m_ref[...] + (1.0 - ADAM_B1) * gv
        nv = ADAM_B2 * v_ref[...] + (1.0 - ADAM_B2) * (gv * gv)
        d_ref[...] = -ADAM_LR * ((nm / c1) / (jnp.sqrt(nv / c2) + ADAM_EPS) + ADAM_WD * w_ref[...])
        nm_ref[...] = nm
        nv_ref[...] = nv

    blk = pl.BlockSpec((None, tr, C), lambda l, i: (l, i, 0))
    blk_t = pl.BlockSpec((None, C, tr), lambda l, i: (l, 0, i))
    shp = jax.ShapeDtypeStruct((L, R, C), F32)
    return pl.pallas_call(
        body, name=name, grid=(L, R // tr), in_specs=[blk, blk_t, blk, blk], out_specs=(blk,) * 4,
        out_shape=(shp,) * 4, compiler_params=_cparams(("parallel", "parallel")),
    )(w, g_t, m, v)


REST_FF1 = 0
REST_FF2 = 512
REST_WO = 1024
REST_UP = 1152
REST_ATTN = 1280
REST_ROWS = 1312
IN_ROWS = IN_COLS // N_DEV
CONV_BITS_ROWS = 16
SHARD_ROWS = IN_ROWS + REST_ROWS + CONV_BITS_ROWS
SMALL = (("norm_mix", (DEPTH, D)), ("b_gate", (DEPTH, 3 * D)), ("pool_mix", (DEPTH, 4, POOL_G, POOL_G)),
         ("pool_scale", (DEPTH, POOL_W)), ("conv_w", (DEPTH, 3, CONV_W)), ("q_gain", (DEPTH, HEAD)),
         ("k_gain", (DEPTH, HEAD)), ("norm_mlp", (DEPTH, D)))


def _pack_weight_shards(w, l):
    b = lambda a: a.astype(BF16)
    rest = jnp.concatenate([
        b(w["w_ff1"][l].T), b(w["w_ff2"][l]), b(w["w_o"][l]),
        jnp.concatenate([b(w["w_pool_up"][l].T), b(w["w_conv_out"][l].T)], axis=1),
        b(w["w_attn_up"][l].T).reshape(REST_ROWS - REST_ATTN, D)], axis=0)
    return b(w["w_in"][l].T), rest


def _unpack_gathered(g_in, g_rest, g_conv, small_w, l):
    rest = g_rest
    take = lambda r0, rows, c0=0, cols=D: rest[:, r0:r0 + rows, c0:c0 + cols].reshape(N_DEV * rows, cols)
    conv = g_conv[:, 3 * l:3 * l + 3, :CONV_W // N_DEV]
    p = {
        "w_in_t": g_in.reshape(IN_COLS, D),
        "w_ff1_t": take(REST_FF1, 512), "w_ff2": take(REST_FF2, 512), "w_o": take(REST_WO, 128),
        "w_pool_up_t": take(REST_UP, 128, 0, POOL_W), "w_conv_out_t": take(REST_UP, 128, POOL_W, CONV_W),
        "w_attn_up_t": rest[:, REST_ATTN:].reshape(D, ATTN_OUT),
        "conv_w": jnp.transpose(conv, (1, 0, 2)).reshape(3, CONV_W),
        "pool_mix": small_w["pool_mix"][l],
    }
    for name in ("norm_mix", "b_gate", "pool_scale", "q_gain", "k_gain", "norm_mlp"):
        p[name] = small_w[name][l][None, :]
    return p


def _pack_small_grads(grads):
    flat = jnp.concatenate([jnp.stack([grads[l][name] for l in range(DEPTH)]).reshape(-1) for name, _ in SMALL])
    return jnp.pad(flat, (0, (-flat.shape[0]) % (8 * 128))).reshape(-1, 128)


def _unpack_grads(rest, small, dev):
    out = {
        "w_ff1_t": jnp.stack([a[REST_FF1:REST_FF1 + 512] for a in rest]),
        "w_ff2": jnp.stack([a[REST_FF2:REST_FF2 + 512] for a in rest]),
        "w_o": jnp.stack([a[REST_WO:REST_WO + 128] for a in rest]),
        "w_pool_up": jnp.stack([a[REST_UP:REST_UP + 128, :POOL_W].T for a in rest]),
        "w_conv_out": jnp.stack([a[REST_UP:REST_UP + 128, POOL_W:].T for a in rest]),
        "w_attn_up": jnp.stack([a[REST_ATTN:].reshape(D // N_DEV, ATTN_OUT).T for a in rest]),
    }
    out.update(_unpack_small_like_grads(small, dev))
    return out


def _unpack_small_like_grads(small, dev):
    out = {}
    flat = small.reshape(-1)
    off = 0
    for name, shp in SMALL:
        n = 1
        for s in shp:
            n *= s
        out[name] = flat[off:off + n].reshape(shp)
        off += n
    width = CONV_W // N_DEV
    out["conv_w"] = lax.dynamic_slice_in_dim(out["conv_w"], dev * width, width, axis=2)
    return out


def _pack_small_like_grads(src, dev):
    parts = []
    for name, shp in SMALL:
        a = src[name]
        if name == "conv_w":
            a = lax.dynamic_update_slice_in_dim(jnp.zeros(shp, F32), a, dev * (CONV_W // N_DEV), axis=2)
        parts.append(a.reshape(-1))
    flat = jnp.concatenate(parts)
    return jnp.pad(flat, (0, (-flat.shape[0]) % (8 * 128))).reshape(-1, 128)


WEIGHTS = ("norm_mix", "w_in", "b_gate", "pool_mix", "pool_scale", "conv_w", "q_gain", "k_gain", "w_pool_up",
           "w_conv_out", "w_attn_up", "w_o", "norm_mlp", "w_ff1", "w_ff2")


def kernel(x, norm_mix, w_in, b_gate, pool_mix, pool_scale, conv_w, q_gain, k_gain, w_pool_up, w_conv_out, w_attn_up, w_o, norm_mlp, w_ff1, w_ff2, loss_target, m_norm_mix, m_w_in, m_b_gate, m_pool_mix, m_pool_scale, m_conv_w, m_q_gain, m_k_gain, m_w_pool_up, m_w_conv_out, m_w_attn_up, m_w_o, m_norm_mlp, m_w_ff1, m_w_ff2, v_norm_mix, v_w_in, v_b_gate, v_pool_mix, v_pool_scale, v_conv_w, v_q_gain, v_k_gain, v_w_pool_up, v_w_conv_out, v_w_attn_up, v_w_o, v_norm_mlp, v_w_ff1, v_w_ff2):
    w = dict(zip(WEIGHTS, (norm_mix, w_in, b_gate, pool_mix, pool_scale, conv_w, q_gain, k_gain, w_pool_up,
                           w_conv_out, w_attn_up, w_o, norm_mlp, w_ff1, w_ff2)))
    m = dict(zip(WEIGHTS, (m_norm_mix, m_w_in, m_b_gate, m_pool_mix, m_pool_scale, m_conv_w, m_q_gain, m_k_gain,
                           m_w_pool_up, m_w_conv_out, m_w_attn_up, m_w_o, m_norm_mlp, m_w_ff1, m_w_ff2)))
    v = dict(zip(WEIGHTS, (v_norm_mix, v_w_in, v_b_gate, v_pool_mix, v_pool_scale, v_conv_w, v_q_gain, v_k_gain,
                           v_w_pool_up, v_w_conv_out, v_w_attn_up, v_w_o, v_norm_mlp, v_w_ff1, v_w_ff2)))
    xi, yi, ci = _mesh_pos()
    dev = 4 * xi + 2 * yi + ci

    saved, params = [], []
    act = x[0]
    w_in_t, m_in_t, v_in_t = (jnp.swapaxes(src["w_in"], 1, 2) for src in (w, m, v))
    h = _rmsnorm_fwd("l0_norm_mix", act, w["norm_mix"][0][None])
    conv_shard = jnp.pad(w["conv_w"].reshape(DEPTH * 3, CONV_W // N_DEV), ((0, 8 - DEPTH * 3), (0, 128 - CONV_W // N_DEV)))
    g_conv = _all_gather("gather_conv", None, conv_shard)
    for l in range(DEPTH):
        s_in, s_rest = _pack_weight_shards(w, l)
        g_in = _all_gather(f"gather_in{l}", 1 + 2 * l, s_in)
        g_rest = _all_gather(f"gather_rest{l}", 2 + 2 * l, s_rest)
        params.append(_unpack_gathered(g_in, g_rest, g_conv, w, l))
    for l in range(DEPTH):
        act, h, s = _layer_fwd(l, act, h, params[l], w["norm_mix"][l + 1][None] if l + 1 < DEPTH else None,
                               loss_target[0])
        saved.append(s)
    dy = act
    loss = _fold_rows("loss_sum", h, lanes=True, scale=0.5 / D)
    total = lax.psum(loss[0, 0], ("x", "y", "c"))
    in_t, rest, grads, pending = [None] * DEPTH, [None] * DEPTH, [None] * DEPTH, None
    start = total.reshape(1, 1)
    for l in reversed(range(DEPTH)):
        dy, grads[l], rest[l], done, pending = _layer_bwd(l, dy, params[l], saved[l], pending, 5 + 4 * l, start)
        start = None
        if done is not None:
            in_t[l + 1] = done[0]
    dx = dy
    g_small = _all_reduce_small("ar_small", _pack_small_grads(grads))
    pending.chip_sums(after=g_small)
    g = _unpack_grads(rest, g_small, dev)

    delta, new_m, new_v = {}, {}, {}
    g["w_ff1"], delta["w_ff1"], new_m["w_ff1"], new_v["w_ff1"] = _adamw_t("adamw_w_ff1", w["w_ff1"], g["w_ff1_t"],
                                                                          m["w_ff1"], v["w_ff1"])
    for name in ("w_pool_up", "w_conv_out", "w_attn_up", "w_o", "w_ff2"):
        shp = w[name].shape
        two_d = (shp[0] * shp[1], shp[2])
        d_, m_, v_ = _adamw("adamw_" + name, w[name].reshape(two_d), g[name].reshape(two_d),
                            m[name].reshape(two_d), v[name].reshape(two_d))
        delta[name], new_m[name], new_v[name] = d_.reshape(shp), m_.reshape(shp), v_.reshape(shp)
    for name, shp in SMALL:
        shp = w[name].shape
        two_d = (-1, shp[-1])
        d_, m_, v_ = _adamw("adamw_" + name, w[name].reshape(two_d), g[name].reshape(two_d),
                            m[name].reshape(two_d), v[name].reshape(two_d))
        delta[name], new_m[name], new_v[name] = d_.reshape(shp), m_.reshape(shp), v_.reshape(shp)
    in_t[0], _ = pending.finish(after=new_v[SMALL[-1][0]])
    g_t = jnp.stack(in_t)
    two_d = (g_t.shape[0] * g_t.shape[1], g_t.shape[2])
    d_, m_, v_ = _adamw("adamw_w_in", *[a.reshape(two_d) for a in (w_in_t, g_t, m_in_t, v_in_t)])
    g["w_in"] = jnp.swapaxes(g_t, 1, 2)
    delta["w_in"], new_m["w_in"], new_v["w_in"] = (jnp.swapaxes(a.reshape(g_t.shape), 1, 2) for a in (d_, m_, v_))

    return (total, dx[None], *[g[n] for n in WEIGHTS], *[delta[n] for n in WEIGHTS],
            *[new_m[n] for n in WEIGHTS], *[new_v[n] for n in WEIGHTS])
```

```python
import functools

import jax
import jax.numpy as jnp
from jax import lax
from jax.experimental import pallas as pl
from jax.experimental.pallas import tpu as pltpu
from jax.experimental.pallas import tpu_sc as plsc

F32 = jnp.float32
BF16 = jnp.bfloat16
MESH = pl.DeviceIdType.MESH

D = 1024
SEQ = 4096
DEPTH = 2
N_DEV = 8
POOL_WINDOWS = (2, 4, 8, 16)
POOL_W = 512
POOL_G = 128
CONV_W = 512
HEAD = 64
ATTN_GROUPS = ((128, 1), (512, 4), (2048, 16))
HPG = 4
ATTN_W = 768
ATTN_OUT = 256
ATTN_BLK = 128
D_FF = 4096
EPS = 1e-6
MASK = -1e30
OFF_POOL = 0
OFF_CB = 512
OFF_CC = 1024
OFF_CX = 1536
OFF_Q = 2048
OFF_K = 2816
OFF_V = 3584
OFF_GATE = 4352
IN_COLS = 7424
ADAM_LR = 0.001
ADAM_B1 = 0.9
ADAM_B2 = 0.999
ADAM_EPS = 1e-08
ADAM_WD = 0.01
ADAM_STEP = 10

ROW_TILE = 512
SEQ_CHUNK = 256
HALO = 16
VMEM_LIMIT = 56 * 1024 * 1024


def _cparams(sem=None):
    return pltpu.CompilerParams(dimension_semantics=sem, vmem_limit_bytes=VMEM_LIMIT)


def _call(body, args, in_specs, after=None, **kw):
    if after is not None:
        after = list(after) if isinstance(after, (list, tuple)) else [after]
        n, k, inner = len(args), len(after), body

        def body(*refs):
            return inner(*refs[:n], *refs[n + k:])

        args = list(args) + after
        in_specs = list(in_specs) + [pl.BlockSpec(memory_space=pl.ANY)] * k
    return pl.pallas_call(body, in_specs=in_specs, **kw)(*args)


_DIMS = {"nn": (((1,), (0,)), ((), ())), "nt": (((1,), (1,)), ((), ())), "tn": (((0,), (0,)), ((), ()))}


def _matmul(name, a, b, mode, tm, tn, tk, out_dtypes=(F32,), extras=(), epilogue=None, into=None, after=None,
            b_rows=None):
    if mode == "tn":
        K, M = a.shape
    else:
        M, K = a.shape
    b_shape = b.shape if b_rows is None else (b.shape[0] * b_rows[1], b.shape[2])
    N = b_shape[0] if mode == "nt" else b_shape[1]
    assert b_shape[1 if mode == "nt" else 0] == K, (name, b_shape, K)
    assert M % tm == 0 and N % tn == 0 and K % tk == 0, (name, M, N, K, tm, tn, tk)
    nk = K // tk
    n_extra = len(extras)
    n_out = len(out_dtypes)
    dims = _DIMS[mode]
    n_alias = 0 if into is None or isinstance(into[0], jax.ShapeDtypeStruct) else 1

    def body(a_ref, b_ref, *rest):
        extra_refs = rest[:n_extra]
        out_refs = rest[n_extra + n_alias:n_extra + n_alias + n_out]

        def finish(acc):
            if epilogue is None:
                res = (acc,)
            else:
                res = epilogue(acc, *[r[...] for r in extra_refs])
            for o_ref, v in zip(out_refs, res):
                o_ref[...] = v.astype(o_ref.dtype)

        bv = b_ref[...]
        if b_rows is not None:
            bv = bv.reshape(-1, bv.shape[-1])
        part = lax.dot_general(a_ref[...].astype(BF16), bv.astype(BF16), dims, preferred_element_type=F32)
        if nk == 1:
            finish(part)
        else:
            acc_ref = rest[-1]
            k = pl.program_id(2)

            @pl.when(k == 0)
            def _():
                acc_ref[...] = part

            @pl.when(k > 0)
            def _():
                acc_ref[...] += part

            @pl.when(k == nk - 1)
            def _():
                finish(acc_ref[...])

    if mode == "tn":
        a_spec = pl.BlockSpec((tk, tm), lambda i, j, k: (k, i))
    else:
        a_spec = pl.BlockSpec((tm, tk), lambda i, j, k: (i, k))
    if b_rows is not None:
        row0, rows = b_rows
        t_rows = tn if mode == "nt" else tk
        assert t_rows % rows == 0 and row0 % rows == 0, (name, t_rows, b_rows)
        if mode == "nt":
            b_spec = pl.BlockSpec((t_rows // rows, rows, tk), lambda i, j, k: (j, row0 // rows, k))
        else:
            b_spec = pl.BlockSpec((t_rows // rows, rows, tn), lambda i, j, k: (k, row0 // rows, j))
    elif mode == "nt":
        b_spec = pl.BlockSpec((tn, tk), lambda i, j, k: (j, k))
    else:
        b_spec = pl.BlockSpec((tk, tn), lambda i, j, k: (k, j))
    in_specs = [a_spec, b_spec]
    args = [a, b]
    for arr, kind in extras:
        if kind == "mn":
            in_specs.append(pl.BlockSpec((tm, tn), lambda i, j, k: (i, j)))
        else:
            in_specs.append(pl.BlockSpec((1, tn), lambda i, j, k: (0, j)))
        args.append(arr)
    part = [isinstance(dt, tuple) for dt in out_dtypes]
    out_shape = tuple(jax.ShapeDtypeStruct((M // tm * 8, N), dt[0]) if p else jax.ShapeDtypeStruct((M, N), dt)
                      for dt, p in zip(out_dtypes, part))
    out_specs = tuple(pl.BlockSpec((8 if p else tm, tn), lambda i, j, k: (i, j)) for p in part)
    aliases = {}
    if into is not None:
        buf, row0, col0 = into
        assert n_out == 1 and (M // N_DEV) % tm == 0 and row0 % tm == 0 and col0 % tn == 0
        per_dev = M // N_DEV // tm
        out_shape = (jax.ShapeDtypeStruct(buf.shape, buf.dtype),)
        out_specs = (pl.BlockSpec((None, tm, tn), lambda i, j, k: (i // per_dev, row0 // tm + i % per_dev,
                                                                   col0 // tn + j)),)
        if not isinstance(buf, jax.ShapeDtypeStruct):
            aliases = {len(args): 0}
            in_specs.append(pl.BlockSpec(memory_space=pl.ANY))
            args.append(buf)
    scratch = [] if nk == 1 else [pltpu.VMEM((tm, tn), F32)]
    res = _call(
        body, args, in_specs, after, name=name, grid=(M // tm, N // tn, nk), out_specs=out_specs,
        out_shape=out_shape, scratch_shapes=scratch, input_output_aliases=aliases,
        compiler_params=_cparams(("parallel", "parallel", "arbitrary")))
    return res if n_out > 1 else res[0]


def _rmsnorm_fwd(name, x, gain, after=None):
    S_, D_ = x.shape

    def body(x_ref, g_ref, h_ref):
        xf = x_ref[...]
        ms = jnp.mean(xf * xf, axis=-1, keepdims=True)
        h_ref[...] = (xf * lax.rsqrt(ms + EPS) * g_ref[...]).astype(BF16)

    return _call(
        body, [x, gain], [pl.BlockSpec((ROW_TILE, D_), lambda i: (i, 0)), pl.BlockSpec((1, D_), lambda i: (0, 0))],
        after, name=name, grid=(S_ // ROW_TILE,),
        out_specs=pl.BlockSpec((ROW_TILE, D_), lambda i: (i, 0)),
        out_shape=jax.ShapeDtypeStruct((S_, D_), BF16),
        compiler_params=_cparams(("parallel",)))


def _rmsnorm_bwd(name, x, gain, dh, dres, after=None):
    S_, D_ = x.shape
    n = S_ // ROW_TILE

    def body(x_ref, g_ref, dh_ref, dres_ref, dx_ref, dg_ref, acc_ref):
        i = pl.program_id(0)
        xf = x_ref[...]
        rstd = lax.rsqrt(jnp.mean(xf * xf, axis=-1, keepdims=True) + EPS)
        xhat = xf * rstd
        dhv = dh_ref[...]
        dxhat = dhv * g_ref[...]
        c = jnp.mean(dxhat * xhat, axis=-1, keepdims=True)
        dx_ref[...] = dres_ref[...] + rstd * (dxhat - xhat * c)
        part = jnp.sum((dhv * xhat).reshape(ROW_TILE // 8, 8, D_), axis=0)

        @pl.when(i == 0)
        def _():
            acc_ref[...] = part

        @pl.when(i > 0)
        def _():
            acc_ref[...] += part

        @pl.when(i == n - 1)
        def _():
            dg_ref[...] = jnp.sum(acc_ref[...], axis=0, keepdims=True)

    row = pl.BlockSpec((ROW_TILE, D_), lambda i: (i, 0))
    vec = pl.BlockSpec((1, D_), lambda i: (0, 0))
    return _call(
        body, [x, gain, dh, dres], [row, vec, row, row], after, name=name, grid=(n,), out_specs=(row, vec),
        out_shape=(jax.ShapeDtypeStruct((S_, D_), F32), jax.ShapeDtypeStruct((1, D_), F32)),
        scratch_shapes=[pltpu.VMEM((8, D_), F32)],
        compiler_params=_cparams(("arbitrary",)))


def _loss_head(name, y, target):
    S_, D_ = y.shape
    n = S_ // ROW_TILE

    def body(y_ref, t_ref, dy_ref, l_ref, acc_ref):
        i = pl.program_id(0)
        e = y_ref[...] - t_ref[...]
        dy_ref[...] = e * (1.0 / D_)
        part = jnp.sum((e * e).reshape(ROW_TILE // 8, 8, D_), axis=0)

        @pl.when(i == 0)
        def _():
            acc_ref[...] = part

        @pl.when(i > 0)
        def _():
            acc_ref[...] += part

        @pl.when(i == n - 1)
        def _():
            s = jnp.sum(acc_ref[...], axis=1, keepdims=True)
            l_ref[...] = jnp.sum(s, axis=0, keepdims=True) * (0.5 / D_)

    row = pl.BlockSpec((ROW_TILE, D_), lambda i: (i, 0))
    return pl.pallas_call(
        body, name=name, grid=(n,), in_specs=[row, row],
        out_specs=(row, pl.BlockSpec((1, 1), lambda i: (0, 0))),
        out_shape=(jax.ShapeDtypeStruct((S_, D_), F32), jax.ShapeDtypeStruct((1, 1), F32)),
        scratch_shapes=[pltpu.VMEM((8, D_), F32)],
        compiler_params=_cparams(("arbitrary",)),
    )(y, target)


def _rows_with_halo(ref, cols, i, n_chunks, before, after):
    r0 = pl.multiple_of(i * SEQ_CHUNK, SEQ_CHUNK)
    parts = []
    if before:
        h0 = pl.multiple_of(jnp.maximum(r0 - HALO, 0), 8)
        halo = ref[pl.ds(h0, HALO), cols]
        parts.append(jnp.where(i > 0, halo, jnp.zeros_like(halo)))
    parts.append(ref[pl.ds(r0, SEQ_CHUNK), cols])
    if after:
        a0 = pl.multiple_of(jnp.minimum(r0 + SEQ_CHUNK, (n_chunks - 1) * SEQ_CHUNK + SEQ_CHUNK - HALO), 8)
        halo = ref[pl.ds(a0, HALO), cols]
        parts.append(jnp.where(i < n_chunks - 1, halo, jnp.zeros_like(halo)))
    return parts[0] if len(parts) == 1 else jnp.concatenate(parts, axis=0)


def _shift_down(v, k):
    return pltpu.roll(v, k, 0)


def _shift_up(v, k):
    return pltpu.roll(v, v.shape[0] - k, 0)


def _pool_diff(xx, w, t_main):
    s = xx
    k = 1
    while k < w:
        s = s + _shift_down(s, k)
        k *= 2
    cnt = jnp.minimum(t_main + 1, w).astype(F32)
    return s[HALO:] / cnt - xx[HALO:]


def _pool_fwd(name, z, pool_mix, pool_scale):
    S_ = z.shape[0]
    n_chunks = S_ // SEQ_CHUNK

    cols = slice(0, POOL_G)

    def body(u_ref, mix_ref, sc_ref, y_ref):
        mixg = mix_ref[...].astype(BF16)
        scg = sc_ref[...]
        for g, w in enumerate(POOL_WINDOWS):
            @pl.when(pl.program_id(0) == g)
            def _(w=w):
                def chunk(i, carry):
                    r0 = pl.multiple_of(i * SEQ_CHUNK, SEQ_CHUNK)
                    xx = _rows_with_halo(u_ref, cols, i, n_chunks, True, False)
                    t = r0 + lax.broadcasted_iota(jnp.int32, (SEQ_CHUNK, POOL_G), 0)
                    d = _pool_diff(xx, w, t)
                    y = jnp.dot(d.astype(BF16), mixg, preferred_element_type=F32) * scg
                    y_ref[pl.ds(r0, SEQ_CHUNK), :] = y.astype(BF16)
                    return carry

                lax.fori_loop(0, n_chunks, chunk, 0)

    slab = pl.BlockSpec((S_, POOL_G), lambda g: (0, g))
    return pl.pallas_call(
        body, name=name, grid=(len(POOL_WINDOWS),),
        in_specs=[slab, pl.BlockSpec((None, POOL_G, POOL_G), lambda g: (g, 0, 0)),
                  pl.BlockSpec((1, POOL_G), lambda g: (0, g))],
        out_specs=slab, out_shape=jax.ShapeDtypeStruct((S_, POOL_W), BF16),
        compiler_params=_cparams(("parallel",)),
    )(z, pool_mix, pool_scale)


def _pool_bwd(name, z, dy, pool_mix, pool_scale, dz, after=None):
    S_ = z.shape[0]
    n_chunks = S_ // SEQ_CHUNK
    rows_a = SEQ_CHUNK + HALO

    cols = slice(0, POOL_G)

    def body(u_ref, dy_ref, mix_ref, sc_ref, du_ref, dmix_ref, dsc_ref):
        mixg = mix_ref[...].astype(BF16)
        scg = sc_ref[...]
        for g, w in enumerate(POOL_WINDOWS):
            @pl.when(pl.program_id(0) == g)
            def _(w=w):
                def chunk(i, carry):
                    dmix_acc, dsc_acc = carry
                    r0 = pl.multiple_of(i * SEQ_CHUNK, SEQ_CHUNK)
                    xx = _rows_with_halo(u_ref, cols, i, n_chunks, True, False)
                    t = r0 + lax.broadcasted_iota(jnp.int32, (SEQ_CHUNK, POOL_G), 0)
                    d = _pool_diff(xx, w, t).astype(BF16)
                    ypre = jnp.dot(d, mixg, preferred_element_type=F32)
                    dyy = _rows_with_halo(dy_ref, cols, i, n_chunks, False, True)
                    dys = (dyy * scg).astype(BF16)
                    dsc_acc = dsc_acc + jnp.sum((dyy[:SEQ_CHUNK] * ypre).reshape(SEQ_CHUNK // 8, 8, POOL_G), axis=0)
                    dmix_acc = dmix_acc + lax.dot_general(d, dys[:SEQ_CHUNK], _DIMS["tn"],
                                                          preferred_element_type=F32)
                    dd = lax.dot_general(dys, mixg, _DIMS["nt"], preferred_element_type=F32)
                    ta = r0 + lax.broadcasted_iota(jnp.int32, (rows_a, POOL_G), 0)
                    f = dd / jnp.minimum(ta + 1, w).astype(F32)
                    k = 1
                    while k < w:
                        f = f + _shift_up(f, k)
                        k *= 2
                    du_ref[pl.ds(r0, SEQ_CHUNK), :] = (f[:SEQ_CHUNK] - dd[:SEQ_CHUNK]).astype(BF16)
                    return dmix_acc, dsc_acc

                dmix_acc, dsc_acc = lax.fori_loop(
                    0, n_chunks, chunk, (jnp.zeros((POOL_G, POOL_G), F32), jnp.zeros((8, POOL_G), F32)))
                dmix_ref[...] = dmix_acc
                dsc_ref[...] = jnp.sum(dsc_acc, axis=0, keepdims=True)

    slab = pl.BlockSpec((S_, POOL_G), lambda g: (0, g))
    mix_spec = pl.BlockSpec((None, POOL_G, POOL_G), lambda g: (g, 0, 0))
    vec = pl.BlockSpec((1, POOL_G), lambda g: (0, g))
    return _call(
        _drop_ref(body, 4), [z, dy, pool_mix, pool_scale, dz], [slab, slab, mix_spec, vec, ANY], after, name=name,
        grid=(len(POOL_WINDOWS),), out_specs=(slab, mix_spec, vec),
        out_shape=(jax.ShapeDtypeStruct(dz.shape, dz.dtype), jax.ShapeDtypeStruct((4, POOL_G, POOL_G), F32),
                   jax.ShapeDtypeStruct((1, POOL_W), F32)),
        input_output_aliases={4: 0}, compiler_params=_cparams(("parallel",)))


def _conv_specs(S_):
    slab = lambda off: pl.BlockSpec((S_, 128), lambda c, off=off: (0, off // 128 + c))
    return slab(OFF_CB), slab(OFF_CC), slab(OFF_CX)


def _conv_fwd(name, z, conv_w, after=None):
    S_ = z.shape[0]
    n_chunks = S_ // SEQ_CHUNK
    col = slice(0, 128)

    def body(b_ref, c_ref, x_ref, w_ref, y_ref):
        w0, w1, w2 = w_ref[0:1, :], w_ref[1:2, :], w_ref[2:3, :]

        def chunk(i, carry):
            r0 = pl.multiple_of(i * SEQ_CHUNK, SEQ_CHUNK)
            u = _rows_with_halo(c_ref, col, i, n_chunks, True, False) * _rows_with_halo(x_ref, col, i, n_chunks, True, False)
            y = w2 * u + w1 * _shift_down(u, 1) + w0 * _shift_down(u, 2)
            y_ref[pl.ds(r0, SEQ_CHUNK), :] = (b_ref[pl.ds(r0, SEQ_CHUNK), :] * y[HALO:]).astype(BF16)
            return carry

        lax.fori_loop(0, n_chunks, chunk, 0)

    sb, sc, sx = _conv_specs(S_)
    return _call(
        body, [z, z, z, conv_w], [sb, sc, sx, pl.BlockSpec((3, 128), lambda c: (0, c))], after,
        name=name, grid=(CONV_W // 128,),
        out_specs=pl.BlockSpec((S_, 128), lambda c: (0, c)),
        out_shape=jax.ShapeDtypeStruct((S_, CONV_W), BF16),
        compiler_params=_cparams(("parallel",)))


def _conv_bwd(name, z, dy, conv_w, after=None):
    S_ = z.shape[0]
    n_chunks = S_ // SEQ_CHUNK
    col = slice(0, 128)
    lo, hi = HALO, HALO + SEQ_CHUNK

    def body(b_ref, c_ref, x_ref, dy_ref, w_ref, db_ref, dc_ref, dx_ref, dw_ref):
        w0, w1, w2 = w_ref[0:1, :], w_ref[1:2, :], w_ref[2:3, :]

        def chunk(i, carry):
            a0, a1, a2 = carry
            r0 = pl.multiple_of(i * SEQ_CHUNK, SEQ_CHUNK)
            cc = _rows_with_halo(c_ref, col, i, n_chunks, True, True)
            xx = _rows_with_halo(x_ref, col, i, n_chunks, True, True)
            bb = _rows_with_halo(b_ref, col, i, n_chunks, True, True)
            dyy = _rows_with_halo(dy_ref, col, i, n_chunks, True, True)
            u = cc * xx
            u1 = _shift_down(u, 1)
            u2 = _shift_down(u, 2)
            y = w2 * u + w1 * u1 + w0 * u2
            dyv = dyy * bb
            du = w2 * dyv + w1 * _shift_up(dyv, 1) + w0 * _shift_up(dyv, 2)
            db_ref[pl.ds(r0, SEQ_CHUNK), :] = (dyy[lo:hi] * y[lo:hi]).astype(BF16)
            dc_ref[pl.ds(r0, SEQ_CHUNK), :] = (du[lo:hi] * xx[lo:hi]).astype(BF16)
            dx_ref[pl.ds(r0, SEQ_CHUNK), :] = (du[lo:hi] * cc[lo:hi]).astype(BF16)
            red = lambda v: jnp.sum(v.reshape(SEQ_CHUNK // 8, 8, 128), axis=0)
            dm = dyv[lo:hi]
            return a0 + red(dm * u2[lo:hi]), a1 + red(dm * u1[lo:hi]), a2 + red(dm * u[lo:hi])

        zero = jnp.zeros((8, 128), F32)
        a0, a1, a2 = lax.fori_loop(0, n_chunks, chunk, (zero, zero, zero))
        dw_ref[0:1, :] = jnp.sum(a0, axis=0, keepdims=True)
        dw_ref[1:2, :] = jnp.sum(a1, axis=0, keepdims=True)
        dw_ref[2:3, :] = jnp.sum(a2, axis=0, keepdims=True)

    sb, sc, sx = _conv_specs(S_)
    slab = pl.BlockSpec((S_, 128), lambda c: (0, c))
    wspec = pl.BlockSpec((3, 128), lambda c: (0, c))
    act = jax.ShapeDtypeStruct((S_, CONV_W), BF16)
    return _call(
        body, [z, z, z, dy, conv_w], [sb, sc, sx, slab, wspec], after, name=name, grid=(CONV_W // 128,),
        out_specs=(slab, slab, slab, wspec),
        out_shape=(act, act, act, jax.ShapeDtypeStruct((3, CONV_W), F32)),
        compiler_params=_cparams(("parallel",)))


def _head_ones(pw):
    a = lax.broadcasted_iota(jnp.int32, (pw, pw), 0) // HEAD
    b = lax.broadcasted_iota(jnp.int32, (pw, pw), 1) // HEAD
    return (a == b).astype(BF16)


def _head_sum(v, ones):
    hi = v.astype(BF16)
    lo = (v - hi.astype(F32)).astype(BF16)
    return jnp.dot(hi, ones, preferred_element_type=F32) + jnp.dot(lo, ones, preferred_element_type=F32)


def _head_norm(v, gain, ones):
    rstd = lax.rsqrt(_head_sum(v * v, ones) * (1.0 / HEAD) + EPS)
    xhat = v * rstd
    return xhat * gain, xhat, rstd


def _head_norm_bwd(dy, xhat, rstd, gain, ones):
    dxhat = dy * gain
    c = _head_sum(dxhat * xhat, ones) * (1.0 / HEAD)
    dv = rstd * (dxhat - xhat * c)
    dg = jnp.sum((dy * xhat).reshape(dy.shape[0] // 8, 8, dy.shape[1]), axis=0)
    return dv, dg


def _head_masks(pw):
    lane_head = lax.broadcasted_iota(jnp.int32, (1, pw), 1) // HEAD
    return [lane_head == h for h in range(pw // HEAD)]


def _only(mask, v):
    return jnp.where(mask, v, jnp.zeros_like(v))


def _attn_specs(S_, g, dil):
    rows = ATTN_BLK * dil
    nb = S_ // rows
    pw = 128 if dil > 1 else ATTN_OUT
    cq, ck, cv = ((OFF_Q + g * ATTN_OUT) // pw, (OFF_K + g * ATTN_OUT) // pw, (OFF_V + g * ATTN_OUT) // pw)
    return rows, nb, pw, pw // HEAD, ATTN_OUT // pw, cq, ck, cv


ATTN_BATCH = 4


def _attn_group(dil):
    return 4 if dil == 1 else 1


def _attn_block_specs(rows, grp, pw, last=None):
    step = (lambda n: n) if last is None else (lambda n: jnp.minimum(n, last))
    cur = lambda c: pl.BlockSpec((rows * grp, pw), lambda hp, n, c=c: (step(n), c + hp))
    prev = lambda c: pl.BlockSpec((rows, pw), lambda hp, n, c=c: (jnp.maximum(step(n) * grp - 1, 0), c + hp))
    return cur, prev


def _band_mask(has_prev):
    qi = lax.broadcasted_iota(jnp.int32, (ATTN_BLK, 2 * ATTN_BLK), 0)
    ki = lax.broadcasted_iota(jnp.int32, (ATTN_BLK, 2 * ATTN_BLK), 1)
    in_prev = jnp.logical_and(ki < ATTN_BLK, ki >= qi)
    if has_prev is not True:
        in_prev = jnp.logical_and(in_prev, has_prev)
    return jnp.logical_or(in_prev, jnp.logical_and(ki >= ATTN_BLK, ki - ATTN_BLK <= qi))


def _rows_of(ref, r, dil):
    if dil == 1:
        return ref[r * ATTN_BLK:(r + 1) * ATTN_BLK, :]
    return ref[pl.ds(r, ATTN_BLK, stride=dil), :]


def _put_rows(ref, r, dil, val):
    if dil == 1:
        ref[r * ATTN_BLK:(r + 1) * ATTN_BLK, :] = val.astype(ref.dtype)
    else:
        ref[pl.ds(r, ATTN_BLK, stride=dil), :] = val.astype(ref.dtype)


def _attn_fwd(name, z, q_gain, k_gain, g, dil):
    S_ = z.shape[0]
    rows, nb, pw, heads, npairs, cq, ck, cv = _attn_specs(S_, g, dil)
    scale = HEAD ** -0.5

    grp = _attn_group(dil)
    nsteps = nb // grp

    def body(q_ref, k_ref, kp_ref, v_ref, vp_ref, gq_ref, gk_ref, o_ref, l_ref):
        n = pl.program_id(1)
        ones, hmask = _head_ones(pw), _head_masks(pw)
        gq, gk = jnp.tile(gq_ref[...], (1, heads)), jnp.tile(gk_ref[...], (1, heads))
        mask_first, mask_rest = _band_mask(n > 0), _band_mask(True)
        for r0 in range(0, dil * grp, ATTN_BATCH):
            rs = range(r0, min(r0 + ATTN_BATCH, dil * grp))
            qn, kn, vv, s, p = {}, {}, {}, {}, {}
            kcn = {}
            for r in rs:
                q, kc, vc = _rows_of(q_ref, r, dil), _rows_of(k_ref, r, dil), _rows_of(v_ref, r, dil)
                kcn[r] = _head_norm(kc, gk, ones)[0]
                if dil == 1 and r > 0:
                    kpn = kcn[r - 1] if r - 1 in kcn else _head_norm(_rows_of(k_ref, r - 1, dil), gk, ones)[0]
                    vp = _rows_of(v_ref, r - 1, dil)
                else:
                    kpn, vp = _head_norm(_rows_of(kp_ref, r, dil), gk, ones)[0], _rows_of(vp_ref, r, dil)
                qn[r] = _head_norm(q, gq, ones)[0].astype(BF16)
                kn[r] = jnp.concatenate([kpn, kcn[r]], axis=0).astype(BF16)
                vv[r] = jnp.concatenate([vp, vc], axis=0).astype(BF16)
            keys = [(r, h) for r in rs for h in range(heads)]
            for r, h in keys:
                s[r, h] = lax.dot_general(_only(hmask[h], qn[r]), kn[r], _DIMS["nt"],
                                          preferred_element_type=F32) * scale
            lse, den = {}, {}
            for key in keys:
                mask = mask_rest if (dil == 1 and key[0] > 0) else mask_first
                sm = jnp.where(mask, s[key], MASK)
                m = jnp.max(sm, axis=-1, keepdims=True)
                e = jnp.exp(sm - m)
                den[key] = jnp.sum(e, axis=-1, keepdims=True)
                p[key] = e.astype(BF16)
                lse[key] = m + jnp.log(den[key])
            for r in rs:
                out = jnp.zeros((ATTN_BLK, pw), F32)
                lse_all = jnp.zeros((ATTN_BLK, pw), F32)
                for h in range(heads):
                    out = jnp.where(hmask[h], jnp.dot(p[r, h], vv[r], preferred_element_type=F32) / den[r, h], out)
                    lse_all = jnp.where(hmask[h], lse[r, h], lse_all)
                _put_rows(o_ref, r, dil, out)
                _put_rows(l_ref, r, dil, lse_all)

    cur, prev = _attn_block_specs(rows, grp, pw)
    gspec = pl.BlockSpec((1, HEAD), lambda hp, n: (0, 0))
    shp = jax.ShapeDtypeStruct((S_, ATTN_OUT), F32)
    return pl.pallas_call(
        body, name=name, grid=(npairs, nsteps),
        in_specs=[cur(cq), cur(ck), prev(ck), cur(cv), prev(cv), gspec, gspec],
        out_specs=(cur(0), cur(0)), out_shape=(shp, shp),
        compiler_params=_cparams(("parallel", "parallel")),
    )(z, z, z, z, z, q_gain, k_gain)


def _attn_combine(name, os_, ls_):
    S_ = os_[0].shape[0]

    def body(o0, o1, o2, l0, l1, l2, o_ref, l_ref):
        a, b, c = l0[...], l1[...], l2[...]
        m = jnp.maximum(jnp.maximum(a, b), c)
        ea, eb, ec = jnp.exp(a - m), jnp.exp(b - m), jnp.exp(c - m)
        zsum = ea + eb + ec
        o_ref[...] = (ea * o0[...] + eb * o1[...] + ec * o2[...]) / zsum
        l_ref[...] = m + jnp.log(zsum)

    row = pl.BlockSpec((ROW_TILE, ATTN_OUT), lambda i: (i, 0))
    shp = jax.ShapeDtypeStruct((S_, ATTN_OUT), F32)
    return pl.pallas_call(
        body, name=name, grid=(S_ // ROW_TILE,), in_specs=[row] * 6, out_specs=(row, row), out_shape=(shp, shp),
        compiler_params=_cparams(("parallel",)),
    )(*os_, *ls_)


def _attn_bwd(name, z, q_gain, k_gain, do, o, lse, g, dil, after=None):
    S_ = z.shape[0]
    rows, nb, pw, heads, npairs, cq, ck, cv = _attn_specs(S_, g, dil)
    scale = HEAD ** -0.5

    def body(q_ref, kc_ref, kp_ref, vc_ref, vp_ref, gq_ref, gk_ref, do_ref, o_ref, l_ref,
             dq_ref, dk_ref, dv_ref, dgq_ref, dgk_ref, ck_ref, cvv_ref, gq_acc, gk_acc):
        hp = pl.program_id(0)
        n = pl.program_id(1)
        ones = _head_ones(pw)
        gq, gk = jnp.tile(gq_ref[...], (1, heads)), jnp.tile(gk_ref[...], (1, heads))

        @pl.when(n == 0)
        def _():
            ck_ref[...] = jnp.zeros_like(ck_ref)
            cvv_ref[...] = jnp.zeros_like(cvv_ref)

        @pl.when(jnp.logical_and(n == 0, hp == 0))
        def _():
            gq_acc[...] = jnp.zeros_like(gq_acc)
            gk_acc[...] = jnp.zeros_like(gk_acc)

        pl.when(n < nb)(functools.partial(query_step, n, ones, gq, gk, q_ref, kc_ref, kp_ref, vc_ref, vp_ref, do_ref,
                                          o_ref, l_ref, dq_ref, dk_ref, dv_ref, ck_ref, cvv_ref, gq_acc, gk_acc))

        @pl.when(n == nb)
        def _():
            dgk = jnp.zeros((8, pw), F32)
            for r in range(dil):
                _, kphat, kprstd = _head_norm(_rows_of(kp_ref, r, dil), gk, ones)
                dk_all, dg = _head_norm_bwd(_rows_of(ck_ref, r, dil), kphat, kprstd, gk, ones)
                dgk = dgk + dg
                _put_rows(dk_ref, r, dil, dk_all)
                _put_rows(dv_ref, r, dil, _rows_of(cvv_ref, r, dil))
            gk_acc[...] += dgk

        @pl.when(jnp.logical_and(n == nb, hp == npairs - 1))
        def _():
            fold = lambda a: sum(a[:, h * HEAD:(h + 1) * HEAD] for h in range(heads))
            dgq_ref[...] = jnp.sum(fold(gq_acc[...]), axis=0, keepdims=True)
            dgk_ref[...] = jnp.sum(fold(gk_acc[...]), axis=0, keepdims=True)

    def query_step(n, ones, gq, gk, q_ref, kc_ref, kp_ref, vc_ref, vp_ref, do_ref, o_ref, l_ref,
                   dq_ref, dk_ref, dv_ref, ck_ref, cvv_ref, gq_acc, gk_acc):
        mask = _band_mask(n > 0)
        hmask = _head_masks(pw)
        dgq = jnp.zeros((8, pw), F32)
        dgk = jnp.zeros((8, pw), F32)
        for r0 in range(0, dil, ATTN_BATCH):
            rs = range(r0, min(r0 + ATTN_BATCH, dil))
            keys = [(r, h) for r in rs for h in range(heads)]
            qn, qhat, qrstd, kn, kphat, kprstd, vv, dob, delta, lse = ({} for _ in range(10))
            for r in rs:
                q, kc, kp = _rows_of(q_ref, r, dil), _rows_of(kc_ref, r, dil), _rows_of(kp_ref, r, dil)
                dov = _rows_of(do_ref, r, dil)
                qn_f, qhat[r], qrstd[r] = _head_norm(q, gq, ones)
                kpn, kphat[r], kprstd[r] = _head_norm(kp, gk, ones)
                qn[r] = qn_f.astype(BF16)
                kn[r] = jnp.concatenate([kpn, _head_norm(kc, gk, ones)[0]], axis=0).astype(BF16)
                vv[r] = jnp.concatenate([_rows_of(vp_ref, r, dil), _rows_of(vc_ref, r, dil)], axis=0).astype(BF16)
                dob[r] = dov.astype(BF16)
                delta[r] = _head_sum(dov * _rows_of(o_ref, r, dil), ones)
                lse[r] = _rows_of(l_ref, r, dil)
            s, dp = {}, {}
            for r, h in keys:
                s[r, h] = lax.dot_general(_only(hmask[h], qn[r]), kn[r], _DIMS["nt"],
                                          preferred_element_type=F32) * scale
                dp[r, h] = lax.dot_general(_only(hmask[h], dob[r]), vv[r], _DIMS["nt"], preferred_element_type=F32)
            p, ds = {}, {}
            for r, h in keys:
                col = slice(h * HEAD, h * HEAD + 1)
                pk = jnp.where(mask, jnp.exp(jnp.where(mask, s[r, h], MASK) - lse[r][:, col]), 0.0)
                ds[r, h] = (pk * (dp[r, h] - delta[r][:, col]) * scale).astype(BF16)
                p[r, h] = pk.astype(BF16)
            dqn, dkn, dvv = {}, {}, {}
            for r in rs:
                dqn[r] = jnp.zeros((ATTN_BLK, pw), F32)
                dkn[r] = jnp.zeros((2 * ATTN_BLK, pw), F32)
                dvv[r] = jnp.zeros((2 * ATTN_BLK, pw), F32)
                for h in range(heads):
                    dqn[r] = jnp.where(hmask[h], jnp.dot(ds[r, h], kn[r], preferred_element_type=F32), dqn[r])
                    dkn[r] = jnp.where(hmask[h], lax.dot_general(ds[r, h], qn[r], _DIMS["tn"],
                                                                 preferred_element_type=F32), dkn[r])
                    dvv[r] = jnp.where(hmask[h], lax.dot_general(p[r, h], dob[r], _DIMS["tn"],
                                                                 preferred_element_type=F32), dvv[r])
            for r in rs:
                dq_all, dg = _head_norm_bwd(dqn[r], qhat[r], qrstd[r], gq, ones)
                dgq = dgq + dg
                dk_all, dg = _head_norm_bwd(_rows_of(ck_ref, r, dil) + dkn[r][:ATTN_BLK], kphat[r], kprstd[r], gk, ones)
                dgk = dgk + dg
                dv_all = _rows_of(cvv_ref, r, dil) + dvv[r][:ATTN_BLK]
                _put_rows(dq_ref, r, dil, dq_all)
                _put_rows(dk_ref, r, dil, dk_all)
                _put_rows(dv_ref, r, dil, dv_all)
                _put_rows(ck_ref, r, dil, dkn[r][ATTN_BLK:])
                _put_rows(cvv_ref, r, dil, dvv[r][ATTN_BLK:])
        gq_acc[...] += dgq
        gk_acc[...] += dgk

    last = nb - 1
    cur = lambda c: pl.BlockSpec((rows, pw), lambda hp, n, c=c: (jnp.minimum(n, last), c + hp))
    prev = lambda c: pl.BlockSpec((rows, pw), lambda hp, n, c=c: (jnp.maximum(n - 1, 0), c + hp))
    gspec = pl.BlockSpec((1, HEAD), lambda hp, n: (0, 0))
    act = jax.ShapeDtypeStruct((S_, ATTN_OUT), F32)
    vec = jax.ShapeDtypeStruct((1, HEAD), F32)
    return _call(
        body, [z, z, z, z, z, q_gain, k_gain, do, o, lse],
        [cur(cq), cur(ck), prev(ck), cur(cv), prev(cv), gspec, gspec, cur(0), cur(0), cur(0)], after,
        name=name, grid=(npairs, nb + 1),
        out_specs=(cur(0), prev(0), prev(0), gspec, gspec),
        out_shape=(act, act, act, vec, vec),
        scratch_shapes=[pltpu.VMEM((rows, pw), F32), pltpu.VMEM((rows, pw), F32),
                        pltpu.VMEM((8, pw), F32), pltpu.VMEM((8, pw), F32)],
        compiler_params=_cparams(("arbitrary", "arbitrary")))


MIX_TN = 256
MIX_TM = 2048


def _sigmoid(v):
    return 1.0 / (1.0 + jnp.exp(-v))


def _mix_fwd(name, z, b_gate, ys, ws):
    S_ = z.shape[0]
    tm, tn = MIX_TM, MIX_TN
    gblk = OFF_GATE // tn

    def body(yp, yc, ya, wp, wc, wa, g0, g1, g2, b0, b1, b2, m_ref):
        acc = None
        for y_ref, w_ref, g_ref, b_ref in ((yp, wp, g0, b0), (yc, wc, g1, b1), (ya, wa, g2, b2)):
            u = lax.dot_general(y_ref[...].astype(BF16), w_ref[...], _DIMS["nt"], preferred_element_type=F32)
            t = _sigmoid(g_ref[...] + b_ref[...]) * u
            acc = t if acc is None else acc + t
        m_ref[...] = acc.astype(BF16)

    yspec = lambda w: pl.BlockSpec((tm, w), lambda i, j: (i, 0))
    wspec = lambda w: pl.BlockSpec((tn, w), lambda i, j: (j, 0))
    gspec = lambda b: pl.BlockSpec((tm, tn), lambda i, j, b=b: (i, gblk + b * (D // tn) + j))
    bspec = lambda b: pl.BlockSpec((1, tn), lambda i, j, b=b: (0, b * (D // tn) + j))
    return pl.pallas_call(
        body, name=name, grid=(S_ // tm, D // tn),
        in_specs=[yspec(POOL_W), yspec(CONV_W), yspec(ATTN_OUT), wspec(POOL_W), wspec(CONV_W), wspec(ATTN_OUT),
                  gspec(0), gspec(1), gspec(2), bspec(0), bspec(1), bspec(2)],
        out_specs=pl.BlockSpec((tm, tn), lambda i, j: (i, j)),
        out_shape=jax.ShapeDtypeStruct((S_, D), BF16),
        compiler_params=_cparams(("parallel", "parallel")),
    )(*ys, *ws, z, z, z, b_gate, b_gate, b_gate)


def _mix_bwd(name, z, b_gate, y, w, dmerged, branch, dz, after=None):
    S_ = z.shape[0]
    tm, tn = MIX_TM, MIX_TN
    width = y.shape[1]
    gblk = OFF_GATE // tn + branch * (D // tn)
    nj = D // tn

    def body(y_ref, w_ref, g_ref, b_ref, dm_ref, dy_ref, dw_ref, dg_ref, db_ref, acc_ref):
        i, j = pl.program_id(0), pl.program_id(1)
        yb = y_ref[...].astype(BF16)
        u = lax.dot_general(yb, w_ref[...], _DIMS["nt"], preferred_element_type=F32)
        sg = _sigmoid(g_ref[...] + b_ref[...])
        dm = dm_ref[...]
        du = (sg * dm).astype(BF16)
        dpre = dm * u * sg * (1.0 - sg)
        dg_ref[...] = dpre.astype(BF16)
        cols = pl.ds(pl.multiple_of(j * tn, tn), tn)
        db_ref[...] = _rows8(dpre)
        d_w = lax.dot_general(du, yb, _DIMS["tn"], preferred_element_type=F32)
        d_y = jnp.dot(du, w_ref[...], preferred_element_type=F32)

        @pl.when(i == 0)
        def _():
            dw_ref[cols, :] = d_w

        @pl.when(i > 0)
        def _():
            dw_ref[cols, :] += d_w

        @pl.when(j == 0)
        def _():
            acc_ref[...] = d_y

        @pl.when(j > 0)
        def _():
            acc_ref[...] += d_y

        @pl.when(j == nj - 1)
        def _():
            dy_ref[...] = acc_ref[...]

    rows = pl.BlockSpec((tm, width), lambda i, j: (i, 0))
    blk = pl.BlockSpec((tm, tn), lambda i, j: (i, j))
    gate = pl.BlockSpec((tm, tn), lambda i, j: (i, gblk + j))
    args = [y, w, z, b_gate, dmerged]
    in_specs = [rows, pl.BlockSpec((tn, width), lambda i, j: (j, 0)), gate,
                pl.BlockSpec((1, tn), lambda i, j: (0, branch * nj + j)), blk]
    aliases = {}
    if not isinstance(dz, jax.ShapeDtypeStruct):
        body = _drop_ref(body, len(args))
        aliases = {len(args): 2}
        args.append(dz)
        in_specs.append(ANY)
    return _call(
        body, args, in_specs, after, name=name, grid=(S_ // tm, nj),
        out_specs=(rows, pl.BlockSpec((D, width), lambda i, j: (0, 0)), gate, pl.BlockSpec((8, tn), lambda i, j: (i, j))),
        out_shape=(jax.ShapeDtypeStruct((S_, width), F32), jax.ShapeDtypeStruct((D, width), F32),
                   jax.ShapeDtypeStruct(dz.shape, dz.dtype), jax.ShapeDtypeStruct((S_ // tm * 8, D), F32)),
        scratch_shapes=[pltpu.VMEM((tm, width), F32)], input_output_aliases=aliases,
        compiler_params=_cparams(("arbitrary", "arbitrary")))


def _relu2_epilogue(acc):
    r = jnp.maximum(acc, 0.0)
    return (r * r,)


def _relu2_bwd_epilogue(acc, r):
    return (acc * (2.0 * jnp.sqrt(r.astype(F32))),)


def _residual_norm_epilogue(acc, xr, gain):
    x = xr + acc
    ms = jnp.mean(x * x, axis=-1, keepdims=True)
    return x, x * lax.rsqrt(ms + EPS) * gain


def _fold_rows(name, part, lanes=False, scale=1.0):
    R, N = part.shape

    def body(p_ref, o_ref):
        s = jnp.sum(p_ref[...], axis=0, keepdims=True)
        if lanes:
            s = jnp.sum(s, axis=1, keepdims=True)
        o_ref[...] = s * scale

    out_n = 1 if lanes else N
    return pl.pallas_call(
        body, name=name, grid=(1,), in_specs=[pl.BlockSpec((R, N), lambda i: (0, 0))],
        out_specs=pl.BlockSpec((1, out_n), lambda i: (0, 0)), out_shape=jax.ShapeDtypeStruct((1, out_n), F32))(part)


def _rows8(v):
    return jnp.sum(v.reshape(v.shape[0] // 8, 8, v.shape[1]), axis=0)


def _residual_loss_epilogue(acc, xr, target):
    e = xr + acc - target
    return e * (1.0 / D), _rows8(e * e)


def _norm_bwd_epilogue(acc, x, dres, gain):
    rstd = lax.rsqrt(jnp.mean(x * x, axis=-1, keepdims=True) + EPS)
    xhat = x * rstd
    dxhat = acc * gain
    c = jnp.mean(dxhat * xhat, axis=-1, keepdims=True)
    return dres + rstd * (dxhat - xhat * c), _rows8(acc * xhat)


def _layer_fwd(l, x, h, p, next_gain, target=None, early=None):
    t = f"l{l}_"
    z = _matmul(t + "in_proj", h, p["w_in_t"], "nt", 512, 3712, 1024, after=early)
    y_pool = _pool_fwd(t + "pool", z, p["pool_mix"], p["pool_scale"])
    os_, ls_ = [], []
    for g, (_, dil) in enumerate(ATTN_GROUPS):
        o_g, l_g = _attn_fwd(t + f"attn{g}", z, p["q_gain"], p["k_gain"], g, dil)
        os_.append(o_g)
        ls_.append(l_g)
    y_attn, lse = _attn_combine(t + "attn_mix", os_, ls_)
    y_conv = _conv_fwd(t + "conv", z, p["conv_w"], after=y_attn)
    merged = _mix_fwd(t + "merge", z, p["b_gate"], (y_pool, y_conv, y_attn),
                      (p["w_pool_up_t"], p["w_conv_out_t"], p["w_attn_up_t"]))
    x1, h2 = _matmul(t + "out_proj", merged, p["rest"], "nn", 1024, D, 1024, out_dtypes=(F32, BF16),
                     extras=((x, "mn"), (p["norm_mlp"], "n")), epilogue=_residual_norm_epilogue, b_rows=ROWS_WO)
    r = _matmul(t + "ff1", h2, p["rest"], "nt", 1024, 1024, 1024, out_dtypes=(BF16,), epilogue=_relu2_epilogue,
                b_rows=ROWS_FF1)
    if next_gain is None:
        x2, h_out = _matmul(t + "ff2", r, p["rest"], "nn", 512, D, D_FF, out_dtypes=(F32, (F32, "rows8")),
                            extras=((x1, "mn"), (target, "mn")), epilogue=_residual_loss_epilogue, b_rows=ROWS_FF2)
    else:
        x2, h_out = _matmul(t + "ff2", r, p["rest"], "nn", 512, D, D_FF, out_dtypes=(F32, BF16),
                            extras=((x1, "mn"), (next_gain, "n")), epilogue=_residual_norm_epilogue,
                            b_rows=ROWS_FF2)
    saved = dict(x=x, h=h, z=z, y_pool=y_pool, y_conv=y_conv, y_attn=y_attn, lse=lse, merged=merged,
                 x1=x1, h2=h2, r=r)
    return x2, h_out, saved


def _layer_bwd(l, dx2, p, s, pending, collective_id, start_after=None):
    t = f"l{l}_b_"
    g = {}
    rest = jax.ShapeDtypeStruct((N_DEV, REST_ROWS, D), F32)
    da = _matmul(t + "d_ff2_in", dx2, p["rest"], "nt", 1024, 1024, 1024, out_dtypes=(BF16,),
                 extras=((s["r"], "mn"),), epilogue=_relu2_bwd_epilogue,
                 after=start_after, b_rows=ROWS_FF2)
    rest = _matmul(t + "dw_ff2", s["r"], dx2, "tn", 512, 1024, 4096, into=(rest, REST_FF2, 0), after=da)
    tok = rest if pending is None else pending.chip_sums(after=rest)
    dx1, part = _matmul(t + "d_ff1_in", da, p["rest"], "nn", 512, D, D_FF, out_dtypes=(F32, (F32, "rows8")),
                        extras=((s["x1"], "mn"), (dx2, "mn"), (p["norm_mlp"], "n")), epilogue=_norm_bwd_epilogue,
                        after=tok, b_rows=ROWS_FF1)
    g["norm_mlp"] = _fold_rows(t + "d_norm_mlp", part)
    rest = _matmul(t + "dw_ff1", da, s["h2"], "tn", 512, 1024, 4096, into=(rest, REST_FF1, 0), after=dx1)
    dmerged = _matmul(t + "d_out_proj_in", dx1, p["rest"], "nt", 1024, 1024, 1024, b_rows=ROWS_WO)
    rest = _matmul(t + "dw_o", s["merged"], dx1, "tn", 128, 1024, 4096, into=(rest, REST_WO, 0), after=dmerged)
    ys = (s["y_pool"], s["y_conv"], s["y_attn"])
    names = ("w_pool_up_t", "w_conv_out_t", "w_attn_up_t")
    dys, dbs = [], []
    tok = rest
    dz = jax.ShapeDtypeStruct((dx2.shape[0], IN_COLS), BF16)
    for b in range(3):
        dy_b, dw_t, dz, db = _mix_bwd(t + f"merge{b}", s["z"], p["b_gate"], ys[b], p[names[b]], dmerged, b, dz,
                                      after=tok)
        tok = dy_b
        width = ys[b].shape[1]
        if b < 2:
            rest = rest.at[:, REST_UP:REST_ATTN, b * width:(b + 1) * width].set(
                dw_t.reshape(N_DEV, REST_ATTN - REST_UP, width))
        else:
            rest = rest.at[:, REST_ATTN:REST_ROWS, :].set(dw_t.reshape(N_DEV, REST_ROWS - REST_ATTN, D))
        dys.append(dy_b)
        dbs.append(_fold_rows(t + f"d_b_gate{b}", db))
    g["b_gate"] = jnp.concatenate(dbs, axis=1)
    rs_rest = _ReduceScatter(f"rs_rest{l}", collective_id, rest)
    pending_sum = None if pending is None else pending.finish(after=rest)
    dz, g["pool_mix"], g["pool_scale"] = _pool_bwd(t + "pool", s["z"], dys[0], p["pool_mix"], p["pool_scale"], dz,
                                                   after=rest if pending is None else pending_sum[0])
    dcb, dcc, dcx, g["conv_w"] = _conv_bwd(t + "conv", s["z"], dys[1], p["conv_w"], after=dz)
    tok = rs_rest.chip_sums(after=dcb)
    dqs, dks, dvs = [], [], []
    gq = gk = None
    for gi, (_, dil) in enumerate(ATTN_GROUPS):
        dq, dk, dv, dgq, dgk = _attn_bwd(t + f"attn{gi}", s["z"], p["q_gain"], p["k_gain"], dys[2], s["y_attn"],
                                         s["lse"], gi, dil, after=tok)
        tok = dq
        dqs.append(dq)
        dks.append(dk)
        dvs.append(dv)
        gq = dgq if gq is None else gq + dgq
        gk = dgk if gk is None else gk + dgk
    g["q_gain"], g["k_gain"] = gq, gk
    rest_sum, _ = rs_rest.finish(after=tok)
    col = OFF_CB
    for piece in [dcb, dcc, dcx] + dqs + dks + dvs:
        dz = lax.dynamic_update_slice(dz, piece.astype(BF16), (0, col))
        col += piece.shape[1]
    in_t = _matmul(t + "dw_in", dz, s["h"], "tn", 256, 1024, 4096, after=rest_sum)
    rs_in = _ReduceScatter(f"rs_in{l}", collective_id + 2, in_t.reshape(N_DEV, IN_ROWS, D))
    dx, part = _matmul(t + "d_in_proj_in", dz, p["w_in_t"], "nn", 256, D, IN_COLS, out_dtypes=(F32, (F32, "rows8")),
                       extras=((s["x"], "mn"), (dx1, "mn"), (p["norm_mix"], "n")), epilogue=_norm_bwd_epilogue,
                       after=in_t)
    g["norm_mix"] = _fold_rows(t + "d_norm_mix", part)
    return dx, g, rest_sum, pending_sum, rs_in


ANY = pl.BlockSpec(memory_space=pl.ANY)


def _mesh_pos():
    return lax.axis_index("x"), lax.axis_index("y"), lax.axis_index("c")


def _other_chips(x, y):
    return [(1 - x, y), (x, 1 - y), (1 - x, 1 - y)]


def _comm_call(name, collective_id, peers, body, arrs, out_shape, sem_counts, after=None):
    n_in, n_out = len(arrs), len(out_shape)
    if collective_id is None:
        def tc_body(*refs):
            body(refs[:n_in], refs[n_in:n_in + n_out], *refs[n_in + n_out:])

        return pl.pallas_call(
            tc_body, name=name, out_shape=tuple(out_shape), in_specs=[ANY] * n_in, out_specs=(ANY,) * n_out,
            scratch_shapes=[pltpu.SemaphoreType.DMA((n,)) for n in sem_counts])(*arrs)

    n_after = 0 if after is None else 1

    def seq_body(*refs):
        barrier = pltpu.get_barrier_semaphore()
        ps = peers(*_mesh_pos())
        for p in ps:
            pl.semaphore_signal(barrier, inc=1, device_id=p, device_id_type=MESH)
        pl.semaphore_wait(barrier, len(ps))
        outs = refs[n_in + n_after:]
        body(refs[:n_in], outs[:n_out], *outs[n_out:])

    return pl.kernel(
        seq_body, out_type=tuple(out_shape), mesh=plsc.ScalarSubcoreMesh(axis_name="seq", num_cores=1), name=name,
        scratch_types=[pltpu.SemaphoreType.DMA((n,)) for n in sem_counts],
        compiler_params=pltpu.CompilerParams(collective_id=collective_id),
    )(*arrs, *([] if after is None else [after]))


def _ordered(name, x, after):
    def body(x_ref, after_ref, o_ref):
        del x_ref, after_ref, o_ref

    return pl.pallas_call(
        body, name=name, out_shape=jax.ShapeDtypeStruct(x.shape, x.dtype), in_specs=[ANY, ANY], out_specs=ANY,
        input_output_aliases={0: 0})(x, after)


def _all_gather(name, collective_id, shard, after=None):
    R, C = shard.shape

    def peers(x, y, c):
        return [(x, y, 1 - c)] + [(*chip, c) for chip in _other_chips(x, y)]

    def body(in_refs, out_refs, send_sems, recv_sems, local_sems):
        (x_ref,), (out_ref,) = in_refs, out_refs
        x, y, c = _mesh_pos()
        me, sibling = (x, y, c), (x, y, 1 - c)
        chips = _other_chips(x, y)

        def slot(px, py, pc):
            return out_ref.at[4 * px + 2 * py + pc]

        def copy(k, block, to, src=None):
            return pltpu.make_async_remote_copy(
                src_ref=slot(*block) if src is None else src, dst_ref=slot(*block),
                send_sem=send_sems.at[k], recv_sem=recv_sems.at[k], device_id=to, device_id_type=MESH)

        mine = pltpu.make_async_copy(x_ref, slot(*me), local_sems.at[0])
        mine.start()
        first = [copy(0, me, sibling, src=x_ref)]
        first += [copy(1 + j, me, (*chip, c), src=x_ref) for j, chip in enumerate(chips)]
        for cp in first:
            cp.start()
        passed = [copy(4 + j, (*chip, c), sibling) for j, chip in enumerate(chips)]
        for j, chip in enumerate(chips):
            copy(1 + j, (*chip, c), me).wait_recv()
            passed[j].start()
        copy(0, sibling, me).wait_recv()
        for j, chip in enumerate(chips):
            copy(4 + j, (*chip, 1 - c), me).wait_recv()
        for cp in first + passed:
            cp.wait_send()
        mine.wait()

    return _comm_call(name, collective_id, peers, body, [shard],
                      [jax.ShapeDtypeStruct((N_DEV, R, C), shard.dtype)], (7, 7, 1), after)[0]


def _rs_sibling_exchange(name, collective_id, arrs):
    n = len(arrs)

    def body(in_refs, out_refs, send_sems, recv_sems):
        x, y, c = _mesh_pos()
        cps = []
        for k, (src, dst) in enumerate(zip(in_refs, out_refs)):
            src = src.at[:, 1 - c] if len(src.shape) == 4 else src
            cps.append(pltpu.make_async_remote_copy(src_ref=src, dst_ref=dst, send_sem=send_sems.at[k],
                                                    recv_sem=recv_sems.at[k], device_id=(x, y, 1 - c),
                                                    device_id_type=MESH))
        for cp in cps:
            cp.start()
        for cp in cps:
            cp.wait()

    out_shape = [jax.ShapeDtypeStruct(a.shape[:1] + a.shape[2:] if a.ndim == 4 else a.shape, a.dtype) for a in arrs]
    return _comm_call(name, collective_id, lambda x, y, c: [(x, y, 1 - c)], body, arrs, out_shape, (n, n))


def _rs_rows(r):
    return r // 2 if (r // 2) % 16 == 0 else r


def _drop_ref(body, idx):
    def wrapped(*refs):
        return body(*refs[:idx], *refs[idx + 1:])

    return wrapped


def _rs_chip_sum(name, ids, big, rbig, after=None):
    _, _, R, C = big.shape
    rows = _rs_rows(R)

    def body(ids_ref, mine_ref, theirs_ref, t16_ref, own_ref):
        p = pl.program_id(1)
        s = mine_ref[...] + theirs_ref[...]
        t16_ref[...] = s.astype(BF16)

        @pl.when(p == ids_ref[1])
        def _():
            own_ref[...] = s

    in_specs = [pl.BlockSpec((None, None, rows, C), lambda i, p, ids: (p, ids[0], i, 0)),
                pl.BlockSpec((None, rows, C), lambda i, p, ids: (p, i, 0))]
    args = [ids, big, rbig]
    if after is not None:
        body = _drop_ref(body, len(args))
        in_specs.append(ANY)
        args.append(after)
    grid_spec = pltpu.PrefetchScalarGridSpec(
        num_scalar_prefetch=1, grid=(R // rows, 4), in_specs=in_specs,
        out_specs=(pl.BlockSpec((None, rows, C), lambda i, p, ids: (p, i, 0)),
                   pl.BlockSpec((rows, C), lambda i, p, ids: (i, 0))))
    return pl.pallas_call(
        body, name=name, grid_spec=grid_spec,
        out_shape=(jax.ShapeDtypeStruct((4, R, C), BF16), jax.ShapeDtypeStruct((R, C), F32)),
        compiler_params=_cparams(("parallel", "arbitrary")),
    )(*args)


def _add2(name, a, b):
    def body(a_ref, b_ref, o_ref):
        o_ref[...] = a_ref[...] + b_ref[...]

    full = pl.BlockSpec(a.shape, lambda i: (0, 0))
    return pl.pallas_call(body, name=name, grid=(1,), in_specs=[full, full], out_specs=full,
                          out_shape=jax.ShapeDtypeStruct(a.shape, a.dtype))(a, b)


def _rs_chip_exchange(name, collective_id, arrs):
    n = len(arrs)

    def body(in_refs, out_refs, send_sems, recv_sems, local_sems):
        x, y, c = _mesh_pos()
        p_me = 2 * x + y
        chips = _other_chips(x, y)

        def part(ref, p):
            return ref.at[p] if len(ref.shape) == 3 else ref

        local = [pltpu.make_async_copy(part(src, p_me), dst.at[p_me], local_sems.at[k])
                 for k, (src, dst) in enumerate(zip(in_refs, out_refs))]
        for cp in local:
            cp.start()
        sends = []
        for j, (px, py) in enumerate(chips):
            for k, (src, dst) in enumerate(zip(in_refs, out_refs)):
                sends.append(pltpu.make_async_remote_copy(
                    src_ref=part(src, 2 * px + py), dst_ref=dst.at[p_me], send_sem=send_sems.at[n * j + k],
                    recv_sem=recv_sems.at[n * j + k], device_id=(px, py, c), device_id_type=MESH))
        for cp in sends:
            cp.start()
        for j, (px, py) in enumerate(chips):
            for k, (src, dst) in enumerate(zip(in_refs, out_refs)):
                pltpu.make_async_remote_copy(
                    src_ref=part(src, p_me), dst_ref=dst.at[2 * px + py], send_sem=send_sems.at[n * j + k],
                    recv_sem=recv_sems.at[n * j + k], device_id=(px, py, c), device_id_type=MESH).wait_recv()
        for cp in sends:
            cp.wait_send()
        for cp in local:
            cp.wait()

    out_shape = [jax.ShapeDtypeStruct((4,) + a.shape[-2:], a.dtype) for a in arrs]
    return _comm_call(name, collective_id, lambda x, y, c: [(*chip, c) for chip in _other_chips(x, y)], body, arrs,
                      out_shape, (3 * n, 3 * n, n))


def _rs_final_sum(name, ids, recv, own, rows, after=None):
    _, R, C = recv.shape
    assert R % rows == 0

    def body(ids_ref, r_ref, own_ref, o_ref):
        acc = None
        for p in range(4):
            term = jnp.where(ids_ref[1] == p, own_ref[...], r_ref[p].astype(F32))
            acc = term if acc is None else acc + term
        o_ref[...] = acc

    in_specs = [pl.BlockSpec((4, rows, C), lambda i, ids: (0, i, 0)), pl.BlockSpec((rows, C), lambda i, ids: (i, 0))]
    args = [ids, recv, own]
    if after is not None:
        body = _drop_ref(body, len(args))
        in_specs.append(ANY)
        args.append(after)
    grid_spec = pltpu.PrefetchScalarGridSpec(
        num_scalar_prefetch=1, grid=(R // rows,), in_specs=in_specs,
        out_specs=pl.BlockSpec((rows, C), lambda i, ids: (i, 0)))
    return pl.pallas_call(
        body, name=name, grid_spec=grid_spec, out_shape=jax.ShapeDtypeStruct((R, C), F32),
        compiler_params=_cparams(("parallel",)),
    )(*args)


class _ReduceScatter:
    def __init__(self, tag, collective_id, big, small=None):
        x, y, c = _mesh_pos()
        self.tag, self.cid, self.small = tag, collective_id, small
        self.ids = jnp.stack([c, 2 * x + y]).astype(jnp.int32)
        self.big = big.reshape((4, 2) + big.shape[1:])
        self.got = _rs_sibling_exchange(tag + "_sibling", collective_id,
                                        [self.big] + ([] if small is None else [small]))

    def chip_sums(self, after=None):
        t16, self.own = _rs_chip_sum(self.tag + "_chip_sum", self.ids, self.big, self.got[0], after)
        arrs = [t16]
        if self.small is not None:
            self.ts = _add2(self.tag + "_chip_sum_small", self.small, self.got[1])
            arrs.append(self.ts)
        self.recv = _rs_chip_exchange(self.tag + "_chips", self.cid + 1, arrs)
        return t16

    def finish(self, after=None):
        out = _rs_final_sum(self.tag + "_final", self.ids, self.recv[0], self.own, _rs_rows(self.own.shape[0]), after)
        out_small = None
        if self.small is not None:
            out_small = _rs_final_sum(self.tag + "_final_small", self.ids, self.recv[1], self.ts, self.small.shape[0])
        return out, out_small


def _all_reduce_small(tag, small):
    x, y, c = _mesh_pos()
    ids = jnp.stack([c, 2 * x + y]).astype(jnp.int32)
    (theirs,) = _rs_sibling_exchange(tag + "_sibling", None, [small])
    ts = _add2(tag + "_chip_sum", small, theirs)
    (recv,) = _rs_chip_exchange(tag + "_chips", None, [ts])
    return _rs_final_sum(tag + "_final", ids, recv, ts, small.shape[0])


def _adamw(name, w, g, m, v):
    R, C = w.shape
    tiles = [t for t in range(8, 513, 8) if R % t == 0]
    tr = max(tiles) if tiles else R
    c1 = 1.0 - ADAM_B1 ** ADAM_STEP
    c2 = 1.0 - ADAM_B2 ** ADAM_STEP

    def body(w_ref, g_ref, m_ref, v_ref, d_ref, nm_ref, nv_ref):
        gv = g_ref[...]
        nm = ADAM_B1 * m_ref[...] + (1.0 - ADAM_B1) * gv
        nv = ADAM_B2 * v_ref[...] + (1.0 - ADAM_B2) * (gv * gv)
        d_ref[...] = -ADAM_LR * ((nm / c1) / (jnp.sqrt(nv / c2) + ADAM_EPS) + ADAM_WD * w_ref[...])
        nm_ref[...] = nm
        nv_ref[...] = nv

    blk = pl.BlockSpec((tr, C), lambda i: (i, 0))
    shp = jax.ShapeDtypeStruct((R, C), F32)
    return pl.pallas_call(
        body, name=name, grid=(R // tr,), in_specs=[blk] * 4, out_specs=(blk,) * 3, out_shape=(shp,) * 3,
        compiler_params=_cparams(("parallel",)),
    )(w, g, m, v)


def _adamw_t(name, w, g_t, m, v):
    L, R, C = w.shape
    tr = 256
    c1 = 1.0 - ADAM_B1 ** ADAM_STEP
    c2 = 1.0 - ADAM_B2 ** ADAM_STEP

    def body(w_ref, g_ref, m_ref, v_ref, go_ref, d_ref, nm_ref, nv_ref):
        gv = g_ref[...].T
        go_ref[...] = gv
        nm = ADAM_B1 * m_ref[...] + (1.0 - ADAM_B1) * gv
        nv = ADAM_B2 * v_ref[...] + (1.0 - ADAM_B2) * (gv * gv)
        d_ref[...] = -ADAM_LR * ((nm / c1) / (jnp.sqrt(nv / c2) + ADAM_EPS) + ADAM_WD * w_ref[...])
        nm_ref[...] = nm
        nv_ref[...] = nv

    blk = pl.BlockSpec((None, tr, C), lambda l, i: (l, i, 0))
    blk_t = pl.BlockSpec((None, C, tr), lambda l, i: (l, 0, i))
    shp = jax.ShapeDtypeStruct((L, R, C), F32)
    return pl.pallas_call(
        body, name=name, grid=(L, R // tr), in_specs=[blk, blk_t, blk, blk], out_specs=(blk,) * 4,
        out_shape=(shp,) * 4, compiler_params=_cparams(("parallel", "parallel")),
    )(w, g_t, m, v)


REST_FF1 = 0
REST_FF2 = 512
REST_WO = 1024
REST_UP = 1152
REST_ATTN = 1280
REST_ROWS = 1312
IN_ROWS = IN_COLS // N_DEV
ROWS_FF1 = (REST_FF1, D_FF // N_DEV)
ROWS_FF2 = (REST_FF2, D_FF // N_DEV)
ROWS_WO = (REST_WO, D // N_DEV)
SMALL = (("norm_mix", (DEPTH, D)), ("b_gate", (DEPTH, 3 * D)), ("pool_mix", (DEPTH, 4, POOL_G, POOL_G)),
         ("pool_scale", (DEPTH, POOL_W)), ("conv_w", (DEPTH, 3, CONV_W)), ("q_gain", (DEPTH, HEAD)),
         ("k_gain", (DEPTH, HEAD)), ("norm_mlp", (DEPTH, D)))


def _pack_weight_shards(w, l):
    b = lambda a: a.astype(BF16)
    rest = jnp.concatenate([
        b(w["w_ff1"][l].T), b(w["w_ff2"][l]), b(w["w_o"][l]),
        jnp.concatenate([b(w["w_pool_up"][l].T), b(w["w_conv_out"][l].T)], axis=1),
        b(w["w_attn_up"][l].T).reshape(REST_ROWS - REST_ATTN, D)], axis=0)
    return b(w["w_in"][l].T), rest


def _unpack_gathered(g_in, g_rest, g_conv, small_w, l):
    rest = g_rest
    take = lambda r0, rows, c0=0, cols=D: rest[:, r0:r0 + rows, c0:c0 + cols].reshape(N_DEV * rows, cols)
    conv = g_conv[:, 3 * l:3 * l + 3, :CONV_W // N_DEV]
    p = {
        "w_in_t": g_in.reshape(IN_COLS, D),
        "rest": rest,
        "w_pool_up_t": take(REST_UP, 128, 0, POOL_W), "w_conv_out_t": take(REST_UP, 128, POOL_W, CONV_W),
        "w_attn_up_t": rest[:, REST_ATTN:].reshape(D, ATTN_OUT),
        "conv_w": jnp.transpose(conv, (1, 0, 2)).reshape(3, CONV_W),
        "pool_mix": small_w["pool_mix"][l],
    }
    for name in ("norm_mix", "b_gate", "pool_scale", "q_gain", "k_gain", "norm_mlp"):
        p[name] = small_w[name][l][None, :]
    return p


def _pack_small_grads(grads):
    flat = jnp.concatenate([jnp.stack([grads[l][name] for l in range(DEPTH)]).reshape(-1) for name, _ in SMALL])
    return jnp.pad(flat, (0, (-flat.shape[0]) % (8 * 128))).reshape(-1, 128)


def _unpack_grads(rest, small, dev):
    out = {
        "w_ff1_t": jnp.stack([a[REST_FF1:REST_FF1 + 512] for a in rest]),
        "w_ff2": jnp.stack([a[REST_FF2:REST_FF2 + 512] for a in rest]),
        "w_o": jnp.stack([a[REST_WO:REST_WO + 128] for a in rest]),
        "w_pool_up": jnp.stack([a[REST_UP:REST_UP + 128, :POOL_W].T for a in rest]),
        "w_conv_out": jnp.stack([a[REST_UP:REST_UP + 128, POOL_W:].T for a in rest]),
        "w_attn_up": jnp.stack([a[REST_ATTN:].reshape(D // N_DEV, ATTN_OUT).T for a in rest]),
    }
    out.update(_unpack_small_like_grads(small, dev))
    return out


def _unpack_small_like_grads(small, dev):
    out = {}
    flat = small.reshape(-1)
    off = 0
    for name, shp in SMALL:
        n = 1
        for s in shp:
            n *= s
        out[name] = flat[off:off + n].reshape(shp)
        off += n
    width = CONV_W // N_DEV
    out["conv_w"] = lax.dynamic_slice_in_dim(out["conv_w"], dev * width, width, axis=2)
    return out


def _pack_small_like_grads(src, dev):
    parts = []
    for name, shp in SMALL:
        a = src[name]
        if name == "conv_w":
            a = lax.dynamic_update_slice_in_dim(jnp.zeros(shp, F32), a, dev * (CONV_W // N_DEV), axis=2)
        parts.append(a.reshape(-1))
    flat = jnp.concatenate(parts)
    return jnp.pad(flat, (0, (-flat.shape[0]) % (8 * 128))).reshape(-1, 128)


WEIGHTS = ("norm_mix", "w_in", "b_gate", "pool_mix", "pool_scale", "conv_w", "q_gain", "k_gain", "w_pool_up",
           "w_conv_out", "w_attn_up", "w_o", "norm_mlp", "w_ff1", "w_ff2")


def kernel(x, norm_mix, w_in, b_gate, pool_mix, pool_scale, conv_w, q_gain, k_gain, w_pool_up, w_conv_out, w_attn_up, w_o, norm_mlp, w_ff1, w_ff2, loss_target, m_norm_mix, m_w_in, m_b_gate, m_pool_mix, m_pool_scale, m_conv_w, m_q_gain, m_k_gain, m_w_pool_up, m_w_conv_out, m_w_attn_up, m_w_o, m_norm_mlp, m_w_ff1, m_w_ff2, v_norm_mix, v_w_in, v_b_gate, v_pool_mix, v_pool_scale, v_conv_w, v_q_gain, v_k_gain, v_w_pool_up, v_w_conv_out, v_w_attn_up, v_w_o, v_norm_mlp, v_w_ff1, v_w_ff2):
    w = dict(zip(WEIGHTS, (norm_mix, w_in, b_gate, pool_mix, pool_scale, conv_w, q_gain, k_gain, w_pool_up,
                           w_conv_out, w_attn_up, w_o, norm_mlp, w_ff1, w_ff2)))
    m = dict(zip(WEIGHTS, (m_norm_mix, m_w_in, m_b_gate, m_pool_mix, m_pool_scale, m_conv_w, m_q_gain, m_k_gain,
                           m_w_pool_up, m_w_conv_out, m_w_attn_up, m_w_o, m_norm_mlp, m_w_ff1, m_w_ff2)))
    v = dict(zip(WEIGHTS, (v_norm_mix, v_w_in, v_b_gate, v_pool_mix, v_pool_scale, v_conv_w, v_q_gain, v_k_gain,
                           v_w_pool_up, v_w_conv_out, v_w_attn_up, v_w_o, v_norm_mlp, v_w_ff1, v_w_ff2)))
    xi, yi, ci = _mesh_pos()
    dev = 4 * xi + 2 * yi + ci

    saved, params = [], []
    act = x[0]
    w_in_t, m_in_t, v_in_t = (jnp.swapaxes(src["w_in"], 1, 2) for src in (w, m, v))
    h = _rmsnorm_fwd("l0_norm_mix", act, w["norm_mix"][0][None])
    conv_shard = jnp.pad(w["conv_w"].reshape(DEPTH * 3, CONV_W // N_DEV), ((0, 8 - DEPTH * 3), (0, 128 - CONV_W // N_DEV)))
    g_conv = _all_gather("gather_conv", None, conv_shard)
    for l in range(DEPTH):
        s_in, s_rest = _pack_weight_shards(w, l)
        g_in = _all_gather(f"gather_in{l}", 1 + 2 * l, s_in)
        g_rest = _all_gather(f"gather_rest{l}", 2 + 2 * l, s_rest)
        params.append(_unpack_gathered(g_in, g_rest, g_conv, w, l))
    for l in range(DEPTH):
        act, h, s = _layer_fwd(l, act, h, params[l], w["norm_mix"][l + 1][None] if l + 1 < DEPTH else None,
                               loss_target[0])
        saved.append(s)
    dy = act
    loss = _fold_rows("loss_sum", h, lanes=True, scale=0.5 / D)
    total = lax.psum(loss[0, 0], ("x", "y", "c"))
    in_t, rest, grads, pending = [None] * DEPTH, [None] * DEPTH, [None] * DEPTH, None
    start = total.reshape(1, 1)
    for l in reversed(range(DEPTH)):
        dy, grads[l], rest[l], done, pending = _layer_bwd(l, dy, params[l], saved[l], pending, 5 + 4 * l, start)
        start = None
        if done is not None:
            in_t[l + 1] = done[0]
    dx = dy
    g_small = _all_reduce_small("ar_small", _pack_small_grads(grads))
    pending.chip_sums(after=g_small)
    g = _unpack_grads(rest, g_small, dev)

    delta, new_m, new_v = {}, {}, {}
    g["w_ff1"], delta["w_ff1"], new_m["w_ff1"], new_v["w_ff1"] = _adamw_t("adamw_w_ff1", w["w_ff1"], g["w_ff1_t"],
                                                                          m["w_ff1"], v["w_ff1"])
    for name in ("w_pool_up", "w_conv_out", "w_attn_up", "w_o", "w_ff2"):
        shp = w[name].shape
        two_d = (shp[0] * shp[1], shp[2])
        d_, m_, v_ = _adamw("adamw_" + name, w[name].reshape(two_d), g[name].reshape(two_d),
                            m[name].reshape(two_d), v[name].reshape(two_d))
        delta[name], new_m[name], new_v[name] = d_.reshape(shp), m_.reshape(shp), v_.reshape(shp)
    for name, shp in SMALL:
        shp = w[name].shape
        two_d = (-1, shp[-1])
        d_, m_, v_ = _adamw("adamw_" + name, w[name].reshape(two_d), g[name].reshape(two_d),
                            m[name].reshape(two_d), v[name].reshape(two_d))
        delta[name], new_m[name], new_v[name] = d_.reshape(shp), m_.reshape(shp), v_.reshape(shp)
    in_t[0], _ = pending.finish(after=new_v[SMALL[-1][0]])
    g_t = jnp.stack(in_t)
    two_d = (g_t.shape[0] * g_t.shape[1], g_t.shape[2])
    d_, m_, v_ = _adamw("adamw_w_in", *[a.reshape(two_d) for a in (w_in_t, g_t, m_in_t, v_in_t)])
    g["w_in"] = jnp.swapaxes(g_t, 1, 2)
    delta["w_in"], new_m["w_in"], new_v["w_in"] = (jnp.swapaxes(a.reshape(g_t.shape), 1, 2) for a in (d_, m_, v_))

    return (total, dx[None], *[g[n] for n in WEIGHTS], *[delta[n] for n in WEIGHTS],
            *[new_m[n] for n in WEIGHTS], *[new_v[n] for n in WEIGHTS])
```

```python
import functools

import jax
import jax.numpy as jnp
from jax import lax
from jax.experimental import pallas as pl
from jax.experimental.pallas import tpu as pltpu
from jax.experimental.pallas import tpu_sc as plsc

F32 = jnp.float32
BF16 = jnp.bfloat16
MESH = pl.DeviceIdType.MESH

D = 1024
SEQ = 4096
DEPTH = 2
N_DEV = 8
POOL_WINDOWS = (2, 4, 8, 16)
POOL_W = 512
POOL_G = 128
CONV_W = 512
HEAD = 64
ATTN_GROUPS = ((128, 1), (512, 4), (2048, 16))
HPG = 4
ATTN_W = 768
ATTN_OUT = 256
ATTN_BLK = 128
D_FF = 4096
EPS = 1e-6
MASK = -1e30
OFF_POOL = 0
OFF_CB = 512
OFF_CC = 1024
OFF_CX = 1536
OFF_Q = 2048
OFF_K = 2816
OFF_V = 3584
OFF_GATE = 4352
IN_COLS = 7424
ADAM_LR = 0.001
ADAM_B1 = 0.9
ADAM_B2 = 0.999
ADAM_EPS = 1e-08
ADAM_WD = 0.01
ADAM_STEP = 10

ROW_TILE = 512
SEQ_CHUNK = 256
HALO = 16
VMEM_LIMIT = 56 * 1024 * 1024


def _cparams(sem=None):
    return pltpu.CompilerParams(dimension_semantics=sem, vmem_limit_bytes=VMEM_LIMIT)


def _call(body, args, in_specs, after=None, **kw):
    if after is not None:
        after = list(after) if isinstance(after, (list, tuple)) else [after]
        n, k, inner = len(args), len(after), body

        def body(*refs):
            return inner(*refs[:n], *refs[n + k:])

        args = list(args) + after
        in_specs = list(in_specs) + [pl.BlockSpec(memory_space=pl.ANY)] * k
    return pl.pallas_call(body, in_specs=in_specs, **kw)(*args)


_DIMS = {"nn": (((1,), (0,)), ((), ())), "nt": (((1,), (1,)), ((), ())), "tn": (((0,), (0,)), ((), ()))}


def _matmul(name, a, b, mode, tm, tn, tk, out_dtypes=(F32,), extras=(), epilogue=None, into=None, after=None,
            b_rows=None):
    if mode == "tn":
        K, M = a.shape
    else:
        M, K = a.shape
    b_shape = b.shape if b_rows is None else (b.shape[0] * b_rows[1], b.shape[2])
    N = b_shape[0] if mode == "nt" else b_shape[1]
    assert b_shape[1 if mode == "nt" else 0] == K, (name, b_shape, K)
    assert M % tm == 0 and N % tn == 0 and K % tk == 0, (name, M, N, K, tm, tn, tk)
    nk = K // tk
    n_extra = len(extras)
    n_out = len(out_dtypes)
    dims = _DIMS[mode]
    n_alias = 0 if into is None or isinstance(into[0], jax.ShapeDtypeStruct) else 1

    def body(a_ref, b_ref, *rest):
        extra_refs = rest[:n_extra]
        out_refs = rest[n_extra + n_alias:n_extra + n_alias + n_out]

        def finish(acc):
            if epilogue is None:
                res = (acc,)
            else:
                res = epilogue(acc, *[r[...] for r in extra_refs])
            for o_ref, v in zip(out_refs, res):
                o_ref[...] = v.astype(o_ref.dtype)

        bv = b_ref[...]
        if b_rows is not None:
            bv = bv.reshape(-1, bv.shape[-1])
        part = lax.dot_general(a_ref[...].astype(BF16), bv.astype(BF16), dims, preferred_element_type=F32)
        if nk == 1:
            finish(part)
        else:
            acc_ref = rest[-1]
            k = pl.program_id(2)

            @pl.when(k == 0)
            def _():
                acc_ref[...] = part

            @pl.when(k > 0)
            def _():
                acc_ref[...] += part

            @pl.when(k == nk - 1)
            def _():
                finish(acc_ref[...])

    if mode == "tn":
        a_spec = pl.BlockSpec((tk, tm), lambda i, j, k: (k, i))
    else:
        a_spec = pl.BlockSpec((tm, tk), lambda i, j, k: (i, k))
    if b_rows is not None:
        row0, rows = b_rows
        t_rows = tn if mode == "nt" else tk
        assert t_rows % rows == 0 and row0 % rows == 0, (name, t_rows, b_rows)
        if mode == "nt":
            b_spec = pl.BlockSpec((t_rows // rows, rows, tk), lambda i, j, k: (j, row0 // rows, k))
        else:
            b_spec = pl.BlockSpec((t_rows // rows, rows, tn), lambda i, j, k: (k, row0 // rows, j))
    elif mode == "nt":
        b_spec = pl.BlockSpec((tn, tk), lambda i, j, k: (j, k))
    else:
        b_spec = pl.BlockSpec((tk, tn), lambda i, j, k: (k, j))
    in_specs = [a_spec, b_spec]
    args = [a, b]
    for arr, kind in extras:
        if kind == "mn":
            in_specs.append(pl.BlockSpec((tm, tn), lambda i, j, k: (i, j)))
        else:
            in_specs.append(pl.BlockSpec((1, tn), lambda i, j, k: (0, j)))
        args.append(arr)
    part = [isinstance(dt, tuple) for dt in out_dtypes]
    out_shape = tuple(jax.ShapeDtypeStruct((M // tm * 8, N), dt[0]) if p else jax.ShapeDtypeStruct((M, N), dt)
                      for dt, p in zip(out_dtypes, part))
    out_specs = tuple(pl.BlockSpec((8 if p else tm, tn), lambda i, j, k: (i, j)) for p in part)
    aliases = {}
    if into is not None:
        buf, row0, col0 = into
        assert n_out == 1 and (M // N_DEV) % tm == 0 and row0 % tm == 0 and col0 % tn == 0
        per_dev = M // N_DEV // tm
        out_shape = (jax.ShapeDtypeStruct(buf.shape, buf.dtype),)
        out_specs = (pl.BlockSpec((None, tm, tn), lambda i, j, k: (i // per_dev, row0 // tm + i % per_dev,
                                                                   col0 // tn + j)),)
        if not isinstance(buf, jax.ShapeDtypeStruct):
            aliases = {len(args): 0}
            in_specs.append(pl.BlockSpec(memory_space=pl.ANY))
            args.append(buf)
    scratch = [] if nk == 1 else [pltpu.VMEM((tm, tn), F32)]
    res = _call(
        body, args, in_specs, after, name=name, grid=(M // tm, N // tn, nk), out_specs=out_specs,
        out_shape=out_shape, scratch_shapes=scratch, input_output_aliases=aliases,
        compiler_params=_cparams(("parallel", "parallel", "arbitrary")))
    return res if n_out > 1 else res[0]


def _rmsnorm_fwd(name, x, gain, after=None):
    S_, D_ = x.shape

    def body(x_ref, g_ref, h_ref):
        xf = x_ref[...]
        ms = jnp.mean(xf * xf, axis=-1, keepdims=True)
        h_ref[...] = (xf * lax.rsqrt(ms + EPS) * g_ref[...]).astype(BF16)

    return _call(
        body, [x, gain], [pl.BlockSpec((ROW_TILE, D_), lambda i: (i, 0)), pl.BlockSpec((1, D_), lambda i: (0, 0))],
        after, name=name, grid=(S_ // ROW_TILE,),
        out_specs=pl.BlockSpec((ROW_TILE, D_), lambda i: (i, 0)),
        out_shape=jax.ShapeDtypeStruct((S_, D_), BF16),
        compiler_params=_cparams(("parallel",)))


def _rmsnorm_bwd(name, x, gain, dh, dres, after=None):
    S_, D_ = x.shape
    n = S_ // ROW_TILE

    def body(x_ref, g_ref, dh_ref, dres_ref, dx_ref, dg_ref, acc_ref):
        i = pl.program_id(0)
        xf = x_ref[...]
        rstd = lax.rsqrt(jnp.mean(xf * xf, axis=-1, keepdims=True) + EPS)
        xhat = xf * rstd
        dhv = dh_ref[...]
        dxhat = dhv * g_ref[...]
        c = jnp.mean(dxhat * xhat, axis=-1, keepdims=True)
        dx_ref[...] = dres_ref[...] + rstd * (dxhat - xhat * c)
        part = jnp.sum((dhv * xhat).reshape(ROW_TILE // 8, 8, D_), axis=0)

        @pl.when(i == 0)
        def _():
            acc_ref[...] = part

        @pl.when(i > 0)
        def _():
            acc_ref[...] += part

        @pl.when(i == n - 1)
        def _():
            dg_ref[...] = jnp.sum(acc_ref[...], axis=0, keepdims=True)

    row = pl.BlockSpec((ROW_TILE, D_), lambda i: (i, 0))
    vec = pl.BlockSpec((1, D_), lambda i: (0, 0))
    return _call(
        body, [x, gain, dh, dres], [row, vec, row, row], after, name=name, grid=(n,), out_specs=(row, vec),
        out_shape=(jax.ShapeDtypeStruct((S_, D_), F32), jax.ShapeDtypeStruct((1, D_), F32)),
        scratch_shapes=[pltpu.VMEM((8, D_), F32)],
        compiler_params=_cparams(("arbitrary",)))


def _loss_head(name, y, target):
    S_, D_ = y.shape
    n = S_ // ROW_TILE

    def body(y_ref, t_ref, dy_ref, l_ref, acc_ref):
        i = pl.program_id(0)
        e = y_ref[...] - t_ref[...]
        dy_ref[...] = e * (1.0 / D_)
        part = jnp.sum((e * e).reshape(ROW_TILE // 8, 8, D_), axis=0)

        @pl.when(i == 0)
        def _():
            acc_ref[...] = part

        @pl.when(i > 0)
        def _():
            acc_ref[...] += part

        @pl.when(i == n - 1)
        def _():
            s = jnp.sum(acc_ref[...], axis=1, keepdims=True)
            l_ref[...] = jnp.sum(s, axis=0, keepdims=True) * (0.5 / D_)

    row = pl.BlockSpec((ROW_TILE, D_), lambda i: (i, 0))
    return pl.pallas_call(
        body, name=name, grid=(n,), in_specs=[row, row],
        out_specs=(row, pl.BlockSpec((1, 1), lambda i: (0, 0))),
        out_shape=(jax.ShapeDtypeStruct((S_, D_), F32), jax.ShapeDtypeStruct((1, 1), F32)),
        scratch_shapes=[pltpu.VMEM((8, D_), F32)],
        compiler_params=_cparams(("arbitrary",)),
    )(y, target)


def _rows_with_halo(ref, cols, i, n_chunks, before, after):
    r0 = pl.multiple_of(i * SEQ_CHUNK, SEQ_CHUNK)
    parts = []
    if before:
        h0 = pl.multiple_of(jnp.maximum(r0 - HALO, 0), 8)
        halo = ref[pl.ds(h0, HALO), cols]
        parts.append(jnp.where(i > 0, halo, jnp.zeros_like(halo)))
    parts.append(ref[pl.ds(r0, SEQ_CHUNK), cols])
    if after:
        a0 = pl.multiple_of(jnp.minimum(r0 + SEQ_CHUNK, (n_chunks - 1) * SEQ_CHUNK + SEQ_CHUNK - HALO), 8)
        halo = ref[pl.ds(a0, HALO), cols]
        parts.append(jnp.where(i < n_chunks - 1, halo, jnp.zeros_like(halo)))
    return parts[0] if len(parts) == 1 else jnp.concatenate(parts, axis=0)


def _shift_down(v, k):
    return pltpu.roll(v, k, 0)


def _shift_up(v, k):
    return pltpu.roll(v, v.shape[0] - k, 0)


def _pool_diff(xx, w, t_main):
    s = xx
    k = 1
    while k < w:
        s = s + _shift_down(s, k)
        k *= 2
    cnt = jnp.minimum(t_main + 1, w).astype(F32)
    return s[HALO:] / cnt - xx[HALO:]


def _pool_fwd(name, z, pool_mix, pool_scale):
    S_ = z.shape[0]
    n_chunks = S_ // SEQ_CHUNK

    cols = slice(0, POOL_G)

    def body(u_ref, mix_ref, sc_ref, y_ref):
        mixg = mix_ref[...].astype(BF16)
        scg = sc_ref[...]
        for g, w in enumerate(POOL_WINDOWS):
            @pl.when(pl.program_id(0) == g)
            def _(w=w):
                def chunk(i, carry):
                    r0 = pl.multiple_of(i * SEQ_CHUNK, SEQ_CHUNK)
                    xx = _rows_with_halo(u_ref, cols, i, n_chunks, True, False)
                    t = r0 + lax.broadcasted_iota(jnp.int32, (SEQ_CHUNK, POOL_G), 0)
                    d = _pool_diff(xx, w, t)
                    y = jnp.dot(d.astype(BF16), mixg, preferred_element_type=F32) * scg
                    y_ref[pl.ds(r0, SEQ_CHUNK), :] = y.astype(BF16)
                    return carry

                lax.fori_loop(0, n_chunks, chunk, 0)

    slab = pl.BlockSpec((S_, POOL_G), lambda g: (0, g))
    return pl.pallas_call(
        body, name=name, grid=(len(POOL_WINDOWS),),
        in_specs=[slab, pl.BlockSpec((None, POOL_G, POOL_G), lambda g: (g, 0, 0)),
                  pl.BlockSpec((1, POOL_G), lambda g: (0, g))],
        out_specs=slab, out_shape=jax.ShapeDtypeStruct((S_, POOL_W), BF16),
        compiler_params=_cparams(("parallel",)),
    )(z, pool_mix, pool_scale)


def _pool_bwd(name, z, dy, pool_mix, pool_scale, dz, after=None):
    S_ = z.shape[0]
    n_chunks = S_ // SEQ_CHUNK
    rows_a = SEQ_CHUNK + HALO

    cols = slice(0, POOL_G)

    def body(u_ref, dy_ref, mix_ref, sc_ref, du_ref, dmix_ref, dsc_ref):
        mixg = mix_ref[...].astype(BF16)
        scg = sc_ref[...]
        for g, w in enumerate(POOL_WINDOWS):
            @pl.when(pl.program_id(0) == g)
            def _(w=w):
                def chunk(i, carry):
                    dmix_acc, dsc_acc = carry
                    r0 = pl.multiple_of(i * SEQ_CHUNK, SEQ_CHUNK)
                    xx = _rows_with_halo(u_ref, cols, i, n_chunks, True, False)
                    t = r0 + lax.broadcasted_iota(jnp.int32, (SEQ_CHUNK, POOL_G), 0)
                    d = _pool_diff(xx, w, t).astype(BF16)
                    ypre = jnp.dot(d, mixg, preferred_element_type=F32)
                    dyy = _rows_with_halo(dy_ref, cols, i, n_chunks, False, True)
                    dys = (dyy * scg).astype(BF16)
                    dsc_acc = dsc_acc + jnp.sum((dyy[:SEQ_CHUNK] * ypre).reshape(SEQ_CHUNK // 8, 8, POOL_G), axis=0)
                    dmix_acc = dmix_acc + lax.dot_general(d, dys[:SEQ_CHUNK], _DIMS["tn"],
                                                          preferred_element_type=F32)
                    dd = lax.dot_general(dys, mixg, _DIMS["nt"], preferred_element_type=F32)
                    ta = r0 + lax.broadcasted_iota(jnp.int32, (rows_a, POOL_G), 0)
                    f = dd / jnp.minimum(ta + 1, w).astype(F32)
                    k = 1
                    while k < w:
                        f = f + _shift_up(f, k)
                        k *= 2
                    du_ref[pl.ds(r0, SEQ_CHUNK), :] = (f[:SEQ_CHUNK] - dd[:SEQ_CHUNK]).astype(BF16)
                    return dmix_acc, dsc_acc

                dmix_acc, dsc_acc = lax.fori_loop(
                    0, n_chunks, chunk, (jnp.zeros((POOL_G, POOL_G), F32), jnp.zeros((8, POOL_G), F32)))
                dmix_ref[...] = dmix_acc
                dsc_ref[...] = jnp.sum(dsc_acc, axis=0, keepdims=True)

    slab = pl.BlockSpec((S_, POOL_G), lambda g: (0, g))
    mix_spec = pl.BlockSpec((None, POOL_G, POOL_G), lambda g: (g, 0, 0))
    vec = pl.BlockSpec((1, POOL_G), lambda g: (0, g))
    return _call(
        _drop_ref(body, 4), [z, dy, pool_mix, pool_scale, dz], [slab, slab, mix_spec, vec, ANY], after, name=name,
        grid=(len(POOL_WINDOWS),), out_specs=(slab, mix_spec, vec),
        out_shape=(jax.ShapeDtypeStruct(dz.shape, dz.dtype), jax.ShapeDtypeStruct((4, POOL_G, POOL_G), F32),
                   jax.ShapeDtypeStruct((1, POOL_W), F32)),
        input_output_aliases={4: 0}, compiler_params=_cparams(("parallel",)))


def _conv_specs(S_):
    slab = lambda off: pl.BlockSpec((S_, 128), lambda c, off=off: (0, off // 128 + c))
    return slab(OFF_CB), slab(OFF_CC), slab(OFF_CX)


def _conv_fwd(name, z, conv_w, after=None):
    S_ = z.shape[0]
    n_chunks = S_ // SEQ_CHUNK
    col = slice(0, 128)

    def body(b_ref, c_ref, x_ref, w_ref, y_ref):
        w0, w1, w2 = w_ref[0:1, :], w_ref[1:2, :], w_ref[2:3, :]

        def chunk(i, carry):
            r0 = pl.multiple_of(i * SEQ_CHUNK, SEQ_CHUNK)
            u = _rows_with_halo(c_ref, col, i, n_chunks, True, False) * _rows_with_halo(x_ref, col, i, n_chunks, True, False)
            y = w2 * u + w1 * _shift_down(u, 1) + w0 * _shift_down(u, 2)
            y_ref[pl.ds(r0, SEQ_CHUNK), :] = (b_ref[pl.ds(r0, SEQ_CHUNK), :] * y[HALO:]).astype(BF16)
            return carry

        lax.fori_loop(0, n_chunks, chunk, 0)

    sb, sc, sx = _conv_specs(S_)
    return _call(
        body, [z, z, z, conv_w], [sb, sc, sx, pl.BlockSpec((3, 128), lambda c: (0, c))], after,
        name=name, grid=(CONV_W // 128,),
        out_specs=pl.BlockSpec((S_, 128), lambda c: (0, c)),
        out_shape=jax.ShapeDtypeStruct((S_, CONV_W), BF16),
        compiler_params=_cparams(("parallel",)))


def _conv_bwd(name, z, dy, conv_w, after=None):
    S_ = z.shape[0]
    n_chunks = S_ // SEQ_CHUNK
    col = slice(0, 128)
    lo, hi = HALO, HALO + SEQ_CHUNK

    def body(b_ref, c_ref, x_ref, dy_ref, w_ref, db_ref, dc_ref, dx_ref, dw_ref):
        w0, w1, w2 = w_ref[0:1, :], w_ref[1:2, :], w_ref[2:3, :]

        def chunk(i, carry):
            a0, a1, a2 = carry
            r0 = pl.multiple_of(i * SEQ_CHUNK, SEQ_CHUNK)
            cc = _rows_with_halo(c_ref, col, i, n_chunks, True, True)
            xx = _rows_with_halo(x_ref, col, i, n_chunks, True, True)
            bb = _rows_with_halo(b_ref, col, i, n_chunks, True, True)
            dyy = _rows_with_halo(dy_ref, col, i, n_chunks, True, True)
            u = cc * xx
            u1 = _shift_down(u, 1)
            u2 = _shift_down(u, 2)
            y = w2 * u + w1 * u1 + w0 * u2
            dyv = dyy * bb
            du = w2 * dyv + w1 * _shift_up(dyv, 1) + w0 * _shift_up(dyv, 2)
            db_ref[pl.ds(r0, SEQ_CHUNK), :] = (dyy[lo:hi] * y[lo:hi]).astype(BF16)
            dc_ref[pl.ds(r0, SEQ_CHUNK), :] = (du[lo:hi] * xx[lo:hi]).astype(BF16)
            dx_ref[pl.ds(r0, SEQ_CHUNK), :] = (du[lo:hi] * cc[lo:hi]).astype(BF16)
            red = lambda v: jnp.sum(v.reshape(SEQ_CHUNK // 8, 8, 128), axis=0)
            dm = dyv[lo:hi]
            return a0 + red(dm * u2[lo:hi]), a1 + red(dm * u1[lo:hi]), a2 + red(dm * u[lo:hi])

        zero = jnp.zeros((8, 128), F32)
        a0, a1, a2 = lax.fori_loop(0, n_chunks, chunk, (zero, zero, zero))
        dw_ref[0:1, :] = jnp.sum(a0, axis=0, keepdims=True)
        dw_ref[1:2, :] = jnp.sum(a1, axis=0, keepdims=True)
        dw_ref[2:3, :] = jnp.sum(a2, axis=0, keepdims=True)

    sb, sc, sx = _conv_specs(S_)
    slab = pl.BlockSpec((S_, 128), lambda c: (0, c))
    wspec = pl.BlockSpec((3, 128), lambda c: (0, c))
    act = jax.ShapeDtypeStruct((S_, CONV_W), BF16)
    return _call(
        body, [z, z, z, dy, conv_w], [sb, sc, sx, slab, wspec], after, name=name, grid=(CONV_W // 128,),
        out_specs=(slab, slab, slab, wspec),
        out_shape=(act, act, act, jax.ShapeDtypeStruct((3, CONV_W), F32)),
        compiler_params=_cparams(("parallel",)))


def _head_ones(pw):
    a = lax.broadcasted_iota(jnp.int32, (pw, pw), 0) // HEAD
    b = lax.broadcasted_iota(jnp.int32, (pw, pw), 1) // HEAD
    return (a == b).astype(BF16)


def _head_sum(v, ones):
    hi = v.astype(BF16)
    lo = (v - hi.astype(F32)).astype(BF16)
    return jnp.dot(hi, ones, preferred_element_type=F32) + jnp.dot(lo, ones, preferred_element_type=F32)


def _head_norm(v, gain, ones):
    rstd = lax.rsqrt(_head_sum(v * v, ones) * (1.0 / HEAD) + EPS)
    xhat = v * rstd
    return xhat * gain, xhat, rstd


def _head_norm_bwd(dy, xhat, rstd, gain, ones):
    dxhat = dy * gain
    c = _head_sum(dxhat * xhat, ones) * (1.0 / HEAD)
    dv = rstd * (dxhat - xhat * c)
    dg = jnp.sum((dy * xhat).reshape(dy.shape[0] // 8, 8, dy.shape[1]), axis=0)
    return dv, dg


def _head_masks(pw):
    lane_head = lax.broadcasted_iota(jnp.int32, (1, pw), 1) // HEAD
    return [lane_head == h for h in range(pw // HEAD)]


def _only(mask, v):
    return jnp.where(mask, v, jnp.zeros_like(v))


def _attn_specs(S_, g, dil):
    rows = ATTN_BLK * dil
    nb = S_ // rows
    pw = 128 if dil > 1 else ATTN_OUT
    cq, ck, cv = ((OFF_Q + g * ATTN_OUT) // pw, (OFF_K + g * ATTN_OUT) // pw, (OFF_V + g * ATTN_OUT) // pw)
    return rows, nb, pw, pw // HEAD, ATTN_OUT // pw, cq, ck, cv


ATTN_BATCH = 4


def _attn_group(dil):
    return 4 if dil == 1 else 1


def _attn_block_specs(rows, grp, pw, last=None):
    step = (lambda n: n) if last is None else (lambda n: jnp.minimum(n, last))
    cur = lambda c: pl.BlockSpec((rows * grp, pw), lambda hp, n, c=c: (step(n), c + hp))
    prev = lambda c: pl.BlockSpec((rows, pw), lambda hp, n, c=c: (jnp.maximum(step(n) * grp - 1, 0), c + hp))
    return cur, prev


def _band_mask(has_prev):
    qi = lax.broadcasted_iota(jnp.int32, (ATTN_BLK, 2 * ATTN_BLK), 0)
    ki = lax.broadcasted_iota(jnp.int32, (ATTN_BLK, 2 * ATTN_BLK), 1)
    in_prev = jnp.logical_and(ki < ATTN_BLK, ki >= qi)
    if has_prev is not True:
        in_prev = jnp.logical_and(in_prev, has_prev)
    return jnp.logical_or(in_prev, jnp.logical_and(ki >= ATTN_BLK, ki - ATTN_BLK <= qi))


def _rows_of(ref, r, dil):
    if dil == 1:
        return ref[r * ATTN_BLK:(r + 1) * ATTN_BLK, :]
    return ref[pl.ds(r, ATTN_BLK, stride=dil), :]


def _put_rows(ref, r, dil, val):
    if dil == 1:
        ref[r * ATTN_BLK:(r + 1) * ATTN_BLK, :] = val.astype(ref.dtype)
    else:
        ref[pl.ds(r, ATTN_BLK, stride=dil), :] = val.astype(ref.dtype)


def _attn_fwd(name, z, q_gain, k_gain, g, dil):
    S_ = z.shape[0]
    rows, nb, pw, heads, npairs, cq, ck, cv = _attn_specs(S_, g, dil)
    scale = HEAD ** -0.5

    grp = _attn_group(dil)
    nsteps = nb // grp

    def body(q_ref, k_ref, kp_ref, v_ref, vp_ref, gq_ref, gk_ref, o_ref, l_ref):
        n = pl.program_id(1)
        ones, hmask = _head_ones(pw), _head_masks(pw)
        gq, gk = jnp.tile(gq_ref[...], (1, heads)), jnp.tile(gk_ref[...], (1, heads))
        mask_first, mask_rest = _band_mask(n > 0), _band_mask(True)
        for r0 in range(0, dil * grp, ATTN_BATCH):
            rs = range(r0, min(r0 + ATTN_BATCH, dil * grp))
            qn, kn, vv, s, p = {}, {}, {}, {}, {}
            kcn = {}
            for r in rs:
                q, kc, vc = _rows_of(q_ref, r, dil), _rows_of(k_ref, r, dil), _rows_of(v_ref, r, dil)
                kcn[r] = _head_norm(kc, gk, ones)[0]
                if dil == 1 and r > 0:
                    kpn = kcn[r - 1] if r - 1 in kcn else _head_norm(_rows_of(k_ref, r - 1, dil), gk, ones)[0]
                    vp = _rows_of(v_ref, r - 1, dil)
                else:
                    kpn, vp = _head_norm(_rows_of(kp_ref, r, dil), gk, ones)[0], _rows_of(vp_ref, r, dil)
                qn[r] = _head_norm(q, gq, ones)[0].astype(BF16)
                kn[r] = jnp.concatenate([kpn, kcn[r]], axis=0).astype(BF16)
                vv[r] = jnp.concatenate([vp, vc], axis=0).astype(BF16)
            keys = [(r, h) for r in rs for h in range(heads)]
            for r, h in keys:
                s[r, h] = lax.dot_general(_only(hmask[h], qn[r]), kn[r], _DIMS["nt"],
                                          preferred_element_type=F32) * scale
            lse, den = {}, {}
            for key in keys:
                mask = mask_rest if (dil == 1 and key[0] > 0) else mask_first
                sm = jnp.where(mask, s[key], MASK)
                m = jnp.max(sm, axis=-1, keepdims=True)
                e = jnp.exp(sm - m)
                den[key] = jnp.sum(e, axis=-1, keepdims=True)
                p[key] = e.astype(BF16)
                lse[key] = m + jnp.log(den[key])
            for r in rs:
                out = jnp.zeros((ATTN_BLK, pw), F32)
                lse_all = jnp.zeros((ATTN_BLK, pw), F32)
                for h in range(heads):
                    out = jnp.where(hmask[h], jnp.dot(p[r, h], vv[r], preferred_element_type=F32) / den[r, h], out)
                    lse_all = jnp.where(hmask[h], lse[r, h], lse_all)
                _put_rows(o_ref, r, dil, out)
                _put_rows(l_ref, r, dil, lse_all)

    cur, prev = _attn_block_specs(rows, grp, pw)
    gspec = pl.BlockSpec((1, HEAD), lambda hp, n: (0, 0))
    shp = jax.ShapeDtypeStruct((S_, ATTN_OUT), F32)
    return pl.pallas_call(
        body, name=name, grid=(npairs, nsteps),
        in_specs=[cur(cq), cur(ck), prev(ck), cur(cv), prev(cv), gspec, gspec],
        out_specs=(cur(0), cur(0)), out_shape=(shp, shp),
        compiler_params=_cparams(("parallel", "parallel")),
    )(z, z, z, z, z, q_gain, k_gain)


def _attn_combine(name, os_, ls_):
    S_ = os_[0].shape[0]

    def body(o0, o1, o2, l0, l1, l2, o_ref, l_ref):
        a, b, c = l0[...], l1[...], l2[...]
        m = jnp.maximum(jnp.maximum(a, b), c)
        ea, eb, ec = jnp.exp(a - m), jnp.exp(b - m), jnp.exp(c - m)
        zsum = ea + eb + ec
        o_ref[...] = (ea * o0[...] + eb * o1[...] + ec * o2[...]) / zsum
        l_ref[...] = m + jnp.log(zsum)

    row = pl.BlockSpec((ROW_TILE, ATTN_OUT), lambda i: (i, 0))
    shp = jax.ShapeDtypeStruct((S_, ATTN_OUT), F32)
    return pl.pallas_call(
        body, name=name, grid=(S_ // ROW_TILE,), in_specs=[row] * 6, out_specs=(row, row), out_shape=(shp, shp),
        compiler_params=_cparams(("parallel",)),
    )(*os_, *ls_)


def _attn_bwd(name, z, q_gain, k_gain, do, o, lse, g, dil, after=None):
    S_ = z.shape[0]
    rows, nb, pw, heads, npairs, cq, ck, cv = _attn_specs(S_, g, dil)
    scale = HEAD ** -0.5

    def body(q_ref, kc_ref, kp_ref, vc_ref, vp_ref, gq_ref, gk_ref, do_ref, o_ref, l_ref,
             dq_ref, dk_ref, dv_ref, dgq_ref, dgk_ref, ck_ref, cvv_ref, gq_acc, gk_acc):
        hp = pl.program_id(0)
        n = pl.program_id(1)
        ones = _head_ones(pw)
        gq, gk = jnp.tile(gq_ref[...], (1, heads)), jnp.tile(gk_ref[...], (1, heads))

        @pl.when(n == 0)
        def _():
            ck_ref[...] = jnp.zeros_like(ck_ref)
            cvv_ref[...] = jnp.zeros_like(cvv_ref)

        @pl.when(jnp.logical_and(n == 0, hp == 0))
        def _():
            gq_acc[...] = jnp.zeros_like(gq_acc)
            gk_acc[...] = jnp.zeros_like(gk_acc)

        pl.when(n < nb)(functools.partial(query_step, n, ones, gq, gk, q_ref, kc_ref, kp_ref, vc_ref, vp_ref, do_ref,
                                          o_ref, l_ref, dq_ref, dk_ref, dv_ref, ck_ref, cvv_ref, gq_acc, gk_acc))

        @pl.when(n == nb)
        def _():
            dgk = jnp.zeros((8, pw), F32)
            for r in range(dil):
                _, kphat, kprstd = _head_norm(_rows_of(kp_ref, r, dil), gk, ones)
                dk_all, dg = _head_norm_bwd(_rows_of(ck_ref, r, dil), kphat, kprstd, gk, ones)
                dgk = dgk + dg
                _put_rows(dk_ref, r, dil, dk_all)
                _put_rows(dv_ref, r, dil, _rows_of(cvv_ref, r, dil))
            gk_acc[...] += dgk

        @pl.when(jnp.logical_and(n == nb, hp == npairs - 1))
        def _():
            fold = lambda a: sum(a[:, h * HEAD:(h + 1) * HEAD] for h in range(heads))
            dgq_ref[...] = jnp.sum(fold(gq_acc[...]), axis=0, keepdims=True)
            dgk_ref[...] = jnp.sum(fold(gk_acc[...]), axis=0, keepdims=True)

    def query_step(n, ones, gq, gk, q_ref, kc_ref, kp_ref, vc_ref, vp_ref, do_ref, o_ref, l_ref,
                   dq_ref, dk_ref, dv_ref, ck_ref, cvv_ref, gq_acc, gk_acc):
        mask = _band_mask(n > 0)
        hmask = _head_masks(pw)
        dgq = jnp.zeros((8, pw), F32)
        dgk = jnp.zeros((8, pw), F32)
        for r0 in range(0, dil, ATTN_BATCH):
            rs = range(r0, min(r0 + ATTN_BATCH, dil))
            keys = [(r, h) for r in rs for h in range(heads)]
            qn, qhat, qrstd, kn, kphat, kprstd, vv, dob, delta, lse = ({} for _ in range(10))
            for r in rs:
                q, kc, kp = _rows_of(q_ref, r, dil), _rows_of(kc_ref, r, dil), _rows_of(kp_ref, r, dil)
                dov = _rows_of(do_ref, r, dil)
                qn_f, qhat[r], qrstd[r] = _head_norm(q, gq, ones)
                kpn, kphat[r], kprstd[r] = _head_norm(kp, gk, ones)
                qn[r] = qn_f.astype(BF16)
                kn[r] = jnp.concatenate([kpn, _head_norm(kc, gk, ones)[0]], axis=0).astype(BF16)
                vv[r] = jnp.concatenate([_rows_of(vp_ref, r, dil), _rows_of(vc_ref, r, dil)], axis=0).astype(BF16)
                dob[r] = dov.astype(BF16)
                delta[r] = _head_sum(dov * _rows_of(o_ref, r, dil), ones)
                lse[r] = _rows_of(l_ref, r, dil)
            s, dp = {}, {}
            for r, h in keys:
                s[r, h] = lax.dot_general(_only(hmask[h], qn[r]), kn[r], _DIMS["nt"],
                                          preferred_element_type=F32) * scale
                dp[r, h] = lax.dot_general(_only(hmask[h], dob[r]), vv[r], _DIMS["nt"], preferred_element_type=F32)
            p, ds = {}, {}
            for r, h in keys:
                col = slice(h * HEAD, h * HEAD + 1)
                pk = jnp.where(mask, jnp.exp(jnp.where(mask, s[r, h], MASK) - lse[r][:, col]), 0.0)
                ds[r, h] = (pk * (dp[r, h] - delta[r][:, col]) * scale).astype(BF16)
                p[r, h] = pk.astype(BF16)
            dqn, dkn, dvv = {}, {}, {}
            for r in rs:
                dqn[r] = jnp.zeros((ATTN_BLK, pw), F32)
                dkn[r] = jnp.zeros((2 * ATTN_BLK, pw), F32)
                dvv[r] = jnp.zeros((2 * ATTN_BLK, pw), F32)
                for h in range(heads):
                    dqn[r] = jnp.where(hmask[h], jnp.dot(ds[r, h], kn[r], preferred_element_type=F32), dqn[r])
                    dkn[r] = jnp.where(hmask[h], lax.dot_general(ds[r, h], qn[r], _DIMS["tn"],
                                                                 preferred_element_type=F32), dkn[r])
                    dvv[r] = jnp.where(hmask[h], lax.dot_general(p[r, h], dob[r], _DIMS["tn"],
                                                                 preferred_element_type=F32), dvv[r])
            for r in rs:
                dq_all, dg = _head_norm_bwd(dqn[r], qhat[r], qrstd[r], gq, ones)
                dgq = dgq + dg
                dk_all, dg = _head_norm_bwd(_rows_of(ck_ref, r, dil) + dkn[r][:ATTN_BLK], kphat[r], kprstd[r], gk, ones)
                dgk = dgk + dg
                dv_all = _rows_of(cvv_ref, r, dil) + dvv[r][:ATTN_BLK]
                _put_rows(dq_ref, r, dil, dq_all)
                _put_rows(dk_ref, r, dil, dk_all)
                _put_rows(dv_ref, r, dil, dv_all)
                _put_rows(ck_ref, r, dil, dkn[r][ATTN_BLK:])
                _put_rows(cvv_ref, r, dil, dvv[r][ATTN_BLK:])
        gq_acc[...] += dgq
        gk_acc[...] += dgk

    last = nb - 1
    cur = lambda c: pl.BlockSpec((rows, pw), lambda hp, n, c=c: (jnp.minimum(n, last), c + hp))
    prev = lambda c: pl.BlockSpec((rows, pw), lambda hp, n, c=c: (jnp.maximum(n - 1, 0), c + hp))
    gspec = pl.BlockSpec((1, HEAD), lambda hp, n: (0, 0))
    act = jax.ShapeDtypeStruct((S_, ATTN_OUT), F32)
    vec = jax.ShapeDtypeStruct((1, HEAD), F32)
    return _call(
        body, [z, z, z, z, z, q_gain, k_gain, do, o, lse],
        [cur(cq), cur(ck), prev(ck), cur(cv), prev(cv), gspec, gspec, cur(0), cur(0), cur(0)], after,
        name=name, grid=(npairs, nb + 1),
        out_specs=(cur(0), prev(0), prev(0), gspec, gspec),
        out_shape=(act, act, act, vec, vec),
        scratch_shapes=[pltpu.VMEM((rows, pw), F32), pltpu.VMEM((rows, pw), F32),
                        pltpu.VMEM((8, pw), F32), pltpu.VMEM((8, pw), F32)],
        compiler_params=_cparams(("arbitrary", "arbitrary")))


MIX_TN = 256
MIX_TM = 2048


def _sigmoid(v):
    return 1.0 / (1.0 + jnp.exp(-v))


def _mix_fwd(name, z, b_gate, ys, ws):
    S_ = z.shape[0]
    tm, tn = MIX_TM, MIX_TN
    gblk = OFF_GATE // tn

    def body(yp, yc, ya, wp, wc, wa, g0, g1, g2, b0, b1, b2, m_ref):
        acc = None
        for y_ref, w_ref, g_ref, b_ref in ((yp, wp, g0, b0), (yc, wc, g1, b1), (ya, wa, g2, b2)):
            u = lax.dot_general(y_ref[...].astype(BF16), w_ref[...], _DIMS["nt"], preferred_element_type=F32)
            t = _sigmoid(g_ref[...] + b_ref[...]) * u
            acc = t if acc is None else acc + t
        m_ref[...] = acc.astype(BF16)

    yspec = lambda w: pl.BlockSpec((tm, w), lambda i, j: (i, 0))
    wspec = lambda w: pl.BlockSpec((tn, w), lambda i, j: (j, 0))
    gspec = lambda b: pl.BlockSpec((tm, tn), lambda i, j, b=b: (i, gblk + b * (D // tn) + j))
    bspec = lambda b: pl.BlockSpec((1, tn), lambda i, j, b=b: (0, b * (D // tn) + j))
    return pl.pallas_call(
        body, name=name, grid=(S_ // tm, D // tn),
        in_specs=[yspec(POOL_W), yspec(CONV_W), yspec(ATTN_OUT), wspec(POOL_W), wspec(CONV_W), wspec(ATTN_OUT),
                  gspec(0), gspec(1), gspec(2), bspec(0), bspec(1), bspec(2)],
        out_specs=pl.BlockSpec((tm, tn), lambda i, j: (i, j)),
        out_shape=jax.ShapeDtypeStruct((S_, D), BF16),
        compiler_params=_cparams(("parallel", "parallel")),
    )(*ys, *ws, z, z, z, b_gate, b_gate, b_gate)


def _mix_bwd(name, z, b_gate, y, w, dmerged, branch, dz, after=None):
    S_ = z.shape[0]
    tm, tn = MIX_TM, MIX_TN
    width = y.shape[1]
    gblk = OFF_GATE // tn + branch * (D // tn)
    nj = D // tn

    def body(y_ref, w_ref, g_ref, b_ref, dm_ref, dy_ref, dw_ref, dg_ref, db_ref, acc_ref):
        i, j = pl.program_id(0), pl.program_id(1)
        yb = y_ref[...].astype(BF16)
        u = lax.dot_general(yb, w_ref[...], _DIMS["nt"], preferred_element_type=F32)
        sg = _sigmoid(g_ref[...] + b_ref[...])
        dm = dm_ref[...]
        du = (sg * dm).astype(BF16)
        dpre = dm * u * sg * (1.0 - sg)
        dg_ref[...] = dpre.astype(BF16)
        cols = pl.ds(pl.multiple_of(j * tn, tn), tn)
        db_ref[...] = _rows8(dpre)
        d_w = lax.dot_general(du, yb, _DIMS["tn"], preferred_element_type=F32)
        d_y = jnp.dot(du, w_ref[...], preferred_element_type=F32)

        @pl.when(i == 0)
        def _():
            dw_ref[cols, :] = d_w

        @pl.when(i > 0)
        def _():
            dw_ref[cols, :] += d_w

        @pl.when(j == 0)
        def _():
            acc_ref[...] = d_y

        @pl.when(j > 0)
        def _():
            acc_ref[...] += d_y

        @pl.when(j == nj - 1)
        def _():
            dy_ref[...] = acc_ref[...]

    rows = pl.BlockSpec((tm, width), lambda i, j: (i, 0))
    blk = pl.BlockSpec((tm, tn), lambda i, j: (i, j))
    gate = pl.BlockSpec((tm, tn), lambda i, j: (i, gblk + j))
    args = [y, w, z, b_gate, dmerged]
    in_specs = [rows, pl.BlockSpec((tn, width), lambda i, j: (j, 0)), gate,
                pl.BlockSpec((1, tn), lambda i, j: (0, branch * nj + j)), blk]
    aliases = {}
    if not isinstance(dz, jax.ShapeDtypeStruct):
        body = _drop_ref(body, len(args))
        aliases = {len(args): 2}
        args.append(dz)
        in_specs.append(ANY)
    return _call(
        body, args, in_specs, after, name=name, grid=(S_ // tm, nj),
        out_specs=(rows, pl.BlockSpec((D, width), lambda i, j: (0, 0)), gate, pl.BlockSpec((8, tn), lambda i, j: (i, j))),
        out_shape=(jax.ShapeDtypeStruct((S_, width), F32), jax.ShapeDtypeStruct((D, width), F32),
                   jax.ShapeDtypeStruct(dz.shape, dz.dtype), jax.ShapeDtypeStruct((S_ // tm * 8, D), F32)),
        scratch_shapes=[pltpu.VMEM((tm, width), F32)], input_output_aliases=aliases,
        compiler_params=_cparams(("arbitrary", "arbitrary")))


def _relu2_epilogue(acc):
    r = jnp.maximum(acc, 0.0)
    return (r * r,)


def _relu2_bwd_epilogue(acc, r):
    return (acc * (2.0 * jnp.sqrt(r.astype(F32))),)


def _residual_norm_epilogue(acc, xr, gain):
    x = xr + acc
    ms = jnp.mean(x * x, axis=-1, keepdims=True)
    return x, x * lax.rsqrt(ms + EPS) * gain


def _fold_rows(name, part, lanes=False, scale=1.0):
    R, N = part.shape

    def body(p_ref, o_ref):
        s = jnp.sum(p_ref[...], axis=0, keepdims=True)
        if lanes:
            s = jnp.sum(s, axis=1, keepdims=True)
        o_ref[...] = s * scale

    out_n = 1 if lanes else N
    return pl.pallas_call(
        body, name=name, grid=(1,), in_specs=[pl.BlockSpec((R, N), lambda i: (0, 0))],
        out_specs=pl.BlockSpec((1, out_n), lambda i: (0, 0)), out_shape=jax.ShapeDtypeStruct((1, out_n), F32))(part)


def _rows8(v):
    return jnp.sum(v.reshape(v.shape[0] // 8, 8, v.shape[1]), axis=0)


def _residual_loss_epilogue(acc, xr, target):
    e = xr + acc - target
    return e * (1.0 / D), _rows8(e * e)


def _norm_bwd_epilogue(acc, x, dres, gain):
    rstd = lax.rsqrt(jnp.mean(x * x, axis=-1, keepdims=True) + EPS)
    xhat = x * rstd
    dxhat = acc * gain
    c = jnp.mean(dxhat * xhat, axis=-1, keepdims=True)
    return dres + rstd * (dxhat - xhat * c), _rows8(acc * xhat)


def _norm_bwd_epilogue_twice(acc, x, dres, gain):
    dx, part = _norm_bwd_epilogue(acc, x, dres, gain)
    return dx, part, dx


def _layer_fwd(l, x, h, p, next_gain, target=None, early=None):
    t = f"l{l}_"
    z = _matmul(t + "in_proj", h, p["w_in_t"], "nt", 512, 3712, 1024, after=early)
    y_pool = _pool_fwd(t + "pool", z, p["pool_mix"], p["pool_scale"])
    os_, ls_ = [], []
    for g, (_, dil) in enumerate(ATTN_GROUPS):
        o_g, l_g = _attn_fwd(t + f"attn{g}", z, p["q_gain"], p["k_gain"], g, dil)
        os_.append(o_g)
        ls_.append(l_g)
    y_attn, lse = _attn_combine(t + "attn_mix", os_, ls_)
    y_conv = _conv_fwd(t + "conv", z, p["conv_w"], after=y_attn)
    merged = _mix_fwd(t + "merge", z, p["b_gate"], (y_pool, y_conv, y_attn),
                      (p["w_pool_up_t"], p["w_conv_out_t"], p["w_attn_up_t"]))
    x1, h2 = _matmul(t + "out_proj", merged, p["rest"], "nn", 1024, D, 1024, out_dtypes=(F32, BF16),
                     extras=((x, "mn"), (p["norm_mlp"], "n")), epilogue=_residual_norm_epilogue, b_rows=ROWS_WO)
    r = _matmul(t + "ff1", h2, p["rest"], "nt", 1024, 1024, 1024, out_dtypes=(BF16,), epilogue=_relu2_epilogue,
                b_rows=ROWS_FF1)
    if next_gain is None:
        x2, h_out = _matmul(t + "ff2", r, p["rest"], "nn", 512, D, D_FF, out_dtypes=(F32, (F32, "rows8")),
                            extras=((x1, "mn"), (target, "mn")), epilogue=_residual_loss_epilogue, b_rows=ROWS_FF2)
    else:
        x2, h_out = _matmul(t + "ff2", r, p["rest"], "nn", 512, D, D_FF, out_dtypes=(F32, BF16),
                            extras=((x1, "mn"), (next_gain, "n")), epilogue=_residual_norm_epilogue,
                            b_rows=ROWS_FF2)
    saved = dict(x=x, h=h, z=z, y_pool=y_pool, y_conv=y_conv, y_attn=y_attn, lse=lse, merged=merged,
                 x1=x1, h2=h2, r=r)
    return x2, h_out, saved


def _layer_bwd(l, dx2, p, s, pending, in_t_all, collective_id, start_after=None):
    t = f"l{l}_b_"
    g = {}
    rest = jax.ShapeDtypeStruct((N_DEV, REST_ROWS, D), F32)
    da = _matmul(t + "d_ff2_in", dx2, p["rest"], "nt", 1024, 1024, 1024, out_dtypes=(BF16,),
                 extras=((s["r"], "mn"),), epilogue=_relu2_bwd_epilogue,
                 after=start_after, b_rows=ROWS_FF2)
    rest = _matmul(t + "dw_ff2", s["r"], dx2, "tn", 512, 1024, 4096, into=(rest, REST_FF2, 0), after=da)
    tok = rest if pending is None else pending.chip_sums(after=rest)
    dx1, part, dx1_16 = _matmul(
        t + "d_ff1_in", da, p["rest"], "nn", 512, D, D_FF, out_dtypes=(F32, (F32, "rows8"), BF16),
        extras=((s["x1"], "mn"), (dx2, "mn"), (p["norm_mlp"], "n")), epilogue=_norm_bwd_epilogue_twice,
        after=tok, b_rows=ROWS_FF1)
    g["norm_mlp"] = _fold_rows(t + "d_norm_mlp", part)
    rest = _matmul(t + "dw_ff1", da, s["h2"], "tn", 512, 1024, 4096, into=(rest, REST_FF1, 0), after=dx1)
    dmerged = _matmul(t + "d_out_proj_in", dx1_16, p["rest"], "nt", 1024, 1024, 1024, b_rows=ROWS_WO)
    rest = _matmul(t + "dw_o", s["merged"], dx1_16, "tn", 128, 1024, 4096, into=(rest, REST_WO, 0), after=dmerged)
    ys = (s["y_pool"], s["y_conv"], s["y_attn"])
    names = ("w_pool_up_t", "w_conv_out_t", "w_attn_up_t")
    dys, dbs = [], []
    tok = rest
    dz = jax.ShapeDtypeStruct((dx2.shape[0], IN_COLS), BF16)
    for b in range(3):
        dy_b, dw_t, dz, db = _mix_bwd(t + f"merge{b}", s["z"], p["b_gate"], ys[b], p[names[b]], dmerged, b, dz,
                                      after=tok)
        tok = dy_b
        width = ys[b].shape[1]
        if b < 2:
            rest = rest.at[:, REST_UP:REST_ATTN, b * width:(b + 1) * width].set(
                dw_t.reshape(N_DEV, REST_ATTN - REST_UP, width))
        else:
            rest = rest.at[:, REST_ATTN:REST_ROWS, :].set(dw_t.reshape(N_DEV, REST_ROWS - REST_ATTN, D))
        dys.append(dy_b)
        dbs.append(_fold_rows(t + f"d_b_gate{b}", db))
    g["b_gate"] = jnp.concatenate(dbs, axis=1)
    rs_rest = _ReduceScatter(f"rs_rest{l}", collective_id, rest)
    pending_sum = None if pending is None else pending.finish(after=rest, into=(in_t_all, l + 1))
    dz, g["pool_mix"], g["pool_scale"] = _pool_bwd(t + "pool", s["z"], dys[0], p["pool_mix"], p["pool_scale"], dz,
                                                   after=rest if pending is None else pending_sum[0])
    dcb, dcc, dcx, g["conv_w"] = _conv_bwd(t + "conv", s["z"], dys[1], p["conv_w"], after=dz)
    tok = rs_rest.chip_sums(after=dcb)
    dqs, dks, dvs = [], [], []
    gq = gk = None
    for gi, (_, dil) in enumerate(ATTN_GROUPS):
        dq, dk, dv, dgq, dgk = _attn_bwd(t + f"attn{gi}", s["z"], p["q_gain"], p["k_gain"], dys[2], s["y_attn"],
                                         s["lse"], gi, dil, after=tok)
        tok = dq
        dqs.append(dq)
        dks.append(dk)
        dvs.append(dv)
        gq = dgq if gq is None else gq + dgq
        gk = dgk if gk is None else gk + dgk
    g["q_gain"], g["k_gain"] = gq, gk
    rest_sum, _ = rs_rest.finish(after=tok)
    col = OFF_CB
    for piece in [dcb, dcc, dcx] + dqs + dks + dvs:
        dz = lax.dynamic_update_slice(dz, piece.astype(BF16), (0, col))
        col += piece.shape[1]
    in_t = _matmul(t + "dw_in", dz, s["h"], "tn", 256, 1024, 4096, after=rest_sum)
    rs_in = _ReduceScatter(f"rs_in{l}", collective_id + 2, in_t.reshape(N_DEV, IN_ROWS, D))
    dx, part = _matmul(t + "d_in_proj_in", dz, p["w_in_t"], "nn", 256, D, IN_COLS, out_dtypes=(F32, (F32, "rows8")),
                       extras=((s["x"], "mn"), (dx1, "mn"), (p["norm_mix"], "n")), epilogue=_norm_bwd_epilogue,
                       after=in_t)
    g["norm_mix"] = _fold_rows(t + "d_norm_mix", part)
    return dx, g, rest_sum, in_t_all if pending is None else pending_sum[0], rs_in


ANY = pl.BlockSpec(memory_space=pl.ANY)


def _mesh_pos():
    return lax.axis_index("x"), lax.axis_index("y"), lax.axis_index("c")


def _other_chips(x, y):
    return [(1 - x, y), (x, 1 - y), (1 - x, 1 - y)]


def _comm_call(name, collective_id, peers, body, arrs, out_shape, sem_counts, after=None):
    n_in, n_out = len(arrs), len(out_shape)
    if collective_id is None:
        def tc_body(*refs):
            body(refs[:n_in], refs[n_in:n_in + n_out], *refs[n_in + n_out:])

        return pl.pallas_call(
            tc_body, name=name, out_shape=tuple(out_shape), in_specs=[ANY] * n_in, out_specs=(ANY,) * n_out,
            scratch_shapes=[pltpu.SemaphoreType.DMA((n,)) for n in sem_counts])(*arrs)

    n_after = 0 if after is None else 1

    def seq_body(*refs):
        barrier = pltpu.get_barrier_semaphore()
        ps = peers(*_mesh_pos())
        for p in ps:
            pl.semaphore_signal(barrier, inc=1, device_id=p, device_id_type=MESH)
        pl.semaphore_wait(barrier, len(ps))
        outs = refs[n_in + n_after:]
        body(refs[:n_in], outs[:n_out], *outs[n_out:])

    return pl.kernel(
        seq_body, out_type=tuple(out_shape), mesh=plsc.ScalarSubcoreMesh(axis_name="seq", num_cores=1), name=name,
        scratch_types=[pltpu.SemaphoreType.DMA((n,)) for n in sem_counts],
        compiler_params=pltpu.CompilerParams(collective_id=collective_id),
    )(*arrs, *([] if after is None else [after]))


def _ordered(name, x, after):
    def body(x_ref, after_ref, o_ref):
        del x_ref, after_ref, o_ref

    return pl.pallas_call(
        body, name=name, out_shape=jax.ShapeDtypeStruct(x.shape, x.dtype), in_specs=[ANY, ANY], out_specs=ANY,
        input_output_aliases={0: 0})(x, after)


def _all_gather(name, collective_id, shard, after=None):
    R, C = shard.shape

    def peers(x, y, c):
        return [(x, y, 1 - c)] + [(*chip, c) for chip in _other_chips(x, y)]

    def body(in_refs, out_refs, send_sems, recv_sems, local_sems):
        (x_ref,), (out_ref,) = in_refs, out_refs
        x, y, c = _mesh_pos()
        me, sibling = (x, y, c), (x, y, 1 - c)
        chips = _other_chips(x, y)

        def slot(px, py, pc):
            return out_ref.at[4 * px + 2 * py + pc]

        def copy(k, block, to, src=None):
            return pltpu.make_async_remote_copy(
                src_ref=slot(*block) if src is None else src, dst_ref=slot(*block),
                send_sem=send_sems.at[k], recv_sem=recv_sems.at[k], device_id=to, device_id_type=MESH)

        mine = pltpu.make_async_copy(x_ref, slot(*me), local_sems.at[0])
        mine.start()
        first = [copy(0, me, sibling, src=x_ref)]
        first += [copy(1 + j, me, (*chip, c), src=x_ref) for j, chip in enumerate(chips)]
        for cp in first:
            cp.start()
        passed = [copy(4 + j, (*chip, c), sibling) for j, chip in enumerate(chips)]
        for j, chip in enumerate(chips):
            copy(1 + j, (*chip, c), me).wait_recv()
            passed[j].start()
        copy(0, sibling, me).wait_recv()
        for j, chip in enumerate(chips):
            copy(4 + j, (*chip, 1 - c), me).wait_recv()
        for cp in first + passed:
            cp.wait_send()
        mine.wait()

    return _comm_call(name, collective_id, peers, body, [shard],
                      [jax.ShapeDtypeStruct((N_DEV, R, C), shard.dtype)], (7, 7, 1), after)[0]


def _rs_sibling_exchange(name, collective_id, arrs):
    n = len(arrs)

    def body(in_refs, out_refs, send_sems, recv_sems):
        x, y, c = _mesh_pos()
        cps = []
        for k, (src, dst) in enumerate(zip(in_refs, out_refs)):
            src = src.at[:, 1 - c] if len(src.shape) == 4 else src
            cps.append(pltpu.make_async_remote_copy(src_ref=src, dst_ref=dst, send_sem=send_sems.at[k],
                                                    recv_sem=recv_sems.at[k], device_id=(x, y, 1 - c),
                                                    device_id_type=MESH))
        for cp in cps:
            cp.start()
        for cp in cps:
            cp.wait()

    out_shape = [jax.ShapeDtypeStruct(a.shape[:1] + a.shape[2:] if a.ndim == 4 else a.shape, a.dtype) for a in arrs]
    return _comm_call(name, collective_id, lambda x, y, c: [(x, y, 1 - c)], body, arrs, out_shape, (n, n))


def _rs_rows(r):
    return r // 2 if (r // 2) % 16 == 0 else r


def _drop_ref(body, idx):
    def wrapped(*refs):
        return body(*refs[:idx], *refs[idx + 1:])

    return wrapped


def _rs_chip_sum(name, ids, big, rbig, after=None):
    _, _, R, C = big.shape
    rows = _rs_rows(R)

    def body(ids_ref, mine_ref, theirs_ref, t16_ref, own_ref):
        p = pl.program_id(1)
        s = mine_ref[...] + theirs_ref[...]
        t16_ref[...] = s.astype(BF16)

        @pl.when(p == ids_ref[1])
        def _():
            own_ref[...] = s

    in_specs = [pl.BlockSpec((None, None, rows, C), lambda i, p, ids: (p, ids[0], i, 0)),
                pl.BlockSpec((None, rows, C), lambda i, p, ids: (p, i, 0))]
    args = [ids, big, rbig]
    if after is not None:
        body = _drop_ref(body, len(args))
        in_specs.append(ANY)
        args.append(after)
    grid_spec = pltpu.PrefetchScalarGridSpec(
        num_scalar_prefetch=1, grid=(R // rows, 4), in_specs=in_specs,
        out_specs=(pl.BlockSpec((None, rows, C), lambda i, p, ids: (p, i, 0)),
                   pl.BlockSpec((rows, C), lambda i, p, ids: (i, 0))))
    return pl.pallas_call(
        body, name=name, grid_spec=grid_spec,
        out_shape=(jax.ShapeDtypeStruct((4, R, C), BF16), jax.ShapeDtypeStruct((R, C), F32)),
        compiler_params=_cparams(("parallel", "arbitrary")),
    )(*args)


def _add2(name, a, b):
    def body(a_ref, b_ref, o_ref):
        o_ref[...] = a_ref[...] + b_ref[...]

    full = pl.BlockSpec(a.shape, lambda i: (0, 0))
    return pl.pallas_call(body, name=name, grid=(1,), in_specs=[full, full], out_specs=full,
                          out_shape=jax.ShapeDtypeStruct(a.shape, a.dtype))(a, b)


def _rs_chip_exchange(name, collective_id, arrs):
    n = len(arrs)

    def body(in_refs, out_refs, send_sems, recv_sems, local_sems):
        x, y, c = _mesh_pos()
        p_me = 2 * x + y
        chips = _other_chips(x, y)

        def part(ref, p):
            return ref.at[p] if len(ref.shape) == 3 else ref

        local = [pltpu.make_async_copy(part(src, p_me), dst.at[p_me], local_sems.at[k])
                 for k, (src, dst) in enumerate(zip(in_refs, out_refs))]
        for cp in local:
            cp.start()
        sends = []
        for j, (px, py) in enumerate(chips):
            for k, (src, dst) in enumerate(zip(in_refs, out_refs)):
                sends.append(pltpu.make_async_remote_copy(
                    src_ref=part(src, 2 * px + py), dst_ref=dst.at[p_me], send_sem=send_sems.at[n * j + k],
                    recv_sem=recv_sems.at[n * j + k], device_id=(px, py, c), device_id_type=MESH))
        for cp in sends:
            cp.start()
        for j, (px, py) in enumerate(chips):
            for k, (src, dst) in enumerate(zip(in_refs, out_refs)):
                pltpu.make_async_remote_copy(
                    src_ref=part(src, p_me), dst_ref=dst.at[2 * px + py], send_sem=send_sems.at[n * j + k],
                    recv_sem=recv_sems.at[n * j + k], device_id=(px, py, c), device_id_type=MESH).wait_recv()
        for cp in sends:
            cp.wait_send()
        for cp in local:
            cp.wait()

    out_shape = [jax.ShapeDtypeStruct((4,) + a.shape[-2:], a.dtype) for a in arrs]
    return _comm_call(name, collective_id, lambda x, y, c: [(*chip, c) for chip in _other_chips(x, y)], body, arrs,
                      out_shape, (3 * n, 3 * n, n))


def _rs_final_sum(name, ids, recv, own, rows, after=None, into=None):
    _, R, C = recv.shape
    assert R % rows == 0

    def body(ids_ref, r_ref, own_ref, o_ref):
        acc = None
        for p in range(4):
            term = jnp.where(ids_ref[1] == p, own_ref[...], r_ref[p].astype(F32))
            acc = term if acc is None else acc + term
        o_ref[...] = acc

    in_specs = [pl.BlockSpec((4, rows, C), lambda i, ids: (0, i, 0)), pl.BlockSpec((rows, C), lambda i, ids: (i, 0))]
    args = [ids, recv, own]
    if after is not None:
        body = _drop_ref(body, len(args))
        in_specs.append(ANY)
        args.append(after)
    out_spec, out_shape, aliases = pl.BlockSpec((rows, C), lambda i, ids: (i, 0)), jax.ShapeDtypeStruct((R, C), F32), {}
    if into is not None:
        buf, slot = into
        out_spec = pl.BlockSpec((None, rows, C), lambda i, ids: (slot, i, 0))
        out_shape = jax.ShapeDtypeStruct(buf.shape, F32)
        if not isinstance(buf, jax.ShapeDtypeStruct):
            body = _drop_ref(body, len(args))
            aliases = {len(args): 0}
            in_specs.append(ANY)
            args.append(buf)
    grid_spec = pltpu.PrefetchScalarGridSpec(
        num_scalar_prefetch=1, grid=(R // rows,), in_specs=in_specs, out_specs=out_spec)
    return pl.pallas_call(
        body, name=name, grid_spec=grid_spec, out_shape=out_shape, input_output_aliases=aliases,
        compiler_params=_cparams(("parallel",)),
    )(*args)


class _ReduceScatter:
    def __init__(self, tag, collective_id, big, small=None):
        x, y, c = _mesh_pos()
        self.tag, self.cid, self.small = tag, collective_id, small
        self.ids = jnp.stack([c, 2 * x + y]).astype(jnp.int32)
        self.big = big.reshape((4, 2) + big.shape[1:])
        self.got = _rs_sibling_exchange(tag + "_sibling", collective_id,
                                        [self.big] + ([] if small is None else [small]))

    def chip_sums(self, after=None):
        t16, self.own = _rs_chip_sum(self.tag + "_chip_sum", self.ids, self.big, self.got[0], after)
        arrs = [t16]
        if self.small is not None:
            self.ts = _add2(self.tag + "_chip_sum_small", self.small, self.got[1])
            arrs.append(self.ts)
        self.recv = _rs_chip_exchange(self.tag + "_chips", self.cid + 1, arrs)
        return t16

    def finish(self, after=None, into=None):
        out = _rs_final_sum(self.tag + "_final", self.ids, self.recv[0], self.own, _rs_rows(self.own.shape[0]), after,
                            into)
        out_small = None
        if self.small is not None:
            out_small = _rs_final_sum(self.tag + "_final_small", self.ids, self.recv[1], self.ts, self.small.shape[0])
        return out, out_small


def _all_reduce_small(tag, small):
    x, y, c = _mesh_pos()
    ids = jnp.stack([c, 2 * x + y]).astype(jnp.int32)
    (theirs,) = _rs_sibling_exchange(tag + "_sibling", None, [small])
    ts = _add2(tag + "_chip_sum", small, theirs)
    (recv,) = _rs_chip_exchange(tag + "_chips", None, [ts])
    return _rs_final_sum(tag + "_final", ids, recv, ts, small.shape[0])


def _adamw(name, w, g, m, v):
    R, C = w.shape
    tiles = [t for t in range(8, 513, 8) if R % t == 0]
    tr = max(tiles) if tiles else R
    c1 = 1.0 - ADAM_B1 ** ADAM_STEP
    c2 = 1.0 - ADAM_B2 ** ADAM_STEP

    def body(w_ref, g_ref, m_ref, v_ref, d_ref, nm_ref, nv_ref):
        gv = g_ref[...]
        nm = ADAM_B1 * m_ref[...] + (1.0 - ADAM_B1) * gv
        nv = ADAM_B2 * v_ref[...] + (1.0 - ADAM_B2) * (gv * gv)
        d_ref[...] = -ADAM_LR * ((nm / c1) / (jnp.sqrt(nv / c2) + ADAM_EPS) + ADAM_WD * w_ref[...])
        nm_ref[...] = nm
        nv_ref[...] = nv

    blk = pl.BlockSpec((tr, C), lambda i: (i, 0))
    shp = jax.ShapeDtypeStruct((R, C), F32)
    return pl.pallas_call(
        body, name=name, grid=(R // tr,), in_specs=[blk] * 4, out_specs=(blk,) * 3, out_shape=(shp,) * 3,
        compiler_params=_cparams(("parallel",)),
    )(w, g, m, v)


def _adamw_t(name, w, g_t, m, v):
    L, R, C = w.shape
    tr = 256
    c1 = 1.0 - ADAM_B1 ** ADAM_STEP
    c2 = 1.0 - ADAM_B2 ** ADAM_STEP

    def body(w_ref, g_ref, m_ref, v_ref, go_ref, d_ref, nm_ref, nv_ref):
        gv = g_ref[...].T
        go_ref[...] = gv
        nm = ADAM_B1 * m_ref[...] + (1.0 - ADAM_B1) * gv
        nv = ADAM_B2 * v_ref[...] + (1.0 - ADAM_B2) * (gv * gv)
        d_ref[...] = -ADAM_LR * ((nm / c1) / (jnp.sqrt(nv / c2) + ADAM_EPS) + ADAM_WD * w_ref[...])
        nm_ref[...] = nm
        nv_ref[...] = nv

    blk = pl.BlockSpec((None, tr, C), lambda l, i: (l, i, 0))
    blk_t = pl.BlockSpec((None, C, tr), lambda l, i: (l, 0, i))
    shp = jax.ShapeDtypeStruct((L, R, C), F32)
    return pl.pallas_call(
        body, name=name, grid=(L, R // tr), in_specs=[blk, blk_t, blk, blk], out_specs=(blk,) * 4,
        out_shape=(shp,) * 4, compiler_params=_cparams(("parallel", "parallel")),
    )(w, g_t, m, v)


REST_FF1 = 0
REST_FF2 = 512
REST_WO = 1024
REST_UP = 1152
REST_ATTN = 1280
REST_ROWS = 1312
IN_ROWS = IN_COLS // N_DEV
ROWS_FF1 = (REST_FF1, D_FF // N_DEV)
ROWS_FF2 = (REST_FF2, D_FF // N_DEV)
ROWS_WO = (REST_WO, D // N_DEV)
SMALL = (("norm_mix", (DEPTH, D)), ("b_gate", (DEPTH, 3 * D)), ("pool_mix", (DEPTH, 4, POOL_G, POOL_G)),
         ("pool_scale", (DEPTH, POOL_W)), ("conv_w", (DEPTH, 3, CONV_W)), ("q_gain", (DEPTH, HEAD)),
         ("k_gain", (DEPTH, HEAD)), ("norm_mlp", (DEPTH, D)))


def _pack_weight_shards(w, l):
    b = lambda a: a.astype(BF16)
    rest = jnp.concatenate([
        b(w["w_ff1"][l].T), b(w["w_ff2"][l]), b(w["w_o"][l]),
        jnp.concatenate([b(w["w_pool_up"][l].T), b(w["w_conv_out"][l].T)], axis=1),
        b(w["w_attn_up"][l].T).reshape(REST_ROWS - REST_ATTN, D)], axis=0)
    return b(w["w_in"][l].T), rest


def _unpack_gathered(g_in, g_rest, g_conv, small_w, l):
    rest = g_rest
    take = lambda r0, rows, c0=0, cols=D: rest[:, r0:r0 + rows, c0:c0 + cols].reshape(N_DEV * rows, cols)
    conv = g_conv[:, 3 * l:3 * l + 3, :CONV_W // N_DEV]
    p = {
        "w_in_t": g_in.reshape(IN_COLS, D),
        "rest": rest,
        "w_pool_up_t": take(REST_UP, 128, 0, POOL_W), "w_conv_out_t": take(REST_UP, 128, POOL_W, CONV_W),
        "w_attn_up_t": rest[:, REST_ATTN:].reshape(D, ATTN_OUT),
        "conv_w": jnp.transpose(conv, (1, 0, 2)).reshape(3, CONV_W),
        "pool_mix": small_w["pool_mix"][l],
    }
    for name in ("norm_mix", "b_gate", "pool_scale", "q_gain", "k_gain", "norm_mlp"):
        p[name] = small_w[name][l][None, :]
    return p


def _pack_small_grads(grads):
    flat = jnp.concatenate([jnp.stack([grads[l][name] for l in range(DEPTH)]).reshape(-1) for name, _ in SMALL])
    return jnp.pad(flat, (0, (-flat.shape[0]) % (8 * 128))).reshape(-1, 128)


def _unpack_grads(rest, small, dev):
    out = {
        "w_ff1_t": jnp.stack([a[REST_FF1:REST_FF1 + 512] for a in rest]),
        "w_ff2": jnp.stack([a[REST_FF2:REST_FF2 + 512] for a in rest]),
        "w_o": jnp.stack([a[REST_WO:REST_WO + 128] for a in rest]),
        "w_pool_up": jnp.stack([a[REST_UP:REST_UP + 128, :POOL_W].T for a in rest]),
        "w_conv_out": jnp.stack([a[REST_UP:REST_UP + 128, POOL_W:].T for a in rest]),
        "w_attn_up": jnp.stack([a[REST_ATTN:].reshape(D // N_DEV, ATTN_OUT).T for a in rest]),
    }
    out.update(_unpack_small_like_grads(small, dev))
    return out


def _unpack_small_like_grads(small, dev):
    out = {}
    flat = small.reshape(-1)
    off = 0
    for name, shp in SMALL:
        n = 1
        for s in shp:
            n *= s
        out[name] = flat[off:off + n].reshape(shp)
        off += n
    width = CONV_W // N_DEV
    out["conv_w"] = lax.dynamic_slice_in_dim(out["conv_w"], dev * width, width, axis=2)
    return out


def _pack_small_like_grads(src, dev):
    parts = []
    for name, shp in SMALL:
        a = src[name]
        if name == "conv_w":
            a = lax.dynamic_update_slice_in_dim(jnp.zeros(shp, F32), a, dev * (CONV_W // N_DEV), axis=2)
        parts.append(a.reshape(-1))
    flat = jnp.concatenate(parts)
    return jnp.pad(flat, (0, (-flat.shape[0]) % (8 * 128))).reshape(-1, 128)


WEIGHTS = ("norm_mix", "w_in", "b_gate", "pool_mix", "pool_scale", "conv_w", "q_gain", "k_gain", "w_pool_up",
           "w_conv_out", "w_attn_up", "w_o", "norm_mlp", "w_ff1", "w_ff2")


def kernel(x, norm_mix, w_in, b_gate, pool_mix, pool_scale, conv_w, q_gain, k_gain, w_pool_up, w_conv_out, w_attn_up, w_o, norm_mlp, w_ff1, w_ff2, loss_target, m_norm_mix, m_w_in, m_b_gate, m_pool_mix, m_pool_scale, m_conv_w, m_q_gain, m_k_gain, m_w_pool_up, m_w_conv_out, m_w_attn_up, m_w_o, m_norm_mlp, m_w_ff1, m_w_ff2, v_norm_mix, v_w_in, v_b_gate, v_pool_mix, v_pool_scale, v_conv_w, v_q_gain, v_k_gain, v_w_pool_up, v_w_conv_out, v_w_attn_up, v_w_o, v_norm_mlp, v_w_ff1, v_w_ff2):
    w = dict(zip(WEIGHTS, (norm_mix, w_in, b_gate, pool_mix, pool_scale, conv_w, q_gain, k_gain, w_pool_up,
                           w_conv_out, w_attn_up, w_o, norm_mlp, w_ff1, w_ff2)))
    m = dict(zip(WEIGHTS, (m_norm_mix, m_w_in, m_b_gate, m_pool_mix, m_pool_scale, m_conv_w, m_q_gain, m_k_gain,
                           m_w_pool_up, m_w_conv_out, m_w_attn_up, m_w_o, m_norm_mlp, m_w_ff1, m_w_ff2)))
    v = dict(zip(WEIGHTS, (v_norm_mix, v_w_in, v_b_gate, v_pool_mix, v_pool_scale, v_conv_w, v_q_gain, v_k_gain,
                           v_w_pool_up, v_w_conv_out, v_w_attn_up, v_w_o, v_norm_mlp, v_w_ff1, v_w_ff2)))
    xi, yi, ci = _mesh_pos()
    dev = 4 * xi + 2 * yi + ci

    saved, params = [], []
    act = x[0]
    w_in_t, m_in_t, v_in_t = (jnp.swapaxes(src["w_in"], 1, 2) for src in (w, m, v))
    h = _rmsnorm_fwd("l0_norm_mix", act, w["norm_mix"][0][None])
    conv_shard = jnp.pad(w["conv_w"].reshape(DEPTH * 3, CONV_W // N_DEV), ((0, 8 - DEPTH * 3), (0, 128 - CONV_W // N_DEV)))
    g_conv = _all_gather("gather_conv", None, conv_shard)
    for l in range(DEPTH):
        s_in, s_rest = _pack_weight_shards(w, l)
        g_in = _all_gather(f"gather_in{l}", 1 + 2 * l, s_in)
        g_rest = _all_gather(f"gather_rest{l}", 2 + 2 * l, s_rest)
        params.append(_unpack_gathered(g_in, g_rest, g_conv, w, l))
    for l in range(DEPTH):
        act, h, s = _layer_fwd(l, act, h, params[l], w["norm_mix"][l + 1][None] if l + 1 < DEPTH else None,
                               loss_target[0])
        saved.append(s)
    dy = act
    loss = _fold_rows("loss_sum", h, lanes=True, scale=0.5 / D)
    total = lax.psum(loss[0, 0], ("x", "y", "c"))
    rest, grads, pending = [None] * DEPTH, [None] * DEPTH, None
    in_t = jax.ShapeDtypeStruct((DEPTH, IN_ROWS, D), F32)
    start = total.reshape(1, 1)
    for l in reversed(range(DEPTH)):
        dy, grads[l], rest[l], in_t, pending = _layer_bwd(l, dy, params[l], saved[l], pending, in_t, 5 + 4 * l, start)
        start = None
    dx = dy
    g_small = _all_reduce_small("ar_small", _pack_small_grads(grads))
    pending.chip_sums(after=g_small)
    g = _unpack_grads(rest, g_small, dev)

    delta, new_m, new_v = {}, {}, {}
    g["w_ff1"], delta["w_ff1"], new_m["w_ff1"], new_v["w_ff1"] = _adamw_t("adamw_w_ff1", w["w_ff1"], g["w_ff1_t"],
                                                                          m["w_ff1"], v["w_ff1"])
    for name in ("w_pool_up", "w_conv_out", "w_attn_up", "w_o", "w_ff2"):
        shp = w[name].shape
        two_d = (shp[0] * shp[1], shp[2])
        d_, m_, v_ = _adamw("adamw_" + name, w[name].reshape(two_d), g[name].reshape(two_d),
                            m[name].reshape(two_d), v[name].reshape(two_d))
        delta[name], new_m[name], new_v[name] = d_.reshape(shp), m_.reshape(shp), v_.reshape(shp)
    for name, shp in SMALL:
        shp = w[name].shape
        two_d = (-1, shp[-1])
        d_, m_, v_ = _adamw("adamw_" + name, w[name].reshape(two_d), g[name].reshape(two_d),
                            m[name].reshape(two_d), v[name].reshape(two_d))
        delta[name], new_m[name], new_v[name] = d_.reshape(shp), m_.reshape(shp), v_.reshape(shp)
    g_t, _ = pending.finish(after=new_v[SMALL[-1][0]], into=(in_t, 0))
    two_d = (g_t.shape[0] * g_t.shape[1], g_t.shape[2])
    d_, m_, v_ = _adamw("adamw_w_in", *[a.reshape(two_d) for a in (w_in_t, g_t, m_in_t, v_in_t)])
    g["w_in"] = jnp.swapaxes(g_t, 1, 2)
    delta["w_in"], new_m["w_in"], new_v["w_in"] = (jnp.swapaxes(a.reshape(g_t.shape), 1, 2) for a in (d_, m_, v_))

    return (total, dx[None], *[g[n] for n in WEIGHTS], *[delta[n] for n in WEIGHTS],
            *[new_m[n] for n in WEIGHTS], *[new_v[n] for n in WEIGHTS])
```

```python
import functools

import jax
import jax.numpy as jnp
from jax import lax
from jax.experimental import pallas as pl
from jax.experimental.pallas import tpu as pltpu
from jax.experimental.pallas import tpu_sc as plsc

F32 = jnp.float32
BF16 = jnp.bfloat16
MESH = pl.DeviceIdType.MESH

D = 1024
SEQ = 4096
DEPTH = 2
N_DEV = 8
POOL_WINDOWS = (2, 4, 8, 16)
POOL_W = 512
POOL_G = 128
CONV_W = 512
HEAD = 64
ATTN_GROUPS = ((128, 1), (512, 4), (2048, 16))
HPG = 4
ATTN_W = 768
ATTN_OUT = 256
ATTN_BLK = 128
D_FF = 4096
EPS = 1e-6
MASK = -1e30
OFF_POOL = 0
OFF_CB = 512
OFF_CC = 1024
OFF_CX = 1536
OFF_Q = 2048
OFF_K = 2816
OFF_V = 3584
OFF_GATE = 4352
IN_COLS = 7424
ADAM_LR = 0.001
ADAM_B1 = 0.9
ADAM_B2 = 0.999
ADAM_EPS = 1e-08
ADAM_WD = 0.01
ADAM_STEP = 10

ROW_TILE = 512
SEQ_CHUNK = 256
HALO = 16
VMEM_LIMIT = 56 * 1024 * 1024


def _cparams(sem=None):
    return pltpu.CompilerParams(dimension_semantics=sem, vmem_limit_bytes=VMEM_LIMIT)


def _call(body, args, in_specs, after=None, **kw):
    if after is not None:
        after = list(after) if isinstance(after, (list, tuple)) else [after]
        n, k, inner = len(args), len(after), body

        def body(*refs):
            return inner(*refs[:n], *refs[n + k:])

        args = list(args) + after
        in_specs = list(in_specs) + [pl.BlockSpec(memory_space=pl.ANY)] * k
    return pl.pallas_call(body, in_specs=in_specs, **kw)(*args)


_DIMS = {"nn": (((1,), (0,)), ((), ())), "nt": (((1,), (1,)), ((), ())), "tn": (((0,), (0,)), ((), ()))}


def _matmul(name, a, b, mode, tm, tn, tk, out_dtypes=(F32,), extras=(), epilogue=None, into=None, after=None,
            b_rows=None):
    if mode == "tn":
        K, M = a.shape
    else:
        M, K = a.shape
    b_shape = b.shape if b_rows is None else (b.shape[0] * b_rows[1], b.shape[2])
    N = b_shape[0] if mode == "nt" else b_shape[1]
    assert b_shape[1 if mode == "nt" else 0] == K, (name, b_shape, K)
    assert M % tm == 0 and N % tn == 0 and K % tk == 0, (name, M, N, K, tm, tn, tk)
    nk = K // tk
    n_extra = len(extras)
    n_out = len(out_dtypes)
    dims = _DIMS[mode]
    n_alias = 0 if into is None or isinstance(into[0], jax.ShapeDtypeStruct) else 1

    def body(a_ref, b_ref, *rest):
        extra_refs = rest[:n_extra]
        out_refs = rest[n_extra + n_alias:n_extra + n_alias + n_out]

        def finish(acc):
            if epilogue is None:
                res = (acc,)
            else:
                res = epilogue(acc, *[r[...] for r in extra_refs])
            for o_ref, v in zip(out_refs, res):
                o_ref[...] = v.astype(o_ref.dtype)

        bv = b_ref[...]
        if b_rows is not None:
            bv = bv.reshape(-1, bv.shape[-1])
        part = lax.dot_general(a_ref[...].astype(BF16), bv.astype(BF16), dims, preferred_element_type=F32)
        if nk == 1:
            finish(part)
        else:
            acc_ref = rest[-1]
            k = pl.program_id(2)

            @pl.when(k == 0)
            def _():
                acc_ref[...] = part

            @pl.when(k > 0)
            def _():
                acc_ref[...] += part

            @pl.when(k == nk - 1)
            def _():
                finish(acc_ref[...])

    if mode == "tn":
        a_spec = pl.BlockSpec((tk, tm), lambda i, j, k: (k, i))
    else:
        a_spec = pl.BlockSpec((tm, tk), lambda i, j, k: (i, k))
    if b_rows is not None:
        row0, rows = b_rows
        t_rows = tn if mode == "nt" else tk
        assert t_rows % rows == 0 and row0 % rows == 0, (name, t_rows, b_rows)
        if mode == "nt":
            b_spec = pl.BlockSpec((t_rows // rows, rows, tk), lambda i, j, k: (j, row0 // rows, k))
        else:
            b_spec = pl.BlockSpec((t_rows // rows, rows, tn), lambda i, j, k: (k, row0 // rows, j))
    elif mode == "nt":
        b_spec = pl.BlockSpec((tn, tk), lambda i, j, k: (j, k))
    else:
        b_spec = pl.BlockSpec((tk, tn), lambda i, j, k: (k, j))
    in_specs = [a_spec, b_spec]
    args = [a, b]
    for arr, kind in extras:
        if kind == "mn":
            in_specs.append(pl.BlockSpec((tm, tn), lambda i, j, k: (i, j)))
        else:
            in_specs.append(pl.BlockSpec((1, tn), lambda i, j, k: (0, j)))
        args.append(arr)
    part = [isinstance(dt, tuple) for dt in out_dtypes]
    out_shape = tuple(jax.ShapeDtypeStruct((M // tm * 8, N), dt[0]) if p else jax.ShapeDtypeStruct((M, N), dt)
                      for dt, p in zip(out_dtypes, part))
    out_specs = tuple(pl.BlockSpec((8 if p else tm, tn), lambda i, j, k: (i, j)) for p in part)
    aliases = {}
    if into is not None:
        buf, row0, col0 = into
        assert n_out == 1 and (M // N_DEV) % tm == 0 and row0 % tm == 0 and col0 % tn == 0
        per_dev = M // N_DEV // tm
        out_shape = (jax.ShapeDtypeStruct(buf.shape, buf.dtype),)
        out_specs = (pl.BlockSpec((None, tm, tn), lambda i, j, k: (i // per_dev, row0 // tm + i % per_dev,
                                                                   col0 // tn + j)),)
        if not isinstance(buf, jax.ShapeDtypeStruct):
            aliases = {len(args): 0}
            in_specs.append(pl.BlockSpec(memory_space=pl.ANY))
            args.append(buf)
    scratch = [] if nk == 1 else [pltpu.VMEM((tm, tn), F32)]
    res = _call(
        body, args, in_specs, after, name=name, grid=(M // tm, N // tn, nk), out_specs=out_specs,
        out_shape=out_shape, scratch_shapes=scratch, input_output_aliases=aliases,
        compiler_params=_cparams(("parallel", "parallel", "arbitrary")))
    return res if n_out > 1 else res[0]


def _rmsnorm_fwd(name, x, gain, after=None):
    S_, D_ = x.shape

    def body(x_ref, g_ref, h_ref):
        xf = x_ref[...]
        ms = jnp.mean(xf * xf, axis=-1, keepdims=True)
        h_ref[...] = (xf * lax.rsqrt(ms + EPS) * g_ref[...]).astype(BF16)

    return _call(
        body, [x, gain], [pl.BlockSpec((ROW_TILE, D_), lambda i: (i, 0)), pl.BlockSpec((1, D_), lambda i: (0, 0))],
        after, name=name, grid=(S_ // ROW_TILE,),
        out_specs=pl.BlockSpec((ROW_TILE, D_), lambda i: (i, 0)),
        out_shape=jax.ShapeDtypeStruct((S_, D_), BF16),
        compiler_params=_cparams(("parallel",)))


def _rmsnorm_bwd(name, x, gain, dh, dres, after=None):
    S_, D_ = x.shape
    n = S_ // ROW_TILE

    def body(x_ref, g_ref, dh_ref, dres_ref, dx_ref, dg_ref, acc_ref):
        i = pl.program_id(0)
        xf = x_ref[...]
        rstd = lax.rsqrt(jnp.mean(xf * xf, axis=-1, keepdims=True) + EPS)
        xhat = xf * rstd
        dhv = dh_ref[...]
        dxhat = dhv * g_ref[...]
        c = jnp.mean(dxhat * xhat, axis=-1, keepdims=True)
        dx_ref[...] = dres_ref[...] + rstd * (dxhat - xhat * c)
        part = jnp.sum((dhv * xhat).reshape(ROW_TILE // 8, 8, D_), axis=0)

        @pl.when(i == 0)
        def _():
            acc_ref[...] = part

        @pl.when(i > 0)
        def _():
            acc_ref[...] += part

        @pl.when(i == n - 1)
        def _():
            dg_ref[...] = jnp.sum(acc_ref[...], axis=0, keepdims=True)

    row = pl.BlockSpec((ROW_TILE, D_), lambda i: (i, 0))
    vec = pl.BlockSpec((1, D_), lambda i: (0, 0))
    return _call(
        body, [x, gain, dh, dres], [row, vec, row, row], after, name=name, grid=(n,), out_specs=(row, vec),
        out_shape=(jax.ShapeDtypeStruct((S_, D_), F32), jax.ShapeDtypeStruct((1, D_), F32)),
        scratch_shapes=[pltpu.VMEM((8, D_), F32)],
        compiler_params=_cparams(("arbitrary",)))


def _loss_head(name, y, target):
    S_, D_ = y.shape
    n = S_ // ROW_TILE

    def body(y_ref, t_ref, dy_ref, l_ref, acc_ref):
        i = pl.program_id(0)
        e = y_ref[...] - t_ref[...]
        dy_ref[...] = e * (1.0 / D_)
        part = jnp.sum((e * e).reshape(ROW_TILE // 8, 8, D_), axis=0)

        @pl.when(i == 0)
        def _():
            acc_ref[...] = part

        @pl.when(i > 0)
        def _():
            acc_ref[...] += part

        @pl.when(i == n - 1)
        def _():
            s = jnp.sum(acc_ref[...], axis=1, keepdims=True)
            l_ref[...] = jnp.sum(s, axis=0, keepdims=True) * (0.5 / D_)

    row = pl.BlockSpec((ROW_TILE, D_), lambda i: (i, 0))
    return pl.pallas_call(
        body, name=name, grid=(n,), in_specs=[row, row],
        out_specs=(row, pl.BlockSpec((1, 1), lambda i: (0, 0))),
        out_shape=(jax.ShapeDtypeStruct((S_, D_), F32), jax.ShapeDtypeStruct((1, 1), F32)),
        scratch_shapes=[pltpu.VMEM((8, D_), F32)],
        compiler_params=_cparams(("arbitrary",)),
    )(y, target)


def _rows_with_halo(ref, cols, i, n_chunks, before, after):
    r0 = pl.multiple_of(i * SEQ_CHUNK, SEQ_CHUNK)
    parts = []
    if before:
        h0 = pl.multiple_of(jnp.maximum(r0 - HALO, 0), 8)
        halo = ref[pl.ds(h0, HALO), cols]
        parts.append(jnp.where(i > 0, halo, jnp.zeros_like(halo)))
    parts.append(ref[pl.ds(r0, SEQ_CHUNK), cols])
    if after:
        a0 = pl.multiple_of(jnp.minimum(r0 + SEQ_CHUNK, (n_chunks - 1) * SEQ_CHUNK + SEQ_CHUNK - HALO), 8)
        halo = ref[pl.ds(a0, HALO), cols]
        parts.append(jnp.where(i < n_chunks - 1, halo, jnp.zeros_like(halo)))
    return parts[0] if len(parts) == 1 else jnp.concatenate(parts, axis=0)


def _shift_down(v, k):
    return pltpu.roll(v, k, 0)


def _shift_up(v, k):
    return pltpu.roll(v, v.shape[0] - k, 0)


def _pool_diff(xx, w, t_main):
    s = xx
    k = 1
    while k < w:
        s = s + _shift_down(s, k)
        k *= 2
    cnt = jnp.minimum(t_main + 1, w).astype(F32)
    return s[HALO:] / cnt - xx[HALO:]


def _pool_fwd(name, z, pool_mix, pool_scale):
    S_ = z.shape[0]
    n_chunks = S_ // SEQ_CHUNK

    cols = slice(0, POOL_G)

    def body(u_ref, mix_ref, sc_ref, y_ref):
        mixg = mix_ref[...].astype(BF16)
        scg = sc_ref[...]
        for g, w in enumerate(POOL_WINDOWS):
            @pl.when(pl.program_id(0) == g)
            def _(w=w):
                def chunk(i, carry):
                    r0 = pl.multiple_of(i * SEQ_CHUNK, SEQ_CHUNK)
                    xx = _rows_with_halo(u_ref, cols, i, n_chunks, True, False)
                    t = r0 + lax.broadcasted_iota(jnp.int32, (SEQ_CHUNK, POOL_G), 0)
                    d = _pool_diff(xx, w, t)
                    y = jnp.dot(d.astype(BF16), mixg, preferred_element_type=F32) * scg
                    y_ref[pl.ds(r0, SEQ_CHUNK), :] = y.astype(BF16)
                    return carry

                lax.fori_loop(0, n_chunks, chunk, 0)

    slab = pl.BlockSpec((S_, POOL_G), lambda g: (0, g))
    return pl.pallas_call(
        body, name=name, grid=(len(POOL_WINDOWS),),
        in_specs=[slab, pl.BlockSpec((None, POOL_G, POOL_G), lambda g: (g, 0, 0)),
                  pl.BlockSpec((1, POOL_G), lambda g: (0, g))],
        out_specs=slab, out_shape=jax.ShapeDtypeStruct((S_, POOL_W), BF16),
        compiler_params=_cparams(("parallel",)),
    )(z, pool_mix, pool_scale)


def _pool_bwd(name, z, dy, pool_mix, pool_scale, dz, after=None):
    S_ = z.shape[0]
    n_chunks = S_ // SEQ_CHUNK
    rows_a = SEQ_CHUNK + HALO

    cols = slice(0, POOL_G)

    def body(u_ref, dy_ref, mix_ref, sc_ref, du_ref, dmix_ref, dsc_ref):
        mixg = mix_ref[...].astype(BF16)
        scg = sc_ref[...]
        for g, w in enumerate(POOL_WINDOWS):
            @pl.when(pl.program_id(0) == g)
            def _(w=w):
                def chunk(i, carry):
                    dmix_acc, dsc_acc = carry
                    r0 = pl.multiple_of(i * SEQ_CHUNK, SEQ_CHUNK)
                    xx = _rows_with_halo(u_ref, cols, i, n_chunks, True, False)
                    t = r0 + lax.broadcasted_iota(jnp.int32, (SEQ_CHUNK, POOL_G), 0)
                    d = _pool_diff(xx, w, t).astype(BF16)
                    ypre = jnp.dot(d, mixg, preferred_element_type=F32)
                    dyy = _rows_with_halo(dy_ref, cols, i, n_chunks, False, True)
                    dys = (dyy * scg).astype(BF16)
                    dsc_acc = dsc_acc + jnp.sum((dyy[:SEQ_CHUNK] * ypre).reshape(SEQ_CHUNK // 8, 8, POOL_G), axis=0)
                    dmix_acc = dmix_acc + lax.dot_general(d, dys[:SEQ_CHUNK], _DIMS["tn"],
                                                          preferred_element_type=F32)
                    dd = lax.dot_general(dys, mixg, _DIMS["nt"], preferred_element_type=F32)
                    ta = r0 + lax.broadcasted_iota(jnp.int32, (rows_a, POOL_G), 0)
                    f = dd / jnp.minimum(ta + 1, w).astype(F32)
                    k = 1
                    while k < w:
                        f = f + _shift_up(f, k)
                        k *= 2
                    du_ref[pl.ds(r0, SEQ_CHUNK), :] = (f[:SEQ_CHUNK] - dd[:SEQ_CHUNK]).astype(BF16)
                    return dmix_acc, dsc_acc

                dmix_acc, dsc_acc = lax.fori_loop(
                    0, n_chunks, chunk, (jnp.zeros((POOL_G, POOL_G), F32), jnp.zeros((8, POOL_G), F32)))
                dmix_ref[...] = dmix_acc
                dsc_ref[...] = jnp.sum(dsc_acc, axis=0, keepdims=True)

    slab = pl.BlockSpec((S_, POOL_G), lambda g: (0, g))
    mix_spec = pl.BlockSpec((None, POOL_G, POOL_G), lambda g: (g, 0, 0))
    vec = pl.BlockSpec((1, POOL_G), lambda g: (0, g))
    return _call(
        _drop_ref(body, 4), [z, dy, pool_mix, pool_scale, dz], [slab, slab, mix_spec, vec, ANY], after, name=name,
        grid=(len(POOL_WINDOWS),), out_specs=(slab, mix_spec, vec),
        out_shape=(jax.ShapeDtypeStruct(dz.shape, dz.dtype), jax.ShapeDtypeStruct((4, POOL_G, POOL_G), F32),
                   jax.ShapeDtypeStruct((1, POOL_W), F32)),
        input_output_aliases={4: 0}, compiler_params=_cparams(("parallel",)))


def _conv_specs(S_):
    slab = lambda off: pl.BlockSpec((S_, 128), lambda c, off=off: (0, off // 128 + c))
    return slab(OFF_CB), slab(OFF_CC), slab(OFF_CX)


def _conv_fwd(name, z, conv_w, after=None):
    S_ = z.shape[0]
    n_chunks = S_ // SEQ_CHUNK
    col = slice(0, 128)

    def body(b_ref, c_ref, x_ref, w_ref, y_ref):
        w0, w1, w2 = w_ref[0:1, :], w_ref[1:2, :], w_ref[2:3, :]

        def chunk(i, carry):
            r0 = pl.multiple_of(i * SEQ_CHUNK, SEQ_CHUNK)
            u = _rows_with_halo(c_ref, col, i, n_chunks, True, False) * _rows_with_halo(x_ref, col, i, n_chunks, True, False)
            y = w2 * u + w1 * _shift_down(u, 1) + w0 * _shift_down(u, 2)
            y_ref[pl.ds(r0, SEQ_CHUNK), :] = (b_ref[pl.ds(r0, SEQ_CHUNK), :] * y[HALO:]).astype(BF16)
            return carry

        lax.fori_loop(0, n_chunks, chunk, 0)

    sb, sc, sx = _conv_specs(S_)
    return _call(
        body, [z, z, z, conv_w], [sb, sc, sx, pl.BlockSpec((3, 128), lambda c: (0, c))], after,
        name=name, grid=(CONV_W // 128,),
        out_specs=pl.BlockSpec((S_, 128), lambda c: (0, c)),
        out_shape=jax.ShapeDtypeStruct((S_, CONV_W), BF16),
        compiler_params=_cparams(("parallel",)))


def _conv_bwd(name, z, dy, conv_w, after=None):
    S_ = z.shape[0]
    n_chunks = S_ // SEQ_CHUNK
    col = slice(0, 128)
    lo, hi = HALO, HALO + SEQ_CHUNK

    def body(b_ref, c_ref, x_ref, dy_ref, w_ref, db_ref, dc_ref, dx_ref, dw_ref):
        w0, w1, w2 = w_ref[0:1, :], w_ref[1:2, :], w_ref[2:3, :]

        def chunk(i, carry):
            a0, a1, a2 = carry
            r0 = pl.multiple_of(i * SEQ_CHUNK, SEQ_CHUNK)
            cc = _rows_with_halo(c_ref, col, i, n_chunks, True, True)
            xx = _rows_with_halo(x_ref, col, i, n_chunks, True, True)
            bb = _rows_with_halo(b_ref, col, i, n_chunks, True, True)
            dyy = _rows_with_halo(dy_ref, col, i, n_chunks, True, True)
            u = cc * xx
            u1 = _shift_down(u, 1)
            u2 = _shift_down(u, 2)
            y = w2 * u + w1 * u1 + w0 * u2
            dyv = dyy * bb
            du = w2 * dyv + w1 * _shift_up(dyv, 1) + w0 * _shift_up(dyv, 2)
            db_ref[pl.ds(r0, SEQ_CHUNK), :] = (dyy[lo:hi] * y[lo:hi]).astype(BF16)
            dc_ref[pl.ds(r0, SEQ_CHUNK), :] = (du[lo:hi] * xx[lo:hi]).astype(BF16)
            dx_ref[pl.ds(r0, SEQ_CHUNK), :] = (du[lo:hi] * cc[lo:hi]).astype(BF16)
            red = lambda v: jnp.sum(v.reshape(SEQ_CHUNK // 8, 8, 128), axis=0)
            dm = dyv[lo:hi]
            return a0 + red(dm * u2[lo:hi]), a1 + red(dm * u1[lo:hi]), a2 + red(dm * u[lo:hi])

        zero = jnp.zeros((8, 128), F32)
        a0, a1, a2 = lax.fori_loop(0, n_chunks, chunk, (zero, zero, zero))
        dw_ref[0:1, :] = jnp.sum(a0, axis=0, keepdims=True)
        dw_ref[1:2, :] = jnp.sum(a1, axis=0, keepdims=True)
        dw_ref[2:3, :] = jnp.sum(a2, axis=0, keepdims=True)

    sb, sc, sx = _conv_specs(S_)
    slab = pl.BlockSpec((S_, 128), lambda c: (0, c))
    wspec = pl.BlockSpec((3, 128), lambda c: (0, c))
    act = jax.ShapeDtypeStruct((S_, CONV_W), BF16)
    return _call(
        body, [z, z, z, dy, conv_w], [sb, sc, sx, slab, wspec], after, name=name, grid=(CONV_W // 128,),
        out_specs=(slab, slab, slab, wspec),
        out_shape=(act, act, act, jax.ShapeDtypeStruct((3, CONV_W), F32)),
        compiler_params=_cparams(("parallel",)))


def _head_ones(pw):
    a = lax.broadcasted_iota(jnp.int32, (pw, pw), 0) // HEAD
    b = lax.broadcasted_iota(jnp.int32, (pw, pw), 1) // HEAD
    return (a == b).astype(BF16)


def _head_sum(v, ones):
    hi = v.astype(BF16)
    lo = (v - hi.astype(F32)).astype(BF16)
    return jnp.dot(hi, ones, preferred_element_type=F32) + jnp.dot(lo, ones, preferred_element_type=F32)


def _head_norm(v, gain, ones):
    rstd = lax.rsqrt(_head_sum(v * v, ones) * (1.0 / HEAD) + EPS)
    xhat = v * rstd
    return xhat * gain, xhat, rstd


def _head_norm_bwd(dy, xhat, rstd, gain, ones):
    dxhat = dy * gain
    c = _head_sum(dxhat * xhat, ones) * (1.0 / HEAD)
    dv = rstd * (dxhat - xhat * c)
    dg = jnp.sum((dy * xhat).reshape(dy.shape[0] // 8, 8, dy.shape[1]), axis=0)
    return dv, dg


def _head_masks(pw):
    lane_head = lax.broadcasted_iota(jnp.int32, (1, pw), 1) // HEAD
    return [lane_head == h for h in range(pw // HEAD)]


def _only(mask, v):
    return jnp.where(mask, v, jnp.zeros_like(v))


def _attn_specs(S_, g, dil):
    rows = ATTN_BLK * dil
    nb = S_ // rows
    pw = 128 if dil > 1 else ATTN_OUT
    cq, ck, cv = ((OFF_Q + g * ATTN_OUT) // pw, (OFF_K + g * ATTN_OUT) // pw, (OFF_V + g * ATTN_OUT) // pw)
    return rows, nb, pw, pw // HEAD, ATTN_OUT // pw, cq, ck, cv


ATTN_BATCH = 4


def _attn_group(dil):
    return 4 if dil == 1 else 1


def _attn_block_specs(rows, grp, pw, last=None):
    step = (lambda n: n) if last is None else (lambda n: jnp.minimum(n, last))
    cur = lambda c: pl.BlockSpec((rows * grp, pw), lambda hp, n, c=c: (step(n), c + hp))
    prev = lambda c: pl.BlockSpec((rows, pw), lambda hp, n, c=c: (jnp.maximum(step(n) * grp - 1, 0), c + hp))
    return cur, prev


def _band_mask(has_prev):
    qi = lax.broadcasted_iota(jnp.int32, (ATTN_BLK, 2 * ATTN_BLK), 0)
    ki = lax.broadcasted_iota(jnp.int32, (ATTN_BLK, 2 * ATTN_BLK), 1)
    in_prev = jnp.logical_and(ki < ATTN_BLK, ki >= qi)
    if has_prev is not True:
        in_prev = jnp.logical_and(in_prev, has_prev)
    return jnp.logical_or(in_prev, jnp.logical_and(ki >= ATTN_BLK, ki - ATTN_BLK <= qi))


def _rows_of(ref, r, dil):
    if dil == 1:
        return ref[r * ATTN_BLK:(r + 1) * ATTN_BLK, :]
    return ref[pl.ds(r, ATTN_BLK, stride=dil), :]


def _put_rows(ref, r, dil, val):
    if dil == 1:
        ref[r * ATTN_BLK:(r + 1) * ATTN_BLK, :] = val.astype(ref.dtype)
    else:
        ref[pl.ds(r, ATTN_BLK, stride=dil), :] = val.astype(ref.dtype)


def _attn_fwd(name, z, q_gain, k_gain, g, dil):
    S_ = z.shape[0]
    rows, nb, pw, heads, npairs, cq, ck, cv = _attn_specs(S_, g, dil)
    scale = HEAD ** -0.5

    grp = _attn_group(dil)
    nsteps = nb // grp

    def body(q_ref, k_ref, kp_ref, v_ref, vp_ref, gq_ref, gk_ref, o_ref, l_ref):
        n = pl.program_id(1)
        ones, hmask = _head_ones(pw), _head_masks(pw)
        gq, gk = jnp.tile(gq_ref[...], (1, heads)), jnp.tile(gk_ref[...], (1, heads))
        mask_first, mask_rest = _band_mask(n > 0), _band_mask(True)
        for r0 in range(0, dil * grp, ATTN_BATCH):
            rs = range(r0, min(r0 + ATTN_BATCH, dil * grp))
            qn, kn, vv, s, p = {}, {}, {}, {}, {}
            kcn = {}
            for r in rs:
                q, kc, vc = _rows_of(q_ref, r, dil), _rows_of(k_ref, r, dil), _rows_of(v_ref, r, dil)
                kcn[r] = _head_norm(kc, gk, ones)[0]
                if dil == 1 and r > 0:
                    kpn = kcn[r - 1] if r - 1 in kcn else _head_norm(_rows_of(k_ref, r - 1, dil), gk, ones)[0]
                    vp = _rows_of(v_ref, r - 1, dil)
                else:
                    kpn, vp = _head_norm(_rows_of(kp_ref, r, dil), gk, ones)[0], _rows_of(vp_ref, r, dil)
                qn[r] = _head_norm(q, gq, ones)[0].astype(BF16)
                kn[r] = jnp.concatenate([kpn, kcn[r]], axis=0).astype(BF16)
                vv[r] = jnp.concatenate([vp, vc], axis=0).astype(BF16)
            keys = [(r, h) for r in rs for h in range(heads)]
            for r, h in keys:
                s[r, h] = lax.dot_general(_only(hmask[h], qn[r]), kn[r], _DIMS["nt"],
                                          preferred_element_type=F32) * scale
            lse, den = {}, {}
            for key in keys:
                mask = mask_rest if (dil == 1 and key[0] > 0) else mask_first
                sm = jnp.where(mask, s[key], MASK)
                m = jnp.max(sm, axis=-1, keepdims=True)
                e = jnp.exp(sm - m)
                den[key] = jnp.sum(e, axis=-1, keepdims=True)
                p[key] = e.astype(BF16)
                lse[key] = m + jnp.log(den[key])
            for r in rs:
                out = jnp.zeros((ATTN_BLK, pw), F32)
                lse_all = jnp.zeros((ATTN_BLK, pw), F32)
                for h in range(heads):
                    out = jnp.where(hmask[h], jnp.dot(p[r, h], vv[r], preferred_element_type=F32) / den[r, h], out)
                    lse_all = jnp.where(hmask[h], lse[r, h], lse_all)
                _put_rows(o_ref, r, dil, out)
                _put_rows(l_ref, r, dil, lse_all)

    cur, prev = _attn_block_specs(rows, grp, pw)
    gspec = pl.BlockSpec((1, HEAD), lambda hp, n: (0, 0))
    shp = jax.ShapeDtypeStruct((S_, ATTN_OUT), F32)
    return pl.pallas_call(
        body, name=name, grid=(npairs, nsteps),
        in_specs=[cur(cq), cur(ck), prev(ck), cur(cv), prev(cv), gspec, gspec],
        out_specs=(cur(0), cur(0)), out_shape=(shp, shp),
        compiler_params=_cparams(("parallel", "parallel")),
    )(z, z, z, z, z, q_gain, k_gain)


def _attn_combine(name, os_, ls_):
    S_ = os_[0].shape[0]

    def body(o0, o1, o2, l0, l1, l2, o_ref, l_ref):
        a, b, c = l0[...], l1[...], l2[...]
        m = jnp.maximum(jnp.maximum(a, b), c)
        ea, eb, ec = jnp.exp(a - m), jnp.exp(b - m), jnp.exp(c - m)
        zsum = ea + eb + ec
        o_ref[...] = (ea * o0[...] + eb * o1[...] + ec * o2[...]) / zsum
        l_ref[...] = m + jnp.log(zsum)

    row = pl.BlockSpec((ROW_TILE, ATTN_OUT), lambda i: (i, 0))
    shp = jax.ShapeDtypeStruct((S_, ATTN_OUT), F32)
    return pl.pallas_call(
        body, name=name, grid=(S_ // ROW_TILE,), in_specs=[row] * 6, out_specs=(row, row), out_shape=(shp, shp),
        compiler_params=_cparams(("parallel",)),
    )(*os_, *ls_)


def _attn_bwd(name, z, q_gain, k_gain, do, o, lse, g, dil, after=None):
    S_ = z.shape[0]
    rows, nb, pw, heads, npairs, cq, ck, cv = _attn_specs(S_, g, dil)
    scale = HEAD ** -0.5

    def body(q_ref, kc_ref, kp_ref, vc_ref, vp_ref, gq_ref, gk_ref, do_ref, o_ref, l_ref,
             dq_ref, dk_ref, dv_ref, dgq_ref, dgk_ref, ck_ref, cvv_ref, gq_acc, gk_acc):
        hp = pl.program_id(0)
        n = pl.program_id(1)
        ones = _head_ones(pw)
        gq, gk = jnp.tile(gq_ref[...], (1, heads)), jnp.tile(gk_ref[...], (1, heads))

        @pl.when(n == 0)
        def _():
            ck_ref[...] = jnp.zeros_like(ck_ref)
            cvv_ref[...] = jnp.zeros_like(cvv_ref)

        @pl.when(jnp.logical_and(n == 0, hp == 0))
        def _():
            gq_acc[...] = jnp.zeros_like(gq_acc)
            gk_acc[...] = jnp.zeros_like(gk_acc)

        pl.when(n < nb)(functools.partial(query_step, n, ones, gq, gk, q_ref, kc_ref, kp_ref, vc_ref, vp_ref, do_ref,
                                          o_ref, l_ref, dq_ref, dk_ref, dv_ref, ck_ref, cvv_ref, gq_acc, gk_acc))

        @pl.when(n == nb)
        def _():
            dgk = jnp.zeros((8, pw), F32)
            for r in range(dil):
                _, kphat, kprstd = _head_norm(_rows_of(kp_ref, r, dil), gk, ones)
                dk_all, dg = _head_norm_bwd(_rows_of(ck_ref, r, dil), kphat, kprstd, gk, ones)
                dgk = dgk + dg
                _put_rows(dk_ref, r, dil, dk_all)
                _put_rows(dv_ref, r, dil, _rows_of(cvv_ref, r, dil))
            gk_acc[...] += dgk

        @pl.when(jnp.logical_and(n == nb, hp == npairs - 1))
        def _():
            fold = lambda a: sum(a[:, h * HEAD:(h + 1) * HEAD] for h in range(heads))
            dgq_ref[...] = jnp.sum(fold(gq_acc[...]), axis=0, keepdims=True)
            dgk_ref[...] = jnp.sum(fold(gk_acc[...]), axis=0, keepdims=True)

    def query_step(n, ones, gq, gk, q_ref, kc_ref, kp_ref, vc_ref, vp_ref, do_ref, o_ref, l_ref,
                   dq_ref, dk_ref, dv_ref, ck_ref, cvv_ref, gq_acc, gk_acc):
        mask = _band_mask(n > 0)
        hmask = _head_masks(pw)
        dgq = jnp.zeros((8, pw), F32)
        dgk = jnp.zeros((8, pw), F32)
        for r0 in range(0, dil, ATTN_BATCH):
            rs = range(r0, min(r0 + ATTN_BATCH, dil))
            keys = [(r, h) for r in rs for h in range(heads)]
            qn, qhat, qrstd, kn, kphat, kprstd, vv, dob, delta, lse = ({} for _ in range(10))
            for r in rs:
                q, kc, kp = _rows_of(q_ref, r, dil), _rows_of(kc_ref, r, dil), _rows_of(kp_ref, r, dil)
                dov = _rows_of(do_ref, r, dil)
                qn_f, qhat[r], qrstd[r] = _head_norm(q, gq, ones)
                kpn, kphat[r], kprstd[r] = _head_norm(kp, gk, ones)
                qn[r] = qn_f.astype(BF16)
                kn[r] = jnp.concatenate([kpn, _head_norm(kc, gk, ones)[0]], axis=0).astype(BF16)
                vv[r] = jnp.concatenate([_rows_of(vp_ref, r, dil), _rows_of(vc_ref, r, dil)], axis=0).astype(BF16)
                dob[r] = dov.astype(BF16)
                delta[r] = _head_sum(dov * _rows_of(o_ref, r, dil), ones)
                lse[r] = _rows_of(l_ref, r, dil)
            s, dp = {}, {}
            for r, h in keys:
                s[r, h] = lax.dot_general(_only(hmask[h], qn[r]), kn[r], _DIMS["nt"],
                                          preferred_element_type=F32) * scale
                dp[r, h] = lax.dot_general(_only(hmask[h], dob[r]), vv[r], _DIMS["nt"], preferred_element_type=F32)
            p, ds = {}, {}
            for r, h in keys:
                col = slice(h * HEAD, h * HEAD + 1)
                pk = jnp.where(mask, jnp.exp(jnp.where(mask, s[r, h], MASK) - lse[r][:, col]), 0.0)
                ds[r, h] = (pk * (dp[r, h] - delta[r][:, col]) * scale).astype(BF16)
                p[r, h] = pk.astype(BF16)
            dqn, dkn, dvv = {}, {}, {}
            for r in rs:
                dqn[r] = jnp.zeros((ATTN_BLK, pw), F32)
                dkn[r] = jnp.zeros((2 * ATTN_BLK, pw), F32)
                dvv[r] = jnp.zeros((2 * ATTN_BLK, pw), F32)
                for h in range(heads):
                    dqn[r] = jnp.where(hmask[h], jnp.dot(ds[r, h], kn[r], preferred_element_type=F32), dqn[r])
                    dkn[r] = jnp.where(hmask[h], lax.dot_general(ds[r, h], qn[r], _DIMS["tn"],
                                                                 preferred_element_type=F32), dkn[r])
                    dvv[r] = jnp.where(hmask[h], lax.dot_general(p[r, h], dob[r], _DIMS["tn"],
                                                                 preferred_element_type=F32), dvv[r])
            for r in rs:
                dq_all, dg = _head_norm_bwd(dqn[r], qhat[r], qrstd[r], gq, ones)
                dgq = dgq + dg
                dk_all, dg = _head_norm_bwd(_rows_of(ck_ref, r, dil) + dkn[r][:ATTN_BLK], kphat[r], kprstd[r], gk, ones)
                dgk = dgk + dg
                dv_all = _rows_of(cvv_ref, r, dil) + dvv[r][:ATTN_BLK]
                _put_rows(dq_ref, r, dil, dq_all)
                _put_rows(dk_ref, r, dil, dk_all)
                _put_rows(dv_ref, r, dil, dv_all)
                _put_rows(ck_ref, r, dil, dkn[r][ATTN_BLK:])
                _put_rows(cvv_ref, r, dil, dvv[r][ATTN_BLK:])
        gq_acc[...] += dgq
        gk_acc[...] += dgk

    last = nb - 1
    cur = lambda c: pl.BlockSpec((rows, pw), lambda hp, n, c=c: (jnp.minimum(n, last), c + hp))
    prev = lambda c: pl.BlockSpec((rows, pw), lambda hp, n, c=c: (jnp.maximum(n - 1, 0), c + hp))
    gspec = pl.BlockSpec((1, HEAD), lambda hp, n: (0, 0))
    act = jax.ShapeDtypeStruct((S_, ATTN_OUT), F32)
    vec = jax.ShapeDtypeStruct((1, HEAD), F32)
    return _call(
        body, [z, z, z, z, z, q_gain, k_gain, do, o, lse],
        [cur(cq), cur(ck), prev(ck), cur(cv), prev(cv), gspec, gspec, cur(0), cur(0), cur(0)], after,
        name=name, grid=(npairs, nb + 1),
        out_specs=(cur(0), prev(0), prev(0), gspec, gspec),
        out_shape=(act, act, act, vec, vec),
        scratch_shapes=[pltpu.VMEM((rows, pw), F32), pltpu.VMEM((rows, pw), F32),
                        pltpu.VMEM((8, pw), F32), pltpu.VMEM((8, pw), F32)],
        compiler_params=_cparams(("arbitrary", "arbitrary")))


MIX_TN = 256
MIX_TM = 2048


def _sigmoid(v):
    return 1.0 / (1.0 + jnp.exp(-v))


def _mix_fwd(name, z, b_gate, ys, ws):
    S_ = z.shape[0]
    tm, tn = MIX_TM, MIX_TN
    gblk = OFF_GATE // tn

    def body(yp, yc, ya, wp, wc, wa, g0, g1, g2, b0, b1, b2, m_ref):
        acc = None
        for y_ref, w_ref, g_ref, b_ref in ((yp, wp, g0, b0), (yc, wc, g1, b1), (ya, wa, g2, b2)):
            u = lax.dot_general(y_ref[...].astype(BF16), w_ref[...], _DIMS["nt"], preferred_element_type=F32)
            t = _sigmoid(g_ref[...] + b_ref[...]) * u
            acc = t if acc is None else acc + t
        m_ref[...] = acc.astype(BF16)

    yspec = lambda w: pl.BlockSpec((tm, w), lambda i, j: (i, 0))
    wspec = lambda w: pl.BlockSpec((tn, w), lambda i, j: (j, 0))
    gspec = lambda b: pl.BlockSpec((tm, tn), lambda i, j, b=b: (i, gblk + b * (D // tn) + j))
    bspec = lambda b: pl.BlockSpec((1, tn), lambda i, j, b=b: (0, b * (D // tn) + j))
    return pl.pallas_call(
        body, name=name, grid=(S_ // tm, D // tn),
        in_specs=[yspec(POOL_W), yspec(CONV_W), yspec(ATTN_OUT), wspec(POOL_W), wspec(CONV_W), wspec(ATTN_OUT),
                  gspec(0), gspec(1), gspec(2), bspec(0), bspec(1), bspec(2)],
        out_specs=pl.BlockSpec((tm, tn), lambda i, j: (i, j)),
        out_shape=jax.ShapeDtypeStruct((S_, D), BF16),
        compiler_params=_cparams(("parallel", "parallel")),
    )(*ys, *ws, z, z, z, b_gate, b_gate, b_gate)


def _mix_bwd(name, z, b_gate, y, w, dmerged, branch, dz, after=None):
    S_ = z.shape[0]
    tm, tn = MIX_TM, MIX_TN
    width = y.shape[1]
    gblk = OFF_GATE // tn + branch * (D // tn)
    nj = D // tn

    def body(y_ref, w_ref, g_ref, b_ref, dm_ref, dy_ref, dw_ref, dg_ref, db_ref, acc_ref):
        i, j = pl.program_id(0), pl.program_id(1)
        yb = y_ref[...].astype(BF16)
        u = lax.dot_general(yb, w_ref[...], _DIMS["nt"], preferred_element_type=F32)
        sg = _sigmoid(g_ref[...] + b_ref[...])
        dm = dm_ref[...]
        du = (sg * dm).astype(BF16)
        dpre = dm * u * sg * (1.0 - sg)
        dg_ref[...] = dpre.astype(BF16)
        cols = pl.ds(pl.multiple_of(j * tn, tn), tn)
        db_ref[...] = _rows8(dpre)
        d_w = lax.dot_general(du, yb, _DIMS["tn"], preferred_element_type=F32)
        d_y = jnp.dot(du, w_ref[...], preferred_element_type=F32)

        @pl.when(i == 0)
        def _():
            dw_ref[cols, :] = d_w

        @pl.when(i > 0)
        def _():
            dw_ref[cols, :] += d_w

        @pl.when(j == 0)
        def _():
            acc_ref[...] = d_y

        @pl.when(j > 0)
        def _():
            acc_ref[...] += d_y

        @pl.when(j == nj - 1)
        def _():
            dy_ref[...] = acc_ref[...]

    rows = pl.BlockSpec((tm, width), lambda i, j: (i, 0))
    blk = pl.BlockSpec((tm, tn), lambda i, j: (i, j))
    gate = pl.BlockSpec((tm, tn), lambda i, j: (i, gblk + j))
    args = [y, w, z, b_gate, dmerged]
    in_specs = [rows, pl.BlockSpec((tn, width), lambda i, j: (j, 0)), gate,
                pl.BlockSpec((1, tn), lambda i, j: (0, branch * nj + j)), blk]
    aliases = {}
    if not isinstance(dz, jax.ShapeDtypeStruct):
        body = _drop_ref(body, len(args))
        aliases = {len(args): 2}
        args.append(dz)
        in_specs.append(ANY)
    return _call(
        body, args, in_specs, after, name=name, grid=(S_ // tm, nj),
        out_specs=(rows, pl.BlockSpec((D, width), lambda i, j: (0, 0)), gate, pl.BlockSpec((8, tn), lambda i, j: (i, j))),
        out_shape=(jax.ShapeDtypeStruct((S_, width), F32), jax.ShapeDtypeStruct((D, width), F32),
                   jax.ShapeDtypeStruct(dz.shape, dz.dtype), jax.ShapeDtypeStruct((S_ // tm * 8, D), F32)),
        scratch_shapes=[pltpu.VMEM((tm, width), F32)], input_output_aliases=aliases,
        compiler_params=_cparams(("arbitrary", "arbitrary")))


def _relu2_epilogue(acc):
    r = jnp.maximum(acc, 0.0)
    return (r * r,)


def _relu2_bwd_epilogue(acc, r):
    return (acc * (2.0 * jnp.sqrt(r.astype(F32))),)


def _residual_norm_epilogue(acc, xr, gain):
    x = xr + acc
    ms = jnp.mean(x * x, axis=-1, keepdims=True)
    return x, x * lax.rsqrt(ms + EPS) * gain


def _fold_rows(name, part, lanes=False, scale=1.0):
    R, N = part.shape

    def body(p_ref, o_ref):
        s = jnp.sum(p_ref[...], axis=0, keepdims=True)
        if lanes:
            s = jnp.sum(s, axis=1, keepdims=True)
        o_ref[...] = s * scale

    out_n = 1 if lanes else N
    return pl.pallas_call(
        body, name=name, grid=(1,), in_specs=[pl.BlockSpec((R, N), lambda i: (0, 0))],
        out_specs=pl.BlockSpec((1, out_n), lambda i: (0, 0)), out_shape=jax.ShapeDtypeStruct((1, out_n), F32))(part)


def _rows8(v):
    return jnp.sum(v.reshape(v.shape[0] // 8, 8, v.shape[1]), axis=0)


def _residual_loss_epilogue(acc, xr, target):
    e = xr + acc - target
    dy = e * (1.0 / D)
    return dy, _rows8(e * e), dy


def _norm_bwd_epilogue(acc, x, dres, gain):
    rstd = lax.rsqrt(jnp.mean(x * x, axis=-1, keepdims=True) + EPS)
    xhat = x * rstd
    dxhat = acc * gain
    c = jnp.mean(dxhat * xhat, axis=-1, keepdims=True)
    return dres + rstd * (dxhat - xhat * c), _rows8(acc * xhat)


def _norm_bwd_epilogue_twice(acc, x, dres, gain):
    dx, part = _norm_bwd_epilogue(acc, x, dres, gain)
    return dx, part, dx


def _layer_fwd(l, x, h, p, next_gain, target=None, early=None):
    t = f"l{l}_"
    z = _matmul(t + "in_proj", h, p["w_in_t"], "nt", 512, 3712, 1024, after=early)
    y_pool = _pool_fwd(t + "pool", z, p["pool_mix"], p["pool_scale"])
    os_, ls_ = [], []
    for g, (_, dil) in enumerate(ATTN_GROUPS):
        o_g, l_g = _attn_fwd(t + f"attn{g}", z, p["q_gain"], p["k_gain"], g, dil)
        os_.append(o_g)
        ls_.append(l_g)
    y_attn, lse = _attn_combine(t + "attn_mix", os_, ls_)
    y_conv = _conv_fwd(t + "conv", z, p["conv_w"], after=y_attn)
    merged = _mix_fwd(t + "merge", z, p["b_gate"], (y_pool, y_conv, y_attn),
                      (p["w_pool_up_t"], p["w_conv_out_t"], p["w_attn_up_t"]))
    x1, h2 = _matmul(t + "out_proj", merged, p["rest"], "nn", 1024, D, 1024, out_dtypes=(F32, BF16),
                     extras=((x, "mn"), (p["norm_mlp"], "n")), epilogue=_residual_norm_epilogue, b_rows=ROWS_WO)
    r = _matmul(t + "ff1", h2, p["rest"], "nt", 1024, 1024, 1024, out_dtypes=(BF16,), epilogue=_relu2_epilogue,
                b_rows=ROWS_FF1)
    if next_gain is None:
        dy, h_out, dy_16 = _matmul(t + "ff2", r, p["rest"], "nn", 512, D, D_FF,
                                   out_dtypes=(F32, (F32, "rows8"), BF16), extras=((x1, "mn"), (target, "mn")),
                                   epilogue=_residual_loss_epilogue, b_rows=ROWS_FF2)
        x2 = (dy, dy_16)
    else:
        x2, h_out = _matmul(t + "ff2", r, p["rest"], "nn", 512, D, D_FF, out_dtypes=(F32, BF16),
                            extras=((x1, "mn"), (next_gain, "n")), epilogue=_residual_norm_epilogue,
                            b_rows=ROWS_FF2)
    saved = dict(x=x, h=h, z=z, y_pool=y_pool, y_conv=y_conv, y_attn=y_attn, lse=lse, merged=merged,
                 x1=x1, h2=h2, r=r)
    return x2, h_out, saved


def _layer_bwd(l, dx2, p, s, pending, in_t_all, collective_id, start_after=None):
    t = f"l{l}_b_"
    g = {}
    rest = jax.ShapeDtypeStruct((N_DEV, REST_ROWS, D), F32)
    dx2, dx2_16 = dx2
    da = _matmul(t + "d_ff2_in", dx2_16, p["rest"], "nt", 1024, 1024, 1024, out_dtypes=(BF16,),
                 extras=((s["r"], "mn"),), epilogue=_relu2_bwd_epilogue,
                 after=start_after, b_rows=ROWS_FF2)
    rest = _matmul(t + "dw_ff2", s["r"], dx2_16, "tn", 512, 1024, 4096, into=(rest, REST_FF2, 0), after=da)
    tok = rest if pending is None else pending.chip_sums(after=rest)
    dx1, part, dx1_16 = _matmul(
        t + "d_ff1_in", da, p["rest"], "nn", 512, D, D_FF, out_dtypes=(F32, (F32, "rows8"), BF16),
        extras=((s["x1"], "mn"), (dx2, "mn"), (p["norm_mlp"], "n")), epilogue=_norm_bwd_epilogue_twice,
        after=tok, b_rows=ROWS_FF1)
    g["norm_mlp"] = _fold_rows(t + "d_norm_mlp", part)
    rest = _matmul(t + "dw_ff1", da, s["h2"], "tn", 512, 1024, 4096, into=(rest, REST_FF1, 0), after=dx1)
    dmerged = _matmul(t + "d_out_proj_in", dx1_16, p["rest"], "nt", 1024, 1024, 1024, b_rows=ROWS_WO)
    rest = _matmul(t + "dw_o", s["merged"], dx1_16, "tn", 128, 1024, 4096, into=(rest, REST_WO, 0), after=dmerged)
    ys = (s["y_pool"], s["y_conv"], s["y_attn"])
    names = ("w_pool_up_t", "w_conv_out_t", "w_attn_up_t")
    dys, dbs = [], []
    tok = rest
    dz = jax.ShapeDtypeStruct((dx2.shape[0], IN_COLS), BF16)
    for b in range(3):
        dy_b, dw_t, dz, db = _mix_bwd(t + f"merge{b}", s["z"], p["b_gate"], ys[b], p[names[b]], dmerged, b, dz,
                                      after=tok)
        tok = dy_b
        width = ys[b].shape[1]
        if b < 2:
            rest = rest.at[:, REST_UP:REST_ATTN, b * width:(b + 1) * width].set(
                dw_t.reshape(N_DEV, REST_ATTN - REST_UP, width))
        else:
            rest = rest.at[:, REST_ATTN:REST_ROWS, :].set(dw_t.reshape(N_DEV, REST_ROWS - REST_ATTN, D))
        dys.append(dy_b)
        dbs.append(_fold_rows(t + f"d_b_gate{b}", db))
    g["b_gate"] = jnp.concatenate(dbs, axis=1)
    rs_rest = _ReduceScatter(f"rs_rest{l}", collective_id, rest)
    pending_sum = None if pending is None else pending.finish(after=rest, into=(in_t_all, l + 1))
    dz, g["pool_mix"], g["pool_scale"] = _pool_bwd(t + "pool", s["z"], dys[0], p["pool_mix"], p["pool_scale"], dz,
                                                   after=rest if pending is None else pending_sum[0])
    dcb, dcc, dcx, g["conv_w"] = _conv_bwd(t + "conv", s["z"], dys[1], p["conv_w"], after=dz)
    tok = rs_rest.chip_sums(after=dcb)
    dqs, dks, dvs = [], [], []
    gq = gk = None
    for gi, (_, dil) in enumerate(ATTN_GROUPS):
        dq, dk, dv, dgq, dgk = _attn_bwd(t + f"attn{gi}", s["z"], p["q_gain"], p["k_gain"], dys[2], s["y_attn"],
                                         s["lse"], gi, dil, after=tok)
        tok = dq
        dqs.append(dq)
        dks.append(dk)
        dvs.append(dv)
        gq = dgq if gq is None else gq + dgq
        gk = dgk if gk is None else gk + dgk
    g["q_gain"], g["k_gain"] = gq, gk
    rest_sum, _ = rs_rest.finish(after=tok)
    col = OFF_CB
    for piece in [dcb, dcc, dcx] + dqs + dks + dvs:
        dz = lax.dynamic_update_slice(dz, piece.astype(BF16), (0, col))
        col += piece.shape[1]
    in_t = _matmul(t + "dw_in", dz, s["h"], "tn", 256, 1024, 4096, after=rest_sum)
    rs_in = _ReduceScatter(f"rs_in{l}", collective_id + 2, in_t.reshape(N_DEV, IN_ROWS, D))
    dx = _matmul(t + "d_in_proj_in", dz, p["w_in_t"], "nn", 256, D, IN_COLS,
                 out_dtypes=(F32, (F32, "rows8")) + ((BF16,) if l > 0 else ()),
                 extras=((s["x"], "mn"), (dx1, "mn"), (p["norm_mix"], "n")),
                 epilogue=_norm_bwd_epilogue_twice if l > 0 else _norm_bwd_epilogue, after=in_t)
    dx, part = (dx[0], dx[2]) if l > 0 else (dx[0], None), dx[1]
    g["norm_mix"] = _fold_rows(t + "d_norm_mix", part)
    return dx, g, rest_sum, in_t_all if pending is None else pending_sum[0], rs_in


ANY = pl.BlockSpec(memory_space=pl.ANY)


def _mesh_pos():
    return lax.axis_index("x"), lax.axis_index("y"), lax.axis_index("c")


def _other_chips(x, y):
    return [(1 - x, y), (x, 1 - y), (1 - x, 1 - y)]


def _comm_call(name, collective_id, peers, body, arrs, out_shape, sem_counts, after=None):
    n_in, n_out = len(arrs), len(out_shape)
    if collective_id is None:
        def tc_body(*refs):
            body(refs[:n_in], refs[n_in:n_in + n_out], *refs[n_in + n_out:])

        return pl.pallas_call(
            tc_body, name=name, out_shape=tuple(out_shape), in_specs=[ANY] * n_in, out_specs=(ANY,) * n_out,
            scratch_shapes=[pltpu.SemaphoreType.DMA((n,)) for n in sem_counts])(*arrs)

    n_after = 0 if after is None else 1

    def seq_body(*refs):
        barrier = pltpu.get_barrier_semaphore()
        ps = peers(*_mesh_pos())
        for p in ps:
            pl.semaphore_signal(barrier, inc=1, device_id=p, device_id_type=MESH)
        pl.semaphore_wait(barrier, len(ps))
        outs = refs[n_in + n_after:]
        body(refs[:n_in], outs[:n_out], *outs[n_out:])

    return pl.kernel(
        seq_body, out_type=tuple(out_shape), mesh=plsc.ScalarSubcoreMesh(axis_name="seq", num_cores=1), name=name,
        scratch_types=[pltpu.SemaphoreType.DMA((n,)) for n in sem_counts],
        compiler_params=pltpu.CompilerParams(collective_id=collective_id),
    )(*arrs, *([] if after is None else [after]))


def _ordered(name, x, after):
    def body(x_ref, after_ref, o_ref):
        del x_ref, after_ref, o_ref

    return pl.pallas_call(
        body, name=name, out_shape=jax.ShapeDtypeStruct(x.shape, x.dtype), in_specs=[ANY, ANY], out_specs=ANY,
        input_output_aliases={0: 0})(x, after)


def _all_gather(name, collective_id, shard, after=None):
    R, C = shard.shape

    def peers(x, y, c):
        return [(x, y, 1 - c)] + [(*chip, c) for chip in _other_chips(x, y)]

    def body(in_refs, out_refs, send_sems, recv_sems, local_sems):
        (x_ref,), (out_ref,) = in_refs, out_refs
        x, y, c = _mesh_pos()
        me, sibling = (x, y, c), (x, y, 1 - c)
        chips = _other_chips(x, y)

        def slot(px, py, pc):
            return out_ref.at[4 * px + 2 * py + pc]

        def copy(k, block, to, src=None):
            return pltpu.make_async_remote_copy(
                src_ref=slot(*block) if src is None else src, dst_ref=slot(*block),
                send_sem=send_sems.at[k], recv_sem=recv_sems.at[k], device_id=to, device_id_type=MESH)

        mine = pltpu.make_async_copy(x_ref, slot(*me), local_sems.at[0])
        mine.start()
        first = [copy(0, me, sibling, src=x_ref)]
        first += [copy(1 + j, me, (*chip, c), src=x_ref) for j, chip in enumerate(chips)]
        for cp in first:
            cp.start()
        passed = [copy(4 + j, (*chip, c), sibling) for j, chip in enumerate(chips)]
        for j, chip in enumerate(chips):
            copy(1 + j, (*chip, c), me).wait_recv()
            passed[j].start()
        copy(0, sibling, me).wait_recv()
        for j, chip in enumerate(chips):
            copy(4 + j, (*chip, 1 - c), me).wait_recv()
        for cp in first + passed:
            cp.wait_send()
        mine.wait()

    return _comm_call(name, collective_id, peers, body, [shard],
                      [jax.ShapeDtypeStruct((N_DEV, R, C), shard.dtype)], (7, 7, 1), after)[0]


def _rs_sibling_exchange(name, collective_id, arrs):
    n = len(arrs)

    def body(in_refs, out_refs, send_sems, recv_sems):
        x, y, c = _mesh_pos()
        cps = []
        for k, (src, dst) in enumerate(zip(in_refs, out_refs)):
            src = src.at[:, 1 - c] if len(src.shape) == 4 else src
            cps.append(pltpu.make_async_remote_copy(src_ref=src, dst_ref=dst, send_sem=send_sems.at[k],
                                                    recv_sem=recv_sems.at[k], device_id=(x, y, 1 - c),
                                                    device_id_type=MESH))
        for cp in cps:
            cp.start()
        for cp in cps:
            cp.wait()

    out_shape = [jax.ShapeDtypeStruct(a.shape[:1] + a.shape[2:] if a.ndim == 4 else a.shape, a.dtype) for a in arrs]
    return _comm_call(name, collective_id, lambda x, y, c: [(x, y, 1 - c)], body, arrs, out_shape, (n, n))


def _rs_rows(r):
    return r // 2 if (r // 2) % 16 == 0 else r


def _drop_ref(body, idx):
    def wrapped(*refs):
        return body(*refs[:idx], *refs[idx + 1:])

    return wrapped


def _rs_chip_sum(name, ids, big, rbig, after=None):
    _, _, R, C = big.shape
    rows = _rs_rows(R)

    def body(ids_ref, mine_ref, theirs_ref, t16_ref, own_ref):
        p = pl.program_id(1)
        s = mine_ref[...] + theirs_ref[...]
        t16_ref[...] = s.astype(BF16)

        @pl.when(p == ids_ref[1])
        def _():
            own_ref[...] = s

    in_specs = [pl.BlockSpec((None, None, rows, C), lambda i, p, ids: (p, ids[0], i, 0)),
                pl.BlockSpec((None, rows, C), lambda i, p, ids: (p, i, 0))]
    args = [ids, big, rbig]
    if after is not None:
        body = _drop_ref(body, len(args))
        in_specs.append(ANY)
        args.append(after)
    grid_spec = pltpu.PrefetchScalarGridSpec(
        num_scalar_prefetch=1, grid=(R // rows, 4), in_specs=in_specs,
        out_specs=(pl.BlockSpec((None, rows, C), lambda i, p, ids: (p, i, 0)),
                   pl.BlockSpec((rows, C), lambda i, p, ids: (i, 0))))
    return pl.pallas_call(
        body, name=name, grid_spec=grid_spec,
        out_shape=(jax.ShapeDtypeStruct((4, R, C), BF16), jax.ShapeDtypeStruct((R, C), F32)),
        compiler_params=_cparams(("parallel", "arbitrary")),
    )(*args)


def _add2(name, a, b):
    def body(a_ref, b_ref, o_ref):
        o_ref[...] = a_ref[...] + b_ref[...]

    full = pl.BlockSpec(a.shape, lambda i: (0, 0))
    return pl.pallas_call(body, name=name, grid=(1,), in_specs=[full, full], out_specs=full,
                          out_shape=jax.ShapeDtypeStruct(a.shape, a.dtype))(a, b)


def _rs_chip_exchange(name, collective_id, arrs):
    n = len(arrs)

    def body(in_refs, out_refs, send_sems, recv_sems, local_sems):
        x, y, c = _mesh_pos()
        p_me = 2 * x + y
        chips = _other_chips(x, y)

        def part(ref, p):
            return ref.at[p] if len(ref.shape) == 3 else ref

        local = [pltpu.make_async_copy(part(src, p_me), dst.at[p_me], local_sems.at[k])
                 for k, (src, dst) in enumerate(zip(in_refs, out_refs))]
        for cp in local:
            cp.start()
        sends = []
        for j, (px, py) in enumerate(chips):
            for k, (src, dst) in enumerate(zip(in_refs, out_refs)):
                sends.append(pltpu.make_async_remote_copy(
                    src_ref=part(src, 2 * px + py), dst_ref=dst.at[p_me], send_sem=send_sems.at[n * j + k],
                    recv_sem=recv_sems.at[n * j + k], device_id=(px, py, c), device_id_type=MESH))
        for cp in sends:
            cp.start()
        for j, (px, py) in enumerate(chips):
            for k, (src, dst) in enumerate(zip(in_refs, out_refs)):
                pltpu.make_async_remote_copy(
                    src_ref=part(src, p_me), dst_ref=dst.at[2 * px + py], send_sem=send_sems.at[n * j + k],
                    recv_sem=recv_sems.at[n * j + k], device_id=(px, py, c), device_id_type=MESH).wait_recv()
        for cp in sends:
            cp.wait_send()
        for cp in local:
            cp.wait()

    out_shape = [jax.ShapeDtypeStruct((4,) + a.shape[-2:], a.dtype) for a in arrs]
    return _comm_call(name, collective_id, lambda x, y, c: [(*chip, c) for chip in _other_chips(x, y)], body, arrs,
                      out_shape, (3 * n, 3 * n, n))


def _rs_final_sum(name, ids, recv, own, rows, after=None, into=None):
    _, R, C = recv.shape
    assert R % rows == 0

    def body(ids_ref, r_ref, own_ref, o_ref):
        acc = None
        for p in range(4):
            term = jnp.where(ids_ref[1] == p, own_ref[...], r_ref[p].astype(F32))
            acc = term if acc is None else acc + term
        o_ref[...] = acc

    in_specs = [pl.BlockSpec((4, rows, C), lambda i, ids: (0, i, 0)), pl.BlockSpec((rows, C), lambda i, ids: (i, 0))]
    args = [ids, recv, own]
    if after is not None:
        body = _drop_ref(body, len(args))
        in_specs.append(ANY)
        args.append(after)
    out_spec, out_shape, aliases = pl.BlockSpec((rows, C), lambda i, ids: (i, 0)), jax.ShapeDtypeStruct((R, C), F32), {}
    if into is not None:
        buf, slot = into
        out_spec = pl.BlockSpec((None, rows, C), lambda i, ids: (slot, i, 0))
        out_shape = jax.ShapeDtypeStruct(buf.shape, F32)
        if not isinstance(buf, jax.ShapeDtypeStruct):
            body = _drop_ref(body, len(args))
            aliases = {len(args): 0}
            in_specs.append(ANY)
            args.append(buf)
    grid_spec = pltpu.PrefetchScalarGridSpec(
        num_scalar_prefetch=1, grid=(R // rows,), in_specs=in_specs, out_specs=out_spec)
    return pl.pallas_call(
        body, name=name, grid_spec=grid_spec, out_shape=out_shape, input_output_aliases=aliases,
        compiler_params=_cparams(("parallel",)),
    )(*args)


class _ReduceScatter:
    def __init__(self, tag, collective_id, big, small=None):
        x, y, c = _mesh_pos()
        self.tag, self.cid, self.small = tag, collective_id, small
        self.ids = jnp.stack([c, 2 * x + y]).astype(jnp.int32)
        self.big = big.reshape((4, 2) + big.shape[1:])
        self.got = _rs_sibling_exchange(tag + "_sibling", collective_id,
                                        [self.big] + ([] if small is None else [small]))

    def chip_sums(self, after=None):
        t16, self.own = _rs_chip_sum(self.tag + "_chip_sum", self.ids, self.big, self.got[0], after)
        arrs = [t16]
        if self.small is not None:
            self.ts = _add2(self.tag + "_chip_sum_small", self.small, self.got[1])
            arrs.append(self.ts)
        self.recv = _rs_chip_exchange(self.tag + "_chips", self.cid + 1, arrs)
        return t16

    def finish(self, after=None, into=None):
        out = _rs_final_sum(self.tag + "_final", self.ids, self.recv[0], self.own, _rs_rows(self.own.shape[0]), after,
                            into)
        out_small = None
        if self.small is not None:
            out_small = _rs_final_sum(self.tag + "_final_small", self.ids, self.recv[1], self.ts, self.small.shape[0])
        return out, out_small


def _all_reduce_small(tag, small):
    x, y, c = _mesh_pos()
    ids = jnp.stack([c, 2 * x + y]).astype(jnp.int32)
    (theirs,) = _rs_sibling_exchange(tag + "_sibling", None, [small])
    ts = _add2(tag + "_chip_sum", small, theirs)
    (recv,) = _rs_chip_exchange(tag + "_chips", None, [ts])
    return _rs_final_sum(tag + "_final", ids, recv, ts, small.shape[0])


def _adamw(name, w, g, m, v):
    R, C = w.shape
    tiles = [t for t in range(8, 513, 8) if R % t == 0]
    tr = max(tiles) if tiles else R
    c1 = 1.0 - ADAM_B1 ** ADAM_STEP
    c2 = 1.0 - ADAM_B2 ** ADAM_STEP

    def body(w_ref, g_ref, m_ref, v_ref, d_ref, nm_ref, nv_ref):
        gv = g_ref[...]
        nm = ADAM_B1 * m_ref[...] + (1.0 - ADAM_B1) * gv
        nv = ADAM_B2 * v_ref[...] + (1.0 - ADAM_B2) * (gv * gv)
        d_ref[...] = -ADAM_LR * ((nm / c1) / (jnp.sqrt(nv / c2) + ADAM_EPS) + ADAM_WD * w_ref[...])
        nm_ref[...] = nm
        nv_ref[...] = nv

    blk = pl.BlockSpec((tr, C), lambda i: (i, 0))
    shp = jax.ShapeDtypeStruct((R, C), F32)
    return pl.pallas_call(
        body, name=name, grid=(R // tr,), in_specs=[blk] * 4, out_specs=(blk,) * 3, out_shape=(shp,) * 3,
        compiler_params=_cparams(("parallel",)),
    )(w, g, m, v)


def _adamw_t(name, w, g_t, m, v):
    L, R, C = w.shape
    tr = 256
    c1 = 1.0 - ADAM_B1 ** ADAM_STEP
    c2 = 1.0 - ADAM_B2 ** ADAM_STEP

    def body(w_ref, g_ref, m_ref, v_ref, go_ref, d_ref, nm_ref, nv_ref):
        gv = g_ref[...].T
        go_ref[...] = gv
        nm = ADAM_B1 * m_ref[...] + (1.0 - ADAM_B1) * gv
        nv = ADAM_B2 * v_ref[...] + (1.0 - ADAM_B2) * (gv * gv)
        d_ref[...] = -ADAM_LR * ((nm / c1) / (jnp.sqrt(nv / c2) + ADAM_EPS) + ADAM_WD * w_ref[...])
        nm_ref[...] = nm
        nv_ref[...] = nv

    blk = pl.BlockSpec((None, tr, C), lambda l, i: (l, i, 0))
    blk_t = pl.BlockSpec((None, C, tr), lambda l, i: (l, 0, i))
    shp = jax.ShapeDtypeStruct((L, R, C), F32)
    return pl.pallas_call(
        body, name=name, grid=(L, R // tr), in_specs=[blk, blk_t, blk, blk], out_specs=(blk,) * 4,
        out_shape=(shp,) * 4, compiler_params=_cparams(("parallel", "parallel")),
    )(w, g_t, m, v)


REST_FF1 = 0
REST_FF2 = 512
REST_WO = 1024
REST_UP = 1152
REST_ATTN = 1280
REST_ROWS = 1312
IN_ROWS = IN_COLS // N_DEV
ROWS_FF1 = (REST_FF1, D_FF // N_DEV)
ROWS_FF2 = (REST_FF2, D_FF // N_DEV)
ROWS_WO = (REST_WO, D // N_DEV)
SMALL = (("norm_mix", (DEPTH, D)), ("b_gate", (DEPTH, 3 * D)), ("pool_mix", (DEPTH, 4, POOL_G, POOL_G)),
         ("pool_scale", (DEPTH, POOL_W)), ("conv_w", (DEPTH, 3, CONV_W)), ("q_gain", (DEPTH, HEAD)),
         ("k_gain", (DEPTH, HEAD)), ("norm_mlp", (DEPTH, D)))


def _pack_weight_shards(w, l):
    b = lambda a: a.astype(BF16)
    rest = jnp.concatenate([
        b(w["w_ff1"][l].T), b(w["w_ff2"][l]), b(w["w_o"][l]),
        jnp.concatenate([b(w["w_pool_up"][l].T), b(w["w_conv_out"][l].T)], axis=1),
        b(w["w_attn_up"][l].T).reshape(REST_ROWS - REST_ATTN, D)], axis=0)
    return b(w["w_in"][l].T), rest


def _unpack_gathered(g_in, g_rest, g_conv, small_w, l):
    rest = g_rest
    take = lambda r0, rows, c0=0, cols=D: rest[:, r0:r0 + rows, c0:c0 + cols].reshape(N_DEV * rows, cols)
    conv = g_conv[:, 3 * l:3 * l + 3, :CONV_W // N_DEV]
    p = {
        "w_in_t": g_in.reshape(IN_COLS, D),
        "rest": rest,
        "w_pool_up_t": take(REST_UP, 128, 0, POOL_W), "w_conv_out_t": take(REST_UP, 128, POOL_W, CONV_W),
        "w_attn_up_t": rest[:, REST_ATTN:].reshape(D, ATTN_OUT),
        "conv_w": jnp.transpose(conv, (1, 0, 2)).reshape(3, CONV_W),
        "pool_mix": small_w["pool_mix"][l],
    }
    for name in ("norm_mix", "b_gate", "pool_scale", "q_gain", "k_gain", "norm_mlp"):
        p[name] = small_w[name][l][None, :]
    return p


def _pack_small_grads(grads):
    flat = jnp.concatenate([jnp.stack([grads[l][name] for l in range(DEPTH)]).reshape(-1) for name, _ in SMALL])
    return jnp.pad(flat, (0, (-flat.shape[0]) % (8 * 128))).reshape(-1, 128)


def _unpack_grads(rest, small, dev):
    out = {
        "w_ff1_t": jnp.stack([a[REST_FF1:REST_FF1 + 512] for a in rest]),
        "w_ff2": jnp.stack([a[REST_FF2:REST_FF2 + 512] for a in rest]),
        "w_o": jnp.stack([a[REST_WO:REST_WO + 128] for a in rest]),
        "w_pool_up": jnp.stack([a[REST_UP:REST_UP + 128, :POOL_W].T for a in rest]),
        "w_conv_out": jnp.stack([a[REST_UP:REST_UP + 128, POOL_W:].T for a in rest]),
        "w_attn_up": jnp.stack([a[REST_ATTN:].reshape(D // N_DEV, ATTN_OUT).T for a in rest]),
    }
    out.update(_unpack_small_like_grads(small, dev))
    return out


def _unpack_small_like_grads(small, dev):
    out = {}
    flat = small.reshape(-1)
    off = 0
    for name, shp in SMALL:
        n = 1
        for s in shp:
            n *= s
        out[name] = flat[off:off + n].reshape(shp)
        off += n
    width = CONV_W // N_DEV
    out["conv_w"] = lax.dynamic_slice_in_dim(out["conv_w"], dev * width, width, axis=2)
    return out


def _pack_small_like_grads(src, dev):
    parts = []
    for name, shp in SMALL:
        a = src[name]
        if name == "conv_w":
            a = lax.dynamic_update_slice_in_dim(jnp.zeros(shp, F32), a, dev * (CONV_W // N_DEV), axis=2)
        parts.append(a.reshape(-1))
    flat = jnp.concatenate(parts)
    return jnp.pad(flat, (0, (-flat.shape[0]) % (8 * 128))).reshape(-1, 128)


WEIGHTS = ("norm_mix", "w_in", "b_gate", "pool_mix", "pool_scale", "conv_w", "q_gain", "k_gain", "w_pool_up",
           "w_conv_out", "w_attn_up", "w_o", "norm_mlp", "w_ff1", "w_ff2")


def kernel(x, norm_mix, w_in, b_gate, pool_mix, pool_scale, conv_w, q_gain, k_gain, w_pool_up, w_conv_out, w_attn_up, w_o, norm_mlp, w_ff1, w_ff2, loss_target, m_norm_mix, m_w_in, m_b_gate, m_pool_mix, m_pool_scale, m_conv_w, m_q_gain, m_k_gain, m_w_pool_up, m_w_conv_out, m_w_attn_up, m_w_o, m_norm_mlp, m_w_ff1, m_w_ff2, v_norm_mix, v_w_in, v_b_gate, v_pool_mix, v_pool_scale, v_conv_w, v_q_gain, v_k_gain, v_w_pool_up, v_w_conv_out, v_w_attn_up, v_w_o, v_norm_mlp, v_w_ff1, v_w_ff2):
    w = dict(zip(WEIGHTS, (norm_mix, w_in, b_gate, pool_mix, pool_scale, conv_w, q_gain, k_gain, w_pool_up,
                           w_conv_out, w_attn_up, w_o, norm_mlp, w_ff1, w_ff2)))
    m = dict(zip(WEIGHTS, (m_norm_mix, m_w_in, m_b_gate, m_pool_mix, m_pool_scale, m_conv_w, m_q_gain, m_k_gain,
                           m_w_pool_up, m_w_conv_out, m_w_attn_up, m_w_o, m_norm_mlp, m_w_ff1, m_w_ff2)))
    v = dict(zip(WEIGHTS, (v_norm_mix, v_w_in, v_b_gate, v_pool_mix, v_pool_scale, v_conv_w, v_q_gain, v_k_gain,
                           v_w_pool_up, v_w_conv_out, v_w_attn_up, v_w_o, v_norm_mlp, v_w_ff1, v_w_ff2)))
    xi, yi, ci = _mesh_pos()
    dev = 4 * xi + 2 * yi + ci

    saved, params = [], []
    act = x[0]
    w_in_t, m_in_t, v_in_t = (jnp.swapaxes(src["w_in"], 1, 2) for src in (w, m, v))
    h = _rmsnorm_fwd("l0_norm_mix", act, w["norm_mix"][0][None])
    conv_shard = jnp.pad(w["conv_w"].reshape(DEPTH * 3, CONV_W // N_DEV), ((0, 8 - DEPTH * 3), (0, 128 - CONV_W // N_DEV)))
    g_conv = _all_gather("gather_conv", None, conv_shard)
    for l in range(DEPTH):
        s_in, s_rest = _pack_weight_shards(w, l)
        g_in = _all_gather(f"gather_in{l}", 1 + 2 * l, s_in)
        g_rest = _all_gather(f"gather_rest{l}", 2 + 2 * l, s_rest)
        params.append(_unpack_gathered(g_in, g_rest, g_conv, w, l))
    for l in range(DEPTH):
        act, h, s = _layer_fwd(l, act, h, params[l], w["norm_mix"][l + 1][None] if l + 1 < DEPTH else None,
                               loss_target[0])
        saved.append(s)
    dy = act
    loss = _fold_rows("loss_sum", h, lanes=True, scale=0.5 / D)
    total = lax.psum(loss[0, 0], ("x", "y", "c"))
    rest, grads, pending = [None] * DEPTH, [None] * DEPTH, None
    in_t = jax.ShapeDtypeStruct((DEPTH, IN_ROWS, D), F32)
    start = total.reshape(1, 1)
    for l in reversed(range(DEPTH)):
        dy, grads[l], rest[l], in_t, pending = _layer_bwd(l, dy, params[l], saved[l], pending, in_t, 5 + 4 * l, start)
        start = None
    dx = dy[0]
    g_small = _all_reduce_small("ar_small", _pack_small_grads(grads))
    pending.chip_sums(after=g_small)
    g = _unpack_grads(rest, g_small, dev)

    delta, new_m, new_v = {}, {}, {}
    g["w_ff1"], delta["w_ff1"], new_m["w_ff1"], new_v["w_ff1"] = _adamw_t("adamw_w_ff1", w["w_ff1"], g["w_ff1_t"],
                                                                          m["w_ff1"], v["w_ff1"])
    for name in ("w_pool_up", "w_conv_out", "w_attn_up", "w_o", "w_ff2"):
        shp = w[name].shape
        two_d = (shp[0] * shp[1], shp[2])
        d_, m_, v_ = _adamw("adamw_" + name, w[name].reshape(two_d), g[name].reshape(two_d),
                            m[name].reshape(two_d), v[name].reshape(two_d))
        delta[name], new_m[name], new_v[name] = d_.reshape(shp), m_.reshape(shp), v_.reshape(shp)
    for name, shp in SMALL:
        shp = w[name].shape
        two_d = (-1, shp[-1])
        d_, m_, v_ = _adamw("adamw_" + name, w[name].reshape(two_d), g[name].reshape(two_d),
                            m[name].reshape(two_d), v[name].reshape(two_d))
        delta[name], new_m[name], new_v[name] = d_.reshape(shp), m_.reshape(shp), v_.reshape(shp)
    g_t, _ = pending.finish(after=new_v[SMALL[-1][0]], into=(in_t, 0))
    two_d = (g_t.shape[0] * g_t.shape[1], g_t.shape[2])
    d_, m_, v_ = _adamw("adamw_w_in", *[a.reshape(two_d) for a in (w_in_t, g_t, m_in_t, v_in_t)])
    g["w_in"] = jnp.swapaxes(g_t, 1, 2)
    delta["w_in"], new_m["w_in"], new_v["w_in"] = (jnp.swapaxes(a.reshape(g_t.shape), 1, 2) for a in (d_, m_, v_))

    return (total, dx[None], *[g[n] for n in WEIGHTS], *[delta[n] for n in WEIGHTS],
            *[new_m[n] for n in WEIGHTS], *[new_v[n] for n in WEIGHTS])
```

```python
import functools

import jax
import jax.numpy as jnp
from jax import lax
from jax.experimental import pallas as pl
from jax.experimental.pallas import tpu as pltpu
from jax.experimental.pallas import tpu_sc as plsc

F32 = jnp.float32
BF16 = jnp.bfloat16
MESH = pl.DeviceIdType.MESH

D = 1024
SEQ = 4096
DEPTH = 2
N_DEV = 8
POOL_WINDOWS = (2, 4, 8, 16)
POOL_W = 512
POOL_G = 128
CONV_W = 512
HEAD = 64
ATTN_GROUPS = ((128, 1), (512, 4), (2048, 16))
HPG = 4
ATTN_W = 768
ATTN_OUT = 256
ATTN_BLK = 128
D_FF = 4096
EPS = 1e-6
MASK = -1e30
OFF_POOL = 0
OFF_CB = 512
OFF_CC = 1024
OFF_CX = 1536
OFF_Q = 2048
OFF_K = 2816
OFF_V = 3584
OFF_GATE = 4352
IN_COLS = 7424
ADAM_LR = 0.001
ADAM_B1 = 0.9
ADAM_B2 = 0.999
ADAM_EPS = 1e-08
ADAM_WD = 0.01
ADAM_STEP = 10

ROW_TILE = 512
SEQ_CHUNK = 256
HALO = 16
VMEM_LIMIT = 56 * 1024 * 1024


def _cparams(sem=None):
    return pltpu.CompilerParams(dimension_semantics=sem, vmem_limit_bytes=VMEM_LIMIT)


def _call(body, args, in_specs, after=None, **kw):
    if after is not None:
        after = list(after) if isinstance(after, (list, tuple)) else [after]
        n, k, inner = len(args), len(after), body

        def body(*refs):
            return inner(*refs[:n], *refs[n + k:])

        args = list(args) + after
        in_specs = list(in_specs) + [pl.BlockSpec(memory_space=pl.ANY)] * k
    return pl.pallas_call(body, in_specs=in_specs, **kw)(*args)


_DIMS = {"nn": (((1,), (0,)), ((), ())), "nt": (((1,), (1,)), ((), ())), "tn": (((0,), (0,)), ((), ()))}


def _matmul(name, a, b, mode, tm, tn, tk, out_dtypes=(F32,), extras=(), epilogue=None, into=None, after=None,
            b_rows=None):
    if mode == "tn":
        K, M = a.shape
    else:
        M, K = a.shape
    b_shape = b.shape if b_rows is None else (b.shape[0] * b_rows[1], b.shape[2])
    N = b_shape[0] if mode == "nt" else b_shape[1]
    assert b_shape[1 if mode == "nt" else 0] == K, (name, b_shape, K)
    assert M % tm == 0 and N % tn == 0 and K % tk == 0, (name, M, N, K, tm, tn, tk)
    nk = K // tk
    n_extra = len(extras)
    n_out = len(out_dtypes)
    dims = _DIMS[mode]
    n_alias = 0 if into is None or isinstance(into[0], jax.ShapeDtypeStruct) else 1

    def body(a_ref, b_ref, *rest):
        extra_refs = rest[:n_extra]
        out_refs = rest[n_extra + n_alias:n_extra + n_alias + n_out]

        def finish(acc):
            if epilogue is None:
                res = (acc,)
            else:
                res = epilogue(acc, *[r[...] for r in extra_refs])
            for o_ref, v in zip(out_refs, res):
                o_ref[...] = v.astype(o_ref.dtype)

        bv = b_ref[...]
        if b_rows is not None:
            bv = bv.reshape(-1, bv.shape[-1])
        part = lax.dot_general(a_ref[...].astype(BF16), bv.astype(BF16), dims, preferred_element_type=F32)
        if nk == 1:
            finish(part)
        else:
            acc_ref = rest[-1]
            k = pl.program_id(2)

            @pl.when(k == 0)
            def _():
                acc_ref[...] = part

            @pl.when(k > 0)
            def _():
                acc_ref[...] += part

            @pl.when(k == nk - 1)
            def _():
                finish(acc_ref[...])

    if mode == "tn":
        a_spec = pl.BlockSpec((tk, tm), lambda i, j, k: (k, i))
    else:
        a_spec = pl.BlockSpec((tm, tk), lambda i, j, k: (i, k))
    if b_rows is not None:
        row0, rows = b_rows
        t_rows = tn if mode == "nt" else tk
        assert t_rows % rows == 0 and row0 % rows == 0, (name, t_rows, b_rows)
        if mode == "nt":
            b_spec = pl.BlockSpec((t_rows // rows, rows, tk), lambda i, j, k: (j, row0 // rows, k))
        else:
            b_spec = pl.BlockSpec((t_rows // rows, rows, tn), lambda i, j, k: (k, row0 // rows, j))
    elif mode == "nt":
        b_spec = pl.BlockSpec((tn, tk), lambda i, j, k: (j, k))
    else:
        b_spec = pl.BlockSpec((tk, tn), lambda i, j, k: (k, j))
    in_specs = [a_spec, b_spec]
    args = [a, b]
    for arr, kind in extras:
        if kind == "mn":
            in_specs.append(pl.BlockSpec((tm, tn), lambda i, j, k: (i, j)))
        else:
            in_specs.append(pl.BlockSpec((1, tn), lambda i, j, k: (0, j)))
        args.append(arr)
    part = [isinstance(dt, tuple) for dt in out_dtypes]
    out_shape = tuple(jax.ShapeDtypeStruct((M // tm * 8, N), dt[0]) if p else jax.ShapeDtypeStruct((M, N), dt)
                      for dt, p in zip(out_dtypes, part))
    out_specs = tuple(pl.BlockSpec((8 if p else tm, tn), lambda i, j, k: (i, j)) for p in part)
    aliases = {}
    if into is not None:
        buf, row0, col0 = into
        assert n_out == 1 and (M // N_DEV) % tm == 0 and row0 % tm == 0 and col0 % tn == 0
        per_dev = M // N_DEV // tm
        out_shape = (jax.ShapeDtypeStruct(buf.shape, buf.dtype),)
        out_specs = (pl.BlockSpec((None, tm, tn), lambda i, j, k: (i // per_dev, row0 // tm + i % per_dev,
                                                                   col0 // tn + j)),)
        if not isinstance(buf, jax.ShapeDtypeStruct):
            aliases = {len(args): 0}
            in_specs.append(pl.BlockSpec(memory_space=pl.ANY))
            args.append(buf)
    scratch = [] if nk == 1 else [pltpu.VMEM((tm, tn), F32)]
    res = _call(
        body, args, in_specs, after, name=name, grid=(M // tm, N // tn, nk), out_specs=out_specs,
        out_shape=out_shape, scratch_shapes=scratch, input_output_aliases=aliases,
        compiler_params=_cparams(("parallel", "parallel", "arbitrary")))
    return res if n_out > 1 else res[0]


def _rmsnorm_fwd(name, x, gain, after=None):
    S_, D_ = x.shape

    def body(x_ref, g_ref, h_ref):
        xf = x_ref[...]
        ms = jnp.mean(xf * xf, axis=-1, keepdims=True)
        h_ref[...] = (xf * lax.rsqrt(ms + EPS) * g_ref[...]).astype(BF16)

    return _call(
        body, [x, gain], [pl.BlockSpec((ROW_TILE, D_), lambda i: (i, 0)), pl.BlockSpec((1, D_), lambda i: (0, 0))],
        after, name=name, grid=(S_ // ROW_TILE,),
        out_specs=pl.BlockSpec((ROW_TILE, D_), lambda i: (i, 0)),
        out_shape=jax.ShapeDtypeStruct((S_, D_), BF16),
        compiler_params=_cparams(("parallel",)))


def _rmsnorm_bwd(name, x, gain, dh, dres, after=None):
    S_, D_ = x.shape
    n = S_ // ROW_TILE

    def body(x_ref, g_ref, dh_ref, dres_ref, dx_ref, dg_ref, acc_ref):
        i = pl.program_id(0)
        xf = x_ref[...]
        rstd = lax.rsqrt(jnp.mean(xf * xf, axis=-1, keepdims=True) + EPS)
        xhat = xf * rstd
        dhv = dh_ref[...]
        dxhat = dhv * g_ref[...]
        c = jnp.mean(dxhat * xhat, axis=-1, keepdims=True)
        dx_ref[...] = dres_ref[...] + rstd * (dxhat - xhat * c)
        part = jnp.sum((dhv * xhat).reshape(ROW_TILE // 8, 8, D_), axis=0)

        @pl.when(i == 0)
        def _():
            acc_ref[...] = part

        @pl.when(i > 0)
        def _():
            acc_ref[...] += part

        @pl.when(i == n - 1)
        def _():
            dg_ref[...] = jnp.sum(acc_ref[...], axis=0, keepdims=True)

    row = pl.BlockSpec((ROW_TILE, D_), lambda i: (i, 0))
    vec = pl.BlockSpec((1, D_), lambda i: (0, 0))
    return _call(
        body, [x, gain, dh, dres], [row, vec, row, row], after, name=name, grid=(n,), out_specs=(row, vec),
        out_shape=(jax.ShapeDtypeStruct((S_, D_), F32), jax.ShapeDtypeStruct((1, D_), F32)),
        scratch_shapes=[pltpu.VMEM((8, D_), F32)],
        compiler_params=_cparams(("arbitrary",)))


def _loss_head(name, y, target):
    S_, D_ = y.shape
    n = S_ // ROW_TILE

    def body(y_ref, t_ref, dy_ref, l_ref, acc_ref):
        i = pl.program_id(0)
        e = y_ref[...] - t_ref[...]
        dy_ref[...] = e * (1.0 / D_)
        part = jnp.sum((e * e).reshape(ROW_TILE // 8, 8, D_), axis=0)

        @pl.when(i == 0)
        def _():
            acc_ref[...] = part

        @pl.when(i > 0)
        def _():
            acc_ref[...] += part

        @pl.when(i == n - 1)
        def _():
            s = jnp.sum(acc_ref[...], axis=1, keepdims=True)
            l_ref[...] = jnp.sum(s, axis=0, keepdims=True) * (0.5 / D_)

    row = pl.BlockSpec((ROW_TILE, D_), lambda i: (i, 0))
    return pl.pallas_call(
        body, name=name, grid=(n,), in_specs=[row, row],
        out_specs=(row, pl.BlockSpec((1, 1), lambda i: (0, 0))),
        out_shape=(jax.ShapeDtypeStruct((S_, D_), F32), jax.ShapeDtypeStruct((1, 1), F32)),
        scratch_shapes=[pltpu.VMEM((8, D_), F32)],
        compiler_params=_cparams(("arbitrary",)),
    )(y, target)


def _rows_with_halo(ref, cols, i, n_chunks, before, after):
    r0 = pl.multiple_of(i * SEQ_CHUNK, SEQ_CHUNK)
    parts = []
    if before:
        h0 = pl.multiple_of(jnp.maximum(r0 - HALO, 0), 8)
        halo = ref[pl.ds(h0, HALO), cols]
        parts.append(jnp.where(i > 0, halo, jnp.zeros_like(halo)))
    parts.append(ref[pl.ds(r0, SEQ_CHUNK), cols])
    if after:
        a0 = pl.multiple_of(jnp.minimum(r0 + SEQ_CHUNK, (n_chunks - 1) * SEQ_CHUNK + SEQ_CHUNK - HALO), 8)
        halo = ref[pl.ds(a0, HALO), cols]
        parts.append(jnp.where(i < n_chunks - 1, halo, jnp.zeros_like(halo)))
    return parts[0] if len(parts) == 1 else jnp.concatenate(parts, axis=0)


def _shift_down(v, k):
    return pltpu.roll(v, k, 0)


def _shift_up(v, k):
    return pltpu.roll(v, v.shape[0] - k, 0)


def _pool_diff(xx, w, t_main):
    s = xx
    k = 1
    while k < w:
        s = s + _shift_down(s, k)
        k *= 2
    cnt = jnp.minimum(t_main + 1, w).astype(F32)
    return s[HALO:] / cnt - xx[HALO:]


def _pool_fwd(name, z, pool_mix, pool_scale):
    S_ = z.shape[0]
    n_chunks = S_ // SEQ_CHUNK

    cols = slice(0, POOL_G)

    def body(u_ref, mix_ref, sc_ref, y_ref):
        mixg = mix_ref[...].astype(BF16)
        scg = sc_ref[...]
        for g, w in enumerate(POOL_WINDOWS):
            @pl.when(pl.program_id(0) == g)
            def _(w=w):
                def chunk(i, carry):
                    r0 = pl.multiple_of(i * SEQ_CHUNK, SEQ_CHUNK)
                    xx = _rows_with_halo(u_ref, cols, i, n_chunks, True, False)
                    t = r0 + lax.broadcasted_iota(jnp.int32, (SEQ_CHUNK, POOL_G), 0)
                    d = _pool_diff(xx, w, t)
                    y = jnp.dot(d.astype(BF16), mixg, preferred_element_type=F32) * scg
                    y_ref[pl.ds(r0, SEQ_CHUNK), :] = y.astype(BF16)
                    return carry

                lax.fori_loop(0, n_chunks, chunk, 0)

    slab = pl.BlockSpec((S_, POOL_G), lambda g: (0, g))
    return pl.pallas_call(
        body, name=name, grid=(len(POOL_WINDOWS),),
        in_specs=[slab, pl.BlockSpec((None, POOL_G, POOL_G), lambda g: (g, 0, 0)),
                  pl.BlockSpec((1, POOL_G), lambda g: (0, g))],
        out_specs=slab, out_shape=jax.ShapeDtypeStruct((S_, POOL_W), BF16),
        compiler_params=_cparams(("parallel",)),
    )(z, pool_mix, pool_scale)


def _pool_bwd(name, z, dy, pool_mix, pool_scale, dz, after=None):
    S_ = z.shape[0]
    n_chunks = S_ // SEQ_CHUNK
    rows_a = SEQ_CHUNK + HALO

    cols = slice(0, POOL_G)

    def body(u_ref, dy_ref, mix_ref, sc_ref, du_ref, dmix_ref, dsc_ref):
        mixg = mix_ref[...].astype(BF16)
        scg = sc_ref[...]
        for g, w in enumerate(POOL_WINDOWS):
            @pl.when(pl.program_id(0) == g)
            def _(w=w):
                def chunk(i, carry):
                    dmix_acc, dsc_acc = carry
                    r0 = pl.multiple_of(i * SEQ_CHUNK, SEQ_CHUNK)
                    xx = _rows_with_halo(u_ref, cols, i, n_chunks, True, False)
                    t = r0 + lax.broadcasted_iota(jnp.int32, (SEQ_CHUNK, POOL_G), 0)
                    d = _pool_diff(xx, w, t).astype(BF16)
                    ypre = jnp.dot(d, mixg, preferred_element_type=F32)
                    dyy = _rows_with_halo(dy_ref, cols, i, n_chunks, False, True)
                    dys = (dyy * scg).astype(BF16)
                    dsc_acc = dsc_acc + jnp.sum((dyy[:SEQ_CHUNK] * ypre).reshape(SEQ_CHUNK // 8, 8, POOL_G), axis=0)
                    dmix_acc = dmix_acc + lax.dot_general(d, dys[:SEQ_CHUNK], _DIMS["tn"],
                                                          preferred_element_type=F32)
                    dd = lax.dot_general(dys, mixg, _DIMS["nt"], preferred_element_type=F32)
                    ta = r0 + lax.broadcasted_iota(jnp.int32, (rows_a, POOL_G), 0)
                    f = dd / jnp.minimum(ta + 1, w).astype(F32)
                    k = 1
                    while k < w:
                        f = f + _shift_up(f, k)
                        k *= 2
                    du_ref[pl.ds(r0, SEQ_CHUNK), :] = (f[:SEQ_CHUNK] - dd[:SEQ_CHUNK]).astype(BF16)
                    return dmix_acc, dsc_acc

                dmix_acc, dsc_acc = lax.fori_loop(
                    0, n_chunks, chunk, (jnp.zeros((POOL_G, POOL_G), F32), jnp.zeros((8, POOL_G), F32)))
                dmix_ref[...] = dmix_acc
                dsc_ref[...] = jnp.sum(dsc_acc, axis=0, keepdims=True)

    slab = pl.BlockSpec((S_, POOL_G), lambda g: (0, g))
    mix_spec = pl.BlockSpec((None, POOL_G, POOL_G), lambda g: (g, 0, 0))
    vec = pl.BlockSpec((1, POOL_G), lambda g: (0, g))
    return _call(
        _drop_ref(body, 4), [z, dy, pool_mix, pool_scale, dz], [slab, slab, mix_spec, vec, ANY], after, name=name,
        grid=(len(POOL_WINDOWS),), out_specs=(slab, mix_spec, vec),
        out_shape=(jax.ShapeDtypeStruct(dz.shape, dz.dtype), jax.ShapeDtypeStruct((4, POOL_G, POOL_G), F32),
                   jax.ShapeDtypeStruct((1, POOL_W), F32)),
        input_output_aliases={4: 0}, compiler_params=_cparams(("parallel",)))


def _conv_specs(S_):
    slab = lambda off: pl.BlockSpec((S_, 128), lambda c, off=off: (0, off // 128 + c))
    return slab(OFF_CB), slab(OFF_CC), slab(OFF_CX)


def _conv_fwd(name, z, conv_w, after=None):
    S_ = z.shape[0]
    n_chunks = S_ // SEQ_CHUNK
    col = slice(0, 128)

    def body(b_ref, c_ref, x_ref, w_ref, y_ref):
        w0, w1, w2 = w_ref[0:1, :], w_ref[1:2, :], w_ref[2:3, :]

        def chunk(i, carry):
            r0 = pl.multiple_of(i * SEQ_CHUNK, SEQ_CHUNK)
            u = _rows_with_halo(c_ref, col, i, n_chunks, True, False) * _rows_with_halo(x_ref, col, i, n_chunks, True, False)
            y = w2 * u + w1 * _shift_down(u, 1) + w0 * _shift_down(u, 2)
            y_ref[pl.ds(r0, SEQ_CHUNK), :] = (b_ref[pl.ds(r0, SEQ_CHUNK), :] * y[HALO:]).astype(BF16)
            return carry

        lax.fori_loop(0, n_chunks, chunk, 0)

    sb, sc, sx = _conv_specs(S_)
    return _call(
        body, [z, z, z, conv_w], [sb, sc, sx, pl.BlockSpec((3, 128), lambda c: (0, c))], after,
        name=name, grid=(CONV_W // 128,),
        out_specs=pl.BlockSpec((S_, 128), lambda c: (0, c)),
        out_shape=jax.ShapeDtypeStruct((S_, CONV_W), BF16),
        compiler_params=_cparams(("parallel",)))


def _conv_bwd(name, z, dy, conv_w, after=None):
    S_ = z.shape[0]
    n_chunks = S_ // SEQ_CHUNK
    col = slice(0, 128)
    lo, hi = HALO, HALO + SEQ_CHUNK

    def body(b_ref, c_ref, x_ref, dy_ref, w_ref, db_ref, dc_ref, dx_ref, dw_ref):
        w0, w1, w2 = w_ref[0:1, :], w_ref[1:2, :], w_ref[2:3, :]

        def chunk(i, carry):
            a0, a1, a2 = carry
            r0 = pl.multiple_of(i * SEQ_CHUNK, SEQ_CHUNK)
            cc = _rows_with_halo(c_ref, col, i, n_chunks, True, True)
            xx = _rows_with_halo(x_ref, col, i, n_chunks, True, True)
            bb = _rows_with_halo(b_ref, col, i, n_chunks, True, True)
            dyy = _rows_with_halo(dy_ref, col, i, n_chunks, True, True)
            u = cc * xx
            u1 = _shift_down(u, 1)
            u2 = _shift_down(u, 2)
            y = w2 * u + w1 * u1 + w0 * u2
            dyv = dyy * bb
            du = w2 * dyv + w1 * _shift_up(dyv, 1) + w0 * _shift_up(dyv, 2)
            db_ref[pl.ds(r0, SEQ_CHUNK), :] = (dyy[lo:hi] * y[lo:hi]).astype(BF16)
            dc_ref[pl.ds(r0, SEQ_CHUNK), :] = (du[lo:hi] * xx[lo:hi]).astype(BF16)
            dx_ref[pl.ds(r0, SEQ_CHUNK), :] = (du[lo:hi] * cc[lo:hi]).astype(BF16)
            red = lambda v: jnp.sum(v.reshape(SEQ_CHUNK // 8, 8, 128), axis=0)
            dm = dyv[lo:hi]
            return a0 + red(dm * u2[lo:hi]), a1 + red(dm * u1[lo:hi]), a2 + red(dm * u[lo:hi])

        zero = jnp.zeros((8, 128), F32)
        a0, a1, a2 = lax.fori_loop(0, n_chunks, chunk, (zero, zero, zero))
        dw_ref[0:1, :] = jnp.sum(a0, axis=0, keepdims=True)
        dw_ref[1:2, :] = jnp.sum(a1, axis=0, keepdims=True)
        dw_ref[2:3, :] = jnp.sum(a2, axis=0, keepdims=True)

    sb, sc, sx = _conv_specs(S_)
    slab = pl.BlockSpec((S_, 128), lambda c: (0, c))
    wspec = pl.BlockSpec((3, 128), lambda c: (0, c))
    act = jax.ShapeDtypeStruct((S_, CONV_W), BF16)
    return _call(
        body, [z, z, z, dy, conv_w], [sb, sc, sx, slab, wspec], after, name=name, grid=(CONV_W // 128,),
        out_specs=(slab, slab, slab, wspec),
        out_shape=(act, act, act, jax.ShapeDtypeStruct((3, CONV_W), F32)),
        compiler_params=_cparams(("parallel",)))


def _head_ones(pw):
    a = lax.broadcasted_iota(jnp.int32, (pw, pw), 0) // HEAD
    b = lax.broadcasted_iota(jnp.int32, (pw, pw), 1) // HEAD
    return (a == b).astype(BF16)


def _head_sum(v, ones):
    hi = v.astype(BF16)
    lo = (v - hi.astype(F32)).astype(BF16)
    return jnp.dot(hi, ones, preferred_element_type=F32) + jnp.dot(lo, ones, preferred_element_type=F32)


def _head_norm(v, gain, ones):
    rstd = lax.rsqrt(_head_sum(v * v, ones) * (1.0 / HEAD) + EPS)
    xhat = v * rstd
    return xhat * gain, xhat, rstd


def _head_norm_bwd(dy, xhat, rstd, gain, ones):
    dxhat = dy * gain
    c = _head_sum(dxhat * xhat, ones) * (1.0 / HEAD)
    dv = rstd * (dxhat - xhat * c)
    dg = jnp.sum((dy * xhat).reshape(dy.shape[0] // 8, 8, dy.shape[1]), axis=0)
    return dv, dg


def _head_masks(pw):
    lane_head = lax.broadcasted_iota(jnp.int32, (1, pw), 1) // HEAD
    return [lane_head == h for h in range(pw // HEAD)]


def _only(mask, v):
    return jnp.where(mask, v, jnp.zeros_like(v))


def _attn_specs(S_, g, dil):
    rows = ATTN_BLK * dil
    nb = S_ // rows
    pw = 128 if dil > 1 else ATTN_OUT
    cq, ck, cv = ((OFF_Q + g * ATTN_OUT) // pw, (OFF_K + g * ATTN_OUT) // pw, (OFF_V + g * ATTN_OUT) // pw)
    return rows, nb, pw, pw // HEAD, ATTN_OUT // pw, cq, ck, cv


ATTN_BATCH = 4


def _attn_group(dil):
    return 4 if dil == 1 else 1


def _attn_block_specs(rows, grp, pw, last=None):
    step = (lambda n: n) if last is None else (lambda n: jnp.minimum(n, last))
    cur = lambda c: pl.BlockSpec((rows * grp, pw), lambda hp, n, c=c: (step(n), c + hp))
    prev = lambda c: pl.BlockSpec((rows, pw), lambda hp, n, c=c: (jnp.maximum(step(n) * grp - 1, 0), c + hp))
    return cur, prev


def _band_mask(has_prev):
    qi = lax.broadcasted_iota(jnp.int32, (ATTN_BLK, 2 * ATTN_BLK), 0)
    ki = lax.broadcasted_iota(jnp.int32, (ATTN_BLK, 2 * ATTN_BLK), 1)
    in_prev = jnp.logical_and(ki < ATTN_BLK, ki >= qi)
    if has_prev is not True:
        in_prev = jnp.logical_and(in_prev, has_prev)
    return jnp.logical_or(in_prev, jnp.logical_and(ki >= ATTN_BLK, ki - ATTN_BLK <= qi))


def _rows_of(ref, r, dil):
    if dil == 1:
        return ref[r * ATTN_BLK:(r + 1) * ATTN_BLK, :]
    return ref[pl.ds(r, ATTN_BLK, stride=dil), :]


def _put_rows(ref, r, dil, val):
    if dil == 1:
        ref[r * ATTN_BLK:(r + 1) * ATTN_BLK, :] = val.astype(ref.dtype)
    else:
        ref[pl.ds(r, ATTN_BLK, stride=dil), :] = val.astype(ref.dtype)


def _attn_fwd(name, z, q_gain, k_gain, g, dil):
    S_ = z.shape[0]
    rows, nb, pw, heads, npairs, cq, ck, cv = _attn_specs(S_, g, dil)
    scale = HEAD ** -0.5

    grp = _attn_group(dil)
    nsteps = nb // grp

    def body(q_ref, k_ref, kp_ref, v_ref, vp_ref, gq_ref, gk_ref, o_ref, l_ref):
        n = pl.program_id(1)
        ones, hmask = _head_ones(pw), _head_masks(pw)
        gq, gk = jnp.tile(gq_ref[...], (1, heads)), jnp.tile(gk_ref[...], (1, heads))
        mask_first, mask_rest = _band_mask(n > 0), _band_mask(True)
        for r0 in range(0, dil * grp, ATTN_BATCH):
            rs = range(r0, min(r0 + ATTN_BATCH, dil * grp))
            qn, kn, vv, s, p = {}, {}, {}, {}, {}
            kcn = {}
            for r in rs:
                q, kc, vc = _rows_of(q_ref, r, dil), _rows_of(k_ref, r, dil), _rows_of(v_ref, r, dil)
                kcn[r] = _head_norm(kc, gk, ones)[0]
                if dil == 1 and r > 0:
                    kpn = kcn[r - 1] if r - 1 in kcn else _head_norm(_rows_of(k_ref, r - 1, dil), gk, ones)[0]
                    vp = _rows_of(v_ref, r - 1, dil)
                else:
                    kpn, vp = _head_norm(_rows_of(kp_ref, r, dil), gk, ones)[0], _rows_of(vp_ref, r, dil)
                qn[r] = _head_norm(q, gq, ones)[0].astype(BF16)
                kn[r] = jnp.concatenate([kpn, kcn[r]], axis=0).astype(BF16)
                vv[r] = jnp.concatenate([vp, vc], axis=0).astype(BF16)
            keys = [(r, h) for r in rs for h in range(heads)]
            for r, h in keys:
                s[r, h] = lax.dot_general(_only(hmask[h], qn[r]), kn[r], _DIMS["nt"],
                                          preferred_element_type=F32) * scale
            lse, den = {}, {}
            for key in keys:
                mask = mask_rest if (dil == 1 and key[0] > 0) else mask_first
                sm = jnp.where(mask, s[key], MASK)
                m = jnp.max(sm, axis=-1, keepdims=True)
                e = jnp.exp(sm - m)
                den[key] = jnp.sum(e, axis=-1, keepdims=True)
                p[key] = e.astype(BF16)
                lse[key] = m + jnp.log(den[key])
            for r in rs:
                out = jnp.zeros((ATTN_BLK, pw), F32)
                lse_all = jnp.zeros((ATTN_BLK, pw), F32)
                for h in range(heads):
                    out = jnp.where(hmask[h], jnp.dot(p[r, h], vv[r], preferred_element_type=F32) / den[r, h], out)
                    lse_all = jnp.where(hmask[h], lse[r, h], lse_all)
                _put_rows(o_ref, r, dil, out)
                _put_rows(l_ref, r, dil, lse_all)

    cur, prev = _attn_block_specs(rows, grp, pw)
    gspec = pl.BlockSpec((1, HEAD), lambda hp, n: (0, 0))
    shp = jax.ShapeDtypeStruct((S_, ATTN_OUT), F32)
    return pl.pallas_call(
        body, name=name, grid=(npairs, nsteps),
        in_specs=[cur(cq), cur(ck), prev(ck), cur(cv), prev(cv), gspec, gspec],
        out_specs=(cur(0), cur(0)), out_shape=(shp, shp),
        compiler_params=_cparams(("parallel", "parallel")),
    )(z, z, z, z, z, q_gain, k_gain)


def _attn_combine(name, os_, ls_):
    S_ = os_[0].shape[0]

    def body(o0, o1, o2, l0, l1, l2, o_ref, l_ref):
        a, b, c = l0[...], l1[...], l2[...]
        m = jnp.maximum(jnp.maximum(a, b), c)
        ea, eb, ec = jnp.exp(a - m), jnp.exp(b - m), jnp.exp(c - m)
        zsum = ea + eb + ec
        o_ref[...] = (ea * o0[...] + eb * o1[...] + ec * o2[...]) / zsum
        l_ref[...] = m + jnp.log(zsum)

    row = pl.BlockSpec((ROW_TILE, ATTN_OUT), lambda i: (i, 0))
    shp = jax.ShapeDtypeStruct((S_, ATTN_OUT), F32)
    return pl.pallas_call(
        body, name=name, grid=(S_ // ROW_TILE,), in_specs=[row] * 6, out_specs=(row, row), out_shape=(shp, shp),
        compiler_params=_cparams(("parallel",)),
    )(*os_, *ls_)


def _attn_bwd(name, z, q_gain, k_gain, do, o, lse, g, dil, after=None):
    S_ = z.shape[0]
    rows, nb, pw, heads, npairs, cq, ck, cv = _attn_specs(S_, g, dil)
    scale = HEAD ** -0.5

    def body(q_ref, kc_ref, kp_ref, vc_ref, vp_ref, gq_ref, gk_ref, do_ref, o_ref, l_ref,
             dq_ref, dk_ref, dv_ref, dgq_ref, dgk_ref, ck_ref, cvv_ref, gq_acc, gk_acc):
        hp = pl.program_id(0)
        n = pl.program_id(1)
        ones = _head_ones(pw)
        gq, gk = jnp.tile(gq_ref[...], (1, heads)), jnp.tile(gk_ref[...], (1, heads))

        @pl.when(n == 0)
        def _():
            ck_ref[...] = jnp.zeros_like(ck_ref)
            cvv_ref[...] = jnp.zeros_like(cvv_ref)

        @pl.when(jnp.logical_and(n == 0, hp == 0))
        def _():
            gq_acc[...] = jnp.zeros_like(gq_acc)
            gk_acc[...] = jnp.zeros_like(gk_acc)

        pl.when(n < nb)(functools.partial(query_step, n, ones, gq, gk, q_ref, kc_ref, kp_ref, vc_ref, vp_ref, do_ref,
                                          o_ref, l_ref, dq_ref, dk_ref, dv_ref, ck_ref, cvv_ref, gq_acc, gk_acc))

        @pl.when(n == nb)
        def _():
            dgk = jnp.zeros((8, pw), F32)
            for r in range(dil):
                _, kphat, kprstd = _head_norm(_rows_of(kp_ref, r, dil), gk, ones)
                dk_all, dg = _head_norm_bwd(_rows_of(ck_ref, r, dil), kphat, kprstd, gk, ones)
                dgk = dgk + dg
                _put_rows(dk_ref, r, dil, dk_all)
                _put_rows(dv_ref, r, dil, _rows_of(cvv_ref, r, dil))
            gk_acc[...] += dgk

        @pl.when(jnp.logical_and(n == nb, hp == npairs - 1))
        def _():
            fold = lambda a: sum(a[:, h * HEAD:(h + 1) * HEAD] for h in range(heads))
            dgq_ref[...] = jnp.sum(fold(gq_acc[...]), axis=0, keepdims=True)
            dgk_ref[...] = jnp.sum(fold(gk_acc[...]), axis=0, keepdims=True)

    def query_step(n, ones, gq, gk, q_ref, kc_ref, kp_ref, vc_ref, vp_ref, do_ref, o_ref, l_ref,
                   dq_ref, dk_ref, dv_ref, ck_ref, cvv_ref, gq_acc, gk_acc):
        mask = _band_mask(n > 0)
        hmask = _head_masks(pw)
        dgq = jnp.zeros((8, pw), F32)
        dgk = jnp.zeros((8, pw), F32)
        for r0 in range(0, dil, ATTN_BATCH):
            rs = range(r0, min(r0 + ATTN_BATCH, dil))
            keys = [(r, h) for r in rs for h in range(heads)]
            qn, qhat, qrstd, kn, kphat, kprstd, vv, dob, delta, lse = ({} for _ in range(10))
            for r in rs:
                q, kc, kp = _rows_of(q_ref, r, dil), _rows_of(kc_ref, r, dil), _rows_of(kp_ref, r, dil)
                dov = _rows_of(do_ref, r, dil)
                qn_f, qhat[r], qrstd[r] = _head_norm(q, gq, ones)
                kpn, kphat[r], kprstd[r] = _head_norm(kp, gk, ones)
                qn[r] = qn_f.astype(BF16)
                kn[r] = jnp.concatenate([kpn, _head_norm(kc, gk, ones)[0]], axis=0).astype(BF16)
                vv[r] = jnp.concatenate([_rows_of(vp_ref, r, dil), _rows_of(vc_ref, r, dil)], axis=0).astype(BF16)
                dob[r] = dov.astype(BF16)
                delta[r] = _head_sum(dov * _rows_of(o_ref, r, dil), ones)
                lse[r] = _rows_of(l_ref, r, dil)
            s, dp = {}, {}
            for r, h in keys:
                s[r, h] = lax.dot_general(_only(hmask[h], qn[r]), kn[r], _DIMS["nt"],
                                          preferred_element_type=F32) * scale
                dp[r, h] = lax.dot_general(_only(hmask[h], dob[r]), vv[r], _DIMS["nt"], preferred_element_type=F32)
            p, ds = {}, {}
            for r, h in keys:
                col = slice(h * HEAD, h * HEAD + 1)
                pk = jnp.where(mask, jnp.exp(jnp.where(mask, s[r, h], MASK) - lse[r][:, col]), 0.0)
                ds[r, h] = (pk * (dp[r, h] - delta[r][:, col]) * scale).astype(BF16)
                p[r, h] = pk.astype(BF16)
            dqn, dkn, dvv = {}, {}, {}
            for r in rs:
                dqn[r] = jnp.zeros((ATTN_BLK, pw), F32)
                dkn[r] = jnp.zeros((2 * ATTN_BLK, pw), F32)
                dvv[r] = jnp.zeros((2 * ATTN_BLK, pw), F32)
                for h in range(heads):
                    dqn[r] = jnp.where(hmask[h], jnp.dot(ds[r, h], kn[r], preferred_element_type=F32), dqn[r])
                    dkn[r] = jnp.where(hmask[h], lax.dot_general(ds[r, h], qn[r], _DIMS["tn"],
                                                                 preferred_element_type=F32), dkn[r])
                    dvv[r] = jnp.where(hmask[h], lax.dot_general(p[r, h], dob[r], _DIMS["tn"],
                                                                 preferred_element_type=F32), dvv[r])
            for r in rs:
                dq_all, dg = _head_norm_bwd(dqn[r], qhat[r], qrstd[r], gq, ones)
                dgq = dgq + dg
                dk_all, dg = _head_norm_bwd(_rows_of(ck_ref, r, dil) + dkn[r][:ATTN_BLK], kphat[r], kprstd[r], gk, ones)
                dgk = dgk + dg
                dv_all = _rows_of(cvv_ref, r, dil) + dvv[r][:ATTN_BLK]
                _put_rows(dq_ref, r, dil, dq_all)
                _put_rows(dk_ref, r, dil, dk_all)
                _put_rows(dv_ref, r, dil, dv_all)
                _put_rows(ck_ref, r, dil, dkn[r][ATTN_BLK:])
                _put_rows(cvv_ref, r, dil, dvv[r][ATTN_BLK:])
        gq_acc[...] += dgq
        gk_acc[...] += dgk

    last = nb - 1
    cur = lambda c: pl.BlockSpec((rows, pw), lambda hp, n, c=c: (jnp.minimum(n, last), c + hp))
    prev = lambda c: pl.BlockSpec((rows, pw), lambda hp, n, c=c: (jnp.maximum(n - 1, 0), c + hp))
    gspec = pl.BlockSpec((1, HEAD), lambda hp, n: (0, 0))
    act = jax.ShapeDtypeStruct((S_, ATTN_OUT), F32)
    vec = jax.ShapeDtypeStruct((1, HEAD), F32)
    return _call(
        body, [z, z, z, z, z, q_gain, k_gain, do, o, lse],
        [cur(cq), cur(ck), prev(ck), cur(cv), prev(cv), gspec, gspec, cur(0), cur(0), cur(0)], after,
        name=name, grid=(npairs, nb + 1),
        out_specs=(cur(0), prev(0), prev(0), gspec, gspec),
        out_shape=(act, act, act, vec, vec),
        scratch_shapes=[pltpu.VMEM((rows, pw), F32), pltpu.VMEM((rows, pw), F32),
                        pltpu.VMEM((8, pw), F32), pltpu.VMEM((8, pw), F32)],
        compiler_params=_cparams(("arbitrary", "arbitrary")))


MIX_TN = 256
MIX_TM = 2048


def _sigmoid(v):
    return 1.0 / (1.0 + jnp.exp(-v))


def _mix_fwd(name, z, b_gate, ys, ws):
    S_ = z.shape[0]
    tm, tn = MIX_TM, MIX_TN
    gblk = OFF_GATE // tn

    def body(yp, yc, ya, wp, wc, wa, g0, g1, g2, b0, b1, b2, m_ref):
        acc = None
        for y_ref, w_ref, g_ref, b_ref in ((yp, wp, g0, b0), (yc, wc, g1, b1), (ya, wa, g2, b2)):
            u = lax.dot_general(y_ref[...].astype(BF16), w_ref[...], _DIMS["nt"], preferred_element_type=F32)
            t = _sigmoid(g_ref[...] + b_ref[...]) * u
            acc = t if acc is None else acc + t
        m_ref[...] = acc.astype(BF16)

    yspec = lambda w: pl.BlockSpec((tm, w), lambda i, j: (i, 0))
    wspec = lambda w: pl.BlockSpec((tn, w), lambda i, j: (j, 0))
    gspec = lambda b: pl.BlockSpec((tm, tn), lambda i, j, b=b: (i, gblk + b * (D // tn) + j))
    bspec = lambda b: pl.BlockSpec((1, tn), lambda i, j, b=b: (0, b * (D // tn) + j))
    return pl.pallas_call(
        body, name=name, grid=(S_ // tm, D // tn),
        in_specs=[yspec(POOL_W), yspec(CONV_W), yspec(ATTN_OUT), wspec(POOL_W), wspec(CONV_W), wspec(ATTN_OUT),
                  gspec(0), gspec(1), gspec(2), bspec(0), bspec(1), bspec(2)],
        out_specs=pl.BlockSpec((tm, tn), lambda i, j: (i, j)),
        out_shape=jax.ShapeDtypeStruct((S_, D), BF16),
        compiler_params=_cparams(("parallel", "parallel")),
    )(*ys, *ws, z, z, z, b_gate, b_gate, b_gate)


def _mix_bwd(name, z, b_gate, y, w, dmerged, branch, dz, after=None):
    S_ = z.shape[0]
    tm, tn = MIX_TM, MIX_TN
    width = y.shape[1]
    gblk = OFF_GATE // tn + branch * (D // tn)
    nj = D // tn

    def body(y_ref, w_ref, g_ref, b_ref, dm_ref, dy_ref, dw_ref, dg_ref, db_ref, acc_ref):
        i, j = pl.program_id(0), pl.program_id(1)
        yb = y_ref[...].astype(BF16)
        u = lax.dot_general(yb, w_ref[...], _DIMS["nt"], preferred_element_type=F32)
        sg = _sigmoid(g_ref[...] + b_ref[...])
        dm = dm_ref[...]
        du = (sg * dm).astype(BF16)
        dpre = dm * u * sg * (1.0 - sg)
        dg_ref[...] = dpre.astype(BF16)
        cols = pl.ds(pl.multiple_of(j * tn, tn), tn)
        db_ref[...] = _rows8(dpre)
        d_w = lax.dot_general(du, yb, _DIMS["tn"], preferred_element_type=F32)
        d_y = jnp.dot(du, w_ref[...], preferred_element_type=F32)

        @pl.when(i == 0)
        def _():
            dw_ref[cols, :] = d_w

        @pl.when(i > 0)
        def _():
            dw_ref[cols, :] += d_w

        @pl.when(j == 0)
        def _():
            acc_ref[...] = d_y

        @pl.when(j > 0)
        def _():
            acc_ref[...] += d_y

        @pl.when(j == nj - 1)
        def _():
            dy_ref[...] = acc_ref[...]

    rows = pl.BlockSpec((tm, width), lambda i, j: (i, 0))
    blk = pl.BlockSpec((tm, tn), lambda i, j: (i, j))
    gate = pl.BlockSpec((tm, tn), lambda i, j: (i, gblk + j))
    args = [y, w, z, b_gate, dmerged]
    in_specs = [rows, pl.BlockSpec((tn, width), lambda i, j: (j, 0)), gate,
                pl.BlockSpec((1, tn), lambda i, j: (0, branch * nj + j)), blk]
    aliases = {}
    if not isinstance(dz, jax.ShapeDtypeStruct):
        body = _drop_ref(body, len(args))
        aliases = {len(args): 2}
        args.append(dz)
        in_specs.append(ANY)
    return _call(
        body, args, in_specs, after, name=name, grid=(S_ // tm, nj),
        out_specs=(rows, pl.BlockSpec((D, width), lambda i, j: (0, 0)), gate, pl.BlockSpec((8, tn), lambda i, j: (i, j))),
        out_shape=(jax.ShapeDtypeStruct((S_, width), F32), jax.ShapeDtypeStruct((D, width), F32),
                   jax.ShapeDtypeStruct(dz.shape, dz.dtype), jax.ShapeDtypeStruct((S_ // tm * 8, D), F32)),
        scratch_shapes=[pltpu.VMEM((tm, width), F32)], input_output_aliases=aliases,
        compiler_params=_cparams(("arbitrary", "arbitrary")))


def _relu2_epilogue(acc):
    r = jnp.maximum(acc, 0.0)
    return (r * r,)


def _relu2_bwd_epilogue(acc, r):
    return (acc * (2.0 * jnp.sqrt(r.astype(F32))),)


def _residual_norm_epilogue(acc, xr, gain):
    x = xr + acc
    ms = jnp.mean(x * x, axis=-1, keepdims=True)
    return x, x * lax.rsqrt(ms + EPS) * gain


def _fold_rows(name, part, lanes=False, scale=1.0):
    R, N = part.shape

    def body(p_ref, o_ref):
        s = jnp.sum(p_ref[...], axis=0, keepdims=True)
        if lanes:
            s = jnp.sum(s, axis=1, keepdims=True)
        o_ref[...] = s * scale

    out_n = 1 if lanes else N
    return pl.pallas_call(
        body, name=name, grid=(1,), in_specs=[pl.BlockSpec((R, N), lambda i: (0, 0))],
        out_specs=pl.BlockSpec((1, out_n), lambda i: (0, 0)), out_shape=jax.ShapeDtypeStruct((1, out_n), F32))(part)


def _rows8(v):
    return jnp.sum(v.reshape(v.shape[0] // 8, 8, v.shape[1]), axis=0)


def _residual_loss_epilogue(acc, xr, target):
    e = xr + acc - target
    dy = e * (1.0 / D)
    return dy, _rows8(e * e), dy


def _norm_bwd_epilogue(acc, x, dres, gain):
    rstd = lax.rsqrt(jnp.mean(x * x, axis=-1, keepdims=True) + EPS)
    xhat = x * rstd
    dxhat = acc * gain
    c = jnp.mean(dxhat * xhat, axis=-1, keepdims=True)
    return dres + rstd * (dxhat - xhat * c), _rows8(acc * xhat)


def _norm_bwd_epilogue_twice(acc, x, dres, gain):
    dx, part = _norm_bwd_epilogue(acc, x, dres, gain)
    return dx, part, dx


def _layer_fwd(l, x, h, p, next_gain, target=None, early=None):
    t = f"l{l}_"
    z = _matmul(t + "in_proj", h, p["w_in_t"], "nt", 512, 3712, 1024, after=early)
    y_pool = _pool_fwd(t + "pool", z, p["pool_mix"], p["pool_scale"])
    os_, ls_ = [], []
    for g, (_, dil) in enumerate(ATTN_GROUPS):
        o_g, l_g = _attn_fwd(t + f"attn{g}", z, p["q_gain"], p["k_gain"], g, dil)
        os_.append(o_g)
        ls_.append(l_g)
    y_attn, lse = _attn_combine(t + "attn_mix", os_, ls_)
    y_conv = _conv_fwd(t + "conv", z, p["conv_w"], after=y_attn)
    merged = _mix_fwd(t + "merge", z, p["b_gate"], (y_pool, y_conv, y_attn),
                      (p["w_pool_up_t"], p["w_conv_out_t"], p["w_attn_up_t"]))
    x1, h2 = _matmul(t + "out_proj", merged, p["rest"], "nn", 1024, D, 1024, out_dtypes=(F32, BF16),
                     extras=((x, "mn"), (p["norm_mlp"], "n")), epilogue=_residual_norm_epilogue, b_rows=ROWS_WO)
    r = _matmul(t + "ff1", h2, p["rest"], "nt", 1024, 1024, 1024, out_dtypes=(BF16,), epilogue=_relu2_epilogue,
                b_rows=ROWS_FF1)
    if next_gain is None:
        dy, h_out, dy_16 = _matmul(t + "ff2", r, p["rest"], "nn", 512, D, D_FF,
                                   out_dtypes=(F32, (F32, "rows8"), BF16), extras=((x1, "mn"), (target, "mn")),
                                   epilogue=_residual_loss_epilogue, b_rows=ROWS_FF2)
        x2 = (dy, dy_16)
    else:
        x2, h_out = _matmul(t + "ff2", r, p["rest"], "nn", 512, D, D_FF, out_dtypes=(F32, BF16),
                            extras=((x1, "mn"), (next_gain, "n")), epilogue=_residual_norm_epilogue,
                            b_rows=ROWS_FF2)
    saved = dict(x=x, h=h, z=z, y_pool=y_pool, y_conv=y_conv, y_attn=y_attn, lse=lse, merged=merged,
                 x1=x1, h2=h2, r=r)
    return x2, h_out, saved


def _layer_bwd(l, dx2, p, s, pending, in_t_all, collective_id, start_after=None):
    t = f"l{l}_b_"
    g = {}
    rest = jax.ShapeDtypeStruct((N_DEV, REST_ROWS, D), F32)
    dx2, dx2_16 = dx2
    da = _matmul(t + "d_ff2_in", dx2_16, p["rest"], "nt", 1024, 1024, 1024, out_dtypes=(BF16,),
                 extras=((s["r"], "mn"),), epilogue=_relu2_bwd_epilogue,
                 after=start_after, b_rows=ROWS_FF2)
    rest = _matmul(t + "dw_ff2", s["r"], dx2_16, "tn", 512, 1024, 4096, into=(rest, REST_FF2, 0), after=da)
    tok = rest if pending is None else pending.chip_sums(after=rest)
    dx1, part, dx1_16 = _matmul(
        t + "d_ff1_in", da, p["rest"], "nn", 512, D, D_FF, out_dtypes=(F32, (F32, "rows8"), BF16),
        extras=((s["x1"], "mn"), (dx2, "mn"), (p["norm_mlp"], "n")), epilogue=_norm_bwd_epilogue_twice,
        after=tok, b_rows=ROWS_FF1)
    g["norm_mlp"] = _fold_rows(t + "d_norm_mlp", part)
    rest = _matmul(t + "dw_ff1", da, s["h2"], "tn", 512, 1024, 4096, into=(rest, REST_FF1, 0), after=dx1)
    dmerged = _matmul(t + "d_out_proj_in", dx1_16, p["rest"], "nt", 1024, 1024, 1024, b_rows=ROWS_WO)
    rest = _matmul(t + "dw_o", s["merged"], dx1_16, "tn", 128, 1024, 4096, into=(rest, REST_WO, 0), after=dmerged)
    ys = (s["y_pool"], s["y_conv"], s["y_attn"])
    names = ("w_pool_up_t", "w_conv_out_t", "w_attn_up_t")
    dys, dbs = [], []
    tok = rest
    dz = jax.ShapeDtypeStruct((dx2.shape[0], IN_COLS), BF16)
    for b in range(3):
        dy_b, dw_t, dz, db = _mix_bwd(t + f"merge{b}", s["z"], p["b_gate"], ys[b], p[names[b]], dmerged, b, dz,
                                      after=tok)
        tok = dy_b
        width = ys[b].shape[1]
        if b < 2:
            rest = rest.at[:, REST_UP:REST_ATTN, b * width:(b + 1) * width].set(
                dw_t.reshape(N_DEV, REST_ATTN - REST_UP, width))
        else:
            rest = rest.at[:, REST_ATTN:REST_ROWS, :].set(dw_t.reshape(N_DEV, REST_ROWS - REST_ATTN, D))
        dys.append(dy_b)
        dbs.append(_fold_rows(t + f"d_b_gate{b}", db))
    g["b_gate"] = jnp.concatenate(dbs, axis=1)
    rs_rest = _ReduceScatter(f"rs_rest{l}", collective_id, rest)
    pending_sum = None if pending is None else pending.finish(after=rest, into=(in_t_all, l + 1))
    dz, g["pool_mix"], g["pool_scale"] = _pool_bwd(t + "pool", s["z"], dys[0], p["pool_mix"], p["pool_scale"], dz,
                                                   after=rest if pending is None else pending_sum[0])
    dcb, dcc, dcx, g["conv_w"] = _conv_bwd(t + "conv", s["z"], dys[1], p["conv_w"], after=dz)
    tok = rs_rest.chip_sums(after=dcb)
    dqs, dks, dvs = [], [], []
    gq = gk = None
    for gi, (_, dil) in enumerate(ATTN_GROUPS):
        dq, dk, dv, dgq, dgk = _attn_bwd(t + f"attn{gi}", s["z"], p["q_gain"], p["k_gain"], dys[2], s["y_attn"],
                                         s["lse"], gi, dil, after=tok)
        tok = dq
        dqs.append(dq)
        dks.append(dk)
        dvs.append(dv)
        gq = dgq if gq is None else gq + dgq
        gk = dgk if gk is None else gk + dgk
    g["q_gain"], g["k_gain"] = gq, gk
    rest_sum, _ = rs_rest.finish(after=tok)
    col = OFF_CB
    for piece in [dcb, dcc, dcx] + dqs + dks + dvs:
        dz = lax.dynamic_update_slice(dz, piece.astype(BF16), (0, col))
        col += piece.shape[1]
    in_t = _matmul(t + "dw_in", dz, s["h"], "tn", 256, 1024, 4096, after=rest_sum)
    rs_in = _ReduceScatter(f"rs_in{l}", collective_id + 2, in_t.reshape(N_DEV, IN_ROWS, D))
    dx = _matmul(t + "d_in_proj_in", dz, p["w_in_t"], "nn", 256, D, IN_COLS,
                 out_dtypes=(F32, (F32, "rows8")) + ((BF16,) if l > 0 else ()),
                 extras=((s["x"], "mn"), (dx1, "mn"), (p["norm_mix"], "n")),
                 epilogue=_norm_bwd_epilogue_twice if l > 0 else _norm_bwd_epilogue, after=in_t)
    dx, part = (dx[0], dx[2]) if l > 0 else (dx[0], None), dx[1]
    g["norm_mix"] = _fold_rows(t + "d_norm_mix", part)
    return dx, g, rest_sum, in_t_all if pending is None else pending_sum[0], rs_in


ANY = pl.BlockSpec(memory_space=pl.ANY)


def _mesh_pos():
    return lax.axis_index("x"), lax.axis_index("y"), lax.axis_index("c")


def _other_chips(x, y):
    return [(1 - x, y), (x, 1 - y), (1 - x, 1 - y)]


def _comm_call(name, collective_id, peers, body, arrs, out_shape, sem_counts, after=None):
    n_in, n_out = len(arrs), len(out_shape)
    if collective_id is None:
        def tc_body(*refs):
            body(refs[:n_in], refs[n_in:n_in + n_out], *refs[n_in + n_out:])

        return pl.pallas_call(
            tc_body, name=name, out_shape=tuple(out_shape), in_specs=[ANY] * n_in, out_specs=(ANY,) * n_out,
            scratch_shapes=[pltpu.SemaphoreType.DMA((n,)) for n in sem_counts])(*arrs)

    n_after = 0 if after is None else 1

    def seq_body(*refs):
        barrier = pltpu.get_barrier_semaphore()
        ps = peers(*_mesh_pos())
        for p in ps:
            pl.semaphore_signal(barrier, inc=1, device_id=p, device_id_type=MESH)
        pl.semaphore_wait(barrier, len(ps))
        outs = refs[n_in + n_after:]
        body(refs[:n_in], outs[:n_out], *outs[n_out:])

    return pl.kernel(
        seq_body, out_type=tuple(out_shape), mesh=plsc.ScalarSubcoreMesh(axis_name="seq", num_cores=1), name=name,
        scratch_types=[pltpu.SemaphoreType.DMA((n,)) for n in sem_counts],
        compiler_params=pltpu.CompilerParams(collective_id=collective_id),
    )(*arrs, *([] if after is None else [after]))


def _ordered(name, x, after):
    def body(x_ref, after_ref, o_ref):
        del x_ref, after_ref, o_ref

    return pl.pallas_call(
        body, name=name, out_shape=jax.ShapeDtypeStruct(x.shape, x.dtype), in_specs=[ANY, ANY], out_specs=ANY,
        input_output_aliases={0: 0})(x, after)


def _all_gather(name, collective_id, shard, after=None):
    R, C = shard.shape

    def peers(x, y, c):
        return [(x, y, 1 - c)] + [(*chip, c) for chip in _other_chips(x, y)]

    def body(in_refs, out_refs, send_sems, recv_sems, local_sems):
        (x_ref,), (out_ref,) = in_refs, out_refs
        x, y, c = _mesh_pos()
        me, sibling = (x, y, c), (x, y, 1 - c)
        chips = _other_chips(x, y)

        def slot(px, py, pc):
            return out_ref.at[4 * px + 2 * py + pc]

        def copy(k, block, to, src=None):
            return pltpu.make_async_remote_copy(
                src_ref=slot(*block) if src is None else src, dst_ref=slot(*block),
                send_sem=send_sems.at[k], recv_sem=recv_sems.at[k], device_id=to, device_id_type=MESH)

        mine = pltpu.make_async_copy(x_ref, slot(*me), local_sems.at[0])
        mine.start()
        first = [copy(0, me, sibling, src=x_ref)]
        first += [copy(1 + j, me, (*chip, c), src=x_ref) for j, chip in enumerate(chips)]
        for cp in first:
            cp.start()
        passed = [copy(4 + j, (*chip, c), sibling) for j, chip in enumerate(chips)]
        for j, chip in enumerate(chips):
            copy(1 + j, (*chip, c), me).wait_recv()
            passed[j].start()
        copy(0, sibling, me).wait_recv()
        for j, chip in enumerate(chips):
            copy(4 + j, (*chip, 1 - c), me).wait_recv()
        for cp in first + passed:
            cp.wait_send()
        mine.wait()

    return _comm_call(name, collective_id, peers, body, [shard],
                      [jax.ShapeDtypeStruct((N_DEV, R, C), shard.dtype)], (7, 7, 1), after)[0]


def _rs_sibling_exchange(name, collective_id, arrs):
    n = len(arrs)

    def body(in_refs, out_refs, send_sems, recv_sems):
        x, y, c = _mesh_pos()
        cps = []
        for k, (src, dst) in enumerate(zip(in_refs, out_refs)):
            src = src.at[:, 1 - c] if len(src.shape) == 4 else src
            cps.append(pltpu.make_async_remote_copy(src_ref=src, dst_ref=dst, send_sem=send_sems.at[k],
                                                    recv_sem=recv_sems.at[k], device_id=(x, y, 1 - c),
                                                    device_id_type=MESH))
        for cp in cps:
            cp.start()
        for cp in cps:
            cp.wait()

    out_shape = [jax.ShapeDtypeStruct(a.shape[:1] + a.shape[2:] if a.ndim == 4 else a.shape, a.dtype) for a in arrs]
    return _comm_call(name, collective_id, lambda x, y, c: [(x, y, 1 - c)], body, arrs, out_shape, (n, n))


def _rs_rows(r):
    return r // 2 if (r // 2) % 16 == 0 else r


def _drop_ref(body, idx):
    def wrapped(*refs):
        return body(*refs[:idx], *refs[idx + 1:])

    return wrapped


def _rs_chip_sum(name, ids, big, rbig, after=None):
    _, _, R, C = big.shape
    rows = _rs_rows(R)

    def body(ids_ref, mine_ref, theirs_ref, t16_ref, own_ref):
        p = pl.program_id(1)
        s = mine_ref[...] + theirs_ref[...]
        t16_ref[...] = s.astype(BF16)

        @pl.when(p == ids_ref[1])
        def _():
            own_ref[...] = s

    in_specs = [pl.BlockSpec((None, None, rows, C), lambda i, p, ids: (p, ids[0], i, 0)),
                pl.BlockSpec((None, rows, C), lambda i, p, ids: (p, i, 0))]
    args = [ids, big, rbig]
    if after is not None:
        body = _drop_ref(body, len(args))
        in_specs.append(ANY)
        args.append(after)
    grid_spec = pltpu.PrefetchScalarGridSpec(
        num_scalar_prefetch=1, grid=(R // rows, 4), in_specs=in_specs,
        out_specs=(pl.BlockSpec((None, rows, C), lambda i, p, ids: (p, i, 0)),
                   pl.BlockSpec((rows, C), lambda i, p, ids: (i, 0))))
    return pl.pallas_call(
        body, name=name, grid_spec=grid_spec,
        out_shape=(jax.ShapeDtypeStruct((4, R, C), BF16), jax.ShapeDtypeStruct((R, C), F32)),
        compiler_params=_cparams(("parallel", "arbitrary")),
    )(*args)


def _add2(name, a, b):
    def body(a_ref, b_ref, o_ref):
        o_ref[...] = a_ref[...] + b_ref[...]

    full = pl.BlockSpec(a.shape, lambda i: (0, 0))
    return pl.pallas_call(body, name=name, grid=(1,), in_specs=[full, full], out_specs=full,
                          out_shape=jax.ShapeDtypeStruct(a.shape, a.dtype))(a, b)


def _rs_chip_exchange(name, collective_id, arrs):
    n = len(arrs)

    def body(in_refs, out_refs, send_sems, recv_sems, local_sems):
        x, y, c = _mesh_pos()
        p_me = 2 * x + y
        chips = _other_chips(x, y)

        def part(ref, p):
            return ref.at[p] if len(ref.shape) == 3 else ref

        local = [pltpu.make_async_copy(part(src, p_me), dst.at[p_me], local_sems.at[k])
                 for k, (src, dst) in enumerate(zip(in_refs, out_refs))]
        for cp in local:
            cp.start()
        sends = []
        for j, (px, py) in enumerate(chips):
            for k, (src, dst) in enumerate(zip(in_refs, out_refs)):
                sends.append(pltpu.make_async_remote_copy(
                    src_ref=part(src, 2 * px + py), dst_ref=dst.at[p_me], send_sem=send_sems.at[n * j + k],
                    recv_sem=recv_sems.at[n * j + k], device_id=(px, py, c), device_id_type=MESH))
        for cp in sends:
            cp.start()
        for j, (px, py) in enumerate(chips):
            for k, (src, dst) in enumerate(zip(in_refs, out_refs)):
                pltpu.make_async_remote_copy(
                    src_ref=part(src, p_me), dst_ref=dst.at[2 * px + py], send_sem=send_sems.at[n * j + k],
                    recv_sem=recv_sems.at[n * j + k], device_id=(px, py, c), device_id_type=MESH).wait_recv()
        for cp in sends:
            cp.wait_send()
        for cp in local:
            cp.wait()

    out_shape = [jax.ShapeDtypeStruct((4,) + a.shape[-2:], a.dtype) for a in arrs]
    return _comm_call(name, collective_id, lambda x, y, c: [(*chip, c) for chip in _other_chips(x, y)], body, arrs,
                      out_shape, (3 * n, 3 * n, n))


def _rs_final_sum(name, ids, recv, own, rows, after=None, into=None):
    _, R, C = recv.shape
    assert R % rows == 0

    def body(ids_ref, r_ref, own_ref, o_ref):
        acc = None
        for p in range(4):
            term = jnp.where(ids_ref[1] == p, own_ref[...], r_ref[p].astype(F32))
            acc = term if acc is None else acc + term
        o_ref[...] = acc

    in_specs = [pl.BlockSpec((4, rows, C), lambda i, ids: (0, i, 0)), pl.BlockSpec((rows, C), lambda i, ids: (i, 0))]
    args = [ids, recv, own]
    if after is not None:
        body = _drop_ref(body, len(args))
        in_specs.append(ANY)
        args.append(after)
    out_spec, out_shape, aliases = pl.BlockSpec((rows, C), lambda i, ids: (i, 0)), jax.ShapeDtypeStruct((R, C), F32), {}
    if into is not None:
        buf, slot = into
        out_spec = pl.BlockSpec((None, rows, C), lambda i, ids: (slot, i, 0))
        out_shape = jax.ShapeDtypeStruct(buf.shape, F32)
        if not isinstance(buf, jax.ShapeDtypeStruct):
            body = _drop_ref(body, len(args))
            aliases = {len(args): 0}
            in_specs.append(ANY)
            args.append(buf)
    grid_spec = pltpu.PrefetchScalarGridSpec(
        num_scalar_prefetch=1, grid=(R // rows,), in_specs=in_specs, out_specs=out_spec)
    return pl.pallas_call(
        body, name=name, grid_spec=grid_spec, out_shape=out_shape, input_output_aliases=aliases,
        compiler_params=_cparams(("parallel",)),
    )(*args)


class _ReduceScatter:
    def __init__(self, tag, collective_id, big, small=None):
        x, y, c = _mesh_pos()
        self.tag, self.cid, self.small = tag, collective_id, small
        self.ids = jnp.stack([c, 2 * x + y]).astype(jnp.int32)
        self.big = big.reshape((4, 2) + big.shape[1:])
        self.got = _rs_sibling_exchange(tag + "_sibling", collective_id,
                                        [self.big] + ([] if small is None else [small]))

    def chip_sums(self, after=None):
        t16, self.own = _rs_chip_sum(self.tag + "_chip_sum", self.ids, self.big, self.got[0], after)
        arrs = [t16]
        if self.small is not None:
            self.ts = _add2(self.tag + "_chip_sum_small", self.small, self.got[1])
            arrs.append(self.ts)
        self.recv = _rs_chip_exchange(self.tag + "_chips", self.cid + 1, arrs)
        return t16

    def finish(self, after=None, into=None):
        out = _rs_final_sum(self.tag + "_final", self.ids, self.recv[0], self.own, _rs_rows(self.own.shape[0]), after,
                            into)
        out_small = None
        if self.small is not None:
            out_small = _rs_final_sum(self.tag + "_final_small", self.ids, self.recv[1], self.ts, self.small.shape[0])
        return out, out_small


def _all_reduce_small(tag, small):
    x, y, c = _mesh_pos()
    ids = jnp.stack([c, 2 * x + y]).astype(jnp.int32)
    (theirs,) = _rs_sibling_exchange(tag + "_sibling", None, [small])
    ts = _add2(tag + "_chip_sum", small, theirs)
    (recv,) = _rs_chip_exchange(tag + "_chips", None, [ts])
    return _rs_final_sum(tag + "_final", ids, recv, ts, small.shape[0])


def _adamw(name, w, g, m, v):
    R, C = w.shape
    tiles = [t for t in range(8, 513, 8) if R % t == 0]
    tr = max(tiles) if tiles else R
    c1 = 1.0 - ADAM_B1 ** ADAM_STEP
    c2 = 1.0 - ADAM_B2 ** ADAM_STEP

    def body(w_ref, g_ref, m_ref, v_ref, d_ref, nm_ref, nv_ref):
        gv = g_ref[...]
        nm = ADAM_B1 * m_ref[...] + (1.0 - ADAM_B1) * gv
        nv = ADAM_B2 * v_ref[...] + (1.0 - ADAM_B2) * (gv * gv)
        d_ref[...] = -ADAM_LR * ((nm / c1) / (jnp.sqrt(nv / c2) + ADAM_EPS) + ADAM_WD * w_ref[...])
        nm_ref[...] = nm
        nv_ref[...] = nv

    blk = pl.BlockSpec((tr, C), lambda i: (i, 0))
    shp = jax.ShapeDtypeStruct((R, C), F32)
    return pl.pallas_call(
        body, name=name, grid=(R // tr,), in_specs=[blk] * 4, out_specs=(blk,) * 3, out_shape=(shp,) * 3,
        compiler_params=_cparams(("parallel",)),
    )(w, g, m, v)


def _adamw_t(name, w, g_t, m, v):
    L, R, C = w.shape
    tr = 256
    c1 = 1.0 - ADAM_B1 ** ADAM_STEP
    c2 = 1.0 - ADAM_B2 ** ADAM_STEP

    def body(w_ref, g_ref, m_ref, v_ref, go_ref, d_ref, nm_ref, nv_ref):
        gv = g_ref[...].T
        go_ref[...] = gv
        nm = ADAM_B1 * m_ref[...] + (1.0 - ADAM_B1) * gv
        nv = ADAM_B2 * v_ref[...] + (1.0 - ADAM_B2) * (gv * gv)
        d_ref[...] = -ADAM_LR * ((nm / c1) / (jnp.sqrt(nv / c2) + ADAM_EPS) + ADAM_WD * w_ref[...])
        nm_ref[...] = nm
        nv_ref[...] = nv

    blk = pl.BlockSpec((None, tr, C), lambda l, i: (l, i, 0))
    blk_t = pl.BlockSpec((None, C, tr), lambda l, i: (l, 0, i))
    shp = jax.ShapeDtypeStruct((L, R, C), F32)
    return pl.pallas_call(
        body, name=name, grid=(L, R // tr), in_specs=[blk, blk_t, blk, blk], out_specs=(blk,) * 4,
        out_shape=(shp,) * 4, compiler_params=_cparams(("parallel", "parallel")),
    )(w, g_t, m, v)


REST_FF1 = 0
REST_FF2 = 512
REST_WO = 1024
REST_UP = 1152
REST_ATTN = 1280
REST_ROWS = 1312
IN_ROWS = IN_COLS // N_DEV
ROWS_FF1 = (REST_FF1, D_FF // N_DEV)
ROWS_FF2 = (REST_FF2, D_FF // N_DEV)
ROWS_WO = (REST_WO, D // N_DEV)
SMALL = (("norm_mix", (DEPTH, D)), ("b_gate", (DEPTH, 3 * D)), ("pool_mix", (DEPTH, 4, POOL_G, POOL_G)),
         ("pool_scale", (DEPTH, POOL_W)), ("conv_w", (DEPTH, 3, CONV_W)), ("q_gain", (DEPTH, HEAD)),
         ("k_gain", (DEPTH, HEAD)), ("norm_mlp", (DEPTH, D)))


def _pack_weight_shards(w, l):
    b = lambda a: a.astype(BF16)
    rest = jnp.concatenate([
        b(w["w_ff1"][l].T), b(w["w_ff2"][l]), b(w["w_o"][l]),
        jnp.concatenate([b(w["w_pool_up"][l].T), b(w["w_conv_out"][l].T)], axis=1),
        b(w["w_attn_up"][l].T).reshape(REST_ROWS - REST_ATTN, D)], axis=0)
    return b(w["w_in"][l].T), rest


def _unpack_gathered(g_in, g_rest, g_conv, small_w, l):
    rest = g_rest
    take = lambda r0, rows, c0=0, cols=D: rest[:, r0:r0 + rows, c0:c0 + cols].reshape(N_DEV * rows, cols)
    conv = g_conv[:, 3 * l:3 * l + 3, :CONV_W // N_DEV]
    p = {
        "w_in_t": g_in.reshape(IN_COLS, D),
        "rest": rest,
        "w_pool_up_t": take(REST_UP, 128, 0, POOL_W), "w_conv_out_t": take(REST_UP, 128, POOL_W, CONV_W),
        "w_attn_up_t": rest[:, REST_ATTN:].reshape(D, ATTN_OUT),
        "conv_w": jnp.transpose(conv, (1, 0, 2)).reshape(3, CONV_W),
        "pool_mix": small_w["pool_mix"][l],
    }
    for name in ("norm_mix", "b_gate", "pool_scale", "q_gain", "k_gain", "norm_mlp"):
        p[name] = small_w[name][l][None, :]
    return p


def _pack_small_grads(grads, loss):
    flat = jnp.concatenate([jnp.stack([grads[l][name] for l in range(DEPTH)]).reshape(-1) for name, _ in SMALL]
                           + [loss.reshape(1)])
    return jnp.pad(flat, (0, (-flat.shape[0]) % (8 * 128))).reshape(-1, 128)


def _unpack_grads(rest, small, dev):
    out = {
        "w_ff1_t": jnp.stack([a[REST_FF1:REST_FF1 + 512] for a in rest]),
        "w_ff2": jnp.stack([a[REST_FF2:REST_FF2 + 512] for a in rest]),
        "w_o": jnp.stack([a[REST_WO:REST_WO + 128] for a in rest]),
        "w_pool_up": jnp.stack([a[REST_UP:REST_UP + 128, :POOL_W].T for a in rest]),
        "w_conv_out": jnp.stack([a[REST_UP:REST_UP + 128, POOL_W:].T for a in rest]),
        "w_attn_up": jnp.stack([a[REST_ATTN:].reshape(D // N_DEV, ATTN_OUT).T for a in rest]),
    }
    out.update(_unpack_small_like_grads(small, dev))
    return out


def _unpack_small_like_grads(small, dev):
    out = {}
    flat = small.reshape(-1)
    off = 0
    for name, shp in SMALL:
        n = 1
        for s in shp:
            n *= s
        out[name] = flat[off:off + n].reshape(shp)
        off += n
    out["loss"] = flat[off]
    width = CONV_W // N_DEV
    out["conv_w"] = lax.dynamic_slice_in_dim(out["conv_w"], dev * width, width, axis=2)
    return out


def _pack_small_like_grads(src, dev):
    parts = []
    for name, shp in SMALL:
        a = src[name]
        if name == "conv_w":
            a = lax.dynamic_update_slice_in_dim(jnp.zeros(shp, F32), a, dev * (CONV_W // N_DEV), axis=2)
        parts.append(a.reshape(-1))
    flat = jnp.concatenate(parts)
    return jnp.pad(flat, (0, (-flat.shape[0]) % (8 * 128))).reshape(-1, 128)


WEIGHTS = ("norm_mix", "w_in", "b_gate", "pool_mix", "pool_scale", "conv_w", "q_gain", "k_gain", "w_pool_up",
           "w_conv_out", "w_attn_up", "w_o", "norm_mlp", "w_ff1", "w_ff2")


def kernel(x, norm_mix, w_in, b_gate, pool_mix, pool_scale, conv_w, q_gain, k_gain, w_pool_up, w_conv_out, w_attn_up, w_o, norm_mlp, w_ff1, w_ff2, loss_target, m_norm_mix, m_w_in, m_b_gate, m_pool_mix, m_pool_scale, m_conv_w, m_q_gain, m_k_gain, m_w_pool_up, m_w_conv_out, m_w_attn_up, m_w_o, m_norm_mlp, m_w_ff1, m_w_ff2, v_norm_mix, v_w_in, v_b_gate, v_pool_mix, v_pool_scale, v_conv_w, v_q_gain, v_k_gain, v_w_pool_up, v_w_conv_out, v_w_attn_up, v_w_o, v_norm_mlp, v_w_ff1, v_w_ff2):
    w = dict(zip(WEIGHTS, (norm_mix, w_in, b_gate, pool_mix, pool_scale, conv_w, q_gain, k_gain, w_pool_up,
                           w_conv_out, w_attn_up, w_o, norm_mlp, w_ff1, w_ff2)))
    m = dict(zip(WEIGHTS, (m_norm_mix, m_w_in, m_b_gate, m_pool_mix, m_pool_scale, m_conv_w, m_q_gain, m_k_gain,
                           m_w_pool_up, m_w_conv_out, m_w_attn_up, m_w_o, m_norm_mlp, m_w_ff1, m_w_ff2)))
    v = dict(zip(WEIGHTS, (v_norm_mix, v_w_in, v_b_gate, v_pool_mix, v_pool_scale, v_conv_w, v_q_gain, v_k_gain,
                           v_w_pool_up, v_w_conv_out, v_w_attn_up, v_w_o, v_norm_mlp, v_w_ff1, v_w_ff2)))
    xi, yi, ci = _mesh_pos()
    dev = 4 * xi + 2 * yi + ci

    saved, params = [], []
    act = x[0]
    w_in_t, m_in_t, v_in_t = (jnp.swapaxes(src["w_in"], 1, 2) for src in (w, m, v))
    h = _rmsnorm_fwd("l0_norm_mix", act, w["norm_mix"][0][None])
    conv_shard = jnp.pad(w["conv_w"].reshape(DEPTH * 3, CONV_W // N_DEV), ((0, 8 - DEPTH * 3), (0, 128 - CONV_W // N_DEV)))
    g_conv = _all_gather("gather_conv", None, conv_shard)
    for l in range(DEPTH):
        s_in, s_rest = _pack_weight_shards(w, l)
        g_in = _all_gather(f"gather_in{l}", 1 + 2 * l, s_in)
        g_rest = _all_gather(f"gather_rest{l}", 2 + 2 * l, s_rest)
        params.append(_unpack_gathered(g_in, g_rest, g_conv, w, l))
    for l in range(DEPTH):
        act, h, s = _layer_fwd(l, act, h, params[l], w["norm_mix"][l + 1][None] if l + 1 < DEPTH else None,
                               loss_target[0])
        saved.append(s)
    dy = act
    loss = _fold_rows("loss_sum", h, lanes=True, scale=0.5 / D)
    rest, grads, pending = [None] * DEPTH, [None] * DEPTH, None
    in_t = jax.ShapeDtypeStruct((DEPTH, IN_ROWS, D), F32)
    for l in reversed(range(DEPTH)):
        dy, grads[l], rest[l], in_t, pending = _layer_bwd(l, dy, params[l], saved[l], pending, in_t, 5 + 4 * l)
    dx = dy[0]
    g_small = _all_reduce_small("ar_small", _pack_small_grads(grads, loss))
    pending.chip_sums(after=g_small)
    g = _unpack_grads(rest, g_small, dev)
    total = g.pop("loss")

    delta, new_m, new_v = {}, {}, {}
    g["w_ff1"], delta["w_ff1"], new_m["w_ff1"], new_v["w_ff1"] = _adamw_t("adamw_w_ff1", w["w_ff1"], g["w_ff1_t"],
                                                                          m["w_ff1"], v["w_ff1"])
    for name in ("w_pool_up", "w_conv_out", "w_attn_up", "w_o", "w_ff2"):
        shp = w[name].shape
        two_d = (shp[0] * shp[1], shp[2])
        d_, m_, v_ = _adamw("adamw_" + name, w[name].reshape(two_d), g[name].reshape(two_d),
                            m[name].reshape(two_d), v[name].reshape(two_d))
        delta[name], new_m[name], new_v[name] = d_.reshape(shp), m_.reshape(shp), v_.reshape(shp)
    for name, shp in SMALL:
        shp = w[name].shape
        two_d = (-1, shp[-1])
        d_, m_, v_ = _adamw("adamw_" + name, w[name].reshape(two_d), g[name].reshape(two_d),
                            m[name].reshape(two_d), v[name].reshape(two_d))
        delta[name], new_m[name], new_v[name] = d_.reshape(shp), m_.reshape(shp), v_.reshape(shp)
    g_t, _ = pending.finish(after=new_v[SMALL[-1][0]], into=(in_t, 0))
    two_d = (g_t.shape[0] * g_t.shape[1], g_t.shape[2])
    d_, m_, v_ = _adamw("adamw_w_in", *[a.reshape(two_d) for a in (w_in_t, g_t, m_in_t, v_in_t)])
    g["w_in"] = jnp.swapaxes(g_t, 1, 2)
    delta["w_in"], new_m["w_in"], new_v["w_in"] = (jnp.swapaxes(a.reshape(g_t.shape), 1, 2) for a in (d_, m_, v_))

    return (total, dx[None], *[g[n] for n in WEIGHTS], *[delta[n] for n in WEIGHTS],
            *[new_m[n] for n in WEIGHTS], *[new_v[n] for n in WEIGHTS])
```

```python
import functools

import jax
import jax.numpy as jnp
from jax import lax
from jax.experimental import pallas as pl
from jax.experimental.pallas import tpu as pltpu
from jax.experimental.pallas import tpu_sc as plsc

F32 = jnp.float32
BF16 = jnp.bfloat16
MESH = pl.DeviceIdType.MESH

D = 1024
SEQ = 4096
DEPTH = 2
N_DEV = 8
POOL_WINDOWS = (2, 4, 8, 16)
POOL_W = 512
POOL_G = 128
CONV_W = 512
HEAD = 64
ATTN_GROUPS = ((128, 1), (512, 4), (2048, 16))
HPG = 4
ATTN_W = 768
ATTN_OUT = 256
ATTN_BLK = 128
D_FF = 4096
EPS = 1e-6
MASK = -1e30
OFF_POOL = 0
OFF_CB = 512
OFF_CC = 1024
OFF_CX = 1536
OFF_Q = 2048
OFF_K = 2816
OFF_V = 3584
OFF_GATE = 4352
IN_COLS = 7424
ADAM_LR = 0.001
ADAM_B1 = 0.9
ADAM_B2 = 0.999
ADAM_EPS = 1e-08
ADAM_WD = 0.01
ADAM_STEP = 10

ROW_TILE = 512
SEQ_CHUNK = 256
HALO = 16
VMEM_LIMIT = 56 * 1024 * 1024


def _cparams(sem=None):
    return pltpu.CompilerParams(dimension_semantics=sem, vmem_limit_bytes=VMEM_LIMIT)


def _call(body, args, in_specs, after=None, **kw):
    if after is not None:
        after = list(after) if isinstance(after, (list, tuple)) else [after]
        n, k, inner = len(args), len(after), body

        def body(*refs):
            return inner(*refs[:n], *refs[n + k:])

        args = list(args) + after
        in_specs = list(in_specs) + [pl.BlockSpec(memory_space=pl.ANY)] * k
    return pl.pallas_call(body, in_specs=in_specs, **kw)(*args)


_DIMS = {"nn": (((1,), (0,)), ((), ())), "nt": (((1,), (1,)), ((), ())), "tn": (((0,), (0,)), ((), ()))}


def _matmul(name, a, b, mode, tm, tn, tk, out_dtypes=(F32,), extras=(), epilogue=None, into=None, after=None,
            b_rows=None):
    if mode == "tn":
        K, M = a.shape
    else:
        M, K = a.shape
    b_shape = b.shape if b_rows is None else (b.shape[0] * b_rows[1], b.shape[2])
    N = b_shape[0] if mode == "nt" else b_shape[1]
    assert b_shape[1 if mode == "nt" else 0] == K, (name, b_shape, K)
    assert M % tm == 0 and N % tn == 0 and K % tk == 0, (name, M, N, K, tm, tn, tk)
    nk = K // tk
    n_extra = len(extras)
    n_out = len(out_dtypes)
    dims = _DIMS[mode]
    n_alias = 0 if into is None or isinstance(into[0], jax.ShapeDtypeStruct) else 1

    def body(a_ref, b_ref, *rest):
        extra_refs = rest[:n_extra]
        out_refs = rest[n_extra + n_alias:n_extra + n_alias + n_out]

        def finish(acc):
            if epilogue is None:
                res = (acc,)
            else:
                res = epilogue(acc, *[r[...] for r in extra_refs])
            for o_ref, v in zip(out_refs, res):
                o_ref[...] = v.astype(o_ref.dtype)

        bv = b_ref[...]
        if b_rows is not None:
            bv = bv.reshape(-1, bv.shape[-1])
        part = lax.dot_general(a_ref[...].astype(BF16), bv.astype(BF16), dims, preferred_element_type=F32)
        if nk == 1:
            finish(part)
        else:
            acc_ref = rest[-1]
            k = pl.program_id(2)

            @pl.when(k == 0)
            def _():
                acc_ref[...] = part

            @pl.when(k > 0)
            def _():
                acc_ref[...] += part

            @pl.when(k == nk - 1)
            def _():
                finish(acc_ref[...])

    if mode == "tn":
        a_spec = pl.BlockSpec((tk, tm), lambda i, j, k: (k, i))
    else:
        a_spec = pl.BlockSpec((tm, tk), lambda i, j, k: (i, k))
    if b_rows is not None:
        row0, rows = b_rows
        t_rows = tn if mode == "nt" else tk
        assert t_rows % rows == 0 and row0 % rows == 0, (name, t_rows, b_rows)
        if mode == "nt":
            b_spec = pl.BlockSpec((t_rows // rows, rows, tk), lambda i, j, k: (j, row0 // rows, k))
        else:
            b_spec = pl.BlockSpec((t_rows // rows, rows, tn), lambda i, j, k: (k, row0 // rows, j))
    elif mode == "nt":
        b_spec = pl.BlockSpec((tn, tk), lambda i, j, k: (j, k))
    else:
        b_spec = pl.BlockSpec((tk, tn), lambda i, j, k: (k, j))
    in_specs = [a_spec, b_spec]
    args = [a, b]
    for arr, kind in extras:
        if kind == "mn":
            in_specs.append(pl.BlockSpec((tm, tn), lambda i, j, k: (i, j)))
        else:
            in_specs.append(pl.BlockSpec((1, tn), lambda i, j, k: (0, j)))
        args.append(arr)
    part = [isinstance(dt, tuple) for dt in out_dtypes]
    out_shape = tuple(jax.ShapeDtypeStruct((M // tm * 8, N), dt[0]) if p else jax.ShapeDtypeStruct((M, N), dt)
                      for dt, p in zip(out_dtypes, part))
    out_specs = tuple(pl.BlockSpec((8 if p else tm, tn), lambda i, j, k: (i, j)) for p in part)
    aliases = {}
    if into is not None:
        buf, row0, col0 = into
        assert n_out == 1 and (M // N_DEV) % tm == 0 and row0 % tm == 0 and col0 % tn == 0
        per_dev = M // N_DEV // tm
        out_shape = (jax.ShapeDtypeStruct(buf.shape, buf.dtype),)
        out_specs = (pl.BlockSpec((None, tm, tn), lambda i, j, k: (i // per_dev, row0 // tm + i % per_dev,
                                                                   col0 // tn + j)),)
        if not isinstance(buf, jax.ShapeDtypeStruct):
            aliases = {len(args): 0}
            in_specs.append(pl.BlockSpec(memory_space=pl.ANY))
            args.append(buf)
    scratch = [] if nk == 1 else [pltpu.VMEM((tm, tn), F32)]
    res = _call(
        body, args, in_specs, after, name=name, grid=(M // tm, N // tn, nk), out_specs=out_specs,
        out_shape=out_shape, scratch_shapes=scratch, input_output_aliases=aliases,
        compiler_params=_cparams(("parallel", "parallel", "arbitrary")))
    return res if n_out > 1 else res[0]


def _rmsnorm_fwd(name, x, gain, after=None):
    S_, D_ = x.shape

    def body(x_ref, g_ref, h_ref):
        xf = x_ref[...]
        ms = jnp.mean(xf * xf, axis=-1, keepdims=True)
        h_ref[...] = (xf * lax.rsqrt(ms + EPS) * g_ref[...]).astype(BF16)

    return _call(
        body, [x, gain], [pl.BlockSpec((ROW_TILE, D_), lambda i: (i, 0)), pl.BlockSpec((1, D_), lambda i: (0, 0))],
        after, name=name, grid=(S_ // ROW_TILE,),
        out_specs=pl.BlockSpec((ROW_TILE, D_), lambda i: (i, 0)),
        out_shape=jax.ShapeDtypeStruct((S_, D_), BF16),
        compiler_params=_cparams(("parallel",)))


def _rmsnorm_bwd(name, x, gain, dh, dres, after=None):
    S_, D_ = x.shape
    n = S_ // ROW_TILE

    def body(x_ref, g_ref, dh_ref, dres_ref, dx_ref, dg_ref, acc_ref):
        i = pl.program_id(0)
        xf = x_ref[...]
        rstd = lax.rsqrt(jnp.mean(xf * xf, axis=-1, keepdims=True) + EPS)
        xhat = xf * rstd
        dhv = dh_ref[...]
        dxhat = dhv * g_ref[...]
        c = jnp.mean(dxhat * xhat, axis=-1, keepdims=True)
        dx_ref[...] = dres_ref[...] + rstd * (dxhat - xhat * c)
        part = jnp.sum((dhv * xhat).reshape(ROW_TILE // 8, 8, D_), axis=0)

        @pl.when(i == 0)
        def _():
            acc_ref[...] = part

        @pl.when(i > 0)
        def _():
            acc_ref[...] += part

        @pl.when(i == n - 1)
        def _():
            dg_ref[...] = jnp.sum(acc_ref[...], axis=0, keepdims=True)

    row = pl.BlockSpec((ROW_TILE, D_), lambda i: (i, 0))
    vec = pl.BlockSpec((1, D_), lambda i: (0, 0))
    return _call(
        body, [x, gain, dh, dres], [row, vec, row, row], after, name=name, grid=(n,), out_specs=(row, vec),
        out_shape=(jax.ShapeDtypeStruct((S_, D_), F32), jax.ShapeDtypeStruct((1, D_), F32)),
        scratch_shapes=[pltpu.VMEM((8, D_), F32)],
        compiler_params=_cparams(("arbitrary",)))


def _loss_head(name, y, target):
    S_, D_ = y.shape
    n = S_ // ROW_TILE

    def body(y_ref, t_ref, dy_ref, l_ref, acc_ref):
        i = pl.program_id(0)
        e = y_ref[...] - t_ref[...]
        dy_ref[...] = e * (1.0 / D_)
        part = jnp.sum((e * e).reshape(ROW_TILE // 8, 8, D_), axis=0)

        @pl.when(i == 0)
        def _():
            acc_ref[...] = part

        @pl.when(i > 0)
        def _():
            acc_ref[...] += part

        @pl.when(i == n - 1)
        def _():
            s = jnp.sum(acc_ref[...], axis=1, keepdims=True)
            l_ref[...] = jnp.sum(s, axis=0, keepdims=True) * (0.5 / D_)

    row = pl.BlockSpec((ROW_TILE, D_), lambda i: (i, 0))
    return pl.pallas_call(
        body, name=name, grid=(n,), in_specs=[row, row],
        out_specs=(row, pl.BlockSpec((1, 1), lambda i: (0, 0))),
        out_shape=(jax.ShapeDtypeStruct((S_, D_), F32), jax.ShapeDtypeStruct((1, 1), F32)),
        scratch_shapes=[pltpu.VMEM((8, D_), F32)],
        compiler_params=_cparams(("arbitrary",)),
    )(y, target)


def _rows_with_halo(ref, cols, i, n_chunks, before, after):
    r0 = pl.multiple_of(i * SEQ_CHUNK, SEQ_CHUNK)
    parts = []
    if before:
        h0 = pl.multiple_of(jnp.maximum(r0 - HALO, 0), 8)
        halo = ref[pl.ds(h0, HALO), cols]
        parts.append(jnp.where(i > 0, halo, jnp.zeros_like(halo)))
    parts.append(ref[pl.ds(r0, SEQ_CHUNK), cols])
    if after:
        a0 = pl.multiple_of(jnp.minimum(r0 + SEQ_CHUNK, (n_chunks - 1) * SEQ_CHUNK + SEQ_CHUNK - HALO), 8)
        halo = ref[pl.ds(a0, HALO), cols]
        parts.append(jnp.where(i < n_chunks - 1, halo, jnp.zeros_like(halo)))
    return parts[0] if len(parts) == 1 else jnp.concatenate(parts, axis=0)


def _shift_down(v, k):
    return pltpu.roll(v, k, 0)


def _shift_up(v, k):
    return pltpu.roll(v, v.shape[0] - k, 0)


def _pool_diff(xx, w, t_main):
    s = xx
    k = 1
    while k < w:
        s = s + _shift_down(s, k)
        k *= 2
    cnt = jnp.minimum(t_main + 1, w).astype(F32)
    return s[HALO:] / cnt - xx[HALO:]


def _pool_fwd(name, z, pool_mix, pool_scale):
    S_ = z.shape[0]
    n_chunks = S_ // SEQ_CHUNK

    cols = slice(0, POOL_G)

    def body(u_ref, mix_ref, sc_ref, y_ref):
        mixg = mix_ref[...].astype(BF16)
        scg = sc_ref[...]
        for g, w in enumerate(POOL_WINDOWS):
            @pl.when(pl.program_id(0) == g)
            def _(w=w):
                def chunk(i, carry):
                    r0 = pl.multiple_of(i * SEQ_CHUNK, SEQ_CHUNK)
                    xx = _rows_with_halo(u_ref, cols, i, n_chunks, True, False)
                    t = r0 + lax.broadcasted_iota(jnp.int32, (SEQ_CHUNK, POOL_G), 0)
                    d = _pool_diff(xx, w, t)
                    y = jnp.dot(d.astype(BF16), mixg, preferred_element_type=F32) * scg
                    y_ref[pl.ds(r0, SEQ_CHUNK), :] = y.astype(BF16)
                    return carry

                lax.fori_loop(0, n_chunks, chunk, 0)

    slab = pl.BlockSpec((S_, POOL_G), lambda g: (0, g))
    return pl.pallas_call(
        body, name=name, grid=(len(POOL_WINDOWS),),
        in_specs=[slab, pl.BlockSpec((None, POOL_G, POOL_G), lambda g: (g, 0, 0)),
                  pl.BlockSpec((1, POOL_G), lambda g: (0, g))],
        out_specs=slab, out_shape=jax.ShapeDtypeStruct((S_, POOL_W), BF16),
        compiler_params=_cparams(("parallel",)),
    )(z, pool_mix, pool_scale)


def _pool_bwd(name, z, dy, pool_mix, pool_scale, dz, after=None):
    S_ = z.shape[0]
    n_chunks = S_ // SEQ_CHUNK
    rows_a = SEQ_CHUNK + HALO

    cols = slice(0, POOL_G)

    def body(u_ref, dy_ref, mix_ref, sc_ref, du_ref, dmix_ref, dsc_ref):
        mixg = mix_ref[...].astype(BF16)
        scg = sc_ref[...]
        for g, w in enumerate(POOL_WINDOWS):
            @pl.when(pl.program_id(0) == g)
            def _(w=w):
                def chunk(i, carry):
                    dmix_acc, dsc_acc = carry
                    r0 = pl.multiple_of(i * SEQ_CHUNK, SEQ_CHUNK)
                    xx = _rows_with_halo(u_ref, cols, i, n_chunks, True, False)
                    t = r0 + lax.broadcasted_iota(jnp.int32, (SEQ_CHUNK, POOL_G), 0)
                    d = _pool_diff(xx, w, t).astype(BF16)
                    ypre = jnp.dot(d, mixg, preferred_element_type=F32)
                    dyy = _rows_with_halo(dy_ref, cols, i, n_chunks, False, True)
                    dys = (dyy * scg).astype(BF16)
                    dsc_acc = dsc_acc + jnp.sum((dyy[:SEQ_CHUNK] * ypre).reshape(SEQ_CHUNK // 8, 8, POOL_G), axis=0)
                    dmix_acc = dmix_acc + lax.dot_general(d, dys[:SEQ_CHUNK], _DIMS["tn"],
                                                          preferred_element_type=F32)
                    dd = lax.dot_general(dys, mixg, _DIMS["nt"], preferred_element_type=F32)
                    ta = r0 + lax.broadcasted_iota(jnp.int32, (rows_a, POOL_G), 0)
                    f = dd / jnp.minimum(ta + 1, w).astype(F32)
                    k = 1
                    while k < w:
                        f = f + _shift_up(f, k)
                        k *= 2
                    du_ref[pl.ds(r0, SEQ_CHUNK), :] = (f[:SEQ_CHUNK] - dd[:SEQ_CHUNK]).astype(BF16)
                    return dmix_acc, dsc_acc

                dmix_acc, dsc_acc = lax.fori_loop(
                    0, n_chunks, chunk, (jnp.zeros((POOL_G, POOL_G), F32), jnp.zeros((8, POOL_G), F32)))
                dmix_ref[...] = dmix_acc
                dsc_ref[...] = jnp.sum(dsc_acc, axis=0, keepdims=True)

    slab = pl.BlockSpec((S_, POOL_G), lambda g: (0, g))
    mix_spec = pl.BlockSpec((None, POOL_G, POOL_G), lambda g: (g, 0, 0))
    vec = pl.BlockSpec((1, POOL_G), lambda g: (0, g))
    return _call(
        _drop_ref(body, 4), [z, dy, pool_mix, pool_scale, dz], [slab, slab, mix_spec, vec, ANY], after, name=name,
        grid=(len(POOL_WINDOWS),), out_specs=(slab, mix_spec, vec),
        out_shape=(jax.ShapeDtypeStruct(dz.shape, dz.dtype), jax.ShapeDtypeStruct((4, POOL_G, POOL_G), F32),
                   jax.ShapeDtypeStruct((1, POOL_W), F32)),
        input_output_aliases={4: 0}, compiler_params=_cparams(("parallel",)))


def _conv_specs(S_):
    slab = lambda off: pl.BlockSpec((S_, 128), lambda c, off=off: (0, off // 128 + c))
    return slab(OFF_CB), slab(OFF_CC), slab(OFF_CX)


def _conv_fwd(name, z, conv_w, after=None):
    S_ = z.shape[0]
    n_chunks = S_ // SEQ_CHUNK
    col = slice(0, 128)

    def body(b_ref, c_ref, x_ref, w_ref, y_ref):
        w0, w1, w2 = w_ref[0:1, :], w_ref[1:2, :], w_ref[2:3, :]

        def chunk(i, carry):
            r0 = pl.multiple_of(i * SEQ_CHUNK, SEQ_CHUNK)
            u = _rows_with_halo(c_ref, col, i, n_chunks, True, False) * _rows_with_halo(x_ref, col, i, n_chunks, True, False)
            y = w2 * u + w1 * _shift_down(u, 1) + w0 * _shift_down(u, 2)
            y_ref[pl.ds(r0, SEQ_CHUNK), :] = (b_ref[pl.ds(r0, SEQ_CHUNK), :] * y[HALO:]).astype(BF16)
            return carry

        lax.fori_loop(0, n_chunks, chunk, 0)

    sb, sc, sx = _conv_specs(S_)
    return _call(
        body, [z, z, z, conv_w], [sb, sc, sx, pl.BlockSpec((3, 128), lambda c: (0, c))], after,
        name=name, grid=(CONV_W // 128,),
        out_specs=pl.BlockSpec((S_, 128), lambda c: (0, c)),
        out_shape=jax.ShapeDtypeStruct((S_, CONV_W), BF16),
        compiler_params=_cparams(("parallel",)))


def _conv_bwd(name, z, dy, conv_w, after=None):
    S_ = z.shape[0]
    n_chunks = S_ // SEQ_CHUNK
    col = slice(0, 128)
    lo, hi = HALO, HALO + SEQ_CHUNK

    def body(b_ref, c_ref, x_ref, dy_ref, w_ref, db_ref, dc_ref, dx_ref, dw_ref):
        w0, w1, w2 = w_ref[0:1, :], w_ref[1:2, :], w_ref[2:3, :]

        def chunk(i, carry):
            a0, a1, a2 = carry
            r0 = pl.multiple_of(i * SEQ_CHUNK, SEQ_CHUNK)
            cc = _rows_with_halo(c_ref, col, i, n_chunks, True, True)
            xx = _rows_with_halo(x_ref, col, i, n_chunks, True, True)
            bb = _rows_with_halo(b_ref, col, i, n_chunks, True, True)
            dyy = _rows_with_halo(dy_ref, col, i, n_chunks, True, True)
            u = cc * xx
            u1 = _shift_down(u, 1)
            u2 = _shift_down(u, 2)
            y = w2 * u + w1 * u1 + w0 * u2
            dyv = dyy * bb
            du = w2 * dyv + w1 * _shift_up(dyv, 1) + w0 * _shift_up(dyv, 2)
            db_ref[pl.ds(r0, SEQ_CHUNK), :] = (dyy[lo:hi] * y[lo:hi]).astype(BF16)
            dc_ref[pl.ds(r0, SEQ_CHUNK), :] = (du[lo:hi] * xx[lo:hi]).astype(BF16)
            dx_ref[pl.ds(r0, SEQ_CHUNK), :] = (du[lo:hi] * cc[lo:hi]).astype(BF16)
            red = lambda v: jnp.sum(v.reshape(SEQ_CHUNK // 8, 8, 128), axis=0)
            dm = dyv[lo:hi]
            return a0 + red(dm * u2[lo:hi]), a1 + red(dm * u1[lo:hi]), a2 + red(dm * u[lo:hi])

        zero = jnp.zeros((8, 128), F32)
        a0, a1, a2 = lax.fori_loop(0, n_chunks, chunk, (zero, zero, zero))
        dw_ref[0:1, :] = jnp.sum(a0, axis=0, keepdims=True)
        dw_ref[1:2, :] = jnp.sum(a1, axis=0, keepdims=True)
        dw_ref[2:3, :] = jnp.sum(a2, axis=0, keepdims=True)

    sb, sc, sx = _conv_specs(S_)
    slab = pl.BlockSpec((S_, 128), lambda c: (0, c))
    wspec = pl.BlockSpec((3, 128), lambda c: (0, c))
    act = jax.ShapeDtypeStruct((S_, CONV_W), BF16)
    return _call(
        body, [z, z, z, dy, conv_w], [sb, sc, sx, slab, wspec], after, name=name, grid=(CONV_W // 128,),
        out_specs=(slab, slab, slab, wspec),
        out_shape=(act, act, act, jax.ShapeDtypeStruct((3, CONV_W), F32)),
        compiler_params=_cparams(("parallel",)))


def _head_ones(pw):
    a = lax.broadcasted_iota(jnp.int32, (pw, pw), 0) // HEAD
    b = lax.broadcasted_iota(jnp.int32, (pw, pw), 1) // HEAD
    return (a == b).astype(BF16)


def _head_sum(v, ones):
    hi = v.astype(BF16)
    lo = (v - hi.astype(F32)).astype(BF16)
    return jnp.dot(hi, ones, preferred_element_type=F32) + jnp.dot(lo, ones, preferred_element_type=F32)


def _head_norm(v, gain, ones):
    rstd = lax.rsqrt(_head_sum(v * v, ones) * (1.0 / HEAD) + EPS)
    xhat = v * rstd
    return xhat * gain, xhat, rstd


def _head_norm_bwd(dy, xhat, rstd, gain, ones):
    dxhat = dy * gain
    c = _head_sum(dxhat * xhat, ones) * (1.0 / HEAD)
    dv = rstd * (dxhat - xhat * c)
    dg = jnp.sum((dy * xhat).reshape(dy.shape[0] // 8, 8, dy.shape[1]), axis=0)
    return dv, dg


def _head_masks(pw):
    lane_head = lax.broadcasted_iota(jnp.int32, (1, pw), 1) // HEAD
    return [lane_head == h for h in range(pw // HEAD)]


def _only(mask, v):
    return jnp.where(mask, v, jnp.zeros_like(v))


def _attn_specs(S_, g, dil):
    rows = ATTN_BLK * dil
    nb = S_ // rows
    pw = 128 if dil > 1 else ATTN_OUT
    cq, ck, cv = ((OFF_Q + g * ATTN_OUT) // pw, (OFF_K + g * ATTN_OUT) // pw, (OFF_V + g * ATTN_OUT) // pw)
    return rows, nb, pw, pw // HEAD, ATTN_OUT // pw, cq, ck, cv


ATTN_BATCH = 4


def _attn_group(dil):
    return 4 if dil == 1 else 1


def _attn_block_specs(rows, grp, pw, last=None):
    step = (lambda n: n) if last is None else (lambda n: jnp.minimum(n, last))
    cur = lambda c: pl.BlockSpec((rows * grp, pw), lambda hp, n, c=c: (step(n), c + hp))
    prev = lambda c: pl.BlockSpec((rows, pw), lambda hp, n, c=c: (jnp.maximum(step(n) * grp - 1, 0), c + hp))
    return cur, prev


def _band_mask(has_prev):
    qi = lax.broadcasted_iota(jnp.int32, (ATTN_BLK, 2 * ATTN_BLK), 0)
    ki = lax.broadcasted_iota(jnp.int32, (ATTN_BLK, 2 * ATTN_BLK), 1)
    in_prev = jnp.logical_and(ki < ATTN_BLK, ki >= qi)
    if has_prev is not True:
        in_prev = jnp.logical_and(in_prev, has_prev)
    return jnp.logical_or(in_prev, jnp.logical_and(ki >= ATTN_BLK, ki - ATTN_BLK <= qi))


def _rows_of(ref, r, dil):
    if dil == 1:
        return ref[r * ATTN_BLK:(r + 1) * ATTN_BLK, :]
    return ref[pl.ds(r, ATTN_BLK, stride=dil), :]


def _put_rows(ref, r, dil, val):
    if dil == 1:
        ref[r * ATTN_BLK:(r + 1) * ATTN_BLK, :] = val.astype(ref.dtype)
    else:
        ref[pl.ds(r, ATTN_BLK, stride=dil), :] = val.astype(ref.dtype)


def _attn_fwd(name, z, q_gain, k_gain, g, dil):
    S_ = z.shape[0]
    rows, nb, pw, heads, npairs, cq, ck, cv = _attn_specs(S_, g, dil)
    scale = HEAD ** -0.5

    grp = _attn_group(dil)
    nsteps = nb // grp

    def body(q_ref, k_ref, kp_ref, v_ref, vp_ref, gq_ref, gk_ref, o_ref, l_ref):
        n = pl.program_id(1)
        ones, hmask = _head_ones(pw), _head_masks(pw)
        gq, gk = jnp.tile(gq_ref[...], (1, heads)), jnp.tile(gk_ref[...], (1, heads))
        mask_first, mask_rest = _band_mask(n > 0), _band_mask(True)
        for r0 in range(0, dil * grp, ATTN_BATCH):
            rs = range(r0, min(r0 + ATTN_BATCH, dil * grp))
            qn, kn, vv, s, p = {}, {}, {}, {}, {}
            kcn = {}
            for r in rs:
                q, kc, vc = _rows_of(q_ref, r, dil), _rows_of(k_ref, r, dil), _rows_of(v_ref, r, dil)
                kcn[r] = _head_norm(kc, gk, ones)[0]
                if dil == 1 and r > 0:
                    kpn = kcn[r - 1] if r - 1 in kcn else _head_norm(_rows_of(k_ref, r - 1, dil), gk, ones)[0]
                    vp = _rows_of(v_ref, r - 1, dil)
                else:
                    kpn, vp = _head_norm(_rows_of(kp_ref, r, dil), gk, ones)[0], _rows_of(vp_ref, r, dil)
                qn[r] = _head_norm(q, gq, ones)[0].astype(BF16)
                kn[r] = jnp.concatenate([kpn, kcn[r]], axis=0).astype(BF16)
                vv[r] = jnp.concatenate([vp, vc], axis=0).astype(BF16)
            keys = [(r, h) for r in rs for h in range(heads)]
            for r, h in keys:
                s[r, h] = lax.dot_general(_only(hmask[h], qn[r]), kn[r], _DIMS["nt"],
                                          preferred_element_type=F32) * scale
            lse, den = {}, {}
            for key in keys:
                mask = mask_rest if (dil == 1 and key[0] > 0) else mask_first
                sm = jnp.where(mask, s[key], MASK)
                m = jnp.max(sm, axis=-1, keepdims=True)
                e = jnp.exp(sm - m)
                den[key] = jnp.sum(e, axis=-1, keepdims=True)
                p[key] = e.astype(BF16)
                lse[key] = m + jnp.log(den[key])
            for r in rs:
                out = jnp.zeros((ATTN_BLK, pw), F32)
                lse_all = jnp.zeros((ATTN_BLK, pw), F32)
                for h in range(heads):
                    out = jnp.where(hmask[h], jnp.dot(p[r, h], vv[r], preferred_element_type=F32) / den[r, h], out)
                    lse_all = jnp.where(hmask[h], lse[r, h], lse_all)
                _put_rows(o_ref, r, dil, out)
                _put_rows(l_ref, r, dil, lse_all)

    cur, prev = _attn_block_specs(rows, grp, pw)
    gspec = pl.BlockSpec((1, HEAD), lambda hp, n: (0, 0))
    shp = jax.ShapeDtypeStruct((S_, ATTN_OUT), F32)
    return pl.pallas_call(
        body, name=name, grid=(npairs, nsteps),
        in_specs=[cur(cq), cur(ck), prev(ck), cur(cv), prev(cv), gspec, gspec],
        out_specs=(cur(0), cur(0)), out_shape=(shp, shp),
        compiler_params=_cparams(("parallel", "parallel")),
    )(z, z, z, z, z, q_gain, k_gain)


def _attn_combine(name, os_, ls_):
    S_ = os_[0].shape[0]

    def body(o0, o1, o2, l0, l1, l2, o_ref, l_ref):
        a, b, c = l0[...], l1[...], l2[...]
        m = jnp.maximum(jnp.maximum(a, b), c)
        ea, eb, ec = jnp.exp(a - m), jnp.exp(b - m), jnp.exp(c - m)
        zsum = ea + eb + ec
        o_ref[...] = (ea * o0[...] + eb * o1[...] + ec * o2[...]) / zsum
        l_ref[...] = m + jnp.log(zsum)

    row = pl.BlockSpec((ROW_TILE, ATTN_OUT), lambda i: (i, 0))
    shp = jax.ShapeDtypeStruct((S_, ATTN_OUT), F32)
    return pl.pallas_call(
        body, name=name, grid=(S_ // ROW_TILE,), in_specs=[row] * 6, out_specs=(row, row), out_shape=(shp, shp),
        compiler_params=_cparams(("parallel",)),
    )(*os_, *ls_)


def _attn_bwd(name, z, q_gain, k_gain, do, o, lse, g, dil, after=None):
    S_ = z.shape[0]
    rows, nb, pw, heads, npairs, cq, ck, cv = _attn_specs(S_, g, dil)
    scale = HEAD ** -0.5

    def body(q_ref, kc_ref, kp_ref, vc_ref, vp_ref, gq_ref, gk_ref, do_ref, o_ref, l_ref,
             dq_ref, dk_ref, dv_ref, dgq_ref, dgk_ref, ck_ref, cvv_ref, gq_acc, gk_acc):
        hp = pl.program_id(0)
        n = pl.program_id(1)
        ones = _head_ones(pw)
        gq, gk = jnp.tile(gq_ref[...], (1, heads)), jnp.tile(gk_ref[...], (1, heads))

        @pl.when(n == 0)
        def _():
            ck_ref[...] = jnp.zeros_like(ck_ref)
            cvv_ref[...] = jnp.zeros_like(cvv_ref)

        @pl.when(jnp.logical_and(n == 0, hp == 0))
        def _():
            gq_acc[...] = jnp.zeros_like(gq_acc)
            gk_acc[...] = jnp.zeros_like(gk_acc)

        pl.when(n < nb)(functools.partial(query_step, n, ones, gq, gk, q_ref, kc_ref, kp_ref, vc_ref, vp_ref, do_ref,
                                          o_ref, l_ref, dq_ref, dk_ref, dv_ref, ck_ref, cvv_ref, gq_acc, gk_acc))

        @pl.when(n == nb)
        def _():
            dgk = jnp.zeros((8, pw), F32)
            for r in range(dil):
                _, kphat, kprstd = _head_norm(_rows_of(kp_ref, r, dil), gk, ones)
                dk_all, dg = _head_norm_bwd(_rows_of(ck_ref, r, dil), kphat, kprstd, gk, ones)
                dgk = dgk + dg
                _put_rows(dk_ref, r, dil, dk_all)
                _put_rows(dv_ref, r, dil, _rows_of(cvv_ref, r, dil))
            gk_acc[...] += dgk

        @pl.when(jnp.logical_and(n == nb, hp == npairs - 1))
        def _():
            fold = lambda a: sum(a[:, h * HEAD:(h + 1) * HEAD] for h in range(heads))
            dgq_ref[...] = jnp.sum(fold(gq_acc[...]), axis=0, keepdims=True)
            dgk_ref[...] = jnp.sum(fold(gk_acc[...]), axis=0, keepdims=True)

    def query_step(n, ones, gq, gk, q_ref, kc_ref, kp_ref, vc_ref, vp_ref, do_ref, o_ref, l_ref,
                   dq_ref, dk_ref, dv_ref, ck_ref, cvv_ref, gq_acc, gk_acc):
        mask = _band_mask(n > 0)
        hmask = _head_masks(pw)
        dgq = jnp.zeros((8, pw), F32)
        dgk = jnp.zeros((8, pw), F32)
        for r0 in range(0, dil, ATTN_BATCH):
            rs = range(r0, min(r0 + ATTN_BATCH, dil))
            keys = [(r, h) for r in rs for h in range(heads)]
            qn, qhat, qrstd, kn, kphat, kprstd, vv, dob, delta, lse = ({} for _ in range(10))
            for r in rs:
                q, kc, kp = _rows_of(q_ref, r, dil), _rows_of(kc_ref, r, dil), _rows_of(kp_ref, r, dil)
                dov = _rows_of(do_ref, r, dil)
                qn_f, qhat[r], qrstd[r] = _head_norm(q, gq, ones)
                kpn, kphat[r], kprstd[r] = _head_norm(kp, gk, ones)
                qn[r] = qn_f.astype(BF16)
                kn[r] = jnp.concatenate([kpn, _head_norm(kc, gk, ones)[0]], axis=0).astype(BF16)
                vv[r] = jnp.concatenate([_rows_of(vp_ref, r, dil), _rows_of(vc_ref, r, dil)], axis=0).astype(BF16)
                dob[r] = dov.astype(BF16)
                delta[r] = _head_sum(dov * _rows_of(o_ref, r, dil), ones)
                lse[r] = _rows_of(l_ref, r, dil)
            s, dp = {}, {}
            for r, h in keys:
                s[r, h] = lax.dot_general(_only(hmask[h], qn[r]), kn[r], _DIMS["nt"],
                                          preferred_element_type=F32) * scale
                dp[r, h] = lax.dot_general(_only(hmask[h], dob[r]), vv[r], _DIMS["nt"], preferred_element_type=F32)
            p, ds = {}, {}
            for r, h in keys:
                col = slice(h * HEAD, h * HEAD + 1)
                pk = jnp.where(mask, jnp.exp(jnp.where(mask, s[r, h], MASK) - lse[r][:, col]), 0.0)
                ds[r, h] = (pk * (dp[r, h] - delta[r][:, col]) * scale).astype(BF16)
                p[r, h] = pk.astype(BF16)
            dqn, dkn, dvv = {}, {}, {}
            for r in rs:
                dqn[r] = jnp.zeros((ATTN_BLK, pw), F32)
                dkn[r] = jnp.zeros((2 * ATTN_BLK, pw), F32)
                dvv[r] = jnp.zeros((2 * ATTN_BLK, pw), F32)
                for h in range(heads):
                    dqn[r] = jnp.where(hmask[h], jnp.dot(ds[r, h], kn[r], preferred_element_type=F32), dqn[r])
                    dkn[r] = jnp.where(hmask[h], lax.dot_general(ds[r, h], qn[r], _DIMS["tn"],
                                                                 preferred_element_type=F32), dkn[r])
                    dvv[r] = jnp.where(hmask[h], lax.dot_general(p[r, h], dob[r], _DIMS["tn"],
                                                                 preferred_element_type=F32), dvv[r])
            for r in rs:
                dq_all, dg = _head_norm_bwd(dqn[r], qhat[r], qrstd[r], gq, ones)
                dgq = dgq + dg
                dk_all, dg = _head_norm_bwd(_rows_of(ck_ref, r, dil) + dkn[r][:ATTN_BLK], kphat[r], kprstd[r], gk, ones)
                dgk = dgk + dg
                dv_all = _rows_of(cvv_ref, r, dil) + dvv[r][:ATTN_BLK]
                _put_rows(dq_ref, r, dil, dq_all)
                _put_rows(dk_ref, r, dil, dk_all)
                _put_rows(dv_ref, r, dil, dv_all)
                _put_rows(ck_ref, r, dil, dkn[r][ATTN_BLK:])
                _put_rows(cvv_ref, r, dil, dvv[r][ATTN_BLK:])
        gq_acc[...] += dgq
        gk_acc[...] += dgk

    last = nb - 1
    cur = lambda c: pl.BlockSpec((rows, pw), lambda hp, n, c=c: (jnp.minimum(n, last), c + hp))
    prev = lambda c: pl.BlockSpec((rows, pw), lambda hp, n, c=c: (jnp.maximum(n - 1, 0), c + hp))
    gspec = pl.BlockSpec((1, HEAD), lambda hp, n: (0, 0))
    act = jax.ShapeDtypeStruct((S_, ATTN_OUT), F32)
    vec = jax.ShapeDtypeStruct((1, HEAD), F32)
    return _call(
        body, [z, z, z, z, z, q_gain, k_gain, do, o, lse],
        [cur(cq), cur(ck), prev(ck), cur(cv), prev(cv), gspec, gspec, cur(0), cur(0), cur(0)], after,
        name=name, grid=(npairs, nb + 1),
        out_specs=(cur(0), prev(0), prev(0), gspec, gspec),
        out_shape=(act, act, act, vec, vec),
        scratch_shapes=[pltpu.VMEM((rows, pw), F32), pltpu.VMEM((rows, pw), F32),
                        pltpu.VMEM((8, pw), F32), pltpu.VMEM((8, pw), F32)],
        compiler_params=_cparams(("arbitrary", "arbitrary")))


MIX_TN = 256
MIX_TM = 2048


def _sigmoid(v):
    return 1.0 / (1.0 + jnp.exp(-v))


def _mix_fwd(name, z, b_gate, ys, ws):
    S_ = z.shape[0]
    tm, tn = MIX_TM, MIX_TN
    gblk = OFF_GATE // tn

    def body(yp, yc, ya, wp, wc, wa, g0, g1, g2, b0, b1, b2, m_ref):
        acc = None
        for y_ref, w_ref, g_ref, b_ref in ((yp, wp, g0, b0), (yc, wc, g1, b1), (ya, wa, g2, b2)):
            u = lax.dot_general(y_ref[...].astype(BF16), w_ref[...], _DIMS["nt"], preferred_element_type=F32)
            t = _sigmoid(g_ref[...] + b_ref[...]) * u
            acc = t if acc is None else acc + t
        m_ref[...] = acc.astype(BF16)

    yspec = lambda w: pl.BlockSpec((tm, w), lambda i, j: (i, 0))
    wspec = lambda w: pl.BlockSpec((tn, w), lambda i, j: (j, 0))
    gspec = lambda b: pl.BlockSpec((tm, tn), lambda i, j, b=b: (i, gblk + b * (D // tn) + j))
    bspec = lambda b: pl.BlockSpec((1, tn), lambda i, j, b=b: (0, b * (D // tn) + j))
    return pl.pallas_call(
        body, name=name, grid=(S_ // tm, D // tn),
        in_specs=[yspec(POOL_W), yspec(CONV_W), yspec(ATTN_OUT), wspec(POOL_W), wspec(CONV_W), wspec(ATTN_OUT),
                  gspec(0), gspec(1), gspec(2), bspec(0), bspec(1), bspec(2)],
        out_specs=pl.BlockSpec((tm, tn), lambda i, j: (i, j)),
        out_shape=jax.ShapeDtypeStruct((S_, D), BF16),
        compiler_params=_cparams(("parallel", "parallel")),
    )(*ys, *ws, z, z, z, b_gate, b_gate, b_gate)


def _mix_bwd(name, z, b_gate, y, w, dmerged, branch, dz, after=None):
    S_ = z.shape[0]
    tm, tn = MIX_TM, MIX_TN
    width = y.shape[1]
    gblk = OFF_GATE // tn + branch * (D // tn)
    nj = D // tn

    def body(y_ref, w_ref, g_ref, b_ref, dm_ref, dy_ref, dw_ref, dg_ref, db_ref, acc_ref):
        i, j = pl.program_id(0), pl.program_id(1)
        yb = y_ref[...].astype(BF16)
        u = lax.dot_general(yb, w_ref[...], _DIMS["nt"], preferred_element_type=F32)
        sg = _sigmoid(g_ref[...] + b_ref[...])
        dm = dm_ref[...]
        du = (sg * dm).astype(BF16)
        dpre = dm * u * sg * (1.0 - sg)
        dg_ref[...] = dpre.astype(BF16)
        cols = pl.ds(pl.multiple_of(j * tn, tn), tn)
        db_ref[...] = _rows8(dpre)
        d_w = lax.dot_general(du, yb, _DIMS["tn"], preferred_element_type=F32)
        d_y = jnp.dot(du, w_ref[...], preferred_element_type=F32)

        @pl.when(i == 0)
        def _():
            dw_ref[cols, :] = d_w

        @pl.when(i > 0)
        def _():
            dw_ref[cols, :] += d_w

        @pl.when(j == 0)
        def _():
            acc_ref[...] = d_y

        @pl.when(j > 0)
        def _():
            acc_ref[...] += d_y

        @pl.when(j == nj - 1)
        def _():
            dy_ref[...] = acc_ref[...]

    rows = pl.BlockSpec((tm, width), lambda i, j: (i, 0))
    blk = pl.BlockSpec((tm, tn), lambda i, j: (i, j))
    gate = pl.BlockSpec((tm, tn), lambda i, j: (i, gblk + j))
    args = [y, w, z, b_gate, dmerged]
    in_specs = [rows, pl.BlockSpec((tn, width), lambda i, j: (j, 0)), gate,
                pl.BlockSpec((1, tn), lambda i, j: (0, branch * nj + j)), blk]
    aliases = {}
    if not isinstance(dz, jax.ShapeDtypeStruct):
        body = _drop_ref(body, len(args))
        aliases = {len(args): 2}
        args.append(dz)
        in_specs.append(ANY)
    return _call(
        body, args, in_specs, after, name=name, grid=(S_ // tm, nj),
        out_specs=(rows, pl.BlockSpec((D, width), lambda i, j: (0, 0)), gate, pl.BlockSpec((8, tn), lambda i, j: (i, j))),
        out_shape=(jax.ShapeDtypeStruct((S_, width), F32), jax.ShapeDtypeStruct((D, width), F32),
                   jax.ShapeDtypeStruct(dz.shape, dz.dtype), jax.ShapeDtypeStruct((S_ // tm * 8, D), F32)),
        scratch_shapes=[pltpu.VMEM((tm, width), F32)], input_output_aliases=aliases,
        compiler_params=_cparams(("arbitrary", "arbitrary")))


def _relu2_epilogue(acc):
    r = jnp.maximum(acc, 0.0)
    return (r * r,)


def _relu2_bwd_epilogue(acc, r):
    return (acc * (2.0 * jnp.sqrt(r.astype(F32))),)


def _residual_norm_epilogue(acc, xr, gain):
    x = xr + acc
    ms = jnp.mean(x * x, axis=-1, keepdims=True)
    return x, x * lax.rsqrt(ms + EPS) * gain


def _fold_rows(name, part, lanes=False, scale=1.0):
    R, N = part.shape

    def body(p_ref, o_ref):
        s = jnp.sum(p_ref[...], axis=0, keepdims=True)
        if lanes:
            s = jnp.sum(s, axis=1, keepdims=True)
        o_ref[...] = s * scale

    out_n = 1 if lanes else N
    return pl.pallas_call(
        body, name=name, grid=(1,), in_specs=[pl.BlockSpec((R, N), lambda i: (0, 0))],
        out_specs=pl.BlockSpec((1, out_n), lambda i: (0, 0)), out_shape=jax.ShapeDtypeStruct((1, out_n), F32))(part)


def _rows8(v):
    return jnp.sum(v.reshape(v.shape[0] // 8, 8, v.shape[1]), axis=0)


def _residual_loss_epilogue(acc, xr, target):
    e = xr + acc - target
    dy = e * (1.0 / D)
    return dy, _rows8(e * e), dy


def _norm_bwd_epilogue(acc, x, dres, gain):
    rstd = lax.rsqrt(jnp.mean(x * x, axis=-1, keepdims=True) + EPS)
    xhat = x * rstd
    dxhat = acc * gain
    c = jnp.mean(dxhat * xhat, axis=-1, keepdims=True)
    return dres + rstd * (dxhat - xhat * c), _rows8(acc * xhat)


def _norm_bwd_epilogue_twice(acc, x, dres, gain):
    dx, part = _norm_bwd_epilogue(acc, x, dres, gain)
    return dx, part, dx


def _layer_fwd(l, x, h, p, next_gain, target=None, early=None):
    t = f"l{l}_"
    z = _matmul(t + "in_proj", h, p["w_in_t"], "nt", 512, 3712, 1024, after=early)
    y_pool = _pool_fwd(t + "pool", z, p["pool_mix"], p["pool_scale"])
    os_, ls_ = [], []
    for g, (_, dil) in enumerate(ATTN_GROUPS):
        o_g, l_g = _attn_fwd(t + f"attn{g}", z, p["q_gain"], p["k_gain"], g, dil)
        os_.append(o_g)
        ls_.append(l_g)
    y_attn, lse = _attn_combine(t + "attn_mix", os_, ls_)
    y_conv = _conv_fwd(t + "conv", z, p["conv_w"], after=y_attn)
    merged = _mix_fwd(t + "merge", z, p["b_gate"], (y_pool, y_conv, y_attn),
                      (p["w_pool_up_t"], p["w_conv_out_t"], p["w_attn_up_t"]))
    x1, h2 = _matmul(t + "out_proj", merged, p["rest"], "nn", 1024, D, 1024, out_dtypes=(F32, BF16),
                     extras=((x, "mn"), (p["norm_mlp"], "n")), epilogue=_residual_norm_epilogue, b_rows=ROWS_WO)
    r = _matmul(t + "ff1", h2, p["rest"], "nt", 1024, 1024, 1024, out_dtypes=(BF16,), epilogue=_relu2_epilogue,
                b_rows=ROWS_FF1)
    if next_gain is None:
        dy, h_out, dy_16 = _matmul(t + "ff2", r, p["rest"], "nn", 512, D, D_FF,
                                   out_dtypes=(F32, (F32, "rows8"), BF16), extras=((x1, "mn"), (target, "mn")),
                                   epilogue=_residual_loss_epilogue, b_rows=ROWS_FF2)
        x2 = (dy, dy_16)
    else:
        x2, h_out = _matmul(t + "ff2", r, p["rest"], "nn", 512, D, D_FF, out_dtypes=(F32, BF16),
                            extras=((x1, "mn"), (next_gain, "n")), epilogue=_residual_norm_epilogue,
                            b_rows=ROWS_FF2)
    saved = dict(x=x, h=h, z=z, y_pool=y_pool, y_conv=y_conv, y_attn=y_attn, lse=lse, merged=merged,
                 x1=x1, h2=h2, r=r)
    return x2, h_out, saved


def _layer_bwd(l, dx2, p, s, pending, in_t_all, collective_id, start_after=None):
    t = f"l{l}_b_"
    g = {}
    rest = jax.ShapeDtypeStruct((N_DEV, REST_ROWS, D), F32)
    dx2, dx2_16 = dx2
    da = _matmul(t + "d_ff2_in", dx2_16, p["rest"], "nt", 1024, 1024, 1024, out_dtypes=(BF16,),
                 extras=((s["r"], "mn"),), epilogue=_relu2_bwd_epilogue,
                 after=start_after, b_rows=ROWS_FF2)
    rest = _matmul(t + "dw_ff2", s["r"], dx2_16, "tn", 512, 1024, 4096, into=(rest, REST_FF2, 0), after=da)
    tok = rest if pending is None else pending.chip_sums(after=rest)
    dx1, part, dx1_16 = _matmul(
        t + "d_ff1_in", da, p["rest"], "nn", 512, D, D_FF, out_dtypes=(F32, (F32, "rows8"), BF16),
        extras=((s["x1"], "mn"), (dx2, "mn"), (p["norm_mlp"], "n")), epilogue=_norm_bwd_epilogue_twice,
        after=tok, b_rows=ROWS_FF1)
    g["norm_mlp"] = _fold_rows(t + "d_norm_mlp", part)
    rest = _matmul(t + "dw_ff1", da, s["h2"], "tn", 512, 1024, 4096, into=(rest, REST_FF1, 0), after=dx1)
    dmerged = _matmul(t + "d_out_proj_in", dx1_16, p["rest"], "nt", 1024, 1024, 1024, b_rows=ROWS_WO)
    rest = _matmul(t + "dw_o", s["merged"], dx1_16, "tn", 128, 1024, 4096, into=(rest, REST_WO, 0), after=dmerged)
    ys = (s["y_pool"], s["y_conv"], s["y_attn"])
    names = ("w_pool_up_t", "w_conv_out_t", "w_attn_up_t")
    dys, dbs = [], []
    tok = rest
    dz = jax.ShapeDtypeStruct((dx2.shape[0], IN_COLS), BF16)
    for b in range(3):
        dy_b, dw_t, dz, db = _mix_bwd(t + f"merge{b}", s["z"], p["b_gate"], ys[b], p[names[b]], dmerged, b, dz,
                                      after=tok)
        tok = dy_b
        width = ys[b].shape[1]
        if b < 2:
            rest = rest.at[:, REST_UP:REST_ATTN, b * width:(b + 1) * width].set(
                dw_t.reshape(N_DEV, REST_ATTN - REST_UP, width))
        else:
            rest = rest.at[:, REST_ATTN:REST_ROWS, :].set(dw_t.reshape(N_DEV, REST_ROWS - REST_ATTN, D))
        dys.append(dy_b)
        dbs.append(_fold_rows(t + f"d_b_gate{b}", db))
    g["b_gate"] = jnp.concatenate(dbs, axis=1)
    rs_rest = _ReduceScatter(f"rs_rest{l}", collective_id, rest)
    pending_sum = None if pending is None else pending.finish(after=rest, into=(in_t_all, l + 1))
    dz, g["pool_mix"], g["pool_scale"] = _pool_bwd(t + "pool", s["z"], dys[0], p["pool_mix"], p["pool_scale"], dz,
                                                   after=rest if pending is None else pending_sum[0])
    dcb, dcc, dcx, g["conv_w"] = _conv_bwd(t + "conv", s["z"], dys[1], p["conv_w"], after=dz)
    tok = rs_rest.chip_sums(after=dcb)
    dqs, dks, dvs = [], [], []
    gq = gk = None
    for gi, (_, dil) in enumerate(ATTN_GROUPS):
        dq, dk, dv, dgq, dgk = _attn_bwd(t + f"attn{gi}", s["z"], p["q_gain"], p["k_gain"], dys[2], s["y_attn"],
                                         s["lse"], gi, dil, after=tok)
        tok = dq
        dqs.append(dq)
        dks.append(dk)
        dvs.append(dv)
        gq = dgq if gq is None else gq + dgq
        gk = dgk if gk is None else gk + dgk
    g["q_gain"], g["k_gain"] = gq, gk
    rest_sum, _ = rs_rest.finish(after=tok)
    col = OFF_CB
    for piece in [dcb, dcc, dcx] + dqs + dks + dvs:
        dz = lax.dynamic_update_slice(dz, piece.astype(BF16), (0, col))
        col += piece.shape[1]
    in_t = _matmul(t + "dw_in", dz, s["h"], "tn", 256, 1024, 4096, after=rest_sum)
    rs_in = _ReduceScatter(f"rs_in{l}", collective_id + 2, in_t.reshape(N_DEV, IN_ROWS, D))
    dx = _matmul(t + "d_in_proj_in", dz, p["w_in_t"], "nn", 256, D, IN_COLS,
                 out_dtypes=(F32, (F32, "rows8")) + ((BF16,) if l > 0 else ()),
                 extras=((s["x"], "mn"), (dx1, "mn"), (p["norm_mix"], "n")),
                 epilogue=_norm_bwd_epilogue_twice if l > 0 else _norm_bwd_epilogue, after=in_t)
    dx, part = (dx[0], dx[2]) if l > 0 else (dx[0], None), dx[1]
    g["norm_mix"] = _fold_rows(t + "d_norm_mix", part)
    return dx, g, rest_sum, in_t_all if pending is None else pending_sum[0], rs_in


ANY = pl.BlockSpec(memory_space=pl.ANY)


def _mesh_pos():
    return lax.axis_index("x"), lax.axis_index("y"), lax.axis_index("c")


def _other_chips(x, y):
    return [(1 - x, y), (x, 1 - y), (1 - x, 1 - y)]


def _comm_call(name, collective_id, peers, body, arrs, out_shape, sem_counts, after=None):
    n_in, n_out = len(arrs), len(out_shape)
    if collective_id is None:
        def tc_body(*refs):
            body(refs[:n_in], refs[n_in:n_in + n_out], *refs[n_in + n_out:])

        return pl.pallas_call(
            tc_body, name=name, out_shape=tuple(out_shape), in_specs=[ANY] * n_in, out_specs=(ANY,) * n_out,
            scratch_shapes=[pltpu.SemaphoreType.DMA((n,)) for n in sem_counts])(*arrs)

    n_after = 0 if after is None else 1

    def seq_body(*refs):
        barrier = pltpu.get_barrier_semaphore()
        ps = peers(*_mesh_pos())
        for p in ps:
            pl.semaphore_signal(barrier, inc=1, device_id=p, device_id_type=MESH)
        pl.semaphore_wait(barrier, len(ps))
        outs = refs[n_in + n_after:]
        body(refs[:n_in], outs[:n_out], *outs[n_out:])

    return pl.kernel(
        seq_body, out_type=tuple(out_shape), mesh=plsc.ScalarSubcoreMesh(axis_name="seq", num_cores=1), name=name,
        scratch_types=[pltpu.SemaphoreType.DMA((n,)) for n in sem_counts],
        compiler_params=pltpu.CompilerParams(collective_id=collective_id),
    )(*arrs, *([] if after is None else [after]))


def _ordered(name, x, after):
    def body(x_ref, after_ref, o_ref):
        del x_ref, after_ref, o_ref

    return pl.pallas_call(
        body, name=name, out_shape=jax.ShapeDtypeStruct(x.shape, x.dtype), in_specs=[ANY, ANY], out_specs=ANY,
        input_output_aliases={0: 0})(x, after)


def _all_gather(name, collective_id, shard, after=None):
    R, C = shard.shape

    def peers(x, y, c):
        return [(x, y, 1 - c)] + [(*chip, c) for chip in _other_chips(x, y)]

    def body(in_refs, out_refs, send_sems, recv_sems, local_sems):
        (x_ref,), (out_ref,) = in_refs, out_refs
        x, y, c = _mesh_pos()
        me, sibling = (x, y, c), (x, y, 1 - c)
        chips = _other_chips(x, y)

        def slot(px, py, pc):
            return out_ref.at[4 * px + 2 * py + pc]

        def copy(k, block, to, src=None):
            return pltpu.make_async_remote_copy(
                src_ref=slot(*block) if src is None else src, dst_ref=slot(*block),
                send_sem=send_sems.at[k], recv_sem=recv_sems.at[k], device_id=to, device_id_type=MESH)

        mine = pltpu.make_async_copy(x_ref, slot(*me), local_sems.at[0])
        mine.start()
        first = [copy(0, me, sibling, src=x_ref)]
        first += [copy(1 + j, me, (*chip, c), src=x_ref) for j, chip in enumerate(chips)]
        for cp in first:
            cp.start()
        passed = [copy(4 + j, (*chip, c), sibling) for j, chip in enumerate(chips)]
        for j, chip in enumerate(chips):
            copy(1 + j, (*chip, c), me).wait_recv()
            passed[j].start()
        copy(0, sibling, me).wait_recv()
        for j, chip in enumerate(chips):
            copy(4 + j, (*chip, 1 - c), me).wait_recv()
        for cp in first + passed:
            cp.wait_send()
        mine.wait()

    return _comm_call(name, collective_id, peers, body, [shard],
                      [jax.ShapeDtypeStruct((N_DEV, R, C), shard.dtype)], (7, 7, 1), after)[0]


def _rs_sibling_exchange(name, collective_id, arrs):
    n = len(arrs)

    def body(in_refs, out_refs, send_sems, recv_sems):
        x, y, c = _mesh_pos()
        cps = []
        for k, (src, dst) in enumerate(zip(in_refs, out_refs)):
            src = src.at[:, 1 - c] if len(src.shape) == 4 else src
            cps.append(pltpu.make_async_remote_copy(src_ref=src, dst_ref=dst, send_sem=send_sems.at[k],
                                                    recv_sem=recv_sems.at[k], device_id=(x, y, 1 - c),
                                                    device_id_type=MESH))
        for cp in cps:
            cp.start()
        for cp in cps:
            cp.wait()

    out_shape = [jax.ShapeDtypeStruct(a.shape[:1] + a.shape[2:] if a.ndim == 4 else a.shape, a.dtype) for a in arrs]
    return _comm_call(name, collective_id, lambda x, y, c: [(x, y, 1 - c)], body, arrs, out_shape, (n, n))


def _rs_rows(r):
    return r // 2 if (r // 2) % 16 == 0 else r


def _drop_ref(body, idx):
    def wrapped(*refs):
        return body(*refs[:idx], *refs[idx + 1:])

    return wrapped


def _rs_chip_sum(name, ids, big, rbig, after=None):
    _, _, R, C = big.shape
    rows = _rs_rows(R)

    def body(ids_ref, mine_ref, theirs_ref, t16_ref, own_ref):
        p = pl.program_id(1)
        s = mine_ref[...] + theirs_ref[...]
        t16_ref[...] = s.astype(BF16)

        @pl.when(p == ids_ref[1])
        def _():
            own_ref[...] = s

    in_specs = [pl.BlockSpec((None, None, rows, C), lambda i, p, ids: (p, ids[0], i, 0)),
                pl.BlockSpec((None, rows, C), lambda i, p, ids: (p, i, 0))]
    args = [ids, big, rbig]
    if after is not None:
        body = _drop_ref(body, len(args))
        in_specs.append(ANY)
        args.append(after)
    grid_spec = pltpu.PrefetchScalarGridSpec(
        num_scalar_prefetch=1, grid=(R // rows, 4), in_specs=in_specs,
        out_specs=(pl.BlockSpec((None, rows, C), lambda i, p, ids: (p, i, 0)),
                   pl.BlockSpec((rows, C), lambda i, p, ids: (i, 0))))
    return pl.pallas_call(
        body, name=name, grid_spec=grid_spec,
        out_shape=(jax.ShapeDtypeStruct((4, R, C), BF16), jax.ShapeDtypeStruct((R, C), F32)),
        compiler_params=_cparams(("parallel", "arbitrary")),
    )(*args)


def _add2(name, a, b):
    def body(a_ref, b_ref, o_ref):
        o_ref[...] = a_ref[...] + b_ref[...]

    full = pl.BlockSpec(a.shape, lambda i: (0, 0))
    return pl.pallas_call(body, name=name, grid=(1,), in_specs=[full, full], out_specs=full,
                          out_shape=jax.ShapeDtypeStruct(a.shape, a.dtype))(a, b)


def _rs_chip_exchange(name, collective_id, arrs):
    n = len(arrs)

    def body(in_refs, out_refs, send_sems, recv_sems, local_sems):
        x, y, c = _mesh_pos()
        p_me = 2 * x + y
        chips = _other_chips(x, y)

        def part(ref, p):
            return ref.at[p] if len(ref.shape) == 3 else ref

        local = [pltpu.make_async_copy(part(src, p_me), dst.at[p_me], local_sems.at[k])
                 for k, (src, dst) in enumerate(zip(in_refs, out_refs))]
        for cp in local:
            cp.start()
        sends = []
        for j, (px, py) in enumerate(chips):
            for k, (src, dst) in enumerate(zip(in_refs, out_refs)):
                sends.append(pltpu.make_async_remote_copy(
                    src_ref=part(src, 2 * px + py), dst_ref=dst.at[p_me], send_sem=send_sems.at[n * j + k],
                    recv_sem=recv_sems.at[n * j + k], device_id=(px, py, c), device_id_type=MESH))
        for cp in sends:
            cp.start()
        for j, (px, py) in enumerate(chips):
            for k, (src, dst) in enumerate(zip(in_refs, out_refs)):
                pltpu.make_async_remote_copy(
                    src_ref=part(src, p_me), dst_ref=dst.at[2 * px + py], send_sem=send_sems.at[n * j + k],
                    recv_sem=recv_sems.at[n * j + k], device_id=(px, py, c), device_id_type=MESH).wait_recv()
        for cp in sends:
            cp.wait_send()
        for cp in local:
            cp.wait()

    out_shape = [jax.ShapeDtypeStruct((4,) + a.shape[-2:], a.dtype) for a in arrs]
    return _comm_call(name, collective_id, lambda x, y, c: [(*chip, c) for chip in _other_chips(x, y)], body, arrs,
                      out_shape, (3 * n, 3 * n, n))


def _rs_final_sum(name, ids, recv, own, rows, after=None, into=None):
    _, R, C = recv.shape
    assert R % rows == 0

    def body(ids_ref, r_ref, own_ref, o_ref):
        acc = None
        for p in range(4):
            term = jnp.where(ids_ref[1] == p, own_ref[...], r_ref[p].astype(F32))
            acc = term if acc is None else acc + term
        o_ref[...] = acc

    in_specs = [pl.BlockSpec((4, rows, C), lambda i, ids: (0, i, 0)), pl.BlockSpec((rows, C), lambda i, ids: (i, 0))]
    args = [ids, recv, own]
    if after is not None:
        body = _drop_ref(body, len(args))
        in_specs.append(ANY)
        args.append(after)
    out_spec, out_shape, aliases = pl.BlockSpec((rows, C), lambda i, ids: (i, 0)), jax.ShapeDtypeStruct((R, C), F32), {}
    if into is not None:
        buf, slot = into
        out_spec = pl.BlockSpec((None, rows, C), lambda i, ids: (slot, i, 0))
        out_shape = jax.ShapeDtypeStruct(buf.shape, F32)
        if not isinstance(buf, jax.ShapeDtypeStruct):
            body = _drop_ref(body, len(args))
            aliases = {len(args): 0}
            in_specs.append(ANY)
            args.append(buf)
    grid_spec = pltpu.PrefetchScalarGridSpec(
        num_scalar_prefetch=1, grid=(R // rows,), in_specs=in_specs, out_specs=out_spec)
    return pl.pallas_call(
        body, name=name, grid_spec=grid_spec, out_shape=out_shape, input_output_aliases=aliases,
        compiler_params=_cparams(("parallel",)),
    )(*args)


class _ReduceScatter:
    def __init__(self, tag, collective_id, big, small=None):
        x, y, c = _mesh_pos()
        self.tag, self.cid, self.small = tag, collective_id, small
        self.ids = jnp.stack([c, 2 * x + y]).astype(jnp.int32)
        self.big = big.reshape((4, 2) + big.shape[1:])
        self.got = _rs_sibling_exchange(tag + "_sibling", collective_id,
                                        [self.big] + ([] if small is None else [small]))

    def chip_sums(self, after=None):
        t16, self.own = _rs_chip_sum(self.tag + "_chip_sum", self.ids, self.big, self.got[0], after)
        arrs = [t16]
        if self.small is not None:
            self.ts = _add2(self.tag + "_chip_sum_small", self.small, self.got[1])
            arrs.append(self.ts)
        self.recv = _rs_chip_exchange(self.tag + "_chips", self.cid + 1, arrs)
        return t16

    def finish(self, after=None, into=None):
        out = _rs_final_sum(self.tag + "_final", self.ids, self.recv[0], self.own, _rs_rows(self.own.shape[0]), after,
                            into)
        out_small = None
        if self.small is not None:
            out_small = _rs_final_sum(self.tag + "_final_small", self.ids, self.recv[1], self.ts, self.small.shape[0])
        return out, out_small


def _all_reduce_small(tag, small, collective_id):
    x, y, c = _mesh_pos()
    ids = jnp.stack([c, 2 * x + y]).astype(jnp.int32)
    (theirs,) = _rs_sibling_exchange(tag + "_sibling", None, [small])
    ts = _add2(tag + "_chip_sum", small, theirs)
    (recv,) = _rs_chip_exchange(tag + "_chips", collective_id, [ts])
    return ts, lambda after: _rs_final_sum(tag + "_final", ids, recv, ts, small.shape[0], after)


def _adamw(name, w, g, m, v):
    R, C = w.shape
    tiles = [t for t in range(8, 513, 8) if R % t == 0]
    tr = max(tiles) if tiles else R
    c1 = 1.0 - ADAM_B1 ** ADAM_STEP
    c2 = 1.0 - ADAM_B2 ** ADAM_STEP

    def body(w_ref, g_ref, m_ref, v_ref, d_ref, nm_ref, nv_ref):
        gv = g_ref[...]
        nm = ADAM_B1 * m_ref[...] + (1.0 - ADAM_B1) * gv
        nv = ADAM_B2 * v_ref[...] + (1.0 - ADAM_B2) * (gv * gv)
        d_ref[...] = -ADAM_LR * ((nm / c1) / (jnp.sqrt(nv / c2) + ADAM_EPS) + ADAM_WD * w_ref[...])
        nm_ref[...] = nm
        nv_ref[...] = nv

    blk = pl.BlockSpec((tr, C), lambda i: (i, 0))
    shp = jax.ShapeDtypeStruct((R, C), F32)
    return pl.pallas_call(
        body, name=name, grid=(R // tr,), in_specs=[blk] * 4, out_specs=(blk,) * 3, out_shape=(shp,) * 3,
        compiler_params=_cparams(("parallel",)),
    )(w, g, m, v)


def _adamw_t(name, w, g_t, m, v):
    L, R, C = w.shape
    tr = 256
    c1 = 1.0 - ADAM_B1 ** ADAM_STEP
    c2 = 1.0 - ADAM_B2 ** ADAM_STEP

    def body(w_ref, g_ref, m_ref, v_ref, go_ref, d_ref, nm_ref, nv_ref):
        gv = g_ref[...].T
        go_ref[...] = gv
        nm = ADAM_B1 * m_ref[...] + (1.0 - ADAM_B1) * gv
        nv = ADAM_B2 * v_ref[...] + (1.0 - ADAM_B2) * (gv * gv)
        d_ref[...] = -ADAM_LR * ((nm / c1) / (jnp.sqrt(nv / c2) + ADAM_EPS) + ADAM_WD * w_ref[...])
        nm_ref[...] = nm
        nv_ref[...] = nv

    blk = pl.BlockSpec((None, tr, C), lambda l, i: (l, i, 0))
    blk_t = pl.BlockSpec((None, C, tr), lambda l, i: (l, 0, i))
    shp = jax.ShapeDtypeStruct((L, R, C), F32)
    return pl.pallas_call(
        body, name=name, grid=(L, R // tr), in_specs=[blk, blk_t, blk, blk], out_specs=(blk,) * 4,
        out_shape=(shp,) * 4, compiler_params=_cparams(("parallel", "parallel")),
    )(w, g_t, m, v)


REST_FF1 = 0
REST_FF2 = 512
REST_WO = 1024
REST_UP = 1152
REST_ATTN = 1280
REST_ROWS = 1312
IN_ROWS = IN_COLS // N_DEV
ROWS_FF1 = (REST_FF1, D_FF // N_DEV)
ROWS_FF2 = (REST_FF2, D_FF // N_DEV)
ROWS_WO = (REST_WO, D // N_DEV)
SMALL = (("norm_mix", (DEPTH, D)), ("b_gate", (DEPTH, 3 * D)), ("pool_mix", (DEPTH, 4, POOL_G, POOL_G)),
         ("pool_scale", (DEPTH, POOL_W)), ("conv_w", (DEPTH, 3, CONV_W)), ("q_gain", (DEPTH, HEAD)),
         ("k_gain", (DEPTH, HEAD)), ("norm_mlp", (DEPTH, D)))


def _pack_weight_shards(w, l):
    b = lambda a: a.astype(BF16)
    rest = jnp.concatenate([
        b(w["w_ff1"][l].T), b(w["w_ff2"][l]), b(w["w_o"][l]),
        jnp.concatenate([b(w["w_pool_up"][l].T), b(w["w_conv_out"][l].T)], axis=1),
        b(w["w_attn_up"][l].T).reshape(REST_ROWS - REST_ATTN, D)], axis=0)
    return b(w["w_in"][l].T), rest


def _unpack_gathered(g_in, g_rest, g_conv, small_w, l):
    rest = g_rest
    take = lambda r0, rows, c0=0, cols=D: rest[:, r0:r0 + rows, c0:c0 + cols].reshape(N_DEV * rows, cols)
    conv = g_conv[:, 3 * l:3 * l + 3, :CONV_W // N_DEV]
    p = {
        "w_in_t": g_in.reshape(IN_COLS, D),
        "rest": rest,
        "w_pool_up_t": take(REST_UP, 128, 0, POOL_W), "w_conv_out_t": take(REST_UP, 128, POOL_W, CONV_W),
        "w_attn_up_t": rest[:, REST_ATTN:].reshape(D, ATTN_OUT),
        "conv_w": jnp.transpose(conv, (1, 0, 2)).reshape(3, CONV_W),
        "pool_mix": small_w["pool_mix"][l],
    }
    for name in ("norm_mix", "b_gate", "pool_scale", "q_gain", "k_gain", "norm_mlp"):
        p[name] = small_w[name][l][None, :]
    return p


def _pack_small_grads(grads, loss):
    flat = jnp.concatenate([jnp.stack([grads[l][name] for l in range(DEPTH)]).reshape(-1) for name, _ in SMALL]
                           + [loss.reshape(1)])
    return jnp.pad(flat, (0, (-flat.shape[0]) % (8 * 128))).reshape(-1, 128)


def _unpack_grads(rest):
    return {
        "w_ff1_t": jnp.stack([a[REST_FF1:REST_FF1 + 512] for a in rest]),
        "w_ff2": jnp.stack([a[REST_FF2:REST_FF2 + 512] for a in rest]),
        "w_o": jnp.stack([a[REST_WO:REST_WO + 128] for a in rest]),
        "w_pool_up": jnp.stack([a[REST_UP:REST_UP + 128, :POOL_W].T for a in rest]),
        "w_conv_out": jnp.stack([a[REST_UP:REST_UP + 128, POOL_W:].T for a in rest]),
        "w_attn_up": jnp.stack([a[REST_ATTN:].reshape(D // N_DEV, ATTN_OUT).T for a in rest]),
    }


def _unpack_small_like_grads(small, dev):
    out = {}
    flat = small.reshape(-1)
    off = 0
    for name, shp in SMALL:
        n = 1
        for s in shp:
            n *= s
        out[name] = flat[off:off + n].reshape(shp)
        off += n
    out["loss"] = flat[off]
    width = CONV_W // N_DEV
    out["conv_w"] = lax.dynamic_slice_in_dim(out["conv_w"], dev * width, width, axis=2)
    return out


def _pack_small_like_grads(src, dev):
    parts = []
    for name, shp in SMALL:
        a = src[name]
        if name == "conv_w":
            a = lax.dynamic_update_slice_in_dim(jnp.zeros(shp, F32), a, dev * (CONV_W // N_DEV), axis=2)
        parts.append(a.reshape(-1))
    flat = jnp.concatenate(parts)
    return jnp.pad(flat, (0, (-flat.shape[0]) % (8 * 128))).reshape(-1, 128)


WEIGHTS = ("norm_mix", "w_in", "b_gate", "pool_mix", "pool_scale", "conv_w", "q_gain", "k_gain", "w_pool_up",
           "w_conv_out", "w_attn_up", "w_o", "norm_mlp", "w_ff1", "w_ff2")


def kernel(x, norm_mix, w_in, b_gate, pool_mix, pool_scale, conv_w, q_gain, k_gain, w_pool_up, w_conv_out, w_attn_up, w_o, norm_mlp, w_ff1, w_ff2, loss_target, m_norm_mix, m_w_in, m_b_gate, m_pool_mix, m_pool_scale, m_conv_w, m_q_gain, m_k_gain, m_w_pool_up, m_w_conv_out, m_w_attn_up, m_w_o, m_norm_mlp, m_w_ff1, m_w_ff2, v_norm_mix, v_w_in, v_b_gate, v_pool_mix, v_pool_scale, v_conv_w, v_q_gain, v_k_gain, v_w_pool_up, v_w_conv_out, v_w_attn_up, v_w_o, v_norm_mlp, v_w_ff1, v_w_ff2):
    w = dict(zip(WEIGHTS, (norm_mix, w_in, b_gate, pool_mix, pool_scale, conv_w, q_gain, k_gain, w_pool_up,
                           w_conv_out, w_attn_up, w_o, norm_mlp, w_ff1, w_ff2)))
    m = dict(zip(WEIGHTS, (m_norm_mix, m_w_in, m_b_gate, m_pool_mix, m_pool_scale, m_conv_w, m_q_gain, m_k_gain,
                           m_w_pool_up, m_w_conv_out, m_w_attn_up, m_w_o, m_norm_mlp, m_w_ff1, m_w_ff2)))
    v = dict(zip(WEIGHTS, (v_norm_mix, v_w_in, v_b_gate, v_pool_mix, v_pool_scale, v_conv_w, v_q_gain, v_k_gain,
                           v_w_pool_up, v_w_conv_out, v_w_attn_up, v_w_o, v_norm_mlp, v_w_ff1, v_w_ff2)))
    xi, yi, ci = _mesh_pos()
    dev = 4 * xi + 2 * yi + ci

    saved, params = [], []
    act = x[0]
    w_in_t, m_in_t, v_in_t = (jnp.swapaxes(src["w_in"], 1, 2) for src in (w, m, v))
    h = _rmsnorm_fwd("l0_norm_mix", act, w["norm_mix"][0][None])
    conv_shard = jnp.pad(w["conv_w"].reshape(DEPTH * 3, CONV_W // N_DEV), ((0, 8 - DEPTH * 3), (0, 128 - CONV_W // N_DEV)))
    g_conv = _all_gather("gather_conv", None, conv_shard)
    for l in range(DEPTH):
        s_in, s_rest = _pack_weight_shards(w, l)
        g_in = _all_gather(f"gather_in{l}", 1 + 2 * l, s_in)
        g_rest = _all_gather(f"gather_rest{l}", 2 + 2 * l, s_rest)
        params.append(_unpack_gathered(g_in, g_rest, g_conv, w, l))
    for l in range(DEPTH):
        act, h, s = _layer_fwd(l, act, h, params[l], w["norm_mix"][l + 1][None] if l + 1 < DEPTH else None,
                               loss_target[0])
        saved.append(s)
    dy = act
    loss = _fold_rows("loss_sum", h, lanes=True, scale=0.5 / D)
    rest, grads, pending = [None] * DEPTH, [None] * DEPTH, None
    in_t = jax.ShapeDtypeStruct((DEPTH, IN_ROWS, D), F32)
    for l in reversed(range(DEPTH)):
        dy, grads[l], rest[l], in_t, pending = _layer_bwd(l, dy, params[l], saved[l], pending, in_t, 5 + 4 * l)
    dx = dy[0]
    small_chip_sum, small_finish = _all_reduce_small("ar_small", _pack_small_grads(grads, loss), 5 + 4 * DEPTH)
    pending.chip_sums(after=small_chip_sum)
    g = _unpack_grads(rest)

    delta, new_m, new_v = {}, {}, {}
    g["w_ff1"], delta["w_ff1"], new_m["w_ff1"], new_v["w_ff1"] = _adamw_t("adamw_w_ff1", w["w_ff1"], g["w_ff1_t"],
                                                                          m["w_ff1"], v["w_ff1"])
    for name in ("w_pool_up", "w_conv_out", "w_attn_up", "w_o", "w_ff2"):
        shp = w[name].shape
        two_d = (shp[0] * shp[1], shp[2])
        d_, m_, v_ = _adamw("adamw_" + name, w[name].reshape(two_d), g[name].reshape(two_d),
                            m[name].reshape(two_d), v[name].reshape(two_d))
        delta[name], new_m[name], new_v[name] = d_.reshape(shp), m_.reshape(shp), v_.reshape(shp)
    g.update(_unpack_small_like_grads(small_finish(after=new_v["w_ff2"]), dev))
    total = g.pop("loss")
    for name, shp in SMALL:
        shp = w[name].shape
        two_d = (-1, shp[-1])
        d_, m_, v_ = _adamw("adamw_" + name, w[name].reshape(two_d), g[name].reshape(two_d),
                            m[name].reshape(two_d), v[name].reshape(two_d))
        delta[name], new_m[name], new_v[name] = d_.reshape(shp), m_.reshape(shp), v_.reshape(shp)
    g_t, _ = pending.finish(after=new_v[SMALL[-1][0]], into=(in_t, 0))
    two_d = (g_t.shape[0] * g_t.shape[1], g_t.shape[2])
    d_, m_, v_ = _adamw("adamw_w_in", *[a.reshape(two_d) for a in (w_in_t, g_t, m_in_t, v_in_t)])
    g["w_in"] = jnp.swapaxes(g_t, 1, 2)
    delta["w_in"], new_m["w_in"], new_v["w_in"] = (jnp.swapaxes(a.reshape(g_t.shape), 1, 2) for a in (d_, m_, v_))

    return (total, dx[None], *[g[n] for n in WEIGHTS], *[delta[n] for n in WEIGHTS],
            *[new_m[n] for n in WEIGHTS], *[new_v[n] for n in WEIGHTS])
```

```python
import functools

import jax
import jax.numpy as jnp
from jax import lax
from jax.experimental import pallas as pl
from jax.experimental.pallas import tpu as pltpu
from jax.experimental.pallas import tpu_sc as plsc

F32 = jnp.float32
BF16 = jnp.bfloat16
MESH = pl.DeviceIdType.MESH

D = 1024
SEQ = 4096
DEPTH = 2
N_DEV = 8
POOL_WINDOWS = (2, 4, 8, 16)
POOL_W = 512
POOL_G = 128
CONV_W = 512
HEAD = 64
ATTN_GROUPS = ((128, 1), (512, 4), (2048, 16))
HPG = 4
ATTN_W = 768
ATTN_OUT = 256
ATTN_BLK = 128
D_FF = 4096
EPS = 1e-6
MASK = -1e30
OFF_POOL = 0
OFF_CB = 512
OFF_CC = 1024
OFF_CX = 1536
OFF_Q = 2048
OFF_K = 2816
OFF_V = 3584
OFF_GATE = 4352
IN_COLS = 7424
ADAM_LR = 0.001
ADAM_B1 = 0.9
ADAM_B2 = 0.999
ADAM_EPS = 1e-08
ADAM_WD = 0.01
ADAM_STEP = 10

ROW_TILE = 512
SEQ_CHUNK = 256
HALO = 16
VMEM_LIMIT = 56 * 1024 * 1024


def _cparams(sem=None):
    return pltpu.CompilerParams(dimension_semantics=sem, vmem_limit_bytes=VMEM_LIMIT)


def _call(body, args, in_specs, after=None, **kw):
    if after is not None:
        after = list(after) if isinstance(after, (list, tuple)) else [after]
        n, k, inner = len(args), len(after), body

        def body(*refs):
            return inner(*refs[:n], *refs[n + k:])

        args = list(args) + after
        in_specs = list(in_specs) + [pl.BlockSpec(memory_space=pl.ANY)] * k
    return pl.pallas_call(body, in_specs=in_specs, **kw)(*args)


_DIMS = {"nn": (((1,), (0,)), ((), ())), "nt": (((1,), (1,)), ((), ())), "tn": (((0,), (0,)), ((), ()))}


def _matmul(name, a, b, mode, tm, tn, tk, out_dtypes=(F32,), extras=(), epilogue=None, into=None, after=None,
            b_rows=None):
    if mode == "tn":
        K, M = a.shape
    else:
        M, K = a.shape
    b_shape = b.shape if b_rows is None else (b.shape[0] * b_rows[1], b.shape[2])
    N = b_shape[0] if mode == "nt" else b_shape[1]
    assert b_shape[1 if mode == "nt" else 0] == K, (name, b_shape, K)
    assert M % tm == 0 and N % tn == 0 and K % tk == 0, (name, M, N, K, tm, tn, tk)
    nk = K // tk
    n_extra = len(extras)
    n_out = len(out_dtypes)
    dims = _DIMS[mode]
    n_alias = 0 if into is None or isinstance(into[0], jax.ShapeDtypeStruct) else 1

    def body(a_ref, b_ref, *rest):
        extra_refs = rest[:n_extra]
        out_refs = rest[n_extra + n_alias:n_extra + n_alias + n_out]

        def finish(acc):
            if epilogue is None:
                res = (acc,)
            else:
                res = epilogue(acc, *[r[...] for r in extra_refs])
            for o_ref, v in zip(out_refs, res):
                o_ref[...] = v.astype(o_ref.dtype)

        bv = b_ref[...]
        if b_rows is not None:
            bv = bv.reshape(-1, bv.shape[-1])
        part = lax.dot_general(a_ref[...].astype(BF16), bv.astype(BF16), dims, preferred_element_type=F32)
        if nk == 1:
            finish(part)
        else:
            acc_ref = rest[-1]
            k = pl.program_id(2)

            @pl.when(k == 0)
            def _():
                acc_ref[...] = part

            @pl.when(k > 0)
            def _():
                acc_ref[...] += part

            @pl.when(k == nk - 1)
            def _():
                finish(acc_ref[...])

    if mode == "tn":
        a_spec = pl.BlockSpec((tk, tm), lambda i, j, k: (k, i))
    else:
        a_spec = pl.BlockSpec((tm, tk), lambda i, j, k: (i, k))
    if b_rows is not None:
        row0, rows = b_rows
        t_rows = tn if mode == "nt" else tk
        assert t_rows % rows == 0 and row0 % rows == 0, (name, t_rows, b_rows)
        if mode == "nt":
            b_spec = pl.BlockSpec((t_rows // rows, rows, tk), lambda i, j, k: (j, row0 // rows, k))
        else:
            b_spec = pl.BlockSpec((t_rows // rows, rows, tn), lambda i, j, k: (k, row0 // rows, j))
    elif mode == "nt":
        b_spec = pl.BlockSpec((tn, tk), lambda i, j, k: (j, k))
    else:
        b_spec = pl.BlockSpec((tk, tn), lambda i, j, k: (k, j))
    in_specs = [a_spec, b_spec]
    args = [a, b]
    for arr, kind in extras:
        if kind == "mn":
            in_specs.append(pl.BlockSpec((tm, tn), lambda i, j, k: (i, j)))
        else:
            in_specs.append(pl.BlockSpec((1, tn), lambda i, j, k: (0, j)))
        args.append(arr)
    part = [isinstance(dt, tuple) for dt in out_dtypes]
    out_shape = tuple(jax.ShapeDtypeStruct((M // tm * 8, N), dt[0]) if p else jax.ShapeDtypeStruct((M, N), dt)
                      for dt, p in zip(out_dtypes, part))
    out_specs = tuple(pl.BlockSpec((8 if p else tm, tn), lambda i, j, k: (i, j)) for p in part)
    aliases = {}
    if into is not None:
        buf, row0, col0 = into
        assert n_out == 1 and (M // N_DEV) % tm == 0 and row0 % tm == 0 and col0 % tn == 0
        per_dev = M // N_DEV // tm
        out_shape = (jax.ShapeDtypeStruct(buf.shape, buf.dtype),)
        out_specs = (pl.BlockSpec((None, tm, tn), lambda i, j, k: (i // per_dev, row0 // tm + i % per_dev,
                                                                   col0 // tn + j)),)
        if not isinstance(buf, jax.ShapeDtypeStruct):
            aliases = {len(args): 0}
            in_specs.append(pl.BlockSpec(memory_space=pl.ANY))
            args.append(buf)
    scratch = [] if nk == 1 else [pltpu.VMEM((tm, tn), F32)]
    res = _call(
        body, args, in_specs, after, name=name, grid=(M // tm, N // tn, nk), out_specs=out_specs,
        out_shape=out_shape, scratch_shapes=scratch, input_output_aliases=aliases,
        compiler_params=_cparams(("parallel", "parallel", "arbitrary")))
    return res if n_out > 1 else res[0]


def _rmsnorm_fwd(name, x, gain, after=None):
    S_, D_ = x.shape

    def body(x_ref, g_ref, h_ref):
        xf = x_ref[...]
        ms = jnp.mean(xf * xf, axis=-1, keepdims=True)
        h_ref[...] = (xf * lax.rsqrt(ms + EPS) * g_ref[...]).astype(BF16)

    return _call(
        body, [x, gain], [pl.BlockSpec((ROW_TILE, D_), lambda i: (i, 0)), pl.BlockSpec((1, D_), lambda i: (0, 0))],
        after, name=name, grid=(S_ // ROW_TILE,),
        out_specs=pl.BlockSpec((ROW_TILE, D_), lambda i: (i, 0)),
        out_shape=jax.ShapeDtypeStruct((S_, D_), BF16),
        compiler_params=_cparams(("parallel",)))


def _rows_with_halo(ref, cols, i, n_chunks, before, after):
    r0 = pl.multiple_of(i * SEQ_CHUNK, SEQ_CHUNK)
    parts = []
    if before:
        h0 = pl.multiple_of(jnp.maximum(r0 - HALO, 0), 8)
        halo = ref[pl.ds(h0, HALO), cols]
        parts.append(jnp.where(i > 0, halo, jnp.zeros_like(halo)))
    parts.append(ref[pl.ds(r0, SEQ_CHUNK), cols])
    if after:
        a0 = pl.multiple_of(jnp.minimum(r0 + SEQ_CHUNK, (n_chunks - 1) * SEQ_CHUNK + SEQ_CHUNK - HALO), 8)
        halo = ref[pl.ds(a0, HALO), cols]
        parts.append(jnp.where(i < n_chunks - 1, halo, jnp.zeros_like(halo)))
    return parts[0] if len(parts) == 1 else jnp.concatenate(parts, axis=0)


def _shift_down(v, k):
    return pltpu.roll(v, k, 0)


def _shift_up(v, k):
    return pltpu.roll(v, v.shape[0] - k, 0)


def _pool_diff(xx, w, t_main):
    s = xx
    k = 1
    while k < w:
        s = s + _shift_down(s, k)
        k *= 2
    cnt = jnp.minimum(t_main + 1, w).astype(F32)
    return s[HALO:] / cnt - xx[HALO:]


def _pool_fwd(name, z, pool_mix, pool_scale):
    S_ = z.shape[0]
    n_chunks = S_ // SEQ_CHUNK

    cols = slice(0, POOL_G)

    def body(u_ref, mix_ref, sc_ref, y_ref):
        mixg = mix_ref[...].astype(BF16)
        scg = sc_ref[...]
        for g, w in enumerate(POOL_WINDOWS):
            @pl.when(pl.program_id(0) == g)
            def _(w=w):
                def chunk(i, carry):
                    r0 = pl.multiple_of(i * SEQ_CHUNK, SEQ_CHUNK)
                    xx = _rows_with_halo(u_ref, cols, i, n_chunks, True, False)
                    t = r0 + lax.broadcasted_iota(jnp.int32, (SEQ_CHUNK, POOL_G), 0)
                    d = _pool_diff(xx, w, t)
                    y = jnp.dot(d.astype(BF16), mixg, preferred_element_type=F32) * scg
                    y_ref[pl.ds(r0, SEQ_CHUNK), :] = y.astype(BF16)
                    return carry

                lax.fori_loop(0, n_chunks, chunk, 0)

    slab = pl.BlockSpec((S_, POOL_G), lambda g: (0, g))
    return pl.pallas_call(
        body, name=name, grid=(len(POOL_WINDOWS),),
        in_specs=[slab, pl.BlockSpec((None, POOL_G, POOL_G), lambda g: (g, 0, 0)),
                  pl.BlockSpec((1, POOL_G), lambda g: (0, g))],
        out_specs=slab, out_shape=jax.ShapeDtypeStruct((S_, POOL_W), BF16),
        compiler_params=_cparams(("parallel",)),
    )(z, pool_mix, pool_scale)


def _pool_bwd(name, z, dy, pool_mix, pool_scale, dz, after=None):
    S_ = z.shape[0]
    n_chunks = S_ // SEQ_CHUNK
    rows_a = SEQ_CHUNK + HALO

    cols = slice(0, POOL_G)

    def body(u_ref, dy_ref, mix_ref, sc_ref, du_ref, dmix_ref, dsc_ref):
        mixg = mix_ref[...].astype(BF16)
        scg = sc_ref[...]
        for g, w in enumerate(POOL_WINDOWS):
            @pl.when(pl.program_id(0) == g)
            def _(w=w):
                def chunk(i, carry):
                    dmix_acc, dsc_acc = carry
                    r0 = pl.multiple_of(i * SEQ_CHUNK, SEQ_CHUNK)
                    xx = _rows_with_halo(u_ref, cols, i, n_chunks, True, False)
                    t = r0 + lax.broadcasted_iota(jnp.int32, (SEQ_CHUNK, POOL_G), 0)
                    d = _pool_diff(xx, w, t).astype(BF16)
                    ypre = jnp.dot(d, mixg, preferred_element_type=F32)
                    dyy = _rows_with_halo(dy_ref, cols, i, n_chunks, False, True)
                    dys = (dyy * scg).astype(BF16)
                    dsc_acc = dsc_acc + jnp.sum((dyy[:SEQ_CHUNK] * ypre).reshape(SEQ_CHUNK // 8, 8, POOL_G), axis=0)
                    dmix_acc = dmix_acc + lax.dot_general(d, dys[:SEQ_CHUNK], _DIMS["tn"],
                                                          preferred_element_type=F32)
                    dd = lax.dot_general(dys, mixg, _DIMS["nt"], preferred_element_type=F32)
                    ta = r0 + lax.broadcasted_iota(jnp.int32, (rows_a, POOL_G), 0)
                    f = dd / jnp.minimum(ta + 1, w).astype(F32)
                    k = 1
                    while k < w:
                        f = f + _shift_up(f, k)
                        k *= 2
                    du_ref[pl.ds(r0, SEQ_CHUNK), :] = (f[:SEQ_CHUNK] - dd[:SEQ_CHUNK]).astype(BF16)
                    return dmix_acc, dsc_acc

                dmix_acc, dsc_acc = lax.fori_loop(
                    0, n_chunks, chunk, (jnp.zeros((POOL_G, POOL_G), F32), jnp.zeros((8, POOL_G), F32)))
                dmix_ref[...] = dmix_acc
                dsc_ref[...] = jnp.sum(dsc_acc, axis=0, keepdims=True)

    slab = pl.BlockSpec((S_, POOL_G), lambda g: (0, g))
    mix_spec = pl.BlockSpec((None, POOL_G, POOL_G), lambda g: (g, 0, 0))
    vec = pl.BlockSpec((1, POOL_G), lambda g: (0, g))
    return _call(
        _drop_ref(body, 4), [z, dy, pool_mix, pool_scale, dz], [slab, slab, mix_spec, vec, ANY], after, name=name,
        grid=(len(POOL_WINDOWS),), out_specs=(slab, mix_spec, vec),
        out_shape=(jax.ShapeDtypeStruct(dz.shape, dz.dtype), jax.ShapeDtypeStruct((4, POOL_G, POOL_G), F32),
                   jax.ShapeDtypeStruct((1, POOL_W), F32)),
        input_output_aliases={4: 0}, compiler_params=_cparams(("parallel",)))


def _conv_specs(S_):
    slab = lambda off: pl.BlockSpec((S_, 128), lambda c, off=off: (0, off // 128 + c))
    return slab(OFF_CB), slab(OFF_CC), slab(OFF_CX)


def _conv_fwd(name, z, conv_w, after=None):
    S_ = z.shape[0]
    n_chunks = S_ // SEQ_CHUNK
    col = slice(0, 128)

    def body(b_ref, c_ref, x_ref, w_ref, y_ref):
        w0, w1, w2 = w_ref[0:1, :], w_ref[1:2, :], w_ref[2:3, :]

        def chunk(i, carry):
            r0 = pl.multiple_of(i * SEQ_CHUNK, SEQ_CHUNK)
            u = _rows_with_halo(c_ref, col, i, n_chunks, True, False) * _rows_with_halo(x_ref, col, i, n_chunks, True, False)
            y = w2 * u + w1 * _shift_down(u, 1) + w0 * _shift_down(u, 2)
            y_ref[pl.ds(r0, SEQ_CHUNK), :] = (b_ref[pl.ds(r0, SEQ_CHUNK), :] * y[HALO:]).astype(BF16)
            return carry

        lax.fori_loop(0, n_chunks, chunk, 0)

    sb, sc, sx = _conv_specs(S_)
    return _call(
        body, [z, z, z, conv_w], [sb, sc, sx, pl.BlockSpec((3, 128), lambda c: (0, c))], after,
        name=name, grid=(CONV_W // 128,),
        out_specs=pl.BlockSpec((S_, 128), lambda c: (0, c)),
        out_shape=jax.ShapeDtypeStruct((S_, CONV_W), BF16),
        compiler_params=_cparams(("parallel",)))


def _conv_bwd(name, z, dy, conv_w, after=None):
    S_ = z.shape[0]
    n_chunks = S_ // SEQ_CHUNK
    col = slice(0, 128)
    lo, hi = HALO, HALO + SEQ_CHUNK

    def body(b_ref, c_ref, x_ref, dy_ref, w_ref, db_ref, dc_ref, dx_ref, dw_ref):
        w0, w1, w2 = w_ref[0:1, :], w_ref[1:2, :], w_ref[2:3, :]

        def chunk(i, carry):
            a0, a1, a2 = carry
            r0 = pl.multiple_of(i * SEQ_CHUNK, SEQ_CHUNK)
            cc = _rows_with_halo(c_ref, col, i, n_chunks, True, True)
            xx = _rows_with_halo(x_ref, col, i, n_chunks, True, True)
            bb = _rows_with_halo(b_ref, col, i, n_chunks, True, True)
            dyy = _rows_with_halo(dy_ref, col, i, n_chunks, True, True)
            u = cc * xx
            u1 = _shift_down(u, 1)
            u2 = _shift_down(u, 2)
            y = w2 * u + w1 * u1 + w0 * u2
            dyv = dyy * bb
            du = w2 * dyv + w1 * _shift_up(dyv, 1) + w0 * _shift_up(dyv, 2)
            db_ref[pl.ds(r0, SEQ_CHUNK), :] = (dyy[lo:hi] * y[lo:hi]).astype(BF16)
            dc_ref[pl.ds(r0, SEQ_CHUNK), :] = (du[lo:hi] * xx[lo:hi]).astype(BF16)
            dx_ref[pl.ds(r0, SEQ_CHUNK), :] = (du[lo:hi] * cc[lo:hi]).astype(BF16)
            red = lambda v: jnp.sum(v.reshape(SEQ_CHUNK // 8, 8, 128), axis=0)
            dm = dyv[lo:hi]
            return a0 + red(dm * u2[lo:hi]), a1 + red(dm * u1[lo:hi]), a2 + red(dm * u[lo:hi])

        zero = jnp.zeros((8, 128), F32)
        a0, a1, a2 = lax.fori_loop(0, n_chunks, chunk, (zero, zero, zero))
        dw_ref[0:1, :] = jnp.sum(a0, axis=0, keepdims=True)
        dw_ref[1:2, :] = jnp.sum(a1, axis=0, keepdims=True)
        dw_ref[2:3, :] = jnp.sum(a2, axis=0, keepdims=True)

    sb, sc, sx = _conv_specs(S_)
    slab = pl.BlockSpec((S_, 128), lambda c: (0, c))
    wspec = pl.BlockSpec((3, 128), lambda c: (0, c))
    act = jax.ShapeDtypeStruct((S_, CONV_W), BF16)
    return _call(
        body, [z, z, z, dy, conv_w], [sb, sc, sx, slab, wspec], after, name=name, grid=(CONV_W // 128,),
        out_specs=(slab, slab, slab, wspec),
        out_shape=(act, act, act, jax.ShapeDtypeStruct((3, CONV_W), F32)),
        compiler_params=_cparams(("parallel",)))


def _head_ones(pw):
    a = lax.broadcasted_iota(jnp.int32, (pw, pw), 0) // HEAD
    b = lax.broadcasted_iota(jnp.int32, (pw, pw), 1) // HEAD
    return (a == b).astype(BF16)


def _head_sum(v, ones):
    hi = v.astype(BF16)
    lo = (v - hi.astype(F32)).astype(BF16)
    return jnp.dot(hi, ones, preferred_element_type=F32) + jnp.dot(lo, ones, preferred_element_type=F32)


def _head_norm(v, gain, ones):
    rstd = lax.rsqrt(_head_sum(v * v, ones) * (1.0 / HEAD) + EPS)
    xhat = v * rstd
    return xhat * gain, xhat, rstd


def _head_norm_bwd(dy, xhat, rstd, gain, ones):
    dxhat = dy * gain
    c = _head_sum(dxhat * xhat, ones) * (1.0 / HEAD)
    dv = rstd * (dxhat - xhat * c)
    dg = jnp.sum((dy * xhat).reshape(dy.shape[0] // 8, 8, dy.shape[1]), axis=0)
    return dv, dg


def _head_masks(pw):
    lane_head = lax.broadcasted_iota(jnp.int32, (1, pw), 1) // HEAD
    return [lane_head == h for h in range(pw // HEAD)]


def _only(mask, v):
    return jnp.where(mask, v, jnp.zeros_like(v))


def _attn_specs(S_, g, dil):
    rows = ATTN_BLK * dil
    nb = S_ // rows
    pw = 128 if dil > 1 else ATTN_OUT
    cq, ck, cv = ((OFF_Q + g * ATTN_OUT) // pw, (OFF_K + g * ATTN_OUT) // pw, (OFF_V + g * ATTN_OUT) // pw)
    return rows, nb, pw, pw // HEAD, ATTN_OUT // pw, cq, ck, cv


ATTN_BATCH = 4


def _attn_group(dil):
    return 4 if dil == 1 else 1


def _attn_block_specs(rows, grp, pw, last=None):
    step = (lambda n: n) if last is None else (lambda n: jnp.minimum(n, last))
    cur = lambda c: pl.BlockSpec((rows * grp, pw), lambda hp, n, c=c: (step(n), c + hp))
    prev = lambda c: pl.BlockSpec((rows, pw), lambda hp, n, c=c: (jnp.maximum(step(n) * grp - 1, 0), c + hp))
    return cur, prev


def _band_mask(has_prev):
    qi = lax.broadcasted_iota(jnp.int32, (ATTN_BLK, 2 * ATTN_BLK), 0)
    ki = lax.broadcasted_iota(jnp.int32, (ATTN_BLK, 2 * ATTN_BLK), 1)
    in_prev = jnp.logical_and(ki < ATTN_BLK, ki >= qi)
    if has_prev is not True:
        in_prev = jnp.logical_and(in_prev, has_prev)
    return jnp.logical_or(in_prev, jnp.logical_and(ki >= ATTN_BLK, ki - ATTN_BLK <= qi))


def _rows_of(ref, r, dil):
    if dil == 1:
        return ref[r * ATTN_BLK:(r + 1) * ATTN_BLK, :]
    return ref[pl.ds(r, ATTN_BLK, stride=dil), :]


def _put_rows(ref, r, dil, val):
    if dil == 1:
        ref[r * ATTN_BLK:(r + 1) * ATTN_BLK, :] = val.astype(ref.dtype)
    else:
        ref[pl.ds(r, ATTN_BLK, stride=dil), :] = val.astype(ref.dtype)


def _attn_fwd(name, z, q_gain, k_gain, g, dil):
    S_ = z.shape[0]
    rows, nb, pw, heads, npairs, cq, ck, cv = _attn_specs(S_, g, dil)
    scale = HEAD ** -0.5

    grp = _attn_group(dil)
    nsteps = nb // grp

    def body(q_ref, k_ref, kp_ref, v_ref, vp_ref, gq_ref, gk_ref, o_ref, l_ref):
        n = pl.program_id(1)
        ones, hmask = _head_ones(pw), _head_masks(pw)
        gq, gk = jnp.tile(gq_ref[...], (1, heads)), jnp.tile(gk_ref[...], (1, heads))
        mask_first, mask_rest = _band_mask(n > 0), _band_mask(True)
        for r0 in range(0, dil * grp, ATTN_BATCH):
            rs = range(r0, min(r0 + ATTN_BATCH, dil * grp))
            qn, kn, vv, s, p = {}, {}, {}, {}, {}
            kcn = {}
            for r in rs:
                q, kc, vc = _rows_of(q_ref, r, dil), _rows_of(k_ref, r, dil), _rows_of(v_ref, r, dil)
                kcn[r] = _head_norm(kc, gk, ones)[0]
                if dil == 1 and r > 0:
                    kpn = kcn[r - 1] if r - 1 in kcn else _head_norm(_rows_of(k_ref, r - 1, dil), gk, ones)[0]
                    vp = _rows_of(v_ref, r - 1, dil)
                else:
                    kpn, vp = _head_norm(_rows_of(kp_ref, r, dil), gk, ones)[0], _rows_of(vp_ref, r, dil)
                qn[r] = _head_norm(q, gq, ones)[0].astype(BF16)
                kn[r] = jnp.concatenate([kpn, kcn[r]], axis=0).astype(BF16)
                vv[r] = jnp.concatenate([vp, vc], axis=0).astype(BF16)
            keys = [(r, h) for r in rs for h in range(heads)]
            for r, h in keys:
                s[r, h] = lax.dot_general(_only(hmask[h], qn[r]), kn[r], _DIMS["nt"],
                                          preferred_element_type=F32) * scale
            lse, den = {}, {}
            for key in keys:
                mask = mask_rest if (dil == 1 and key[0] > 0) else mask_first
                sm = jnp.where(mask, s[key], MASK)
                m = jnp.max(sm, axis=-1, keepdims=True)
                e = jnp.exp(sm - m)
                den[key] = jnp.sum(e, axis=-1, keepdims=True)
                p[key] = e.astype(BF16)
                lse[key] = m + jnp.log(den[key])
            for r in rs:
                out = jnp.zeros((ATTN_BLK, pw), F32)
                lse_all = jnp.zeros((ATTN_BLK, pw), F32)
                for h in range(heads):
                    out = jnp.where(hmask[h], jnp.dot(p[r, h], vv[r], preferred_element_type=F32) / den[r, h], out)
                    lse_all = jnp.where(hmask[h], lse[r, h], lse_all)
                _put_rows(o_ref, r, dil, out)
                _put_rows(l_ref, r, dil, lse_all)

    cur, prev = _attn_block_specs(rows, grp, pw)
    gspec = pl.BlockSpec((1, HEAD), lambda hp, n: (0, 0))
    shp = jax.ShapeDtypeStruct((S_, ATTN_OUT), F32)
    return pl.pallas_call(
        body, name=name, grid=(npairs, nsteps),
        in_specs=[cur(cq), cur(ck), prev(ck), cur(cv), prev(cv), gspec, gspec],
        out_specs=(cur(0), cur(0)), out_shape=(shp, shp),
        compiler_params=_cparams(("parallel", "parallel")),
    )(z, z, z, z, z, q_gain, k_gain)


def _attn_combine(name, os_, ls_):
    S_ = os_[0].shape[0]

    def body(o0, o1, o2, l0, l1, l2, o_ref, l_ref):
        a, b, c = l0[...], l1[...], l2[...]
        m = jnp.maximum(jnp.maximum(a, b), c)
        ea, eb, ec = jnp.exp(a - m), jnp.exp(b - m), jnp.exp(c - m)
        zsum = ea + eb + ec
        o_ref[...] = (ea * o0[...] + eb * o1[...] + ec * o2[...]) / zsum
        l_ref[...] = m + jnp.log(zsum)

    row = pl.BlockSpec((ROW_TILE, ATTN_OUT), lambda i: (i, 0))
    shp = jax.ShapeDtypeStruct((S_, ATTN_OUT), F32)
    return pl.pallas_call(
        body, name=name, grid=(S_ // ROW_TILE,), in_specs=[row] * 6, out_specs=(row, row), out_shape=(shp, shp),
        compiler_params=_cparams(("parallel",)),
    )(*os_, *ls_)


def _attn_bwd(name, z, q_gain, k_gain, do, o, lse, g, dil, after=None):
    S_ = z.shape[0]
    rows, nb, pw, heads, npairs, cq, ck, cv = _attn_specs(S_, g, dil)
    scale = HEAD ** -0.5

    def body(q_ref, kc_ref, kp_ref, vc_ref, vp_ref, gq_ref, gk_ref, do_ref, o_ref, l_ref,
             dq_ref, dk_ref, dv_ref, dgq_ref, dgk_ref, ck_ref, cvv_ref, gq_acc, gk_acc):
        hp = pl.program_id(0)
        n = pl.program_id(1)
        ones = _head_ones(pw)
        gq, gk = jnp.tile(gq_ref[...], (1, heads)), jnp.tile(gk_ref[...], (1, heads))

        @pl.when(n == 0)
        def _():
            ck_ref[...] = jnp.zeros_like(ck_ref)
            cvv_ref[...] = jnp.zeros_like(cvv_ref)

        @pl.when(jnp.logical_and(n == 0, hp == 0))
        def _():
            gq_acc[...] = jnp.zeros_like(gq_acc)
            gk_acc[...] = jnp.zeros_like(gk_acc)

        pl.when(n < nb)(functools.partial(query_step, n, ones, gq, gk, q_ref, kc_ref, kp_ref, vc_ref, vp_ref, do_ref,
                                          o_ref, l_ref, dq_ref, dk_ref, dv_ref, ck_ref, cvv_ref, gq_acc, gk_acc))

        @pl.when(n == nb)
        def _():
            dgk = jnp.zeros((8, pw), F32)
            for r in range(dil):
                _, kphat, kprstd = _head_norm(_rows_of(kp_ref, r, dil), gk, ones)
                dk_all, dg = _head_norm_bwd(_rows_of(ck_ref, r, dil), kphat, kprstd, gk, ones)
                dgk = dgk + dg
                _put_rows(dk_ref, r, dil, dk_all)
                _put_rows(dv_ref, r, dil, _rows_of(cvv_ref, r, dil))
            gk_acc[...] += dgk

        @pl.when(jnp.logical_and(n == nb, hp == npairs - 1))
        def _():
            fold = lambda a: sum(a[:, h * HEAD:(h + 1) * HEAD] for h in range(heads))
            dgq_ref[...] = jnp.sum(fold(gq_acc[...]), axis=0, keepdims=True)
            dgk_ref[...] = jnp.sum(fold(gk_acc[...]), axis=0, keepdims=True)

    def query_step(n, ones, gq, gk, q_ref, kc_ref, kp_ref, vc_ref, vp_ref, do_ref, o_ref, l_ref,
                   dq_ref, dk_ref, dv_ref, ck_ref, cvv_ref, gq_acc, gk_acc):
        mask = _band_mask(n > 0)
        hmask = _head_masks(pw)
        dgq = jnp.zeros((8, pw), F32)
        dgk = jnp.zeros((8, pw), F32)
        for r0 in range(0, dil, ATTN_BATCH):
            rs = range(r0, min(r0 + ATTN_BATCH, dil))
            keys = [(r, h) for r in rs for h in range(heads)]
            qn, qhat, qrstd, kn, kphat, kprstd, vv, dob, delta, lse = ({} for _ in range(10))
            for r in rs:
                q, kc, kp = _rows_of(q_ref, r, dil), _rows_of(kc_ref, r, dil), _rows_of(kp_ref, r, dil)
                dov = _rows_of(do_ref, r, dil)
                qn_f, qhat[r], qrstd[r] = _head_norm(q, gq, ones)
                kpn, kphat[r], kprstd[r] = _head_norm(kp, gk, ones)
                qn[r] = qn_f.astype(BF16)
                kn[r] = jnp.concatenate([kpn, _head_norm(kc, gk, ones)[0]], axis=0).astype(BF16)
                vv[r] = jnp.concatenate([_rows_of(vp_ref, r, dil), _rows_of(vc_ref, r, dil)], axis=0).astype(BF16)
                dob[r] = dov.astype(BF16)
                delta[r] = _head_sum(dov * _rows_of(o_ref, r, dil), ones)
                lse[r] = _rows_of(l_ref, r, dil)
            s, dp = {}, {}
            for r, h in keys:
                s[r, h] = lax.dot_general(_only(hmask[h], qn[r]), kn[r], _DIMS["nt"],
                                          preferred_element_type=F32) * scale
                dp[r, h] = lax.dot_general(_only(hmask[h], dob[r]), vv[r], _DIMS["nt"], preferred_element_type=F32)
            p, ds = {}, {}
            for r, h in keys:
                col = slice(h * HEAD, h * HEAD + 1)
                pk = jnp.where(mask, jnp.exp(jnp.where(mask, s[r, h], MASK) - lse[r][:, col]), 0.0)
                ds[r, h] = (pk * (dp[r, h] - delta[r][:, col]) * scale).astype(BF16)
                p[r, h] = pk.astype(BF16)
            dqn, dkn, dvv = {}, {}, {}
            for r in rs:
                dqn[r] = jnp.zeros((ATTN_BLK, pw), F32)
                dkn[r] = jnp.zeros((2 * ATTN_BLK, pw), F32)
                dvv[r] = jnp.zeros((2 * ATTN_BLK, pw), F32)
                for h in range(heads):
                    dqn[r] = jnp.where(hmask[h], jnp.dot(ds[r, h], kn[r], preferred_element_type=F32), dqn[r])
                    dkn[r] = jnp.where(hmask[h], lax.dot_general(ds[r, h], qn[r], _DIMS["tn"],
                                                                 preferred_element_type=F32), dkn[r])
                    dvv[r] = jnp.where(hmask[h], lax.dot_general(p[r, h], dob[r], _DIMS["tn"],
                                                                 preferred_element_type=F32), dvv[r])
            for r in rs:
                dq_all, dg = _head_norm_bwd(dqn[r], qhat[r], qrstd[r], gq, ones)
                dgq = dgq + dg
                dk_all, dg = _head_norm_bwd(_rows_of(ck_ref, r, dil) + dkn[r][:ATTN_BLK], kphat[r], kprstd[r], gk, ones)
                dgk = dgk + dg
                dv_all = _rows_of(cvv_ref, r, dil) + dvv[r][:ATTN_BLK]
                _put_rows(dq_ref, r, dil, dq_all)
                _put_rows(dk_ref, r, dil, dk_all)
                _put_rows(dv_ref, r, dil, dv_all)
                _put_rows(ck_ref, r, dil, dkn[r][ATTN_BLK:])
                _put_rows(cvv_ref, r, dil, dvv[r][ATTN_BLK:])
        gq_acc[...] += dgq
        gk_acc[...] += dgk

    last = nb - 1
    cur = lambda c: pl.BlockSpec((rows, pw), lambda hp, n, c=c: (jnp.minimum(n, last), c + hp))
    prev = lambda c: pl.BlockSpec((rows, pw), lambda hp, n, c=c: (jnp.maximum(n - 1, 0), c + hp))
    gspec = pl.BlockSpec((1, HEAD), lambda hp, n: (0, 0))
    act = jax.ShapeDtypeStruct((S_, ATTN_OUT), F32)
    vec = jax.ShapeDtypeStruct((1, HEAD), F32)
    return _call(
        body, [z, z, z, z, z, q_gain, k_gain, do, o, lse],
        [cur(cq), cur(ck), prev(ck), cur(cv), prev(cv), gspec, gspec, cur(0), cur(0), cur(0)], after,
        name=name, grid=(npairs, nb + 1),
        out_specs=(cur(0), prev(0), prev(0), gspec, gspec),
        out_shape=(act, act, act, vec, vec),
        scratch_shapes=[pltpu.VMEM((rows, pw), F32), pltpu.VMEM((rows, pw), F32),
                        pltpu.VMEM((8, pw), F32), pltpu.VMEM((8, pw), F32)],
        compiler_params=_cparams(("arbitrary", "arbitrary")))


MIX_TN = 256
MIX_TM = 2048


def _sigmoid(v):
    return 1.0 / (1.0 + jnp.exp(-v))


def _mix_fwd(name, z, b_gate, ys, ws):
    S_ = z.shape[0]
    tm, tn = MIX_TM, MIX_TN
    gblk = OFF_GATE // tn

    def body(yp, yc, ya, wp, wc, wa, g0, g1, g2, b0, b1, b2, m_ref):
        acc = None
        for y_ref, w_ref, g_ref, b_ref in ((yp, wp, g0, b0), (yc, wc, g1, b1), (ya, wa, g2, b2)):
            u = lax.dot_general(y_ref[...].astype(BF16), w_ref[...], _DIMS["nt"], preferred_element_type=F32)
            t = _sigmoid(g_ref[...] + b_ref[...]) * u
            acc = t if acc is None else acc + t
        m_ref[...] = acc.astype(BF16)

    yspec = lambda w: pl.BlockSpec((tm, w), lambda i, j: (i, 0))
    wspec = lambda w: pl.BlockSpec((tn, w), lambda i, j: (j, 0))
    gspec = lambda b: pl.BlockSpec((tm, tn), lambda i, j, b=b: (i, gblk + b * (D // tn) + j))
    bspec = lambda b: pl.BlockSpec((1, tn), lambda i, j, b=b: (0, b * (D // tn) + j))
    return pl.pallas_call(
        body, name=name, grid=(S_ // tm, D // tn),
        in_specs=[yspec(POOL_W), yspec(CONV_W), yspec(ATTN_OUT), wspec(POOL_W), wspec(CONV_W), wspec(ATTN_OUT),
                  gspec(0), gspec(1), gspec(2), bspec(0), bspec(1), bspec(2)],
        out_specs=pl.BlockSpec((tm, tn), lambda i, j: (i, j)),
        out_shape=jax.ShapeDtypeStruct((S_, D), BF16),
        compiler_params=_cparams(("parallel", "parallel")),
    )(*ys, *ws, z, z, z, b_gate, b_gate, b_gate)


def _mix_bwd(name, z, b_gate, y, w, dmerged, branch, dz, after=None):
    S_ = z.shape[0]
    tm, tn = MIX_TM, MIX_TN
    width = y.shape[1]
    gblk = OFF_GATE // tn + branch * (D // tn)
    nj = D // tn

    def body(y_ref, w_ref, g_ref, b_ref, dm_ref, dy_ref, dw_ref, dg_ref, db_ref, acc_ref):
        i, j = pl.program_id(0), pl.program_id(1)
        yb = y_ref[...].astype(BF16)
        u = lax.dot_general(yb, w_ref[...], _DIMS["nt"], preferred_element_type=F32)
        sg = _sigmoid(g_ref[...] + b_ref[...])
        dm = dm_ref[...]
        du = (sg * dm).astype(BF16)
        dpre = dm * u * sg * (1.0 - sg)
        dg_ref[...] = dpre.astype(BF16)
        cols = pl.ds(pl.multiple_of(j * tn, tn), tn)
        db_ref[...] = _rows8(dpre)
        d_w = lax.dot_general(du, yb, _DIMS["tn"], preferred_element_type=F32)
        d_y = jnp.dot(du, w_ref[...], preferred_element_type=F32)

        @pl.when(i == 0)
        def _():
            dw_ref[cols, :] = d_w

        @pl.when(i > 0)
        def _():
            dw_ref[cols, :] += d_w

        @pl.when(j == 0)
        def _():
            acc_ref[...] = d_y

        @pl.when(j > 0)
        def _():
            acc_ref[...] += d_y

        @pl.when(j == nj - 1)
        def _():
            dy_ref[...] = acc_ref[...]

    rows = pl.BlockSpec((tm, width), lambda i, j: (i, 0))
    blk = pl.BlockSpec((tm, tn), lambda i, j: (i, j))
    gate = pl.BlockSpec((tm, tn), lambda i, j: (i, gblk + j))
    args = [y, w, z, b_gate, dmerged]
    in_specs = [rows, pl.BlockSpec((tn, width), lambda i, j: (j, 0)), gate,
                pl.BlockSpec((1, tn), lambda i, j: (0, branch * nj + j)), blk]
    aliases = {}
    if not isinstance(dz, jax.ShapeDtypeStruct):
        body = _drop_ref(body, len(args))
        aliases = {len(args): 2}
        args.append(dz)
        in_specs.append(ANY)
    return _call(
        body, args, in_specs, after, name=name, grid=(S_ // tm, nj),
        out_specs=(rows, pl.BlockSpec((D, width), lambda i, j: (0, 0)), gate, pl.BlockSpec((8, tn), lambda i, j: (i, j))),
        out_shape=(jax.ShapeDtypeStruct((S_, width), F32), jax.ShapeDtypeStruct((D, width), F32),
                   jax.ShapeDtypeStruct(dz.shape, dz.dtype), jax.ShapeDtypeStruct((S_ // tm * 8, D), F32)),
        scratch_shapes=[pltpu.VMEM((tm, width), F32)], input_output_aliases=aliases,
        compiler_params=_cparams(("arbitrary", "arbitrary")))


def _relu2_epilogue(acc):
    r = jnp.maximum(acc, 0.0)
    return (r * r,)


def _relu2_bwd_epilogue(acc, r):
    return (acc * (2.0 * jnp.sqrt(r.astype(F32))),)


def _residual_norm_epilogue(acc, xr, gain):
    x = xr + acc
    ms = jnp.mean(x * x, axis=-1, keepdims=True)
    return x, x * lax.rsqrt(ms + EPS) * gain


def _fold_rows(name, part, lanes=False, scale=1.0):
    R, N = part.shape

    def body(p_ref, o_ref):
        s = jnp.sum(p_ref[...], axis=0, keepdims=True)
        if lanes:
            s = jnp.sum(s, axis=1, keepdims=True)
        o_ref[...] = s * scale

    out_n = 1 if lanes else N
    return pl.pallas_call(
        body, name=name, grid=(1,), in_specs=[pl.BlockSpec((R, N), lambda i: (0, 0))],
        out_specs=pl.BlockSpec((1, out_n), lambda i: (0, 0)), out_shape=jax.ShapeDtypeStruct((1, out_n), F32))(part)


def _rows8(v):
    return jnp.sum(v.reshape(v.shape[0] // 8, 8, v.shape[1]), axis=0)


def _residual_loss_epilogue(acc, xr, target):
    e = xr + acc - target
    dy = e * (1.0 / D)
    return dy, _rows8(e * e), dy


def _norm_bwd_epilogue(acc, x, dres, gain):
    rstd = lax.rsqrt(jnp.mean(x * x, axis=-1, keepdims=True) + EPS)
    xhat = x * rstd
    dxhat = acc * gain
    c = jnp.mean(dxhat * xhat, axis=-1, keepdims=True)
    return dres + rstd * (dxhat - xhat * c), _rows8(acc * xhat)


def _norm_bwd_epilogue_twice(acc, x, dres, gain):
    dx, part = _norm_bwd_epilogue(acc, x, dres, gain)
    return dx, part, dx


def _layer_fwd(l, x, h, p, next_gain, target=None):
    t = f"l{l}_"
    z = _matmul(t + "in_proj", h, p["w_in_t"], "nt", 512, 3712, 1024)
    y_pool = _pool_fwd(t + "pool", z, p["pool_mix"], p["pool_scale"])
    os_, ls_ = [], []
    for g, (_, dil) in enumerate(ATTN_GROUPS):
        o_g, l_g = _attn_fwd(t + f"attn{g}", z, p["q_gain"], p["k_gain"], g, dil)
        os_.append(o_g)
        ls_.append(l_g)
    y_attn, lse = _attn_combine(t + "attn_mix", os_, ls_)
    y_conv = _conv_fwd(t + "conv", z, p["conv_w"], after=y_attn)
    merged = _mix_fwd(t + "merge", z, p["b_gate"], (y_pool, y_conv, y_attn),
                      (p["w_pool_up_t"], p["w_conv_out_t"], p["w_attn_up_t"]))
    x1, h2 = _matmul(t + "out_proj", merged, p["rest"], "nn", 1024, D, 1024, out_dtypes=(F32, BF16),
                     extras=((x, "mn"), (p["norm_mlp"], "n")), epilogue=_residual_norm_epilogue, b_rows=ROWS_WO)
    r = _matmul(t + "ff1", h2, p["rest"], "nt", 1024, 1024, 1024, out_dtypes=(BF16,), epilogue=_relu2_epilogue,
                b_rows=ROWS_FF1)
    if next_gain is None:
        dy, h_out, dy_16 = _matmul(t + "ff2", r, p["rest"], "nn", 512, D, D_FF,
                                   out_dtypes=(F32, (F32, "rows8"), BF16), extras=((x1, "mn"), (target, "mn")),
                                   epilogue=_residual_loss_epilogue, b_rows=ROWS_FF2)
        x2 = (dy, dy_16)
    else:
        x2, h_out = _matmul(t + "ff2", r, p["rest"], "nn", 512, D, D_FF, out_dtypes=(F32, BF16),
                            extras=((x1, "mn"), (next_gain, "n")), epilogue=_residual_norm_epilogue,
                            b_rows=ROWS_FF2)
    saved = dict(x=x, h=h, z=z, y_pool=y_pool, y_conv=y_conv, y_attn=y_attn, lse=lse, merged=merged,
                 x1=x1, h2=h2, r=r)
    return x2, h_out, saved


def _layer_bwd(l, dx2, p, s, pending, in_t_all, collective_id, start_after=None):
    t = f"l{l}_b_"
    g = {}
    rest = jax.ShapeDtypeStruct((N_DEV, REST_ROWS, D), F32)
    dx2, dx2_16 = dx2
    da = _matmul(t + "d_ff2_in", dx2_16, p["rest"], "nt", 1024, 1024, 1024, out_dtypes=(BF16,),
                 extras=((s["r"], "mn"),), epilogue=_relu2_bwd_epilogue,
                 after=start_after, b_rows=ROWS_FF2)
    rest = _matmul(t + "dw_ff2", s["r"], dx2_16, "tn", 512, 1024, 4096, into=(rest, REST_FF2, 0), after=da)
    tok = rest if pending is None else pending.chip_sums(after=rest)
    dx1, part, dx1_16 = _matmul(
        t + "d_ff1_in", da, p["rest"], "nn", 512, D, D_FF, out_dtypes=(F32, (F32, "rows8"), BF16),
        extras=((s["x1"], "mn"), (dx2, "mn"), (p["norm_mlp"], "n")), epilogue=_norm_bwd_epilogue_twice,
        after=tok, b_rows=ROWS_FF1)
    g["norm_mlp"] = _fold_rows(t + "d_norm_mlp", part)
    rest = _matmul(t + "dw_ff1", da, s["h2"], "tn", 512, 1024, 4096, into=(rest, REST_FF1, 0), after=dx1)
    dmerged = _matmul(t + "d_out_proj_in", dx1_16, p["rest"], "nt", 1024, 1024, 1024, b_rows=ROWS_WO)
    rest = _matmul(t + "dw_o", s["merged"], dx1_16, "tn", 128, 1024, 4096, into=(rest, REST_WO, 0), after=dmerged)
    ys = (s["y_pool"], s["y_conv"], s["y_attn"])
    names = ("w_pool_up_t", "w_conv_out_t", "w_attn_up_t")
    dys, dbs = [], []
    tok = rest
    dz = jax.ShapeDtypeStruct((dx2.shape[0], IN_COLS), BF16)
    for b in range(3):
        dy_b, dw_t, dz, db = _mix_bwd(t + f"merge{b}", s["z"], p["b_gate"], ys[b], p[names[b]], dmerged, b, dz,
                                      after=tok)
        tok = dy_b
        width = ys[b].shape[1]
        if b < 2:
            rest = rest.at[:, REST_UP:REST_ATTN, b * width:(b + 1) * width].set(
                dw_t.reshape(N_DEV, REST_ATTN - REST_UP, width))
        else:
            rest = rest.at[:, REST_ATTN:REST_ROWS, :].set(dw_t.reshape(N_DEV, REST_ROWS - REST_ATTN, D))
        dys.append(dy_b)
        dbs.append(_fold_rows(t + f"d_b_gate{b}", db))
    g["b_gate"] = jnp.concatenate(dbs, axis=1)
    rs_rest = _ReduceScatter(f"rs_rest{l}", collective_id, rest)
    if pending is not None:
        in_t_all = pending.finish(after=rest, into=(in_t_all, l + 1))
    dz, g["pool_mix"], g["pool_scale"] = _pool_bwd(t + "pool", s["z"], dys[0], p["pool_mix"], p["pool_scale"], dz,
                                                   after=rest if pending is None else in_t_all)
    dcb, dcc, dcx, g["conv_w"] = _conv_bwd(t + "conv", s["z"], dys[1], p["conv_w"], after=dz)
    tok = rs_rest.chip_sums(after=dcb)
    dqs, dks, dvs = [], [], []
    gq = gk = None
    for gi, (_, dil) in enumerate(ATTN_GROUPS):
        dq, dk, dv, dgq, dgk = _attn_bwd(t + f"attn{gi}", s["z"], p["q_gain"], p["k_gain"], dys[2], s["y_attn"],
                                         s["lse"], gi, dil, after=tok)
        tok = dq
        dqs.append(dq)
        dks.append(dk)
        dvs.append(dv)
        gq = dgq if gq is None else gq + dgq
        gk = dgk if gk is None else gk + dgk
    g["q_gain"], g["k_gain"] = gq, gk
    rest_sum = rs_rest.finish(after=tok)
    col = OFF_CB
    for piece in [dcb, dcc, dcx] + dqs + dks + dvs:
        dz = lax.dynamic_update_slice(dz, piece.astype(BF16), (0, col))
        col += piece.shape[1]
    in_t = _matmul(t + "dw_in", dz, s["h"], "tn", 256, 1024, 4096, after=rest_sum)
    rs_in = _ReduceScatter(f"rs_in{l}", collective_id + 2, in_t.reshape(N_DEV, IN_ROWS, D))
    dx = _matmul(t + "d_in_proj_in", dz, p["w_in_t"], "nn", 256, D, IN_COLS,
                 out_dtypes=(F32, (F32, "rows8")) + ((BF16,) if l > 0 else ()),
                 extras=((s["x"], "mn"), (dx1, "mn"), (p["norm_mix"], "n")),
                 epilogue=_norm_bwd_epilogue_twice if l > 0 else _norm_bwd_epilogue, after=in_t)
    dx, part = (dx[0], dx[2]) if l > 0 else (dx[0], None), dx[1]
    g["norm_mix"] = _fold_rows(t + "d_norm_mix", part)
    return dx, g, rest_sum, in_t_all, rs_in


ANY = pl.BlockSpec(memory_space=pl.ANY)


def _mesh_pos():
    return lax.axis_index("x"), lax.axis_index("y"), lax.axis_index("c")


def _other_chips(x, y):
    return [(1 - x, y), (x, 1 - y), (1 - x, 1 - y)]


def _comm_call(name, collective_id, peers, body, arrs, out_shape, sem_counts, after=None):
    n_in, n_out = len(arrs), len(out_shape)
    if collective_id is None:
        def tc_body(*refs):
            body(refs[:n_in], refs[n_in:n_in + n_out], *refs[n_in + n_out:])

        return pl.pallas_call(
            tc_body, name=name, out_shape=tuple(out_shape), in_specs=[ANY] * n_in, out_specs=(ANY,) * n_out,
            scratch_shapes=[pltpu.SemaphoreType.DMA((n,)) for n in sem_counts])(*arrs)

    n_after = 0 if after is None else 1

    def seq_body(*refs):
        barrier = pltpu.get_barrier_semaphore()
        ps = peers(*_mesh_pos())
        for p in ps:
            pl.semaphore_signal(barrier, inc=1, device_id=p, device_id_type=MESH)
        pl.semaphore_wait(barrier, len(ps))
        outs = refs[n_in + n_after:]
        body(refs[:n_in], outs[:n_out], *outs[n_out:])

    return pl.kernel(
        seq_body, out_type=tuple(out_shape), mesh=plsc.ScalarSubcoreMesh(axis_name="seq", num_cores=1), name=name,
        scratch_types=[pltpu.SemaphoreType.DMA((n,)) for n in sem_counts],
        compiler_params=pltpu.CompilerParams(collective_id=collective_id),
    )(*arrs, *([] if after is None else [after]))


def _all_gather(name, collective_id, shard, after=None):
    R, C = shard.shape

    def peers(x, y, c):
        return [(x, y, 1 - c)] + [(*chip, c) for chip in _other_chips(x, y)]

    def body(in_refs, out_refs, send_sems, recv_sems, local_sems):
        (x_ref,), (out_ref,) = in_refs, out_refs
        x, y, c = _mesh_pos()
        me, sibling = (x, y, c), (x, y, 1 - c)
        chips = _other_chips(x, y)

        def slot(px, py, pc):
            return out_ref.at[4 * px + 2 * py + pc]

        def copy(k, block, to, src=None):
            return pltpu.make_async_remote_copy(
                src_ref=slot(*block) if src is None else src, dst_ref=slot(*block),
                send_sem=send_sems.at[k], recv_sem=recv_sems.at[k], device_id=to, device_id_type=MESH)

        mine = pltpu.make_async_copy(x_ref, slot(*me), local_sems.at[0])
        mine.start()
        first = [copy(0, me, sibling, src=x_ref)]
        first += [copy(1 + j, me, (*chip, c), src=x_ref) for j, chip in enumerate(chips)]
        for cp in first:
            cp.start()
        passed = [copy(4 + j, (*chip, c), sibling) for j, chip in enumerate(chips)]
        for j, chip in enumerate(chips):
            copy(1 + j, (*chip, c), me).wait_recv()
            passed[j].start()
        copy(0, sibling, me).wait_recv()
        for j, chip in enumerate(chips):
            copy(4 + j, (*chip, 1 - c), me).wait_recv()
        for cp in first + passed:
            cp.wait_send()
        mine.wait()

    return _comm_call(name, collective_id, peers, body, [shard],
                      [jax.ShapeDtypeStruct((N_DEV, R, C), shard.dtype)], (7, 7, 1), after)[0]


def _rs_sibling_exchange(name, collective_id, arrs):
    n = len(arrs)

    def body(in_refs, out_refs, send_sems, recv_sems):
        x, y, c = _mesh_pos()
        cps = []
        for k, (src, dst) in enumerate(zip(in_refs, out_refs)):
            src = src.at[:, 1 - c] if len(src.shape) == 4 else src
            cps.append(pltpu.make_async_remote_copy(src_ref=src, dst_ref=dst, send_sem=send_sems.at[k],
                                                    recv_sem=recv_sems.at[k], device_id=(x, y, 1 - c),
                                                    device_id_type=MESH))
        for cp in cps:
            cp.start()
        for cp in cps:
            cp.wait()

    out_shape = [jax.ShapeDtypeStruct(a.shape[:1] + a.shape[2:] if a.ndim == 4 else a.shape, a.dtype) for a in arrs]
    return _comm_call(name, collective_id, lambda x, y, c: [(x, y, 1 - c)], body, arrs, out_shape, (n, n))


def _rs_rows(r):
    return r // 2 if (r // 2) % 16 == 0 else r


def _drop_ref(body, idx):
    def wrapped(*refs):
        return body(*refs[:idx], *refs[idx + 1:])

    return wrapped


def _rs_chip_sum(name, ids, big, rbig, after=None):
    _, _, R, C = big.shape
    rows = _rs_rows(R)

    def body(ids_ref, mine_ref, theirs_ref, t16_ref, own_ref):
        p = pl.program_id(1)
        s = mine_ref[...] + theirs_ref[...]
        t16_ref[...] = s.astype(BF16)

        @pl.when(p == ids_ref[1])
        def _():
            own_ref[...] = s

    in_specs = [pl.BlockSpec((None, None, rows, C), lambda i, p, ids: (p, ids[0], i, 0)),
                pl.BlockSpec((None, rows, C), lambda i, p, ids: (p, i, 0))]
    args = [ids, big, rbig]
    if after is not None:
        body = _drop_ref(body, len(args))
        in_specs.append(ANY)
        args.append(after)
    grid_spec = pltpu.PrefetchScalarGridSpec(
        num_scalar_prefetch=1, grid=(R // rows, 4), in_specs=in_specs,
        out_specs=(pl.BlockSpec((None, rows, C), lambda i, p, ids: (p, i, 0)),
                   pl.BlockSpec((rows, C), lambda i, p, ids: (i, 0))))
    return pl.pallas_call(
        body, name=name, grid_spec=grid_spec,
        out_shape=(jax.ShapeDtypeStruct((4, R, C), BF16), jax.ShapeDtypeStruct((R, C), F32)),
        compiler_params=_cparams(("parallel", "arbitrary")),
    )(*args)


def _add2(name, a, b):
    def body(a_ref, b_ref, o_ref):
        o_ref[...] = a_ref[...] + b_ref[...]

    full = pl.BlockSpec(a.shape, lambda i: (0, 0))
    return pl.pallas_call(body, name=name, grid=(1,), in_specs=[full, full], out_specs=full,
                          out_shape=jax.ShapeDtypeStruct(a.shape, a.dtype))(a, b)


def _rs_chip_exchange(name, collective_id, arrs, after=None):
    n = len(arrs)

    def body(in_refs, out_refs, send_sems, recv_sems, local_sems):
        x, y, c = _mesh_pos()
        p_me = 2 * x + y
        chips = _other_chips(x, y)

        def part(ref, p):
            return ref.at[p] if len(ref.shape) == 3 else ref

        local = [pltpu.make_async_copy(part(src, p_me), dst.at[p_me], local_sems.at[k])
                 for k, (src, dst) in enumerate(zip(in_refs, out_refs))]
        for cp in local:
            cp.start()
        sends = []
        for j, (px, py) in enumerate(chips):
            for k, (src, dst) in enumerate(zip(in_refs, out_refs)):
                sends.append(pltpu.make_async_remote_copy(
                    src_ref=part(src, 2 * px + py), dst_ref=dst.at[p_me], send_sem=send_sems.at[n * j + k],
                    recv_sem=recv_sems.at[n * j + k], device_id=(px, py, c), device_id_type=MESH))
        for cp in sends:
            cp.start()
        for j, (px, py) in enumerate(chips):
            for k, (src, dst) in enumerate(zip(in_refs, out_refs)):
                pltpu.make_async_remote_copy(
                    src_ref=part(src, p_me), dst_ref=dst.at[2 * px + py], send_sem=send_sems.at[n * j + k],
                    recv_sem=recv_sems.at[n * j + k], device_id=(px, py, c), device_id_type=MESH).wait_recv()
        for cp in sends:
            cp.wait_send()
        for cp in local:
            cp.wait()

    out_shape = [jax.ShapeDtypeStruct((4,) + a.shape[-2:], a.dtype) for a in arrs]
    return _comm_call(name, collective_id, lambda x, y, c: [(*chip, c) for chip in _other_chips(x, y)], body, arrs,
                      out_shape, (3 * n, 3 * n, n), after)


def _rs_final_sum(name, ids, recv, own, rows, after=None, into=None):
    _, R, C = recv.shape
    assert R % rows == 0

    def body(ids_ref, r_ref, own_ref, o_ref):
        acc = None
        for p in range(4):
            term = jnp.where(ids_ref[1] == p, own_ref[...], r_ref[p].astype(F32))
            acc = term if acc is None else acc + term
        o_ref[...] = acc

    in_specs = [pl.BlockSpec((4, rows, C), lambda i, ids: (0, i, 0)), pl.BlockSpec((rows, C), lambda i, ids: (i, 0))]
    args = [ids, recv, own]
    if after is not None:
        body = _drop_ref(body, len(args))
        in_specs.append(ANY)
        args.append(after)
    out_spec, out_shape, aliases = pl.BlockSpec((rows, C), lambda i, ids: (i, 0)), jax.ShapeDtypeStruct((R, C), F32), {}
    if into is not None:
        buf, slot = into
        out_spec = pl.BlockSpec((None, rows, C), lambda i, ids: (slot, i, 0))
        out_shape = jax.ShapeDtypeStruct(buf.shape, F32)
        if not isinstance(buf, jax.ShapeDtypeStruct):
            body = _drop_ref(body, len(args))
            aliases = {len(args): 0}
            in_specs.append(ANY)
            args.append(buf)
    grid_spec = pltpu.PrefetchScalarGridSpec(
        num_scalar_prefetch=1, grid=(R // rows,), in_specs=in_specs, out_specs=out_spec)
    return pl.pallas_call(
        body, name=name, grid_spec=grid_spec, out_shape=out_shape, input_output_aliases=aliases,
        compiler_params=_cparams(("parallel",)),
    )(*args)


class _ReduceScatter:
    def __init__(self, tag, collective_id, big):
        x, y, c = _mesh_pos()
        self.tag, self.cid = tag, collective_id
        self.ids = jnp.stack([c, 2 * x + y]).astype(jnp.int32)
        self.big = big.reshape((4, 2) + big.shape[1:])
        (self.got,) = _rs_sibling_exchange(tag + "_sibling", collective_id, [self.big])

    def chip_sums(self, after=None):
        t16, self.own = _rs_chip_sum(self.tag + "_chip_sum", self.ids, self.big, self.got, after)
        (self.recv,) = _rs_chip_exchange(self.tag + "_chips", self.cid + 1, [t16])
        return t16

    def finish(self, after=None, into=None):
        return _rs_final_sum(self.tag + "_final", self.ids, self.recv, self.own, _rs_rows(self.own.shape[0]), after,
                             into)


def _all_reduce_small(tag, small, collective_id, launch_after):
    x, y, c = _mesh_pos()
    ids = jnp.stack([c, 2 * x + y]).astype(jnp.int32)
    (theirs,) = _rs_sibling_exchange(tag + "_sibling", None, [small])
    ts = _add2(tag + "_chip_sum", small, theirs)
    (recv,) = _rs_chip_exchange(tag + "_chips", collective_id, [ts], launch_after)
    return lambda after: _rs_final_sum(tag + "_final", ids, recv, ts, small.shape[0], after)


def _adamw(name, w, g, m, v):
    R, C = w.shape
    tiles = [t for t in range(8, 513, 8) if R % t == 0]
    tr = max(tiles) if tiles else R
    c1 = 1.0 - ADAM_B1 ** ADAM_STEP
    c2 = 1.0 - ADAM_B2 ** ADAM_STEP

    def body(w_ref, g_ref, m_ref, v_ref, d_ref, nm_ref, nv_ref):
        gv = g_ref[...]
        nm = ADAM_B1 * m_ref[...] + (1.0 - ADAM_B1) * gv
        nv = ADAM_B2 * v_ref[...] + (1.0 - ADAM_B2) * (gv * gv)
        d_ref[...] = -ADAM_LR * ((nm / c1) / (jnp.sqrt(nv / c2) + ADAM_EPS) + ADAM_WD * w_ref[...])
        nm_ref[...] = nm
        nv_ref[...] = nv

    blk = pl.BlockSpec((tr, C), lambda i: (i, 0))
    shp = jax.ShapeDtypeStruct((R, C), F32)
    return pl.pallas_call(
        body, name=name, grid=(R // tr,), in_specs=[blk] * 4, out_specs=(blk,) * 3, out_shape=(shp,) * 3,
        compiler_params=_cparams(("parallel",)),
    )(w, g, m, v)


def _adamw_t(name, w, g_t, m, v):
    L, R, C = w.shape
    tr = 256
    c1 = 1.0 - ADAM_B1 ** ADAM_STEP
    c2 = 1.0 - ADAM_B2 ** ADAM_STEP

    def body(w_ref, g_ref, m_ref, v_ref, go_ref, d_ref, nm_ref, nv_ref):
        gv = g_ref[...].T
        go_ref[...] = gv
        nm = ADAM_B1 * m_ref[...] + (1.0 - ADAM_B1) * gv
        nv = ADAM_B2 * v_ref[...] + (1.0 - ADAM_B2) * (gv * gv)
        d_ref[...] = -ADAM_LR * ((nm / c1) / (jnp.sqrt(nv / c2) + ADAM_EPS) + ADAM_WD * w_ref[...])
        nm_ref[...] = nm
        nv_ref[...] = nv

    blk = pl.BlockSpec((None, tr, C), lambda l, i: (l, i, 0))
    blk_t = pl.BlockSpec((None, C, tr), lambda l, i: (l, 0, i))
    shp = jax.ShapeDtypeStruct((L, R, C), F32)
    return pl.pallas_call(
        body, name=name, grid=(L, R // tr), in_specs=[blk, blk_t, blk, blk], out_specs=(blk,) * 4,
        out_shape=(shp,) * 4, compiler_params=_cparams(("parallel", "parallel")),
    )(w, g_t, m, v)


REST_FF1 = 0
REST_FF2 = 512
REST_WO = 1024
REST_UP = 1152
REST_ATTN = 1280
REST_ROWS = 1312
IN_ROWS = IN_COLS // N_DEV
ROWS_FF1 = (REST_FF1, D_FF // N_DEV)
ROWS_FF2 = (REST_FF2, D_FF // N_DEV)
ROWS_WO = (REST_WO, D // N_DEV)
SMALL = (("norm_mix", (DEPTH, D)), ("b_gate", (DEPTH, 3 * D)), ("pool_mix", (DEPTH, 4, POOL_G, POOL_G)),
         ("pool_scale", (DEPTH, POOL_W)), ("conv_w", (DEPTH, 3, CONV_W)), ("q_gain", (DEPTH, HEAD)),
         ("k_gain", (DEPTH, HEAD)), ("norm_mlp", (DEPTH, D)))


def _pack_weight_shards(w, l):
    b = lambda a: a.astype(BF16)
    rest = jnp.concatenate([
        b(w["w_ff1"][l].T), b(w["w_ff2"][l]), b(w["w_o"][l]),
        jnp.concatenate([b(w["w_pool_up"][l].T), b(w["w_conv_out"][l].T)], axis=1),
        b(w["w_attn_up"][l].T).reshape(REST_ROWS - REST_ATTN, D)], axis=0)
    return b(w["w_in"][l].T), rest


def _unpack_gathered(g_in, g_rest, g_conv, small_w, l):
    rest = g_rest
    take = lambda r0, rows, c0=0, cols=D: rest[:, r0:r0 + rows, c0:c0 + cols].reshape(N_DEV * rows, cols)
    conv = g_conv[:, 3 * l:3 * l + 3, :CONV_W // N_DEV]
    p = {
        "w_in_t": g_in.reshape(IN_COLS, D),
        "rest": rest,
        "w_pool_up_t": take(REST_UP, 128, 0, POOL_W), "w_conv_out_t": take(REST_UP, 128, POOL_W, CONV_W),
        "w_attn_up_t": rest[:, REST_ATTN:].reshape(D, ATTN_OUT),
        "conv_w": jnp.transpose(conv, (1, 0, 2)).reshape(3, CONV_W),
        "pool_mix": small_w["pool_mix"][l],
    }
    for name in ("norm_mix", "b_gate", "pool_scale", "q_gain", "k_gain", "norm_mlp"):
        p[name] = small_w[name][l][None, :]
    return p


def _pack_small_grads(grads, loss):
    flat = jnp.concatenate([jnp.stack([grads[l][name] for l in range(DEPTH)]).reshape(-1) for name, _ in SMALL]
                           + [loss.reshape(1)])
    return jnp.pad(flat, (0, (-flat.shape[0]) % (8 * 128))).reshape(-1, 128)


def _unpack_grads(rest):
    return {
        "w_ff1_t": jnp.stack([a[REST_FF1:REST_FF1 + 512] for a in rest]),
        "w_ff2": jnp.stack([a[REST_FF2:REST_FF2 + 512] for a in rest]),
        "w_o": jnp.stack([a[REST_WO:REST_WO + 128] for a in rest]),
        "w_pool_up": jnp.stack([a[REST_UP:REST_UP + 128, :POOL_W].T for a in rest]),
        "w_conv_out": jnp.stack([a[REST_UP:REST_UP + 128, POOL_W:].T for a in rest]),
        "w_attn_up": jnp.stack([a[REST_ATTN:].reshape(D // N_DEV, ATTN_OUT).T for a in rest]),
    }


def _unpack_small_like_grads(small, dev):
    out = {}
    flat = small.reshape(-1)
    off = 0
    for name, shp in SMALL:
        n = 1
        for s in shp:
            n *= s
        out[name] = flat[off:off + n].reshape(shp)
        off += n
    out["loss"] = flat[off]
    width = CONV_W // N_DEV
    out["conv_w"] = lax.dynamic_slice_in_dim(out["conv_w"], dev * width, width, axis=2)
    return out


WEIGHTS = ("norm_mix", "w_in", "b_gate", "pool_mix", "pool_scale", "conv_w", "q_gain", "k_gain", "w_pool_up",
           "w_conv_out", "w_attn_up", "w_o", "norm_mlp", "w_ff1", "w_ff2")


def kernel(x, norm_mix, w_in, b_gate, pool_mix, pool_scale, conv_w, q_gain, k_gain, w_pool_up, w_conv_out, w_attn_up, w_o, norm_mlp, w_ff1, w_ff2, loss_target, m_norm_mix, m_w_in, m_b_gate, m_pool_mix, m_pool_scale, m_conv_w, m_q_gain, m_k_gain, m_w_pool_up, m_w_conv_out, m_w_attn_up, m_w_o, m_norm_mlp, m_w_ff1, m_w_ff2, v_norm_mix, v_w_in, v_b_gate, v_pool_mix, v_pool_scale, v_conv_w, v_q_gain, v_k_gain, v_w_pool_up, v_w_conv_out, v_w_attn_up, v_w_o, v_norm_mlp, v_w_ff1, v_w_ff2):
    w = dict(zip(WEIGHTS, (norm_mix, w_in, b_gate, pool_mix, pool_scale, conv_w, q_gain, k_gain, w_pool_up,
                           w_conv_out, w_attn_up, w_o, norm_mlp, w_ff1, w_ff2)))
    m = dict(zip(WEIGHTS, (m_norm_mix, m_w_in, m_b_gate, m_pool_mix, m_pool_scale, m_conv_w, m_q_gain, m_k_gain,
                           m_w_pool_up, m_w_conv_out, m_w_attn_up, m_w_o, m_norm_mlp, m_w_ff1, m_w_ff2)))
    v = dict(zip(WEIGHTS, (v_norm_mix, v_w_in, v_b_gate, v_pool_mix, v_pool_scale, v_conv_w, v_q_gain, v_k_gain,
                           v_w_pool_up, v_w_conv_out, v_w_attn_up, v_w_o, v_norm_mlp, v_w_ff1, v_w_ff2)))
    xi, yi, ci = _mesh_pos()
    dev = 4 * xi + 2 * yi + ci

    saved, params = [], []
    act = x[0]
    w_in_t, m_in_t, v_in_t = (jnp.swapaxes(src["w_in"], 1, 2) for src in (w, m, v))
    h = _rmsnorm_fwd("l0_norm_mix", act, w["norm_mix"][0][None])
    conv_shard = jnp.pad(w["conv_w"].reshape(DEPTH * 3, CONV_W // N_DEV), ((0, 8 - DEPTH * 3), (0, 128 - CONV_W // N_DEV)))
    g_conv = _all_gather("gather_conv", None, conv_shard)
    for l in range(DEPTH):
        s_in, s_rest = _pack_weight_shards(w, l)
        g_in = _all_gather(f"gather_in{l}", 1 + 2 * l, s_in)
        g_rest = _all_gather(f"gather_rest{l}", 2 + 2 * l, s_rest)
        params.append(_unpack_gathered(g_in, g_rest, g_conv, w, l))
    for l in range(DEPTH):
        act, h, s = _layer_fwd(l, act, h, params[l], w["norm_mix"][l + 1][None] if l + 1 < DEPTH else None,
                               loss_target[0])
        saved.append(s)
    dy = act
    loss = _fold_rows("loss_sum", h, lanes=True, scale=0.5 / D)
    rest, grads, pending = [None] * DEPTH, [None] * DEPTH, None
    in_t = jax.ShapeDtypeStruct((DEPTH, IN_ROWS, D), F32)
    for l in reversed(range(DEPTH)):
        dy, grads[l], rest[l], in_t, pending = _layer_bwd(l, dy, params[l], saved[l], pending, in_t, 5 + 4 * l)
    dx = dy[0]
    launched = pending.chip_sums(after=dx)
    small_finish = _all_reduce_small("ar_small", _pack_small_grads(grads, loss), 5 + 4 * DEPTH, launched)
    g = _unpack_grads(rest)

    delta, new_m, new_v = {}, {}, {}
    g["w_ff1"], delta["w_ff1"], new_m["w_ff1"], new_v["w_ff1"] = _adamw_t("adamw_w_ff1", w["w_ff1"], g["w_ff1_t"],
                                                                          m["w_ff1"], v["w_ff1"])
    for name in ("w_pool_up", "w_conv_out", "w_attn_up", "w_o", "w_ff2"):
        shp = w[name].shape
        two_d = (shp[0] * shp[1], shp[2])
        d_, m_, v_ = _adamw("adamw_" + name, w[name].reshape(two_d), g[name].reshape(two_d),
                            m[name].reshape(two_d), v[name].reshape(two_d))
        delta[name], new_m[name], new_v[name] = d_.reshape(shp), m_.reshape(shp), v_.reshape(shp)
    g_t = pending.finish(after=new_v["w_ff2"], into=(in_t, 0))
    two_d = (g_t.shape[0] * g_t.shape[1], g_t.shape[2])
    d_, m_, v_ = _adamw("adamw_w_in", *[a.reshape(two_d) for a in (w_in_t, g_t, m_in_t, v_in_t)])
    g["w_in"] = jnp.swapaxes(g_t, 1, 2)
    delta["w_in"], new_m["w_in"], new_v["w_in"] = (jnp.swapaxes(a.reshape(g_t.shape), 1, 2) for a in (d_, m_, v_))
    g.update(_unpack_small_like_grads(small_finish(after=v_), dev))
    total = g.pop("loss")
    for name, shp in SMALL:
        shp = w[name].shape
        two_d = (-1, shp[-1])
        d_, m_, v_ = _adamw("adamw_" + name, w[name].reshape(two_d), g[name].reshape(two_d),
                            m[name].reshape(two_d), v[name].reshape(two_d))
        delta[name], new_m[name], new_v[name] = d_.reshape(shp), m_.reshape(shp), v_.reshape(shp)

    return (total, dx[None], *[g[n] for n in WEIGHTS], *[delta[n] for n in WEIGHTS],
            *[new_m[n] for n in WEIGHTS], *[new_v[n] for n in WEIGHTS])
```

```python
import functools

import jax
import jax.numpy as jnp
from jax import lax
from jax.experimental import pallas as pl
from jax.experimental.pallas import tpu as pltpu
from jax.experimental.pallas import tpu_sc as plsc

F32 = jnp.float32
BF16 = jnp.bfloat16
MESH = pl.DeviceIdType.MESH

D = 1024
SEQ = 4096
DEPTH = 2
N_DEV = 8
POOL_WINDOWS = (2, 4, 8, 16)
POOL_W = 512
POOL_G = 128
CONV_W = 512
HEAD = 64
ATTN_GROUPS = ((128, 1), (512, 4), (2048, 16))
HPG = 4
ATTN_W = 768
ATTN_OUT = 256
ATTN_BLK = 128
D_FF = 4096
EPS = 1e-6
MASK = -1e30
OFF_POOL = 0
OFF_CB = 512
OFF_CC = 1024
OFF_CX = 1536
OFF_Q = 2048
OFF_K = 2816
OFF_V = 3584
OFF_GATE = 4352
IN_COLS = 7424
ADAM_LR = 0.001
ADAM_B1 = 0.9
ADAM_B2 = 0.999
ADAM_EPS = 1e-08
ADAM_WD = 0.01
ADAM_STEP = 10

ROW_TILE = 512
SEQ_CHUNK = 256
HALO = 16
VMEM_LIMIT = 56 * 1024 * 1024


def _cparams(sem=None):
    return pltpu.CompilerParams(dimension_semantics=sem, vmem_limit_bytes=VMEM_LIMIT)


def _call(body, args, in_specs, after=None, **kw):
    if after is not None:
        after = list(after) if isinstance(after, (list, tuple)) else [after]
        n, k, inner = len(args), len(after), body

        def body(*refs):
            return inner(*refs[:n], *refs[n + k:])

        args = list(args) + after
        in_specs = list(in_specs) + [pl.BlockSpec(memory_space=pl.ANY)] * k
    return pl.pallas_call(body, in_specs=in_specs, **kw)(*args)


_DIMS = {"nn": (((1,), (0,)), ((), ())), "nt": (((1,), (1,)), ((), ())), "tn": (((0,), (0,)), ((), ()))}


def _matmul(name, a, b, mode, tm, tn, tk, out_dtypes=(F32,), extras=(), epilogue=None, into=None, after=None,
            b_rows=None):
    if mode == "tn":
        K, M = a.shape
    else:
        M, K = a.shape
    b_shape = b.shape if b_rows is None else (b.shape[0] * b_rows[1], b.shape[2])
    N = b_shape[0] if mode == "nt" else b_shape[1]
    assert b_shape[1 if mode == "nt" else 0] == K, (name, b_shape, K)
    assert M % tm == 0 and N % tn == 0 and K % tk == 0, (name, M, N, K, tm, tn, tk)
    nk = K // tk
    n_extra = len(extras)
    n_out = len(out_dtypes)
    dims = _DIMS[mode]
    n_alias = 0 if into is None or isinstance(into[0], jax.ShapeDtypeStruct) else 1

    def body(a_ref, b_ref, *rest):
        extra_refs = rest[:n_extra]
        out_refs = rest[n_extra + n_alias:n_extra + n_alias + n_out]

        def finish(acc):
            if epilogue is None:
                res = (acc,)
            else:
                res = epilogue(acc, *[r[...] for r in extra_refs])
            for o_ref, v in zip(out_refs, res):
                o_ref[...] = v.astype(o_ref.dtype)

        bv = b_ref[...]
        if b_rows is not None:
            bv = bv.reshape(-1, bv.shape[-1])
        part = lax.dot_general(a_ref[...].astype(BF16), bv.astype(BF16), dims, preferred_element_type=F32)
        if nk == 1:
            finish(part)
        else:
            acc_ref = rest[-1]
            k = pl.program_id(2)

            @pl.when(k == 0)
            def _():
                acc_ref[...] = part

            @pl.when(k > 0)
            def _():
                acc_ref[...] += part

            @pl.when(k == nk - 1)
            def _():
                finish(acc_ref[...])

    if mode == "tn":
        a_spec = pl.BlockSpec((tk, tm), lambda i, j, k: (k, i))
    else:
        a_spec = pl.BlockSpec((tm, tk), lambda i, j, k: (i, k))
    if b_rows is not None:
        row0, rows = b_rows
        t_rows = tn if mode == "nt" else tk
        assert t_rows % rows == 0 and row0 % rows == 0, (name, t_rows, b_rows)
        if mode == "nt":
            b_spec = pl.BlockSpec((t_rows // rows, rows, tk), lambda i, j, k: (j, row0 // rows, k))
        else:
            b_spec = pl.BlockSpec((t_rows // rows, rows, tn), lambda i, j, k: (k, row0 // rows, j))
    elif mode == "nt":
        b_spec = pl.BlockSpec((tn, tk), lambda i, j, k: (j, k))
    else:
        b_spec = pl.BlockSpec((tk, tn), lambda i, j, k: (k, j))
    in_specs = [a_spec, b_spec]
    args = [a, b]
    for arr, kind in extras:
        if kind == "mn":
            in_specs.append(pl.BlockSpec((tm, tn), lambda i, j, k: (i, j)))
        else:
            in_specs.append(pl.BlockSpec((1, tn), lambda i, j, k: (0, j)))
        args.append(arr)
    part = [isinstance(dt, tuple) for dt in out_dtypes]
    out_shape = tuple(jax.ShapeDtypeStruct((M // tm * 8, N), dt[0]) if p else jax.ShapeDtypeStruct((M, N), dt)
                      for dt, p in zip(out_dtypes, part))
    out_specs = tuple(pl.BlockSpec((8 if p else tm, tn), lambda i, j, k: (i, j)) for p in part)
    aliases = {}
    if into is not None:
        buf, row0, col0 = into
        assert n_out == 1 and (M // N_DEV) % tm == 0 and row0 % tm == 0 and col0 % tn == 0
        per_dev = M // N_DEV // tm
        out_shape = (jax.ShapeDtypeStruct(buf.shape, buf.dtype),)
        out_specs = (pl.BlockSpec((None, tm, tn), lambda i, j, k: (i // per_dev, row0 // tm + i % per_dev,
                                                                   col0 // tn + j)),)
        if not isinstance(buf, jax.ShapeDtypeStruct):
            aliases = {len(args): 0}
            in_specs.append(pl.BlockSpec(memory_space=pl.ANY))
            args.append(buf)
    scratch = [] if nk == 1 else [pltpu.VMEM((tm, tn), F32)]
    res = _call(
        body, args, in_specs, after, name=name, grid=(M // tm, N // tn, nk), out_specs=out_specs,
        out_shape=out_shape, scratch_shapes=scratch, input_output_aliases=aliases,
        compiler_params=_cparams(("parallel", "parallel", "arbitrary")))
    return res if n_out > 1 else res[0]


def _rmsnorm_fwd(name, x, gain, after=None):
    S_, D_ = x.shape

    def body(x_ref, g_ref, h_ref):
        xf = x_ref[...]
        ms = jnp.mean(xf * xf, axis=-1, keepdims=True)
        h_ref[...] = (xf * lax.rsqrt(ms + EPS) * g_ref[...]).astype(BF16)

    return _call(
        body, [x, gain], [pl.BlockSpec((ROW_TILE, D_), lambda i: (i, 0)), pl.BlockSpec((1, D_), lambda i: (0, 0))],
        after, name=name, grid=(S_ // ROW_TILE,),
        out_specs=pl.BlockSpec((ROW_TILE, D_), lambda i: (i, 0)),
        out_shape=jax.ShapeDtypeStruct((S_, D_), BF16),
        compiler_params=_cparams(("parallel",)))


def _rows_with_halo(ref, cols, i, n_chunks, before, after):
    r0 = pl.multiple_of(i * SEQ_CHUNK, SEQ_CHUNK)
    parts = []
    if before:
        h0 = pl.multiple_of(jnp.maximum(r0 - HALO, 0), 8)
        halo = ref[pl.ds(h0, HALO), cols]
        parts.append(jnp.where(i > 0, halo, jnp.zeros_like(halo)))
    parts.append(ref[pl.ds(r0, SEQ_CHUNK), cols])
    if after:
        a0 = pl.multiple_of(jnp.minimum(r0 + SEQ_CHUNK, (n_chunks - 1) * SEQ_CHUNK + SEQ_CHUNK - HALO), 8)
        halo = ref[pl.ds(a0, HALO), cols]
        parts.append(jnp.where(i < n_chunks - 1, halo, jnp.zeros_like(halo)))
    return parts[0] if len(parts) == 1 else jnp.concatenate(parts, axis=0)


def _shift_down(v, k):
    return pltpu.roll(v, k, 0)


def _shift_up(v, k):
    return pltpu.roll(v, v.shape[0] - k, 0)


def _pool_diff(xx, w, t_main):
    s = xx
    k = 1
    while k < w:
        s = s + _shift_down(s, k)
        k *= 2
    cnt = jnp.minimum(t_main + 1, w).astype(F32)
    return s[HALO:] / cnt - xx[HALO:]


def _pool_fwd(name, z, pool_mix, pool_scale):
    S_ = z.shape[0]
    n_chunks = S_ // SEQ_CHUNK

    cols = slice(0, POOL_G)

    def body(u_ref, mix_ref, sc_ref, y_ref):
        mixg = mix_ref[...].astype(BF16)
        scg = sc_ref[...]
        for g, w in enumerate(POOL_WINDOWS):
            @pl.when(pl.program_id(0) == g)
            def _(w=w):
                def chunk(i, carry):
                    r0 = pl.multiple_of(i * SEQ_CHUNK, SEQ_CHUNK)
                    xx = _rows_with_halo(u_ref, cols, i, n_chunks, True, False)
                    t = r0 + lax.broadcasted_iota(jnp.int32, (SEQ_CHUNK, POOL_G), 0)
                    d = _pool_diff(xx, w, t)
                    y = jnp.dot(d.astype(BF16), mixg, preferred_element_type=F32) * scg
                    y_ref[pl.ds(r0, SEQ_CHUNK), :] = y.astype(BF16)
                    return carry

                lax.fori_loop(0, n_chunks, chunk, 0)

    slab = pl.BlockSpec((S_, POOL_G), lambda g: (0, g))
    return pl.pallas_call(
        body, name=name, grid=(len(POOL_WINDOWS),),
        in_specs=[slab, pl.BlockSpec((None, POOL_G, POOL_G), lambda g: (g, 0, 0)),
                  pl.BlockSpec((1, POOL_G), lambda g: (0, g))],
        out_specs=slab, out_shape=jax.ShapeDtypeStruct((S_, POOL_W), BF16),
        compiler_params=_cparams(("parallel",)),
    )(z, pool_mix, pool_scale)


def _pool_bwd(name, z, dy, pool_mix, pool_scale, dz, after=None):
    S_ = z.shape[0]
    n_chunks = S_ // SEQ_CHUNK
    rows_a = SEQ_CHUNK + HALO

    cols = slice(0, POOL_G)

    def body(u_ref, dy_ref, mix_ref, sc_ref, du_ref, dmix_ref, dsc_ref):
        mixg = mix_ref[...].astype(BF16)
        scg = sc_ref[...]
        for g, w in enumerate(POOL_WINDOWS):
            @pl.when(pl.program_id(0) == g)
            def _(w=w):
                def chunk(i, carry):
                    dmix_acc, dsc_acc = carry
                    r0 = pl.multiple_of(i * SEQ_CHUNK, SEQ_CHUNK)
                    xx = _rows_with_halo(u_ref, cols, i, n_chunks, True, False)
                    t = r0 + lax.broadcasted_iota(jnp.int32, (SEQ_CHUNK, POOL_G), 0)
                    d = _pool_diff(xx, w, t).astype(BF16)
                    ypre = jnp.dot(d, mixg, preferred_element_type=F32)
                    dyy = _rows_with_halo(dy_ref, cols, i, n_chunks, False, True)
                    dys = (dyy * scg).astype(BF16)
                    dsc_acc = dsc_acc + jnp.sum((dyy[:SEQ_CHUNK] * ypre).reshape(SEQ_CHUNK // 8, 8, POOL_G), axis=0)
                    dmix_acc = dmix_acc + lax.dot_general(d, dys[:SEQ_CHUNK], _DIMS["tn"],
                                                          preferred_element_type=F32)
                    dd = lax.dot_general(dys, mixg, _DIMS["nt"], preferred_element_type=F32)
                    ta = r0 + lax.broadcasted_iota(jnp.int32, (rows_a, POOL_G), 0)
                    f = dd / jnp.minimum(ta + 1, w).astype(F32)
                    k = 1
                    while k < w:
                        f = f + _shift_up(f, k)
                        k *= 2
                    du_ref[pl.ds(r0, SEQ_CHUNK), :] = (f[:SEQ_CHUNK] - dd[:SEQ_CHUNK]).astype(BF16)
                    return dmix_acc, dsc_acc

                dmix_acc, dsc_acc = lax.fori_loop(
                    0, n_chunks, chunk, (jnp.zeros((POOL_G, POOL_G), F32), jnp.zeros((8, POOL_G), F32)))
                dmix_ref[...] = dmix_acc
                dsc_ref[...] = jnp.sum(dsc_acc, axis=0, keepdims=True)

    slab = pl.BlockSpec((S_, POOL_G), lambda g: (0, g))
    mix_spec = pl.BlockSpec((None, POOL_G, POOL_G), lambda g: (g, 0, 0))
    vec = pl.BlockSpec((1, POOL_G), lambda g: (0, g))
    return _call(
        _drop_ref(body, 4), [z, dy, pool_mix, pool_scale, dz], [slab, slab, mix_spec, vec, ANY], after, name=name,
        grid=(len(POOL_WINDOWS),), out_specs=(slab, mix_spec, vec),
        out_shape=(jax.ShapeDtypeStruct(dz.shape, dz.dtype), jax.ShapeDtypeStruct((4, POOL_G, POOL_G), F32),
                   jax.ShapeDtypeStruct((1, POOL_W), F32)),
        input_output_aliases={4: 0}, compiler_params=_cparams(("parallel",)))


def _conv_specs(S_):
    slab = lambda off: pl.BlockSpec((S_, 128), lambda c, off=off: (0, off // 128 + c))
    return slab(OFF_CB), slab(OFF_CC), slab(OFF_CX)


def _conv_fwd(name, z, conv_w, after=None):
    S_ = z.shape[0]
    n_chunks = S_ // SEQ_CHUNK
    col = slice(0, 128)

    def body(b_ref, c_ref, x_ref, w_ref, y_ref):
        w0, w1, w2 = w_ref[0:1, :], w_ref[1:2, :], w_ref[2:3, :]

        def chunk(i, carry):
            r0 = pl.multiple_of(i * SEQ_CHUNK, SEQ_CHUNK)
            u = _rows_with_halo(c_ref, col, i, n_chunks, True, False) * _rows_with_halo(x_ref, col, i, n_chunks, True, False)
            y = w2 * u + w1 * _shift_down(u, 1) + w0 * _shift_down(u, 2)
            y_ref[pl.ds(r0, SEQ_CHUNK), :] = (b_ref[pl.ds(r0, SEQ_CHUNK), :] * y[HALO:]).astype(BF16)
            return carry

        lax.fori_loop(0, n_chunks, chunk, 0)

    sb, sc, sx = _conv_specs(S_)
    return _call(
        body, [z, z, z, conv_w], [sb, sc, sx, pl.BlockSpec((3, 128), lambda c: (0, c))], after,
        name=name, grid=(CONV_W // 128,),
        out_specs=pl.BlockSpec((S_, 128), lambda c: (0, c)),
        out_shape=jax.ShapeDtypeStruct((S_, CONV_W), BF16),
        compiler_params=_cparams(("parallel",)))


def _conv_bwd(name, z, dy, conv_w, after=None):
    S_ = z.shape[0]
    n_chunks = S_ // SEQ_CHUNK
    col = slice(0, 128)
    lo, hi = HALO, HALO + SEQ_CHUNK

    def body(b_ref, c_ref, x_ref, dy_ref, w_ref, db_ref, dc_ref, dx_ref, dw_ref):
        w0, w1, w2 = w_ref[0:1, :], w_ref[1:2, :], w_ref[2:3, :]

        def chunk(i, carry):
            a0, a1, a2 = carry
            r0 = pl.multiple_of(i * SEQ_CHUNK, SEQ_CHUNK)
            cc = _rows_with_halo(c_ref, col, i, n_chunks, True, True)
            xx = _rows_with_halo(x_ref, col, i, n_chunks, True, True)
            bb = _rows_with_halo(b_ref, col, i, n_chunks, True, True)
            dyy = _rows_with_halo(dy_ref, col, i, n_chunks, True, True)
            u = cc * xx
            u1 = _shift_down(u, 1)
            u2 = _shift_down(u, 2)
            y = w2 * u + w1 * u1 + w0 * u2
            dyv = dyy * bb
            du = w2 * dyv + w1 * _shift_up(dyv, 1) + w0 * _shift_up(dyv, 2)
            db_ref[pl.ds(r0, SEQ_CHUNK), :] = (dyy[lo:hi] * y[lo:hi]).astype(BF16)
            dc_ref[pl.ds(r0, SEQ_CHUNK), :] = (du[lo:hi] * xx[lo:hi]).astype(BF16)
            dx_ref[pl.ds(r0, SEQ_CHUNK), :] = (du[lo:hi] * cc[lo:hi]).astype(BF16)
            red = lambda v: jnp.sum(v.reshape(SEQ_CHUNK // 8, 8, 128), axis=0)
            dm = dyv[lo:hi]
            return a0 + red(dm * u2[lo:hi]), a1 + red(dm * u1[lo:hi]), a2 + red(dm * u[lo:hi])

        zero = jnp.zeros((8, 128), F32)
        a0, a1, a2 = lax.fori_loop(0, n_chunks, chunk, (zero, zero, zero))
        dw_ref[0:1, :] = jnp.sum(a0, axis=0, keepdims=True)
        dw_ref[1:2, :] = jnp.sum(a1, axis=0, keepdims=True)
        dw_ref[2:3, :] = jnp.sum(a2, axis=0, keepdims=True)

    sb, sc, sx = _conv_specs(S_)
    slab = pl.BlockSpec((S_, 128), lambda c: (0, c))
    wspec = pl.BlockSpec((3, 128), lambda c: (0, c))
    act = jax.ShapeDtypeStruct((S_, CONV_W), BF16)
    return _call(
        body, [z, z, z, dy, conv_w], [sb, sc, sx, slab, wspec], after, name=name, grid=(CONV_W // 128,),
        out_specs=(slab, slab, slab, wspec),
        out_shape=(act, act, act, jax.ShapeDtypeStruct((3, CONV_W), F32)),
        compiler_params=_cparams(("parallel",)))


def _head_ones(pw):
    a = lax.broadcasted_iota(jnp.int32, (pw, pw), 0) // HEAD
    b = lax.broadcasted_iota(jnp.int32, (pw, pw), 1) // HEAD
    return (a == b).astype(BF16)


def _head_sum(v, ones):
    hi = v.astype(BF16)
    lo = (v - hi.astype(F32)).astype(BF16)
    return jnp.dot(hi, ones, preferred_element_type=F32) + jnp.dot(lo, ones, preferred_element_type=F32)


def _head_norm(v, gain, ones):
    rstd = lax.rsqrt(_head_sum(v * v, ones) * (1.0 / HEAD) + EPS)
    xhat = v * rstd
    return xhat * gain, xhat, rstd


def _head_norm_bwd(dy, xhat, rstd, gain, ones):
    dxhat = dy * gain
    c = _head_sum(dxhat * xhat, ones) * (1.0 / HEAD)
    dv = rstd * (dxhat - xhat * c)
    dg = jnp.sum((dy * xhat).reshape(dy.shape[0] // 8, 8, dy.shape[1]), axis=0)
    return dv, dg


def _head_masks(pw):
    lane_head = lax.broadcasted_iota(jnp.int32, (1, pw), 1) // HEAD
    return [lane_head == h for h in range(pw // HEAD)]


def _only(mask, v):
    return jnp.where(mask, v, jnp.zeros_like(v))


def _attn_specs(S_, g, dil):
    rows = ATTN_BLK * dil
    nb = S_ // rows
    pw = 128 if dil > 1 else ATTN_OUT
    cq, ck, cv = ((OFF_Q + g * ATTN_OUT) // pw, (OFF_K + g * ATTN_OUT) // pw, (OFF_V + g * ATTN_OUT) // pw)
    return rows, nb, pw, pw // HEAD, ATTN_OUT // pw, cq, ck, cv


ATTN_BATCH = 4


def _attn_group(dil):
    return 4 if dil == 1 else 1


def _attn_block_specs(rows, grp, pw, last=None):
    step = (lambda n: n) if last is None else (lambda n: jnp.minimum(n, last))
    cur = lambda c: pl.BlockSpec((rows * grp, pw), lambda hp, n, c=c: (step(n), c + hp))
    prev = lambda c: pl.BlockSpec((rows, pw), lambda hp, n, c=c: (jnp.maximum(step(n) * grp - 1, 0), c + hp))
    return cur, prev


def _band_mask(has_prev):
    qi = lax.broadcasted_iota(jnp.int32, (ATTN_BLK, 2 * ATTN_BLK), 0)
    ki = lax.broadcasted_iota(jnp.int32, (ATTN_BLK, 2 * ATTN_BLK), 1)
    in_prev = jnp.logical_and(ki < ATTN_BLK, ki >= qi)
    if has_prev is not True:
        in_prev = jnp.logical_and(in_prev, has_prev)
    return jnp.logical_or(in_prev, jnp.logical_and(ki >= ATTN_BLK, ki - ATTN_BLK <= qi))


def _rows_of(ref, r, dil):
    if dil == 1:
        return ref[r * ATTN_BLK:(r + 1) * ATTN_BLK, :]
    return ref[pl.ds(r, ATTN_BLK, stride=dil), :]


def _put_rows(ref, r, dil, val):
    if dil == 1:
        ref[r * ATTN_BLK:(r + 1) * ATTN_BLK, :] = val.astype(ref.dtype)
    else:
        ref[pl.ds(r, ATTN_BLK, stride=dil), :] = val.astype(ref.dtype)


def _attn_fwd(name, z, q_gain, k_gain, g, dil):
    S_ = z.shape[0]
    rows, nb, pw, heads, npairs, cq, ck, cv = _attn_specs(S_, g, dil)
    scale = HEAD ** -0.5

    grp = _attn_group(dil)
    nsteps = nb // grp

    def body(q_ref, k_ref, kp_ref, v_ref, vp_ref, gq_ref, gk_ref, o_ref, l_ref):
        n = pl.program_id(1)
        ones, hmask = _head_ones(pw), _head_masks(pw)
        gq, gk = jnp.tile(gq_ref[...], (1, heads)), jnp.tile(gk_ref[...], (1, heads))
        mask_first, mask_rest = _band_mask(n > 0), _band_mask(True)
        for r0 in range(0, dil * grp, ATTN_BATCH):
            rs = range(r0, min(r0 + ATTN_BATCH, dil * grp))
            qn, kn, vv, s, p = {}, {}, {}, {}, {}
            kcn = {}
            for r in rs:
                q, kc, vc = _rows_of(q_ref, r, dil), _rows_of(k_ref, r, dil), _rows_of(v_ref, r, dil)
                kcn[r] = _head_norm(kc, gk, ones)[0]
                if dil == 1 and r > 0:
                    kpn = kcn[r - 1] if r - 1 in kcn else _head_norm(_rows_of(k_ref, r - 1, dil), gk, ones)[0]
                    vp = _rows_of(v_ref, r - 1, dil)
                else:
                    kpn, vp = _head_norm(_rows_of(kp_ref, r, dil), gk, ones)[0], _rows_of(vp_ref, r, dil)
                qn[r] = _head_norm(q, gq, ones)[0].astype(BF16)
                kn[r] = jnp.concatenate([kpn, kcn[r]], axis=0).astype(BF16)
                vv[r] = jnp.concatenate([vp, vc], axis=0).astype(BF16)
            keys = [(r, h) for r in rs for h in range(heads)]
            for r, h in keys:
                s[r, h] = lax.dot_general(_only(hmask[h], qn[r]), kn[r], _DIMS["nt"],
                                          preferred_element_type=F32) * scale
            lse, den = {}, {}
            for key in keys:
                mask = mask_rest if (dil == 1 and key[0] > 0) else mask_first
                sm = jnp.where(mask, s[key], MASK)
                m = jnp.max(sm, axis=-1, keepdims=True)
                e = jnp.exp(sm - m)
                den[key] = jnp.sum(e, axis=-1, keepdims=True)
                p[key] = e.astype(BF16)
                lse[key] = m + jnp.log(den[key])
            for r in rs:
                out = jnp.zeros((ATTN_BLK, pw), F32)
                lse_all = jnp.zeros((ATTN_BLK, pw), F32)
                for h in range(heads):
                    out = jnp.where(hmask[h], jnp.dot(p[r, h], vv[r], preferred_element_type=F32) / den[r, h], out)
                    lse_all = jnp.where(hmask[h], lse[r, h], lse_all)
                _put_rows(o_ref, r, dil, out)
                _put_rows(l_ref, r, dil, lse_all)

    cur, prev = _attn_block_specs(rows, grp, pw)
    gspec = pl.BlockSpec((1, HEAD), lambda hp, n: (0, 0))
    shp = jax.ShapeDtypeStruct((S_, ATTN_OUT), F32)
    return pl.pallas_call(
        body, name=name, grid=(npairs, nsteps),
        in_specs=[cur(cq), cur(ck), prev(ck), cur(cv), prev(cv), gspec, gspec],
        out_specs=(cur(0), cur(0)), out_shape=(shp, shp),
        compiler_params=_cparams(("parallel", "parallel")),
    )(z, z, z, z, z, q_gain, k_gain)


def _attn_combine(name, os_, ls_):
    S_ = os_[0].shape[0]

    def body(o0, o1, o2, l0, l1, l2, o_ref, l_ref):
        a, b, c = l0[...], l1[...], l2[...]
        m = jnp.maximum(jnp.maximum(a, b), c)
        ea, eb, ec = jnp.exp(a - m), jnp.exp(b - m), jnp.exp(c - m)
        zsum = ea + eb + ec
        o_ref[...] = (ea * o0[...] + eb * o1[...] + ec * o2[...]) / zsum
        l_ref[...] = m + jnp.log(zsum)

    row = pl.BlockSpec((ROW_TILE, ATTN_OUT), lambda i: (i, 0))
    shp = jax.ShapeDtypeStruct((S_, ATTN_OUT), F32)
    return pl.pallas_call(
        body, name=name, grid=(S_ // ROW_TILE,), in_specs=[row] * 6, out_specs=(row, row), out_shape=(shp, shp),
        compiler_params=_cparams(("parallel",)),
    )(*os_, *ls_)


def _attn_bwd(name, z, q_gain, k_gain, do, o, lse, g, dil, after=None):
    S_ = z.shape[0]
    rows, nb, pw, heads, npairs, cq, ck, cv = _attn_specs(S_, g, dil)
    scale = HEAD ** -0.5

    def body(q_ref, kc_ref, kp_ref, vc_ref, vp_ref, gq_ref, gk_ref, do_ref, o_ref, l_ref,
             dq_ref, dk_ref, dv_ref, dgq_ref, dgk_ref, ck_ref, cvv_ref, gq_acc, gk_acc):
        hp = pl.program_id(0)
        n = pl.program_id(1)
        ones = _head_ones(pw)
        gq, gk = jnp.tile(gq_ref[...], (1, heads)), jnp.tile(gk_ref[...], (1, heads))

        @pl.when(jnp.logical_and(n == 0, hp == 0))
        def _():
            gq_acc[...] = jnp.zeros_like(gq_acc)
            gk_acc[...] = jnp.zeros_like(gk_acc)

        refs = (q_ref, kc_ref, kp_ref, vc_ref, vp_ref, do_ref, o_ref, l_ref, dq_ref, dk_ref, dv_ref, ck_ref, cvv_ref,
                gq_acc, gk_acc)
        pl.when(n == 0)(functools.partial(query_step, False, ones, gq, gk, *refs))
        pl.when(jnp.logical_and(n > 0, n < nb))(functools.partial(query_step, True, ones, gq, gk, *refs))

        @pl.when(n == nb)
        def _():
            dgk = jnp.zeros((8, pw), F32)
            for r in range(dil):
                _, kphat, kprstd = _head_norm(_rows_of(kp_ref, r, dil), gk, ones)
                dk_all, dg = _head_norm_bwd(_rows_of(ck_ref, r, dil), kphat, kprstd, gk, ones)
                dgk = dgk + dg
                _put_rows(dk_ref, r, dil, dk_all)
                _put_rows(dv_ref, r, dil, _rows_of(cvv_ref, r, dil))
            gk_acc[...] += dgk

        @pl.when(jnp.logical_and(n == nb, hp == npairs - 1))
        def _():
            fold = lambda a: sum(a[:, h * HEAD:(h + 1) * HEAD] for h in range(heads))
            dgq_ref[...] = jnp.sum(fold(gq_acc[...]), axis=0, keepdims=True)
            dgk_ref[...] = jnp.sum(fold(gk_acc[...]), axis=0, keepdims=True)

    def query_step(has_prev, ones, gq, gk, q_ref, kc_ref, kp_ref, vc_ref, vp_ref, do_ref, o_ref, l_ref,
                   dq_ref, dk_ref, dv_ref, ck_ref, cvv_ref, gq_acc, gk_acc):
        n_keys = (2 if has_prev else 1) * ATTN_BLK
        mask = _band_mask(True)
        if not has_prev:
            mask = mask[:, ATTN_BLK:]
        hmask = _head_masks(pw)
        dgq = jnp.zeros((8, pw), F32)
        dgk = jnp.zeros((8, pw), F32)
        for r0 in range(0, dil, ATTN_BATCH):
            rs = range(r0, min(r0 + ATTN_BATCH, dil))
            keys = [(r, h) for r in rs for h in range(heads)]
            qn, qhat, qrstd, kn, kphat, kprstd, vv, dob, delta, lse = ({} for _ in range(10))
            for r in rs:
                q, kc = _rows_of(q_ref, r, dil), _rows_of(kc_ref, r, dil)
                dov = _rows_of(do_ref, r, dil)
                qn_f, qhat[r], qrstd[r] = _head_norm(q, gq, ones)
                qn[r] = qn_f.astype(BF16)
                kcn, vc = _head_norm(kc, gk, ones)[0], _rows_of(vc_ref, r, dil)
                if has_prev:
                    kpn, kphat[r], kprstd[r] = _head_norm(_rows_of(kp_ref, r, dil), gk, ones)
                    kn[r] = jnp.concatenate([kpn, kcn], axis=0).astype(BF16)
                    vv[r] = jnp.concatenate([_rows_of(vp_ref, r, dil), vc], axis=0).astype(BF16)
                else:
                    kn[r], vv[r] = kcn.astype(BF16), vc.astype(BF16)
                dob[r] = dov.astype(BF16)
                delta[r] = _head_sum(dov * _rows_of(o_ref, r, dil), ones)
                lse[r] = _rows_of(l_ref, r, dil)
            s, dp = {}, {}
            for r, h in keys:
                s[r, h] = lax.dot_general(_only(hmask[h], qn[r]), kn[r], _DIMS["nt"],
                                          preferred_element_type=F32) * scale
                dp[r, h] = lax.dot_general(_only(hmask[h], dob[r]), vv[r], _DIMS["nt"], preferred_element_type=F32)
            p, ds = {}, {}
            for r, h in keys:
                col = slice(h * HEAD, h * HEAD + 1)
                pk = jnp.where(mask, jnp.exp(jnp.where(mask, s[r, h], MASK) - lse[r][:, col]), 0.0)
                ds[r, h] = (pk * (dp[r, h] - delta[r][:, col]) * scale).astype(BF16)
                p[r, h] = pk.astype(BF16)
            dqn, dkn, dvv = {}, {}, {}
            for r in rs:
                dqn[r] = jnp.zeros((ATTN_BLK, pw), F32)
                dkn[r] = jnp.zeros((n_keys, pw), F32)
                dvv[r] = jnp.zeros((n_keys, pw), F32)
                for h in range(heads):
                    dqn[r] = jnp.where(hmask[h], jnp.dot(ds[r, h], kn[r], preferred_element_type=F32), dqn[r])
                    dkn[r] = jnp.where(hmask[h], lax.dot_general(ds[r, h], qn[r], _DIMS["tn"],
                                                                 preferred_element_type=F32), dkn[r])
                    dvv[r] = jnp.where(hmask[h], lax.dot_general(p[r, h], dob[r], _DIMS["tn"],
                                                                 preferred_element_type=F32), dvv[r])
            for r in rs:
                dq_all, dg = _head_norm_bwd(dqn[r], qhat[r], qrstd[r], gq, ones)
                dgq = dgq + dg
                _put_rows(dq_ref, r, dil, dq_all)
                if has_prev:
                    dk_all, dg = _head_norm_bwd(_rows_of(ck_ref, r, dil) + dkn[r][:ATTN_BLK], kphat[r], kprstd[r], gk,
                                                ones)
                    dgk = dgk + dg
                    _put_rows(dk_ref, r, dil, dk_all)
                    _put_rows(dv_ref, r, dil, _rows_of(cvv_ref, r, dil) + dvv[r][:ATTN_BLK])
                _put_rows(ck_ref, r, dil, dkn[r][n_keys - ATTN_BLK:])
                _put_rows(cvv_ref, r, dil, dvv[r][n_keys - ATTN_BLK:])
        gq_acc[...] += dgq
        gk_acc[...] += dgk

    last = nb - 1
    cur = lambda c: pl.BlockSpec((rows, pw), lambda hp, n, c=c: (jnp.minimum(n, last), c + hp))
    prev = lambda c: pl.BlockSpec((rows, pw), lambda hp, n, c=c: (jnp.maximum(n - 1, 0), c + hp))
    gspec = pl.BlockSpec((1, HEAD), lambda hp, n: (0, 0))
    act = jax.ShapeDtypeStruct((S_, ATTN_OUT), F32)
    vec = jax.ShapeDtypeStruct((1, HEAD), F32)
    return _call(
        body, [z, z, z, z, z, q_gain, k_gain, do, o, lse],
        [cur(cq), cur(ck), prev(ck), cur(cv), prev(cv), gspec, gspec, cur(0), cur(0), cur(0)], after,
        name=name, grid=(npairs, nb + 1),
        out_specs=(cur(0), prev(0), prev(0), gspec, gspec),
        out_shape=(act, act, act, vec, vec),
        scratch_shapes=[pltpu.VMEM((rows, pw), F32), pltpu.VMEM((rows, pw), F32),
                        pltpu.VMEM((8, pw), F32), pltpu.VMEM((8, pw), F32)],
        compiler_params=_cparams(("arbitrary", "arbitrary")))


MIX_TN = 256
MIX_TM = 2048


def _sigmoid(v):
    return 1.0 / (1.0 + jnp.exp(-v))


def _mix_fwd(name, z, b_gate, ys, ws):
    S_ = z.shape[0]
    tm, tn = MIX_TM, MIX_TN
    gblk = OFF_GATE // tn

    def body(yp, yc, ya, wp, wc, wa, g0, g1, g2, b0, b1, b2, m_ref):
        acc = None
        for y_ref, w_ref, g_ref, b_ref in ((yp, wp, g0, b0), (yc, wc, g1, b1), (ya, wa, g2, b2)):
            u = lax.dot_general(y_ref[...].astype(BF16), w_ref[...], _DIMS["nt"], preferred_element_type=F32)
            t = _sigmoid(g_ref[...] + b_ref[...]) * u
            acc = t if acc is None else acc + t
        m_ref[...] = acc.astype(BF16)

    yspec = lambda w: pl.BlockSpec((tm, w), lambda i, j: (i, 0))
    wspec = lambda w: pl.BlockSpec((tn, w), lambda i, j: (j, 0))
    gspec = lambda b: pl.BlockSpec((tm, tn), lambda i, j, b=b: (i, gblk + b * (D // tn) + j))
    bspec = lambda b: pl.BlockSpec((1, tn), lambda i, j, b=b: (0, b * (D // tn) + j))
    return pl.pallas_call(
        body, name=name, grid=(S_ // tm, D // tn),
        in_specs=[yspec(POOL_W), yspec(CONV_W), yspec(ATTN_OUT), wspec(POOL_W), wspec(CONV_W), wspec(ATTN_OUT),
                  gspec(0), gspec(1), gspec(2), bspec(0), bspec(1), bspec(2)],
        out_specs=pl.BlockSpec((tm, tn), lambda i, j: (i, j)),
        out_shape=jax.ShapeDtypeStruct((S_, D), BF16),
        compiler_params=_cparams(("parallel", "parallel")),
    )(*ys, *ws, z, z, z, b_gate, b_gate, b_gate)


def _mix_bwd(name, z, b_gate, y, w, dmerged, branch, dz, after=None):
    S_ = z.shape[0]
    tm, tn = MIX_TM, MIX_TN
    width = y.shape[1]
    gblk = OFF_GATE // tn + branch * (D // tn)
    nj = D // tn

    def body(y_ref, w_ref, g_ref, b_ref, dm_ref, dy_ref, dw_ref, dg_ref, db_ref, acc_ref):
        i, j = pl.program_id(0), pl.program_id(1)
        yb = y_ref[...].astype(BF16)
        u = lax.dot_general(yb, w_ref[...], _DIMS["nt"], preferred_element_type=F32)
        sg = _sigmoid(g_ref[...] + b_ref[...])
        dm = dm_ref[...]
        du = (sg * dm).astype(BF16)
        dpre = dm * u * sg * (1.0 - sg)
        dg_ref[...] = dpre.astype(BF16)
        cols = pl.ds(pl.multiple_of(j * tn, tn), tn)
        db_ref[...] = _rows8(dpre)
        d_w = lax.dot_general(du, yb, _DIMS["tn"], preferred_element_type=F32)
        d_y = jnp.dot(du, w_ref[...], preferred_element_type=F32)

        @pl.when(i == 0)
        def _():
            dw_ref[cols, :] = d_w

        @pl.when(i > 0)
        def _():
            dw_ref[cols, :] += d_w

        @pl.when(j == 0)
        def _():
            acc_ref[...] = d_y

        @pl.when(j > 0)
        def _():
            acc_ref[...] += d_y

        @pl.when(j == nj - 1)
        def _():
            dy_ref[...] = acc_ref[...]

    rows = pl.BlockSpec((tm, width), lambda i, j: (i, 0))
    blk = pl.BlockSpec((tm, tn), lambda i, j: (i, j))
    gate = pl.BlockSpec((tm, tn), lambda i, j: (i, gblk + j))
    args = [y, w, z, b_gate, dmerged]
    in_specs = [rows, pl.BlockSpec((tn, width), lambda i, j: (j, 0)), gate,
                pl.BlockSpec((1, tn), lambda i, j: (0, branch * nj + j)), blk]
    aliases = {}
    if not isinstance(dz, jax.ShapeDtypeStruct):
        body = _drop_ref(body, len(args))
        aliases = {len(args): 2}
        args.append(dz)
        in_specs.append(ANY)
    return _call(
        body, args, in_specs, after, name=name, grid=(S_ // tm, nj),
        out_specs=(rows, pl.BlockSpec((D, width), lambda i, j: (0, 0)), gate, pl.BlockSpec((8, tn), lambda i, j: (i, j))),
        out_shape=(jax.ShapeDtypeStruct((S_, width), F32), jax.ShapeDtypeStruct((D, width), F32),
                   jax.ShapeDtypeStruct(dz.shape, dz.dtype), jax.ShapeDtypeStruct((S_ // tm * 8, D), F32)),
        scratch_shapes=[pltpu.VMEM((tm, width), F32)], input_output_aliases=aliases,
        compiler_params=_cparams(("arbitrary", "arbitrary")))


def _relu2_epilogue(acc):
    r = jnp.maximum(acc, 0.0)
    return (r * r,)


def _relu2_bwd_epilogue(acc, r):
    return (acc * (2.0 * jnp.sqrt(r.astype(F32))),)


def _residual_norm_epilogue(acc, xr, gain):
    x = xr + acc
    ms = jnp.mean(x * x, axis=-1, keepdims=True)
    return x, x * lax.rsqrt(ms + EPS) * gain


def _fold_rows(name, part, lanes=False, scale=1.0):
    R, N = part.shape

    def body(p_ref, o_ref):
        s = jnp.sum(p_ref[...], axis=0, keepdims=True)
        if lanes:
            s = jnp.sum(s, axis=1, keepdims=True)
        o_ref[...] = s * scale

    out_n = 1 if lanes else N
    return pl.pallas_call(
        body, name=name, grid=(1,), in_specs=[pl.BlockSpec((R, N), lambda i: (0, 0))],
        out_specs=pl.BlockSpec((1, out_n), lambda i: (0, 0)), out_shape=jax.ShapeDtypeStruct((1, out_n), F32))(part)


def _rows8(v):
    return jnp.sum(v.reshape(v.shape[0] // 8, 8, v.shape[1]), axis=0)


def _residual_loss_epilogue(acc, xr, target):
    e = xr + acc - target
    dy = e * (1.0 / D)
    return dy, _rows8(e * e), dy


def _norm_bwd_epilogue(acc, x, dres, gain):
    rstd = lax.rsqrt(jnp.mean(x * x, axis=-1, keepdims=True) + EPS)
    xhat = x * rstd
    dxhat = acc * gain
    c = jnp.mean(dxhat * xhat, axis=-1, keepdims=True)
    return dres + rstd * (dxhat - xhat * c), _rows8(acc * xhat)


def _norm_bwd_epilogue_twice(acc, x, dres, gain):
    dx, part = _norm_bwd_epilogue(acc, x, dres, gain)
    return dx, part, dx


def _layer_fwd(l, x, h, p, next_gain, target=None):
    t = f"l{l}_"
    z = _matmul(t + "in_proj", h, p["w_in_t"], "nt", 512, 3712, 1024)
    y_pool = _pool_fwd(t + "pool", z, p["pool_mix"], p["pool_scale"])
    os_, ls_ = [], []
    for g, (_, dil) in enumerate(ATTN_GROUPS):
        o_g, l_g = _attn_fwd(t + f"attn{g}", z, p["q_gain"], p["k_gain"], g, dil)
        os_.append(o_g)
        ls_.append(l_g)
    y_attn, lse = _attn_combine(t + "attn_mix", os_, ls_)
    y_conv = _conv_fwd(t + "conv", z, p["conv_w"], after=y_attn)
    merged = _mix_fwd(t + "merge", z, p["b_gate"], (y_pool, y_conv, y_attn),
                      (p["w_pool_up_t"], p["w_conv_out_t"], p["w_attn_up_t"]))
    x1, h2 = _matmul(t + "out_proj", merged, p["rest"], "nn", 1024, D, 1024, out_dtypes=(F32, BF16),
                     extras=((x, "mn"), (p["norm_mlp"], "n")), epilogue=_residual_norm_epilogue, b_rows=ROWS_WO)
    r = _matmul(t + "ff1", h2, p["rest"], "nt", 1024, 1024, 1024, out_dtypes=(BF16,), epilogue=_relu2_epilogue,
                b_rows=ROWS_FF1)
    if next_gain is None:
        dy, h_out, dy_16 = _matmul(t + "ff2", r, p["rest"], "nn", 512, D, D_FF,
                                   out_dtypes=(F32, (F32, "rows8"), BF16), extras=((x1, "mn"), (target, "mn")),
                                   epilogue=_residual_loss_epilogue, b_rows=ROWS_FF2)
        x2 = (dy, dy_16)
    else:
        x2, h_out = _matmul(t + "ff2", r, p["rest"], "nn", 512, D, D_FF, out_dtypes=(F32, BF16),
                            extras=((x1, "mn"), (next_gain, "n")), epilogue=_residual_norm_epilogue,
                            b_rows=ROWS_FF2)
    saved = dict(x=x, h=h, z=z, y_pool=y_pool, y_conv=y_conv, y_attn=y_attn, lse=lse, merged=merged,
                 x1=x1, h2=h2, r=r)
    return x2, h_out, saved


def _layer_bwd(l, dx2, p, s, pending, in_t_all, collective_id, start_after=None):
    t = f"l{l}_b_"
    g = {}
    rest = jax.ShapeDtypeStruct((N_DEV, REST_ROWS, D), F32)
    dx2, dx2_16 = dx2
    da = _matmul(t + "d_ff2_in", dx2_16, p["rest"], "nt", 1024, 1024, 1024, out_dtypes=(BF16,),
                 extras=((s["r"], "mn"),), epilogue=_relu2_bwd_epilogue,
                 after=start_after, b_rows=ROWS_FF2)
    rest = _matmul(t + "dw_ff2", s["r"], dx2_16, "tn", 512, 1024, 4096, into=(rest, REST_FF2, 0), after=da)
    tok = rest if pending is None else pending.chip_sums(after=rest)
    dx1, part, dx1_16 = _matmul(
        t + "d_ff1_in", da, p["rest"], "nn", 512, D, D_FF, out_dtypes=(F32, (F32, "rows8"), BF16),
        extras=((s["x1"], "mn"), (dx2, "mn"), (p["norm_mlp"], "n")), epilogue=_norm_bwd_epilogue_twice,
        after=tok, b_rows=ROWS_FF1)
    g["norm_mlp"] = _fold_rows(t + "d_norm_mlp", part)
    rest = _matmul(t + "dw_ff1", da, s["h2"], "tn", 512, 1024, 4096, into=(rest, REST_FF1, 0), after=dx1)
    dmerged = _matmul(t + "d_out_proj_in", dx1_16, p["rest"], "nt", 1024, 1024, 1024, b_rows=ROWS_WO)
    rest = _matmul(t + "dw_o", s["merged"], dx1_16, "tn", 128, 1024, 4096, into=(rest, REST_WO, 0), after=dmerged)
    ys = (s["y_pool"], s["y_conv"], s["y_attn"])
    names = ("w_pool_up_t", "w_conv_out_t", "w_attn_up_t")
    dys, dbs = [], []
    tok = rest
    dz = jax.ShapeDtypeStruct((dx2.shape[0], IN_COLS), BF16)
    for b in range(3):
        dy_b, dw_t, dz, db = _mix_bwd(t + f"merge{b}", s["z"], p["b_gate"], ys[b], p[names[b]], dmerged, b, dz,
                                      after=tok)
        tok = dy_b
        width = ys[b].shape[1]
        if b < 2:
            rest = rest.at[:, REST_UP:REST_ATTN, b * width:(b + 1) * width].set(
                dw_t.reshape(N_DEV, REST_ATTN - REST_UP, width))
        else:
            rest = rest.at[:, REST_ATTN:REST_ROWS, :].set(dw_t.reshape(N_DEV, REST_ROWS - REST_ATTN, D))
        dys.append(dy_b)
        dbs.append(_fold_rows(t + f"d_b_gate{b}", db))
    g["b_gate"] = jnp.concatenate(dbs, axis=1)
    rs_rest = _ReduceScatter(f"rs_rest{l}", collective_id, rest)
    if pending is not None:
        in_t_all = pending.finish(after=rest, into=(in_t_all, l + 1))
    dz, g["pool_mix"], g["pool_scale"] = _pool_bwd(t + "pool", s["z"], dys[0], p["pool_mix"], p["pool_scale"], dz,
                                                   after=rest if pending is None else in_t_all)
    dcb, dcc, dcx, g["conv_w"] = _conv_bwd(t + "conv", s["z"], dys[1], p["conv_w"], after=dz)
    tok = rs_rest.chip_sums(after=dcb)
    dqs, dks, dvs = [], [], []
    gq = gk = None
    for gi, (_, dil) in enumerate(ATTN_GROUPS):
        dq, dk, dv, dgq, dgk = _attn_bwd(t + f"attn{gi}", s["z"], p["q_gain"], p["k_gain"], dys[2], s["y_attn"],
                                         s["lse"], gi, dil, after=tok)
        tok = dq
        dqs.append(dq)
        dks.append(dk)
        dvs.append(dv)
        gq = dgq if gq is None else gq + dgq
        gk = dgk if gk is None else gk + dgk
    g["q_gain"], g["k_gain"] = gq, gk
    rest_sum = rs_rest.finish(after=tok)
    col = OFF_CB
    for piece in [dcb, dcc, dcx] + dqs + dks + dvs:
        dz = lax.dynamic_update_slice(dz, piece.astype(BF16), (0, col))
        col += piece.shape[1]
    in_t = _matmul(t + "dw_in", dz, s["h"], "tn", 256, 1024, 4096, after=rest_sum)
    rs_in = _ReduceScatter(f"rs_in{l}", collective_id + 2, in_t.reshape(N_DEV, IN_ROWS, D))
    dx = _matmul(t + "d_in_proj_in", dz, p["w_in_t"], "nn", 256, D, IN_COLS,
                 out_dtypes=(F32, (F32, "rows8")) + ((BF16,) if l > 0 else ()),
                 extras=((s["x"], "mn"), (dx1, "mn"), (p["norm_mix"], "n")),
                 epilogue=_norm_bwd_epilogue_twice if l > 0 else _norm_bwd_epilogue, after=in_t)
    dx, part = (dx[0], dx[2]) if l > 0 else (dx[0], None), dx[1]
    g["norm_mix"] = _fold_rows(t + "d_norm_mix", part)
    return dx, g, rest_sum, in_t_all, rs_in


ANY = pl.BlockSpec(memory_space=pl.ANY)


def _mesh_pos():
    return lax.axis_index("x"), lax.axis_index("y"), lax.axis_index("c")


def _other_chips(x, y):
    return [(1 - x, y), (x, 1 - y), (1 - x, 1 - y)]


def _comm_call(name, collective_id, peers, body, arrs, out_shape, sem_counts, after=None):
    n_in, n_out = len(arrs), len(out_shape)
    if collective_id is None:
        def tc_body(*refs):
            body(refs[:n_in], refs[n_in:n_in + n_out], *refs[n_in + n_out:])

        return pl.pallas_call(
            tc_body, name=name, out_shape=tuple(out_shape), in_specs=[ANY] * n_in, out_specs=(ANY,) * n_out,
            scratch_shapes=[pltpu.SemaphoreType.DMA((n,)) for n in sem_counts])(*arrs)

    n_after = 0 if after is None else 1

    def seq_body(*refs):
        barrier = pltpu.get_barrier_semaphore()
        ps = peers(*_mesh_pos())
        for p in ps:
            pl.semaphore_signal(barrier, inc=1, device_id=p, device_id_type=MESH)
        pl.semaphore_wait(barrier, len(ps))
        outs = refs[n_in + n_after:]
        body(refs[:n_in], outs[:n_out], *outs[n_out:])

    return pl.kernel(
        seq_body, out_type=tuple(out_shape), mesh=plsc.ScalarSubcoreMesh(axis_name="seq", num_cores=1), name=name,
        scratch_types=[pltpu.SemaphoreType.DMA((n,)) for n in sem_counts],
        compiler_params=pltpu.CompilerParams(collective_id=collective_id),
    )(*arrs, *([] if after is None else [after]))


def _all_gather(name, collective_id, shard, after=None):
    R, C = shard.shape

    def peers(x, y, c):
        return [(x, y, 1 - c)] + [(*chip, c) for chip in _other_chips(x, y)]

    def body(in_refs, out_refs, send_sems, recv_sems, local_sems):
        (x_ref,), (out_ref,) = in_refs, out_refs
        x, y, c = _mesh_pos()
        me, sibling = (x, y, c), (x, y, 1 - c)
        chips = _other_chips(x, y)

        def slot(px, py, pc):
            return out_ref.at[4 * px + 2 * py + pc]

        def copy(k, block, to, src=None):
            return pltpu.make_async_remote_copy(
                src_ref=slot(*block) if src is None else src, dst_ref=slot(*block),
                send_sem=send_sems.at[k], recv_sem=recv_sems.at[k], device_id=to, device_id_type=MESH)

        mine = pltpu.make_async_copy(x_ref, slot(*me), local_sems.at[0])
        mine.start()
        first = [copy(0, me, sibling, src=x_ref)]
        first += [copy(1 + j, me, (*chip, c), src=x_ref) for j, chip in enumerate(chips)]
        for cp in first:
            cp.start()
        passed = [copy(4 + j, (*chip, c), sibling) for j, chip in enumerate(chips)]
        for j, chip in enumerate(chips):
            copy(1 + j, (*chip, c), me).wait_recv()
            passed[j].start()
        copy(0, sibling, me).wait_recv()
        for j, chip in enumerate(chips):
            copy(4 + j, (*chip, 1 - c), me).wait_recv()
        for cp in first + passed:
            cp.wait_send()
        mine.wait()

    return _comm_call(name, collective_id, peers, body, [shard],
                      [jax.ShapeDtypeStruct((N_DEV, R, C), shard.dtype)], (7, 7, 1), after)[0]


def _rs_sibling_exchange(name, collective_id, arrs):
    n = len(arrs)

    def body(in_refs, out_refs, send_sems, recv_sems):
        x, y, c = _mesh_pos()
        cps = []
        for k, (src, dst) in enumerate(zip(in_refs, out_refs)):
            src = src.at[:, 1 - c] if len(src.shape) == 4 else src
            cps.append(pltpu.make_async_remote_copy(src_ref=src, dst_ref=dst, send_sem=send_sems.at[k],
                                                    recv_sem=recv_sems.at[k], device_id=(x, y, 1 - c),
                                                    device_id_type=MESH))
        for cp in cps:
            cp.start()
        for cp in cps:
            cp.wait()

    out_shape = [jax.ShapeDtypeStruct(a.shape[:1] + a.shape[2:] if a.ndim == 4 else a.shape, a.dtype) for a in arrs]
    return _comm_call(name, collective_id, lambda x, y, c: [(x, y, 1 - c)], body, arrs, out_shape, (n, n))


def _rs_rows(r):
    return r // 2 if (r // 2) % 16 == 0 else r


def _drop_ref(body, idx):
    def wrapped(*refs):
        return body(*refs[:idx], *refs[idx + 1:])

    return wrapped


def _rs_chip_sum(name, ids, big, rbig, after=None):
    _, _, R, C = big.shape
    rows = _rs_rows(R)

    def body(ids_ref, mine_ref, theirs_ref, t16_ref, own_ref):
        p = pl.program_id(1)
        s = mine_ref[...] + theirs_ref[...]
        t16_ref[...] = s.astype(BF16)

        @pl.when(p == ids_ref[1])
        def _():
            own_ref[...] = s

    in_specs = [pl.BlockSpec((None, None, rows, C), lambda i, p, ids: (p, ids[0], i, 0)),
                pl.BlockSpec((None, rows, C), lambda i, p, ids: (p, i, 0))]
    args = [ids, big, rbig]
    if after is not None:
        body = _drop_ref(body, len(args))
        in_specs.append(ANY)
        args.append(after)
    grid_spec = pltpu.PrefetchScalarGridSpec(
        num_scalar_prefetch=1, grid=(R // rows, 4), in_specs=in_specs,
        out_specs=(pl.BlockSpec((None, rows, C), lambda i, p, ids: (p, i, 0)),
                   pl.BlockSpec((rows, C), lambda i, p, ids: (i, 0))))
    return pl.pallas_call(
        body, name=name, grid_spec=grid_spec,
        out_shape=(jax.ShapeDtypeStruct((4, R, C), BF16), jax.ShapeDtypeStruct((R, C), F32)),
        compiler_params=_cparams(("parallel", "arbitrary")),
    )(*args)


def _add2(name, a, b):
    def body(a_ref, b_ref, o_ref):
        o_ref[...] = a_ref[...] + b_ref[...]

    full = pl.BlockSpec(a.shape, lambda i: (0, 0))
    return pl.pallas_call(body, name=name, grid=(1,), in_specs=[full, full], out_specs=full,
                          out_shape=jax.ShapeDtypeStruct(a.shape, a.dtype))(a, b)


def _rs_chip_exchange(name, collective_id, arrs, after=None):
    n = len(arrs)

    def body(in_refs, out_refs, send_sems, recv_sems, local_sems):
        x, y, c = _mesh_pos()
        p_me = 2 * x + y
        chips = _other_chips(x, y)

        def part(ref, p):
            return ref.at[p] if len(ref.shape) == 3 else ref

        local = [pltpu.make_async_copy(part(src, p_me), dst.at[p_me], local_sems.at[k])
                 for k, (src, dst) in enumerate(zip(in_refs, out_refs))]
        for cp in local:
            cp.start()
        sends = []
        for j, (px, py) in enumerate(chips):
            for k, (src, dst) in enumerate(zip(in_refs, out_refs)):
                sends.append(pltpu.make_async_remote_copy(
                    src_ref=part(src, 2 * px + py), dst_ref=dst.at[p_me], send_sem=send_sems.at[n * j + k],
                    recv_sem=recv_sems.at[n * j + k], device_id=(px, py, c), device_id_type=MESH))
        for cp in sends:
            cp.start()
        for j, (px, py) in enumerate(chips):
            for k, (src, dst) in enumerate(zip(in_refs, out_refs)):
                pltpu.make_async_remote_copy(
                    src_ref=part(src, p_me), dst_ref=dst.at[2 * px + py], send_sem=send_sems.at[n * j + k],
                    recv_sem=recv_sems.at[n * j + k], device_id=(px, py, c), device_id_type=MESH).wait_recv()
        for cp in sends:
            cp.wait_send()
        for cp in local:
            cp.wait()

    out_shape = [jax.ShapeDtypeStruct((4,) + a.shape[-2:], a.dtype) for a in arrs]
    return _comm_call(name, collective_id, lambda x, y, c: [(*chip, c) for chip in _other_chips(x, y)], body, arrs,
                      out_shape, (3 * n, 3 * n, n), after)


def _rs_final_sum(name, ids, recv, own, rows, after=None, into=None):
    _, R, C = recv.shape
    assert R % rows == 0

    def body(ids_ref, r_ref, own_ref, o_ref):
        acc = None
        for p in range(4):
            term = jnp.where(ids_ref[1] == p, own_ref[...], r_ref[p].astype(F32))
            acc = term if acc is None else acc + term
        o_ref[...] = acc

    in_specs = [pl.BlockSpec((4, rows, C), lambda i, ids: (0, i, 0)), pl.BlockSpec((rows, C), lambda i, ids: (i, 0))]
    args = [ids, recv, own]
    if after is not None:
        body = _drop_ref(body, len(args))
        in_specs.append(ANY)
        args.append(after)
    out_spec, out_shape, aliases = pl.BlockSpec((rows, C), lambda i, ids: (i, 0)), jax.ShapeDtypeStruct((R, C), F32), {}
    if into is not None:
        buf, slot = into
        out_spec = pl.BlockSpec((None, rows, C), lambda i, ids: (slot, i, 0))
        out_shape = jax.ShapeDtypeStruct(buf.shape, F32)
        if not isinstance(buf, jax.ShapeDtypeStruct):
            body = _drop_ref(body, len(args))
            aliases = {len(args): 0}
            in_specs.append(ANY)
            args.append(buf)
    grid_spec = pltpu.PrefetchScalarGridSpec(
        num_scalar_prefetch=1, grid=(R // rows,), in_specs=in_specs, out_specs=out_spec)
    return pl.pallas_call(
        body, name=name, grid_spec=grid_spec, out_shape=out_shape, input_output_aliases=aliases,
        compiler_params=_cparams(("parallel",)),
    )(*args)


class _ReduceScatter:
    def __init__(self, tag, collective_id, big):
        x, y, c = _mesh_pos()
        self.tag, self.cid = tag, collective_id
        self.ids = jnp.stack([c, 2 * x + y]).astype(jnp.int32)
        self.big = big.reshape((4, 2) + big.shape[1:])
        (self.got,) = _rs_sibling_exchange(tag + "_sibling", collective_id, [self.big])

    def chip_sums(self, after=None):
        t16, self.own = _rs_chip_sum(self.tag + "_chip_sum", self.ids, self.big, self.got, after)
        (self.recv,) = _rs_chip_exchange(self.tag + "_chips", self.cid + 1, [t16])
        return t16

    def finish(self, after=None, into=None):
        return _rs_final_sum(self.tag + "_final", self.ids, self.recv, self.own, _rs_rows(self.own.shape[0]), after,
                             into)


def _all_reduce_small(tag, small, collective_id, launch_after):
    x, y, c = _mesh_pos()
    ids = jnp.stack([c, 2 * x + y]).astype(jnp.int32)
    (theirs,) = _rs_sibling_exchange(tag + "_sibling", None, [small])
    ts = _add2(tag + "_chip_sum", small, theirs)
    (recv,) = _rs_chip_exchange(tag + "_chips", collective_id, [ts], launch_after)
    return lambda after: _rs_final_sum(tag + "_final", ids, recv, ts, small.shape[0], after)


def _adamw(name, w, g, m, v):
    R, C = w.shape
    tiles = [t for t in range(8, 513, 8) if R % t == 0]
    tr = max(tiles) if tiles else R
    c1 = 1.0 - ADAM_B1 ** ADAM_STEP
    c2 = 1.0 - ADAM_B2 ** ADAM_STEP

    def body(w_ref, g_ref, m_ref, v_ref, d_ref, nm_ref, nv_ref):
        gv = g_ref[...]
        nm = ADAM_B1 * m_ref[...] + (1.0 - ADAM_B1) * gv
        nv = ADAM_B2 * v_ref[...] + (1.0 - ADAM_B2) * (gv * gv)
        d_ref[...] = -ADAM_LR * ((nm / c1) / (jnp.sqrt(nv / c2) + ADAM_EPS) + ADAM_WD * w_ref[...])
        nm_ref[...] = nm
        nv_ref[...] = nv

    blk = pl.BlockSpec((tr, C), lambda i: (i, 0))
    shp = jax.ShapeDtypeStruct((R, C), F32)
    return pl.pallas_call(
        body, name=name, grid=(R // tr,), in_specs=[blk] * 4, out_specs=(blk,) * 3, out_shape=(shp,) * 3,
        compiler_params=_cparams(("parallel",)),
    )(w, g, m, v)


def _adamw_t(name, w, g_t, m, v):
    L, R, C = w.shape
    tr = 256
    c1 = 1.0 - ADAM_B1 ** ADAM_STEP
    c2 = 1.0 - ADAM_B2 ** ADAM_STEP

    def body(w_ref, g_ref, m_ref, v_ref, go_ref, d_ref, nm_ref, nv_ref):
        gv = g_ref[...].T
        go_ref[...] = gv
        nm = ADAM_B1 * m_ref[...] + (1.0 - ADAM_B1) * gv
        nv = ADAM_B2 * v_ref[...] + (1.0 - ADAM_B2) * (gv * gv)
        d_ref[...] = -ADAM_LR * ((nm / c1) / (jnp.sqrt(nv / c2) + ADAM_EPS) + ADAM_WD * w_ref[...])
        nm_ref[...] = nm
        nv_ref[...] = nv

    blk = pl.BlockSpec((None, tr, C), lambda l, i: (l, i, 0))
    blk_t = pl.BlockSpec((None, C, tr), lambda l, i: (l, 0, i))
    shp = jax.ShapeDtypeStruct((L, R, C), F32)
    return pl.pallas_call(
        body, name=name, grid=(L, R // tr), in_specs=[blk, blk_t, blk, blk], out_specs=(blk,) * 4,
        out_shape=(shp,) * 4, compiler_params=_cparams(("parallel", "parallel")),
    )(w, g_t, m, v)


REST_FF1 = 0
REST_FF2 = 512
REST_WO = 1024
REST_UP = 1152
REST_ATTN = 1280
REST_ROWS = 1312
IN_ROWS = IN_COLS // N_DEV
ROWS_FF1 = (REST_FF1, D_FF // N_DEV)
ROWS_FF2 = (REST_FF2, D_FF // N_DEV)
ROWS_WO = (REST_WO, D // N_DEV)
SMALL = (("norm_mix", (DEPTH, D)), ("b_gate", (DEPTH, 3 * D)), ("pool_mix", (DEPTH, 4, POOL_G, POOL_G)),
         ("pool_scale", (DEPTH, POOL_W)), ("conv_w", (DEPTH, 3, CONV_W)), ("q_gain", (DEPTH, HEAD)),
         ("k_gain", (DEPTH, HEAD)), ("norm_mlp", (DEPTH, D)))


def _pack_weight_shards(w, l):
    b = lambda a: a.astype(BF16)
    rest = jnp.concatenate([
        b(w["w_ff1"][l].T), b(w["w_ff2"][l]), b(w["w_o"][l]),
        jnp.concatenate([b(w["w_pool_up"][l].T), b(w["w_conv_out"][l].T)], axis=1),
        b(w["w_attn_up"][l].T).reshape(REST_ROWS - REST_ATTN, D)], axis=0)
    return b(w["w_in"][l].T), rest


def _unpack_gathered(g_in, g_rest, g_conv, small_w, l):
    rest = g_rest
    take = lambda r0, rows, c0=0, cols=D: rest[:, r0:r0 + rows, c0:c0 + cols].reshape(N_DEV * rows, cols)
    conv = g_conv[:, 3 * l:3 * l + 3, :CONV_W // N_DEV]
    p = {
        "w_in_t": g_in.reshape(IN_COLS, D),
        "rest": rest,
        "w_pool_up_t": take(REST_UP, 128, 0, POOL_W), "w_conv_out_t": take(REST_UP, 128, POOL_W, CONV_W),
        "w_attn_up_t": rest[:, REST_ATTN:].reshape(D, ATTN_OUT),
        "conv_w": jnp.transpose(conv, (1, 0, 2)).reshape(3, CONV_W),
        "pool_mix": small_w["pool_mix"][l],
    }
    for name in ("norm_mix", "b_gate", "pool_scale", "q_gain", "k_gain", "norm_mlp"):
        p[name] = small_w[name][l][None, :]
    return p


def _pack_small_grads(grads, loss):
    flat = jnp.concatenate([jnp.stack([grads[l][name] for l in range(DEPTH)]).reshape(-1) for name, _ in SMALL]
                           + [loss.reshape(1)])
    return jnp.pad(flat, (0, (-flat.shape[0]) % (8 * 128))).reshape(-1, 128)


def _unpack_grads(rest):
    return {
        "w_ff1_t": jnp.stack([a[REST_FF1:REST_FF1 + 512] for a in rest]),
        "w_ff2": jnp.stack([a[REST_FF2:REST_FF2 + 512] for a in rest]),
        "w_o": jnp.stack([a[REST_WO:REST_WO + 128] for a in rest]),
        "w_pool_up": jnp.stack([a[REST_UP:REST_UP + 128, :POOL_W].T for a in rest]),
        "w_conv_out": jnp.stack([a[REST_UP:REST_UP + 128, POOL_W:].T for a in rest]),
        "w_attn_up": jnp.stack([a[REST_ATTN:].reshape(D // N_DEV, ATTN_OUT).T for a in rest]),
    }


def _unpack_small_like_grads(small, dev):
    out = {}
    flat = small.reshape(-1)
    off = 0
    for name, shp in SMALL:
        n = 1
        for s in shp:
            n *= s
        out[name] = flat[off:off + n].reshape(shp)
        off += n
    out["loss"] = flat[off]
    width = CONV_W // N_DEV
    out["conv_w"] = lax.dynamic_slice_in_dim(out["conv_w"], dev * width, width, axis=2)
    return out


WEIGHTS = ("norm_mix", "w_in", "b_gate", "pool_mix", "pool_scale", "conv_w", "q_gain", "k_gain", "w_pool_up",
           "w_conv_out", "w_attn_up", "w_o", "norm_mlp", "w_ff1", "w_ff2")


def kernel(x, norm_mix, w_in, b_gate, pool_mix, pool_scale, conv_w, q_gain, k_gain, w_pool_up, w_conv_out, w_attn_up, w_o, norm_mlp, w_ff1, w_ff2, loss_target, m_norm_mix, m_w_in, m_b_gate, m_pool_mix, m_pool_scale, m_conv_w, m_q_gain, m_k_gain, m_w_pool_up, m_w_conv_out, m_w_attn_up, m_w_o, m_norm_mlp, m_w_ff1, m_w_ff2, v_norm_mix, v_w_in, v_b_gate, v_pool_mix, v_pool_scale, v_conv_w, v_q_gain, v_k_gain, v_w_pool_up, v_w_conv_out, v_w_attn_up, v_w_o, v_norm_mlp, v_w_ff1, v_w_ff2):
    w = dict(zip(WEIGHTS, (norm_mix, w_in, b_gate, pool_mix, pool_scale, conv_w, q_gain, k_gain, w_pool_up,
                           w_conv_out, w_attn_up, w_o, norm_mlp, w_ff1, w_ff2)))
    m = dict(zip(WEIGHTS, (m_norm_mix, m_w_in, m_b_gate, m_pool_mix, m_pool_scale, m_conv_w, m_q_gain, m_k_gain,
                           m_w_pool_up, m_w_conv_out, m_w_attn_up, m_w_o, m_norm_mlp, m_w_ff1, m_w_ff2)))
    v = dict(zip(WEIGHTS, (v_norm_mix, v_w_in, v_b_gate, v_pool_mix, v_pool_scale, v_conv_w, v_q_gain, v_k_gain,
                           v_w_pool_up, v_w_conv_out, v_w_attn_up, v_w_o, v_norm_mlp, v_w_ff1, v_w_ff2)))
    xi, yi, ci = _mesh_pos()
    dev = 4 * xi + 2 * yi + ci

    saved, params = [], []
    act = x[0]
    w_in_t, m_in_t, v_in_t = (jnp.swapaxes(src["w_in"], 1, 2) for src in (w, m, v))
    h = _rmsnorm_fwd("l0_norm_mix", act, w["norm_mix"][0][None])
    conv_shard = jnp.pad(w["conv_w"].reshape(DEPTH * 3, CONV_W // N_DEV), ((0, 8 - DEPTH * 3), (0, 128 - CONV_W // N_DEV)))
    g_conv = _all_gather("gather_conv", None, conv_shard)
    for l in range(DEPTH):
        s_in, s_rest = _pack_weight_shards(w, l)
        g_in = _all_gather(f"gather_in{l}", 1 + 2 * l, s_in)
        g_rest = _all_gather(f"gather_rest{l}", 2 + 2 * l, s_rest)
        params.append(_unpack_gathered(g_in, g_rest, g_conv, w, l))
    for l in range(DEPTH):
        act, h, s = _layer_fwd(l, act, h, params[l], w["norm_mix"][l + 1][None] if l + 1 < DEPTH else None,
                               loss_target[0])
        saved.append(s)
    dy = act
    loss = _fold_rows("loss_sum", h, lanes=True, scale=0.5 / D)
    rest, grads, pending = [None] * DEPTH, [None] * DEPTH, None
    in_t = jax.ShapeDtypeStruct((DEPTH, IN_ROWS, D), F32)
    for l in reversed(range(DEPTH)):
        dy, grads[l], rest[l], in_t, pending = _layer_bwd(l, dy, params[l], saved[l], pending, in_t, 5 + 4 * l)
    dx = dy[0]
    launched = pending.chip_sums(after=dx)
    small_finish = _all_reduce_small("ar_small", _pack_small_grads(grads, loss), 5 + 4 * DEPTH, launched)
    g = _unpack_grads(rest)

    delta, new_m, new_v = {}, {}, {}
    g["w_ff1"], delta["w_ff1"], new_m["w_ff1"], new_v["w_ff1"] = _adamw_t("adamw_w_ff1", w["w_ff1"], g["w_ff1_t"],
                                                                          m["w_ff1"], v["w_ff1"])
    for name in ("w_pool_up", "w_conv_out", "w_attn_up", "w_o", "w_ff2"):
        shp = w[name].shape
        two_d = (shp[0] * shp[1], shp[2])
        d_, m_, v_ = _adamw("adamw_" + name, w[name].reshape(two_d), g[name].reshape(two_d),
                            m[name].reshape(two_d), v[name].reshape(two_d))
        delta[name], new_m[name], new_v[name] = d_.reshape(shp), m_.reshape(shp), v_.reshape(shp)
    g_t = pending.finish(after=new_v["w_ff2"], into=(in_t, 0))
    two_d = (g_t.shape[0] * g_t.shape[1], g_t.shape[2])
    d_, m_, v_ = _adamw("adamw_w_in", *[a.reshape(two_d) for a in (w_in_t, g_t, m_in_t, v_in_t)])
    g["w_in"] = jnp.swapaxes(g_t, 1, 2)
    delta["w_in"], new_m["w_in"], new_v["w_in"] = (jnp.swapaxes(a.reshape(g_t.shape), 1, 2) for a in (d_, m_, v_))
    g.update(_unpack_small_like_grads(small_finish(after=v_), dev))
    total = g.pop("loss")
    for name, shp in SMALL:
        shp = w[name].shape
        two_d = (-1, shp[-1])
        d_, m_, v_ = _adamw("adamw_" + name, w[name].reshape(two_d), g[name].reshape(two_d),
                            m[name].reshape(two_d), v[name].reshape(two_d))
        delta[name], new_m[name], new_v[name] = d_.reshape(shp), m_.reshape(shp), v_.reshape(shp)

    return (total, dx[None], *[g[n] for n in WEIGHTS], *[delta[n] for n in WEIGHTS],
            *[new_m[n] for n in WEIGHTS], *[new_v[n] for n in WEIGHTS])
```

```python
import functools

import jax
import jax.numpy as jnp
from jax import lax
from jax.experimental import pallas as pl
from jax.experimental.pallas import tpu as pltpu
from jax.experimental.pallas import tpu_sc as plsc

F32 = jnp.float32
BF16 = jnp.bfloat16
MESH = pl.DeviceIdType.MESH

D = 1024
SEQ = 4096
DEPTH = 2
N_DEV = 8
POOL_WINDOWS = (2, 4, 8, 16)
POOL_W = 512
POOL_G = 128
CONV_W = 512
HEAD = 64
ATTN_GROUPS = ((128, 1), (512, 4), (2048, 16))
HPG = 4
ATTN_W = 768
ATTN_OUT = 256
ATTN_BLK = 128
D_FF = 4096
EPS = 1e-6
MASK = -1e30
OFF_POOL = 0
OFF_CB = 512
OFF_CC = 1024
OFF_CX = 1536
OFF_Q = 2048
OFF_K = 2816
OFF_V = 3584
OFF_GATE = 4352
IN_COLS = 7424
ADAM_LR = 0.001
ADAM_B1 = 0.9
ADAM_B2 = 0.999
ADAM_EPS = 1e-08
ADAM_WD = 0.01
ADAM_STEP = 10

ROW_TILE = 512
SEQ_CHUNK = 256
HALO = 16
VMEM_LIMIT = 56 * 1024 * 1024


def _cparams(sem=None):
    return pltpu.CompilerParams(dimension_semantics=sem, vmem_limit_bytes=VMEM_LIMIT)


def _call(body, args, in_specs, after=None, **kw):
    if after is not None:
        after = list(after) if isinstance(after, (list, tuple)) else [after]
        n, k, inner = len(args), len(after), body

        def body(*refs):
            return inner(*refs[:n], *refs[n + k:])

        args = list(args) + after
        in_specs = list(in_specs) + [pl.BlockSpec(memory_space=pl.ANY)] * k
    return pl.pallas_call(body, in_specs=in_specs, **kw)(*args)


_DIMS = {"nn": (((1,), (0,)), ((), ())), "nt": (((1,), (1,)), ((), ())), "tn": (((0,), (0,)), ((), ()))}


def _matmul(name, a, b, mode, tm, tn, tk, out_dtypes=(F32,), extras=(), epilogue=None, into=None, after=None,
            b_rows=None):
    if mode == "tn":
        K, M = a.shape
    else:
        M, K = a.shape
    b_shape = b.shape if b_rows is None else (b.shape[0] * b_rows[1], b.shape[2])
    N = b_shape[0] if mode == "nt" else b_shape[1]
    assert b_shape[1 if mode == "nt" else 0] == K, (name, b_shape, K)
    assert M % tm == 0 and N % tn == 0 and K % tk == 0, (name, M, N, K, tm, tn, tk)
    nk = K // tk
    n_extra = len(extras)
    n_out = len(out_dtypes)
    dims = _DIMS[mode]
    n_alias = 0 if into is None or isinstance(into[0], jax.ShapeDtypeStruct) else 1

    def body(a_ref, b_ref, *rest):
        extra_refs = rest[:n_extra]
        out_refs = rest[n_extra + n_alias:n_extra + n_alias + n_out]

        def finish(acc):
            if epilogue is None:
                res = (acc,)
            else:
                res = epilogue(acc, *[r[...] for r in extra_refs])
            for o_ref, v in zip(out_refs, res):
                o_ref[...] = v.astype(o_ref.dtype)

        bv = b_ref[...]
        if b_rows is not None:
            bv = bv.reshape(-1, bv.shape[-1])
        part = lax.dot_general(a_ref[...].astype(BF16), bv.astype(BF16), dims, preferred_element_type=F32)
        if nk == 1:
            finish(part)
        else:
            acc_ref = rest[-1]
            k = pl.program_id(2)

            @pl.when(k == 0)
            def _():
                acc_ref[...] = part

            @pl.when(k > 0)
            def _():
                acc_ref[...] += part

            @pl.when(k == nk - 1)
            def _():
                finish(acc_ref[...])

    if mode == "tn":
        a_spec = pl.BlockSpec((tk, tm), lambda i, j, k: (k, i))
    else:
        a_spec = pl.BlockSpec((tm, tk), lambda i, j, k: (i, k))
    if b_rows is not None:
        row0, rows = b_rows
        t_rows = tn if mode == "nt" else tk
        assert t_rows % rows == 0 and row0 % rows == 0, (name, t_rows, b_rows)
        if mode == "nt":
            b_spec = pl.BlockSpec((t_rows // rows, rows, tk), lambda i, j, k: (j, row0 // rows, k))
        else:
            b_spec = pl.BlockSpec((t_rows // rows, rows, tn), lambda i, j, k: (k, row0 // rows, j))
    elif mode == "nt":
        b_spec = pl.BlockSpec((tn, tk), lambda i, j, k: (j, k))
    else:
        b_spec = pl.BlockSpec((tk, tn), lambda i, j, k: (k, j))
    in_specs = [a_spec, b_spec]
    args = [a, b]
    for arr, kind in extras:
        if kind == "mn":
            in_specs.append(pl.BlockSpec((tm, tn), lambda i, j, k: (i, j)))
        else:
            in_specs.append(pl.BlockSpec((1, tn), lambda i, j, k: (0, j)))
        args.append(arr)
    part = [isinstance(dt, tuple) for dt in out_dtypes]
    out_shape = tuple(jax.ShapeDtypeStruct((M // tm * 8, N), dt[0]) if p else jax.ShapeDtypeStruct((M, N), dt)
                      for dt, p in zip(out_dtypes, part))
    out_specs = tuple(pl.BlockSpec((8 if p else tm, tn), lambda i, j, k: (i, j)) for p in part)
    aliases = {}
    if into is not None:
        buf, row0, col0 = into
        assert n_out == 1 and (M // N_DEV) % tm == 0 and row0 % tm == 0 and col0 % tn == 0
        per_dev = M // N_DEV // tm
        out_shape = (jax.ShapeDtypeStruct(buf.shape, buf.dtype),)
        out_specs = (pl.BlockSpec((None, tm, tn), lambda i, j, k: (i // per_dev, row0 // tm + i % per_dev,
                                                                   col0 // tn + j)),)
        if not isinstance(buf, jax.ShapeDtypeStruct):
            aliases = {len(args): 0}
            in_specs.append(pl.BlockSpec(memory_space=pl.ANY))
            args.append(buf)
    scratch = [] if nk == 1 else [pltpu.VMEM((tm, tn), F32)]
    res = _call(
        body, args, in_specs, after, name=name, grid=(M // tm, N // tn, nk), out_specs=out_specs,
        out_shape=out_shape, scratch_shapes=scratch, input_output_aliases=aliases,
        compiler_params=_cparams(("parallel", "parallel", "arbitrary")))
    return res if n_out > 1 else res[0]


def _rmsnorm_fwd(name, x, gain, after=None):
    S_, D_ = x.shape

    def body(x_ref, g_ref, h_ref):
        xf = x_ref[...]
        ms = jnp.mean(xf * xf, axis=-1, keepdims=True)
        h_ref[...] = (xf * lax.rsqrt(ms + EPS) * g_ref[...]).astype(BF16)

    return _call(
        body, [x, gain], [pl.BlockSpec((ROW_TILE, D_), lambda i: (i, 0)), pl.BlockSpec((1, D_), lambda i: (0, 0))],
        after, name=name, grid=(S_ // ROW_TILE,),
        out_specs=pl.BlockSpec((ROW_TILE, D_), lambda i: (i, 0)),
        out_shape=jax.ShapeDtypeStruct((S_, D_), BF16),
        compiler_params=_cparams(("parallel",)))


def _rows_with_halo(ref, cols, i, n_chunks, before, after):
    r0 = pl.multiple_of(i * SEQ_CHUNK, SEQ_CHUNK)
    parts = []
    if before:
        h0 = pl.multiple_of(jnp.maximum(r0 - HALO, 0), 8)
        halo = ref[pl.ds(h0, HALO), cols]
        parts.append(jnp.where(i > 0, halo, jnp.zeros_like(halo)))
    parts.append(ref[pl.ds(r0, SEQ_CHUNK), cols])
    if after:
        a0 = pl.multiple_of(jnp.minimum(r0 + SEQ_CHUNK, (n_chunks - 1) * SEQ_CHUNK + SEQ_CHUNK - HALO), 8)
        halo = ref[pl.ds(a0, HALO), cols]
        parts.append(jnp.where(i < n_chunks - 1, halo, jnp.zeros_like(halo)))
    return parts[0] if len(parts) == 1 else jnp.concatenate(parts, axis=0)


def _shift_down(v, k):
    return pltpu.roll(v, k, 0)


def _shift_up(v, k):
    return pltpu.roll(v, v.shape[0] - k, 0)


def _pool_diff(xx, w, t_main):
    s = xx
    k = 1
    while k < w:
        s = s + _shift_down(s, k)
        k *= 2
    cnt = jnp.minimum(t_main + 1, w).astype(F32)
    return s[HALO:] / cnt - xx[HALO:]


def _pool_fwd(name, z, pool_mix, pool_scale):
    S_ = z.shape[0]
    n_chunks = S_ // SEQ_CHUNK

    cols = slice(0, POOL_G)

    def body(u_ref, mix_ref, sc_ref, y_ref):
        mixg = mix_ref[...].astype(BF16)
        scg = sc_ref[...]
        for g, w in enumerate(POOL_WINDOWS):
            @pl.when(pl.program_id(0) == g)
            def _(w=w):
                def chunk(i, carry):
                    r0 = pl.multiple_of(i * SEQ_CHUNK, SEQ_CHUNK)
                    xx = _rows_with_halo(u_ref, cols, i, n_chunks, True, False)
                    t = r0 + lax.broadcasted_iota(jnp.int32, (SEQ_CHUNK, POOL_G), 0)
                    d = _pool_diff(xx, w, t)
                    y = jnp.dot(d.astype(BF16), mixg, preferred_element_type=F32) * scg
                    y_ref[pl.ds(r0, SEQ_CHUNK), :] = y.astype(BF16)
                    return carry

                lax.fori_loop(0, n_chunks, chunk, 0)

    slab = pl.BlockSpec((S_, POOL_G), lambda g: (0, g))
    return pl.pallas_call(
        body, name=name, grid=(len(POOL_WINDOWS),),
        in_specs=[slab, pl.BlockSpec((None, POOL_G, POOL_G), lambda g: (g, 0, 0)),
                  pl.BlockSpec((1, POOL_G), lambda g: (0, g))],
        out_specs=slab, out_shape=jax.ShapeDtypeStruct((S_, POOL_W), BF16),
        compiler_params=_cparams(("parallel",)),
    )(z, pool_mix, pool_scale)


def _pool_bwd(name, z, dy, pool_mix, pool_scale, dz, after=None):
    S_ = z.shape[0]
    n_chunks = S_ // SEQ_CHUNK
    rows_a = SEQ_CHUNK + HALO

    cols = slice(0, POOL_G)

    def body(u_ref, dy_ref, mix_ref, sc_ref, du_ref, dmix_ref, dsc_ref):
        mixg = mix_ref[...].astype(BF16)
        scg = sc_ref[...]
        for g, w in enumerate(POOL_WINDOWS):
            @pl.when(pl.program_id(0) == g)
            def _(w=w):
                def chunk(i, carry):
                    dmix_acc, dsc_acc = carry
                    r0 = pl.multiple_of(i * SEQ_CHUNK, SEQ_CHUNK)
                    xx = _rows_with_halo(u_ref, cols, i, n_chunks, True, False)
                    t = r0 + lax.broadcasted_iota(jnp.int32, (SEQ_CHUNK, POOL_G), 0)
                    d = _pool_diff(xx, w, t).astype(BF16)
                    ypre = jnp.dot(d, mixg, preferred_element_type=F32)
                    dyy = _rows_with_halo(dy_ref, cols, i, n_chunks, False, True)
                    dys = (dyy * scg).astype(BF16)
                    dsc_acc = dsc_acc + jnp.sum((dyy[:SEQ_CHUNK] * ypre).reshape(SEQ_CHUNK // 8, 8, POOL_G), axis=0)
                    dmix_acc = dmix_acc + lax.dot_general(d, dys[:SEQ_CHUNK], _DIMS["tn"],
                                                          preferred_element_type=F32)
                    dd = lax.dot_general(dys, mixg, _DIMS["nt"], preferred_element_type=F32)
                    ta = r0 + lax.broadcasted_iota(jnp.int32, (rows_a, POOL_G), 0)
                    f = dd / jnp.minimum(ta + 1, w).astype(F32)
                    k = 1
                    while k < w:
                        f = f + _shift_up(f, k)
                        k *= 2
                    du_ref[pl.ds(r0, SEQ_CHUNK), :] = (f[:SEQ_CHUNK] - dd[:SEQ_CHUNK]).astype(BF16)
                    return dmix_acc, dsc_acc

                dmix_acc, dsc_acc = lax.fori_loop(
                    0, n_chunks, chunk, (jnp.zeros((POOL_G, POOL_G), F32), jnp.zeros((8, POOL_G), F32)))
                dmix_ref[...] = dmix_acc
                dsc_ref[...] = jnp.sum(dsc_acc, axis=0, keepdims=True)

    slab = pl.BlockSpec((S_, POOL_G), lambda g: (0, g))
    mix_spec = pl.BlockSpec((None, POOL_G, POOL_G), lambda g: (g, 0, 0))
    vec = pl.BlockSpec((1, POOL_G), lambda g: (0, g))
    return _call(
        _drop_ref(body, 4), [z, dy, pool_mix, pool_scale, dz], [slab, slab, mix_spec, vec, ANY], after, name=name,
        grid=(len(POOL_WINDOWS),), out_specs=(slab, mix_spec, vec),
        out_shape=(jax.ShapeDtypeStruct(dz.shape, dz.dtype), jax.ShapeDtypeStruct((4, POOL_G, POOL_G), F32),
                   jax.ShapeDtypeStruct((1, POOL_W), F32)),
        input_output_aliases={4: 0}, compiler_params=_cparams(("parallel",)))


def _conv_specs(S_):
    slab = lambda off: pl.BlockSpec((S_, 128), lambda c, off=off: (0, off // 128 + c))
    return slab(OFF_CB), slab(OFF_CC), slab(OFF_CX)


def _conv_fwd(name, z, conv_w, after=None):
    S_ = z.shape[0]
    n_chunks = S_ // SEQ_CHUNK
    col = slice(0, 128)

    def body(b_ref, c_ref, x_ref, w_ref, y_ref):
        w0, w1, w2 = w_ref[0:1, :], w_ref[1:2, :], w_ref[2:3, :]

        def chunk(i, carry):
            r0 = pl.multiple_of(i * SEQ_CHUNK, SEQ_CHUNK)
            u = _rows_with_halo(c_ref, col, i, n_chunks, True, False) * _rows_with_halo(x_ref, col, i, n_chunks, True, False)
            y = w2 * u + w1 * _shift_down(u, 1) + w0 * _shift_down(u, 2)
            y_ref[pl.ds(r0, SEQ_CHUNK), :] = (b_ref[pl.ds(r0, SEQ_CHUNK), :] * y[HALO:]).astype(BF16)
            return carry

        lax.fori_loop(0, n_chunks, chunk, 0)

    sb, sc, sx = _conv_specs(S_)
    return _call(
        body, [z, z, z, conv_w], [sb, sc, sx, pl.BlockSpec((3, 128), lambda c: (0, c))], after,
        name=name, grid=(CONV_W // 128,),
        out_specs=pl.BlockSpec((S_, 128), lambda c: (0, c)),
        out_shape=jax.ShapeDtypeStruct((S_, CONV_W), BF16),
        compiler_params=_cparams(("parallel",)))


def _conv_bwd(name, z, dy, conv_w, after=None):
    S_ = z.shape[0]
    n_chunks = S_ // SEQ_CHUNK
    col = slice(0, 128)
    lo, hi = HALO, HALO + SEQ_CHUNK

    def body(b_ref, c_ref, x_ref, dy_ref, w_ref, db_ref, dc_ref, dx_ref, dw_ref):
        w0, w1, w2 = w_ref[0:1, :], w_ref[1:2, :], w_ref[2:3, :]

        def chunk(i, carry):
            a0, a1, a2 = carry
            r0 = pl.multiple_of(i * SEQ_CHUNK, SEQ_CHUNK)
            cc = _rows_with_halo(c_ref, col, i, n_chunks, True, True)
            xx = _rows_with_halo(x_ref, col, i, n_chunks, True, True)
            bb = _rows_with_halo(b_ref, col, i, n_chunks, True, True)
            dyy = _rows_with_halo(dy_ref, col, i, n_chunks, True, True)
            u = cc * xx
            u1 = _shift_down(u, 1)
            u2 = _shift_down(u, 2)
            y = w2 * u + w1 * u1 + w0 * u2
            dyv = dyy * bb
            du = w2 * dyv + w1 * _shift_up(dyv, 1) + w0 * _shift_up(dyv, 2)
            db_ref[pl.ds(r0, SEQ_CHUNK), :] = (dyy[lo:hi] * y[lo:hi]).astype(BF16)
            dc_ref[pl.ds(r0, SEQ_CHUNK), :] = (du[lo:hi] * xx[lo:hi]).astype(BF16)
            dx_ref[pl.ds(r0, SEQ_CHUNK), :] = (du[lo:hi] * cc[lo:hi]).astype(BF16)
            red = lambda v: jnp.sum(v.reshape(SEQ_CHUNK // 8, 8, 128), axis=0)
            dm = dyv[lo:hi]
            return a0 + red(dm * u2[lo:hi]), a1 + red(dm * u1[lo:hi]), a2 + red(dm * u[lo:hi])

        zero = jnp.zeros((8, 128), F32)
        a0, a1, a2 = lax.fori_loop(0, n_chunks, chunk, (zero, zero, zero))
        dw_ref[0:1, :] = jnp.sum(a0, axis=0, keepdims=True)
        dw_ref[1:2, :] = jnp.sum(a1, axis=0, keepdims=True)
        dw_ref[2:3, :] = jnp.sum(a2, axis=0, keepdims=True)

    sb, sc, sx = _conv_specs(S_)
    slab = pl.BlockSpec((S_, 128), lambda c: (0, c))
    wspec = pl.BlockSpec((3, 128), lambda c: (0, c))
    act = jax.ShapeDtypeStruct((S_, CONV_W), BF16)
    return _call(
        body, [z, z, z, dy, conv_w], [sb, sc, sx, slab, wspec], after, name=name, grid=(CONV_W // 128,),
        out_specs=(slab, slab, slab, wspec),
        out_shape=(act, act, act, jax.ShapeDtypeStruct((3, CONV_W), F32)),
        compiler_params=_cparams(("parallel",)))


def _head_ones(pw):
    a = lax.broadcasted_iota(jnp.int32, (pw, pw), 0) // HEAD
    b = lax.broadcasted_iota(jnp.int32, (pw, pw), 1) // HEAD
    return (a == b).astype(BF16)


def _head_sum(v, ones):
    hi = v.astype(BF16)
    lo = (v - hi.astype(F32)).astype(BF16)
    return jnp.dot(hi, ones, preferred_element_type=F32) + jnp.dot(lo, ones, preferred_element_type=F32)


def _head_norm(v, gain, ones):
    rstd = lax.rsqrt(_head_sum(v * v, ones) * (1.0 / HEAD) + EPS)
    xhat = v * rstd
    return xhat * gain, xhat, rstd


def _head_norm_bwd(dy, xhat, rstd, gain, ones):
    dxhat = dy * gain
    c = _head_sum(dxhat * xhat, ones) * (1.0 / HEAD)
    dv = rstd * (dxhat - xhat * c)
    dg = jnp.sum((dy * xhat).reshape(dy.shape[0] // 8, 8, dy.shape[1]), axis=0)
    return dv, dg


def _head_masks(pw):
    lane_head = lax.broadcasted_iota(jnp.int32, (1, pw), 1) // HEAD
    return [lane_head == h for h in range(pw // HEAD)]


def _only(mask, v):
    return jnp.where(mask, v, jnp.zeros_like(v))


def _attn_specs(S_, g, dil):
    rows = ATTN_BLK * dil
    nb = S_ // rows
    pw = 128 if dil > 1 else ATTN_OUT
    cq, ck, cv = ((OFF_Q + g * ATTN_OUT) // pw, (OFF_K + g * ATTN_OUT) // pw, (OFF_V + g * ATTN_OUT) // pw)
    return rows, nb, pw, pw // HEAD, ATTN_OUT // pw, cq, ck, cv


ATTN_BATCH = 4


def _attn_group(dil):
    return 4 if dil == 1 else 1


def _attn_block_specs(rows, grp, pw, last=None):
    step = (lambda n: n) if last is None else (lambda n: jnp.minimum(n, last))
    cur = lambda c: pl.BlockSpec((rows * grp, pw), lambda hp, n, c=c: (step(n), c + hp))
    prev = lambda c: pl.BlockSpec((rows, pw), lambda hp, n, c=c: (jnp.maximum(step(n) * grp - 1, 0), c + hp))
    return cur, prev


def _band_mask(has_prev):
    qi = lax.broadcasted_iota(jnp.int32, (ATTN_BLK, 2 * ATTN_BLK), 0)
    ki = lax.broadcasted_iota(jnp.int32, (ATTN_BLK, 2 * ATTN_BLK), 1)
    in_prev = jnp.logical_and(ki < ATTN_BLK, ki >= qi)
    if has_prev is not True:
        in_prev = jnp.logical_and(in_prev, has_prev)
    return jnp.logical_or(in_prev, jnp.logical_and(ki >= ATTN_BLK, ki - ATTN_BLK <= qi))


def _rows_of(ref, r, dil):
    if dil == 1:
        return ref[r * ATTN_BLK:(r + 1) * ATTN_BLK, :]
    return ref[pl.ds(r, ATTN_BLK, stride=dil), :]


def _put_rows(ref, r, dil, val):
    if dil == 1:
        ref[r * ATTN_BLK:(r + 1) * ATTN_BLK, :] = val.astype(ref.dtype)
    else:
        ref[pl.ds(r, ATTN_BLK, stride=dil), :] = val.astype(ref.dtype)


def _attn_fwd(name, z, q_gain, k_gain, g, dil):
    S_ = z.shape[0]
    rows, nb, pw, heads, npairs, cq, ck, cv = _attn_specs(S_, g, dil)
    scale = HEAD ** -0.5

    grp = _attn_group(dil)
    nsteps = nb // grp

    def body(q_ref, k_ref, kp_ref, v_ref, vp_ref, gq_ref, gk_ref, o_ref, l_ref):
        n = pl.program_id(1)
        ones, hmask = _head_ones(pw), _head_masks(pw)
        gq, gk = jnp.tile(gq_ref[...], (1, heads)), jnp.tile(gk_ref[...], (1, heads))
        mask_rest = _band_mask(True)

        def step(first):
            if first:
                mask_first = mask_rest[:, ATTN_BLK:]
            else:
                mask_first = _band_mask(n > 0) if dil == 1 else mask_rest
            for r0 in range(0, dil * grp, ATTN_BATCH):
                rs = range(r0, min(r0 + ATTN_BATCH, dil * grp))
                qn, kn, vv, s, p = {}, {}, {}, {}, {}
                kcn = {}
                for r in rs:
                    q, kc, vc = _rows_of(q_ref, r, dil), _rows_of(k_ref, r, dil), _rows_of(v_ref, r, dil)
                    kcn[r] = _head_norm(kc, gk, ones)[0]
                    qn[r] = _head_norm(q, gq, ones)[0].astype(BF16)
                    if first:
                        kn[r], vv[r] = kcn[r].astype(BF16), vc.astype(BF16)
                        continue
                    if dil == 1 and r > 0:
                        kpn = kcn[r - 1] if r - 1 in kcn else _head_norm(_rows_of(k_ref, r - 1, dil), gk, ones)[0]
                        vp = _rows_of(v_ref, r - 1, dil)
                    else:
                        kpn, vp = _head_norm(_rows_of(kp_ref, r, dil), gk, ones)[0], _rows_of(vp_ref, r, dil)
                    kn[r] = jnp.concatenate([kpn, kcn[r]], axis=0).astype(BF16)
                    vv[r] = jnp.concatenate([vp, vc], axis=0).astype(BF16)
                keys = [(r, h) for r in rs for h in range(heads)]
                for r, h in keys:
                    s[r, h] = lax.dot_general(_only(hmask[h], qn[r]), kn[r], _DIMS["nt"],
                                              preferred_element_type=F32) * scale
                lse, den = {}, {}
                for key in keys:
                    mask = mask_rest if (dil == 1 and key[0] > 0) else mask_first
                    sm = jnp.where(mask, s[key], MASK)
                    m = jnp.max(sm, axis=-1, keepdims=True)
                    e = jnp.exp(sm - m)
                    den[key] = jnp.sum(e, axis=-1, keepdims=True)
                    p[key] = e.astype(BF16)
                    lse[key] = m + jnp.log(den[key])
                for r in rs:
                    out = jnp.zeros((ATTN_BLK, pw), F32)
                    lse_all = jnp.zeros((ATTN_BLK, pw), F32)
                    for h in range(heads):
                        out = jnp.where(hmask[h], jnp.dot(p[r, h], vv[r], preferred_element_type=F32) / den[r, h],
                                        out)
                        lse_all = jnp.where(hmask[h], lse[r, h], lse_all)
                    _put_rows(o_ref, r, dil, out)
                    _put_rows(l_ref, r, dil, lse_all)

        if dil == 1:
            step(False)
        else:
            pl.when(n == 0)(functools.partial(step, True))
            pl.when(n > 0)(functools.partial(step, False))

    cur, prev = _attn_block_specs(rows, grp, pw)
    gspec = pl.BlockSpec((1, HEAD), lambda hp, n: (0, 0))
    shp = jax.ShapeDtypeStruct((S_, ATTN_OUT), F32)
    return pl.pallas_call(
        body, name=name, grid=(npairs, nsteps),
        in_specs=[cur(cq), cur(ck), prev(ck), cur(cv), prev(cv), gspec, gspec],
        out_specs=(cur(0), cur(0)), out_shape=(shp, shp),
        compiler_params=_cparams(("parallel", "parallel")),
    )(z, z, z, z, z, q_gain, k_gain)


def _attn_combine(name, os_, ls_):
    S_ = os_[0].shape[0]

    def body(o0, o1, o2, l0, l1, l2, o_ref, l_ref):
        a, b, c = l0[...], l1[...], l2[...]
        m = jnp.maximum(jnp.maximum(a, b), c)
        ea, eb, ec = jnp.exp(a - m), jnp.exp(b - m), jnp.exp(c - m)
        zsum = ea + eb + ec
        o_ref[...] = (ea * o0[...] + eb * o1[...] + ec * o2[...]) / zsum
        l_ref[...] = m + jnp.log(zsum)

    row = pl.BlockSpec((ROW_TILE, ATTN_OUT), lambda i: (i, 0))
    shp = jax.ShapeDtypeStruct((S_, ATTN_OUT), F32)
    return pl.pallas_call(
        body, name=name, grid=(S_ // ROW_TILE,), in_specs=[row] * 6, out_specs=(row, row), out_shape=(shp, shp),
        compiler_params=_cparams(("parallel",)),
    )(*os_, *ls_)


def _attn_bwd(name, z, q_gain, k_gain, do, o, lse, g, dil, after=None):
    S_ = z.shape[0]
    rows, nb, pw, heads, npairs, cq, ck, cv = _attn_specs(S_, g, dil)
    scale = HEAD ** -0.5

    def body(q_ref, kc_ref, kp_ref, vc_ref, vp_ref, gq_ref, gk_ref, do_ref, o_ref, l_ref,
             dq_ref, dk_ref, dv_ref, dgq_ref, dgk_ref, ck_ref, cvv_ref, gq_acc, gk_acc):
        hp = pl.program_id(0)
        n = pl.program_id(1)
        ones = _head_ones(pw)
        gq, gk = jnp.tile(gq_ref[...], (1, heads)), jnp.tile(gk_ref[...], (1, heads))

        @pl.when(jnp.logical_and(n == 0, hp == 0))
        def _():
            gq_acc[...] = jnp.zeros_like(gq_acc)
            gk_acc[...] = jnp.zeros_like(gk_acc)

        refs = (q_ref, kc_ref, kp_ref, vc_ref, vp_ref, do_ref, o_ref, l_ref, dq_ref, dk_ref, dv_ref, ck_ref, cvv_ref,
                gq_acc, gk_acc)
        pl.when(n == 0)(functools.partial(query_step, False, ones, gq, gk, *refs))
        pl.when(jnp.logical_and(n > 0, n < nb))(functools.partial(query_step, True, ones, gq, gk, *refs))

        @pl.when(n == nb)
        def _():
            dgk = jnp.zeros((8, pw), F32)
            for r in range(dil):
                _, kphat, kprstd = _head_norm(_rows_of(kp_ref, r, dil), gk, ones)
                dk_all, dg = _head_norm_bwd(_rows_of(ck_ref, r, dil), kphat, kprstd, gk, ones)
                dgk = dgk + dg
                _put_rows(dk_ref, r, dil, dk_all)
                _put_rows(dv_ref, r, dil, _rows_of(cvv_ref, r, dil))
            gk_acc[...] += dgk

        @pl.when(jnp.logical_and(n == nb, hp == npairs - 1))
        def _():
            fold = lambda a: sum(a[:, h * HEAD:(h + 1) * HEAD] for h in range(heads))
            dgq_ref[...] = jnp.sum(fold(gq_acc[...]), axis=0, keepdims=True)
            dgk_ref[...] = jnp.sum(fold(gk_acc[...]), axis=0, keepdims=True)

    def query_step(has_prev, ones, gq, gk, q_ref, kc_ref, kp_ref, vc_ref, vp_ref, do_ref, o_ref, l_ref,
                   dq_ref, dk_ref, dv_ref, ck_ref, cvv_ref, gq_acc, gk_acc):
        n_keys = (2 if has_prev else 1) * ATTN_BLK
        mask = _band_mask(True)
        if not has_prev:
            mask = mask[:, ATTN_BLK:]
        hmask = _head_masks(pw)
        dgq = jnp.zeros((8, pw), F32)
        dgk = jnp.zeros((8, pw), F32)
        for r0 in range(0, dil, ATTN_BATCH):
            rs = range(r0, min(r0 + ATTN_BATCH, dil))
            keys = [(r, h) for r in rs for h in range(heads)]
            qn, qhat, qrstd, kn, kphat, kprstd, vv, dob, delta, lse = ({} for _ in range(10))
            for r in rs:
                q, kc = _rows_of(q_ref, r, dil), _rows_of(kc_ref, r, dil)
                dov = _rows_of(do_ref, r, dil)
                qn_f, qhat[r], qrstd[r] = _head_norm(q, gq, ones)
                qn[r] = qn_f.astype(BF16)
                kcn, vc = _head_norm(kc, gk, ones)[0], _rows_of(vc_ref, r, dil)
                if has_prev:
                    kpn, kphat[r], kprstd[r] = _head_norm(_rows_of(kp_ref, r, dil), gk, ones)
                    kn[r] = jnp.concatenate([kpn, kcn], axis=0).astype(BF16)
                    vv[r] = jnp.concatenate([_rows_of(vp_ref, r, dil), vc], axis=0).astype(BF16)
                else:
                    kn[r], vv[r] = kcn.astype(BF16), vc.astype(BF16)
                dob[r] = dov.astype(BF16)
                delta[r] = _head_sum(dov * _rows_of(o_ref, r, dil), ones)
                lse[r] = _rows_of(l_ref, r, dil)
            s, dp = {}, {}
            for r, h in keys:
                s[r, h] = lax.dot_general(_only(hmask[h], qn[r]), kn[r], _DIMS["nt"],
                                          preferred_element_type=F32) * scale
                dp[r, h] = lax.dot_general(_only(hmask[h], dob[r]), vv[r], _DIMS["nt"], preferred_element_type=F32)
            p, ds = {}, {}
            for r, h in keys:
                col = slice(h * HEAD, h * HEAD + 1)
                pk = jnp.where(mask, jnp.exp(jnp.where(mask, s[r, h], MASK) - lse[r][:, col]), 0.0)
                ds[r, h] = (pk * (dp[r, h] - delta[r][:, col]) * scale).astype(BF16)
                p[r, h] = pk.astype(BF16)
            dqn, dkn, dvv = {}, {}, {}
            for r in rs:
                dqn[r] = jnp.zeros((ATTN_BLK, pw), F32)
                dkn[r] = jnp.zeros((n_keys, pw), F32)
                dvv[r] = jnp.zeros((n_keys, pw), F32)
                for h in range(heads):
                    dqn[r] = jnp.where(hmask[h], jnp.dot(ds[r, h], kn[r], preferred_element_type=F32), dqn[r])
                    dkn[r] = jnp.where(hmask[h], lax.dot_general(ds[r, h], qn[r], _DIMS["tn"],
                                                                 preferred_element_type=F32), dkn[r])
                    dvv[r] = jnp.where(hmask[h], lax.dot_general(p[r, h], dob[r], _DIMS["tn"],
                                                                 preferred_element_type=F32), dvv[r])
            for r in rs:
                dq_all, dg = _head_norm_bwd(dqn[r], qhat[r], qrstd[r], gq, ones)
                dgq = dgq + dg
                _put_rows(dq_ref, r, dil, dq_all)
                if has_prev:
                    dk_all, dg = _head_norm_bwd(_rows_of(ck_ref, r, dil) + dkn[r][:ATTN_BLK], kphat[r], kprstd[r], gk,
                                                ones)
                    dgk = dgk + dg
                    _put_rows(dk_ref, r, dil, dk_all)
                    _put_rows(dv_ref, r, dil, _rows_of(cvv_ref, r, dil) + dvv[r][:ATTN_BLK])
                _put_rows(ck_ref, r, dil, dkn[r][n_keys - ATTN_BLK:])
                _put_rows(cvv_ref, r, dil, dvv[r][n_keys - ATTN_BLK:])
        gq_acc[...] += dgq
        gk_acc[...] += dgk

    last = nb - 1
    cur = lambda c: pl.BlockSpec((rows, pw), lambda hp, n, c=c: (jnp.minimum(n, last), c + hp))
    prev = lambda c: pl.BlockSpec((rows, pw), lambda hp, n, c=c: (jnp.maximum(n - 1, 0), c + hp))
    gspec = pl.BlockSpec((1, HEAD), lambda hp, n: (0, 0))
    act = jax.ShapeDtypeStruct((S_, ATTN_OUT), F32)
    vec = jax.ShapeDtypeStruct((1, HEAD), F32)
    return _call(
        body, [z, z, z, z, z, q_gain, k_gain, do, o, lse],
        [cur(cq), cur(ck), prev(ck), cur(cv), prev(cv), gspec, gspec, cur(0), cur(0), cur(0)], after,
        name=name, grid=(npairs, nb + 1),
        out_specs=(cur(0), prev(0), prev(0), gspec, gspec),
        out_shape=(act, act, act, vec, vec),
        scratch_shapes=[pltpu.VMEM((rows, pw), F32), pltpu.VMEM((rows, pw), F32),
                        pltpu.VMEM((8, pw), F32), pltpu.VMEM((8, pw), F32)],
        compiler_params=_cparams(("arbitrary", "arbitrary")))


MIX_TN = 256
MIX_TM = 2048


def _sigmoid(v):
    return 1.0 / (1.0 + jnp.exp(-v))


def _mix_fwd(name, z, b_gate, ys, ws):
    S_ = z.shape[0]
    tm, tn = MIX_TM, MIX_TN
    gblk = OFF_GATE // tn

    def body(yp, yc, ya, wp, wc, wa, g0, g1, g2, b0, b1, b2, m_ref):
        acc = None
        for y_ref, w_ref, g_ref, b_ref in ((yp, wp, g0, b0), (yc, wc, g1, b1), (ya, wa, g2, b2)):
            u = lax.dot_general(y_ref[...].astype(BF16), w_ref[...], _DIMS["nt"], preferred_element_type=F32)
            t = _sigmoid(g_ref[...] + b_ref[...]) * u
            acc = t if acc is None else acc + t
        m_ref[...] = acc.astype(BF16)

    yspec = lambda w: pl.BlockSpec((tm, w), lambda i, j: (i, 0))
    wspec = lambda w: pl.BlockSpec((tn, w), lambda i, j: (j, 0))
    gspec = lambda b: pl.BlockSpec((tm, tn), lambda i, j, b=b: (i, gblk + b * (D // tn) + j))
    bspec = lambda b: pl.BlockSpec((1, tn), lambda i, j, b=b: (0, b * (D // tn) + j))
    return pl.pallas_call(
        body, name=name, grid=(S_ // tm, D // tn),
        in_specs=[yspec(POOL_W), yspec(CONV_W), yspec(ATTN_OUT), wspec(POOL_W), wspec(CONV_W), wspec(ATTN_OUT),
                  gspec(0), gspec(1), gspec(2), bspec(0), bspec(1), bspec(2)],
        out_specs=pl.BlockSpec((tm, tn), lambda i, j: (i, j)),
        out_shape=jax.ShapeDtypeStruct((S_, D), BF16),
        compiler_params=_cparams(("parallel", "parallel")),
    )(*ys, *ws, z, z, z, b_gate, b_gate, b_gate)


def _mix_bwd(name, z, b_gate, y, w, dmerged, branch, dz, after=None):
    S_ = z.shape[0]
    tm, tn = MIX_TM, MIX_TN
    width = y.shape[1]
    gblk = OFF_GATE // tn + branch * (D // tn)
    nj = D // tn

    def body(y_ref, w_ref, g_ref, b_ref, dm_ref, dy_ref, dw_ref, dg_ref, db_ref, acc_ref):
        i, j = pl.program_id(0), pl.program_id(1)
        yb = y_ref[...].astype(BF16)
        u = lax.dot_general(yb, w_ref[...], _DIMS["nt"], preferred_element_type=F32)
        sg = _sigmoid(g_ref[...] + b_ref[...])
        dm = dm_ref[...]
        du = (sg * dm).astype(BF16)
        dpre = dm * u * sg * (1.0 - sg)
        dg_ref[...] = dpre.astype(BF16)
        cols = pl.ds(pl.multiple_of(j * tn, tn), tn)
        db_ref[...] = _rows8(dpre)
        d_w = lax.dot_general(du, yb, _DIMS["tn"], preferred_element_type=F32)
        d_y = jnp.dot(du, w_ref[...], preferred_element_type=F32)

        @pl.when(i == 0)
        def _():
            dw_ref[cols, :] = d_w

        @pl.when(i > 0)
        def _():
            dw_ref[cols, :] += d_w

        @pl.when(j == 0)
        def _():
            acc_ref[...] = d_y

        @pl.when(j > 0)
        def _():
            acc_ref[...] += d_y

        @pl.when(j == nj - 1)
        def _():
            dy_ref[...] = acc_ref[...]

    rows = pl.BlockSpec((tm, width), lambda i, j: (i, 0))
    blk = pl.BlockSpec((tm, tn), lambda i, j: (i, j))
    gate = pl.BlockSpec((tm, tn), lambda i, j: (i, gblk + j))
    args = [y, w, z, b_gate, dmerged]
    in_specs = [rows, pl.BlockSpec((tn, width), lambda i, j: (j, 0)), gate,
                pl.BlockSpec((1, tn), lambda i, j: (0, branch * nj + j)), blk]
    aliases = {}
    if not isinstance(dz, jax.ShapeDtypeStruct):
        body = _drop_ref(body, len(args))
        aliases = {len(args): 2}
        args.append(dz)
        in_specs.append(ANY)
    return _call(
        body, args, in_specs, after, name=name, grid=(S_ // tm, nj),
        out_specs=(rows, pl.BlockSpec((D, width), lambda i, j: (0, 0)), gate, pl.BlockSpec((8, tn), lambda i, j: (i, j))),
        out_shape=(jax.ShapeDtypeStruct((S_, width), F32), jax.ShapeDtypeStruct((D, width), F32),
                   jax.ShapeDtypeStruct(dz.shape, dz.dtype), jax.ShapeDtypeStruct((S_ // tm * 8, D), F32)),
        scratch_shapes=[pltpu.VMEM((tm, width), F32)], input_output_aliases=aliases,
        compiler_params=_cparams(("arbitrary", "arbitrary")))


def _relu2_epilogue(acc):
    r = jnp.maximum(acc, 0.0)
    return (r * r,)


def _relu2_bwd_epilogue(acc, r):
    return (acc * (2.0 * jnp.sqrt(r.astype(F32))),)


def _residual_norm_epilogue(acc, xr, gain):
    x = xr + acc
    ms = jnp.mean(x * x, axis=-1, keepdims=True)
    return x, x * lax.rsqrt(ms + EPS) * gain


def _fold_rows(name, part, lanes=False, scale=1.0):
    R, N = part.shape

    def body(p_ref, o_ref):
        s = jnp.sum(p_ref[...], axis=0, keepdims=True)
        if lanes:
            s = jnp.sum(s, axis=1, keepdims=True)
        o_ref[...] = s * scale

    out_n = 1 if lanes else N
    return pl.pallas_call(
        body, name=name, grid=(1,), in_specs=[pl.BlockSpec((R, N), lambda i: (0, 0))],
        out_specs=pl.BlockSpec((1, out_n), lambda i: (0, 0)), out_shape=jax.ShapeDtypeStruct((1, out_n), F32))(part)


def _rows8(v):
    return jnp.sum(v.reshape(v.shape[0] // 8, 8, v.shape[1]), axis=0)


def _residual_loss_epilogue(acc, xr, target):
    e = xr + acc - target
    dy = e * (1.0 / D)
    return dy, _rows8(e * e), dy


def _norm_bwd_epilogue(acc, x, dres, gain):
    rstd = lax.rsqrt(jnp.mean(x * x, axis=-1, keepdims=True) + EPS)
    xhat = x * rstd
    dxhat = acc * gain
    c = jnp.mean(dxhat * xhat, axis=-1, keepdims=True)
    return dres + rstd * (dxhat - xhat * c), _rows8(acc * xhat)


def _norm_bwd_epilogue_twice(acc, x, dres, gain):
    dx, part = _norm_bwd_epilogue(acc, x, dres, gain)
    return dx, part, dx


def _layer_fwd(l, x, h, p, next_gain, target=None):
    t = f"l{l}_"
    z = _matmul(t + "in_proj", h, p["w_in_t"], "nt", 512, 3712, 1024)
    y_pool = _pool_fwd(t + "pool", z, p["pool_mix"], p["pool_scale"])
    os_, ls_ = [], []
    for g, (_, dil) in enumerate(ATTN_GROUPS):
        o_g, l_g = _attn_fwd(t + f"attn{g}", z, p["q_gain"], p["k_gain"], g, dil)
        os_.append(o_g)
        ls_.append(l_g)
    y_attn, lse = _attn_combine(t + "attn_mix", os_, ls_)
    y_conv = _conv_fwd(t + "conv", z, p["conv_w"], after=y_attn)
    merged = _mix_fwd(t + "merge", z, p["b_gate"], (y_pool, y_conv, y_attn),
                      (p["w_pool_up_t"], p["w_conv_out_t"], p["w_attn_up_t"]))
    x1, h2 = _matmul(t + "out_proj", merged, p["rest"], "nn", 1024, D, 1024, out_dtypes=(F32, BF16),
                     extras=((x, "mn"), (p["norm_mlp"], "n")), epilogue=_residual_norm_epilogue, b_rows=ROWS_WO)
    r = _matmul(t + "ff1", h2, p["rest"], "nt", 1024, 1024, 1024, out_dtypes=(BF16,), epilogue=_relu2_epilogue,
                b_rows=ROWS_FF1)
    if next_gain is None:
        dy, h_out, dy_16 = _matmul(t + "ff2", r, p["rest"], "nn", 512, D, D_FF,
                                   out_dtypes=(F32, (F32, "rows8"), BF16), extras=((x1, "mn"), (target, "mn")),
                                   epilogue=_residual_loss_epilogue, b_rows=ROWS_FF2)
        x2 = (dy, dy_16)
    else:
        x2, h_out = _matmul(t + "ff2", r, p["rest"], "nn", 512, D, D_FF, out_dtypes=(F32, BF16),
                            extras=((x1, "mn"), (next_gain, "n")), epilogue=_residual_norm_epilogue,
                            b_rows=ROWS_FF2)
    saved = dict(x=x, h=h, z=z, y_pool=y_pool, y_conv=y_conv, y_attn=y_attn, lse=lse, merged=merged,
                 x1=x1, h2=h2, r=r)
    return x2, h_out, saved


def _layer_bwd(l, dx2, p, s, pending, in_t_all, collective_id, start_after=None):
    t = f"l{l}_b_"
    g = {}
    rest = jax.ShapeDtypeStruct((N_DEV, REST_ROWS, D), F32)
    dx2, dx2_16 = dx2
    da = _matmul(t + "d_ff2_in", dx2_16, p["rest"], "nt", 1024, 1024, 1024, out_dtypes=(BF16,),
                 extras=((s["r"], "mn"),), epilogue=_relu2_bwd_epilogue,
                 after=start_after, b_rows=ROWS_FF2)
    rest = _matmul(t + "dw_ff2", s["r"], dx2_16, "tn", 512, 1024, 4096, into=(rest, REST_FF2, 0), after=da)
    tok = rest if pending is None else pending.chip_sums(after=rest)
    dx1, part, dx1_16 = _matmul(
        t + "d_ff1_in", da, p["rest"], "nn", 512, D, D_FF, out_dtypes=(F32, (F32, "rows8"), BF16),
        extras=((s["x1"], "mn"), (dx2, "mn"), (p["norm_mlp"], "n")), epilogue=_norm_bwd_epilogue_twice,
        after=tok, b_rows=ROWS_FF1)
    g["norm_mlp"] = _fold_rows(t + "d_norm_mlp", part)
    rest = _matmul(t + "dw_ff1", da, s["h2"], "tn", 512, 1024, 4096, into=(rest, REST_FF1, 0), after=dx1)
    dmerged = _matmul(t + "d_out_proj_in", dx1_16, p["rest"], "nt", 1024, 1024, 1024, b_rows=ROWS_WO)
    rest = _matmul(t + "dw_o", s["merged"], dx1_16, "tn", 128, 1024, 4096, into=(rest, REST_WO, 0), after=dmerged)
    ys = (s["y_pool"], s["y_conv"], s["y_attn"])
    names = ("w_pool_up_t", "w_conv_out_t", "w_attn_up_t")
    dys, dbs = [], []
    tok = rest
    dz = jax.ShapeDtypeStruct((dx2.shape[0], IN_COLS), BF16)
    for b in range(3):
        dy_b, dw_t, dz, db = _mix_bwd(t + f"merge{b}", s["z"], p["b_gate"], ys[b], p[names[b]], dmerged, b, dz,
                                      after=tok)
        tok = dy_b
        width = ys[b].shape[1]
        if b < 2:
            rest = rest.at[:, REST_UP:REST_ATTN, b * width:(b + 1) * width].set(
                dw_t.reshape(N_DEV, REST_ATTN - REST_UP, width))
        else:
            rest = rest.at[:, REST_ATTN:REST_ROWS, :].set(dw_t.reshape(N_DEV, REST_ROWS - REST_ATTN, D))
        dys.append(dy_b)
        dbs.append(_fold_rows(t + f"d_b_gate{b}", db))
    g["b_gate"] = jnp.concatenate(dbs, axis=1)
    rs_rest = _ReduceScatter(f"rs_rest{l}", collective_id, rest)
    if pending is not None:
        in_t_all = pending.finish(after=rest, into=(in_t_all, l + 1))
    dz, g["pool_mix"], g["pool_scale"] = _pool_bwd(t + "pool", s["z"], dys[0], p["pool_mix"], p["pool_scale"], dz,
                                                   after=rest if pending is None else in_t_all)
    dcb, dcc, dcx, g["conv_w"] = _conv_bwd(t + "conv", s["z"], dys[1], p["conv_w"], after=dz)
    tok = rs_rest.chip_sums(after=dcb)
    dqs, dks, dvs = [], [], []
    gq = gk = None
    for gi, (_, dil) in enumerate(ATTN_GROUPS):
        dq, dk, dv, dgq, dgk = _attn_bwd(t + f"attn{gi}", s["z"], p["q_gain"], p["k_gain"], dys[2], s["y_attn"],
                                         s["lse"], gi, dil, after=tok)
        tok = dq
        dqs.append(dq)
        dks.append(dk)
        dvs.append(dv)
        gq = dgq if gq is None else gq + dgq
        gk = dgk if gk is None else gk + dgk
    g["q_gain"], g["k_gain"] = gq, gk
    rest_sum = rs_rest.finish(after=tok)
    col = OFF_CB
    for piece in [dcb, dcc, dcx] + dqs + dks + dvs:
        dz = lax.dynamic_update_slice(dz, piece.astype(BF16), (0, col))
        col += piece.shape[1]
    in_t = _matmul(t + "dw_in", dz, s["h"], "tn", 256, 1024, 4096, after=rest_sum)
    rs_in = _ReduceScatter(f"rs_in{l}", collective_id + 2, in_t.reshape(N_DEV, IN_ROWS, D))
    dx = _matmul(t + "d_in_proj_in", dz, p["w_in_t"], "nn", 256, D, IN_COLS,
                 out_dtypes=(F32, (F32, "rows8")) + ((BF16,) if l > 0 else ()),
                 extras=((s["x"], "mn"), (dx1, "mn"), (p["norm_mix"], "n")),
                 epilogue=_norm_bwd_epilogue_twice if l > 0 else _norm_bwd_epilogue, after=in_t)
    dx, part = (dx[0], dx[2]) if l > 0 else (dx[0], None), dx[1]
    g["norm_mix"] = _fold_rows(t + "d_norm_mix", part)
    return dx, g, rest_sum, in_t_all, rs_in


ANY = pl.BlockSpec(memory_space=pl.ANY)


def _mesh_pos():
    return lax.axis_index("x"), lax.axis_index("y"), lax.axis_index("c")


def _other_chips(x, y):
    return [(1 - x, y), (x, 1 - y), (1 - x, 1 - y)]


def _comm_call(name, collective_id, peers, body, arrs, out_shape, sem_counts, after=None):
    n_in, n_out = len(arrs), len(out_shape)
    if collective_id is None:
        def tc_body(*refs):
            body(refs[:n_in], refs[n_in:n_in + n_out], *refs[n_in + n_out:])

        return pl.pallas_call(
            tc_body, name=name, out_shape=tuple(out_shape), in_specs=[ANY] * n_in, out_specs=(ANY,) * n_out,
            scratch_shapes=[pltpu.SemaphoreType.DMA((n,)) for n in sem_counts])(*arrs)

    n_after = 0 if after is None else 1

    def seq_body(*refs):
        barrier = pltpu.get_barrier_semaphore()
        ps = peers(*_mesh_pos())
        for p in ps:
            pl.semaphore_signal(barrier, inc=1, device_id=p, device_id_type=MESH)
        pl.semaphore_wait(barrier, len(ps))
        outs = refs[n_in + n_after:]
        body(refs[:n_in], outs[:n_out], *outs[n_out:])

    return pl.kernel(
        seq_body, out_type=tuple(out_shape), mesh=plsc.ScalarSubcoreMesh(axis_name="seq", num_cores=1), name=name,
        scratch_types=[pltpu.SemaphoreType.DMA((n,)) for n in sem_counts],
        compiler_params=pltpu.CompilerParams(collective_id=collective_id),
    )(*arrs, *([] if after is None else [after]))


def _all_gather(name, collective_id, shard, after=None):
    R, C = shard.shape

    def peers(x, y, c):
        return [(x, y, 1 - c)] + [(*chip, c) for chip in _other_chips(x, y)]

    def body(in_refs, out_refs, send_sems, recv_sems, local_sems):
        (x_ref,), (out_ref,) = in_refs, out_refs
        x, y, c = _mesh_pos()
        me, sibling = (x, y, c), (x, y, 1 - c)
        chips = _other_chips(x, y)

        def slot(px, py, pc):
            return out_ref.at[4 * px + 2 * py + pc]

        def copy(k, block, to, src=None):
            return pltpu.make_async_remote_copy(
                src_ref=slot(*block) if src is None else src, dst_ref=slot(*block),
                send_sem=send_sems.at[k], recv_sem=recv_sems.at[k], device_id=to, device_id_type=MESH)

        mine = pltpu.make_async_copy(x_ref, slot(*me), local_sems.at[0])
        mine.start()
        first = [copy(0, me, sibling, src=x_ref)]
        first += [copy(1 + j, me, (*chip, c), src=x_ref) for j, chip in enumerate(chips)]
        for cp in first:
            cp.start()
        passed = [copy(4 + j, (*chip, c), sibling) for j, chip in enumerate(chips)]
        for j, chip in enumerate(chips):
            copy(1 + j, (*chip, c), me).wait_recv()
            passed[j].start()
        copy(0, sibling, me).wait_recv()
        for j, chip in enumerate(chips):
            copy(4 + j, (*chip, 1 - c), me).wait_recv()
        for cp in first + passed:
            cp.wait_send()
        mine.wait()

    return _comm_call(name, collective_id, peers, body, [shard],
                      [jax.ShapeDtypeStruct((N_DEV, R, C), shard.dtype)], (7, 7, 1), after)[0]


def _rs_sibling_exchange(name, collective_id, arrs):
    n = len(arrs)

    def body(in_refs, out_refs, send_sems, recv_sems):
        x, y, c = _mesh_pos()
        cps = []
        for k, (src, dst) in enumerate(zip(in_refs, out_refs)):
            src = src.at[:, 1 - c] if len(src.shape) == 4 else src
            cps.append(pltpu.make_async_remote_copy(src_ref=src, dst_ref=dst, send_sem=send_sems.at[k],
                                                    recv_sem=recv_sems.at[k], device_id=(x, y, 1 - c),
                                                    device_id_type=MESH))
        for cp in cps:
            cp.start()
        for cp in cps:
            cp.wait()

    out_shape = [jax.ShapeDtypeStruct(a.shape[:1] + a.shape[2:] if a.ndim == 4 else a.shape, a.dtype) for a in arrs]
    return _comm_call(name, collective_id, lambda x, y, c: [(x, y, 1 - c)], body, arrs, out_shape, (n, n))


def _rs_rows(r):
    return r // 2 if (r // 2) % 16 == 0 else r


def _drop_ref(body, idx):
    def wrapped(*refs):
        return body(*refs[:idx], *refs[idx + 1:])

    return wrapped


def _rs_chip_sum(name, ids, big, rbig, after=None):
    _, _, R, C = big.shape
    rows = _rs_rows(R)

    def body(ids_ref, mine_ref, theirs_ref, t16_ref, own_ref):
        p = pl.program_id(1)
        s = mine_ref[...] + theirs_ref[...]
        t16_ref[...] = s.astype(BF16)

        @pl.when(p == ids_ref[1])
        def _():
            own_ref[...] = s

    in_specs = [pl.BlockSpec((None, None, rows, C), lambda i, p, ids: (p, ids[0], i, 0)),
                pl.BlockSpec((None, rows, C), lambda i, p, ids: (p, i, 0))]
    args = [ids, big, rbig]
    if after is not None:
        body = _drop_ref(body, len(args))
        in_specs.append(ANY)
        args.append(after)
    grid_spec = pltpu.PrefetchScalarGridSpec(
        num_scalar_prefetch=1, grid=(R // rows, 4), in_specs=in_specs,
        out_specs=(pl.BlockSpec((None, rows, C), lambda i, p, ids: (p, i, 0)),
                   pl.BlockSpec((rows, C), lambda i, p, ids: (i, 0))))
    return pl.pallas_call(
        body, name=name, grid_spec=grid_spec,
        out_shape=(jax.ShapeDtypeStruct((4, R, C), BF16), jax.ShapeDtypeStruct((R, C), F32)),
        compiler_params=_cparams(("parallel", "arbitrary")),
    )(*args)


def _add2(name, a, b):
    def body(a_ref, b_ref, o_ref):
        o_ref[...] = a_ref[...] + b_ref[...]

    full = pl.BlockSpec(a.shape, lambda i: (0, 0))
    return pl.pallas_call(body, name=name, grid=(1,), in_specs=[full, full], out_specs=full,
                          out_shape=jax.ShapeDtypeStruct(a.shape, a.dtype))(a, b)


def _rs_chip_exchange(name, collective_id, arrs, after=None):
    n = len(arrs)

    def body(in_refs, out_refs, send_sems, recv_sems, local_sems):
        x, y, c = _mesh_pos()
        p_me = 2 * x + y
        chips = _other_chips(x, y)

        def part(ref, p):
            return ref.at[p] if len(ref.shape) == 3 else ref

        local = [pltpu.make_async_copy(part(src, p_me), dst.at[p_me], local_sems.at[k])
                 for k, (src, dst) in enumerate(zip(in_refs, out_refs))]
        for cp in local:
            cp.start()
        sends = []
        for j, (px, py) in enumerate(chips):
            for k, (src, dst) in enumerate(zip(in_refs, out_refs)):
                sends.append(pltpu.make_async_remote_copy(
                    src_ref=part(src, 2 * px + py), dst_ref=dst.at[p_me], send_sem=send_sems.at[n * j + k],
                    recv_sem=recv_sems.at[n * j + k], device_id=(px, py, c), device_id_type=MESH))
        for cp in sends:
            cp.start()
        for j, (px, py) in enumerate(chips):
            for k, (src, dst) in enumerate(zip(in_refs, out_refs)):
                pltpu.make_async_remote_copy(
                    src_ref=part(src, p_me), dst_ref=dst.at[2 * px + py], send_sem=send_sems.at[n * j + k],
                    recv_sem=recv_sems.at[n * j + k], device_id=(px, py, c), device_id_type=MESH).wait_recv()
        for cp in sends:
            cp.wait_send()
        for cp in local:
            cp.wait()

    out_shape = [jax.ShapeDtypeStruct((4,) + a.shape[-2:], a.dtype) for a in arrs]
    return _comm_call(name, collective_id, lambda x, y, c: [(*chip, c) for chip in _other_chips(x, y)], body, arrs,
                      out_shape, (3 * n, 3 * n, n), after)


def _rs_final_sum(name, ids, recv, own, rows, after=None, into=None):
    _, R, C = recv.shape
    assert R % rows == 0

    def body(ids_ref, r_ref, own_ref, o_ref):
        acc = None
        for p in range(4):
            term = jnp.where(ids_ref[1] == p, own_ref[...], r_ref[p].astype(F32))
            acc = term if acc is None else acc + term
        o_ref[...] = acc

    in_specs = [pl.BlockSpec((4, rows, C), lambda i, ids: (0, i, 0)), pl.BlockSpec((rows, C), lambda i, ids: (i, 0))]
    args = [ids, recv, own]
    if after is not None:
        body = _drop_ref(body, len(args))
        in_specs.append(ANY)
        args.append(after)
    out_spec, out_shape, aliases = pl.BlockSpec((rows, C), lambda i, ids: (i, 0)), jax.ShapeDtypeStruct((R, C), F32), {}
    if into is not None:
        buf, slot = into
        out_spec = pl.BlockSpec((None, rows, C), lambda i, ids: (slot, i, 0))
        out_shape = jax.ShapeDtypeStruct(buf.shape, F32)
        if not isinstance(buf, jax.ShapeDtypeStruct):
            body = _drop_ref(body, len(args))
            aliases = {len(args): 0}
            in_specs.append(ANY)
            args.append(buf)
    grid_spec = pltpu.PrefetchScalarGridSpec(
        num_scalar_prefetch=1, grid=(R // rows,), in_specs=in_specs, out_specs=out_spec)
    return pl.pallas_call(
        body, name=name, grid_spec=grid_spec, out_shape=out_shape, input_output_aliases=aliases,
        compiler_params=_cparams(("parallel",)),
    )(*args)


class _ReduceScatter:
    def __init__(self, tag, collective_id, big):
        x, y, c = _mesh_pos()
        self.tag, self.cid = tag, collective_id
        self.ids = jnp.stack([c, 2 * x + y]).astype(jnp.int32)
        self.big = big.reshape((4, 2) + big.shape[1:])
        (self.got,) = _rs_sibling_exchange(tag + "_sibling", collective_id, [self.big])

    def chip_sums(self, after=None):
        t16, self.own = _rs_chip_sum(self.tag + "_chip_sum", self.ids, self.big, self.got, after)
        (self.recv,) = _rs_chip_exchange(self.tag + "_chips", self.cid + 1, [t16])
        return t16

    def finish(self, after=None, into=None):
        return _rs_final_sum(self.tag + "_final", self.ids, self.recv, self.own, _rs_rows(self.own.shape[0]), after,
                             into)


def _all_reduce_small(tag, small, collective_id, launch_after):
    x, y, c = _mesh_pos()
    ids = jnp.stack([c, 2 * x + y]).astype(jnp.int32)
    (theirs,) = _rs_sibling_exchange(tag + "_sibling", None, [small])
    ts = _add2(tag + "_chip_sum", small, theirs)
    (recv,) = _rs_chip_exchange(tag + "_chips", collective_id, [ts], launch_after)
    return lambda after: _rs_final_sum(tag + "_final", ids, recv, ts, small.shape[0], after)


def _adamw(name, w, g, m, v):
    R, C = w.shape
    tiles = [t for t in range(8, 513, 8) if R % t == 0]
    tr = max(tiles) if tiles else R
    c1 = 1.0 - ADAM_B1 ** ADAM_STEP
    c2 = 1.0 - ADAM_B2 ** ADAM_STEP

    def body(w_ref, g_ref, m_ref, v_ref, d_ref, nm_ref, nv_ref):
        gv = g_ref[...]
        nm = ADAM_B1 * m_ref[...] + (1.0 - ADAM_B1) * gv
        nv = ADAM_B2 * v_ref[...] + (1.0 - ADAM_B2) * (gv * gv)
        d_ref[...] = -ADAM_LR * ((nm / c1) / (jnp.sqrt(nv / c2) + ADAM_EPS) + ADAM_WD * w_ref[...])
        nm_ref[...] = nm
        nv_ref[...] = nv

    blk = pl.BlockSpec((tr, C), lambda i: (i, 0))
    shp = jax.ShapeDtypeStruct((R, C), F32)
    return pl.pallas_call(
        body, name=name, grid=(R // tr,), in_specs=[blk] * 4, out_specs=(blk,) * 3, out_shape=(shp,) * 3,
        compiler_params=_cparams(("parallel",)),
    )(w, g, m, v)


def _adamw_t(name, w, g_t, m, v):
    L, R, C = w.shape
    tr = 256
    c1 = 1.0 - ADAM_B1 ** ADAM_STEP
    c2 = 1.0 - ADAM_B2 ** ADAM_STEP

    def body(w_ref, g_ref, m_ref, v_ref, go_ref, d_ref, nm_ref, nv_ref):
        gv = g_ref[...].T
        go_ref[...] = gv
        nm = ADAM_B1 * m_ref[...] + (1.0 - ADAM_B1) * gv
        nv = ADAM_B2 * v_ref[...] + (1.0 - ADAM_B2) * (gv * gv)
        d_ref[...] = -ADAM_LR * ((nm / c1) / (jnp.sqrt(nv / c2) + ADAM_EPS) + ADAM_WD * w_ref[...])
        nm_ref[...] = nm
        nv_ref[...] = nv

    blk = pl.BlockSpec((None, tr, C), lambda l, i: (l, i, 0))
    blk_t = pl.BlockSpec((None, C, tr), lambda l, i: (l, 0, i))
    shp = jax.ShapeDtypeStruct((L, R, C), F32)
    return pl.pallas_call(
        body, name=name, grid=(L, R // tr), in_specs=[blk, blk_t, blk, blk], out_specs=(blk,) * 4,
        out_shape=(shp,) * 4, compiler_params=_cparams(("parallel", "parallel")),
    )(w, g_t, m, v)


REST_FF1 = 0
REST_FF2 = 512
REST_WO = 1024
REST_UP = 1152
REST_ATTN = 1280
REST_ROWS = 1312
IN_ROWS = IN_COLS // N_DEV
ROWS_FF1 = (REST_FF1, D_FF // N_DEV)
ROWS_FF2 = (REST_FF2, D_FF // N_DEV)
ROWS_WO = (REST_WO, D // N_DEV)
SMALL = (("norm_mix", (DEPTH, D)), ("b_gate", (DEPTH, 3 * D)), ("pool_mix", (DEPTH, 4, POOL_G, POOL_G)),
         ("pool_scale", (DEPTH, POOL_W)), ("conv_w", (DEPTH, 3, CONV_W)), ("q_gain", (DEPTH, HEAD)),
         ("k_gain", (DEPTH, HEAD)), ("norm_mlp", (DEPTH, D)))


def _pack_weight_shards(w, l):
    b = lambda a: a.astype(BF16)
    rest = jnp.concatenate([
        b(w["w_ff1"][l].T), b(w["w_ff2"][l]), b(w["w_o"][l]),
        jnp.concatenate([b(w["w_pool_up"][l].T), b(w["w_conv_out"][l].T)], axis=1),
        b(w["w_attn_up"][l].T).reshape(REST_ROWS - REST_ATTN, D)], axis=0)
    return b(w["w_in"][l].T), rest


def _unpack_gathered(g_in, g_rest, g_conv, small_w, l):
    rest = g_rest
    take = lambda r0, rows, c0=0, cols=D: rest[:, r0:r0 + rows, c0:c0 + cols].reshape(N_DEV * rows, cols)
    conv = g_conv[:, 3 * l:3 * l + 3, :CONV_W // N_DEV]
    p = {
        "w_in_t": g_in.reshape(IN_COLS, D),
        "rest": rest,
        "w_pool_up_t": take(REST_UP, 128, 0, POOL_W), "w_conv_out_t": take(REST_UP, 128, POOL_W, CONV_W),
        "w_attn_up_t": rest[:, REST_ATTN:].reshape(D, ATTN_OUT),
        "conv_w": jnp.transpose(conv, (1, 0, 2)).reshape(3, CONV_W),
        "pool_mix": small_w["pool_mix"][l],
    }
    for name in ("norm_mix", "b_gate", "pool_scale", "q_gain", "k_gain", "norm_mlp"):
        p[name] = small_w[name][l][None, :]
    return p


def _pack_small_grads(grads, loss):
    flat = jnp.concatenate([jnp.stack([grads[l][name] for l in range(DEPTH)]).reshape(-1) for name, _ in SMALL]
                           + [loss.reshape(1)])
    return jnp.pad(flat, (0, (-flat.shape[0]) % (8 * 128))).reshape(-1, 128)


def _unpack_grads(rest):
    return {
        "w_ff1_t": jnp.stack([a[REST_FF1:REST_FF1 + 512] for a in rest]),
        "w_ff2": jnp.stack([a[REST_FF2:REST_FF2 + 512] for a in rest]),
        "w_o": jnp.stack([a[REST_WO:REST_WO + 128] for a in rest]),
        "w_pool_up": jnp.stack([a[REST_UP:REST_UP + 128, :POOL_W].T for a in rest]),
        "w_conv_out": jnp.stack([a[REST_UP:REST_UP + 128, POOL_W:].T for a in rest]),
        "w_attn_up": jnp.stack([a[REST_ATTN:].reshape(D // N_DEV, ATTN_OUT).T for a in rest]),
    }


def _unpack_small_like_grads(small, dev):
    out = {}
    flat = small.reshape(-1)
    off = 0
    for name, shp in SMALL:
        n = 1
        for s in shp:
            n *= s
        out[name] = flat[off:off + n].reshape(shp)
        off += n
    out["loss"] = flat[off]
    width = CONV_W // N_DEV
    out["conv_w"] = lax.dynamic_slice_in_dim(out["conv_w"], dev * width, width, axis=2)
    return out


WEIGHTS = ("norm_mix", "w_in", "b_gate", "pool_mix", "pool_scale", "conv_w", "q_gain", "k_gain", "w_pool_up",
           "w_conv_out", "w_attn_up", "w_o", "norm_mlp", "w_ff1", "w_ff2")


def kernel(x, norm_mix, w_in, b_gate, pool_mix, pool_scale, conv_w, q_gain, k_gain, w_pool_up, w_conv_out, w_attn_up, w_o, norm_mlp, w_ff1, w_ff2, loss_target, m_norm_mix, m_w_in, m_b_gate, m_pool_mix, m_pool_scale, m_conv_w, m_q_gain, m_k_gain, m_w_pool_up, m_w_conv_out, m_w_attn_up, m_w_o, m_norm_mlp, m_w_ff1, m_w_ff2, v_norm_mix, v_w_in, v_b_gate, v_pool_mix, v_pool_scale, v_conv_w, v_q_gain, v_k_gain, v_w_pool_up, v_w_conv_out, v_w_attn_up, v_w_o, v_norm_mlp, v_w_ff1, v_w_ff2):
    w = dict(zip(WEIGHTS, (norm_mix, w_in, b_gate, pool_mix, pool_scale, conv_w, q_gain, k_gain, w_pool_up,
                           w_conv_out, w_attn_up, w_o, norm_mlp, w_ff1, w_ff2)))
    m = dict(zip(WEIGHTS, (m_norm_mix, m_w_in, m_b_gate, m_pool_mix, m_pool_scale, m_conv_w, m_q_gain, m_k_gain,
                           m_w_pool_up, m_w_conv_out, m_w_attn_up, m_w_o, m_norm_mlp, m_w_ff1, m_w_ff2)))
    v = dict(zip(WEIGHTS, (v_norm_mix, v_w_in, v_b_gate, v_pool_mix, v_pool_scale, v_conv_w, v_q_gain, v_k_gain,
                           v_w_pool_up, v_w_conv_out, v_w_attn_up, v_w_o, v_norm_mlp, v_w_ff1, v_w_ff2)))
    xi, yi, ci = _mesh_pos()
    dev = 4 * xi + 2 * yi + ci

    saved, params = [], []
    act = x[0]
    w_in_t, m_in_t, v_in_t = (jnp.swapaxes(src["w_in"], 1, 2) for src in (w, m, v))
    h = _rmsnorm_fwd("l0_norm_mix", act, w["norm_mix"][0][None])
    conv_shard = jnp.pad(w["conv_w"].reshape(DEPTH * 3, CONV_W // N_DEV), ((0, 8 - DEPTH * 3), (0, 128 - CONV_W // N_DEV)))
    g_conv = _all_gather("gather_conv", None, conv_shard)
    for l in range(DEPTH):
        s_in, s_rest = _pack_weight_shards(w, l)
        g_in = _all_gather(f"gather_in{l}", 1 + 2 * l, s_in)
        g_rest = _all_gather(f"gather_rest{l}", 2 + 2 * l, s_rest)
        params.append(_unpack_gathered(g_in, g_rest, g_conv, w, l))
    for l in range(DEPTH):
        act, h, s = _layer_fwd(l, act, h, params[l], w["norm_mix"][l + 1][None] if l + 1 < DEPTH else None,
                               loss_target[0])
        saved.append(s)
    dy = act
    loss = _fold_rows("loss_sum", h, lanes=True, scale=0.5 / D)
    rest, grads, pending = [None] * DEPTH, [None] * DEPTH, None
    in_t = jax.ShapeDtypeStruct((DEPTH, IN_ROWS, D), F32)
    for l in reversed(range(DEPTH)):
        dy, grads[l], rest[l], in_t, pending = _layer_bwd(l, dy, params[l], saved[l], pending, in_t, 5 + 4 * l)
    dx = dy[0]
    launched = pending.chip_sums(after=dx)
    small_finish = _all_reduce_small("ar_small", _pack_small_grads(grads, loss), 5 + 4 * DEPTH, launched)
    g = _unpack_grads(rest)

    delta, new_m, new_v = {}, {}, {}
    g["w_ff1"], delta["w_ff1"], new_m["w_ff1"], new_v["w_ff1"] = _adamw_t("adamw_w_ff1", w["w_ff1"], g["w_ff1_t"],
                                                                          m["w_ff1"], v["w_ff1"])
    for name in ("w_pool_up", "w_conv_out", "w_attn_up", "w_o", "w_ff2"):
        shp = w[name].shape
        two_d = (shp[0] * shp[1], shp[2])
        d_, m_, v_ = _adamw("adamw_" + name, w[name].reshape(two_d), g[name].reshape(two_d),
                            m[name].reshape(two_d), v[name].reshape(two_d))
        delta[name], new_m[name], new_v[name] = d_.reshape(shp), m_.reshape(shp), v_.reshape(shp)
    g_t = pending.finish(after=new_v["w_ff2"], into=(in_t, 0))
    two_d = (g_t.shape[0] * g_t.shape[1], g_t.shape[2])
    d_, m_, v_ = _adamw("adamw_w_in", *[a.reshape(two_d) for a in (w_in_t, g_t, m_in_t, v_in_t)])
    g["w_in"] = jnp.swapaxes(g_t, 1, 2)
    delta["w_in"], new_m["w_in"], new_v["w_in"] = (jnp.swapaxes(a.reshape(g_t.shape), 1, 2) for a in (d_, m_, v_))
    g.update(_unpack_small_like_grads(small_finish(after=v_), dev))
    total = g.pop("loss")
    for name, shp in SMALL:
        shp = w[name].shape
        two_d = (-1, shp[-1])
        d_, m_, v_ = _adamw("adamw_" + name, w[name].reshape(two_d), g[name].reshape(two_d),
                            m[name].reshape(two_d), v[name].reshape(two_d))
        delta[name], new_m[name], new_v[name] = d_.reshape(shp), m_.reshape(shp), v_.reshape(shp)

    return (total, dx[None], *[g[n] for n in WEIGHTS], *[delta[n] for n in WEIGHTS],
            *[new_m[n] for n in WEIGHTS], *[new_v[n] for n in WEIGHTS])
```
